```python
import math
import jax, jax.numpy as jnp
from jax import lax
import numpy as np

D_MODEL = 1024
BATCH = 8
SEQ = 4096
DEPTH = 1

MEM_LEN = 256
HEAD_DIM = 64
GMLP_HEADS = 4
ATTN_HEADS = 8
MEM_HEADS = 4
GMLP_WIDTH = GMLP_HEADS * HEAD_DIM
ATTN_WIDTH = ATTN_HEADS * HEAD_DIM
MEM_WIDTH = MEM_HEADS * HEAD_DIM
MIX_WIDTH = GMLP_WIDTH + ATTN_WIDTH + MEM_WIDTH
IN_WIDTH = 3 * GMLP_WIDTH + 4 * ATTN_WIDTH + 2 * MEM_WIDTH
CHUNK = 128
BLOCK = 128
DILATED_CONFIGS = ((128, 1), (512, 4), (2048, 16))
PAD_MULT = max(d for _, d in DILATED_CONFIGS) * BLOCK
EPS = 1e-6

kernel_name = "hybrid_gmlp_dilated_memory_layer"


def _rms(x, g):
    xf = x.astype(jnp.float32)
    y = xf * lax.rsqrt(jnp.mean(xf * xf, axis=-1, keepdims=True) + EPS)
    return (y * g.astype(jnp.float32)).astype(x.dtype)


def _dilated_branch(q, k, v, dilation, n_win):
    B, Sp, H, hd = q.shape
    L = Sp // dilation
    nb = L // BLOCK

    def to_blocks(t):
        return t.reshape(B, L, dilation, H, hd).transpose(0, 2, 3, 1, 4).reshape(B, dilation, H, nb, BLOCK, hd)

    def with_prev(t):
        prev = jnp.pad(t, ((0, 0), (0, 0), (0, 0), (1, 0), (0, 0), (0, 0)))[:, :, :, :-1]
        return jnp.concatenate([prev, t], axis=4)

    qb = to_blocks(q)
    kc = with_prev(to_blocks(k))
    vc = with_prev(to_blocks(v))
    s = jnp.einsum('bdhnqc,bdhnkc->bdhnqk', qb, kc).astype(jnp.float32) * (1.0 / math.sqrt(hd))
    qi = jnp.arange(BLOCK)[:, None] + BLOCK
    ki = jnp.arange(2 * BLOCK)[None, :]
    rel = qi - ki
    blk = jnp.arange(nb)[:, None, None]
    valid = (rel >= 0) & (rel <= n_win) & ((blk > 0) | (ki >= BLOCK))
    s = jnp.where(valid, s, -jnp.inf)
    lse = jax.nn.logsumexp(s, axis=-1)
    p = jnp.exp(s - lse[..., None])
    o = jnp.einsum('bdhnqk,bdhnkc->bdhnqc', p.astype(v.dtype), vc)
    o = o.reshape(B, dilation, H, L, hd).transpose(0, 3, 1, 2, 4).reshape(B, Sp, H, hd)
    lse = lse.reshape(B, dilation, H, L).transpose(0, 3, 1, 2).reshape(B, Sp, H)
    return o, lse


def _dilated_attention(q, k, v):
    B, S, H, hd = q.shape
    Sp = ((S + PAD_MULT - 1) // PAD_MULT) * PAD_MULT
    pad = ((0, 0), (0, Sp - S), (0, 0), (0, 0))
    qp, kp, vp = jnp.pad(q, pad), jnp.pad(k, pad), jnp.pad(v, pad)
    outs, lses = [], []
    for window, dil in DILATED_CONFIGS:
        o, l = _dilated_branch(qp, kp, vp, dil, window // dil)
        outs.append(o)
        lses.append(l)
    w = jax.nn.softmax(jnp.stack(lses, axis=0), axis=0)
    out = sum(w[i][..., None] * outs[i].astype(jnp.float32) for i in range(len(outs)))
    return out[:, :S].astype(q.dtype)


def _chunked_gmlp(u, v, v_gain, w_s, b_s):
    B, S, GH, hd = v.shape
    nc = S // CHUNK
    vn = _rms(v, v_gain).reshape(B, nc, CHUNK, GH, hd)
    tril = jnp.tril(jnp.ones((CHUNK, CHUNK), dtype=w_s.dtype))
    sp = jnp.einsum('hts,bcshd->bcthd', w_s * tril, vn) + b_s.T[:, :, None]
    return u * sp.reshape(B, S, GH, hd)


def _memory_attention(qm, mem, mem_gain, w_mem_kv, q_gain, k_gain):
    B, S, H, hd = qm.shape
    kv = _rms(mem, mem_gain) @ w_mem_kv
    mk, mv = jnp.split(kv, 2, axis=-1)
    mk = _rms(mk.reshape(B, -1, H, hd), k_gain)
    mv = mv.reshape(B, -1, H, hd)
    qn = _rms(qm, q_gain)
    s = jnp.einsum('bshc,bmhc->bhsm', qn, mk).astype(jnp.float32) * (1.0 / math.sqrt(hd))
    p = jax.nn.softmax(s, axis=-1)
    return jnp.einsum('bhsm,bmhc->bshc', p.astype(mv.dtype), mv)


def _fwd_setup_inputs(seed: int = 0) -> dict:
    key = jax.random.key(seed)
    ks = jax.random.split(key, 16)
    f32 = jnp.float32
    x = jax.random.normal(ks[0], (BATCH, SEQ, D_MODEL), f32)
    mem = jax.random.normal(ks[1], (BATCH, MEM_LEN, D_MODEL), f32)
    norm_gain = 1.0 + 0.02 * jax.random.normal(ks[2], (DEPTH, D_MODEL), f32)
    w_in = jax.random.normal(ks[3], (DEPTH, D_MODEL, IN_WIDTH), f32) * D_MODEL ** -0.5
    gmlp_v_gain = 1.0 + 0.02 * jax.random.normal(ks[4], (DEPTH, GMLP_HEADS, HEAD_DIM), f32)
    gmlp_w_s = jax.random.normal(ks[5], (DEPTH, GMLP_HEADS, CHUNK, CHUNK), f32) * CHUNK ** -0.5
    gmlp_b = 1.0 + 0.02 * jax.random.normal(ks[6], (DEPTH, GMLP_HEADS, CHUNK), f32)
    attn_q_gain = 1.0 + 0.02 * jax.random.normal(ks[7], (DEPTH, HEAD_DIM), f32)
    attn_k_gain = 1.0 + 0.02 * jax.random.normal(ks[8], (DEPTH, HEAD_DIM), f32)
    mem_norm_gain = 1.0 + 0.02 * jax.random.normal(ks[9], (DEPTH, D_MODEL), f32)
    w_mem_kv = jax.random.normal(ks[10], (DEPTH, D_MODEL, 2 * MEM_WIDTH), f32) * D_MODEL ** -0.5
    mem_q_gain = 1.0 + 0.02 * jax.random.normal(ks[11], (DEPTH, HEAD_DIM), f32)
    mem_k_gain = 1.0 + 0.02 * jax.random.normal(ks[12], (DEPTH, HEAD_DIM), f32)
    w_out = jax.random.normal(ks[13], (DEPTH, MIX_WIDTH, D_MODEL), f32) * MIX_WIDTH ** -0.5
    return {"x": x, "mem": mem, "norm_gain": norm_gain, "w_in": w_in,
            "gmlp_v_gain": gmlp_v_gain, "gmlp_w_s": gmlp_w_s, "gmlp_b": gmlp_b,
            "attn_q_gain": attn_q_gain, "attn_k_gain": attn_k_gain,
            "mem_norm_gain": mem_norm_gain, "w_mem_kv": w_mem_kv,
            "mem_q_gain": mem_q_gain, "mem_k_gain": mem_k_gain, "w_out": w_out}


def _fwd_reference(x, mem, norm_gain, w_in, gmlp_v_gain, gmlp_w_s, gmlp_b,
              attn_q_gain, attn_k_gain, mem_norm_gain, w_mem_kv,
              mem_q_gain, mem_k_gain, w_out):
    B, S, _ = x.shape
    split_points = np.cumsum([GMLP_WIDTH] * 3 + [ATTN_WIDTH] * 4 + [MEM_WIDTH])
    for l in range(DEPTH):
        h = _rms(x, norm_gain[l])
        proj = h @ w_in[l]
        g_u, g_v, g_gate, a_q, a_k, a_v, a_gate, m_q, m_gate = jnp.split(proj, split_points, axis=-1)

        y_g = _chunked_gmlp(g_u.reshape(B, S, GMLP_HEADS, HEAD_DIM), g_v.reshape(B, S, GMLP_HEADS, HEAD_DIM),
                            gmlp_v_gain[l], gmlp_w_s[l], gmlp_b[l]).reshape(B, S, GMLP_WIDTH)
        y_g = y_g * jax.nn.silu(g_gate)

        q = _rms(a_q.reshape(B, S, ATTN_HEADS, HEAD_DIM), attn_q_gain[l])
        k = _rms(a_k.reshape(B, S, ATTN_HEADS, HEAD_DIM), attn_k_gain[l])
        v = a_v.reshape(B, S, ATTN_HEADS, HEAD_DIM)
        y_a = _dilated_attention(q, k, v).reshape(B, S, ATTN_WIDTH) * jax.nn.silu(a_gate)

        y_m = _memory_attention(m_q.reshape(B, S, MEM_HEADS, HEAD_DIM), mem, mem_norm_gain[l], w_mem_kv[l],
                                mem_q_gain[l], mem_k_gain[l]).reshape(B, S, MEM_WIDTH)
        y_m = y_m * jax.nn.silu(m_gate)

        y = jnp.concatenate([y_g, y_a, y_m], axis=-1) @ w_out[l]
        x = x + y
    return x


import jax as _jax
import jax.numpy as _jnp

TWIN_FORMAT = 'train_step'
FWD_PARAMS = ['x', 'mem', 'norm_gain', 'w_in', 'gmlp_v_gain', 'gmlp_w_s', 'gmlp_b', 'attn_q_gain', 'attn_k_gain', 'mem_norm_gain', 'w_mem_kv', 'mem_q_gain', 'mem_k_gain', 'w_out']
TWIN_WEIGHTS = ['norm_gain', 'w_in', 'gmlp_v_gain', 'gmlp_w_s', 'gmlp_b', 'attn_q_gain', 'attn_k_gain', 'mem_norm_gain', 'w_mem_kv', 'mem_q_gain', 'mem_k_gain', 'w_out']
TWIN_DIFF_INPUT = 'x'
TWIN_INPUTS = ['x', 'mem', 'norm_gain', 'w_in', 'gmlp_v_gain', 'gmlp_w_s', 'gmlp_b', 'attn_q_gain', 'attn_k_gain', 'mem_norm_gain', 'w_mem_kv', 'mem_q_gain', 'mem_k_gain', 'w_out', 'loss_target', 'm_norm_gain', 'm_w_in', 'm_gmlp_v_gain', 'm_gmlp_w_s', 'm_gmlp_b', 'm_attn_q_gain', 'm_attn_k_gain', 'm_mem_norm_gain', 'm_w_mem_kv', 'm_mem_q_gain', 'm_mem_k_gain', 'm_w_out', 'v_norm_gain', 'v_w_in', 'v_gmlp_v_gain', 'v_gmlp_w_s', 'v_gmlp_b', 'v_attn_q_gain', 'v_attn_k_gain', 'v_mem_norm_gain', 'v_w_mem_kv', 'v_mem_q_gain', 'v_mem_k_gain', 'v_w_out']
TWIN_OUTPUTS = ['loss', 'grad_x', 'grad_norm_gain', 'grad_w_in', 'grad_gmlp_v_gain', 'grad_gmlp_w_s', 'grad_gmlp_b', 'grad_attn_q_gain', 'grad_attn_k_gain', 'grad_mem_norm_gain', 'grad_w_mem_kv', 'grad_mem_q_gain', 'grad_mem_k_gain', 'grad_w_out', 'delta_norm_gain', 'delta_w_in', 'delta_gmlp_v_gain', 'delta_gmlp_w_s', 'delta_gmlp_b', 'delta_attn_q_gain', 'delta_attn_k_gain', 'delta_mem_norm_gain', 'delta_w_mem_kv', 'delta_mem_q_gain', 'delta_mem_k_gain', 'delta_w_out', 'new_m_norm_gain', 'new_m_w_in', 'new_m_gmlp_v_gain', 'new_m_gmlp_w_s', 'new_m_gmlp_b', 'new_m_attn_q_gain', 'new_m_attn_k_gain', 'new_m_mem_norm_gain', 'new_m_w_mem_kv', 'new_m_mem_q_gain', 'new_m_mem_k_gain', 'new_m_w_out', 'new_v_norm_gain', 'new_v_w_in', 'new_v_gmlp_v_gain', 'new_v_gmlp_w_s', 'new_v_gmlp_b', 'new_v_attn_q_gain', 'new_v_attn_k_gain', 'new_v_mem_norm_gain', 'new_v_w_mem_kv', 'new_v_mem_q_gain', 'new_v_mem_k_gain', 'new_v_w_out']
TWIN_LEAF_KINDS = {'loss': 'loss', 'grad_x': 'grad_x', 'grad_norm_gain': 'grad_w', 'grad_w_in': 'grad_w', 'grad_gmlp_v_gain': 'grad_w', 'grad_gmlp_w_s': 'grad_w', 'grad_gmlp_b': 'grad_w', 'grad_attn_q_gain': 'grad_w', 'grad_attn_k_gain': 'grad_w', 'grad_mem_norm_gain': 'grad_w', 'grad_w_mem_kv': 'grad_w', 'grad_mem_q_gain': 'grad_w', 'grad_mem_k_gain': 'grad_w', 'grad_w_out': 'grad_w', 'delta_norm_gain': 'delta_w', 'delta_w_in': 'delta_w', 'delta_gmlp_v_gain': 'delta_w', 'delta_gmlp_w_s': 'delta_w', 'delta_gmlp_b': 'delta_w', 'delta_attn_q_gain': 'delta_w', 'delta_attn_k_gain': 'delta_w', 'delta_mem_norm_gain': 'delta_w', 'delta_w_mem_kv': 'delta_w', 'delta_mem_q_gain': 'delta_w', 'delta_mem_k_gain': 'delta_w', 'delta_w_out': 'delta_w', 'new_m_norm_gain': 'new_m', 'new_m_w_in': 'new_m', 'new_m_gmlp_v_gain': 'new_m', 'new_m_gmlp_w_s': 'new_m', 'new_m_gmlp_b': 'new_m', 'new_m_attn_q_gain': 'new_m', 'new_m_attn_k_gain': 'new_m', 'new_m_mem_norm_gain': 'new_m', 'new_m_w_mem_kv': 'new_m', 'new_m_mem_q_gain': 'new_m', 'new_m_mem_k_gain': 'new_m', 'new_m_w_out': 'new_m', 'new_v_norm_gain': 'new_v', 'new_v_w_in': 'new_v', 'new_v_gmlp_v_gain': 'new_v', 'new_v_gmlp_w_s': 'new_v', 'new_v_gmlp_b': 'new_v', 'new_v_attn_q_gain': 'new_v', 'new_v_attn_k_gain': 'new_v', 'new_v_mem_norm_gain': 'new_v', 'new_v_w_mem_kv': 'new_v', 'new_v_mem_q_gain': 'new_v', 'new_v_mem_k_gain': 'new_v', 'new_v_w_out': 'new_v'}


def _forward(args):
    return _fwd_reference(*[args[k] for k in FWD_PARAMS])


def _output_shape():
    def fwd():
        inp = _fwd_setup_inputs(0)
        return _fwd_reference(*[inp[k] for k in FWD_PARAMS])
    out = _jax.eval_shape(fwd)
    return out.shape, out.dtype

N_MICROBATCH = 1
ADAM_LR = 0.001
ADAM_B1 = 0.9
ADAM_B2 = 0.999
ADAM_EPS = 1e-08
ADAM_WD = 0.01
ADAM_STEP = 10
PER_EXAMPLE_BATCH_AXIS = {'x': 0, 'mem': 0, 'loss_target': 0}
SHARED_INPUTS = []
_WEIGHT_DTYPES = {'norm_gain': _jnp.float32, 'w_in': _jnp.float32, 'gmlp_v_gain': _jnp.float32, 'gmlp_w_s': _jnp.float32, 'gmlp_b': _jnp.float32, 'attn_q_gain': _jnp.float32, 'attn_k_gain': _jnp.float32, 'mem_norm_gain': _jnp.float32, 'w_mem_kv': _jnp.float32, 'mem_q_gain': _jnp.float32, 'mem_k_gain': _jnp.float32, 'w_out': _jnp.float32}
MOMENT_SCALE = {'norm_gain': 9.321654e+00, 'w_in': 2.525505e-01, 'gmlp_v_gain': 5.659202e+00, 'gmlp_w_s': 3.326005e-01, 'gmlp_b': 5.670026e+00, 'attn_q_gain': 8.103602e-01, 'attn_k_gain': 8.098860e-01, 'mem_norm_gain': 2.906147e-02, 'w_mem_kv': 1.932712e-02, 'mem_q_gain': 4.435759e-01, 'mem_k_gain': 4.432918e-01, 'w_out': 2.461579e-01}


def _to_microbatches(a, axis):
    t = _jnp.moveaxis(a, axis, 0)
    t = t.reshape((N_MICROBATCH, t.shape[0] // N_MICROBATCH) + t.shape[1:])
    return _jnp.moveaxis(t, 1, axis + 1)


def setup_inputs(seed: int = 0) -> dict:
    inp = _fwd_setup_inputs(seed)
    key = _jax.random.fold_in(_jax.random.key(seed), 7919)
    shape, _ = _output_shape()
    out = dict(inp)
    out["loss_target"] = _jax.random.normal(_jax.random.fold_in(key, 0), shape, _jnp.float32)
    for i, name in enumerate(TWIN_WEIGHTS):
        w = inp[name].astype(_jnp.float32)
        if MOMENT_SCALE is None:
            s = _jnp.sqrt(_jnp.mean(_jnp.square(w)) + 1e-30)
        else:
            s = MOMENT_SCALE[name]
        km, kv = _jax.random.split(_jax.random.fold_in(key, i + 1))
        out[name] = w
        out["m_" + name] = s * _jax.random.normal(km, w.shape, _jnp.float32)
        out["v_" + name] = (s * s) * _jax.random.uniform(kv, w.shape, _jnp.float32, 0.5, 1.5)
    if N_MICROBATCH > 1:
        for name, axis in PER_EXAMPLE_BATCH_AXIS.items():
            out[name] = _to_microbatches(out[name], axis)
    return {'x': out['x'], 'mem': out['mem'], 'norm_gain': out['norm_gain'], 'w_in': out['w_in'], 'gmlp_v_gain': out['gmlp_v_gain'], 'gmlp_w_s': out['gmlp_w_s'], 'gmlp_b': out['gmlp_b'], 'attn_q_gain': out['attn_q_gain'], 'attn_k_gain': out['attn_k_gain'], 'mem_norm_gain': out['mem_norm_gain'], 'w_mem_kv': out['w_mem_kv'], 'mem_q_gain': out['mem_q_gain'], 'mem_k_gain': out['mem_k_gain'], 'w_out': out['w_out'], 'loss_target': out['loss_target'], 'm_norm_gain': out['m_norm_gain'], 'm_w_in': out['m_w_in'], 'm_gmlp_v_gain': out['m_gmlp_v_gain'], 'm_gmlp_w_s': out['m_gmlp_w_s'], 'm_gmlp_b': out['m_gmlp_b'], 'm_attn_q_gain': out['m_attn_q_gain'], 'm_attn_k_gain': out['m_attn_k_gain'], 'm_mem_norm_gain': out['m_mem_norm_gain'], 'm_w_mem_kv': out['m_w_mem_kv'], 'm_mem_q_gain': out['m_mem_q_gain'], 'm_mem_k_gain': out['m_mem_k_gain'], 'm_w_out': out['m_w_out'], 'v_norm_gain': out['v_norm_gain'], 'v_w_in': out['v_w_in'], 'v_gmlp_v_gain': out['v_gmlp_v_gain'], 'v_gmlp_w_s': out['v_gmlp_w_s'], 'v_gmlp_b': out['v_gmlp_b'], 'v_attn_q_gain': out['v_attn_q_gain'], 'v_attn_k_gain': out['v_attn_k_gain'], 'v_mem_norm_gain': out['v_mem_norm_gain'], 'v_w_mem_kv': out['v_w_mem_kv'], 'v_mem_q_gain': out['v_mem_q_gain'], 'v_mem_k_gain': out['v_mem_k_gain'], 'v_w_out': out['v_w_out']}


def _loss(weights, diff, rest, loss_target):
    with _jax.named_scope("forward"):
        args = {**rest, TWIN_DIFF_INPUT: diff, **{k: w.astype(_WEIGHT_DTYPES[k]) for k, w in weights.items()}}
        y = _forward(args)
    with _jax.named_scope("loss_head"):
        err = _jnp.square(y.astype(_jnp.float32) - loss_target)
        return 0.5 * _jnp.sum(_jnp.mean(err, axis=-1)) if err.ndim else 0.5 * err


def _adamw(w, g, m, v):
    m = ADAM_B1 * m + (1.0 - ADAM_B1) * g
    v = ADAM_B2 * v + (1.0 - ADAM_B2) * _jnp.square(g)
    m_hat = m / (1.0 - ADAM_B1 ** ADAM_STEP)
    v_hat = v / (1.0 - ADAM_B2 ** ADAM_STEP)
    delta = -ADAM_LR * (m_hat / (_jnp.sqrt(v_hat) + ADAM_EPS) + ADAM_WD * w)
    return delta, m, v


def reference(x, mem, norm_gain, w_in, gmlp_v_gain, gmlp_w_s, gmlp_b, attn_q_gain, attn_k_gain, mem_norm_gain, w_mem_kv, mem_q_gain, mem_k_gain, w_out, loss_target, m_norm_gain, m_w_in, m_gmlp_v_gain, m_gmlp_w_s, m_gmlp_b, m_attn_q_gain, m_attn_k_gain, m_mem_norm_gain, m_w_mem_kv, m_mem_q_gain, m_mem_k_gain, m_w_out, v_norm_gain, v_w_in, v_gmlp_v_gain, v_gmlp_w_s, v_gmlp_b, v_attn_q_gain, v_attn_k_gain, v_mem_norm_gain, v_w_mem_kv, v_mem_q_gain, v_mem_k_gain, v_w_out):
    given = dict(x=x, mem=mem, norm_gain=norm_gain, w_in=w_in, gmlp_v_gain=gmlp_v_gain, gmlp_w_s=gmlp_w_s, gmlp_b=gmlp_b, attn_q_gain=attn_q_gain, attn_k_gain=attn_k_gain, mem_norm_gain=mem_norm_gain, w_mem_kv=w_mem_kv, mem_q_gain=mem_q_gain, mem_k_gain=mem_k_gain, w_out=w_out, loss_target=loss_target, m_norm_gain=m_norm_gain, m_w_in=m_w_in, m_gmlp_v_gain=m_gmlp_v_gain, m_gmlp_w_s=m_gmlp_w_s, m_gmlp_b=m_gmlp_b, m_attn_q_gain=m_attn_q_gain, m_attn_k_gain=m_attn_k_gain, m_mem_norm_gain=m_mem_norm_gain, m_w_mem_kv=m_w_mem_kv, m_mem_q_gain=m_mem_q_gain, m_mem_k_gain=m_mem_k_gain, m_w_out=m_w_out, v_norm_gain=v_norm_gain, v_w_in=v_w_in, v_gmlp_v_gain=v_gmlp_v_gain, v_gmlp_w_s=v_gmlp_w_s, v_gmlp_b=v_gmlp_b, v_attn_q_gain=v_attn_q_gain, v_attn_k_gain=v_attn_k_gain, v_mem_norm_gain=v_mem_norm_gain, v_w_mem_kv=v_w_mem_kv, v_mem_q_gain=v_mem_q_gain, v_mem_k_gain=v_mem_k_gain, v_w_out=v_w_out)
    weights = {n: given[n] for n in TWIN_WEIGHTS}
    shared = {n: given[n] for n in SHARED_INPUTS}
    per_example = {n: given[n] for n in ['x', 'mem']}
    grad_fn = _jax.value_and_grad(_loss, argnums=(0, 1))

    def one_microbatch(ex, loss_target):
        ex = dict(ex)
        diff = ex.pop(TWIN_DIFF_INPUT)
        return grad_fn(weights, diff, {**shared, **ex}, loss_target)

    if N_MICROBATCH == 1:
        loss, (grad_w, grad_x) = one_microbatch(per_example, given["loss_target"])
    else:
        def body(carry, xs):
            loss_sum, grad_sum = carry
            l_k, (gw_k, gx_k) = one_microbatch(xs[0], xs[1])
            with _jax.named_scope("update"):
                return (loss_sum + l_k, _jax.tree.map(_jnp.add, grad_sum, gw_k)), gx_k

        init = (_jnp.zeros((), _jnp.float32), _jax.tree.map(_jnp.zeros_like, weights))
        (loss, grad_w), grad_x = _jax.lax.scan(body, init, (per_example, given["loss_target"]))
    with _jax.named_scope("update"):
        delta_w, new_m, new_v = {}, {}, {}
        for n in TWIN_WEIGHTS:
            delta_w[n], new_m[n], new_v[n] = _adamw(weights[n], grad_w[n], given["m_" + n], given["v_" + n])
    return (loss, grad_x, *[grad_w[n] for n in TWIN_WEIGHTS], *[delta_w[n] for n in TWIN_WEIGHTS],
            *[new_m[n] for n in TWIN_WEIGHTS], *[new_v[n] for n in TWIN_WEIGHTS])
```

```python
import functools
import math

import jax
import jax.numpy as jnp
from jax import lax
from jax.experimental import pallas as pl
from jax.experimental.pallas import tpu as pltpu

F32 = jnp.float32
BF16 = jnp.bfloat16

SEQ = 4096
D_MODEL = 1024
HEAD_DIM = 64
LANES = 128
CHUNK = 128
GMLP_W, ATTN_W, MEM_W = 256, 512, 256
IN_W = 3 * GMLP_W + 4 * ATTN_W + 2 * MEM_W
MEM_LEN = 256
DILATIONS = (1, 4, 16)
EPS = 1e-6
QK_SCALE = 1.0 / math.sqrt(HEAD_DIM)
C_GU, C_GV, C_GG, C_AQ, C_AK, C_AV, C_AG, C_MQ, C_MG = 0, 256, 512, 768, 1280, 1792, 2304, 2816, 3072

ADAM_LR, ADAM_B1, ADAM_B2, ADAM_EPS, ADAM_WD, ADAM_STEP = 0.001, 0.9, 0.999, 1e-08, 0.01, 10

VMEM_LIMIT = 48 * 1024 * 1024
MESH = pl.DeviceIdType.MESH

SM_WS, SM_NG, SM_MNG, SM_VG, SM_B, SM_AQ, SM_AK, SM_MQG, SM_MKG, SM_ROWS = 0, 512, 520, 528, 530, 534, 535, 536, 537, 544


def _call(body, **kw):
    return pl.pallas_call(body, **kw)


def _params(**kw):
    return pltpu.CompilerParams(vmem_limit_bytes=VMEM_LIMIT, **kw)


def _dot(a, b):
    return jnp.dot(a, b, preferred_element_type=F32)


def _dot_nt(a, b):
    return lax.dot_general(a, b, (((1,), (1,)), ((), ())), preferred_element_type=F32)


def _dot_tn(a, b):
    return lax.dot_general(a, b, (((0,), (0,)), ((), ())), preferred_element_type=F32)


def _head_blockdiag():
    r = lax.shift_right_logical(lax.broadcasted_iota(jnp.int32, (LANES, LANES), 0), 6)
    c = lax.shift_right_logical(lax.broadcasted_iota(jnp.int32, (LANES, LANES), 1), 6)
    return jnp.where(r == c, 1.0, 0.0).astype(BF16)


def _headsum(v, bd):
    hi = v.astype(BF16)
    lo = (v - hi.astype(F32)).astype(BF16)
    return _dot(hi, bd) + _dot(lo, bd)


def _lo_mask(rows):
    return lax.broadcasted_iota(jnp.int32, (rows, LANES), 1) < HEAD_DIM


def _sigmoid(x):
    return 1.0 / (1.0 + jnp.exp(-x))


def _fold_heads(v):
    return v + pltpu.roll(v, HEAD_DIM, 1)


def _fwd_proj(x, gain, wt):
    tm = 256

    def body(x_ref, g_ref, wt_ref, o_ref):
        xv = x_ref[...]
        ms = jnp.mean(xv * xv, axis=-1, keepdims=True)
        h = (xv * lax.rsqrt(ms + EPS) * g_ref[...]).astype(BF16)
        o_ref[...] = _dot_nt(h, wt_ref[...])

    return _call(
        body, name="fwd_proj", grid=(SEQ // tm,),
        in_specs=[pl.BlockSpec((tm, D_MODEL), lambda i: (i, 0)),
                  pl.BlockSpec((1, D_MODEL), lambda i: (0, 0)),
                  pl.BlockSpec((IN_W, D_MODEL), lambda i: (0, 0))],
        out_specs=pl.BlockSpec((tm, IN_W), lambda i: (i, 0)),
        out_shape=jax.ShapeDtypeStruct((SEQ, IN_W), F32),
        compiler_params=_params(),
    )(x, gain, wt)


def _gmlp_weights(w_ref):
    ti = lax.broadcasted_iota(jnp.int32, (CHUNK, CHUNK), 0)
    si = lax.broadcasted_iota(jnp.int32, (CHUNK, CHUNK), 1)
    tril = si <= ti
    return tril, [jnp.where(tril, w_ref[h], 0.0).astype(BF16) for h in range(4)]


def _gmlp_fwd(proj, vgain, w_s, bias_full):
    tm = 512

    def body(p_ref, vg_ref, w_ref, b_ref, y_ref):
        bd = _head_blockdiag()
        lo = _lo_mask(CHUNK)
        _, wm = _gmlp_weights(w_ref)
        for c in range(tm // CHUNK):
            rows = pl.ds(c * CHUNK, CHUNK)
            for p in range(2):
                cs = slice(p * LANES, (p + 1) * LANES)
                u = p_ref[rows, C_GU + p * LANES:C_GU + (p + 1) * LANES]
                v = p_ref[rows, C_GV + p * LANES:C_GV + (p + 1) * LANES]
                gt = p_ref[rows, C_GG + p * LANES:C_GG + (p + 1) * LANES]
                r = lax.rsqrt(_headsum(v * v, bd) * (1.0 / HEAD_DIM) + EPS)
                vn = (v * r * vg_ref[:, cs]).astype(BF16)
                sp = jnp.where(lo, _dot(wm[2 * p], vn), _dot(wm[2 * p + 1], vn)) + b_ref[:, cs]
                y_ref[rows, cs] = (u * sp * (gt * _sigmoid(gt))).astype(BF16)

    return _call(
        body, name="gmlp_fwd", grid=(SEQ // tm,),
        in_specs=[pl.BlockSpec((tm, 3 * GMLP_W), lambda i: (i, 0)),
                  pl.BlockSpec((1, GMLP_W), lambda i: (0, 0)),
                  pl.BlockSpec((4, CHUNK, CHUNK), lambda i: (0, 0, 0)),
                  pl.BlockSpec((CHUNK, GMLP_W), lambda i: (0, 0))],
        out_specs=pl.BlockSpec((tm, GMLP_W), lambda i: (i, 0)),
        out_shape=jax.ShapeDtypeStruct((SEQ, GMLP_W), BF16),
        compiler_params=_params(),
    )(proj, vgain, w_s, bias_full)


def _gmlp_bwd(proj, dyc, vgain, w_s, bias_full):
    tm = 512
    nsteps = SEQ // tm

    def body(p_ref, dy_ref, vg_ref, w_ref, b_ref, dg_ref, gw_ref, gb_ref, gv_ref):
        i = pl.program_id(0)
        bd = _head_blockdiag()
        lo = _lo_mask(CHUNK)
        tril, wm = _gmlp_weights(w_ref)

        @pl.when(i == 0)
        def _():
            gw_ref[...] = jnp.zeros_like(gw_ref)
            gb_ref[...] = jnp.zeros_like(gb_ref)
            gv_ref[...] = jnp.zeros_like(gv_ref)

        for c in range(tm // CHUNK):
            rows = pl.ds(c * CHUNK, CHUNK)
            for p in range(2):
                cs = slice(p * LANES, (p + 1) * LANES)
                u = p_ref[rows, C_GU + p * LANES:C_GU + (p + 1) * LANES]
                v = p_ref[rows, C_GV + p * LANES:C_GV + (p + 1) * LANES]
                gt = p_ref[rows, C_GG + p * LANES:C_GG + (p + 1) * LANES]
                dy = dy_ref[rows, cs]
                g = vg_ref[:, cs]
                r = lax.rsqrt(_headsum(v * v, bd) * (1.0 / HEAD_DIM) + EPS)
                z = v * r
                vn = (z * g).astype(BF16)
                sp = jnp.where(lo, _dot(wm[2 * p], vn), _dot(wm[2 * p + 1], vn)) + b_ref[:, cs]
                sg = _sigmoid(gt)
                sl = gt * sg
                dsl = sg * (1.0 + gt * (1.0 - sg))
                du = dy * sp * sl
                dsp = dy * u * sl
                dgt = dy * u * sp * dsl
                dspb = dsp.astype(BF16)
                dvn = jnp.where(lo, _dot_tn(wm[2 * p], dspb), _dot_tn(wm[2 * p + 1], dspb))
                gw_ref[2 * p] += _dot_nt(jnp.where(lo, dsp, 0.0).astype(BF16), vn)
                gw_ref[2 * p + 1] += _dot_nt(jnp.where(lo, 0.0, dsp).astype(BF16), vn)
                gb_ref[:, cs] += _headsum(dsp, bd)
                gv_ref[:, cs] += jnp.sum(dvn * z, axis=0, keepdims=True)
                dz = dvn * g
                dv = r * (dz - z * (_headsum(dz * z, bd) * (1.0 / HEAD_DIM)))
                dg_ref[rows, C_GU + p * LANES:C_GU + (p + 1) * LANES] = du.astype(BF16)
                dg_ref[rows, C_GV + p * LANES:C_GV + (p + 1) * LANES] = dv.astype(BF16)
                dg_ref[rows, C_GG + p * LANES:C_GG + (p + 1) * LANES] = dgt.astype(BF16)

        @pl.when(i == nsteps - 1)
        def _():
            for h in range(4):
                gw_ref[h] = jnp.where(tril, gw_ref[h], 0.0)

    return _call(
        body, name="gmlp_bwd", grid=(nsteps,),
        in_specs=[pl.BlockSpec((tm, 3 * GMLP_W), lambda i: (i, 0)),
                  pl.BlockSpec((tm, GMLP_W), lambda i: (i, 0)),
                  pl.BlockSpec((1, GMLP_W), lambda i: (0, 0)),
                  pl.BlockSpec((4, CHUNK, CHUNK), lambda i: (0, 0, 0)),
                  pl.BlockSpec((CHUNK, GMLP_W), lambda i: (0, 0))],
        out_specs=[pl.BlockSpec((tm, 3 * GMLP_W), lambda i: (i, 0)),
                   pl.BlockSpec((4, CHUNK, CHUNK), lambda i: (0, 0, 0)),
                   pl.BlockSpec((CHUNK, GMLP_W), lambda i: (0, 0)),
                   pl.BlockSpec((1, GMLP_W), lambda i: (0, 0))],
        out_shape=[jax.ShapeDtypeStruct((SEQ, 3 * GMLP_W), BF16),
                   jax.ShapeDtypeStruct((4, CHUNK, CHUNK), F32),
                   jax.ShapeDtypeStruct((CHUNK, GMLP_W), F32),
                   jax.ShapeDtypeStruct((1, GMLP_W), F32)],
        compiler_params=_params(),
    )(proj, dyc, vgain, w_s, bias_full)


def _band_masks():
    qi = lax.broadcasted_iota(jnp.int32, (CHUNK, 2 * CHUNK), 0)
    kj = lax.broadcasted_iota(jnp.int32, (CHUNK, 2 * CHUNK), 1)
    valid2 = ((kj < CHUNK) & (kj >= qi)) | ((kj >= CHUNK) & (kj - CHUNK <= qi))
    q1 = lax.broadcasted_iota(jnp.int32, (CHUNK, CHUNK), 0)
    k1 = lax.broadcasted_iota(jnp.int32, (CHUNK, CHUNK), 1)
    return k1 <= q1, valid2


def _rows_of(ref, start, d):
    if d == 1:
        return ref.at[pl.ds(pl.multiple_of(start, CHUNK), CHUNK), :]
    return ref.at[pl.ds(start, CHUNK, stride=d), :]


def _for_blocks(d, first_fn, rest_fn):
    nblk = SEQ // CHUNK
    sh = d.bit_length() - 1

    def first(j, carry):
        first_fn(j * CHUNK if d == 1 else j)
        return carry

    def rest(j, carry):
        start = (j & (d - 1)) + lax.shift_right_logical(j, sh) * (CHUNK * d)
        rest_fn(start, start - CHUNK * d)
        return carry

    lax.fori_loop(0, d, first, 0)
    lax.fori_loop(d, nblk, rest, 0)


def _attn_fwd(proj, gq2, gk2):
    tn = 512

    def body(q_ref, k_ref, v_ref, g_ref, gq_ref, gk_ref, o_ref, l_ref, ya_ref, qn_ref, kn_ref):
        bd = _head_blockdiag()
        lo = _lo_mask(CHUNK)
        valid1, valid2 = _band_masks()

        def norm(t, carry):
            rows = pl.ds(pl.multiple_of(t * tn, tn), tn)
            q = q_ref[rows, :]
            qn_ref[rows, :] = q * lax.rsqrt(_headsum(q * q, bd) * (1.0 / HEAD_DIM) + EPS) * (gq_ref[...] * QK_SCALE)
            k = k_ref[rows, :]
            kn_ref[rows, :] = k * lax.rsqrt(_headsum(k * k, bd) * (1.0 / HEAD_DIM) + EPS) * gk_ref[...]
            return carry

        lax.fori_loop(0, SEQ // tn, norm, 0)

        def block(d, start, prev):
            qb = _rows_of(qn_ref, start, d)[...]
            if prev is None:
                kc = _rows_of(kn_ref, start, d)[...]
                vc = _rows_of(v_ref, start, d)[...]
                valid = valid1
            else:
                kc = jnp.concatenate([_rows_of(kn_ref, prev, d)[...], _rows_of(kn_ref, start, d)[...]], axis=0)
                vc = jnp.concatenate([_rows_of(v_ref, prev, d)[...], _rows_of(v_ref, start, d)[...]], axis=0)
                valid = valid2
            kcb = kc.astype(BF16)
            vcb = vc.astype(BF16)
            res = []
            for h in range(2):
                qh = jnp.where(lo if h == 0 else ~lo, qb, 0.0).astype(BF16)
                s = jnp.where(valid, _dot_nt(qh, kcb), -jnp.inf)
                m = jnp.max(s, axis=-1, keepdims=True)
                p = jnp.exp(s - m)
                l = jnp.sum(p, axis=-1, keepdims=True)
                res.append((_dot(p.astype(BF16), vcb) * (1.0 / l), m + jnp.log(l)))
            ob = jnp.where(lo, res[0][0], res[1][0])
            lb = jnp.where(lo, res[0][1], res[1][1])
            o_rows = _rows_of(o_ref, start, d)
            l_rows = _rows_of(l_ref, start, d)
            if d == DILATIONS[0]:
                o_rows[...] = ob
                l_rows[...] = lb
            else:
                lold = l_rows[...]
                mx = jnp.maximum(lold, lb)
                ea = jnp.exp(lold - mx)
                eb = jnp.exp(lb - mx)
                inv = 1.0 / (ea + eb)
                o_rows[...] = o_rows[...] * (ea * inv) + ob * (eb * inv)
                l_rows[...] = mx + jnp.log(ea + eb)

        for d in DILATIONS:
            _for_blocks(d, functools.partial(block, d, prev=None), functools.partial(block, d))

        def fin(t, carry):
            rows = pl.ds(pl.multiple_of(t * tn, tn), tn)
            g = g_ref[rows, :]
            ya_ref[rows, :] = (o_ref[rows, :] * (g * _sigmoid(g))).astype(BF16)
            return carry

        lax.fori_loop(0, SEQ // tn, fin, 0)

    col = lambda c0: pl.BlockSpec((SEQ, LANES), lambda p: (0, c0 // LANES + p))
    vec = pl.BlockSpec((1, LANES), lambda p: (0, 0))
    out = pl.BlockSpec((SEQ, LANES), lambda p: (0, p))
    return _call(
        body, name="attn_fwd", grid=(ATTN_W // LANES,),
        in_specs=[col(C_AQ), col(C_AK), col(C_AV), col(C_AG), vec, vec],
        out_specs=[out, out, out],
        out_shape=[jax.ShapeDtypeStruct((SEQ, ATTN_W), F32), jax.ShapeDtypeStruct((SEQ, ATTN_W), F32),
                   jax.ShapeDtypeStruct((SEQ, ATTN_W), BF16)],
        scratch_shapes=[pltpu.VMEM((SEQ, LANES), F32), pltpu.VMEM((SEQ, LANES), F32)],
        compiler_params=_params(),
    )(proj, proj, proj, proj, gq2, gk2)


def _attn_bwd(proj, o, lse, dyc, gq2, gk2):
    tn = 512
    npairs = ATTN_W // LANES

    def body(proj_hbm, o_hbm, l_hbm, dyc_hbm, gq_ref, gk_ref,
             dq_ref, dk_ref, dv_ref, dgt_ref, gqg_ref, gkg_ref,
             qb_, kb_, vb_, gb_, ob_, lb_, yb_, dkb_, dvb_):
        pair = pl.program_id(0)
        bd = _head_blockdiag()
        lo = _lo_mask(CHUNK)
        lo2 = lax.broadcasted_iota(jnp.int32, (2 * CHUNK, LANES), 1) < HEAD_DIM
        valid1, valid2 = _band_masks()
        gqs = gq_ref[...] * QK_SCALE
        gk = gk_ref[...]

        def pcol(c0):
            return proj_hbm.at[:, pl.ds(pl.multiple_of(c0 + pair * LANES, LANES), LANES)]

        def acol(hbm, c0=0):
            return hbm.at[:, pl.ds(pl.multiple_of(c0 + pair * LANES, LANES), LANES)]

        pltpu.sync_copy(pcol(C_AQ), qb_)
        pltpu.sync_copy(pcol(C_AK), kb_)
        pltpu.sync_copy(pcol(C_AV), vb_)
        pltpu.sync_copy(pcol(C_AG), gb_)
        pltpu.sync_copy(acol(o_hbm), ob_)
        pltpu.sync_copy(acol(l_hbm), lb_)
        pltpu.sync_copy(acol(dyc_hbm, GMLP_W), yb_)

        @pl.when(pair == 0)
        def _():
            gqg_ref[...] = jnp.zeros_like(gqg_ref)
            gkg_ref[...] = jnp.zeros_like(gkg_ref)

        def pre(t, carry):
            rows = pl.ds(pl.multiple_of(t * tn, tn), tn)
            g = gb_[rows, :]
            ov = ob_[rows, :]
            dya = yb_[rows, :]
            sg = _sigmoid(g)
            dgt_ref[rows, :] = (dya * ov * (sg * (1.0 + g * (1.0 - sg)))).astype(BF16)
            do = dya * (g * sg)
            yb_[rows, :] = do
            ob_[rows, :] = _headsum(do * ov, bd)
            zero = jnp.zeros((tn, LANES), F32)
            gb_[rows, :] = zero
            dkb_[rows, :] = zero
            dvb_[rows, :] = zero
            q = qb_[rows, :]
            qb_[rows, :] = q * lax.rsqrt(_headsum(q * q, bd) * (1.0 / HEAD_DIM) + EPS) * gqs
            k = kb_[rows, :]
            kb_[rows, :] = k * lax.rsqrt(_headsum(k * k, bd) * (1.0 / HEAD_DIM) + EPS) * gk
            return carry

        lax.fori_loop(0, SEQ // tn, pre, 0)

        def block(d, start, prev):
            qv = _rows_of(qb_, start, d)[...]
            dov = _rows_of(yb_, start, d)[...]
            lv = _rows_of(lb_, start, d)[...]
            dl = _rows_of(ob_, start, d)[...]
            if prev is None:
                kc = _rows_of(kb_, start, d)[...]
                vc = _rows_of(vb_, start, d)[...]
                valid, lok = valid1, lo
            else:
                kc = jnp.concatenate([_rows_of(kb_, prev, d)[...], _rows_of(kb_, start, d)[...]], axis=0)
                vc = jnp.concatenate([_rows_of(vb_, prev, d)[...], _rows_of(vb_, start, d)[...]], axis=0)
                valid, lok = valid2, lo2
            kcb = kc.astype(BF16)
            vcb = vc.astype(BF16)
            qvb = qv.astype(BF16)
            dovb = dov.astype(BF16)
            parts = []
            for h in range(2):
                mh = lo if h == 0 else ~lo
                hc = h * HEAD_DIM
                qh = jnp.where(mh, qv, 0.0).astype(BF16)
                doh = jnp.where(mh, dov, 0.0).astype(BF16)
                s = jnp.where(valid, _dot_nt(qh, kcb), -jnp.inf)
                p = jnp.exp(s - lv[:, hc:hc + 1])
                dp = _dot_nt(doh, vcb)
                ds = (p * (dp - dl[:, hc:hc + 1])).astype(BF16)
                parts.append((_dot(ds, kcb), _dot_tn(ds, qvb), _dot_tn(p.astype(BF16), dovb)))
            dqb = jnp.where(lo, parts[0][0], parts[1][0])
            dkc = jnp.where(lok, parts[0][1], parts[1][1])
            dvc = jnp.where(lok, parts[0][2], parts[1][2])
            dq_rows = _rows_of(gb_, start, d)
            dq_rows[...] = dq_rows[...] + dqb
            if prev is None:
                dk_rows = _rows_of(dkb_, start, d)
                dk_rows[...] = dk_rows[...] + dkc
                dv_rows = _rows_of(dvb_, start, d)
                dv_rows[...] = dv_rows[...] + dvc
            else:
                for st, sl in ((prev, slice(0, CHUNK)), (start, slice(CHUNK, 2 * CHUNK))):
                    dk_rows = _rows_of(dkb_, st, d)
                    dk_rows[...] = dk_rows[...] + dkc[sl]
                    dv_rows = _rows_of(dvb_, st, d)
                    dv_rows[...] = dv_rows[...] + dvc[sl]

        for d in DILATIONS:
            _for_blocks(d, functools.partial(block, d, prev=None), functools.partial(block, d))

        pltpu.sync_copy(pcol(C_AQ), qb_)
        pltpu.sync_copy(pcol(C_AK), kb_)

        def post(t, carry):
            gq_acc, gk_acc = carry
            rows = pl.ds(pl.multiple_of(t * tn, tn), tn)
            outs = []
            for raw_, acc_, gain in ((qb_, gb_, gqs), (kb_, dkb_, gk)):
                a = raw_[rows, :]
                r = lax.rsqrt(_headsum(a * a, bd) * (1.0 / HEAD_DIM) + EPS)
                z = a * r
                dn = acc_[rows, :]
                dz = dn * gain
                outs.append((r * (dz - z * (_headsum(dz * z, bd) * (1.0 / HEAD_DIM))), jnp.sum(dn * z, axis=0, keepdims=True)))
            dq_ref[rows, :] = outs[0][0].astype(BF16)
            dk_ref[rows, :] = outs[1][0].astype(BF16)
            dv_ref[rows, :] = dvb_[rows, :].astype(BF16)
            return gq_acc + outs[0][1] * QK_SCALE, gk_acc + outs[1][1]

        zero = jnp.zeros((1, LANES), F32)
        gq_acc, gk_acc = lax.fori_loop(0, SEQ // tn, post, (zero, zero))
        gqg_ref[...] += gq_acc
        gkg_ref[...] += gk_acc

        @pl.when(pair == npairs - 1)
        def _():
            gqg_ref[...] = _fold_heads(gqg_ref[...])
            gkg_ref[...] = _fold_heads(gkg_ref[...])

    hbm = pl.BlockSpec(memory_space=pl.ANY)
    vec = pl.BlockSpec((1, LANES), lambda p: (0, 0))
    out = pl.BlockSpec((SEQ, LANES), lambda p: (0, p))
    big = jax.ShapeDtypeStruct((SEQ, ATTN_W), BF16)
    return _call(
        body, name="attn_bwd", grid=(npairs,),
        in_specs=[hbm, hbm, hbm, hbm, vec, vec],
        out_specs=[out, out, out, out, vec, vec],
        out_shape=[big, big, big, big, jax.ShapeDtypeStruct((1, LANES), F32), jax.ShapeDtypeStruct((1, LANES), F32)],
        scratch_shapes=[pltpu.VMEM((SEQ, LANES), F32) for _ in range(9)],
        compiler_params=_params(),
    )(proj, o, lse, dyc, gq2, gk2)


def _mem_kv(mem, gain, wkv):
    def body(m_ref, g_ref, w_ref, kv_ref, hm_ref):
        mv = m_ref[...]
        ms = jnp.mean(mv * mv, axis=-1, keepdims=True)
        hm = (mv * lax.rsqrt(ms + EPS) * g_ref[...]).astype(BF16)
        hm_ref[...] = hm
        kv_ref[...] = _dot(hm, w_ref[...])

    return _call(
        body, name="mem_kv",
        out_shape=[jax.ShapeDtypeStruct((MEM_LEN, 2 * MEM_W), F32), jax.ShapeDtypeStruct((MEM_LEN, D_MODEL), BF16)],
        compiler_params=_params(),
    )(mem, gain, wkv)


def _mem_keys(kv_ref, kg_ref, bd, p):
    mk = kv_ref[:, p * LANES:(p + 1) * LANES]
    r = lax.rsqrt(_headsum(mk * mk, bd) * (1.0 / HEAD_DIM) + EPS)
    z = mk * r
    mkn = (z * kg_ref[:, p * LANES:(p + 1) * LANES]).astype(BF16)
    mvp = kv_ref[:, MEM_W + p * LANES:MEM_W + (p + 1) * LANES].astype(BF16)
    return mkn, mvp, r, z


def _mem_fwd(proj, kv, qg4, kg4):
    tm = 512

    def body(q_ref, g_ref, kv_ref, qg_ref, kg_ref, om_ref, ym_ref):
        bd = _head_blockdiag()
        lo = _lo_mask(tm)
        for p in range(2):
            cs = slice(p * LANES, (p + 1) * LANES)
            mkn, mvp, _, _ = _mem_keys(kv_ref, kg_ref, bd, p)
            q = q_ref[:, cs]
            qn = q * lax.rsqrt(_headsum(q * q, bd) * (1.0 / HEAD_DIM) + EPS) * (qg_ref[:, cs] * QK_SCALE)
            res = []
            for h in range(2):
                qh = jnp.where(lo if h == 0 else ~lo, qn, 0.0).astype(BF16)
                s = _dot_nt(qh, mkn)
                e = jnp.exp(s - jnp.max(s, axis=-1, keepdims=True))
                res.append(_dot(e.astype(BF16), mvp) * (1.0 / jnp.sum(e, axis=-1, keepdims=True)))
            ov = jnp.where(lo, res[0], res[1])
            g = g_ref[:, cs]
            om_ref[:, cs] = ov
            ym_ref[:, cs] = (ov * (g * _sigmoid(g))).astype(BF16)

    vec = pl.BlockSpec((1, MEM_W), lambda i: (0, 0))
    return _call(
        body, name="mem_fwd", grid=(SEQ // tm,),
        in_specs=[pl.BlockSpec((tm, MEM_W), lambda i: (i, C_MQ // MEM_W)),
                  pl.BlockSpec((tm, MEM_W), lambda i: (i, C_MG // MEM_W)),
                  pl.BlockSpec((MEM_LEN, 2 * MEM_W), lambda i: (0, 0)), vec, vec],
        out_specs=[pl.BlockSpec((tm, MEM_W), lambda i: (i, 0)), pl.BlockSpec((tm, MEM_W), lambda i: (i, 0))],
        out_shape=[jax.ShapeDtypeStruct((SEQ, MEM_W), F32), jax.ShapeDtypeStruct((SEQ, MEM_W), BF16)],
        compiler_params=_params(),
    )(proj, proj, kv, qg4, kg4)


def _mem_bwd(proj, om, dyc, kv, hm, mem, mgain, wkv, qg4, kg4):
    tm = 512
    nsteps = SEQ // tm

    def body(q_ref, g_ref, om_ref, dy_ref, kv_ref, hm_ref, mem_ref, mg_ref, w_ref, qg_ref, kg_ref,
             dq_ref, dgt_ref, gqg_ref, gkg_ref, gw_ref, gmg_ref, dmk_ref, dmv_ref, gq_acc):
        i = pl.program_id(0)
        bd = _head_blockdiag()
        lo = _lo_mask(tm)
        lom = _lo_mask(MEM_LEN)

        @pl.when(i == 0)
        def _():
            dmk_ref[...] = jnp.zeros_like(dmk_ref)
            dmv_ref[...] = jnp.zeros_like(dmv_ref)
            gq_acc[...] = jnp.zeros_like(gq_acc)

        for p in range(2):
            cs = slice(p * LANES, (p + 1) * LANES)
            mkn, mvp, _, _ = _mem_keys(kv_ref, kg_ref, bd, p)
            gqs = qg_ref[:, cs] * QK_SCALE
            q = q_ref[:, cs]
            r = lax.rsqrt(_headsum(q * q, bd) * (1.0 / HEAD_DIM) + EPS)
            z = q * r
            qn = z * gqs
            qnb = qn.astype(BF16)
            g = g_ref[:, cs]
            ov = om_ref[:, cs]
            dym = dy_ref[:, cs]
            sg = _sigmoid(g)
            dgt_ref[:, cs] = (dym * ov * (sg * (1.0 + g * (1.0 - sg)))).astype(BF16)
            do = dym * (g * sg)
            dob = do.astype(BF16)
            delta = _headsum(do * ov, bd)
            parts = []
            for h in range(2):
                mh = lo if h == 0 else ~lo
                hc = h * HEAD_DIM
                qh = jnp.where(mh, qn, 0.0).astype(BF16)
                doh = jnp.where(mh, do, 0.0).astype(BF16)
                s = _dot_nt(qh, mkn)
                e = jnp.exp(s - jnp.max(s, axis=-1, keepdims=True))
                pr = e * (1.0 / jnp.sum(e, axis=-1, keepdims=True))
                dp = _dot_nt(doh, mvp)
                ds = (pr * (dp - delta[:, hc:hc + 1])).astype(BF16)
                parts.append((_dot(ds, mkn), _dot_tn(ds, qnb), _dot_tn(pr.astype(BF16), dob)))
            dqn = jnp.where(lo, parts[0][0], parts[1][0])
            dmk_ref[:, cs] += jnp.where(lom, parts[0][1], parts[1][1])
            dmv_ref[:, cs] += jnp.where(lom, parts[0][2], parts[1][2])
            dz = dqn * gqs
            dq_ref[:, cs] = (r * (dz - z * (_headsum(dz * z, bd) * (1.0 / HEAD_DIM)))).astype(BF16)
            gq_acc[:, cs] += jnp.sum(dqn * z, axis=0, keepdims=True) * QK_SCALE

        @pl.when(i == nsteps - 1)
        def _():
            gqg_ref[...] = _fold_heads(gq_acc[:, 0:LANES] + gq_acc[:, LANES:2 * LANES])
            dkv = []
            gk = jnp.zeros((1, LANES), F32)
            for p in range(2):
                cs = slice(p * LANES, (p + 1) * LANES)
                _, _, r, z = _mem_keys(kv_ref, kg_ref, bd, p)
                dn = dmk_ref[:, cs]
                dz = dn * kg_ref[:, cs]
                gk = gk + jnp.sum(dn * z, axis=0, keepdims=True)
                dkv.append(r * (dz - z * (_headsum(dz * z, bd) * (1.0 / HEAD_DIM))))
            gkg_ref[...] = _fold_heads(gk)
            dkvb = jnp.concatenate(dkv + [dmv_ref[...]], axis=1).astype(BF16)
            gw_ref[...] = _dot_tn(hm_ref[...], dkvb)
            dhm = _dot_nt(dkvb, w_ref[...])
            mv = mem_ref[...]
            zm = mv * lax.rsqrt(jnp.mean(mv * mv, axis=-1, keepdims=True) + EPS)
            gmg_ref[...] = jnp.sum(dhm * zm, axis=0, keepdims=True)

    const = lambda shape: pl.BlockSpec(shape, lambda i: (0,) * len(shape))
    row = lambda j: pl.BlockSpec((tm, MEM_W), lambda i: (i, j))
    return _call(
        body, name="mem_bwd", grid=(nsteps,),
        in_specs=[row(C_MQ // MEM_W), row(C_MG // MEM_W), row(0), row((GMLP_W + ATTN_W) // MEM_W),
                  const((MEM_LEN, 2 * MEM_W)), const((MEM_LEN, D_MODEL)), const((MEM_LEN, D_MODEL)),
                  const((1, D_MODEL)), const((D_MODEL, 2 * MEM_W)), const((1, MEM_W)), const((1, MEM_W))],
        out_specs=[row(0), row(0), const((1, LANES)), const((1, LANES)),
                   const((D_MODEL, 2 * MEM_W)), const((1, D_MODEL))],
        out_shape=[jax.ShapeDtypeStruct((SEQ, MEM_W), BF16), jax.ShapeDtypeStruct((SEQ, MEM_W), BF16),
                   jax.ShapeDtypeStruct((1, LANES), F32), jax.ShapeDtypeStruct((1, LANES), F32),
                   jax.ShapeDtypeStruct((D_MODEL, 2 * MEM_W), F32), jax.ShapeDtypeStruct((1, D_MODEL), F32)],
        scratch_shapes=[pltpu.VMEM((MEM_LEN, MEM_W), F32), pltpu.VMEM((MEM_LEN, MEM_W), F32),
                        pltpu.VMEM((1, MEM_W), F32)],
        compiler_params=_params(),
    )(proj, proj, om, dyc, kv, hm, mem, mgain, wkv, qg4, kg4)


def _out_loss(yg, ya, ym, x, tgt, wo):
    tm = 256
    nsteps = SEQ // tm
    parts = ((0, GMLP_W), (GMLP_W, ATTN_W), (GMLP_W + ATTN_W, MEM_W))

    def body(yg_ref, ya_ref, ym_ref, x_ref, t_ref, w_ref, dy_ref, dyc_ref, gw_ref, ls_ref):
        i = pl.program_id(0)

        @pl.when(i == 0)
        def _():
            gw_ref[...] = jnp.zeros_like(gw_ref)
            ls_ref[...] = jnp.zeros_like(ls_ref)

        ys = (yg_ref[...], ya_ref[...], ym_ref[...])
        y = sum(_dot(yv, w_ref[r0:r0 + n, :]) for yv, (r0, n) in zip(ys, parts))
        err = x_ref[...] + y - t_ref[...]
        ls_ref[...] += jnp.sum(err * err, axis=0, keepdims=True)
        dy = err * (1.0 / D_MODEL)
        dy_ref[...] = dy
        dyb = dy.astype(BF16)
        dyc_ref[...] = _dot_nt(dyb, w_ref[...])
        for yv, (r0, n) in zip(ys, parts):
            gw_ref[r0:r0 + n, :] += _dot_tn(yv, dyb)

    row = lambda w: pl.BlockSpec((tm, w), lambda i: (i, 0))
    const = lambda shape: pl.BlockSpec(shape, lambda i: (0, 0))
    return _call(
        body, name="out_loss", grid=(nsteps,),
        in_specs=[row(GMLP_W), row(ATTN_W), row(MEM_W), row(D_MODEL), row(D_MODEL), const((D_MODEL, D_MODEL))],
        out_specs=[row(D_MODEL), row(D_MODEL), const((D_MODEL, D_MODEL)), const((1, D_MODEL))],
        out_shape=[jax.ShapeDtypeStruct((SEQ, D_MODEL), F32), jax.ShapeDtypeStruct((SEQ, D_MODEL), F32),
                   jax.ShapeDtypeStruct((D_MODEL, D_MODEL), F32), jax.ShapeDtypeStruct((1, D_MODEL), F32)],
        compiler_params=_params(),
    )(yg, ya, ym, x, tgt, wo)


def _proj_bwd(x, dy, gain, wt, dg, daq, dak, dav, dag, dmq, dmg):
    tm = 256
    nsteps = SEQ // tm
    pieces = ((C_GU, 3 * GMLP_W), (C_AQ, ATTN_W), (C_AK, ATTN_W), (C_AV, ATTN_W), (C_AG, ATTN_W),
              (C_MQ, MEM_W), (C_MG, MEM_W))

    def body(x_ref, dy_ref, g_ref, wt_hbm, p0, p1, p2, p3, p4, p5, p6, gx_ref, gwt_hbm, gg_ref, wt_v, acc):
        i = pl.program_id(0)

        @pl.when(i == 0)
        def _():
            pltpu.sync_copy(wt_hbm, wt_v)
            acc[...] = jnp.zeros_like(acc)
            gg_ref[...] = jnp.zeros_like(gg_ref)

        xv = x_ref[...]
        r = lax.rsqrt(jnp.mean(xv * xv, axis=-1, keepdims=True) + EPS)
        z = xv * r
        g = g_ref[...]
        h = (z * g).astype(BF16)
        dh = jnp.zeros((tm, D_MODEL), F32)
        for pref, (c0, w) in zip((p0, p1, p2, p3, p4, p5, p6), pieces):
            dp = pref[...]
            dh = dh + _dot(dp, wt_v[c0:c0 + w, :])
            acc[c0:c0 + w, :] += _dot_tn(dp, h)
        gg_ref[...] += jnp.sum(dh * z, axis=0, keepdims=True)
        dz = dh * g
        gx_ref[...] = dy_ref[...] + r * (dz - z * jnp.mean(dz * z, axis=-1, keepdims=True))

        @pl.when(i == nsteps - 1)
        def _():
            pltpu.sync_copy(acc, gwt_hbm)

    row = lambda w: pl.BlockSpec((tm, w), lambda i: (i, 0))
    hbm = pl.BlockSpec(memory_space=pl.ANY)
    vec = pl.BlockSpec((1, D_MODEL), lambda i: (0, 0))
    return _call(
        body, name="proj_bwd", grid=(nsteps,),
        in_specs=[row(D_MODEL), row(D_MODEL), vec, hbm] + [row(w) for _, w in pieces],
        out_specs=[row(D_MODEL), hbm, vec],
        out_shape=[jax.ShapeDtypeStruct((SEQ, D_MODEL), F32), jax.ShapeDtypeStruct((IN_W, D_MODEL), F32),
                   jax.ShapeDtypeStruct((1, D_MODEL), F32)],
        scratch_shapes=[pltpu.VMEM((IN_W, D_MODEL), BF16), pltpu.VMEM((IN_W, D_MODEL), F32)],
        compiler_params=_params(),
    )(x, dy, gain, wt, dg, daq, dak, dav, dag, dmq, dmg)


def _gather_weights(wt_sh, wkv_sh, wo_sh):
    shards = (wt_sh, wkv_sh, wo_sh)
    nrows = tuple(a.shape[0] for a in shards)

    def body(a0, a1, a2, o0, o1, o2, send_sems, recv_sems):
        x, y, c = lax.axis_index("x"), lax.axis_index("y"), lax.axis_index("c")
        me = 2 * x + y
        sib = (x, y, 1 - c)
        chips = ((1 - x, y), (x, 1 - y), (1 - x, 1 - y))
        ins, outs = (a0, a1, a2), (o0, o1, o2)

        def half(a, chip, hf):
            n = nrows[a] // 2
            return outs[a].at[pl.ds(pl.multiple_of(chip * nrows[a] + hf * n, 16), n), :]

        def copy(k, ref, to):
            return pltpu.make_async_remote_copy(src_ref=ref, dst_ref=ref, send_sem=send_sems.at[k],
                                                recv_sem=recv_sems.at[k], device_id=to, device_id_type=MESH)

        for a in range(3):
            outs[a][pl.ds(pl.multiple_of(me * nrows[a], 16), nrows[a]), :] = ins[a][...].astype(BF16)
        started = []
        for a in range(3):
            for j, (px, py) in enumerate(chips):
                cp = copy(a * 6 + j * 2, half(a, me, c), (px, py, c))
                cp.start()
                started.append(cp)
        for a in range(3):
            for j, (px, py) in enumerate(chips):
                landed = half(a, 2 * px + py, c)
                copy(a * 6 + j * 2, landed, (px, py, c)).wait_recv()
                fw = copy(a * 6 + j * 2 + 1, landed, sib)
                fw.start()
                started.append(fw)
        for a in range(3):
            for j, (px, py) in enumerate(chips):
                copy(a * 6 + j * 2 + 1, half(a, 2 * px + py, 1 - c), sib).wait_recv()
        for cp in started:
            cp.wait_send()

    return _call(
        body, name="gather_weights",
        out_shape=[jax.ShapeDtypeStruct((4 * a.shape[0], a.shape[1]), BF16) for a in shards],
        in_specs=[pl.BlockSpec(memory_space=pltpu.VMEM)] * 3,
        out_specs=[pl.BlockSpec(memory_space=pltpu.VMEM)] * 3,
        scratch_shapes=[pltpu.SemaphoreType.DMA((18,)), pltpu.SemaphoreType.DMA((18,))],
        compiler_params=_params(),
    )(*shards)


def _reduce_grads(gwt, gwkv, gwo, small):
    bigs = (gwt, gwkv, gwo)
    piece = tuple(a.shape[0] // 8 for a in bigs)
    width = tuple(a.shape[1] for a in bigs)
    views = tuple(a.reshape(4, 2, r, w) for a, r, w in zip(bigs, piece, width))

    def body(g0, g1, g2, sm, o0, o1, o2, osm,
             l0, l1, l2, ra0, ra1, ra2, rb0, rb1, rb2, rc0, rc1, rc2, sa, sb, sc, acc_s,
             send_sems, recv_sems, local_sems):
        x, y, c = lax.axis_index("x"), lax.axis_index("y"), lax.axis_index("c")
        sib, xn, yn = (x, y, 1 - c), (1 - x, y, c), (x, 1 - y, c)
        gs, outs = (g0, g1, g2), (o0, o1, o2)
        loc, ra, rb, rc = (l0, l1, l2), (ra0, ra1, ra2), (rb0, rb1, rb2), (rc0, rc1, rc2)

        def copy(k, src, dst, to):
            return pltpu.make_async_remote_copy(src_ref=src, dst_ref=dst, send_sem=send_sems.at[k],
                                                recv_sem=recv_sems.at[k], device_id=to, device_id_type=MESH)

        started = []

        def go(cp):
            cp.start()
            started.append(cp)

        mine = []
        for a in range(3):
            go(copy(a, gs[a].at[:, 1 - c], ra[a], sib))
            cp = pltpu.make_async_copy(gs[a].at[:, c], loc[a], local_sems.at[a])
            cp.start()
            mine.append(cp)
        go(copy(3, sm, sa, sib))
        for a in range(3):
            mine[a].wait()
            copy(a, gs[a].at[:, 1 - c], ra[a], sib).wait_recv()
            ra[a][...] = loc[a][...] + ra[a][...]
            go(copy(4 + a, ra[a].at[pl.ds(2 * (1 - x), 2)], rb[a], xn))
        copy(3, sm, sa, sib).wait_recv()
        acc_s[...] = sm[...] + sa[...]
        go(copy(7, acc_s, sb, xn))
        for a in range(3):
            copy(4 + a, ra[a].at[pl.ds(2 * (1 - x), 2)], rb[a], xn).wait_recv()
            rb[a][...] = ra[a][pl.ds(2 * x, 2)] + rb[a][...]
            go(copy(8 + a, rb[a].at[1 - y], rc[a], yn))
        copy(7, acc_s, sb, xn).wait_recv()
        sb[...] = acc_s[...] + sb[...]
        go(copy(11, sb, sc, yn))
        for a in range(3):
            copy(8 + a, rb[a].at[1 - y], rc[a], yn).wait_recv()
            mine_rows = outs[a].at[pl.ds(pl.multiple_of(c * piece[a], 8), piece[a]), :]
            mine_rows[...] = rb[a][y] + rc[a][...]
            go(copy(12 + a, mine_rows, mine_rows, sib))
        copy(11, sb, sc, yn).wait_recv()
        osm[...] = sb[...] + sc[...]
        for a in range(3):
            theirs = outs[a].at[pl.ds(pl.multiple_of((1 - c) * piece[a], 8), piece[a]), :]
            copy(12 + a, theirs, theirs, sib).wait_recv()
        for cp in started:
            cp.wait_send()

    vm = pl.BlockSpec(memory_space=pltpu.VMEM)
    hbm = pl.BlockSpec(memory_space=pl.ANY)
    scratch = []
    for n, shape_of in ((4, None), (4, None), (2, None), (0, None)):
        for r, w in zip(piece, width):
            scratch.append(pltpu.VMEM((n, r, w) if n else (r, w), F32))
    scratch += [pltpu.VMEM(small.shape, F32) for _ in range(4)]
    scratch += [pltpu.SemaphoreType.DMA((15,)), pltpu.SemaphoreType.DMA((15,)), pltpu.SemaphoreType.DMA((3,))]
    return _call(
        body, name="reduce_grads",
        out_shape=[jax.ShapeDtypeStruct((2 * r, w), F32) for r, w in zip(piece, width)]
        + [jax.ShapeDtypeStruct(small.shape, F32)],
        in_specs=[hbm, hbm, hbm, vm],
        out_specs=[vm, vm, vm, vm],
        scratch_shapes=scratch,
        compiler_params=_params(),
    )(*views, small)


def _adamw(w, g, m, v):
    rows, cols = w.shape
    tm = 256 if rows % 256 == 0 and rows > 256 else rows

    def body(w_ref, g_ref, m_ref, v_ref, d_ref, nm_ref, nv_ref):
        gv = g_ref[...]
        nm = ADAM_B1 * m_ref[...] + (1.0 - ADAM_B1) * gv
        nv = ADAM_B2 * v_ref[...] + (1.0 - ADAM_B2) * (gv * gv)
        m_hat = nm / (1.0 - ADAM_B1 ** ADAM_STEP)
        v_hat = nv / (1.0 - ADAM_B2 ** ADAM_STEP)
        d_ref[...] = -ADAM_LR * (m_hat / (jnp.sqrt(v_hat) + ADAM_EPS) + ADAM_WD * w_ref[...])
        nm_ref[...] = nm
        nv_ref[...] = nv

    blk = pl.BlockSpec((tm, cols), lambda i: (i, 0))
    return _call(
        body, name="adamw", grid=(rows // tm,),
        in_specs=[blk] * 4, out_specs=[blk] * 3,
        out_shape=[jax.ShapeDtypeStruct((rows, cols), F32)] * 3,
        compiler_params=_params(),
    )(w, g, m, v)


def _row128(a):
    return jnp.pad(a.reshape(1, HEAD_DIM), ((0, 0), (0, LANES - HEAD_DIM)))


def _pack_small(ws, ng, mng, vg, b, aq, ak, mq, mk):
    rows = [ws.reshape(512, LANES), ng.reshape(8, LANES), mng.reshape(8, LANES), vg.reshape(2, LANES),
            b.reshape(4, LANES), _row128(aq), _row128(ak), _row128(mq), _row128(mk),
            jnp.zeros((SM_ROWS - SM_MKG - 1, LANES), F32)]
    return jnp.concatenate(rows, axis=0)


def _unpack_small(p):
    gain = lambda r: p[r:r + 1, :HEAD_DIM]
    return (p[SM_NG:SM_NG + 8].reshape(1, D_MODEL), p[SM_VG:SM_VG + 2].reshape(1, 4, HEAD_DIM),
            p[SM_WS:SM_WS + 512].reshape(1, 4, CHUNK, CHUNK), p[SM_B:SM_B + 4].reshape(1, 4, CHUNK),
            gain(SM_AQ), gain(SM_AK), p[SM_MNG:SM_MNG + 8].reshape(1, D_MODEL), gain(SM_MQG), gain(SM_MKG))


def _local_grads(x, mem, tgt, norm_gain, wt, gmlp_v_gain, gmlp_w_s, gmlp_b, attn_q_gain, attn_k_gain,
                 mem_norm_gain, wkv, mem_q_gain, mem_k_gain, wo):
    vg = gmlp_v_gain.reshape(1, GMLP_W)
    bias_full = jnp.repeat(gmlp_b.T, HEAD_DIM, axis=1)
    gq2, gk2 = jnp.tile(attn_q_gain, (1, 2)), jnp.tile(attn_k_gain, (1, 2))
    qg4, kg4 = jnp.tile(mem_q_gain, (1, 4)), jnp.tile(mem_k_gain, (1, 4))

    proj = _fwd_proj(x, norm_gain, wt)
    yg = _gmlp_fwd(proj, vg, gmlp_w_s, bias_full)
    o, lse, ya = _attn_fwd(proj, gq2, gk2)
    kv, hm = _mem_kv(mem, mem_norm_gain, wkv)
    om, ym = _mem_fwd(proj, kv, qg4, kg4)
    dy, dyc, g_wo, err2 = _out_loss(yg, ya, ym, x, tgt, wo)
    dg, g_ws, g_bfull, g_vg = _gmlp_bwd(proj, dyc, vg, gmlp_w_s, bias_full)
    daq, dak, dav, dag, g_aq, g_ak = _attn_bwd(proj, o, lse, dyc, gq2, gk2)
    dmq, dmg, g_mq, g_mk, g_wkv, g_mng = _mem_bwd(proj, om, dyc, kv, hm, mem, mem_norm_gain, wkv, qg4, kg4)
    gx, g_wt, g_ng = _proj_bwd(x, dy, norm_gain, wt, dg, daq, dak, dav, dag, dmq, dmg)

    loss = (0.5 / D_MODEL) * jnp.sum(err2)
    g_b = g_bfull[:, ::HEAD_DIM].T
    small = _pack_small(g_ws, g_ng, g_mng, g_vg, g_b, g_aq[:, :HEAD_DIM], g_ak[:, :HEAD_DIM],
                        g_mq[:, :HEAD_DIM], g_mk[:, :HEAD_DIM])
    return loss, gx, g_wt, g_wkv, g_wo, small


def kernel(x, mem, norm_gain, w_in, gmlp_v_gain, gmlp_w_s, gmlp_b, attn_q_gain, attn_k_gain, mem_norm_gain, w_mem_kv, mem_q_gain, mem_k_gain, w_out, loss_target, m_norm_gain, m_w_in, m_gmlp_v_gain, m_gmlp_w_s, m_gmlp_b, m_attn_q_gain, m_attn_k_gain, m_mem_norm_gain, m_w_mem_kv, m_mem_q_gain, m_mem_k_gain, m_w_out, v_norm_gain, v_w_in, v_gmlp_v_gain, v_gmlp_w_s, v_gmlp_b, v_attn_q_gain, v_attn_k_gain, v_mem_norm_gain, v_w_mem_kv, v_mem_q_gain, v_mem_k_gain, v_w_out):
    wt, wkv, wo = _gather_weights(w_in[0].T, w_mem_kv[0], w_out[0])
    loss, gx, g_wt, g_wkv, g_wo, small = _local_grads(
        x[0], mem[0], loss_target[0], norm_gain, wt, gmlp_v_gain[0], gmlp_w_s[0], gmlp_b[0],
        attn_q_gain, attn_k_gain, mem_norm_gain, wkv, mem_q_gain, mem_k_gain, wo)
    g_wt_sh, g_wkv_sh, g_wo_sh, g_small = _reduce_grads(g_wt, g_wkv, g_wo, small)
    loss = lax.psum(loss, ("x", "y", "c"))

    pack = lambda p: _pack_small(p[3][0], p[0], p[7], p[2][0], p[4][0], p[5], p[6], p[9], p[10])
    ws = (norm_gain, w_in, gmlp_v_gain, gmlp_w_s, gmlp_b, attn_q_gain, attn_k_gain, mem_norm_gain, w_mem_kv,
          mem_q_gain, mem_k_gain, w_out)
    ms = (m_norm_gain, m_w_in, m_gmlp_v_gain, m_gmlp_w_s, m_gmlp_b, m_attn_q_gain, m_attn_k_gain, m_mem_norm_gain,
          m_w_mem_kv, m_mem_q_gain, m_mem_k_gain, m_w_out)
    vs = (v_norm_gain, v_w_in, v_gmlp_v_gain, v_gmlp_w_s, v_gmlp_b, v_attn_q_gain, v_attn_k_gain, v_mem_norm_gain,
          v_w_mem_kv, v_mem_q_gain, v_mem_k_gain, v_w_out)
    g_in = g_wt_sh.T
    big = {1: g_in, 8: g_wkv_sh, 11: g_wo_sh}
    upd = {i: _adamw(ws[i][0], big[i], ms[i][0], vs[i][0]) for i in big}
    upd_small = _adamw(pack(ws), g_small, pack(ms), pack(vs))
    g_sm = _unpack_small(g_small)
    d_sm, m_sm, v_sm = (_unpack_small(u) for u in upd_small)
    small_pos = (0, 2, 3, 4, 5, 6, 7, 9, 10)

    def leaves(big_of, small_of):
        out = [None] * 12
        for i in big:
            out[i] = big_of(i)[None]
        for j, i in enumerate(small_pos):
            out[i] = small_of[j]
        return out

    grads = leaves(lambda i: big[i], g_sm)
    deltas = leaves(lambda i: upd[i][0], d_sm)
    new_m = leaves(lambda i: upd[i][1], m_sm)
    new_v = leaves(lambda i: upd[i][2], v_sm)
    return (loss, gx[None], *grads, *deltas, *new_m, *new_v)
```

```python
import functools
import math

import jax
import jax.numpy as jnp
from jax import lax
from jax.experimental import pallas as pl
from jax.experimental.pallas import tpu as pltpu

F32 = jnp.float32
BF16 = jnp.bfloat16

SEQ = 4096
D_MODEL = 1024
HEAD_DIM = 64
LANES = 128
CHUNK = 128
GMLP_W, ATTN_W, MEM_W = 256, 512, 256
IN_W = 3 * GMLP_W + 4 * ATTN_W + 2 * MEM_W
MEM_LEN = 256
DILATIONS = (1, 4, 16)
EPS = 1e-6
QK_SCALE = 1.0 / math.sqrt(HEAD_DIM)
C_GU, C_GV, C_GG, C_AQ, C_AK, C_AV, C_AG, C_MQ, C_MG = 0, 256, 512, 768, 1280, 1792, 2304, 2816, 3072

ADAM_LR, ADAM_B1, ADAM_B2, ADAM_EPS, ADAM_WD, ADAM_STEP = 0.001, 0.9, 0.999, 1e-08, 0.01, 10

VMEM_LIMIT = 48 * 1024 * 1024
ATTN_UNROLL = 4
MESH = pl.DeviceIdType.MESH

SM_WS, SM_NG, SM_MNG, SM_VG, SM_B, SM_AQ, SM_AK, SM_MQG, SM_MKG, SM_ROWS = 0, 512, 520, 528, 530, 534, 535, 536, 537, 544


def _call(body, **kw):
    return pl.pallas_call(body, **kw)


def _params(**kw):
    return pltpu.CompilerParams(vmem_limit_bytes=VMEM_LIMIT, **kw)


def _dot(a, b):
    return jnp.dot(a, b, preferred_element_type=F32)


def _dot_nt(a, b):
    return lax.dot_general(a, b, (((1,), (1,)), ((), ())), preferred_element_type=F32)


def _dot_tn(a, b):
    return lax.dot_general(a, b, (((0,), (0,)), ((), ())), preferred_element_type=F32)


def _head_blockdiag():
    r = lax.shift_right_logical(lax.broadcasted_iota(jnp.int32, (LANES, LANES), 0), 6)
    c = lax.shift_right_logical(lax.broadcasted_iota(jnp.int32, (LANES, LANES), 1), 6)
    return jnp.where(r == c, 1.0, 0.0).astype(BF16)


def _headsum(v, bd):
    hi = v.astype(BF16)
    lo = (v - hi.astype(F32)).astype(BF16)
    return _dot(hi, bd) + _dot(lo, bd)


def _lo_mask(rows):
    return lax.broadcasted_iota(jnp.int32, (rows, LANES), 1) < HEAD_DIM


def _sigmoid(x):
    return 1.0 / (1.0 + jnp.exp(-x))


def _fold_heads(v):
    return v + pltpu.roll(v, HEAD_DIM, 1)


def _fwd_proj(x, gain, wt):
    tm = 256

    def body(x_ref, g_ref, wt_ref, o_ref):
        xv = x_ref[...]
        ms = jnp.mean(xv * xv, axis=-1, keepdims=True)
        h = (xv * lax.rsqrt(ms + EPS) * g_ref[...]).astype(BF16)
        o_ref[...] = _dot_nt(h, wt_ref[...])

    return _call(
        body, name="fwd_proj", grid=(SEQ // tm,),
        in_specs=[pl.BlockSpec((tm, D_MODEL), lambda i: (i, 0)),
                  pl.BlockSpec((1, D_MODEL), lambda i: (0, 0)),
                  pl.BlockSpec((IN_W, D_MODEL), lambda i: (0, 0))],
        out_specs=pl.BlockSpec((tm, IN_W), lambda i: (i, 0)),
        out_shape=jax.ShapeDtypeStruct((SEQ, IN_W), F32),
        compiler_params=_params(),
    )(x, gain, wt)


def _gmlp_weights(w_ref):
    ti = lax.broadcasted_iota(jnp.int32, (CHUNK, CHUNK), 0)
    si = lax.broadcasted_iota(jnp.int32, (CHUNK, CHUNK), 1)
    tril = si <= ti
    return tril, [jnp.where(tril, w_ref[h], 0.0).astype(BF16) for h in range(4)]


def _gmlp_fwd(proj, vgain, w_s, bias_full):
    tm = 512

    def body(p_ref, vg_ref, w_ref, b_ref, y_ref):
        bd = _head_blockdiag()
        lo = _lo_mask(CHUNK)
        _, wm = _gmlp_weights(w_ref)
        for c in range(tm // CHUNK):
            rows = pl.ds(c * CHUNK, CHUNK)
            for p in range(2):
                cs = slice(p * LANES, (p + 1) * LANES)
                u = p_ref[rows, C_GU + p * LANES:C_GU + (p + 1) * LANES]
                v = p_ref[rows, C_GV + p * LANES:C_GV + (p + 1) * LANES]
                gt = p_ref[rows, C_GG + p * LANES:C_GG + (p + 1) * LANES]
                r = lax.rsqrt(_headsum(v * v, bd) * (1.0 / HEAD_DIM) + EPS)
                vn = (v * r * vg_ref[:, cs]).astype(BF16)
                sp = jnp.where(lo, _dot(wm[2 * p], vn), _dot(wm[2 * p + 1], vn)) + b_ref[:, cs]
                y_ref[rows, cs] = (u * sp * (gt * _sigmoid(gt))).astype(BF16)

    return _call(
        body, name="gmlp_fwd", grid=(SEQ // tm,),
        in_specs=[pl.BlockSpec((tm, 3 * GMLP_W), lambda i: (i, 0)),
                  pl.BlockSpec((1, GMLP_W), lambda i: (0, 0)),
                  pl.BlockSpec((4, CHUNK, CHUNK), lambda i: (0, 0, 0)),
                  pl.BlockSpec((CHUNK, GMLP_W), lambda i: (0, 0))],
        out_specs=pl.BlockSpec((tm, GMLP_W), lambda i: (i, 0)),
        out_shape=jax.ShapeDtypeStruct((SEQ, GMLP_W), BF16),
        compiler_params=_params(),
    )(proj, vgain, w_s, bias_full)


def _gmlp_bwd(proj, dyc, vgain, w_s, bias_full):
    tm = 512
    nsteps = SEQ // tm

    def body(p_ref, dy_ref, vg_ref, w_ref, b_ref, dg_ref, gw_ref, gb_ref, gv_ref):
        i = pl.program_id(0)
        bd = _head_blockdiag()
        lo = _lo_mask(CHUNK)
        tril, wm = _gmlp_weights(w_ref)

        @pl.when(i == 0)
        def _():
            gw_ref[...] = jnp.zeros_like(gw_ref)
            gb_ref[...] = jnp.zeros_like(gb_ref)
            gv_ref[...] = jnp.zeros_like(gv_ref)

        for c in range(tm // CHUNK):
            rows = pl.ds(c * CHUNK, CHUNK)
            for p in range(2):
                cs = slice(p * LANES, (p + 1) * LANES)
                u = p_ref[rows, C_GU + p * LANES:C_GU + (p + 1) * LANES]
                v = p_ref[rows, C_GV + p * LANES:C_GV + (p + 1) * LANES]
                gt = p_ref[rows, C_GG + p * LANES:C_GG + (p + 1) * LANES]
                dy = dy_ref[rows, cs]
                g = vg_ref[:, cs]
                r = lax.rsqrt(_headsum(v * v, bd) * (1.0 / HEAD_DIM) + EPS)
                z = v * r
                vn = (z * g).astype(BF16)
                sp = jnp.where(lo, _dot(wm[2 * p], vn), _dot(wm[2 * p + 1], vn)) + b_ref[:, cs]
                sg = _sigmoid(gt)
                sl = gt * sg
                dsl = sg * (1.0 + gt * (1.0 - sg))
                du = dy * sp * sl
                dsp = dy * u * sl
                dgt = dy * u * sp * dsl
                dspb = dsp.astype(BF16)
                dvn = jnp.where(lo, _dot_tn(wm[2 * p], dspb), _dot_tn(wm[2 * p + 1], dspb))
                gw_ref[2 * p] += _dot_nt(jnp.where(lo, dsp, 0.0).astype(BF16), vn)
                gw_ref[2 * p + 1] += _dot_nt(jnp.where(lo, 0.0, dsp).astype(BF16), vn)
                gb_ref[:, cs] += _headsum(dsp, bd)
                gv_ref[:, cs] += jnp.sum(dvn * z, axis=0, keepdims=True)
                dz = dvn * g
                dv = r * (dz - z * (_headsum(dz * z, bd) * (1.0 / HEAD_DIM)))
                dg_ref[rows, C_GU + p * LANES:C_GU + (p + 1) * LANES] = du.astype(BF16)
                dg_ref[rows, C_GV + p * LANES:C_GV + (p + 1) * LANES] = dv.astype(BF16)
                dg_ref[rows, C_GG + p * LANES:C_GG + (p + 1) * LANES] = dgt.astype(BF16)

        @pl.when(i == nsteps - 1)
        def _():
            for h in range(4):
                gw_ref[h] = jnp.where(tril, gw_ref[h], 0.0)

    return _call(
        body, name="gmlp_bwd", grid=(nsteps,),
        in_specs=[pl.BlockSpec((tm, 3 * GMLP_W), lambda i: (i, 0)),
                  pl.BlockSpec((tm, GMLP_W), lambda i: (i, 0)),
                  pl.BlockSpec((1, GMLP_W), lambda i: (0, 0)),
                  pl.BlockSpec((4, CHUNK, CHUNK), lambda i: (0, 0, 0)),
                  pl.BlockSpec((CHUNK, GMLP_W), lambda i: (0, 0))],
        out_specs=[pl.BlockSpec((tm, 3 * GMLP_W), lambda i: (i, 0)),
                   pl.BlockSpec((4, CHUNK, CHUNK), lambda i: (0, 0, 0)),
                   pl.BlockSpec((CHUNK, GMLP_W), lambda i: (0, 0)),
                   pl.BlockSpec((1, GMLP_W), lambda i: (0, 0))],
        out_shape=[jax.ShapeDtypeStruct((SEQ, 3 * GMLP_W), BF16),
                   jax.ShapeDtypeStruct((4, CHUNK, CHUNK), F32),
                   jax.ShapeDtypeStruct((CHUNK, GMLP_W), F32),
                   jax.ShapeDtypeStruct((1, GMLP_W), F32)],
        compiler_params=_params(),
    )(proj, dyc, vgain, w_s, bias_full)


def _band_masks():
    qi = lax.broadcasted_iota(jnp.int32, (CHUNK, 2 * CHUNK), 0)
    kj = lax.broadcasted_iota(jnp.int32, (CHUNK, 2 * CHUNK), 1)
    valid2 = ((kj < CHUNK) & (kj >= qi)) | ((kj >= CHUNK) & (kj - CHUNK <= qi))
    q1 = lax.broadcasted_iota(jnp.int32, (CHUNK, CHUNK), 0)
    k1 = lax.broadcasted_iota(jnp.int32, (CHUNK, CHUNK), 1)
    return k1 <= q1, valid2


def _rows_of(ref, start, d):
    if d == 1:
        return ref.at[pl.ds(start if isinstance(start, int) else pl.multiple_of(start, CHUNK), CHUNK), :]
    return ref.at[pl.ds(start, CHUNK, stride=d), :]


def _unrolled(lo, hi, unroll, run):
    groups = (hi - lo) // unroll
    if groups:
        def body(g, carry):
            run([lo + g * unroll + t for t in range(unroll)])
            return carry

        lax.fori_loop(0, groups, body, 0)
    if lo + groups * unroll < hi:
        run(range(lo + groups * unroll, hi))


def _for_blocks(d, group_fn, unroll):
    nblk = SEQ // CHUNK
    sh = d.bit_length() - 1

    def first(j):
        return (j * CHUNK if d == 1 else j, None)

    def rest(j):
        start = (j & (d - 1)) + (j >> sh) * (CHUNK * d)
        return (start, start - CHUNK * d)

    _unrolled(0, d, unroll, lambda js: group_fn(d, [first(j) for j in js]))
    _unrolled(d, nblk, unroll, lambda js: group_fn(d, [rest(j) for j in js]))


def _attn_fwd(proj, gq2, gk2):
    tn = 512

    def body(q_ref, k_ref, v_ref, g_ref, gq_ref, gk_ref, o_ref, l_ref, ya_ref, qn_ref, kn_ref):
        bd = _head_blockdiag()
        lo = _lo_mask(CHUNK)
        valid1, valid2 = _band_masks()

        def norm(t, carry):
            rows = pl.ds(pl.multiple_of(t * tn, tn), tn)
            q = q_ref[rows, :]
            qn_ref[rows, :] = q * lax.rsqrt(_headsum(q * q, bd) * (1.0 / HEAD_DIM) + EPS) * (gq_ref[...] * QK_SCALE)
            k = k_ref[rows, :]
            kn_ref[rows, :] = k * lax.rsqrt(_headsum(k * k, bd) * (1.0 / HEAD_DIM) + EPS) * gk_ref[...]
            return carry

        lax.fori_loop(0, SEQ // tn, norm, 0)

        def load_kv(ref, d, start, prev):
            own = _rows_of(ref, start, d)[...]
            if prev is None:
                return own.astype(BF16)
            return jnp.concatenate([_rows_of(ref, prev, d)[...], own], axis=0).astype(BF16)

        def group(d, blocks):
            valid = valid1 if blocks[0][1] is None else valid2
            chains = [(b, h) for b in range(len(blocks)) for h in range(2)]
            qs = [_rows_of(qn_ref, start, d)[...] for start, _ in blocks]
            ks = [load_kv(kn_ref, d, start, prev) for start, prev in blocks]
            vs = [load_kv(v_ref, d, start, prev) for start, prev in blocks]
            ss = [_dot_nt(jnp.where(lo if h == 0 else ~lo, qs[b], 0.0).astype(BF16), ks[b]) for b, h in chains]
            ms, ps, ls = [], [], []
            for s in ss:
                s = jnp.where(valid, s, -jnp.inf)
                m = jnp.max(s, axis=-1, keepdims=True)
                p = jnp.exp(s - m)
                ms.append(m)
                ls.append(jnp.sum(p, axis=-1, keepdims=True))
                ps.append(p.astype(BF16))
            os_ = [_dot(p, vs[b]) for p, (b, h) in zip(ps, chains)]
            for b, (start, _) in enumerate(blocks):
                c0, c1 = 2 * b, 2 * b + 1
                ob = jnp.where(lo, os_[c0] * (1.0 / ls[c0]), os_[c1] * (1.0 / ls[c1]))
                lb = jnp.where(lo, ms[c0] + jnp.log(ls[c0]), ms[c1] + jnp.log(ls[c1]))
                o_rows = _rows_of(o_ref, start, d)
                l_rows = _rows_of(l_ref, start, d)
                if d != DILATIONS[0]:
                    lold = l_rows[...]
                    mx = jnp.maximum(lold, lb)
                    ea = jnp.exp(lold - mx)
                    eb = jnp.exp(lb - mx)
                    inv = 1.0 / (ea + eb)
                    ob = o_rows[...] * (ea * inv) + ob * (eb * inv)
                    lb = mx + jnp.log(ea + eb)
                o_rows[...] = ob
                l_rows[...] = lb

        for d in DILATIONS:
            _for_blocks(d, group, ATTN_UNROLL)

        def fin(t, carry):
            rows = pl.ds(pl.multiple_of(t * tn, tn), tn)
            g = g_ref[rows, :]
            ya_ref[rows, :] = (o_ref[rows, :] * (g * _sigmoid(g))).astype(BF16)
            return carry

        lax.fori_loop(0, SEQ // tn, fin, 0)

    col = lambda c0: pl.BlockSpec((SEQ, LANES), lambda p: (0, c0 // LANES + p))
    vec = pl.BlockSpec((1, LANES), lambda p: (0, 0))
    out = pl.BlockSpec((SEQ, LANES), lambda p: (0, p))
    return _call(
        body, name="attn_fwd", grid=(ATTN_W // LANES,),
        in_specs=[col(C_AQ), col(C_AK), col(C_AV), col(C_AG), vec, vec],
        out_specs=[out, out, out],
        out_shape=[jax.ShapeDtypeStruct((SEQ, ATTN_W), F32), jax.ShapeDtypeStruct((SEQ, ATTN_W), F32),
                   jax.ShapeDtypeStruct((SEQ, ATTN_W), BF16)],
        scratch_shapes=[pltpu.VMEM((SEQ, LANES), F32), pltpu.VMEM((SEQ, LANES), F32)],
        compiler_params=_params(),
    )(proj, proj, proj, proj, gq2, gk2)


def _attn_bwd(proj, o, lse, dyc, gq2, gk2):
    tn = 512
    npairs = ATTN_W // LANES

    def body(proj_hbm, o_hbm, l_hbm, dyc_hbm, gq_ref, gk_ref,
             dq_ref, dk_ref, dv_ref, dgt_ref, gqg_ref, gkg_ref,
             qb_, kb_, vb_, gb_, ob_, lb_, yb_, dkb_, dvb_):
        pair = pl.program_id(0)
        bd = _head_blockdiag()
        lo = _lo_mask(CHUNK)
        lo2 = lax.broadcasted_iota(jnp.int32, (2 * CHUNK, LANES), 1) < HEAD_DIM
        valid1, valid2 = _band_masks()
        gqs = gq_ref[...] * QK_SCALE
        gk = gk_ref[...]

        def pcol(c0):
            return proj_hbm.at[:, pl.ds(pl.multiple_of(c0 + pair * LANES, LANES), LANES)]

        def acol(hbm, c0=0):
            return hbm.at[:, pl.ds(pl.multiple_of(c0 + pair * LANES, LANES), LANES)]

        pltpu.sync_copy(pcol(C_AQ), qb_)
        pltpu.sync_copy(pcol(C_AK), kb_)
        pltpu.sync_copy(pcol(C_AV), vb_)
        pltpu.sync_copy(pcol(C_AG), gb_)
        pltpu.sync_copy(acol(o_hbm), ob_)
        pltpu.sync_copy(acol(l_hbm), lb_)
        pltpu.sync_copy(acol(dyc_hbm, GMLP_W), yb_)

        @pl.when(pair == 0)
        def _():
            gqg_ref[...] = jnp.zeros_like(gqg_ref)
            gkg_ref[...] = jnp.zeros_like(gkg_ref)

        def pre(t, carry):
            rows = pl.ds(pl.multiple_of(t * tn, tn), tn)
            g = gb_[rows, :]
            ov = ob_[rows, :]
            dya = yb_[rows, :]
            sg = _sigmoid(g)
            dgt_ref[rows, :] = (dya * ov * (sg * (1.0 + g * (1.0 - sg)))).astype(BF16)
            do = dya * (g * sg)
            yb_[rows, :] = do
            ob_[rows, :] = _headsum(do * ov, bd)
            zero = jnp.zeros((tn, LANES), F32)
            gb_[rows, :] = zero
            dkb_[rows, :] = zero
            dvb_[rows, :] = zero
            q = qb_[rows, :]
            qb_[rows, :] = q * lax.rsqrt(_headsum(q * q, bd) * (1.0 / HEAD_DIM) + EPS) * gqs
            k = kb_[rows, :]
            kb_[rows, :] = k * lax.rsqrt(_headsum(k * k, bd) * (1.0 / HEAD_DIM) + EPS) * gk
            return carry

        lax.fori_loop(0, SEQ // tn, pre, 0)

        def load_kv(ref, d, start, prev):
            own = _rows_of(ref, start, d)[...]
            if prev is None:
                return own.astype(BF16)
            return jnp.concatenate([_rows_of(ref, prev, d)[...], own], axis=0).astype(BF16)

        def group(d, blocks):
            first = blocks[0][1] is None
            valid, lok = (valid1, lo) if first else (valid2, lo2)
            chains = [(b, h) for b in range(len(blocks)) for h in range(2)]
            mask = lambda h: lo if h == 0 else ~lo
            qs = [_rows_of(qb_, start, d)[...] for start, _ in blocks]
            dos = [_rows_of(yb_, start, d)[...] for start, _ in blocks]
            lvs = [_rows_of(lb_, start, d)[...] for start, _ in blocks]
            dls = [_rows_of(ob_, start, d)[...] for start, _ in blocks]
            ks = [load_kv(kb_, d, start, prev) for start, prev in blocks]
            vs = [load_kv(vb_, d, start, prev) for start, prev in blocks]
            qbs = [q.astype(BF16) for q in qs]
            dobs = [do.astype(BF16) for do in dos]
            ss = [_dot_nt(jnp.where(mask(h), qs[b], 0.0).astype(BF16), ks[b]) for b, h in chains]
            dps = [_dot_nt(jnp.where(mask(h), dos[b], 0.0).astype(BF16), vs[b]) for b, h in chains]
            pbs, dss = [], []
            for s, dp, (b, h) in zip(ss, dps, chains):
                hc = h * HEAD_DIM
                p = jnp.exp(jnp.where(valid, s, -jnp.inf) - lvs[b][:, hc:hc + 1])
                pbs.append(p.astype(BF16))
                dss.append((p * (dp - dls[b][:, hc:hc + 1])).astype(BF16))
            dqs = [_dot(ds, ks[b]) for ds, (b, h) in zip(dss, chains)]
            dks = [_dot_tn(ds, qbs[b]) for ds, (b, h) in zip(dss, chains)]
            dvs = [_dot_tn(p, dobs[b]) for p, (b, h) in zip(pbs, chains)]
            for b, (start, prev) in enumerate(blocks):
                c0, c1 = 2 * b, 2 * b + 1
                dq_rows = _rows_of(gb_, start, d)
                dq_rows[...] = dq_rows[...] + jnp.where(lo, dqs[c0], dqs[c1])
                dkc = jnp.where(lok, dks[c0], dks[c1])
                dvc = jnp.where(lok, dvs[c0], dvs[c1])
                spans = ((start, slice(0, CHUNK)),) if first else ((prev, slice(0, CHUNK)), (start, slice(CHUNK, 2 * CHUNK)))
                for st, sl in spans:
                    dk_rows = _rows_of(dkb_, st, d)
                    dk_rows[...] = dk_rows[...] + dkc[sl]
                    dv_rows = _rows_of(dvb_, st, d)
                    dv_rows[...] = dv_rows[...] + dvc[sl]

        for d in DILATIONS:
            _for_blocks(d, group, ATTN_UNROLL)

        pltpu.sync_copy(pcol(C_AQ), qb_)
        pltpu.sync_copy(pcol(C_AK), kb_)

        def post(t, carry):
            gq_acc, gk_acc = carry
            rows = pl.ds(pl.multiple_of(t * tn, tn), tn)
            outs = []
            for raw_, acc_, gain in ((qb_, gb_, gqs), (kb_, dkb_, gk)):
                a = raw_[rows, :]
                r = lax.rsqrt(_headsum(a * a, bd) * (1.0 / HEAD_DIM) + EPS)
                z = a * r
                dn = acc_[rows, :]
                dz = dn * gain
                outs.append((r * (dz - z * (_headsum(dz * z, bd) * (1.0 / HEAD_DIM))), jnp.sum(dn * z, axis=0, keepdims=True)))
            dq_ref[rows, :] = outs[0][0].astype(BF16)
            dk_ref[rows, :] = outs[1][0].astype(BF16)
            dv_ref[rows, :] = dvb_[rows, :].astype(BF16)
            return gq_acc + outs[0][1] * QK_SCALE, gk_acc + outs[1][1]

        zero = jnp.zeros((1, LANES), F32)
        gq_acc, gk_acc = lax.fori_loop(0, SEQ // tn, post, (zero, zero))
        gqg_ref[...] += gq_acc
        gkg_ref[...] += gk_acc

        @pl.when(pair == npairs - 1)
        def _():
            gqg_ref[...] = _fold_heads(gqg_ref[...])
            gkg_ref[...] = _fold_heads(gkg_ref[...])

    hbm = pl.BlockSpec(memory_space=pl.ANY)
    vec = pl.BlockSpec((1, LANES), lambda p: (0, 0))
    out = pl.BlockSpec((SEQ, LANES), lambda p: (0, p))
    big = jax.ShapeDtypeStruct((SEQ, ATTN_W), BF16)
    return _call(
        body, name="attn_bwd", grid=(npairs,),
        in_specs=[hbm, hbm, hbm, hbm, vec, vec],
        out_specs=[out, out, out, out, vec, vec],
        out_shape=[big, big, big, big, jax.ShapeDtypeStruct((1, LANES), F32), jax.ShapeDtypeStruct((1, LANES), F32)],
        scratch_shapes=[pltpu.VMEM((SEQ, LANES), F32) for _ in range(9)],
        compiler_params=_params(),
    )(proj, o, lse, dyc, gq2, gk2)


def _mem_kv(mem, gain, wkv):
    def body(m_ref, g_ref, w_ref, kv_ref, hm_ref):
        mv = m_ref[...]
        ms = jnp.mean(mv * mv, axis=-1, keepdims=True)
        hm = (mv * lax.rsqrt(ms + EPS) * g_ref[...]).astype(BF16)
        hm_ref[...] = hm
        kv_ref[...] = _dot(hm, w_ref[...])

    return _call(
        body, name="mem_kv",
        out_shape=[jax.ShapeDtypeStruct((MEM_LEN, 2 * MEM_W), F32), jax.ShapeDtypeStruct((MEM_LEN, D_MODEL), BF16)],
        compiler_params=_params(),
    )(mem, gain, wkv)


def _mem_keys(kv_ref, kg_ref, bd, p):
    mk = kv_ref[:, p * LANES:(p + 1) * LANES]
    r = lax.rsqrt(_headsum(mk * mk, bd) * (1.0 / HEAD_DIM) + EPS)
    z = mk * r
    mkn = (z * kg_ref[:, p * LANES:(p + 1) * LANES]).astype(BF16)
    mvp = kv_ref[:, MEM_W + p * LANES:MEM_W + (p + 1) * LANES].astype(BF16)
    return mkn, mvp, r, z


def _mem_fwd(proj, kv, qg4, kg4):
    tm = 512

    def body(q_ref, g_ref, kv_ref, qg_ref, kg_ref, om_ref, ym_ref):
        bd = _head_blockdiag()
        lo = _lo_mask(tm)
        for p in range(2):
            cs = slice(p * LANES, (p + 1) * LANES)
            mkn, mvp, _, _ = _mem_keys(kv_ref, kg_ref, bd, p)
            q = q_ref[:, cs]
            qn = q * lax.rsqrt(_headsum(q * q, bd) * (1.0 / HEAD_DIM) + EPS) * (qg_ref[:, cs] * QK_SCALE)
            res = []
            for h in range(2):
                qh = jnp.where(lo if h == 0 else ~lo, qn, 0.0).astype(BF16)
                s = _dot_nt(qh, mkn)
                e = jnp.exp(s - jnp.max(s, axis=-1, keepdims=True))
                res.append(_dot(e.astype(BF16), mvp) * (1.0 / jnp.sum(e, axis=-1, keepdims=True)))
            ov = jnp.where(lo, res[0], res[1])
            g = g_ref[:, cs]
            om_ref[:, cs] = ov
            ym_ref[:, cs] = (ov * (g * _sigmoid(g))).astype(BF16)

    vec = pl.BlockSpec((1, MEM_W), lambda i: (0, 0))
    return _call(
        body, name="mem_fwd", grid=(SEQ // tm,),
        in_specs=[pl.BlockSpec((tm, MEM_W), lambda i: (i, C_MQ // MEM_W)),
                  pl.BlockSpec((tm, MEM_W), lambda i: (i, C_MG // MEM_W)),
                  pl.BlockSpec((MEM_LEN, 2 * MEM_W), lambda i: (0, 0)), vec, vec],
        out_specs=[pl.BlockSpec((tm, MEM_W), lambda i: (i, 0)), pl.BlockSpec((tm, MEM_W), lambda i: (i, 0))],
        out_shape=[jax.ShapeDtypeStruct((SEQ, MEM_W), F32), jax.ShapeDtypeStruct((SEQ, MEM_W), BF16)],
        compiler_params=_params(),
    )(proj, proj, kv, qg4, kg4)


def _mem_bwd(proj, om, dyc, kv, hm, mem, mgain, wkv, qg4, kg4):
    tm = 512
    nsteps = SEQ // tm

    def body(q_ref, g_ref, om_ref, dy_ref, kv_ref, hm_ref, mem_ref, mg_ref, w_ref, qg_ref, kg_ref,
             dq_ref, dgt_ref, gqg_ref, gkg_ref, gw_ref, gmg_ref, dmk_ref, dmv_ref, gq_acc):
        i = pl.program_id(0)
        bd = _head_blockdiag()
        lo = _lo_mask(tm)
        lom = _lo_mask(MEM_LEN)

        @pl.when(i == 0)
        def _():
            dmk_ref[...] = jnp.zeros_like(dmk_ref)
            dmv_ref[...] = jnp.zeros_like(dmv_ref)
            gq_acc[...] = jnp.zeros_like(gq_acc)

        for p in range(2):
            cs = slice(p * LANES, (p + 1) * LANES)
            mkn, mvp, _, _ = _mem_keys(kv_ref, kg_ref, bd, p)
            gqs = qg_ref[:, cs] * QK_SCALE
            q = q_ref[:, cs]
            r = lax.rsqrt(_headsum(q * q, bd) * (1.0 / HEAD_DIM) + EPS)
            z = q * r
            qn = z * gqs
            qnb = qn.astype(BF16)
            g = g_ref[:, cs]
            ov = om_ref[:, cs]
            dym = dy_ref[:, cs]
            sg = _sigmoid(g)
            dgt_ref[:, cs] = (dym * ov * (sg * (1.0 + g * (1.0 - sg)))).astype(BF16)
            do = dym * (g * sg)
            dob = do.astype(BF16)
            delta = _headsum(do * ov, bd)
            parts = []
            for h in range(2):
                mh = lo if h == 0 else ~lo
                hc = h * HEAD_DIM
                qh = jnp.where(mh, qn, 0.0).astype(BF16)
                doh = jnp.where(mh, do, 0.0).astype(BF16)
                s = _dot_nt(qh, mkn)
                e = jnp.exp(s - jnp.max(s, axis=-1, keepdims=True))
                pr = e * (1.0 / jnp.sum(e, axis=-1, keepdims=True))
                dp = _dot_nt(doh, mvp)
                ds = (pr * (dp - delta[:, hc:hc + 1])).astype(BF16)
                parts.append((_dot(ds, mkn), _dot_tn(ds, qnb), _dot_tn(pr.astype(BF16), dob)))
            dqn = jnp.where(lo, parts[0][0], parts[1][0])
            dmk_ref[:, cs] += jnp.where(lom, parts[0][1], parts[1][1])
            dmv_ref[:, cs] += jnp.where(lom, parts[0][2], parts[1][2])
            dz = dqn * gqs
            dq_ref[:, cs] = (r * (dz - z * (_headsum(dz * z, bd) * (1.0 / HEAD_DIM)))).astype(BF16)
            gq_acc[:, cs] += jnp.sum(dqn * z, axis=0, keepdims=True) * QK_SCALE

        @pl.when(i == nsteps - 1)
        def _():
            gqg_ref[...] = _fold_heads(gq_acc[:, 0:LANES] + gq_acc[:, LANES:2 * LANES])
            dkv = []
            gk = jnp.zeros((1, LANES), F32)
            for p in range(2):
                cs = slice(p * LANES, (p + 1) * LANES)
                _, _, r, z = _mem_keys(kv_ref, kg_ref, bd, p)
                dn = dmk_ref[:, cs]
                dz = dn * kg_ref[:, cs]
                gk = gk + jnp.sum(dn * z, axis=0, keepdims=True)
                dkv.append(r * (dz - z * (_headsum(dz * z, bd) * (1.0 / HEAD_DIM))))
            gkg_ref[...] = _fold_heads(gk)
            dkvb = jnp.concatenate(dkv + [dmv_ref[...]], axis=1).astype(BF16)
            gw_ref[...] = _dot_tn(hm_ref[...], dkvb)
            dhm = _dot_nt(dkvb, w_ref[...])
            mv = mem_ref[...]
            zm = mv * lax.rsqrt(jnp.mean(mv * mv, axis=-1, keepdims=True) + EPS)
            gmg_ref[...] = jnp.sum(dhm * zm, axis=0, keepdims=True)

    const = lambda shape: pl.BlockSpec(shape, lambda i: (0,) * len(shape))
    row = lambda j: pl.BlockSpec((tm, MEM_W), lambda i: (i, j))
    return _call(
        body, name="mem_bwd", grid=(nsteps,),
        in_specs=[row(C_MQ // MEM_W), row(C_MG // MEM_W), row(0), row((GMLP_W + ATTN_W) // MEM_W),
                  const((MEM_LEN, 2 * MEM_W)), const((MEM_LEN, D_MODEL)), const((MEM_LEN, D_MODEL)),
                  const((1, D_MODEL)), const((D_MODEL, 2 * MEM_W)), const((1, MEM_W)), const((1, MEM_W))],
        out_specs=[row(0), row(0), const((1, LANES)), const((1, LANES)),
                   const((D_MODEL, 2 * MEM_W)), const((1, D_MODEL))],
        out_shape=[jax.ShapeDtypeStruct((SEQ, MEM_W), BF16), jax.ShapeDtypeStruct((SEQ, MEM_W), BF16),
                   jax.ShapeDtypeStruct((1, LANES), F32), jax.ShapeDtypeStruct((1, LANES), F32),
                   jax.ShapeDtypeStruct((D_MODEL, 2 * MEM_W), F32), jax.ShapeDtypeStruct((1, D_MODEL), F32)],
        scratch_shapes=[pltpu.VMEM((MEM_LEN, MEM_W), F32), pltpu.VMEM((MEM_LEN, MEM_W), F32),
                        pltpu.VMEM((1, MEM_W), F32)],
        compiler_params=_params(),
    )(proj, proj, om, dyc, kv, hm, mem, mgain, wkv, qg4, kg4)


def _out_loss(yg, ya, ym, x, tgt, wo):
    tm = 256
    nsteps = SEQ // tm
    parts = ((0, GMLP_W), (GMLP_W, ATTN_W), (GMLP_W + ATTN_W, MEM_W))

    def body(yg_ref, ya_ref, ym_ref, x_ref, t_ref, w_ref, dy_ref, dyc_ref, gw_ref, ls_ref):
        i = pl.program_id(0)

        @pl.when(i == 0)
        def _():
            gw_ref[...] = jnp.zeros_like(gw_ref)
            ls_ref[...] = jnp.zeros_like(ls_ref)

        ys = (yg_ref[...], ya_ref[...], ym_ref[...])
        y = sum(_dot(yv, w_ref[r0:r0 + n, :]) for yv, (r0, n) in zip(ys, parts))
        err = x_ref[...] + y - t_ref[...]
        ls_ref[...] += jnp.sum(err * err, axis=0, keepdims=True)
        dy = err * (1.0 / D_MODEL)
        dy_ref[...] = dy
        dyb = dy.astype(BF16)
        dyc_ref[...] = _dot_nt(dyb, w_ref[...])
        for yv, (r0, n) in zip(ys, parts):
            gw_ref[r0:r0 + n, :] += _dot_tn(yv, dyb)

    row = lambda w: pl.BlockSpec((tm, w), lambda i: (i, 0))
    const = lambda shape: pl.BlockSpec(shape, lambda i: (0, 0))
    return _call(
        body, name="out_loss", grid=(nsteps,),
        in_specs=[row(GMLP_W), row(ATTN_W), row(MEM_W), row(D_MODEL), row(D_MODEL), const((D_MODEL, D_MODEL))],
        out_specs=[row(D_MODEL), row(D_MODEL), const((D_MODEL, D_MODEL)), const((1, D_MODEL))],
        out_shape=[jax.ShapeDtypeStruct((SEQ, D_MODEL), F32), jax.ShapeDtypeStruct((SEQ, D_MODEL), F32),
                   jax.ShapeDtypeStruct((D_MODEL, D_MODEL), F32), jax.ShapeDtypeStruct((1, D_MODEL), F32)],
        compiler_params=_params(),
    )(yg, ya, ym, x, tgt, wo)


def _proj_bwd(x, dy, gain, wt, dg, daq, dak, dav, dag, dmq, dmg):
    tm = 256
    nsteps = SEQ // tm
    pieces = ((C_GU, 3 * GMLP_W), (C_AQ, ATTN_W), (C_AK, ATTN_W), (C_AV, ATTN_W), (C_AG, ATTN_W),
              (C_MQ, MEM_W), (C_MG, MEM_W))

    def body(x_ref, dy_ref, g_ref, wt_hbm, p0, p1, p2, p3, p4, p5, p6, gx_ref, gwt_hbm, gg_ref, wt_v, acc):
        i = pl.program_id(0)

        @pl.when(i == 0)
        def _():
            pltpu.sync_copy(wt_hbm, wt_v)
            acc[...] = jnp.zeros_like(acc)
            gg_ref[...] = jnp.zeros_like(gg_ref)

        xv = x_ref[...]
        r = lax.rsqrt(jnp.mean(xv * xv, axis=-1, keepdims=True) + EPS)
        z = xv * r
        g = g_ref[...]
        h = (z * g).astype(BF16)
        dh = jnp.zeros((tm, D_MODEL), F32)
        for pref, (c0, w) in zip((p0, p1, p2, p3, p4, p5, p6), pieces):
            dp = pref[...]
            dh = dh + _dot(dp, wt_v[c0:c0 + w, :])
            acc[c0:c0 + w, :] += _dot_tn(dp, h)
        gg_ref[...] += jnp.sum(dh * z, axis=0, keepdims=True)
        dz = dh * g
        gx_ref[...] = dy_ref[...] + r * (dz - z * jnp.mean(dz * z, axis=-1, keepdims=True))

        @pl.when(i == nsteps - 1)
        def _():
            pltpu.sync_copy(acc, gwt_hbm)

    row = lambda w: pl.BlockSpec((tm, w), lambda i: (i, 0))
    hbm = pl.BlockSpec(memory_space=pl.ANY)
    vec = pl.BlockSpec((1, D_MODEL), lambda i: (0, 0))
    return _call(
        body, name="proj_bwd", grid=(nsteps,),
        in_specs=[row(D_MODEL), row(D_MODEL), vec, hbm] + [row(w) for _, w in pieces],
        out_specs=[row(D_MODEL), hbm, vec],
        out_shape=[jax.ShapeDtypeStruct((SEQ, D_MODEL), F32), jax.ShapeDtypeStruct((IN_W, D_MODEL), F32),
                   jax.ShapeDtypeStruct((1, D_MODEL), F32)],
        scratch_shapes=[pltpu.VMEM((IN_W, D_MODEL), BF16), pltpu.VMEM((IN_W, D_MODEL), F32)],
        compiler_params=_params(),
    )(x, dy, gain, wt, dg, daq, dak, dav, dag, dmq, dmg)


def _gather_weights(wt_sh, wkv_sh, wo_sh):
    shards = (wt_sh, wkv_sh, wo_sh)
    nrows = tuple(a.shape[0] for a in shards)

    def body(a0, a1, a2, o0, o1, o2, send_sems, recv_sems):
        x, y, c = lax.axis_index("x"), lax.axis_index("y"), lax.axis_index("c")
        me = 2 * x + y
        sib = (x, y, 1 - c)
        chips = ((1 - x, y), (x, 1 - y), (1 - x, 1 - y))
        ins, outs = (a0, a1, a2), (o0, o1, o2)

        def half(a, chip, hf):
            n = nrows[a] // 2
            return outs[a].at[pl.ds(pl.multiple_of(chip * nrows[a] + hf * n, 16), n), :]

        def copy(k, ref, to):
            return pltpu.make_async_remote_copy(src_ref=ref, dst_ref=ref, send_sem=send_sems.at[k],
                                                recv_sem=recv_sems.at[k], device_id=to, device_id_type=MESH)

        for a in range(3):
            outs[a][pl.ds(pl.multiple_of(me * nrows[a], 16), nrows[a]), :] = ins[a][...].astype(BF16)
        started = []
        for a in range(3):
            for j, (px, py) in enumerate(chips):
                cp = copy(a * 6 + j * 2, half(a, me, c), (px, py, c))
                cp.start()
                started.append(cp)
        for a in range(3):
            for j, (px, py) in enumerate(chips):
                landed = half(a, 2 * px + py, c)
                copy(a * 6 + j * 2, landed, (px, py, c)).wait_recv()
                fw = copy(a * 6 + j * 2 + 1, landed, sib)
                fw.start()
                started.append(fw)
        for a in range(3):
            for j, (px, py) in enumerate(chips):
                copy(a * 6 + j * 2 + 1, half(a, 2 * px + py, 1 - c), sib).wait_recv()
        for cp in started:
            cp.wait_send()

    return _call(
        body, name="gather_weights",
        out_shape=[jax.ShapeDtypeStruct((4 * a.shape[0], a.shape[1]), BF16) for a in shards],
        in_specs=[pl.BlockSpec(memory_space=pltpu.VMEM)] * 3,
        out_specs=[pl.BlockSpec(memory_space=pltpu.VMEM)] * 3,
        scratch_shapes=[pltpu.SemaphoreType.DMA((18,)), pltpu.SemaphoreType.DMA((18,))],
        compiler_params=_params(),
    )(*shards)


def _reduce_grads(gwt, gwkv, gwo, small):
    bigs = (gwt, gwkv, gwo)
    piece = tuple(a.shape[0] // 8 for a in bigs)
    width = tuple(a.shape[1] for a in bigs)
    views = tuple(a.reshape(4, 2, r, w) for a, r, w in zip(bigs, piece, width))

    def body(g0, g1, g2, sm, o0, o1, o2, osm,
             l0, l1, l2, ra0, ra1, ra2, rb0, rb1, rb2, rc0, rc1, rc2, sa, sb, sc, acc_s,
             send_sems, recv_sems, local_sems):
        x, y, c = lax.axis_index("x"), lax.axis_index("y"), lax.axis_index("c")
        sib, xn, yn = (x, y, 1 - c), (1 - x, y, c), (x, 1 - y, c)
        gs, outs = (g0, g1, g2), (o0, o1, o2)
        loc, ra, rb, rc = (l0, l1, l2), (ra0, ra1, ra2), (rb0, rb1, rb2), (rc0, rc1, rc2)

        def copy(k, src, dst, to):
            return pltpu.make_async_remote_copy(src_ref=src, dst_ref=dst, send_sem=send_sems.at[k],
                                                recv_sem=recv_sems.at[k], device_id=to, device_id_type=MESH)

        started = []

        def go(cp):
            cp.start()
            started.append(cp)

        mine = []
        for a in range(3):
            go(copy(a, gs[a].at[:, 1 - c], ra[a], sib))
            cp = pltpu.make_async_copy(gs[a].at[:, c], loc[a], local_sems.at[a])
            cp.start()
            mine.append(cp)
        go(copy(3, sm, sa, sib))
        for a in range(3):
            mine[a].wait()
            copy(a, gs[a].at[:, 1 - c], ra[a], sib).wait_recv()
            ra[a][...] = loc[a][...] + ra[a][...]
            go(copy(4 + a, ra[a].at[pl.ds(2 * (1 - x), 2)], rb[a], xn))
        copy(3, sm, sa, sib).wait_recv()
        acc_s[...] = sm[...] + sa[...]
        go(copy(7, acc_s, sb, xn))
        for a in range(3):
            copy(4 + a, ra[a].at[pl.ds(2 * (1 - x), 2)], rb[a], xn).wait_recv()
            rb[a][...] = ra[a][pl.ds(2 * x, 2)] + rb[a][...]
            go(copy(8 + a, rb[a].at[1 - y], rc[a], yn))
        copy(7, acc_s, sb, xn).wait_recv()
        sb[...] = acc_s[...] + sb[...]
        go(copy(11, sb, sc, yn))
        for a in range(3):
            copy(8 + a, rb[a].at[1 - y], rc[a], yn).wait_recv()
            mine_rows = outs[a].at[pl.ds(pl.multiple_of(c * piece[a], 8), piece[a]), :]
            mine_rows[...] = rb[a][y] + rc[a][...]
            go(copy(12 + a, mine_rows, mine_rows, sib))
        copy(11, sb, sc, yn).wait_recv()
        osm[...] = sb[...] + sc[...]
        for a in range(3):
            theirs = outs[a].at[pl.ds(pl.multiple_of((1 - c) * piece[a], 8), piece[a]), :]
            copy(12 + a, theirs, theirs, sib).wait_recv()
        for cp in started:
            cp.wait_send()

    vm = pl.BlockSpec(memory_space=pltpu.VMEM)
    hbm = pl.BlockSpec(memory_space=pl.ANY)
    scratch = []
    for n, shape_of in ((4, None), (4, None), (2, None), (0, None)):
        for r, w in zip(piece, width):
            scratch.append(pltpu.VMEM((n, r, w) if n else (r, w), F32))
    scratch += [pltpu.VMEM(small.shape, F32) for _ in range(4)]
    scratch += [pltpu.SemaphoreType.DMA((15,)), pltpu.SemaphoreType.DMA((15,)), pltpu.SemaphoreType.DMA((3,))]
    return _call(
        body, name="reduce_grads",
        out_shape=[jax.ShapeDtypeStruct((2 * r, w), F32) for r, w in zip(piece, width)]
        + [jax.ShapeDtypeStruct(small.shape, F32)],
        in_specs=[hbm, hbm, hbm, vm],
        out_specs=[vm, vm, vm, vm],
        scratch_shapes=scratch,
        compiler_params=_params(),
    )(*views, small)


def _adamw(w, g, m, v):
    rows, cols = w.shape
    tm = 256 if rows % 256 == 0 and rows > 256 else rows

    def body(w_ref, g_ref, m_ref, v_ref, d_ref, nm_ref, nv_ref):
        gv = g_ref[...]
        nm = ADAM_B1 * m_ref[...] + (1.0 - ADAM_B1) * gv
        nv = ADAM_B2 * v_ref[...] + (1.0 - ADAM_B2) * (gv * gv)
        m_hat = nm / (1.0 - ADAM_B1 ** ADAM_STEP)
        v_hat = nv / (1.0 - ADAM_B2 ** ADAM_STEP)
        d_ref[...] = -ADAM_LR * (m_hat / (jnp.sqrt(v_hat) + ADAM_EPS) + ADAM_WD * w_ref[...])
        nm_ref[...] = nm
        nv_ref[...] = nv

    blk = pl.BlockSpec((tm, cols), lambda i: (i, 0))
    return _call(
        body, name="adamw", grid=(rows // tm,),
        in_specs=[blk] * 4, out_specs=[blk] * 3,
        out_shape=[jax.ShapeDtypeStruct((rows, cols), F32)] * 3,
        compiler_params=_params(),
    )(w, g, m, v)


def _row128(a):
    return jnp.pad(a.reshape(1, HEAD_DIM), ((0, 0), (0, LANES - HEAD_DIM)))


def _pack_small(ws, ng, mng, vg, b, aq, ak, mq, mk):
    rows = [ws.reshape(512, LANES), ng.reshape(8, LANES), mng.reshape(8, LANES), vg.reshape(2, LANES),
            b.reshape(4, LANES), _row128(aq), _row128(ak), _row128(mq), _row128(mk),
            jnp.zeros((SM_ROWS - SM_MKG - 1, LANES), F32)]
    return jnp.concatenate(rows, axis=0)


def _unpack_small(p):
    gain = lambda r: p[r:r + 1, :HEAD_DIM]
    return (p[SM_NG:SM_NG + 8].reshape(1, D_MODEL), p[SM_VG:SM_VG + 2].reshape(1, 4, HEAD_DIM),
            p[SM_WS:SM_WS + 512].reshape(1, 4, CHUNK, CHUNK), p[SM_B:SM_B + 4].reshape(1, 4, CHUNK),
            gain(SM_AQ), gain(SM_AK), p[SM_MNG:SM_MNG + 8].reshape(1, D_MODEL), gain(SM_MQG), gain(SM_MKG))


def _local_grads(x, mem, tgt, norm_gain, wt, gmlp_v_gain, gmlp_w_s, gmlp_b, attn_q_gain, attn_k_gain,
                 mem_norm_gain, wkv, mem_q_gain, mem_k_gain, wo):
    vg = gmlp_v_gain.reshape(1, GMLP_W)
    bias_full = jnp.repeat(gmlp_b.T, HEAD_DIM, axis=1)
    gq2, gk2 = jnp.tile(attn_q_gain, (1, 2)), jnp.tile(attn_k_gain, (1, 2))
    qg4, kg4 = jnp.tile(mem_q_gain, (1, 4)), jnp.tile(mem_k_gain, (1, 4))

    proj = _fwd_proj(x, norm_gain, wt)
    yg = _gmlp_fwd(proj, vg, gmlp_w_s, bias_full)
    o, lse, ya = _attn_fwd(proj, gq2, gk2)
    kv, hm = _mem_kv(mem, mem_norm_gain, wkv)
    om, ym = _mem_fwd(proj, kv, qg4, kg4)
    dy, dyc, g_wo, err2 = _out_loss(yg, ya, ym, x, tgt, wo)
    dg, g_ws, g_bfull, g_vg = _gmlp_bwd(proj, dyc, vg, gmlp_w_s, bias_full)
    daq, dak, dav, dag, g_aq, g_ak = _attn_bwd(proj, o, lse, dyc, gq2, gk2)
    dmq, dmg, g_mq, g_mk, g_wkv, g_mng = _mem_bwd(proj, om, dyc, kv, hm, mem, mem_norm_gain, wkv, qg4, kg4)
    gx, g_wt, g_ng = _proj_bwd(x, dy, norm_gain, wt, dg, daq, dak, dav, dag, dmq, dmg)

    loss = (0.5 / D_MODEL) * jnp.sum(err2)
    g_b = g_bfull[:, ::HEAD_DIM].T
    small = _pack_small(g_ws, g_ng, g_mng, g_vg, g_b, g_aq[:, :HEAD_DIM], g_ak[:, :HEAD_DIM],
                        g_mq[:, :HEAD_DIM], g_mk[:, :HEAD_DIM])
    return loss, gx, g_wt, g_wkv, g_wo, small


def kernel(x, mem, norm_gain, w_in, gmlp_v_gain, gmlp_w_s, gmlp_b, attn_q_gain, attn_k_gain, mem_norm_gain, w_mem_kv, mem_q_gain, mem_k_gain, w_out, loss_target, m_norm_gain, m_w_in, m_gmlp_v_gain, m_gmlp_w_s, m_gmlp_b, m_attn_q_gain, m_attn_k_gain, m_mem_norm_gain, m_w_mem_kv, m_mem_q_gain, m_mem_k_gain, m_w_out, v_norm_gain, v_w_in, v_gmlp_v_gain, v_gmlp_w_s, v_gmlp_b, v_attn_q_gain, v_attn_k_gain, v_mem_norm_gain, v_w_mem_kv, v_mem_q_gain, v_mem_k_gain, v_w_out):
    wt, wkv, wo = _gather_weights(w_in[0].T, w_mem_kv[0], w_out[0])
    loss, gx, g_wt, g_wkv, g_wo, small = _local_grads(
        x[0], mem[0], loss_target[0], norm_gain, wt, gmlp_v_gain[0], gmlp_w_s[0], gmlp_b[0],
        attn_q_gain, attn_k_gain, mem_norm_gain, wkv, mem_q_gain, mem_k_gain, wo)
    g_wt_sh, g_wkv_sh, g_wo_sh, g_small = _reduce_grads(g_wt, g_wkv, g_wo, small)
    loss = lax.psum(loss, ("x", "y", "c"))

    pack = lambda p: _pack_small(p[3][0], p[0], p[7], p[2][0], p[4][0], p[5], p[6], p[9], p[10])
    ws = (norm_gain, w_in, gmlp_v_gain, gmlp_w_s, gmlp_b, attn_q_gain, attn_k_gain, mem_norm_gain, w_mem_kv,
          mem_q_gain, mem_k_gain, w_out)
    ms = (m_norm_gain, m_w_in, m_gmlp_v_gain, m_gmlp_w_s, m_gmlp_b, m_attn_q_gain, m_attn_k_gain, m_mem_norm_gain,
          m_w_mem_kv, m_mem_q_gain, m_mem_k_gain, m_w_out)
    vs = (v_norm_gain, v_w_in, v_gmlp_v_gain, v_gmlp_w_s, v_gmlp_b, v_attn_q_gain, v_attn_k_gain, v_mem_norm_gain,
          v_w_mem_kv, v_mem_q_gain, v_mem_k_gain, v_w_out)
    g_in = g_wt_sh.T
    big = {1: g_in, 8: g_wkv_sh, 11: g_wo_sh}
    upd = {i: _adamw(ws[i][0], big[i], ms[i][0], vs[i][0]) for i in big}
    upd_small = _adamw(pack(ws), g_small, pack(ms), pack(vs))
    g_sm = _unpack_small(g_small)
    d_sm, m_sm, v_sm = (_unpack_small(u) for u in upd_small)
    small_pos = (0, 2, 3, 4, 5, 6, 7, 9, 10)

    def leaves(big_of, small_of):
        out = [None] * 12
        for i in big:
            out[i] = big_of(i)[None]
        for j, i in enumerate(small_pos):
            out[i] = small_of[j]
        return out

    grads = leaves(lambda i: big[i], g_sm)
    deltas = leaves(lambda i: upd[i][0], d_sm)
    new_m = leaves(lambda i: upd[i][1], m_sm)
    new_v = leaves(lambda i: upd[i][2], v_sm)
    return (loss, gx[None], *grads, *deltas, *new_m, *new_v)
```

```python
import functools
import math

import jax
import jax.numpy as jnp
from jax import lax
from jax.experimental import pallas as pl
from jax.experimental.pallas import tpu as pltpu

F32 = jnp.float32
BF16 = jnp.bfloat16

SEQ = 4096
D_MODEL = 1024
HEAD_DIM = 64
LANES = 128
CHUNK = 128
GMLP_W, ATTN_W, MEM_W = 256, 512, 256
IN_W = 3 * GMLP_W + 4 * ATTN_W + 2 * MEM_W
MEM_LEN = 256
DILATIONS = (1, 4, 16)
EPS = 1e-6
QK_SCALE = 1.0 / math.sqrt(HEAD_DIM)
C_GU, C_GV, C_GG, C_AQ, C_AK, C_AV, C_AG, C_MQ, C_MG = 0, 256, 512, 768, 1280, 1792, 2304, 2816, 3072

ADAM_LR, ADAM_B1, ADAM_B2, ADAM_EPS, ADAM_WD, ADAM_STEP = 0.001, 0.9, 0.999, 1e-08, 0.01, 10

VMEM_LIMIT = 48 * 1024 * 1024
ATTN_UNROLL = 4
MESH = pl.DeviceIdType.MESH

SM_WS, SM_NG, SM_MNG, SM_VG, SM_B, SM_AQ, SM_AK, SM_MQG, SM_MKG, SM_LOSS, SM_ROWS = 0, 512, 520, 528, 536, 544, 552, 560, 568, 576, 640


def _call(body, **kw):
    return pl.pallas_call(body, **kw)


def _params(**kw):
    return pltpu.CompilerParams(vmem_limit_bytes=VMEM_LIMIT, **kw)


def _dot(a, b):
    return jnp.dot(a, b, preferred_element_type=F32)


def _dot_nt(a, b):
    return lax.dot_general(a, b, (((1,), (1,)), ((), ())), preferred_element_type=F32)


def _dot_tn(a, b):
    return lax.dot_general(a, b, (((0,), (0,)), ((), ())), preferred_element_type=F32)


def _head_blockdiag():
    r = lax.shift_right_logical(lax.broadcasted_iota(jnp.int32, (LANES, LANES), 0), 6)
    c = lax.shift_right_logical(lax.broadcasted_iota(jnp.int32, (LANES, LANES), 1), 6)
    return jnp.where(r == c, 1.0, 0.0).astype(BF16)


def _headsum(v, bd):
    hi = v.astype(BF16)
    lo = (v - hi.astype(F32)).astype(BF16)
    return _dot(hi, bd) + _dot(lo, bd)


def _lo_mask(rows):
    return lax.broadcasted_iota(jnp.int32, (rows, LANES), 1) < HEAD_DIM


def _sigmoid(x):
    return 1.0 / (1.0 + jnp.exp(-x))


def _fold_heads(v):
    return v + pltpu.roll(v, HEAD_DIM, 1)


def _fwd_proj(x, gain, wt):
    tm = 256

    def body(x_ref, g_ref, wt_ref, o_ref):
        xv = x_ref[...]
        ms = jnp.mean(xv * xv, axis=-1, keepdims=True)
        h = (xv * lax.rsqrt(ms + EPS) * g_ref[...]).astype(BF16)
        o_ref[...] = _dot_nt(h, wt_ref[...])

    return _call(
        body, name="fwd_proj", grid=(SEQ // tm,),
        in_specs=[pl.BlockSpec((tm, D_MODEL), lambda i: (i, 0)),
                  pl.BlockSpec((1, D_MODEL), lambda i: (0, 0)),
                  pl.BlockSpec((IN_W, D_MODEL), lambda i: (0, 0))],
        out_specs=pl.BlockSpec((tm, IN_W), lambda i: (i, 0)),
        out_shape=jax.ShapeDtypeStruct((SEQ, IN_W), F32),
        compiler_params=_params(),
    )(x, gain, wt)


def _gmlp_weights(w_ref):
    ti = lax.broadcasted_iota(jnp.int32, (CHUNK, CHUNK), 0)
    si = lax.broadcasted_iota(jnp.int32, (CHUNK, CHUNK), 1)
    tril = si <= ti
    return tril, [jnp.where(tril, w_ref[h], 0.0).astype(BF16) for h in range(4)]


def _gmlp_fwd(proj, vgain, w_s, bias_full):
    tm = 512

    def body(p_ref, vg_ref, w_ref, b_ref, y_ref):
        bd = _head_blockdiag()
        lo = _lo_mask(CHUNK)
        _, wm = _gmlp_weights(w_ref)
        for c in range(tm // CHUNK):
            rows = pl.ds(c * CHUNK, CHUNK)
            for p in range(2):
                cs = slice(p * LANES, (p + 1) * LANES)
                u = p_ref[rows, C_GU + p * LANES:C_GU + (p + 1) * LANES]
                v = p_ref[rows, C_GV + p * LANES:C_GV + (p + 1) * LANES]
                gt = p_ref[rows, C_GG + p * LANES:C_GG + (p + 1) * LANES]
                r = lax.rsqrt(_headsum(v * v, bd) * (1.0 / HEAD_DIM) + EPS)
                vn = (v * r * vg_ref[:, cs]).astype(BF16)
                sp = jnp.where(lo, _dot(wm[2 * p], vn), _dot(wm[2 * p + 1], vn)) + b_ref[:, cs]
                y_ref[rows, cs] = (u * sp * (gt * _sigmoid(gt))).astype(BF16)

    return _call(
        body, name="gmlp_fwd", grid=(SEQ // tm,),
        in_specs=[pl.BlockSpec((tm, 3 * GMLP_W), lambda i: (i, 0)),
                  pl.BlockSpec((1, GMLP_W), lambda i: (0, 0)),
                  pl.BlockSpec((4, CHUNK, CHUNK), lambda i: (0, 0, 0)),
                  pl.BlockSpec((CHUNK, GMLP_W), lambda i: (0, 0))],
        out_specs=pl.BlockSpec((tm, GMLP_W), lambda i: (i, 0)),
        out_shape=jax.ShapeDtypeStruct((SEQ, GMLP_W), BF16),
        compiler_params=_params(),
    )(proj, vgain, w_s, bias_full)


def _gmlp_bwd(proj, dyc, vgain, w_s, bias_full):
    tm = 512
    nsteps = SEQ // tm

    def body(p_ref, dy_ref, vg_ref, w_ref, b_ref, dg_ref, gw_ref, gb_ref, gv_ref):
        i = pl.program_id(0)
        bd = _head_blockdiag()
        lo = _lo_mask(CHUNK)
        tril, wm = _gmlp_weights(w_ref)

        @pl.when(i == 0)
        def _():
            gw_ref[...] = jnp.zeros_like(gw_ref)
            gb_ref[...] = jnp.zeros_like(gb_ref)
            gv_ref[...] = jnp.zeros_like(gv_ref)

        for c in range(tm // CHUNK):
            rows = pl.ds(c * CHUNK, CHUNK)
            for p in range(2):
                cs = slice(p * LANES, (p + 1) * LANES)
                u = p_ref[rows, C_GU + p * LANES:C_GU + (p + 1) * LANES]
                v = p_ref[rows, C_GV + p * LANES:C_GV + (p + 1) * LANES]
                gt = p_ref[rows, C_GG + p * LANES:C_GG + (p + 1) * LANES]
                dy = dy_ref[rows, cs]
                g = vg_ref[:, cs]
                r = lax.rsqrt(_headsum(v * v, bd) * (1.0 / HEAD_DIM) + EPS)
                z = v * r
                vn = (z * g).astype(BF16)
                sp = jnp.where(lo, _dot(wm[2 * p], vn), _dot(wm[2 * p + 1], vn)) + b_ref[:, cs]
                sg = _sigmoid(gt)
                sl = gt * sg
                dsl = sg * (1.0 + gt * (1.0 - sg))
                du = dy * sp * sl
                dsp = dy * u * sl
                dgt = dy * u * sp * dsl
                dspb = dsp.astype(BF16)
                dvn = jnp.where(lo, _dot_tn(wm[2 * p], dspb), _dot_tn(wm[2 * p + 1], dspb))
                gw_ref[2 * p] += _dot_nt(jnp.where(lo, dsp, 0.0).astype(BF16), vn)
                gw_ref[2 * p + 1] += _dot_nt(jnp.where(lo, 0.0, dsp).astype(BF16), vn)
                gb_ref[:, cs] += _headsum(dsp, bd)
                gv_ref[:, cs] += jnp.sum(dvn * z, axis=0, keepdims=True)
                dz = dvn * g
                dv = r * (dz - z * (_headsum(dz * z, bd) * (1.0 / HEAD_DIM)))
                dg_ref[rows, C_GU + p * LANES:C_GU + (p + 1) * LANES] = du.astype(BF16)
                dg_ref[rows, C_GV + p * LANES:C_GV + (p + 1) * LANES] = dv.astype(BF16)
                dg_ref[rows, C_GG + p * LANES:C_GG + (p + 1) * LANES] = dgt.astype(BF16)

        @pl.when(i == nsteps - 1)
        def _():
            for h in range(4):
                gw_ref[h] = jnp.where(tril, gw_ref[h], 0.0)

    return _call(
        body, name="gmlp_bwd", grid=(nsteps,),
        in_specs=[pl.BlockSpec((tm, 3 * GMLP_W), lambda i: (i, 0)),
                  pl.BlockSpec((tm, GMLP_W), lambda i: (i, 0)),
                  pl.BlockSpec((1, GMLP_W), lambda i: (0, 0)),
                  pl.BlockSpec((4, CHUNK, CHUNK), lambda i: (0, 0, 0)),
                  pl.BlockSpec((CHUNK, GMLP_W), lambda i: (0, 0))],
        out_specs=[pl.BlockSpec((tm, 3 * GMLP_W), lambda i: (i, 0)),
                   pl.BlockSpec((4, CHUNK, CHUNK), lambda i: (0, 0, 0)),
                   pl.BlockSpec((CHUNK, GMLP_W), lambda i: (0, 0)),
                   pl.BlockSpec((1, GMLP_W), lambda i: (0, 0))],
        out_shape=[jax.ShapeDtypeStruct((SEQ, 3 * GMLP_W), BF16),
                   jax.ShapeDtypeStruct((4, CHUNK, CHUNK), F32),
                   jax.ShapeDtypeStruct((CHUNK, GMLP_W), F32),
                   jax.ShapeDtypeStruct((1, GMLP_W), F32)],
        compiler_params=_params(),
    )(proj, dyc, vgain, w_s, bias_full)


def _band_masks():
    qi = lax.broadcasted_iota(jnp.int32, (CHUNK, 2 * CHUNK), 0)
    kj = lax.broadcasted_iota(jnp.int32, (CHUNK, 2 * CHUNK), 1)
    valid2 = ((kj < CHUNK) & (kj >= qi)) | ((kj >= CHUNK) & (kj - CHUNK <= qi))
    q1 = lax.broadcasted_iota(jnp.int32, (CHUNK, CHUNK), 0)
    k1 = lax.broadcasted_iota(jnp.int32, (CHUNK, CHUNK), 1)
    return k1 <= q1, valid2


def _rows_of(ref, start, d):
    if d == 1:
        return ref.at[pl.ds(start if isinstance(start, int) else pl.multiple_of(start, CHUNK), CHUNK), :]
    return ref.at[pl.ds(start, CHUNK, stride=d), :]


def _unrolled(lo, hi, unroll, run):
    groups = (hi - lo) // unroll
    if groups:
        def body(g, carry):
            run([lo + g * unroll + t for t in range(unroll)])
            return carry

        lax.fori_loop(0, groups, body, 0)
    if lo + groups * unroll < hi:
        run(range(lo + groups * unroll, hi))


def _for_blocks(d, group_fn, unroll):
    nblk = SEQ // CHUNK
    sh = d.bit_length() - 1

    def first(j):
        return (j * CHUNK if d == 1 else j, None)

    def rest(j):
        start = (j & (d - 1)) + (j >> sh) * (CHUNK * d)
        return (start, start - CHUNK * d)

    _unrolled(0, d, unroll, lambda js: group_fn(d, [first(j) for j in js]))
    _unrolled(d, nblk, unroll, lambda js: group_fn(d, [rest(j) for j in js]))


def _attn_fwd(proj, gq2, gk2):
    tn = 512

    def body(q_ref, k_ref, v_ref, g_ref, gq_ref, gk_ref, o_ref, l_ref, ya_ref, qn_ref, kn_ref):
        bd = _head_blockdiag()
        lo = _lo_mask(CHUNK)
        valid1, valid2 = _band_masks()

        def norm(t, carry):
            rows = pl.ds(pl.multiple_of(t * tn, tn), tn)
            q = q_ref[rows, :]
            qn_ref[rows, :] = q * lax.rsqrt(_headsum(q * q, bd) * (1.0 / HEAD_DIM) + EPS) * (gq_ref[...] * QK_SCALE)
            k = k_ref[rows, :]
            kn_ref[rows, :] = k * lax.rsqrt(_headsum(k * k, bd) * (1.0 / HEAD_DIM) + EPS) * gk_ref[...]
            return carry

        lax.fori_loop(0, SEQ // tn, norm, 0)

        def load_kv(ref, d, start, prev):
            own = _rows_of(ref, start, d)[...]
            if prev is None:
                return own.astype(BF16)
            return jnp.concatenate([_rows_of(ref, prev, d)[...], own], axis=0).astype(BF16)

        def group(d, blocks):
            valid = valid1 if blocks[0][1] is None else valid2
            chains = [(b, h) for b in range(len(blocks)) for h in range(2)]
            qs = [_rows_of(qn_ref, start, d)[...] for start, _ in blocks]
            ks = [load_kv(kn_ref, d, start, prev) for start, prev in blocks]
            vs = [load_kv(v_ref, d, start, prev) for start, prev in blocks]
            ss = [_dot_nt(jnp.where(lo if h == 0 else ~lo, qs[b], 0.0).astype(BF16), ks[b]) for b, h in chains]
            ms, ps, ls = [], [], []
            for s in ss:
                s = jnp.where(valid, s, -jnp.inf)
                m = jnp.max(s, axis=-1, keepdims=True)
                p = jnp.exp(s - m)
                ms.append(m)
                ls.append(jnp.sum(p, axis=-1, keepdims=True))
                ps.append(p.astype(BF16))
            os_ = [_dot(p, vs[b]) for p, (b, h) in zip(ps, chains)]
            for b, (start, _) in enumerate(blocks):
                c0, c1 = 2 * b, 2 * b + 1
                ob = jnp.where(lo, os_[c0] * (1.0 / ls[c0]), os_[c1] * (1.0 / ls[c1]))
                lb = jnp.where(lo, ms[c0] + jnp.log(ls[c0]), ms[c1] + jnp.log(ls[c1]))
                o_rows = _rows_of(o_ref, start, d)
                l_rows = _rows_of(l_ref, start, d)
                if d != DILATIONS[0]:
                    lold = l_rows[...]
                    mx = jnp.maximum(lold, lb)
                    ea = jnp.exp(lold - mx)
                    eb = jnp.exp(lb - mx)
                    inv = 1.0 / (ea + eb)
                    ob = o_rows[...] * (ea * inv) + ob * (eb * inv)
                    lb = mx + jnp.log(ea + eb)
                o_rows[...] = ob
                l_rows[...] = lb

        for d in DILATIONS:
            _for_blocks(d, group, ATTN_UNROLL)

        def fin(t, carry):
            rows = pl.ds(pl.multiple_of(t * tn, tn), tn)
            g = g_ref[rows, :]
            ya_ref[rows, :] = (o_ref[rows, :] * (g * _sigmoid(g))).astype(BF16)
            return carry

        lax.fori_loop(0, SEQ // tn, fin, 0)

    col = lambda c0: pl.BlockSpec((SEQ, LANES), lambda p: (0, c0 // LANES + p))
    vec = pl.BlockSpec((1, LANES), lambda p: (0, 0))
    out = pl.BlockSpec((SEQ, LANES), lambda p: (0, p))
    return _call(
        body, name="attn_fwd", grid=(ATTN_W // LANES,),
        in_specs=[col(C_AQ), col(C_AK), col(C_AV), col(C_AG), vec, vec],
        out_specs=[out, out, out],
        out_shape=[jax.ShapeDtypeStruct((SEQ, ATTN_W), F32), jax.ShapeDtypeStruct((SEQ, ATTN_W), F32),
                   jax.ShapeDtypeStruct((SEQ, ATTN_W), BF16)],
        scratch_shapes=[pltpu.VMEM((SEQ, LANES), F32), pltpu.VMEM((SEQ, LANES), F32)],
        compiler_params=_params(),
    )(proj, proj, proj, proj, gq2, gk2)


def _attn_bwd(proj, o, lse, dyc, gq2, gk2):
    tn = 512
    npairs = ATTN_W // LANES

    def body(proj_hbm, o_hbm, l_hbm, dyc_hbm, gq_ref, gk_ref,
             dq_ref, dk_ref, dv_ref, dgt_ref, gqg_ref, gkg_ref,
             qb_, kb_, vb_, gb_, ob_, lb_, yb_, dkb_, dvb_):
        pair = pl.program_id(0)
        bd = _head_blockdiag()
        lo = _lo_mask(CHUNK)
        lo2 = lax.broadcasted_iota(jnp.int32, (2 * CHUNK, LANES), 1) < HEAD_DIM
        valid1, valid2 = _band_masks()
        gqs = gq_ref[...] * QK_SCALE
        gk = gk_ref[...]

        def pcol(c0):
            return proj_hbm.at[:, pl.ds(pl.multiple_of(c0 + pair * LANES, LANES), LANES)]

        def acol(hbm, c0=0):
            return hbm.at[:, pl.ds(pl.multiple_of(c0 + pair * LANES, LANES), LANES)]

        pltpu.sync_copy(pcol(C_AQ), qb_)
        pltpu.sync_copy(pcol(C_AK), kb_)
        pltpu.sync_copy(pcol(C_AV), vb_)
        pltpu.sync_copy(pcol(C_AG), gb_)
        pltpu.sync_copy(acol(o_hbm), ob_)
        pltpu.sync_copy(acol(l_hbm), lb_)
        pltpu.sync_copy(acol(dyc_hbm, GMLP_W), yb_)

        @pl.when(pair == 0)
        def _():
            gqg_ref[...] = jnp.zeros_like(gqg_ref)
            gkg_ref[...] = jnp.zeros_like(gkg_ref)

        def pre(t, carry):
            rows = pl.ds(pl.multiple_of(t * tn, tn), tn)
            g = gb_[rows, :]
            ov = ob_[rows, :]
            dya = yb_[rows, :]
            sg = _sigmoid(g)
            dgt_ref[rows, :] = (dya * ov * (sg * (1.0 + g * (1.0 - sg)))).astype(BF16)
            do = dya * (g * sg)
            yb_[rows, :] = do
            ob_[rows, :] = _headsum(do * ov, bd)
            zero = jnp.zeros((tn, LANES), F32)
            gb_[rows, :] = zero
            dkb_[rows, :] = zero
            dvb_[rows, :] = zero
            q = qb_[rows, :]
            qb_[rows, :] = q * lax.rsqrt(_headsum(q * q, bd) * (1.0 / HEAD_DIM) + EPS) * gqs
            k = kb_[rows, :]
            kb_[rows, :] = k * lax.rsqrt(_headsum(k * k, bd) * (1.0 / HEAD_DIM) + EPS) * gk
            return carry

        lax.fori_loop(0, SEQ // tn, pre, 0)

        def load_kv(ref, d, start, prev):
            own = _rows_of(ref, start, d)[...]
            if prev is None:
                return own.astype(BF16)
            return jnp.concatenate([_rows_of(ref, prev, d)[...], own], axis=0).astype(BF16)

        def group(d, blocks):
            first = blocks[0][1] is None
            valid, lok = (valid1, lo) if first else (valid2, lo2)
            chains = [(b, h) for b in range(len(blocks)) for h in range(2)]
            mask = lambda h: lo if h == 0 else ~lo
            qs = [_rows_of(qb_, start, d)[...] for start, _ in blocks]
            dos = [_rows_of(yb_, start, d)[...] for start, _ in blocks]
            lvs = [_rows_of(lb_, start, d)[...] for start, _ in blocks]
            dls = [_rows_of(ob_, start, d)[...] for start, _ in blocks]
            ks = [load_kv(kb_, d, start, prev) for start, prev in blocks]
            vs = [load_kv(vb_, d, start, prev) for start, prev in blocks]
            qbs = [q.astype(BF16) for q in qs]
            dobs = [do.astype(BF16) for do in dos]
            ss = [_dot_nt(jnp.where(mask(h), qs[b], 0.0).astype(BF16), ks[b]) for b, h in chains]
            dps = [_dot_nt(jnp.where(mask(h), dos[b], 0.0).astype(BF16), vs[b]) for b, h in chains]
            pbs, dss = [], []
            for s, dp, (b, h) in zip(ss, dps, chains):
                hc = h * HEAD_DIM
                p = jnp.exp(jnp.where(valid, s, -jnp.inf) - lvs[b][:, hc:hc + 1])
                pbs.append(p.astype(BF16))
                dss.append((p * (dp - dls[b][:, hc:hc + 1])).astype(BF16))
            dqs = [_dot(ds, ks[b]) for ds, (b, h) in zip(dss, chains)]
            dks = [_dot_tn(ds, qbs[b]) for ds, (b, h) in zip(dss, chains)]
            dvs = [_dot_tn(p, dobs[b]) for p, (b, h) in zip(pbs, chains)]
            for b, (start, prev) in enumerate(blocks):
                c0, c1 = 2 * b, 2 * b + 1
                dq_rows = _rows_of(gb_, start, d)
                dq_rows[...] = dq_rows[...] + jnp.where(lo, dqs[c0], dqs[c1])
                dkc = jnp.where(lok, dks[c0], dks[c1])
                dvc = jnp.where(lok, dvs[c0], dvs[c1])
                spans = ((start, slice(0, CHUNK)),) if first else ((prev, slice(0, CHUNK)), (start, slice(CHUNK, 2 * CHUNK)))
                for st, sl in spans:
                    dk_rows = _rows_of(dkb_, st, d)
                    dk_rows[...] = dk_rows[...] + dkc[sl]
                    dv_rows = _rows_of(dvb_, st, d)
                    dv_rows[...] = dv_rows[...] + dvc[sl]

        for d in DILATIONS:
            _for_blocks(d, group, ATTN_UNROLL)

        pltpu.sync_copy(pcol(C_AQ), qb_)
        pltpu.sync_copy(pcol(C_AK), kb_)

        def post(t, carry):
            gq_acc, gk_acc = carry
            rows = pl.ds(pl.multiple_of(t * tn, tn), tn)
            outs = []
            for raw_, acc_, gain in ((qb_, gb_, gqs), (kb_, dkb_, gk)):
                a = raw_[rows, :]
                r = lax.rsqrt(_headsum(a * a, bd) * (1.0 / HEAD_DIM) + EPS)
                z = a * r
                dn = acc_[rows, :]
                dz = dn * gain
                outs.append((r * (dz - z * (_headsum(dz * z, bd) * (1.0 / HEAD_DIM))), jnp.sum(dn * z, axis=0, keepdims=True)))
            dq_ref[rows, :] = outs[0][0].astype(BF16)
            dk_ref[rows, :] = outs[1][0].astype(BF16)
            dv_ref[rows, :] = dvb_[rows, :].astype(BF16)
            return gq_acc + outs[0][1] * QK_SCALE, gk_acc + outs[1][1]

        zero = jnp.zeros((1, LANES), F32)
        gq_acc, gk_acc = lax.fori_loop(0, SEQ // tn, post, (zero, zero))
        gqg_ref[...] += gq_acc
        gkg_ref[...] += gk_acc

        @pl.when(pair == npairs - 1)
        def _():
            gqg_ref[...] = _fold_heads(gqg_ref[...])
            gkg_ref[...] = _fold_heads(gkg_ref[...])

    hbm = pl.BlockSpec(memory_space=pl.ANY)
    vec = pl.BlockSpec((1, LANES), lambda p: (0, 0))
    out = pl.BlockSpec((SEQ, LANES), lambda p: (0, p))
    big = jax.ShapeDtypeStruct((SEQ, ATTN_W), BF16)
    return _call(
        body, name="attn_bwd", grid=(npairs,),
        in_specs=[hbm, hbm, hbm, hbm, vec, vec],
        out_specs=[out, out, out, out, vec, vec],
        out_shape=[big, big, big, big, jax.ShapeDtypeStruct((1, LANES), F32), jax.ShapeDtypeStruct((1, LANES), F32)],
        scratch_shapes=[pltpu.VMEM((SEQ, LANES), F32) for _ in range(9)],
        compiler_params=_params(),
    )(proj, o, lse, dyc, gq2, gk2)


def _mem_kv(mem, gain, wkv):
    def body(m_ref, g_ref, w_ref, kv_ref, hm_ref):
        mv = m_ref[...]
        ms = jnp.mean(mv * mv, axis=-1, keepdims=True)
        hm = (mv * lax.rsqrt(ms + EPS) * g_ref[...]).astype(BF16)
        hm_ref[...] = hm
        kv_ref[...] = _dot(hm, w_ref[...])

    return _call(
        body, name="mem_kv",
        out_shape=[jax.ShapeDtypeStruct((MEM_LEN, 2 * MEM_W), F32), jax.ShapeDtypeStruct((MEM_LEN, D_MODEL), BF16)],
        compiler_params=_params(),
    )(mem, gain, wkv)


def _mem_keys(kv_ref, kg_ref, bd, p):
    mk = kv_ref[:, p * LANES:(p + 1) * LANES]
    r = lax.rsqrt(_headsum(mk * mk, bd) * (1.0 / HEAD_DIM) + EPS)
    z = mk * r
    mkn = (z * kg_ref[:, p * LANES:(p + 1) * LANES]).astype(BF16)
    mvp = kv_ref[:, MEM_W + p * LANES:MEM_W + (p + 1) * LANES].astype(BF16)
    return mkn, mvp, r, z


def _mem_fwd(proj, kv, qg4, kg4):
    tm = 512

    def body(q_ref, g_ref, kv_ref, qg_ref, kg_ref, om_ref, ym_ref):
        bd = _head_blockdiag()
        lo = _lo_mask(tm)
        for p in range(2):
            cs = slice(p * LANES, (p + 1) * LANES)
            mkn, mvp, _, _ = _mem_keys(kv_ref, kg_ref, bd, p)
            q = q_ref[:, cs]
            qn = q * lax.rsqrt(_headsum(q * q, bd) * (1.0 / HEAD_DIM) + EPS) * (qg_ref[:, cs] * QK_SCALE)
            res = []
            for h in range(2):
                qh = jnp.where(lo if h == 0 else ~lo, qn, 0.0).astype(BF16)
                s = _dot_nt(qh, mkn)
                e = jnp.exp(s - jnp.max(s, axis=-1, keepdims=True))
                res.append(_dot(e.astype(BF16), mvp) * (1.0 / jnp.sum(e, axis=-1, keepdims=True)))
            ov = jnp.where(lo, res[0], res[1])
            g = g_ref[:, cs]
            om_ref[:, cs] = ov
            ym_ref[:, cs] = (ov * (g * _sigmoid(g))).astype(BF16)

    vec = pl.BlockSpec((1, MEM_W), lambda i: (0, 0))
    return _call(
        body, name="mem_fwd", grid=(SEQ // tm,),
        in_specs=[pl.BlockSpec((tm, MEM_W), lambda i: (i, C_MQ // MEM_W)),
                  pl.BlockSpec((tm, MEM_W), lambda i: (i, C_MG // MEM_W)),
                  pl.BlockSpec((MEM_LEN, 2 * MEM_W), lambda i: (0, 0)), vec, vec],
        out_specs=[pl.BlockSpec((tm, MEM_W), lambda i: (i, 0)), pl.BlockSpec((tm, MEM_W), lambda i: (i, 0))],
        out_shape=[jax.ShapeDtypeStruct((SEQ, MEM_W), F32), jax.ShapeDtypeStruct((SEQ, MEM_W), BF16)],
        compiler_params=_params(),
    )(proj, proj, kv, qg4, kg4)


def _mem_bwd(proj, om, dyc, kv, hm, mem, mgain, wkv, qg4, kg4):
    tm = 512
    nsteps = SEQ // tm

    def body(q_ref, g_ref, om_ref, dy_ref, kv_ref, hm_ref, mem_ref, mg_ref, w_ref, qg_ref, kg_ref,
             dq_ref, dgt_ref, gqg_ref, gkg_ref, gw_ref, gmg_ref, dmk_ref, dmv_ref, gq_acc):
        i = pl.program_id(0)
        bd = _head_blockdiag()
        lo = _lo_mask(tm)
        lom = _lo_mask(MEM_LEN)

        @pl.when(i == 0)
        def _():
            dmk_ref[...] = jnp.zeros_like(dmk_ref)
            dmv_ref[...] = jnp.zeros_like(dmv_ref)
            gq_acc[...] = jnp.zeros_like(gq_acc)

        for p in range(2):
            cs = slice(p * LANES, (p + 1) * LANES)
            mkn, mvp, _, _ = _mem_keys(kv_ref, kg_ref, bd, p)
            gqs = qg_ref[:, cs] * QK_SCALE
            q = q_ref[:, cs]
            r = lax.rsqrt(_headsum(q * q, bd) * (1.0 / HEAD_DIM) + EPS)
            z = q * r
            qn = z * gqs
            qnb = qn.astype(BF16)
            g = g_ref[:, cs]
            ov = om_ref[:, cs]
            dym = dy_ref[:, cs]
            sg = _sigmoid(g)
            dgt_ref[:, cs] = (dym * ov * (sg * (1.0 + g * (1.0 - sg)))).astype(BF16)
            do = dym * (g * sg)
            dob = do.astype(BF16)
            delta = _headsum(do * ov, bd)
            parts = []
            for h in range(2):
                mh = lo if h == 0 else ~lo
                hc = h * HEAD_DIM
                qh = jnp.where(mh, qn, 0.0).astype(BF16)
                doh = jnp.where(mh, do, 0.0).astype(BF16)
                s = _dot_nt(qh, mkn)
                e = jnp.exp(s - jnp.max(s, axis=-1, keepdims=True))
                pr = e * (1.0 / jnp.sum(e, axis=-1, keepdims=True))
                dp = _dot_nt(doh, mvp)
                ds = (pr * (dp - delta[:, hc:hc + 1])).astype(BF16)
                parts.append((_dot(ds, mkn), _dot_tn(ds, qnb), _dot_tn(pr.astype(BF16), dob)))
            dqn = jnp.where(lo, parts[0][0], parts[1][0])
            dmk_ref[:, cs] += jnp.where(lom, parts[0][1], parts[1][1])
            dmv_ref[:, cs] += jnp.where(lom, parts[0][2], parts[1][2])
            dz = dqn * gqs
            dq_ref[:, cs] = (r * (dz - z * (_headsum(dz * z, bd) * (1.0 / HEAD_DIM)))).astype(BF16)
            gq_acc[:, cs] += jnp.sum(dqn * z, axis=0, keepdims=True) * QK_SCALE

        @pl.when(i == nsteps - 1)
        def _():
            gqg_ref[...] = _fold_heads(gq_acc[:, 0:LANES] + gq_acc[:, LANES:2 * LANES])
            dkv = []
            gk = jnp.zeros((1, LANES), F32)
            for p in range(2):
                cs = slice(p * LANES, (p + 1) * LANES)
                _, _, r, z = _mem_keys(kv_ref, kg_ref, bd, p)
                dn = dmk_ref[:, cs]
                dz = dn * kg_ref[:, cs]
                gk = gk + jnp.sum(dn * z, axis=0, keepdims=True)
                dkv.append(r * (dz - z * (_headsum(dz * z, bd) * (1.0 / HEAD_DIM))))
            gkg_ref[...] = _fold_heads(gk)
            dkvb = jnp.concatenate(dkv + [dmv_ref[...]], axis=1).astype(BF16)
            gw_ref[...] = _dot_tn(hm_ref[...], dkvb)
            dhm = _dot_nt(dkvb, w_ref[...])
            mv = mem_ref[...]
            zm = mv * lax.rsqrt(jnp.mean(mv * mv, axis=-1, keepdims=True) + EPS)
            gmg_ref[...] = jnp.sum(dhm * zm, axis=0, keepdims=True)

    const = lambda shape: pl.BlockSpec(shape, lambda i: (0,) * len(shape))
    row = lambda j: pl.BlockSpec((tm, MEM_W), lambda i: (i, j))
    return _call(
        body, name="mem_bwd", grid=(nsteps,),
        in_specs=[row(C_MQ // MEM_W), row(C_MG // MEM_W), row(0), row((GMLP_W + ATTN_W) // MEM_W),
                  const((MEM_LEN, 2 * MEM_W)), const((MEM_LEN, D_MODEL)), const((MEM_LEN, D_MODEL)),
                  const((1, D_MODEL)), const((D_MODEL, 2 * MEM_W)), const((1, MEM_W)), const((1, MEM_W))],
        out_specs=[row(0), row(0), const((1, LANES)), const((1, LANES)),
                   const((D_MODEL, 2 * MEM_W)), const((1, D_MODEL))],
        out_shape=[jax.ShapeDtypeStruct((SEQ, MEM_W), BF16), jax.ShapeDtypeStruct((SEQ, MEM_W), BF16),
                   jax.ShapeDtypeStruct((1, LANES), F32), jax.ShapeDtypeStruct((1, LANES), F32),
                   jax.ShapeDtypeStruct((D_MODEL, 2 * MEM_W), F32), jax.ShapeDtypeStruct((1, D_MODEL), F32)],
        scratch_shapes=[pltpu.VMEM((MEM_LEN, MEM_W), F32), pltpu.VMEM((MEM_LEN, MEM_W), F32),
                        pltpu.VMEM((1, MEM_W), F32)],
        compiler_params=_params(),
    )(proj, proj, om, dyc, kv, hm, mem, mgain, wkv, qg4, kg4)


def _out_loss(yg, ya, ym, x, tgt, wo):
    tm = 256
    nsteps = SEQ // tm
    parts = ((0, GMLP_W), (GMLP_W, ATTN_W), (GMLP_W + ATTN_W, MEM_W))

    def body(yg_ref, ya_ref, ym_ref, x_ref, t_ref, w_ref, dy_ref, dyc_ref, gw_ref, ls_ref):
        i = pl.program_id(0)

        @pl.when(i == 0)
        def _():
            gw_ref[...] = jnp.zeros_like(gw_ref)
            ls_ref[...] = jnp.zeros_like(ls_ref)

        ys = (yg_ref[...], ya_ref[...], ym_ref[...])
        y = sum(_dot(yv, w_ref[r0:r0 + n, :]) for yv, (r0, n) in zip(ys, parts))
        err = x_ref[...] + y - t_ref[...]
        ls_ref[...] += jnp.sum(err * err, axis=0, keepdims=True)
        dy = err * (1.0 / D_MODEL)
        dy_ref[...] = dy
        dyb = dy.astype(BF16)
        dyc_ref[...] = _dot_nt(dyb, w_ref[...])
        for yv, (r0, n) in zip(ys, parts):
            gw_ref[r0:r0 + n, :] += _dot_tn(yv, dyb)

    row = lambda w: pl.BlockSpec((tm, w), lambda i: (i, 0))
    const = lambda shape: pl.BlockSpec(shape, lambda i: (0, 0))
    return _call(
        body, name="out_loss", grid=(nsteps,),
        in_specs=[row(GMLP_W), row(ATTN_W), row(MEM_W), row(D_MODEL), row(D_MODEL), const((D_MODEL, D_MODEL))],
        out_specs=[row(D_MODEL), row(D_MODEL), const((D_MODEL, D_MODEL)), const((1, D_MODEL))],
        out_shape=[jax.ShapeDtypeStruct((SEQ, D_MODEL), F32), jax.ShapeDtypeStruct((SEQ, D_MODEL), F32),
                   jax.ShapeDtypeStruct((D_MODEL, D_MODEL), F32), jax.ShapeDtypeStruct((1, D_MODEL), F32)],
        compiler_params=_params(),
    )(yg, ya, ym, x, tgt, wo)


def _proj_bwd(x, dy, gain, wt, dg, daq, dak, dav, dag, dmq, dmg):
    tm = 256
    nsteps = SEQ // tm
    pieces = ((C_GU, 3 * GMLP_W), (C_AQ, ATTN_W), (C_AK, ATTN_W), (C_AV, ATTN_W), (C_AG, ATTN_W),
              (C_MQ, MEM_W), (C_MG, MEM_W))

    def body(x_ref, dy_ref, g_ref, wt_hbm, p0, p1, p2, p3, p4, p5, p6, gx_ref, gwt_hbm, gg_ref, wt_v, acc):
        i = pl.program_id(0)

        @pl.when(i == 0)
        def _():
            pltpu.sync_copy(wt_hbm, wt_v)
            acc[...] = jnp.zeros_like(acc)
            gg_ref[...] = jnp.zeros_like(gg_ref)

        xv = x_ref[...]
        r = lax.rsqrt(jnp.mean(xv * xv, axis=-1, keepdims=True) + EPS)
        z = xv * r
        g = g_ref[...]
        h = (z * g).astype(BF16)
        dh = jnp.zeros((tm, D_MODEL), F32)
        for pref, (c0, w) in zip((p0, p1, p2, p3, p4, p5, p6), pieces):
            dp = pref[...]
            dh = dh + _dot(dp, wt_v[c0:c0 + w, :])
            acc[c0:c0 + w, :] += _dot_tn(dp, h)
        gg_ref[...] += jnp.sum(dh * z, axis=0, keepdims=True)
        dz = dh * g
        gx_ref[...] = dy_ref[...] + r * (dz - z * jnp.mean(dz * z, axis=-1, keepdims=True))

        @pl.when(i == nsteps - 1)
        def _():
            pltpu.sync_copy(acc, gwt_hbm)

    row = lambda w: pl.BlockSpec((tm, w), lambda i: (i, 0))
    hbm = pl.BlockSpec(memory_space=pl.ANY)
    vec = pl.BlockSpec((1, D_MODEL), lambda i: (0, 0))
    return _call(
        body, name="proj_bwd", grid=(nsteps,),
        in_specs=[row(D_MODEL), row(D_MODEL), vec, hbm] + [row(w) for _, w in pieces],
        out_specs=[row(D_MODEL), hbm, vec],
        out_shape=[jax.ShapeDtypeStruct((SEQ, D_MODEL), F32), jax.ShapeDtypeStruct((IN_W, D_MODEL), F32),
                   jax.ShapeDtypeStruct((1, D_MODEL), F32)],
        scratch_shapes=[pltpu.VMEM((IN_W, D_MODEL), BF16), pltpu.VMEM((IN_W, D_MODEL), F32)],
        compiler_params=_params(),
    )(x, dy, gain, wt, dg, daq, dak, dav, dag, dmq, dmg)


def _gather_weights(wt_sh, wkv_sh, wo_sh):
    shards = (wt_sh, wkv_sh, wo_sh)
    nrows = tuple(a.shape[0] for a in shards)

    def body(a0, a1, a2, o0, o1, o2, send_sems, recv_sems):
        x, y, c = lax.axis_index("x"), lax.axis_index("y"), lax.axis_index("c")
        me = 2 * x + y
        sib = (x, y, 1 - c)
        chips = ((1 - x, y), (x, 1 - y), (1 - x, 1 - y))
        ins, outs = (a0, a1, a2), (o0, o1, o2)

        def half(a, chip, hf):
            n = nrows[a] // 2
            return outs[a].at[pl.ds(pl.multiple_of(chip * nrows[a] + hf * n, 16), n), :]

        def copy(k, ref, to):
            return pltpu.make_async_remote_copy(src_ref=ref, dst_ref=ref, send_sem=send_sems.at[k],
                                                recv_sem=recv_sems.at[k], device_id=to, device_id_type=MESH)

        for a in range(3):
            outs[a][pl.ds(pl.multiple_of(me * nrows[a], 16), nrows[a]), :] = ins[a][...].astype(BF16)
        started = []
        for a in range(3):
            for j, (px, py) in enumerate(chips):
                cp = copy(a * 6 + j * 2, half(a, me, c), (px, py, c))
                cp.start()
                started.append(cp)
        for a in range(3):
            for j, (px, py) in enumerate(chips):
                landed = half(a, 2 * px + py, c)
                copy(a * 6 + j * 2, landed, (px, py, c)).wait_recv()
                fw = copy(a * 6 + j * 2 + 1, landed, sib)
                fw.start()
                started.append(fw)
        for a in range(3):
            for j, (px, py) in enumerate(chips):
                copy(a * 6 + j * 2 + 1, half(a, 2 * px + py, 1 - c), sib).wait_recv()
        for cp in started:
            cp.wait_send()

    return _call(
        body, name="gather_weights",
        out_shape=[jax.ShapeDtypeStruct((4 * a.shape[0], a.shape[1]), BF16) for a in shards],
        in_specs=[pl.BlockSpec(memory_space=pltpu.VMEM)] * 3,
        out_specs=[pl.BlockSpec(memory_space=pltpu.VMEM)] * 3,
        scratch_shapes=[pltpu.SemaphoreType.DMA((18,)), pltpu.SemaphoreType.DMA((18,))],
        compiler_params=_params(),
    )(*shards)


def _reduce_grads(gwt, gwkv, gwo, small):
    bigs = (gwt, gwkv, gwo)
    piece = tuple(a.shape[0] // 8 for a in bigs)
    width = tuple(a.shape[1] for a in bigs)
    views = tuple(a.reshape(2, 2, 2, r, w) for a, r, w in zip(bigs, piece, width))

    def body(g0, g1, g2, sm, o0, o1, o2, osm, *rest):
        loc, ra, s_b, r_b, acc1, s_c, r_c = (rest[3 * i:3 * i + 3] for i in range(7))
        sa, sb, sc, acc_s, send_sems, recv_sems, local_sems = rest[21:]
        x, y, c = lax.axis_index("x"), lax.axis_index("y"), lax.axis_index("c")
        sib, xn, yn = (x, y, 1 - c), (1 - x, y, c), (x, 1 - y, c)
        gs, outs = (g0, g1, g2), (o0, o1, o2)

        def copy(k, src, dst, to):
            return pltpu.make_async_remote_copy(src_ref=src, dst_ref=dst, send_sem=send_sems.at[k],
                                                recv_sem=recv_sems.at[k], device_id=to, device_id_type=MESH)

        started = []

        def go(cp):
            cp.start()
            started.append(cp)

        mine = []
        for a in range(3):
            go(copy(a, gs[a].at[:, :, 1 - c], ra[a], sib))
            cp = pltpu.make_async_copy(gs[a].at[:, :, c], loc[a], local_sems.at[a])
            cp.start()
            mine.append(cp)
        go(copy(3, sm, sa, sib))
        for a in range(3):
            hw = width[a] // 2
            mine[a].wait()
            copy(a, gs[a].at[:, :, 1 - c], ra[a], sib).wait_recv()
            ra[a][...] = loc[a][...] + ra[a][...]
            s_b[a][0] = ra[a][1 - x, :, :, :hw].astype(BF16)
            s_b[a][1] = ra[a][:, 1 - y, :, hw:].astype(BF16)
            go(copy(4 + 2 * a, s_b[a].at[0], r_b[a].at[0], xn))
            go(copy(5 + 2 * a, s_b[a].at[1], r_b[a].at[1], yn))
        copy(3, sm, sa, sib).wait_recv()
        acc_s[...] = sm[...] + sa[...]
        go(copy(10, acc_s, sb, xn))
        for a in range(3):
            hw = width[a] // 2
            copy(4 + 2 * a, s_b[a].at[0], r_b[a].at[0], xn).wait_recv()
            copy(5 + 2 * a, s_b[a].at[1], r_b[a].at[1], yn).wait_recv()
            acc1[a][0] = ra[a][x, :, :, :hw] + r_b[a][0].astype(F32)
            acc1[a][1] = ra[a][:, y, :, hw:] + r_b[a][1].astype(F32)
            s_c[a][0] = acc1[a][0, 1 - y].astype(BF16)
            s_c[a][1] = acc1[a][1, 1 - x].astype(BF16)
            go(copy(11 + 2 * a, s_c[a].at[0], r_c[a].at[0], yn))
            go(copy(12 + 2 * a, s_c[a].at[1], r_c[a].at[1], xn))
        copy(10, acc_s, sb, xn).wait_recv()
        sb[...] = acc_s[...] + sb[...]
        go(copy(17, sb, sc, yn))
        for a in range(3):
            hw = width[a] // 2
            copy(11 + 2 * a, s_c[a].at[0], r_c[a].at[0], yn).wait_recv()
            copy(12 + 2 * a, s_c[a].at[1], r_c[a].at[1], xn).wait_recv()
            my_rows = pl.ds(pl.multiple_of(c * piece[a], 8), piece[a])
            outs[a][my_rows, :hw] = acc1[a][0, y] + r_c[a][0].astype(F32)
            outs[a][my_rows, hw:] = acc1[a][1, x] + r_c[a][1].astype(F32)
            go(copy(18 + a, outs[a].at[my_rows, :], outs[a].at[my_rows, :], sib))
        copy(17, sb, sc, yn).wait_recv()
        osm[...] = sb[...] + sc[...]
        for a in range(3):
            theirs = outs[a].at[pl.ds(pl.multiple_of((1 - c) * piece[a], 8), piece[a]), :]
            copy(18 + a, theirs, theirs, sib).wait_recv()
        for cp in started:
            cp.wait_send()

    vm = pl.BlockSpec(memory_space=pltpu.VMEM)
    hbm = pl.BlockSpec(memory_space=pl.ANY)
    kinds = (((2, 2), 1, F32), ((2, 2), 1, F32), ((2, 2), 2, BF16), ((2, 2), 2, BF16), ((2, 2), 2, F32),
             ((2,), 2, BF16), ((2,), 2, BF16))
    scratch = [pltpu.VMEM(lead + (r, w // split), dt) for lead, split, dt in kinds for r, w in zip(piece, width)]
    scratch += [pltpu.VMEM(small.shape, F32) for _ in range(4)]
    scratch += [pltpu.SemaphoreType.DMA((21,)), pltpu.SemaphoreType.DMA((21,)), pltpu.SemaphoreType.DMA((3,))]
    return _call(
        body, name="reduce_grads",
        out_shape=[jax.ShapeDtypeStruct((2 * r, w), F32) for r, w in zip(piece, width)]
        + [jax.ShapeDtypeStruct(small.shape, F32)],
        in_specs=[hbm, hbm, hbm, vm],
        out_specs=[vm, vm, vm, vm],
        scratch_shapes=scratch,
        compiler_params=_params(),
    )(*views, small)


def _adamw(w, g, m, v):
    rows, cols = w.shape
    tm = max(t for t in range(8, 257, 8) if rows % t == 0)

    def body(w_ref, g_ref, m_ref, v_ref, d_ref, nm_ref, nv_ref):
        gv = g_ref[...]
        nm = ADAM_B1 * m_ref[...] + (1.0 - ADAM_B1) * gv
        nv = ADAM_B2 * v_ref[...] + (1.0 - ADAM_B2) * (gv * gv)
        m_hat = nm / (1.0 - ADAM_B1 ** ADAM_STEP)
        v_hat = nv / (1.0 - ADAM_B2 ** ADAM_STEP)
        d_ref[...] = -ADAM_LR * (m_hat / (jnp.sqrt(v_hat) + ADAM_EPS) + ADAM_WD * w_ref[...])
        nm_ref[...] = nm
        nv_ref[...] = nv

    blk = pl.BlockSpec((tm, cols), lambda i: (i, 0))
    return _call(
        body, name="adamw", grid=(rows // tm,),
        in_specs=[blk] * 4, out_specs=[blk] * 3,
        out_shape=[jax.ShapeDtypeStruct((rows, cols), F32)] * 3,
        compiler_params=_params(),
    )(w, g, m, v)


def _rows8(a, rows=8):
    return jnp.pad(a, ((0, rows - a.shape[0]), (0, LANES - a.shape[1])))


def _pack_small(ws, ng, mng, vg, b, aq, ak, mq, mk, err2=None):
    tail = jnp.zeros((SM_ROWS - SM_LOSS, LANES), F32) if err2 is None else _rows8(err2.reshape(8, LANES), SM_ROWS - SM_LOSS)
    rows = [ws.reshape(512, LANES), ng.reshape(8, LANES), mng.reshape(8, LANES), _rows8(vg.reshape(2, LANES)),
            _rows8(b.reshape(4, LANES)), _rows8(aq), _rows8(ak), _rows8(mq), _rows8(mk), tail]
    return jnp.concatenate(rows, axis=0)


def _unpack_small(p):
    gain = lambda r: p[r:r + 1, :HEAD_DIM]
    return (p[SM_NG:SM_NG + 8].reshape(1, D_MODEL), p[SM_VG:SM_VG + 2].reshape(1, 4, HEAD_DIM),
            p[SM_WS:SM_WS + 512].reshape(1, 4, CHUNK, CHUNK), p[SM_B:SM_B + 4].reshape(1, 4, CHUNK),
            gain(SM_AQ), gain(SM_AK), p[SM_MNG:SM_MNG + 8].reshape(1, D_MODEL), gain(SM_MQG), gain(SM_MKG))


def _local_grads(x, mem, tgt, norm_gain, wt, gmlp_v_gain, gmlp_w_s, gmlp_b, attn_q_gain, attn_k_gain,
                 mem_norm_gain, wkv, mem_q_gain, mem_k_gain, wo):
    vg = gmlp_v_gain.reshape(1, GMLP_W)
    bias_full = jnp.repeat(gmlp_b.T, HEAD_DIM, axis=1)
    gq2, gk2 = jnp.tile(attn_q_gain, (1, 2)), jnp.tile(attn_k_gain, (1, 2))
    qg4, kg4 = jnp.tile(mem_q_gain, (1, 4)), jnp.tile(mem_k_gain, (1, 4))

    proj = _fwd_proj(x, norm_gain, wt)
    yg = _gmlp_fwd(proj, vg, gmlp_w_s, bias_full)
    o, lse, ya = _attn_fwd(proj, gq2, gk2)
    kv, hm = _mem_kv(mem, mem_norm_gain, wkv)
    om, ym = _mem_fwd(proj, kv, qg4, kg4)
    dy, dyc, g_wo, err2 = _out_loss(yg, ya, ym, x, tgt, wo)
    dg, g_ws, g_bfull, g_vg = _gmlp_bwd(proj, dyc, vg, gmlp_w_s, bias_full)
    daq, dak, dav, dag, g_aq, g_ak = _attn_bwd(proj, o, lse, dyc, gq2, gk2)
    dmq, dmg, g_mq, g_mk, g_wkv, g_mng = _mem_bwd(proj, om, dyc, kv, hm, mem, mem_norm_gain, wkv, qg4, kg4)
    gx, g_wt, g_ng = _proj_bwd(x, dy, norm_gain, wt, dg, daq, dak, dav, dag, dmq, dmg)

    g_b = g_bfull[:, ::HEAD_DIM].T
    small = _pack_small(g_ws, g_ng, g_mng, g_vg, g_b, g_aq[:, :HEAD_DIM], g_ak[:, :HEAD_DIM],
                        g_mq[:, :HEAD_DIM], g_mk[:, :HEAD_DIM], err2)
    return gx, g_wt, g_wkv, g_wo, small


def kernel(x, mem, norm_gain, w_in, gmlp_v_gain, gmlp_w_s, gmlp_b, attn_q_gain, attn_k_gain, mem_norm_gain, w_mem_kv, mem_q_gain, mem_k_gain, w_out, loss_target, m_norm_gain, m_w_in, m_gmlp_v_gain, m_gmlp_w_s, m_gmlp_b, m_attn_q_gain, m_attn_k_gain, m_mem_norm_gain, m_w_mem_kv, m_mem_q_gain, m_mem_k_gain, m_w_out, v_norm_gain, v_w_in, v_gmlp_v_gain, v_gmlp_w_s, v_gmlp_b, v_attn_q_gain, v_attn_k_gain, v_mem_norm_gain, v_w_mem_kv, v_mem_q_gain, v_mem_k_gain, v_w_out):
    wt, wkv, wo = _gather_weights(w_in[0].T, w_mem_kv[0], w_out[0])
    gx, g_wt, g_wkv, g_wo, small = _local_grads(
        x[0], mem[0], loss_target[0], norm_gain, wt, gmlp_v_gain[0], gmlp_w_s[0], gmlp_b[0],
        attn_q_gain, attn_k_gain, mem_norm_gain, wkv, mem_q_gain, mem_k_gain, wo)
    g_wt_sh, g_wkv_sh, g_wo_sh, g_small = _reduce_grads(g_wt, g_wkv, g_wo, small)
    loss = (0.5 / D_MODEL) * jnp.sum(g_small[SM_LOSS:SM_LOSS + 8])

    pack = lambda p: _pack_small(p[3][0], p[0], p[7], p[2][0], p[4][0], p[5], p[6], p[9], p[10])
    ws = (norm_gain, w_in, gmlp_v_gain, gmlp_w_s, gmlp_b, attn_q_gain, attn_k_gain, mem_norm_gain, w_mem_kv,
          mem_q_gain, mem_k_gain, w_out)
    ms = (m_norm_gain, m_w_in, m_gmlp_v_gain, m_gmlp_w_s, m_gmlp_b, m_attn_q_gain, m_attn_k_gain, m_mem_norm_gain,
          m_w_mem_kv, m_mem_q_gain, m_mem_k_gain, m_w_out)
    vs = (v_norm_gain, v_w_in, v_gmlp_v_gain, v_gmlp_w_s, v_gmlp_b, v_attn_q_gain, v_attn_k_gain, v_mem_norm_gain,
          v_w_mem_kv, v_mem_q_gain, v_mem_k_gain, v_w_out)
    big = {1: g_wt_sh, 8: g_wkv_sh, 11: g_wo_sh}
    form = lambda t, i: t[i][0].T if i == 1 else t[i][0]
    upd = {i: _adamw(form(ws, i), big[i], form(ms, i), form(vs, i)) for i in big}
    upd_small = _adamw(pack(ws), g_small, pack(ms), pack(vs))
    g_sm = _unpack_small(g_small)
    d_sm, m_sm, v_sm = (_unpack_small(u) for u in upd_small)
    small_pos = (0, 2, 3, 4, 5, 6, 7, 9, 10)

    def leaves(big_of, small_of):
        out = [None] * 12
        for i in big:
            out[i] = (big_of(i).T if i == 1 else big_of(i))[None]
        for j, i in enumerate(small_pos):
            out[i] = small_of[j]
        return out

    grads = leaves(lambda i: big[i], g_sm)
    deltas = leaves(lambda i: upd[i][0], d_sm)
    new_m = leaves(lambda i: upd[i][1], m_sm)
    new_v = leaves(lambda i: upd[i][2], v_sm)
    return (loss, gx[None], *grads, *deltas, *new_m, *new_v)
```

```python
import functools
import math

import jax
import jax.numpy as jnp
from jax import lax
from jax.experimental import pallas as pl
from jax.experimental.pallas import tpu as pltpu

F32 = jnp.float32
BF16 = jnp.bfloat16

SEQ = 4096
D_MODEL = 1024
HEAD_DIM = 64
LANES = 128
CHUNK = 128
GMLP_W, ATTN_W, MEM_W = 256, 512, 256
IN_W = 3 * GMLP_W + 4 * ATTN_W + 2 * MEM_W
MEM_LEN = 256
DILATIONS = (1, 4, 16)
EPS = 1e-6
QK_SCALE = 1.0 / math.sqrt(HEAD_DIM)
C_GU, C_GV, C_GG, C_AQ, C_AK, C_AV, C_AG, C_MQ, C_MG = 0, 256, 512, 768, 1280, 1792, 2304, 2816, 3072

ADAM_LR, ADAM_B1, ADAM_B2, ADAM_EPS, ADAM_WD, ADAM_STEP = 0.001, 0.9, 0.999, 1e-08, 0.01, 10

VMEM_LIMIT = 48 * 1024 * 1024
ATTN_UNROLL = 4
MESH = pl.DeviceIdType.MESH

TINY_ORDER = (0, 2, 4, 5, 6, 7, 9, 10)


def _call(body, **kw):
    return pl.pallas_call(body, **kw)


def _params(**kw):
    return pltpu.CompilerParams(vmem_limit_bytes=VMEM_LIMIT, **kw)


def _dot(a, b):
    return jnp.dot(a, b, preferred_element_type=F32)


def _dot_nt(a, b):
    return lax.dot_general(a, b, (((1,), (1,)), ((), ())), preferred_element_type=F32)


def _dot_tn(a, b):
    return lax.dot_general(a, b, (((0,), (0,)), ((), ())), preferred_element_type=F32)


def _head_blockdiag():
    r = lax.shift_right_logical(lax.broadcasted_iota(jnp.int32, (LANES, LANES), 0), 6)
    c = lax.shift_right_logical(lax.broadcasted_iota(jnp.int32, (LANES, LANES), 1), 6)
    return jnp.where(r == c, 1.0, 0.0).astype(BF16)


def _headsum(v, bd):
    hi = v.astype(BF16)
    lo = (v - hi.astype(F32)).astype(BF16)
    return _dot(hi, bd) + _dot(lo, bd)


def _lo_mask(rows):
    return lax.broadcasted_iota(jnp.int32, (rows, LANES), 1) < HEAD_DIM


def _sigmoid(x):
    return 1.0 / (1.0 + jnp.exp(-x))


def _fold_heads(v):
    return v + pltpu.roll(v, HEAD_DIM, 1)


def _put_rows(ref, vec, accumulate=False):
    for j in range(vec.shape[1] // LANES):
        piece = vec[:, j * LANES:(j + 1) * LANES]
        ref[j:j + 1, :] = ref[j:j + 1, :] + piece if accumulate else piece


def _fwd_proj(x, gain, wt):
    tm = 256

    def body(x_ref, g_ref, wt_ref, o_ref):
        xv = x_ref[...]
        ms = jnp.mean(xv * xv, axis=-1, keepdims=True)
        h = (xv * lax.rsqrt(ms + EPS) * g_ref[...]).astype(BF16)
        o_ref[...] = _dot_nt(h, wt_ref[...])

    return _call(
        body, name="fwd_proj", grid=(SEQ // tm,),
        in_specs=[pl.BlockSpec((tm, D_MODEL), lambda i: (i, 0)),
                  pl.BlockSpec((1, D_MODEL), lambda i: (0, 0)),
                  pl.BlockSpec((IN_W, D_MODEL), lambda i: (0, 0))],
        out_specs=pl.BlockSpec((tm, IN_W), lambda i: (i, 0)),
        out_shape=jax.ShapeDtypeStruct((SEQ, IN_W), F32),
        compiler_params=_params(),
    )(x, gain, wt)


def _gmlp_weights(w_ref):
    ti = lax.broadcasted_iota(jnp.int32, (CHUNK, CHUNK), 0)
    si = lax.broadcasted_iota(jnp.int32, (CHUNK, CHUNK), 1)
    tril = si <= ti
    return tril, [jnp.where(tril, w_ref[h], 0.0).astype(BF16) for h in range(4)]


def _gmlp_fwd(proj, vgain, w_s, bias_full):
    tm = 512

    def body(p_ref, vg_ref, w_ref, b_ref, y_ref):
        bd = _head_blockdiag()
        lo = _lo_mask(CHUNK)
        _, wm = _gmlp_weights(w_ref)
        for c in range(tm // CHUNK):
            rows = pl.ds(c * CHUNK, CHUNK)
            for p in range(2):
                cs = slice(p * LANES, (p + 1) * LANES)
                u = p_ref[rows, C_GU + p * LANES:C_GU + (p + 1) * LANES]
                v = p_ref[rows, C_GV + p * LANES:C_GV + (p + 1) * LANES]
                gt = p_ref[rows, C_GG + p * LANES:C_GG + (p + 1) * LANES]
                r = lax.rsqrt(_headsum(v * v, bd) * (1.0 / HEAD_DIM) + EPS)
                vn = (v * r * vg_ref[:, cs]).astype(BF16)
                sp = jnp.where(lo, _dot(wm[2 * p], vn), _dot(wm[2 * p + 1], vn)) + b_ref[:, cs]
                y_ref[rows, cs] = (u * sp * (gt * _sigmoid(gt))).astype(BF16)

    return _call(
        body, name="gmlp_fwd", grid=(SEQ // tm,),
        in_specs=[pl.BlockSpec((tm, 3 * GMLP_W), lambda i: (i, 0)),
                  pl.BlockSpec((1, GMLP_W), lambda i: (0, 0)),
                  pl.BlockSpec((4, CHUNK, CHUNK), lambda i: (0, 0, 0)),
                  pl.BlockSpec((CHUNK, GMLP_W), lambda i: (0, 0))],
        out_specs=pl.BlockSpec((tm, GMLP_W), lambda i: (i, 0)),
        out_shape=jax.ShapeDtypeStruct((SEQ, GMLP_W), BF16),
        compiler_params=_params(),
    )(proj, vgain, w_s, bias_full)


def _gmlp_bwd(proj, dyc, vgain, w_s, bias_full):
    tm = 512
    nsteps = SEQ // tm

    def body(p_ref, dy_ref, vg_ref, w_ref, b_ref, dg_ref, gw_ref, gb_ref, gv_ref):
        i = pl.program_id(0)
        bd = _head_blockdiag()
        lo = _lo_mask(CHUNK)
        tril, wm = _gmlp_weights(w_ref)
        ri = lax.broadcasted_iota(jnp.int32, (16, LANES), 0)
        li = lax.broadcasted_iota(jnp.int32, (16, LANES), 1)
        head_rows = [jnp.where(((ri == 2 * p) & (li < HEAD_DIM)) | ((ri == 2 * p + 1) & (li >= HEAD_DIM)), 1.0, 0.0).astype(BF16)
                     for p in range(2)]

        @pl.when(i == 0)
        def _():
            gw_ref[...] = jnp.zeros_like(gw_ref)
            gb_ref[...] = jnp.zeros_like(gb_ref)
            gv_ref[...] = jnp.zeros_like(gv_ref)

        for c in range(tm // CHUNK):
            rows = pl.ds(c * CHUNK, CHUNK)
            for p in range(2):
                cs = slice(p * LANES, (p + 1) * LANES)
                u = p_ref[rows, C_GU + p * LANES:C_GU + (p + 1) * LANES]
                v = p_ref[rows, C_GV + p * LANES:C_GV + (p + 1) * LANES]
                gt = p_ref[rows, C_GG + p * LANES:C_GG + (p + 1) * LANES]
                dy = dy_ref[rows, cs]
                g = vg_ref[:, cs]
                r = lax.rsqrt(_headsum(v * v, bd) * (1.0 / HEAD_DIM) + EPS)
                z = v * r
                vn = (z * g).astype(BF16)
                sp = jnp.where(lo, _dot(wm[2 * p], vn), _dot(wm[2 * p + 1], vn)) + b_ref[:, cs]
                sg = _sigmoid(gt)
                sl = gt * sg
                dsl = sg * (1.0 + gt * (1.0 - sg))
                du = dy * sp * sl
                dsp = dy * u * sl
                dgt = dy * u * sp * dsl
                dspb = dsp.astype(BF16)
                dvn = jnp.where(lo, _dot_tn(wm[2 * p], dspb), _dot_tn(wm[2 * p + 1], dspb))
                gw_ref[2 * p] += _dot_nt(jnp.where(lo, dsp, 0.0).astype(BF16), vn)
                gw_ref[2 * p + 1] += _dot_nt(jnp.where(lo, 0.0, dsp).astype(BF16), vn)
                dsp_lo = (dsp - dspb.astype(F32)).astype(BF16)
                gb_ref[...] += (_dot_nt(head_rows[p], dspb) + _dot_nt(head_rows[p], dsp_lo))[0:8]
                gvp = jnp.sum(dvn * z, axis=0, keepdims=True)
                gv_ref[2 * p:2 * p + 1, :] += gvp
                gv_ref[2 * p + 1:2 * p + 2, :] += pltpu.roll(gvp, HEAD_DIM, 1)
                dz = dvn * g
                dv = r * (dz - z * (_headsum(dz * z, bd) * (1.0 / HEAD_DIM)))
                dg_ref[rows, C_GU + p * LANES:C_GU + (p + 1) * LANES] = du.astype(BF16)
                dg_ref[rows, C_GV + p * LANES:C_GV + (p + 1) * LANES] = dv.astype(BF16)
                dg_ref[rows, C_GG + p * LANES:C_GG + (p + 1) * LANES] = dgt.astype(BF16)

        @pl.when(i == nsteps - 1)
        def _():
            for h in range(4):
                gw_ref[h] = jnp.where(tril, gw_ref[h], 0.0)

    return _call(
        body, name="gmlp_bwd", grid=(nsteps,),
        in_specs=[pl.BlockSpec((tm, 3 * GMLP_W), lambda i: (i, 0)),
                  pl.BlockSpec((tm, GMLP_W), lambda i: (i, 0)),
                  pl.BlockSpec((1, GMLP_W), lambda i: (0, 0)),
                  pl.BlockSpec((4, CHUNK, CHUNK), lambda i: (0, 0, 0)),
                  pl.BlockSpec((CHUNK, GMLP_W), lambda i: (0, 0))],
        out_specs=[pl.BlockSpec((tm, 3 * GMLP_W), lambda i: (i, 0)),
                   pl.BlockSpec((4, CHUNK, CHUNK), lambda i: (0, 0, 0)),
                   pl.BlockSpec((8, LANES), lambda i: (0, 0)),
                   pl.BlockSpec((8, LANES), lambda i: (0, 0))],
        out_shape=[jax.ShapeDtypeStruct((SEQ, 3 * GMLP_W), BF16),
                   jax.ShapeDtypeStruct((4, CHUNK, CHUNK), F32),
                   jax.ShapeDtypeStruct((8, LANES), F32),
                   jax.ShapeDtypeStruct((8, LANES), F32)],
        compiler_params=_params(),
    )(proj, dyc, vgain, w_s, bias_full)


def _band_masks():
    qi = lax.broadcasted_iota(jnp.int32, (CHUNK, 2 * CHUNK), 0)
    kj = lax.broadcasted_iota(jnp.int32, (CHUNK, 2 * CHUNK), 1)
    valid2 = ((kj < CHUNK) & (kj >= qi)) | ((kj >= CHUNK) & (kj - CHUNK <= qi))
    q1 = lax.broadcasted_iota(jnp.int32, (CHUNK, CHUNK), 0)
    k1 = lax.broadcasted_iota(jnp.int32, (CHUNK, CHUNK), 1)
    return k1 <= q1, valid2


def _stack_heads(v, lo):
    return jnp.concatenate([jnp.where(lo, v, 0.0), jnp.where(lo, 0.0, v)], axis=0).astype(BF16)


def _rows_of(ref, start, d):
    if d == 1:
        return ref.at[pl.ds(start if isinstance(start, int) else pl.multiple_of(start, CHUNK), CHUNK), :]
    return ref.at[pl.ds(start, CHUNK, stride=d), :]


def _unrolled(lo, hi, unroll, run):
    groups = (hi - lo) // unroll
    if groups:
        def body(g, carry):
            run([lo + g * unroll + t for t in range(unroll)])
            return carry

        lax.fori_loop(0, groups, body, 0)
    if lo + groups * unroll < hi:
        run(range(lo + groups * unroll, hi))


def _for_blocks(d, group_fn, unroll):
    nblk = SEQ // CHUNK
    sh = d.bit_length() - 1

    def first(j):
        return (j * CHUNK if d == 1 else j, None)

    def rest(j):
        start = (j & (d - 1)) + (j >> sh) * (CHUNK * d)
        return (start, start - CHUNK * d)

    _unrolled(0, d, unroll, lambda js: group_fn(d, [first(j) for j in js]))
    _unrolled(d, nblk, unroll, lambda js: group_fn(d, [rest(j) for j in js]))


def _attn_fwd(proj, gq2, gk2):
    tn = 512

    def body(q_ref, k_ref, v_ref, g_ref, gq_ref, gk_ref, o_ref, l_ref, ya_ref, qn_ref, kn_ref):
        bd = _head_blockdiag()
        lo = _lo_mask(CHUNK)
        valid1, valid2 = _band_masks()

        def norm(t, carry):
            rows = pl.ds(pl.multiple_of(t * tn, tn), tn)
            q = q_ref[rows, :]
            qn_ref[rows, :] = q * lax.rsqrt(_headsum(q * q, bd) * (1.0 / HEAD_DIM) + EPS) * (gq_ref[...] * QK_SCALE)
            k = k_ref[rows, :]
            kn_ref[rows, :] = k * lax.rsqrt(_headsum(k * k, bd) * (1.0 / HEAD_DIM) + EPS) * gk_ref[...]
            return carry

        lax.fori_loop(0, SEQ // tn, norm, 0)

        def load_kv(ref, d, start, prev):
            own = _rows_of(ref, start, d)[...]
            if prev is None:
                return own.astype(BF16)
            return jnp.concatenate([_rows_of(ref, prev, d)[...], own], axis=0).astype(BF16)

        def group(d, blocks):
            valid = valid1 if blocks[0][1] is None else valid2
            valid = jnp.concatenate([valid, valid], axis=0)
            qs = [_rows_of(qn_ref, start, d)[...] for start, _ in blocks]
            ks = [load_kv(kn_ref, d, start, prev) for start, prev in blocks]
            vs = [load_kv(v_ref, d, start, prev) for start, prev in blocks]
            ss = [_dot_nt(_stack_heads(q, lo), k) for q, k in zip(qs, ks)]
            ms, ps, ls = [], [], []
            for s in ss:
                s = jnp.where(valid, s, -jnp.inf)
                m = jnp.max(s, axis=-1, keepdims=True)
                p = jnp.exp(s - m)
                ms.append(m)
                ls.append(jnp.sum(p, axis=-1, keepdims=True))
                ps.append(p.astype(BF16))
            os_ = [_dot(p, v) for p, v in zip(ps, vs)]
            for b, (start, _) in enumerate(blocks):
                on = os_[b] * (1.0 / ls[b])
                ln = ms[b] + jnp.log(ls[b])
                ob = jnp.where(lo, on[:CHUNK], on[CHUNK:])
                lb = jnp.where(lo, ln[:CHUNK], ln[CHUNK:])
                o_rows = _rows_of(o_ref, start, d)
                l_rows = _rows_of(l_ref, start, d)
                if d != DILATIONS[0]:
                    lold = l_rows[...]
                    mx = jnp.maximum(lold, lb)
                    ea = jnp.exp(lold - mx)
                    eb = jnp.exp(lb - mx)
                    inv = 1.0 / (ea + eb)
                    ob = o_rows[...] * (ea * inv) + ob * (eb * inv)
                    lb = mx + jnp.log(ea + eb)
                o_rows[...] = ob
                l_rows[...] = lb

        for d in DILATIONS:
            _for_blocks(d, group, ATTN_UNROLL)

        def fin(t, carry):
            rows = pl.ds(pl.multiple_of(t * tn, tn), tn)
            g = g_ref[rows, :]
            ya_ref[rows, :] = (o_ref[rows, :] * (g * _sigmoid(g))).astype(BF16)
            return carry

        lax.fori_loop(0, SEQ // tn, fin, 0)

    col = lambda c0: pl.BlockSpec((SEQ, LANES), lambda p: (0, c0 // LANES + p))
    vec = pl.BlockSpec((1, LANES), lambda p: (0, 0))
    out = pl.BlockSpec((SEQ, LANES), lambda p: (0, p))
    return _call(
        body, name="attn_fwd", grid=(ATTN_W // LANES,),
        in_specs=[col(C_AQ), col(C_AK), col(C_AV), col(C_AG), vec, vec],
        out_specs=[out, out, out],
        out_shape=[jax.ShapeDtypeStruct((SEQ, ATTN_W), F32), jax.ShapeDtypeStruct((SEQ, ATTN_W), F32),
                   jax.ShapeDtypeStruct((SEQ, ATTN_W), BF16)],
        scratch_shapes=[pltpu.VMEM((SEQ, LANES), F32), pltpu.VMEM((SEQ, LANES), F32)],
        compiler_params=_params(),
    )(proj, proj, proj, proj, gq2, gk2)


def _attn_bwd(proj, o, lse, dyc, gq2, gk2):
    tn = 512
    npairs = ATTN_W // LANES

    def body(proj_hbm, o_hbm, l_hbm, dyc_hbm, gq_ref, gk_ref,
             dq_ref, dk_ref, dv_ref, dgt_ref, gqg_ref, gkg_ref,
             qb_, kb_, vb_, gb_, ob_, lb_, yb_, dkb_, dvb_, sems):
        pair = pl.program_id(0)
        bd = _head_blockdiag()
        lo = _lo_mask(CHUNK)
        lo2 = lax.broadcasted_iota(jnp.int32, (2 * CHUNK, LANES), 1) < HEAD_DIM
        valid1, valid2 = _band_masks()
        gqs = gq_ref[...] * QK_SCALE
        gk = gk_ref[...]

        def pcol(c0):
            return proj_hbm.at[:, pl.ds(pl.multiple_of(c0 + pair * LANES, LANES), LANES)]

        def acol(hbm, c0=0):
            return hbm.at[:, pl.ds(pl.multiple_of(c0 + pair * LANES, LANES), LANES)]

        loads = [pltpu.make_async_copy(src, dst, sems.at[n]) for n, (src, dst) in enumerate((
            (pcol(C_AQ), qb_), (pcol(C_AK), kb_), (pcol(C_AG), gb_), (acol(o_hbm), ob_),
            (acol(dyc_hbm, GMLP_W), yb_), (pcol(C_AV), vb_), (acol(l_hbm), lb_)))]
        for cp in loads:
            cp.start()

        @pl.when(pair == 0)
        def _():
            gqg_ref[...] = jnp.zeros_like(gqg_ref)
            gkg_ref[...] = jnp.zeros_like(gkg_ref)

        def pre_qk(t, carry):
            rows = pl.ds(pl.multiple_of(t * tn, tn), tn)
            zero = jnp.zeros((tn, LANES), F32)
            dkb_[rows, :] = zero
            dvb_[rows, :] = zero
            q = qb_[rows, :]
            qb_[rows, :] = q * lax.rsqrt(_headsum(q * q, bd) * (1.0 / HEAD_DIM) + EPS) * gqs
            k = kb_[rows, :]
            kb_[rows, :] = k * lax.rsqrt(_headsum(k * k, bd) * (1.0 / HEAD_DIM) + EPS) * gk
            return carry

        def pre_gate(t, carry):
            rows = pl.ds(pl.multiple_of(t * tn, tn), tn)
            g = gb_[rows, :]
            ov = ob_[rows, :]
            dya = yb_[rows, :]
            sg = _sigmoid(g)
            dgt_ref[rows, :] = (dya * ov * (sg * (1.0 + g * (1.0 - sg)))).astype(BF16)
            do = dya * (g * sg)
            yb_[rows, :] = do
            ob_[rows, :] = _headsum(do * ov, bd)
            gb_[rows, :] = jnp.zeros((tn, LANES), F32)
            return carry

        loads[0].wait()
        loads[1].wait()
        lax.fori_loop(0, SEQ // tn, pre_qk, 0)
        for cp in loads[2:5]:
            cp.wait()
        lax.fori_loop(0, SEQ // tn, pre_gate, 0)
        loads[5].wait()
        loads[6].wait()

        def load_kv(ref, d, start, prev):
            own = _rows_of(ref, start, d)[...]
            if prev is None:
                return own.astype(BF16)
            return jnp.concatenate([_rows_of(ref, prev, d)[...], own], axis=0).astype(BF16)

        def group(d, blocks):
            first = blocks[0][1] is None
            valid, lok = (valid1, lo) if first else (valid2, lo2)
            chains = [(b, h) for b in range(len(blocks)) for h in range(2)]
            mask = lambda h: lo if h == 0 else ~lo
            qs = [_rows_of(qb_, start, d)[...] for start, _ in blocks]
            dos = [_rows_of(yb_, start, d)[...] for start, _ in blocks]
            lvs = [_rows_of(lb_, start, d)[...] for start, _ in blocks]
            dls = [_rows_of(ob_, start, d)[...] for start, _ in blocks]
            ks = [load_kv(kb_, d, start, prev) for start, prev in blocks]
            vs = [load_kv(vb_, d, start, prev) for start, prev in blocks]
            qbs = [q.astype(BF16) for q in qs]
            dobs = [do.astype(BF16) for do in dos]
            ss = [_dot_nt(jnp.where(mask(h), qs[b], 0.0).astype(BF16), ks[b]) for b, h in chains]
            dps = [_dot_nt(jnp.where(mask(h), dos[b], 0.0).astype(BF16), vs[b]) for b, h in chains]
            pbs, dss = [], []
            for s, dp, (b, h) in zip(ss, dps, chains):
                hc = h * HEAD_DIM
                p = jnp.exp(jnp.where(valid, s, -jnp.inf) - lvs[b][:, hc:hc + 1])
                pbs.append(p.astype(BF16))
                dss.append((p * (dp - dls[b][:, hc:hc + 1])).astype(BF16))
            dqs = [_dot(ds, ks[b]) for ds, (b, h) in zip(dss, chains)]
            dks = [_dot_tn(ds, qbs[b]) for ds, (b, h) in zip(dss, chains)]
            dvs = [_dot_tn(p, dobs[b]) for p, (b, h) in zip(pbs, chains)]
            for b, (start, prev) in enumerate(blocks):
                c0, c1 = 2 * b, 2 * b + 1
                dq_rows = _rows_of(gb_, start, d)
                dq_rows[...] = dq_rows[...] + jnp.where(lo, dqs[c0], dqs[c1])
                dkc = jnp.where(lok, dks[c0], dks[c1])
                dvc = jnp.where(lok, dvs[c0], dvs[c1])
                spans = ((start, slice(0, CHUNK)),) if first else ((prev, slice(0, CHUNK)), (start, slice(CHUNK, 2 * CHUNK)))
                for st, sl in spans:
                    dk_rows = _rows_of(dkb_, st, d)
                    dk_rows[...] = dk_rows[...] + dkc[sl]
                    dv_rows = _rows_of(dvb_, st, d)
                    dv_rows[...] = dv_rows[...] + dvc[sl]

        for d in DILATIONS:
            _for_blocks(d, group, ATTN_UNROLL)

        pltpu.sync_copy(pcol(C_AQ), qb_)
        pltpu.sync_copy(pcol(C_AK), kb_)

        def post(t, carry):
            gq_acc, gk_acc = carry
            rows = pl.ds(pl.multiple_of(t * tn, tn), tn)
            outs = []
            for raw_, acc_, gain in ((qb_, gb_, gqs), (kb_, dkb_, gk)):
                a = raw_[rows, :]
                r = lax.rsqrt(_headsum(a * a, bd) * (1.0 / HEAD_DIM) + EPS)
                z = a * r
                dn = acc_[rows, :]
                dz = dn * gain
                outs.append((r * (dz - z * (_headsum(dz * z, bd) * (1.0 / HEAD_DIM))), jnp.sum(dn * z, axis=0, keepdims=True)))
            dq_ref[rows, :] = outs[0][0].astype(BF16)
            dk_ref[rows, :] = outs[1][0].astype(BF16)
            dv_ref[rows, :] = dvb_[rows, :].astype(BF16)
            return gq_acc + outs[0][1] * QK_SCALE, gk_acc + outs[1][1]

        zero = jnp.zeros((1, LANES), F32)
        gq_acc, gk_acc = lax.fori_loop(0, SEQ // tn, post, (zero, zero))
        gqg_ref[0:1, :] += gq_acc
        gkg_ref[0:1, :] += gk_acc

        @pl.when(pair == npairs - 1)
        def _():
            gqg_ref[0:1, :] = _fold_heads(gqg_ref[0:1, :])
            gkg_ref[0:1, :] = _fold_heads(gkg_ref[0:1, :])

    hbm = pl.BlockSpec(memory_space=pl.ANY)
    vec = pl.BlockSpec((1, LANES), lambda p: (0, 0))
    blk8 = pl.BlockSpec((8, LANES), lambda p: (0, 0))
    out = pl.BlockSpec((SEQ, LANES), lambda p: (0, p))
    big = jax.ShapeDtypeStruct((SEQ, ATTN_W), BF16)
    return _call(
        body, name="attn_bwd", grid=(npairs,),
        in_specs=[hbm, hbm, hbm, hbm, vec, vec],
        out_specs=[out, out, out, out, blk8, blk8],
        out_shape=[big, big, big, big, jax.ShapeDtypeStruct((8, LANES), F32), jax.ShapeDtypeStruct((8, LANES), F32)],
        scratch_shapes=[pltpu.VMEM((SEQ, LANES), F32) for _ in range(9)] + [pltpu.SemaphoreType.DMA((7,))],
        compiler_params=_params(),
    )(proj, o, lse, dyc, gq2, gk2)


def _mem_kv(mem, gain, wkv):
    def body(m_ref, g_ref, w_ref, kv_ref, hm_ref):
        mv = m_ref[...]
        ms = jnp.mean(mv * mv, axis=-1, keepdims=True)
        hm = (mv * lax.rsqrt(ms + EPS) * g_ref[...]).astype(BF16)
        hm_ref[...] = hm
        kv_ref[...] = _dot(hm, w_ref[...])

    return _call(
        body, name="mem_kv",
        out_shape=[jax.ShapeDtypeStruct((MEM_LEN, 2 * MEM_W), F32), jax.ShapeDtypeStruct((MEM_LEN, D_MODEL), BF16)],
        compiler_params=_params(),
    )(mem, gain, wkv)


def _mem_keys(kv_ref, kg_ref, bd, p):
    mk = kv_ref[:, p * LANES:(p + 1) * LANES]
    r = lax.rsqrt(_headsum(mk * mk, bd) * (1.0 / HEAD_DIM) + EPS)
    z = mk * r
    mkn = (z * kg_ref[:, p * LANES:(p + 1) * LANES]).astype(BF16)
    mvp = kv_ref[:, MEM_W + p * LANES:MEM_W + (p + 1) * LANES].astype(BF16)
    return mkn, mvp, r, z


def _mem_fwd(proj, kv, qg4, kg4):
    tm = 512

    def body(q_ref, g_ref, kv_ref, qg_ref, kg_ref, om_ref, ym_ref):
        bd = _head_blockdiag()
        lo = _lo_mask(tm)
        for p in range(2):
            cs = slice(p * LANES, (p + 1) * LANES)
            mkn, mvp, _, _ = _mem_keys(kv_ref, kg_ref, bd, p)
            q = q_ref[:, cs]
            qn = q * lax.rsqrt(_headsum(q * q, bd) * (1.0 / HEAD_DIM) + EPS) * (qg_ref[:, cs] * QK_SCALE)
            res = []
            for h in range(2):
                qh = jnp.where(lo if h == 0 else ~lo, qn, 0.0).astype(BF16)
                s = _dot_nt(qh, mkn)
                e = jnp.exp(s - jnp.max(s, axis=-1, keepdims=True))
                res.append(_dot(e.astype(BF16), mvp) * (1.0 / jnp.sum(e, axis=-1, keepdims=True)))
            ov = jnp.where(lo, res[0], res[1])
            g = g_ref[:, cs]
            om_ref[:, cs] = ov
            ym_ref[:, cs] = (ov * (g * _sigmoid(g))).astype(BF16)

    vec = pl.BlockSpec((1, MEM_W), lambda i: (0, 0))
    return _call(
        body, name="mem_fwd", grid=(SEQ // tm,),
        in_specs=[pl.BlockSpec((tm, MEM_W), lambda i: (i, C_MQ // MEM_W)),
                  pl.BlockSpec((tm, MEM_W), lambda i: (i, C_MG // MEM_W)),
                  pl.BlockSpec((MEM_LEN, 2 * MEM_W), lambda i: (0, 0)), vec, vec],
        out_specs=[pl.BlockSpec((tm, MEM_W), lambda i: (i, 0)), pl.BlockSpec((tm, MEM_W), lambda i: (i, 0))],
        out_shape=[jax.ShapeDtypeStruct((SEQ, MEM_W), F32), jax.ShapeDtypeStruct((SEQ, MEM_W), BF16)],
        compiler_params=_params(),
    )(proj, proj, kv, qg4, kg4)


def _mem_bwd(proj, om, dyc, kv, hm, mem, mgain, wkv, qg4, kg4):
    tm = 512
    nsteps = SEQ // tm

    def body(q_ref, g_ref, om_ref, dy_ref, kv_ref, hm_ref, mem_ref, mg_ref, w_ref, qg_ref, kg_ref,
             dq_ref, dgt_ref, gqg_ref, gkg_ref, gw_ref, gmg_ref, dmk_ref, dmv_ref, gq_acc):
        i = pl.program_id(0)
        bd = _head_blockdiag()
        lo = _lo_mask(tm)
        lom = _lo_mask(MEM_LEN)

        @pl.when(i == 0)
        def _():
            dmk_ref[...] = jnp.zeros_like(dmk_ref)
            dmv_ref[...] = jnp.zeros_like(dmv_ref)
            gq_acc[...] = jnp.zeros_like(gq_acc)

        for p in range(2):
            cs = slice(p * LANES, (p + 1) * LANES)
            mkn, mvp, _, _ = _mem_keys(kv_ref, kg_ref, bd, p)
            gqs = qg_ref[:, cs] * QK_SCALE
            q = q_ref[:, cs]
            r = lax.rsqrt(_headsum(q * q, bd) * (1.0 / HEAD_DIM) + EPS)
            z = q * r
            qn = z * gqs
            qnb = qn.astype(BF16)
            g = g_ref[:, cs]
            ov = om_ref[:, cs]
            dym = dy_ref[:, cs]
            sg = _sigmoid(g)
            dgt_ref[:, cs] = (dym * ov * (sg * (1.0 + g * (1.0 - sg)))).astype(BF16)
            do = dym * (g * sg)
            dob = do.astype(BF16)
            delta = _headsum(do * ov, bd)
            parts = []
            for h in range(2):
                mh = lo if h == 0 else ~lo
                hc = h * HEAD_DIM
                qh = jnp.where(mh, qn, 0.0).astype(BF16)
                doh = jnp.where(mh, do, 0.0).astype(BF16)
                s = _dot_nt(qh, mkn)
                e = jnp.exp(s - jnp.max(s, axis=-1, keepdims=True))
                pr = e * (1.0 / jnp.sum(e, axis=-1, keepdims=True))
                dp = _dot_nt(doh, mvp)
                ds = (pr * (dp - delta[:, hc:hc + 1])).astype(BF16)
                parts.append((_dot(ds, mkn), _dot_tn(ds, qnb), _dot_tn(pr.astype(BF16), dob)))
            dqn = jnp.where(lo, parts[0][0], parts[1][0])
            dmk_ref[:, cs] += jnp.where(lom, parts[0][1], parts[1][1])
            dmv_ref[:, cs] += jnp.where(lom, parts[0][2], parts[1][2])
            dz = dqn * gqs
            dq_ref[:, cs] = (r * (dz - z * (_headsum(dz * z, bd) * (1.0 / HEAD_DIM)))).astype(BF16)
            gq_acc[:, cs] += jnp.sum(dqn * z, axis=0, keepdims=True) * QK_SCALE

        @pl.when(i == nsteps - 1)
        def _():
            gqg_ref[...] = jnp.zeros_like(gqg_ref)
            gkg_ref[...] = jnp.zeros_like(gkg_ref)
            gqg_ref[0:1, :] = _fold_heads(gq_acc[:, 0:LANES] + gq_acc[:, LANES:2 * LANES])
            dkv = []
            gk = jnp.zeros((1, LANES), F32)
            for p in range(2):
                cs = slice(p * LANES, (p + 1) * LANES)
                _, _, r, z = _mem_keys(kv_ref, kg_ref, bd, p)
                dn = dmk_ref[:, cs]
                dz = dn * kg_ref[:, cs]
                gk = gk + jnp.sum(dn * z, axis=0, keepdims=True)
                dkv.append(r * (dz - z * (_headsum(dz * z, bd) * (1.0 / HEAD_DIM))))
            gkg_ref[0:1, :] = _fold_heads(gk)
            dkvb = jnp.concatenate(dkv + [dmv_ref[...]], axis=1).astype(BF16)
            gw_ref[...] = _dot_tn(hm_ref[...], dkvb)
            dhm = _dot_nt(dkvb, w_ref[...])
            mv = mem_ref[...]
            zm = mv * lax.rsqrt(jnp.mean(mv * mv, axis=-1, keepdims=True) + EPS)
            _put_rows(gmg_ref, jnp.sum(dhm * zm, axis=0, keepdims=True))

    const = lambda shape: pl.BlockSpec(shape, lambda i: (0,) * len(shape))
    row = lambda j: pl.BlockSpec((tm, MEM_W), lambda i: (i, j))
    blk8 = jax.ShapeDtypeStruct((8, LANES), F32)
    return _call(
        body, name="mem_bwd", grid=(nsteps,),
        in_specs=[row(C_MQ // MEM_W), row(C_MG // MEM_W), row(0), row((GMLP_W + ATTN_W) // MEM_W),
                  const((MEM_LEN, 2 * MEM_W)), const((MEM_LEN, D_MODEL)), const((MEM_LEN, D_MODEL)),
                  const((1, D_MODEL)), const((D_MODEL, 2 * MEM_W)), const((1, MEM_W)), const((1, MEM_W))],
        out_specs=[row(0), row(0), const((8, LANES)), const((8, LANES)),
                   const((D_MODEL, 2 * MEM_W)), const((8, LANES))],
        out_shape=[jax.ShapeDtypeStruct((SEQ, MEM_W), BF16), jax.ShapeDtypeStruct((SEQ, MEM_W), BF16),
                   blk8, blk8, jax.ShapeDtypeStruct((D_MODEL, 2 * MEM_W), F32), blk8],
        scratch_shapes=[pltpu.VMEM((MEM_LEN, MEM_W), F32), pltpu.VMEM((MEM_LEN, MEM_W), F32),
                        pltpu.VMEM((1, MEM_W), F32)],
        compiler_params=_params(),
    )(proj, proj, om, dyc, kv, hm, mem, mgain, wkv, qg4, kg4)


def _out_loss(yg, ya, ym, x, tgt, wo):
    tm = 256
    nsteps = SEQ // tm
    parts = ((0, GMLP_W), (GMLP_W, ATTN_W), (GMLP_W + ATTN_W, MEM_W))

    def body(yg_ref, ya_ref, ym_ref, x_ref, t_ref, w_ref, dy_ref, dyc_ref, gw_ref, ls_ref):
        i = pl.program_id(0)

        @pl.when(i == 0)
        def _():
            gw_ref[...] = jnp.zeros_like(gw_ref)
            ls_ref[...] = jnp.zeros_like(ls_ref)

        ys = (yg_ref[...], ya_ref[...], ym_ref[...])
        y = sum(_dot(yv, w_ref[r0:r0 + n, :]) for yv, (r0, n) in zip(ys, parts))
        err = x_ref[...] + y - t_ref[...]
        _put_rows(ls_ref, jnp.sum(err * err, axis=0, keepdims=True), accumulate=True)
        dy = err * (1.0 / D_MODEL)
        dy_ref[...] = dy
        dyb = dy.astype(BF16)
        dyc_ref[...] = _dot_nt(dyb, w_ref[...])
        for yv, (r0, n) in zip(ys, parts):
            gw_ref[r0:r0 + n, :] += _dot_tn(yv, dyb)

    row = lambda w: pl.BlockSpec((tm, w), lambda i: (i, 0))
    const = lambda shape: pl.BlockSpec(shape, lambda i: (0, 0))
    return _call(
        body, name="out_loss", grid=(nsteps,),
        in_specs=[row(GMLP_W), row(ATTN_W), row(MEM_W), row(D_MODEL), row(D_MODEL), const((D_MODEL, D_MODEL))],
        out_specs=[row(D_MODEL), row(D_MODEL), const((D_MODEL, D_MODEL)), const((8, LANES))],
        out_shape=[jax.ShapeDtypeStruct((SEQ, D_MODEL), F32), jax.ShapeDtypeStruct((SEQ, D_MODEL), F32),
                   jax.ShapeDtypeStruct((D_MODEL, D_MODEL), F32), jax.ShapeDtypeStruct((8, LANES), F32)],
        compiler_params=_params(),
    )(yg, ya, ym, x, tgt, wo)


def _proj_bwd(x, dy, gain, wt, dg, daq, dak, dav, dag, dmq, dmg):
    tm = 256
    nsteps = SEQ // tm
    pieces = ((C_GU, 3 * GMLP_W), (C_AQ, ATTN_W), (C_AK, ATTN_W), (C_AV, ATTN_W), (C_AG, ATTN_W),
              (C_MQ, MEM_W), (C_MG, MEM_W))

    def body(x_ref, dy_ref, g_ref, wt_hbm, p0, p1, p2, p3, p4, p5, p6, gx_ref, gwt_hbm, gg_ref, wt_v, acc):
        i = pl.program_id(0)

        @pl.when(i == 0)
        def _():
            pltpu.sync_copy(wt_hbm, wt_v)
            acc[...] = jnp.zeros_like(acc)
            gg_ref[...] = jnp.zeros_like(gg_ref)

        xv = x_ref[...]
        r = lax.rsqrt(jnp.mean(xv * xv, axis=-1, keepdims=True) + EPS)
        z = xv * r
        g = g_ref[...]
        h = (z * g).astype(BF16)
        dh = jnp.zeros((tm, D_MODEL), F32)
        for pref, (c0, w) in zip((p0, p1, p2, p3, p4, p5, p6), pieces):
            dp = pref[...]
            dh = dh + _dot(dp, wt_v[c0:c0 + w, :])
            acc[c0:c0 + w, :] += _dot_tn(dp, h)
        _put_rows(gg_ref, jnp.sum(dh * z, axis=0, keepdims=True), accumulate=True)
        dz = dh * g
        gx_ref[...] = dy_ref[...] + r * (dz - z * jnp.mean(dz * z, axis=-1, keepdims=True))

        @pl.when(i == nsteps - 1)
        def _():
            pltpu.sync_copy(acc, gwt_hbm)

    row = lambda w: pl.BlockSpec((tm, w), lambda i: (i, 0))
    hbm = pl.BlockSpec(memory_space=pl.ANY)
    vec = pl.BlockSpec((1, D_MODEL), lambda i: (0, 0))
    return _call(
        body, name="proj_bwd", grid=(nsteps,),
        in_specs=[row(D_MODEL), row(D_MODEL), vec, hbm] + [row(w) for _, w in pieces],
        out_specs=[row(D_MODEL), hbm, pl.BlockSpec((8, LANES), lambda i: (0, 0))],
        out_shape=[jax.ShapeDtypeStruct((SEQ, D_MODEL), F32), jax.ShapeDtypeStruct((IN_W, D_MODEL), F32),
                   jax.ShapeDtypeStruct((8, LANES), F32)],
        scratch_shapes=[pltpu.VMEM((IN_W, D_MODEL), BF16), pltpu.VMEM((IN_W, D_MODEL), F32)],
        compiler_params=_params(),
    )(x, dy, gain, wt, dg, daq, dak, dav, dag, dmq, dmg)


def _gather_weights(wt_sh, wkv_sh, wo_sh):
    shards = (wt_sh, wkv_sh, wo_sh)
    nrows = tuple(a.shape[0] for a in shards)

    def body(a0, a1, a2, o0, o1, o2, send_sems, recv_sems):
        x, y, c = lax.axis_index("x"), lax.axis_index("y"), lax.axis_index("c")
        sib, xn, yn = (x, y, 1 - c), (1 - x, y, c), (x, 1 - y, c)
        me, cx, cy, cd = 2 * x + y, 2 * (1 - x) + y, 2 * x + (1 - y), 2 * (1 - x) + (1 - y)
        ins, outs = (a0, a1, a2), (o0, o1, o2)

        def part(a, chip, hf, quarter=None):
            n = nrows[a] // 2
            base = chip * nrows[a] + hf * n
            if quarter is not None:
                n = n // 2
                base = base + quarter * n
            return outs[a].at[pl.ds(pl.multiple_of(base, 16), n), :]

        def copy(k, ref, to):
            return pltpu.make_async_remote_copy(src_ref=ref, dst_ref=ref, send_sem=send_sems.at[k],
                                                recv_sem=recv_sems.at[k], device_id=to, device_id_type=MESH)

        started = []

        def go(cp):
            cp.start()
            started.append(cp)

        for a in range(3):
            outs[a][pl.ds(pl.multiple_of(me * nrows[a], 16), nrows[a]), :] = ins[a][...].astype(BF16)
        for a in range(3):
            go(copy(8 * a, part(a, me, c), xn))
            go(copy(8 * a + 1, part(a, me, c), yn))
        for a in range(3):
            k = 8 * a
            copy(k, part(a, cx, c), xn).wait_recv()
            go(copy(k + 4, part(a, cx, c, 1), yn))
            go(copy(k + 2, part(a, cx, c), sib))
            copy(k + 1, part(a, cy, c), yn).wait_recv()
            go(copy(k + 5, part(a, cy, c, 0), xn))
            go(copy(k + 3, part(a, cy, c), sib))
        for a in range(3):
            k = 8 * a
            copy(k + 4, part(a, cd, c, 1), yn).wait_recv()
            go(copy(k + 7, part(a, cd, c, 1), sib))
            copy(k + 5, part(a, cd, c, 0), xn).wait_recv()
            go(copy(k + 6, part(a, cd, c, 0), sib))
        for a in range(3):
            k = 8 * a
            copy(k + 2, part(a, cx, 1 - c), sib).wait_recv()
            copy(k + 3, part(a, cy, 1 - c), sib).wait_recv()
            copy(k + 6, part(a, cd, 1 - c, 0), sib).wait_recv()
            copy(k + 7, part(a, cd, 1 - c, 1), sib).wait_recv()
        for cp in started:
            cp.wait_send()

    return _call(
        body, name="gather_weights",
        out_shape=[jax.ShapeDtypeStruct((4 * a.shape[0], a.shape[1]), BF16) for a in shards],
        in_specs=[pl.BlockSpec(memory_space=pltpu.VMEM)] * 3,
        out_specs=[pl.BlockSpec(memory_space=pltpu.VMEM)] * 3,
        scratch_shapes=[pltpu.SemaphoreType.DMA((24,)), pltpu.SemaphoreType.DMA((24,))],
        compiler_params=_params(),
    )(*shards)


def _reduce_grads(gwt, gwkv, gwo, g_ws, tiny):
    bigs = (gwt, gwkv, gwo)
    piece = tuple(a.shape[0] // 8 for a in bigs)
    width = tuple(a.shape[1] for a in bigs)
    views = tuple(a.reshape(2, 2, 2, r, w) for a, r, w in zip(bigs, piece, width))

    def body(g0, g1, g2, ws_in, tiny_in, o0, o1, o2, o_ws, o_tiny, *rest):
        loc, ra, s_b, r_b, acc1, s_c, r_c = (rest[3 * i:3 * i + 3] for i in range(7))
        sm, sa, sb, sc, acc_s, send_sems, recv_sems, local_sems = rest[21:]
        n_ws = ws_in.shape[0]
        sm[0:n_ws, :] = ws_in[...]
        sm[n_ws:, :] = tiny_in[...]
        x, y, c = lax.axis_index("x"), lax.axis_index("y"), lax.axis_index("c")
        sib, xn, yn = (x, y, 1 - c), (1 - x, y, c), (x, 1 - y, c)
        gs, outs = (g0, g1, g2), (o0, o1, o2)

        def copy(k, src, dst, to):
            return pltpu.make_async_remote_copy(src_ref=src, dst_ref=dst, send_sem=send_sems.at[k],
                                                recv_sem=recv_sems.at[k], device_id=to, device_id_type=MESH)

        started = []

        def go(cp):
            cp.start()
            started.append(cp)

        mine = []
        for a in range(3):
            go(copy(a, gs[a].at[:, :, 1 - c], ra[a], sib))
            cp = pltpu.make_async_copy(gs[a].at[:, :, c], loc[a], local_sems.at[a])
            cp.start()
            mine.append(cp)
        go(copy(3, sm, sa, sib))
        for a in range(3):
            hw = width[a] // 2
            mine[a].wait()
            copy(a, gs[a].at[:, :, 1 - c], ra[a], sib).wait_recv()
            ra[a][...] = loc[a][...] + ra[a][...]
            s_b[a][0] = ra[a][1 - x, :, :, :hw].astype(BF16)
            s_b[a][1] = ra[a][:, 1 - y, :, hw:].astype(BF16)
            go(copy(4 + 2 * a, s_b[a].at[0], r_b[a].at[0], xn))
            go(copy(5 + 2 * a, s_b[a].at[1], r_b[a].at[1], yn))
        copy(3, sm, sa, sib).wait_recv()
        acc_s[...] = sm[...] + sa[...]
        go(copy(10, acc_s, sb, xn))
        for a in range(3):
            hw = width[a] // 2
            copy(4 + 2 * a, s_b[a].at[0], r_b[a].at[0], xn).wait_recv()
            copy(5 + 2 * a, s_b[a].at[1], r_b[a].at[1], yn).wait_recv()
            acc1[a][0] = ra[a][x, :, :, :hw] + r_b[a][0].astype(F32)
            acc1[a][1] = ra[a][:, y, :, hw:] + r_b[a][1].astype(F32)
            s_c[a][0] = acc1[a][0, 1 - y].astype(BF16)
            s_c[a][1] = acc1[a][1, 1 - x].astype(BF16)
            go(copy(11 + 2 * a, s_c[a].at[0], r_c[a].at[0], yn))
            go(copy(12 + 2 * a, s_c[a].at[1], r_c[a].at[1], xn))
        copy(10, acc_s, sb, xn).wait_recv()
        sb[...] = acc_s[...] + sb[...]
        go(copy(17, sb, sc, yn))
        for a in range(3):
            hw = width[a] // 2
            copy(11 + 2 * a, s_c[a].at[0], r_c[a].at[0], yn).wait_recv()
            copy(12 + 2 * a, s_c[a].at[1], r_c[a].at[1], xn).wait_recv()
            my_rows = pl.ds(pl.multiple_of(c * piece[a], 8), piece[a])
            outs[a][my_rows, :hw] = acc1[a][0, y] + r_c[a][0].astype(F32)
            outs[a][my_rows, hw:] = acc1[a][1, x] + r_c[a][1].astype(F32)
            go(copy(18 + a, outs[a].at[my_rows, :], outs[a].at[my_rows, :], sib))
        copy(17, sb, sc, yn).wait_recv()
        o_ws[...] = sb[0:n_ws, :] + sc[0:n_ws, :]
        o_tiny[...] = sb[n_ws:, :] + sc[n_ws:, :]
        for a in range(3):
            theirs = outs[a].at[pl.ds(pl.multiple_of((1 - c) * piece[a], 8), piece[a]), :]
            copy(18 + a, theirs, theirs, sib).wait_recv()
        for cp in started:
            cp.wait_send()

    vm = pl.BlockSpec(memory_space=pltpu.VMEM)
    hbm = pl.BlockSpec(memory_space=pl.ANY)
    kinds = (((2, 2), 1, F32), ((2, 2), 1, F32), ((2, 2), 2, BF16), ((2, 2), 2, BF16), ((2, 2), 2, F32),
             ((2,), 2, BF16), ((2,), 2, BF16))
    scratch = [pltpu.VMEM(lead + (r, w // split), dt) for lead, split, dt in kinds for r, w in zip(piece, width)]
    small_shape = (g_ws.shape[0] + tiny.shape[0], LANES)
    scratch += [pltpu.VMEM(small_shape, F32) for _ in range(5)]
    scratch += [pltpu.SemaphoreType.DMA((21,)), pltpu.SemaphoreType.DMA((21,)), pltpu.SemaphoreType.DMA((3,))]
    return _call(
        body, name="reduce_grads",
        out_shape=[jax.ShapeDtypeStruct((2 * r, w), F32) for r, w in zip(piece, width)]
        + [jax.ShapeDtypeStruct(g_ws.shape, F32), jax.ShapeDtypeStruct(tiny.shape, F32)],
        in_specs=[hbm, hbm, hbm, vm, vm],
        out_specs=[vm, vm, vm, vm, vm],
        scratch_shapes=scratch,
        compiler_params=_params(),
    )(*views, g_ws, tiny)


def _adam_update(w, g, m, v):
    nm = ADAM_B1 * m + (1.0 - ADAM_B1) * g
    nv = ADAM_B2 * v + (1.0 - ADAM_B2) * (g * g)
    m_hat = nm / (1.0 - ADAM_B1 ** ADAM_STEP)
    v_hat = nv / (1.0 - ADAM_B2 ** ADAM_STEP)
    return -ADAM_LR * (m_hat / (jnp.sqrt(v_hat) + ADAM_EPS) + ADAM_WD * w), nm, nv


def _adamw(w, g, m, v):
    rows, cols = w.shape
    tm = max(t for t in range(8, 257, 8) if rows % t == 0)

    def body(w_ref, g_ref, m_ref, v_ref, d_ref, nm_ref, nv_ref):
        d_ref[...], nm_ref[...], nv_ref[...] = _adam_update(w_ref[...], g_ref[...], m_ref[...], v_ref[...])

    blk = pl.BlockSpec((tm, cols), lambda i: (i, 0))
    return _call(
        body, name="adamw", grid=(rows // tm,),
        in_specs=[blk] * 4, out_specs=[blk] * 3,
        out_shape=[jax.ShapeDtypeStruct((rows, cols), F32)] * 3,
        compiler_params=_params(),
    )(w, g, m, v)


def _adamw_tiny(tiny, weights, ms, vs):
    shapes = [w.shape for w in weights]
    n = len(weights)

    def grad_of(t_ref, k, shape):
        base = 8 * k
        if shape[1] > LANES:
            return [t_ref[base + j:base + j + 1, :] for j in range(shape[1] // LANES)]
        return [t_ref[base:base + shape[0], 0:shape[1]]]

    def body(t_ref, *refs):
        w_refs, m_refs, v_refs = refs[:n], refs[n:2 * n], refs[2 * n:3 * n]
        loss_ref, outs = refs[3 * n], refs[3 * n + 1:]
        loss_ref[...] = (0.5 / D_MODEL) * jnp.sum(t_ref[8 * n:8 * n + 8, :], keepdims=True)
        for k, shape in enumerate(shapes):
            g_ref, d_ref, nm_ref, nv_ref = outs[4 * k:4 * k + 4]
            for j, g in enumerate(grad_of(t_ref, k, shape)):
                cols = slice(j * LANES, (j + 1) * LANES) if shape[1] > LANES else slice(None)
                g_ref[:, cols] = g
                d_ref[:, cols], nm_ref[:, cols], nv_ref[:, cols] = _adam_update(
                    w_refs[k][:, cols], g, m_refs[k][:, cols], v_refs[k][:, cols])

    out_shape = [jax.ShapeDtypeStruct((1, 1), F32)]
    for shape in shapes:
        out_shape += [jax.ShapeDtypeStruct(shape, F32)] * 4
    return _call(body, name="adamw_tiny", out_shape=out_shape, compiler_params=_params())(tiny, *weights, *ms, *vs)


def _local_grads(x, mem, tgt, norm_gain, wt, gmlp_v_gain, gmlp_w_s, gmlp_b, attn_q_gain, attn_k_gain,
                 mem_norm_gain, wkv, mem_q_gain, mem_k_gain, wo):
    vg = gmlp_v_gain.reshape(1, GMLP_W)
    bias_full = jnp.repeat(gmlp_b.T, HEAD_DIM, axis=1)
    gq2, gk2 = jnp.tile(attn_q_gain, (1, 2)), jnp.tile(attn_k_gain, (1, 2))
    qg4, kg4 = jnp.tile(mem_q_gain, (1, 4)), jnp.tile(mem_k_gain, (1, 4))

    proj = _fwd_proj(x, norm_gain, wt)
    yg = _gmlp_fwd(proj, vg, gmlp_w_s, bias_full)
    o, lse, ya = _attn_fwd(proj, gq2, gk2)
    kv, hm = _mem_kv(mem, mem_norm_gain, wkv)
    om, ym = _mem_fwd(proj, kv, qg4, kg4)
    dy, dyc, g_wo, err2 = _out_loss(yg, ya, ym, x, tgt, wo)
    dg, g_ws, g_b, g_vg = _gmlp_bwd(proj, dyc, vg, gmlp_w_s, bias_full)
    daq, dak, dav, dag, g_aq, g_ak = _attn_bwd(proj, o, lse, dyc, gq2, gk2)
    dmq, dmg, g_mq, g_mk, g_wkv, g_mng = _mem_bwd(proj, om, dyc, kv, hm, mem, mem_norm_gain, wkv, qg4, kg4)
    gx, g_wt, g_ng = _proj_bwd(x, dy, norm_gain, wt, dg, daq, dak, dav, dag, dmq, dmg)

    tiny = jnp.concatenate([g_ng, g_vg, g_b, g_aq, g_ak, g_mng, g_mq, g_mk, err2], axis=0)
    return gx, g_wt, g_wkv, g_wo, g_ws.reshape(4 * CHUNK, CHUNK), tiny


def kernel(x, mem, norm_gain, w_in, gmlp_v_gain, gmlp_w_s, gmlp_b, attn_q_gain, attn_k_gain, mem_norm_gain, w_mem_kv, mem_q_gain, mem_k_gain, w_out, loss_target, m_norm_gain, m_w_in, m_gmlp_v_gain, m_gmlp_w_s, m_gmlp_b, m_attn_q_gain, m_attn_k_gain, m_mem_norm_gain, m_w_mem_kv, m_mem_q_gain, m_mem_k_gain, m_w_out, v_norm_gain, v_w_in, v_gmlp_v_gain, v_gmlp_w_s, v_gmlp_b, v_attn_q_gain, v_attn_k_gain, v_mem_norm_gain, v_w_mem_kv, v_mem_q_gain, v_mem_k_gain, v_w_out):
    wt, wkv, wo = _gather_weights(w_in[0].T, w_mem_kv[0], w_out[0])
    gx, g_wt, g_wkv, g_wo, g_ws, tiny = _local_grads(
        x[0], mem[0], loss_target[0], norm_gain, wt, gmlp_v_gain[0], gmlp_w_s[0], gmlp_b[0],
        attn_q_gain, attn_k_gain, mem_norm_gain, wkv, mem_q_gain, mem_k_gain, wo)
    g_wt_sh, g_wkv_sh, g_wo_sh, g_ws, tiny = _reduce_grads(g_wt, g_wkv, g_wo, g_ws, tiny)

    ws = (norm_gain, w_in, gmlp_v_gain, gmlp_w_s, gmlp_b, attn_q_gain, attn_k_gain, mem_norm_gain, w_mem_kv,
          mem_q_gain, mem_k_gain, w_out)
    ms = (m_norm_gain, m_w_in, m_gmlp_v_gain, m_gmlp_w_s, m_gmlp_b, m_attn_q_gain, m_attn_k_gain, m_mem_norm_gain,
          m_w_mem_kv, m_mem_q_gain, m_mem_k_gain, m_w_out)
    vs = (v_norm_gain, v_w_in, v_gmlp_v_gain, v_gmlp_w_s, v_gmlp_b, v_attn_q_gain, v_attn_k_gain, v_mem_norm_gain,
          v_w_mem_kv, v_mem_q_gain, v_mem_k_gain, v_w_out)
    form = {1: lambda a: a[0].T, 3: lambda a: a.reshape(4 * CHUNK, CHUNK), 2: lambda a: a[0], 4: lambda a: a[0],
            8: lambda a: a[0], 11: lambda a: a[0]}
    back = {1: lambda a: a.T[None], 3: lambda a: a.reshape(1, 4, CHUNK, CHUNK), 2: lambda a: a[None],
            4: lambda a: a[None], 8: lambda a: a[None], 11: lambda a: a[None]}
    fwd = lambda t, i: form.get(i, lambda a: a)(t[i])
    out = {}
    for i, g in ((1, g_wt_sh), (3, g_ws), (8, g_wkv_sh), (11, g_wo_sh)):
        out[i] = (g, *_adamw(fwd(ws, i), g, fwd(ms, i), fwd(vs, i)))
    res = _adamw_tiny(tiny, [fwd(ws, i) for i in TINY_ORDER], [fwd(ms, i) for i in TINY_ORDER],
                      [fwd(vs, i) for i in TINY_ORDER])
    for k, i in enumerate(TINY_ORDER):
        out[i] = res[1 + 4 * k:5 + 4 * k]
    leaves = [[back.get(i, lambda a: a)(out[i][j]) for i in range(12)] for j in range(4)]
    return (res[0].reshape(()), gx[None], *leaves[0], *leaves[1], *leaves[2], *leaves[3])
```

```python
import functools
import math

import jax
import jax.numpy as jnp
from jax import lax
from jax.experimental import pallas as pl
from jax.experimental.pallas import tpu as pltpu

F32 = jnp.float32
BF16 = jnp.bfloat16

SEQ = 4096
D_MODEL = 1024
HEAD_DIM = 64
LANES = 128
CHUNK = 128
GMLP_W, ATTN_W, MEM_W = 256, 512, 256
IN_W = 3 * GMLP_W + 4 * ATTN_W + 2 * MEM_W
MEM_LEN = 256
DILATIONS = (1, 4, 16)
EPS = 1e-6
QK_SCALE = 1.0 / math.sqrt(HEAD_DIM)
C_GU, C_GV, C_GG, C_AQ, C_AK, C_AV, C_AG, C_MQ, C_MG = 0, 256, 512, 768, 1280, 1792, 2304, 2816, 3072

ADAM_LR, ADAM_B1, ADAM_B2, ADAM_EPS, ADAM_WD, ADAM_STEP = 0.001, 0.9, 0.999, 1e-08, 0.01, 10

VMEM_LIMIT = 48 * 1024 * 1024
ATTN_UNROLL = 4
MESH = pl.DeviceIdType.MESH

TINY_ORDER = (0, 2, 4, 5, 6, 7, 9, 10)


def _call(body, **kw):
    return pl.pallas_call(body, **kw)


def _params(**kw):
    return pltpu.CompilerParams(vmem_limit_bytes=VMEM_LIMIT, **kw)


def _dot(a, b):
    return jnp.dot(a, b, preferred_element_type=F32)


def _dot_nt(a, b):
    return lax.dot_general(a, b, (((1,), (1,)), ((), ())), preferred_element_type=F32)


def _dot_tn(a, b):
    return lax.dot_general(a, b, (((0,), (0,)), ((), ())), preferred_element_type=F32)


def _head_blockdiag():
    r = lax.shift_right_logical(lax.broadcasted_iota(jnp.int32, (LANES, LANES), 0), 6)
    c = lax.shift_right_logical(lax.broadcasted_iota(jnp.int32, (LANES, LANES), 1), 6)
    return jnp.where(r == c, 1.0, 0.0).astype(BF16)


def _headsum(v, bd):
    hi = v.astype(BF16)
    lo = (v - hi.astype(F32)).astype(BF16)
    return _dot(hi, bd) + _dot(lo, bd)


def _lo_mask(rows):
    return lax.broadcasted_iota(jnp.int32, (rows, LANES), 1) < HEAD_DIM


def _sigmoid(x):
    return 1.0 / (1.0 + jnp.exp(-x))


def _fold_heads(v):
    return v + pltpu.roll(v, HEAD_DIM, 1)


def _put_rows(ref, vec, accumulate=False):
    for j in range(vec.shape[1] // LANES):
        piece = vec[:, j * LANES:(j + 1) * LANES]
        ref[j:j + 1, :] = ref[j:j + 1, :] + piece if accumulate else piece


def _fwd_proj(x, gain, wt):
    tm = 512

    def body(x_ref, g_ref, wt_ref, o_ref):
        xv = x_ref[...]
        ms = jnp.mean(xv * xv, axis=-1, keepdims=True)
        h = (xv * lax.rsqrt(ms + EPS) * g_ref[...]).astype(BF16)
        o_ref[...] = _dot_nt(h, wt_ref[...])

    return _call(
        body, name="fwd_proj", grid=(SEQ // tm,),
        in_specs=[pl.BlockSpec((tm, D_MODEL), lambda i: (i, 0)),
                  pl.BlockSpec((1, D_MODEL), lambda i: (0, 0)),
                  pl.BlockSpec((IN_W, D_MODEL), lambda i: (0, 0))],
        out_specs=pl.BlockSpec((tm, IN_W), lambda i: (i, 0)),
        out_shape=jax.ShapeDtypeStruct((SEQ, IN_W), F32),
        compiler_params=_params(),
    )(x, gain, wt)


def _gmlp_weights(w_ref):
    ti = lax.broadcasted_iota(jnp.int32, (CHUNK, CHUNK), 0)
    si = lax.broadcasted_iota(jnp.int32, (CHUNK, CHUNK), 1)
    tril = si <= ti
    return tril, [jnp.where(tril, w_ref[h], 0.0).astype(BF16) for h in range(4)]


def _gmlp_fwd(proj, vgain, w_s, bias_full):
    tm = 512

    def body(p_ref, vg_ref, w_ref, b_ref, y_ref):
        bd = _head_blockdiag()
        lo = _lo_mask(CHUNK)
        _, wm = _gmlp_weights(w_ref)
        for c in range(tm // CHUNK):
            rows = pl.ds(c * CHUNK, CHUNK)
            for p in range(2):
                cs = slice(p * LANES, (p + 1) * LANES)
                u = p_ref[rows, C_GU + p * LANES:C_GU + (p + 1) * LANES]
                v = p_ref[rows, C_GV + p * LANES:C_GV + (p + 1) * LANES]
                gt = p_ref[rows, C_GG + p * LANES:C_GG + (p + 1) * LANES]
                r = lax.rsqrt(_headsum(v * v, bd) * (1.0 / HEAD_DIM) + EPS)
                vn = (v * r * vg_ref[:, cs]).astype(BF16)
                sp = jnp.where(lo, _dot(wm[2 * p], vn), _dot(wm[2 * p + 1], vn)) + b_ref[:, cs]
                y_ref[rows, cs] = (u * sp * (gt * _sigmoid(gt))).astype(BF16)

    return _call(
        body, name="gmlp_fwd", grid=(SEQ // tm,),
        in_specs=[pl.BlockSpec((tm, 3 * GMLP_W), lambda i: (i, 0)),
                  pl.BlockSpec((1, GMLP_W), lambda i: (0, 0)),
                  pl.BlockSpec((4, CHUNK, CHUNK), lambda i: (0, 0, 0)),
                  pl.BlockSpec((CHUNK, GMLP_W), lambda i: (0, 0))],
        out_specs=pl.BlockSpec((tm, GMLP_W), lambda i: (i, 0)),
        out_shape=jax.ShapeDtypeStruct((SEQ, GMLP_W), BF16),
        compiler_params=_params(),
    )(proj, vgain, w_s, bias_full)


def _gmlp_bwd(proj, dyc, vgain, w_s, bias_full):
    tm = 512
    nsteps = SEQ // tm

    def body(p_ref, dy_ref, vg_ref, w_ref, b_ref, dg_ref, gw_ref, gb_ref, gv_ref):
        i = pl.program_id(0)
        bd = _head_blockdiag()
        lo = _lo_mask(CHUNK)
        tril, wm = _gmlp_weights(w_ref)
        ri = lax.broadcasted_iota(jnp.int32, (16, LANES), 0)
        li = lax.broadcasted_iota(jnp.int32, (16, LANES), 1)
        head_rows = [jnp.where(((ri == 2 * p) & (li < HEAD_DIM)) | ((ri == 2 * p + 1) & (li >= HEAD_DIM)), 1.0, 0.0).astype(BF16)
                     for p in range(2)]

        @pl.when(i == 0)
        def _():
            gw_ref[...] = jnp.zeros_like(gw_ref)
            gb_ref[...] = jnp.zeros_like(gb_ref)
            gv_ref[...] = jnp.zeros_like(gv_ref)

        for c in range(tm // CHUNK):
            rows = pl.ds(c * CHUNK, CHUNK)
            for p in range(2):
                cs = slice(p * LANES, (p + 1) * LANES)
                u = p_ref[rows, C_GU + p * LANES:C_GU + (p + 1) * LANES]
                v = p_ref[rows, C_GV + p * LANES:C_GV + (p + 1) * LANES]
                gt = p_ref[rows, C_GG + p * LANES:C_GG + (p + 1) * LANES]
                dy = dy_ref[rows, cs]
                g = vg_ref[:, cs]
                r = lax.rsqrt(_headsum(v * v, bd) * (1.0 / HEAD_DIM) + EPS)
                z = v * r
                vn = (z * g).astype(BF16)
                sp = jnp.where(lo, _dot(wm[2 * p], vn), _dot(wm[2 * p + 1], vn)) + b_ref[:, cs]
                sg = _sigmoid(gt)
                sl = gt * sg
                dsl = sg * (1.0 + gt * (1.0 - sg))
                du = dy * sp * sl
                dsp = dy * u * sl
                dgt = dy * u * sp * dsl
                dspb = dsp.astype(BF16)
                dvn = jnp.where(lo, _dot_tn(wm[2 * p], dspb), _dot_tn(wm[2 * p + 1], dspb))
                gw_ref[2 * p] += _dot_nt(jnp.where(lo, dsp, 0.0).astype(BF16), vn)
                gw_ref[2 * p + 1] += _dot_nt(jnp.where(lo, 0.0, dsp).astype(BF16), vn)
                dsp_lo = (dsp - dspb.astype(F32)).astype(BF16)
                gb_ref[...] += (_dot_nt(head_rows[p], dspb) + _dot_nt(head_rows[p], dsp_lo))[0:8]
                gvp = jnp.sum(dvn * z, axis=0, keepdims=True)
                gv_ref[2 * p:2 * p + 1, :] += gvp
                gv_ref[2 * p + 1:2 * p + 2, :] += pltpu.roll(gvp, HEAD_DIM, 1)
                dz = dvn * g
                dv = r * (dz - z * (_headsum(dz * z, bd) * (1.0 / HEAD_DIM)))
                dg_ref[rows, C_GU + p * LANES:C_GU + (p + 1) * LANES] = du.astype(BF16)
                dg_ref[rows, C_GV + p * LANES:C_GV + (p + 1) * LANES] = dv.astype(BF16)
                dg_ref[rows, C_GG + p * LANES:C_GG + (p + 1) * LANES] = dgt.astype(BF16)

        @pl.when(i == nsteps - 1)
        def _():
            for h in range(4):
                gw_ref[h] = jnp.where(tril, gw_ref[h], 0.0)

    return _call(
        body, name="gmlp_bwd", grid=(nsteps,),
        in_specs=[pl.BlockSpec((tm, 3 * GMLP_W), lambda i: (i, 0)),
                  pl.BlockSpec((tm, GMLP_W), lambda i: (i, 0)),
                  pl.BlockSpec((1, GMLP_W), lambda i: (0, 0)),
                  pl.BlockSpec((4, CHUNK, CHUNK), lambda i: (0, 0, 0)),
                  pl.BlockSpec((CHUNK, GMLP_W), lambda i: (0, 0))],
        out_specs=[pl.BlockSpec((tm, 3 * GMLP_W), lambda i: (i, 0)),
                   pl.BlockSpec((4, CHUNK, CHUNK), lambda i: (0, 0, 0)),
                   pl.BlockSpec((8, LANES), lambda i: (0, 0)),
                   pl.BlockSpec((8, LANES), lambda i: (0, 0))],
        out_shape=[jax.ShapeDtypeStruct((SEQ, 3 * GMLP_W), BF16),
                   jax.ShapeDtypeStruct((4, CHUNK, CHUNK), F32),
                   jax.ShapeDtypeStruct((8, LANES), F32),
                   jax.ShapeDtypeStruct((8, LANES), F32)],
        compiler_params=_params(),
    )(proj, dyc, vgain, w_s, bias_full)


def _band_masks():
    qi = lax.broadcasted_iota(jnp.int32, (CHUNK, 2 * CHUNK), 0)
    kj = lax.broadcasted_iota(jnp.int32, (CHUNK, 2 * CHUNK), 1)
    valid2 = ((kj < CHUNK) & (kj >= qi)) | ((kj >= CHUNK) & (kj - CHUNK <= qi))
    q1 = lax.broadcasted_iota(jnp.int32, (CHUNK, CHUNK), 0)
    k1 = lax.broadcasted_iota(jnp.int32, (CHUNK, CHUNK), 1)
    return k1 <= q1, valid2


def _stack_heads(v, lo):
    return jnp.concatenate([jnp.where(lo, v, 0.0), jnp.where(lo, 0.0, v)], axis=0).astype(BF16)


def _rows_of(ref, start, d):
    if d == 1:
        return ref.at[pl.ds(start if isinstance(start, int) else pl.multiple_of(start, CHUNK), CHUNK), :]
    return ref.at[pl.ds(start, CHUNK, stride=d), :]


def _unrolled(lo, hi, unroll, run):
    groups = (hi - lo) // unroll
    if groups:
        def body(g, carry):
            run([lo + g * unroll + t for t in range(unroll)])
            return carry

        lax.fori_loop(0, groups, body, 0)
    if lo + groups * unroll < hi:
        run(range(lo + groups * unroll, hi))


def _for_blocks(d, group_fn, unroll):
    nblk = SEQ // CHUNK
    sh = d.bit_length() - 1

    def first(j):
        return (j * CHUNK if d == 1 else j, None)

    def rest(j):
        start = (j & (d - 1)) + (j >> sh) * (CHUNK * d)
        return (start, start - CHUNK * d)

    _unrolled(0, d, unroll, lambda js: group_fn(d, [first(j) for j in js]))
    _unrolled(d, nblk, unroll, lambda js: group_fn(d, [rest(j) for j in js]))


def _attn_fwd(proj, gq2, gk2):
    tn = 512

    def body(q_ref, k_ref, v_ref, g_ref, gq_ref, gk_ref, o_ref, l_ref, ya_ref, qn_ref, kn_ref):
        bd = _head_blockdiag()
        lo = _lo_mask(CHUNK)
        valid1, valid2 = _band_masks()

        def norm(t, carry):
            rows = pl.ds(pl.multiple_of(t * tn, tn), tn)
            q = q_ref[rows, :]
            qn_ref[rows, :] = q * lax.rsqrt(_headsum(q * q, bd) * (1.0 / HEAD_DIM) + EPS) * (gq_ref[...] * QK_SCALE)
            k = k_ref[rows, :]
            kn_ref[rows, :] = k * lax.rsqrt(_headsum(k * k, bd) * (1.0 / HEAD_DIM) + EPS) * gk_ref[...]
            return carry

        lax.fori_loop(0, SEQ // tn, norm, 0)

        def load_kv(ref, d, start, prev):
            own = _rows_of(ref, start, d)[...]
            if prev is None:
                return own.astype(BF16)
            return jnp.concatenate([_rows_of(ref, prev, d)[...], own], axis=0).astype(BF16)

        def group(d, blocks):
            valid = valid1 if blocks[0][1] is None else valid2
            valid = jnp.concatenate([valid, valid], axis=0)
            qs = [_rows_of(qn_ref, start, d)[...] for start, _ in blocks]
            ks = [load_kv(kn_ref, d, start, prev) for start, prev in blocks]
            vs = [load_kv(v_ref, d, start, prev) for start, prev in blocks]
            ss = [_dot_nt(_stack_heads(q, lo), k) for q, k in zip(qs, ks)]
            ms, ps, ls = [], [], []
            for s in ss:
                s = jnp.where(valid, s, -jnp.inf)
                m = jnp.max(s, axis=-1, keepdims=True)
                p = jnp.exp(s - m)
                ms.append(m)
                ls.append(jnp.sum(p, axis=-1, keepdims=True))
                ps.append(p.astype(BF16))
            os_ = [_dot(p, v) for p, v in zip(ps, vs)]
            for b, (start, _) in enumerate(blocks):
                on = os_[b] * (1.0 / ls[b])
                ln = ms[b] + jnp.log(ls[b])
                ob = jnp.where(lo, on[:CHUNK], on[CHUNK:])
                lb = jnp.where(lo, ln[:CHUNK], ln[CHUNK:])
                o_rows = _rows_of(o_ref, start, d)
                l_rows = _rows_of(l_ref, start, d)
                if d != DILATIONS[0]:
                    lold = l_rows[...]
                    mx = jnp.maximum(lold, lb)
                    ea = jnp.exp(lold - mx)
                    eb = jnp.exp(lb - mx)
                    inv = 1.0 / (ea + eb)
                    ob = o_rows[...] * (ea * inv) + ob * (eb * inv)
                    lb = mx + jnp.log(ea + eb)
                o_rows[...] = ob
                l_rows[...] = lb

        for d in DILATIONS:
            _for_blocks(d, group, ATTN_UNROLL)

        def fin(t, carry):
            rows = pl.ds(pl.multiple_of(t * tn, tn), tn)
            g = g_ref[rows, :]
            ya_ref[rows, :] = (o_ref[rows, :] * (g * _sigmoid(g))).astype(BF16)
            return carry

        lax.fori_loop(0, SEQ // tn, fin, 0)

    col = lambda c0: pl.BlockSpec((SEQ, LANES), lambda p: (0, c0 // LANES + p))
    vec = pl.BlockSpec((1, LANES), lambda p: (0, 0))
    out = pl.BlockSpec((SEQ, LANES), lambda p: (0, p))
    return _call(
        body, name="attn_fwd", grid=(ATTN_W // LANES,),
        in_specs=[col(C_AQ), col(C_AK), col(C_AV), col(C_AG), vec, vec],
        out_specs=[out, out, out],
        out_shape=[jax.ShapeDtypeStruct((SEQ, ATTN_W), F32), jax.ShapeDtypeStruct((SEQ, ATTN_W), F32),
                   jax.ShapeDtypeStruct((SEQ, ATTN_W), BF16)],
        scratch_shapes=[pltpu.VMEM((SEQ, LANES), F32), pltpu.VMEM((SEQ, LANES), F32)],
        compiler_params=_params(),
    )(proj, proj, proj, proj, gq2, gk2)


def _attn_bwd(proj, o, lse, dyc, gq2, gk2):
    tn = 512
    npairs = ATTN_W // LANES

    def body(proj_hbm, o_hbm, l_hbm, dyc_hbm, gq_ref, gk_ref,
             dq_ref, dk_ref, dv_ref, dgt_ref, gqg_ref, gkg_ref,
             qb_, kb_, vb_, gb_, ob_, lb_, yb_, dkb_, dvb_, sems):
        pair = pl.program_id(0)
        bd = _head_blockdiag()
        lo = _lo_mask(CHUNK)
        lo2 = lax.broadcasted_iota(jnp.int32, (2 * CHUNK, LANES), 1) < HEAD_DIM
        valid1, valid2 = _band_masks()
        gqs = gq_ref[...] * QK_SCALE
        gk = gk_ref[...]

        def pcol(c0):
            return proj_hbm.at[:, pl.ds(pl.multiple_of(c0 + pair * LANES, LANES), LANES)]

        def acol(hbm, c0=0):
            return hbm.at[:, pl.ds(pl.multiple_of(c0 + pair * LANES, LANES), LANES)]

        loads = [pltpu.make_async_copy(src, dst, sems.at[n]) for n, (src, dst) in enumerate((
            (pcol(C_AQ), qb_), (pcol(C_AK), kb_), (pcol(C_AG), gb_), (acol(o_hbm), ob_),
            (acol(dyc_hbm, GMLP_W), yb_), (pcol(C_AV), vb_), (acol(l_hbm), lb_)))]
        for cp in loads:
            cp.start()

        @pl.when(pair == 0)
        def _():
            gqg_ref[...] = jnp.zeros_like(gqg_ref)
            gkg_ref[...] = jnp.zeros_like(gkg_ref)

        def pre_qk(t, carry):
            rows = pl.ds(pl.multiple_of(t * tn, tn), tn)
            zero = jnp.zeros((tn, LANES), F32)
            dkb_[rows, :] = zero
            dvb_[rows, :] = zero
            q = qb_[rows, :]
            qb_[rows, :] = q * lax.rsqrt(_headsum(q * q, bd) * (1.0 / HEAD_DIM) + EPS) * gqs
            k = kb_[rows, :]
            kb_[rows, :] = k * lax.rsqrt(_headsum(k * k, bd) * (1.0 / HEAD_DIM) + EPS) * gk
            return carry

        def pre_gate(t, carry):
            rows = pl.ds(pl.multiple_of(t * tn, tn), tn)
            g = gb_[rows, :]
            ov = ob_[rows, :]
            dya = yb_[rows, :]
            sg = _sigmoid(g)
            dgt_ref[rows, :] = (dya * ov * (sg * (1.0 + g * (1.0 - sg)))).astype(BF16)
            do = dya * (g * sg)
            yb_[rows, :] = do
            ob_[rows, :] = _headsum(do * ov, bd)
            gb_[rows, :] = jnp.zeros((tn, LANES), F32)
            return carry

        loads[0].wait()
        loads[1].wait()
        lax.fori_loop(0, SEQ // tn, pre_qk, 0)
        for cp in loads[2:5]:
            cp.wait()
        lax.fori_loop(0, SEQ // tn, pre_gate, 0)
        loads[5].wait()
        loads[6].wait()

        def load_kv(ref, d, start, prev):
            own = _rows_of(ref, start, d)[...]
            if prev is None:
                return own.astype(BF16)
            return jnp.concatenate([_rows_of(ref, prev, d)[...], own], axis=0).astype(BF16)

        def group(d, blocks):
            first = blocks[0][1] is None
            valid, lok = (valid1, lo) if first else (valid2, lo2)
            chains = [(b, h) for b in range(len(blocks)) for h in range(2)]
            mask = lambda h: lo if h == 0 else ~lo
            qs = [_rows_of(qb_, start, d)[...] for start, _ in blocks]
            dos = [_rows_of(yb_, start, d)[...] for start, _ in blocks]
            lvs = [_rows_of(lb_, start, d)[...] for start, _ in blocks]
            dls = [_rows_of(ob_, start, d)[...] for start, _ in blocks]
            ks = [load_kv(kb_, d, start, prev) for start, prev in blocks]
            vs = [load_kv(vb_, d, start, prev) for start, prev in blocks]
            qbs = [q.astype(BF16) for q in qs]
            dobs = [do.astype(BF16) for do in dos]
            ss = [_dot_nt(jnp.where(mask(h), qs[b], 0.0).astype(BF16), ks[b]) for b, h in chains]
            dps = [_dot_nt(jnp.where(mask(h), dos[b], 0.0).astype(BF16), vs[b]) for b, h in chains]
            pbs, dss = [], []
            for s, dp, (b, h) in zip(ss, dps, chains):
                hc = h * HEAD_DIM
                p = jnp.exp(jnp.where(valid, s, -jnp.inf) - lvs[b][:, hc:hc + 1])
                pbs.append(p.astype(BF16))
                dss.append((p * (dp - dls[b][:, hc:hc + 1])).astype(BF16))
            dqs = [_dot(ds, ks[b]) for ds, (b, h) in zip(dss, chains)]
            dks = [_dot_tn(ds, qbs[b]) for ds, (b, h) in zip(dss, chains)]
            dvs = [_dot_tn(p, dobs[b]) for p, (b, h) in zip(pbs, chains)]
            for b, (start, prev) in enumerate(blocks):
                c0, c1 = 2 * b, 2 * b + 1
                dq_rows = _rows_of(gb_, start, d)
                dq_rows[...] = dq_rows[...] + jnp.where(lo, dqs[c0], dqs[c1])
                dkc = jnp.where(lok, dks[c0], dks[c1])
                dvc = jnp.where(lok, dvs[c0], dvs[c1])
                spans = ((start, slice(0, CHUNK)),) if first else ((prev, slice(0, CHUNK)), (start, slice(CHUNK, 2 * CHUNK)))
                for st, sl in spans:
                    dk_rows = _rows_of(dkb_, st, d)
                    dk_rows[...] = dk_rows[...] + dkc[sl]
                    dv_rows = _rows_of(dvb_, st, d)
                    dv_rows[...] = dv_rows[...] + dvc[sl]

        for d in DILATIONS:
            _for_blocks(d, group, ATTN_UNROLL)

        pltpu.sync_copy(pcol(C_AQ), qb_)
        pltpu.sync_copy(pcol(C_AK), kb_)

        def post(t, carry):
            gq_acc, gk_acc = carry
            rows = pl.ds(pl.multiple_of(t * tn, tn), tn)
            outs = []
            for raw_, acc_, gain in ((qb_, gb_, gqs), (kb_, dkb_, gk)):
                a = raw_[rows, :]
                r = lax.rsqrt(_headsum(a * a, bd) * (1.0 / HEAD_DIM) + EPS)
                z = a * r
                dn = acc_[rows, :]
                dz = dn * gain
                outs.append((r * (dz - z * (_headsum(dz * z, bd) * (1.0 / HEAD_DIM))), jnp.sum(dn * z, axis=0, keepdims=True)))
            dq_ref[rows, :] = outs[0][0].astype(BF16)
            dk_ref[rows, :] = outs[1][0].astype(BF16)
            dv_ref[rows, :] = dvb_[rows, :].astype(BF16)
            return gq_acc + outs[0][1] * QK_SCALE, gk_acc + outs[1][1]

        zero = jnp.zeros((1, LANES), F32)
        gq_acc, gk_acc = lax.fori_loop(0, SEQ // tn, post, (zero, zero))
        gqg_ref[0:1, :] += gq_acc
        gkg_ref[0:1, :] += gk_acc

        @pl.when(pair == npairs - 1)
        def _():
            gqg_ref[0:1, :] = _fold_heads(gqg_ref[0:1, :])
            gkg_ref[0:1, :] = _fold_heads(gkg_ref[0:1, :])

    hbm = pl.BlockSpec(memory_space=pl.ANY)
    vec = pl.BlockSpec((1, LANES), lambda p: (0, 0))
    blk8 = pl.BlockSpec((8, LANES), lambda p: (0, 0))
    out = pl.BlockSpec((SEQ, LANES), lambda p: (0, p))
    big = jax.ShapeDtypeStruct((SEQ, ATTN_W), BF16)
    return _call(
        body, name="attn_bwd", grid=(npairs,),
        in_specs=[hbm, hbm, hbm, hbm, vec, vec],
        out_specs=[out, out, out, out, blk8, blk8],
        out_shape=[big, big, big, big, jax.ShapeDtypeStruct((8, LANES), F32), jax.ShapeDtypeStruct((8, LANES), F32)],
        scratch_shapes=[pltpu.VMEM((SEQ, LANES), F32) for _ in range(9)] + [pltpu.SemaphoreType.DMA((7,))],
        compiler_params=_params(),
    )(proj, o, lse, dyc, gq2, gk2)


def _mem_kv(mem, gain, wkv):
    def body(m_ref, g_ref, w_ref, kv_ref, hm_ref):
        mv = m_ref[...]
        ms = jnp.mean(mv * mv, axis=-1, keepdims=True)
        hm = (mv * lax.rsqrt(ms + EPS) * g_ref[...]).astype(BF16)
        hm_ref[...] = hm
        kv_ref[...] = _dot(hm, w_ref[...])

    return _call(
        body, name="mem_kv",
        out_shape=[jax.ShapeDtypeStruct((MEM_LEN, 2 * MEM_W), F32), jax.ShapeDtypeStruct((MEM_LEN, D_MODEL), BF16)],
        compiler_params=_params(),
    )(mem, gain, wkv)


def _mem_keys(kv_ref, kg_ref, bd, p):
    mk = kv_ref[:, p * LANES:(p + 1) * LANES]
    r = lax.rsqrt(_headsum(mk * mk, bd) * (1.0 / HEAD_DIM) + EPS)
    z = mk * r
    mkn = (z * kg_ref[:, p * LANES:(p + 1) * LANES]).astype(BF16)
    mvp = kv_ref[:, MEM_W + p * LANES:MEM_W + (p + 1) * LANES].astype(BF16)
    return mkn, mvp, r, z


def _mem_fwd(proj, kv, qg4, kg4):
    tm = 512

    def body(q_ref, g_ref, kv_ref, qg_ref, kg_ref, om_ref, ym_ref):
        bd = _head_blockdiag()
        lo = _lo_mask(tm)
        for p in range(2):
            cs = slice(p * LANES, (p + 1) * LANES)
            mkn, mvp, _, _ = _mem_keys(kv_ref, kg_ref, bd, p)
            q = q_ref[:, cs]
            qn = q * lax.rsqrt(_headsum(q * q, bd) * (1.0 / HEAD_DIM) + EPS) * (qg_ref[:, cs] * QK_SCALE)
            res = []
            for h in range(2):
                qh = jnp.where(lo if h == 0 else ~lo, qn, 0.0).astype(BF16)
                s = _dot_nt(qh, mkn)
                e = jnp.exp(s - jnp.max(s, axis=-1, keepdims=True))
                res.append(_dot(e.astype(BF16), mvp) * (1.0 / jnp.sum(e, axis=-1, keepdims=True)))
            ov = jnp.where(lo, res[0], res[1])
            g = g_ref[:, cs]
            om_ref[:, cs] = ov
            ym_ref[:, cs] = (ov * (g * _sigmoid(g))).astype(BF16)

    vec = pl.BlockSpec((1, MEM_W), lambda i: (0, 0))
    return _call(
        body, name="mem_fwd", grid=(SEQ // tm,),
        in_specs=[pl.BlockSpec((tm, MEM_W), lambda i: (i, C_MQ // MEM_W)),
                  pl.BlockSpec((tm, MEM_W), lambda i: (i, C_MG // MEM_W)),
                  pl.BlockSpec((MEM_LEN, 2 * MEM_W), lambda i: (0, 0)), vec, vec],
        out_specs=[pl.BlockSpec((tm, MEM_W), lambda i: (i, 0)), pl.BlockSpec((tm, MEM_W), lambda i: (i, 0))],
        out_shape=[jax.ShapeDtypeStruct((SEQ, MEM_W), F32), jax.ShapeDtypeStruct((SEQ, MEM_W), BF16)],
        compiler_params=_params(),
    )(proj, proj, kv, qg4, kg4)


def _mem_bwd(proj, om, dyc, kv, hm, mem, mgain, wkv, qg4, kg4):
    tm = 512
    nsteps = SEQ // tm

    def body(q_ref, g_ref, om_ref, dy_ref, kv_ref, hm_ref, mem_ref, mg_ref, w_ref, qg_ref, kg_ref,
             dq_ref, dgt_ref, gqg_ref, gkg_ref, gw_ref, gmg_ref, dmk_ref, dmv_ref, gq_acc):
        i = pl.program_id(0)
        bd = _head_blockdiag()
        lo = _lo_mask(tm)
        lom = _lo_mask(MEM_LEN)

        @pl.when(i == 0)
        def _():
            dmk_ref[...] = jnp.zeros_like(dmk_ref)
            dmv_ref[...] = jnp.zeros_like(dmv_ref)
            gq_acc[...] = jnp.zeros_like(gq_acc)

        for p in range(2):
            cs = slice(p * LANES, (p + 1) * LANES)
            mkn, mvp, _, _ = _mem_keys(kv_ref, kg_ref, bd, p)
            gqs = qg_ref[:, cs] * QK_SCALE
            q = q_ref[:, cs]
            r = lax.rsqrt(_headsum(q * q, bd) * (1.0 / HEAD_DIM) + EPS)
            z = q * r
            qn = z * gqs
            qnb = qn.astype(BF16)
            g = g_ref[:, cs]
            ov = om_ref[:, cs]
            dym = dy_ref[:, cs]
            sg = _sigmoid(g)
            dgt_ref[:, cs] = (dym * ov * (sg * (1.0 + g * (1.0 - sg)))).astype(BF16)
            do = dym * (g * sg)
            dob = do.astype(BF16)
            delta = _headsum(do * ov, bd)
            parts = []
            for h in range(2):
                mh = lo if h == 0 else ~lo
                hc = h * HEAD_DIM
                qh = jnp.where(mh, qn, 0.0).astype(BF16)
                doh = jnp.where(mh, do, 0.0).astype(BF16)
                s = _dot_nt(qh, mkn)
                e = jnp.exp(s - jnp.max(s, axis=-1, keepdims=True))
                pr = e * (1.0 / jnp.sum(e, axis=-1, keepdims=True))
                dp = _dot_nt(doh, mvp)
                ds = (pr * (dp - delta[:, hc:hc + 1])).astype(BF16)
                parts.append((_dot(ds, mkn), _dot_tn(ds, qnb), _dot_tn(pr.astype(BF16), dob)))
            dqn = jnp.where(lo, parts[0][0], parts[1][0])
            dmk_ref[:, cs] += jnp.where(lom, parts[0][1], parts[1][1])
            dmv_ref[:, cs] += jnp.where(lom, parts[0][2], parts[1][2])
            dz = dqn * gqs
            dq_ref[:, cs] = (r * (dz - z * (_headsum(dz * z, bd) * (1.0 / HEAD_DIM)))).astype(BF16)
            gq_acc[:, cs] += jnp.sum(dqn * z, axis=0, keepdims=True) * QK_SCALE

        @pl.when(i == nsteps - 1)
        def _():
            gqg_ref[...] = jnp.zeros_like(gqg_ref)
            gkg_ref[...] = jnp.zeros_like(gkg_ref)
            gqg_ref[0:1, :] = _fold_heads(gq_acc[:, 0:LANES] + gq_acc[:, LANES:2 * LANES])
            dkv = []
            gk = jnp.zeros((1, LANES), F32)
            for p in range(2):
                cs = slice(p * LANES, (p + 1) * LANES)
                _, _, r, z = _mem_keys(kv_ref, kg_ref, bd, p)
                dn = dmk_ref[:, cs]
                dz = dn * kg_ref[:, cs]
                gk = gk + jnp.sum(dn * z, axis=0, keepdims=True)
                dkv.append(r * (dz - z * (_headsum(dz * z, bd) * (1.0 / HEAD_DIM))))
            gkg_ref[0:1, :] = _fold_heads(gk)
            dkvb = jnp.concatenate(dkv + [dmv_ref[...]], axis=1).astype(BF16)
            gw_ref[...] = _dot_tn(hm_ref[...], dkvb)
            dhm = _dot_nt(dkvb, w_ref[...])
            mv = mem_ref[...]
            zm = mv * lax.rsqrt(jnp.mean(mv * mv, axis=-1, keepdims=True) + EPS)
            _put_rows(gmg_ref, jnp.sum(dhm * zm, axis=0, keepdims=True))

    const = lambda shape: pl.BlockSpec(shape, lambda i: (0,) * len(shape))
    row = lambda j: pl.BlockSpec((tm, MEM_W), lambda i: (i, j))
    blk8 = jax.ShapeDtypeStruct((8, LANES), F32)
    return _call(
        body, name="mem_bwd", grid=(nsteps,),
        in_specs=[row(C_MQ // MEM_W), row(C_MG // MEM_W), row(0), row((GMLP_W + ATTN_W) // MEM_W),
                  const((MEM_LEN, 2 * MEM_W)), const((MEM_LEN, D_MODEL)), const((MEM_LEN, D_MODEL)),
                  const((1, D_MODEL)), const((D_MODEL, 2 * MEM_W)), const((1, MEM_W)), const((1, MEM_W))],
        out_specs=[row(0), row(0), const((8, LANES)), const((8, LANES)),
                   const((D_MODEL, 2 * MEM_W)), const((8, LANES))],
        out_shape=[jax.ShapeDtypeStruct((SEQ, MEM_W), BF16), jax.ShapeDtypeStruct((SEQ, MEM_W), BF16),
                   blk8, blk8, jax.ShapeDtypeStruct((D_MODEL, 2 * MEM_W), F32), blk8],
        scratch_shapes=[pltpu.VMEM((MEM_LEN, MEM_W), F32), pltpu.VMEM((MEM_LEN, MEM_W), F32),
                        pltpu.VMEM((1, MEM_W), F32)],
        compiler_params=_params(),
    )(proj, proj, om, dyc, kv, hm, mem, mgain, wkv, qg4, kg4)


def _out_loss(yg, ya, ym, x, tgt, wo):
    tm = 512
    nsteps = SEQ // tm
    parts = ((0, GMLP_W), (GMLP_W, ATTN_W), (GMLP_W + ATTN_W, MEM_W))

    def body(yg_ref, ya_ref, ym_ref, x_ref, t_ref, w_ref, dy_ref, dyc_ref, gw_ref, ls_ref):
        i = pl.program_id(0)

        @pl.when(i == 0)
        def _():
            gw_ref[...] = jnp.zeros_like(gw_ref)
            ls_ref[...] = jnp.zeros_like(ls_ref)

        ys = (yg_ref[...], ya_ref[...], ym_ref[...])
        y = sum(_dot(yv, w_ref[r0:r0 + n, :]) for yv, (r0, n) in zip(ys, parts))
        err = x_ref[...] + y - t_ref[...]
        _put_rows(ls_ref, jnp.sum(err * err, axis=0, keepdims=True), accumulate=True)
        dy = err * (1.0 / D_MODEL)
        dy_ref[...] = dy
        dyb = dy.astype(BF16)
        dyc_ref[...] = _dot_nt(dyb, w_ref[...])
        for yv, (r0, n) in zip(ys, parts):
            gw_ref[r0:r0 + n, :] += _dot_tn(yv, dyb)

    row = lambda w: pl.BlockSpec((tm, w), lambda i: (i, 0))
    const = lambda shape: pl.BlockSpec(shape, lambda i: (0, 0))
    return _call(
        body, name="out_loss", grid=(nsteps,),
        in_specs=[row(GMLP_W), row(ATTN_W), row(MEM_W), row(D_MODEL), row(D_MODEL), const((D_MODEL, D_MODEL))],
        out_specs=[row(D_MODEL), row(D_MODEL), const((D_MODEL, D_MODEL)), const((8, LANES))],
        out_shape=[jax.ShapeDtypeStruct((SEQ, D_MODEL), F32), jax.ShapeDtypeStruct((SEQ, D_MODEL), F32),
                   jax.ShapeDtypeStruct((D_MODEL, D_MODEL), F32), jax.ShapeDtypeStruct((8, LANES), F32)],
        compiler_params=_params(),
    )(yg, ya, ym, x, tgt, wo)


def _proj_bwd(x, dy, gain, wt, dg, daq, dak, dav, dag, dmq, dmg):
    tm = 512
    nsteps = SEQ // tm
    pieces = ((C_GU, 3 * GMLP_W), (C_AQ, ATTN_W), (C_AK, ATTN_W), (C_AV, ATTN_W), (C_AG, ATTN_W),
              (C_MQ, MEM_W), (C_MG, MEM_W))

    def body(x_ref, dy_ref, g_ref, wt_hbm, p0, p1, p2, p3, p4, p5, p6, gx_ref, gwt_hbm, gg_ref, wt_v, acc):
        i = pl.program_id(0)

        @pl.when(i == 0)
        def _():
            pltpu.sync_copy(wt_hbm, wt_v)
            acc[...] = jnp.zeros_like(acc)
            gg_ref[...] = jnp.zeros_like(gg_ref)

        xv = x_ref[...]
        r = lax.rsqrt(jnp.mean(xv * xv, axis=-1, keepdims=True) + EPS)
        z = xv * r
        g = g_ref[...]
        h = (z * g).astype(BF16)
        dh = jnp.zeros((tm, D_MODEL), F32)
        for pref, (c0, w) in zip((p0, p1, p2, p3, p4, p5, p6), pieces):
            dp = pref[...]
            dh = dh + _dot(dp, wt_v[c0:c0 + w, :])
            acc[c0:c0 + w, :] += _dot_tn(dp, h)
        _put_rows(gg_ref, jnp.sum(dh * z, axis=0, keepdims=True), accumulate=True)
        dz = dh * g
        gx_ref[...] = dy_ref[...] + r * (dz - z * jnp.mean(dz * z, axis=-1, keepdims=True))

        @pl.when(i == nsteps - 1)
        def _():
            pltpu.sync_copy(acc, gwt_hbm)

    row = lambda w: pl.BlockSpec((tm, w), lambda i: (i, 0))
    hbm = pl.BlockSpec(memory_space=pl.ANY)
    vec = pl.BlockSpec((1, D_MODEL), lambda i: (0, 0))
    return _call(
        body, name="proj_bwd", grid=(nsteps,),
        in_specs=[row(D_MODEL), row(D_MODEL), vec, hbm] + [row(w) for _, w in pieces],
        out_specs=[row(D_MODEL), hbm, pl.BlockSpec((8, LANES), lambda i: (0, 0))],
        out_shape=[jax.ShapeDtypeStruct((SEQ, D_MODEL), F32), jax.ShapeDtypeStruct((IN_W, D_MODEL), F32),
                   jax.ShapeDtypeStruct((8, LANES), F32)],
        scratch_shapes=[pltpu.VMEM((IN_W, D_MODEL), BF16), pltpu.VMEM((IN_W, D_MODEL), F32)],
        compiler_params=_params(),
    )(x, dy, gain, wt, dg, daq, dak, dav, dag, dmq, dmg)


def _gather_weights(wt_sh, wkv_sh, wo_sh):
    shards = (wt_sh, wkv_sh, wo_sh)
    nrows = tuple(a.shape[0] for a in shards)

    def body(a0, a1, a2, o0, o1, o2, send_sems, recv_sems):
        x, y, c = lax.axis_index("x"), lax.axis_index("y"), lax.axis_index("c")
        sib, xn, yn = (x, y, 1 - c), (1 - x, y, c), (x, 1 - y, c)
        me, cx, cy, cd = 2 * x + y, 2 * (1 - x) + y, 2 * x + (1 - y), 2 * (1 - x) + (1 - y)
        ins, outs = (a0, a1, a2), (o0, o1, o2)

        def part(a, chip, hf, quarter=None):
            n = nrows[a] // 2
            base = chip * nrows[a] + hf * n
            if quarter is not None:
                n = n // 2
                base = base + quarter * n
            return outs[a].at[pl.ds(pl.multiple_of(base, 16), n), :]

        def copy(k, ref, to):
            return pltpu.make_async_remote_copy(src_ref=ref, dst_ref=ref, send_sem=send_sems.at[k],
                                                recv_sem=recv_sems.at[k], device_id=to, device_id_type=MESH)

        started = []

        def go(cp):
            cp.start()
            started.append(cp)

        for a in range(3):
            outs[a][pl.ds(pl.multiple_of(me * nrows[a], 16), nrows[a]), :] = ins[a][...].astype(BF16)
        for a in range(3):
            go(copy(8 * a, part(a, me, c), xn))
            go(copy(8 * a + 1, part(a, me, c), yn))
        for a in range(3):
            k = 8 * a
            copy(k, part(a, cx, c), xn).wait_recv()
            go(copy(k + 4, part(a, cx, c, 1), yn))
            go(copy(k + 2, part(a, cx, c), sib))
            copy(k + 1, part(a, cy, c), yn).wait_recv()
            go(copy(k + 5, part(a, cy, c, 0), xn))
            go(copy(k + 3, part(a, cy, c), sib))
        for a in range(3):
            k = 8 * a
            copy(k + 4, part(a, cd, c, 1), yn).wait_recv()
            go(copy(k + 7, part(a, cd, c, 1), sib))
            copy(k + 5, part(a, cd, c, 0), xn).wait_recv()
            go(copy(k + 6, part(a, cd, c, 0), sib))
        for a in range(3):
            k = 8 * a
            copy(k + 2, part(a, cx, 1 - c), sib).wait_recv()
            copy(k + 3, part(a, cy, 1 - c), sib).wait_recv()
            copy(k + 6, part(a, cd, 1 - c, 0), sib).wait_recv()
            copy(k + 7, part(a, cd, 1 - c, 1), sib).wait_recv()
        for cp in started:
            cp.wait_send()

    return _call(
        body, name="gather_weights",
        out_shape=[jax.ShapeDtypeStruct((4 * a.shape[0], a.shape[1]), BF16) for a in shards],
        in_specs=[pl.BlockSpec(memory_space=pltpu.VMEM)] * 3,
        out_specs=[pl.BlockSpec(memory_space=pltpu.VMEM)] * 3,
        scratch_shapes=[pltpu.SemaphoreType.DMA((24,)), pltpu.SemaphoreType.DMA((24,))],
        compiler_params=_params(),
    )(*shards)


def _reduce_grads(gwt, gwkv, gwo, g_ws, tiny):
    bigs = (gwt, gwkv, gwo)
    piece = tuple(a.shape[0] // 8 for a in bigs)
    width = tuple(a.shape[1] for a in bigs)
    views = tuple(a.reshape(2, 2, 2, r, w) for a, r, w in zip(bigs, piece, width))

    def body(g0, g1, g2, ws_in, tiny_in, o0, o1, o2, o_ws, o_tiny, *rest):
        loc, ra, s_b, r_b, acc1, s_c, r_c = (rest[3 * i:3 * i + 3] for i in range(7))
        sm, sa, sb, sc, acc_s, send_sems, recv_sems, local_sems = rest[21:]
        n_ws = ws_in.shape[0]
        sm[0:n_ws, :] = ws_in[...]
        sm[n_ws:, :] = tiny_in[...]
        x, y, c = lax.axis_index("x"), lax.axis_index("y"), lax.axis_index("c")
        sib, xn, yn = (x, y, 1 - c), (1 - x, y, c), (x, 1 - y, c)
        gs, outs = (g0, g1, g2), (o0, o1, o2)

        def copy(k, src, dst, to):
            return pltpu.make_async_remote_copy(src_ref=src, dst_ref=dst, send_sem=send_sems.at[k],
                                                recv_sem=recv_sems.at[k], device_id=to, device_id_type=MESH)

        started = []

        def go(cp):
            cp.start()
            started.append(cp)

        mine = []
        for a in range(3):
            go(copy(a, gs[a].at[:, :, 1 - c], ra[a], sib))
            cp = pltpu.make_async_copy(gs[a].at[:, :, c], loc[a], local_sems.at[a])
            cp.start()
            mine.append(cp)
        go(copy(3, sm, sa, sib))
        for a in range(3):
            hw = width[a] // 2
            mine[a].wait()
            copy(a, gs[a].at[:, :, 1 - c], ra[a], sib).wait_recv()
            ra[a][...] = loc[a][...] + ra[a][...]
            s_b[a][0] = ra[a][1 - x, :, :, :hw].astype(BF16)
            s_b[a][1] = ra[a][:, 1 - y, :, hw:].astype(BF16)
            go(copy(4 + 2 * a, s_b[a].at[0], r_b[a].at[0], xn))
            go(copy(5 + 2 * a, s_b[a].at[1], r_b[a].at[1], yn))
        copy(3, sm, sa, sib).wait_recv()
        acc_s[...] = sm[...] + sa[...]
        go(copy(10, acc_s, sb, xn))
        for a in range(3):
            hw = width[a] // 2
            copy(4 + 2 * a, s_b[a].at[0], r_b[a].at[0], xn).wait_recv()
            copy(5 + 2 * a, s_b[a].at[1], r_b[a].at[1], yn).wait_recv()
            acc1[a][0] = ra[a][x, :, :, :hw] + r_b[a][0].astype(F32)
            acc1[a][1] = ra[a][:, y, :, hw:] + r_b[a][1].astype(F32)
            s_c[a][0] = acc1[a][0, 1 - y].astype(BF16)
            s_c[a][1] = acc1[a][1, 1 - x].astype(BF16)
            go(copy(11 + 2 * a, s_c[a].at[0], r_c[a].at[0], yn))
            go(copy(12 + 2 * a, s_c[a].at[1], r_c[a].at[1], xn))
        copy(10, acc_s, sb, xn).wait_recv()
        sb[...] = acc_s[...] + sb[...]
        go(copy(17, sb, sc, yn))
        for a in range(3):
            hw = width[a] // 2
            copy(11 + 2 * a, s_c[a].at[0], r_c[a].at[0], yn).wait_recv()
            copy(12 + 2 * a, s_c[a].at[1], r_c[a].at[1], xn).wait_recv()
            my_rows = pl.ds(pl.multiple_of(c * piece[a], 8), piece[a])
            outs[a][my_rows, :hw] = acc1[a][0, y] + r_c[a][0].astype(F32)
            outs[a][my_rows, hw:] = acc1[a][1, x] + r_c[a][1].astype(F32)
            go(copy(18 + a, outs[a].at[my_rows, :], outs[a].at[my_rows, :], sib))
        copy(17, sb, sc, yn).wait_recv()
        o_ws[...] = sb[0:n_ws, :] + sc[0:n_ws, :]
        o_tiny[...] = sb[n_ws:, :] + sc[n_ws:, :]
        for a in range(3):
            theirs = outs[a].at[pl.ds(pl.multiple_of((1 - c) * piece[a], 8), piece[a]), :]
            copy(18 + a, theirs, theirs, sib).wait_recv()
        for cp in started:
            cp.wait_send()

    vm = pl.BlockSpec(memory_space=pltpu.VMEM)
    hbm = pl.BlockSpec(memory_space=pl.ANY)
    kinds = (((2, 2), 1, F32), ((2, 2), 1, F32), ((2, 2), 2, BF16), ((2, 2), 2, BF16), ((2, 2), 2, F32),
             ((2,), 2, BF16), ((2,), 2, BF16))
    scratch = [pltpu.VMEM(lead + (r, w // split), dt) for lead, split, dt in kinds for r, w in zip(piece, width)]
    small_shape = (g_ws.shape[0] + tiny.shape[0], LANES)
    scratch += [pltpu.VMEM(small_shape, F32) for _ in range(5)]
    scratch += [pltpu.SemaphoreType.DMA((21,)), pltpu.SemaphoreType.DMA((21,)), pltpu.SemaphoreType.DMA((3,))]
    return _call(
        body, name="reduce_grads",
        out_shape=[jax.ShapeDtypeStruct((2 * r, w), F32) for r, w in zip(piece, width)]
        + [jax.ShapeDtypeStruct(g_ws.shape, F32), jax.ShapeDtypeStruct(tiny.shape, F32)],
        in_specs=[hbm, hbm, hbm, vm, vm],
        out_specs=[vm, vm, vm, vm, vm],
        scratch_shapes=scratch,
        compiler_params=_params(),
    )(*views, g_ws, tiny)


def _adam_update(w, g, m, v):
    nm = ADAM_B1 * m + (1.0 - ADAM_B1) * g
    nv = ADAM_B2 * v + (1.0 - ADAM_B2) * (g * g)
    m_hat = nm / (1.0 - ADAM_B1 ** ADAM_STEP)
    v_hat = nv / (1.0 - ADAM_B2 ** ADAM_STEP)
    return -ADAM_LR * (m_hat / (jnp.sqrt(v_hat) + ADAM_EPS) + ADAM_WD * w), nm, nv


def _adamw(w, g, m, v):
    rows, cols = w.shape
    tm = max(t for t in range(8, 257, 8) if rows % t == 0)

    def body(w_ref, g_ref, m_ref, v_ref, d_ref, nm_ref, nv_ref):
        d_ref[...], nm_ref[...], nv_ref[...] = _adam_update(w_ref[...], g_ref[...], m_ref[...], v_ref[...])

    blk = pl.BlockSpec((tm, cols), lambda i: (i, 0))
    return _call(
        body, name="adamw", grid=(rows // tm,),
        in_specs=[blk] * 4, out_specs=[blk] * 3,
        out_shape=[jax.ShapeDtypeStruct((rows, cols), F32)] * 3,
        compiler_params=_params(),
    )(w, g, m, v)


def _adamw_tiny(tiny, weights, ms, vs):
    shapes = [w.shape for w in weights]
    n = len(weights)

    def grad_of(t_ref, k, shape):
        base = 8 * k
        if shape[1] > LANES:
            return [t_ref[base + j:base + j + 1, :] for j in range(shape[1] // LANES)]
        return [t_ref[base:base + shape[0], 0:shape[1]]]

    def body(t_ref, *refs):
        w_refs, m_refs, v_refs = refs[:n], refs[n:2 * n], refs[2 * n:3 * n]
        loss_ref, outs = refs[3 * n], refs[3 * n + 1:]
        loss_ref[...] = (0.5 / D_MODEL) * jnp.sum(t_ref[8 * n:8 * n + 8, :], keepdims=True)
        for k, shape in enumerate(shapes):
            g_ref, d_ref, nm_ref, nv_ref = outs[4 * k:4 * k + 4]
            for j, g in enumerate(grad_of(t_ref, k, shape)):
                cols = slice(j * LANES, (j + 1) * LANES) if shape[1] > LANES else slice(None)
                g_ref[:, cols] = g
                d_ref[:, cols], nm_ref[:, cols], nv_ref[:, cols] = _adam_update(
                    w_refs[k][:, cols], g, m_refs[k][:, cols], v_refs[k][:, cols])

    out_shape = [jax.ShapeDtypeStruct((1, 1), F32)]
    for shape in shapes:
        out_shape += [jax.ShapeDtypeStruct(shape, F32)] * 4
    return _call(body, name="adamw_tiny", out_shape=out_shape, compiler_params=_params())(tiny, *weights, *ms, *vs)


def _local_grads(x, mem, tgt, norm_gain, wt, gmlp_v_gain, gmlp_w_s, gmlp_b, attn_q_gain, attn_k_gain,
                 mem_norm_gain, wkv, mem_q_gain, mem_k_gain, wo):
    vg = gmlp_v_gain.reshape(1, GMLP_W)
    bias_full = jnp.repeat(gmlp_b.T, HEAD_DIM, axis=1)
    gq2, gk2 = jnp.tile(attn_q_gain, (1, 2)), jnp.tile(attn_k_gain, (1, 2))
    qg4, kg4 = jnp.tile(mem_q_gain, (1, 4)), jnp.tile(mem_k_gain, (1, 4))

    proj = _fwd_proj(x, norm_gain, wt)
    yg = _gmlp_fwd(proj, vg, gmlp_w_s, bias_full)
    o, lse, ya = _attn_fwd(proj, gq2, gk2)
    kv, hm = _mem_kv(mem, mem_norm_gain, wkv)
    om, ym = _mem_fwd(proj, kv, qg4, kg4)
    dy, dyc, g_wo, err2 = _out_loss(yg, ya, ym, x, tgt, wo)
    dg, g_ws, g_b, g_vg = _gmlp_bwd(proj, dyc, vg, gmlp_w_s, bias_full)
    daq, dak, dav, dag, g_aq, g_ak = _attn_bwd(proj, o, lse, dyc, gq2, gk2)
    dmq, dmg, g_mq, g_mk, g_wkv, g_mng = _mem_bwd(proj, om, dyc, kv, hm, mem, mem_norm_gain, wkv, qg4, kg4)
    gx, g_wt, g_ng = _proj_bwd(x, dy, norm_gain, wt, dg, daq, dak, dav, dag, dmq, dmg)

    tiny = jnp.concatenate([g_ng, g_vg, g_b, g_aq, g_ak, g_mng, g_mq, g_mk, err2], axis=0)
    return gx, g_wt, g_wkv, g_wo, g_ws.reshape(4 * CHUNK, CHUNK), tiny


def kernel(x, mem, norm_gain, w_in, gmlp_v_gain, gmlp_w_s, gmlp_b, attn_q_gain, attn_k_gain, mem_norm_gain, w_mem_kv, mem_q_gain, mem_k_gain, w_out, loss_target, m_norm_gain, m_w_in, m_gmlp_v_gain, m_gmlp_w_s, m_gmlp_b, m_attn_q_gain, m_attn_k_gain, m_mem_norm_gain, m_w_mem_kv, m_mem_q_gain, m_mem_k_gain, m_w_out, v_norm_gain, v_w_in, v_gmlp_v_gain, v_gmlp_w_s, v_gmlp_b, v_attn_q_gain, v_attn_k_gain, v_mem_norm_gain, v_w_mem_kv, v_mem_q_gain, v_mem_k_gain, v_w_out):
    wt, wkv, wo = _gather_weights(w_in[0].T, w_mem_kv[0], w_out[0])
    gx, g_wt, g_wkv, g_wo, g_ws, tiny = _local_grads(
        x[0], mem[0], loss_target[0], norm_gain, wt, gmlp_v_gain[0], gmlp_w_s[0], gmlp_b[0],
        attn_q_gain, attn_k_gain, mem_norm_gain, wkv, mem_q_gain, mem_k_gain, wo)
    g_wt_sh, g_wkv_sh, g_wo_sh, g_ws, tiny = _reduce_grads(g_wt, g_wkv, g_wo, g_ws, tiny)

    ws = (norm_gain, w_in, gmlp_v_gain, gmlp_w_s, gmlp_b, attn_q_gain, attn_k_gain, mem_norm_gain, w_mem_kv,
          mem_q_gain, mem_k_gain, w_out)
    ms = (m_norm_gain, m_w_in, m_gmlp_v_gain, m_gmlp_w_s, m_gmlp_b, m_attn_q_gain, m_attn_k_gain, m_mem_norm_gain,
          m_w_mem_kv, m_mem_q_gain, m_mem_k_gain, m_w_out)
    vs = (v_norm_gain, v_w_in, v_gmlp_v_gain, v_gmlp_w_s, v_gmlp_b, v_attn_q_gain, v_attn_k_gain, v_mem_norm_gain,
          v_w_mem_kv, v_mem_q_gain, v_mem_k_gain, v_w_out)
    form = {1: lambda a: a[0].T, 3: lambda a: a.reshape(4 * CHUNK, CHUNK), 2: lambda a: a[0], 4: lambda a: a[0],
            8: lambda a: a[0], 11: lambda a: a[0]}
    back = {1: lambda a: a.T[None], 3: lambda a: a.reshape(1, 4, CHUNK, CHUNK), 2: lambda a: a[None],
            4: lambda a: a[None], 8: lambda a: a[None], 11: lambda a: a[None]}
    fwd = lambda t, i: form.get(i, lambda a: a)(t[i])
    out = {}
    for i, g in ((1, g_wt_sh), (3, g_ws), (8, g_wkv_sh), (11, g_wo_sh)):
        out[i] = (g, *_adamw(fwd(ws, i), g, fwd(ms, i), fwd(vs, i)))
    res = _adamw_tiny(tiny, [fwd(ws, i) for i in TINY_ORDER], [fwd(ms, i) for i in TINY_ORDER],
                      [fwd(vs, i) for i in TINY_ORDER])
    for k, i in enumerate(TINY_ORDER):
        out[i] = res[1 + 4 * k:5 + 4 * k]
    leaves = [[back.get(i, lambda a: a)(out[i][j]) for i in range(12)] for j in range(4)]
    return (res[0].reshape(()), gx[None], *leaves[0], *leaves[1], *leaves[2], *leaves[3])
```

```python
import functools
import math

import jax
import jax.numpy as jnp
from jax import lax
from jax.experimental import pallas as pl
from jax.experimental.pallas import tpu as pltpu

F32 = jnp.float32
BF16 = jnp.bfloat16

SEQ = 4096
D_MODEL = 1024
HEAD_DIM = 64
LANES = 128
CHUNK = 128
GMLP_W, ATTN_W, MEM_W = 256, 512, 256
IN_W = 3 * GMLP_W + 4 * ATTN_W + 2 * MEM_W
MEM_LEN = 256
DILATIONS = (1, 4, 16)
EPS = 1e-6
QK_SCALE = 1.0 / math.sqrt(HEAD_DIM)
C_GU, C_GV, C_GG, C_AQ, C_AK, C_AV, C_AG, C_MQ, C_MG = 0, 256, 512, 768, 1280, 1792, 2304, 2816, 3072

ADAM_LR, ADAM_B1, ADAM_B2, ADAM_EPS, ADAM_WD, ADAM_STEP = 0.001, 0.9, 0.999, 1e-08, 0.01, 10

VMEM_LIMIT = 48 * 1024 * 1024
ATTN_UNROLL = 4
MESH = pl.DeviceIdType.MESH

TINY_ORDER = (0, 2, 4, 5, 6, 7, 9, 10)


def _call(body, **kw):
    return pl.pallas_call(body, **kw)


def _params(**kw):
    return pltpu.CompilerParams(vmem_limit_bytes=VMEM_LIMIT, **kw)


def _dot(a, b):
    return jnp.dot(a, b, preferred_element_type=F32)


def _dot_nt(a, b):
    return lax.dot_general(a, b, (((1,), (1,)), ((), ())), preferred_element_type=F32)


def _dot_tn(a, b):
    return lax.dot_general(a, b, (((0,), (0,)), ((), ())), preferred_element_type=F32)


def _head_blockdiag():
    r = lax.shift_right_logical(lax.broadcasted_iota(jnp.int32, (LANES, LANES), 0), 6)
    c = lax.shift_right_logical(lax.broadcasted_iota(jnp.int32, (LANES, LANES), 1), 6)
    return jnp.where(r == c, 1.0, 0.0).astype(BF16)


def _headsum(v, bd):
    hi = v.astype(BF16)
    lo = (v - hi.astype(F32)).astype(BF16)
    return _dot(hi, bd) + _dot(lo, bd)


def _lo_mask(rows):
    return lax.broadcasted_iota(jnp.int32, (rows, LANES), 1) < HEAD_DIM


def _sigmoid(x):
    return 1.0 / (1.0 + jnp.exp(-x))


def _fold_heads(v):
    return v + pltpu.roll(v, HEAD_DIM, 1)


def _put_rows(ref, vec, accumulate=False):
    for j in range(vec.shape[1] // LANES):
        piece = vec[:, j * LANES:(j + 1) * LANES]
        ref[j:j + 1, :] = ref[j:j + 1, :] + piece if accumulate else piece


def _fwd_proj(x, gain, wt):
    tm = 512

    def body(x_ref, g_ref, wt_ref, o_ref):
        xv = x_ref[...]
        ms = jnp.mean(xv * xv, axis=-1, keepdims=True)
        h = (xv * lax.rsqrt(ms + EPS) * g_ref[...]).astype(BF16)
        o_ref[...] = _dot_nt(h, wt_ref[...])

    return _call(
        body, name="fwd_proj", grid=(SEQ // tm,),
        in_specs=[pl.BlockSpec((tm, D_MODEL), lambda i: (i, 0)),
                  pl.BlockSpec((1, D_MODEL), lambda i: (0, 0)),
                  pl.BlockSpec((IN_W, D_MODEL), lambda i: (0, 0))],
        out_specs=pl.BlockSpec((tm, IN_W), lambda i: (i, 0)),
        out_shape=jax.ShapeDtypeStruct((SEQ, IN_W), F32),
        compiler_params=_params(),
    )(x, gain, wt)


def _gmlp_weights(w_ref):
    ti = lax.broadcasted_iota(jnp.int32, (CHUNK, CHUNK), 0)
    si = lax.broadcasted_iota(jnp.int32, (CHUNK, CHUNK), 1)
    tril = si <= ti
    return tril, [jnp.where(tril, w_ref[h], 0.0).astype(BF16) for h in range(4)]


def _gmlp_fwd(proj, vgain, w_s, bias_full):
    tm = 512

    def body(p_ref, vg_ref, w_ref, b_ref, y_ref):
        bd = _head_blockdiag()
        lo = _lo_mask(CHUNK)
        _, wm = _gmlp_weights(w_ref)
        for c in range(tm // CHUNK):
            rows = pl.ds(c * CHUNK, CHUNK)
            for p in range(2):
                cs = slice(p * LANES, (p + 1) * LANES)
                u = p_ref[rows, C_GU + p * LANES:C_GU + (p + 1) * LANES]
                v = p_ref[rows, C_GV + p * LANES:C_GV + (p + 1) * LANES]
                gt = p_ref[rows, C_GG + p * LANES:C_GG + (p + 1) * LANES]
                r = lax.rsqrt(_headsum(v * v, bd) * (1.0 / HEAD_DIM) + EPS)
                vn = (v * r * vg_ref[:, cs]).astype(BF16)
                sp = jnp.where(lo, _dot(wm[2 * p], vn), _dot(wm[2 * p + 1], vn)) + b_ref[:, cs]
                y_ref[rows, cs] = (u * sp * (gt * _sigmoid(gt))).astype(BF16)

    return _call(
        body, name="gmlp_fwd", grid=(SEQ // tm,),
        in_specs=[pl.BlockSpec((tm, 3 * GMLP_W), lambda i: (i, 0)),
                  pl.BlockSpec((1, GMLP_W), lambda i: (0, 0)),
                  pl.BlockSpec((4, CHUNK, CHUNK), lambda i: (0, 0, 0)),
                  pl.BlockSpec((CHUNK, GMLP_W), lambda i: (0, 0))],
        out_specs=pl.BlockSpec((tm, GMLP_W), lambda i: (i, 0)),
        out_shape=jax.ShapeDtypeStruct((SEQ, GMLP_W), BF16),
        compiler_params=_params(),
    )(proj, vgain, w_s, bias_full)


def _gmlp_bwd(proj, dyc, vgain, w_s, bias_full):
    tm = 512
    nsteps = SEQ // tm

    def body(p_ref, dy_ref, vg_ref, w_ref, b_ref, dg_ref, gw_ref, gb_ref, gv_ref):
        i = pl.program_id(0)
        bd = _head_blockdiag()
        lo = _lo_mask(CHUNK)
        tril, wm = _gmlp_weights(w_ref)
        ri = lax.broadcasted_iota(jnp.int32, (16, LANES), 0)
        li = lax.broadcasted_iota(jnp.int32, (16, LANES), 1)
        head_rows = [jnp.where(((ri == 2 * p) & (li < HEAD_DIM)) | ((ri == 2 * p + 1) & (li >= HEAD_DIM)), 1.0, 0.0).astype(BF16)
                     for p in range(2)]

        @pl.when(i == 0)
        def _():
            gw_ref[...] = jnp.zeros_like(gw_ref)
            gb_ref[...] = jnp.zeros_like(gb_ref)
            gv_ref[...] = jnp.zeros_like(gv_ref)

        for c in range(tm // CHUNK):
            rows = pl.ds(c * CHUNK, CHUNK)
            for p in range(2):
                cs = slice(p * LANES, (p + 1) * LANES)
                u = p_ref[rows, C_GU + p * LANES:C_GU + (p + 1) * LANES]
                v = p_ref[rows, C_GV + p * LANES:C_GV + (p + 1) * LANES]
                gt = p_ref[rows, C_GG + p * LANES:C_GG + (p + 1) * LANES]
                dy = dy_ref[rows, cs]
                g = vg_ref[:, cs]
                r = lax.rsqrt(_headsum(v * v, bd) * (1.0 / HEAD_DIM) + EPS)
                z = v * r
                vn = (z * g).astype(BF16)
                sp = jnp.where(lo, _dot(wm[2 * p], vn), _dot(wm[2 * p + 1], vn)) + b_ref[:, cs]
                sg = _sigmoid(gt)
                sl = gt * sg
                dsl = sg * (1.0 + gt * (1.0 - sg))
                du = dy * sp * sl
                dsp = dy * u * sl
                dgt = dy * u * sp * dsl
                dspb = dsp.astype(BF16)
                dvn = jnp.where(lo, _dot_tn(wm[2 * p], dspb), _dot_tn(wm[2 * p + 1], dspb))
                gw_ref[2 * p] += _dot_nt(jnp.where(lo, dsp, 0.0).astype(BF16), vn)
                gw_ref[2 * p + 1] += _dot_nt(jnp.where(lo, 0.0, dsp).astype(BF16), vn)
                dsp_lo = (dsp - dspb.astype(F32)).astype(BF16)
                gb_ref[...] += (_dot_nt(head_rows[p], dspb) + _dot_nt(head_rows[p], dsp_lo))[0:8]
                gvp = jnp.sum(dvn * z, axis=0, keepdims=True)
                gv_ref[2 * p:2 * p + 1, :] += gvp
                gv_ref[2 * p + 1:2 * p + 2, :] += pltpu.roll(gvp, HEAD_DIM, 1)
                dz = dvn * g
                dv = r * (dz - z * (_headsum(dz * z, bd) * (1.0 / HEAD_DIM)))
                dg_ref[rows, C_GU + p * LANES:C_GU + (p + 1) * LANES] = du.astype(BF16)
                dg_ref[rows, C_GV + p * LANES:C_GV + (p + 1) * LANES] = dv.astype(BF16)
                dg_ref[rows, C_GG + p * LANES:C_GG + (p + 1) * LANES] = dgt.astype(BF16)

        @pl.when(i == nsteps - 1)
        def _():
            for h in range(4):
                gw_ref[h] = jnp.where(tril, gw_ref[h], 0.0)

    return _call(
        body, name="gmlp_bwd", grid=(nsteps,),
        in_specs=[pl.BlockSpec((tm, 3 * GMLP_W), lambda i: (i, 0)),
                  pl.BlockSpec((tm, GMLP_W), lambda i: (i, 0)),
                  pl.BlockSpec((1, GMLP_W), lambda i: (0, 0)),
                  pl.BlockSpec((4, CHUNK, CHUNK), lambda i: (0, 0, 0)),
                  pl.BlockSpec((CHUNK, GMLP_W), lambda i: (0, 0))],
        out_specs=[pl.BlockSpec((tm, 3 * GMLP_W), lambda i: (i, 0)),
                   pl.BlockSpec((4, CHUNK, CHUNK), lambda i: (0, 0, 0)),
                   pl.BlockSpec((8, LANES), lambda i: (0, 0)),
                   pl.BlockSpec((8, LANES), lambda i: (0, 0))],
        out_shape=[jax.ShapeDtypeStruct((SEQ, 3 * GMLP_W), BF16),
                   jax.ShapeDtypeStruct((4, CHUNK, CHUNK), F32),
                   jax.ShapeDtypeStruct((8, LANES), F32),
                   jax.ShapeDtypeStruct((8, LANES), F32)],
        compiler_params=_params(),
    )(proj, dyc, vgain, w_s, bias_full)


def _band_masks():
    qi = lax.broadcasted_iota(jnp.int32, (CHUNK, 2 * CHUNK), 0)
    kj = lax.broadcasted_iota(jnp.int32, (CHUNK, 2 * CHUNK), 1)
    valid2 = ((kj < CHUNK) & (kj >= qi)) | ((kj >= CHUNK) & (kj - CHUNK <= qi))
    q1 = lax.broadcasted_iota(jnp.int32, (CHUNK, CHUNK), 0)
    k1 = lax.broadcasted_iota(jnp.int32, (CHUNK, CHUNK), 1)
    return k1 <= q1, valid2


def _stack_heads(v, lo):
    return jnp.concatenate([jnp.where(lo, v, 0.0), jnp.where(lo, 0.0, v)], axis=0).astype(BF16)


def _rows_of(ref, start, d):
    if d == 1:
        return ref.at[pl.ds(start if isinstance(start, int) else pl.multiple_of(start, CHUNK), CHUNK), :]
    return ref.at[pl.ds(start, CHUNK, stride=d), :]


def _unrolled(lo, hi, unroll, run):
    groups = (hi - lo) // unroll
    if groups:
        def body(g, carry):
            run([lo + g * unroll + t for t in range(unroll)])
            return carry

        lax.fori_loop(0, groups, body, 0)
    if lo + groups * unroll < hi:
        run(range(lo + groups * unroll, hi))


def _for_blocks(d, group_fn, unroll):
    nblk = SEQ // CHUNK
    sh = d.bit_length() - 1

    def first(j):
        return (j * CHUNK if d == 1 else j, None)

    def rest(j):
        start = (j & (d - 1)) + (j >> sh) * (CHUNK * d)
        return (start, start - CHUNK * d)

    _unrolled(0, d, unroll, lambda js: group_fn(d, [first(j) for j in js]))
    _unrolled(d, nblk, unroll, lambda js: group_fn(d, [rest(j) for j in js]))


def _attn_fwd(proj, gq2, gk2):
    tn = 512

    def body(q_ref, k_ref, v_ref, g_ref, gq_ref, gk_ref, o_ref, l_ref, ya_ref, qn_ref, kn_ref):
        bd = _head_blockdiag()
        lo = _lo_mask(CHUNK)
        valid1, valid2 = _band_masks()

        def norm(t, carry):
            rows = pl.ds(pl.multiple_of(t * tn, tn), tn)
            q = q_ref[rows, :]
            qn_ref[rows, :] = q * lax.rsqrt(_headsum(q * q, bd) * (1.0 / HEAD_DIM) + EPS) * (gq_ref[...] * QK_SCALE)
            k = k_ref[rows, :]
            kn_ref[rows, :] = k * lax.rsqrt(_headsum(k * k, bd) * (1.0 / HEAD_DIM) + EPS) * gk_ref[...]
            return carry

        lax.fori_loop(0, SEQ // tn, norm, 0)

        def load_kv(ref, d, start, prev):
            own = _rows_of(ref, start, d)[...]
            if prev is None:
                return own.astype(BF16)
            return jnp.concatenate([_rows_of(ref, prev, d)[...], own], axis=0).astype(BF16)

        def group(d, blocks):
            valid = valid1 if blocks[0][1] is None else valid2
            valid = jnp.concatenate([valid, valid], axis=0)
            qs = [_rows_of(qn_ref, start, d)[...] for start, _ in blocks]
            ks = [load_kv(kn_ref, d, start, prev) for start, prev in blocks]
            vs = [load_kv(v_ref, d, start, prev) for start, prev in blocks]
            ss = [_dot_nt(_stack_heads(q, lo), k) for q, k in zip(qs, ks)]
            ms, ps, ls = [], [], []
            for s in ss:
                s = jnp.where(valid, s, -jnp.inf)
                m = jnp.max(s, axis=-1, keepdims=True)
                p = jnp.exp(s - m)
                ms.append(m)
                ls.append(jnp.sum(p, axis=-1, keepdims=True))
                ps.append(p.astype(BF16))
            os_ = [_dot(p, v) for p, v in zip(ps, vs)]
            for b, (start, _) in enumerate(blocks):
                on = os_[b] * (1.0 / ls[b])
                ln = ms[b] + jnp.log(ls[b])
                ob = jnp.where(lo, on[:CHUNK], on[CHUNK:])
                lb = jnp.where(lo, ln[:CHUNK], ln[CHUNK:])
                o_rows = _rows_of(o_ref, start, d)
                l_rows = _rows_of(l_ref, start, d)
                if d != DILATIONS[0]:
                    lold = l_rows[...]
                    mx = jnp.maximum(lold, lb)
                    ea = jnp.exp(lold - mx)
                    eb = jnp.exp(lb - mx)
                    inv = 1.0 / (ea + eb)
                    ob = o_rows[...] * (ea * inv) + ob * (eb * inv)
                    lb = mx + jnp.log(ea + eb)
                o_rows[...] = ob
                l_rows[...] = lb

        for d in DILATIONS:
            _for_blocks(d, group, ATTN_UNROLL)

        def fin(t, carry):
            rows = pl.ds(pl.multiple_of(t * tn, tn), tn)
            g = g_ref[rows, :]
            ya_ref[rows, :] = (o_ref[rows, :] * (g * _sigmoid(g))).astype(BF16)
            return carry

        lax.fori_loop(0, SEQ // tn, fin, 0)

    col = lambda c0: pl.BlockSpec((SEQ, LANES), lambda p: (0, c0 // LANES + p))
    vec = pl.BlockSpec((1, LANES), lambda p: (0, 0))
    out = pl.BlockSpec((SEQ, LANES), lambda p: (0, p))
    return _call(
        body, name="attn_fwd", grid=(ATTN_W // LANES,),
        in_specs=[col(C_AQ), col(C_AK), col(C_AV), col(C_AG), vec, vec],
        out_specs=[out, out, out],
        out_shape=[jax.ShapeDtypeStruct((SEQ, ATTN_W), F32), jax.ShapeDtypeStruct((SEQ, ATTN_W), F32),
                   jax.ShapeDtypeStruct((SEQ, ATTN_W), BF16)],
        scratch_shapes=[pltpu.VMEM((SEQ, LANES), F32), pltpu.VMEM((SEQ, LANES), F32)],
        compiler_params=_params(),
    )(proj, proj, proj, proj, gq2, gk2)


def _attn_bwd(proj, o, lse, dyc, gq2, gk2, *ride_along):
    tn = 512
    npairs = ATTN_W // LANES
    nride = len(ride_along)
    nbufs = nride * len(RS_KINDS)

    def body(proj_hbm, o_hbm, l_hbm, dyc_hbm, gq_ref, gk_ref, *rest):
        ride_in, rest = rest[:nride], rest[nride:]
        dq_ref, dk_ref, dv_ref, dgt_ref, gqg_ref, gkg_ref = rest[:6]
        ride_out, rest = rest[6:6 + nride], rest[6 + nride:]
        qb_, kb_, vb_, gb_, ob_, lb_, yb_, dkb_, dvb_, sems = rest[:10]
        rs_bufs, (send_sems, recv_sems, local_sems) = rest[10:10 + nbufs], rest[10 + nbufs:]
        rs_stage = _rs_stages(ride_in, ride_out, rs_bufs, send_sems, recv_sems, local_sems)
        pair = pl.program_id(0)
        for step in range(npairs):
            pl.when(pair == step)(rs_stage[step])
        bd = _head_blockdiag()
        lo = _lo_mask(CHUNK)
        lo2 = lax.broadcasted_iota(jnp.int32, (2 * CHUNK, LANES), 1) < HEAD_DIM
        valid1, valid2 = _band_masks()
        gqs = gq_ref[...] * QK_SCALE
        gk = gk_ref[...]

        def pcol(c0):
            return proj_hbm.at[:, pl.ds(pl.multiple_of(c0 + pair * LANES, LANES), LANES)]

        def acol(hbm, c0=0):
            return hbm.at[:, pl.ds(pl.multiple_of(c0 + pair * LANES, LANES), LANES)]

        loads = [pltpu.make_async_copy(src, dst, sems.at[n]) for n, (src, dst) in enumerate((
            (pcol(C_AQ), qb_), (pcol(C_AK), kb_), (pcol(C_AG), gb_), (acol(o_hbm), ob_),
            (acol(dyc_hbm, GMLP_W), yb_), (pcol(C_AV), vb_), (acol(l_hbm), lb_)))]
        for cp in loads:
            cp.start()

        @pl.when(pair == 0)
        def _():
            gqg_ref[...] = jnp.zeros_like(gqg_ref)
            gkg_ref[...] = jnp.zeros_like(gkg_ref)

        def pre_qk(t, carry):
            rows = pl.ds(pl.multiple_of(t * tn, tn), tn)
            zero = jnp.zeros((tn, LANES), F32)
            dkb_[rows, :] = zero
            dvb_[rows, :] = zero
            q = qb_[rows, :]
            qb_[rows, :] = q * lax.rsqrt(_headsum(q * q, bd) * (1.0 / HEAD_DIM) + EPS) * gqs
            k = kb_[rows, :]
            kb_[rows, :] = k * lax.rsqrt(_headsum(k * k, bd) * (1.0 / HEAD_DIM) + EPS) * gk
            return carry

        def pre_gate(t, carry):
            rows = pl.ds(pl.multiple_of(t * tn, tn), tn)
            g = gb_[rows, :]
            ov = ob_[rows, :]
            dya = yb_[rows, :]
            sg = _sigmoid(g)
            dgt_ref[rows, :] = (dya * ov * (sg * (1.0 + g * (1.0 - sg)))).astype(BF16)
            do = dya * (g * sg)
            yb_[rows, :] = do
            ob_[rows, :] = _headsum(do * ov, bd)
            gb_[rows, :] = jnp.zeros((tn, LANES), F32)
            return carry

        loads[0].wait()
        loads[1].wait()
        lax.fori_loop(0, SEQ // tn, pre_qk, 0)
        for cp in loads[2:5]:
            cp.wait()
        lax.fori_loop(0, SEQ // tn, pre_gate, 0)
        loads[5].wait()
        loads[6].wait()

        def load_kv(ref, d, start, prev):
            own = _rows_of(ref, start, d)[...]
            if prev is None:
                return own.astype(BF16)
            return jnp.concatenate([_rows_of(ref, prev, d)[...], own], axis=0).astype(BF16)

        def group(d, blocks):
            first = blocks[0][1] is None
            valid, lok = (valid1, lo) if first else (valid2, lo2)
            chains = [(b, h) for b in range(len(blocks)) for h in range(2)]
            mask = lambda h: lo if h == 0 else ~lo
            qs = [_rows_of(qb_, start, d)[...] for start, _ in blocks]
            dos = [_rows_of(yb_, start, d)[...] for start, _ in blocks]
            lvs = [_rows_of(lb_, start, d)[...] for start, _ in blocks]
            dls = [_rows_of(ob_, start, d)[...] for start, _ in blocks]
            ks = [load_kv(kb_, d, start, prev) for start, prev in blocks]
            vs = [load_kv(vb_, d, start, prev) for start, prev in blocks]
            qbs = [q.astype(BF16) for q in qs]
            dobs = [do.astype(BF16) for do in dos]
            ss = [_dot_nt(jnp.where(mask(h), qs[b], 0.0).astype(BF16), ks[b]) for b, h in chains]
            dps = [_dot_nt(jnp.where(mask(h), dos[b], 0.0).astype(BF16), vs[b]) for b, h in chains]
            pbs, dss = [], []
            for s, dp, (b, h) in zip(ss, dps, chains):
                hc = h * HEAD_DIM
                p = jnp.exp(jnp.where(valid, s, -jnp.inf) - lvs[b][:, hc:hc + 1])
                pbs.append(p.astype(BF16))
                dss.append((p * (dp - dls[b][:, hc:hc + 1])).astype(BF16))
            dqs = [_dot(ds, ks[b]) for ds, (b, h) in zip(dss, chains)]
            dks = [_dot_tn(ds, qbs[b]) for ds, (b, h) in zip(dss, chains)]
            dvs = [_dot_tn(p, dobs[b]) for p, (b, h) in zip(pbs, chains)]
            for b, (start, prev) in enumerate(blocks):
                c0, c1 = 2 * b, 2 * b + 1
                dq_rows = _rows_of(gb_, start, d)
                dq_rows[...] = dq_rows[...] + jnp.where(lo, dqs[c0], dqs[c1])
                dkc = jnp.where(lok, dks[c0], dks[c1])
                dvc = jnp.where(lok, dvs[c0], dvs[c1])
                spans = ((start, slice(0, CHUNK)),) if first else ((prev, slice(0, CHUNK)), (start, slice(CHUNK, 2 * CHUNK)))
                for st, sl in spans:
                    dk_rows = _rows_of(dkb_, st, d)
                    dk_rows[...] = dk_rows[...] + dkc[sl]
                    dv_rows = _rows_of(dvb_, st, d)
                    dv_rows[...] = dv_rows[...] + dvc[sl]

        for d in DILATIONS:
            _for_blocks(d, group, ATTN_UNROLL)

        pltpu.sync_copy(pcol(C_AQ), qb_)
        pltpu.sync_copy(pcol(C_AK), kb_)

        def post(t, carry):
            gq_acc, gk_acc = carry
            rows = pl.ds(pl.multiple_of(t * tn, tn), tn)
            outs = []
            for raw_, acc_, gain in ((qb_, gb_, gqs), (kb_, dkb_, gk)):
                a = raw_[rows, :]
                r = lax.rsqrt(_headsum(a * a, bd) * (1.0 / HEAD_DIM) + EPS)
                z = a * r
                dn = acc_[rows, :]
                dz = dn * gain
                outs.append((r * (dz - z * (_headsum(dz * z, bd) * (1.0 / HEAD_DIM))), jnp.sum(dn * z, axis=0, keepdims=True)))
            dq_ref[rows, :] = outs[0][0].astype(BF16)
            dk_ref[rows, :] = outs[1][0].astype(BF16)
            dv_ref[rows, :] = dvb_[rows, :].astype(BF16)
            return gq_acc + outs[0][1] * QK_SCALE, gk_acc + outs[1][1]

        zero = jnp.zeros((1, LANES), F32)
        gq_acc, gk_acc = lax.fori_loop(0, SEQ // tn, post, (zero, zero))
        gqg_ref[0:1, :] += gq_acc
        gkg_ref[0:1, :] += gk_acc

        @pl.when(pair == npairs - 1)
        def _():
            gqg_ref[0:1, :] = _fold_heads(gqg_ref[0:1, :])
            gkg_ref[0:1, :] = _fold_heads(gkg_ref[0:1, :])
            rs_stage[npairs]()

    hbm = pl.BlockSpec(memory_space=pl.ANY)
    vec = pl.BlockSpec((1, LANES), lambda p: (0, 0))
    blk8 = pl.BlockSpec((8, LANES), lambda p: (0, 0))
    out = pl.BlockSpec((SEQ, LANES), lambda p: (0, p))
    big = jax.ShapeDtypeStruct((SEQ, ATTN_W), BF16)
    nsem = RS_SEMS * nride
    return _call(
        body, name="attn_bwd", grid=(npairs,),
        in_specs=[hbm, hbm, hbm, hbm, vec, vec] + [hbm] * nride,
        out_specs=[out, out, out, out, blk8, blk8] + [hbm] * nride,
        out_shape=[big, big, big, big, jax.ShapeDtypeStruct((8, LANES), F32), jax.ShapeDtypeStruct((8, LANES), F32)]
        + [jax.ShapeDtypeStruct((2, g.shape[0] // 8, g.shape[1]), F32) for g in ride_along],
        scratch_shapes=[pltpu.VMEM((SEQ, LANES), F32) for _ in range(9)] + [pltpu.SemaphoreType.DMA((7,))]
        + _rs_scratch(ride_along) + [pltpu.SemaphoreType.DMA((nsem,)), pltpu.SemaphoreType.DMA((nsem,)),
                                     pltpu.SemaphoreType.DMA((nride,))],
        compiler_params=_params(),
    )(proj, o, lse, dyc, gq2, gk2, *[_rs_view(g) for g in ride_along])


def _mem_kv(mem, gain, wkv):
    def body(m_ref, g_ref, w_ref, kv_ref, hm_ref):
        mv = m_ref[...]
        ms = jnp.mean(mv * mv, axis=-1, keepdims=True)
        hm = (mv * lax.rsqrt(ms + EPS) * g_ref[...]).astype(BF16)
        hm_ref[...] = hm
        kv_ref[...] = _dot(hm, w_ref[...])

    return _call(
        body, name="mem_kv",
        out_shape=[jax.ShapeDtypeStruct((MEM_LEN, 2 * MEM_W), F32), jax.ShapeDtypeStruct((MEM_LEN, D_MODEL), BF16)],
        compiler_params=_params(),
    )(mem, gain, wkv)


def _mem_keys(kv_ref, kg_ref, bd, p):
    mk = kv_ref[:, p * LANES:(p + 1) * LANES]
    r = lax.rsqrt(_headsum(mk * mk, bd) * (1.0 / HEAD_DIM) + EPS)
    z = mk * r
    mkn = (z * kg_ref[:, p * LANES:(p + 1) * LANES]).astype(BF16)
    mvp = kv_ref[:, MEM_W + p * LANES:MEM_W + (p + 1) * LANES].astype(BF16)
    return mkn, mvp, r, z


def _mem_fwd(proj, kv, qg4, kg4):
    tm = 512

    def body(q_ref, g_ref, kv_ref, qg_ref, kg_ref, om_ref, ym_ref):
        bd = _head_blockdiag()
        lo = _lo_mask(tm)
        for p in range(2):
            cs = slice(p * LANES, (p + 1) * LANES)
            mkn, mvp, _, _ = _mem_keys(kv_ref, kg_ref, bd, p)
            q = q_ref[:, cs]
            qn = q * lax.rsqrt(_headsum(q * q, bd) * (1.0 / HEAD_DIM) + EPS) * (qg_ref[:, cs] * QK_SCALE)
            res = []
            for h in range(2):
                qh = jnp.where(lo if h == 0 else ~lo, qn, 0.0).astype(BF16)
                s = _dot_nt(qh, mkn)
                e = jnp.exp(s - jnp.max(s, axis=-1, keepdims=True))
                res.append(_dot(e.astype(BF16), mvp) * (1.0 / jnp.sum(e, axis=-1, keepdims=True)))
            ov = jnp.where(lo, res[0], res[1])
            g = g_ref[:, cs]
            om_ref[:, cs] = ov
            ym_ref[:, cs] = (ov * (g * _sigmoid(g))).astype(BF16)

    vec = pl.BlockSpec((1, MEM_W), lambda i: (0, 0))
    return _call(
        body, name="mem_fwd", grid=(SEQ // tm,),
        in_specs=[pl.BlockSpec((tm, MEM_W), lambda i: (i, C_MQ // MEM_W)),
                  pl.BlockSpec((tm, MEM_W), lambda i: (i, C_MG // MEM_W)),
                  pl.BlockSpec((MEM_LEN, 2 * MEM_W), lambda i: (0, 0)), vec, vec],
        out_specs=[pl.BlockSpec((tm, MEM_W), lambda i: (i, 0)), pl.BlockSpec((tm, MEM_W), lambda i: (i, 0))],
        out_shape=[jax.ShapeDtypeStruct((SEQ, MEM_W), F32), jax.ShapeDtypeStruct((SEQ, MEM_W), BF16)],
        compiler_params=_params(),
    )(proj, proj, kv, qg4, kg4)


def _mem_bwd(proj, om, dyc, kv, hm, mem, mgain, wkv, qg4, kg4):
    tm = 512
    nsteps = SEQ // tm

    def body(q_ref, g_ref, om_ref, dy_ref, kv_ref, hm_ref, mem_ref, mg_ref, w_ref, qg_ref, kg_ref,
             dq_ref, dgt_ref, gqg_ref, gkg_ref, gw_ref, gmg_ref, dmk_ref, dmv_ref, gq_acc):
        i = pl.program_id(0)
        bd = _head_blockdiag()
        lo = _lo_mask(tm)
        lom = _lo_mask(MEM_LEN)

        @pl.when(i == 0)
        def _():
            dmk_ref[...] = jnp.zeros_like(dmk_ref)
            dmv_ref[...] = jnp.zeros_like(dmv_ref)
            gq_acc[...] = jnp.zeros_like(gq_acc)

        for p in range(2):
            cs = slice(p * LANES, (p + 1) * LANES)
            mkn, mvp, _, _ = _mem_keys(kv_ref, kg_ref, bd, p)
            gqs = qg_ref[:, cs] * QK_SCALE
            q = q_ref[:, cs]
            r = lax.rsqrt(_headsum(q * q, bd) * (1.0 / HEAD_DIM) + EPS)
            z = q * r
            qn = z * gqs
            qnb = qn.astype(BF16)
            g = g_ref[:, cs]
            ov = om_ref[:, cs]
            dym = dy_ref[:, cs]
            sg = _sigmoid(g)
            dgt_ref[:, cs] = (dym * ov * (sg * (1.0 + g * (1.0 - sg)))).astype(BF16)
            do = dym * (g * sg)
            dob = do.astype(BF16)
            delta = _headsum(do * ov, bd)
            parts = []
            for h in range(2):
                mh = lo if h == 0 else ~lo
                hc = h * HEAD_DIM
                qh = jnp.where(mh, qn, 0.0).astype(BF16)
                doh = jnp.where(mh, do, 0.0).astype(BF16)
                s = _dot_nt(qh, mkn)
                e = jnp.exp(s - jnp.max(s, axis=-1, keepdims=True))
                pr = e * (1.0 / jnp.sum(e, axis=-1, keepdims=True))
                dp = _dot_nt(doh, mvp)
                ds = (pr * (dp - delta[:, hc:hc + 1])).astype(BF16)
                parts.append((_dot(ds, mkn), _dot_tn(ds, qnb), _dot_tn(pr.astype(BF16), dob)))
            dqn = jnp.where(lo, parts[0][0], parts[1][0])
            dmk_ref[:, cs] += jnp.where(lom, parts[0][1], parts[1][1])
            dmv_ref[:, cs] += jnp.where(lom, parts[0][2], parts[1][2])
            dz = dqn * gqs
            dq_ref[:, cs] = (r * (dz - z * (_headsum(dz * z, bd) * (1.0 / HEAD_DIM)))).astype(BF16)
            gq_acc[:, cs] += jnp.sum(dqn * z, axis=0, keepdims=True) * QK_SCALE

        @pl.when(i == nsteps - 1)
        def _():
            gqg_ref[...] = jnp.zeros_like(gqg_ref)
            gkg_ref[...] = jnp.zeros_like(gkg_ref)
            gqg_ref[0:1, :] = _fold_heads(gq_acc[:, 0:LANES] + gq_acc[:, LANES:2 * LANES])
            dkv = []
            gk = jnp.zeros((1, LANES), F32)
            for p in range(2):
                cs = slice(p * LANES, (p + 1) * LANES)
                _, _, r, z = _mem_keys(kv_ref, kg_ref, bd, p)
                dn = dmk_ref[:, cs]
                dz = dn * kg_ref[:, cs]
                gk = gk + jnp.sum(dn * z, axis=0, keepdims=True)
                dkv.append(r * (dz - z * (_headsum(dz * z, bd) * (1.0 / HEAD_DIM))))
            gkg_ref[0:1, :] = _fold_heads(gk)
            dkvb = jnp.concatenate(dkv + [dmv_ref[...]], axis=1).astype(BF16)
            gw_ref[...] = _dot_tn(hm_ref[...], dkvb)
            dhm = _dot_nt(dkvb, w_ref[...])
            mv = mem_ref[...]
            zm = mv * lax.rsqrt(jnp.mean(mv * mv, axis=-1, keepdims=True) + EPS)
            _put_rows(gmg_ref, jnp.sum(dhm * zm, axis=0, keepdims=True))

    const = lambda shape: pl.BlockSpec(shape, lambda i: (0,) * len(shape))
    row = lambda j: pl.BlockSpec((tm, MEM_W), lambda i: (i, j))
    blk8 = jax.ShapeDtypeStruct((8, LANES), F32)
    return _call(
        body, name="mem_bwd", grid=(nsteps,),
        in_specs=[row(C_MQ // MEM_W), row(C_MG // MEM_W), row(0), row((GMLP_W + ATTN_W) // MEM_W),
                  const((MEM_LEN, 2 * MEM_W)), const((MEM_LEN, D_MODEL)), const((MEM_LEN, D_MODEL)),
                  const((1, D_MODEL)), const((D_MODEL, 2 * MEM_W)), const((1, MEM_W)), const((1, MEM_W))],
        out_specs=[row(0), row(0), const((8, LANES)), const((8, LANES)),
                   const((D_MODEL, 2 * MEM_W)), const((8, LANES))],
        out_shape=[jax.ShapeDtypeStruct((SEQ, MEM_W), BF16), jax.ShapeDtypeStruct((SEQ, MEM_W), BF16),
                   blk8, blk8, jax.ShapeDtypeStruct((D_MODEL, 2 * MEM_W), F32), blk8],
        scratch_shapes=[pltpu.VMEM((MEM_LEN, MEM_W), F32), pltpu.VMEM((MEM_LEN, MEM_W), F32),
                        pltpu.VMEM((1, MEM_W), F32)],
        compiler_params=_params(),
    )(proj, proj, om, dyc, kv, hm, mem, mgain, wkv, qg4, kg4)


def _out_loss(yg, ya, ym, x, tgt, wo):
    tm = 512
    nsteps = SEQ // tm
    parts = ((0, GMLP_W), (GMLP_W, ATTN_W), (GMLP_W + ATTN_W, MEM_W))

    def body(yg_ref, ya_ref, ym_ref, x_ref, t_ref, w_ref, dy_ref, dyc_ref, gw_ref, ls_ref):
        i = pl.program_id(0)

        @pl.when(i == 0)
        def _():
            gw_ref[...] = jnp.zeros_like(gw_ref)
            ls_ref[...] = jnp.zeros_like(ls_ref)

        ys = (yg_ref[...], ya_ref[...], ym_ref[...])
        y = sum(_dot(yv, w_ref[r0:r0 + n, :]) for yv, (r0, n) in zip(ys, parts))
        err = x_ref[...] + y - t_ref[...]
        _put_rows(ls_ref, jnp.sum(err * err, axis=0, keepdims=True), accumulate=True)
        dy = err * (1.0 / D_MODEL)
        dy_ref[...] = dy
        dyb = dy.astype(BF16)
        dyc_ref[...] = _dot_nt(dyb, w_ref[...])
        for yv, (r0, n) in zip(ys, parts):
            gw_ref[r0:r0 + n, :] += _dot_tn(yv, dyb)

    row = lambda w: pl.BlockSpec((tm, w), lambda i: (i, 0))
    const = lambda shape: pl.BlockSpec(shape, lambda i: (0, 0))
    return _call(
        body, name="out_loss", grid=(nsteps,),
        in_specs=[row(GMLP_W), row(ATTN_W), row(MEM_W), row(D_MODEL), row(D_MODEL), const((D_MODEL, D_MODEL))],
        out_specs=[row(D_MODEL), row(D_MODEL), const((D_MODEL, D_MODEL)), const((8, LANES))],
        out_shape=[jax.ShapeDtypeStruct((SEQ, D_MODEL), F32), jax.ShapeDtypeStruct((SEQ, D_MODEL), F32),
                   jax.ShapeDtypeStruct((D_MODEL, D_MODEL), F32), jax.ShapeDtypeStruct((8, LANES), F32)],
        compiler_params=_params(),
    )(yg, ya, ym, x, tgt, wo)


def _proj_bwd(x, dy, gain, wt, dg, daq, dak, dav, dag, dmq, dmg):
    tm = 512
    nsteps = SEQ // tm
    pieces = ((C_GU, 3 * GMLP_W), (C_AQ, ATTN_W), (C_AK, ATTN_W), (C_AV, ATTN_W), (C_AG, ATTN_W),
              (C_MQ, MEM_W), (C_MG, MEM_W))

    def body(x_ref, dy_ref, g_ref, wt_hbm, p0, p1, p2, p3, p4, p5, p6, gx_ref, gwt_hbm, gg_ref, wt_v, acc):
        i = pl.program_id(0)

        @pl.when(i == 0)
        def _():
            pltpu.sync_copy(wt_hbm, wt_v)
            acc[...] = jnp.zeros_like(acc)
            gg_ref[...] = jnp.zeros_like(gg_ref)

        xv = x_ref[...]
        r = lax.rsqrt(jnp.mean(xv * xv, axis=-1, keepdims=True) + EPS)
        z = xv * r
        g = g_ref[...]
        h = (z * g).astype(BF16)
        dh = jnp.zeros((tm, D_MODEL), F32)
        for pref, (c0, w) in zip((p0, p1, p2, p3, p4, p5, p6), pieces):
            dp = pref[...]
            dh = dh + _dot(dp, wt_v[c0:c0 + w, :])
            acc[c0:c0 + w, :] += _dot_tn(dp, h)
        _put_rows(gg_ref, jnp.sum(dh * z, axis=0, keepdims=True), accumulate=True)
        dz = dh * g
        gx_ref[...] = dy_ref[...] + r * (dz - z * jnp.mean(dz * z, axis=-1, keepdims=True))

        @pl.when(i == nsteps - 1)
        def _():
            pltpu.sync_copy(acc, gwt_hbm)

    row = lambda w: pl.BlockSpec((tm, w), lambda i: (i, 0))
    hbm = pl.BlockSpec(memory_space=pl.ANY)
    vec = pl.BlockSpec((1, D_MODEL), lambda i: (0, 0))
    return _call(
        body, name="proj_bwd", grid=(nsteps,),
        in_specs=[row(D_MODEL), row(D_MODEL), vec, hbm] + [row(w) for _, w in pieces],
        out_specs=[row(D_MODEL), hbm, pl.BlockSpec((8, LANES), lambda i: (0, 0))],
        out_shape=[jax.ShapeDtypeStruct((SEQ, D_MODEL), F32), jax.ShapeDtypeStruct((IN_W, D_MODEL), F32),
                   jax.ShapeDtypeStruct((8, LANES), F32)],
        scratch_shapes=[pltpu.VMEM((IN_W, D_MODEL), BF16), pltpu.VMEM((IN_W, D_MODEL), F32)],
        compiler_params=_params(),
    )(x, dy, gain, wt, dg, daq, dak, dav, dag, dmq, dmg)


def _gather_weights(wt_sh, wkv_sh, wo_sh):
    shards = (wt_sh, wkv_sh, wo_sh)
    nrows = tuple(a.shape[0] for a in shards)

    def body(a0, a1, a2, o0, o1, o2, send_sems, recv_sems):
        x, y, c = lax.axis_index("x"), lax.axis_index("y"), lax.axis_index("c")
        sib, xn, yn = (x, y, 1 - c), (1 - x, y, c), (x, 1 - y, c)
        me, cx, cy, cd = 2 * x + y, 2 * (1 - x) + y, 2 * x + (1 - y), 2 * (1 - x) + (1 - y)
        ins, outs = (a0, a1, a2), (o0, o1, o2)

        def part(a, chip, hf, quarter=None):
            n = nrows[a] // 2
            base = chip * nrows[a] + hf * n
            if quarter is not None:
                n = n // 2
                base = base + quarter * n
            return outs[a].at[pl.ds(pl.multiple_of(base, 16), n), :]

        def copy(k, ref, to):
            return pltpu.make_async_remote_copy(src_ref=ref, dst_ref=ref, send_sem=send_sems.at[k],
                                                recv_sem=recv_sems.at[k], device_id=to, device_id_type=MESH)

        started = []

        def go(cp):
            cp.start()
            started.append(cp)

        for a in range(3):
            outs[a][pl.ds(pl.multiple_of(me * nrows[a], 16), nrows[a]), :] = ins[a][...].astype(BF16)
        for a in range(3):
            go(copy(8 * a, part(a, me, c), xn))
            go(copy(8 * a + 1, part(a, me, c), yn))
        for a in range(3):
            k = 8 * a
            copy(k, part(a, cx, c), xn).wait_recv()
            go(copy(k + 4, part(a, cx, c, 1), yn))
            go(copy(k + 2, part(a, cx, c), sib))
            copy(k + 1, part(a, cy, c), yn).wait_recv()
            go(copy(k + 5, part(a, cy, c, 0), xn))
            go(copy(k + 3, part(a, cy, c), sib))
        for a in range(3):
            k = 8 * a
            copy(k + 4, part(a, cd, c, 1), yn).wait_recv()
            go(copy(k + 7, part(a, cd, c, 1), sib))
            copy(k + 5, part(a, cd, c, 0), xn).wait_recv()
            go(copy(k + 6, part(a, cd, c, 0), sib))
        for a in range(3):
            k = 8 * a
            copy(k + 2, part(a, cx, 1 - c), sib).wait_recv()
            copy(k + 3, part(a, cy, 1 - c), sib).wait_recv()
            copy(k + 6, part(a, cd, 1 - c, 0), sib).wait_recv()
            copy(k + 7, part(a, cd, 1 - c, 1), sib).wait_recv()
        for cp in started:
            cp.wait_send()

    return _call(
        body, name="gather_weights",
        out_shape=[jax.ShapeDtypeStruct((4 * a.shape[0], a.shape[1]), BF16) for a in shards],
        in_specs=[pl.BlockSpec(memory_space=pltpu.VMEM)] * 3,
        out_specs=[pl.BlockSpec(memory_space=pltpu.VMEM)] * 3,
        scratch_shapes=[pltpu.SemaphoreType.DMA((24,)), pltpu.SemaphoreType.DMA((24,))],
        compiler_params=_params(),
    )(*shards)


RS_SEMS = 6
RS_KINDS = (((2, 2), 1, F32), ((2, 2), 1, F32), ((2, 2), 2, BF16), ((2, 2), 2, BF16), ((2, 2), 2, F32),
            ((2,), 2, BF16), ((2,), 2, BF16), ((2,), 1, F32))


def _rs_view(g):
    return g.reshape(2, 2, 2, g.shape[0] // 8, g.shape[1])


def _rs_scratch(grads):
    return [pltpu.VMEM(lead + (g.shape[0] // 8, g.shape[1] // split), dt) for lead, split, dt in RS_KINDS for g in grads]


def _rs_stages(gs, outs, bufs, send_sems, recv_sems, local_sems):
    n = len(gs)
    loc, ra, s_b, r_b, acc1, s_c, r_c, fin = (bufs[n * i:n * i + n] for i in range(len(RS_KINDS)))
    piece = [g.shape[3] for g in gs]
    half_w = [g.shape[4] // 2 for g in gs]
    x, y, c = lax.axis_index("x"), lax.axis_index("y"), lax.axis_index("c")
    sib, xn, yn = (x, y, 1 - c), (1 - x, y, c), (x, 1 - y, c)

    def copy(a, j, src, dst, to):
        k = RS_SEMS * a + j
        return pltpu.make_async_remote_copy(src_ref=src, dst_ref=dst, send_sem=send_sems.at[k],
                                            recv_sem=recv_sems.at[k], device_id=to, device_id_type=MESH)

    def step_a(a):
        return (copy(a, 0, gs[a].at[:, :, 1 - c], ra[a], sib),
                pltpu.make_async_copy(gs[a].at[:, :, c], loc[a], local_sems.at[a]))

    def step_b(a):
        return copy(a, 1, s_b[a].at[0], r_b[a].at[0], xn), copy(a, 2, s_b[a].at[1], r_b[a].at[1], yn)

    def step_c(a):
        return copy(a, 3, s_c[a].at[0], r_c[a].at[0], yn), copy(a, 4, s_c[a].at[1], r_c[a].at[1], xn)

    def step_d(a, half):
        rows = fin[a].at[half]
        return copy(a, 5, rows, rows, sib)

    def start():
        for a in range(n):
            for cp in step_a(a):
                cp.start()

    def a_to_b():
        for a in range(n):
            for cp in step_a(a):
                cp.wait()
            ra[a][...] = loc[a][...] + ra[a][...]
            s_b[a][0] = ra[a][1 - x, :, :, :half_w[a]].astype(BF16)
            s_b[a][1] = ra[a][:, 1 - y, :, half_w[a]:].astype(BF16)
            for cp in step_b(a):
                cp.start()

    def b_to_c():
        for a in range(n):
            for cp in step_b(a):
                cp.wait()
            acc1[a][0] = ra[a][x, :, :, :half_w[a]] + r_b[a][0].astype(F32)
            acc1[a][1] = ra[a][:, y, :, half_w[a]:] + r_b[a][1].astype(F32)
            s_c[a][0] = acc1[a][0, 1 - y].astype(BF16)
            s_c[a][1] = acc1[a][1, 1 - x].astype(BF16)
            for cp in step_c(a):
                cp.start()

    def c_to_d():
        for a in range(n):
            for cp in step_c(a):
                cp.wait()
            fin[a][c, :, :half_w[a]] = acc1[a][0, y] + r_c[a][0].astype(F32)
            fin[a][c, :, half_w[a]:] = acc1[a][1, x] + r_c[a][1].astype(F32)
            step_d(a, c).start()

    def finish():
        for a in range(n):
            step_d(a, 1 - c).wait_recv()
            step_d(a, c).wait_send()
            pltpu.sync_copy(fin[a], outs[a])

    return start, a_to_b, b_to_c, c_to_d, finish


def _reduce_grads(gwt, g_ws, tiny):
    grads = (gwt,)

    def body(g0, ws_in, tiny_in, o0, o_ws, o_tiny, *rest):
        nb = len(RS_KINDS)
        sm, sa, sb, sc, acc_s, send_sems, recv_sems, local_sems = rest[nb:]
        start, a_to_b, b_to_c, c_to_d, finish = _rs_stages((g0,), (o0,), rest[:nb], send_sems, recv_sems, local_sems)
        n_ws = ws_in.shape[0]
        sm[0:n_ws, :] = ws_in[...]
        sm[n_ws:, :] = tiny_in[...]
        x, y, c = lax.axis_index("x"), lax.axis_index("y"), lax.axis_index("c")

        def small(j, src, dst, to):
            k = RS_SEMS + j
            return pltpu.make_async_remote_copy(src_ref=src, dst_ref=dst, send_sem=send_sems.at[k],
                                                recv_sem=recv_sems.at[k], device_id=to, device_id_type=MESH)

        along_c, along_x, along_y = (small(0, sm, sa, (x, y, 1 - c)), small(1, acc_s, sb, (1 - x, y, c)),
                                     small(2, sb, sc, (x, 1 - y, c)))
        start()
        along_c.start()
        a_to_b()
        along_c.wait()
        acc_s[...] = sm[...] + sa[...]
        along_x.start()
        b_to_c()
        along_x.wait()
        sb[...] = acc_s[...] + sb[...]
        along_y.start()
        c_to_d()
        along_y.wait()
        o_ws[...] = sb[0:n_ws, :] + sc[0:n_ws, :]
        o_tiny[...] = sb[n_ws:, :] + sc[n_ws:, :]
        finish()

    vm = pl.BlockSpec(memory_space=pltpu.VMEM)
    hbm = pl.BlockSpec(memory_space=pl.ANY)
    small_shape = (g_ws.shape[0] + tiny.shape[0], LANES)
    scratch = _rs_scratch(grads) + [pltpu.VMEM(small_shape, F32) for _ in range(5)]
    scratch += [pltpu.SemaphoreType.DMA((RS_SEMS + 3,)), pltpu.SemaphoreType.DMA((RS_SEMS + 3,)), pltpu.SemaphoreType.DMA((1,))]
    return _call(
        body, name="reduce_grads",
        out_shape=[jax.ShapeDtypeStruct((2, gwt.shape[0] // 8, gwt.shape[1]), F32),
                   jax.ShapeDtypeStruct(g_ws.shape, F32), jax.ShapeDtypeStruct(tiny.shape, F32)],
        in_specs=[hbm, vm, vm],
        out_specs=[hbm, vm, vm],
        scratch_shapes=scratch,
        compiler_params=_params(),
    )(_rs_view(gwt), g_ws, tiny)


def _adam_update(w, g, m, v):
    nm = ADAM_B1 * m + (1.0 - ADAM_B1) * g
    nv = ADAM_B2 * v + (1.0 - ADAM_B2) * (g * g)
    m_hat = nm / (1.0 - ADAM_B1 ** ADAM_STEP)
    v_hat = nv / (1.0 - ADAM_B2 ** ADAM_STEP)
    return -ADAM_LR * (m_hat / (jnp.sqrt(v_hat) + ADAM_EPS) + ADAM_WD * w), nm, nv


def _adamw(w, g, m, v):
    rows, cols = w.shape
    tm = max(t for t in range(8, 257, 8) if rows % t == 0)

    def body(w_ref, g_ref, m_ref, v_ref, d_ref, nm_ref, nv_ref):
        d_ref[...], nm_ref[...], nv_ref[...] = _adam_update(w_ref[...], g_ref[...], m_ref[...], v_ref[...])

    blk = pl.BlockSpec((tm, cols), lambda i: (i, 0))
    return _call(
        body, name="adamw", grid=(rows // tm,),
        in_specs=[blk] * 4, out_specs=[blk] * 3,
        out_shape=[jax.ShapeDtypeStruct((rows, cols), F32)] * 3,
        compiler_params=_params(),
    )(w, g, m, v)


def _adamw_tiny(tiny, weights, ms, vs):
    shapes = [w.shape for w in weights]
    n = len(weights)

    def grad_of(t_ref, k, shape):
        base = 8 * k
        if shape[1] > LANES:
            return [t_ref[base + j:base + j + 1, :] for j in range(shape[1] // LANES)]
        return [t_ref[base:base + shape[0], 0:shape[1]]]

    def body(t_ref, *refs):
        w_refs, m_refs, v_refs = refs[:n], refs[n:2 * n], refs[2 * n:3 * n]
        loss_ref, outs = refs[3 * n], refs[3 * n + 1:]
        loss_ref[...] = (0.5 / D_MODEL) * jnp.sum(t_ref[8 * n:8 * n + 8, :], keepdims=True)
        for k, shape in enumerate(shapes):
            g_ref, d_ref, nm_ref, nv_ref = outs[4 * k:4 * k + 4]
            for j, g in enumerate(grad_of(t_ref, k, shape)):
                cols = slice(j * LANES, (j + 1) * LANES) if shape[1] > LANES else slice(None)
                g_ref[:, cols] = g
                d_ref[:, cols], nm_ref[:, cols], nv_ref[:, cols] = _adam_update(
                    w_refs[k][:, cols], g, m_refs[k][:, cols], v_refs[k][:, cols])

    out_shape = [jax.ShapeDtypeStruct((1, 1), F32)]
    for shape in shapes:
        out_shape += [jax.ShapeDtypeStruct(shape, F32)] * 4
    return _call(body, name="adamw_tiny", out_shape=out_shape, compiler_params=_params())(tiny, *weights, *ms, *vs)


def _local_grads(x, mem, tgt, norm_gain, wt, gmlp_v_gain, gmlp_w_s, gmlp_b, attn_q_gain, attn_k_gain,
                 mem_norm_gain, wkv, mem_q_gain, mem_k_gain, wo):
    vg = gmlp_v_gain.reshape(1, GMLP_W)
    bias_full = jnp.repeat(gmlp_b.T, HEAD_DIM, axis=1)
    gq2, gk2 = jnp.tile(attn_q_gain, (1, 2)), jnp.tile(attn_k_gain, (1, 2))
    qg4, kg4 = jnp.tile(mem_q_gain, (1, 4)), jnp.tile(mem_k_gain, (1, 4))

    proj = _fwd_proj(x, norm_gain, wt)
    yg = _gmlp_fwd(proj, vg, gmlp_w_s, bias_full)
    o, lse, ya = _attn_fwd(proj, gq2, gk2)
    kv, hm = _mem_kv(mem, mem_norm_gain, wkv)
    om, ym = _mem_fwd(proj, kv, qg4, kg4)
    dy, dyc, g_wo, err2 = _out_loss(yg, ya, ym, x, tgt, wo)
    dmq, dmg, g_mq, g_mk, g_wkv, g_mng = _mem_bwd(proj, om, dyc, kv, hm, mem, mem_norm_gain, wkv, qg4, kg4)
    daq, dak, dav, dag, g_aq, g_ak, g_wkv_sh, g_wo_sh = _attn_bwd(proj, o, lse, dyc, gq2, gk2, g_wkv, g_wo)
    dg, g_ws, g_b, g_vg = _gmlp_bwd(proj, dyc, vg, gmlp_w_s, bias_full)
    gx, g_wt, g_ng = _proj_bwd(x, dy, norm_gain, wt, dg, daq, dak, dav, dag, dmq, dmg)

    tiny = jnp.concatenate([g_ng, g_vg, g_b, g_aq, g_ak, g_mng, g_mq, g_mk, err2], axis=0)
    chip_block = lambda g: g.reshape(2 * g.shape[1], g.shape[2])
    return gx, g_wt, chip_block(g_wkv_sh), chip_block(g_wo_sh), g_ws.reshape(4 * CHUNK, CHUNK), tiny


def kernel(x, mem, norm_gain, w_in, gmlp_v_gain, gmlp_w_s, gmlp_b, attn_q_gain, attn_k_gain, mem_norm_gain, w_mem_kv, mem_q_gain, mem_k_gain, w_out, loss_target, m_norm_gain, m_w_in, m_gmlp_v_gain, m_gmlp_w_s, m_gmlp_b, m_attn_q_gain, m_attn_k_gain, m_mem_norm_gain, m_w_mem_kv, m_mem_q_gain, m_mem_k_gain, m_w_out, v_norm_gain, v_w_in, v_gmlp_v_gain, v_gmlp_w_s, v_gmlp_b, v_attn_q_gain, v_attn_k_gain, v_mem_norm_gain, v_w_mem_kv, v_mem_q_gain, v_mem_k_gain, v_w_out):
    wt, wkv, wo = _gather_weights(w_in[0].T, w_mem_kv[0], w_out[0])
    gx, g_wt, g_wkv_sh, g_wo_sh, g_ws, tiny = _local_grads(
        x[0], mem[0], loss_target[0], norm_gain, wt, gmlp_v_gain[0], gmlp_w_s[0], gmlp_b[0],
        attn_q_gain, attn_k_gain, mem_norm_gain, wkv, mem_q_gain, mem_k_gain, wo)
    g_wt_sh, g_ws, tiny = _reduce_grads(g_wt, g_ws, tiny)
    g_wt_sh = g_wt_sh.reshape(2 * g_wt_sh.shape[1], g_wt_sh.shape[2])

    ws = (norm_gain, w_in, gmlp_v_gain, gmlp_w_s, gmlp_b, attn_q_gain, attn_k_gain, mem_norm_gain, w_mem_kv,
          mem_q_gain, mem_k_gain, w_out)
    ms = (m_norm_gain, m_w_in, m_gmlp_v_gain, m_gmlp_w_s, m_gmlp_b, m_attn_q_gain, m_attn_k_gain, m_mem_norm_gain,
          m_w_mem_kv, m_mem_q_gain, m_mem_k_gain, m_w_out)
    vs = (v_norm_gain, v_w_in, v_gmlp_v_gain, v_gmlp_w_s, v_gmlp_b, v_attn_q_gain, v_attn_k_gain, v_mem_norm_gain,
          v_w_mem_kv, v_mem_q_gain, v_mem_k_gain, v_w_out)
    form = {1: lambda a: a[0].T, 3: lambda a: a.reshape(4 * CHUNK, CHUNK), 2: lambda a: a[0], 4: lambda a: a[0],
            8: lambda a: a[0], 11: lambda a: a[0]}
    back = {1: lambda a: a.T[None], 3: lambda a: a.reshape(1, 4, CHUNK, CHUNK), 2: lambda a: a[None],
            4: lambda a: a[None], 8: lambda a: a[None], 11: lambda a: a[None]}
    fwd = lambda t, i: form.get(i, lambda a: a)(t[i])
    out = {}
    for i, g in ((1, g_wt_sh), (3, g_ws), (8, g_wkv_sh), (11, g_wo_sh)):
        out[i] = (g, *_adamw(fwd(ws, i), g, fwd(ms, i), fwd(vs, i)))
    res = _adamw_tiny(tiny, [fwd(ws, i) for i in TINY_ORDER], [fwd(ms, i) for i in TINY_ORDER],
                      [fwd(vs, i) for i in TINY_ORDER])
    for k, i in enumerate(TINY_ORDER):
        out[i] = res[1 + 4 * k:5 + 4 * k]
    leaves = [[back.get(i, lambda a: a)(out[i][j]) for i in range(12)] for j in range(4)]
    return (res[0].reshape(()), gx[None], *leaves[0], *leaves[1], *leaves[2], *leaves[3])
```

```python
import functools
import math

import jax
import jax.numpy as jnp
from jax import lax
from jax.experimental import pallas as pl
from jax.experimental.pallas import tpu as pltpu

F32 = jnp.float32
BF16 = jnp.bfloat16

SEQ = 4096
D_MODEL = 1024
HEAD_DIM = 64
LANES = 128
CHUNK = 128
GMLP_W, ATTN_W, MEM_W = 256, 512, 256
IN_W = 3 * GMLP_W + 4 * ATTN_W + 2 * MEM_W
MEM_LEN = 256
DILATIONS = (1, 4, 16)
EPS = 1e-6
QK_SCALE = 1.0 / math.sqrt(HEAD_DIM)
C_GU, C_GV, C_GG, C_AQ, C_AK, C_AV, C_AG, C_MQ, C_MG = 0, 256, 512, 768, 1280, 1792, 2304, 2816, 3072

ADAM_LR, ADAM_B1, ADAM_B2, ADAM_EPS, ADAM_WD, ADAM_STEP = 0.001, 0.9, 0.999, 1e-08, 0.01, 10

VMEM_LIMIT = 48 * 1024 * 1024
PB_EARLY = 768
ATTN_UNROLL = 4
MESH = pl.DeviceIdType.MESH

TINY_ORDER = (0, 2, 4, 5, 6, 7, 9, 10)


def _call(body, **kw):
    return pl.pallas_call(body, **kw)


def _params(**kw):
    return pltpu.CompilerParams(vmem_limit_bytes=VMEM_LIMIT, **kw)


def _dot(a, b):
    return jnp.dot(a, b, preferred_element_type=F32)


def _dot_nt(a, b):
    return lax.dot_general(a, b, (((1,), (1,)), ((), ())), preferred_element_type=F32)


def _dot_tn(a, b):
    return lax.dot_general(a, b, (((0,), (0,)), ((), ())), preferred_element_type=F32)


def _head_blockdiag():
    r = lax.shift_right_logical(lax.broadcasted_iota(jnp.int32, (LANES, LANES), 0), 6)
    c = lax.shift_right_logical(lax.broadcasted_iota(jnp.int32, (LANES, LANES), 1), 6)
    return jnp.where(r == c, 1.0, 0.0).astype(BF16)


def _headsum(v, bd):
    hi = v.astype(BF16)
    lo = (v - hi.astype(F32)).astype(BF16)
    return _dot(hi, bd) + _dot(lo, bd)


def _lo_mask(rows):
    return lax.broadcasted_iota(jnp.int32, (rows, LANES), 1) < HEAD_DIM


def _sigmoid(x):
    return 1.0 / (1.0 + jnp.exp(-x))


def _fold_heads(v):
    return v + pltpu.roll(v, HEAD_DIM, 1)


def _put_rows(ref, vec, accumulate=False):
    for j in range(vec.shape[1] // LANES):
        piece = vec[:, j * LANES:(j + 1) * LANES]
        ref[j:j + 1, :] = ref[j:j + 1, :] + piece if accumulate else piece


def _fwd_proj(x, gain, wt):
    tm = 512

    def body(x_ref, g_ref, wt_ref, o_ref):
        xv = x_ref[...]
        ms = jnp.mean(xv * xv, axis=-1, keepdims=True)
        h = (xv * lax.rsqrt(ms + EPS) * g_ref[...]).astype(BF16)
        o_ref[...] = _dot_nt(h, wt_ref[...])

    return _call(
        body, name="fwd_proj", grid=(SEQ // tm,),
        in_specs=[pl.BlockSpec((tm, D_MODEL), lambda i: (i, 0)),
                  pl.BlockSpec((1, D_MODEL), lambda i: (0, 0)),
                  pl.BlockSpec((IN_W, D_MODEL), lambda i: (0, 0))],
        out_specs=pl.BlockSpec((tm, IN_W), lambda i: (i, 0)),
        out_shape=jax.ShapeDtypeStruct((SEQ, IN_W), F32),
        compiler_params=_params(),
    )(x, gain, wt)


def _gmlp_weights(w_ref):
    ti = lax.broadcasted_iota(jnp.int32, (CHUNK, CHUNK), 0)
    si = lax.broadcasted_iota(jnp.int32, (CHUNK, CHUNK), 1)
    tril = si <= ti
    return tril, [jnp.where(tril, w_ref[h], 0.0).astype(BF16) for h in range(4)]


def _gmlp_fwd(proj, vgain, w_s, bias_full):
    tm = 512

    def body(p_ref, vg_ref, w_ref, b_ref, y_ref):
        bd = _head_blockdiag()
        lo = _lo_mask(CHUNK)
        _, wm = _gmlp_weights(w_ref)
        for c in range(tm // CHUNK):
            rows = pl.ds(c * CHUNK, CHUNK)
            for p in range(2):
                cs = slice(p * LANES, (p + 1) * LANES)
                u = p_ref[rows, C_GU + p * LANES:C_GU + (p + 1) * LANES]
                v = p_ref[rows, C_GV + p * LANES:C_GV + (p + 1) * LANES]
                gt = p_ref[rows, C_GG + p * LANES:C_GG + (p + 1) * LANES]
                r = lax.rsqrt(_headsum(v * v, bd) * (1.0 / HEAD_DIM) + EPS)
                vn = (v * r * vg_ref[:, cs]).astype(BF16)
                sp = jnp.where(lo, _dot(wm[2 * p], vn), _dot(wm[2 * p + 1], vn)) + b_ref[:, cs]
                y_ref[rows, cs] = (u * sp * (gt * _sigmoid(gt))).astype(BF16)

    return _call(
        body, name="gmlp_fwd", grid=(SEQ // tm,),
        in_specs=[pl.BlockSpec((tm, 3 * GMLP_W), lambda i: (i, 0)),
                  pl.BlockSpec((1, GMLP_W), lambda i: (0, 0)),
                  pl.BlockSpec((4, CHUNK, CHUNK), lambda i: (0, 0, 0)),
                  pl.BlockSpec((CHUNK, GMLP_W), lambda i: (0, 0))],
        out_specs=pl.BlockSpec((tm, GMLP_W), lambda i: (i, 0)),
        out_shape=jax.ShapeDtypeStruct((SEQ, GMLP_W), BF16),
        compiler_params=_params(),
    )(proj, vgain, w_s, bias_full)


def _gmlp_bwd(proj, dyc, vgain, w_s, bias_full):
    tm = 512
    nsteps = SEQ // tm

    def body(p_ref, dy_ref, vg_ref, w_ref, b_ref, dg_ref, gw_ref, gb_ref, gv_ref):
        i = pl.program_id(0)
        bd = _head_blockdiag()
        lo = _lo_mask(CHUNK)
        tril, wm = _gmlp_weights(w_ref)
        ri = lax.broadcasted_iota(jnp.int32, (16, LANES), 0)
        li = lax.broadcasted_iota(jnp.int32, (16, LANES), 1)
        head_rows = [jnp.where(((ri == 2 * p) & (li < HEAD_DIM)) | ((ri == 2 * p + 1) & (li >= HEAD_DIM)), 1.0, 0.0).astype(BF16)
                     for p in range(2)]

        @pl.when(i == 0)
        def _():
            gw_ref[...] = jnp.zeros_like(gw_ref)
            gb_ref[...] = jnp.zeros_like(gb_ref)
            gv_ref[...] = jnp.zeros_like(gv_ref)

        for c in range(tm // CHUNK):
            rows = pl.ds(c * CHUNK, CHUNK)
            for p in range(2):
                cs = slice(p * LANES, (p + 1) * LANES)
                u = p_ref[rows, C_GU + p * LANES:C_GU + (p + 1) * LANES]
                v = p_ref[rows, C_GV + p * LANES:C_GV + (p + 1) * LANES]
                gt = p_ref[rows, C_GG + p * LANES:C_GG + (p + 1) * LANES]
                dy = dy_ref[rows, cs]
                g = vg_ref[:, cs]
                r = lax.rsqrt(_headsum(v * v, bd) * (1.0 / HEAD_DIM) + EPS)
                z = v * r
                vn = (z * g).astype(BF16)
                sp = jnp.where(lo, _dot(wm[2 * p], vn), _dot(wm[2 * p + 1], vn)) + b_ref[:, cs]
                sg = _sigmoid(gt)
                sl = gt * sg
                dsl = sg * (1.0 + gt * (1.0 - sg))
                du = dy * sp * sl
                dsp = dy * u * sl
                dgt = dy * u * sp * dsl
                dspb = dsp.astype(BF16)
                dvn = jnp.where(lo, _dot_tn(wm[2 * p], dspb), _dot_tn(wm[2 * p + 1], dspb))
                gw_ref[2 * p] += _dot_nt(jnp.where(lo, dsp, 0.0).astype(BF16), vn)
                gw_ref[2 * p + 1] += _dot_nt(jnp.where(lo, 0.0, dsp).astype(BF16), vn)
                dsp_lo = (dsp - dspb.astype(F32)).astype(BF16)
                gb_ref[...] += (_dot_nt(head_rows[p], dspb) + _dot_nt(head_rows[p], dsp_lo))[0:8]
                gvp = jnp.sum(dvn * z, axis=0, keepdims=True)
                gv_ref[2 * p:2 * p + 1, :] += gvp
                gv_ref[2 * p + 1:2 * p + 2, :] += pltpu.roll(gvp, HEAD_DIM, 1)
                dz = dvn * g
                dv = r * (dz - z * (_headsum(dz * z, bd) * (1.0 / HEAD_DIM)))
                dg_ref[rows, C_GU + p * LANES:C_GU + (p + 1) * LANES] = du.astype(BF16)
                dg_ref[rows, C_GV + p * LANES:C_GV + (p + 1) * LANES] = dv.astype(BF16)
                dg_ref[rows, C_GG + p * LANES:C_GG + (p + 1) * LANES] = dgt.astype(BF16)

        @pl.when(i == nsteps - 1)
        def _():
            for h in range(4):
                gw_ref[h] = jnp.where(tril, gw_ref[h], 0.0)

    return _call(
        body, name="gmlp_bwd", grid=(nsteps,),
        in_specs=[pl.BlockSpec((tm, 3 * GMLP_W), lambda i: (i, 0)),
                  pl.BlockSpec((tm, GMLP_W), lambda i: (i, 0)),
                  pl.BlockSpec((1, GMLP_W), lambda i: (0, 0)),
                  pl.BlockSpec((4, CHUNK, CHUNK), lambda i: (0, 0, 0)),
                  pl.BlockSpec((CHUNK, GMLP_W), lambda i: (0, 0))],
        out_specs=[pl.BlockSpec((tm, 3 * GMLP_W), lambda i: (i, 0)),
                   pl.BlockSpec((4, CHUNK, CHUNK), lambda i: (0, 0, 0)),
                   pl.BlockSpec((8, LANES), lambda i: (0, 0)),
                   pl.BlockSpec((8, LANES), lambda i: (0, 0))],
        out_shape=[jax.ShapeDtypeStruct((SEQ, 3 * GMLP_W), BF16),
                   jax.ShapeDtypeStruct((4, CHUNK, CHUNK), F32),
                   jax.ShapeDtypeStruct((8, LANES), F32),
                   jax.ShapeDtypeStruct((8, LANES), F32)],
        compiler_params=_params(),
    )(proj, dyc, vgain, w_s, bias_full)


def _band_masks():
    qi = lax.broadcasted_iota(jnp.int32, (CHUNK, 2 * CHUNK), 0)
    kj = lax.broadcasted_iota(jnp.int32, (CHUNK, 2 * CHUNK), 1)
    valid2 = ((kj < CHUNK) & (kj >= qi)) | ((kj >= CHUNK) & (kj - CHUNK <= qi))
    q1 = lax.broadcasted_iota(jnp.int32, (CHUNK, CHUNK), 0)
    k1 = lax.broadcasted_iota(jnp.int32, (CHUNK, CHUNK), 1)
    return k1 <= q1, valid2


def _stack_heads(v, lo):
    return jnp.concatenate([jnp.where(lo, v, 0.0), jnp.where(lo, 0.0, v)], axis=0).astype(BF16)


def _rows_of(ref, start, d):
    if d == 1:
        return ref.at[pl.ds(start if isinstance(start, int) else pl.multiple_of(start, CHUNK), CHUNK), :]
    return ref.at[pl.ds(start, CHUNK, stride=d), :]


def _unrolled(lo, hi, unroll, run):
    groups = (hi - lo) // unroll
    if groups:
        def body(g, carry):
            run([lo + g * unroll + t for t in range(unroll)])
            return carry

        lax.fori_loop(0, groups, body, 0)
    if lo + groups * unroll < hi:
        run(range(lo + groups * unroll, hi))


def _for_blocks(d, group_fn, unroll):
    nblk = SEQ // CHUNK
    sh = d.bit_length() - 1

    def first(j):
        return (j * CHUNK if d == 1 else j, None)

    def rest(j):
        start = (j & (d - 1)) + (j >> sh) * (CHUNK * d)
        return (start, start - CHUNK * d)

    _unrolled(0, d, unroll, lambda js: group_fn(d, [first(j) for j in js]))
    _unrolled(d, nblk, unroll, lambda js: group_fn(d, [rest(j) for j in js]))


def _attn_fwd(proj, gq2, gk2):
    tn = 512

    def body(q_ref, k_ref, v_ref, g_ref, gq_ref, gk_ref, o_ref, l_ref, ya_ref, qn_ref, kn_ref):
        bd = _head_blockdiag()
        lo = _lo_mask(CHUNK)
        valid1, valid2 = _band_masks()

        def norm(t, carry):
            rows = pl.ds(pl.multiple_of(t * tn, tn), tn)
            q = q_ref[rows, :]
            qn_ref[rows, :] = q * lax.rsqrt(_headsum(q * q, bd) * (1.0 / HEAD_DIM) + EPS) * (gq_ref[...] * QK_SCALE)
            k = k_ref[rows, :]
            kn_ref[rows, :] = k * lax.rsqrt(_headsum(k * k, bd) * (1.0 / HEAD_DIM) + EPS) * gk_ref[...]
            return carry

        lax.fori_loop(0, SEQ // tn, norm, 0)

        def load_kv(ref, d, start, prev):
            own = _rows_of(ref, start, d)[...]
            if prev is None:
                return own.astype(BF16)
            return jnp.concatenate([_rows_of(ref, prev, d)[...], own], axis=0).astype(BF16)

        def group(d, blocks):
            valid = valid1 if blocks[0][1] is None else valid2
            valid = jnp.concatenate([valid, valid], axis=0)
            qs = [_rows_of(qn_ref, start, d)[...] for start, _ in blocks]
            ks = [load_kv(kn_ref, d, start, prev) for start, prev in blocks]
            vs = [load_kv(v_ref, d, start, prev) for start, prev in blocks]
            ss = [_dot_nt(_stack_heads(q, lo), k) for q, k in zip(qs, ks)]
            ms, ps, ls = [], [], []
            for s in ss:
                s = jnp.where(valid, s, -jnp.inf)
                m = jnp.max(s, axis=-1, keepdims=True)
                p = jnp.exp(s - m)
                ms.append(m)
                ls.append(jnp.sum(p, axis=-1, keepdims=True))
                ps.append(p.astype(BF16))
            os_ = [_dot(p, v) for p, v in zip(ps, vs)]
            for b, (start, _) in enumerate(blocks):
                on = os_[b] * (1.0 / ls[b])
                ln = ms[b] + jnp.log(ls[b])
                ob = jnp.where(lo, on[:CHUNK], on[CHUNK:])
                lb = jnp.where(lo, ln[:CHUNK], ln[CHUNK:])
                o_rows = _rows_of(o_ref, start, d)
                l_rows = _rows_of(l_ref, start, d)
                if d != DILATIONS[0]:
                    lold = l_rows[...]
                    mx = jnp.maximum(lold, lb)
                    ea = jnp.exp(lold - mx)
                    eb = jnp.exp(lb - mx)
                    inv = 1.0 / (ea + eb)
                    ob = o_rows[...] * (ea * inv) + ob * (eb * inv)
                    lb = mx + jnp.log(ea + eb)
                o_rows[...] = ob
                l_rows[...] = lb

        for d in DILATIONS:
            _for_blocks(d, group, ATTN_UNROLL)

        def fin(t, carry):
            rows = pl.ds(pl.multiple_of(t * tn, tn), tn)
            g = g_ref[rows, :]
            ya_ref[rows, :] = (o_ref[rows, :] * (g * _sigmoid(g))).astype(BF16)
            return carry

        lax.fori_loop(0, SEQ // tn, fin, 0)

    col = lambda c0: pl.BlockSpec((SEQ, LANES), lambda p: (0, c0 // LANES + p))
    vec = pl.BlockSpec((1, LANES), lambda p: (0, 0))
    out = pl.BlockSpec((SEQ, LANES), lambda p: (0, p))
    return _call(
        body, name="attn_fwd", grid=(ATTN_W // LANES,),
        in_specs=[col(C_AQ), col(C_AK), col(C_AV), col(C_AG), vec, vec],
        out_specs=[out, out, out],
        out_shape=[jax.ShapeDtypeStruct((SEQ, ATTN_W), F32), jax.ShapeDtypeStruct((SEQ, ATTN_W), F32),
                   jax.ShapeDtypeStruct((SEQ, ATTN_W), BF16)],
        scratch_shapes=[pltpu.VMEM((SEQ, LANES), F32), pltpu.VMEM((SEQ, LANES), F32)],
        compiler_params=_params(),
    )(proj, proj, proj, proj, gq2, gk2)


def _attn_bwd(proj, o, lse, dyc, gq2, gk2, *ride_along):
    tn = 512
    npairs = ATTN_W // LANES
    nride = len(ride_along)
    nbufs = nride * len(RS_KINDS)

    def body(proj_hbm, o_hbm, l_hbm, dyc_hbm, gq_ref, gk_ref, *rest):
        ride_in, rest = rest[:nride], rest[nride:]
        dq_ref, dk_ref, dv_ref, dgt_ref, gqg_ref, gkg_ref = rest[:6]
        ride_out, rest = rest[6:6 + nride], rest[6 + nride:]
        qb_, kb_, vb_, gb_, ob_, lb_, yb_, dkb_, dvb_, sems = rest[:10]
        rs_bufs, (send_sems, recv_sems, local_sems) = rest[10:10 + nbufs], rest[10 + nbufs:]
        rs_stage = _rs_stages(ride_in, ride_out, rs_bufs, send_sems, recv_sems, local_sems, [g.shape[1] for g in ride_along])
        pair = pl.program_id(0)
        for step in range(npairs):
            pl.when(pair == step)(rs_stage[step])
        bd = _head_blockdiag()
        lo = _lo_mask(CHUNK)
        lo2 = lax.broadcasted_iota(jnp.int32, (2 * CHUNK, LANES), 1) < HEAD_DIM
        valid1, valid2 = _band_masks()
        gqs = gq_ref[...] * QK_SCALE
        gk = gk_ref[...]

        def pcol(c0):
            return proj_hbm.at[:, pl.ds(pl.multiple_of(c0 + pair * LANES, LANES), LANES)]

        def acol(hbm, c0=0):
            return hbm.at[:, pl.ds(pl.multiple_of(c0 + pair * LANES, LANES), LANES)]

        loads = [pltpu.make_async_copy(src, dst, sems.at[n]) for n, (src, dst) in enumerate((
            (pcol(C_AQ), qb_), (pcol(C_AK), kb_), (pcol(C_AG), gb_), (acol(o_hbm), ob_),
            (acol(dyc_hbm, GMLP_W), yb_), (pcol(C_AV), vb_), (acol(l_hbm), lb_)))]
        for cp in loads:
            cp.start()

        @pl.when(pair == 0)
        def _():
            gqg_ref[...] = jnp.zeros_like(gqg_ref)
            gkg_ref[...] = jnp.zeros_like(gkg_ref)

        def pre_qk(t, carry):
            rows = pl.ds(pl.multiple_of(t * tn, tn), tn)
            zero = jnp.zeros((tn, LANES), F32)
            dkb_[rows, :] = zero
            dvb_[rows, :] = zero
            q = qb_[rows, :]
            qb_[rows, :] = q * lax.rsqrt(_headsum(q * q, bd) * (1.0 / HEAD_DIM) + EPS) * gqs
            k = kb_[rows, :]
            kb_[rows, :] = k * lax.rsqrt(_headsum(k * k, bd) * (1.0 / HEAD_DIM) + EPS) * gk
            return carry

        def pre_gate(t, carry):
            rows = pl.ds(pl.multiple_of(t * tn, tn), tn)
            g = gb_[rows, :]
            ov = ob_[rows, :]
            dya = yb_[rows, :]
            sg = _sigmoid(g)
            dgt_ref[rows, :] = (dya * ov * (sg * (1.0 + g * (1.0 - sg)))).astype(BF16)
            do = dya * (g * sg)
            yb_[rows, :] = do
            ob_[rows, :] = _headsum(do * ov, bd)
            gb_[rows, :] = jnp.zeros((tn, LANES), F32)
            return carry

        loads[0].wait()
        loads[1].wait()
        lax.fori_loop(0, SEQ // tn, pre_qk, 0)
        for cp in loads[2:5]:
            cp.wait()
        lax.fori_loop(0, SEQ // tn, pre_gate, 0)
        loads[5].wait()
        loads[6].wait()

        def load_kv(ref, d, start, prev):
            own = _rows_of(ref, start, d)[...]
            if prev is None:
                return own.astype(BF16)
            return jnp.concatenate([_rows_of(ref, prev, d)[...], own], axis=0).astype(BF16)

        def group(d, blocks):
            first = blocks[0][1] is None
            valid, lok = (valid1, lo) if first else (valid2, lo2)
            chains = [(b, h) for b in range(len(blocks)) for h in range(2)]
            mask = lambda h: lo if h == 0 else ~lo
            qs = [_rows_of(qb_, start, d)[...] for start, _ in blocks]
            dos = [_rows_of(yb_, start, d)[...] for start, _ in blocks]
            lvs = [_rows_of(lb_, start, d)[...] for start, _ in blocks]
            dls = [_rows_of(ob_, start, d)[...] for start, _ in blocks]
            ks = [load_kv(kb_, d, start, prev) for start, prev in blocks]
            vs = [load_kv(vb_, d, start, prev) for start, prev in blocks]
            qbs = [q.astype(BF16) for q in qs]
            dobs = [do.astype(BF16) for do in dos]
            ss = [_dot_nt(jnp.where(mask(h), qs[b], 0.0).astype(BF16), ks[b]) for b, h in chains]
            dps = [_dot_nt(jnp.where(mask(h), dos[b], 0.0).astype(BF16), vs[b]) for b, h in chains]
            pbs, dss = [], []
            for s, dp, (b, h) in zip(ss, dps, chains):
                hc = h * HEAD_DIM
                p = jnp.exp(jnp.where(valid, s, -jnp.inf) - lvs[b][:, hc:hc + 1])
                pbs.append(p.astype(BF16))
                dss.append((p * (dp - dls[b][:, hc:hc + 1])).astype(BF16))
            dqs = [_dot(ds, ks[b]) for ds, (b, h) in zip(dss, chains)]
            dks = [_dot_tn(ds, qbs[b]) for ds, (b, h) in zip(dss, chains)]
            dvs = [_dot_tn(p, dobs[b]) for p, (b, h) in zip(pbs, chains)]
            for b, (start, prev) in enumerate(blocks):
                c0, c1 = 2 * b, 2 * b + 1
                dq_rows = _rows_of(gb_, start, d)
                dq_rows[...] = dq_rows[...] + jnp.where(lo, dqs[c0], dqs[c1])
                dkc = jnp.where(lok, dks[c0], dks[c1])
                dvc = jnp.where(lok, dvs[c0], dvs[c1])
                spans = ((start, slice(0, CHUNK)),) if first else ((prev, slice(0, CHUNK)), (start, slice(CHUNK, 2 * CHUNK)))
                for st, sl in spans:
                    dk_rows = _rows_of(dkb_, st, d)
                    dk_rows[...] = dk_rows[...] + dkc[sl]
                    dv_rows = _rows_of(dvb_, st, d)
                    dv_rows[...] = dv_rows[...] + dvc[sl]

        for d in DILATIONS:
            _for_blocks(d, group, ATTN_UNROLL)

        pltpu.sync_copy(pcol(C_AQ), qb_)
        pltpu.sync_copy(pcol(C_AK), kb_)

        def post(t, carry):
            gq_acc, gk_acc = carry
            rows = pl.ds(pl.multiple_of(t * tn, tn), tn)
            outs = []
            for raw_, acc_, gain in ((qb_, gb_, gqs), (kb_, dkb_, gk)):
                a = raw_[rows, :]
                r = lax.rsqrt(_headsum(a * a, bd) * (1.0 / HEAD_DIM) + EPS)
                z = a * r
                dn = acc_[rows, :]
                dz = dn * gain
                outs.append((r * (dz - z * (_headsum(dz * z, bd) * (1.0 / HEAD_DIM))), jnp.sum(dn * z, axis=0, keepdims=True)))
            dq_ref[rows, :] = outs[0][0].astype(BF16)
            dk_ref[rows, :] = outs[1][0].astype(BF16)
            dv_ref[rows, :] = dvb_[rows, :].astype(BF16)
            return gq_acc + outs[0][1] * QK_SCALE, gk_acc + outs[1][1]

        zero = jnp.zeros((1, LANES), F32)
        gq_acc, gk_acc = lax.fori_loop(0, SEQ // tn, post, (zero, zero))
        gqg_ref[0:1, :] += gq_acc
        gkg_ref[0:1, :] += gk_acc

        @pl.when(pair == npairs - 1)
        def _():
            gqg_ref[0:1, :] = _fold_heads(gqg_ref[0:1, :])
            gkg_ref[0:1, :] = _fold_heads(gkg_ref[0:1, :])
            rs_stage[npairs]()

    hbm = pl.BlockSpec(memory_space=pl.ANY)
    vec = pl.BlockSpec((1, LANES), lambda p: (0, 0))
    blk8 = pl.BlockSpec((8, LANES), lambda p: (0, 0))
    out = pl.BlockSpec((SEQ, LANES), lambda p: (0, p))
    big = jax.ShapeDtypeStruct((SEQ, ATTN_W), BF16)
    nsem = RS_SEMS * nride
    return _call(
        body, name="attn_bwd", grid=(npairs,),
        in_specs=[hbm, hbm, hbm, hbm, vec, vec] + [hbm] * nride,
        out_specs=[out, out, out, out, blk8, blk8] + [hbm] * nride,
        out_shape=[big, big, big, big, jax.ShapeDtypeStruct((8, LANES), F32), jax.ShapeDtypeStruct((8, LANES), F32)]
        + [jax.ShapeDtypeStruct((2, g.shape[0] // 8, g.shape[1]), F32) for g in ride_along],
        scratch_shapes=[pltpu.VMEM((SEQ, LANES), F32) for _ in range(9)] + [pltpu.SemaphoreType.DMA((7,))]
        + _rs_scratch([g.shape for g in ride_along]) + [pltpu.SemaphoreType.DMA((nsem,)), pltpu.SemaphoreType.DMA((nsem,)),
                                     pltpu.SemaphoreType.DMA((nride,))],
        compiler_params=_params(),
    )(proj, o, lse, dyc, gq2, gk2, *[_rs_view(g) for g in ride_along])


def _mem_kv(mem, gain, wkv):
    def body(m_ref, g_ref, w_ref, kv_ref, hm_ref):
        mv = m_ref[...]
        ms = jnp.mean(mv * mv, axis=-1, keepdims=True)
        hm = (mv * lax.rsqrt(ms + EPS) * g_ref[...]).astype(BF16)
        hm_ref[...] = hm
        kv_ref[...] = _dot(hm, w_ref[...])

    return _call(
        body, name="mem_kv",
        out_shape=[jax.ShapeDtypeStruct((MEM_LEN, 2 * MEM_W), F32), jax.ShapeDtypeStruct((MEM_LEN, D_MODEL), BF16)],
        compiler_params=_params(),
    )(mem, gain, wkv)


def _mem_keys(kv_ref, kg_ref, bd, p):
    mk = kv_ref[:, p * LANES:(p + 1) * LANES]
    r = lax.rsqrt(_headsum(mk * mk, bd) * (1.0 / HEAD_DIM) + EPS)
    z = mk * r
    mkn = (z * kg_ref[:, p * LANES:(p + 1) * LANES]).astype(BF16)
    mvp = kv_ref[:, MEM_W + p * LANES:MEM_W + (p + 1) * LANES].astype(BF16)
    return mkn, mvp, r, z


def _mem_fwd(proj, kv, qg4, kg4):
    tm = 512

    def body(q_ref, g_ref, kv_ref, qg_ref, kg_ref, om_ref, ym_ref):
        bd = _head_blockdiag()
        lo = _lo_mask(tm)
        for p in range(2):
            cs = slice(p * LANES, (p + 1) * LANES)
            mkn, mvp, _, _ = _mem_keys(kv_ref, kg_ref, bd, p)
            q = q_ref[:, cs]
            qn = q * lax.rsqrt(_headsum(q * q, bd) * (1.0 / HEAD_DIM) + EPS) * (qg_ref[:, cs] * QK_SCALE)
            res = []
            for h in range(2):
                qh = jnp.where(lo if h == 0 else ~lo, qn, 0.0).astype(BF16)
                s = _dot_nt(qh, mkn)
                e = jnp.exp(s - jnp.max(s, axis=-1, keepdims=True))
                res.append(_dot(e.astype(BF16), mvp) * (1.0 / jnp.sum(e, axis=-1, keepdims=True)))
            ov = jnp.where(lo, res[0], res[1])
            g = g_ref[:, cs]
            om_ref[:, cs] = ov
            ym_ref[:, cs] = (ov * (g * _sigmoid(g))).astype(BF16)

    vec = pl.BlockSpec((1, MEM_W), lambda i: (0, 0))
    return _call(
        body, name="mem_fwd", grid=(SEQ // tm,),
        in_specs=[pl.BlockSpec((tm, MEM_W), lambda i: (i, C_MQ // MEM_W)),
                  pl.BlockSpec((tm, MEM_W), lambda i: (i, C_MG // MEM_W)),
                  pl.BlockSpec((MEM_LEN, 2 * MEM_W), lambda i: (0, 0)), vec, vec],
        out_specs=[pl.BlockSpec((tm, MEM_W), lambda i: (i, 0)), pl.BlockSpec((tm, MEM_W), lambda i: (i, 0))],
        out_shape=[jax.ShapeDtypeStruct((SEQ, MEM_W), F32), jax.ShapeDtypeStruct((SEQ, MEM_W), BF16)],
        compiler_params=_params(),
    )(proj, proj, kv, qg4, kg4)


def _mem_bwd(proj, om, dyc, kv, hm, mem, mgain, wkv, qg4, kg4):
    tm = 512
    nsteps = SEQ // tm

    def body(q_ref, g_ref, om_ref, dy_ref, kv_ref, hm_ref, mem_ref, mg_ref, w_ref, qg_ref, kg_ref,
             dq_ref, dgt_ref, gqg_ref, gkg_ref, gw_ref, gmg_ref, dmk_ref, dmv_ref, gq_acc):
        i = pl.program_id(0)
        bd = _head_blockdiag()
        lo = _lo_mask(tm)
        lom = _lo_mask(MEM_LEN)

        @pl.when(i == 0)
        def _():
            dmk_ref[...] = jnp.zeros_like(dmk_ref)
            dmv_ref[...] = jnp.zeros_like(dmv_ref)
            gq_acc[...] = jnp.zeros_like(gq_acc)

        for p in range(2):
            cs = slice(p * LANES, (p + 1) * LANES)
            mkn, mvp, _, _ = _mem_keys(kv_ref, kg_ref, bd, p)
            gqs = qg_ref[:, cs] * QK_SCALE
            q = q_ref[:, cs]
            r = lax.rsqrt(_headsum(q * q, bd) * (1.0 / HEAD_DIM) + EPS)
            z = q * r
            qn = z * gqs
            qnb = qn.astype(BF16)
            g = g_ref[:, cs]
            ov = om_ref[:, cs]
            dym = dy_ref[:, cs]
            sg = _sigmoid(g)
            dgt_ref[:, cs] = (dym * ov * (sg * (1.0 + g * (1.0 - sg)))).astype(BF16)
            do = dym * (g * sg)
            dob = do.astype(BF16)
            delta = _headsum(do * ov, bd)
            parts = []
            for h in range(2):
                mh = lo if h == 0 else ~lo
                hc = h * HEAD_DIM
                qh = jnp.where(mh, qn, 0.0).astype(BF16)
                doh = jnp.where(mh, do, 0.0).astype(BF16)
                s = _dot_nt(qh, mkn)
                e = jnp.exp(s - jnp.max(s, axis=-1, keepdims=True))
                pr = e * (1.0 / jnp.sum(e, axis=-1, keepdims=True))
                dp = _dot_nt(doh, mvp)
                ds = (pr * (dp - delta[:, hc:hc + 1])).astype(BF16)
                parts.append((_dot(ds, mkn), _dot_tn(ds, qnb), _dot_tn(pr.astype(BF16), dob)))
            dqn = jnp.where(lo, parts[0][0], parts[1][0])
            dmk_ref[:, cs] += jnp.where(lom, parts[0][1], parts[1][1])
            dmv_ref[:, cs] += jnp.where(lom, parts[0][2], parts[1][2])
            dz = dqn * gqs
            dq_ref[:, cs] = (r * (dz - z * (_headsum(dz * z, bd) * (1.0 / HEAD_DIM)))).astype(BF16)
            gq_acc[:, cs] += jnp.sum(dqn * z, axis=0, keepdims=True) * QK_SCALE

        @pl.when(i == nsteps - 1)
        def _():
            gqg_ref[...] = jnp.zeros_like(gqg_ref)
            gkg_ref[...] = jnp.zeros_like(gkg_ref)
            gqg_ref[0:1, :] = _fold_heads(gq_acc[:, 0:LANES] + gq_acc[:, LANES:2 * LANES])
            dkv = []
            gk = jnp.zeros((1, LANES), F32)
            for p in range(2):
                cs = slice(p * LANES, (p + 1) * LANES)
                _, _, r, z = _mem_keys(kv_ref, kg_ref, bd, p)
                dn = dmk_ref[:, cs]
                dz = dn * kg_ref[:, cs]
                gk = gk + jnp.sum(dn * z, axis=0, keepdims=True)
                dkv.append(r * (dz - z * (_headsum(dz * z, bd) * (1.0 / HEAD_DIM))))
            gkg_ref[0:1, :] = _fold_heads(gk)
            dkvb = jnp.concatenate(dkv + [dmv_ref[...]], axis=1).astype(BF16)
            gw_ref[...] = _dot_tn(hm_ref[...], dkvb)
            dhm = _dot_nt(dkvb, w_ref[...])
            mv = mem_ref[...]
            zm = mv * lax.rsqrt(jnp.mean(mv * mv, axis=-1, keepdims=True) + EPS)
            _put_rows(gmg_ref, jnp.sum(dhm * zm, axis=0, keepdims=True))

    const = lambda shape: pl.BlockSpec(shape, lambda i: (0,) * len(shape))
    row = lambda j: pl.BlockSpec((tm, MEM_W), lambda i: (i, j))
    blk8 = jax.ShapeDtypeStruct((8, LANES), F32)
    return _call(
        body, name="mem_bwd", grid=(nsteps,),
        in_specs=[row(C_MQ // MEM_W), row(C_MG // MEM_W), row(0), row((GMLP_W + ATTN_W) // MEM_W),
                  const((MEM_LEN, 2 * MEM_W)), const((MEM_LEN, D_MODEL)), const((MEM_LEN, D_MODEL)),
                  const((1, D_MODEL)), const((D_MODEL, 2 * MEM_W)), const((1, MEM_W)), const((1, MEM_W))],
        out_specs=[row(0), row(0), const((8, LANES)), const((8, LANES)),
                   const((D_MODEL, 2 * MEM_W)), const((8, LANES))],
        out_shape=[jax.ShapeDtypeStruct((SEQ, MEM_W), BF16), jax.ShapeDtypeStruct((SEQ, MEM_W), BF16),
                   blk8, blk8, jax.ShapeDtypeStruct((D_MODEL, 2 * MEM_W), F32), blk8],
        scratch_shapes=[pltpu.VMEM((MEM_LEN, MEM_W), F32), pltpu.VMEM((MEM_LEN, MEM_W), F32),
                        pltpu.VMEM((1, MEM_W), F32)],
        compiler_params=_params(),
    )(proj, proj, om, dyc, kv, hm, mem, mgain, wkv, qg4, kg4)


def _out_loss(yg, ya, ym, x, tgt, wo):
    tm = 512
    nsteps = SEQ // tm
    parts = ((0, GMLP_W), (GMLP_W, ATTN_W), (GMLP_W + ATTN_W, MEM_W))

    def body(yg_ref, ya_ref, ym_ref, x_ref, t_ref, w_ref, dy_ref, dyc_ref, gw_ref, ls_ref):
        i = pl.program_id(0)

        @pl.when(i == 0)
        def _():
            gw_ref[...] = jnp.zeros_like(gw_ref)
            ls_ref[...] = jnp.zeros_like(ls_ref)

        ys = (yg_ref[...], ya_ref[...], ym_ref[...])
        y = sum(_dot(yv, w_ref[r0:r0 + n, :]) for yv, (r0, n) in zip(ys, parts))
        err = x_ref[...] + y - t_ref[...]
        _put_rows(ls_ref, jnp.sum(err * err, axis=0, keepdims=True), accumulate=True)
        dy = err * (1.0 / D_MODEL)
        dy_ref[...] = dy
        dyb = dy.astype(BF16)
        dyc_ref[...] = _dot_nt(dyb, w_ref[...])
        for yv, (r0, n) in zip(ys, parts):
            gw_ref[r0:r0 + n, :] += _dot_tn(yv, dyb)

    row = lambda w: pl.BlockSpec((tm, w), lambda i: (i, 0))
    const = lambda shape: pl.BlockSpec(shape, lambda i: (0, 0))
    return _call(
        body, name="out_loss", grid=(nsteps,),
        in_specs=[row(GMLP_W), row(ATTN_W), row(MEM_W), row(D_MODEL), row(D_MODEL), const((D_MODEL, D_MODEL))],
        out_specs=[row(D_MODEL), row(D_MODEL), const((D_MODEL, D_MODEL)), const((8, LANES))],
        out_shape=[jax.ShapeDtypeStruct((SEQ, D_MODEL), F32), jax.ShapeDtypeStruct((SEQ, D_MODEL), F32),
                   jax.ShapeDtypeStruct((D_MODEL, D_MODEL), F32), jax.ShapeDtypeStruct((8, LANES), F32)],
        compiler_params=_params(),
    )(yg, ya, ym, x, tgt, wo)


def _proj_bwd(x, dy, gain, wt, dg, daq, dak, dav, dag, dmq, dmg):
    tm = 256
    nsteps = SEQ // tm
    late_w = D_MODEL - PB_EARLY
    stage_at = (0, 4, 8, 12)
    pieces = ((C_GU, 3 * GMLP_W), (C_AQ, ATTN_W), (C_AK, ATTN_W), (C_AV, ATTN_W), (C_AG, ATTN_W),
              (C_MQ, MEM_W), (C_MG, MEM_W))
    nbufs = len(RS_KINDS) - 1

    def body(x_ref, dy_ref, g_ref, wt_hbm, p0, p1, p2, p3, p4, p5, p6, gx_ref, early_hbm, late_hbm, gg_ref,
             wt_v, acc_e, acc_l, *rest):
        rs_bufs, (send_sems, recv_sems, local_sems, wt_sem) = rest[:nbufs], rest[nbufs:]
        phase, i = pl.program_id(0), pl.program_id(1)
        prefs = (p0, p1, p2, p3, p4, p5, p6)

        def half_block(xx, yy, half):
            n = IN_W // 8
            return acc_e.at[pl.ds(pl.multiple_of((2 * xx + yy) * 2 * n + half * n, 8), n), :]

        stages = _rs_stages((half_block,), (early_hbm,), rs_bufs, send_sems, recv_sems, local_sems, [PB_EARLY])
        wt_load = pltpu.make_async_copy(wt_hbm, wt_v, wt_sem)

        @pl.when((phase == 0) & (i == 0))
        def _():
            wt_load.start()
            acc_e[...] = jnp.zeros_like(acc_e)
            acc_l[...] = jnp.zeros_like(acc_l)
            gg_ref[...] = jnp.zeros_like(gg_ref)

        xv = x_ref[...]
        r = lax.rsqrt(jnp.mean(xv * xv, axis=-1, keepdims=True) + EPS)
        z = xv * r
        g = g_ref[...]
        h = (z * g).astype(BF16)

        @pl.when(phase == 0)
        def _():
            for pref, (c0, w) in zip(prefs, pieces):
                acc_e[c0:c0 + w, :] += _dot_tn(pref[...], h[:, :PB_EARLY])

        @pl.when(phase == 1)
        def _():
            pl.when(i == 0)(wt_load.wait)
            for k, at in enumerate(stage_at):
                pl.when(i == at)(stages[k])
            dh = jnp.zeros((tm, D_MODEL), F32)
            for pref, (c0, w) in zip(prefs, pieces):
                dp = pref[...]
                dh = dh + _dot(dp, wt_v[c0:c0 + w, :])
                acc_l[c0:c0 + w, :] += _dot_tn(dp, h[:, PB_EARLY:])
            _put_rows(gg_ref, jnp.sum(dh * z, axis=0, keepdims=True), accumulate=True)
            dz = dh * g
            gx_ref[...] = dy_ref[...] + r * (dz - z * jnp.mean(dz * z, axis=-1, keepdims=True))

            @pl.when(i == nsteps - 1)
            def _():
                stages[len(stage_at)]()
                pltpu.sync_copy(acc_l, late_hbm)

    row = lambda w: pl.BlockSpec((tm, w), lambda p, i: (i, 0))
    second = pl.BlockSpec((tm, D_MODEL), lambda p, i: (i * p, 0))
    hbm = pl.BlockSpec(memory_space=pl.ANY)
    return _call(
        body, name="proj_bwd", grid=(2, nsteps),
        in_specs=[row(D_MODEL), second, pl.BlockSpec((1, D_MODEL), lambda p, i: (0, 0)), hbm] + [row(w) for _, w in pieces],
        out_specs=[second, hbm, hbm, pl.BlockSpec((8, LANES), lambda p, i: (0, 0))],
        out_shape=[jax.ShapeDtypeStruct((SEQ, D_MODEL), F32), jax.ShapeDtypeStruct((2, IN_W // 8, PB_EARLY), F32),
                   jax.ShapeDtypeStruct((IN_W, late_w), F32), jax.ShapeDtypeStruct((8, LANES), F32)],
        scratch_shapes=[pltpu.VMEM((IN_W, D_MODEL), BF16), pltpu.VMEM((IN_W, PB_EARLY), F32), pltpu.VMEM((IN_W, late_w), F32)]
        + _rs_scratch([(IN_W, PB_EARLY)], in_vmem=True)
        + [pltpu.SemaphoreType.DMA((RS_SEMS,)), pltpu.SemaphoreType.DMA((RS_SEMS,)), pltpu.SemaphoreType.DMA((1,)),
           pltpu.SemaphoreType.DMA],
        compiler_params=_params(),
    )(x, dy, gain, wt, dg, daq, dak, dav, dag, dmq, dmg)


def _gather_weights(wt_sh, wkv_sh, wo_sh):
    shards = (wt_sh, wkv_sh, wo_sh)
    nrows = tuple(a.shape[0] for a in shards)

    def body(a0, a1, a2, o0, o1, o2, send_sems, recv_sems):
        x, y, c = lax.axis_index("x"), lax.axis_index("y"), lax.axis_index("c")
        sib, xn, yn = (x, y, 1 - c), (1 - x, y, c), (x, 1 - y, c)
        me, cx, cy, cd = 2 * x + y, 2 * (1 - x) + y, 2 * x + (1 - y), 2 * (1 - x) + (1 - y)
        ins, outs = (a0, a1, a2), (o0, o1, o2)

        def part(a, chip, hf, quarter=None):
            n = nrows[a] // 2
            base = chip * nrows[a] + hf * n
            if quarter is not None:
                n = n // 2
                base = base + quarter * n
            return outs[a].at[pl.ds(pl.multiple_of(base, 16), n), :]

        def copy(k, ref, to):
            return pltpu.make_async_remote_copy(src_ref=ref, dst_ref=ref, send_sem=send_sems.at[k],
                                                recv_sem=recv_sems.at[k], device_id=to, device_id_type=MESH)

        started = []

        def go(cp):
            cp.start()
            started.append(cp)

        for a in range(3):
            outs[a][pl.ds(pl.multiple_of(me * nrows[a], 16), nrows[a]), :] = ins[a][...].astype(BF16)
        for a in range(3):
            go(copy(8 * a, part(a, me, c), xn))
            go(copy(8 * a + 1, part(a, me, c), yn))
        for a in range(3):
            k = 8 * a
            copy(k, part(a, cx, c), xn).wait_recv()
            go(copy(k + 4, part(a, cx, c, 1), yn))
            go(copy(k + 2, part(a, cx, c), sib))
            copy(k + 1, part(a, cy, c), yn).wait_recv()
            go(copy(k + 5, part(a, cy, c, 0), xn))
            go(copy(k + 3, part(a, cy, c), sib))
        for a in range(3):
            k = 8 * a
            copy(k + 4, part(a, cd, c, 1), yn).wait_recv()
            go(copy(k + 7, part(a, cd, c, 1), sib))
            copy(k + 5, part(a, cd, c, 0), xn).wait_recv()
            go(copy(k + 6, part(a, cd, c, 0), sib))
        for a in range(3):
            k = 8 * a
            copy(k + 2, part(a, cx, 1 - c), sib).wait_recv()
            copy(k + 3, part(a, cy, 1 - c), sib).wait_recv()
            copy(k + 6, part(a, cd, 1 - c, 0), sib).wait_recv()
            copy(k + 7, part(a, cd, 1 - c, 1), sib).wait_recv()
        for cp in started:
            cp.wait_send()

    return _call(
        body, name="gather_weights",
        out_shape=[jax.ShapeDtypeStruct((4 * a.shape[0], a.shape[1]), BF16) for a in shards],
        in_specs=[pl.BlockSpec(memory_space=pltpu.VMEM)] * 3,
        out_specs=[pl.BlockSpec(memory_space=pltpu.VMEM)] * 3,
        scratch_shapes=[pltpu.SemaphoreType.DMA((24,)), pltpu.SemaphoreType.DMA((24,))],
        compiler_params=_params(),
    )(*shards)


RS_SEMS = 6
RS_KINDS = (((2, 2), 1, F32), ((2, 2), 1, F32), ((2, 2), 2, BF16), ((2, 2), 2, BF16), ((2, 2), 2, F32),
            ((2,), 2, BF16), ((2,), 2, BF16), ((2,), 1, F32))


def _rs_view(g):
    return g.reshape(2, 2, 2, g.shape[0] // 8, g.shape[1])


def _rs_scratch(shapes, in_vmem=False):
    kinds = RS_KINDS[1:] if in_vmem else RS_KINDS
    return [pltpu.VMEM(lead + (r // 8, w // split), dt) for lead, split, dt in kinds for r, w in shapes]


def _rs_stages(gs, outs, bufs, send_sems, recv_sems, local_sems, widths):
    n = len(gs)
    if len(bufs) < n * len(RS_KINDS):
        bufs = [None] * n + list(bufs)
    loc, ra, s_b, r_b, acc1, s_c, r_c, fin = (bufs[n * i:n * i + n] for i in range(len(RS_KINDS)))
    half_w = [w // 2 for w in widths]
    chips = [(xx, yy) for xx in range(2) for yy in range(2)]
    x, y, c = lax.axis_index("x"), lax.axis_index("y"), lax.axis_index("c")
    sib, xn, yn = (x, y, 1 - c), (1 - x, y, c), (x, 1 - y, c)

    def copy(a, j, src, dst, to):
        k = RS_SEMS * a + j
        return pltpu.make_async_remote_copy(src_ref=src, dst_ref=dst, send_sem=send_sems.at[k],
                                            recv_sem=recv_sems.at[k], device_id=to, device_id_type=MESH)

    def step_a(a):
        if callable(gs[a]):
            return [copy(a, 0, gs[a](xx, yy, 1 - c), ra[a].at[xx, yy], sib) for xx, yy in chips]
        return [copy(a, 0, gs[a].at[:, :, 1 - c], ra[a], sib),
                pltpu.make_async_copy(gs[a].at[:, :, c], loc[a], local_sems.at[a])]

    def finish_a(a):
        if callable(gs[a]):
            copy(a, 0, ra[a], ra[a], sib).wait()
            for xx, yy in chips:
                ra[a][xx, yy] = gs[a](xx, yy, c)[...] + ra[a][xx, yy]
        else:
            for cp in step_a(a):
                cp.wait()
            ra[a][...] = loc[a][...] + ra[a][...]

    def step_b(a):
        return copy(a, 1, s_b[a].at[0], r_b[a].at[0], xn), copy(a, 2, s_b[a].at[1], r_b[a].at[1], yn)

    def step_c(a):
        return copy(a, 3, s_c[a].at[0], r_c[a].at[0], yn), copy(a, 4, s_c[a].at[1], r_c[a].at[1], xn)

    def step_d(a, half):
        rows = fin[a].at[half]
        return copy(a, 5, rows, rows, sib)

    def start():
        for a in range(n):
            for cp in step_a(a):
                cp.start()

    def a_to_b():
        for a in range(n):
            finish_a(a)
            s_b[a][0] = ra[a][1 - x, :, :, :half_w[a]].astype(BF16)
            s_b[a][1] = ra[a][:, 1 - y, :, half_w[a]:].astype(BF16)
            for cp in step_b(a):
                cp.start()

    def b_to_c():
        for a in range(n):
            for cp in step_b(a):
                cp.wait()
            acc1[a][0] = ra[a][x, :, :, :half_w[a]] + r_b[a][0].astype(F32)
            acc1[a][1] = ra[a][:, y, :, half_w[a]:] + r_b[a][1].astype(F32)
            s_c[a][0] = acc1[a][0, 1 - y].astype(BF16)
            s_c[a][1] = acc1[a][1, 1 - x].astype(BF16)
            for cp in step_c(a):
                cp.start()

    def c_to_d():
        for a in range(n):
            for cp in step_c(a):
                cp.wait()
            fin[a][c, :, :half_w[a]] = acc1[a][0, y] + r_c[a][0].astype(F32)
            fin[a][c, :, half_w[a]:] = acc1[a][1, x] + r_c[a][1].astype(F32)
            step_d(a, c).start()

    def finish():
        for a in range(n):
            step_d(a, 1 - c).wait_recv()
            step_d(a, c).wait_send()
            pltpu.sync_copy(fin[a], outs[a])

    return start, a_to_b, b_to_c, c_to_d, finish


def _reduce_grads(gwt, g_ws, tiny):
    grads = (gwt,)

    def body(g0, ws_in, tiny_in, o0, o_ws, o_tiny, *rest):
        nb = len(RS_KINDS)
        sm, sa, sb, sc, acc_s, send_sems, recv_sems, local_sems = rest[nb:]
        start, a_to_b, b_to_c, c_to_d, finish = _rs_stages((g0,), (o0,), rest[:nb], send_sems, recv_sems, local_sems,
                                                           [gwt.shape[1]])
        n_ws = ws_in.shape[0]
        sm[0:n_ws, :] = ws_in[...]
        sm[n_ws:, :] = tiny_in[...]
        x, y, c = lax.axis_index("x"), lax.axis_index("y"), lax.axis_index("c")

        def small(j, src, dst, to):
            k = RS_SEMS + j
            return pltpu.make_async_remote_copy(src_ref=src, dst_ref=dst, send_sem=send_sems.at[k],
                                                recv_sem=recv_sems.at[k], device_id=to, device_id_type=MESH)

        along_c, along_x, along_y = (small(0, sm, sa, (x, y, 1 - c)), small(1, acc_s, sb, (1 - x, y, c)),
                                     small(2, sb, sc, (x, 1 - y, c)))
        start()
        along_c.start()
        a_to_b()
        along_c.wait()
        acc_s[...] = sm[...] + sa[...]
        along_x.start()
        b_to_c()
        along_x.wait()
        sb[...] = acc_s[...] + sb[...]
        along_y.start()
        c_to_d()
        along_y.wait()
        o_ws[...] = sb[0:n_ws, :] + sc[0:n_ws, :]
        o_tiny[...] = sb[n_ws:, :] + sc[n_ws:, :]
        finish()

    vm = pl.BlockSpec(memory_space=pltpu.VMEM)
    hbm = pl.BlockSpec(memory_space=pl.ANY)
    small_shape = (g_ws.shape[0] + tiny.shape[0], LANES)
    scratch = _rs_scratch([g.shape for g in grads]) + [pltpu.VMEM(small_shape, F32) for _ in range(5)]
    scratch += [pltpu.SemaphoreType.DMA((RS_SEMS + 3,)), pltpu.SemaphoreType.DMA((RS_SEMS + 3,)), pltpu.SemaphoreType.DMA((1,))]
    return _call(
        body, name="reduce_grads",
        out_shape=[jax.ShapeDtypeStruct((2, gwt.shape[0] // 8, gwt.shape[1]), F32),
                   jax.ShapeDtypeStruct(g_ws.shape, F32), jax.ShapeDtypeStruct(tiny.shape, F32)],
        in_specs=[hbm, vm, vm],
        out_specs=[hbm, vm, vm],
        scratch_shapes=scratch,
        compiler_params=_params(),
    )(_rs_view(gwt), g_ws, tiny)


def _adam_update(w, g, m, v):
    nm = ADAM_B1 * m + (1.0 - ADAM_B1) * g
    nv = ADAM_B2 * v + (1.0 - ADAM_B2) * (g * g)
    m_hat = nm / (1.0 - ADAM_B1 ** ADAM_STEP)
    v_hat = nv / (1.0 - ADAM_B2 ** ADAM_STEP)
    return -ADAM_LR * (m_hat / (jnp.sqrt(v_hat) + ADAM_EPS) + ADAM_WD * w), nm, nv


def _adamw(w, g, m, v):
    rows, cols = w.shape
    tm = max(t for t in range(8, 257, 8) if rows % t == 0)

    def body(w_ref, g_ref, m_ref, v_ref, d_ref, nm_ref, nv_ref):
        d_ref[...], nm_ref[...], nv_ref[...] = _adam_update(w_ref[...], g_ref[...], m_ref[...], v_ref[...])

    blk = pl.BlockSpec((tm, cols), lambda i: (i, 0))
    return _call(
        body, name="adamw", grid=(rows // tm,),
        in_specs=[blk] * 4, out_specs=[blk] * 3,
        out_shape=[jax.ShapeDtypeStruct((rows, cols), F32)] * 3,
        compiler_params=_params(),
    )(w, g, m, v)


def _adamw_tiny(tiny, weights, ms, vs):
    shapes = [w.shape for w in weights]
    n = len(weights)

    def grad_of(t_ref, k, shape):
        base = 8 * k
        if shape[1] > LANES:
            return [t_ref[base + j:base + j + 1, :] for j in range(shape[1] // LANES)]
        return [t_ref[base:base + shape[0], 0:shape[1]]]

    def body(t_ref, *refs):
        w_refs, m_refs, v_refs = refs[:n], refs[n:2 * n], refs[2 * n:3 * n]
        loss_ref, outs = refs[3 * n], refs[3 * n + 1:]
        loss_ref[...] = (0.5 / D_MODEL) * jnp.sum(t_ref[8 * n:8 * n + 8, :], keepdims=True)
        for k, shape in enumerate(shapes):
            g_ref, d_ref, nm_ref, nv_ref = outs[4 * k:4 * k + 4]
            for j, g in enumerate(grad_of(t_ref, k, shape)):
                cols = slice(j * LANES, (j + 1) * LANES) if shape[1] > LANES else slice(None)
                g_ref[:, cols] = g
                d_ref[:, cols], nm_ref[:, cols], nv_ref[:, cols] = _adam_update(
                    w_refs[k][:, cols], g, m_refs[k][:, cols], v_refs[k][:, cols])

    out_shape = [jax.ShapeDtypeStruct((1, 1), F32)]
    for shape in shapes:
        out_shape += [jax.ShapeDtypeStruct(shape, F32)] * 4
    return _call(body, name="adamw_tiny", out_shape=out_shape, compiler_params=_params())(tiny, *weights, *ms, *vs)


def _local_grads(x, mem, tgt, norm_gain, wt, gmlp_v_gain, gmlp_w_s, gmlp_b, attn_q_gain, attn_k_gain,
                 mem_norm_gain, wkv, mem_q_gain, mem_k_gain, wo):
    vg = gmlp_v_gain.reshape(1, GMLP_W)
    bias_full = jnp.repeat(gmlp_b.T, HEAD_DIM, axis=1)
    gq2, gk2 = jnp.tile(attn_q_gain, (1, 2)), jnp.tile(attn_k_gain, (1, 2))
    qg4, kg4 = jnp.tile(mem_q_gain, (1, 4)), jnp.tile(mem_k_gain, (1, 4))

    proj = _fwd_proj(x, norm_gain, wt)
    yg = _gmlp_fwd(proj, vg, gmlp_w_s, bias_full)
    o, lse, ya = _attn_fwd(proj, gq2, gk2)
    kv, hm = _mem_kv(mem, mem_norm_gain, wkv)
    om, ym = _mem_fwd(proj, kv, qg4, kg4)
    dy, dyc, g_wo, err2 = _out_loss(yg, ya, ym, x, tgt, wo)
    dmq, dmg, g_mq, g_mk, g_wkv, g_mng = _mem_bwd(proj, om, dyc, kv, hm, mem, mem_norm_gain, wkv, qg4, kg4)
    daq, dak, dav, dag, g_aq, g_ak, g_wkv_sh, g_wo_sh = _attn_bwd(proj, o, lse, dyc, gq2, gk2, g_wkv, g_wo)
    dg, g_ws, g_b, g_vg = _gmlp_bwd(proj, dyc, vg, gmlp_w_s, bias_full)
    gx, g_wt_early_sh, g_wt_late, g_ng = _proj_bwd(x, dy, norm_gain, wt, dg, daq, dak, dav, dag, dmq, dmg)

    tiny = jnp.concatenate([g_ng, g_vg, g_b, g_aq, g_ak, g_mng, g_mq, g_mk, err2], axis=0)
    return gx, g_wt_early_sh, g_wt_late, g_wkv_sh, g_wo_sh, g_ws.reshape(4 * CHUNK, CHUNK), tiny


def kernel(x, mem, norm_gain, w_in, gmlp_v_gain, gmlp_w_s, gmlp_b, attn_q_gain, attn_k_gain, mem_norm_gain, w_mem_kv, mem_q_gain, mem_k_gain, w_out, loss_target, m_norm_gain, m_w_in, m_gmlp_v_gain, m_gmlp_w_s, m_gmlp_b, m_attn_q_gain, m_attn_k_gain, m_mem_norm_gain, m_w_mem_kv, m_mem_q_gain, m_mem_k_gain, m_w_out, v_norm_gain, v_w_in, v_gmlp_v_gain, v_gmlp_w_s, v_gmlp_b, v_attn_q_gain, v_attn_k_gain, v_mem_norm_gain, v_w_mem_kv, v_mem_q_gain, v_mem_k_gain, v_w_out):
    wt, wkv, wo = _gather_weights(w_in[0].T, w_mem_kv[0], w_out[0])
    gx, g_wt_early_sh, g_wt_late, g_wkv_sh, g_wo_sh, g_ws, tiny = _local_grads(
        x[0], mem[0], loss_target[0], norm_gain, wt, gmlp_v_gain[0], gmlp_w_s[0], gmlp_b[0],
        attn_q_gain, attn_k_gain, mem_norm_gain, wkv, mem_q_gain, mem_k_gain, wo)
    g_wt_late_sh, g_ws, tiny = _reduce_grads(g_wt_late, g_ws, tiny)
    chip_block = lambda g: g.reshape(2 * g.shape[1], g.shape[2])
    g_wt_sh = jnp.concatenate([chip_block(g_wt_early_sh), chip_block(g_wt_late_sh)], axis=1)
    g_wkv_sh, g_wo_sh = chip_block(g_wkv_sh), chip_block(g_wo_sh)

    ws = (norm_gain, w_in, gmlp_v_gain, gmlp_w_s, gmlp_b, attn_q_gain, attn_k_gain, mem_norm_gain, w_mem_kv,
          mem_q_gain, mem_k_gain, w_out)
    ms = (m_norm_gain, m_w_in, m_gmlp_v_gain, m_gmlp_w_s, m_gmlp_b, m_attn_q_gain, m_attn_k_gain, m_mem_norm_gain,
          m_w_mem_kv, m_mem_q_gain, m_mem_k_gain, m_w_out)
    vs = (v_norm_gain, v_w_in, v_gmlp_v_gain, v_gmlp_w_s, v_gmlp_b, v_attn_q_gain, v_attn_k_gain, v_mem_norm_gain,
          v_w_mem_kv, v_mem_q_gain, v_mem_k_gain, v_w_out)
    form = {1: lambda a: a[0].T, 3: lambda a: a.reshape(4 * CHUNK, CHUNK), 2: lambda a: a[0], 4: lambda a: a[0],
            8: lambda a: a[0], 11: lambda a: a[0]}
    back = {1: lambda a: a.T[None], 3: lambda a: a.reshape(1, 4, CHUNK, CHUNK), 2: lambda a: a[None],
            4: lambda a: a[None], 8: lambda a: a[None], 11: lambda a: a[None]}
    fwd = lambda t, i: form.get(i, lambda a: a)(t[i])
    out = {}
    for i, g in ((1, g_wt_sh), (3, g_ws), (8, g_wkv_sh), (11, g_wo_sh)):
        out[i] = (g, *_adamw(fwd(ws, i), g, fwd(ms, i), fwd(vs, i)))
    res = _adamw_tiny(tiny, [fwd(ws, i) for i in TINY_ORDER], [fwd(ms, i) for i in TINY_ORDER],
                      [fwd(vs, i) for i in TINY_ORDER])
    for k, i in enumerate(TINY_ORDER):
        out[i] = res[1 + 4 * k:5 + 4 * k]
    leaves = [[back.get(i, lambda a: a)(out[i][j]) for i in range(12)] for j in range(4)]
    return (res[0].reshape(()), gx[None], *leaves[0], *leaves[1], *leaves[2], *leaves[3])
```

```python
import functools
import math

import jax
import jax.numpy as jnp
from jax import lax
from jax.experimental import pallas as pl
from jax.experimental.pallas import tpu as pltpu

F32 = jnp.float32
BF16 = jnp.bfloat16

SEQ = 4096
D_MODEL = 1024
HEAD_DIM = 64
LANES = 128
CHUNK = 128
GMLP_W, ATTN_W, MEM_W = 256, 512, 256
IN_W = 3 * GMLP_W + 4 * ATTN_W + 2 * MEM_W
MEM_LEN = 256
DILATIONS = (1, 4, 16)
EPS = 1e-6
QK_SCALE = 1.0 / math.sqrt(HEAD_DIM)
C_GU, C_GV, C_GG, C_AQ, C_AK, C_AV, C_AG, C_MQ, C_MG = 0, 256, 512, 768, 1280, 1792, 2304, 2816, 3072

ADAM_LR, ADAM_B1, ADAM_B2, ADAM_EPS, ADAM_WD, ADAM_STEP = 0.001, 0.9, 0.999, 1e-08, 0.01, 10

VMEM_LIMIT = 48 * 1024 * 1024
PB_EARLY = 768
ATTN_UNROLL = 4
MESH = pl.DeviceIdType.MESH

TINY_ORDER = (0, 2, 4, 5, 6, 7, 9, 10)


def _call(body, **kw):
    return pl.pallas_call(body, **kw)


def _params(**kw):
    return pltpu.CompilerParams(vmem_limit_bytes=VMEM_LIMIT, **kw)


def _dot(a, b):
    return jnp.dot(a, b, preferred_element_type=F32)


def _dot_nt(a, b):
    return lax.dot_general(a, b, (((1,), (1,)), ((), ())), preferred_element_type=F32)


def _dot_tn(a, b):
    return lax.dot_general(a, b, (((0,), (0,)), ((), ())), preferred_element_type=F32)


def _head_blockdiag():
    r = lax.shift_right_logical(lax.broadcasted_iota(jnp.int32, (LANES, LANES), 0), 6)
    c = lax.shift_right_logical(lax.broadcasted_iota(jnp.int32, (LANES, LANES), 1), 6)
    return jnp.where(r == c, 1.0, 0.0).astype(BF16)


def _headsum(v, bd):
    hi = v.astype(BF16)
    lo = (v - hi.astype(F32)).astype(BF16)
    return _dot(hi, bd) + _dot(lo, bd)


def _lo_mask(rows):
    return lax.broadcasted_iota(jnp.int32, (rows, LANES), 1) < HEAD_DIM


def _sigmoid(x):
    return 1.0 / (1.0 + jnp.exp(-x))


def _fold_heads(v):
    return v + pltpu.roll(v, HEAD_DIM, 1)


def _put_rows(ref, vec, accumulate=False):
    for j in range(vec.shape[1] // LANES):
        piece = vec[:, j * LANES:(j + 1) * LANES]
        ref[j:j + 1, :] = ref[j:j + 1, :] + piece if accumulate else piece


def _fwd_proj(x, gain, wt):
    tm = 512

    def body(x_ref, g_ref, wt_ref, o_ref):
        xv = x_ref[...]
        ms = jnp.mean(xv * xv, axis=-1, keepdims=True)
        h = (xv * lax.rsqrt(ms + EPS) * g_ref[...]).astype(BF16)
        o_ref[...] = _dot_nt(h, wt_ref[...])

    return _call(
        body, name="fwd_proj", grid=(SEQ // tm,),
        in_specs=[pl.BlockSpec((tm, D_MODEL), lambda i: (i, 0)),
                  pl.BlockSpec((1, D_MODEL), lambda i: (0, 0)),
                  pl.BlockSpec((IN_W, D_MODEL), lambda i: (0, 0))],
        out_specs=pl.BlockSpec((tm, IN_W), lambda i: (i, 0)),
        out_shape=jax.ShapeDtypeStruct((SEQ, IN_W), F32),
        compiler_params=_params(),
    )(x, gain, wt)


def _gmlp_weights(w_ref):
    ti = lax.broadcasted_iota(jnp.int32, (CHUNK, CHUNK), 0)
    si = lax.broadcasted_iota(jnp.int32, (CHUNK, CHUNK), 1)
    tril = si <= ti
    return tril, [jnp.where(tril, w_ref[h], 0.0).astype(BF16) for h in range(4)]


def _gmlp_fwd(proj, vgain, w_s, bias_full):
    tm = 512

    def body(p_ref, vg_ref, w_ref, b_ref, y_ref):
        bd = _head_blockdiag()
        lo = _lo_mask(CHUNK)
        _, wm = _gmlp_weights(w_ref)
        for c in range(tm // CHUNK):
            rows = pl.ds(c * CHUNK, CHUNK)
            for p in range(2):
                cs = slice(p * LANES, (p + 1) * LANES)
                u = p_ref[rows, C_GU + p * LANES:C_GU + (p + 1) * LANES]
                v = p_ref[rows, C_GV + p * LANES:C_GV + (p + 1) * LANES]
                gt = p_ref[rows, C_GG + p * LANES:C_GG + (p + 1) * LANES]
                r = lax.rsqrt(_headsum(v * v, bd) * (1.0 / HEAD_DIM) + EPS)
                vn = (v * r * vg_ref[:, cs]).astype(BF16)
                sp = jnp.where(lo, _dot(wm[2 * p], vn), _dot(wm[2 * p + 1], vn)) + b_ref[:, cs]
                y_ref[rows, cs] = (u * sp * (gt * _sigmoid(gt))).astype(BF16)

    return _call(
        body, name="gmlp_fwd", grid=(SEQ // tm,),
        in_specs=[pl.BlockSpec((tm, 3 * GMLP_W), lambda i: (i, 0)),
                  pl.BlockSpec((1, GMLP_W), lambda i: (0, 0)),
                  pl.BlockSpec((4, CHUNK, CHUNK), lambda i: (0, 0, 0)),
                  pl.BlockSpec((CHUNK, GMLP_W), lambda i: (0, 0))],
        out_specs=pl.BlockSpec((tm, GMLP_W), lambda i: (i, 0)),
        out_shape=jax.ShapeDtypeStruct((SEQ, GMLP_W), BF16),
        compiler_params=_params(),
    )(proj, vgain, w_s, bias_full)


def _gmlp_bwd(proj, dyc, vgain, w_s, bias_full):
    tm = 512
    nsteps = SEQ // tm

    def body(p_ref, dy_ref, vg_ref, w_ref, b_ref, dg_ref, gw_ref, gb_ref, gv_ref):
        i = pl.program_id(0)
        bd = _head_blockdiag()
        lo = _lo_mask(CHUNK)
        tril, wm = _gmlp_weights(w_ref)
        ri = lax.broadcasted_iota(jnp.int32, (16, LANES), 0)
        li = lax.broadcasted_iota(jnp.int32, (16, LANES), 1)
        head_rows = [jnp.where(((ri == 2 * p) & (li < HEAD_DIM)) | ((ri == 2 * p + 1) & (li >= HEAD_DIM)), 1.0, 0.0).astype(BF16)
                     for p in range(2)]

        @pl.when(i == 0)
        def _():
            gw_ref[...] = jnp.zeros_like(gw_ref)
            gb_ref[...] = jnp.zeros_like(gb_ref)
            gv_ref[...] = jnp.zeros_like(gv_ref)

        for c in range(tm // CHUNK):
            rows = pl.ds(c * CHUNK, CHUNK)
            for p in range(2):
                cs = slice(p * LANES, (p + 1) * LANES)
                u = p_ref[rows, C_GU + p * LANES:C_GU + (p + 1) * LANES]
                v = p_ref[rows, C_GV + p * LANES:C_GV + (p + 1) * LANES]
                gt = p_ref[rows, C_GG + p * LANES:C_GG + (p + 1) * LANES]
                dy = dy_ref[rows, cs]
                g = vg_ref[:, cs]
                r = lax.rsqrt(_headsum(v * v, bd) * (1.0 / HEAD_DIM) + EPS)
                z = v * r
                vn = (z * g).astype(BF16)
                sp = jnp.where(lo, _dot(wm[2 * p], vn), _dot(wm[2 * p + 1], vn)) + b_ref[:, cs]
                sg = _sigmoid(gt)
                sl = gt * sg
                dsl = sg * (1.0 + gt * (1.0 - sg))
                du = dy * sp * sl
                dsp = dy * u * sl
                dgt = dy * u * sp * dsl
                dspb = dsp.astype(BF16)
                dvn = jnp.where(lo, _dot_tn(wm[2 * p], dspb), _dot_tn(wm[2 * p + 1], dspb))
                gw_ref[2 * p] += _dot_nt(jnp.where(lo, dsp, 0.0).astype(BF16), vn)
                gw_ref[2 * p + 1] += _dot_nt(jnp.where(lo, 0.0, dsp).astype(BF16), vn)
                dsp_lo = (dsp - dspb.astype(F32)).astype(BF16)
                gb_ref[...] += (_dot_nt(head_rows[p], dspb) + _dot_nt(head_rows[p], dsp_lo))[0:8]
                gvp = jnp.sum(dvn * z, axis=0, keepdims=True)
                gv_ref[2 * p:2 * p + 1, :] += gvp
                gv_ref[2 * p + 1:2 * p + 2, :] += pltpu.roll(gvp, HEAD_DIM, 1)
                dz = dvn * g
                dv = r * (dz - z * (_headsum(dz * z, bd) * (1.0 / HEAD_DIM)))
                dg_ref[rows, C_GU + p * LANES:C_GU + (p + 1) * LANES] = du.astype(BF16)
                dg_ref[rows, C_GV + p * LANES:C_GV + (p + 1) * LANES] = dv.astype(BF16)
                dg_ref[rows, C_GG + p * LANES:C_GG + (p + 1) * LANES] = dgt.astype(BF16)

        @pl.when(i == nsteps - 1)
        def _():
            for h in range(4):
                gw_ref[h] = jnp.where(tril, gw_ref[h], 0.0)

    return _call(
        body, name="gmlp_bwd", grid=(nsteps,),
        in_specs=[pl.BlockSpec((tm, 3 * GMLP_W), lambda i: (i, 0)),
                  pl.BlockSpec((tm, GMLP_W), lambda i: (i, 0)),
                  pl.BlockSpec((1, GMLP_W), lambda i: (0, 0)),
                  pl.BlockSpec((4, CHUNK, CHUNK), lambda i: (0, 0, 0)),
                  pl.BlockSpec((CHUNK, GMLP_W), lambda i: (0, 0))],
        out_specs=[pl.BlockSpec((tm, 3 * GMLP_W), lambda i: (i, 0)),
                   pl.BlockSpec((4, CHUNK, CHUNK), lambda i: (0, 0, 0)),
                   pl.BlockSpec((8, LANES), lambda i: (0, 0)),
                   pl.BlockSpec((8, LANES), lambda i: (0, 0))],
        out_shape=[jax.ShapeDtypeStruct((SEQ, 3 * GMLP_W), BF16),
                   jax.ShapeDtypeStruct((4, CHUNK, CHUNK), F32),
                   jax.ShapeDtypeStruct((8, LANES), F32),
                   jax.ShapeDtypeStruct((8, LANES), F32)],
        compiler_params=_params(),
    )(proj, dyc, vgain, w_s, bias_full)


def _band_masks():
    qi = lax.broadcasted_iota(jnp.int32, (CHUNK, 2 * CHUNK), 0)
    kj = lax.broadcasted_iota(jnp.int32, (CHUNK, 2 * CHUNK), 1)
    valid2 = ((kj < CHUNK) & (kj >= qi)) | ((kj >= CHUNK) & (kj - CHUNK <= qi))
    q1 = lax.broadcasted_iota(jnp.int32, (CHUNK, CHUNK), 0)
    k1 = lax.broadcasted_iota(jnp.int32, (CHUNK, CHUNK), 1)
    return k1 <= q1, valid2


def _stack_heads(v, lo):
    return jnp.concatenate([jnp.where(lo, v, 0.0), jnp.where(lo, 0.0, v)], axis=0).astype(BF16)


def _rows_of(ref, start, d):
    if d == 1:
        return ref.at[pl.ds(start if isinstance(start, int) else pl.multiple_of(start, CHUNK), CHUNK), :]
    return ref.at[pl.ds(start, CHUNK, stride=d), :]


def _unrolled(lo, hi, unroll, run):
    groups = (hi - lo) // unroll
    if groups:
        def body(g, carry):
            run([lo + g * unroll + t for t in range(unroll)])
            return carry

        lax.fori_loop(0, groups, body, 0)
    if lo + groups * unroll < hi:
        run(range(lo + groups * unroll, hi))


def _for_blocks(d, group_fn, unroll):
    nblk = SEQ // CHUNK
    sh = d.bit_length() - 1

    def first(j):
        return (j * CHUNK if d == 1 else j, None)

    def rest(j):
        start = (j & (d - 1)) + (j >> sh) * (CHUNK * d)
        return (start, start - CHUNK * d)

    _unrolled(0, d, unroll, lambda js: group_fn(d, [first(j) for j in js]))
    _unrolled(d, nblk, unroll, lambda js: group_fn(d, [rest(j) for j in js]))


def _attn_fwd(proj, gq2, gk2):
    tn = 512

    def body(q_ref, k_ref, v_ref, g_ref, gq_ref, gk_ref, o_ref, l_ref, ya_ref, qn_ref, kn_ref):
        bd = _head_blockdiag()
        lo = _lo_mask(CHUNK)
        valid1, valid2 = _band_masks()

        def norm(t, carry):
            rows = pl.ds(pl.multiple_of(t * tn, tn), tn)
            q = q_ref[rows, :]
            qn_ref[rows, :] = q * lax.rsqrt(_headsum(q * q, bd) * (1.0 / HEAD_DIM) + EPS) * (gq_ref[...] * QK_SCALE)
            k = k_ref[rows, :]
            kn_ref[rows, :] = k * lax.rsqrt(_headsum(k * k, bd) * (1.0 / HEAD_DIM) + EPS) * gk_ref[...]
            return carry

        lax.fori_loop(0, SEQ // tn, norm, 0)

        def load_kv(ref, d, start, prev):
            own = _rows_of(ref, start, d)[...]
            if prev is None:
                return own.astype(BF16)
            return jnp.concatenate([_rows_of(ref, prev, d)[...], own], axis=0).astype(BF16)

        def group(d, blocks):
            valid = valid1 if blocks[0][1] is None else valid2
            valid = jnp.concatenate([valid, valid], axis=0)
            qs = [_rows_of(qn_ref, start, d)[...] for start, _ in blocks]
            ks = [load_kv(kn_ref, d, start, prev) for start, prev in blocks]
            vs = [load_kv(v_ref, d, start, prev) for start, prev in blocks]
            ss = [_dot_nt(_stack_heads(q, lo), k) for q, k in zip(qs, ks)]
            ms, ps, ls = [], [], []
            for s in ss:
                s = jnp.where(valid, s, -jnp.inf)
                m = jnp.max(s, axis=-1, keepdims=True)
                p = jnp.exp(s - m)
                ms.append(m)
                ls.append(jnp.sum(p, axis=-1, keepdims=True))
                ps.append(p.astype(BF16))
            os_ = [_dot(p, v) for p, v in zip(ps, vs)]
            for b, (start, _) in enumerate(blocks):
                on = os_[b] * (1.0 / ls[b])
                ln = ms[b] + jnp.log(ls[b])
                ob = jnp.where(lo, on[:CHUNK], on[CHUNK:])
                lb = jnp.where(lo, ln[:CHUNK], ln[CHUNK:])
                o_rows = _rows_of(o_ref, start, d)
                l_rows = _rows_of(l_ref, start, d)
                if d != DILATIONS[0]:
                    lold = l_rows[...]
                    mx = jnp.maximum(lold, lb)
                    ea = jnp.exp(lold - mx)
                    eb = jnp.exp(lb - mx)
                    inv = 1.0 / (ea + eb)
                    ob = o_rows[...] * (ea * inv) + ob * (eb * inv)
                    lb = mx + jnp.log(ea + eb)
                o_rows[...] = ob
                l_rows[...] = lb

        for d in DILATIONS:
            _for_blocks(d, group, ATTN_UNROLL)

        def fin(t, carry):
            rows = pl.ds(pl.multiple_of(t * tn, tn), tn)
            g = g_ref[rows, :]
            ya_ref[rows, :] = (o_ref[rows, :] * (g * _sigmoid(g))).astype(BF16)
            return carry

        lax.fori_loop(0, SEQ // tn, fin, 0)

    col = lambda c0: pl.BlockSpec((SEQ, LANES), lambda p: (0, c0 // LANES + p))
    vec = pl.BlockSpec((1, LANES), lambda p: (0, 0))
    out = pl.BlockSpec((SEQ, LANES), lambda p: (0, p))
    return _call(
        body, name="attn_fwd", grid=(ATTN_W // LANES,),
        in_specs=[col(C_AQ), col(C_AK), col(C_AV), col(C_AG), vec, vec],
        out_specs=[out, out, out],
        out_shape=[jax.ShapeDtypeStruct((SEQ, ATTN_W), F32), jax.ShapeDtypeStruct((SEQ, ATTN_W), F32),
                   jax.ShapeDtypeStruct((SEQ, ATTN_W), BF16)],
        scratch_shapes=[pltpu.VMEM((SEQ, LANES), F32), pltpu.VMEM((SEQ, LANES), F32)],
        compiler_params=_params(),
    )(proj, proj, proj, proj, gq2, gk2)


def _attn_bwd(proj, o, lse, dyc, gq2, gk2, *ride_along):
    tn = 512
    npairs = ATTN_W // LANES
    nride = len(ride_along)
    nbufs = nride * len(RS_KINDS)

    def body(proj_hbm, o_hbm, l_hbm, dyc_hbm, gq_ref, gk_ref, *rest):
        ride_in, rest = rest[:nride], rest[nride:]
        dq_ref, dk_ref, dv_ref, dgt_ref, gqg_ref, gkg_ref = rest[:6]
        ride_out, rest = rest[6:6 + nride], rest[6 + nride:]
        qb_, kb_, vb_, gb_, ob_, lb_, yb_, dkb_, dvb_, sems = rest[:10]
        rs_bufs, (send_sems, recv_sems, local_sems) = rest[10:10 + nbufs], rest[10 + nbufs:]
        rs_stage = _rs_stages(ride_in, ride_out, rs_bufs, send_sems, recv_sems, local_sems, [g.shape[1] for g in ride_along])
        pair = pl.program_id(0)
        for step in range(npairs):
            pl.when(pair == step)(rs_stage[step])
        bd = _head_blockdiag()
        lo = _lo_mask(CHUNK)
        lo2 = lax.broadcasted_iota(jnp.int32, (2 * CHUNK, LANES), 1) < HEAD_DIM
        valid1, valid2 = _band_masks()
        gqs = gq_ref[...] * QK_SCALE
        gk = gk_ref[...]

        def pcol(c0):
            return proj_hbm.at[:, pl.ds(pl.multiple_of(c0 + pair * LANES, LANES), LANES)]

        def acol(hbm, c0=0):
            return hbm.at[:, pl.ds(pl.multiple_of(c0 + pair * LANES, LANES), LANES)]

        loads = [pltpu.make_async_copy(src, dst, sems.at[n]) for n, (src, dst) in enumerate((
            (pcol(C_AQ), qb_), (pcol(C_AK), kb_), (pcol(C_AG), gb_), (acol(o_hbm), ob_),
            (acol(dyc_hbm, GMLP_W), yb_), (pcol(C_AV), vb_), (acol(l_hbm), lb_)))]
        for cp in loads:
            cp.start()

        @pl.when(pair == 0)
        def _():
            gqg_ref[...] = jnp.zeros_like(gqg_ref)
            gkg_ref[...] = jnp.zeros_like(gkg_ref)

        def pre_qk(t, carry):
            rows = pl.ds(pl.multiple_of(t * tn, tn), tn)
            zero = jnp.zeros((tn, LANES), F32)
            dkb_[rows, :] = zero
            dvb_[rows, :] = zero
            q = qb_[rows, :]
            qb_[rows, :] = q * lax.rsqrt(_headsum(q * q, bd) * (1.0 / HEAD_DIM) + EPS) * gqs
            k = kb_[rows, :]
            kb_[rows, :] = k * lax.rsqrt(_headsum(k * k, bd) * (1.0 / HEAD_DIM) + EPS) * gk
            return carry

        def pre_gate(t, carry):
            rows = pl.ds(pl.multiple_of(t * tn, tn), tn)
            g = gb_[rows, :]
            ov = ob_[rows, :]
            dya = yb_[rows, :]
            sg = _sigmoid(g)
            dgt_ref[rows, :] = (dya * ov * (sg * (1.0 + g * (1.0 - sg)))).astype(BF16)
            do = dya * (g * sg)
            yb_[rows, :] = do
            ob_[rows, :] = _headsum(do * ov, bd)
            gb_[rows, :] = jnp.zeros((tn, LANES), F32)
            return carry

        loads[0].wait()
        loads[1].wait()
        lax.fori_loop(0, SEQ // tn, pre_qk, 0)
        for cp in loads[2:5]:
            cp.wait()
        lax.fori_loop(0, SEQ // tn, pre_gate, 0)
        loads[5].wait()
        loads[6].wait()

        def load_kv(ref, d, start, prev):
            own = _rows_of(ref, start, d)[...]
            if prev is None:
                return own.astype(BF16)
            return jnp.concatenate([_rows_of(ref, prev, d)[...], own], axis=0).astype(BF16)

        def group(d, blocks):
            first = blocks[0][1] is None
            valid, lok = (valid1, lo) if first else (valid2, lo2)
            chains = [(b, h) for b in range(len(blocks)) for h in range(2)]
            mask = lambda h: lo if h == 0 else ~lo
            qs = [_rows_of(qb_, start, d)[...] for start, _ in blocks]
            dos = [_rows_of(yb_, start, d)[...] for start, _ in blocks]
            lvs = [_rows_of(lb_, start, d)[...] for start, _ in blocks]
            dls = [_rows_of(ob_, start, d)[...] for start, _ in blocks]
            ks = [load_kv(kb_, d, start, prev) for start, prev in blocks]
            vs = [load_kv(vb_, d, start, prev) for start, prev in blocks]
            qbs = [q.astype(BF16) for q in qs]
            dobs = [do.astype(BF16) for do in dos]
            ss = [_dot_nt(jnp.where(mask(h), qs[b], 0.0).astype(BF16), ks[b]) for b, h in chains]
            dps = [_dot_nt(jnp.where(mask(h), dos[b], 0.0).astype(BF16), vs[b]) for b, h in chains]
            pbs, dss = [], []
            for s, dp, (b, h) in zip(ss, dps, chains):
                hc = h * HEAD_DIM
                p = jnp.exp(jnp.where(valid, s, -jnp.inf) - lvs[b][:, hc:hc + 1])
                pbs.append(p.astype(BF16))
                dss.append((p * (dp - dls[b][:, hc:hc + 1])).astype(BF16))
            dqs = [_dot(ds, ks[b]) for ds, (b, h) in zip(dss, chains)]
            dks = [_dot_tn(ds, qbs[b]) for ds, (b, h) in zip(dss, chains)]
            dvs = [_dot_tn(p, dobs[b]) for p, (b, h) in zip(pbs, chains)]
            for b, (start, prev) in enumerate(blocks):
                c0, c1 = 2 * b, 2 * b + 1
                dq_rows = _rows_of(gb_, start, d)
                dq_rows[...] = dq_rows[...] + jnp.where(lo, dqs[c0], dqs[c1])
                dkc = jnp.where(lok, dks[c0], dks[c1])
                dvc = jnp.where(lok, dvs[c0], dvs[c1])
                spans = ((start, slice(0, CHUNK)),) if first else ((prev, slice(0, CHUNK)), (start, slice(CHUNK, 2 * CHUNK)))
                for st, sl in spans:
                    dk_rows = _rows_of(dkb_, st, d)
                    dk_rows[...] = dk_rows[...] + dkc[sl]
                    dv_rows = _rows_of(dvb_, st, d)
                    dv_rows[...] = dv_rows[...] + dvc[sl]

        for d in DILATIONS:
            _for_blocks(d, group, ATTN_UNROLL)

        pltpu.sync_copy(pcol(C_AQ), qb_)
        pltpu.sync_copy(pcol(C_AK), kb_)

        def post(t, carry):
            gq_acc, gk_acc = carry
            rows = pl.ds(pl.multiple_of(t * tn, tn), tn)
            outs = []
            for raw_, acc_, gain in ((qb_, gb_, gqs), (kb_, dkb_, gk)):
                a = raw_[rows, :]
                r = lax.rsqrt(_headsum(a * a, bd) * (1.0 / HEAD_DIM) + EPS)
                z = a * r
                dn = acc_[rows, :]
                dz = dn * gain
                outs.append((r * (dz - z * (_headsum(dz * z, bd) * (1.0 / HEAD_DIM))), jnp.sum(dn * z, axis=0, keepdims=True)))
            dq_ref[rows, :] = outs[0][0].astype(BF16)
            dk_ref[rows, :] = outs[1][0].astype(BF16)
            dv_ref[rows, :] = dvb_[rows, :].astype(BF16)
            return gq_acc + outs[0][1] * QK_SCALE, gk_acc + outs[1][1]

        zero = jnp.zeros((1, LANES), F32)
        gq_acc, gk_acc = lax.fori_loop(0, SEQ // tn, post, (zero, zero))
        gqg_ref[0:1, :] += gq_acc
        gkg_ref[0:1, :] += gk_acc

        @pl.when(pair == npairs - 1)
        def _():
            gqg_ref[0:1, :] = _fold_heads(gqg_ref[0:1, :])
            gkg_ref[0:1, :] = _fold_heads(gkg_ref[0:1, :])
            rs_stage[npairs]()

    hbm = pl.BlockSpec(memory_space=pl.ANY)
    vec = pl.BlockSpec((1, LANES), lambda p: (0, 0))
    blk8 = pl.BlockSpec((8, LANES), lambda p: (0, 0))
    out = pl.BlockSpec((SEQ, LANES), lambda p: (0, p))
    big = jax.ShapeDtypeStruct((SEQ, ATTN_W), BF16)
    nsem = RS_SEMS * nride
    return _call(
        body, name="attn_bwd", grid=(npairs,),
        in_specs=[hbm, hbm, hbm, hbm, vec, vec] + [hbm] * nride,
        out_specs=[out, out, out, out, blk8, blk8] + [hbm] * nride,
        out_shape=[big, big, big, big, jax.ShapeDtypeStruct((8, LANES), F32), jax.ShapeDtypeStruct((8, LANES), F32)]
        + [jax.ShapeDtypeStruct((2, g.shape[0] // 8, g.shape[1]), F32) for g in ride_along],
        scratch_shapes=[pltpu.VMEM((SEQ, LANES), F32) for _ in range(9)] + [pltpu.SemaphoreType.DMA((7,))]
        + _rs_scratch([g.shape for g in ride_along]) + [pltpu.SemaphoreType.DMA((nsem,)), pltpu.SemaphoreType.DMA((nsem,)),
                                     pltpu.SemaphoreType.DMA((nride,))],
        compiler_params=_params(),
    )(proj, o, lse, dyc, gq2, gk2, *[_rs_view(g) for g in ride_along])


def _mem_kv(mem, gain, wkv):
    def body(m_ref, g_ref, w_ref, kv_ref, hm_ref):
        mv = m_ref[...]
        ms = jnp.mean(mv * mv, axis=-1, keepdims=True)
        hm = (mv * lax.rsqrt(ms + EPS) * g_ref[...]).astype(BF16)
        hm_ref[...] = hm
        kv_ref[...] = _dot(hm, w_ref[...])

    return _call(
        body, name="mem_kv",
        out_shape=[jax.ShapeDtypeStruct((MEM_LEN, 2 * MEM_W), F32), jax.ShapeDtypeStruct((MEM_LEN, D_MODEL), BF16)],
        compiler_params=_params(),
    )(mem, gain, wkv)


def _mem_keys(kv_ref, kg_ref, bd, p):
    mk = kv_ref[:, p * LANES:(p + 1) * LANES]
    r = lax.rsqrt(_headsum(mk * mk, bd) * (1.0 / HEAD_DIM) + EPS)
    z = mk * r
    mkn = (z * kg_ref[:, p * LANES:(p + 1) * LANES]).astype(BF16)
    mvp = kv_ref[:, MEM_W + p * LANES:MEM_W + (p + 1) * LANES].astype(BF16)
    return mkn, mvp, r, z


def _mem_fwd(proj, kv, qg4, kg4):
    tm = 512

    def body(q_ref, g_ref, kv_ref, qg_ref, kg_ref, om_ref, ym_ref):
        bd = _head_blockdiag()
        lo = _lo_mask(tm)
        for p in range(2):
            cs = slice(p * LANES, (p + 1) * LANES)
            mkn, mvp, _, _ = _mem_keys(kv_ref, kg_ref, bd, p)
            q = q_ref[:, cs]
            qn = q * lax.rsqrt(_headsum(q * q, bd) * (1.0 / HEAD_DIM) + EPS) * (qg_ref[:, cs] * QK_SCALE)
            res = []
            for h in range(2):
                qh = jnp.where(lo if h == 0 else ~lo, qn, 0.0).astype(BF16)
                s = _dot_nt(qh, mkn)
                e = jnp.exp(s - jnp.max(s, axis=-1, keepdims=True))
                res.append(_dot(e.astype(BF16), mvp) * (1.0 / jnp.sum(e, axis=-1, keepdims=True)))
            ov = jnp.where(lo, res[0], res[1])
            g = g_ref[:, cs]
            om_ref[:, cs] = ov
            ym_ref[:, cs] = (ov * (g * _sigmoid(g))).astype(BF16)

    vec = pl.BlockSpec((1, MEM_W), lambda i: (0, 0))
    return _call(
        body, name="mem_fwd", grid=(SEQ // tm,),
        in_specs=[pl.BlockSpec((tm, MEM_W), lambda i: (i, C_MQ // MEM_W)),
                  pl.BlockSpec((tm, MEM_W), lambda i: (i, C_MG // MEM_W)),
                  pl.BlockSpec((MEM_LEN, 2 * MEM_W), lambda i: (0, 0)), vec, vec],
        out_specs=[pl.BlockSpec((tm, MEM_W), lambda i: (i, 0)), pl.BlockSpec((tm, MEM_W), lambda i: (i, 0))],
        out_shape=[jax.ShapeDtypeStruct((SEQ, MEM_W), F32), jax.ShapeDtypeStruct((SEQ, MEM_W), BF16)],
        compiler_params=_params(),
    )(proj, proj, kv, qg4, kg4)


def _mem_bwd(proj, om, dyc, kv, hm, mem, mgain, wkv, qg4, kg4):
    tm = 512
    nsteps = SEQ // tm

    def body(q_ref, g_ref, om_ref, dy_ref, kv_ref, hm_ref, mem_ref, mg_ref, w_ref, qg_ref, kg_ref,
             dq_ref, dgt_ref, gqg_ref, gkg_ref, gw_ref, gmg_ref, dmk_ref, dmv_ref, gq_acc):
        i = pl.program_id(0)
        bd = _head_blockdiag()
        lo = _lo_mask(tm)
        lom = _lo_mask(MEM_LEN)

        @pl.when(i == 0)
        def _():
            dmk_ref[...] = jnp.zeros_like(dmk_ref)
            dmv_ref[...] = jnp.zeros_like(dmv_ref)
            gq_acc[...] = jnp.zeros_like(gq_acc)

        for p in range(2):
            cs = slice(p * LANES, (p + 1) * LANES)
            mkn, mvp, _, _ = _mem_keys(kv_ref, kg_ref, bd, p)
            gqs = qg_ref[:, cs] * QK_SCALE
            q = q_ref[:, cs]
            r = lax.rsqrt(_headsum(q * q, bd) * (1.0 / HEAD_DIM) + EPS)
            z = q * r
            qn = z * gqs
            qnb = qn.astype(BF16)
            g = g_ref[:, cs]
            ov = om_ref[:, cs]
            dym = dy_ref[:, cs]
            sg = _sigmoid(g)
            dgt_ref[:, cs] = (dym * ov * (sg * (1.0 + g * (1.0 - sg)))).astype(BF16)
            do = dym * (g * sg)
            dob = do.astype(BF16)
            delta = _headsum(do * ov, bd)
            parts = []
            for h in range(2):
                mh = lo if h == 0 else ~lo
                hc = h * HEAD_DIM
                qh = jnp.where(mh, qn, 0.0).astype(BF16)
                doh = jnp.where(mh, do, 0.0).astype(BF16)
                s = _dot_nt(qh, mkn)
                e = jnp.exp(s - jnp.max(s, axis=-1, keepdims=True))
                pr = e * (1.0 / jnp.sum(e, axis=-1, keepdims=True))
                dp = _dot_nt(doh, mvp)
                ds = (pr * (dp - delta[:, hc:hc + 1])).astype(BF16)
                parts.append((_dot(ds, mkn), _dot_tn(ds, qnb), _dot_tn(pr.astype(BF16), dob)))
            dqn = jnp.where(lo, parts[0][0], parts[1][0])
            dmk_ref[:, cs] += jnp.where(lom, parts[0][1], parts[1][1])
            dmv_ref[:, cs] += jnp.where(lom, parts[0][2], parts[1][2])
            dz = dqn * gqs
            dq_ref[:, cs] = (r * (dz - z * (_headsum(dz * z, bd) * (1.0 / HEAD_DIM)))).astype(BF16)
            gq_acc[:, cs] += jnp.sum(dqn * z, axis=0, keepdims=True) * QK_SCALE

        @pl.when(i == nsteps - 1)
        def _():
            gqg_ref[...] = jnp.zeros_like(gqg_ref)
            gkg_ref[...] = jnp.zeros_like(gkg_ref)
            gqg_ref[0:1, :] = _fold_heads(gq_acc[:, 0:LANES] + gq_acc[:, LANES:2 * LANES])
            dkv = []
            gk = jnp.zeros((1, LANES), F32)
            for p in range(2):
                cs = slice(p * LANES, (p + 1) * LANES)
                _, _, r, z = _mem_keys(kv_ref, kg_ref, bd, p)
                dn = dmk_ref[:, cs]
                dz = dn * kg_ref[:, cs]
                gk = gk + jnp.sum(dn * z, axis=0, keepdims=True)
                dkv.append(r * (dz - z * (_headsum(dz * z, bd) * (1.0 / HEAD_DIM))))
            gkg_ref[0:1, :] = _fold_heads(gk)
            dkvb = jnp.concatenate(dkv + [dmv_ref[...]], axis=1).astype(BF16)
            gw_ref[...] = _dot_tn(hm_ref[...], dkvb)
            dhm = _dot_nt(dkvb, w_ref[...])
            mv = mem_ref[...]
            zm = mv * lax.rsqrt(jnp.mean(mv * mv, axis=-1, keepdims=True) + EPS)
            _put_rows(gmg_ref, jnp.sum(dhm * zm, axis=0, keepdims=True))

    const = lambda shape: pl.BlockSpec(shape, lambda i: (0,) * len(shape))
    row = lambda j: pl.BlockSpec((tm, MEM_W), lambda i: (i, j))
    blk8 = jax.ShapeDtypeStruct((8, LANES), F32)
    return _call(
        body, name="mem_bwd", grid=(nsteps,),
        in_specs=[row(C_MQ // MEM_W), row(C_MG // MEM_W), row(0), row((GMLP_W + ATTN_W) // MEM_W),
                  const((MEM_LEN, 2 * MEM_W)), const((MEM_LEN, D_MODEL)), const((MEM_LEN, D_MODEL)),
                  const((1, D_MODEL)), const((D_MODEL, 2 * MEM_W)), const((1, MEM_W)), const((1, MEM_W))],
        out_specs=[row(0), row(0), const((8, LANES)), const((8, LANES)),
                   const((D_MODEL, 2 * MEM_W)), const((8, LANES))],
        out_shape=[jax.ShapeDtypeStruct((SEQ, MEM_W), BF16), jax.ShapeDtypeStruct((SEQ, MEM_W), BF16),
                   blk8, blk8, jax.ShapeDtypeStruct((D_MODEL, 2 * MEM_W), F32), blk8],
        scratch_shapes=[pltpu.VMEM((MEM_LEN, MEM_W), F32), pltpu.VMEM((MEM_LEN, MEM_W), F32),
                        pltpu.VMEM((1, MEM_W), F32)],
        compiler_params=_params(),
    )(proj, proj, om, dyc, kv, hm, mem, mgain, wkv, qg4, kg4)


def _out_loss(yg, ya, ym, x, tgt, wo):
    tm = 512
    nsteps = SEQ // tm
    parts = ((0, GMLP_W), (GMLP_W, ATTN_W), (GMLP_W + ATTN_W, MEM_W))

    def body(yg_ref, ya_ref, ym_ref, x_ref, t_ref, w_ref, dy_ref, dyc_ref, gw_ref, ls_ref):
        i = pl.program_id(0)

        @pl.when(i == 0)
        def _():
            gw_ref[...] = jnp.zeros_like(gw_ref)
            ls_ref[...] = jnp.zeros_like(ls_ref)

        ys = (yg_ref[...], ya_ref[...], ym_ref[...])
        y = sum(_dot(yv, w_ref[r0:r0 + n, :]) for yv, (r0, n) in zip(ys, parts))
        err = x_ref[...] + y - t_ref[...]
        _put_rows(ls_ref, jnp.sum(err * err, axis=0, keepdims=True), accumulate=True)
        dy = err * (1.0 / D_MODEL)
        dy_ref[...] = dy
        dyb = dy.astype(BF16)
        dyc_ref[...] = _dot_nt(dyb, w_ref[...])
        for yv, (r0, n) in zip(ys, parts):
            gw_ref[r0:r0 + n, :] += _dot_tn(yv, dyb)

    row = lambda w: pl.BlockSpec((tm, w), lambda i: (i, 0))
    const = lambda shape: pl.BlockSpec(shape, lambda i: (0, 0))
    return _call(
        body, name="out_loss", grid=(nsteps,),
        in_specs=[row(GMLP_W), row(ATTN_W), row(MEM_W), row(D_MODEL), row(D_MODEL), const((D_MODEL, D_MODEL))],
        out_specs=[row(D_MODEL), row(D_MODEL), const((D_MODEL, D_MODEL)), const((8, LANES))],
        out_shape=[jax.ShapeDtypeStruct((SEQ, D_MODEL), F32), jax.ShapeDtypeStruct((SEQ, D_MODEL), F32),
                   jax.ShapeDtypeStruct((D_MODEL, D_MODEL), F32), jax.ShapeDtypeStruct((8, LANES), F32)],
        compiler_params=_params(),
    )(yg, ya, ym, x, tgt, wo)


def _proj_bwd(x, dy, gain, wt, dg, daq, dak, dav, dag, dmq, dmg):
    tm = 256
    nsteps = SEQ // tm
    late_w = D_MODEL - PB_EARLY
    stage_at = (0, 3, 11, 15)
    pieces = ((C_GU, 3 * GMLP_W), (C_AQ, ATTN_W), (C_AK, ATTN_W), (C_AV, ATTN_W), (C_AG, ATTN_W),
              (C_MQ, MEM_W), (C_MG, MEM_W))
    nbufs = len(RS_KINDS) - 1

    def body(x_ref, dy_ref, g_ref, wt_hbm, p0, p1, p2, p3, p4, p5, p6, gx_ref, early_hbm, late_hbm, gg_ref,
             wt_v, acc_e, acc_l, *rest):
        rs_bufs, (send_sems, recv_sems, local_sems, wt_sem) = rest[:nbufs], rest[nbufs:]
        phase, i = pl.program_id(0), pl.program_id(1)
        prefs = (p0, p1, p2, p3, p4, p5, p6)

        def half_block(xx, yy, half):
            n = IN_W // 8
            return acc_e.at[pl.ds(pl.multiple_of((2 * xx + yy) * 2 * n + half * n, 8), n), :]

        stages = _rs_stages((half_block,), (early_hbm,), rs_bufs, send_sems, recv_sems, local_sems, [PB_EARLY])
        wt_load = pltpu.make_async_copy(wt_hbm, wt_v, wt_sem)

        @pl.when((phase == 0) & (i == 0))
        def _():
            wt_load.start()
            acc_e[...] = jnp.zeros_like(acc_e)
            acc_l[...] = jnp.zeros_like(acc_l)
            gg_ref[...] = jnp.zeros_like(gg_ref)

        xv = x_ref[...]
        r = lax.rsqrt(jnp.mean(xv * xv, axis=-1, keepdims=True) + EPS)
        z = xv * r
        g = g_ref[...]
        h = (z * g).astype(BF16)

        @pl.when(phase == 0)
        def _():
            for pref, (c0, w) in zip(prefs, pieces):
                acc_e[c0:c0 + w, :] += _dot_tn(pref[...], h[:, :PB_EARLY])

        @pl.when(phase == 1)
        def _():
            pl.when(i == 0)(wt_load.wait)
            for k, at in enumerate(stage_at):
                pl.when(i == at)(stages[k])
            dh = jnp.zeros((tm, D_MODEL), F32)
            for pref, (c0, w) in zip(prefs, pieces):
                dp = pref[...]
                dh = dh + _dot(dp, wt_v[c0:c0 + w, :])
                acc_l[c0:c0 + w, :] += _dot_tn(dp, h[:, PB_EARLY:])
            _put_rows(gg_ref, jnp.sum(dh * z, axis=0, keepdims=True), accumulate=True)
            dz = dh * g
            gx_ref[...] = dy_ref[...] + r * (dz - z * jnp.mean(dz * z, axis=-1, keepdims=True))

            @pl.when(i == nsteps - 1)
            def _():
                stages[len(stage_at)]()
                pltpu.sync_copy(acc_l, late_hbm)

    row = lambda w: pl.BlockSpec((tm, w), lambda p, i: (i, 0))
    second = pl.BlockSpec((tm, D_MODEL), lambda p, i: (i * p, 0))
    hbm = pl.BlockSpec(memory_space=pl.ANY)
    return _call(
        body, name="proj_bwd", grid=(2, nsteps),
        in_specs=[row(D_MODEL), second, pl.BlockSpec((1, D_MODEL), lambda p, i: (0, 0)), hbm] + [row(w) for _, w in pieces],
        out_specs=[second, hbm, hbm, pl.BlockSpec((8, LANES), lambda p, i: (0, 0))],
        out_shape=[jax.ShapeDtypeStruct((SEQ, D_MODEL), F32), jax.ShapeDtypeStruct((2, IN_W // 8, PB_EARLY), F32),
                   jax.ShapeDtypeStruct((IN_W, late_w), F32), jax.ShapeDtypeStruct((8, LANES), F32)],
        scratch_shapes=[pltpu.VMEM((IN_W, D_MODEL), BF16), pltpu.VMEM((IN_W, PB_EARLY), F32), pltpu.VMEM((IN_W, late_w), F32)]
        + _rs_scratch([(IN_W, PB_EARLY)], in_vmem=True)
        + [pltpu.SemaphoreType.DMA((RS_SEMS,)), pltpu.SemaphoreType.DMA((RS_SEMS,)), pltpu.SemaphoreType.DMA((1,)),
           pltpu.SemaphoreType.DMA],
        compiler_params=_params(),
    )(x, dy, gain, wt, dg, daq, dak, dav, dag, dmq, dmg)


def _gather_weights(wt_sh, wkv_sh, wo_sh):
    shards = (wt_sh, wkv_sh, wo_sh)
    nrows = tuple(a.shape[0] for a in shards)

    def body(a0, a1, a2, o0, o1, o2, send_sems, recv_sems):
        x, y, c = lax.axis_index("x"), lax.axis_index("y"), lax.axis_index("c")
        sib, xn, yn = (x, y, 1 - c), (1 - x, y, c), (x, 1 - y, c)
        me, cx, cy, cd = 2 * x + y, 2 * (1 - x) + y, 2 * x + (1 - y), 2 * (1 - x) + (1 - y)
        ins, outs = (a0, a1, a2), (o0, o1, o2)

        def part(a, chip, hf, quarter=None):
            n = nrows[a] // 2
            base = chip * nrows[a] + hf * n
            if quarter is not None:
                n = n // 2
                base = base + quarter * n
            return outs[a].at[pl.ds(pl.multiple_of(base, 16), n), :]

        def copy(k, ref, to):
            return pltpu.make_async_remote_copy(src_ref=ref, dst_ref=ref, send_sem=send_sems.at[k],
                                                recv_sem=recv_sems.at[k], device_id=to, device_id_type=MESH)

        started = []

        def go(cp):
            cp.start()
            started.append(cp)

        for a in range(3):
            outs[a][pl.ds(pl.multiple_of(me * nrows[a], 16), nrows[a]), :] = ins[a][...].astype(BF16)
        for a in range(3):
            go(copy(8 * a, part(a, me, c), xn))
            go(copy(8 * a + 1, part(a, me, c), yn))
        for a in range(3):
            k = 8 * a
            copy(k, part(a, cx, c), xn).wait_recv()
            go(copy(k + 4, part(a, cx, c, 1), yn))
            go(copy(k + 2, part(a, cx, c), sib))
            copy(k + 1, part(a, cy, c), yn).wait_recv()
            go(copy(k + 5, part(a, cy, c, 0), xn))
            go(copy(k + 3, part(a, cy, c), sib))
        for a in range(3):
            k = 8 * a
            copy(k + 4, part(a, cd, c, 1), yn).wait_recv()
            go(copy(k + 7, part(a, cd, c, 1), sib))
            copy(k + 5, part(a, cd, c, 0), xn).wait_recv()
            go(copy(k + 6, part(a, cd, c, 0), sib))
        for a in range(3):
            k = 8 * a
            copy(k + 2, part(a, cx, 1 - c), sib).wait_recv()
            copy(k + 3, part(a, cy, 1 - c), sib).wait_recv()
            copy(k + 6, part(a, cd, 1 - c, 0), sib).wait_recv()
            copy(k + 7, part(a, cd, 1 - c, 1), sib).wait_recv()
        for cp in started:
            cp.wait_send()

    return _call(
        body, name="gather_weights",
        out_shape=[jax.ShapeDtypeStruct((4 * a.shape[0], a.shape[1]), BF16) for a in shards],
        in_specs=[pl.BlockSpec(memory_space=pltpu.VMEM)] * 3,
        out_specs=[pl.BlockSpec(memory_space=pltpu.VMEM)] * 3,
        scratch_shapes=[pltpu.SemaphoreType.DMA((24,)), pltpu.SemaphoreType.DMA((24,))],
        compiler_params=_params(),
    )(*shards)


RS_SEMS = 6
RS_KINDS = (((2, 2), 1, F32), ((2, 2), 1, F32), ((2, 2), 2, BF16), ((2, 2), 2, BF16), ((2, 2), 2, F32),
            ((2,), 2, BF16), ((2,), 2, BF16), ((2,), 1, F32))


def _rs_view(g):
    return g.reshape(2, 2, 2, g.shape[0] // 8, g.shape[1])


def _rs_scratch(shapes, in_vmem=False):
    kinds = RS_KINDS[1:] if in_vmem else RS_KINDS
    return [pltpu.VMEM(lead + (r // 8, w // split), dt) for lead, split, dt in kinds for r, w in shapes]


def _rs_stages(gs, outs, bufs, send_sems, recv_sems, local_sems, widths):
    n = len(gs)
    if len(bufs) < n * len(RS_KINDS):
        bufs = [None] * n + list(bufs)
    loc, ra, s_b, r_b, acc1, s_c, r_c, fin = (bufs[n * i:n * i + n] for i in range(len(RS_KINDS)))
    half_w = [w // 2 for w in widths]
    chips = [(xx, yy) for xx in range(2) for yy in range(2)]
    x, y, c = lax.axis_index("x"), lax.axis_index("y"), lax.axis_index("c")
    sib, xn, yn = (x, y, 1 - c), (1 - x, y, c), (x, 1 - y, c)

    def copy(a, j, src, dst, to):
        k = RS_SEMS * a + j
        return pltpu.make_async_remote_copy(src_ref=src, dst_ref=dst, send_sem=send_sems.at[k],
                                            recv_sem=recv_sems.at[k], device_id=to, device_id_type=MESH)

    def step_a(a):
        if callable(gs[a]):
            return [copy(a, 0, gs[a](xx, yy, 1 - c), ra[a].at[xx, yy], sib) for xx, yy in chips]
        return [copy(a, 0, gs[a].at[:, :, 1 - c], ra[a], sib),
                pltpu.make_async_copy(gs[a].at[:, :, c], loc[a], local_sems.at[a])]

    def finish_a(a):
        if callable(gs[a]):
            copy(a, 0, ra[a], ra[a], sib).wait()
            for xx, yy in chips:
                ra[a][xx, yy] = gs[a](xx, yy, c)[...] + ra[a][xx, yy]
        else:
            for cp in step_a(a):
                cp.wait()
            ra[a][...] = loc[a][...] + ra[a][...]

    def step_b(a):
        return copy(a, 1, s_b[a].at[0], r_b[a].at[0], xn), copy(a, 2, s_b[a].at[1], r_b[a].at[1], yn)

    def step_c(a):
        return copy(a, 3, s_c[a].at[0], r_c[a].at[0], yn), copy(a, 4, s_c[a].at[1], r_c[a].at[1], xn)

    def step_d(a, half):
        rows = fin[a].at[half]
        return copy(a, 5, rows, rows, sib)

    def start():
        for a in range(n):
            for cp in step_a(a):
                cp.start()

    def a_to_b():
        for a in range(n):
            finish_a(a)
            s_b[a][0] = ra[a][1 - x, :, :, :half_w[a]].astype(BF16)
            s_b[a][1] = ra[a][:, 1 - y, :, half_w[a]:].astype(BF16)
            for cp in step_b(a):
                cp.start()

    def b_to_c():
        for a in range(n):
            for cp in step_b(a):
                cp.wait()
            acc1[a][0] = ra[a][x, :, :, :half_w[a]] + r_b[a][0].astype(F32)
            acc1[a][1] = ra[a][:, y, :, half_w[a]:] + r_b[a][1].astype(F32)
            s_c[a][0] = acc1[a][0, 1 - y].astype(BF16)
            s_c[a][1] = acc1[a][1, 1 - x].astype(BF16)
            for cp in step_c(a):
                cp.start()

    def c_to_d():
        for a in range(n):
            for cp in step_c(a):
                cp.wait()
            fin[a][c, :, :half_w[a]] = acc1[a][0, y] + r_c[a][0].astype(F32)
            fin[a][c, :, half_w[a]:] = acc1[a][1, x] + r_c[a][1].astype(F32)
            step_d(a, c).start()

    def finish():
        for a in range(n):
            step_d(a, 1 - c).wait_recv()
            step_d(a, c).wait_send()
            pltpu.sync_copy(fin[a], outs[a])

    return start, a_to_b, b_to_c, c_to_d, finish


def _reduce_grads(gwt, g_ws, tiny):
    grads = (gwt,)

    def body(g0, ws_in, tiny_in, o0, o_ws, o_tiny, *rest):
        nb = len(RS_KINDS)
        sm, sa, sb, sc, acc_s, send_sems, recv_sems, local_sems = rest[nb:]
        start, a_to_b, b_to_c, c_to_d, finish = _rs_stages((g0,), (o0,), rest[:nb], send_sems, recv_sems, local_sems,
                                                           [gwt.shape[1]])
        n_ws = ws_in.shape[0]
        sm[0:n_ws, :] = ws_in[...]
        sm[n_ws:, :] = tiny_in[...]
        x, y, c = lax.axis_index("x"), lax.axis_index("y"), lax.axis_index("c")

        def small(j, src, dst, to):
            k = RS_SEMS + j
            return pltpu.make_async_remote_copy(src_ref=src, dst_ref=dst, send_sem=send_sems.at[k],
                                                recv_sem=recv_sems.at[k], device_id=to, device_id_type=MESH)

        along_c, along_x, along_y = (small(0, sm, sa, (x, y, 1 - c)), small(1, acc_s, sb, (1 - x, y, c)),
                                     small(2, sb, sc, (x, 1 - y, c)))
        start()
        along_c.start()
        a_to_b()
        along_c.wait()
        acc_s[...] = sm[...] + sa[...]
        along_x.start()
        b_to_c()
        along_x.wait()
        sb[...] = acc_s[...] + sb[...]
        along_y.start()
        c_to_d()
        along_y.wait()
        o_ws[...] = sb[0:n_ws, :] + sc[0:n_ws, :]
        o_tiny[...] = sb[n_ws:, :] + sc[n_ws:, :]
        finish()

    vm = pl.BlockSpec(memory_space=pltpu.VMEM)
    hbm = pl.BlockSpec(memory_space=pl.ANY)
    small_shape = (g_ws.shape[0] + tiny.shape[0], LANES)
    scratch = _rs_scratch([g.shape for g in grads]) + [pltpu.VMEM(small_shape, F32) for _ in range(5)]
    scratch += [pltpu.SemaphoreType.DMA((RS_SEMS + 3,)), pltpu.SemaphoreType.DMA((RS_SEMS + 3,)), pltpu.SemaphoreType.DMA((1,))]
    return _call(
        body, name="reduce_grads",
        out_shape=[jax.ShapeDtypeStruct((2, gwt.shape[0] // 8, gwt.shape[1]), F32),
                   jax.ShapeDtypeStruct(g_ws.shape, F32), jax.ShapeDtypeStruct(tiny.shape, F32)],
        in_specs=[hbm, vm, vm],
        out_specs=[hbm, vm, vm],
        scratch_shapes=scratch,
        compiler_params=_params(),
    )(_rs_view(gwt), g_ws, tiny)


def _adam_update(w, g, m, v):
    nm = ADAM_B1 * m + (1.0 - ADAM_B1) * g
    nv = ADAM_B2 * v + (1.0 - ADAM_B2) * (g * g)
    m_hat = nm / (1.0 - ADAM_B1 ** ADAM_STEP)
    v_hat = nv / (1.0 - ADAM_B2 ** ADAM_STEP)
    return -ADAM_LR * (m_hat / (jnp.sqrt(v_hat) + ADAM_EPS) + ADAM_WD * w), nm, nv


def _adamw(w, g, m, v):
    rows, cols = w.shape
    tm = max(t for t in range(8, 257, 8) if rows % t == 0)
    parts = tuple(g) if isinstance(g, (tuple, list)) else (g,)
    n = len(parts)

    def body(w_ref, m_ref, v_ref, *refs):
        gv = jnp.concatenate([r[...] for r in refs[:n]], axis=1)
        d_ref, nm_ref, nv_ref = refs[n:n + 3]
        d_ref[...], nm_ref[...], nv_ref[...] = _adam_update(w_ref[...], gv, m_ref[...], v_ref[...])
        if n > 1:
            refs[n + 3][...] = gv

    blk = pl.BlockSpec((tm, cols), lambda i: (i, 0))
    nout = 3 if n == 1 else 4
    res = _call(
        body, name="adamw", grid=(rows // tm,),
        in_specs=[blk] * 3 + [pl.BlockSpec((tm, p.shape[1]), lambda i: (i, 0)) for p in parts], out_specs=[blk] * nout,
        out_shape=[jax.ShapeDtypeStruct((rows, cols), F32)] * nout,
        compiler_params=_params(),
    )(w, m, v, *parts)
    return (parts[0] if n == 1 else res[3], *res[:3])


def _adamw_tiny(tiny, weights, ms, vs):
    shapes = [w.shape for w in weights]
    n = len(weights)

    def grad_of(t_ref, k, shape):
        base = 8 * k
        if shape[1] > LANES:
            return [t_ref[base + j:base + j + 1, :] for j in range(shape[1] // LANES)]
        return [t_ref[base:base + shape[0], 0:shape[1]]]

    def body(t_ref, *refs):
        w_refs, m_refs, v_refs = refs[:n], refs[n:2 * n], refs[2 * n:3 * n]
        loss_ref, outs = refs[3 * n], refs[3 * n + 1:]
        loss_ref[...] = (0.5 / D_MODEL) * jnp.sum(t_ref[8 * n:8 * n + 8, :], keepdims=True)
        for k, shape in enumerate(shapes):
            g_ref, d_ref, nm_ref, nv_ref = outs[4 * k:4 * k + 4]
            for j, g in enumerate(grad_of(t_ref, k, shape)):
                cols = slice(j * LANES, (j + 1) * LANES) if shape[1] > LANES else slice(None)
                g_ref[:, cols] = g
                d_ref[:, cols], nm_ref[:, cols], nv_ref[:, cols] = _adam_update(
                    w_refs[k][:, cols], g, m_refs[k][:, cols], v_refs[k][:, cols])

    out_shape = [jax.ShapeDtypeStruct((1, 1), F32)]
    for shape in shapes:
        out_shape += [jax.ShapeDtypeStruct(shape, F32)] * 4
    return _call(body, name="adamw_tiny", out_shape=out_shape, compiler_params=_params())(tiny, *weights, *ms, *vs)


def _local_grads(x, mem, tgt, norm_gain, wt, gmlp_v_gain, gmlp_w_s, gmlp_b, attn_q_gain, attn_k_gain,
                 mem_norm_gain, wkv, mem_q_gain, mem_k_gain, wo):
    vg = gmlp_v_gain.reshape(1, GMLP_W)
    bias_full = jnp.repeat(gmlp_b.T, HEAD_DIM, axis=1)
    gq2, gk2 = jnp.tile(attn_q_gain, (1, 2)), jnp.tile(attn_k_gain, (1, 2))
    qg4, kg4 = jnp.tile(mem_q_gain, (1, 4)), jnp.tile(mem_k_gain, (1, 4))

    proj = _fwd_proj(x, norm_gain, wt)
    yg = _gmlp_fwd(proj, vg, gmlp_w_s, bias_full)
    o, lse, ya = _attn_fwd(proj, gq2, gk2)
    kv, hm = _mem_kv(mem, mem_norm_gain, wkv)
    om, ym = _mem_fwd(proj, kv, qg4, kg4)
    dy, dyc, g_wo, err2 = _out_loss(yg, ya, ym, x, tgt, wo)
    dmq, dmg, g_mq, g_mk, g_wkv, g_mng = _mem_bwd(proj, om, dyc, kv, hm, mem, mem_norm_gain, wkv, qg4, kg4)
    daq, dak, dav, dag, g_aq, g_ak, g_wkv_sh, g_wo_sh = _attn_bwd(proj, o, lse, dyc, gq2, gk2, g_wkv, g_wo)
    dg, g_ws, g_b, g_vg = _gmlp_bwd(proj, dyc, vg, gmlp_w_s, bias_full)
    gx, g_wt_early_sh, g_wt_late, g_ng = _proj_bwd(x, dy, norm_gain, wt, dg, daq, dak, dav, dag, dmq, dmg)

    tiny = jnp.concatenate([g_ng, g_vg, g_b, g_aq, g_ak, g_mng, g_mq, g_mk, err2], axis=0)
    return gx, g_wt_early_sh, g_wt_late, g_wkv_sh, g_wo_sh, g_ws.reshape(4 * CHUNK, CHUNK), tiny


def kernel(x, mem, norm_gain, w_in, gmlp_v_gain, gmlp_w_s, gmlp_b, attn_q_gain, attn_k_gain, mem_norm_gain, w_mem_kv, mem_q_gain, mem_k_gain, w_out, loss_target, m_norm_gain, m_w_in, m_gmlp_v_gain, m_gmlp_w_s, m_gmlp_b, m_attn_q_gain, m_attn_k_gain, m_mem_norm_gain, m_w_mem_kv, m_mem_q_gain, m_mem_k_gain, m_w_out, v_norm_gain, v_w_in, v_gmlp_v_gain, v_gmlp_w_s, v_gmlp_b, v_attn_q_gain, v_attn_k_gain, v_mem_norm_gain, v_w_mem_kv, v_mem_q_gain, v_mem_k_gain, v_w_out):
    wt, wkv, wo = _gather_weights(w_in[0].T, w_mem_kv[0], w_out[0])
    gx, g_wt_early_sh, g_wt_late, g_wkv_sh, g_wo_sh, g_ws, tiny = _local_grads(
        x[0], mem[0], loss_target[0], norm_gain, wt, gmlp_v_gain[0], gmlp_w_s[0], gmlp_b[0],
        attn_q_gain, attn_k_gain, mem_norm_gain, wkv, mem_q_gain, mem_k_gain, wo)
    g_wt_late_sh, g_ws, tiny = _reduce_grads(g_wt_late, g_ws, tiny)
    chip_block = lambda g: g.reshape(2 * g.shape[1], g.shape[2])
    g_wt_sh = (chip_block(g_wt_early_sh), chip_block(g_wt_late_sh))
    g_wkv_sh, g_wo_sh = chip_block(g_wkv_sh), chip_block(g_wo_sh)

    ws = (norm_gain, w_in, gmlp_v_gain, gmlp_w_s, gmlp_b, attn_q_gain, attn_k_gain, mem_norm_gain, w_mem_kv,
          mem_q_gain, mem_k_gain, w_out)
    ms = (m_norm_gain, m_w_in, m_gmlp_v_gain, m_gmlp_w_s, m_gmlp_b, m_attn_q_gain, m_attn_k_gain, m_mem_norm_gain,
          m_w_mem_kv, m_mem_q_gain, m_mem_k_gain, m_w_out)
    vs = (v_norm_gain, v_w_in, v_gmlp_v_gain, v_gmlp_w_s, v_gmlp_b, v_attn_q_gain, v_attn_k_gain, v_mem_norm_gain,
          v_w_mem_kv, v_mem_q_gain, v_mem_k_gain, v_w_out)
    form = {1: lambda a: a[0].T, 3: lambda a: a.reshape(4 * CHUNK, CHUNK), 2: lambda a: a[0], 4: lambda a: a[0],
            8: lambda a: a[0], 11: lambda a: a[0]}
    back = {1: lambda a: a.T[None], 3: lambda a: a.reshape(1, 4, CHUNK, CHUNK), 2: lambda a: a[None],
            4: lambda a: a[None], 8: lambda a: a[None], 11: lambda a: a[None]}
    fwd = lambda t, i: form.get(i, lambda a: a)(t[i])
    out = {}
    for i, g in ((1, g_wt_sh), (3, g_ws), (8, g_wkv_sh), (11, g_wo_sh)):
        out[i] = _adamw(fwd(ws, i), g, fwd(ms, i), fwd(vs, i))
    res = _adamw_tiny(tiny, [fwd(ws, i) for i in TINY_ORDER], [fwd(ms, i) for i in TINY_ORDER],
                      [fwd(vs, i) for i in TINY_ORDER])
    for k, i in enumerate(TINY_ORDER):
        out[i] = res[1 + 4 * k:5 + 4 * k]
    leaves = [[back.get(i, lambda a: a)(out[i][j]) for i in range(12)] for j in range(4)]
    return (res[0].reshape(()), gx[None], *leaves[0], *leaves[1], *leaves[2], *leaves[3])
```

```python
import functools
import math

import jax
import jax.numpy as jnp
from jax import lax
from jax.experimental import pallas as pl
from jax.experimental.pallas import tpu as pltpu

F32 = jnp.float32
BF16 = jnp.bfloat16

SEQ = 4096
D_MODEL = 1024
HEAD_DIM = 64
LANES = 128
CHUNK = 128
GMLP_W, ATTN_W, MEM_W = 256, 512, 256
IN_W = 3 * GMLP_W + 4 * ATTN_W + 2 * MEM_W
MEM_LEN = 256
DILATIONS = (1, 4, 16)
EPS = 1e-6
QK_SCALE = 1.0 / math.sqrt(HEAD_DIM)
C_GU, C_GV, C_GG, C_AQ, C_AK, C_AV, C_AG, C_MQ, C_MG = 0, 256, 512, 768, 1280, 1792, 2304, 2816, 3072

ADAM_LR, ADAM_B1, ADAM_B2, ADAM_EPS, ADAM_WD, ADAM_STEP = 0.001, 0.9, 0.999, 1e-08, 0.01, 10

VMEM_LIMIT = 48 * 1024 * 1024
RS_CHUNKS = 4
ATTN_UNROLL = 4
MESH = pl.DeviceIdType.MESH

TINY_ORDER = (0, 2, 4, 5, 6, 7, 9, 10)


def _call(body, **kw):
    return pl.pallas_call(body, **kw)


def _params(**kw):
    return pltpu.CompilerParams(vmem_limit_bytes=VMEM_LIMIT, **kw)


def _dot(a, b):
    return jnp.dot(a, b, preferred_element_type=F32)


def _dot_nt(a, b):
    return lax.dot_general(a, b, (((1,), (1,)), ((), ())), preferred_element_type=F32)


def _dot_tn(a, b):
    return lax.dot_general(a, b, (((0,), (0,)), ((), ())), preferred_element_type=F32)


def _head_blockdiag():
    r = lax.shift_right_logical(lax.broadcasted_iota(jnp.int32, (LANES, LANES), 0), 6)
    c = lax.shift_right_logical(lax.broadcasted_iota(jnp.int32, (LANES, LANES), 1), 6)
    return jnp.where(r == c, 1.0, 0.0).astype(BF16)


def _headsum(v, bd):
    hi = v.astype(BF16)
    lo = (v - hi.astype(F32)).astype(BF16)
    return _dot(hi, bd) + _dot(lo, bd)


def _lo_mask(rows):
    return lax.broadcasted_iota(jnp.int32, (rows, LANES), 1) < HEAD_DIM


def _sigmoid(x):
    return 1.0 / (1.0 + jnp.exp(-x))


def _fold_heads(v):
    return v + pltpu.roll(v, HEAD_DIM, 1)


def _put_rows(ref, vec, accumulate=False):
    for j in range(vec.shape[1] // LANES):
        piece = vec[:, j * LANES:(j + 1) * LANES]
        ref[j:j + 1, :] = ref[j:j + 1, :] + piece if accumulate else piece


def _fwd_proj(x, gain, wt):
    tm = 512

    def body(x_ref, g_ref, wt_ref, o_ref):
        xv = x_ref[...]
        ms = jnp.mean(xv * xv, axis=-1, keepdims=True)
        h = (xv * lax.rsqrt(ms + EPS) * g_ref[...]).astype(BF16)
        o_ref[...] = _dot_nt(h, wt_ref[...])

    return _call(
        body, name="fwd_proj", grid=(SEQ // tm,),
        in_specs=[pl.BlockSpec((tm, D_MODEL), lambda i: (i, 0)),
                  pl.BlockSpec((1, D_MODEL), lambda i: (0, 0)),
                  pl.BlockSpec((IN_W, D_MODEL), lambda i: (0, 0))],
        out_specs=pl.BlockSpec((tm, IN_W), lambda i: (i, 0)),
        out_shape=jax.ShapeDtypeStruct((SEQ, IN_W), F32),
        compiler_params=_params(),
    )(x, gain, wt)


def _gmlp_weights(w_ref):
    ti = lax.broadcasted_iota(jnp.int32, (CHUNK, CHUNK), 0)
    si = lax.broadcasted_iota(jnp.int32, (CHUNK, CHUNK), 1)
    tril = si <= ti
    return tril, [jnp.where(tril, w_ref[h], 0.0).astype(BF16) for h in range(4)]


def _gmlp_fwd(proj, vgain, w_s, bias_full):
    tm = 512

    def body(p_ref, vg_ref, w_ref, b_ref, y_ref):
        bd = _head_blockdiag()
        lo = _lo_mask(CHUNK)
        _, wm = _gmlp_weights(w_ref)
        for c in range(tm // CHUNK):
            rows = pl.ds(c * CHUNK, CHUNK)
            for p in range(2):
                cs = slice(p * LANES, (p + 1) * LANES)
                u = p_ref[rows, C_GU + p * LANES:C_GU + (p + 1) * LANES]
                v = p_ref[rows, C_GV + p * LANES:C_GV + (p + 1) * LANES]
                gt = p_ref[rows, C_GG + p * LANES:C_GG + (p + 1) * LANES]
                r = lax.rsqrt(_headsum(v * v, bd) * (1.0 / HEAD_DIM) + EPS)
                vn = (v * r * vg_ref[:, cs]).astype(BF16)
                sp = jnp.where(lo, _dot(wm[2 * p], vn), _dot(wm[2 * p + 1], vn)) + b_ref[:, cs]
                y_ref[rows, cs] = (u * sp * (gt * _sigmoid(gt))).astype(BF16)

    return _call(
        body, name="gmlp_fwd", grid=(SEQ // tm,),
        in_specs=[pl.BlockSpec((tm, 3 * GMLP_W), lambda i: (i, 0)),
                  pl.BlockSpec((1, GMLP_W), lambda i: (0, 0)),
                  pl.BlockSpec((4, CHUNK, CHUNK), lambda i: (0, 0, 0)),
                  pl.BlockSpec((CHUNK, GMLP_W), lambda i: (0, 0))],
        out_specs=pl.BlockSpec((tm, GMLP_W), lambda i: (i, 0)),
        out_shape=jax.ShapeDtypeStruct((SEQ, GMLP_W), BF16),
        compiler_params=_params(),
    )(proj, vgain, w_s, bias_full)


def _gmlp_bwd(proj, dyc, vgain, w_s, bias_full):
    tm = 512
    nsteps = SEQ // tm

    def body(p_ref, dy_ref, vg_ref, w_ref, b_ref, dg_ref, gw_ref, gb_ref, gv_ref):
        i = pl.program_id(0)
        bd = _head_blockdiag()
        lo = _lo_mask(CHUNK)
        tril, wm = _gmlp_weights(w_ref)
        ri = lax.broadcasted_iota(jnp.int32, (16, LANES), 0)
        li = lax.broadcasted_iota(jnp.int32, (16, LANES), 1)
        head_rows = [jnp.where(((ri == 2 * p) & (li < HEAD_DIM)) | ((ri == 2 * p + 1) & (li >= HEAD_DIM)), 1.0, 0.0).astype(BF16)
                     for p in range(2)]

        @pl.when(i == 0)
        def _():
            gw_ref[...] = jnp.zeros_like(gw_ref)
            gb_ref[...] = jnp.zeros_like(gb_ref)
            gv_ref[...] = jnp.zeros_like(gv_ref)

        for c in range(tm // CHUNK):
            rows = pl.ds(c * CHUNK, CHUNK)
            for p in range(2):
                cs = slice(p * LANES, (p + 1) * LANES)
                u = p_ref[rows, C_GU + p * LANES:C_GU + (p + 1) * LANES]
                v = p_ref[rows, C_GV + p * LANES:C_GV + (p + 1) * LANES]
                gt = p_ref[rows, C_GG + p * LANES:C_GG + (p + 1) * LANES]
                dy = dy_ref[rows, cs]
                g = vg_ref[:, cs]
                r = lax.rsqrt(_headsum(v * v, bd) * (1.0 / HEAD_DIM) + EPS)
                z = v * r
                vn = (z * g).astype(BF16)
                sp = jnp.where(lo, _dot(wm[2 * p], vn), _dot(wm[2 * p + 1], vn)) + b_ref[:, cs]
                sg = _sigmoid(gt)
                sl = gt * sg
                dsl = sg * (1.0 + gt * (1.0 - sg))
                du = dy * sp * sl
                dsp = dy * u * sl
                dgt = dy * u * sp * dsl
                dspb = dsp.astype(BF16)
                dvn = jnp.where(lo, _dot_tn(wm[2 * p], dspb), _dot_tn(wm[2 * p + 1], dspb))
                gw_ref[2 * p] += _dot_nt(jnp.where(lo, dsp, 0.0).astype(BF16), vn)
                gw_ref[2 * p + 1] += _dot_nt(jnp.where(lo, 0.0, dsp).astype(BF16), vn)
                dsp_lo = (dsp - dspb.astype(F32)).astype(BF16)
                gb_ref[...] += (_dot_nt(head_rows[p], dspb) + _dot_nt(head_rows[p], dsp_lo))[0:8]
                gvp = jnp.sum(dvn * z, axis=0, keepdims=True)
                gv_ref[2 * p:2 * p + 1, :] += gvp
                gv_ref[2 * p + 1:2 * p + 2, :] += pltpu.roll(gvp, HEAD_DIM, 1)
                dz = dvn * g
                dv = r * (dz - z * (_headsum(dz * z, bd) * (1.0 / HEAD_DIM)))
                dg_ref[rows, C_GU + p * LANES:C_GU + (p + 1) * LANES] = du.astype(BF16)
                dg_ref[rows, C_GV + p * LANES:C_GV + (p + 1) * LANES] = dv.astype(BF16)
                dg_ref[rows, C_GG + p * LANES:C_GG + (p + 1) * LANES] = dgt.astype(BF16)

        @pl.when(i == nsteps - 1)
        def _():
            for h in range(4):
                gw_ref[h] = jnp.where(tril, gw_ref[h], 0.0)

    return _call(
        body, name="gmlp_bwd", grid=(nsteps,),
        in_specs=[pl.BlockSpec((tm, 3 * GMLP_W), lambda i: (i, 0)),
                  pl.BlockSpec((tm, GMLP_W), lambda i: (i, 0)),
                  pl.BlockSpec((1, GMLP_W), lambda i: (0, 0)),
                  pl.BlockSpec((4, CHUNK, CHUNK), lambda i: (0, 0, 0)),
                  pl.BlockSpec((CHUNK, GMLP_W), lambda i: (0, 0))],
        out_specs=[pl.BlockSpec((tm, 3 * GMLP_W), lambda i: (i, 0)),
                   pl.BlockSpec((4, CHUNK, CHUNK), lambda i: (0, 0, 0)),
                   pl.BlockSpec((8, LANES), lambda i: (0, 0)),
                   pl.BlockSpec((8, LANES), lambda i: (0, 0))],
        out_shape=[jax.ShapeDtypeStruct((SEQ, 3 * GMLP_W), BF16),
                   jax.ShapeDtypeStruct((4, CHUNK, CHUNK), F32),
                   jax.ShapeDtypeStruct((8, LANES), F32),
                   jax.ShapeDtypeStruct((8, LANES), F32)],
        compiler_params=_params(),
    )(proj, dyc, vgain, w_s, bias_full)


def _band_masks():
    qi = lax.broadcasted_iota(jnp.int32, (CHUNK, 2 * CHUNK), 0)
    kj = lax.broadcasted_iota(jnp.int32, (CHUNK, 2 * CHUNK), 1)
    valid2 = ((kj < CHUNK) & (kj >= qi)) | ((kj >= CHUNK) & (kj - CHUNK <= qi))
    q1 = lax.broadcasted_iota(jnp.int32, (CHUNK, CHUNK), 0)
    k1 = lax.broadcasted_iota(jnp.int32, (CHUNK, CHUNK), 1)
    return k1 <= q1, valid2


def _stack_heads(v, lo):
    return jnp.concatenate([jnp.where(lo, v, 0.0), jnp.where(lo, 0.0, v)], axis=0).astype(BF16)


def _rows_of(ref, start, d):
    if d == 1:
        return ref.at[pl.ds(start if isinstance(start, int) else pl.multiple_of(start, CHUNK), CHUNK), :]
    return ref.at[pl.ds(start, CHUNK, stride=d), :]


def _unrolled(lo, hi, unroll, run):
    groups = (hi - lo) // unroll
    if groups:
        def body(g, carry):
            run([lo + g * unroll + t for t in range(unroll)])
            return carry

        lax.fori_loop(0, groups, body, 0)
    if lo + groups * unroll < hi:
        run(range(lo + groups * unroll, hi))


def _for_blocks(d, group_fn, unroll):
    nblk = SEQ // CHUNK
    sh = d.bit_length() - 1

    def first(j):
        return (j * CHUNK if d == 1 else j, None)

    def rest(j):
        start = (j & (d - 1)) + (j >> sh) * (CHUNK * d)
        return (start, start - CHUNK * d)

    _unrolled(0, d, unroll, lambda js: group_fn(d, [first(j) for j in js]))
    _unrolled(d, nblk, unroll, lambda js: group_fn(d, [rest(j) for j in js]))


def _attn_fwd(proj, gq2, gk2):
    tn = 512

    def body(q_ref, k_ref, v_ref, g_ref, gq_ref, gk_ref, o_ref, l_ref, ya_ref, qn_ref, kn_ref):
        bd = _head_blockdiag()
        lo = _lo_mask(CHUNK)
        valid1, valid2 = _band_masks()

        def norm(t, carry):
            rows = pl.ds(pl.multiple_of(t * tn, tn), tn)
            q = q_ref[rows, :]
            qn_ref[rows, :] = q * lax.rsqrt(_headsum(q * q, bd) * (1.0 / HEAD_DIM) + EPS) * (gq_ref[...] * QK_SCALE)
            k = k_ref[rows, :]
            kn_ref[rows, :] = k * lax.rsqrt(_headsum(k * k, bd) * (1.0 / HEAD_DIM) + EPS) * gk_ref[...]
            return carry

        lax.fori_loop(0, SEQ // tn, norm, 0)

        def load_kv(ref, d, start, prev):
            own = _rows_of(ref, start, d)[...]
            if prev is None:
                return own.astype(BF16)
            return jnp.concatenate([_rows_of(ref, prev, d)[...], own], axis=0).astype(BF16)

        def group(d, blocks):
            valid = valid1 if blocks[0][1] is None else valid2
            valid = jnp.concatenate([valid, valid], axis=0)
            qs = [_rows_of(qn_ref, start, d)[...] for start, _ in blocks]
            ks = [load_kv(kn_ref, d, start, prev) for start, prev in blocks]
            vs = [load_kv(v_ref, d, start, prev) for start, prev in blocks]
            ss = [_dot_nt(_stack_heads(q, lo), k) for q, k in zip(qs, ks)]
            ms, ps, ls = [], [], []
            for s in ss:
                s = jnp.where(valid, s, -jnp.inf)
                m = jnp.max(s, axis=-1, keepdims=True)
                p = jnp.exp(s - m)
                ms.append(m)
                ls.append(jnp.sum(p, axis=-1, keepdims=True))
                ps.append(p.astype(BF16))
            os_ = [_dot(p, v) for p, v in zip(ps, vs)]
            for b, (start, _) in enumerate(blocks):
                on = os_[b] * (1.0 / ls[b])
                ln = ms[b] + jnp.log(ls[b])
                ob = jnp.where(lo, on[:CHUNK], on[CHUNK:])
                lb = jnp.where(lo, ln[:CHUNK], ln[CHUNK:])
                o_rows = _rows_of(o_ref, start, d)
                l_rows = _rows_of(l_ref, start, d)
                if d != DILATIONS[0]:
                    lold = l_rows[...]
                    mx = jnp.maximum(lold, lb)
                    ea = jnp.exp(lold - mx)
                    eb = jnp.exp(lb - mx)
                    inv = 1.0 / (ea + eb)
                    ob = o_rows[...] * (ea * inv) + ob * (eb * inv)
                    lb = mx + jnp.log(ea + eb)
                o_rows[...] = ob
                l_rows[...] = lb

        for d in DILATIONS:
            _for_blocks(d, group, ATTN_UNROLL)

        def fin(t, carry):
            rows = pl.ds(pl.multiple_of(t * tn, tn), tn)
            g = g_ref[rows, :]
            ya_ref[rows, :] = (o_ref[rows, :] * (g * _sigmoid(g))).astype(BF16)
            return carry

        lax.fori_loop(0, SEQ // tn, fin, 0)

    col = lambda c0: pl.BlockSpec((SEQ, LANES), lambda p: (0, c0 // LANES + p))
    vec = pl.BlockSpec((1, LANES), lambda p: (0, 0))
    out = pl.BlockSpec((SEQ, LANES), lambda p: (0, p))
    return _call(
        body, name="attn_fwd", grid=(ATTN_W // LANES,),
        in_specs=[col(C_AQ), col(C_AK), col(C_AV), col(C_AG), vec, vec],
        out_specs=[out, out, out],
        out_shape=[jax.ShapeDtypeStruct((SEQ, ATTN_W), F32), jax.ShapeDtypeStruct((SEQ, ATTN_W), F32),
                   jax.ShapeDtypeStruct((SEQ, ATTN_W), BF16)],
        scratch_shapes=[pltpu.VMEM((SEQ, LANES), F32), pltpu.VMEM((SEQ, LANES), F32)],
        compiler_params=_params(),
    )(proj, proj, proj, proj, gq2, gk2)


def _attn_bwd(proj, o, lse, dyc, gq2, gk2, *ride_along):
    tn = 512
    npairs = ATTN_W // LANES
    nride = len(ride_along)
    nbufs = nride * len(RS_KINDS)

    def body(proj_hbm, o_hbm, l_hbm, dyc_hbm, gq_ref, gk_ref, *rest):
        ride_in, rest = rest[:nride], rest[nride:]
        dq_ref, dk_ref, dv_ref, dgt_ref, gqg_ref, gkg_ref = rest[:6]
        ride_out, rest = rest[6:6 + nride], rest[6 + nride:]
        qb_, kb_, vb_, gb_, ob_, lb_, yb_, dkb_, dvb_, sems = rest[:10]
        rs_bufs, (send_sems, recv_sems, local_sems) = rest[10:10 + nbufs], rest[10 + nbufs:]
        rs_stage = _rs_stages(ride_in, ride_out, rs_bufs, send_sems, recv_sems, local_sems, [g.shape[1] for g in ride_along])
        pair = pl.program_id(0)
        for step in range(npairs):
            pl.when(pair == step)(rs_stage[step])
        bd = _head_blockdiag()
        lo = _lo_mask(CHUNK)
        lo2 = lax.broadcasted_iota(jnp.int32, (2 * CHUNK, LANES), 1) < HEAD_DIM
        valid1, valid2 = _band_masks()
        gqs = gq_ref[...] * QK_SCALE
        gk = gk_ref[...]

        def pcol(c0):
            return proj_hbm.at[:, pl.ds(pl.multiple_of(c0 + pair * LANES, LANES), LANES)]

        def acol(hbm, c0=0):
            return hbm.at[:, pl.ds(pl.multiple_of(c0 + pair * LANES, LANES), LANES)]

        loads = [pltpu.make_async_copy(src, dst, sems.at[n]) for n, (src, dst) in enumerate((
            (pcol(C_AQ), qb_), (pcol(C_AK), kb_), (pcol(C_AG), gb_), (acol(o_hbm), ob_),
            (acol(dyc_hbm, GMLP_W), yb_), (pcol(C_AV), vb_), (acol(l_hbm), lb_)))]
        for cp in loads:
            cp.start()

        @pl.when(pair == 0)
        def _():
            gqg_ref[...] = jnp.zeros_like(gqg_ref)
            gkg_ref[...] = jnp.zeros_like(gkg_ref)

        def pre_qk(t, carry):
            rows = pl.ds(pl.multiple_of(t * tn, tn), tn)
            zero = jnp.zeros((tn, LANES), F32)
            dkb_[rows, :] = zero
            dvb_[rows, :] = zero
            q = qb_[rows, :]
            qb_[rows, :] = q * lax.rsqrt(_headsum(q * q, bd) * (1.0 / HEAD_DIM) + EPS) * gqs
            k = kb_[rows, :]
            kb_[rows, :] = k * lax.rsqrt(_headsum(k * k, bd) * (1.0 / HEAD_DIM) + EPS) * gk
            return carry

        def pre_gate(t, carry):
            rows = pl.ds(pl.multiple_of(t * tn, tn), tn)
            g = gb_[rows, :]
            ov = ob_[rows, :]
            dya = yb_[rows, :]
            sg = _sigmoid(g)
            dgt_ref[rows, :] = (dya * ov * (sg * (1.0 + g * (1.0 - sg)))).astype(BF16)
            do = dya * (g * sg)
            yb_[rows, :] = do
            ob_[rows, :] = _headsum(do * ov, bd)
            gb_[rows, :] = jnp.zeros((tn, LANES), F32)
            return carry

        loads[0].wait()
        loads[1].wait()
        lax.fori_loop(0, SEQ // tn, pre_qk, 0)
        for cp in loads[2:5]:
            cp.wait()
        lax.fori_loop(0, SEQ // tn, pre_gate, 0)
        loads[5].wait()
        loads[6].wait()

        def load_kv(ref, d, start, prev):
            own = _rows_of(ref, start, d)[...]
            if prev is None:
                return own.astype(BF16)
            return jnp.concatenate([_rows_of(ref, prev, d)[...], own], axis=0).astype(BF16)

        def group(d, blocks):
            first = blocks[0][1] is None
            valid, lok = (valid1, lo) if first else (valid2, lo2)
            chains = [(b, h) for b in range(len(blocks)) for h in range(2)]
            mask = lambda h: lo if h == 0 else ~lo
            qs = [_rows_of(qb_, start, d)[...] for start, _ in blocks]
            dos = [_rows_of(yb_, start, d)[...] for start, _ in blocks]
            lvs = [_rows_of(lb_, start, d)[...] for start, _ in blocks]
            dls = [_rows_of(ob_, start, d)[...] for start, _ in blocks]
            ks = [load_kv(kb_, d, start, prev) for start, prev in blocks]
            vs = [load_kv(vb_, d, start, prev) for start, prev in blocks]
            qbs = [q.astype(BF16) for q in qs]
            dobs = [do.astype(BF16) for do in dos]
            ss = [_dot_nt(jnp.where(mask(h), qs[b], 0.0).astype(BF16), ks[b]) for b, h in chains]
            dps = [_dot_nt(jnp.where(mask(h), dos[b], 0.0).astype(BF16), vs[b]) for b, h in chains]
            pbs, dss = [], []
            for s, dp, (b, h) in zip(ss, dps, chains):
                hc = h * HEAD_DIM
                p = jnp.exp(jnp.where(valid, s, -jnp.inf) - lvs[b][:, hc:hc + 1])
                pbs.append(p.astype(BF16))
                dss.append((p * (dp - dls[b][:, hc:hc + 1])).astype(BF16))
            dqs = [_dot(ds, ks[b]) for ds, (b, h) in zip(dss, chains)]
            dks = [_dot_tn(ds, qbs[b]) for ds, (b, h) in zip(dss, chains)]
            dvs = [_dot_tn(p, dobs[b]) for p, (b, h) in zip(pbs, chains)]
            for b, (start, prev) in enumerate(blocks):
                c0, c1 = 2 * b, 2 * b + 1
                dq_rows = _rows_of(gb_, start, d)
                dq_rows[...] = dq_rows[...] + jnp.where(lo, dqs[c0], dqs[c1])
                dkc = jnp.where(lok, dks[c0], dks[c1])
                dvc = jnp.where(lok, dvs[c0], dvs[c1])
                spans = ((start, slice(0, CHUNK)),) if first else ((prev, slice(0, CHUNK)), (start, slice(CHUNK, 2 * CHUNK)))
                for st, sl in spans:
                    dk_rows = _rows_of(dkb_, st, d)
                    dk_rows[...] = dk_rows[...] + dkc[sl]
                    dv_rows = _rows_of(dvb_, st, d)
                    dv_rows[...] = dv_rows[...] + dvc[sl]

        for d in DILATIONS:
            _for_blocks(d, group, ATTN_UNROLL)

        pltpu.sync_copy(pcol(C_AQ), qb_)
        pltpu.sync_copy(pcol(C_AK), kb_)

        def post(t, carry):
            gq_acc, gk_acc = carry
            rows = pl.ds(pl.multiple_of(t * tn, tn), tn)
            outs = []
            for raw_, acc_, gain in ((qb_, gb_, gqs), (kb_, dkb_, gk)):
                a = raw_[rows, :]
                r = lax.rsqrt(_headsum(a * a, bd) * (1.0 / HEAD_DIM) + EPS)
                z = a * r
                dn = acc_[rows, :]
                dz = dn * gain
                outs.append((r * (dz - z * (_headsum(dz * z, bd) * (1.0 / HEAD_DIM))), jnp.sum(dn * z, axis=0, keepdims=True)))
            dq_ref[rows, :] = outs[0][0].astype(BF16)
            dk_ref[rows, :] = outs[1][0].astype(BF16)
            dv_ref[rows, :] = dvb_[rows, :].astype(BF16)
            return gq_acc + outs[0][1] * QK_SCALE, gk_acc + outs[1][1]

        zero = jnp.zeros((1, LANES), F32)
        gq_acc, gk_acc = lax.fori_loop(0, SEQ // tn, post, (zero, zero))
        gqg_ref[0:1, :] += gq_acc
        gkg_ref[0:1, :] += gk_acc

        @pl.when(pair == npairs - 1)
        def _():
            gqg_ref[0:1, :] = _fold_heads(gqg_ref[0:1, :])
            gkg_ref[0:1, :] = _fold_heads(gkg_ref[0:1, :])
            rs_stage[npairs]()

    hbm = pl.BlockSpec(memory_space=pl.ANY)
    vec = pl.BlockSpec((1, LANES), lambda p: (0, 0))
    blk8 = pl.BlockSpec((8, LANES), lambda p: (0, 0))
    out = pl.BlockSpec((SEQ, LANES), lambda p: (0, p))
    big = jax.ShapeDtypeStruct((SEQ, ATTN_W), BF16)
    nsem = RS_SEMS * nride
    return _call(
        body, name="attn_bwd", grid=(npairs,),
        in_specs=[hbm, hbm, hbm, hbm, vec, vec] + [hbm] * nride,
        out_specs=[out, out, out, out, blk8, blk8] + [hbm] * nride,
        out_shape=[big, big, big, big, jax.ShapeDtypeStruct((8, LANES), F32), jax.ShapeDtypeStruct((8, LANES), F32)]
        + [jax.ShapeDtypeStruct((2, g.shape[0] // 8, g.shape[1]), F32) for g in ride_along],
        scratch_shapes=[pltpu.VMEM((SEQ, LANES), F32) for _ in range(9)] + [pltpu.SemaphoreType.DMA((7,))]
        + _rs_scratch([g.shape for g in ride_along]) + [pltpu.SemaphoreType.DMA((nsem,)), pltpu.SemaphoreType.DMA((nsem,)),
                                     pltpu.SemaphoreType.DMA((nride,))],
        compiler_params=_params(),
    )(proj, o, lse, dyc, gq2, gk2, *[_rs_view(g) for g in ride_along])


def _mem_kv(mem, gain, wkv):
    def body(m_ref, g_ref, w_ref, kv_ref, hm_ref):
        mv = m_ref[...]
        ms = jnp.mean(mv * mv, axis=-1, keepdims=True)
        hm = (mv * lax.rsqrt(ms + EPS) * g_ref[...]).astype(BF16)
        hm_ref[...] = hm
        kv_ref[...] = _dot(hm, w_ref[...])

    return _call(
        body, name="mem_kv",
        out_shape=[jax.ShapeDtypeStruct((MEM_LEN, 2 * MEM_W), F32), jax.ShapeDtypeStruct((MEM_LEN, D_MODEL), BF16)],
        compiler_params=_params(),
    )(mem, gain, wkv)


def _mem_keys(kv_ref, kg_ref, bd, p):
    mk = kv_ref[:, p * LANES:(p + 1) * LANES]
    r = lax.rsqrt(_headsum(mk * mk, bd) * (1.0 / HEAD_DIM) + EPS)
    z = mk * r
    mkn = (z * kg_ref[:, p * LANES:(p + 1) * LANES]).astype(BF16)
    mvp = kv_ref[:, MEM_W + p * LANES:MEM_W + (p + 1) * LANES].astype(BF16)
    return mkn, mvp, r, z


def _mem_fwd(proj, kv, qg4, kg4):
    tm = 512

    def body(q_ref, g_ref, kv_ref, qg_ref, kg_ref, om_ref, ym_ref):
        bd = _head_blockdiag()
        lo = _lo_mask(tm)
        for p in range(2):
            cs = slice(p * LANES, (p + 1) * LANES)
            mkn, mvp, _, _ = _mem_keys(kv_ref, kg_ref, bd, p)
            q = q_ref[:, cs]
            qn = q * lax.rsqrt(_headsum(q * q, bd) * (1.0 / HEAD_DIM) + EPS) * (qg_ref[:, cs] * QK_SCALE)
            res = []
            for h in range(2):
                qh = jnp.where(lo if h == 0 else ~lo, qn, 0.0).astype(BF16)
                s = _dot_nt(qh, mkn)
                e = jnp.exp(s - jnp.max(s, axis=-1, keepdims=True))
                res.append(_dot(e.astype(BF16), mvp) * (1.0 / jnp.sum(e, axis=-1, keepdims=True)))
            ov = jnp.where(lo, res[0], res[1])
            g = g_ref[:, cs]
            om_ref[:, cs] = ov
            ym_ref[:, cs] = (ov * (g * _sigmoid(g))).astype(BF16)

    vec = pl.BlockSpec((1, MEM_W), lambda i: (0, 0))
    return _call(
        body, name="mem_fwd", grid=(SEQ // tm,),
        in_specs=[pl.BlockSpec((tm, MEM_W), lambda i: (i, C_MQ // MEM_W)),
                  pl.BlockSpec((tm, MEM_W), lambda i: (i, C_MG // MEM_W)),
                  pl.BlockSpec((MEM_LEN, 2 * MEM_W), lambda i: (0, 0)), vec, vec],
        out_specs=[pl.BlockSpec((tm, MEM_W), lambda i: (i, 0)), pl.BlockSpec((tm, MEM_W), lambda i: (i, 0))],
        out_shape=[jax.ShapeDtypeStruct((SEQ, MEM_W), F32), jax.ShapeDtypeStruct((SEQ, MEM_W), BF16)],
        compiler_params=_params(),
    )(proj, proj, kv, qg4, kg4)


def _mem_bwd(proj, om, dyc, kv, hm, mem, mgain, wkv, qg4, kg4):
    tm = 512
    nsteps = SEQ // tm

    def body(q_ref, g_ref, om_ref, dy_ref, kv_ref, hm_ref, mem_ref, mg_ref, w_ref, qg_ref, kg_ref,
             dq_ref, dgt_ref, gqg_ref, gkg_ref, gw_ref, gmg_ref, dmk_ref, dmv_ref, gq_acc):
        i = pl.program_id(0)
        bd = _head_blockdiag()
        lo = _lo_mask(tm)
        lom = _lo_mask(MEM_LEN)

        @pl.when(i == 0)
        def _():
            dmk_ref[...] = jnp.zeros_like(dmk_ref)
            dmv_ref[...] = jnp.zeros_like(dmv_ref)
            gq_acc[...] = jnp.zeros_like(gq_acc)

        for p in range(2):
            cs = slice(p * LANES, (p + 1) * LANES)
            mkn, mvp, _, _ = _mem_keys(kv_ref, kg_ref, bd, p)
            gqs = qg_ref[:, cs] * QK_SCALE
            q = q_ref[:, cs]
            r = lax.rsqrt(_headsum(q * q, bd) * (1.0 / HEAD_DIM) + EPS)
            z = q * r
            qn = z * gqs
            qnb = qn.astype(BF16)
            g = g_ref[:, cs]
            ov = om_ref[:, cs]
            dym = dy_ref[:, cs]
            sg = _sigmoid(g)
            dgt_ref[:, cs] = (dym * ov * (sg * (1.0 + g * (1.0 - sg)))).astype(BF16)
            do = dym * (g * sg)
            dob = do.astype(BF16)
            delta = _headsum(do * ov, bd)
            parts = []
            for h in range(2):
                mh = lo if h == 0 else ~lo
                hc = h * HEAD_DIM
                qh = jnp.where(mh, qn, 0.0).astype(BF16)
                doh = jnp.where(mh, do, 0.0).astype(BF16)
                s = _dot_nt(qh, mkn)
                e = jnp.exp(s - jnp.max(s, axis=-1, keepdims=True))
                pr = e * (1.0 / jnp.sum(e, axis=-1, keepdims=True))
                dp = _dot_nt(doh, mvp)
                ds = (pr * (dp - delta[:, hc:hc + 1])).astype(BF16)
                parts.append((_dot(ds, mkn), _dot_tn(ds, qnb), _dot_tn(pr.astype(BF16), dob)))
            dqn = jnp.where(lo, parts[0][0], parts[1][0])
            dmk_ref[:, cs] += jnp.where(lom, parts[0][1], parts[1][1])
            dmv_ref[:, cs] += jnp.where(lom, parts[0][2], parts[1][2])
            dz = dqn * gqs
            dq_ref[:, cs] = (r * (dz - z * (_headsum(dz * z, bd) * (1.0 / HEAD_DIM)))).astype(BF16)
            gq_acc[:, cs] += jnp.sum(dqn * z, axis=0, keepdims=True) * QK_SCALE

        @pl.when(i == nsteps - 1)
        def _():
            gqg_ref[...] = jnp.zeros_like(gqg_ref)
            gkg_ref[...] = jnp.zeros_like(gkg_ref)
            gqg_ref[0:1, :] = _fold_heads(gq_acc[:, 0:LANES] + gq_acc[:, LANES:2 * LANES])
            dkv = []
            gk = jnp.zeros((1, LANES), F32)
            for p in range(2):
                cs = slice(p * LANES, (p + 1) * LANES)
                _, _, r, z = _mem_keys(kv_ref, kg_ref, bd, p)
                dn = dmk_ref[:, cs]
                dz = dn * kg_ref[:, cs]
                gk = gk + jnp.sum(dn * z, axis=0, keepdims=True)
                dkv.append(r * (dz - z * (_headsum(dz * z, bd) * (1.0 / HEAD_DIM))))
            gkg_ref[0:1, :] = _fold_heads(gk)
            dkvb = jnp.concatenate(dkv + [dmv_ref[...]], axis=1).astype(BF16)
            gw_ref[...] = _dot_tn(hm_ref[...], dkvb)
            dhm = _dot_nt(dkvb, w_ref[...])
            mv = mem_ref[...]
            zm = mv * lax.rsqrt(jnp.mean(mv * mv, axis=-1, keepdims=True) + EPS)
            _put_rows(gmg_ref, jnp.sum(dhm * zm, axis=0, keepdims=True))

    const = lambda shape: pl.BlockSpec(shape, lambda i: (0,) * len(shape))
    row = lambda j: pl.BlockSpec((tm, MEM_W), lambda i: (i, j))
    blk8 = jax.ShapeDtypeStruct((8, LANES), F32)
    return _call(
        body, name="mem_bwd", grid=(nsteps,),
        in_specs=[row(C_MQ // MEM_W), row(C_MG // MEM_W), row(0), row((GMLP_W + ATTN_W) // MEM_W),
                  const((MEM_LEN, 2 * MEM_W)), const((MEM_LEN, D_MODEL)), const((MEM_LEN, D_MODEL)),
                  const((1, D_MODEL)), const((D_MODEL, 2 * MEM_W)), const((1, MEM_W)), const((1, MEM_W))],
        out_specs=[row(0), row(0), const((8, LANES)), const((8, LANES)),
                   const((D_MODEL, 2 * MEM_W)), const((8, LANES))],
        out_shape=[jax.ShapeDtypeStruct((SEQ, MEM_W), BF16), jax.ShapeDtypeStruct((SEQ, MEM_W), BF16),
                   blk8, blk8, jax.ShapeDtypeStruct((D_MODEL, 2 * MEM_W), F32), blk8],
        scratch_shapes=[pltpu.VMEM((MEM_LEN, MEM_W), F32), pltpu.VMEM((MEM_LEN, MEM_W), F32),
                        pltpu.VMEM((1, MEM_W), F32)],
        compiler_params=_params(),
    )(proj, proj, om, dyc, kv, hm, mem, mgain, wkv, qg4, kg4)


def _out_loss(yg, ya, ym, x, tgt, wo):
    tm = 512
    nsteps = SEQ // tm
    parts = ((0, GMLP_W), (GMLP_W, ATTN_W), (GMLP_W + ATTN_W, MEM_W))

    def body(yg_ref, ya_ref, ym_ref, x_ref, t_ref, w_ref, dy_ref, dyc_ref, gw_ref, ls_ref):
        i = pl.program_id(0)

        @pl.when(i == 0)
        def _():
            gw_ref[...] = jnp.zeros_like(gw_ref)
            ls_ref[...] = jnp.zeros_like(ls_ref)

        ys = (yg_ref[...], ya_ref[...], ym_ref[...])
        y = sum(_dot(yv, w_ref[r0:r0 + n, :]) for yv, (r0, n) in zip(ys, parts))
        err = x_ref[...] + y - t_ref[...]
        _put_rows(ls_ref, jnp.sum(err * err, axis=0, keepdims=True), accumulate=True)
        dy = err * (1.0 / D_MODEL)
        dy_ref[...] = dy
        dyb = dy.astype(BF16)
        dyc_ref[...] = _dot_nt(dyb, w_ref[...])
        for yv, (r0, n) in zip(ys, parts):
            gw_ref[r0:r0 + n, :] += _dot_tn(yv, dyb)

    row = lambda w: pl.BlockSpec((tm, w), lambda i: (i, 0))
    const = lambda shape: pl.BlockSpec(shape, lambda i: (0, 0))
    return _call(
        body, name="out_loss", grid=(nsteps,),
        in_specs=[row(GMLP_W), row(ATTN_W), row(MEM_W), row(D_MODEL), row(D_MODEL), const((D_MODEL, D_MODEL))],
        out_specs=[row(D_MODEL), row(D_MODEL), const((D_MODEL, D_MODEL)), const((8, LANES))],
        out_shape=[jax.ShapeDtypeStruct((SEQ, D_MODEL), F32), jax.ShapeDtypeStruct((SEQ, D_MODEL), F32),
                   jax.ShapeDtypeStruct((D_MODEL, D_MODEL), F32), jax.ShapeDtypeStruct((8, LANES), F32)],
        compiler_params=_params(),
    )(yg, ya, ym, x, tgt, wo)


def _proj_bwd(x, dy, gain, wt, dg, daq, dak, dav, dag, dmq, dmg):
    tm = 512
    nsteps = SEQ // tm
    pieces = ((C_GU, 3 * GMLP_W), (C_AQ, ATTN_W), (C_AK, ATTN_W), (C_AV, ATTN_W), (C_AG, ATTN_W),
              (C_MQ, MEM_W), (C_MG, MEM_W))

    def body(x_ref, dy_ref, g_ref, wt_hbm, p0, p1, p2, p3, p4, p5, p6, gx_ref, gwt_hbm, gg_ref, wt_v, acc, wt_sem):
        i = pl.program_id(0)
        wt_load = pltpu.make_async_copy(wt_hbm, wt_v, wt_sem)

        @pl.when(i == 0)
        def _():
            wt_load.start()
            acc[...] = jnp.zeros_like(acc)
            gg_ref[...] = jnp.zeros_like(gg_ref)

        xv = x_ref[...]
        r = lax.rsqrt(jnp.mean(xv * xv, axis=-1, keepdims=True) + EPS)
        z = xv * r
        g = g_ref[...]
        h = (z * g).astype(BF16)
        pl.when(i == 0)(wt_load.wait)
        dh = jnp.zeros((tm, D_MODEL), F32)
        for pref, (c0, w) in zip((p0, p1, p2, p3, p4, p5, p6), pieces):
            dp = pref[...]
            dh = dh + _dot(dp, wt_v[c0:c0 + w, :])
            acc[c0:c0 + w, :] += _dot_tn(dp, h)
        _put_rows(gg_ref, jnp.sum(dh * z, axis=0, keepdims=True), accumulate=True)
        dz = dh * g
        gx_ref[...] = dy_ref[...] + r * (dz - z * jnp.mean(dz * z, axis=-1, keepdims=True))

        @pl.when(i == nsteps - 1)
        def _():
            pltpu.sync_copy(acc, gwt_hbm)

    row = lambda w: pl.BlockSpec((tm, w), lambda i: (i, 0))
    hbm = pl.BlockSpec(memory_space=pl.ANY)
    vec = pl.BlockSpec((1, D_MODEL), lambda i: (0, 0))
    return _call(
        body, name="proj_bwd", grid=(nsteps,),
        in_specs=[row(D_MODEL), row(D_MODEL), vec, hbm] + [row(w) for _, w in pieces],
        out_specs=[row(D_MODEL), hbm, pl.BlockSpec((8, LANES), lambda i: (0, 0))],
        out_shape=[jax.ShapeDtypeStruct((SEQ, D_MODEL), F32), jax.ShapeDtypeStruct((IN_W, D_MODEL), F32),
                   jax.ShapeDtypeStruct((8, LANES), F32)],
        scratch_shapes=[pltpu.VMEM((IN_W, D_MODEL), BF16), pltpu.VMEM((IN_W, D_MODEL), F32), pltpu.SemaphoreType.DMA],
        compiler_params=_params(),
    )(x, dy, gain, wt, dg, daq, dak, dav, dag, dmq, dmg)


def _gather_weights(wt_sh, wkv_sh, wo_sh):
    shards = (wt_sh, wkv_sh, wo_sh)
    nrows = tuple(a.shape[0] for a in shards)

    def body(a0, a1, a2, o0, o1, o2, send_sems, recv_sems):
        x, y, c = lax.axis_index("x"), lax.axis_index("y"), lax.axis_index("c")
        sib, xn, yn = (x, y, 1 - c), (1 - x, y, c), (x, 1 - y, c)
        me, cx, cy, cd = 2 * x + y, 2 * (1 - x) + y, 2 * x + (1 - y), 2 * (1 - x) + (1 - y)
        ins, outs = (a0, a1, a2), (o0, o1, o2)

        def part(a, chip, hf, quarter=None):
            n = nrows[a] // 2
            base = chip * nrows[a] + hf * n
            if quarter is not None:
                n = n // 2
                base = base + quarter * n
            return outs[a].at[pl.ds(pl.multiple_of(base, 16), n), :]

        def copy(k, ref, to):
            return pltpu.make_async_remote_copy(src_ref=ref, dst_ref=ref, send_sem=send_sems.at[k],
                                                recv_sem=recv_sems.at[k], device_id=to, device_id_type=MESH)

        started = []

        def go(cp):
            cp.start()
            started.append(cp)

        for a in range(3):
            outs[a][pl.ds(pl.multiple_of(me * nrows[a], 16), nrows[a]), :] = ins[a][...].astype(BF16)
        for a in range(3):
            go(copy(8 * a, part(a, me, c), xn))
            go(copy(8 * a + 1, part(a, me, c), yn))
        for a in range(3):
            k = 8 * a
            copy(k, part(a, cx, c), xn).wait_recv()
            go(copy(k + 4, part(a, cx, c, 1), yn))
            go(copy(k + 2, part(a, cx, c), sib))
            copy(k + 1, part(a, cy, c), yn).wait_recv()
            go(copy(k + 5, part(a, cy, c, 0), xn))
            go(copy(k + 3, part(a, cy, c), sib))
        for a in range(3):
            k = 8 * a
            copy(k + 4, part(a, cd, c, 1), yn).wait_recv()
            go(copy(k + 7, part(a, cd, c, 1), sib))
            copy(k + 5, part(a, cd, c, 0), xn).wait_recv()
            go(copy(k + 6, part(a, cd, c, 0), sib))
        for a in range(3):
            k = 8 * a
            copy(k + 2, part(a, cx, 1 - c), sib).wait_recv()
            copy(k + 3, part(a, cy, 1 - c), sib).wait_recv()
            copy(k + 6, part(a, cd, 1 - c, 0), sib).wait_recv()
            copy(k + 7, part(a, cd, 1 - c, 1), sib).wait_recv()
        for cp in started:
            cp.wait_send()

    return _call(
        body, name="gather_weights",
        out_shape=[jax.ShapeDtypeStruct((4 * a.shape[0], a.shape[1]), BF16) for a in shards],
        in_specs=[pl.BlockSpec(memory_space=pltpu.VMEM)] * 3,
        out_specs=[pl.BlockSpec(memory_space=pltpu.VMEM)] * 3,
        scratch_shapes=[pltpu.SemaphoreType.DMA((24,)), pltpu.SemaphoreType.DMA((24,))],
        compiler_params=_params(),
    )(*shards)


RS_SEMS = 6
RS_KINDS = (((2, 2), 1, F32), ((2, 2), 1, F32), ((2, 2), 2, BF16), ((2, 2), 2, BF16), ((2, 2), 2, F32),
            ((2,), 2, BF16), ((2,), 2, BF16), ((2,), 1, F32))


def _rs_view(g):
    return g.reshape(2, 2, 2, g.shape[0] // 8, g.shape[1])


def _rs_scratch(shapes, in_vmem=False):
    kinds = RS_KINDS[1:] if in_vmem else RS_KINDS
    return [pltpu.VMEM(lead + (r // 8, w // split), dt) for lead, split, dt in kinds for r, w in shapes]


def _rs_stages(gs, outs, bufs, send_sems, recv_sems, local_sems, widths):
    n = len(gs)
    if len(bufs) < n * len(RS_KINDS):
        bufs = [None] * n + list(bufs)
    loc, ra, s_b, r_b, acc1, s_c, r_c, fin = (bufs[n * i:n * i + n] for i in range(len(RS_KINDS)))
    half_w = [w // 2 for w in widths]
    chips = [(xx, yy) for xx in range(2) for yy in range(2)]
    x, y, c = lax.axis_index("x"), lax.axis_index("y"), lax.axis_index("c")
    sib, xn, yn = (x, y, 1 - c), (1 - x, y, c), (x, 1 - y, c)

    def copy(a, j, src, dst, to):
        k = RS_SEMS * a + j
        return pltpu.make_async_remote_copy(src_ref=src, dst_ref=dst, send_sem=send_sems.at[k],
                                            recv_sem=recv_sems.at[k], device_id=to, device_id_type=MESH)

    def step_a(a):
        if callable(gs[a]):
            return [copy(a, 0, gs[a](xx, yy, 1 - c), ra[a].at[xx, yy], sib) for xx, yy in chips]
        return [copy(a, 0, gs[a].at[:, :, 1 - c], ra[a], sib),
                pltpu.make_async_copy(gs[a].at[:, :, c], loc[a], local_sems.at[a])]

    def finish_a(a):
        if callable(gs[a]):
            copy(a, 0, ra[a], ra[a], sib).wait()
            for xx, yy in chips:
                ra[a][xx, yy] = gs[a](xx, yy, c)[...] + ra[a][xx, yy]
        else:
            for cp in step_a(a):
                cp.wait()
            ra[a][...] = loc[a][...] + ra[a][...]

    def step_b(a):
        return copy(a, 1, s_b[a].at[0], r_b[a].at[0], xn), copy(a, 2, s_b[a].at[1], r_b[a].at[1], yn)

    def step_c(a):
        return copy(a, 3, s_c[a].at[0], r_c[a].at[0], yn), copy(a, 4, s_c[a].at[1], r_c[a].at[1], xn)

    def step_d(a, half):
        rows = fin[a].at[half]
        return copy(a, 5, rows, rows, sib)

    def start():
        for a in range(n):
            for cp in step_a(a):
                cp.start()

    def a_to_b():
        for a in range(n):
            finish_a(a)
            s_b[a][0] = ra[a][1 - x, :, :, :half_w[a]].astype(BF16)
            s_b[a][1] = ra[a][:, 1 - y, :, half_w[a]:].astype(BF16)
            for cp in step_b(a):
                cp.start()

    def b_to_c():
        for a in range(n):
            for cp in step_b(a):
                cp.wait()
            acc1[a][0] = ra[a][x, :, :, :half_w[a]] + r_b[a][0].astype(F32)
            acc1[a][1] = ra[a][:, y, :, half_w[a]:] + r_b[a][1].astype(F32)
            s_c[a][0] = acc1[a][0, 1 - y].astype(BF16)
            s_c[a][1] = acc1[a][1, 1 - x].astype(BF16)
            for cp in step_c(a):
                cp.start()

    def c_to_d():
        for a in range(n):
            for cp in step_c(a):
                cp.wait()
            fin[a][c, :, :half_w[a]] = acc1[a][0, y] + r_c[a][0].astype(F32)
            fin[a][c, :, half_w[a]:] = acc1[a][1, x] + r_c[a][1].astype(F32)
            step_d(a, c).start()

    def finish():
        for a in range(n):
            step_d(a, 1 - c).wait_recv()
            step_d(a, c).wait_send()
            pltpu.sync_copy(fin[a], outs[a])

    return start, a_to_b, b_to_c, c_to_d, finish


def _reduce_grads(gwt, g_ws, tiny):
    cw = gwt.shape[1] // RS_CHUNKS
    chunk_shape = (gwt.shape[0], cw)

    def body(g0, ws_in, tiny_in, *rest):
        outs, o_ws, o_tiny = rest[:RS_CHUNKS], rest[RS_CHUNKS], rest[RS_CHUNKS + 1]
        rest = rest[RS_CHUNKS + 2:]
        nb = len(RS_KINDS) * RS_CHUNKS
        sm, sa, sb, sc, acc_s, send_sems, recv_sems, local_sems = rest[nb:]
        blocks = [g0.at[:, :, :, :, pl.ds(j * cw, cw)] for j in range(RS_CHUNKS)]
        start, a_to_b, b_to_c, c_to_d, finish = _rs_stages(blocks, outs, rest[:nb], send_sems, recv_sems, local_sems,
                                                           [cw] * RS_CHUNKS)
        n_ws = ws_in.shape[0]
        sm[0:n_ws, :] = ws_in[...]
        sm[n_ws:, :] = tiny_in[...]
        x, y, c = lax.axis_index("x"), lax.axis_index("y"), lax.axis_index("c")

        def small(j, src, dst, to):
            k = RS_SEMS * RS_CHUNKS + j
            return pltpu.make_async_remote_copy(src_ref=src, dst_ref=dst, send_sem=send_sems.at[k],
                                                recv_sem=recv_sems.at[k], device_id=to, device_id_type=MESH)

        along_c, along_x, along_y = (small(0, sm, sa, (x, y, 1 - c)), small(1, acc_s, sb, (1 - x, y, c)),
                                     small(2, sb, sc, (x, 1 - y, c)))
        start()
        along_c.start()
        a_to_b()
        along_c.wait()
        acc_s[...] = sm[...] + sa[...]
        along_x.start()
        b_to_c()
        along_x.wait()
        sb[...] = acc_s[...] + sb[...]
        along_y.start()
        c_to_d()
        along_y.wait()
        o_ws[...] = sb[0:n_ws, :] + sc[0:n_ws, :]
        o_tiny[...] = sb[n_ws:, :] + sc[n_ws:, :]
        finish()

    vm = pl.BlockSpec(memory_space=pltpu.VMEM)
    hbm = pl.BlockSpec(memory_space=pl.ANY)
    small_shape = (g_ws.shape[0] + tiny.shape[0], LANES)
    scratch = _rs_scratch([chunk_shape] * RS_CHUNKS) + [pltpu.VMEM(small_shape, F32) for _ in range(5)]
    nsem = RS_SEMS * RS_CHUNKS + 3
    scratch += [pltpu.SemaphoreType.DMA((nsem,)), pltpu.SemaphoreType.DMA((nsem,)), pltpu.SemaphoreType.DMA((RS_CHUNKS,))]
    return _call(
        body, name="reduce_grads",
        out_shape=[jax.ShapeDtypeStruct((2, gwt.shape[0] // 8, cw), F32)] * RS_CHUNKS
        + [jax.ShapeDtypeStruct(g_ws.shape, F32), jax.ShapeDtypeStruct(tiny.shape, F32)],
        in_specs=[hbm, vm, vm],
        out_specs=[hbm] * RS_CHUNKS + [vm, vm],
        scratch_shapes=scratch,
        compiler_params=_params(),
    )(_rs_view(gwt), g_ws, tiny)


def _adam_update(w, g, m, v):
    nm = ADAM_B1 * m + (1.0 - ADAM_B1) * g
    nv = ADAM_B2 * v + (1.0 - ADAM_B2) * (g * g)
    m_hat = nm / (1.0 - ADAM_B1 ** ADAM_STEP)
    v_hat = nv / (1.0 - ADAM_B2 ** ADAM_STEP)
    return -ADAM_LR * (m_hat / (jnp.sqrt(v_hat) + ADAM_EPS) + ADAM_WD * w), nm, nv


def _adamw(w, g, m, v):
    rows, cols = w.shape
    tm = max(t for t in range(8, 257, 8) if rows % t == 0)
    parts = tuple(g) if isinstance(g, (tuple, list)) else (g,)
    n = len(parts)

    def body(w_ref, m_ref, v_ref, *refs):
        gv = jnp.concatenate([r[...] for r in refs[:n]], axis=1)
        d_ref, nm_ref, nv_ref = refs[n:n + 3]
        d_ref[...], nm_ref[...], nv_ref[...] = _adam_update(w_ref[...], gv, m_ref[...], v_ref[...])
        if n > 1:
            refs[n + 3][...] = gv

    blk = pl.BlockSpec((tm, cols), lambda i: (i, 0))
    nout = 3 if n == 1 else 4
    res = _call(
        body, name="adamw", grid=(rows // tm,),
        in_specs=[blk] * 3 + [pl.BlockSpec((tm, p.shape[1]), lambda i: (i, 0)) for p in parts], out_specs=[blk] * nout,
        out_shape=[jax.ShapeDtypeStruct((rows, cols), F32)] * nout,
        compiler_params=_params(),
    )(w, m, v, *parts)
    return (parts[0] if n == 1 else res[3], *res[:3])


def _adamw_tiny(tiny, weights, ms, vs):
    shapes = [w.shape for w in weights]
    n = len(weights)

    def grad_of(t_ref, k, shape):
        base = 8 * k
        if shape[1] > LANES:
            return [t_ref[base + j:base + j + 1, :] for j in range(shape[1] // LANES)]
        return [t_ref[base:base + shape[0], 0:shape[1]]]

    def body(t_ref, *refs):
        w_refs, m_refs, v_refs = refs[:n], refs[n:2 * n], refs[2 * n:3 * n]
        loss_ref, outs = refs[3 * n], refs[3 * n + 1:]
        loss_ref[...] = (0.5 / D_MODEL) * jnp.sum(t_ref[8 * n:8 * n + 8, :], keepdims=True)
        for k, shape in enumerate(shapes):
            g_ref, d_ref, nm_ref, nv_ref = outs[4 * k:4 * k + 4]
            for j, g in enumerate(grad_of(t_ref, k, shape)):
                cols = slice(j * LANES, (j + 1) * LANES) if shape[1] > LANES else slice(None)
                g_ref[:, cols] = g
                d_ref[:, cols], nm_ref[:, cols], nv_ref[:, cols] = _adam_update(
                    w_refs[k][:, cols], g, m_refs[k][:, cols], v_refs[k][:, cols])

    out_shape = [jax.ShapeDtypeStruct((1, 1), F32)]
    for shape in shapes:
        out_shape += [jax.ShapeDtypeStruct(shape, F32)] * 4
    return _call(body, name="adamw_tiny", out_shape=out_shape, compiler_params=_params())(tiny, *weights, *ms, *vs)


def _local_grads(x, mem, tgt, norm_gain, wt, gmlp_v_gain, gmlp_w_s, gmlp_b, attn_q_gain, attn_k_gain,
                 mem_norm_gain, wkv, mem_q_gain, mem_k_gain, wo):
    vg = gmlp_v_gain.reshape(1, GMLP_W)
    bias_full = jnp.repeat(gmlp_b.T, HEAD_DIM, axis=1)
    gq2, gk2 = jnp.tile(attn_q_gain, (1, 2)), jnp.tile(attn_k_gain, (1, 2))
    qg4, kg4 = jnp.tile(mem_q_gain, (1, 4)), jnp.tile(mem_k_gain, (1, 4))

    proj = _fwd_proj(x, norm_gain, wt)
    yg = _gmlp_fwd(proj, vg, gmlp_w_s, bias_full)
    o, lse, ya = _attn_fwd(proj, gq2, gk2)
    kv, hm = _mem_kv(mem, mem_norm_gain, wkv)
    om, ym = _mem_fwd(proj, kv, qg4, kg4)
    dy, dyc, g_wo, err2 = _out_loss(yg, ya, ym, x, tgt, wo)
    dmq, dmg, g_mq, g_mk, g_wkv, g_mng = _mem_bwd(proj, om, dyc, kv, hm, mem, mem_norm_gain, wkv, qg4, kg4)
    daq, dak, dav, dag, g_aq, g_ak, g_wkv_sh, g_wo_sh = _attn_bwd(proj, o, lse, dyc, gq2, gk2, g_wkv, g_wo)
    dg, g_ws, g_b, g_vg = _gmlp_bwd(proj, dyc, vg, gmlp_w_s, bias_full)
    gx, g_wt, g_ng = _proj_bwd(x, dy, norm_gain, wt, dg, daq, dak, dav, dag, dmq, dmg)

    tiny = jnp.concatenate([g_ng, g_vg, g_b, g_aq, g_ak, g_mng, g_mq, g_mk, err2], axis=0)
    return gx, g_wt, g_wkv_sh, g_wo_sh, g_ws.reshape(4 * CHUNK, CHUNK), tiny


def kernel(x, mem, norm_gain, w_in, gmlp_v_gain, gmlp_w_s, gmlp_b, attn_q_gain, attn_k_gain, mem_norm_gain, w_mem_kv, mem_q_gain, mem_k_gain, w_out, loss_target, m_norm_gain, m_w_in, m_gmlp_v_gain, m_gmlp_w_s, m_gmlp_b, m_attn_q_gain, m_attn_k_gain, m_mem_norm_gain, m_w_mem_kv, m_mem_q_gain, m_mem_k_gain, m_w_out, v_norm_gain, v_w_in, v_gmlp_v_gain, v_gmlp_w_s, v_gmlp_b, v_attn_q_gain, v_attn_k_gain, v_mem_norm_gain, v_w_mem_kv, v_mem_q_gain, v_mem_k_gain, v_w_out):
    wt, wkv, wo = _gather_weights(w_in[0].T, w_mem_kv[0], w_out[0])
    gx, g_wt, g_wkv_sh, g_wo_sh, g_ws, tiny = _local_grads(
        x[0], mem[0], loss_target[0], norm_gain, wt, gmlp_v_gain[0], gmlp_w_s[0], gmlp_b[0],
        attn_q_gain, attn_k_gain, mem_norm_gain, wkv, mem_q_gain, mem_k_gain, wo)
    *g_wt_sh, g_ws, tiny = _reduce_grads(g_wt, g_ws, tiny)
    chip_block = lambda g: g.reshape(2 * g.shape[1], g.shape[2])
    g_wt_sh = tuple(chip_block(g) for g in g_wt_sh)
    g_wkv_sh, g_wo_sh = chip_block(g_wkv_sh), chip_block(g_wo_sh)

    ws = (norm_gain, w_in, gmlp_v_gain, gmlp_w_s, gmlp_b, attn_q_gain, attn_k_gain, mem_norm_gain, w_mem_kv,
          mem_q_gain, mem_k_gain, w_out)
    ms = (m_norm_gain, m_w_in, m_gmlp_v_gain, m_gmlp_w_s, m_gmlp_b, m_attn_q_gain, m_attn_k_gain, m_mem_norm_gain,
          m_w_mem_kv, m_mem_q_gain, m_mem_k_gain, m_w_out)
    vs = (v_norm_gain, v_w_in, v_gmlp_v_gain, v_gmlp_w_s, v_gmlp_b, v_attn_q_gain, v_attn_k_gain, v_mem_norm_gain,
          v_w_mem_kv, v_mem_q_gain, v_mem_k_gain, v_w_out)
    form = {1: lambda a: a[0].T, 3: lambda a: a.reshape(4 * CHUNK, CHUNK), 2: lambda a: a[0], 4: lambda a: a[0],
            8: lambda a: a[0], 11: lambda a: a[0]}
    back = {1: lambda a: a.T[None], 3: lambda a: a.reshape(1, 4, CHUNK, CHUNK), 2: lambda a: a[None],
            4: lambda a: a[None], 8: lambda a: a[None], 11: lambda a: a[None]}
    fwd = lambda t, i: form.get(i, lambda a: a)(t[i])
    out = {}
    for i, g in ((1, g_wt_sh), (3, g_ws), (8, g_wkv_sh), (11, g_wo_sh)):
        out[i] = _adamw(fwd(ws, i), g, fwd(ms, i), fwd(vs, i))
    res = _adamw_tiny(tiny, [fwd(ws, i) for i in TINY_ORDER], [fwd(ms, i) for i in TINY_ORDER],
                      [fwd(vs, i) for i in TINY_ORDER])
    for k, i in enumerate(TINY_ORDER):
        out[i] = res[1 + 4 * k:5 + 4 * k]
    leaves = [[back.get(i, lambda a: a)(out[i][j]) for i in range(12)] for j in range(4)]
    return (res[0].reshape(()), gx[None], *leaves[0], *leaves[1], *leaves[2], *leaves[3])
```

```python
import functools
import math

import jax
import jax.numpy as jnp
from jax import lax
from jax.experimental import pallas as pl
from jax.experimental.pallas import tpu as pltpu

F32 = jnp.float32
BF16 = jnp.bfloat16

SEQ = 4096
D_MODEL = 1024
HEAD_DIM = 64
LANES = 128
CHUNK = 128
GMLP_W, ATTN_W, MEM_W = 256, 512, 256
IN_W = 3 * GMLP_W + 4 * ATTN_W + 2 * MEM_W
MEM_LEN = 256
DILATIONS = (1, 4, 16)
EPS = 1e-6
QK_SCALE = 1.0 / math.sqrt(HEAD_DIM)
C_GU, C_GV, C_GG, C_AQ, C_AK, C_AV, C_AG, C_MQ, C_MG = 0, 256, 512, 768, 1280, 1792, 2304, 2816, 3072

ADAM_LR, ADAM_B1, ADAM_B2, ADAM_EPS, ADAM_WD, ADAM_STEP = 0.001, 0.9, 0.999, 1e-08, 0.01, 10

VMEM_LIMIT = 48 * 1024 * 1024
RS_CHUNKS = 4
ATTN_UNROLL = 4
MESH = pl.DeviceIdType.MESH

TINY_ORDER = (0, 2, 4, 5, 6, 7, 9, 10)


def _call(body, **kw):
    return pl.pallas_call(body, **kw)


def _params(**kw):
    return pltpu.CompilerParams(vmem_limit_bytes=VMEM_LIMIT, **kw)


def _dot(a, b):
    return jnp.dot(a, b, preferred_element_type=F32)


def _dot_nt(a, b):
    return lax.dot_general(a, b, (((1,), (1,)), ((), ())), preferred_element_type=F32)


def _dot_tn(a, b):
    return lax.dot_general(a, b, (((0,), (0,)), ((), ())), preferred_element_type=F32)


def _head_blockdiag():
    r = lax.shift_right_logical(lax.broadcasted_iota(jnp.int32, (LANES, LANES), 0), 6)
    c = lax.shift_right_logical(lax.broadcasted_iota(jnp.int32, (LANES, LANES), 1), 6)
    return jnp.where(r == c, 1.0, 0.0).astype(BF16)


def _headsum(v, bd):
    hi = v.astype(BF16)
    lo = (v - hi.astype(F32)).astype(BF16)
    return _dot(hi, bd) + _dot(lo, bd)


def _lo_mask(rows):
    return lax.broadcasted_iota(jnp.int32, (rows, LANES), 1) < HEAD_DIM


def _sigmoid(x):
    return 1.0 / (1.0 + jnp.exp(-x))


def _fold_heads(v):
    return v + pltpu.roll(v, HEAD_DIM, 1)


def _put_rows(ref, vec, accumulate=False):
    for j in range(vec.shape[1] // LANES):
        piece = vec[:, j * LANES:(j + 1) * LANES]
        ref[j:j + 1, :] = ref[j:j + 1, :] + piece if accumulate else piece


def _fwd_proj(x, gain, wt, *ride_along):
    tm = 512
    nsteps = SEQ // tm
    nride = len(ride_along)
    stage_at = (0, 3, 5)

    def body(x_ref, g_ref, wt_ref, *rest):
        shards, o_ref, gathered = rest[:nride], rest[nride], rest[nride + 1:2 * nride + 1]
        lands, (send_sems, recv_sems) = rest[2 * nride + 1:3 * nride + 1], rest[3 * nride + 1:]
        i = pl.program_id(0)
        stages = _gather_stages(shards, lands, send_sems, recv_sems)
        for k, at in enumerate(stage_at):
            pl.when(i == at)(stages[k])
        xv = x_ref[...]
        ms = jnp.mean(xv * xv, axis=-1, keepdims=True)
        h = (xv * lax.rsqrt(ms + EPS) * g_ref[...]).astype(BF16)
        o_ref[...] = _dot_nt(h, wt_ref[...])

        @pl.when(i == nsteps - 1)
        def _():
            stages[len(stage_at)]()
            for land, out in zip(lands, gathered):
                pltpu.sync_copy(land, out)

    full = [jax.ShapeDtypeStruct((4 * a.shape[0], a.shape[1]), BF16) for a in ride_along]
    hbm = pl.BlockSpec(memory_space=pl.ANY)
    return _call(
        body, name="fwd_proj", grid=(nsteps,),
        in_specs=[pl.BlockSpec((tm, D_MODEL), lambda i: (i, 0)),
                  pl.BlockSpec((1, D_MODEL), lambda i: (0, 0)),
                  pl.BlockSpec((IN_W, D_MODEL), lambda i: (0, 0))]
        + [pl.BlockSpec(a.shape, lambda i: (0, 0)) for a in ride_along],
        out_specs=[pl.BlockSpec((tm, IN_W), lambda i: (i, 0))] + [hbm] * nride,
        out_shape=[jax.ShapeDtypeStruct((SEQ, IN_W), F32)] + full,
        scratch_shapes=[pltpu.VMEM(s.shape, BF16) for s in full]
        + [pltpu.SemaphoreType.DMA((AG_SEMS * nride,)), pltpu.SemaphoreType.DMA((AG_SEMS * nride,))],
        compiler_params=_params(),
    )(x, gain, wt, *ride_along)


def _gmlp_weights(w_ref):
    ti = lax.broadcasted_iota(jnp.int32, (CHUNK, CHUNK), 0)
    si = lax.broadcasted_iota(jnp.int32, (CHUNK, CHUNK), 1)
    tril = si <= ti
    return tril, [jnp.where(tril, w_ref[h], 0.0).astype(BF16) for h in range(4)]


def _gmlp_fwd(proj, vgain, w_s, bias_full):
    tm = 512

    def body(p_ref, vg_ref, w_ref, b_ref, y_ref):
        bd = _head_blockdiag()
        lo = _lo_mask(CHUNK)
        _, wm = _gmlp_weights(w_ref)
        for c in range(tm // CHUNK):
            rows = pl.ds(c * CHUNK, CHUNK)
            for p in range(2):
                cs = slice(p * LANES, (p + 1) * LANES)
                u = p_ref[rows, C_GU + p * LANES:C_GU + (p + 1) * LANES]
                v = p_ref[rows, C_GV + p * LANES:C_GV + (p + 1) * LANES]
                gt = p_ref[rows, C_GG + p * LANES:C_GG + (p + 1) * LANES]
                r = lax.rsqrt(_headsum(v * v, bd) * (1.0 / HEAD_DIM) + EPS)
                vn = (v * r * vg_ref[:, cs]).astype(BF16)
                sp = jnp.where(lo, _dot(wm[2 * p], vn), _dot(wm[2 * p + 1], vn)) + b_ref[:, cs]
                y_ref[rows, cs] = (u * sp * (gt * _sigmoid(gt))).astype(BF16)

    return _call(
        body, name="gmlp_fwd", grid=(SEQ // tm,),
        in_specs=[pl.BlockSpec((tm, 3 * GMLP_W), lambda i: (i, 0)),
                  pl.BlockSpec((1, GMLP_W), lambda i: (0, 0)),
                  pl.BlockSpec((4, CHUNK, CHUNK), lambda i: (0, 0, 0)),
                  pl.BlockSpec((CHUNK, GMLP_W), lambda i: (0, 0))],
        out_specs=pl.BlockSpec((tm, GMLP_W), lambda i: (i, 0)),
        out_shape=jax.ShapeDtypeStruct((SEQ, GMLP_W), BF16),
        compiler_params=_params(),
    )(proj, vgain, w_s, bias_full)


def _gmlp_bwd(proj, dyc, vgain, w_s, bias_full):
    tm = 512
    nsteps = SEQ // tm

    def body(p_ref, dy_ref, vg_ref, w_ref, b_ref, dg_ref, gw_ref, gb_ref, gv_ref):
        i = pl.program_id(0)
        bd = _head_blockdiag()
        lo = _lo_mask(CHUNK)
        tril, wm = _gmlp_weights(w_ref)
        ri = lax.broadcasted_iota(jnp.int32, (16, LANES), 0)
        li = lax.broadcasted_iota(jnp.int32, (16, LANES), 1)
        head_rows = [jnp.where(((ri == 2 * p) & (li < HEAD_DIM)) | ((ri == 2 * p + 1) & (li >= HEAD_DIM)), 1.0, 0.0).astype(BF16)
                     for p in range(2)]

        @pl.when(i == 0)
        def _():
            gw_ref[...] = jnp.zeros_like(gw_ref)
            gb_ref[...] = jnp.zeros_like(gb_ref)
            gv_ref[...] = jnp.zeros_like(gv_ref)

        for c in range(tm // CHUNK):
            rows = pl.ds(c * CHUNK, CHUNK)
            for p in range(2):
                cs = slice(p * LANES, (p + 1) * LANES)
                u = p_ref[rows, C_GU + p * LANES:C_GU + (p + 1) * LANES]
                v = p_ref[rows, C_GV + p * LANES:C_GV + (p + 1) * LANES]
                gt = p_ref[rows, C_GG + p * LANES:C_GG + (p + 1) * LANES]
                dy = dy_ref[rows, cs]
                g = vg_ref[:, cs]
                r = lax.rsqrt(_headsum(v * v, bd) * (1.0 / HEAD_DIM) + EPS)
                z = v * r
                vn = (z * g).astype(BF16)
                sp = jnp.where(lo, _dot(wm[2 * p], vn), _dot(wm[2 * p + 1], vn)) + b_ref[:, cs]
                sg = _sigmoid(gt)
                sl = gt * sg
                dsl = sg * (1.0 + gt * (1.0 - sg))
                du = dy * sp * sl
                dsp = dy * u * sl
                dgt = dy * u * sp * dsl
                dspb = dsp.astype(BF16)
                dvn = jnp.where(lo, _dot_tn(wm[2 * p], dspb), _dot_tn(wm[2 * p + 1], dspb))
                gw_ref[2 * p] += _dot_nt(jnp.where(lo, dsp, 0.0).astype(BF16), vn)
                gw_ref[2 * p + 1] += _dot_nt(jnp.where(lo, 0.0, dsp).astype(BF16), vn)
                dsp_lo = (dsp - dspb.astype(F32)).astype(BF16)
                gb_ref[...] += (_dot_nt(head_rows[p], dspb) + _dot_nt(head_rows[p], dsp_lo))[0:8]
                gvp = jnp.sum(dvn * z, axis=0, keepdims=True)
                gv_ref[2 * p:2 * p + 1, :] += gvp
                gv_ref[2 * p + 1:2 * p + 2, :] += pltpu.roll(gvp, HEAD_DIM, 1)
                dz = dvn * g
                dv = r * (dz - z * (_headsum(dz * z, bd) * (1.0 / HEAD_DIM)))
                dg_ref[rows, C_GU + p * LANES:C_GU + (p + 1) * LANES] = du.astype(BF16)
                dg_ref[rows, C_GV + p * LANES:C_GV + (p + 1) * LANES] = dv.astype(BF16)
                dg_ref[rows, C_GG + p * LANES:C_GG + (p + 1) * LANES] = dgt.astype(BF16)

        @pl.when(i == nsteps - 1)
        def _():
            for h in range(4):
                gw_ref[h] = jnp.where(tril, gw_ref[h], 0.0)

    return _call(
        body, name="gmlp_bwd", grid=(nsteps,),
        in_specs=[pl.BlockSpec((tm, 3 * GMLP_W), lambda i: (i, 0)),
                  pl.BlockSpec((tm, GMLP_W), lambda i: (i, 0)),
                  pl.BlockSpec((1, GMLP_W), lambda i: (0, 0)),
                  pl.BlockSpec((4, CHUNK, CHUNK), lambda i: (0, 0, 0)),
                  pl.BlockSpec((CHUNK, GMLP_W), lambda i: (0, 0))],
        out_specs=[pl.BlockSpec((tm, 3 * GMLP_W), lambda i: (i, 0)),
                   pl.BlockSpec((4, CHUNK, CHUNK), lambda i: (0, 0, 0)),
                   pl.BlockSpec((8, LANES), lambda i: (0, 0)),
                   pl.BlockSpec((8, LANES), lambda i: (0, 0))],
        out_shape=[jax.ShapeDtypeStruct((SEQ, 3 * GMLP_W), BF16),
                   jax.ShapeDtypeStruct((4, CHUNK, CHUNK), F32),
                   jax.ShapeDtypeStruct((8, LANES), F32),
                   jax.ShapeDtypeStruct((8, LANES), F32)],
        compiler_params=_params(),
    )(proj, dyc, vgain, w_s, bias_full)


def _band_masks():
    qi = lax.broadcasted_iota(jnp.int32, (CHUNK, 2 * CHUNK), 0)
    kj = lax.broadcasted_iota(jnp.int32, (CHUNK, 2 * CHUNK), 1)
    valid2 = ((kj < CHUNK) & (kj >= qi)) | ((kj >= CHUNK) & (kj - CHUNK <= qi))
    q1 = lax.broadcasted_iota(jnp.int32, (CHUNK, CHUNK), 0)
    k1 = lax.broadcasted_iota(jnp.int32, (CHUNK, CHUNK), 1)
    return k1 <= q1, valid2


def _stack_heads(v, lo):
    return jnp.concatenate([jnp.where(lo, v, 0.0), jnp.where(lo, 0.0, v)], axis=0).astype(BF16)


def _rows_of(ref, start, d):
    if d == 1:
        return ref.at[pl.ds(start if isinstance(start, int) else pl.multiple_of(start, CHUNK), CHUNK), :]
    return ref.at[pl.ds(start, CHUNK, stride=d), :]


def _unrolled(lo, hi, unroll, run):
    groups = (hi - lo) // unroll
    if groups:
        def body(g, carry):
            run([lo + g * unroll + t for t in range(unroll)])
            return carry

        lax.fori_loop(0, groups, body, 0)
    if lo + groups * unroll < hi:
        run(range(lo + groups * unroll, hi))


def _for_blocks(d, group_fn, unroll):
    nblk = SEQ // CHUNK
    sh = d.bit_length() - 1

    def first(j):
        return (j * CHUNK if d == 1 else j, None)

    def rest(j):
        start = (j & (d - 1)) + (j >> sh) * (CHUNK * d)
        return (start, start - CHUNK * d)

    _unrolled(0, d, unroll, lambda js: group_fn(d, [first(j) for j in js]))
    _unrolled(d, nblk, unroll, lambda js: group_fn(d, [rest(j) for j in js]))


def _attn_fwd(proj, gq2, gk2):
    tn = 512

    def body(q_ref, k_ref, v_ref, g_ref, gq_ref, gk_ref, o_ref, l_ref, ya_ref, qn_ref, kn_ref):
        bd = _head_blockdiag()
        lo = _lo_mask(CHUNK)
        valid1, valid2 = _band_masks()

        def norm(t, carry):
            rows = pl.ds(pl.multiple_of(t * tn, tn), tn)
            q = q_ref[rows, :]
            qn_ref[rows, :] = q * lax.rsqrt(_headsum(q * q, bd) * (1.0 / HEAD_DIM) + EPS) * (gq_ref[...] * QK_SCALE)
            k = k_ref[rows, :]
            kn_ref[rows, :] = k * lax.rsqrt(_headsum(k * k, bd) * (1.0 / HEAD_DIM) + EPS) * gk_ref[...]
            return carry

        lax.fori_loop(0, SEQ // tn, norm, 0)

        def load_kv(ref, d, start, prev):
            own = _rows_of(ref, start, d)[...]
            if prev is None:
                return own.astype(BF16)
            return jnp.concatenate([_rows_of(ref, prev, d)[...], own], axis=0).astype(BF16)

        def group(d, blocks):
            valid = valid1 if blocks[0][1] is None else valid2
            valid = jnp.concatenate([valid, valid], axis=0)
            qs = [_rows_of(qn_ref, start, d)[...] for start, _ in blocks]
            ks = [load_kv(kn_ref, d, start, prev) for start, prev in blocks]
            vs = [load_kv(v_ref, d, start, prev) for start, prev in blocks]
            ss = [_dot_nt(_stack_heads(q, lo), k) for q, k in zip(qs, ks)]
            ms, ps, ls = [], [], []
            for s in ss:
                s = jnp.where(valid, s, -jnp.inf)
                m = jnp.max(s, axis=-1, keepdims=True)
                p = jnp.exp(s - m)
                ms.append(m)
                ls.append(jnp.sum(p, axis=-1, keepdims=True))
                ps.append(p.astype(BF16))
            os_ = [_dot(p, v) for p, v in zip(ps, vs)]
            for b, (start, _) in enumerate(blocks):
                on = os_[b] * (1.0 / ls[b])
                ln = ms[b] + jnp.log(ls[b])
                ob = jnp.where(lo, on[:CHUNK], on[CHUNK:])
                lb = jnp.where(lo, ln[:CHUNK], ln[CHUNK:])
                o_rows = _rows_of(o_ref, start, d)
                l_rows = _rows_of(l_ref, start, d)
                if d != DILATIONS[0]:
                    lold = l_rows[...]
                    mx = jnp.maximum(lold, lb)
                    ea = jnp.exp(lold - mx)
                    eb = jnp.exp(lb - mx)
                    inv = 1.0 / (ea + eb)
                    ob = o_rows[...] * (ea * inv) + ob * (eb * inv)
                    lb = mx + jnp.log(ea + eb)
                o_rows[...] = ob
                l_rows[...] = lb

        for d in DILATIONS:
            _for_blocks(d, group, ATTN_UNROLL)

        def fin(t, carry):
            rows = pl.ds(pl.multiple_of(t * tn, tn), tn)
            g = g_ref[rows, :]
            ya_ref[rows, :] = (o_ref[rows, :] * (g * _sigmoid(g))).astype(BF16)
            return carry

        lax.fori_loop(0, SEQ // tn, fin, 0)

    col = lambda c0: pl.BlockSpec((SEQ, LANES), lambda p: (0, c0 // LANES + p))
    vec = pl.BlockSpec((1, LANES), lambda p: (0, 0))
    out = pl.BlockSpec((SEQ, LANES), lambda p: (0, p))
    return _call(
        body, name="attn_fwd", grid=(ATTN_W // LANES,),
        in_specs=[col(C_AQ), col(C_AK), col(C_AV), col(C_AG), vec, vec],
        out_specs=[out, out, out],
        out_shape=[jax.ShapeDtypeStruct((SEQ, ATTN_W), F32), jax.ShapeDtypeStruct((SEQ, ATTN_W), F32),
                   jax.ShapeDtypeStruct((SEQ, ATTN_W), BF16)],
        scratch_shapes=[pltpu.VMEM((SEQ, LANES), F32), pltpu.VMEM((SEQ, LANES), F32)],
        compiler_params=_params(),
    )(proj, proj, proj, proj, gq2, gk2)


def _attn_bwd(proj, o, lse, dyc, gq2, gk2, *ride_along):
    tn = 512
    npairs = ATTN_W // LANES
    nride = len(ride_along)
    nbufs = nride * len(RS_KINDS)

    def body(proj_hbm, o_hbm, l_hbm, dyc_hbm, gq_ref, gk_ref, *rest):
        ride_in, rest = rest[:nride], rest[nride:]
        dq_ref, dk_ref, dv_ref, dgt_ref, gqg_ref, gkg_ref = rest[:6]
        ride_out, rest = rest[6:6 + nride], rest[6 + nride:]
        qb_, kb_, vb_, gb_, ob_, lb_, yb_, dkb_, dvb_, sems = rest[:10]
        rs_bufs, (send_sems, recv_sems, local_sems) = rest[10:10 + nbufs], rest[10 + nbufs:]
        rs_stage = _rs_stages(ride_in, ride_out, rs_bufs, send_sems, recv_sems, local_sems, [g.shape[1] for g in ride_along])
        pair = pl.program_id(0)
        for step in range(npairs):
            pl.when(pair == step)(rs_stage[step])
        bd = _head_blockdiag()
        lo = _lo_mask(CHUNK)
        lo2 = lax.broadcasted_iota(jnp.int32, (2 * CHUNK, LANES), 1) < HEAD_DIM
        valid1, valid2 = _band_masks()
        gqs = gq_ref[...] * QK_SCALE
        gk = gk_ref[...]

        def pcol(c0):
            return proj_hbm.at[:, pl.ds(pl.multiple_of(c0 + pair * LANES, LANES), LANES)]

        def acol(hbm, c0=0):
            return hbm.at[:, pl.ds(pl.multiple_of(c0 + pair * LANES, LANES), LANES)]

        loads = [pltpu.make_async_copy(src, dst, sems.at[n]) for n, (src, dst) in enumerate((
            (pcol(C_AQ), qb_), (pcol(C_AK), kb_), (pcol(C_AG), gb_), (acol(o_hbm), ob_),
            (acol(dyc_hbm, GMLP_W), yb_), (pcol(C_AV), vb_), (acol(l_hbm), lb_)))]
        for cp in loads:
            cp.start()

        @pl.when(pair == 0)
        def _():
            gqg_ref[...] = jnp.zeros_like(gqg_ref)
            gkg_ref[...] = jnp.zeros_like(gkg_ref)

        def pre_qk(t, carry):
            rows = pl.ds(pl.multiple_of(t * tn, tn), tn)
            zero = jnp.zeros((tn, LANES), F32)
            dkb_[rows, :] = zero
            dvb_[rows, :] = zero
            q = qb_[rows, :]
            qb_[rows, :] = q * lax.rsqrt(_headsum(q * q, bd) * (1.0 / HEAD_DIM) + EPS) * gqs
            k = kb_[rows, :]
            kb_[rows, :] = k * lax.rsqrt(_headsum(k * k, bd) * (1.0 / HEAD_DIM) + EPS) * gk
            return carry

        def pre_gate(t, carry):
            rows = pl.ds(pl.multiple_of(t * tn, tn), tn)
            g = gb_[rows, :]
            ov = ob_[rows, :]
            dya = yb_[rows, :]
            sg = _sigmoid(g)
            dgt_ref[rows, :] = (dya * ov * (sg * (1.0 + g * (1.0 - sg)))).astype(BF16)
            do = dya * (g * sg)
            yb_[rows, :] = do
            ob_[rows, :] = _headsum(do * ov, bd)
            gb_[rows, :] = jnp.zeros((tn, LANES), F32)
            return carry

        loads[0].wait()
        loads[1].wait()
        lax.fori_loop(0, SEQ // tn, pre_qk, 0)
        for cp in loads[2:5]:
            cp.wait()
        lax.fori_loop(0, SEQ // tn, pre_gate, 0)
        loads[5].wait()
        loads[6].wait()

        def load_kv(ref, d, start, prev):
            own = _rows_of(ref, start, d)[...]
            if prev is None:
                return own.astype(BF16)
            return jnp.concatenate([_rows_of(ref, prev, d)[...], own], axis=0).astype(BF16)

        def group(d, blocks):
            first = blocks[0][1] is None
            valid, lok = (valid1, lo) if first else (valid2, lo2)
            chains = [(b, h) for b in range(len(blocks)) for h in range(2)]
            mask = lambda h: lo if h == 0 else ~lo
            qs = [_rows_of(qb_, start, d)[...] for start, _ in blocks]
            dos = [_rows_of(yb_, start, d)[...] for start, _ in blocks]
            lvs = [_rows_of(lb_, start, d)[...] for start, _ in blocks]
            dls = [_rows_of(ob_, start, d)[...] for start, _ in blocks]
            ks = [load_kv(kb_, d, start, prev) for start, prev in blocks]
            vs = [load_kv(vb_, d, start, prev) for start, prev in blocks]
            qbs = [q.astype(BF16) for q in qs]
            dobs = [do.astype(BF16) for do in dos]
            ss = [_dot_nt(jnp.where(mask(h), qs[b], 0.0).astype(BF16), ks[b]) for b, h in chains]
            dps = [_dot_nt(jnp.where(mask(h), dos[b], 0.0).astype(BF16), vs[b]) for b, h in chains]
            pbs, dss = [], []
            for s, dp, (b, h) in zip(ss, dps, chains):
                hc = h * HEAD_DIM
                p = jnp.exp(jnp.where(valid, s, -jnp.inf) - lvs[b][:, hc:hc + 1])
                pbs.append(p.astype(BF16))
                dss.append((p * (dp - dls[b][:, hc:hc + 1])).astype(BF16))
            dqs = [_dot(ds, ks[b]) for ds, (b, h) in zip(dss, chains)]
            dks = [_dot_tn(ds, qbs[b]) for ds, (b, h) in zip(dss, chains)]
            dvs = [_dot_tn(p, dobs[b]) for p, (b, h) in zip(pbs, chains)]
            for b, (start, prev) in enumerate(blocks):
                c0, c1 = 2 * b, 2 * b + 1
                dq_rows = _rows_of(gb_, start, d)
                dq_rows[...] = dq_rows[...] + jnp.where(lo, dqs[c0], dqs[c1])
                dkc = jnp.where(lok, dks[c0], dks[c1])
                dvc = jnp.where(lok, dvs[c0], dvs[c1])
                spans = ((start, slice(0, CHUNK)),) if first else ((prev, slice(0, CHUNK)), (start, slice(CHUNK, 2 * CHUNK)))
                for st, sl in spans:
                    dk_rows = _rows_of(dkb_, st, d)
                    dk_rows[...] = dk_rows[...] + dkc[sl]
                    dv_rows = _rows_of(dvb_, st, d)
                    dv_rows[...] = dv_rows[...] + dvc[sl]

        for d in DILATIONS:
            _for_blocks(d, group, ATTN_UNROLL)

        reloads = [pltpu.make_async_copy(pcol(C_AQ), vb_, sems.at[0]), pltpu.make_async_copy(pcol(C_AK), lb_, sems.at[1])]
        for cp in reloads:
            cp.start()
        for cp in reloads:
            cp.wait()

        def post(t, carry):
            gq_acc, gk_acc = carry
            rows = pl.ds(pl.multiple_of(t * tn, tn), tn)
            outs = []
            for raw_, acc_, gain in ((vb_, gb_, gqs), (lb_, dkb_, gk)):
                a = raw_[rows, :]
                r = lax.rsqrt(_headsum(a * a, bd) * (1.0 / HEAD_DIM) + EPS)
                z = a * r
                dn = acc_[rows, :]
                dz = dn * gain
                outs.append((r * (dz - z * (_headsum(dz * z, bd) * (1.0 / HEAD_DIM))), jnp.sum(dn * z, axis=0, keepdims=True)))
            dq_ref[rows, :] = outs[0][0].astype(BF16)
            dk_ref[rows, :] = outs[1][0].astype(BF16)
            dv_ref[rows, :] = dvb_[rows, :].astype(BF16)
            return gq_acc + outs[0][1] * QK_SCALE, gk_acc + outs[1][1]

        zero = jnp.zeros((1, LANES), F32)
        gq_acc, gk_acc = lax.fori_loop(0, SEQ // tn, post, (zero, zero))
        gqg_ref[0:1, :] += gq_acc
        gkg_ref[0:1, :] += gk_acc

        @pl.when(pair == npairs - 1)
        def _():
            gqg_ref[0:1, :] = _fold_heads(gqg_ref[0:1, :])
            gkg_ref[0:1, :] = _fold_heads(gkg_ref[0:1, :])
            rs_stage[npairs]()

    hbm = pl.BlockSpec(memory_space=pl.ANY)
    vec = pl.BlockSpec((1, LANES), lambda p: (0, 0))
    blk8 = pl.BlockSpec((8, LANES), lambda p: (0, 0))
    out = pl.BlockSpec((SEQ, LANES), lambda p: (0, p))
    big = jax.ShapeDtypeStruct((SEQ, ATTN_W), BF16)
    nsem = RS_SEMS * nride
    return _call(
        body, name="attn_bwd", grid=(npairs,),
        in_specs=[hbm, hbm, hbm, hbm, vec, vec] + [hbm] * nride,
        out_specs=[out, out, out, out, blk8, blk8] + [hbm] * nride,
        out_shape=[big, big, big, big, jax.ShapeDtypeStruct((8, LANES), F32), jax.ShapeDtypeStruct((8, LANES), F32)]
        + [jax.ShapeDtypeStruct((2, g.shape[0] // 8, g.shape[1]), F32) for g in ride_along],
        scratch_shapes=[pltpu.VMEM((SEQ, LANES), F32) for _ in range(9)] + [pltpu.SemaphoreType.DMA((7,))]
        + _rs_scratch([g.shape for g in ride_along]) + [pltpu.SemaphoreType.DMA((nsem,)), pltpu.SemaphoreType.DMA((nsem,)),
                                     pltpu.SemaphoreType.DMA((nride,))],
        compiler_params=_params(),
    )(proj, o, lse, dyc, gq2, gk2, *[_rs_view(g) for g in ride_along])


def _mem_kv(mem, gain, wkv):
    def body(m_ref, g_ref, w_ref, kv_ref, hm_ref):
        mv = m_ref[...]
        ms = jnp.mean(mv * mv, axis=-1, keepdims=True)
        hm = (mv * lax.rsqrt(ms + EPS) * g_ref[...]).astype(BF16)
        hm_ref[...] = hm
        kv_ref[...] = _dot(hm, w_ref[...])

    return _call(
        body, name="mem_kv",
        out_shape=[jax.ShapeDtypeStruct((MEM_LEN, 2 * MEM_W), F32), jax.ShapeDtypeStruct((MEM_LEN, D_MODEL), BF16)],
        compiler_params=_params(),
    )(mem, gain, wkv)


def _mem_keys(kv_ref, kg_ref, bd, p):
    mk = kv_ref[:, p * LANES:(p + 1) * LANES]
    r = lax.rsqrt(_headsum(mk * mk, bd) * (1.0 / HEAD_DIM) + EPS)
    z = mk * r
    mkn = (z * kg_ref[:, p * LANES:(p + 1) * LANES]).astype(BF16)
    mvp = kv_ref[:, MEM_W + p * LANES:MEM_W + (p + 1) * LANES].astype(BF16)
    return mkn, mvp, r, z


def _mem_fwd(proj, kv, qg4, kg4):
    tm = 512

    def body(q_ref, g_ref, kv_ref, qg_ref, kg_ref, om_ref, ym_ref):
        bd = _head_blockdiag()
        lo = _lo_mask(tm)
        for p in range(2):
            cs = slice(p * LANES, (p + 1) * LANES)
            mkn, mvp, _, _ = _mem_keys(kv_ref, kg_ref, bd, p)
            q = q_ref[:, cs]
            qn = q * lax.rsqrt(_headsum(q * q, bd) * (1.0 / HEAD_DIM) + EPS) * (qg_ref[:, cs] * QK_SCALE)
            res = []
            for h in range(2):
                qh = jnp.where(lo if h == 0 else ~lo, qn, 0.0).astype(BF16)
                s = _dot_nt(qh, mkn)
                e = jnp.exp(s - jnp.max(s, axis=-1, keepdims=True))
                res.append(_dot(e.astype(BF16), mvp) * (1.0 / jnp.sum(e, axis=-1, keepdims=True)))
            ov = jnp.where(lo, res[0], res[1])
            g = g_ref[:, cs]
            om_ref[:, cs] = ov
            ym_ref[:, cs] = (ov * (g * _sigmoid(g))).astype(BF16)

    vec = pl.BlockSpec((1, MEM_W), lambda i: (0, 0))
    return _call(
        body, name="mem_fwd", grid=(SEQ // tm,),
        in_specs=[pl.BlockSpec((tm, MEM_W), lambda i: (i, C_MQ // MEM_W)),
                  pl.BlockSpec((tm, MEM_W), lambda i: (i, C_MG // MEM_W)),
                  pl.BlockSpec((MEM_LEN, 2 * MEM_W), lambda i: (0, 0)), vec, vec],
        out_specs=[pl.BlockSpec((tm, MEM_W), lambda i: (i, 0)), pl.BlockSpec((tm, MEM_W), lambda i: (i, 0))],
        out_shape=[jax.ShapeDtypeStruct((SEQ, MEM_W), F32), jax.ShapeDtypeStruct((SEQ, MEM_W), BF16)],
        compiler_params=_params(),
    )(proj, proj, kv, qg4, kg4)


def _mem_bwd(proj, om, dyc, kv, hm, mem, mgain, wkv, qg4, kg4):
    tm = 512
    nsteps = SEQ // tm

    def body(q_ref, g_ref, om_ref, dy_ref, kv_ref, hm_ref, mem_ref, mg_ref, w_ref, qg_ref, kg_ref,
             dq_ref, dgt_ref, gqg_ref, gkg_ref, gw_ref, gmg_ref, dmk_ref, dmv_ref, gq_acc):
        i = pl.program_id(0)
        bd = _head_blockdiag()
        lo = _lo_mask(tm)
        lom = _lo_mask(MEM_LEN)

        @pl.when(i == 0)
        def _():
            dmk_ref[...] = jnp.zeros_like(dmk_ref)
            dmv_ref[...] = jnp.zeros_like(dmv_ref)
            gq_acc[...] = jnp.zeros_like(gq_acc)

        for p in range(2):
            cs = slice(p * LANES, (p + 1) * LANES)
            mkn, mvp, _, _ = _mem_keys(kv_ref, kg_ref, bd, p)
            gqs = qg_ref[:, cs] * QK_SCALE
            q = q_ref[:, cs]
            r = lax.rsqrt(_headsum(q * q, bd) * (1.0 / HEAD_DIM) + EPS)
            z = q * r
            qn = z * gqs
            qnb = qn.astype(BF16)
            g = g_ref[:, cs]
            ov = om_ref[:, cs]
            dym = dy_ref[:, cs]
            sg = _sigmoid(g)
            dgt_ref[:, cs] = (dym * ov * (sg * (1.0 + g * (1.0 - sg)))).astype(BF16)
            do = dym * (g * sg)
            dob = do.astype(BF16)
            delta = _headsum(do * ov, bd)
            parts = []
            for h in range(2):
                mh = lo if h == 0 else ~lo
                hc = h * HEAD_DIM
                qh = jnp.where(mh, qn, 0.0).astype(BF16)
                doh = jnp.where(mh, do, 0.0).astype(BF16)
                s = _dot_nt(qh, mkn)
                e = jnp.exp(s - jnp.max(s, axis=-1, keepdims=True))
                pr = e * (1.0 / jnp.sum(e, axis=-1, keepdims=True))
                dp = _dot_nt(doh, mvp)
                ds = (pr * (dp - delta[:, hc:hc + 1])).astype(BF16)
                parts.append((_dot(ds, mkn), _dot_tn(ds, qnb), _dot_tn(pr.astype(BF16), dob)))
            dqn = jnp.where(lo, parts[0][0], parts[1][0])
            dmk_ref[:, cs] += jnp.where(lom, parts[0][1], parts[1][1])
            dmv_ref[:, cs] += jnp.where(lom, parts[0][2], parts[1][2])
            dz = dqn * gqs
            dq_ref[:, cs] = (r * (dz - z * (_headsum(dz * z, bd) * (1.0 / HEAD_DIM)))).astype(BF16)
            gq_acc[:, cs] += jnp.sum(dqn * z, axis=0, keepdims=True) * QK_SCALE

        @pl.when(i == nsteps - 1)
        def _():
            gqg_ref[...] = jnp.zeros_like(gqg_ref)
            gkg_ref[...] = jnp.zeros_like(gkg_ref)
            gqg_ref[0:1, :] = _fold_heads(gq_acc[:, 0:LANES] + gq_acc[:, LANES:2 * LANES])
            dkv = []
            gk = jnp.zeros((1, LANES), F32)
            for p in range(2):
                cs = slice(p * LANES, (p + 1) * LANES)
                _, _, r, z = _mem_keys(kv_ref, kg_ref, bd, p)
                dn = dmk_ref[:, cs]
                dz = dn * kg_ref[:, cs]
                gk = gk + jnp.sum(dn * z, axis=0, keepdims=True)
                dkv.append(r * (dz - z * (_headsum(dz * z, bd) * (1.0 / HEAD_DIM))))
            gkg_ref[0:1, :] = _fold_heads(gk)
            dkvb = jnp.concatenate(dkv + [dmv_ref[...]], axis=1).astype(BF16)
            gw_ref[...] = _dot_tn(hm_ref[...], dkvb)
            dhm = _dot_nt(dkvb, w_ref[...])
            mv = mem_ref[...]
            zm = mv * lax.rsqrt(jnp.mean(mv * mv, axis=-1, keepdims=True) + EPS)
            _put_rows(gmg_ref, jnp.sum(dhm * zm, axis=0, keepdims=True))

    const = lambda shape: pl.BlockSpec(shape, lambda i: (0,) * len(shape))
    row = lambda j: pl.BlockSpec((tm, MEM_W), lambda i: (i, j))
    blk8 = jax.ShapeDtypeStruct((8, LANES), F32)
    return _call(
        body, name="mem_bwd", grid=(nsteps,),
        in_specs=[row(C_MQ // MEM_W), row(C_MG // MEM_W), row(0), row((GMLP_W + ATTN_W) // MEM_W),
                  const((MEM_LEN, 2 * MEM_W)), const((MEM_LEN, D_MODEL)), const((MEM_LEN, D_MODEL)),
                  const((1, D_MODEL)), const((D_MODEL, 2 * MEM_W)), const((1, MEM_W)), const((1, MEM_W))],
        out_specs=[row(0), row(0), const((8, LANES)), const((8, LANES)),
                   const((D_MODEL, 2 * MEM_W)), const((8, LANES))],
        out_shape=[jax.ShapeDtypeStruct((SEQ, MEM_W), BF16), jax.ShapeDtypeStruct((SEQ, MEM_W), BF16),
                   blk8, blk8, jax.ShapeDtypeStruct((D_MODEL, 2 * MEM_W), F32), blk8],
        scratch_shapes=[pltpu.VMEM((MEM_LEN, MEM_W), F32), pltpu.VMEM((MEM_LEN, MEM_W), F32),
                        pltpu.VMEM((1, MEM_W), F32)],
        compiler_params=_params(),
    )(proj, proj, om, dyc, kv, hm, mem, mgain, wkv, qg4, kg4)


def _out_loss(yg, ya, ym, x, tgt, wo):
    tm = 512
    nsteps = SEQ // tm
    parts = ((0, GMLP_W), (GMLP_W, ATTN_W), (GMLP_W + ATTN_W, MEM_W))

    def body(yg_ref, ya_ref, ym_ref, x_ref, t_ref, w_ref, dy_ref, dyc_ref, gw_ref, ls_ref):
        i = pl.program_id(0)

        @pl.when(i == 0)
        def _():
            gw_ref[...] = jnp.zeros_like(gw_ref)
            ls_ref[...] = jnp.zeros_like(ls_ref)

        ys = (yg_ref[...], ya_ref[...], ym_ref[...])
        y = sum(_dot(yv, w_ref[r0:r0 + n, :]) for yv, (r0, n) in zip(ys, parts))
        err = x_ref[...] + y - t_ref[...]
        _put_rows(ls_ref, jnp.sum(err * err, axis=0, keepdims=True), accumulate=True)
        dy = err * (1.0 / D_MODEL)
        dy_ref[...] = dy
        dyb = dy.astype(BF16)
        dyc_ref[...] = _dot_nt(dyb, w_ref[...])
        for yv, (r0, n) in zip(ys, parts):
            gw_ref[r0:r0 + n, :] += _dot_tn(yv, dyb)

    row = lambda w: pl.BlockSpec((tm, w), lambda i: (i, 0))
    const = lambda shape: pl.BlockSpec(shape, lambda i: (0, 0))
    return _call(
        body, name="out_loss", grid=(nsteps,),
        in_specs=[row(GMLP_W), row(ATTN_W), row(MEM_W), row(D_MODEL), row(D_MODEL), const((D_MODEL, D_MODEL))],
        out_specs=[row(D_MODEL), row(D_MODEL), const((D_MODEL, D_MODEL)), const((8, LANES))],
        out_shape=[jax.ShapeDtypeStruct((SEQ, D_MODEL), F32), jax.ShapeDtypeStruct((SEQ, D_MODEL), F32),
                   jax.ShapeDtypeStruct((D_MODEL, D_MODEL), F32), jax.ShapeDtypeStruct((8, LANES), F32)],
        compiler_params=_params(),
    )(yg, ya, ym, x, tgt, wo)


def _proj_bwd(x, dy, gain, wt, dg, daq, dak, dav, dag, dmq, dmg):
    tm = 512
    nsteps = SEQ // tm
    pieces = ((C_GU, 3 * GMLP_W), (C_AQ, ATTN_W), (C_AK, ATTN_W), (C_AV, ATTN_W), (C_AG, ATTN_W),
              (C_MQ, MEM_W), (C_MG, MEM_W))

    def body(x_ref, dy_ref, g_ref, wt_hbm, p0, p1, p2, p3, p4, p5, p6, gx_ref, gwt_hbm, gg_ref, wt_v, acc, wt_sem):
        i = pl.program_id(0)
        wt_load = pltpu.make_async_copy(wt_hbm, wt_v, wt_sem)

        @pl.when(i == 0)
        def _():
            wt_load.start()
            acc[...] = jnp.zeros_like(acc)
            gg_ref[...] = jnp.zeros_like(gg_ref)

        xv = x_ref[...]
        r = lax.rsqrt(jnp.mean(xv * xv, axis=-1, keepdims=True) + EPS)
        z = xv * r
        g = g_ref[...]
        h = (z * g).astype(BF16)
        pl.when(i == 0)(wt_load.wait)
        dh = jnp.zeros((tm, D_MODEL), F32)
        for pref, (c0, w) in zip((p0, p1, p2, p3, p4, p5, p6), pieces):
            dp = pref[...]
            dh = dh + _dot(dp, wt_v[c0:c0 + w, :])
            acc[c0:c0 + w, :] += _dot_tn(dp, h)
        _put_rows(gg_ref, jnp.sum(dh * z, axis=0, keepdims=True), accumulate=True)
        dz = dh * g
        gx_ref[...] = dy_ref[...] + r * (dz - z * jnp.mean(dz * z, axis=-1, keepdims=True))

        @pl.when(i == nsteps - 1)
        def _():
            pltpu.sync_copy(acc, gwt_hbm)

    row = lambda w: pl.BlockSpec((tm, w), lambda i: (i, 0))
    hbm = pl.BlockSpec(memory_space=pl.ANY)
    vec = pl.BlockSpec((1, D_MODEL), lambda i: (0, 0))
    return _call(
        body, name="proj_bwd", grid=(nsteps,),
        in_specs=[row(D_MODEL), row(D_MODEL), vec, hbm] + [row(w) for _, w in pieces],
        out_specs=[row(D_MODEL), hbm, pl.BlockSpec((8, LANES), lambda i: (0, 0))],
        out_shape=[jax.ShapeDtypeStruct((SEQ, D_MODEL), F32), jax.ShapeDtypeStruct((IN_W, D_MODEL), F32),
                   jax.ShapeDtypeStruct((8, LANES), F32)],
        scratch_shapes=[pltpu.VMEM((IN_W, D_MODEL), BF16), pltpu.VMEM((IN_W, D_MODEL), F32), pltpu.SemaphoreType.DMA],
        compiler_params=_params(),
    )(x, dy, gain, wt, dg, daq, dak, dav, dag, dmq, dmg)


AG_SEMS = 8


def _gather_stages(ins, lands, send_sems, recv_sems):
    n = len(ins)
    nrows = [a.shape[0] for a in ins]
    x, y, c = lax.axis_index("x"), lax.axis_index("y"), lax.axis_index("c")
    sib, xn, yn = (x, y, 1 - c), (1 - x, y, c), (x, 1 - y, c)
    me, cx, cy, cd = 2 * x + y, 2 * (1 - x) + y, 2 * x + (1 - y), 2 * (1 - x) + (1 - y)

    def part(a, chip, hf, quarter=None):
        rows = nrows[a] // 2
        base = chip * nrows[a] + hf * rows
        if quarter is not None:
            rows = rows // 2
            base = base + quarter * rows
        return lands[a].at[pl.ds(pl.multiple_of(base, 16), rows), :]

    def copy(a, j, ref, to):
        k = AG_SEMS * a + j
        return pltpu.make_async_remote_copy(src_ref=ref, dst_ref=ref, send_sem=send_sems.at[k],
                                            recv_sem=recv_sems.at[k], device_id=to, device_id_type=MESH)

    def own(a):
        return [copy(a, 0, part(a, me, c), xn), copy(a, 1, part(a, me, c), yn)]

    def neighbours(a):
        return [copy(a, 4, part(a, cx, c, 1), yn), copy(a, 2, part(a, cx, c), sib),
                copy(a, 5, part(a, cy, c, 0), xn), copy(a, 3, part(a, cy, c), sib)]

    def diagonal(a):
        return [copy(a, 7, part(a, cd, c, 1), sib), copy(a, 6, part(a, cd, c, 0), sib)]

    def send_own():
        for a in range(n):
            lands[a][pl.ds(pl.multiple_of(me * nrows[a], 16), nrows[a]), :] = ins[a][...].astype(BF16)
            for cp in own(a):
                cp.start()

    def pass_on_neighbours():
        for a in range(n):
            copy(a, 0, part(a, cx, c), xn).wait_recv()
            copy(a, 1, part(a, cy, c), yn).wait_recv()
            for cp in neighbours(a):
                cp.start()

    def pass_on_diagonal():
        for a in range(n):
            copy(a, 4, part(a, cd, c, 1), yn).wait_recv()
            copy(a, 5, part(a, cd, c, 0), xn).wait_recv()
            for cp in diagonal(a):
                cp.start()

    def finish():
        for a in range(n):
            copy(a, 2, part(a, cx, 1 - c), sib).wait_recv()
            copy(a, 3, part(a, cy, 1 - c), sib).wait_recv()
            copy(a, 6, part(a, cd, 1 - c, 0), sib).wait_recv()
            copy(a, 7, part(a, cd, 1 - c, 1), sib).wait_recv()
            for cp in own(a) + neighbours(a) + diagonal(a):
                cp.wait_send()

    return send_own, pass_on_neighbours, pass_on_diagonal, finish


def _gather_weights(wt_sh):
    def body(a0, o0, send_sems, recv_sems):
        for stage in _gather_stages((a0,), (o0,), send_sems, recv_sems):
            stage()

    return _call(
        body, name="gather_weights",
        out_shape=jax.ShapeDtypeStruct((4 * wt_sh.shape[0], wt_sh.shape[1]), BF16),
        in_specs=[pl.BlockSpec(memory_space=pltpu.VMEM)],
        out_specs=pl.BlockSpec(memory_space=pltpu.VMEM),
        scratch_shapes=[pltpu.SemaphoreType.DMA((AG_SEMS,)), pltpu.SemaphoreType.DMA((AG_SEMS,))],
        compiler_params=_params(),
    )(wt_sh)


RS_SEMS = 6
RS_KINDS = (((2, 2), 1, F32), ((2, 2), 1, F32), ((2, 2), 2, BF16), ((2, 2), 2, BF16), ((2, 2), 2, F32),
            ((2,), 2, BF16), ((2,), 2, BF16), ((2,), 1, F32))


def _rs_view(g):
    return g.reshape(2, 2, 2, g.shape[0] // 8, g.shape[1])


def _rs_scratch(shapes, in_vmem=False):
    kinds = RS_KINDS[1:] if in_vmem else RS_KINDS
    return [pltpu.VMEM(lead + (r // 8, w // split), dt) for lead, split, dt in kinds for r, w in shapes]


def _rs_stages(gs, outs, bufs, send_sems, recv_sems, local_sems, widths):
    n = len(gs)
    if len(bufs) < n * len(RS_KINDS):
        bufs = [None] * n + list(bufs)
    loc, ra, s_b, r_b, acc1, s_c, r_c, fin = (bufs[n * i:n * i + n] for i in range(len(RS_KINDS)))
    half_w = [w // 2 for w in widths]
    chips = [(xx, yy) for xx in range(2) for yy in range(2)]
    x, y, c = lax.axis_index("x"), lax.axis_index("y"), lax.axis_index("c")
    sib, xn, yn = (x, y, 1 - c), (1 - x, y, c), (x, 1 - y, c)

    def copy(a, j, src, dst, to):
        k = RS_SEMS * a + j
        return pltpu.make_async_remote_copy(src_ref=src, dst_ref=dst, send_sem=send_sems.at[k],
                                            recv_sem=recv_sems.at[k], device_id=to, device_id_type=MESH)

    def step_a(a):
        if callable(gs[a]):
            return [copy(a, 0, gs[a](xx, yy, 1 - c), ra[a].at[xx, yy], sib) for xx, yy in chips]
        return [copy(a, 0, gs[a].at[:, :, 1 - c], ra[a], sib),
                pltpu.make_async_copy(gs[a].at[:, :, c], loc[a], local_sems.at[a])]

    def finish_a(a):
        if callable(gs[a]):
            copy(a, 0, ra[a], ra[a], sib).wait()
            for xx, yy in chips:
                ra[a][xx, yy] = gs[a](xx, yy, c)[...] + ra[a][xx, yy]
        else:
            for cp in step_a(a):
                cp.wait()
            ra[a][...] = loc[a][...] + ra[a][...]

    def step_b(a):
        return copy(a, 1, s_b[a].at[0], r_b[a].at[0], xn), copy(a, 2, s_b[a].at[1], r_b[a].at[1], yn)

    def step_c(a):
        return copy(a, 3, s_c[a].at[0], r_c[a].at[0], yn), copy(a, 4, s_c[a].at[1], r_c[a].at[1], xn)

    def step_d(a, half):
        rows = fin[a].at[half]
        return copy(a, 5, rows, rows, sib)

    def start():
        for a in range(n):
            for cp in step_a(a):
                cp.start()

    def a_to_b():
        for a in range(n):
            finish_a(a)
            s_b[a][0] = ra[a][1 - x, :, :, :half_w[a]].astype(BF16)
            s_b[a][1] = ra[a][:, 1 - y, :, half_w[a]:].astype(BF16)
            for cp in step_b(a):
                cp.start()

    def b_to_c():
        for a in range(n):
            for cp in step_b(a):
                cp.wait()
            acc1[a][0] = ra[a][x, :, :, :half_w[a]] + r_b[a][0].astype(F32)
            acc1[a][1] = ra[a][:, y, :, half_w[a]:] + r_b[a][1].astype(F32)
            s_c[a][0] = acc1[a][0, 1 - y].astype(BF16)
            s_c[a][1] = acc1[a][1, 1 - x].astype(BF16)
            for cp in step_c(a):
                cp.start()

    def c_to_d():
        for a in range(n):
            for cp in step_c(a):
                cp.wait()
            fin[a][c, :, :half_w[a]] = acc1[a][0, y] + r_c[a][0].astype(F32)
            fin[a][c, :, half_w[a]:] = acc1[a][1, x] + r_c[a][1].astype(F32)
            step_d(a, c).start()

    def finish():
        for a in range(n):
            step_d(a, 1 - c).wait_recv()
            step_d(a, c).wait_send()
            pltpu.sync_copy(fin[a], outs[a])

    return start, a_to_b, b_to_c, c_to_d, finish


def _reduce_grads(gwt, g_ws, tiny):
    cw = gwt.shape[1] // RS_CHUNKS
    chunk_shape = (gwt.shape[0], cw)

    def body(g0, ws_in, tiny_in, *rest):
        outs, o_ws, o_tiny = rest[:RS_CHUNKS], rest[RS_CHUNKS], rest[RS_CHUNKS + 1]
        rest = rest[RS_CHUNKS + 2:]
        nb = len(RS_KINDS) * RS_CHUNKS
        sm, sa, sb, sc, acc_s, send_sems, recv_sems, local_sems = rest[nb:]
        blocks = [g0.at[:, :, :, :, pl.ds(j * cw, cw)] for j in range(RS_CHUNKS)]
        start, a_to_b, b_to_c, c_to_d, finish = _rs_stages(blocks, outs, rest[:nb], send_sems, recv_sems, local_sems,
                                                           [cw] * RS_CHUNKS)
        n_ws = ws_in.shape[0]
        sm[0:n_ws, :] = ws_in[...]
        sm[n_ws:, :] = tiny_in[...]
        x, y, c = lax.axis_index("x"), lax.axis_index("y"), lax.axis_index("c")

        def small(j, src, dst, to):
            k = RS_SEMS * RS_CHUNKS + j
            return pltpu.make_async_remote_copy(src_ref=src, dst_ref=dst, send_sem=send_sems.at[k],
                                                recv_sem=recv_sems.at[k], device_id=to, device_id_type=MESH)

        along_c, along_x, along_y = (small(0, sm, sa, (x, y, 1 - c)), small(1, acc_s, sb, (1 - x, y, c)),
                                     small(2, sb, sc, (x, 1 - y, c)))
        start()
        along_c.start()
        a_to_b()
        along_c.wait()
        acc_s[...] = sm[...] + sa[...]
        along_x.start()
        b_to_c()
        along_x.wait()
        sb[...] = acc_s[...] + sb[...]
        along_y.start()
        c_to_d()
        along_y.wait()
        o_ws[...] = sb[0:n_ws, :] + sc[0:n_ws, :]
        o_tiny[...] = sb[n_ws:, :] + sc[n_ws:, :]
        finish()

    vm = pl.BlockSpec(memory_space=pltpu.VMEM)
    hbm = pl.BlockSpec(memory_space=pl.ANY)
    small_shape = (g_ws.shape[0] + tiny.shape[0], LANES)
    scratch = _rs_scratch([chunk_shape] * RS_CHUNKS) + [pltpu.VMEM(small_shape, F32) for _ in range(5)]
    nsem = RS_SEMS * RS_CHUNKS + 3
    scratch += [pltpu.SemaphoreType.DMA((nsem,)), pltpu.SemaphoreType.DMA((nsem,)), pltpu.SemaphoreType.DMA((RS_CHUNKS,))]
    return _call(
        body, name="reduce_grads",
        out_shape=[jax.ShapeDtypeStruct((2, gwt.shape[0] // 8, cw), F32)] * RS_CHUNKS
        + [jax.ShapeDtypeStruct(g_ws.shape, F32), jax.ShapeDtypeStruct(tiny.shape, F32)],
        in_specs=[hbm, vm, vm],
        out_specs=[hbm] * RS_CHUNKS + [vm, vm],
        scratch_shapes=scratch,
        compiler_params=_params(),
    )(_rs_view(gwt), g_ws, tiny)


def _adam_update(w, g, m, v):
    nm = ADAM_B1 * m + (1.0 - ADAM_B1) * g
    nv = ADAM_B2 * v + (1.0 - ADAM_B2) * (g * g)
    m_hat = nm / (1.0 - ADAM_B1 ** ADAM_STEP)
    v_hat = nv / (1.0 - ADAM_B2 ** ADAM_STEP)
    return -ADAM_LR * (m_hat / (jnp.sqrt(v_hat) + ADAM_EPS) + ADAM_WD * w), nm, nv


def _adamw(w, g, m, v):
    rows, cols = w.shape
    tm = max(t for t in range(8, 257, 8) if rows % t == 0)
    parts = tuple(g) if isinstance(g, (tuple, list)) else (g,)
    n = len(parts)

    def body(w_ref, m_ref, v_ref, *refs):
        gv = jnp.concatenate([r[...] for r in refs[:n]], axis=1)
        d_ref, nm_ref, nv_ref = refs[n:n + 3]
        d_ref[...], nm_ref[...], nv_ref[...] = _adam_update(w_ref[...], gv, m_ref[...], v_ref[...])
        if n > 1:
            refs[n + 3][...] = gv

    blk = pl.BlockSpec((tm, cols), lambda i: (i, 0))
    nout = 3 if n == 1 else 4
    res = _call(
        body, name="adamw", grid=(rows // tm,),
        in_specs=[blk] * 3 + [pl.BlockSpec((tm, p.shape[1]), lambda i: (i, 0)) for p in parts], out_specs=[blk] * nout,
        out_shape=[jax.ShapeDtypeStruct((rows, cols), F32)] * nout,
        compiler_params=_params(),
    )(w, m, v, *parts)
    return (parts[0] if n == 1 else res[3], *res[:3])


def _adamw_tiny(tiny, weights, ms, vs):
    shapes = [w.shape for w in weights]
    n = len(weights)

    def grad_of(t_ref, k, shape):
        base = 8 * k
        if shape[1] > LANES:
            return [t_ref[base + j:base + j + 1, :] for j in range(shape[1] // LANES)]
        return [t_ref[base:base + shape[0], 0:shape[1]]]

    def body(t_ref, *refs):
        w_refs, m_refs, v_refs = refs[:n], refs[n:2 * n], refs[2 * n:3 * n]
        loss_ref, outs = refs[3 * n], refs[3 * n + 1:]
        loss_ref[...] = (0.5 / D_MODEL) * jnp.sum(t_ref[8 * n:8 * n + 8, :], keepdims=True)
        for k, shape in enumerate(shapes):
            g_ref, d_ref, nm_ref, nv_ref = outs[4 * k:4 * k + 4]
            for j, g in enumerate(grad_of(t_ref, k, shape)):
                cols = slice(j * LANES, (j + 1) * LANES) if shape[1] > LANES else slice(None)
                g_ref[:, cols] = g
                d_ref[:, cols], nm_ref[:, cols], nv_ref[:, cols] = _adam_update(
                    w_refs[k][:, cols], g, m_refs[k][:, cols], v_refs[k][:, cols])

    out_shape = [jax.ShapeDtypeStruct((1, 1), F32)]
    for shape in shapes:
        out_shape += [jax.ShapeDtypeStruct(shape, F32)] * 4
    return _call(body, name="adamw_tiny", out_shape=out_shape, compiler_params=_params())(tiny, *weights, *ms, *vs)


def _local_grads(x, mem, tgt, norm_gain, wt, gmlp_v_gain, gmlp_w_s, gmlp_b, attn_q_gain, attn_k_gain,
                 mem_norm_gain, wkv_sh, mem_q_gain, mem_k_gain, wo_sh):
    vg = gmlp_v_gain.reshape(1, GMLP_W)
    bias_full = jnp.repeat(gmlp_b.T, HEAD_DIM, axis=1)
    gq2, gk2 = jnp.tile(attn_q_gain, (1, 2)), jnp.tile(attn_k_gain, (1, 2))
    qg4, kg4 = jnp.tile(mem_q_gain, (1, 4)), jnp.tile(mem_k_gain, (1, 4))

    proj, wkv, wo = _fwd_proj(x, norm_gain, wt, wkv_sh, wo_sh)
    yg = _gmlp_fwd(proj, vg, gmlp_w_s, bias_full)
    o, lse, ya = _attn_fwd(proj, gq2, gk2)
    kv, hm = _mem_kv(mem, mem_norm_gain, wkv)
    om, ym = _mem_fwd(proj, kv, qg4, kg4)
    dy, dyc, g_wo, err2 = _out_loss(yg, ya, ym, x, tgt, wo)
    dmq, dmg, g_mq, g_mk, g_wkv, g_mng = _mem_bwd(proj, om, dyc, kv, hm, mem, mem_norm_gain, wkv, qg4, kg4)
    daq, dak, dav, dag, g_aq, g_ak, g_wkv_sh, g_wo_sh = _attn_bwd(proj, o, lse, dyc, gq2, gk2, g_wkv, g_wo)
    dg, g_ws, g_b, g_vg = _gmlp_bwd(proj, dyc, vg, gmlp_w_s, bias_full)
    gx, g_wt, g_ng = _proj_bwd(x, dy, norm_gain, wt, dg, daq, dak, dav, dag, dmq, dmg)

    tiny = jnp.concatenate([g_ng, g_vg, g_b, g_aq, g_ak, g_mng, g_mq, g_mk, err2], axis=0)
    return gx, g_wt, g_wkv_sh, g_wo_sh, g_ws.reshape(4 * CHUNK, CHUNK), tiny


def kernel(x, mem, norm_gain, w_in, gmlp_v_gain, gmlp_w_s, gmlp_b, attn_q_gain, attn_k_gain, mem_norm_gain, w_mem_kv, mem_q_gain, mem_k_gain, w_out, loss_target, m_norm_gain, m_w_in, m_gmlp_v_gain, m_gmlp_w_s, m_gmlp_b, m_attn_q_gain, m_attn_k_gain, m_mem_norm_gain, m_w_mem_kv, m_mem_q_gain, m_mem_k_gain, m_w_out, v_norm_gain, v_w_in, v_gmlp_v_gain, v_gmlp_w_s, v_gmlp_b, v_attn_q_gain, v_attn_k_gain, v_mem_norm_gain, v_w_mem_kv, v_mem_q_gain, v_mem_k_gain, v_w_out):
    wt = _gather_weights(w_in[0].T)
    gx, g_wt, g_wkv_sh, g_wo_sh, g_ws, tiny = _local_grads(
        x[0], mem[0], loss_target[0], norm_gain, wt, gmlp_v_gain[0], gmlp_w_s[0], gmlp_b[0],
        attn_q_gain, attn_k_gain, mem_norm_gain, w_mem_kv[0], mem_q_gain, mem_k_gain, w_out[0])
    *g_wt_sh, g_ws, tiny = _reduce_grads(g_wt, g_ws, tiny)
    chip_block = lambda g: g.reshape(2 * g.shape[1], g.shape[2])
    g_wt_sh = tuple(chip_block(g) for g in g_wt_sh)
    g_wkv_sh, g_wo_sh = chip_block(g_wkv_sh), chip_block(g_wo_sh)

    ws = (norm_gain, w_in, gmlp_v_gain, gmlp_w_s, gmlp_b, attn_q_gain, attn_k_gain, mem_norm_gain, w_mem_kv,
          mem_q_gain, mem_k_gain, w_out)
    ms = (m_norm_gain, m_w_in, m_gmlp_v_gain, m_gmlp_w_s, m_gmlp_b, m_attn_q_gain, m_attn_k_gain, m_mem_norm_gain,
          m_w_mem_kv, m_mem_q_gain, m_mem_k_gain, m_w_out)
    vs = (v_norm_gain, v_w_in, v_gmlp_v_gain, v_gmlp_w_s, v_gmlp_b, v_attn_q_gain, v_attn_k_gain, v_mem_norm_gain,
          v_w_mem_kv, v_mem_q_gain, v_mem_k_gain, v_w_out)
    form = {1: lambda a: a[0].T, 3: lambda a: a.reshape(4 * CHUNK, CHUNK), 2: lambda a: a[0], 4: lambda a: a[0],
            8: lambda a: a[0], 11: lambda a: a[0]}
    back = {1: lambda a: a.T[None], 3: lambda a: a.reshape(1, 4, CHUNK, CHUNK), 2: lambda a: a[None],
            4: lambda a: a[None], 8: lambda a: a[None], 11: lambda a: a[None]}
    fwd = lambda t, i: form.get(i, lambda a: a)(t[i])
    out = {}
    for i, g in ((1, g_wt_sh), (3, g_ws), (8, g_wkv_sh), (11, g_wo_sh)):
        out[i] = _adamw(fwd(ws, i), g, fwd(ms, i), fwd(vs, i))
    res = _adamw_tiny(tiny, [fwd(ws, i) for i in TINY_ORDER], [fwd(ms, i) for i in TINY_ORDER],
                      [fwd(vs, i) for i in TINY_ORDER])
    for k, i in enumerate(TINY_ORDER):
        out[i] = res[1 + 4 * k:5 + 4 * k]
    leaves = [[back.get(i, lambda a: a)(out[i][j]) for i in range(12)] for j in range(4)]
    return (res[0].reshape(()), gx[None], *leaves[0], *leaves[1], *leaves[2], *leaves[3])
```

```python
import functools
import math

import jax
import jax.numpy as jnp
from jax import lax
from jax.experimental import pallas as pl
from jax.experimental.pallas import tpu as pltpu

F32 = jnp.float32
BF16 = jnp.bfloat16

SEQ = 4096
D_MODEL = 1024
HEAD_DIM = 64
LANES = 128
CHUNK = 128
GMLP_W, ATTN_W, MEM_W = 256, 512, 256
IN_W = 3 * GMLP_W + 4 * ATTN_W + 2 * MEM_W
MEM_LEN = 256
DILATIONS = (1, 4, 16)
EPS = 1e-6
QK_SCALE = 1.0 / math.sqrt(HEAD_DIM)
C_GU, C_GV, C_GG, C_AQ, C_AK, C_AV, C_AG, C_MQ, C_MG = 0, 256, 512, 768, 1280, 1792, 2304, 2816, 3072

ADAM_LR, ADAM_B1, ADAM_B2, ADAM_EPS, ADAM_WD, ADAM_STEP = 0.001, 0.9, 0.999, 1e-08, 0.01, 10

VMEM_LIMIT = 48 * 1024 * 1024
RS_CHUNKS = 4
ATTN_UNROLL = 4
MESH = pl.DeviceIdType.MESH

TINY_ORDER = (0, 2, 4, 5, 6, 7, 9, 10)


def _call(body, **kw):
    return pl.pallas_call(body, **kw)


def _params(**kw):
    return pltpu.CompilerParams(vmem_limit_bytes=VMEM_LIMIT, **kw)


def _dot(a, b):
    return jnp.dot(a, b, preferred_element_type=F32)


def _dot_nt(a, b):
    return lax.dot_general(a, b, (((1,), (1,)), ((), ())), preferred_element_type=F32)


def _dot_tn(a, b):
    return lax.dot_general(a, b, (((0,), (0,)), ((), ())), preferred_element_type=F32)


def _head_blockdiag():
    r = lax.shift_right_logical(lax.broadcasted_iota(jnp.int32, (LANES, LANES), 0), 6)
    c = lax.shift_right_logical(lax.broadcasted_iota(jnp.int32, (LANES, LANES), 1), 6)
    return jnp.where(r == c, 1.0, 0.0).astype(BF16)


def _headsum(v, bd):
    hi = v.astype(BF16)
    lo = (v - hi.astype(F32)).astype(BF16)
    return _dot(hi, bd) + _dot(lo, bd)


def _lo_mask(rows):
    return lax.broadcasted_iota(jnp.int32, (rows, LANES), 1) < HEAD_DIM


def _sigmoid(x):
    return 1.0 / (1.0 + jnp.exp(-x))


def _fold_heads(v):
    return v + pltpu.roll(v, HEAD_DIM, 1)


def _put_rows(ref, vec, accumulate=False):
    for j in range(vec.shape[1] // LANES):
        piece = vec[:, j * LANES:(j + 1) * LANES]
        ref[j:j + 1, :] = ref[j:j + 1, :] + piece if accumulate else piece


def _gather_proj(x, gain, wt_sh, *ride_along):
    tm = 512
    nrow = SEQ // tm
    nride = len(ride_along)
    widths = (768, 896, 768, 896)
    pair = 2 * wt_sh.shape[0]
    assert pair % LANES == 0 and sum(widths[:2]) == pair

    def body(x_ref, g_ref, wt_sh_ref, *rest):
        shards, rest = rest[:nride], rest[nride:]
        proj_hbm, wt_hbm, gathered = rest[0], rest[1], rest[2:2 + nride]
        h_scr, land, res = rest[2 + nride:5 + nride]
        lands, (send0, recv0, send1, recv1, out_sems, copy_sems) = rest[5 + nride:5 + 2 * nride], rest[5 + 2 * nride:]
        u, i = pl.program_id(0), pl.program_id(1)
        cx_, cy_ = lax.axis_index("x"), lax.axis_index("y")
        (send_own, pass_on_neighbours, pass_on_diagonal, _), (y_complete, x_complete, diagonal_complete, sends_done) = (
            _gather_stages((wt_sh_ref,), (land,), send0, recv0))
        ride, _ = _gather_stages(shards, lands, send1, recv1)
        first = lambda k: (u == k) & (i == 0)
        last = (u == 3) & (i == nrow - 1)
        copies = [pltpu.make_async_copy(land, wt_hbm, copy_sems.at[0])] + [
            pltpu.make_async_copy(src, dst, copy_sems.at[1 + k]) for k, (src, dst) in enumerate(zip(lands, gathered))]

        pl.when(first(0))(send_own)

        @pl.when(u == 0)
        def _():
            xv = x_ref[...]
            ms = jnp.mean(xv * xv, axis=-1, keepdims=True)
            h_scr[pl.ds(pl.multiple_of(i * tm, tm), tm), :] = (xv * lax.rsqrt(ms + EPS) * g_ref[...]).astype(BF16)

        @pl.when(first(1))
        def _():
            pass_on_neighbours()
            ride[0]()
            y_complete()

        @pl.when(first(2))
        def _():
            x_complete()
            pass_on_diagonal()
            ride[1]()

        @pl.when(first(3))
        def _():
            diagonal_complete()
            copies[0].start()
            ride[2]()

        col0 = (pair * cx_ + 896 * cy_, pair * cx_ + 768 * (1 - cy_),
                pair * (1 - cx_) + 896 * cy_, pair * (1 - cx_) + 768 * (1 - cy_))
        slot = i % 2
        rows = pl.ds(pl.multiple_of(i * tm, tm), tm)

        def writeback(k, rows_):
            c0 = pl.multiple_of(col0[k], LANES)
            return pltpu.make_async_copy(res.at[slot, :, pl.ds(0, widths[k])], proj_hbm.at[rows_, pl.ds(c0, widths[k])],
                                         out_sems.at[slot])

        for k in range(4):
            @pl.when(u == k)
            def _(k=k):
                pl.when(i >= 2)(writeback(k, rows).wait)
                if k > 0:
                    pl.when(i < 2)(writeback(k - 1, rows).wait)
                w_rows = land[pl.ds(pl.multiple_of(col0[k], LANES), widths[k]), :]
                res[slot, :, 0:widths[k]] = _dot_nt(h_scr[rows, :], w_rows)
                writeback(k, rows).start()

        @pl.when(last)
        def _():
            sends_done()
            ride[3]()
            for cp in copies[1:]:
                cp.start()
            for cp in copies:
                cp.wait()
            pltpu.make_async_copy(res.at[0, :, pl.ds(0, widths[3])], proj_hbm.at[rows, pl.ds(0, widths[3])], out_sems.at[0]).wait()
            pltpu.make_async_copy(res.at[1, :, pl.ds(0, widths[3])], proj_hbm.at[rows, pl.ds(0, widths[3])], out_sems.at[1]).wait()

    full = [jax.ShapeDtypeStruct((4 * a.shape[0], a.shape[1]), BF16) for a in (wt_sh,) + ride_along]
    hbm = pl.BlockSpec(memory_space=pl.ANY)
    const = lambda a: pl.BlockSpec(a.shape, lambda u, i: (0, 0))
    return _call(
        body, name="gather_proj", grid=(4, nrow),
        in_specs=[pl.BlockSpec((tm, D_MODEL), lambda u, i: (jnp.where(u == 0, i, nrow - 1), 0)),
                  pl.BlockSpec((1, D_MODEL), lambda u, i: (0, 0)), const(wt_sh)] + [const(a) for a in ride_along],
        out_specs=[hbm] * (2 + nride),
        out_shape=[jax.ShapeDtypeStruct((SEQ, IN_W), F32)] + full,
        scratch_shapes=[pltpu.VMEM((SEQ, D_MODEL), BF16), pltpu.VMEM(full[0].shape, BF16), pltpu.VMEM((2, tm, max(widths)), F32)]
        + [pltpu.VMEM(s.shape, BF16) for s in full[1:]]
        + [pltpu.SemaphoreType.DMA((AG_SEMS,)), pltpu.SemaphoreType.DMA((AG_SEMS,)),
           pltpu.SemaphoreType.DMA((AG_SEMS * nride,)), pltpu.SemaphoreType.DMA((AG_SEMS * nride,)),
           pltpu.SemaphoreType.DMA((2,)), pltpu.SemaphoreType.DMA((1 + nride,))],
        compiler_params=_params(),
    )(x, gain, wt_sh, *ride_along)


def _gmlp_weights(w_ref):
    ti = lax.broadcasted_iota(jnp.int32, (CHUNK, CHUNK), 0)
    si = lax.broadcasted_iota(jnp.int32, (CHUNK, CHUNK), 1)
    tril = si <= ti
    return tril, [jnp.where(tril, w_ref[h], 0.0).astype(BF16) for h in range(4)]


def _gmlp_fwd(proj, vgain, w_s, bias_full):
    tm = 512

    def body(p_ref, vg_ref, w_ref, b_ref, y_ref):
        bd = _head_blockdiag()
        lo = _lo_mask(CHUNK)
        _, wm = _gmlp_weights(w_ref)
        for c in range(tm // CHUNK):
            rows = pl.ds(c * CHUNK, CHUNK)
            for p in range(2):
                cs = slice(p * LANES, (p + 1) * LANES)
                u = p_ref[rows, C_GU + p * LANES:C_GU + (p + 1) * LANES]
                v = p_ref[rows, C_GV + p * LANES:C_GV + (p + 1) * LANES]
                gt = p_ref[rows, C_GG + p * LANES:C_GG + (p + 1) * LANES]
                r = lax.rsqrt(_headsum(v * v, bd) * (1.0 / HEAD_DIM) + EPS)
                vn = (v * r * vg_ref[:, cs]).astype(BF16)
                sp = jnp.where(lo, _dot(wm[2 * p], vn), _dot(wm[2 * p + 1], vn)) + b_ref[:, cs]
                y_ref[rows, cs] = (u * sp * (gt * _sigmoid(gt))).astype(BF16)

    return _call(
        body, name="gmlp_fwd", grid=(SEQ // tm,),
        in_specs=[pl.BlockSpec((tm, 3 * GMLP_W), lambda i: (i, 0)),
                  pl.BlockSpec((1, GMLP_W), lambda i: (0, 0)),
                  pl.BlockSpec((4, CHUNK, CHUNK), lambda i: (0, 0, 0)),
                  pl.BlockSpec((CHUNK, GMLP_W), lambda i: (0, 0))],
        out_specs=pl.BlockSpec((tm, GMLP_W), lambda i: (i, 0)),
        out_shape=jax.ShapeDtypeStruct((SEQ, GMLP_W), BF16),
        compiler_params=_params(),
    )(proj, vgain, w_s, bias_full)


def _gmlp_bwd(proj, dyc, vgain, w_s, bias_full):
    tm = 512
    nsteps = SEQ // tm

    def body(p_ref, dy_ref, vg_ref, w_ref, b_ref, dg_ref, gw_ref, gb_ref, gv_ref):
        i = pl.program_id(0)
        bd = _head_blockdiag()
        lo = _lo_mask(CHUNK)
        tril, wm = _gmlp_weights(w_ref)
        ri = lax.broadcasted_iota(jnp.int32, (16, LANES), 0)
        li = lax.broadcasted_iota(jnp.int32, (16, LANES), 1)
        head_rows = [jnp.where(((ri == 2 * p) & (li < HEAD_DIM)) | ((ri == 2 * p + 1) & (li >= HEAD_DIM)), 1.0, 0.0).astype(BF16)
                     for p in range(2)]

        @pl.when(i == 0)
        def _():
            gw_ref[...] = jnp.zeros_like(gw_ref)
            gb_ref[...] = jnp.zeros_like(gb_ref)
            gv_ref[...] = jnp.zeros_like(gv_ref)

        for c in range(tm // CHUNK):
            rows = pl.ds(c * CHUNK, CHUNK)
            for p in range(2):
                cs = slice(p * LANES, (p + 1) * LANES)
                u = p_ref[rows, C_GU + p * LANES:C_GU + (p + 1) * LANES]
                v = p_ref[rows, C_GV + p * LANES:C_GV + (p + 1) * LANES]
                gt = p_ref[rows, C_GG + p * LANES:C_GG + (p + 1) * LANES]
                dy = dy_ref[rows, cs]
                g = vg_ref[:, cs]
                r = lax.rsqrt(_headsum(v * v, bd) * (1.0 / HEAD_DIM) + EPS)
                z = v * r
                vn = (z * g).astype(BF16)
                sp = jnp.where(lo, _dot(wm[2 * p], vn), _dot(wm[2 * p + 1], vn)) + b_ref[:, cs]
                sg = _sigmoid(gt)
                sl = gt * sg
                dsl = sg * (1.0 + gt * (1.0 - sg))
                du = dy * sp * sl
                dsp = dy * u * sl
                dgt = dy * u * sp * dsl
                dspb = dsp.astype(BF16)
                dvn = jnp.where(lo, _dot_tn(wm[2 * p], dspb), _dot_tn(wm[2 * p + 1], dspb))
                gw_ref[2 * p] += _dot_nt(jnp.where(lo, dsp, 0.0).astype(BF16), vn)
                gw_ref[2 * p + 1] += _dot_nt(jnp.where(lo, 0.0, dsp).astype(BF16), vn)
                dsp_lo = (dsp - dspb.astype(F32)).astype(BF16)
                gb_ref[...] += (_dot_nt(head_rows[p], dspb) + _dot_nt(head_rows[p], dsp_lo))[0:8]
                gvp = jnp.sum(dvn * z, axis=0, keepdims=True)
                gv_ref[2 * p:2 * p + 1, :] += gvp
                gv_ref[2 * p + 1:2 * p + 2, :] += pltpu.roll(gvp, HEAD_DIM, 1)
                dz = dvn * g
                dv = r * (dz - z * (_headsum(dz * z, bd) * (1.0 / HEAD_DIM)))
                dg_ref[rows, C_GU + p * LANES:C_GU + (p + 1) * LANES] = du.astype(BF16)
                dg_ref[rows, C_GV + p * LANES:C_GV + (p + 1) * LANES] = dv.astype(BF16)
                dg_ref[rows, C_GG + p * LANES:C_GG + (p + 1) * LANES] = dgt.astype(BF16)

        @pl.when(i == nsteps - 1)
        def _():
            for h in range(4):
                gw_ref[h] = jnp.where(tril, gw_ref[h], 0.0)

    return _call(
        body, name="gmlp_bwd", grid=(nsteps,),
        in_specs=[pl.BlockSpec((tm, 3 * GMLP_W), lambda i: (i, 0)),
                  pl.BlockSpec((tm, GMLP_W), lambda i: (i, 0)),
                  pl.BlockSpec((1, GMLP_W), lambda i: (0, 0)),
                  pl.BlockSpec((4, CHUNK, CHUNK), lambda i: (0, 0, 0)),
                  pl.BlockSpec((CHUNK, GMLP_W), lambda i: (0, 0))],
        out_specs=[pl.BlockSpec((tm, 3 * GMLP_W), lambda i: (i, 0)),
                   pl.BlockSpec((4, CHUNK, CHUNK), lambda i: (0, 0, 0)),
                   pl.BlockSpec((8, LANES), lambda i: (0, 0)),
                   pl.BlockSpec((8, LANES), lambda i: (0, 0))],
        out_shape=[jax.ShapeDtypeStruct((SEQ, 3 * GMLP_W), BF16),
                   jax.ShapeDtypeStruct((4, CHUNK, CHUNK), F32),
                   jax.ShapeDtypeStruct((8, LANES), F32),
                   jax.ShapeDtypeStruct((8, LANES), F32)],
        compiler_params=_params(),
    )(proj, dyc, vgain, w_s, bias_full)


def _band_masks():
    qi = lax.broadcasted_iota(jnp.int32, (CHUNK, 2 * CHUNK), 0)
    kj = lax.broadcasted_iota(jnp.int32, (CHUNK, 2 * CHUNK), 1)
    valid2 = ((kj < CHUNK) & (kj >= qi)) | ((kj >= CHUNK) & (kj - CHUNK <= qi))
    q1 = lax.broadcasted_iota(jnp.int32, (CHUNK, CHUNK), 0)
    k1 = lax.broadcasted_iota(jnp.int32, (CHUNK, CHUNK), 1)
    return k1 <= q1, valid2


def _stack_heads(v, lo):
    return jnp.concatenate([jnp.where(lo, v, 0.0), jnp.where(lo, 0.0, v)], axis=0).astype(BF16)


def _rows_of(ref, start, d):
    if d == 1:
        return ref.at[pl.ds(start if isinstance(start, int) else pl.multiple_of(start, CHUNK), CHUNK), :]
    return ref.at[pl.ds(start, CHUNK, stride=d), :]


def _unrolled(lo, hi, unroll, run):
    groups = (hi - lo) // unroll
    if groups:
        def body(g, carry):
            run([lo + g * unroll + t for t in range(unroll)])
            return carry

        lax.fori_loop(0, groups, body, 0)
    if lo + groups * unroll < hi:
        run(range(lo + groups * unroll, hi))


def _for_blocks(d, group_fn, unroll):
    nblk = SEQ // CHUNK
    sh = d.bit_length() - 1

    def first(j):
        return (j * CHUNK if d == 1 else j, None)

    def rest(j):
        start = (j & (d - 1)) + (j >> sh) * (CHUNK * d)
        return (start, start - CHUNK * d)

    _unrolled(0, d, unroll, lambda js: group_fn(d, [first(j) for j in js]))
    _unrolled(d, nblk, unroll, lambda js: group_fn(d, [rest(j) for j in js]))


def _attn_fwd(proj, gq2, gk2):
    tn = 512

    def body(q_ref, k_ref, v_ref, g_ref, gq_ref, gk_ref, o_ref, l_ref, ya_ref, qn_ref, kn_ref):
        bd = _head_blockdiag()
        lo = _lo_mask(CHUNK)
        valid1, valid2 = _band_masks()

        def norm(t, carry):
            rows = pl.ds(pl.multiple_of(t * tn, tn), tn)
            q = q_ref[rows, :]
            qn_ref[rows, :] = q * lax.rsqrt(_headsum(q * q, bd) * (1.0 / HEAD_DIM) + EPS) * (gq_ref[...] * QK_SCALE)
            k = k_ref[rows, :]
            kn_ref[rows, :] = k * lax.rsqrt(_headsum(k * k, bd) * (1.0 / HEAD_DIM) + EPS) * gk_ref[...]
            return carry

        lax.fori_loop(0, SEQ // tn, norm, 0)

        def load_kv(ref, d, start, prev):
            own = _rows_of(ref, start, d)[...]
            if prev is None:
                return own.astype(BF16)
            return jnp.concatenate([_rows_of(ref, prev, d)[...], own], axis=0).astype(BF16)

        def group(d, blocks):
            valid = valid1 if blocks[0][1] is None else valid2
            valid = jnp.concatenate([valid, valid], axis=0)
            qs = [_rows_of(qn_ref, start, d)[...] for start, _ in blocks]
            ks = [load_kv(kn_ref, d, start, prev) for start, prev in blocks]
            vs = [load_kv(v_ref, d, start, prev) for start, prev in blocks]
            ss = [_dot_nt(_stack_heads(q, lo), k) for q, k in zip(qs, ks)]
            ms, ps, ls = [], [], []
            for s in ss:
                s = jnp.where(valid, s, -jnp.inf)
                m = jnp.max(s, axis=-1, keepdims=True)
                p = jnp.exp(s - m)
                ms.append(m)
                ls.append(jnp.sum(p, axis=-1, keepdims=True))
                ps.append(p.astype(BF16))
            os_ = [_dot(p, v) for p, v in zip(ps, vs)]
            for b, (start, _) in enumerate(blocks):
                on = os_[b] * (1.0 / ls[b])
                ln = ms[b] + jnp.log(ls[b])
                ob = jnp.where(lo, on[:CHUNK], on[CHUNK:])
                lb = jnp.where(lo, ln[:CHUNK], ln[CHUNK:])
                o_rows = _rows_of(o_ref, start, d)
                l_rows = _rows_of(l_ref, start, d)
                if d != DILATIONS[0]:
                    lold = l_rows[...]
                    mx = jnp.maximum(lold, lb)
                    ea = jnp.exp(lold - mx)
                    eb = jnp.exp(lb - mx)
                    inv = 1.0 / (ea + eb)
                    ob = o_rows[...] * (ea * inv) + ob * (eb * inv)
                    lb = mx + jnp.log(ea + eb)
                o_rows[...] = ob
                l_rows[...] = lb

        for d in DILATIONS:
            _for_blocks(d, group, ATTN_UNROLL)

        def fin(t, carry):
            rows = pl.ds(pl.multiple_of(t * tn, tn), tn)
            g = g_ref[rows, :]
            ya_ref[rows, :] = (o_ref[rows, :] * (g * _sigmoid(g))).astype(BF16)
            return carry

        lax.fori_loop(0, SEQ // tn, fin, 0)

    col = lambda c0: pl.BlockSpec((SEQ, LANES), lambda p: (0, c0 // LANES + p))
    vec = pl.BlockSpec((1, LANES), lambda p: (0, 0))
    out = pl.BlockSpec((SEQ, LANES), lambda p: (0, p))
    return _call(
        body, name="attn_fwd", grid=(ATTN_W // LANES,),
        in_specs=[col(C_AQ), col(C_AK), col(C_AV), col(C_AG), vec, vec],
        out_specs=[out, out, out],
        out_shape=[jax.ShapeDtypeStruct((SEQ, ATTN_W), F32), jax.ShapeDtypeStruct((SEQ, ATTN_W), F32),
                   jax.ShapeDtypeStruct((SEQ, ATTN_W), BF16)],
        scratch_shapes=[pltpu.VMEM((SEQ, LANES), F32), pltpu.VMEM((SEQ, LANES), F32)],
        compiler_params=_params(),
    )(proj, proj, proj, proj, gq2, gk2)


def _attn_bwd(proj, o, lse, dyc, gq2, gk2, *ride_along):
    tn = 512
    npairs = ATTN_W // LANES
    nride = len(ride_along)
    nbufs = nride * len(RS_KINDS)

    def body(proj_hbm, o_hbm, l_hbm, dyc_hbm, gq_ref, gk_ref, *rest):
        ride_in, rest = rest[:nride], rest[nride:]
        dq_ref, dk_ref, dv_ref, dgt_ref, gqg_ref, gkg_ref = rest[:6]
        ride_out, rest = rest[6:6 + nride], rest[6 + nride:]
        qb_, kb_, vb_, gb_, ob_, lb_, yb_, dkb_, dvb_, sems = rest[:10]
        rs_bufs, (send_sems, recv_sems, local_sems) = rest[10:10 + nbufs], rest[10 + nbufs:]
        rs_stage = _rs_stages(ride_in, ride_out, rs_bufs, send_sems, recv_sems, local_sems, [g.shape[1] for g in ride_along])
        pair = pl.program_id(0)
        for step in range(npairs):
            pl.when(pair == step)(rs_stage[step])
        bd = _head_blockdiag()
        lo = _lo_mask(CHUNK)
        lo2 = lax.broadcasted_iota(jnp.int32, (2 * CHUNK, LANES), 1) < HEAD_DIM
        valid1, valid2 = _band_masks()
        gqs = gq_ref[...] * QK_SCALE
        gk = gk_ref[...]

        def pcol(c0):
            return proj_hbm.at[:, pl.ds(pl.multiple_of(c0 + pair * LANES, LANES), LANES)]

        def acol(hbm, c0=0):
            return hbm.at[:, pl.ds(pl.multiple_of(c0 + pair * LANES, LANES), LANES)]

        loads = [pltpu.make_async_copy(src, dst, sems.at[n]) for n, (src, dst) in enumerate((
            (pcol(C_AQ), qb_), (pcol(C_AK), kb_), (pcol(C_AG), gb_), (acol(o_hbm), ob_),
            (acol(dyc_hbm, GMLP_W), yb_), (pcol(C_AV), vb_), (acol(l_hbm), lb_)))]
        for cp in loads:
            cp.start()

        @pl.when(pair == 0)
        def _():
            gqg_ref[...] = jnp.zeros_like(gqg_ref)
            gkg_ref[...] = jnp.zeros_like(gkg_ref)

        def pre_qk(t, carry):
            rows = pl.ds(pl.multiple_of(t * tn, tn), tn)
            zero = jnp.zeros((tn, LANES), F32)
            dkb_[rows, :] = zero
            dvb_[rows, :] = zero
            q = qb_[rows, :]
            qb_[rows, :] = q * lax.rsqrt(_headsum(q * q, bd) * (1.0 / HEAD_DIM) + EPS) * gqs
            k = kb_[rows, :]
            kb_[rows, :] = k * lax.rsqrt(_headsum(k * k, bd) * (1.0 / HEAD_DIM) + EPS) * gk
            return carry

        def pre_gate(t, carry):
            rows = pl.ds(pl.multiple_of(t * tn, tn), tn)
            g = gb_[rows, :]
            ov = ob_[rows, :]
            dya = yb_[rows, :]
            sg = _sigmoid(g)
            dgt_ref[rows, :] = (dya * ov * (sg * (1.0 + g * (1.0 - sg)))).astype(BF16)
            do = dya * (g * sg)
            yb_[rows, :] = do
            ob_[rows, :] = _headsum(do * ov, bd)
            gb_[rows, :] = jnp.zeros((tn, LANES), F32)
            return carry

        loads[0].wait()
        loads[1].wait()
        lax.fori_loop(0, SEQ // tn, pre_qk, 0)
        for cp in loads[2:5]:
            cp.wait()
        lax.fori_loop(0, SEQ // tn, pre_gate, 0)
        loads[5].wait()
        loads[6].wait()

        def load_kv(ref, d, start, prev):
            own = _rows_of(ref, start, d)[...]
            if prev is None:
                return own.astype(BF16)
            return jnp.concatenate([_rows_of(ref, prev, d)[...], own], axis=0).astype(BF16)

        def group(d, blocks):
            first = blocks[0][1] is None
            valid, lok = (valid1, lo) if first else (valid2, lo2)
            chains = [(b, h) for b in range(len(blocks)) for h in range(2)]
            mask = lambda h: lo if h == 0 else ~lo
            qs = [_rows_of(qb_, start, d)[...] for start, _ in blocks]
            dos = [_rows_of(yb_, start, d)[...] for start, _ in blocks]
            lvs = [_rows_of(lb_, start, d)[...] for start, _ in blocks]
            dls = [_rows_of(ob_, start, d)[...] for start, _ in blocks]
            ks = [load_kv(kb_, d, start, prev) for start, prev in blocks]
            vs = [load_kv(vb_, d, start, prev) for start, prev in blocks]
            qbs = [q.astype(BF16) for q in qs]
            dobs = [do.astype(BF16) for do in dos]
            ss = [_dot_nt(jnp.where(mask(h), qs[b], 0.0).astype(BF16), ks[b]) for b, h in chains]
            dps = [_dot_nt(jnp.where(mask(h), dos[b], 0.0).astype(BF16), vs[b]) for b, h in chains]
            pbs, dss = [], []
            for s, dp, (b, h) in zip(ss, dps, chains):
                hc = h * HEAD_DIM
                p = jnp.exp(jnp.where(valid, s, -jnp.inf) - lvs[b][:, hc:hc + 1])
                pbs.append(p.astype(BF16))
                dss.append((p * (dp - dls[b][:, hc:hc + 1])).astype(BF16))
            dqs = [_dot(ds, ks[b]) for ds, (b, h) in zip(dss, chains)]
            dks = [_dot_tn(ds, qbs[b]) for ds, (b, h) in zip(dss, chains)]
            dvs = [_dot_tn(p, dobs[b]) for p, (b, h) in zip(pbs, chains)]
            for b, (start, prev) in enumerate(blocks):
                c0, c1 = 2 * b, 2 * b + 1
                dq_rows = _rows_of(gb_, start, d)
                dq_rows[...] = dq_rows[...] + jnp.where(lo, dqs[c0], dqs[c1])
                dkc = jnp.where(lok, dks[c0], dks[c1])
                dvc = jnp.where(lok, dvs[c0], dvs[c1])
                spans = ((start, slice(0, CHUNK)),) if first else ((prev, slice(0, CHUNK)), (start, slice(CHUNK, 2 * CHUNK)))
                for st, sl in spans:
                    dk_rows = _rows_of(dkb_, st, d)
                    dk_rows[...] = dk_rows[...] + dkc[sl]
                    dv_rows = _rows_of(dvb_, st, d)
                    dv_rows[...] = dv_rows[...] + dvc[sl]

        for d in DILATIONS:
            _for_blocks(d, group, ATTN_UNROLL)

        reloads = [pltpu.make_async_copy(pcol(C_AQ), vb_, sems.at[0]), pltpu.make_async_copy(pcol(C_AK), lb_, sems.at[1])]
        for cp in reloads:
            cp.start()
        for cp in reloads:
            cp.wait()

        def post(t, carry):
            gq_acc, gk_acc = carry
            rows = pl.ds(pl.multiple_of(t * tn, tn), tn)
            outs = []
            for raw_, acc_, gain in ((vb_, gb_, gqs), (lb_, dkb_, gk)):
                a = raw_[rows, :]
                r = lax.rsqrt(_headsum(a * a, bd) * (1.0 / HEAD_DIM) + EPS)
                z = a * r
                dn = acc_[rows, :]
                dz = dn * gain
                outs.append((r * (dz - z * (_headsum(dz * z, bd) * (1.0 / HEAD_DIM))), jnp.sum(dn * z, axis=0, keepdims=True)))
            dq_ref[rows, :] = outs[0][0].astype(BF16)
            dk_ref[rows, :] = outs[1][0].astype(BF16)
            dv_ref[rows, :] = dvb_[rows, :].astype(BF16)
            return gq_acc + outs[0][1] * QK_SCALE, gk_acc + outs[1][1]

        zero = jnp.zeros((1, LANES), F32)
        gq_acc, gk_acc = lax.fori_loop(0, SEQ // tn, post, (zero, zero))
        gqg_ref[0:1, :] += gq_acc
        gkg_ref[0:1, :] += gk_acc

        @pl.when(pair == npairs - 1)
        def _():
            gqg_ref[0:1, :] = _fold_heads(gqg_ref[0:1, :])
            gkg_ref[0:1, :] = _fold_heads(gkg_ref[0:1, :])
            rs_stage[npairs]()

    hbm = pl.BlockSpec(memory_space=pl.ANY)
    vec = pl.BlockSpec((1, LANES), lambda p: (0, 0))
    blk8 = pl.BlockSpec((8, LANES), lambda p: (0, 0))
    out = pl.BlockSpec((SEQ, LANES), lambda p: (0, p))
    big = jax.ShapeDtypeStruct((SEQ, ATTN_W), BF16)
    nsem = RS_SEMS * nride
    return _call(
        body, name="attn_bwd", grid=(npairs,),
        in_specs=[hbm, hbm, hbm, hbm, vec, vec] + [hbm] * nride,
        out_specs=[out, out, out, out, blk8, blk8] + [hbm] * nride,
        out_shape=[big, big, big, big, jax.ShapeDtypeStruct((8, LANES), F32), jax.ShapeDtypeStruct((8, LANES), F32)]
        + [jax.ShapeDtypeStruct((2, g.shape[0] // 8, g.shape[1]), F32) for g in ride_along],
        scratch_shapes=[pltpu.VMEM((SEQ, LANES), F32) for _ in range(9)] + [pltpu.SemaphoreType.DMA((7,))]
        + _rs_scratch([g.shape for g in ride_along]) + [pltpu.SemaphoreType.DMA((nsem,)), pltpu.SemaphoreType.DMA((nsem,)),
                                     pltpu.SemaphoreType.DMA((nride,))],
        compiler_params=_params(),
    )(proj, o, lse, dyc, gq2, gk2, *[_rs_view(g) for g in ride_along])


def _mem_kv(mem, gain, wkv):
    def body(m_ref, g_ref, w_ref, kv_ref, hm_ref):
        mv = m_ref[...]
        ms = jnp.mean(mv * mv, axis=-1, keepdims=True)
        hm = (mv * lax.rsqrt(ms + EPS) * g_ref[...]).astype(BF16)
        hm_ref[...] = hm
        kv_ref[...] = _dot(hm, w_ref[...])

    return _call(
        body, name="mem_kv",
        out_shape=[jax.ShapeDtypeStruct((MEM_LEN, 2 * MEM_W), F32), jax.ShapeDtypeStruct((MEM_LEN, D_MODEL), BF16)],
        compiler_params=_params(),
    )(mem, gain, wkv)


def _mem_keys(kv_ref, kg_ref, bd, p):
    mk = kv_ref[:, p * LANES:(p + 1) * LANES]
    r = lax.rsqrt(_headsum(mk * mk, bd) * (1.0 / HEAD_DIM) + EPS)
    z = mk * r
    mkn = (z * kg_ref[:, p * LANES:(p + 1) * LANES]).astype(BF16)
    mvp = kv_ref[:, MEM_W + p * LANES:MEM_W + (p + 1) * LANES].astype(BF16)
    return mkn, mvp, r, z


def _mem_fwd(proj, kv, qg4, kg4):
    tm = 512

    def body(q_ref, g_ref, kv_ref, qg_ref, kg_ref, om_ref, ym_ref):
        bd = _head_blockdiag()
        lo = _lo_mask(tm)
        for p in range(2):
            cs = slice(p * LANES, (p + 1) * LANES)
            mkn, mvp, _, _ = _mem_keys(kv_ref, kg_ref, bd, p)
            q = q_ref[:, cs]
            qn = q * lax.rsqrt(_headsum(q * q, bd) * (1.0 / HEAD_DIM) + EPS) * (qg_ref[:, cs] * QK_SCALE)
            res = []
            for h in range(2):
                qh = jnp.where(lo if h == 0 else ~lo, qn, 0.0).astype(BF16)
                s = _dot_nt(qh, mkn)
                e = jnp.exp(s - jnp.max(s, axis=-1, keepdims=True))
                res.append(_dot(e.astype(BF16), mvp) * (1.0 / jnp.sum(e, axis=-1, keepdims=True)))
            ov = jnp.where(lo, res[0], res[1])
            g = g_ref[:, cs]
            om_ref[:, cs] = ov
            ym_ref[:, cs] = (ov * (g * _sigmoid(g))).astype(BF16)

    vec = pl.BlockSpec((1, MEM_W), lambda i: (0, 0))
    return _call(
        body, name="mem_fwd", grid=(SEQ // tm,),
        in_specs=[pl.BlockSpec((tm, MEM_W), lambda i: (i, C_MQ // MEM_W)),
                  pl.BlockSpec((tm, MEM_W), lambda i: (i, C_MG // MEM_W)),
                  pl.BlockSpec((MEM_LEN, 2 * MEM_W), lambda i: (0, 0)), vec, vec],
        out_specs=[pl.BlockSpec((tm, MEM_W), lambda i: (i, 0)), pl.BlockSpec((tm, MEM_W), lambda i: (i, 0))],
        out_shape=[jax.ShapeDtypeStruct((SEQ, MEM_W), F32), jax.ShapeDtypeStruct((SEQ, MEM_W), BF16)],
        compiler_params=_params(),
    )(proj, proj, kv, qg4, kg4)


def _mem_bwd(proj, om, dyc, kv, hm, mem, mgain, wkv, qg4, kg4):
    tm = 512
    nsteps = SEQ // tm

    def body(q_ref, g_ref, om_ref, dy_ref, kv_ref, hm_ref, mem_ref, mg_ref, w_ref, qg_ref, kg_ref,
             dq_ref, dgt_ref, gqg_ref, gkg_ref, gw_ref, gmg_ref, dmk_ref, dmv_ref, gq_acc):
        i = pl.program_id(0)
        bd = _head_blockdiag()
        lo = _lo_mask(tm)
        lom = _lo_mask(MEM_LEN)

        @pl.when(i == 0)
        def _():
            dmk_ref[...] = jnp.zeros_like(dmk_ref)
            dmv_ref[...] = jnp.zeros_like(dmv_ref)
            gq_acc[...] = jnp.zeros_like(gq_acc)

        for p in range(2):
            cs = slice(p * LANES, (p + 1) * LANES)
            mkn, mvp, _, _ = _mem_keys(kv_ref, kg_ref, bd, p)
            gqs = qg_ref[:, cs] * QK_SCALE
            q = q_ref[:, cs]
            r = lax.rsqrt(_headsum(q * q, bd) * (1.0 / HEAD_DIM) + EPS)
            z = q * r
            qn = z * gqs
            qnb = qn.astype(BF16)
            g = g_ref[:, cs]
            ov = om_ref[:, cs]
            dym = dy_ref[:, cs]
            sg = _sigmoid(g)
            dgt_ref[:, cs] = (dym * ov * (sg * (1.0 + g * (1.0 - sg)))).astype(BF16)
            do = dym * (g * sg)
            dob = do.astype(BF16)
            delta = _headsum(do * ov, bd)
            parts = []
            for h in range(2):
                mh = lo if h == 0 else ~lo
                hc = h * HEAD_DIM
                qh = jnp.where(mh, qn, 0.0).astype(BF16)
                doh = jnp.where(mh, do, 0.0).astype(BF16)
                s = _dot_nt(qh, mkn)
                e = jnp.exp(s - jnp.max(s, axis=-1, keepdims=True))
                pr = e * (1.0 / jnp.sum(e, axis=-1, keepdims=True))
                dp = _dot_nt(doh, mvp)
                ds = (pr * (dp - delta[:, hc:hc + 1])).astype(BF16)
                parts.append((_dot(ds, mkn), _dot_tn(ds, qnb), _dot_tn(pr.astype(BF16), dob)))
            dqn = jnp.where(lo, parts[0][0], parts[1][0])
            dmk_ref[:, cs] += jnp.where(lom, parts[0][1], parts[1][1])
            dmv_ref[:, cs] += jnp.where(lom, parts[0][2], parts[1][2])
            dz = dqn * gqs
            dq_ref[:, cs] = (r * (dz - z * (_headsum(dz * z, bd) * (1.0 / HEAD_DIM)))).astype(BF16)
            gq_acc[:, cs] += jnp.sum(dqn * z, axis=0, keepdims=True) * QK_SCALE

        @pl.when(i == nsteps - 1)
        def _():
            gqg_ref[...] = jnp.zeros_like(gqg_ref)
            gkg_ref[...] = jnp.zeros_like(gkg_ref)
            gqg_ref[0:1, :] = _fold_heads(gq_acc[:, 0:LANES] + gq_acc[:, LANES:2 * LANES])
            dkv = []
            gk = jnp.zeros((1, LANES), F32)
            for p in range(2):
                cs = slice(p * LANES, (p + 1) * LANES)
                _, _, r, z = _mem_keys(kv_ref, kg_ref, bd, p)
                dn = dmk_ref[:, cs]
                dz = dn * kg_ref[:, cs]
                gk = gk + jnp.sum(dn * z, axis=0, keepdims=True)
                dkv.append(r * (dz - z * (_headsum(dz * z, bd) * (1.0 / HEAD_DIM))))
            gkg_ref[0:1, :] = _fold_heads(gk)
            dkvb = jnp.concatenate(dkv + [dmv_ref[...]], axis=1).astype(BF16)
            gw_ref[...] = _dot_tn(hm_ref[...], dkvb)
            dhm = _dot_nt(dkvb, w_ref[...])
            mv = mem_ref[...]
            zm = mv * lax.rsqrt(jnp.mean(mv * mv, axis=-1, keepdims=True) + EPS)
            _put_rows(gmg_ref, jnp.sum(dhm * zm, axis=0, keepdims=True))

    const = lambda shape: pl.BlockSpec(shape, lambda i: (0,) * len(shape))
    row = lambda j: pl.BlockSpec((tm, MEM_W), lambda i: (i, j))
    blk8 = jax.ShapeDtypeStruct((8, LANES), F32)
    return _call(
        body, name="mem_bwd", grid=(nsteps,),
        in_specs=[row(C_MQ // MEM_W), row(C_MG // MEM_W), row(0), row((GMLP_W + ATTN_W) // MEM_W),
                  const((MEM_LEN, 2 * MEM_W)), const((MEM_LEN, D_MODEL)), const((MEM_LEN, D_MODEL)),
                  const((1, D_MODEL)), const((D_MODEL, 2 * MEM_W)), const((1, MEM_W)), const((1, MEM_W))],
        out_specs=[row(0), row(0), const((8, LANES)), const((8, LANES)),
                   const((D_MODEL, 2 * MEM_W)), const((8, LANES))],
        out_shape=[jax.ShapeDtypeStruct((SEQ, MEM_W), BF16), jax.ShapeDtypeStruct((SEQ, MEM_W), BF16),
                   blk8, blk8, jax.ShapeDtypeStruct((D_MODEL, 2 * MEM_W), F32), blk8],
        scratch_shapes=[pltpu.VMEM((MEM_LEN, MEM_W), F32), pltpu.VMEM((MEM_LEN, MEM_W), F32),
                        pltpu.VMEM((1, MEM_W), F32)],
        compiler_params=_params(),
    )(proj, proj, om, dyc, kv, hm, mem, mgain, wkv, qg4, kg4)


def _out_loss(yg, ya, ym, x, tgt, wo):
    tm = 512
    nsteps = SEQ // tm
    parts = ((0, GMLP_W), (GMLP_W, ATTN_W), (GMLP_W + ATTN_W, MEM_W))

    def body(yg_ref, ya_ref, ym_ref, x_ref, t_ref, w_ref, dy_ref, dyc_ref, gw_ref, ls_ref):
        i = pl.program_id(0)

        @pl.when(i == 0)
        def _():
            gw_ref[...] = jnp.zeros_like(gw_ref)
            ls_ref[...] = jnp.zeros_like(ls_ref)

        ys = (yg_ref[...], ya_ref[...], ym_ref[...])
        y = sum(_dot(yv, w_ref[r0:r0 + n, :]) for yv, (r0, n) in zip(ys, parts))
        err = x_ref[...] + y - t_ref[...]
        _put_rows(ls_ref, jnp.sum(err * err, axis=0, keepdims=True), accumulate=True)
        dy = err * (1.0 / D_MODEL)
        dy_ref[...] = dy
        dyb = dy.astype(BF16)
        dyc_ref[...] = _dot_nt(dyb, w_ref[...])
        for yv, (r0, n) in zip(ys, parts):
            gw_ref[r0:r0 + n, :] += _dot_tn(yv, dyb)

    row = lambda w: pl.BlockSpec((tm, w), lambda i: (i, 0))
    const = lambda shape: pl.BlockSpec(shape, lambda i: (0, 0))
    return _call(
        body, name="out_loss", grid=(nsteps,),
        in_specs=[row(GMLP_W), row(ATTN_W), row(MEM_W), row(D_MODEL), row(D_MODEL), const((D_MODEL, D_MODEL))],
        out_specs=[row(D_MODEL), row(D_MODEL), const((D_MODEL, D_MODEL)), const((8, LANES))],
        out_shape=[jax.ShapeDtypeStruct((SEQ, D_MODEL), F32), jax.ShapeDtypeStruct((SEQ, D_MODEL), F32),
                   jax.ShapeDtypeStruct((D_MODEL, D_MODEL), F32), jax.ShapeDtypeStruct((8, LANES), F32)],
        compiler_params=_params(),
    )(yg, ya, ym, x, tgt, wo)


def _proj_bwd(x, dy, gain, wt, dg, daq, dak, dav, dag, dmq, dmg):
    tm = 512
    nsteps = SEQ // tm
    pieces = ((C_GU, 3 * GMLP_W), (C_AQ, ATTN_W), (C_AK, ATTN_W), (C_AV, ATTN_W), (C_AG, ATTN_W),
              (C_MQ, MEM_W), (C_MG, MEM_W))

    def body(x_ref, dy_ref, g_ref, wt_hbm, p0, p1, p2, p3, p4, p5, p6, gx_ref, gwt_hbm, gg_ref, wt_v, acc, wt_sem):
        i = pl.program_id(0)
        wt_load = pltpu.make_async_copy(wt_hbm, wt_v, wt_sem)

        @pl.when(i == 0)
        def _():
            wt_load.start()
            acc[...] = jnp.zeros_like(acc)
            gg_ref[...] = jnp.zeros_like(gg_ref)

        xv = x_ref[...]
        r = lax.rsqrt(jnp.mean(xv * xv, axis=-1, keepdims=True) + EPS)
        z = xv * r
        g = g_ref[...]
        h = (z * g).astype(BF16)
        pl.when(i == 0)(wt_load.wait)
        dh = jnp.zeros((tm, D_MODEL), F32)
        for pref, (c0, w) in zip((p0, p1, p2, p3, p4, p5, p6), pieces):
            dp = pref[...]
            dh = dh + _dot(dp, wt_v[c0:c0 + w, :])
            acc[c0:c0 + w, :] += _dot_tn(dp, h)
        _put_rows(gg_ref, jnp.sum(dh * z, axis=0, keepdims=True), accumulate=True)
        dz = dh * g
        gx_ref[...] = dy_ref[...] + r * (dz - z * jnp.mean(dz * z, axis=-1, keepdims=True))

        @pl.when(i == nsteps - 1)
        def _():
            pltpu.sync_copy(acc, gwt_hbm)

    row = lambda w: pl.BlockSpec((tm, w), lambda i: (i, 0))
    hbm = pl.BlockSpec(memory_space=pl.ANY)
    vec = pl.BlockSpec((1, D_MODEL), lambda i: (0, 0))
    return _call(
        body, name="proj_bwd", grid=(nsteps,),
        in_specs=[row(D_MODEL), row(D_MODEL), vec, hbm] + [row(w) for _, w in pieces],
        out_specs=[row(D_MODEL), hbm, pl.BlockSpec((8, LANES), lambda i: (0, 0))],
        out_shape=[jax.ShapeDtypeStruct((SEQ, D_MODEL), F32), jax.ShapeDtypeStruct((IN_W, D_MODEL), F32),
                   jax.ShapeDtypeStruct((8, LANES), F32)],
        scratch_shapes=[pltpu.VMEM((IN_W, D_MODEL), BF16), pltpu.VMEM((IN_W, D_MODEL), F32), pltpu.SemaphoreType.DMA],
        compiler_params=_params(),
    )(x, dy, gain, wt, dg, daq, dak, dav, dag, dmq, dmg)


AG_SEMS = 8


def _gather_stages(ins, lands, send_sems, recv_sems):
    n = len(ins)
    nrows = [a.shape[0] for a in ins]
    x, y, c = lax.axis_index("x"), lax.axis_index("y"), lax.axis_index("c")
    sib, xn, yn = (x, y, 1 - c), (1 - x, y, c), (x, 1 - y, c)
    me, cx, cy, cd = 2 * x + y, 2 * (1 - x) + y, 2 * x + (1 - y), 2 * (1 - x) + (1 - y)

    def part(a, chip, hf, quarter=None):
        rows = nrows[a] // 2
        base = chip * nrows[a] + hf * rows
        if quarter is not None:
            rows = rows // 2
            base = base + quarter * rows
        return lands[a].at[pl.ds(pl.multiple_of(base, 16), rows), :]

    def copy(a, j, ref, to):
        k = AG_SEMS * a + j
        return pltpu.make_async_remote_copy(src_ref=ref, dst_ref=ref, send_sem=send_sems.at[k],
                                            recv_sem=recv_sems.at[k], device_id=to, device_id_type=MESH)

    def own(a):
        return [copy(a, 0, part(a, me, c), xn), copy(a, 1, part(a, me, c), yn)]

    def neighbours(a):
        return [copy(a, 4, part(a, cx, c, 1), yn), copy(a, 2, part(a, cx, c), sib),
                copy(a, 5, part(a, cy, c, 0), xn), copy(a, 3, part(a, cy, c), sib)]

    def diagonal(a):
        return [copy(a, 7, part(a, cd, c, 1), sib), copy(a, 6, part(a, cd, c, 0), sib)]

    def send_own():
        for a in range(n):
            lands[a][pl.ds(pl.multiple_of(me * nrows[a], 16), nrows[a]), :] = ins[a][...].astype(BF16)
            for cp in own(a):
                cp.start()

    def pass_on_neighbours():
        for a in range(n):
            copy(a, 0, part(a, cx, c), xn).wait_recv()
            copy(a, 1, part(a, cy, c), yn).wait_recv()
            for cp in neighbours(a):
                cp.start()

    def pass_on_diagonal():
        for a in range(n):
            copy(a, 4, part(a, cd, c, 1), yn).wait_recv()
            copy(a, 5, part(a, cd, c, 0), xn).wait_recv()
            for cp in diagonal(a):
                cp.start()

    def y_complete():
        for a in range(n):
            copy(a, 3, part(a, cy, 1 - c), sib).wait_recv()

    def x_complete():
        for a in range(n):
            copy(a, 2, part(a, cx, 1 - c), sib).wait_recv()

    def diagonal_complete():
        for a in range(n):
            copy(a, 6, part(a, cd, 1 - c, 0), sib).wait_recv()
            copy(a, 7, part(a, cd, 1 - c, 1), sib).wait_recv()

    def sends_done():
        for a in range(n):
            for cp in own(a) + neighbours(a) + diagonal(a):
                cp.wait_send()

    def finish():
        y_complete()
        x_complete()
        diagonal_complete()
        sends_done()

    return (send_own, pass_on_neighbours, pass_on_diagonal, finish), (y_complete, x_complete, diagonal_complete, sends_done)


RS_SEMS = 6
RS_KINDS = (((2, 2), 1, F32), ((2, 2), 1, F32), ((2, 2), 2, BF16), ((2, 2), 2, BF16), ((2, 2), 2, F32),
            ((2,), 2, BF16), ((2,), 2, BF16), ((2,), 1, F32))


def _rs_view(g):
    return g.reshape(2, 2, 2, g.shape[0] // 8, g.shape[1])


def _rs_scratch(shapes, in_vmem=False):
    kinds = RS_KINDS[1:] if in_vmem else RS_KINDS
    return [pltpu.VMEM(lead + (r // 8, w // split), dt) for lead, split, dt in kinds for r, w in shapes]


def _rs_stages(gs, outs, bufs, send_sems, recv_sems, local_sems, widths):
    n = len(gs)
    if len(bufs) < n * len(RS_KINDS):
        bufs = [None] * n + list(bufs)
    loc, ra, s_b, r_b, acc1, s_c, r_c, fin = (bufs[n * i:n * i + n] for i in range(len(RS_KINDS)))
    half_w = [w // 2 for w in widths]
    chips = [(xx, yy) for xx in range(2) for yy in range(2)]
    x, y, c = lax.axis_index("x"), lax.axis_index("y"), lax.axis_index("c")
    sib, xn, yn = (x, y, 1 - c), (1 - x, y, c), (x, 1 - y, c)

    def copy(a, j, src, dst, to):
        k = RS_SEMS * a + j
        return pltpu.make_async_remote_copy(src_ref=src, dst_ref=dst, send_sem=send_sems.at[k],
                                            recv_sem=recv_sems.at[k], device_id=to, device_id_type=MESH)

    def step_a(a):
        if callable(gs[a]):
            return [copy(a, 0, gs[a](xx, yy, 1 - c), ra[a].at[xx, yy], sib) for xx, yy in chips]
        return [copy(a, 0, gs[a].at[:, :, 1 - c], ra[a], sib),
                pltpu.make_async_copy(gs[a].at[:, :, c], loc[a], local_sems.at[a])]

    def finish_a(a):
        if callable(gs[a]):
            copy(a, 0, ra[a], ra[a], sib).wait()
            for xx, yy in chips:
                ra[a][xx, yy] = gs[a](xx, yy, c)[...] + ra[a][xx, yy]
        else:
            for cp in step_a(a):
                cp.wait()
            ra[a][...] = loc[a][...] + ra[a][...]

    def step_b(a):
        return copy(a, 1, s_b[a].at[0], r_b[a].at[0], xn), copy(a, 2, s_b[a].at[1], r_b[a].at[1], yn)

    def step_c(a):
        return copy(a, 3, s_c[a].at[0], r_c[a].at[0], yn), copy(a, 4, s_c[a].at[1], r_c[a].at[1], xn)

    def step_d(a, half):
        rows = fin[a].at[half]
        return copy(a, 5, rows, rows, sib)

    def start():
        for a in range(n):
            for cp in step_a(a):
                cp.start()

    def a_to_b():
        for a in range(n):
            finish_a(a)
            s_b[a][0] = ra[a][1 - x, :, :, :half_w[a]].astype(BF16)
            s_b[a][1] = ra[a][:, 1 - y, :, half_w[a]:].astype(BF16)
            for cp in step_b(a):
                cp.start()

    def b_to_c():
        for a in range(n):
            for cp in step_b(a):
                cp.wait()
            acc1[a][0] = ra[a][x, :, :, :half_w[a]] + r_b[a][0].astype(F32)
            acc1[a][1] = ra[a][:, y, :, half_w[a]:] + r_b[a][1].astype(F32)
            s_c[a][0] = acc1[a][0, 1 - y].astype(BF16)
            s_c[a][1] = acc1[a][1, 1 - x].astype(BF16)
            for cp in step_c(a):
                cp.start()

    def c_to_d():
        for a in range(n):
            for cp in step_c(a):
                cp.wait()
            fin[a][c, :, :half_w[a]] = acc1[a][0, y] + r_c[a][0].astype(F32)
            fin[a][c, :, half_w[a]:] = acc1[a][1, x] + r_c[a][1].astype(F32)
            step_d(a, c).start()

    def finish():
        for a in range(n):
            step_d(a, 1 - c).wait_recv()
            step_d(a, c).wait_send()
            pltpu.sync_copy(fin[a], outs[a])

    return start, a_to_b, b_to_c, c_to_d, finish


def _reduce_grads(gwt, g_ws, tiny):
    cw = gwt.shape[1] // RS_CHUNKS
    chunk_shape = (gwt.shape[0], cw)

    def body(g0, ws_in, tiny_in, *rest):
        outs, o_ws, o_tiny = rest[:RS_CHUNKS], rest[RS_CHUNKS], rest[RS_CHUNKS + 1]
        rest = rest[RS_CHUNKS + 2:]
        nb = len(RS_KINDS) * RS_CHUNKS
        sm, sa, sb, sc, acc_s, send_sems, recv_sems, local_sems = rest[nb:]
        blocks = [g0.at[:, :, :, :, pl.ds(j * cw, cw)] for j in range(RS_CHUNKS)]
        start, a_to_b, b_to_c, c_to_d, finish = _rs_stages(blocks, outs, rest[:nb], send_sems, recv_sems, local_sems,
                                                           [cw] * RS_CHUNKS)
        n_ws = ws_in.shape[0]
        sm[0:n_ws, :] = ws_in[...]
        sm[n_ws:, :] = tiny_in[...]
        x, y, c = lax.axis_index("x"), lax.axis_index("y"), lax.axis_index("c")

        def small(j, src, dst, to):
            k = RS_SEMS * RS_CHUNKS + j
            return pltpu.make_async_remote_copy(src_ref=src, dst_ref=dst, send_sem=send_sems.at[k],
                                                recv_sem=recv_sems.at[k], device_id=to, device_id_type=MESH)

        along_c, along_x, along_y = (small(0, sm, sa, (x, y, 1 - c)), small(1, acc_s, sb, (1 - x, y, c)),
                                     small(2, sb, sc, (x, 1 - y, c)))
        start()
        along_c.start()
        a_to_b()
        along_c.wait()
        acc_s[...] = sm[...] + sa[...]
        along_x.start()
        b_to_c()
        along_x.wait()
        sb[...] = acc_s[...] + sb[...]
        along_y.start()
        c_to_d()
        along_y.wait()
        o_ws[...] = sb[0:n_ws, :] + sc[0:n_ws, :]
        o_tiny[...] = sb[n_ws:, :] + sc[n_ws:, :]
        finish()

    vm = pl.BlockSpec(memory_space=pltpu.VMEM)
    hbm = pl.BlockSpec(memory_space=pl.ANY)
    small_shape = (g_ws.shape[0] + tiny.shape[0], LANES)
    scratch = _rs_scratch([chunk_shape] * RS_CHUNKS) + [pltpu.VMEM(small_shape, F32) for _ in range(5)]
    nsem = RS_SEMS * RS_CHUNKS + 3
    scratch += [pltpu.SemaphoreType.DMA((nsem,)), pltpu.SemaphoreType.DMA((nsem,)), pltpu.SemaphoreType.DMA((RS_CHUNKS,))]
    return _call(
        body, name="reduce_grads",
        out_shape=[jax.ShapeDtypeStruct((2, gwt.shape[0] // 8, cw), F32)] * RS_CHUNKS
        + [jax.ShapeDtypeStruct(g_ws.shape, F32), jax.ShapeDtypeStruct(tiny.shape, F32)],
        in_specs=[hbm, vm, vm],
        out_specs=[hbm] * RS_CHUNKS + [vm, vm],
        scratch_shapes=scratch,
        compiler_params=_params(),
    )(_rs_view(gwt), g_ws, tiny)


def _adam_update(w, g, m, v):
    nm = ADAM_B1 * m + (1.0 - ADAM_B1) * g
    nv = ADAM_B2 * v + (1.0 - ADAM_B2) * (g * g)
    m_hat = nm / (1.0 - ADAM_B1 ** ADAM_STEP)
    v_hat = nv / (1.0 - ADAM_B2 ** ADAM_STEP)
    return -ADAM_LR * (m_hat / (jnp.sqrt(v_hat) + ADAM_EPS) + ADAM_WD * w), nm, nv


def _adamw(w, g, m, v):
    rows, cols = w.shape
    tm = max(t for t in range(8, 257, 8) if rows % t == 0)
    parts = tuple(g) if isinstance(g, (tuple, list)) else (g,)
    n = len(parts)

    def body(w_ref, m_ref, v_ref, *refs):
        gv = jnp.concatenate([r[...] for r in refs[:n]], axis=1)
        d_ref, nm_ref, nv_ref = refs[n:n + 3]
        d_ref[...], nm_ref[...], nv_ref[...] = _adam_update(w_ref[...], gv, m_ref[...], v_ref[...])
        if n > 1:
            refs[n + 3][...] = gv

    blk = pl.BlockSpec((tm, cols), lambda i: (i, 0))
    nout = 3 if n == 1 else 4
    res = _call(
        body, name="adamw", grid=(rows // tm,),
        in_specs=[blk] * 3 + [pl.BlockSpec((tm, p.shape[1]), lambda i: (i, 0)) for p in parts], out_specs=[blk] * nout,
        out_shape=[jax.ShapeDtypeStruct((rows, cols), F32)] * nout,
        compiler_params=_params(),
    )(w, m, v, *parts)
    return (parts[0] if n == 1 else res[3], *res[:3])


def _adamw_tiny(tiny, weights, ms, vs):
    shapes = [w.shape for w in weights]
    n = len(weights)

    def grad_of(t_ref, k, shape):
        base = 8 * k
        if shape[1] > LANES:
            return [t_ref[base + j:base + j + 1, :] for j in range(shape[1] // LANES)]
        return [t_ref[base:base + shape[0], 0:shape[1]]]

    def body(t_ref, *refs):
        w_refs, m_refs, v_refs = refs[:n], refs[n:2 * n], refs[2 * n:3 * n]
        loss_ref, outs = refs[3 * n], refs[3 * n + 1:]
        loss_ref[...] = (0.5 / D_MODEL) * jnp.sum(t_ref[8 * n:8 * n + 8, :], keepdims=True)
        for k, shape in enumerate(shapes):
            g_ref, d_ref, nm_ref, nv_ref = outs[4 * k:4 * k + 4]
            for j, g in enumerate(grad_of(t_ref, k, shape)):
                cols = slice(j * LANES, (j + 1) * LANES) if shape[1] > LANES else slice(None)
                g_ref[:, cols] = g
                d_ref[:, cols], nm_ref[:, cols], nv_ref[:, cols] = _adam_update(
                    w_refs[k][:, cols], g, m_refs[k][:, cols], v_refs[k][:, cols])

    out_shape = [jax.ShapeDtypeStruct((1, 1), F32)]
    for shape in shapes:
        out_shape += [jax.ShapeDtypeStruct(shape, F32)] * 4
    return _call(body, name="adamw_tiny", out_shape=out_shape, compiler_params=_params())(tiny, *weights, *ms, *vs)


def _local_grads(x, mem, tgt, norm_gain, wt_sh, gmlp_v_gain, gmlp_w_s, gmlp_b, attn_q_gain, attn_k_gain,
                 mem_norm_gain, wkv_sh, mem_q_gain, mem_k_gain, wo_sh):
    vg = gmlp_v_gain.reshape(1, GMLP_W)
    bias_full = jnp.repeat(gmlp_b.T, HEAD_DIM, axis=1)
    gq2, gk2 = jnp.tile(attn_q_gain, (1, 2)), jnp.tile(attn_k_gain, (1, 2))
    qg4, kg4 = jnp.tile(mem_q_gain, (1, 4)), jnp.tile(mem_k_gain, (1, 4))

    proj, wt, wkv, wo = _gather_proj(x, norm_gain, wt_sh, wkv_sh, wo_sh)
    yg = _gmlp_fwd(proj, vg, gmlp_w_s, bias_full)
    o, lse, ya = _attn_fwd(proj, gq2, gk2)
    kv, hm = _mem_kv(mem, mem_norm_gain, wkv)
    om, ym = _mem_fwd(proj, kv, qg4, kg4)
    dy, dyc, g_wo, err2 = _out_loss(yg, ya, ym, x, tgt, wo)
    dmq, dmg, g_mq, g_mk, g_wkv, g_mng = _mem_bwd(proj, om, dyc, kv, hm, mem, mem_norm_gain, wkv, qg4, kg4)
    daq, dak, dav, dag, g_aq, g_ak, g_wkv_sh, g_wo_sh = _attn_bwd(proj, o, lse, dyc, gq2, gk2, g_wkv, g_wo)
    dg, g_ws, g_b, g_vg = _gmlp_bwd(proj, dyc, vg, gmlp_w_s, bias_full)
    gx, g_wt, g_ng = _proj_bwd(x, dy, norm_gain, wt, dg, daq, dak, dav, dag, dmq, dmg)

    tiny = jnp.concatenate([g_ng, g_vg, g_b, g_aq, g_ak, g_mng, g_mq, g_mk, err2], axis=0)
    return gx, g_wt, g_wkv_sh, g_wo_sh, g_ws.reshape(4 * CHUNK, CHUNK), tiny


def kernel(x, mem, norm_gain, w_in, gmlp_v_gain, gmlp_w_s, gmlp_b, attn_q_gain, attn_k_gain, mem_norm_gain, w_mem_kv, mem_q_gain, mem_k_gain, w_out, loss_target, m_norm_gain, m_w_in, m_gmlp_v_gain, m_gmlp_w_s, m_gmlp_b, m_attn_q_gain, m_attn_k_gain, m_mem_norm_gain, m_w_mem_kv, m_mem_q_gain, m_mem_k_gain, m_w_out, v_norm_gain, v_w_in, v_gmlp_v_gain, v_gmlp_w_s, v_gmlp_b, v_attn_q_gain, v_attn_k_gain, v_mem_norm_gain, v_w_mem_kv, v_mem_q_gain, v_mem_k_gain, v_w_out):
    gx, g_wt, g_wkv_sh, g_wo_sh, g_ws, tiny = _local_grads(
        x[0], mem[0], loss_target[0], norm_gain, w_in[0].T, gmlp_v_gain[0], gmlp_w_s[0], gmlp_b[0],
        attn_q_gain, attn_k_gain, mem_norm_gain, w_mem_kv[0], mem_q_gain, mem_k_gain, w_out[0])
    *g_wt_sh, g_ws, tiny = _reduce_grads(g_wt, g_ws, tiny)
    chip_block = lambda g: g.reshape(2 * g.shape[1], g.shape[2])
    g_wt_sh = tuple(chip_block(g) for g in g_wt_sh)
    g_wkv_sh, g_wo_sh = chip_block(g_wkv_sh), chip_block(g_wo_sh)

    ws = (norm_gain, w_in, gmlp_v_gain, gmlp_w_s, gmlp_b, attn_q_gain, attn_k_gain, mem_norm_gain, w_mem_kv,
          mem_q_gain, mem_k_gain, w_out)
    ms = (m_norm_gain, m_w_in, m_gmlp_v_gain, m_gmlp_w_s, m_gmlp_b, m_attn_q_gain, m_attn_k_gain, m_mem_norm_gain,
          m_w_mem_kv, m_mem_q_gain, m_mem_k_gain, m_w_out)
    vs = (v_norm_gain, v_w_in, v_gmlp_v_gain, v_gmlp_w_s, v_gmlp_b, v_attn_q_gain, v_attn_k_gain, v_mem_norm_gain,
          v_w_mem_kv, v_mem_q_gain, v_mem_k_gain, v_w_out)
    form = {1: lambda a: a[0].T, 3: lambda a: a.reshape(4 * CHUNK, CHUNK), 2: lambda a: a[0], 4: lambda a: a[0],
            8: lambda a: a[0], 11: lambda a: a[0]}
    back = {1: lambda a: a.T[None], 3: lambda a: a.reshape(1, 4, CHUNK, CHUNK), 2: lambda a: a[None],
            4: lambda a: a[None], 8: lambda a: a[None], 11: lambda a: a[None]}
    fwd = lambda t, i: form.get(i, lambda a: a)(t[i])
    out = {}
    for i, g in ((1, g_wt_sh), (3, g_ws), (8, g_wkv_sh), (11, g_wo_sh)):
        out[i] = _adamw(fwd(ws, i), g, fwd(ms, i), fwd(vs, i))
    res = _adamw_tiny(tiny, [fwd(ws, i) for i in TINY_ORDER], [fwd(ms, i) for i in TINY_ORDER],
                      [fwd(vs, i) for i in TINY_ORDER])
    for k, i in enumerate(TINY_ORDER):
        out[i] = res[1 + 4 * k:5 + 4 * k]
    leaves = [[back.get(i, lambda a: a)(out[i][j]) for i in range(12)] for j in range(4)]
    return (res[0].reshape(()), gx[None], *leaves[0], *leaves[1], *leaves[2], *leaves[3])
```

```python
import functools
import math

import jax
import jax.numpy as jnp
from jax import lax
from jax.experimental import pallas as pl
from jax.experimental.pallas import tpu as pltpu

F32 = jnp.float32
BF16 = jnp.bfloat16

SEQ = 4096
D_MODEL = 1024
HEAD_DIM = 64
LANES = 128
CHUNK = 128
GMLP_W, ATTN_W, MEM_W = 256, 512, 256
IN_W = 3 * GMLP_W + 4 * ATTN_W + 2 * MEM_W
MEM_LEN = 256
DILATIONS = (1, 4, 16)
EPS = 1e-6
QK_SCALE = 1.0 / math.sqrt(HEAD_DIM)
C_GU, C_GV, C_GG, C_AQ, C_AK, C_AV, C_AG, C_MQ, C_MG = 0, 256, 512, 768, 1280, 1792, 2304, 2816, 3072

ADAM_LR, ADAM_B1, ADAM_B2, ADAM_EPS, ADAM_WD, ADAM_STEP = 0.001, 0.9, 0.999, 1e-08, 0.01, 10

VMEM_LIMIT = 48 * 1024 * 1024
RS_CHUNKS = 4
ATTN_UNROLL = 4
MESH = pl.DeviceIdType.MESH

TINY_ORDER = (0, 2, 4, 5, 6, 7, 9, 10)


def _call(body, **kw):
    return pl.pallas_call(body, **kw)


def _params(**kw):
    return pltpu.CompilerParams(vmem_limit_bytes=VMEM_LIMIT, **kw)


def _dot(a, b):
    return jnp.dot(a, b, preferred_element_type=F32)


def _dot_nt(a, b):
    return lax.dot_general(a, b, (((1,), (1,)), ((), ())), preferred_element_type=F32)


def _dot_tn(a, b):
    return lax.dot_general(a, b, (((0,), (0,)), ((), ())), preferred_element_type=F32)


def _head_blockdiag():
    r = lax.shift_right_logical(lax.broadcasted_iota(jnp.int32, (LANES, LANES), 0), 6)
    c = lax.shift_right_logical(lax.broadcasted_iota(jnp.int32, (LANES, LANES), 1), 6)
    return jnp.where(r == c, 1.0, 0.0).astype(BF16)


def _headsum(v, bd):
    hi = v.astype(BF16)
    lo = (v - hi.astype(F32)).astype(BF16)
    return _dot(hi, bd) + _dot(lo, bd)


def _lo_mask(rows):
    return lax.broadcasted_iota(jnp.int32, (rows, LANES), 1) < HEAD_DIM


def _sigmoid(x):
    return 1.0 / (1.0 + jnp.exp(-x))


def _fold_heads(v):
    return v + pltpu.roll(v, HEAD_DIM, 1)


def _put_rows(ref, vec, accumulate=False):
    for j in range(vec.shape[1] // LANES):
        piece = vec[:, j * LANES:(j + 1) * LANES]
        ref[j:j + 1, :] = ref[j:j + 1, :] + piece if accumulate else piece


def _gather_proj(x, gain, wt_sh, *ride_along):
    tm = 512
    nrow = SEQ // tm
    nride = len(ride_along)
    widths = (768, 896, 768, 896)
    pair = 2 * wt_sh.shape[0]
    assert pair % LANES == 0 and sum(widths[:2]) == pair

    def body(x_ref, g_ref, wt_sh_ref, *rest):
        shards, rest = rest[:nride], rest[nride:]
        proj_hbm, wt_hbm, gathered = rest[0], rest[1], rest[2:2 + nride]
        h_scr, land, res = rest[2 + nride:5 + nride]
        lands, (send0, recv0, send1, recv1, out_sems, copy_sems) = rest[5 + nride:5 + 2 * nride], rest[5 + 2 * nride:]
        u, i = pl.program_id(0), pl.program_id(1)
        cx_, cy_ = lax.axis_index("x"), lax.axis_index("y")
        (send_own, pass_on_neighbours, pass_on_diagonal, _), (y_complete, x_complete, diagonal_complete, sends_done) = (
            _gather_stages((wt_sh_ref,), (land,), send0, recv0))
        ride, _ = _gather_stages(shards, lands, send1, recv1)
        first = lambda k: (u == k) & (i == 0)
        last = (u == 3) & (i == nrow - 1)
        copies = [pltpu.make_async_copy(land, wt_hbm, copy_sems.at[0])] + [
            pltpu.make_async_copy(src, dst, copy_sems.at[1 + k]) for k, (src, dst) in enumerate(zip(lands, gathered))]

        pl.when(first(0))(send_own)

        @pl.when(u == 0)
        def _():
            xv = x_ref[...]
            ms = jnp.mean(xv * xv, axis=-1, keepdims=True)
            h_scr[pl.ds(pl.multiple_of(i * tm, tm), tm), :] = (xv * lax.rsqrt(ms + EPS) * g_ref[...]).astype(BF16)

        @pl.when(first(1))
        def _():
            pass_on_neighbours()
            ride[0]()
            y_complete()

        @pl.when(first(2))
        def _():
            x_complete()
            pass_on_diagonal()
            ride[1]()

        @pl.when(first(3))
        def _():
            diagonal_complete()
            copies[0].start()
            ride[2]()

        col0 = (pair * cx_ + 896 * cy_, pair * cx_ + 768 * (1 - cy_),
                pair * (1 - cx_) + 896 * cy_, pair * (1 - cx_) + 768 * (1 - cy_))
        slot = i % 2
        rows = pl.ds(pl.multiple_of(i * tm, tm), tm)

        def writeback(k, rows_):
            c0 = pl.multiple_of(col0[k], LANES)
            return pltpu.make_async_copy(res.at[slot, :, pl.ds(0, widths[k])], proj_hbm.at[rows_, pl.ds(c0, widths[k])],
                                         out_sems.at[slot])

        for k in range(4):
            @pl.when(u == k)
            def _(k=k):
                pl.when(i >= 2)(writeback(k, rows).wait)
                if k > 0:
                    pl.when(i < 2)(writeback(k - 1, rows).wait)
                w_rows = land[pl.ds(pl.multiple_of(col0[k], LANES), widths[k]), :]
                res[slot, :, 0:widths[k]] = _dot_nt(h_scr[rows, :], w_rows)
                writeback(k, rows).start()

        @pl.when(last)
        def _():
            sends_done()
            ride[3]()
            for cp in copies[1:]:
                cp.start()
            for cp in copies:
                cp.wait()
            pltpu.make_async_copy(res.at[0, :, pl.ds(0, widths[3])], proj_hbm.at[rows, pl.ds(0, widths[3])], out_sems.at[0]).wait()
            pltpu.make_async_copy(res.at[1, :, pl.ds(0, widths[3])], proj_hbm.at[rows, pl.ds(0, widths[3])], out_sems.at[1]).wait()

    full = [jax.ShapeDtypeStruct((4 * a.shape[0], a.shape[1]), BF16) for a in (wt_sh,) + ride_along]
    hbm = pl.BlockSpec(memory_space=pl.ANY)
    const = lambda a: pl.BlockSpec(a.shape, lambda u, i: (0, 0))
    return _call(
        body, name="gather_proj", grid=(4, nrow),
        in_specs=[pl.BlockSpec((tm, D_MODEL), lambda u, i: (jnp.where(u == 0, i, nrow - 1), 0)),
                  pl.BlockSpec((1, D_MODEL), lambda u, i: (0, 0)), const(wt_sh)] + [const(a) for a in ride_along],
        out_specs=[hbm] * (2 + nride),
        out_shape=[jax.ShapeDtypeStruct((SEQ, IN_W), F32)] + full,
        scratch_shapes=[pltpu.VMEM((SEQ, D_MODEL), BF16), pltpu.VMEM(full[0].shape, BF16), pltpu.VMEM((2, tm, max(widths)), F32)]
        + [pltpu.VMEM(s.shape, BF16) for s in full[1:]]
        + [pltpu.SemaphoreType.DMA((AG_SEMS,)), pltpu.SemaphoreType.DMA((AG_SEMS,)),
           pltpu.SemaphoreType.DMA((AG_SEMS * nride,)), pltpu.SemaphoreType.DMA((AG_SEMS * nride,)),
           pltpu.SemaphoreType.DMA((2,)), pltpu.SemaphoreType.DMA((1 + nride,))],
        compiler_params=_params(),
    )(x, gain, wt_sh, *ride_along)


def _gmlp_weights(w_ref):
    ti = lax.broadcasted_iota(jnp.int32, (CHUNK, CHUNK), 0)
    si = lax.broadcasted_iota(jnp.int32, (CHUNK, CHUNK), 1)
    tril = si <= ti
    return tril, [jnp.where(tril, w_ref[h], 0.0).astype(BF16) for h in range(4)]


def _gmlp_fwd(proj, vgain, w_s, bias_full):
    tm = 512

    def body(p_ref, vg_ref, w_ref, b_ref, y_ref):
        bd = _head_blockdiag()
        lo = _lo_mask(CHUNK)
        _, wm = _gmlp_weights(w_ref)
        for c in range(tm // CHUNK):
            rows = pl.ds(c * CHUNK, CHUNK)
            for p in range(2):
                cs = slice(p * LANES, (p + 1) * LANES)
                u = p_ref[rows, C_GU + p * LANES:C_GU + (p + 1) * LANES]
                v = p_ref[rows, C_GV + p * LANES:C_GV + (p + 1) * LANES]
                gt = p_ref[rows, C_GG + p * LANES:C_GG + (p + 1) * LANES]
                r = lax.rsqrt(_headsum(v * v, bd) * (1.0 / HEAD_DIM) + EPS)
                vn = (v * r * vg_ref[:, cs]).astype(BF16)
                sp = jnp.where(lo, _dot(wm[2 * p], vn), _dot(wm[2 * p + 1], vn)) + b_ref[:, cs]
                y_ref[rows, cs] = (u * sp * (gt * _sigmoid(gt))).astype(BF16)

    return _call(
        body, name="gmlp_fwd", grid=(SEQ // tm,),
        in_specs=[pl.BlockSpec((tm, 3 * GMLP_W), lambda i: (i, 0)),
                  pl.BlockSpec((1, GMLP_W), lambda i: (0, 0)),
                  pl.BlockSpec((4, CHUNK, CHUNK), lambda i: (0, 0, 0)),
                  pl.BlockSpec((CHUNK, GMLP_W), lambda i: (0, 0))],
        out_specs=pl.BlockSpec((tm, GMLP_W), lambda i: (i, 0)),
        out_shape=jax.ShapeDtypeStruct((SEQ, GMLP_W), BF16),
        compiler_params=_params(),
    )(proj, vgain, w_s, bias_full)


def _gmlp_bwd(proj, dyc, vgain, w_s, bias_full):
    tm = 512
    nsteps = SEQ // tm

    def body(p_ref, dy_ref, vg_ref, w_ref, b_ref, dg_ref, gw_ref, gb_ref, gv_ref):
        i = pl.program_id(0)
        bd = _head_blockdiag()
        lo = _lo_mask(CHUNK)
        tril, wm = _gmlp_weights(w_ref)
        ri = lax.broadcasted_iota(jnp.int32, (16, LANES), 0)
        li = lax.broadcasted_iota(jnp.int32, (16, LANES), 1)
        head_rows = [jnp.where(((ri == 2 * p) & (li < HEAD_DIM)) | ((ri == 2 * p + 1) & (li >= HEAD_DIM)), 1.0, 0.0).astype(BF16)
                     for p in range(2)]

        @pl.when(i == 0)
        def _():
            gw_ref[...] = jnp.zeros_like(gw_ref)
            gb_ref[...] = jnp.zeros_like(gb_ref)
            gv_ref[...] = jnp.zeros_like(gv_ref)

        for c in range(tm // CHUNK):
            rows = pl.ds(c * CHUNK, CHUNK)
            for p in range(2):
                cs = slice(p * LANES, (p + 1) * LANES)
                u = p_ref[rows, C_GU + p * LANES:C_GU + (p + 1) * LANES]
                v = p_ref[rows, C_GV + p * LANES:C_GV + (p + 1) * LANES]
                gt = p_ref[rows, C_GG + p * LANES:C_GG + (p + 1) * LANES]
                dy = dy_ref[rows, cs]
                g = vg_ref[:, cs]
                r = lax.rsqrt(_headsum(v * v, bd) * (1.0 / HEAD_DIM) + EPS)
                z = v * r
                vn = (z * g).astype(BF16)
                sp = jnp.where(lo, _dot(wm[2 * p], vn), _dot(wm[2 * p + 1], vn)) + b_ref[:, cs]
                sg = _sigmoid(gt)
                sl = gt * sg
                dsl = sg * (1.0 + gt * (1.0 - sg))
                du = dy * sp * sl
                dsp = dy * u * sl
                dgt = dy * u * sp * dsl
                dspb = dsp.astype(BF16)
                dvn = jnp.where(lo, _dot_tn(wm[2 * p], dspb), _dot_tn(wm[2 * p + 1], dspb))
                gw_ref[2 * p] += _dot_nt(jnp.where(lo, dsp, 0.0).astype(BF16), vn)
                gw_ref[2 * p + 1] += _dot_nt(jnp.where(lo, 0.0, dsp).astype(BF16), vn)
                dsp_lo = (dsp - dspb.astype(F32)).astype(BF16)
                gb_ref[...] += (_dot_nt(head_rows[p], dspb) + _dot_nt(head_rows[p], dsp_lo))[0:8]
                gvp = jnp.sum(dvn * z, axis=0, keepdims=True)
                gv_ref[2 * p:2 * p + 1, :] += gvp
                gv_ref[2 * p + 1:2 * p + 2, :] += pltpu.roll(gvp, HEAD_DIM, 1)
                dz = dvn * g
                dv = r * (dz - z * (_headsum(dz * z, bd) * (1.0 / HEAD_DIM)))
                dg_ref[rows, C_GU + p * LANES:C_GU + (p + 1) * LANES] = du.astype(BF16)
                dg_ref[rows, C_GV + p * LANES:C_GV + (p + 1) * LANES] = dv.astype(BF16)
                dg_ref[rows, C_GG + p * LANES:C_GG + (p + 1) * LANES] = dgt.astype(BF16)

        @pl.when(i == nsteps - 1)
        def _():
            for h in range(4):
                gw_ref[h] = jnp.where(tril, gw_ref[h], 0.0)

    return _call(
        body, name="gmlp_bwd", grid=(nsteps,),
        in_specs=[pl.BlockSpec((tm, 3 * GMLP_W), lambda i: (i, 0)),
                  pl.BlockSpec((tm, GMLP_W), lambda i: (i, 0)),
                  pl.BlockSpec((1, GMLP_W), lambda i: (0, 0)),
                  pl.BlockSpec((4, CHUNK, CHUNK), lambda i: (0, 0, 0)),
                  pl.BlockSpec((CHUNK, GMLP_W), lambda i: (0, 0))],
        out_specs=[pl.BlockSpec((tm, 3 * GMLP_W), lambda i: (i, 0)),
                   pl.BlockSpec((4, CHUNK, CHUNK), lambda i: (0, 0, 0)),
                   pl.BlockSpec((8, LANES), lambda i: (0, 0)),
                   pl.BlockSpec((8, LANES), lambda i: (0, 0))],
        out_shape=[jax.ShapeDtypeStruct((SEQ, 3 * GMLP_W), BF16),
                   jax.ShapeDtypeStruct((4, CHUNK, CHUNK), F32),
                   jax.ShapeDtypeStruct((8, LANES), F32),
                   jax.ShapeDtypeStruct((8, LANES), F32)],
        compiler_params=_params(),
    )(proj, dyc, vgain, w_s, bias_full)


def _band_masks():
    qi = lax.broadcasted_iota(jnp.int32, (CHUNK, 2 * CHUNK), 0)
    kj = lax.broadcasted_iota(jnp.int32, (CHUNK, 2 * CHUNK), 1)
    valid2 = ((kj < CHUNK) & (kj >= qi)) | ((kj >= CHUNK) & (kj - CHUNK <= qi))
    q1 = lax.broadcasted_iota(jnp.int32, (CHUNK, CHUNK), 0)
    k1 = lax.broadcasted_iota(jnp.int32, (CHUNK, CHUNK), 1)
    return k1 <= q1, valid2


def _stack_heads(v, lo):
    return jnp.concatenate([jnp.where(lo, v, 0.0), jnp.where(lo, 0.0, v)], axis=0).astype(BF16)


def _rows_of(ref, start, d):
    if d == 1:
        return ref.at[pl.ds(start if isinstance(start, int) else pl.multiple_of(start, CHUNK), CHUNK), :]
    return ref.at[pl.ds(start, CHUNK, stride=d), :]


def _unrolled(lo, hi, unroll, run):
    groups = (hi - lo) // unroll
    if groups:
        def body(g, carry):
            run([lo + g * unroll + t for t in range(unroll)])
            return carry

        lax.fori_loop(0, groups, body, 0)
    if lo + groups * unroll < hi:
        run(range(lo + groups * unroll, hi))


def _for_blocks(d, group_fn, unroll):
    nblk = SEQ // CHUNK
    sh = d.bit_length() - 1

    def first(j):
        return (j * CHUNK if d == 1 else j, None)

    def rest(j):
        start = (j & (d - 1)) + (j >> sh) * (CHUNK * d)
        return (start, start - CHUNK * d)

    _unrolled(0, d, unroll, lambda js: group_fn(d, [first(j) for j in js]))
    _unrolled(d, nblk, unroll, lambda js: group_fn(d, [rest(j) for j in js]))


def _attn_fwd(proj, gq2, gk2):
    tn = 512

    def body(q_ref, k_ref, v_ref, g_ref, gq_ref, gk_ref, o_ref, l_ref, ya_ref, qn_ref, kn_ref):
        bd = _head_blockdiag()
        lo = _lo_mask(CHUNK)
        valid1, valid2 = _band_masks()

        def norm(t, carry):
            rows = pl.ds(pl.multiple_of(t * tn, tn), tn)
            q = q_ref[rows, :]
            qn_ref[rows, :] = q * lax.rsqrt(_headsum(q * q, bd) * (1.0 / HEAD_DIM) + EPS) * (gq_ref[...] * QK_SCALE)
            k = k_ref[rows, :]
            kn_ref[rows, :] = k * lax.rsqrt(_headsum(k * k, bd) * (1.0 / HEAD_DIM) + EPS) * gk_ref[...]
            return carry

        lax.fori_loop(0, SEQ // tn, norm, 0)

        def load_kv(ref, d, start, prev):
            own = _rows_of(ref, start, d)[...]
            if prev is None:
                return own.astype(BF16)
            return jnp.concatenate([_rows_of(ref, prev, d)[...], own], axis=0).astype(BF16)

        def group(d, blocks):
            valid = valid1 if blocks[0][1] is None else valid2
            valid = jnp.concatenate([valid, valid], axis=0)
            qs = [_rows_of(qn_ref, start, d)[...] for start, _ in blocks]
            ks = [load_kv(kn_ref, d, start, prev) for start, prev in blocks]
            vs = [load_kv(v_ref, d, start, prev) for start, prev in blocks]
            ss = [_dot_nt(_stack_heads(q, lo), k) for q, k in zip(qs, ks)]
            ms, ps, ls = [], [], []
            for s in ss:
                s = jnp.where(valid, s, -jnp.inf)
                m = jnp.max(s, axis=-1, keepdims=True)
                p = jnp.exp(s - m)
                ms.append(m)
                ls.append(jnp.sum(p, axis=-1, keepdims=True))
                ps.append(p.astype(BF16))
            os_ = [_dot(p, v) for p, v in zip(ps, vs)]
            for b, (start, _) in enumerate(blocks):
                on = os_[b] * (1.0 / ls[b])
                ln = ms[b] + jnp.log(ls[b])
                ob = jnp.where(lo, on[:CHUNK], on[CHUNK:])
                lb = jnp.where(lo, ln[:CHUNK], ln[CHUNK:])
                o_rows = _rows_of(o_ref, start, d)
                l_rows = _rows_of(l_ref, start, d)
                if d != DILATIONS[0]:
                    lold = l_rows[...]
                    mx = jnp.maximum(lold, lb)
                    ea = jnp.exp(lold - mx)
                    eb = jnp.exp(lb - mx)
                    inv = 1.0 / (ea + eb)
                    ob = o_rows[...] * (ea * inv) + ob * (eb * inv)
                    lb = mx + jnp.log(ea + eb)
                o_rows[...] = ob
                l_rows[...] = lb

        for d in DILATIONS:
            _for_blocks(d, group, ATTN_UNROLL)

        def fin(t, carry):
            rows = pl.ds(pl.multiple_of(t * tn, tn), tn)
            g = g_ref[rows, :]
            ya_ref[rows, :] = (o_ref[rows, :] * (g * _sigmoid(g))).astype(BF16)
            return carry

        lax.fori_loop(0, SEQ // tn, fin, 0)

    col = lambda c0: pl.BlockSpec((SEQ, LANES), lambda p: (0, c0 // LANES + p))
    vec = pl.BlockSpec((1, LANES), lambda p: (0, 0))
    out = pl.BlockSpec((SEQ, LANES), lambda p: (0, p))
    return _call(
        body, name="attn_fwd", grid=(ATTN_W // LANES,),
        in_specs=[col(C_AQ), col(C_AK), col(C_AV), col(C_AG), vec, vec],
        out_specs=[out, out, out],
        out_shape=[jax.ShapeDtypeStruct((SEQ, ATTN_W), F32), jax.ShapeDtypeStruct((SEQ, ATTN_W), F32),
                   jax.ShapeDtypeStruct((SEQ, ATTN_W), BF16)],
        scratch_shapes=[pltpu.VMEM((SEQ, LANES), F32), pltpu.VMEM((SEQ, LANES), F32)],
        compiler_params=_params(),
    )(proj, proj, proj, proj, gq2, gk2)


def _attn_bwd(proj, o, lse, dyc, gq2, gk2, *ride_along):
    tn = 512
    npairs = ATTN_W // LANES
    nride = len(ride_along)
    nbufs = nride * len(RS_KINDS)

    def body(proj_hbm, o_hbm, l_hbm, dyc_hbm, gq_ref, gk_ref, *rest):
        ride_in, rest = rest[:nride], rest[nride:]
        dq_ref, dk_ref, dv_ref, dgt_ref, gqg_ref, gkg_ref = rest[:6]
        ride_out, rest = rest[6:6 + nride], rest[6 + nride:]
        qb_, kb_, vb_, gb_, ob_, lb_, yb_, dkb_, dvb_, sems = rest[:10]
        rs_bufs, (send_sems, recv_sems, local_sems) = rest[10:10 + nbufs], rest[10 + nbufs:]
        rs_stage = _rs_stages(ride_in, ride_out, rs_bufs, send_sems, recv_sems, local_sems, [g.shape[1] for g in ride_along])
        pair = pl.program_id(0)
        for step in range(npairs):
            pl.when(pair == step)(rs_stage[step])
        bd = _head_blockdiag()
        lo = _lo_mask(CHUNK)
        lo2 = lax.broadcasted_iota(jnp.int32, (2 * CHUNK, LANES), 1) < HEAD_DIM
        valid1, valid2 = _band_masks()
        gqs = gq_ref[...] * QK_SCALE
        gk = gk_ref[...]

        def pcol(c0):
            return proj_hbm.at[:, pl.ds(pl.multiple_of(c0 + pair * LANES, LANES), LANES)]

        def acol(hbm, c0=0):
            return hbm.at[:, pl.ds(pl.multiple_of(c0 + pair * LANES, LANES), LANES)]

        loads = [pltpu.make_async_copy(src, dst, sems.at[n]) for n, (src, dst) in enumerate((
            (pcol(C_AQ), qb_), (pcol(C_AK), kb_), (pcol(C_AG), gb_), (acol(o_hbm), ob_),
            (acol(dyc_hbm, GMLP_W), yb_), (pcol(C_AV), vb_), (acol(l_hbm), lb_)))]
        for cp in loads:
            cp.start()

        @pl.when(pair == 0)
        def _():
            gqg_ref[...] = jnp.zeros_like(gqg_ref)
            gkg_ref[...] = jnp.zeros_like(gkg_ref)

        def pre_qk(t, carry):
            rows = pl.ds(pl.multiple_of(t * tn, tn), tn)
            zero = jnp.zeros((tn, LANES), F32)
            dkb_[rows, :] = zero
            dvb_[rows, :] = zero
            q = qb_[rows, :]
            qb_[rows, :] = q * lax.rsqrt(_headsum(q * q, bd) * (1.0 / HEAD_DIM) + EPS) * gqs
            k = kb_[rows, :]
            kb_[rows, :] = k * lax.rsqrt(_headsum(k * k, bd) * (1.0 / HEAD_DIM) + EPS) * gk
            return carry

        def pre_gate(t, carry):
            rows = pl.ds(pl.multiple_of(t * tn, tn), tn)
            g = gb_[rows, :]
            ov = ob_[rows, :]
            dya = yb_[rows, :]
            sg = _sigmoid(g)
            dgt_ref[rows, :] = (dya * ov * (sg * (1.0 + g * (1.0 - sg)))).astype(BF16)
            do = dya * (g * sg)
            yb_[rows, :] = do
            ob_[rows, :] = _headsum(do * ov, bd)
            gb_[rows, :] = jnp.zeros((tn, LANES), F32)
            return carry

        loads[0].wait()
        loads[1].wait()
        lax.fori_loop(0, SEQ // tn, pre_qk, 0)
        for cp in loads[2:5]:
            cp.wait()
        lax.fori_loop(0, SEQ // tn, pre_gate, 0)
        loads[5].wait()
        loads[6].wait()

        def load_kv(ref, d, start, prev):
            own = _rows_of(ref, start, d)[...]
            if prev is None:
                return own.astype(BF16)
            return jnp.concatenate([_rows_of(ref, prev, d)[...], own], axis=0).astype(BF16)

        def group(d, blocks):
            first = blocks[0][1] is None
            valid, lok = (valid1, lo) if first else (valid2, lo2)
            chains = [(b, h) for b in range(len(blocks)) for h in range(2)]
            mask = lambda h: lo if h == 0 else ~lo
            qs = [_rows_of(qb_, start, d)[...] for start, _ in blocks]
            dos = [_rows_of(yb_, start, d)[...] for start, _ in blocks]
            lvs = [_rows_of(lb_, start, d)[...] for start, _ in blocks]
            dls = [_rows_of(ob_, start, d)[...] for start, _ in blocks]
            ks = [load_kv(kb_, d, start, prev) for start, prev in blocks]
            vs = [load_kv(vb_, d, start, prev) for start, prev in blocks]
            qbs = [q.astype(BF16) for q in qs]
            dobs = [do.astype(BF16) for do in dos]
            ss = [_dot_nt(jnp.where(mask(h), qs[b], 0.0).astype(BF16), ks[b]) for b, h in chains]
            dps = [_dot_nt(jnp.where(mask(h), dos[b], 0.0).astype(BF16), vs[b]) for b, h in chains]
            pbs, dss = [], []
            for s, dp, (b, h) in zip(ss, dps, chains):
                hc = h * HEAD_DIM
                p = jnp.exp(jnp.where(valid, s, -jnp.inf) - lvs[b][:, hc:hc + 1])
                pbs.append(p.astype(BF16))
                dss.append((p * (dp - dls[b][:, hc:hc + 1])).astype(BF16))
            dqs = [_dot(ds, ks[b]) for ds, (b, h) in zip(dss, chains)]
            dks = [_dot_tn(ds, qbs[b]) for ds, (b, h) in zip(dss, chains)]
            dvs = [_dot_tn(p, dobs[b]) for p, (b, h) in zip(pbs, chains)]
            for b, (start, prev) in enumerate(blocks):
                c0, c1 = 2 * b, 2 * b + 1
                dq_rows = _rows_of(gb_, start, d)
                dq_rows[...] = dq_rows[...] + jnp.where(lo, dqs[c0], dqs[c1])
                dkc = jnp.where(lok, dks[c0], dks[c1])
                dvc = jnp.where(lok, dvs[c0], dvs[c1])
                spans = ((start, slice(0, CHUNK)),) if first else ((prev, slice(0, CHUNK)), (start, slice(CHUNK, 2 * CHUNK)))
                for st, sl in spans:
                    dk_rows = _rows_of(dkb_, st, d)
                    dk_rows[...] = dk_rows[...] + dkc[sl]
                    dv_rows = _rows_of(dvb_, st, d)
                    dv_rows[...] = dv_rows[...] + dvc[sl]

        for d in DILATIONS:
            _for_blocks(d, group, ATTN_UNROLL)

        reloads = [pltpu.make_async_copy(pcol(C_AQ), vb_, sems.at[0]), pltpu.make_async_copy(pcol(C_AK), lb_, sems.at[1])]
        for cp in reloads:
            cp.start()
        for cp in reloads:
            cp.wait()

        def post(t, carry):
            gq_acc, gk_acc = carry
            rows = pl.ds(pl.multiple_of(t * tn, tn), tn)
            outs = []
            for raw_, acc_, gain in ((vb_, gb_, gqs), (lb_, dkb_, gk)):
                a = raw_[rows, :]
                r = lax.rsqrt(_headsum(a * a, bd) * (1.0 / HEAD_DIM) + EPS)
                z = a * r
                dn = acc_[rows, :]
                dz = dn * gain
                outs.append((r * (dz - z * (_headsum(dz * z, bd) * (1.0 / HEAD_DIM))), jnp.sum(dn * z, axis=0, keepdims=True)))
            dq_ref[rows, :] = outs[0][0].astype(BF16)
            dk_ref[rows, :] = outs[1][0].astype(BF16)
            dv_ref[rows, :] = dvb_[rows, :].astype(BF16)
            return gq_acc + outs[0][1] * QK_SCALE, gk_acc + outs[1][1]

        zero = jnp.zeros((1, LANES), F32)
        gq_acc, gk_acc = lax.fori_loop(0, SEQ // tn, post, (zero, zero))
        gqg_ref[0:1, :] += gq_acc
        gkg_ref[0:1, :] += gk_acc

        @pl.when(pair == npairs - 1)
        def _():
            gqg_ref[0:1, :] = _fold_heads(gqg_ref[0:1, :])
            gkg_ref[0:1, :] = _fold_heads(gkg_ref[0:1, :])
            rs_stage[npairs]()

    hbm = pl.BlockSpec(memory_space=pl.ANY)
    vec = pl.BlockSpec((1, LANES), lambda p: (0, 0))
    blk8 = pl.BlockSpec((8, LANES), lambda p: (0, 0))
    out = pl.BlockSpec((SEQ, LANES), lambda p: (0, p))
    big = jax.ShapeDtypeStruct((SEQ, ATTN_W), BF16)
    nsem = RS_SEMS * nride
    return _call(
        body, name="attn_bwd", grid=(npairs,),
        in_specs=[hbm, hbm, hbm, hbm, vec, vec] + [hbm] * nride,
        out_specs=[out, out, out, out, blk8, blk8] + [hbm] * nride,
        out_shape=[big, big, big, big, jax.ShapeDtypeStruct((8, LANES), F32), jax.ShapeDtypeStruct((8, LANES), F32)]
        + [jax.ShapeDtypeStruct((2, g.shape[0] // 8, g.shape[1]), F32) for g in ride_along],
        scratch_shapes=[pltpu.VMEM((SEQ, LANES), F32) for _ in range(9)] + [pltpu.SemaphoreType.DMA((7,))]
        + _rs_scratch([g.shape for g in ride_along]) + [pltpu.SemaphoreType.DMA((nsem,)), pltpu.SemaphoreType.DMA((nsem,)),
                                     pltpu.SemaphoreType.DMA((nride,))],
        compiler_params=_params(),
    )(proj, o, lse, dyc, gq2, gk2, *[_rs_view(g) for g in ride_along])


def _mem_kv(mem, gain, wkv):
    def body(m_ref, g_ref, w_ref, kv_ref, hm_ref):
        mv = m_ref[...]
        ms = jnp.mean(mv * mv, axis=-1, keepdims=True)
        hm = (mv * lax.rsqrt(ms + EPS) * g_ref[...]).astype(BF16)
        hm_ref[...] = hm
        kv_ref[...] = _dot(hm, w_ref[...])

    return _call(
        body, name="mem_kv",
        out_shape=[jax.ShapeDtypeStruct((MEM_LEN, 2 * MEM_W), F32), jax.ShapeDtypeStruct((MEM_LEN, D_MODEL), BF16)],
        compiler_params=_params(),
    )(mem, gain, wkv)


def _mem_keys(kv_ref, kg_ref, bd, p):
    mk = kv_ref[:, p * LANES:(p + 1) * LANES]
    r = lax.rsqrt(_headsum(mk * mk, bd) * (1.0 / HEAD_DIM) + EPS)
    z = mk * r
    mkn = (z * kg_ref[:, p * LANES:(p + 1) * LANES]).astype(BF16)
    mvp = kv_ref[:, MEM_W + p * LANES:MEM_W + (p + 1) * LANES].astype(BF16)
    return mkn, mvp, r, z


def _mem_fwd(proj, kv, qg4, kg4):
    tm = 512

    def body(q_ref, g_ref, kv_ref, qg_ref, kg_ref, om_ref, ym_ref):
        bd = _head_blockdiag()
        lo = _lo_mask(tm)
        for p in range(2):
            cs = slice(p * LANES, (p + 1) * LANES)
            mkn, mvp, _, _ = _mem_keys(kv_ref, kg_ref, bd, p)
            q = q_ref[:, cs]
            qn = q * lax.rsqrt(_headsum(q * q, bd) * (1.0 / HEAD_DIM) + EPS) * (qg_ref[:, cs] * QK_SCALE)
            res = []
            for h in range(2):
                qh = jnp.where(lo if h == 0 else ~lo, qn, 0.0).astype(BF16)
                s = _dot_nt(qh, mkn)
                e = jnp.exp(s - jnp.max(s, axis=-1, keepdims=True))
                res.append(_dot(e.astype(BF16), mvp) * (1.0 / jnp.sum(e, axis=-1, keepdims=True)))
            ov = jnp.where(lo, res[0], res[1])
            g = g_ref[:, cs]
            om_ref[:, cs] = ov
            ym_ref[:, cs] = (ov * (g * _sigmoid(g))).astype(BF16)

    vec = pl.BlockSpec((1, MEM_W), lambda i: (0, 0))
    return _call(
        body, name="mem_fwd", grid=(SEQ // tm,),
        in_specs=[pl.BlockSpec((tm, MEM_W), lambda i: (i, C_MQ // MEM_W)),
                  pl.BlockSpec((tm, MEM_W), lambda i: (i, C_MG // MEM_W)),
                  pl.BlockSpec((MEM_LEN, 2 * MEM_W), lambda i: (0, 0)), vec, vec],
        out_specs=[pl.BlockSpec((tm, MEM_W), lambda i: (i, 0)), pl.BlockSpec((tm, MEM_W), lambda i: (i, 0))],
        out_shape=[jax.ShapeDtypeStruct((SEQ, MEM_W), F32), jax.ShapeDtypeStruct((SEQ, MEM_W), BF16)],
        compiler_params=_params(),
    )(proj, proj, kv, qg4, kg4)


def _mem_bwd(proj, om, dyc, kv, hm, mem, mgain, wkv, qg4, kg4):
    tm = 512
    nsteps = SEQ // tm

    def body(q_ref, g_ref, om_ref, dy_ref, kv_ref, hm_ref, mem_ref, mg_ref, w_ref, qg_ref, kg_ref,
             dq_ref, dgt_ref, gqg_ref, gkg_ref, gw_ref, gmg_ref, dmk_ref, dmv_ref, gq_acc):
        i = pl.program_id(0)
        bd = _head_blockdiag()
        lo = _lo_mask(tm)
        lom = _lo_mask(MEM_LEN)

        @pl.when(i == 0)
        def _():
            dmk_ref[...] = jnp.zeros_like(dmk_ref)
            dmv_ref[...] = jnp.zeros_like(dmv_ref)
            gq_acc[...] = jnp.zeros_like(gq_acc)

        for p in range(2):
            cs = slice(p * LANES, (p + 1) * LANES)
            mkn, mvp, _, _ = _mem_keys(kv_ref, kg_ref, bd, p)
            gqs = qg_ref[:, cs] * QK_SCALE
            q = q_ref[:, cs]
            r = lax.rsqrt(_headsum(q * q, bd) * (1.0 / HEAD_DIM) + EPS)
            z = q * r
            qn = z * gqs
            qnb = qn.astype(BF16)
            g = g_ref[:, cs]
            ov = om_ref[:, cs]
            dym = dy_ref[:, cs]
            sg = _sigmoid(g)
            dgt_ref[:, cs] = (dym * ov * (sg * (1.0 + g * (1.0 - sg)))).astype(BF16)
            do = dym * (g * sg)
            dob = do.astype(BF16)
            delta = _headsum(do * ov, bd)
            parts = []
            for h in range(2):
                mh = lo if h == 0 else ~lo
                hc = h * HEAD_DIM
                qh = jnp.where(mh, qn, 0.0).astype(BF16)
                doh = jnp.where(mh, do, 0.0).astype(BF16)
                s = _dot_nt(qh, mkn)
                e = jnp.exp(s - jnp.max(s, axis=-1, keepdims=True))
                pr = e * (1.0 / jnp.sum(e, axis=-1, keepdims=True))
                dp = _dot_nt(doh, mvp)
                ds = (pr * (dp - delta[:, hc:hc + 1])).astype(BF16)
                parts.append((_dot(ds, mkn), _dot_tn(ds, qnb), _dot_tn(pr.astype(BF16), dob)))
            dqn = jnp.where(lo, parts[0][0], parts[1][0])
            dmk_ref[:, cs] += jnp.where(lom, parts[0][1], parts[1][1])
            dmv_ref[:, cs] += jnp.where(lom, parts[0][2], parts[1][2])
            dz = dqn * gqs
            dq_ref[:, cs] = (r * (dz - z * (_headsum(dz * z, bd) * (1.0 / HEAD_DIM)))).astype(BF16)
            gq_acc[:, cs] += jnp.sum(dqn * z, axis=0, keepdims=True) * QK_SCALE

        @pl.when(i == nsteps - 1)
        def _():
            gqg_ref[...] = jnp.zeros_like(gqg_ref)
            gkg_ref[...] = jnp.zeros_like(gkg_ref)
            gqg_ref[0:1, :] = _fold_heads(gq_acc[:, 0:LANES] + gq_acc[:, LANES:2 * LANES])
            dkv = []
            gk = jnp.zeros((1, LANES), F32)
            for p in range(2):
                cs = slice(p * LANES, (p + 1) * LANES)
                _, _, r, z = _mem_keys(kv_ref, kg_ref, bd, p)
                dn = dmk_ref[:, cs]
                dz = dn * kg_ref[:, cs]
                gk = gk + jnp.sum(dn * z, axis=0, keepdims=True)
                dkv.append(r * (dz - z * (_headsum(dz * z, bd) * (1.0 / HEAD_DIM))))
            gkg_ref[0:1, :] = _fold_heads(gk)
            dkvb = jnp.concatenate(dkv + [dmv_ref[...]], axis=1).astype(BF16)
            gw_ref[...] = _dot_tn(hm_ref[...], dkvb)
            dhm = _dot_nt(dkvb, w_ref[...])
            mv = mem_ref[...]
            zm = mv * lax.rsqrt(jnp.mean(mv * mv, axis=-1, keepdims=True) + EPS)
            _put_rows(gmg_ref, jnp.sum(dhm * zm, axis=0, keepdims=True))

    const = lambda shape: pl.BlockSpec(shape, lambda i: (0,) * len(shape))
    row = lambda j: pl.BlockSpec((tm, MEM_W), lambda i: (i, j))
    blk8 = jax.ShapeDtypeStruct((8, LANES), F32)
    return _call(
        body, name="mem_bwd", grid=(nsteps,),
        in_specs=[row(C_MQ // MEM_W), row(C_MG // MEM_W), row(0), row((GMLP_W + ATTN_W) // MEM_W),
                  const((MEM_LEN, 2 * MEM_W)), const((MEM_LEN, D_MODEL)), const((MEM_LEN, D_MODEL)),
                  const((1, D_MODEL)), const((D_MODEL, 2 * MEM_W)), const((1, MEM_W)), const((1, MEM_W))],
        out_specs=[row(0), row(0), const((8, LANES)), const((8, LANES)),
                   const((D_MODEL, 2 * MEM_W)), const((8, LANES))],
        out_shape=[jax.ShapeDtypeStruct((SEQ, MEM_W), BF16), jax.ShapeDtypeStruct((SEQ, MEM_W), BF16),
                   blk8, blk8, jax.ShapeDtypeStruct((D_MODEL, 2 * MEM_W), F32), blk8],
        scratch_shapes=[pltpu.VMEM((MEM_LEN, MEM_W), F32), pltpu.VMEM((MEM_LEN, MEM_W), F32),
                        pltpu.VMEM((1, MEM_W), F32)],
        compiler_params=_params(),
    )(proj, proj, om, dyc, kv, hm, mem, mgain, wkv, qg4, kg4)


def _out_loss(yg, ya, ym, x, tgt, wo):
    tm = 512
    nsteps = SEQ // tm
    parts = ((0, GMLP_W), (GMLP_W, ATTN_W), (GMLP_W + ATTN_W, MEM_W))

    def body(yg_ref, ya_ref, ym_ref, x_ref, t_ref, w_ref, dy_ref, dyc_ref, gw_ref, ls_ref):
        i = pl.program_id(0)

        @pl.when(i == 0)
        def _():
            gw_ref[...] = jnp.zeros_like(gw_ref)
            ls_ref[...] = jnp.zeros_like(ls_ref)

        ys = (yg_ref[...], ya_ref[...], ym_ref[...])
        y = sum(_dot(yv, w_ref[r0:r0 + n, :]) for yv, (r0, n) in zip(ys, parts))
        err = x_ref[...] + y - t_ref[...]
        _put_rows(ls_ref, jnp.sum(err * err, axis=0, keepdims=True), accumulate=True)
        dy = err * (1.0 / D_MODEL)
        dy_ref[...] = dy
        dyb = dy.astype(BF16)
        dyc_ref[...] = _dot_nt(dyb, w_ref[...])
        for yv, (r0, n) in zip(ys, parts):
            gw_ref[r0:r0 + n, :] += _dot_tn(yv, dyb)

    row = lambda w: pl.BlockSpec((tm, w), lambda i: (i, 0))
    const = lambda shape: pl.BlockSpec(shape, lambda i: (0, 0))
    return _call(
        body, name="out_loss", grid=(nsteps,),
        in_specs=[row(GMLP_W), row(ATTN_W), row(MEM_W), row(D_MODEL), row(D_MODEL), const((D_MODEL, D_MODEL))],
        out_specs=[row(D_MODEL), row(D_MODEL), const((D_MODEL, D_MODEL)), const((8, LANES))],
        out_shape=[jax.ShapeDtypeStruct((SEQ, D_MODEL), F32), jax.ShapeDtypeStruct((SEQ, D_MODEL), F32),
                   jax.ShapeDtypeStruct((D_MODEL, D_MODEL), F32), jax.ShapeDtypeStruct((8, LANES), F32)],
        compiler_params=_params(),
    )(yg, ya, ym, x, tgt, wo)


def _proj_bwd(x, dy, gain, wt, dg, daq, dak, dav, dag, dmq, dmg):
    tm = 512
    nsteps = SEQ // tm
    pieces = ((C_GU, 3 * GMLP_W), (C_AQ, ATTN_W), (C_AK, ATTN_W), (C_AV, ATTN_W), (C_AG, ATTN_W),
              (C_MQ, MEM_W), (C_MG, MEM_W))

    def body(x_ref, dy_ref, g_ref, wt_hbm, p0, p1, p2, p3, p4, p5, p6, gx_ref, gwt_hbm, gg_ref, wt_v, acc, wt_sem, out_sems):
        i = pl.program_id(0)
        wt_load = pltpu.make_async_copy(wt_hbm, wt_v, wt_sem)

        @pl.when(i == 0)
        def _():
            wt_load.start()
            acc[...] = jnp.zeros_like(acc)
            gg_ref[...] = jnp.zeros_like(gg_ref)

        xv = x_ref[...]
        r = lax.rsqrt(jnp.mean(xv * xv, axis=-1, keepdims=True) + EPS)
        z = xv * r
        g = g_ref[...]
        h = (z * g).astype(BF16)
        pl.when(i == 0)(wt_load.wait)
        flush = [pltpu.make_async_copy(acc.at[c0:c0 + w, :], gwt_hbm.at[c0:c0 + w, :], out_sems.at[n])
                 for n, (c0, w) in enumerate(pieces)]
        dh = jnp.zeros((tm, D_MODEL), F32)
        for n, (pref, (c0, w)) in enumerate(zip((p0, p1, p2, p3, p4, p5, p6), pieces)):
            dp = pref[...]
            dh = dh + _dot(dp, wt_v[c0:c0 + w, :])
            acc[c0:c0 + w, :] += _dot_tn(dp, h)
            pl.when(i == nsteps - 1)(flush[n].start)
        _put_rows(gg_ref, jnp.sum(dh * z, axis=0, keepdims=True), accumulate=True)
        dz = dh * g
        gx_ref[...] = dy_ref[...] + r * (dz - z * jnp.mean(dz * z, axis=-1, keepdims=True))

        @pl.when(i == nsteps - 1)
        def _():
            for cp in flush:
                cp.wait()

    row = lambda w: pl.BlockSpec((tm, w), lambda i: (i, 0))
    hbm = pl.BlockSpec(memory_space=pl.ANY)
    vec = pl.BlockSpec((1, D_MODEL), lambda i: (0, 0))
    return _call(
        body, name="proj_bwd", grid=(nsteps,),
        in_specs=[row(D_MODEL), row(D_MODEL), vec, hbm] + [row(w) for _, w in pieces],
        out_specs=[row(D_MODEL), hbm, pl.BlockSpec((8, LANES), lambda i: (0, 0))],
        out_shape=[jax.ShapeDtypeStruct((SEQ, D_MODEL), F32), jax.ShapeDtypeStruct((IN_W, D_MODEL), F32),
                   jax.ShapeDtypeStruct((8, LANES), F32)],
        scratch_shapes=[pltpu.VMEM((IN_W, D_MODEL), BF16), pltpu.VMEM((IN_W, D_MODEL), F32), pltpu.SemaphoreType.DMA,
                        pltpu.SemaphoreType.DMA((len(pieces),))],
        compiler_params=_params(),
    )(x, dy, gain, wt, dg, daq, dak, dav, dag, dmq, dmg)


AG_SEMS = 8


def _gather_stages(ins, lands, send_sems, recv_sems):
    n = len(ins)
    nrows = [a.shape[0] for a in ins]
    x, y, c = lax.axis_index("x"), lax.axis_index("y"), lax.axis_index("c")
    sib, xn, yn = (x, y, 1 - c), (1 - x, y, c), (x, 1 - y, c)
    me, cx, cy, cd = 2 * x + y, 2 * (1 - x) + y, 2 * x + (1 - y), 2 * (1 - x) + (1 - y)

    def part(a, chip, hf, quarter=None):
        rows = nrows[a] // 2
        base = chip * nrows[a] + hf * rows
        if quarter is not None:
            rows = rows // 2
            base = base + quarter * rows
        return lands[a].at[pl.ds(pl.multiple_of(base, 16), rows), :]

    def copy(a, j, ref, to):
        k = AG_SEMS * a + j
        return pltpu.make_async_remote_copy(src_ref=ref, dst_ref=ref, send_sem=send_sems.at[k],
                                            recv_sem=recv_sems.at[k], device_id=to, device_id_type=MESH)

    def own(a):
        return [copy(a, 0, part(a, me, c), xn), copy(a, 1, part(a, me, c), yn)]

    def neighbours(a):
        return [copy(a, 4, part(a, cx, c, 1), yn), copy(a, 2, part(a, cx, c), sib),
                copy(a, 5, part(a, cy, c, 0), xn), copy(a, 3, part(a, cy, c), sib)]

    def diagonal(a):
        return [copy(a, 7, part(a, cd, c, 1), sib), copy(a, 6, part(a, cd, c, 0), sib)]

    def send_own():
        for a in range(n):
            lands[a][pl.ds(pl.multiple_of(me * nrows[a], 16), nrows[a]), :] = ins[a][...].astype(BF16)
            for cp in own(a):
                cp.start()

    def pass_on_neighbours():
        for a in range(n):
            copy(a, 0, part(a, cx, c), xn).wait_recv()
            copy(a, 1, part(a, cy, c), yn).wait_recv()
            for cp in neighbours(a):
                cp.start()

    def pass_on_diagonal():
        for a in range(n):
            copy(a, 4, part(a, cd, c, 1), yn).wait_recv()
            copy(a, 5, part(a, cd, c, 0), xn).wait_recv()
            for cp in diagonal(a):
                cp.start()

    def y_complete():
        for a in range(n):
            copy(a, 3, part(a, cy, 1 - c), sib).wait_recv()

    def x_complete():
        for a in range(n):
            copy(a, 2, part(a, cx, 1 - c), sib).wait_recv()

    def diagonal_complete():
        for a in range(n):
            copy(a, 6, part(a, cd, 1 - c, 0), sib).wait_recv()
            copy(a, 7, part(a, cd, 1 - c, 1), sib).wait_recv()

    def sends_done():
        for a in range(n):
            for cp in own(a) + neighbours(a) + diagonal(a):
                cp.wait_send()

    def finish():
        y_complete()
        x_complete()
        diagonal_complete()
        sends_done()

    return (send_own, pass_on_neighbours, pass_on_diagonal, finish), (y_complete, x_complete, diagonal_complete, sends_done)


RS_SEMS = 6
RS_KINDS = (((2, 2), 1, F32), ((2, 2), 1, F32), ((2, 2), 2, BF16), ((2, 2), 2, BF16), ((2, 2), 2, F32),
            ((2,), 2, BF16), ((2,), 2, BF16), ((2,), 1, F32))


def _rs_view(g):
    return g.reshape(2, 2, 2, g.shape[0] // 8, g.shape[1])


def _rs_scratch(shapes, in_vmem=False):
    kinds = RS_KINDS[1:] if in_vmem else RS_KINDS
    return [pltpu.VMEM(lead + (r // 8, w // split), dt) for lead, split, dt in kinds for r, w in shapes]


def _rs_stages(gs, outs, bufs, send_sems, recv_sems, local_sems, widths):
    n = len(gs)
    if len(bufs) < n * len(RS_KINDS):
        bufs = [None] * n + list(bufs)
    loc, ra, s_b, r_b, acc1, s_c, r_c, fin = (bufs[n * i:n * i + n] for i in range(len(RS_KINDS)))
    half_w = [w // 2 for w in widths]
    chips = [(xx, yy) for xx in range(2) for yy in range(2)]
    x, y, c = lax.axis_index("x"), lax.axis_index("y"), lax.axis_index("c")
    sib, xn, yn = (x, y, 1 - c), (1 - x, y, c), (x, 1 - y, c)

    def copy(a, j, src, dst, to):
        k = RS_SEMS * a + j
        return pltpu.make_async_remote_copy(src_ref=src, dst_ref=dst, send_sem=send_sems.at[k],
                                            recv_sem=recv_sems.at[k], device_id=to, device_id_type=MESH)

    def step_a(a):
        if callable(gs[a]):
            return [copy(a, 0, gs[a](xx, yy, 1 - c), ra[a].at[xx, yy], sib) for xx, yy in chips]
        return [copy(a, 0, gs[a].at[:, :, 1 - c], ra[a], sib),
                pltpu.make_async_copy(gs[a].at[:, :, c], loc[a], local_sems.at[a])]

    def finish_a(a):
        if callable(gs[a]):
            copy(a, 0, ra[a], ra[a], sib).wait()
            for xx, yy in chips:
                ra[a][xx, yy] = gs[a](xx, yy, c)[...] + ra[a][xx, yy]
        else:
            for cp in step_a(a):
                cp.wait()
            ra[a][...] = loc[a][...] + ra[a][...]

    def step_b(a):
        return copy(a, 1, s_b[a].at[0], r_b[a].at[0], xn), copy(a, 2, s_b[a].at[1], r_b[a].at[1], yn)

    def step_c(a):
        return copy(a, 3, s_c[a].at[0], r_c[a].at[0], yn), copy(a, 4, s_c[a].at[1], r_c[a].at[1], xn)

    def step_d(a, half):
        rows = fin[a].at[half]
        return copy(a, 5, rows, rows, sib)

    def start():
        for a in range(n):
            for cp in step_a(a):
                cp.start()

    def a_to_b():
        for a in range(n):
            finish_a(a)
            s_b[a][0] = ra[a][1 - x, :, :, :half_w[a]].astype(BF16)
            s_b[a][1] = ra[a][:, 1 - y, :, half_w[a]:].astype(BF16)
            for cp in step_b(a):
                cp.start()

    def b_to_c():
        for a in range(n):
            for cp in step_b(a):
                cp.wait()
            acc1[a][0] = ra[a][x, :, :, :half_w[a]] + r_b[a][0].astype(F32)
            acc1[a][1] = ra[a][:, y, :, half_w[a]:] + r_b[a][1].astype(F32)
            s_c[a][0] = acc1[a][0, 1 - y].astype(BF16)
            s_c[a][1] = acc1[a][1, 1 - x].astype(BF16)
            for cp in step_c(a):
                cp.start()

    def c_to_d():
        for a in range(n):
            for cp in step_c(a):
                cp.wait()
            fin[a][c, :, :half_w[a]] = acc1[a][0, y] + r_c[a][0].astype(F32)
            fin[a][c, :, half_w[a]:] = acc1[a][1, x] + r_c[a][1].astype(F32)
            step_d(a, c).start()

    def finish():
        for a in range(n):
            step_d(a, 1 - c).wait_recv()
            step_d(a, c).wait_send()
            pltpu.sync_copy(fin[a], outs[a])

    return start, a_to_b, b_to_c, c_to_d, finish


def _reduce_grads(gwt, g_ws, tiny):
    cw = gwt.shape[1] // RS_CHUNKS
    chunk_shape = (gwt.shape[0], cw)

    def body(g0, ws_in, tiny_in, *rest):
        outs, o_ws, o_tiny = rest[:RS_CHUNKS], rest[RS_CHUNKS], rest[RS_CHUNKS + 1]
        rest = rest[RS_CHUNKS + 2:]
        nb = len(RS_KINDS) * RS_CHUNKS
        sm, sa, sb, sc, acc_s, send_sems, recv_sems, local_sems = rest[nb:]
        blocks = [g0.at[:, :, :, :, pl.ds(j * cw, cw)] for j in range(RS_CHUNKS)]
        start, a_to_b, b_to_c, c_to_d, finish = _rs_stages(blocks, outs, rest[:nb], send_sems, recv_sems, local_sems,
                                                           [cw] * RS_CHUNKS)
        n_ws = ws_in.shape[0]
        sm[0:n_ws, :] = ws_in[...]
        sm[n_ws:, :] = tiny_in[...]
        x, y, c = lax.axis_index("x"), lax.axis_index("y"), lax.axis_index("c")

        def small(j, src, dst, to):
            k = RS_SEMS * RS_CHUNKS + j
            return pltpu.make_async_remote_copy(src_ref=src, dst_ref=dst, send_sem=send_sems.at[k],
                                                recv_sem=recv_sems.at[k], device_id=to, device_id_type=MESH)

        along_c, along_x, along_y = (small(0, sm, sa, (x, y, 1 - c)), small(1, acc_s, sb, (1 - x, y, c)),
                                     small(2, sb, sc, (x, 1 - y, c)))
        start()
        along_c.start()
        a_to_b()
        along_c.wait()
        acc_s[...] = sm[...] + sa[...]
        along_x.start()
        b_to_c()
        along_x.wait()
        sb[...] = acc_s[...] + sb[...]
        along_y.start()
        c_to_d()
        along_y.wait()
        o_ws[...] = sb[0:n_ws, :] + sc[0:n_ws, :]
        o_tiny[...] = sb[n_ws:, :] + sc[n_ws:, :]
        finish()

    vm = pl.BlockSpec(memory_space=pltpu.VMEM)
    hbm = pl.BlockSpec(memory_space=pl.ANY)
    small_shape = (g_ws.shape[0] + tiny.shape[0], LANES)
    scratch = _rs_scratch([chunk_shape] * RS_CHUNKS) + [pltpu.VMEM(small_shape, F32) for _ in range(5)]
    nsem = RS_SEMS * RS_CHUNKS + 3
    scratch += [pltpu.SemaphoreType.DMA((nsem,)), pltpu.SemaphoreType.DMA((nsem,)), pltpu.SemaphoreType.DMA((RS_CHUNKS,))]
    return _call(
        body, name="reduce_grads",
        out_shape=[jax.ShapeDtypeStruct((2, gwt.shape[0] // 8, cw), F32)] * RS_CHUNKS
        + [jax.ShapeDtypeStruct(g_ws.shape, F32), jax.ShapeDtypeStruct(tiny.shape, F32)],
        in_specs=[hbm, vm, vm],
        out_specs=[hbm] * RS_CHUNKS + [vm, vm],
        scratch_shapes=scratch,
        compiler_params=_params(),
    )(_rs_view(gwt), g_ws, tiny)


def _adam_update(w, g, m, v):
    nm = ADAM_B1 * m + (1.0 - ADAM_B1) * g
    nv = ADAM_B2 * v + (1.0 - ADAM_B2) * (g * g)
    m_hat = nm / (1.0 - ADAM_B1 ** ADAM_STEP)
    v_hat = nv / (1.0 - ADAM_B2 ** ADAM_STEP)
    return -ADAM_LR * (m_hat / (jnp.sqrt(v_hat) + ADAM_EPS) + ADAM_WD * w), nm, nv


def _adamw(w, g, m, v):
    rows, cols = w.shape
    tm = max(t for t in range(8, 257, 8) if rows % t == 0)
    parts = tuple(g) if isinstance(g, (tuple, list)) else (g,)
    n = len(parts)

    def body(w_ref, m_ref, v_ref, *refs):
        gv = jnp.concatenate([r[...] for r in refs[:n]], axis=1)
        d_ref, nm_ref, nv_ref = refs[n:n + 3]
        d_ref[...], nm_ref[...], nv_ref[...] = _adam_update(w_ref[...], gv, m_ref[...], v_ref[...])
        if n > 1:
            refs[n + 3][...] = gv

    blk = pl.BlockSpec((tm, cols), lambda i: (i, 0))
    nout = 3 if n == 1 else 4
    res = _call(
        body, name="adamw", grid=(rows // tm,),
        in_specs=[blk] * 3 + [pl.BlockSpec((tm, p.shape[1]), lambda i: (i, 0)) for p in parts], out_specs=[blk] * nout,
        out_shape=[jax.ShapeDtypeStruct((rows, cols), F32)] * nout,
        compiler_params=_params(),
    )(w, m, v, *parts)
    return (parts[0] if n == 1 else res[3], *res[:3])


def _adamw_tiny(tiny, weights, ms, vs):
    shapes = [w.shape for w in weights]
    n = len(weights)

    def grad_of(t_ref, k, shape):
        base = 8 * k
        if shape[1] > LANES:
            return [t_ref[base + j:base + j + 1, :] for j in range(shape[1] // LANES)]
        return [t_ref[base:base + shape[0], 0:shape[1]]]

    def body(t_ref, *refs):
        w_refs, m_refs, v_refs = refs[:n], refs[n:2 * n], refs[2 * n:3 * n]
        loss_ref, outs = refs[3 * n], refs[3 * n + 1:]
        loss_ref[...] = (0.5 / D_MODEL) * jnp.sum(t_ref[8 * n:8 * n + 8, :], keepdims=True)
        for k, shape in enumerate(shapes):
            g_ref, d_ref, nm_ref, nv_ref = outs[4 * k:4 * k + 4]
            for j, g in enumerate(grad_of(t_ref, k, shape)):
                cols = slice(j * LANES, (j + 1) * LANES) if shape[1] > LANES else slice(None)
                g_ref[:, cols] = g
                d_ref[:, cols], nm_ref[:, cols], nv_ref[:, cols] = _adam_update(
                    w_refs[k][:, cols], g, m_refs[k][:, cols], v_refs[k][:, cols])

    out_shape = [jax.ShapeDtypeStruct((1, 1), F32)]
    for shape in shapes:
        out_shape += [jax.ShapeDtypeStruct(shape, F32)] * 4
    return _call(body, name="adamw_tiny", out_shape=out_shape, compiler_params=_params())(tiny, *weights, *ms, *vs)


def _local_grads(x, mem, tgt, norm_gain, wt_sh, gmlp_v_gain, gmlp_w_s, gmlp_b, attn_q_gain, attn_k_gain,
                 mem_norm_gain, wkv_sh, mem_q_gain, mem_k_gain, wo_sh):
    vg = gmlp_v_gain.reshape(1, GMLP_W)
    bias_full = jnp.repeat(gmlp_b.T, HEAD_DIM, axis=1)
    gq2, gk2 = jnp.tile(attn_q_gain, (1, 2)), jnp.tile(attn_k_gain, (1, 2))
    qg4, kg4 = jnp.tile(mem_q_gain, (1, 4)), jnp.tile(mem_k_gain, (1, 4))

    proj, wt, wkv, wo = _gather_proj(x, norm_gain, wt_sh, wkv_sh, wo_sh)
    yg = _gmlp_fwd(proj, vg, gmlp_w_s, bias_full)
    o, lse, ya = _attn_fwd(proj, gq2, gk2)
    kv, hm = _mem_kv(mem, mem_norm_gain, wkv)
    om, ym = _mem_fwd(proj, kv, qg4, kg4)
    dy, dyc, g_wo, err2 = _out_loss(yg, ya, ym, x, tgt, wo)
    dmq, dmg, g_mq, g_mk, g_wkv, g_mng = _mem_bwd(proj, om, dyc, kv, hm, mem, mem_norm_gain, wkv, qg4, kg4)
    daq, dak, dav, dag, g_aq, g_ak, g_wkv_sh, g_wo_sh = _attn_bwd(proj, o, lse, dyc, gq2, gk2, g_wkv, g_wo)
    dg, g_ws, g_b, g_vg = _gmlp_bwd(proj, dyc, vg, gmlp_w_s, bias_full)
    gx, g_wt, g_ng = _proj_bwd(x, dy, norm_gain, wt, dg, daq, dak, dav, dag, dmq, dmg)

    tiny = jnp.concatenate([g_ng, g_vg, g_b, g_aq, g_ak, g_mng, g_mq, g_mk, err2], axis=0)
    return gx, g_wt, g_wkv_sh, g_wo_sh, g_ws.reshape(4 * CHUNK, CHUNK), tiny


def kernel(x, mem, norm_gain, w_in, gmlp_v_gain, gmlp_w_s, gmlp_b, attn_q_gain, attn_k_gain, mem_norm_gain, w_mem_kv, mem_q_gain, mem_k_gain, w_out, loss_target, m_norm_gain, m_w_in, m_gmlp_v_gain, m_gmlp_w_s, m_gmlp_b, m_attn_q_gain, m_attn_k_gain, m_mem_norm_gain, m_w_mem_kv, m_mem_q_gain, m_mem_k_gain, m_w_out, v_norm_gain, v_w_in, v_gmlp_v_gain, v_gmlp_w_s, v_gmlp_b, v_attn_q_gain, v_attn_k_gain, v_mem_norm_gain, v_w_mem_kv, v_mem_q_gain, v_mem_k_gain, v_w_out):
    gx, g_wt, g_wkv_sh, g_wo_sh, g_ws, tiny = _local_grads(
        x[0], mem[0], loss_target[0], norm_gain, w_in[0].T, gmlp_v_gain[0], gmlp_w_s[0], gmlp_b[0],
        attn_q_gain, attn_k_gain, mem_norm_gain, w_mem_kv[0], mem_q_gain, mem_k_gain, w_out[0])
    *g_wt_sh, g_ws, tiny = _reduce_grads(g_wt, g_ws, tiny)
    chip_block = lambda g: g.reshape(2 * g.shape[1], g.shape[2])
    g_wt_sh = tuple(chip_block(g) for g in g_wt_sh)
    g_wkv_sh, g_wo_sh = chip_block(g_wkv_sh), chip_block(g_wo_sh)

    ws = (norm_gain, w_in, gmlp_v_gain, gmlp_w_s, gmlp_b, attn_q_gain, attn_k_gain, mem_norm_gain, w_mem_kv,
          mem_q_gain, mem_k_gain, w_out)
    ms = (m_norm_gain, m_w_in, m_gmlp_v_gain, m_gmlp_w_s, m_gmlp_b, m_attn_q_gain, m_attn_k_gain, m_mem_norm_gain,
          m_w_mem_kv, m_mem_q_gain, m_mem_k_gain, m_w_out)
    vs = (v_norm_gain, v_w_in, v_gmlp_v_gain, v_gmlp_w_s, v_gmlp_b, v_attn_q_gain, v_attn_k_gain, v_mem_norm_gain,
          v_w_mem_kv, v_mem_q_gain, v_mem_k_gain, v_w_out)
    form = {1: lambda a: a[0].T, 3: lambda a: a.reshape(4 * CHUNK, CHUNK), 2: lambda a: a[0], 4: lambda a: a[0],
            8: lambda a: a[0], 11: lambda a: a[0]}
    back = {1: lambda a: a.T[None], 3: lambda a: a.reshape(1, 4, CHUNK, CHUNK), 2: lambda a: a[None],
            4: lambda a: a[None], 8: lambda a: a[None], 11: lambda a: a[None]}
    fwd = lambda t, i: form.get(i, lambda a: a)(t[i])
    out = {}
    for i, g in ((1, g_wt_sh), (3, g_ws), (8, g_wkv_sh), (11, g_wo_sh)):
        out[i] = _adamw(fwd(ws, i), g, fwd(ms, i), fwd(vs, i))
    res = _adamw_tiny(tiny, [fwd(ws, i) for i in TINY_ORDER], [fwd(ms, i) for i in TINY_ORDER],
                      [fwd(vs, i) for i in TINY_ORDER])
    for k, i in enumerate(TINY_ORDER):
        out[i] = res[1 + 4 * k:5 + 4 * k]
    leaves = [[back.get(i, lambda a: a)(out[i][j]) for i in range(12)] for j in range(4)]
    return (res[0].reshape(()), gx[None], *leaves[0], *leaves[1], *leaves[2], *leaves[3])
```

```python
import functools
import math

import jax
import jax.numpy as jnp
from jax import lax
from jax.experimental import pallas as pl
from jax.experimental.pallas import tpu as pltpu

F32 = jnp.float32
BF16 = jnp.bfloat16

SEQ = 4096
D_MODEL = 1024
HEAD_DIM = 64
LANES = 128
CHUNK = 128
GMLP_W, ATTN_W, MEM_W = 256, 512, 256
IN_W = 3 * GMLP_W + 4 * ATTN_W + 2 * MEM_W
MEM_LEN = 256
DILATIONS = (1, 4, 16)
EPS = 1e-6
QK_SCALE = 1.0 / math.sqrt(HEAD_DIM)
C_GU, C_GV, C_GG, C_AQ, C_AK, C_AV, C_AG, C_MQ, C_MG = 0, 256, 512, 768, 1280, 1792, 2304, 2816, 3072

ADAM_LR, ADAM_B1, ADAM_B2, ADAM_EPS, ADAM_WD, ADAM_STEP = 0.001, 0.9, 0.999, 1e-08, 0.01, 10

VMEM_LIMIT = 48 * 1024 * 1024
RS_CHUNKS = 4
ATTN_UNROLL = 4
MESH = pl.DeviceIdType.MESH

TINY_ORDER = (0, 2, 4, 5, 6, 7, 9, 10)


def _call(body, **kw):
    return pl.pallas_call(body, **kw)


def _params(**kw):
    return pltpu.CompilerParams(vmem_limit_bytes=VMEM_LIMIT, **kw)


def _dot(a, b):
    return jnp.dot(a, b, preferred_element_type=F32)


def _dot_nt(a, b):
    return lax.dot_general(a, b, (((1,), (1,)), ((), ())), preferred_element_type=F32)


def _dot_tn(a, b):
    return lax.dot_general(a, b, (((0,), (0,)), ((), ())), preferred_element_type=F32)


def _head_blockdiag():
    r = lax.shift_right_logical(lax.broadcasted_iota(jnp.int32, (LANES, LANES), 0), 6)
    c = lax.shift_right_logical(lax.broadcasted_iota(jnp.int32, (LANES, LANES), 1), 6)
    return jnp.where(r == c, 1.0, 0.0).astype(BF16)


def _headsum(v, bd):
    hi = v.astype(BF16)
    lo = (v - hi.astype(F32)).astype(BF16)
    return _dot(hi, bd) + _dot(lo, bd)


def _lo_mask(rows):
    return lax.broadcasted_iota(jnp.int32, (rows, LANES), 1) < HEAD_DIM


def _sigmoid(x):
    return 1.0 / (1.0 + jnp.exp(-x))


def _fold_heads(v):
    return v + pltpu.roll(v, HEAD_DIM, 1)


def _put_rows(ref, vec, accumulate=False):
    for j in range(vec.shape[1] // LANES):
        piece = vec[:, j * LANES:(j + 1) * LANES]
        ref[j:j + 1, :] = ref[j:j + 1, :] + piece if accumulate else piece


def _gather_proj(x, gain, wt_sh, *ride_along):
    tm = 512
    nrow = SEQ // tm
    nride = len(ride_along)
    widths = (768, 896, 768, 896)
    pair = 2 * wt_sh.shape[0]
    assert pair % LANES == 0 and sum(widths[:2]) == pair

    def body(x_ref, g_ref, wt_sh_ref, *rest):
        shards, rest = rest[:nride], rest[nride:]
        proj_hbm, wt_hbm, gathered = rest[0], rest[1], rest[2:2 + nride]
        h_scr, land, res = rest[2 + nride:5 + nride]
        lands, (send0, recv0, send1, recv1, out_sems, copy_sems) = rest[5 + nride:5 + 2 * nride], rest[5 + 2 * nride:]
        u, i = pl.program_id(0), pl.program_id(1)
        cx_, cy_ = lax.axis_index("x"), lax.axis_index("y")
        (send_own, pass_on_neighbours, pass_on_diagonal, _), (y_complete, x_complete, diagonal_complete, sends_done) = (
            _gather_stages((wt_sh_ref,), (land,), send0, recv0))
        ride, _ = _gather_stages(shards, lands, send1, recv1)
        first = lambda k: (u == k) & (i == 0)
        last = (u == 3) & (i == nrow - 1)
        copies = [pltpu.make_async_copy(land, wt_hbm, copy_sems.at[0])] + [
            pltpu.make_async_copy(src, dst, copy_sems.at[1 + k]) for k, (src, dst) in enumerate(zip(lands, gathered))]

        pl.when(first(0))(send_own)

        @pl.when(u == 0)
        def _():
            xv = x_ref[...]
            ms = jnp.mean(xv * xv, axis=-1, keepdims=True)
            h_scr[pl.ds(pl.multiple_of(i * tm, tm), tm), :] = (xv * lax.rsqrt(ms + EPS) * g_ref[...]).astype(BF16)

        @pl.when(first(1))
        def _():
            pass_on_neighbours()
            ride[0]()
            y_complete()

        @pl.when(first(2))
        def _():
            x_complete()
            pass_on_diagonal()
            ride[1]()

        @pl.when(first(3))
        def _():
            diagonal_complete()
            copies[0].start()
            ride[2]()

        col0 = (pair * cx_ + 896 * cy_, pair * cx_ + 768 * (1 - cy_),
                pair * (1 - cx_) + 896 * cy_, pair * (1 - cx_) + 768 * (1 - cy_))
        slot = i % 2
        rows = pl.ds(pl.multiple_of(i * tm, tm), tm)

        def writeback(k, rows_):
            c0 = pl.multiple_of(col0[k], LANES)
            return pltpu.make_async_copy(res.at[slot, :, pl.ds(0, widths[k])], proj_hbm.at[rows_, pl.ds(c0, widths[k])],
                                         out_sems.at[slot])

        for k in range(4):
            @pl.when(u == k)
            def _(k=k):
                pl.when(i >= 2)(writeback(k, rows).wait)
                if k > 0:
                    pl.when(i < 2)(writeback(k - 1, rows).wait)
                w_rows = land[pl.ds(pl.multiple_of(col0[k], LANES), widths[k]), :]
                res[slot, :, 0:widths[k]] = _dot_nt(h_scr[rows, :], w_rows)
                writeback(k, rows).start()

        @pl.when(last)
        def _():
            sends_done()
            ride[3]()
            for cp in copies[1:]:
                cp.start()
            for cp in copies:
                cp.wait()
            pltpu.make_async_copy(res.at[0, :, pl.ds(0, widths[3])], proj_hbm.at[rows, pl.ds(0, widths[3])], out_sems.at[0]).wait()
            pltpu.make_async_copy(res.at[1, :, pl.ds(0, widths[3])], proj_hbm.at[rows, pl.ds(0, widths[3])], out_sems.at[1]).wait()

    full = [jax.ShapeDtypeStruct((4 * a.shape[0], a.shape[1]), BF16) for a in (wt_sh,) + ride_along]
    hbm = pl.BlockSpec(memory_space=pl.ANY)
    const = lambda a: pl.BlockSpec(a.shape, lambda u, i: (0, 0))
    return _call(
        body, name="gather_proj", grid=(4, nrow),
        in_specs=[pl.BlockSpec((tm, D_MODEL), lambda u, i: (jnp.where(u == 0, i, nrow - 1), 0)),
                  pl.BlockSpec((1, D_MODEL), lambda u, i: (0, 0)), const(wt_sh)] + [const(a) for a in ride_along],
        out_specs=[hbm] * (2 + nride),
        out_shape=[jax.ShapeDtypeStruct((SEQ, IN_W), F32)] + full,
        scratch_shapes=[pltpu.VMEM((SEQ, D_MODEL), BF16), pltpu.VMEM(full[0].shape, BF16), pltpu.VMEM((2, tm, max(widths)), F32)]
        + [pltpu.VMEM(s.shape, BF16) for s in full[1:]]
        + [pltpu.SemaphoreType.DMA((AG_SEMS,)), pltpu.SemaphoreType.DMA((AG_SEMS,)),
           pltpu.SemaphoreType.DMA((AG_SEMS * nride,)), pltpu.SemaphoreType.DMA((AG_SEMS * nride,)),
           pltpu.SemaphoreType.DMA((2,)), pltpu.SemaphoreType.DMA((1 + nride,))],
        compiler_params=_params(),
    )(x, gain, wt_sh, *ride_along)


def _gmlp_weights(w_ref):
    ti = lax.broadcasted_iota(jnp.int32, (CHUNK, CHUNK), 0)
    si = lax.broadcasted_iota(jnp.int32, (CHUNK, CHUNK), 1)
    tril = si <= ti
    return tril, [jnp.where(tril, w_ref[h], 0.0).astype(BF16) for h in range(4)]


def _gmlp_fwd(proj, vgain, w_s, bias_full):
    tm = 512

    def body(p_ref, vg_ref, w_ref, b_ref, y_ref):
        bd = _head_blockdiag()
        lo = _lo_mask(CHUNK)
        _, wm = _gmlp_weights(w_ref)
        units = [(pl.ds(c * CHUNK, CHUNK), p) for c in range(tm // CHUNK) for p in range(2)]
        col = lambda c0, p: slice(c0 + p * LANES, c0 + (p + 1) * LANES)
        vs = [p_ref[rows, col(C_GV, p)] for rows, p in units]
        rs = [lax.rsqrt(_headsum(v * v, bd) * (1.0 / HEAD_DIM) + EPS) for v in vs]
        vns = [(v * r * vg_ref[:, col(0, p)]).astype(BF16) for v, r, (_, p) in zip(vs, rs, units)]
        sps = [jnp.where(lo, _dot(wm[2 * p], vn), _dot(wm[2 * p + 1], vn)) + b_ref[:, col(0, p)] for vn, (_, p) in zip(vns, units)]
        for sp, (rows, p) in zip(sps, units):
            gt = p_ref[rows, col(C_GG, p)]
            y_ref[rows, col(0, p)] = (p_ref[rows, col(C_GU, p)] * sp * (gt * _sigmoid(gt))).astype(BF16)

    return _call(
        body, name="gmlp_fwd", grid=(SEQ // tm,),
        in_specs=[pl.BlockSpec((tm, 3 * GMLP_W), lambda i: (i, 0)),
                  pl.BlockSpec((1, GMLP_W), lambda i: (0, 0)),
                  pl.BlockSpec((4, CHUNK, CHUNK), lambda i: (0, 0, 0)),
                  pl.BlockSpec((CHUNK, GMLP_W), lambda i: (0, 0))],
        out_specs=pl.BlockSpec((tm, GMLP_W), lambda i: (i, 0)),
        out_shape=jax.ShapeDtypeStruct((SEQ, GMLP_W), BF16),
        compiler_params=_params(),
    )(proj, vgain, w_s, bias_full)


def _gmlp_bwd(proj, dyc, vgain, w_s, bias_full):
    tm = 512
    nsteps = SEQ // tm

    def body(p_ref, dy_ref, vg_ref, w_ref, b_ref, dg_ref, gw_ref, gb_ref, gv_ref):
        i = pl.program_id(0)
        bd = _head_blockdiag()
        lo = _lo_mask(CHUNK)
        tril, wm = _gmlp_weights(w_ref)
        ri = lax.broadcasted_iota(jnp.int32, (16, LANES), 0)
        li = lax.broadcasted_iota(jnp.int32, (16, LANES), 1)
        head_rows = [jnp.where(((ri == 2 * p) & (li < HEAD_DIM)) | ((ri == 2 * p + 1) & (li >= HEAD_DIM)), 1.0, 0.0).astype(BF16)
                     for p in range(2)]

        @pl.when(i == 0)
        def _():
            gw_ref[...] = jnp.zeros_like(gw_ref)
            gb_ref[...] = jnp.zeros_like(gb_ref)
            gv_ref[...] = jnp.zeros_like(gv_ref)

        units = [(pl.ds(c * CHUNK, CHUNK), p) for c in range(tm // CHUNK) for p in range(2)]
        col = lambda c0, p: slice(c0 + p * LANES, c0 + (p + 1) * LANES)
        vs = [p_ref[rows, col(C_GV, p)] for rows, p in units]
        rs = [lax.rsqrt(_headsum(v * v, bd) * (1.0 / HEAD_DIM) + EPS) for v in vs]
        zs = [v * r for v, r in zip(vs, rs)]
        vns = [(z * vg_ref[:, col(0, p)]).astype(BF16) for z, (_, p) in zip(zs, units)]
        sps = [jnp.where(lo, _dot(wm[2 * p], vn), _dot(wm[2 * p + 1], vn)) + b_ref[:, col(0, p)] for vn, (_, p) in zip(vns, units)]
        dsps = []
        for sp, (rows, p) in zip(sps, units):
            u = p_ref[rows, col(C_GU, p)]
            gt = p_ref[rows, col(C_GG, p)]
            dy = dy_ref[rows, col(0, p)]
            sg = _sigmoid(gt)
            sl = gt * sg
            dg_ref[rows, col(C_GU, p)] = (dy * sp * sl).astype(BF16)
            dg_ref[rows, col(C_GG, p)] = (dy * u * sp * (sg * (1.0 + gt * (1.0 - sg)))).astype(BF16)
            dsps.append(dy * u * sl)
        dspbs = [dsp.astype(BF16) for dsp in dsps]
        dvns = [jnp.where(lo, _dot_tn(wm[2 * p], dspb), _dot_tn(wm[2 * p + 1], dspb)) for dspb, (_, p) in zip(dspbs, units)]
        gws = [(_dot_nt(jnp.where(lo, dsp, 0.0).astype(BF16), vn), _dot_nt(jnp.where(lo, 0.0, dsp).astype(BF16), vn))
               for dsp, vn in zip(dsps, vns)]
        gbs = [(_dot_nt(head_rows[p], dspb) + _dot_nt(head_rows[p], (dsp - dspb.astype(F32)).astype(BF16)))[0:8]
               for dsp, dspb, (_, p) in zip(dsps, dspbs, units)]
        for p in range(2):
            mine = [n for n, (_, q) in enumerate(units) if q == p]
            gw_ref[2 * p] += sum(gws[n][0] for n in mine)
            gw_ref[2 * p + 1] += sum(gws[n][1] for n in mine)
            gvp = sum(jnp.sum(dvns[n] * zs[n], axis=0, keepdims=True) for n in mine)
            gv_ref[2 * p:2 * p + 1, :] += gvp
            gv_ref[2 * p + 1:2 * p + 2, :] += pltpu.roll(gvp, HEAD_DIM, 1)
        gb_ref[...] += sum(gbs)
        for dvn, z, r, (rows, p) in zip(dvns, zs, rs, units):
            dz = dvn * vg_ref[:, col(0, p)]
            dg_ref[rows, col(C_GV, p)] = (r * (dz - z * (_headsum(dz * z, bd) * (1.0 / HEAD_DIM)))).astype(BF16)

        @pl.when(i == nsteps - 1)
        def _():
            for h in range(4):
                gw_ref[h] = jnp.where(tril, gw_ref[h], 0.0)

    return _call(
        body, name="gmlp_bwd", grid=(nsteps,),
        in_specs=[pl.BlockSpec((tm, 3 * GMLP_W), lambda i: (i, 0)),
                  pl.BlockSpec((tm, GMLP_W), lambda i: (i, 0)),
                  pl.BlockSpec((1, GMLP_W), lambda i: (0, 0)),
                  pl.BlockSpec((4, CHUNK, CHUNK), lambda i: (0, 0, 0)),
                  pl.BlockSpec((CHUNK, GMLP_W), lambda i: (0, 0))],
        out_specs=[pl.BlockSpec((tm, 3 * GMLP_W), lambda i: (i, 0)),
                   pl.BlockSpec((4, CHUNK, CHUNK), lambda i: (0, 0, 0)),
                   pl.BlockSpec((8, LANES), lambda i: (0, 0)),
                   pl.BlockSpec((8, LANES), lambda i: (0, 0))],
        out_shape=[jax.ShapeDtypeStruct((SEQ, 3 * GMLP_W), BF16),
                   jax.ShapeDtypeStruct((4, CHUNK, CHUNK), F32),
                   jax.ShapeDtypeStruct((8, LANES), F32),
                   jax.ShapeDtypeStruct((8, LANES), F32)],
        compiler_params=_params(),
    )(proj, dyc, vgain, w_s, bias_full)


def _band_masks():
    qi = lax.broadcasted_iota(jnp.int32, (CHUNK, 2 * CHUNK), 0)
    kj = lax.broadcasted_iota(jnp.int32, (CHUNK, 2 * CHUNK), 1)
    valid2 = ((kj < CHUNK) & (kj >= qi)) | ((kj >= CHUNK) & (kj - CHUNK <= qi))
    q1 = lax.broadcasted_iota(jnp.int32, (CHUNK, CHUNK), 0)
    k1 = lax.broadcasted_iota(jnp.int32, (CHUNK, CHUNK), 1)
    return k1 <= q1, valid2


def _stack_heads(v, lo):
    return jnp.concatenate([jnp.where(lo, v, 0.0), jnp.where(lo, 0.0, v)], axis=0).astype(BF16)


def _rows_of(ref, start, d):
    if d == 1:
        return ref.at[pl.ds(start if isinstance(start, int) else pl.multiple_of(start, CHUNK), CHUNK), :]
    return ref.at[pl.ds(start, CHUNK, stride=d), :]


def _unrolled(lo, hi, unroll, run):
    groups = (hi - lo) // unroll
    if groups:
        def body(g, carry):
            run([lo + g * unroll + t for t in range(unroll)])
            return carry

        lax.fori_loop(0, groups, body, 0)
    if lo + groups * unroll < hi:
        run(range(lo + groups * unroll, hi))


def _for_blocks(d, group_fn, unroll):
    nblk = SEQ // CHUNK
    sh = d.bit_length() - 1

    def first(j):
        return (j * CHUNK if d == 1 else j, None)

    def rest(j):
        start = (j & (d - 1)) + (j >> sh) * (CHUNK * d)
        return (start, start - CHUNK * d)

    _unrolled(0, d, unroll, lambda js: group_fn(d, [first(j) for j in js]))
    _unrolled(d, nblk, unroll, lambda js: group_fn(d, [rest(j) for j in js]))


def _attn_fwd(proj, gq2, gk2):
    tn = 512

    def body(q_ref, k_ref, v_ref, g_ref, gq_ref, gk_ref, o_ref, l_ref, ya_ref, qn_ref, kn_ref):
        bd = _head_blockdiag()
        lo = _lo_mask(CHUNK)
        valid1, valid2 = _band_masks()

        def norm(t, carry):
            rows = pl.ds(pl.multiple_of(t * tn, tn), tn)
            q = q_ref[rows, :]
            qn_ref[rows, :] = q * lax.rsqrt(_headsum(q * q, bd) * (1.0 / HEAD_DIM) + EPS) * (gq_ref[...] * QK_SCALE)
            k = k_ref[rows, :]
            kn_ref[rows, :] = k * lax.rsqrt(_headsum(k * k, bd) * (1.0 / HEAD_DIM) + EPS) * gk_ref[...]
            return carry

        lax.fori_loop(0, SEQ // tn, norm, 0)

        def load_kv(ref, d, start, prev):
            own = _rows_of(ref, start, d)[...]
            if prev is None:
                return own.astype(BF16)
            return jnp.concatenate([_rows_of(ref, prev, d)[...], own], axis=0).astype(BF16)

        def group(d, blocks):
            valid = valid1 if blocks[0][1] is None else valid2
            valid = jnp.concatenate([valid, valid], axis=0)
            qs = [_rows_of(qn_ref, start, d)[...] for start, _ in blocks]
            ks = [load_kv(kn_ref, d, start, prev) for start, prev in blocks]
            vs = [load_kv(v_ref, d, start, prev) for start, prev in blocks]
            ss = [_dot_nt(_stack_heads(q, lo), k) for q, k in zip(qs, ks)]
            ms, ps, ls = [], [], []
            for s in ss:
                s = jnp.where(valid, s, -jnp.inf)
                m = jnp.max(s, axis=-1, keepdims=True)
                p = jnp.exp(s - m)
                ms.append(m)
                ls.append(jnp.sum(p, axis=-1, keepdims=True))
                ps.append(p.astype(BF16))
            os_ = [_dot(p, v) for p, v in zip(ps, vs)]
            for b, (start, _) in enumerate(blocks):
                on = os_[b] * (1.0 / ls[b])
                ln = ms[b] + jnp.log(ls[b])
                ob = jnp.where(lo, on[:CHUNK], on[CHUNK:])
                lb = jnp.where(lo, ln[:CHUNK], ln[CHUNK:])
                o_rows = _rows_of(o_ref, start, d)
                l_rows = _rows_of(l_ref, start, d)
                if d != DILATIONS[0]:
                    lold = l_rows[...]
                    mx = jnp.maximum(lold, lb)
                    ea = jnp.exp(lold - mx)
                    eb = jnp.exp(lb - mx)
                    inv = 1.0 / (ea + eb)
                    ob = o_rows[...] * (ea * inv) + ob * (eb * inv)
                    lb = mx + jnp.log(ea + eb)
                o_rows[...] = ob
                l_rows[...] = lb

        for d in DILATIONS:
            _for_blocks(d, group, ATTN_UNROLL)

        def fin(t, carry):
            rows = pl.ds(pl.multiple_of(t * tn, tn), tn)
            g = g_ref[rows, :]
            ya_ref[rows, :] = (o_ref[rows, :] * (g * _sigmoid(g))).astype(BF16)
            return carry

        lax.fori_loop(0, SEQ // tn, fin, 0)

    col = lambda c0: pl.BlockSpec((SEQ, LANES), lambda p: (0, c0 // LANES + p))
    vec = pl.BlockSpec((1, LANES), lambda p: (0, 0))
    out = pl.BlockSpec((SEQ, LANES), lambda p: (0, p))
    return _call(
        body, name="attn_fwd", grid=(ATTN_W // LANES,),
        in_specs=[col(C_AQ), col(C_AK), col(C_AV), col(C_AG), vec, vec],
        out_specs=[out, out, out],
        out_shape=[jax.ShapeDtypeStruct((SEQ, ATTN_W), F32), jax.ShapeDtypeStruct((SEQ, ATTN_W), F32),
                   jax.ShapeDtypeStruct((SEQ, ATTN_W), BF16)],
        scratch_shapes=[pltpu.VMEM((SEQ, LANES), F32), pltpu.VMEM((SEQ, LANES), F32)],
        compiler_params=_params(),
    )(proj, proj, proj, proj, gq2, gk2)


def _attn_bwd(proj, o, lse, dyc, gq2, gk2, *ride_along):
    tn = 512
    npairs = ATTN_W // LANES
    nride = len(ride_along)
    nbufs = nride * len(RS_KINDS)

    def body(proj_hbm, o_hbm, l_hbm, dyc_hbm, gq_ref, gk_ref, *rest):
        ride_in, rest = rest[:nride], rest[nride:]
        dq_ref, dk_ref, dv_ref, dgt_ref, gqg_ref, gkg_ref = rest[:6]
        ride_out, rest = rest[6:6 + nride], rest[6 + nride:]
        qb_, kb_, vb_, gb_, ob_, lb_, yb_, dkb_, dvb_, sems = rest[:10]
        rs_bufs, (send_sems, recv_sems, local_sems) = rest[10:10 + nbufs], rest[10 + nbufs:]
        rs_stage = _rs_stages(ride_in, ride_out, rs_bufs, send_sems, recv_sems, local_sems, [g.shape[1] for g in ride_along])
        pair = pl.program_id(0)
        for step in range(npairs):
            pl.when(pair == step)(rs_stage[step])
        bd = _head_blockdiag()
        lo = _lo_mask(CHUNK)
        lo2 = lax.broadcasted_iota(jnp.int32, (2 * CHUNK, LANES), 1) < HEAD_DIM
        valid1, valid2 = _band_masks()
        gqs = gq_ref[...] * QK_SCALE
        gk = gk_ref[...]

        def pcol(c0):
            return proj_hbm.at[:, pl.ds(pl.multiple_of(c0 + pair * LANES, LANES), LANES)]

        def acol(hbm, c0=0):
            return hbm.at[:, pl.ds(pl.multiple_of(c0 + pair * LANES, LANES), LANES)]

        loads = [pltpu.make_async_copy(src, dst, sems.at[n]) for n, (src, dst) in enumerate((
            (pcol(C_AQ), qb_), (pcol(C_AK), kb_), (pcol(C_AG), gb_), (acol(o_hbm), ob_),
            (acol(dyc_hbm, GMLP_W), yb_), (pcol(C_AV), vb_), (acol(l_hbm), lb_)))]
        for cp in loads:
            cp.start()

        @pl.when(pair == 0)
        def _():
            gqg_ref[...] = jnp.zeros_like(gqg_ref)
            gkg_ref[...] = jnp.zeros_like(gkg_ref)

        def pre_qk(t, carry):
            rows = pl.ds(pl.multiple_of(t * tn, tn), tn)
            zero = jnp.zeros((tn, LANES), F32)
            dkb_[rows, :] = zero
            dvb_[rows, :] = zero
            q = qb_[rows, :]
            qb_[rows, :] = q * lax.rsqrt(_headsum(q * q, bd) * (1.0 / HEAD_DIM) + EPS) * gqs
            k = kb_[rows, :]
            kb_[rows, :] = k * lax.rsqrt(_headsum(k * k, bd) * (1.0 / HEAD_DIM) + EPS) * gk
            return carry

        def pre_gate(t, carry):
            rows = pl.ds(pl.multiple_of(t * tn, tn), tn)
            g = gb_[rows, :]
            ov = ob_[rows, :]
            dya = yb_[rows, :]
            sg = _sigmoid(g)
            dgt_ref[rows, :] = (dya * ov * (sg * (1.0 + g * (1.0 - sg)))).astype(BF16)
            do = dya * (g * sg)
            yb_[rows, :] = do
            ob_[rows, :] = _headsum(do * ov, bd)
            gb_[rows, :] = jnp.zeros((tn, LANES), F32)
            return carry

        loads[0].wait()
        loads[1].wait()
        lax.fori_loop(0, SEQ // tn, pre_qk, 0)
        for cp in loads[2:5]:
            cp.wait()
        lax.fori_loop(0, SEQ // tn, pre_gate, 0)
        loads[5].wait()
        loads[6].wait()

        def load_kv(ref, d, start, prev):
            own = _rows_of(ref, start, d)[...]
            if prev is None:
                return own.astype(BF16)
            return jnp.concatenate([_rows_of(ref, prev, d)[...], own], axis=0).astype(BF16)

        def group(d, blocks):
            first = blocks[0][1] is None
            valid, lok = (valid1, lo) if first else (valid2, lo2)
            chains = [(b, h) for b in range(len(blocks)) for h in range(2)]
            mask = lambda h: lo if h == 0 else ~lo
            qs = [_rows_of(qb_, start, d)[...] for start, _ in blocks]
            dos = [_rows_of(yb_, start, d)[...] for start, _ in blocks]
            lvs = [_rows_of(lb_, start, d)[...] for start, _ in blocks]
            dls = [_rows_of(ob_, start, d)[...] for start, _ in blocks]
            ks = [load_kv(kb_, d, start, prev) for start, prev in blocks]
            vs = [load_kv(vb_, d, start, prev) for start, prev in blocks]
            qbs = [q.astype(BF16) for q in qs]
            dobs = [do.astype(BF16) for do in dos]
            ss = [_dot_nt(jnp.where(mask(h), qs[b], 0.0).astype(BF16), ks[b]) for b, h in chains]
            dps = [_dot_nt(jnp.where(mask(h), dos[b], 0.0).astype(BF16), vs[b]) for b, h in chains]
            pbs, dss = [], []
            for s, dp, (b, h) in zip(ss, dps, chains):
                hc = h * HEAD_DIM
                p = jnp.exp(jnp.where(valid, s, -jnp.inf) - lvs[b][:, hc:hc + 1])
                pbs.append(p.astype(BF16))
                dss.append((p * (dp - dls[b][:, hc:hc + 1])).astype(BF16))
            dqs = [_dot(ds, ks[b]) for ds, (b, h) in zip(dss, chains)]
            dks = [_dot_tn(ds, qbs[b]) for ds, (b, h) in zip(dss, chains)]
            dvs = [_dot_tn(p, dobs[b]) for p, (b, h) in zip(pbs, chains)]
            for b, (start, prev) in enumerate(blocks):
                c0, c1 = 2 * b, 2 * b + 1
                dq_rows = _rows_of(gb_, start, d)
                dq_rows[...] = dq_rows[...] + jnp.where(lo, dqs[c0], dqs[c1])
                dkc = jnp.where(lok, dks[c0], dks[c1])
                dvc = jnp.where(lok, dvs[c0], dvs[c1])
                spans = ((start, slice(0, CHUNK)),) if first else ((prev, slice(0, CHUNK)), (start, slice(CHUNK, 2 * CHUNK)))
                for st, sl in spans:
                    dk_rows = _rows_of(dkb_, st, d)
                    dk_rows[...] = dk_rows[...] + dkc[sl]
                    dv_rows = _rows_of(dvb_, st, d)
                    dv_rows[...] = dv_rows[...] + dvc[sl]

        for d in DILATIONS:
            _for_blocks(d, group, ATTN_UNROLL)

        reloads = [pltpu.make_async_copy(pcol(C_AQ), vb_, sems.at[0]), pltpu.make_async_copy(pcol(C_AK), lb_, sems.at[1])]
        for cp in reloads:
            cp.start()
        for cp in reloads:
            cp.wait()

        def post(t, carry):
            gq_acc, gk_acc = carry
            rows = pl.ds(pl.multiple_of(t * tn, tn), tn)
            outs = []
            for raw_, acc_, gain in ((vb_, gb_, gqs), (lb_, dkb_, gk)):
                a = raw_[rows, :]
                r = lax.rsqrt(_headsum(a * a, bd) * (1.0 / HEAD_DIM) + EPS)
                z = a * r
                dn = acc_[rows, :]
                dz = dn * gain
                outs.append((r * (dz - z * (_headsum(dz * z, bd) * (1.0 / HEAD_DIM))), jnp.sum(dn * z, axis=0, keepdims=True)))
            dq_ref[rows, :] = outs[0][0].astype(BF16)
            dk_ref[rows, :] = outs[1][0].astype(BF16)
            dv_ref[rows, :] = dvb_[rows, :].astype(BF16)
            return gq_acc + outs[0][1] * QK_SCALE, gk_acc + outs[1][1]

        zero = jnp.zeros((1, LANES), F32)
        gq_acc, gk_acc = lax.fori_loop(0, SEQ // tn, post, (zero, zero))
        gqg_ref[0:1, :] += gq_acc
        gkg_ref[0:1, :] += gk_acc

        @pl.when(pair == npairs - 1)
        def _():
            gqg_ref[0:1, :] = _fold_heads(gqg_ref[0:1, :])
            gkg_ref[0:1, :] = _fold_heads(gkg_ref[0:1, :])
            rs_stage[npairs]()

    hbm = pl.BlockSpec(memory_space=pl.ANY)
    vec = pl.BlockSpec((1, LANES), lambda p: (0, 0))
    blk8 = pl.BlockSpec((8, LANES), lambda p: (0, 0))
    out = pl.BlockSpec((SEQ, LANES), lambda p: (0, p))
    big = jax.ShapeDtypeStruct((SEQ, ATTN_W), BF16)
    nsem = RS_SEMS * nride
    return _call(
        body, name="attn_bwd", grid=(npairs,),
        in_specs=[hbm, hbm, hbm, hbm, vec, vec] + [hbm] * nride,
        out_specs=[out, out, out, out, blk8, blk8] + [hbm] * nride,
        out_shape=[big, big, big, big, jax.ShapeDtypeStruct((8, LANES), F32), jax.ShapeDtypeStruct((8, LANES), F32)]
        + [jax.ShapeDtypeStruct((2, g.shape[0] // 8, g.shape[1]), F32) for g in ride_along],
        scratch_shapes=[pltpu.VMEM((SEQ, LANES), F32) for _ in range(9)] + [pltpu.SemaphoreType.DMA((7,))]
        + _rs_scratch([g.shape for g in ride_along]) + [pltpu.SemaphoreType.DMA((nsem,)), pltpu.SemaphoreType.DMA((nsem,)),
                                     pltpu.SemaphoreType.DMA((nride,))],
        compiler_params=_params(),
    )(proj, o, lse, dyc, gq2, gk2, *[_rs_view(g) for g in ride_along])


def _mem_kv(mem, gain, wkv):
    def body(m_ref, g_ref, w_ref, kv_ref, hm_ref):
        mv = m_ref[...]
        ms = jnp.mean(mv * mv, axis=-1, keepdims=True)
        hm = (mv * lax.rsqrt(ms + EPS) * g_ref[...]).astype(BF16)
        hm_ref[...] = hm
        kv_ref[...] = _dot(hm, w_ref[...])

    return _call(
        body, name="mem_kv",
        out_shape=[jax.ShapeDtypeStruct((MEM_LEN, 2 * MEM_W), F32), jax.ShapeDtypeStruct((MEM_LEN, D_MODEL), BF16)],
        compiler_params=_params(),
    )(mem, gain, wkv)


def _mem_keys(kv_ref, kg_ref, bd, p):
    mk = kv_ref[:, p * LANES:(p + 1) * LANES]
    r = lax.rsqrt(_headsum(mk * mk, bd) * (1.0 / HEAD_DIM) + EPS)
    z = mk * r
    mkn = (z * kg_ref[:, p * LANES:(p + 1) * LANES]).astype(BF16)
    mvp = kv_ref[:, MEM_W + p * LANES:MEM_W + (p + 1) * LANES].astype(BF16)
    return mkn, mvp, r, z


def _mem_fwd(proj, kv, qg4, kg4):
    tm = 512

    def body(q_ref, g_ref, kv_ref, qg_ref, kg_ref, om_ref, ym_ref):
        bd = _head_blockdiag()
        lo = _lo_mask(tm)
        keys, qns = [], []
        for p in range(2):
            cs = slice(p * LANES, (p + 1) * LANES)
            keys.append(_mem_keys(kv_ref, kg_ref, bd, p)[:2])
            q = q_ref[:, cs]
            qns.append(q * lax.rsqrt(_headsum(q * q, bd) * (1.0 / HEAD_DIM) + EPS) * (qg_ref[:, cs] * QK_SCALE))
        chains = [(p, h) for p in range(2) for h in range(2)]
        ss = [_dot_nt(jnp.where(lo if h == 0 else ~lo, qns[p], 0.0).astype(BF16), keys[p][0]) for p, h in chains]
        es = [jnp.exp(s - jnp.max(s, axis=-1, keepdims=True)) for s in ss]
        os_ = [_dot(e.astype(BF16), keys[p][1]) for e, (p, h) in zip(es, chains)]
        res = [o * (1.0 / jnp.sum(e, axis=-1, keepdims=True)) for o, e in zip(os_, es)]
        for p in range(2):
            cs = slice(p * LANES, (p + 1) * LANES)
            ov = jnp.where(lo, res[2 * p], res[2 * p + 1])
            g = g_ref[:, cs]
            om_ref[:, cs] = ov
            ym_ref[:, cs] = (ov * (g * _sigmoid(g))).astype(BF16)

    vec = pl.BlockSpec((1, MEM_W), lambda i: (0, 0))
    return _call(
        body, name="mem_fwd", grid=(SEQ // tm,),
        in_specs=[pl.BlockSpec((tm, MEM_W), lambda i: (i, C_MQ // MEM_W)),
                  pl.BlockSpec((tm, MEM_W), lambda i: (i, C_MG // MEM_W)),
                  pl.BlockSpec((MEM_LEN, 2 * MEM_W), lambda i: (0, 0)), vec, vec],
        out_specs=[pl.BlockSpec((tm, MEM_W), lambda i: (i, 0)), pl.BlockSpec((tm, MEM_W), lambda i: (i, 0))],
        out_shape=[jax.ShapeDtypeStruct((SEQ, MEM_W), F32), jax.ShapeDtypeStruct((SEQ, MEM_W), BF16)],
        compiler_params=_params(),
    )(proj, proj, kv, qg4, kg4)


def _mem_bwd(proj, om, dyc, kv, hm, mem, mgain, wkv, qg4, kg4):
    tm = 512
    nsteps = SEQ // tm

    def body(q_ref, g_ref, om_ref, dy_ref, kv_ref, hm_ref, mem_ref, mg_ref, w_ref, qg_ref, kg_ref,
             dq_ref, dgt_ref, gqg_ref, gkg_ref, gw_ref, gmg_ref, dmk_ref, dmv_ref, gq_acc):
        i = pl.program_id(0)
        bd = _head_blockdiag()
        lo = _lo_mask(tm)
        lom = _lo_mask(MEM_LEN)

        @pl.when(i == 0)
        def _():
            dmk_ref[...] = jnp.zeros_like(dmk_ref)
            dmv_ref[...] = jnp.zeros_like(dmv_ref)
            gq_acc[...] = jnp.zeros_like(gq_acc)

        pairs = []
        for p in range(2):
            cs = slice(p * LANES, (p + 1) * LANES)
            mkn, mvp, _, _ = _mem_keys(kv_ref, kg_ref, bd, p)
            gqs = qg_ref[:, cs] * QK_SCALE
            q = q_ref[:, cs]
            r = lax.rsqrt(_headsum(q * q, bd) * (1.0 / HEAD_DIM) + EPS)
            z = q * r
            qn = z * gqs
            g = g_ref[:, cs]
            ov = om_ref[:, cs]
            dym = dy_ref[:, cs]
            sg = _sigmoid(g)
            dgt_ref[:, cs] = (dym * ov * (sg * (1.0 + g * (1.0 - sg)))).astype(BF16)
            do = dym * (g * sg)
            pairs.append(dict(cs=cs, mkn=mkn, mvp=mvp, gqs=gqs, r=r, z=z, qn=qn, qnb=qn.astype(BF16), do=do,
                              dob=do.astype(BF16), delta=_headsum(do * ov, bd)))
        chains = [(pr_, h) for pr_ in pairs for h in range(2)]
        mask = lambda h: lo if h == 0 else ~lo
        ss = [_dot_nt(jnp.where(mask(h), c["qn"], 0.0).astype(BF16), c["mkn"]) for c, h in chains]
        dps = [_dot_nt(jnp.where(mask(h), c["do"], 0.0).astype(BF16), c["mvp"]) for c, h in chains]
        prs, dss = [], []
        for s, dp, (c, h) in zip(ss, dps, chains):
            e = jnp.exp(s - jnp.max(s, axis=-1, keepdims=True))
            pr = e * (1.0 / jnp.sum(e, axis=-1, keepdims=True))
            prs.append(pr.astype(BF16))
            dss.append((pr * (dp - c["delta"][:, h * HEAD_DIM:h * HEAD_DIM + 1])).astype(BF16))
        dqs = [_dot(ds, c["mkn"]) for ds, (c, h) in zip(dss, chains)]
        dks = [_dot_tn(ds, c["qnb"]) for ds, (c, h) in zip(dss, chains)]
        dvs = [_dot_tn(pr, c["dob"]) for pr, (c, h) in zip(prs, chains)]
        for p, c in enumerate(pairs):
            cs, z, r = c["cs"], c["z"], c["r"]
            dqn = jnp.where(lo, dqs[2 * p], dqs[2 * p + 1])
            dmk_ref[:, cs] += jnp.where(lom, dks[2 * p], dks[2 * p + 1])
            dmv_ref[:, cs] += jnp.where(lom, dvs[2 * p], dvs[2 * p + 1])
            dz = dqn * c["gqs"]
            dq_ref[:, cs] = (r * (dz - z * (_headsum(dz * z, bd) * (1.0 / HEAD_DIM)))).astype(BF16)
            gq_acc[:, cs] += jnp.sum(dqn * z, axis=0, keepdims=True) * QK_SCALE

        @pl.when(i == nsteps - 1)
        def _():
            gqg_ref[...] = jnp.zeros_like(gqg_ref)
            gkg_ref[...] = jnp.zeros_like(gkg_ref)
            gqg_ref[0:1, :] = _fold_heads(gq_acc[:, 0:LANES] + gq_acc[:, LANES:2 * LANES])
            dkv = []
            gk = jnp.zeros((1, LANES), F32)
            for p in range(2):
                cs = slice(p * LANES, (p + 1) * LANES)
                _, _, r, z = _mem_keys(kv_ref, kg_ref, bd, p)
                dn = dmk_ref[:, cs]
                dz = dn * kg_ref[:, cs]
                gk = gk + jnp.sum(dn * z, axis=0, keepdims=True)
                dkv.append(r * (dz - z * (_headsum(dz * z, bd) * (1.0 / HEAD_DIM))))
            gkg_ref[0:1, :] = _fold_heads(gk)
            dkvb = jnp.concatenate(dkv + [dmv_ref[...]], axis=1).astype(BF16)
            gw_ref[...] = _dot_tn(hm_ref[...], dkvb)
            dhm = _dot_nt(dkvb, w_ref[...])
            mv = mem_ref[...]
            zm = mv * lax.rsqrt(jnp.mean(mv * mv, axis=-1, keepdims=True) + EPS)
            _put_rows(gmg_ref, jnp.sum(dhm * zm, axis=0, keepdims=True))

    const = lambda shape: pl.BlockSpec(shape, lambda i: (0,) * len(shape))
    row = lambda j: pl.BlockSpec((tm, MEM_W), lambda i: (i, j))
    blk8 = jax.ShapeDtypeStruct((8, LANES), F32)
    return _call(
        body, name="mem_bwd", grid=(nsteps,),
        in_specs=[row(C_MQ // MEM_W), row(C_MG // MEM_W), row(0), row((GMLP_W + ATTN_W) // MEM_W),
                  const((MEM_LEN, 2 * MEM_W)), const((MEM_LEN, D_MODEL)), const((MEM_LEN, D_MODEL)),
                  const((1, D_MODEL)), const((D_MODEL, 2 * MEM_W)), const((1, MEM_W)), const((1, MEM_W))],
        out_specs=[row(0), row(0), const((8, LANES)), const((8, LANES)),
                   const((D_MODEL, 2 * MEM_W)), const((8, LANES))],
        out_shape=[jax.ShapeDtypeStruct((SEQ, MEM_W), BF16), jax.ShapeDtypeStruct((SEQ, MEM_W), BF16),
                   blk8, blk8, jax.ShapeDtypeStruct((D_MODEL, 2 * MEM_W), F32), blk8],
        scratch_shapes=[pltpu.VMEM((MEM_LEN, MEM_W), F32), pltpu.VMEM((MEM_LEN, MEM_W), F32),
                        pltpu.VMEM((1, MEM_W), F32)],
        compiler_params=_params(),
    )(proj, proj, om, dyc, kv, hm, mem, mgain, wkv, qg4, kg4)


def _out_loss(yg, ya, ym, x, tgt, wo):
    tm = 512
    nsteps = SEQ // tm
    parts = ((0, GMLP_W), (GMLP_W, ATTN_W), (GMLP_W + ATTN_W, MEM_W))

    def body(yg_ref, ya_ref, ym_ref, x_ref, t_ref, w_ref, dy_ref, dyc_ref, gw_ref, ls_ref):
        i = pl.program_id(0)

        @pl.when(i == 0)
        def _():
            gw_ref[...] = jnp.zeros_like(gw_ref)
            ls_ref[...] = jnp.zeros_like(ls_ref)

        ys = (yg_ref[...], ya_ref[...], ym_ref[...])
        y = sum(_dot(yv, w_ref[r0:r0 + n, :]) for yv, (r0, n) in zip(ys, parts))
        err = x_ref[...] + y - t_ref[...]
        _put_rows(ls_ref, jnp.sum(err * err, axis=0, keepdims=True), accumulate=True)
        dy = err * (1.0 / D_MODEL)
        dy_ref[...] = dy
        dyb = dy.astype(BF16)
        dyc_ref[...] = _dot_nt(dyb, w_ref[...])
        for yv, (r0, n) in zip(ys, parts):
            gw_ref[r0:r0 + n, :] += _dot_tn(yv, dyb)

    row = lambda w: pl.BlockSpec((tm, w), lambda i: (i, 0))
    const = lambda shape: pl.BlockSpec(shape, lambda i: (0, 0))
    return _call(
        body, name="out_loss", grid=(nsteps,),
        in_specs=[row(GMLP_W), row(ATTN_W), row(MEM_W), row(D_MODEL), row(D_MODEL), const((D_MODEL, D_MODEL))],
        out_specs=[row(D_MODEL), row(D_MODEL), const((D_MODEL, D_MODEL)), const((8, LANES))],
        out_shape=[jax.ShapeDtypeStruct((SEQ, D_MODEL), F32), jax.ShapeDtypeStruct((SEQ, D_MODEL), F32),
                   jax.ShapeDtypeStruct((D_MODEL, D_MODEL), F32), jax.ShapeDtypeStruct((8, LANES), F32)],
        compiler_params=_params(),
    )(yg, ya, ym, x, tgt, wo)


def _proj_bwd(x, dy, gain, wt, dg, daq, dak, dav, dag, dmq, dmg):
    tm = 512
    nsteps = SEQ // tm
    pieces = ((C_GU, 3 * GMLP_W), (C_AQ, ATTN_W), (C_AK, ATTN_W), (C_AV, ATTN_W), (C_AG, ATTN_W),
              (C_MQ, MEM_W), (C_MG, MEM_W))

    def body(x_ref, dy_ref, g_ref, wt_hbm, p0, p1, p2, p3, p4, p5, p6, gx_ref, gwt_hbm, gg_ref, wt_v, acc, wt_sem, out_sems):
        i = pl.program_id(0)
        wt_load = pltpu.make_async_copy(wt_hbm, wt_v, wt_sem)

        @pl.when(i == 0)
        def _():
            wt_load.start()
            acc[...] = jnp.zeros_like(acc)
            gg_ref[...] = jnp.zeros_like(gg_ref)

        xv = x_ref[...]
        r = lax.rsqrt(jnp.mean(xv * xv, axis=-1, keepdims=True) + EPS)
        z = xv * r
        g = g_ref[...]
        h = (z * g).astype(BF16)
        pl.when(i == 0)(wt_load.wait)
        flush = [pltpu.make_async_copy(acc.at[c0:c0 + w, :], gwt_hbm.at[c0:c0 + w, :], out_sems.at[n])
                 for n, (c0, w) in enumerate(pieces)]
        dh = jnp.zeros((tm, D_MODEL), F32)
        for n, (pref, (c0, w)) in enumerate(zip((p0, p1, p2, p3, p4, p5, p6), pieces)):
            dp = pref[...]
            dh = dh + _dot(dp, wt_v[c0:c0 + w, :])
            acc[c0:c0 + w, :] += _dot_tn(dp, h)
            pl.when(i == nsteps - 1)(flush[n].start)
        _put_rows(gg_ref, jnp.sum(dh * z, axis=0, keepdims=True), accumulate=True)
        dz = dh * g
        gx_ref[...] = dy_ref[...] + r * (dz - z * jnp.mean(dz * z, axis=-1, keepdims=True))

        @pl.when(i == nsteps - 1)
        def _():
            for cp in flush:
                cp.wait()

    row = lambda w: pl.BlockSpec((tm, w), lambda i: (i, 0))
    hbm = pl.BlockSpec(memory_space=pl.ANY)
    vec = pl.BlockSpec((1, D_MODEL), lambda i: (0, 0))
    return _call(
        body, name="proj_bwd", grid=(nsteps,),
        in_specs=[row(D_MODEL), row(D_MODEL), vec, hbm] + [row(w) for _, w in pieces],
        out_specs=[row(D_MODEL), hbm, pl.BlockSpec((8, LANES), lambda i: (0, 0))],
        out_shape=[jax.ShapeDtypeStruct((SEQ, D_MODEL), F32), jax.ShapeDtypeStruct((IN_W, D_MODEL), F32),
                   jax.ShapeDtypeStruct((8, LANES), F32)],
        scratch_shapes=[pltpu.VMEM((IN_W, D_MODEL), BF16), pltpu.VMEM((IN_W, D_MODEL), F32), pltpu.SemaphoreType.DMA,
                        pltpu.SemaphoreType.DMA((len(pieces),))],
        compiler_params=_params(),
    )(x, dy, gain, wt, dg, daq, dak, dav, dag, dmq, dmg)


AG_SEMS = 8


def _gather_stages(ins, lands, send_sems, recv_sems):
    n = len(ins)
    nrows = [a.shape[0] for a in ins]
    x, y, c = lax.axis_index("x"), lax.axis_index("y"), lax.axis_index("c")
    sib, xn, yn = (x, y, 1 - c), (1 - x, y, c), (x, 1 - y, c)
    me, cx, cy, cd = 2 * x + y, 2 * (1 - x) + y, 2 * x + (1 - y), 2 * (1 - x) + (1 - y)

    def part(a, chip, hf, quarter=None):
        rows = nrows[a] // 2
        base = chip * nrows[a] + hf * rows
        if quarter is not None:
            rows = rows // 2
            base = base + quarter * rows
        return lands[a].at[pl.ds(pl.multiple_of(base, 16), rows), :]

    def copy(a, j, ref, to):
        k = AG_SEMS * a + j
        return pltpu.make_async_remote_copy(src_ref=ref, dst_ref=ref, send_sem=send_sems.at[k],
                                            recv_sem=recv_sems.at[k], device_id=to, device_id_type=MESH)

    def own(a):
        return [copy(a, 0, part(a, me, c), xn), copy(a, 1, part(a, me, c), yn)]

    def neighbours(a):
        return [copy(a, 4, part(a, cx, c, 1), yn), copy(a, 2, part(a, cx, c), sib),
                copy(a, 5, part(a, cy, c, 0), xn), copy(a, 3, part(a, cy, c), sib)]

    def diagonal(a):
        return [copy(a, 7, part(a, cd, c, 1), sib), copy(a, 6, part(a, cd, c, 0), sib)]

    def send_own():
        for a in range(n):
            lands[a][pl.ds(pl.multiple_of(me * nrows[a], 16), nrows[a]), :] = ins[a][...].astype(BF16)
            for cp in own(a):
                cp.start()

    def pass_on_neighbours():
        for a in range(n):
            copy(a, 0, part(a, cx, c), xn).wait_recv()
            copy(a, 1, part(a, cy, c), yn).wait_recv()
            for cp in neighbours(a):
                cp.start()

    def pass_on_diagonal():
        for a in range(n):
            copy(a, 4, part(a, cd, c, 1), yn).wait_recv()
            copy(a, 5, part(a, cd, c, 0), xn).wait_recv()
            for cp in diagonal(a):
                cp.start()

    def y_complete():
        for a in range(n):
            copy(a, 3, part(a, cy, 1 - c), sib).wait_recv()

    def x_complete():
        for a in range(n):
            copy(a, 2, part(a, cx, 1 - c), sib).wait_recv()

    def diagonal_complete():
        for a in range(n):
            copy(a, 6, part(a, cd, 1 - c, 0), sib).wait_recv()
            copy(a, 7, part(a, cd, 1 - c, 1), sib).wait_recv()

    def sends_done():
        for a in range(n):
            for cp in own(a) + neighbours(a) + diagonal(a):
                cp.wait_send()

    def finish():
        y_complete()
        x_complete()
        diagonal_complete()
        sends_done()

    return (send_own, pass_on_neighbours, pass_on_diagonal, finish), (y_complete, x_complete, diagonal_complete, sends_done)


RS_SEMS = 6
RS_KINDS = (((2, 2), 1, F32), ((2, 2), 1, F32), ((2, 2), 2, BF16), ((2, 2), 2, BF16), ((2, 2), 2, F32),
            ((2,), 2, BF16), ((2,), 2, BF16), ((2,), 1, F32))


def _rs_view(g):
    return g.reshape(2, 2, 2, g.shape[0] // 8, g.shape[1])


def _rs_scratch(shapes, in_vmem=False):
    kinds = RS_KINDS[1:] if in_vmem else RS_KINDS
    return [pltpu.VMEM(lead + (r // 8, w // split), dt) for lead, split, dt in kinds for r, w in shapes]


def _rs_stages(gs, outs, bufs, send_sems, recv_sems, local_sems, widths):
    n = len(gs)
    if len(bufs) < n * len(RS_KINDS):
        bufs = [None] * n + list(bufs)
    loc, ra, s_b, r_b, acc1, s_c, r_c, fin = (bufs[n * i:n * i + n] for i in range(len(RS_KINDS)))
    half_w = [w // 2 for w in widths]
    chips = [(xx, yy) for xx in range(2) for yy in range(2)]
    x, y, c = lax.axis_index("x"), lax.axis_index("y"), lax.axis_index("c")
    sib, xn, yn = (x, y, 1 - c), (1 - x, y, c), (x, 1 - y, c)

    def copy(a, j, src, dst, to):
        k = RS_SEMS * a + j
        return pltpu.make_async_remote_copy(src_ref=src, dst_ref=dst, send_sem=send_sems.at[k],
                                            recv_sem=recv_sems.at[k], device_id=to, device_id_type=MESH)

    def step_a(a):
        if callable(gs[a]):
            return [copy(a, 0, gs[a](xx, yy, 1 - c), ra[a].at[xx, yy], sib) for xx, yy in chips]
        return [copy(a, 0, gs[a].at[:, :, 1 - c], ra[a], sib),
                pltpu.make_async_copy(gs[a].at[:, :, c], loc[a], local_sems.at[a])]

    def finish_a(a):
        if callable(gs[a]):
            copy(a, 0, ra[a], ra[a], sib).wait()
            for xx, yy in chips:
                ra[a][xx, yy] = gs[a](xx, yy, c)[...] + ra[a][xx, yy]
        else:
            for cp in step_a(a):
                cp.wait()
            ra[a][...] = loc[a][...] + ra[a][...]

    def step_b(a):
        return copy(a, 1, s_b[a].at[0], r_b[a].at[0], xn), copy(a, 2, s_b[a].at[1], r_b[a].at[1], yn)

    def step_c(a):
        return copy(a, 3, s_c[a].at[0], r_c[a].at[0], yn), copy(a, 4, s_c[a].at[1], r_c[a].at[1], xn)

    def step_d(a, half):
        rows = fin[a].at[half]
        return copy(a, 5, rows, rows, sib)

    def start():
        for a in range(n):
            for cp in step_a(a):
                cp.start()

    def a_to_b():
        for a in range(n):
            finish_a(a)
            s_b[a][0] = ra[a][1 - x, :, :, :half_w[a]].astype(BF16)
            s_b[a][1] = ra[a][:, 1 - y, :, half_w[a]:].astype(BF16)
            for cp in step_b(a):
                cp.start()

    def b_to_c():
        for a in range(n):
            for cp in step_b(a):
                cp.wait()
            acc1[a][0] = ra[a][x, :, :, :half_w[a]] + r_b[a][0].astype(F32)
            acc1[a][1] = ra[a][:, y, :, half_w[a]:] + r_b[a][1].astype(F32)
            s_c[a][0] = acc1[a][0, 1 - y].astype(BF16)
            s_c[a][1] = acc1[a][1, 1 - x].astype(BF16)
            for cp in step_c(a):
                cp.start()

    def c_to_d():
        for a in range(n):
            for cp in step_c(a):
                cp.wait()
            fin[a][c, :, :half_w[a]] = acc1[a][0, y] + r_c[a][0].astype(F32)
            fin[a][c, :, half_w[a]:] = acc1[a][1, x] + r_c[a][1].astype(F32)
            step_d(a, c).start()

    def finish():
        for a in range(n):
            step_d(a, 1 - c).wait_recv()
            step_d(a, c).wait_send()
            pltpu.sync_copy(fin[a], outs[a])

    return start, a_to_b, b_to_c, c_to_d, finish


def _reduce_grads(gwt, g_ws, tiny):
    cw = gwt.shape[1] // RS_CHUNKS
    chunk_shape = (gwt.shape[0], cw)

    def body(g0, ws_in, tiny_in, *rest):
        outs, o_ws, o_tiny = rest[:RS_CHUNKS], rest[RS_CHUNKS], rest[RS_CHUNKS + 1]
        rest = rest[RS_CHUNKS + 2:]
        nb = len(RS_KINDS) * RS_CHUNKS
        sm, sa, sb, sc, acc_s, send_sems, recv_sems, local_sems = rest[nb:]
        blocks = [g0.at[:, :, :, :, pl.ds(j * cw, cw)] for j in range(RS_CHUNKS)]
        start, a_to_b, b_to_c, c_to_d, finish = _rs_stages(blocks, outs, rest[:nb], send_sems, recv_sems, local_sems,
                                                           [cw] * RS_CHUNKS)
        n_ws = ws_in.shape[0]
        sm[0:n_ws, :] = ws_in[...]
        sm[n_ws:, :] = tiny_in[...]
        x, y, c = lax.axis_index("x"), lax.axis_index("y"), lax.axis_index("c")

        def small(j, src, dst, to):
            k = RS_SEMS * RS_CHUNKS + j
            return pltpu.make_async_remote_copy(src_ref=src, dst_ref=dst, send_sem=send_sems.at[k],
                                                recv_sem=recv_sems.at[k], device_id=to, device_id_type=MESH)

        along_c, along_x, along_y = (small(0, sm, sa, (x, y, 1 - c)), small(1, acc_s, sb, (1 - x, y, c)),
                                     small(2, sb, sc, (x, 1 - y, c)))
        start()
        along_c.start()
        a_to_b()
        along_c.wait()
        acc_s[...] = sm[...] + sa[...]
        along_x.start()
        b_to_c()
        along_x.wait()
        sb[...] = acc_s[...] + sb[...]
        along_y.start()
        c_to_d()
        along_y.wait()
        o_ws[...] = sb[0:n_ws, :] + sc[0:n_ws, :]
        o_tiny[...] = sb[n_ws:, :] + sc[n_ws:, :]
        finish()

    vm = pl.BlockSpec(memory_space=pltpu.VMEM)
    hbm = pl.BlockSpec(memory_space=pl.ANY)
    small_shape = (g_ws.shape[0] + tiny.shape[0], LANES)
    scratch = _rs_scratch([chunk_shape] * RS_CHUNKS) + [pltpu.VMEM(small_shape, F32) for _ in range(5)]
    nsem = RS_SEMS * RS_CHUNKS + 3
    scratch += [pltpu.SemaphoreType.DMA((nsem,)), pltpu.SemaphoreType.DMA((nsem,)), pltpu.SemaphoreType.DMA((RS_CHUNKS,))]
    return _call(
        body, name="reduce_grads",
        out_shape=[jax.ShapeDtypeStruct((2, gwt.shape[0] // 8, cw), F32)] * RS_CHUNKS
        + [jax.ShapeDtypeStruct(g_ws.shape, F32), jax.ShapeDtypeStruct(tiny.shape, F32)],
        in_specs=[hbm, vm, vm],
        out_specs=[hbm] * RS_CHUNKS + [vm, vm],
        scratch_shapes=scratch,
        compiler_params=_params(),
    )(_rs_view(gwt), g_ws, tiny)


def _adam_update(w, g, m, v):
    nm = ADAM_B1 * m + (1.0 - ADAM_B1) * g
    nv = ADAM_B2 * v + (1.0 - ADAM_B2) * (g * g)
    m_hat = nm / (1.0 - ADAM_B1 ** ADAM_STEP)
    v_hat = nv / (1.0 - ADAM_B2 ** ADAM_STEP)
    return -ADAM_LR * (m_hat / (jnp.sqrt(v_hat) + ADAM_EPS) + ADAM_WD * w), nm, nv


def _adamw(w, g, m, v):
    rows, cols = w.shape
    tm = max(t for t in range(8, 257, 8) if rows % t == 0)
    parts = tuple(g) if isinstance(g, (tuple, list)) else (g,)
    n = len(parts)

    def body(w_ref, m_ref, v_ref, *refs):
        gv = jnp.concatenate([r[...] for r in refs[:n]], axis=1)
        d_ref, nm_ref, nv_ref = refs[n:n + 3]
        d_ref[...], nm_ref[...], nv_ref[...] = _adam_update(w_ref[...], gv, m_ref[...], v_ref[...])
        if n > 1:
            refs[n + 3][...] = gv

    blk = pl.BlockSpec((tm, cols), lambda i: (i, 0))
    nout = 3 if n == 1 else 4
    res = _call(
        body, name="adamw", grid=(rows // tm,),
        in_specs=[blk] * 3 + [pl.BlockSpec((tm, p.shape[1]), lambda i: (i, 0)) for p in parts], out_specs=[blk] * nout,
        out_shape=[jax.ShapeDtypeStruct((rows, cols), F32)] * nout,
        compiler_params=_params(),
    )(w, m, v, *parts)
    return (parts[0] if n == 1 else res[3], *res[:3])


def _adamw_tiny(tiny, weights, ms, vs):
    shapes = [w.shape for w in weights]
    n = len(weights)

    def grad_of(t_ref, k, shape):
        base = 8 * k
        if shape[1] > LANES:
            return [t_ref[base + j:base + j + 1, :] for j in range(shape[1] // LANES)]
        return [t_ref[base:base + shape[0], 0:shape[1]]]

    def body(t_ref, *refs):
        w_refs, m_refs, v_refs = refs[:n], refs[n:2 * n], refs[2 * n:3 * n]
        loss_ref, outs = refs[3 * n], refs[3 * n + 1:]
        loss_ref[...] = (0.5 / D_MODEL) * jnp.sum(t_ref[8 * n:8 * n + 8, :], keepdims=True)
        for k, shape in enumerate(shapes):
            g_ref, d_ref, nm_ref, nv_ref = outs[4 * k:4 * k + 4]
            for j, g in enumerate(grad_of(t_ref, k, shape)):
                cols = slice(j * LANES, (j + 1) * LANES) if shape[1] > LANES else slice(None)
                g_ref[:, cols] = g
                d_ref[:, cols], nm_ref[:, cols], nv_ref[:, cols] = _adam_update(
                    w_refs[k][:, cols], g, m_refs[k][:, cols], v_refs[k][:, cols])

    out_shape = [jax.ShapeDtypeStruct((1, 1), F32)]
    for shape in shapes:
        out_shape += [jax.ShapeDtypeStruct(shape, F32)] * 4
    return _call(body, name="adamw_tiny", out_shape=out_shape, compiler_params=_params())(tiny, *weights, *ms, *vs)


def _local_grads(x, mem, tgt, norm_gain, wt_sh, gmlp_v_gain, gmlp_w_s, gmlp_b, attn_q_gain, attn_k_gain,
                 mem_norm_gain, wkv_sh, mem_q_gain, mem_k_gain, wo_sh):
    vg = gmlp_v_gain.reshape(1, GMLP_W)
    bias_full = jnp.repeat(gmlp_b.T, HEAD_DIM, axis=1)
    gq2, gk2 = jnp.tile(attn_q_gain, (1, 2)), jnp.tile(attn_k_gain, (1, 2))
    qg4, kg4 = jnp.tile(mem_q_gain, (1, 4)), jnp.tile(mem_k_gain, (1, 4))

    proj, wt, wkv, wo = _gather_proj(x, norm_gain, wt_sh, wkv_sh, wo_sh)
    yg = _gmlp_fwd(proj, vg, gmlp_w_s, bias_full)
    o, lse, ya = _attn_fwd(proj, gq2, gk2)
    kv, hm = _mem_kv(mem, mem_norm_gain, wkv)
    om, ym = _mem_fwd(proj, kv, qg4, kg4)
    dy, dyc, g_wo, err2 = _out_loss(yg, ya, ym, x, tgt, wo)
    dmq, dmg, g_mq, g_mk, g_wkv, g_mng = _mem_bwd(proj, om, dyc, kv, hm, mem, mem_norm_gain, wkv, qg4, kg4)
    daq, dak, dav, dag, g_aq, g_ak, g_wkv_sh, g_wo_sh = _attn_bwd(proj, o, lse, dyc, gq2, gk2, g_wkv, g_wo)
    dg, g_ws, g_b, g_vg = _gmlp_bwd(proj, dyc, vg, gmlp_w_s, bias_full)
    gx, g_wt, g_ng = _proj_bwd(x, dy, norm_gain, wt, dg, daq, dak, dav, dag, dmq, dmg)

    tiny = jnp.concatenate([g_ng, g_vg, g_b, g_aq, g_ak, g_mng, g_mq, g_mk, err2], axis=0)
    return gx, g_wt, g_wkv_sh, g_wo_sh, g_ws.reshape(4 * CHUNK, CHUNK), tiny


def kernel(x, mem, norm_gain, w_in, gmlp_v_gain, gmlp_w_s, gmlp_b, attn_q_gain, attn_k_gain, mem_norm_gain, w_mem_kv, mem_q_gain, mem_k_gain, w_out, loss_target, m_norm_gain, m_w_in, m_gmlp_v_gain, m_gmlp_w_s, m_gmlp_b, m_attn_q_gain, m_attn_k_gain, m_mem_norm_gain, m_w_mem_kv, m_mem_q_gain, m_mem_k_gain, m_w_out, v_norm_gain, v_w_in, v_gmlp_v_gain, v_gmlp_w_s, v_gmlp_b, v_attn_q_gain, v_attn_k_gain, v_mem_norm_gain, v_w_mem_kv, v_mem_q_gain, v_mem_k_gain, v_w_out):
    gx, g_wt, g_wkv_sh, g_wo_sh, g_ws, tiny = _local_grads(
        x[0], mem[0], loss_target[0], norm_gain, w_in[0].T, gmlp_v_gain[0], gmlp_w_s[0], gmlp_b[0],
        attn_q_gain, attn_k_gain, mem_norm_gain, w_mem_kv[0], mem_q_gain, mem_k_gain, w_out[0])
    *g_wt_sh, g_ws, tiny = _reduce_grads(g_wt, g_ws, tiny)
    chip_block = lambda g: g.reshape(2 * g.shape[1], g.shape[2])
    g_wt_sh = tuple(chip_block(g) for g in g_wt_sh)
    g_wkv_sh, g_wo_sh = chip_block(g_wkv_sh), chip_block(g_wo_sh)

    ws = (norm_gain, w_in, gmlp_v_gain, gmlp_w_s, gmlp_b, attn_q_gain, attn_k_gain, mem_norm_gain, w_mem_kv,
          mem_q_gain, mem_k_gain, w_out)
    ms = (m_norm_gain, m_w_in, m_gmlp_v_gain, m_gmlp_w_s, m_gmlp_b, m_attn_q_gain, m_attn_k_gain, m_mem_norm_gain,
          m_w_mem_kv, m_mem_q_gain, m_mem_k_gain, m_w_out)
    vs = (v_norm_gain, v_w_in, v_gmlp_v_gain, v_gmlp_w_s, v_gmlp_b, v_attn_q_gain, v_attn_k_gain, v_mem_norm_gain,
          v_w_mem_kv, v_mem_q_gain, v_mem_k_gain, v_w_out)
    form = {1: lambda a: a[0].T, 3: lambda a: a.reshape(4 * CHUNK, CHUNK), 2: lambda a: a[0], 4: lambda a: a[0],
            8: lambda a: a[0], 11: lambda a: a[0]}
    back = {1: lambda a: a.T[None], 3: lambda a: a.reshape(1, 4, CHUNK, CHUNK), 2: lambda a: a[None],
            4: lambda a: a[None], 8: lambda a: a[None], 11: lambda a: a[None]}
    fwd = lambda t, i: form.get(i, lambda a: a)(t[i])
    out = {}
    for i, g in ((1, g_wt_sh), (3, g_ws), (8, g_wkv_sh), (11, g_wo_sh)):
        out[i] = _adamw(fwd(ws, i), g, fwd(ms, i), fwd(vs, i))
    res = _adamw_tiny(tiny, [fwd(ws, i) for i in TINY_ORDER], [fwd(ms, i) for i in TINY_ORDER],
                      [fwd(vs, i) for i in TINY_ORDER])
    for k, i in enumerate(TINY_ORDER):
        out[i] = res[1 + 4 * k:5 + 4 * k]
    leaves = [[back.get(i, lambda a: a)(out[i][j]) for i in range(12)] for j in range(4)]
    return (res[0].reshape(()), gx[None], *leaves[0], *leaves[1], *leaves[2], *leaves[3])
```

```python
import functools
import math

import jax
import jax.numpy as jnp
from jax import lax
from jax.experimental import pallas as pl
from jax.experimental.pallas import tpu as pltpu

F32 = jnp.float32
BF16 = jnp.bfloat16

SEQ = 4096
D_MODEL = 1024
HEAD_DIM = 64
LANES = 128
CHUNK = 128
GMLP_W, ATTN_W, MEM_W = 256, 512, 256
IN_W = 3 * GMLP_W + 4 * ATTN_W + 2 * MEM_W
MEM_LEN = 256
DILATIONS = (16, 4, 1)
EPS = 1e-6
QK_SCALE = 1.0 / math.sqrt(HEAD_DIM)
C_GU, C_GV, C_GG, C_AQ, C_AK, C_AV, C_AG, C_MQ, C_MG = 0, 256, 512, 768, 1280, 1792, 2304, 2816, 3072

ADAM_LR, ADAM_B1, ADAM_B2, ADAM_EPS, ADAM_WD, ADAM_STEP = 0.001, 0.9, 0.999, 1e-08, 0.01, 10

VMEM_LIMIT = 48 * 1024 * 1024
RS_CHUNKS = 4
ATTN_UNROLL = 4
MESH = pl.DeviceIdType.MESH

TINY_ORDER = (0, 2, 4, 5, 6, 7, 9, 10)


def _call(body, **kw):
    return pl.pallas_call(body, **kw)


def _params(**kw):
    return pltpu.CompilerParams(vmem_limit_bytes=VMEM_LIMIT, **kw)


def _dot(a, b):
    return jnp.dot(a, b, preferred_element_type=F32)


def _dot_nt(a, b):
    return lax.dot_general(a, b, (((1,), (1,)), ((), ())), preferred_element_type=F32)


def _dot_tn(a, b):
    return lax.dot_general(a, b, (((0,), (0,)), ((), ())), preferred_element_type=F32)


def _head_blockdiag():
    r = lax.shift_right_logical(lax.broadcasted_iota(jnp.int32, (LANES, LANES), 0), 6)
    c = lax.shift_right_logical(lax.broadcasted_iota(jnp.int32, (LANES, LANES), 1), 6)
    return jnp.where(r == c, 1.0, 0.0).astype(BF16)


def _headsum(v, bd):
    hi = v.astype(BF16)
    lo = (v - hi.astype(F32)).astype(BF16)
    return _dot(hi, bd) + _dot(lo, bd)


def _lo_mask(rows):
    return lax.broadcasted_iota(jnp.int32, (rows, LANES), 1) < HEAD_DIM


def _sigmoid(x):
    return 1.0 / (1.0 + jnp.exp(-x))


def _fold_heads(v):
    return v + pltpu.roll(v, HEAD_DIM, 1)


def _put_rows(ref, vec, accumulate=False):
    for j in range(vec.shape[1] // LANES):
        piece = vec[:, j * LANES:(j + 1) * LANES]
        ref[j:j + 1, :] = ref[j:j + 1, :] + piece if accumulate else piece


def _gather_proj(x, gain, wt_sh, *ride_along):
    tm = 512
    nrow = SEQ // tm
    nride = len(ride_along)
    widths = (768, 896, 768, 896)
    pair = 2 * wt_sh.shape[0]
    assert pair % LANES == 0 and sum(widths[:2]) == pair

    def body(x_ref, g_ref, wt_sh_ref, *rest):
        shards, rest = rest[:nride], rest[nride:]
        proj_hbm, wt_hbm, gathered = rest[0], rest[1], rest[2:2 + nride]
        h_scr, land, res = rest[2 + nride:5 + nride]
        lands, (send0, recv0, send1, recv1, out_sems, copy_sems) = rest[5 + nride:5 + 2 * nride], rest[5 + 2 * nride:]
        u, i = pl.program_id(0), pl.program_id(1)
        cx_, cy_ = lax.axis_index("x"), lax.axis_index("y")
        (send_own, pass_on_neighbours, pass_on_diagonal, _), (y_complete, x_complete, diagonal_complete, sends_done) = (
            _gather_stages((wt_sh_ref,), (land,), send0, recv0))
        ride, _ = _gather_stages(shards, lands, send1, recv1)
        first = lambda k: (u == k) & (i == 0)
        last = (u == 3) & (i == nrow - 1)
        copies = [pltpu.make_async_copy(land, wt_hbm, copy_sems.at[0])] + [
            pltpu.make_async_copy(src, dst, copy_sems.at[1 + k]) for k, (src, dst) in enumerate(zip(lands, gathered))]

        pl.when(first(0))(send_own)

        @pl.when(u == 0)
        def _():
            xv = x_ref[...]
            ms = jnp.mean(xv * xv, axis=-1, keepdims=True)
            h_scr[pl.ds(pl.multiple_of(i * tm, tm), tm), :] = (xv * lax.rsqrt(ms + EPS) * g_ref[...]).astype(BF16)

        @pl.when(first(1))
        def _():
            pass_on_neighbours()
            ride[0]()
            y_complete()

        @pl.when(first(2))
        def _():
            x_complete()
            pass_on_diagonal()
            ride[1]()

        @pl.when(first(3))
        def _():
            diagonal_complete()
            copies[0].start()
            ride[2]()

        col0 = (pair * cx_ + 896 * cy_, pair * cx_ + 768 * (1 - cy_),
                pair * (1 - cx_) + 896 * cy_, pair * (1 - cx_) + 768 * (1 - cy_))
        slot = i % 2
        rows = pl.ds(pl.multiple_of(i * tm, tm), tm)

        def writeback(k, rows_):
            c0 = pl.multiple_of(col0[k], LANES)
            return pltpu.make_async_copy(res.at[slot, :, pl.ds(0, widths[k])], proj_hbm.at[rows_, pl.ds(c0, widths[k])],
                                         out_sems.at[slot])

        for k in range(4):
            @pl.when(u == k)
            def _(k=k):
                pl.when(i >= 2)(writeback(k, rows).wait)
                if k > 0:
                    pl.when(i < 2)(writeback(k - 1, rows).wait)
                w_rows = land[pl.ds(pl.multiple_of(col0[k], LANES), widths[k]), :]
                res[slot, :, 0:widths[k]] = _dot_nt(h_scr[rows, :], w_rows)
                writeback(k, rows).start()

        @pl.when(last)
        def _():
            sends_done()
            ride[3]()
            for cp in copies[1:]:
                cp.start()
            for cp in copies:
                cp.wait()
            pltpu.make_async_copy(res.at[0, :, pl.ds(0, widths[3])], proj_hbm.at[rows, pl.ds(0, widths[3])], out_sems.at[0]).wait()
            pltpu.make_async_copy(res.at[1, :, pl.ds(0, widths[3])], proj_hbm.at[rows, pl.ds(0, widths[3])], out_sems.at[1]).wait()

    full = [jax.ShapeDtypeStruct((4 * a.shape[0], a.shape[1]), BF16) for a in (wt_sh,) + ride_along]
    hbm = pl.BlockSpec(memory_space=pl.ANY)
    const = lambda a: pl.BlockSpec(a.shape, lambda u, i: (0, 0))
    return _call(
        body, name="gather_proj", grid=(4, nrow),
        in_specs=[pl.BlockSpec((tm, D_MODEL), lambda u, i: (jnp.where(u == 0, i, nrow - 1), 0)),
                  pl.BlockSpec((1, D_MODEL), lambda u, i: (0, 0)), const(wt_sh)] + [const(a) for a in ride_along],
        out_specs=[hbm] * (2 + nride),
        out_shape=[jax.ShapeDtypeStruct((SEQ, IN_W), F32)] + full,
        scratch_shapes=[pltpu.VMEM((SEQ, D_MODEL), BF16), pltpu.VMEM(full[0].shape, BF16), pltpu.VMEM((2, tm, max(widths)), F32)]
        + [pltpu.VMEM(s.shape, BF16) for s in full[1:]]
        + [pltpu.SemaphoreType.DMA((AG_SEMS,)), pltpu.SemaphoreType.DMA((AG_SEMS,)),
           pltpu.SemaphoreType.DMA((AG_SEMS * nride,)), pltpu.SemaphoreType.DMA((AG_SEMS * nride,)),
           pltpu.SemaphoreType.DMA((2,)), pltpu.SemaphoreType.DMA((1 + nride,))],
        compiler_params=_params(),
    )(x, gain, wt_sh, *ride_along)


def _gmlp_weights(w_ref):
    ti = lax.broadcasted_iota(jnp.int32, (CHUNK, CHUNK), 0)
    si = lax.broadcasted_iota(jnp.int32, (CHUNK, CHUNK), 1)
    tril = si <= ti
    return tril, [jnp.where(tril, w_ref[h], 0.0).astype(BF16) for h in range(4)]


def _gmlp_fwd(proj, vgain, w_s, bias_full):
    tm = 512

    def body(p_ref, vg_ref, w_ref, b_ref, y_ref):
        bd = _head_blockdiag()
        lo = _lo_mask(CHUNK)
        _, wm = _gmlp_weights(w_ref)
        units = [(pl.ds(c * CHUNK, CHUNK), p) for c in range(tm // CHUNK) for p in range(2)]
        col = lambda c0, p: slice(c0 + p * LANES, c0 + (p + 1) * LANES)
        vs = [p_ref[rows, col(C_GV, p)] for rows, p in units]
        rs = [lax.rsqrt(_headsum(v * v, bd) * (1.0 / HEAD_DIM) + EPS) for v in vs]
        vns = [(v * r * vg_ref[:, col(0, p)]).astype(BF16) for v, r, (_, p) in zip(vs, rs, units)]
        sps = [jnp.where(lo, _dot(wm[2 * p], vn), _dot(wm[2 * p + 1], vn)) + b_ref[:, col(0, p)] for vn, (_, p) in zip(vns, units)]
        for sp, (rows, p) in zip(sps, units):
            gt = p_ref[rows, col(C_GG, p)]
            y_ref[rows, col(0, p)] = (p_ref[rows, col(C_GU, p)] * sp * (gt * _sigmoid(gt))).astype(BF16)

    return _call(
        body, name="gmlp_fwd", grid=(SEQ // tm,),
        in_specs=[pl.BlockSpec((tm, 3 * GMLP_W), lambda i: (i, 0)),
                  pl.BlockSpec((1, GMLP_W), lambda i: (0, 0)),
                  pl.BlockSpec((4, CHUNK, CHUNK), lambda i: (0, 0, 0)),
                  pl.BlockSpec((CHUNK, GMLP_W), lambda i: (0, 0))],
        out_specs=pl.BlockSpec((tm, GMLP_W), lambda i: (i, 0)),
        out_shape=jax.ShapeDtypeStruct((SEQ, GMLP_W), BF16),
        compiler_params=_params(),
    )(proj, vgain, w_s, bias_full)


def _gmlp_bwd(proj, dyc, vgain, w_s, bias_full):
    tm = 512
    nsteps = SEQ // tm

    def body(p_ref, dy_ref, vg_ref, w_ref, b_ref, dg_ref, gw_ref, gb_ref, gv_ref):
        i = pl.program_id(0)
        bd = _head_blockdiag()
        lo = _lo_mask(CHUNK)
        tril, wm = _gmlp_weights(w_ref)
        ri = lax.broadcasted_iota(jnp.int32, (16, LANES), 0)
        li = lax.broadcasted_iota(jnp.int32, (16, LANES), 1)
        head_rows = [jnp.where(((ri == 2 * p) & (li < HEAD_DIM)) | ((ri == 2 * p + 1) & (li >= HEAD_DIM)), 1.0, 0.0).astype(BF16)
                     for p in range(2)]

        @pl.when(i == 0)
        def _():
            gw_ref[...] = jnp.zeros_like(gw_ref)
            gb_ref[...] = jnp.zeros_like(gb_ref)
            gv_ref[...] = jnp.zeros_like(gv_ref)

        units = [(pl.ds(c * CHUNK, CHUNK), p) for c in range(tm // CHUNK) for p in range(2)]
        col = lambda c0, p: slice(c0 + p * LANES, c0 + (p + 1) * LANES)
        vs = [p_ref[rows, col(C_GV, p)] for rows, p in units]
        rs = [lax.rsqrt(_headsum(v * v, bd) * (1.0 / HEAD_DIM) + EPS) for v in vs]
        zs = [v * r for v, r in zip(vs, rs)]
        vns = [(z * vg_ref[:, col(0, p)]).astype(BF16) for z, (_, p) in zip(zs, units)]
        sps = [jnp.where(lo, _dot(wm[2 * p], vn), _dot(wm[2 * p + 1], vn)) + b_ref[:, col(0, p)] for vn, (_, p) in zip(vns, units)]
        dsps = []
        for sp, (rows, p) in zip(sps, units):
            u = p_ref[rows, col(C_GU, p)]
            gt = p_ref[rows, col(C_GG, p)]
            dy = dy_ref[rows, col(0, p)]
            sg = _sigmoid(gt)
            sl = gt * sg
            dg_ref[rows, col(C_GU, p)] = (dy * sp * sl).astype(BF16)
            dg_ref[rows, col(C_GG, p)] = (dy * u * sp * (sg * (1.0 + gt * (1.0 - sg)))).astype(BF16)
            dsps.append(dy * u * sl)
        dspbs = [dsp.astype(BF16) for dsp in dsps]
        dvns = [jnp.where(lo, _dot_tn(wm[2 * p], dspb), _dot_tn(wm[2 * p + 1], dspb)) for dspb, (_, p) in zip(dspbs, units)]
        gws = [(_dot_nt(jnp.where(lo, dsp, 0.0).astype(BF16), vn), _dot_nt(jnp.where(lo, 0.0, dsp).astype(BF16), vn))
               for dsp, vn in zip(dsps, vns)]
        gbs = [(_dot_nt(head_rows[p], dspb) + _dot_nt(head_rows[p], (dsp - dspb.astype(F32)).astype(BF16)))[0:8]
               for dsp, dspb, (_, p) in zip(dsps, dspbs, units)]
        for p in range(2):
            mine = [n for n, (_, q) in enumerate(units) if q == p]
            gw_ref[2 * p] += sum(gws[n][0] for n in mine)
            gw_ref[2 * p + 1] += sum(gws[n][1] for n in mine)
            gvp = sum(jnp.sum(dvns[n] * zs[n], axis=0, keepdims=True) for n in mine)
            gv_ref[2 * p:2 * p + 1, :] += gvp
            gv_ref[2 * p + 1:2 * p + 2, :] += pltpu.roll(gvp, HEAD_DIM, 1)
        gb_ref[...] += sum(gbs)
        for dvn, z, r, (rows, p) in zip(dvns, zs, rs, units):
            dz = dvn * vg_ref[:, col(0, p)]
            dg_ref[rows, col(C_GV, p)] = (r * (dz - z * (_headsum(dz * z, bd) * (1.0 / HEAD_DIM)))).astype(BF16)

        @pl.when(i == nsteps - 1)
        def _():
            for h in range(4):
                gw_ref[h] = jnp.where(tril, gw_ref[h], 0.0)

    return _call(
        body, name="gmlp_bwd", grid=(nsteps,),
        in_specs=[pl.BlockSpec((tm, 3 * GMLP_W), lambda i: (i, 0)),
                  pl.BlockSpec((tm, GMLP_W), lambda i: (i, 0)),
                  pl.BlockSpec((1, GMLP_W), lambda i: (0, 0)),
                  pl.BlockSpec((4, CHUNK, CHUNK), lambda i: (0, 0, 0)),
                  pl.BlockSpec((CHUNK, GMLP_W), lambda i: (0, 0))],
        out_specs=[pl.BlockSpec((tm, 3 * GMLP_W), lambda i: (i, 0)),
                   pl.BlockSpec((4, CHUNK, CHUNK), lambda i: (0, 0, 0)),
                   pl.BlockSpec((8, LANES), lambda i: (0, 0)),
                   pl.BlockSpec((8, LANES), lambda i: (0, 0))],
        out_shape=[jax.ShapeDtypeStruct((SEQ, 3 * GMLP_W), BF16),
                   jax.ShapeDtypeStruct((4, CHUNK, CHUNK), F32),
                   jax.ShapeDtypeStruct((8, LANES), F32),
                   jax.ShapeDtypeStruct((8, LANES), F32)],
        compiler_params=_params(),
    )(proj, dyc, vgain, w_s, bias_full)


def _band_masks():
    qi = lax.broadcasted_iota(jnp.int32, (CHUNK, 2 * CHUNK), 0)
    kj = lax.broadcasted_iota(jnp.int32, (CHUNK, 2 * CHUNK), 1)
    valid2 = ((kj < CHUNK) & (kj >= qi)) | ((kj >= CHUNK) & (kj - CHUNK <= qi))
    q1 = lax.broadcasted_iota(jnp.int32, (CHUNK, CHUNK), 0)
    k1 = lax.broadcasted_iota(jnp.int32, (CHUNK, CHUNK), 1)
    return k1 <= q1, valid2


def _stack_heads(v, lo):
    return jnp.concatenate([jnp.where(lo, v, 0.0), jnp.where(lo, 0.0, v)], axis=0).astype(BF16)


def _rows_of(ref, start, d):
    if d == 1:
        return ref.at[pl.ds(start if isinstance(start, int) else pl.multiple_of(start, CHUNK), CHUNK), :]
    return ref.at[pl.ds(start, CHUNK, stride=d), :]


def _unrolled(lo, hi, unroll, run):
    groups = (hi - lo) // unroll
    if groups:
        def body(g, carry):
            run([lo + g * unroll + t for t in range(unroll)])
            return carry

        lax.fori_loop(0, groups, body, 0)
    if lo + groups * unroll < hi:
        run(range(lo + groups * unroll, hi))


def _for_blocks(d, group_fn, unroll):
    nblk = SEQ // CHUNK
    sh = d.bit_length() - 1

    def first(j):
        return (j * CHUNK if d == 1 else j, None)

    def rest(j):
        start = (j & (d - 1)) + (j >> sh) * (CHUNK * d)
        return (start, start - CHUNK * d)

    _unrolled(0, d, unroll, lambda js: group_fn(d, [first(j) for j in js]))
    _unrolled(d, nblk, unroll, lambda js: group_fn(d, [rest(j) for j in js]))


def _attn_fwd(proj, gq2, gk2):
    tn = 512

    def body(q_ref, k_ref, v_ref, g_ref, gq_ref, gk_ref, o_ref, l_ref, ya_ref, qn_ref, kn_ref):
        bd = _head_blockdiag()
        lo = _lo_mask(CHUNK)
        valid1, valid2 = _band_masks()

        def norm(t, carry):
            rows = pl.ds(pl.multiple_of(t * tn, tn), tn)
            q = q_ref[rows, :]
            qn_ref[rows, :] = q * lax.rsqrt(_headsum(q * q, bd) * (1.0 / HEAD_DIM) + EPS) * (gq_ref[...] * QK_SCALE)
            k = k_ref[rows, :]
            kn_ref[rows, :] = k * lax.rsqrt(_headsum(k * k, bd) * (1.0 / HEAD_DIM) + EPS) * gk_ref[...]
            return carry

        lax.fori_loop(0, SEQ // tn, norm, 0)

        def load_kv(ref, d, start, prev):
            own = _rows_of(ref, start, d)[...]
            if prev is None:
                return own.astype(BF16)
            return jnp.concatenate([_rows_of(ref, prev, d)[...], own], axis=0).astype(BF16)

        def group(d, blocks):
            valid = valid1 if blocks[0][1] is None else valid2
            valid = jnp.concatenate([valid, valid], axis=0)
            qs = [_rows_of(qn_ref, start, d)[...] for start, _ in blocks]
            ks = [load_kv(kn_ref, d, start, prev) for start, prev in blocks]
            vs = [load_kv(v_ref, d, start, prev) for start, prev in blocks]
            ss = [_dot_nt(_stack_heads(q, lo), k) for q, k in zip(qs, ks)]
            ms, ps, ls = [], [], []
            for s in ss:
                s = jnp.where(valid, s, -jnp.inf)
                m = jnp.max(s, axis=-1, keepdims=True)
                p = jnp.exp(s - m)
                ms.append(m)
                ls.append(jnp.sum(p, axis=-1, keepdims=True))
                ps.append(p.astype(BF16))
            os_ = [_dot(p, v) for p, v in zip(ps, vs)]
            for b, (start, _) in enumerate(blocks):
                on = os_[b] * (1.0 / ls[b])
                ln = ms[b] + jnp.log(ls[b])
                ob = jnp.where(lo, on[:CHUNK], on[CHUNK:])
                lb = jnp.where(lo, ln[:CHUNK], ln[CHUNK:])
                o_rows = _rows_of(o_ref, start, d)
                l_rows = _rows_of(l_ref, start, d)
                if d != DILATIONS[0]:
                    lold = l_rows[...]
                    mx = jnp.maximum(lold, lb)
                    ea = jnp.exp(lold - mx)
                    eb = jnp.exp(lb - mx)
                    inv = 1.0 / (ea + eb)
                    ob = o_rows[...] * (ea * inv) + ob * (eb * inv)
                    lb = mx + jnp.log(ea + eb)
                o_rows[...] = ob
                l_rows[...] = lb

        for d in DILATIONS:
            _for_blocks(d, group, ATTN_UNROLL)

        def fin(t, carry):
            rows = pl.ds(pl.multiple_of(t * tn, tn), tn)
            g = g_ref[rows, :]
            ya_ref[rows, :] = (o_ref[rows, :] * (g * _sigmoid(g))).astype(BF16)
            return carry

        lax.fori_loop(0, SEQ // tn, fin, 0)

    col = lambda c0: pl.BlockSpec((SEQ, LANES), lambda p: (0, c0 // LANES + p))
    vec = pl.BlockSpec((1, LANES), lambda p: (0, 0))
    out = pl.BlockSpec((SEQ, LANES), lambda p: (0, p))
    return _call(
        body, name="attn_fwd", grid=(ATTN_W // LANES,),
        in_specs=[col(C_AQ), col(C_AK), col(C_AV), col(C_AG), vec, vec],
        out_specs=[out, out, out],
        out_shape=[jax.ShapeDtypeStruct((SEQ, ATTN_W), F32), jax.ShapeDtypeStruct((SEQ, ATTN_W), F32),
                   jax.ShapeDtypeStruct((SEQ, ATTN_W), BF16)],
        scratch_shapes=[pltpu.VMEM((SEQ, LANES), F32), pltpu.VMEM((SEQ, LANES), F32)],
        compiler_params=_params(),
    )(proj, proj, proj, proj, gq2, gk2)


def _attn_bwd(proj, o, lse, dyc, gq2, gk2, *ride_along):
    tn = 512
    npairs = ATTN_W // LANES
    nride = len(ride_along)
    nbufs = nride * len(RS_KINDS)

    def body(proj_hbm, o_hbm, l_hbm, dyc_hbm, gq_ref, gk_ref, *rest):
        ride_in, rest = rest[:nride], rest[nride:]
        dq_ref, dk_ref, dv_ref, dgt_ref, gqg_ref, gkg_ref = rest[:6]
        ride_out, rest = rest[6:6 + nride], rest[6 + nride:]
        qb_, kb_, vb_, gb_, ob_, lb_, yb_, dkb_, dvb_, sems = rest[:10]
        rs_bufs, (send_sems, recv_sems, local_sems) = rest[10:10 + nbufs], rest[10 + nbufs:]
        rs_stage = _rs_stages(ride_in, ride_out, rs_bufs, send_sems, recv_sems, local_sems, [g.shape[1] for g in ride_along])
        pair = pl.program_id(0)
        for step in range(npairs):
            pl.when(pair == step)(rs_stage[step])
        bd = _head_blockdiag()
        lo = _lo_mask(CHUNK)
        lo2 = lax.broadcasted_iota(jnp.int32, (2 * CHUNK, LANES), 1) < HEAD_DIM
        valid1, valid2 = _band_masks()
        gqs = gq_ref[...] * QK_SCALE
        gk = gk_ref[...]

        def pcol(c0):
            return proj_hbm.at[:, pl.ds(pl.multiple_of(c0 + pair * LANES, LANES), LANES)]

        def acol(hbm, c0=0):
            return hbm.at[:, pl.ds(pl.multiple_of(c0 + pair * LANES, LANES), LANES)]

        loads = [pltpu.make_async_copy(src, dst, sems.at[n]) for n, (src, dst) in enumerate((
            (pcol(C_AQ), qb_), (pcol(C_AK), kb_), (pcol(C_AG), gb_), (acol(o_hbm), ob_),
            (acol(dyc_hbm, GMLP_W), yb_), (pcol(C_AV), vb_), (acol(l_hbm), lb_)))]
        for cp in loads:
            cp.start()

        @pl.when(pair == 0)
        def _():
            gqg_ref[...] = jnp.zeros_like(gqg_ref)
            gkg_ref[...] = jnp.zeros_like(gkg_ref)

        def pre_qk(t, carry):
            rows = pl.ds(pl.multiple_of(t * tn, tn), tn)
            q = qb_[rows, :]
            qb_[rows, :] = q * lax.rsqrt(_headsum(q * q, bd) * (1.0 / HEAD_DIM) + EPS) * gqs
            k = kb_[rows, :]
            kb_[rows, :] = k * lax.rsqrt(_headsum(k * k, bd) * (1.0 / HEAD_DIM) + EPS) * gk
            return carry

        def pre_gate(t, carry):
            rows = pl.ds(pl.multiple_of(t * tn, tn), tn)
            g = gb_[rows, :]
            ov = ob_[rows, :]
            dya = yb_[rows, :]
            sg = _sigmoid(g)
            dgt_ref[rows, :] = (dya * ov * (sg * (1.0 + g * (1.0 - sg)))).astype(BF16)
            do = dya * (g * sg)
            yb_[rows, :] = do
            ob_[rows, :] = _headsum(do * ov, bd)
            return carry

        loads[0].wait()
        loads[1].wait()
        lax.fori_loop(0, SEQ // tn, pre_qk, 0)
        for cp in loads[2:5]:
            cp.wait()
        lax.fori_loop(0, SEQ // tn, pre_gate, 0)
        loads[5].wait()
        loads[6].wait()

        def load_kv(ref, d, start, prev):
            own = _rows_of(ref, start, d)[...]
            if prev is None:
                return own.astype(BF16)
            return jnp.concatenate([_rows_of(ref, prev, d)[...], own], axis=0).astype(BF16)

        def group(d, blocks):
            first = blocks[0][1] is None
            valid, lok = (valid1, lo) if first else (valid2, lo2)
            chains = [(b, h) for b in range(len(blocks)) for h in range(2)]
            mask = lambda h: lo if h == 0 else ~lo
            qs = [_rows_of(qb_, start, d)[...] for start, _ in blocks]
            dos = [_rows_of(yb_, start, d)[...] for start, _ in blocks]
            lvs = [_rows_of(lb_, start, d)[...] for start, _ in blocks]
            dls = [_rows_of(ob_, start, d)[...] for start, _ in blocks]
            ks = [load_kv(kb_, d, start, prev) for start, prev in blocks]
            vs = [load_kv(vb_, d, start, prev) for start, prev in blocks]
            qbs = [q.astype(BF16) for q in qs]
            dobs = [do.astype(BF16) for do in dos]
            ss = [_dot_nt(jnp.where(mask(h), qs[b], 0.0).astype(BF16), ks[b]) for b, h in chains]
            dps = [_dot_nt(jnp.where(mask(h), dos[b], 0.0).astype(BF16), vs[b]) for b, h in chains]
            pbs, dss = [], []
            for s, dp, (b, h) in zip(ss, dps, chains):
                hc = h * HEAD_DIM
                p = jnp.exp(jnp.where(valid, s, -jnp.inf) - lvs[b][:, hc:hc + 1])
                pbs.append(p.astype(BF16))
                dss.append((p * (dp - dls[b][:, hc:hc + 1])).astype(BF16))
            dqs = [_dot(ds, ks[b]) for ds, (b, h) in zip(dss, chains)]
            dks = [_dot_tn(ds, qbs[b]) for ds, (b, h) in zip(dss, chains)]
            dvs = [_dot_tn(p, dobs[b]) for p, (b, h) in zip(pbs, chains)]
            assign = d == DILATIONS[0]
            for b, (start, prev) in enumerate(blocks):
                c0, c1 = 2 * b, 2 * b + 1
                dq_rows = _rows_of(gb_, start, d)
                dqb = jnp.where(lo, dqs[c0], dqs[c1])
                dq_rows[...] = dqb if assign else dq_rows[...] + dqb
                dkc = jnp.where(lok, dks[c0], dks[c1])
                dvc = jnp.where(lok, dvs[c0], dvs[c1])
                spans = ((start, slice(0, CHUNK), True),) if first else (
                    (prev, slice(0, CHUNK), False), (start, slice(CHUNK, 2 * CHUNK), True))
                for st, sl, own in spans:
                    dk_rows = _rows_of(dkb_, st, d)
                    dv_rows = _rows_of(dvb_, st, d)
                    if assign and own:
                        dk_rows[...] = dkc[sl]
                        dv_rows[...] = dvc[sl]
                    else:
                        dk_rows[...] = dk_rows[...] + dkc[sl]
                        dv_rows[...] = dv_rows[...] + dvc[sl]

        for d in DILATIONS:
            _for_blocks(d, group, ATTN_UNROLL)

        reloads = [pltpu.make_async_copy(pcol(C_AQ), vb_, sems.at[0]), pltpu.make_async_copy(pcol(C_AK), lb_, sems.at[1])]
        for cp in reloads:
            cp.start()
        for cp in reloads:
            cp.wait()

        def post(t, carry):
            gq_acc, gk_acc = carry
            rows = pl.ds(pl.multiple_of(t * tn, tn), tn)
            outs = []
            for raw_, acc_, gain in ((vb_, gb_, gqs), (lb_, dkb_, gk)):
                a = raw_[rows, :]
                r = lax.rsqrt(_headsum(a * a, bd) * (1.0 / HEAD_DIM) + EPS)
                z = a * r
                dn = acc_[rows, :]
                dz = dn * gain
                outs.append((r * (dz - z * (_headsum(dz * z, bd) * (1.0 / HEAD_DIM))), jnp.sum(dn * z, axis=0, keepdims=True)))
            dq_ref[rows, :] = outs[0][0].astype(BF16)
            dk_ref[rows, :] = outs[1][0].astype(BF16)
            dv_ref[rows, :] = dvb_[rows, :].astype(BF16)
            return gq_acc + outs[0][1] * QK_SCALE, gk_acc + outs[1][1]

        zero = jnp.zeros((1, LANES), F32)
        gq_acc, gk_acc = lax.fori_loop(0, SEQ // tn, post, (zero, zero))
        gqg_ref[0:1, :] += gq_acc
        gkg_ref[0:1, :] += gk_acc

        @pl.when(pair == npairs - 1)
        def _():
            gqg_ref[0:1, :] = _fold_heads(gqg_ref[0:1, :])
            gkg_ref[0:1, :] = _fold_heads(gkg_ref[0:1, :])
            rs_stage[npairs]()

    hbm = pl.BlockSpec(memory_space=pl.ANY)
    vec = pl.BlockSpec((1, LANES), lambda p: (0, 0))
    blk8 = pl.BlockSpec((8, LANES), lambda p: (0, 0))
    out = pl.BlockSpec((SEQ, LANES), lambda p: (0, p))
    big = jax.ShapeDtypeStruct((SEQ, ATTN_W), BF16)
    nsem = RS_SEMS * nride
    return _call(
        body, name="attn_bwd", grid=(npairs,),
        in_specs=[hbm, hbm, hbm, hbm, vec, vec] + [hbm] * nride,
        out_specs=[out, out, out, out, blk8, blk8] + [hbm] * nride,
        out_shape=[big, big, big, big, jax.ShapeDtypeStruct((8, LANES), F32), jax.ShapeDtypeStruct((8, LANES), F32)]
        + [jax.ShapeDtypeStruct((2, g.shape[0] // 8, g.shape[1]), F32) for g in ride_along],
        scratch_shapes=[pltpu.VMEM((SEQ, LANES), F32) for _ in range(9)] + [pltpu.SemaphoreType.DMA((7,))]
        + _rs_scratch([g.shape for g in ride_along]) + [pltpu.SemaphoreType.DMA((nsem,)), pltpu.SemaphoreType.DMA((nsem,)),
                                     pltpu.SemaphoreType.DMA((nride,))],
        compiler_params=_params(),
    )(proj, o, lse, dyc, gq2, gk2, *[_rs_view(g) for g in ride_along])


def _mem_kv(mem, gain, wkv):
    def body(m_ref, g_ref, w_ref, kv_ref, hm_ref):
        mv = m_ref[...]
        ms = jnp.mean(mv * mv, axis=-1, keepdims=True)
        hm = (mv * lax.rsqrt(ms + EPS) * g_ref[...]).astype(BF16)
        hm_ref[...] = hm
        kv_ref[...] = _dot(hm, w_ref[...])

    return _call(
        body, name="mem_kv",
        out_shape=[jax.ShapeDtypeStruct((MEM_LEN, 2 * MEM_W), F32), jax.ShapeDtypeStruct((MEM_LEN, D_MODEL), BF16)],
        compiler_params=_params(),
    )(mem, gain, wkv)


def _mem_keys(kv_ref, kg_ref, bd, p):
    mk = kv_ref[:, p * LANES:(p + 1) * LANES]
    r = lax.rsqrt(_headsum(mk * mk, bd) * (1.0 / HEAD_DIM) + EPS)
    z = mk * r
    mkn = (z * kg_ref[:, p * LANES:(p + 1) * LANES]).astype(BF16)
    mvp = kv_ref[:, MEM_W + p * LANES:MEM_W + (p + 1) * LANES].astype(BF16)
    return mkn, mvp, r, z


def _mem_fwd(proj, kv, qg4, kg4):
    tm = 512

    def body(q_ref, g_ref, kv_ref, qg_ref, kg_ref, om_ref, ym_ref):
        bd = _head_blockdiag()
        lo = _lo_mask(tm)
        keys, qns = [], []
        for p in range(2):
            cs = slice(p * LANES, (p + 1) * LANES)
            keys.append(_mem_keys(kv_ref, kg_ref, bd, p)[:2])
            q = q_ref[:, cs]
            qns.append(q * lax.rsqrt(_headsum(q * q, bd) * (1.0 / HEAD_DIM) + EPS) * (qg_ref[:, cs] * QK_SCALE))
        chains = [(p, h) for p in range(2) for h in range(2)]
        ss = [_dot_nt(jnp.where(lo if h == 0 else ~lo, qns[p], 0.0).astype(BF16), keys[p][0]) for p, h in chains]
        es = [jnp.exp(s - jnp.max(s, axis=-1, keepdims=True)) for s in ss]
        os_ = [_dot(e.astype(BF16), keys[p][1]) for e, (p, h) in zip(es, chains)]
        res = [o * (1.0 / jnp.sum(e, axis=-1, keepdims=True)) for o, e in zip(os_, es)]
        for p in range(2):
            cs = slice(p * LANES, (p + 1) * LANES)
            ov = jnp.where(lo, res[2 * p], res[2 * p + 1])
            g = g_ref[:, cs]
            om_ref[:, cs] = ov
            ym_ref[:, cs] = (ov * (g * _sigmoid(g))).astype(BF16)

    vec = pl.BlockSpec((1, MEM_W), lambda i: (0, 0))
    return _call(
        body, name="mem_fwd", grid=(SEQ // tm,),
        in_specs=[pl.BlockSpec((tm, MEM_W), lambda i: (i, C_MQ // MEM_W)),
                  pl.BlockSpec((tm, MEM_W), lambda i: (i, C_MG // MEM_W)),
                  pl.BlockSpec((MEM_LEN, 2 * MEM_W), lambda i: (0, 0)), vec, vec],
        out_specs=[pl.BlockSpec((tm, MEM_W), lambda i: (i, 0)), pl.BlockSpec((tm, MEM_W), lambda i: (i, 0))],
        out_shape=[jax.ShapeDtypeStruct((SEQ, MEM_W), F32), jax.ShapeDtypeStruct((SEQ, MEM_W), BF16)],
        compiler_params=_params(),
    )(proj, proj, kv, qg4, kg4)


def _mem_bwd(proj, om, dyc, kv, hm, mem, mgain, wkv, qg4, kg4):
    tm = 512
    nsteps = SEQ // tm

    def body(q_ref, g_ref, om_ref, dy_ref, kv_ref, hm_ref, mem_ref, mg_ref, w_ref, qg_ref, kg_ref,
             dq_ref, dgt_ref, gqg_ref, gkg_ref, gw_ref, gmg_ref, dmk_ref, dmv_ref, gq_acc):
        i = pl.program_id(0)
        bd = _head_blockdiag()
        lo = _lo_mask(tm)
        lom = _lo_mask(MEM_LEN)

        @pl.when(i == 0)
        def _():
            dmk_ref[...] = jnp.zeros_like(dmk_ref)
            dmv_ref[...] = jnp.zeros_like(dmv_ref)
            gq_acc[...] = jnp.zeros_like(gq_acc)

        pairs = []
        for p in range(2):
            cs = slice(p * LANES, (p + 1) * LANES)
            mkn, mvp, _, _ = _mem_keys(kv_ref, kg_ref, bd, p)
            gqs = qg_ref[:, cs] * QK_SCALE
            q = q_ref[:, cs]
            r = lax.rsqrt(_headsum(q * q, bd) * (1.0 / HEAD_DIM) + EPS)
            z = q * r
            qn = z * gqs
            g = g_ref[:, cs]
            ov = om_ref[:, cs]
            dym = dy_ref[:, cs]
            sg = _sigmoid(g)
            dgt_ref[:, cs] = (dym * ov * (sg * (1.0 + g * (1.0 - sg)))).astype(BF16)
            do = dym * (g * sg)
            pairs.append(dict(cs=cs, mkn=mkn, mvp=mvp, gqs=gqs, r=r, z=z, qn=qn, qnb=qn.astype(BF16), do=do,
                              dob=do.astype(BF16), delta=_headsum(do * ov, bd)))
        chains = [(pr_, h) for pr_ in pairs for h in range(2)]
        mask = lambda h: lo if h == 0 else ~lo
        ss = [_dot_nt(jnp.where(mask(h), c["qn"], 0.0).astype(BF16), c["mkn"]) for c, h in chains]
        dps = [_dot_nt(jnp.where(mask(h), c["do"], 0.0).astype(BF16), c["mvp"]) for c, h in chains]
        prs, dss = [], []
        for s, dp, (c, h) in zip(ss, dps, chains):
            e = jnp.exp(s - jnp.max(s, axis=-1, keepdims=True))
            pr = e * (1.0 / jnp.sum(e, axis=-1, keepdims=True))
            prs.append(pr.astype(BF16))
            dss.append((pr * (dp - c["delta"][:, h * HEAD_DIM:h * HEAD_DIM + 1])).astype(BF16))
        dqs = [_dot(ds, c["mkn"]) for ds, (c, h) in zip(dss, chains)]
        dks = [_dot_tn(ds, c["qnb"]) for ds, (c, h) in zip(dss, chains)]
        dvs = [_dot_tn(pr, c["dob"]) for pr, (c, h) in zip(prs, chains)]
        for p, c in enumerate(pairs):
            cs, z, r = c["cs"], c["z"], c["r"]
            dqn = jnp.where(lo, dqs[2 * p], dqs[2 * p + 1])
            dmk_ref[:, cs] += jnp.where(lom, dks[2 * p], dks[2 * p + 1])
            dmv_ref[:, cs] += jnp.where(lom, dvs[2 * p], dvs[2 * p + 1])
            dz = dqn * c["gqs"]
            dq_ref[:, cs] = (r * (dz - z * (_headsum(dz * z, bd) * (1.0 / HEAD_DIM)))).astype(BF16)
            gq_acc[:, cs] += jnp.sum(dqn * z, axis=0, keepdims=True) * QK_SCALE

        @pl.when(i == nsteps - 1)
        def _():
            gqg_ref[...] = jnp.zeros_like(gqg_ref)
            gkg_ref[...] = jnp.zeros_like(gkg_ref)
            gqg_ref[0:1, :] = _fold_heads(gq_acc[:, 0:LANES] + gq_acc[:, LANES:2 * LANES])
            dkv = []
            gk = jnp.zeros((1, LANES), F32)
            for p in range(2):
                cs = slice(p * LANES, (p + 1) * LANES)
                _, _, r, z = _mem_keys(kv_ref, kg_ref, bd, p)
                dn = dmk_ref[:, cs]
                dz = dn * kg_ref[:, cs]
                gk = gk + jnp.sum(dn * z, axis=0, keepdims=True)
                dkv.append(r * (dz - z * (_headsum(dz * z, bd) * (1.0 / HEAD_DIM))))
            gkg_ref[0:1, :] = _fold_heads(gk)
            dkvb = jnp.concatenate(dkv + [dmv_ref[...]], axis=1).astype(BF16)
            gw_ref[...] = _dot_tn(hm_ref[...], dkvb)
            dhm = _dot_nt(dkvb, w_ref[...])
            mv = mem_ref[...]
            zm = mv * lax.rsqrt(jnp.mean(mv * mv, axis=-1, keepdims=True) + EPS)
            _put_rows(gmg_ref, jnp.sum(dhm * zm, axis=0, keepdims=True))

    const = lambda shape: pl.BlockSpec(shape, lambda i: (0,) * len(shape))
    row = lambda j: pl.BlockSpec((tm, MEM_W), lambda i: (i, j))
    blk8 = jax.ShapeDtypeStruct((8, LANES), F32)
    return _call(
        body, name="mem_bwd", grid=(nsteps,),
        in_specs=[row(C_MQ // MEM_W), row(C_MG // MEM_W), row(0), row((GMLP_W + ATTN_W) // MEM_W),
                  const((MEM_LEN, 2 * MEM_W)), const((MEM_LEN, D_MODEL)), const((MEM_LEN, D_MODEL)),
                  const((1, D_MODEL)), const((D_MODEL, 2 * MEM_W)), const((1, MEM_W)), const((1, MEM_W))],
        out_specs=[row(0), row(0), const((8, LANES)), const((8, LANES)),
                   const((D_MODEL, 2 * MEM_W)), const((8, LANES))],
        out_shape=[jax.ShapeDtypeStruct((SEQ, MEM_W), BF16), jax.ShapeDtypeStruct((SEQ, MEM_W), BF16),
                   blk8, blk8, jax.ShapeDtypeStruct((D_MODEL, 2 * MEM_W), F32), blk8],
        scratch_shapes=[pltpu.VMEM((MEM_LEN, MEM_W), F32), pltpu.VMEM((MEM_LEN, MEM_W), F32),
                        pltpu.VMEM((1, MEM_W), F32)],
        compiler_params=_params(),
    )(proj, proj, om, dyc, kv, hm, mem, mgain, wkv, qg4, kg4)


def _out_loss(yg, ya, ym, x, tgt, wo):
    tm = 512
    nsteps = SEQ // tm
    parts = ((0, GMLP_W), (GMLP_W, ATTN_W), (GMLP_W + ATTN_W, MEM_W))

    def body(yg_ref, ya_ref, ym_ref, x_ref, t_ref, w_ref, dy_ref, dyc_ref, gw_ref, ls_ref):
        i = pl.program_id(0)

        @pl.when(i == 0)
        def _():
            gw_ref[...] = jnp.zeros_like(gw_ref)
            ls_ref[...] = jnp.zeros_like(ls_ref)

        ys = (yg_ref[...], ya_ref[...], ym_ref[...])
        y = sum(_dot(yv, w_ref[r0:r0 + n, :]) for yv, (r0, n) in zip(ys, parts))
        err = x_ref[...] + y - t_ref[...]
        _put_rows(ls_ref, jnp.sum(err * err, axis=0, keepdims=True), accumulate=True)
        dy = err * (1.0 / D_MODEL)
        dy_ref[...] = dy
        dyb = dy.astype(BF16)
        dyc_ref[...] = _dot_nt(dyb, w_ref[...])
        for yv, (r0, n) in zip(ys, parts):
            gw_ref[r0:r0 + n, :] += _dot_tn(yv, dyb)

    row = lambda w: pl.BlockSpec((tm, w), lambda i: (i, 0))
    const = lambda shape: pl.BlockSpec(shape, lambda i: (0, 0))
    return _call(
        body, name="out_loss", grid=(nsteps,),
        in_specs=[row(GMLP_W), row(ATTN_W), row(MEM_W), row(D_MODEL), row(D_MODEL), const((D_MODEL, D_MODEL))],
        out_specs=[row(D_MODEL), row(D_MODEL), const((D_MODEL, D_MODEL)), const((8, LANES))],
        out_shape=[jax.ShapeDtypeStruct((SEQ, D_MODEL), F32), jax.ShapeDtypeStruct((SEQ, D_MODEL), F32),
                   jax.ShapeDtypeStruct((D_MODEL, D_MODEL), F32), jax.ShapeDtypeStruct((8, LANES), F32)],
        compiler_params=_params(),
    )(yg, ya, ym, x, tgt, wo)


def _proj_bwd(x, dy, gain, wt, dg, daq, dak, dav, dag, dmq, dmg):
    tm = 512
    nsteps = SEQ // tm
    pieces = ((C_GU, 3 * GMLP_W), (C_AQ, ATTN_W), (C_AK, ATTN_W), (C_AV, ATTN_W), (C_AG, ATTN_W),
              (C_MQ, MEM_W), (C_MG, MEM_W))

    def body(x_ref, dy_ref, g_ref, wt_hbm, p0, p1, p2, p3, p4, p5, p6, gx_ref, gwt_hbm, gg_ref, wt_v, acc, wt_sem, out_sems):
        i = pl.program_id(0)
        wt_load = pltpu.make_async_copy(wt_hbm, wt_v, wt_sem)

        @pl.when(i == 0)
        def _():
            wt_load.start()
            acc[...] = jnp.zeros_like(acc)
            gg_ref[...] = jnp.zeros_like(gg_ref)

        xv = x_ref[...]
        r = lax.rsqrt(jnp.mean(xv * xv, axis=-1, keepdims=True) + EPS)
        z = xv * r
        g = g_ref[...]
        h = (z * g).astype(BF16)
        pl.when(i == 0)(wt_load.wait)
        flush = [pltpu.make_async_copy(acc.at[c0:c0 + w, :], gwt_hbm.at[c0:c0 + w, :], out_sems.at[n])
                 for n, (c0, w) in enumerate(pieces)]
        dh = jnp.zeros((tm, D_MODEL), F32)
        for n, (pref, (c0, w)) in enumerate(zip((p0, p1, p2, p3, p4, p5, p6), pieces)):
            dp = pref[...]
            dh = dh + _dot(dp, wt_v[c0:c0 + w, :])
            acc[c0:c0 + w, :] += _dot_tn(dp, h)
            pl.when(i == nsteps - 1)(flush[n].start)
        _put_rows(gg_ref, jnp.sum(dh * z, axis=0, keepdims=True), accumulate=True)
        dz = dh * g
        gx_ref[...] = dy_ref[...] + r * (dz - z * jnp.mean(dz * z, axis=-1, keepdims=True))

        @pl.when(i == nsteps - 1)
        def _():
            for cp in flush:
                cp.wait()

    row = lambda w: pl.BlockSpec((tm, w), lambda i: (i, 0))
    hbm = pl.BlockSpec(memory_space=pl.ANY)
    vec = pl.BlockSpec((1, D_MODEL), lambda i: (0, 0))
    return _call(
        body, name="proj_bwd", grid=(nsteps,),
        in_specs=[row(D_MODEL), row(D_MODEL), vec, hbm] + [row(w) for _, w in pieces],
        out_specs=[row(D_MODEL), hbm, pl.BlockSpec((8, LANES), lambda i: (0, 0))],
        out_shape=[jax.ShapeDtypeStruct((SEQ, D_MODEL), F32), jax.ShapeDtypeStruct((IN_W, D_MODEL), F32),
                   jax.ShapeDtypeStruct((8, LANES), F32)],
        scratch_shapes=[pltpu.VMEM((IN_W, D_MODEL), BF16), pltpu.VMEM((IN_W, D_MODEL), F32), pltpu.SemaphoreType.DMA,
                        pltpu.SemaphoreType.DMA((len(pieces),))],
        compiler_params=_params(),
    )(x, dy, gain, wt, dg, daq, dak, dav, dag, dmq, dmg)


AG_SEMS = 8


def _gather_stages(ins, lands, send_sems, recv_sems):
    n = len(ins)
    nrows = [a.shape[0] for a in ins]
    x, y, c = lax.axis_index("x"), lax.axis_index("y"), lax.axis_index("c")
    sib, xn, yn = (x, y, 1 - c), (1 - x, y, c), (x, 1 - y, c)
    me, cx, cy, cd = 2 * x + y, 2 * (1 - x) + y, 2 * x + (1 - y), 2 * (1 - x) + (1 - y)

    def part(a, chip, hf, quarter=None):
        rows = nrows[a] // 2
        base = chip * nrows[a] + hf * rows
        if quarter is not None:
            rows = rows // 2
            base = base + quarter * rows
        return lands[a].at[pl.ds(pl.multiple_of(base, 16), rows), :]

    def copy(a, j, ref, to):
        k = AG_SEMS * a + j
        return pltpu.make_async_remote_copy(src_ref=ref, dst_ref=ref, send_sem=send_sems.at[k],
                                            recv_sem=recv_sems.at[k], device_id=to, device_id_type=MESH)

    def own(a):
        return [copy(a, 0, part(a, me, c), xn), copy(a, 1, part(a, me, c), yn)]

    def neighbours(a):
        return [copy(a, 4, part(a, cx, c, 1), yn), copy(a, 2, part(a, cx, c), sib),
                copy(a, 5, part(a, cy, c, 0), xn), copy(a, 3, part(a, cy, c), sib)]

    def diagonal(a):
        return [copy(a, 7, part(a, cd, c, 1), sib), copy(a, 6, part(a, cd, c, 0), sib)]

    def send_own():
        for a in range(n):
            lands[a][pl.ds(pl.multiple_of(me * nrows[a], 16), nrows[a]), :] = ins[a][...].astype(BF16)
            for cp in own(a):
                cp.start()

    def pass_on_neighbours():
        for a in range(n):
            copy(a, 0, part(a, cx, c), xn).wait_recv()
            copy(a, 1, part(a, cy, c), yn).wait_recv()
            for cp in neighbours(a):
                cp.start()

    def pass_on_diagonal():
        for a in range(n):
            copy(a, 4, part(a, cd, c, 1), yn).wait_recv()
            copy(a, 5, part(a, cd, c, 0), xn).wait_recv()
            for cp in diagonal(a):
                cp.start()

    def y_complete():
        for a in range(n):
            copy(a, 3, part(a, cy, 1 - c), sib).wait_recv()

    def x_complete():
        for a in range(n):
            copy(a, 2, part(a, cx, 1 - c), sib).wait_recv()

    def diagonal_complete():
        for a in range(n):
            copy(a, 6, part(a, cd, 1 - c, 0), sib).wait_recv()
            copy(a, 7, part(a, cd, 1 - c, 1), sib).wait_recv()

    def sends_done():
        for a in range(n):
            for cp in own(a) + neighbours(a) + diagonal(a):
                cp.wait_send()

    def finish():
        y_complete()
        x_complete()
        diagonal_complete()
        sends_done()

    return (send_own, pass_on_neighbours, pass_on_diagonal, finish), (y_complete, x_complete, diagonal_complete, sends_done)


RS_SEMS = 6
RS_KINDS = (((2, 2), 1, F32), ((2, 2), 1, F32), ((2, 2), 2, BF16), ((2, 2), 2, BF16), ((2, 2), 2, F32),
            ((2,), 2, BF16), ((2,), 2, BF16), ((2,), 1, F32))


def _rs_view(g):
    return g.reshape(2, 2, 2, g.shape[0] // 8, g.shape[1])


def _rs_scratch(shapes, in_vmem=False):
    kinds = RS_KINDS[1:] if in_vmem else RS_KINDS
    return [pltpu.VMEM(lead + (r // 8, w // split), dt) for lead, split, dt in kinds for r, w in shapes]


def _rs_stages(gs, outs, bufs, send_sems, recv_sems, local_sems, widths):
    n = len(gs)
    if len(bufs) < n * len(RS_KINDS):
        bufs = [None] * n + list(bufs)
    loc, ra, s_b, r_b, acc1, s_c, r_c, fin = (bufs[n * i:n * i + n] for i in range(len(RS_KINDS)))
    half_w = [w // 2 for w in widths]
    chips = [(xx, yy) for xx in range(2) for yy in range(2)]
    x, y, c = lax.axis_index("x"), lax.axis_index("y"), lax.axis_index("c")
    sib, xn, yn = (x, y, 1 - c), (1 - x, y, c), (x, 1 - y, c)

    def copy(a, j, src, dst, to):
        k = RS_SEMS * a + j
        return pltpu.make_async_remote_copy(src_ref=src, dst_ref=dst, send_sem=send_sems.at[k],
                                            recv_sem=recv_sems.at[k], device_id=to, device_id_type=MESH)

    def step_a(a):
        if callable(gs[a]):
            return [copy(a, 0, gs[a](xx, yy, 1 - c), ra[a].at[xx, yy], sib) for xx, yy in chips]
        return [copy(a, 0, gs[a].at[:, :, 1 - c], ra[a], sib),
                pltpu.make_async_copy(gs[a].at[:, :, c], loc[a], local_sems.at[a])]

    def finish_a(a):
        if callable(gs[a]):
            copy(a, 0, ra[a], ra[a], sib).wait()
            for xx, yy in chips:
                ra[a][xx, yy] = gs[a](xx, yy, c)[...] + ra[a][xx, yy]
        else:
            for cp in step_a(a):
                cp.wait()
            ra[a][...] = loc[a][...] + ra[a][...]

    def step_b(a):
        return copy(a, 1, s_b[a].at[0], r_b[a].at[0], xn), copy(a, 2, s_b[a].at[1], r_b[a].at[1], yn)

    def step_c(a):
        return copy(a, 3, s_c[a].at[0], r_c[a].at[0], yn), copy(a, 4, s_c[a].at[1], r_c[a].at[1], xn)

    def step_d(a, half):
        rows = fin[a].at[half]
        return copy(a, 5, rows, rows, sib)

    def start():
        for a in range(n):
            for cp in step_a(a):
                cp.start()

    def a_to_b():
        for a in range(n):
            finish_a(a)
            s_b[a][0] = ra[a][1 - x, :, :, :half_w[a]].astype(BF16)
            s_b[a][1] = ra[a][:, 1 - y, :, half_w[a]:].astype(BF16)
            for cp in step_b(a):
                cp.start()

    def b_to_c():
        for a in range(n):
            for cp in step_b(a):
                cp.wait()
            acc1[a][0] = ra[a][x, :, :, :half_w[a]] + r_b[a][0].astype(F32)
            acc1[a][1] = ra[a][:, y, :, half_w[a]:] + r_b[a][1].astype(F32)
            s_c[a][0] = acc1[a][0, 1 - y].astype(BF16)
            s_c[a][1] = acc1[a][1, 1 - x].astype(BF16)
            for cp in step_c(a):
                cp.start()

    def c_to_d():
        for a in range(n):
            for cp in step_c(a):
                cp.wait()
            fin[a][c, :, :half_w[a]] = acc1[a][0, y] + r_c[a][0].astype(F32)
            fin[a][c, :, half_w[a]:] = acc1[a][1, x] + r_c[a][1].astype(F32)
            step_d(a, c).start()

    def finish():
        for a in range(n):
            step_d(a, 1 - c).wait_recv()
            step_d(a, c).wait_send()
            pltpu.sync_copy(fin[a], outs[a])

    return start, a_to_b, b_to_c, c_to_d, finish


def _reduce_grads(gwt, g_ws, tiny):
    cw = gwt.shape[1] // RS_CHUNKS
    chunk_shape = (gwt.shape[0], cw)

    def body(g0, ws_in, tiny_in, *rest):
        outs, o_ws, o_tiny = rest[:RS_CHUNKS], rest[RS_CHUNKS], rest[RS_CHUNKS + 1]
        rest = rest[RS_CHUNKS + 2:]
        nb = len(RS_KINDS) * RS_CHUNKS
        sm, sa, sb, sc, acc_s, send_sems, recv_sems, local_sems = rest[nb:]
        blocks = [g0.at[:, :, :, :, pl.ds(j * cw, cw)] for j in range(RS_CHUNKS)]
        start, a_to_b, b_to_c, c_to_d, finish = _rs_stages(blocks, outs, rest[:nb], send_sems, recv_sems, local_sems,
                                                           [cw] * RS_CHUNKS)
        n_ws = ws_in.shape[0]
        sm[0:n_ws, :] = ws_in[...]
        sm[n_ws:, :] = tiny_in[...]
        x, y, c = lax.axis_index("x"), lax.axis_index("y"), lax.axis_index("c")

        def small(j, src, dst, to):
            k = RS_SEMS * RS_CHUNKS + j
            return pltpu.make_async_remote_copy(src_ref=src, dst_ref=dst, send_sem=send_sems.at[k],
                                                recv_sem=recv_sems.at[k], device_id=to, device_id_type=MESH)

        along_c, along_x, along_y = (small(0, sm, sa, (x, y, 1 - c)), small(1, acc_s, sb, (1 - x, y, c)),
                                     small(2, sb, sc, (x, 1 - y, c)))
        start()
        along_c.start()
        a_to_b()
        along_c.wait()
        acc_s[...] = sm[...] + sa[...]
        along_x.start()
        b_to_c()
        along_x.wait()
        sb[...] = acc_s[...] + sb[...]
        along_y.start()
        c_to_d()
        along_y.wait()
        o_ws[...] = sb[0:n_ws, :] + sc[0:n_ws, :]
        o_tiny[...] = sb[n_ws:, :] + sc[n_ws:, :]
        finish()

    vm = pl.BlockSpec(memory_space=pltpu.VMEM)
    hbm = pl.BlockSpec(memory_space=pl.ANY)
    small_shape = (g_ws.shape[0] + tiny.shape[0], LANES)
    scratch = _rs_scratch([chunk_shape] * RS_CHUNKS) + [pltpu.VMEM(small_shape, F32) for _ in range(5)]
    nsem = RS_SEMS * RS_CHUNKS + 3
    scratch += [pltpu.SemaphoreType.DMA((nsem,)), pltpu.SemaphoreType.DMA((nsem,)), pltpu.SemaphoreType.DMA((RS_CHUNKS,))]
    return _call(
        body, name="reduce_grads",
        out_shape=[jax.ShapeDtypeStruct((2, gwt.shape[0] // 8, cw), F32)] * RS_CHUNKS
        + [jax.ShapeDtypeStruct(g_ws.shape, F32), jax.ShapeDtypeStruct(tiny.shape, F32)],
        in_specs=[hbm, vm, vm],
        out_specs=[hbm] * RS_CHUNKS + [vm, vm],
        scratch_shapes=scratch,
        compiler_params=_params(),
    )(_rs_view(gwt), g_ws, tiny)


def _adam_update(w, g, m, v):
    nm = ADAM_B1 * m + (1.0 - ADAM_B1) * g
    nv = ADAM_B2 * v + (1.0 - ADAM_B2) * (g * g)
    m_hat = nm / (1.0 - ADAM_B1 ** ADAM_STEP)
    v_hat = nv / (1.0 - ADAM_B2 ** ADAM_STEP)
    return -ADAM_LR * (m_hat / (jnp.sqrt(v_hat) + ADAM_EPS) + ADAM_WD * w), nm, nv


def _adamw(w, g, m, v):
    rows, cols = w.shape
    tm = max(t for t in range(8, 257, 8) if rows % t == 0)
    parts = tuple(g) if isinstance(g, (tuple, list)) else (g,)
    n = len(parts)

    def body(w_ref, m_ref, v_ref, *refs):
        gv = jnp.concatenate([r[...] for r in refs[:n]], axis=1)
        d_ref, nm_ref, nv_ref = refs[n:n + 3]
        d_ref[...], nm_ref[...], nv_ref[...] = _adam_update(w_ref[...], gv, m_ref[...], v_ref[...])
        if n > 1:
            refs[n + 3][...] = gv

    blk = pl.BlockSpec((tm, cols), lambda i: (i, 0))
    nout = 3 if n == 1 else 4
    res = _call(
        body, name="adamw", grid=(rows // tm,),
        in_specs=[blk] * 3 + [pl.BlockSpec((tm, p.shape[1]), lambda i: (i, 0)) for p in parts], out_specs=[blk] * nout,
        out_shape=[jax.ShapeDtypeStruct((rows, cols), F32)] * nout,
        compiler_params=_params(),
    )(w, m, v, *parts)
    return (parts[0] if n == 1 else res[3], *res[:3])


def _adamw_tiny(tiny, weights, ms, vs):
    shapes = [w.shape for w in weights]
    n = len(weights)

    def grad_of(t_ref, k, shape):
        base = 8 * k
        if shape[1] > LANES:
            return [t_ref[base + j:base + j + 1, :] for j in range(shape[1] // LANES)]
        return [t_ref[base:base + shape[0], 0:shape[1]]]

    def body(t_ref, *refs):
        w_refs, m_refs, v_refs = refs[:n], refs[n:2 * n], refs[2 * n:3 * n]
        loss_ref, outs = refs[3 * n], refs[3 * n + 1:]
        loss_ref[...] = (0.5 / D_MODEL) * jnp.sum(t_ref[8 * n:8 * n + 8, :], keepdims=True)
        for k, shape in enumerate(shapes):
            g_ref, d_ref, nm_ref, nv_ref = outs[4 * k:4 * k + 4]
            for j, g in enumerate(grad_of(t_ref, k, shape)):
                cols = slice(j * LANES, (j + 1) * LANES) if shape[1] > LANES else slice(None)
                g_ref[:, cols] = g
                d_ref[:, cols], nm_ref[:, cols], nv_ref[:, cols] = _adam_update(
                    w_refs[k][:, cols], g, m_refs[k][:, cols], v_refs[k][:, cols])

    out_shape = [jax.ShapeDtypeStruct((1, 1), F32)]
    for shape in shapes:
        out_shape += [jax.ShapeDtypeStruct(shape, F32)] * 4
    return _call(body, name="adamw_tiny", out_shape=out_shape, compiler_params=_params())(tiny, *weights, *ms, *vs)


def _local_grads(x, mem, tgt, norm_gain, wt_sh, gmlp_v_gain, gmlp_w_s, gmlp_b, attn_q_gain, attn_k_gain,
                 mem_norm_gain, wkv_sh, mem_q_gain, mem_k_gain, wo_sh):
    vg = gmlp_v_gain.reshape(1, GMLP_W)
    bias_full = jnp.repeat(gmlp_b.T, HEAD_DIM, axis=1)
    gq2, gk2 = jnp.tile(attn_q_gain, (1, 2)), jnp.tile(attn_k_gain, (1, 2))
    qg4, kg4 = jnp.tile(mem_q_gain, (1, 4)), jnp.tile(mem_k_gain, (1, 4))

    proj, wt, wkv, wo = _gather_proj(x, norm_gain, wt_sh, wkv_sh, wo_sh)
    yg = _gmlp_fwd(proj, vg, gmlp_w_s, bias_full)
    o, lse, ya = _attn_fwd(proj, gq2, gk2)
    kv, hm = _mem_kv(mem, mem_norm_gain, wkv)
    om, ym = _mem_fwd(proj, kv, qg4, kg4)
    dy, dyc, g_wo, err2 = _out_loss(yg, ya, ym, x, tgt, wo)
    dmq, dmg, g_mq, g_mk, g_wkv, g_mng = _mem_bwd(proj, om, dyc, kv, hm, mem, mem_norm_gain, wkv, qg4, kg4)
    daq, dak, dav, dag, g_aq, g_ak, g_wkv_sh, g_wo_sh = _attn_bwd(proj, o, lse, dyc, gq2, gk2, g_wkv, g_wo)
    dg, g_ws, g_b, g_vg = _gmlp_bwd(proj, dyc, vg, gmlp_w_s, bias_full)
    gx, g_wt, g_ng = _proj_bwd(x, dy, norm_gain, wt, dg, daq, dak, dav, dag, dmq, dmg)

    tiny = jnp.concatenate([g_ng, g_vg, g_b, g_aq, g_ak, g_mng, g_mq, g_mk, err2], axis=0)
    return gx, g_wt, g_wkv_sh, g_wo_sh, g_ws.reshape(4 * CHUNK, CHUNK), tiny


def kernel(x, mem, norm_gain, w_in, gmlp_v_gain, gmlp_w_s, gmlp_b, attn_q_gain, attn_k_gain, mem_norm_gain, w_mem_kv, mem_q_gain, mem_k_gain, w_out, loss_target, m_norm_gain, m_w_in, m_gmlp_v_gain, m_gmlp_w_s, m_gmlp_b, m_attn_q_gain, m_attn_k_gain, m_mem_norm_gain, m_w_mem_kv, m_mem_q_gain, m_mem_k_gain, m_w_out, v_norm_gain, v_w_in, v_gmlp_v_gain, v_gmlp_w_s, v_gmlp_b, v_attn_q_gain, v_attn_k_gain, v_mem_norm_gain, v_w_mem_kv, v_mem_q_gain, v_mem_k_gain, v_w_out):
    gx, g_wt, g_wkv_sh, g_wo_sh, g_ws, tiny = _local_grads(
        x[0], mem[0], loss_target[0], norm_gain, w_in[0].T, gmlp_v_gain[0], gmlp_w_s[0], gmlp_b[0],
        attn_q_gain, attn_k_gain, mem_norm_gain, w_mem_kv[0], mem_q_gain, mem_k_gain, w_out[0])
    *g_wt_sh, g_ws, tiny = _reduce_grads(g_wt, g_ws, tiny)
    chip_block = lambda g: g.reshape(2 * g.shape[1], g.shape[2])
    g_wt_sh = tuple(chip_block(g) for g in g_wt_sh)
    g_wkv_sh, g_wo_sh = chip_block(g_wkv_sh), chip_block(g_wo_sh)

    ws = (norm_gain, w_in, gmlp_v_gain, gmlp_w_s, gmlp_b, attn_q_gain, attn_k_gain, mem_norm_gain, w_mem_kv,
          mem_q_gain, mem_k_gain, w_out)
    ms = (m_norm_gain, m_w_in, m_gmlp_v_gain, m_gmlp_w_s, m_gmlp_b, m_attn_q_gain, m_attn_k_gain, m_mem_norm_gain,
          m_w_mem_kv, m_mem_q_gain, m_mem_k_gain, m_w_out)
    vs = (v_norm_gain, v_w_in, v_gmlp_v_gain, v_gmlp_w_s, v_gmlp_b, v_attn_q_gain, v_attn_k_gain, v_mem_norm_gain,
          v_w_mem_kv, v_mem_q_gain, v_mem_k_gain, v_w_out)
    form = {1: lambda a: a[0].T, 3: lambda a: a.reshape(4 * CHUNK, CHUNK), 2: lambda a: a[0], 4: lambda a: a[0],
            8: lambda a: a[0], 11: lambda a: a[0]}
    back = {1: lambda a: a.T[None], 3: lambda a: a.reshape(1, 4, CHUNK, CHUNK), 2: lambda a: a[None],
            4: lambda a: a[None], 8: lambda a: a[None], 11: lambda a: a[None]}
    fwd = lambda t, i: form.get(i, lambda a: a)(t[i])
    out = {}
    for i, g in ((1, g_wt_sh), (3, g_ws), (8, g_wkv_sh), (11, g_wo_sh)):
        out[i] = _adamw(fwd(ws, i), g, fwd(ms, i), fwd(vs, i))
    res = _adamw_tiny(tiny, [fwd(ws, i) for i in TINY_ORDER], [fwd(ms, i) for i in TINY_ORDER],
                      [fwd(vs, i) for i in TINY_ORDER])
    for k, i in enumerate(TINY_ORDER):
        out[i] = res[1 + 4 * k:5 + 4 * k]
    leaves = [[back.get(i, lambda a: a)(out[i][j]) for i in range(12)] for j in range(4)]
    return (res[0].reshape(()), gx[None], *leaves[0], *leaves[1], *leaves[2], *leaves[3])
```

```python
import functools
import math

import jax
import jax.numpy as jnp
from jax import lax
from jax.experimental import pallas as pl
from jax.experimental.pallas import tpu as pltpu

F32 = jnp.float32
BF16 = jnp.bfloat16

SEQ = 4096
D_MODEL = 1024
HEAD_DIM = 64
LANES = 128
CHUNK = 128
GMLP_W, ATTN_W, MEM_W = 256, 512, 256
IN_W = 3 * GMLP_W + 4 * ATTN_W + 2 * MEM_W
MEM_LEN = 256
DILATIONS = (16, 4, 1)
EPS = 1e-6
QK_SCALE = 1.0 / math.sqrt(HEAD_DIM)
C_GU, C_GV, C_GG, C_AQ, C_AK, C_AV, C_AG, C_MQ, C_MG = 0, 256, 512, 768, 1280, 1792, 2304, 2816, 3072

ADAM_LR, ADAM_B1, ADAM_B2, ADAM_EPS, ADAM_WD, ADAM_STEP = 0.001, 0.9, 0.999, 1e-08, 0.01, 10

VMEM_LIMIT = 48 * 1024 * 1024
RS_CHUNKS = 4
ATTN_UNROLL = 4
MESH = pl.DeviceIdType.MESH

TINY_ORDER = (0, 2, 4, 5, 6, 7, 9, 10)


def _call(body, **kw):
    return pl.pallas_call(body, **kw)


def _params(**kw):
    return pltpu.CompilerParams(vmem_limit_bytes=VMEM_LIMIT, **kw)


def _dot(a, b):
    return jnp.dot(a, b, preferred_element_type=F32)


def _dot_nt(a, b):
    return lax.dot_general(a, b, (((1,), (1,)), ((), ())), preferred_element_type=F32)


def _dot_tn(a, b):
    return lax.dot_general(a, b, (((0,), (0,)), ((), ())), preferred_element_type=F32)


def _head_blockdiag():
    r = lax.shift_right_logical(lax.broadcasted_iota(jnp.int32, (LANES, LANES), 0), 6)
    c = lax.shift_right_logical(lax.broadcasted_iota(jnp.int32, (LANES, LANES), 1), 6)
    return jnp.where(r == c, 1.0, 0.0).astype(BF16)


def _headsum(v, bd):
    hi = v.astype(BF16)
    lo = (v - hi.astype(F32)).astype(BF16)
    return _dot(hi, bd) + _dot(lo, bd)


def _lo_mask(rows):
    return lax.broadcasted_iota(jnp.int32, (rows, LANES), 1) < HEAD_DIM


def _sigmoid(x):
    return 1.0 / (1.0 + jnp.exp(-x))


def _fold_heads(v):
    return v + pltpu.roll(v, HEAD_DIM, 1)


def _put_rows(ref, vec, accumulate=False):
    for j in range(vec.shape[1] // LANES):
        piece = vec[:, j * LANES:(j + 1) * LANES]
        ref[j:j + 1, :] = ref[j:j + 1, :] + piece if accumulate else piece


def _gather_proj(x, gain, wt_sh, *ride_along):
    tm = 512
    nrow = SEQ // tm
    nride = len(ride_along)
    widths = (768, 896, 768, 896)
    pair = 2 * wt_sh.shape[0]
    assert pair % LANES == 0 and sum(widths[:2]) == pair

    def body(x_ref, g_ref, wt_sh_ref, *rest):
        shards, rest = rest[:nride], rest[nride:]
        proj_hbm, wt_hbm, gathered = rest[0], rest[1], rest[2:2 + nride]
        h_scr, land, res = rest[2 + nride:5 + nride]
        lands, (send0, recv0, send1, recv1, out_sems, copy_sems) = rest[5 + nride:5 + 2 * nride], rest[5 + 2 * nride:]
        u, i = pl.program_id(0), pl.program_id(1)
        cx_, cy_ = lax.axis_index("x"), lax.axis_index("y")
        (send_own, pass_on_neighbours, pass_on_diagonal, _), (y_complete, x_complete, diagonal_complete, sends_done) = (
            _gather_stages((wt_sh_ref,), (land,), send0, recv0))
        ride, _ = _gather_stages(shards, lands, send1, recv1)
        first = lambda k: (u == k) & (i == 0)
        last = (u == 3) & (i == nrow - 1)
        copies = [pltpu.make_async_copy(land, wt_hbm, copy_sems.at[0])] + [
            pltpu.make_async_copy(src, dst, copy_sems.at[1 + k]) for k, (src, dst) in enumerate(zip(lands, gathered))]

        pl.when(first(0))(send_own)

        @pl.when(u == 0)
        def _():
            xv = x_ref[...]
            ms = jnp.mean(xv * xv, axis=-1, keepdims=True)
            h_scr[pl.ds(pl.multiple_of(i * tm, tm), tm), :] = (xv * lax.rsqrt(ms + EPS) * g_ref[...]).astype(BF16)

        @pl.when(first(1))
        def _():
            pass_on_neighbours()
            ride[0]()
            y_complete()

        @pl.when(first(2))
        def _():
            x_complete()
            pass_on_diagonal()
            ride[1]()

        @pl.when(first(3))
        def _():
            diagonal_complete()
            copies[0].start()
            ride[2]()

        col0 = (pair * cx_ + 896 * cy_, pair * cx_ + 768 * (1 - cy_),
                pair * (1 - cx_) + 896 * cy_, pair * (1 - cx_) + 768 * (1 - cy_))
        slot = i % 2
        rows = pl.ds(pl.multiple_of(i * tm, tm), tm)

        def writeback(k, rows_):
            c0 = pl.multiple_of(col0[k], LANES)
            return pltpu.make_async_copy(res.at[slot, :, pl.ds(0, widths[k])], proj_hbm.at[rows_, pl.ds(c0, widths[k])],
                                         out_sems.at[slot])

        for k in range(4):
            @pl.when(u == k)
            def _(k=k):
                pl.when(i >= 2)(writeback(k, rows).wait)
                if k > 0:
                    pl.when(i < 2)(writeback(k - 1, rows).wait)
                w_rows = land[pl.ds(pl.multiple_of(col0[k], LANES), widths[k]), :]
                res[slot, :, 0:widths[k]] = _dot_nt(h_scr[rows, :], w_rows)
                writeback(k, rows).start()

        @pl.when(last)
        def _():
            sends_done()
            ride[3]()
            for cp in copies[1:]:
                cp.start()
            for cp in copies:
                cp.wait()
            pltpu.make_async_copy(res.at[0, :, pl.ds(0, widths[3])], proj_hbm.at[rows, pl.ds(0, widths[3])], out_sems.at[0]).wait()
            pltpu.make_async_copy(res.at[1, :, pl.ds(0, widths[3])], proj_hbm.at[rows, pl.ds(0, widths[3])], out_sems.at[1]).wait()

    full = [jax.ShapeDtypeStruct((4 * a.shape[0], a.shape[1]), BF16) for a in (wt_sh,) + ride_along]
    hbm = pl.BlockSpec(memory_space=pl.ANY)
    const = lambda a: pl.BlockSpec(a.shape, lambda u, i: (0, 0))
    return _call(
        body, name="gather_proj", grid=(4, nrow),
        in_specs=[pl.BlockSpec((tm, D_MODEL), lambda u, i: (jnp.where(u == 0, i, nrow - 1), 0)),
                  pl.BlockSpec((1, D_MODEL), lambda u, i: (0, 0)), const(wt_sh)] + [const(a) for a in ride_along],
        out_specs=[hbm] * (2 + nride),
        out_shape=[jax.ShapeDtypeStruct((SEQ, IN_W), F32)] + full,
        scratch_shapes=[pltpu.VMEM((SEQ, D_MODEL), BF16), pltpu.VMEM(full[0].shape, BF16), pltpu.VMEM((2, tm, max(widths)), F32)]
        + [pltpu.VMEM(s.shape, BF16) for s in full[1:]]
        + [pltpu.SemaphoreType.DMA((AG_SEMS,)), pltpu.SemaphoreType.DMA((AG_SEMS,)),
           pltpu.SemaphoreType.DMA((AG_SEMS * nride,)), pltpu.SemaphoreType.DMA((AG_SEMS * nride,)),
           pltpu.SemaphoreType.DMA((2,)), pltpu.SemaphoreType.DMA((1 + nride,))],
        compiler_params=_params(),
    )(x, gain, wt_sh, *ride_along)


def _gmlp_weights(w_ref):
    ti = lax.broadcasted_iota(jnp.int32, (CHUNK, CHUNK), 0)
    si = lax.broadcasted_iota(jnp.int32, (CHUNK, CHUNK), 1)
    tril = si <= ti
    return tril, [jnp.where(tril, w_ref[h], 0.0).astype(BF16) for h in range(4)]


def _gmlp_fwd(proj, vgain, w_s, bias_full):
    tm = 512

    def body(p_ref, vg_ref, w_ref, b_ref, y_ref):
        bd = _head_blockdiag()
        lo = _lo_mask(CHUNK)
        _, wm = _gmlp_weights(w_ref)
        units = [(pl.ds(c * CHUNK, CHUNK), p) for c in range(tm // CHUNK) for p in range(2)]
        col = lambda c0, p: slice(c0 + p * LANES, c0 + (p + 1) * LANES)
        vs = [p_ref[rows, col(C_GV, p)] for rows, p in units]
        rs = [lax.rsqrt(_headsum(v * v, bd) * (1.0 / HEAD_DIM) + EPS) for v in vs]
        vns = [(v * r * vg_ref[:, col(0, p)]).astype(BF16) for v, r, (_, p) in zip(vs, rs, units)]
        sps = [jnp.where(lo, _dot(wm[2 * p], vn), _dot(wm[2 * p + 1], vn)) + b_ref[:, col(0, p)] for vn, (_, p) in zip(vns, units)]
        for sp, (rows, p) in zip(sps, units):
            gt = p_ref[rows, col(C_GG, p)]
            y_ref[rows, col(0, p)] = (p_ref[rows, col(C_GU, p)] * sp * (gt * _sigmoid(gt))).astype(BF16)

    return _call(
        body, name="gmlp_fwd", grid=(SEQ // tm,),
        in_specs=[pl.BlockSpec((tm, 3 * GMLP_W), lambda i: (i, 0)),
                  pl.BlockSpec((1, GMLP_W), lambda i: (0, 0)),
                  pl.BlockSpec((4, CHUNK, CHUNK), lambda i: (0, 0, 0)),
                  pl.BlockSpec((CHUNK, GMLP_W), lambda i: (0, 0))],
        out_specs=pl.BlockSpec((tm, GMLP_W), lambda i: (i, 0)),
        out_shape=jax.ShapeDtypeStruct((SEQ, GMLP_W), BF16),
        compiler_params=_params(),
    )(proj, vgain, w_s, bias_full)


def _gmlp_bwd(proj, dyc, vgain, w_s, bias_full):
    tm = 512
    nsteps = SEQ // tm

    def body(p_ref, dy_ref, vg_ref, w_ref, b_ref, dg_ref, gw_ref, gb_ref, gv_ref):
        i = pl.program_id(0)
        bd = _head_blockdiag()
        lo = _lo_mask(CHUNK)
        tril, wm = _gmlp_weights(w_ref)
        ri = lax.broadcasted_iota(jnp.int32, (16, LANES), 0)
        li = lax.broadcasted_iota(jnp.int32, (16, LANES), 1)
        head_rows = [jnp.where(((ri == 2 * p) & (li < HEAD_DIM)) | ((ri == 2 * p + 1) & (li >= HEAD_DIM)), 1.0, 0.0).astype(BF16)
                     for p in range(2)]

        @pl.when(i == 0)
        def _():
            gw_ref[...] = jnp.zeros_like(gw_ref)
            gb_ref[...] = jnp.zeros_like(gb_ref)
            gv_ref[...] = jnp.zeros_like(gv_ref)

        units = [(pl.ds(c * CHUNK, CHUNK), p) for c in range(tm // CHUNK) for p in range(2)]
        col = lambda c0, p: slice(c0 + p * LANES, c0 + (p + 1) * LANES)
        vs = [p_ref[rows, col(C_GV, p)] for rows, p in units]
        rs = [lax.rsqrt(_headsum(v * v, bd) * (1.0 / HEAD_DIM) + EPS) for v in vs]
        zs = [v * r for v, r in zip(vs, rs)]
        vns = [(z * vg_ref[:, col(0, p)]).astype(BF16) for z, (_, p) in zip(zs, units)]
        sps = [jnp.where(lo, _dot(wm[2 * p], vn), _dot(wm[2 * p + 1], vn)) + b_ref[:, col(0, p)] for vn, (_, p) in zip(vns, units)]
        dsps = []
        for sp, (rows, p) in zip(sps, units):
            u = p_ref[rows, col(C_GU, p)]
            gt = p_ref[rows, col(C_GG, p)]
            dy = dy_ref[rows, col(0, p)]
            sg = _sigmoid(gt)
            sl = gt * sg
            dg_ref[rows, col(C_GU, p)] = (dy * sp * sl).astype(BF16)
            dg_ref[rows, col(C_GG, p)] = (dy * u * sp * (sg * (1.0 + gt * (1.0 - sg)))).astype(BF16)
            dsps.append(dy * u * sl)
        dspbs = [dsp.astype(BF16) for dsp in dsps]
        dvns = [jnp.where(lo, _dot_tn(wm[2 * p], dspb), _dot_tn(wm[2 * p + 1], dspb)) for dspb, (_, p) in zip(dspbs, units)]
        gws = [(_dot_nt(jnp.where(lo, dsp, 0.0).astype(BF16), vn), _dot_nt(jnp.where(lo, 0.0, dsp).astype(BF16), vn))
               for dsp, vn in zip(dsps, vns)]
        gbs = [(_dot_nt(head_rows[p], dspb) + _dot_nt(head_rows[p], (dsp - dspb.astype(F32)).astype(BF16)))[0:8]
               for dsp, dspb, (_, p) in zip(dsps, dspbs, units)]
        for p in range(2):
            mine = [n for n, (_, q) in enumerate(units) if q == p]
            gw_ref[2 * p] += sum(gws[n][0] for n in mine)
            gw_ref[2 * p + 1] += sum(gws[n][1] for n in mine)
            gvp = sum(jnp.sum(dvns[n] * zs[n], axis=0, keepdims=True) for n in mine)
            gv_ref[2 * p:2 * p + 1, :] += gvp
            gv_ref[2 * p + 1:2 * p + 2, :] += pltpu.roll(gvp, HEAD_DIM, 1)
        gb_ref[...] += sum(gbs)
        for dvn, z, r, (rows, p) in zip(dvns, zs, rs, units):
            dz = dvn * vg_ref[:, col(0, p)]
            dg_ref[rows, col(C_GV, p)] = (r * (dz - z * (_headsum(dz * z, bd) * (1.0 / HEAD_DIM)))).astype(BF16)

        @pl.when(i == nsteps - 1)
        def _():
            for h in range(4):
                gw_ref[h] = jnp.where(tril, gw_ref[h], 0.0)

    return _call(
        body, name="gmlp_bwd", grid=(nsteps,),
        in_specs=[pl.BlockSpec((tm, 3 * GMLP_W), lambda i: (i, 0)),
                  pl.BlockSpec((tm, GMLP_W), lambda i: (i, 0)),
                  pl.BlockSpec((1, GMLP_W), lambda i: (0, 0)),
                  pl.BlockSpec((4, CHUNK, CHUNK), lambda i: (0, 0, 0)),
                  pl.BlockSpec((CHUNK, GMLP_W), lambda i: (0, 0))],
        out_specs=[pl.BlockSpec((tm, 3 * GMLP_W), lambda i: (i, 0)),
                   pl.BlockSpec((4, CHUNK, CHUNK), lambda i: (0, 0, 0)),
                   pl.BlockSpec((8, LANES), lambda i: (0, 0)),
                   pl.BlockSpec((8, LANES), lambda i: (0, 0))],
        out_shape=[jax.ShapeDtypeStruct((SEQ, 3 * GMLP_W), BF16),
                   jax.ShapeDtypeStruct((4, CHUNK, CHUNK), F32),
                   jax.ShapeDtypeStruct((8, LANES), F32),
                   jax.ShapeDtypeStruct((8, LANES), F32)],
        compiler_params=_params(),
    )(proj, dyc, vgain, w_s, bias_full)


def _band_masks():
    qi = lax.broadcasted_iota(jnp.int32, (CHUNK, 2 * CHUNK), 0)
    kj = lax.broadcasted_iota(jnp.int32, (CHUNK, 2 * CHUNK), 1)
    valid2 = ((kj < CHUNK) & (kj >= qi)) | ((kj >= CHUNK) & (kj - CHUNK <= qi))
    q1 = lax.broadcasted_iota(jnp.int32, (CHUNK, CHUNK), 0)
    k1 = lax.broadcasted_iota(jnp.int32, (CHUNK, CHUNK), 1)
    return k1 <= q1, valid2


def _stack_heads(v, lo):
    return jnp.concatenate([jnp.where(lo, v, 0.0), jnp.where(lo, 0.0, v)], axis=0).astype(BF16)


def _rows_of(ref, start, d):
    if d == 1:
        return ref.at[pl.ds(start if isinstance(start, int) else pl.multiple_of(start, CHUNK), CHUNK), :]
    return ref.at[pl.ds(start, CHUNK, stride=d), :]


def _unrolled(lo, hi, unroll, run):
    groups = (hi - lo) // unroll
    if groups:
        def body(g, carry):
            run([lo + g * unroll + t for t in range(unroll)])
            return carry

        lax.fori_loop(0, groups, body, 0)
    if lo + groups * unroll < hi:
        run(range(lo + groups * unroll, hi))


def _for_blocks(d, group_fn, unroll):
    nblk = SEQ // CHUNK
    sh = d.bit_length() - 1

    def first(j):
        return (j * CHUNK if d == 1 else j, None)

    def rest(j):
        start = (j & (d - 1)) + (j >> sh) * (CHUNK * d)
        return (start, start - CHUNK * d)

    _unrolled(0, d, unroll, lambda js: group_fn(d, [first(j) for j in js]))
    _unrolled(d, nblk, unroll, lambda js: group_fn(d, [rest(j) for j in js]))


def _attn_fwd(proj, gq2, gk2):
    tn = 512

    def body(q_ref, k_ref, v_ref, g_ref, gq_ref, gk_ref, o_ref, l_ref, ya_ref, qn_ref, kn_ref):
        bd = _head_blockdiag()
        lo = _lo_mask(CHUNK)
        valid1, valid2 = _band_masks()

        def norm(t, carry):
            rows = pl.ds(pl.multiple_of(t * tn, tn), tn)
            q = q_ref[rows, :]
            qn_ref[rows, :] = q * lax.rsqrt(_headsum(q * q, bd) * (1.0 / HEAD_DIM) + EPS) * (gq_ref[...] * QK_SCALE)
            k = k_ref[rows, :]
            kn_ref[rows, :] = k * lax.rsqrt(_headsum(k * k, bd) * (1.0 / HEAD_DIM) + EPS) * gk_ref[...]
            return carry

        lax.fori_loop(0, SEQ // tn, norm, 0)

        def load_kv(ref, d, start, prev):
            own = _rows_of(ref, start, d)[...]
            if prev is None:
                return own.astype(BF16)
            return jnp.concatenate([_rows_of(ref, prev, d)[...], own], axis=0).astype(BF16)

        def group(d, blocks):
            valid = valid1 if blocks[0][1] is None else valid2
            valid = jnp.concatenate([valid, valid], axis=0)
            qs = [_rows_of(qn_ref, start, d)[...] for start, _ in blocks]
            ks = [load_kv(kn_ref, d, start, prev) for start, prev in blocks]
            vs = [load_kv(v_ref, d, start, prev) for start, prev in blocks]
            ss = [_dot_nt(_stack_heads(q, lo), k) for q, k in zip(qs, ks)]
            ms, ps, ls = [], [], []
            for s in ss:
                s = jnp.where(valid, s, -jnp.inf)
                m = jnp.max(s, axis=-1, keepdims=True)
                p = jnp.exp(s - m)
                ms.append(m)
                ls.append(jnp.sum(p, axis=-1, keepdims=True))
                ps.append(p.astype(BF16))
            os_ = [_dot(p, v) for p, v in zip(ps, vs)]
            for b, (start, _) in enumerate(blocks):
                heads = lambda v: jnp.where(lo, v[:CHUNK], v[CHUNK:])
                lsum = heads(ls[b])
                ob = heads(os_[b]) * (1.0 / lsum)
                lb = heads(ms[b]) + jnp.log(lsum)
                o_rows = _rows_of(o_ref, start, d)
                l_rows = _rows_of(l_ref, start, d)
                if d != DILATIONS[0]:
                    lold = l_rows[...]
                    mx = jnp.maximum(lold, lb)
                    ea = jnp.exp(lold - mx)
                    eb = jnp.exp(lb - mx)
                    inv = 1.0 / (ea + eb)
                    ob = o_rows[...] * (ea * inv) + ob * (eb * inv)
                    lb = mx + jnp.log(ea + eb)
                o_rows[...] = ob
                l_rows[...] = lb

        for d in DILATIONS:
            _for_blocks(d, group, ATTN_UNROLL)

        def fin(t, carry):
            rows = pl.ds(pl.multiple_of(t * tn, tn), tn)
            g = g_ref[rows, :]
            ya_ref[rows, :] = (o_ref[rows, :] * (g * _sigmoid(g))).astype(BF16)
            return carry

        lax.fori_loop(0, SEQ // tn, fin, 0)

    col = lambda c0: pl.BlockSpec((SEQ, LANES), lambda p: (0, c0 // LANES + p))
    vec = pl.BlockSpec((1, LANES), lambda p: (0, 0))
    out = pl.BlockSpec((SEQ, LANES), lambda p: (0, p))
    return _call(
        body, name="attn_fwd", grid=(ATTN_W // LANES,),
        in_specs=[col(C_AQ), col(C_AK), col(C_AV), col(C_AG), vec, vec],
        out_specs=[out, out, out],
        out_shape=[jax.ShapeDtypeStruct((SEQ, ATTN_W), F32), jax.ShapeDtypeStruct((SEQ, ATTN_W), F32),
                   jax.ShapeDtypeStruct((SEQ, ATTN_W), BF16)],
        scratch_shapes=[pltpu.VMEM((SEQ, LANES), F32), pltpu.VMEM((SEQ, LANES), F32)],
        compiler_params=_params(),
    )(proj, proj, proj, proj, gq2, gk2)


def _attn_bwd(proj, o, lse, dyc, gq2, gk2, *ride_along):
    tn = 512
    npairs = ATTN_W // LANES
    nride = len(ride_along)
    nbufs = nride * len(RS_KINDS)

    def body(proj_hbm, o_hbm, l_hbm, dyc_hbm, gq_ref, gk_ref, *rest):
        ride_in, rest = rest[:nride], rest[nride:]
        dq_ref, dk_ref, dv_ref, dgt_ref, gqg_ref, gkg_ref = rest[:6]
        ride_out, rest = rest[6:6 + nride], rest[6 + nride:]
        qb_, kb_, vb_, gb_, ob_, lb_, yb_, dkb_, dvb_, sems = rest[:10]
        rs_bufs, (send_sems, recv_sems, local_sems) = rest[10:10 + nbufs], rest[10 + nbufs:]
        rs_stage = _rs_stages(ride_in, ride_out, rs_bufs, send_sems, recv_sems, local_sems, [g.shape[1] for g in ride_along])
        pair = pl.program_id(0)
        for step in range(npairs):
            pl.when(pair == step)(rs_stage[step])
        bd = _head_blockdiag()
        lo = _lo_mask(CHUNK)
        lo2 = lax.broadcasted_iota(jnp.int32, (2 * CHUNK, LANES), 1) < HEAD_DIM
        valid1, valid2 = _band_masks()
        gqs = gq_ref[...] * QK_SCALE
        gk = gk_ref[...]

        def pcol(c0, of=None):
            return acol(proj_hbm, c0, of)

        def acol(hbm, c0=0, of=None):
            of = pair if of is None else of
            return hbm.at[:, pl.ds(pl.multiple_of(c0 + of * LANES, LANES), LANES)]

        def input_loads(of):
            return [pltpu.make_async_copy(src, dst, sems.at[n]) for n, (src, dst) in enumerate((
                (pcol(C_AQ, of), qb_), (pcol(C_AK, of), kb_), (pcol(C_AG, of), gb_), (acol(o_hbm, 0, of), ob_),
                (acol(dyc_hbm, GMLP_W, of), yb_), (pcol(C_AV, of), vb_), (acol(l_hbm, 0, of), lb_)))]

        early = (0, 1, 3, 4)
        loads = input_loads(pair)
        for n, cp in enumerate(loads):
            if n in early:
                pl.when(pair == 0)(cp.start)
            else:
                cp.start()

        @pl.when(pair == 0)
        def _():
            gqg_ref[...] = jnp.zeros_like(gqg_ref)
            gkg_ref[...] = jnp.zeros_like(gkg_ref)

        def pre_qk(t, carry):
            rows = pl.ds(pl.multiple_of(t * tn, tn), tn)
            q = qb_[rows, :]
            qb_[rows, :] = q * lax.rsqrt(_headsum(q * q, bd) * (1.0 / HEAD_DIM) + EPS) * gqs
            k = kb_[rows, :]
            kb_[rows, :] = k * lax.rsqrt(_headsum(k * k, bd) * (1.0 / HEAD_DIM) + EPS) * gk
            return carry

        def pre_gate(t, carry):
            rows = pl.ds(pl.multiple_of(t * tn, tn), tn)
            g = gb_[rows, :]
            ov = ob_[rows, :]
            dya = yb_[rows, :]
            sg = _sigmoid(g)
            dgt_ref[rows, :] = (dya * ov * (sg * (1.0 + g * (1.0 - sg)))).astype(BF16)
            do = dya * (g * sg)
            yb_[rows, :] = do
            ob_[rows, :] = _headsum(do * ov, bd)
            return carry

        loads[0].wait()
        loads[1].wait()
        lax.fori_loop(0, SEQ // tn, pre_qk, 0)
        for cp in loads[2:5]:
            cp.wait()
        lax.fori_loop(0, SEQ // tn, pre_gate, 0)
        loads[5].wait()
        loads[6].wait()

        def load_kv(ref, d, start, prev):
            own = _rows_of(ref, start, d)[...]
            if prev is None:
                return own.astype(BF16)
            return jnp.concatenate([_rows_of(ref, prev, d)[...], own], axis=0).astype(BF16)

        def group(d, blocks):
            first = blocks[0][1] is None
            valid, lok = (valid1, lo) if first else (valid2, lo2)
            chains = [(b, h) for b in range(len(blocks)) for h in range(2)]
            mask = lambda h: lo if h == 0 else ~lo
            qs = [_rows_of(qb_, start, d)[...] for start, _ in blocks]
            dos = [_rows_of(yb_, start, d)[...] for start, _ in blocks]
            lvs = [_rows_of(lb_, start, d)[...] for start, _ in blocks]
            dls = [_rows_of(ob_, start, d)[...] for start, _ in blocks]
            ks = [load_kv(kb_, d, start, prev) for start, prev in blocks]
            vs = [load_kv(vb_, d, start, prev) for start, prev in blocks]
            qbs = [q.astype(BF16) for q in qs]
            dobs = [do.astype(BF16) for do in dos]
            ss = [_dot_nt(jnp.where(mask(h), qs[b], 0.0).astype(BF16), ks[b]) for b, h in chains]
            dps = [_dot_nt(jnp.where(mask(h), dos[b], 0.0).astype(BF16), vs[b]) for b, h in chains]
            pbs, dss = [], []
            for s, dp, (b, h) in zip(ss, dps, chains):
                hc = h * HEAD_DIM
                p = jnp.exp(jnp.where(valid, s, -jnp.inf) - lvs[b][:, hc:hc + 1])
                pbs.append(p.astype(BF16))
                dss.append((p * (dp - dls[b][:, hc:hc + 1])).astype(BF16))
            dqs = [_dot(ds, ks[b]) for ds, (b, h) in zip(dss, chains)]
            dks = [_dot_tn(ds, qbs[b]) for ds, (b, h) in zip(dss, chains)]
            dvs = [_dot_tn(p, dobs[b]) for p, (b, h) in zip(pbs, chains)]
            assign = d == DILATIONS[0]
            for b, (start, prev) in enumerate(blocks):
                c0, c1 = 2 * b, 2 * b + 1
                dq_rows = _rows_of(gb_, start, d)
                dqb = jnp.where(lo, dqs[c0], dqs[c1])
                dq_rows[...] = dqb if assign else dq_rows[...] + dqb
                dkc = jnp.where(lok, dks[c0], dks[c1])
                dvc = jnp.where(lok, dvs[c0], dvs[c1])
                spans = ((start, slice(0, CHUNK), True),) if first else (
                    (prev, slice(0, CHUNK), False), (start, slice(CHUNK, 2 * CHUNK), True))
                for st, sl, own in spans:
                    dk_rows = _rows_of(dkb_, st, d)
                    dv_rows = _rows_of(dvb_, st, d)
                    if assign and own:
                        dk_rows[...] = dkc[sl]
                        dv_rows[...] = dvc[sl]
                    else:
                        dk_rows[...] = dk_rows[...] + dkc[sl]
                        dv_rows[...] = dv_rows[...] + dvc[sl]

        for d in DILATIONS:
            _for_blocks(d, group, ATTN_UNROLL)

        reloads = [pltpu.make_async_copy(pcol(C_AQ), vb_, sems.at[7]), pltpu.make_async_copy(pcol(C_AK), lb_, sems.at[8])]
        for cp in reloads:
            cp.start()

        @pl.when(pair < npairs - 1)
        def _():
            nxt = input_loads(pair + 1)
            for n in early:
                nxt[n].start()

        for cp in reloads:
            cp.wait()

        def post(t, carry):
            gq_acc, gk_acc = carry
            rows = pl.ds(pl.multiple_of(t * tn, tn), tn)
            outs = []
            for raw_, acc_, gain in ((vb_, gb_, gqs), (lb_, dkb_, gk)):
                a = raw_[rows, :]
                r = lax.rsqrt(_headsum(a * a, bd) * (1.0 / HEAD_DIM) + EPS)
                z = a * r
                dn = acc_[rows, :]
                dz = dn * gain
                outs.append((r * (dz - z * (_headsum(dz * z, bd) * (1.0 / HEAD_DIM))), jnp.sum(dn * z, axis=0, keepdims=True)))
            dq_ref[rows, :] = outs[0][0].astype(BF16)
            dk_ref[rows, :] = outs[1][0].astype(BF16)
            dv_ref[rows, :] = dvb_[rows, :].astype(BF16)
            return gq_acc + outs[0][1] * QK_SCALE, gk_acc + outs[1][1]

        zero = jnp.zeros((1, LANES), F32)
        gq_acc, gk_acc = lax.fori_loop(0, SEQ // tn, post, (zero, zero))
        gqg_ref[0:1, :] += gq_acc
        gkg_ref[0:1, :] += gk_acc

        @pl.when(pair == npairs - 1)
        def _():
            gqg_ref[0:1, :] = _fold_heads(gqg_ref[0:1, :])
            gkg_ref[0:1, :] = _fold_heads(gkg_ref[0:1, :])
            rs_stage[npairs]()

    hbm = pl.BlockSpec(memory_space=pl.ANY)
    vec = pl.BlockSpec((1, LANES), lambda p: (0, 0))
    blk8 = pl.BlockSpec((8, LANES), lambda p: (0, 0))
    out = pl.BlockSpec((SEQ, LANES), lambda p: (0, p))
    big = jax.ShapeDtypeStruct((SEQ, ATTN_W), BF16)
    nsem = RS_SEMS * nride
    return _call(
        body, name="attn_bwd", grid=(npairs,),
        in_specs=[hbm, hbm, hbm, hbm, vec, vec] + [hbm] * nride,
        out_specs=[out, out, out, out, blk8, blk8] + [hbm] * nride,
        out_shape=[big, big, big, big, jax.ShapeDtypeStruct((8, LANES), F32), jax.ShapeDtypeStruct((8, LANES), F32)]
        + [jax.ShapeDtypeStruct((2, g.shape[0] // 8, g.shape[1]), F32) for g in ride_along],
        scratch_shapes=[pltpu.VMEM((SEQ, LANES), F32) for _ in range(9)] + [pltpu.SemaphoreType.DMA((9,))]
        + _rs_scratch([g.shape for g in ride_along]) + [pltpu.SemaphoreType.DMA((nsem,)), pltpu.SemaphoreType.DMA((nsem,)),
                                     pltpu.SemaphoreType.DMA((nride,))],
        compiler_params=_params(),
    )(proj, o, lse, dyc, gq2, gk2, *[_rs_view(g) for g in ride_along])


def _mem_kv(mem, gain, wkv):
    def body(m_ref, g_ref, w_ref, kv_ref, hm_ref):
        mv = m_ref[...]
        ms = jnp.mean(mv * mv, axis=-1, keepdims=True)
        hm = (mv * lax.rsqrt(ms + EPS) * g_ref[...]).astype(BF16)
        hm_ref[...] = hm
        kv_ref[...] = _dot(hm, w_ref[...])

    return _call(
        body, name="mem_kv",
        out_shape=[jax.ShapeDtypeStruct((MEM_LEN, 2 * MEM_W), F32), jax.ShapeDtypeStruct((MEM_LEN, D_MODEL), BF16)],
        compiler_params=_params(),
    )(mem, gain, wkv)


def _mem_keys(kv_ref, kg_ref, bd, p):
    mk = kv_ref[:, p * LANES:(p + 1) * LANES]
    r = lax.rsqrt(_headsum(mk * mk, bd) * (1.0 / HEAD_DIM) + EPS)
    z = mk * r
    mkn = (z * kg_ref[:, p * LANES:(p + 1) * LANES]).astype(BF16)
    mvp = kv_ref[:, MEM_W + p * LANES:MEM_W + (p + 1) * LANES].astype(BF16)
    return mkn, mvp, r, z


def _mem_fwd(proj, kv, qg4, kg4):
    tm = 512

    def body(q_ref, g_ref, kv_ref, qg_ref, kg_ref, om_ref, ym_ref):
        bd = _head_blockdiag()
        lo = _lo_mask(tm)
        keys, qns = [], []
        for p in range(2):
            cs = slice(p * LANES, (p + 1) * LANES)
            keys.append(_mem_keys(kv_ref, kg_ref, bd, p)[:2])
            q = q_ref[:, cs]
            qns.append(q * lax.rsqrt(_headsum(q * q, bd) * (1.0 / HEAD_DIM) + EPS) * (qg_ref[:, cs] * QK_SCALE))
        chains = [(p, h) for p in range(2) for h in range(2)]
        ss = [_dot_nt(jnp.where(lo if h == 0 else ~lo, qns[p], 0.0).astype(BF16), keys[p][0]) for p, h in chains]
        es = [jnp.exp(s - jnp.max(s, axis=-1, keepdims=True)) for s in ss]
        os_ = [_dot(e.astype(BF16), keys[p][1]) for e, (p, h) in zip(es, chains)]
        res = [o * (1.0 / jnp.sum(e, axis=-1, keepdims=True)) for o, e in zip(os_, es)]
        for p in range(2):
            cs = slice(p * LANES, (p + 1) * LANES)
            ov = jnp.where(lo, res[2 * p], res[2 * p + 1])
            g = g_ref[:, cs]
            om_ref[:, cs] = ov
            ym_ref[:, cs] = (ov * (g * _sigmoid(g))).astype(BF16)

    vec = pl.BlockSpec((1, MEM_W), lambda i: (0, 0))
    return _call(
        body, name="mem_fwd", grid=(SEQ // tm,),
        in_specs=[pl.BlockSpec((tm, MEM_W), lambda i: (i, C_MQ // MEM_W)),
                  pl.BlockSpec((tm, MEM_W), lambda i: (i, C_MG // MEM_W)),
                  pl.BlockSpec((MEM_LEN, 2 * MEM_W), lambda i: (0, 0)), vec, vec],
        out_specs=[pl.BlockSpec((tm, MEM_W), lambda i: (i, 0)), pl.BlockSpec((tm, MEM_W), lambda i: (i, 0))],
        out_shape=[jax.ShapeDtypeStruct((SEQ, MEM_W), F32), jax.ShapeDtypeStruct((SEQ, MEM_W), BF16)],
        compiler_params=_params(),
    )(proj, proj, kv, qg4, kg4)


def _mem_bwd(proj, om, dyc, kv, hm, mem, mgain, wkv, qg4, kg4):
    tm = 512
    nsteps = SEQ // tm

    def body(q_ref, g_ref, om_ref, dy_ref, kv_ref, hm_ref, mem_ref, mg_ref, w_ref, qg_ref, kg_ref,
             dq_ref, dgt_ref, gqg_ref, gkg_ref, gw_ref, gmg_ref, dmk_ref, dmv_ref, gq_acc):
        i = pl.program_id(0)
        bd = _head_blockdiag()
        lo = _lo_mask(tm)
        lom = _lo_mask(MEM_LEN)

        @pl.when(i == 0)
        def _():
            dmk_ref[...] = jnp.zeros_like(dmk_ref)
            dmv_ref[...] = jnp.zeros_like(dmv_ref)
            gq_acc[...] = jnp.zeros_like(gq_acc)

        pairs = []
        for p in range(2):
            cs = slice(p * LANES, (p + 1) * LANES)
            mkn, mvp, _, _ = _mem_keys(kv_ref, kg_ref, bd, p)
            gqs = qg_ref[:, cs] * QK_SCALE
            q = q_ref[:, cs]
            r = lax.rsqrt(_headsum(q * q, bd) * (1.0 / HEAD_DIM) + EPS)
            z = q * r
            qn = z * gqs
            g = g_ref[:, cs]
            ov = om_ref[:, cs]
            dym = dy_ref[:, cs]
            sg = _sigmoid(g)
            dgt_ref[:, cs] = (dym * ov * (sg * (1.0 + g * (1.0 - sg)))).astype(BF16)
            do = dym * (g * sg)
            pairs.append(dict(cs=cs, mkn=mkn, mvp=mvp, gqs=gqs, r=r, z=z, qn=qn, qnb=qn.astype(BF16), do=do,
                              dob=do.astype(BF16), delta=_headsum(do * ov, bd)))
        chains = [(pr_, h) for pr_ in pairs for h in range(2)]
        mask = lambda h: lo if h == 0 else ~lo
        ss = [_dot_nt(jnp.where(mask(h), c["qn"], 0.0).astype(BF16), c["mkn"]) for c, h in chains]
        dps = [_dot_nt(jnp.where(mask(h), c["do"], 0.0).astype(BF16), c["mvp"]) for c, h in chains]
        prs, dss = [], []
        for s, dp, (c, h) in zip(ss, dps, chains):
            e = jnp.exp(s - jnp.max(s, axis=-1, keepdims=True))
            pr = e * (1.0 / jnp.sum(e, axis=-1, keepdims=True))
            prs.append(pr.astype(BF16))
            dss.append((pr * (dp - c["delta"][:, h * HEAD_DIM:h * HEAD_DIM + 1])).astype(BF16))
        dqs = [_dot(ds, c["mkn"]) for ds, (c, h) in zip(dss, chains)]
        dks = [_dot_tn(ds, c["qnb"]) for ds, (c, h) in zip(dss, chains)]
        dvs = [_dot_tn(pr, c["dob"]) for pr, (c, h) in zip(prs, chains)]
        for p, c in enumerate(pairs):
            cs, z, r = c["cs"], c["z"], c["r"]
            dqn = jnp.where(lo, dqs[2 * p], dqs[2 * p + 1])
            dmk_ref[:, cs] += jnp.where(lom, dks[2 * p], dks[2 * p + 1])
            dmv_ref[:, cs] += jnp.where(lom, dvs[2 * p], dvs[2 * p + 1])
            dz = dqn * c["gqs"]
            dq_ref[:, cs] = (r * (dz - z * (_headsum(dz * z, bd) * (1.0 / HEAD_DIM)))).astype(BF16)
            gq_acc[:, cs] += jnp.sum(dqn * z, axis=0, keepdims=True) * QK_SCALE

        @pl.when(i == nsteps - 1)
        def _():
            gqg_ref[...] = jnp.zeros_like(gqg_ref)
            gkg_ref[...] = jnp.zeros_like(gkg_ref)
            gqg_ref[0:1, :] = _fold_heads(gq_acc[:, 0:LANES] + gq_acc[:, LANES:2 * LANES])
            dkv = []
            gk = jnp.zeros((1, LANES), F32)
            for p in range(2):
                cs = slice(p * LANES, (p + 1) * LANES)
                _, _, r, z = _mem_keys(kv_ref, kg_ref, bd, p)
                dn = dmk_ref[:, cs]
                dz = dn * kg_ref[:, cs]
                gk = gk + jnp.sum(dn * z, axis=0, keepdims=True)
                dkv.append(r * (dz - z * (_headsum(dz * z, bd) * (1.0 / HEAD_DIM))))
            gkg_ref[0:1, :] = _fold_heads(gk)
            dkvb = jnp.concatenate(dkv + [dmv_ref[...]], axis=1).astype(BF16)
            gw_ref[...] = _dot_tn(hm_ref[...], dkvb)
            dhm = _dot_nt(dkvb, w_ref[...])
            mv = mem_ref[...]
            zm = mv * lax.rsqrt(jnp.mean(mv * mv, axis=-1, keepdims=True) + EPS)
            _put_rows(gmg_ref, jnp.sum(dhm * zm, axis=0, keepdims=True))

    const = lambda shape: pl.BlockSpec(shape, lambda i: (0,) * len(shape))
    row = lambda j: pl.BlockSpec((tm, MEM_W), lambda i: (i, j))
    blk8 = jax.ShapeDtypeStruct((8, LANES), F32)
    return _call(
        body, name="mem_bwd", grid=(nsteps,),
        in_specs=[row(C_MQ // MEM_W), row(C_MG // MEM_W), row(0), row((GMLP_W + ATTN_W) // MEM_W),
                  const((MEM_LEN, 2 * MEM_W)), const((MEM_LEN, D_MODEL)), const((MEM_LEN, D_MODEL)),
                  const((1, D_MODEL)), const((D_MODEL, 2 * MEM_W)), const((1, MEM_W)), const((1, MEM_W))],
        out_specs=[row(0), row(0), const((8, LANES)), const((8, LANES)),
                   const((D_MODEL, 2 * MEM_W)), const((8, LANES))],
        out_shape=[jax.ShapeDtypeStruct((SEQ, MEM_W), BF16), jax.ShapeDtypeStruct((SEQ, MEM_W), BF16),
                   blk8, blk8, jax.ShapeDtypeStruct((D_MODEL, 2 * MEM_W), F32), blk8],
        scratch_shapes=[pltpu.VMEM((MEM_LEN, MEM_W), F32), pltpu.VMEM((MEM_LEN, MEM_W), F32),
                        pltpu.VMEM((1, MEM_W), F32)],
        compiler_params=_params(),
    )(proj, proj, om, dyc, kv, hm, mem, mgain, wkv, qg4, kg4)


def _out_loss(yg, ya, ym, x, tgt, wo):
    tm = 512
    nsteps = SEQ // tm
    parts = ((0, GMLP_W), (GMLP_W, ATTN_W), (GMLP_W + ATTN_W, MEM_W))

    def body(yg_ref, ya_ref, ym_ref, x_ref, t_ref, w_ref, dy_ref, dyc_ref, gw_ref, ls_ref):
        i = pl.program_id(0)

        @pl.when(i == 0)
        def _():
            gw_ref[...] = jnp.zeros_like(gw_ref)
            ls_ref[...] = jnp.zeros_like(ls_ref)

        ys = (yg_ref[...], ya_ref[...], ym_ref[...])
        y = sum(_dot(yv, w_ref[r0:r0 + n, :]) for yv, (r0, n) in zip(ys, parts))
        err = x_ref[...] + y - t_ref[...]
        _put_rows(ls_ref, jnp.sum(err * err, axis=0, keepdims=True), accumulate=True)
        dy = err * (1.0 / D_MODEL)
        dy_ref[...] = dy
        dyb = dy.astype(BF16)
        dyc_ref[...] = _dot_nt(dyb, w_ref[...])
        for yv, (r0, n) in zip(ys, parts):
            gw_ref[r0:r0 + n, :] += _dot_tn(yv, dyb)

    row = lambda w: pl.BlockSpec((tm, w), lambda i: (i, 0))
    const = lambda shape: pl.BlockSpec(shape, lambda i: (0, 0))
    return _call(
        body, name="out_loss", grid=(nsteps,),
        in_specs=[row(GMLP_W), row(ATTN_W), row(MEM_W), row(D_MODEL), row(D_MODEL), const((D_MODEL, D_MODEL))],
        out_specs=[row(D_MODEL), row(D_MODEL), const((D_MODEL, D_MODEL)), const((8, LANES))],
        out_shape=[jax.ShapeDtypeStruct((SEQ, D_MODEL), F32), jax.ShapeDtypeStruct((SEQ, D_MODEL), F32),
                   jax.ShapeDtypeStruct((D_MODEL, D_MODEL), F32), jax.ShapeDtypeStruct((8, LANES), F32)],
        compiler_params=_params(),
    )(yg, ya, ym, x, tgt, wo)


def _proj_bwd(x, dy, gain, wt, dg, daq, dak, dav, dag, dmq, dmg):
    tm = 512
    nsteps = SEQ // tm
    pieces = ((C_GU, 3 * GMLP_W), (C_AQ, ATTN_W), (C_AK, ATTN_W), (C_AV, ATTN_W), (C_AG, ATTN_W),
              (C_MQ, MEM_W), (C_MG, MEM_W))

    def body(x_ref, dy_ref, g_ref, wt_hbm, p0, p1, p2, p3, p4, p5, p6, gx_ref, gwt_hbm, gg_ref, wt_v, acc, wt_sem, out_sems):
        i = pl.program_id(0)
        wt_load = pltpu.make_async_copy(wt_hbm, wt_v, wt_sem)

        @pl.when(i == 0)
        def _():
            wt_load.start()
            acc[...] = jnp.zeros_like(acc)
            gg_ref[...] = jnp.zeros_like(gg_ref)

        xv = x_ref[...]
        r = lax.rsqrt(jnp.mean(xv * xv, axis=-1, keepdims=True) + EPS)
        z = xv * r
        g = g_ref[...]
        h = (z * g).astype(BF16)
        pl.when(i == 0)(wt_load.wait)
        flush = [pltpu.make_async_copy(acc.at[c0:c0 + w, :], gwt_hbm.at[c0:c0 + w, :], out_sems.at[n])
                 for n, (c0, w) in enumerate(pieces)]
        dh = jnp.zeros((tm, D_MODEL), F32)
        for n, (pref, (c0, w)) in enumerate(zip((p0, p1, p2, p3, p4, p5, p6), pieces)):
            dp = pref[...]
            dh = dh + _dot(dp, wt_v[c0:c0 + w, :])
            acc[c0:c0 + w, :] += _dot_tn(dp, h)
            pl.when(i == nsteps - 1)(flush[n].start)
        _put_rows(gg_ref, jnp.sum(dh * z, axis=0, keepdims=True), accumulate=True)
        dz = dh * g
        gx_ref[...] = dy_ref[...] + r * (dz - z * jnp.mean(dz * z, axis=-1, keepdims=True))

        @pl.when(i == nsteps - 1)
        def _():
            for cp in flush:
                cp.wait()

    row = lambda w: pl.BlockSpec((tm, w), lambda i: (i, 0))
    hbm = pl.BlockSpec(memory_space=pl.ANY)
    vec = pl.BlockSpec((1, D_MODEL), lambda i: (0, 0))
    return _call(
        body, name="proj_bwd", grid=(nsteps,),
        in_specs=[row(D_MODEL), row(D_MODEL), vec, hbm] + [row(w) for _, w in pieces],
        out_specs=[row(D_MODEL), hbm, pl.BlockSpec((8, LANES), lambda i: (0, 0))],
        out_shape=[jax.ShapeDtypeStruct((SEQ, D_MODEL), F32), jax.ShapeDtypeStruct((IN_W, D_MODEL), F32),
                   jax.ShapeDtypeStruct((8, LANES), F32)],
        scratch_shapes=[pltpu.VMEM((IN_W, D_MODEL), BF16), pltpu.VMEM((IN_W, D_MODEL), F32), pltpu.SemaphoreType.DMA,
                        pltpu.SemaphoreType.DMA((len(pieces),))],
        compiler_params=_params(),
    )(x, dy, gain, wt, dg, daq, dak, dav, dag, dmq, dmg)


AG_SEMS = 8


def _gather_stages(ins, lands, send_sems, recv_sems):
    n = len(ins)
    nrows = [a.shape[0] for a in ins]
    x, y, c = lax.axis_index("x"), lax.axis_index("y"), lax.axis_index("c")
    sib, xn, yn = (x, y, 1 - c), (1 - x, y, c), (x, 1 - y, c)
    me, cx, cy, cd = 2 * x + y, 2 * (1 - x) + y, 2 * x + (1 - y), 2 * (1 - x) + (1 - y)

    def part(a, chip, hf, quarter=None):
        rows = nrows[a] // 2
        base = chip * nrows[a] + hf * rows
        if quarter is not None:
            rows = rows // 2
            base = base + quarter * rows
        return lands[a].at[pl.ds(pl.multiple_of(base, 16), rows), :]

    def copy(a, j, ref, to):
        k = AG_SEMS * a + j
        return pltpu.make_async_remote_copy(src_ref=ref, dst_ref=ref, send_sem=send_sems.at[k],
                                            recv_sem=recv_sems.at[k], device_id=to, device_id_type=MESH)

    def own(a):
        return [copy(a, 0, part(a, me, c), xn), copy(a, 1, part(a, me, c), yn)]

    def neighbours(a):
        return [copy(a, 4, part(a, cx, c, 1), yn), copy(a, 2, part(a, cx, c), sib),
                copy(a, 5, part(a, cy, c, 0), xn), copy(a, 3, part(a, cy, c), sib)]

    def diagonal(a):
        return [copy(a, 7, part(a, cd, c, 1), sib), copy(a, 6, part(a, cd, c, 0), sib)]

    def send_own():
        for a in range(n):
            lands[a][pl.ds(pl.multiple_of(me * nrows[a], 16), nrows[a]), :] = ins[a][...].astype(BF16)
            for cp in own(a):
                cp.start()

    def pass_on_neighbours():
        for a in range(n):
            copy(a, 0, part(a, cx, c), xn).wait_recv()
            copy(a, 1, part(a, cy, c), yn).wait_recv()
            for cp in neighbours(a):
                cp.start()

    def pass_on_diagonal():
        for a in range(n):
            copy(a, 4, part(a, cd, c, 1), yn).wait_recv()
            copy(a, 5, part(a, cd, c, 0), xn).wait_recv()
            for cp in diagonal(a):
                cp.start()

    def y_complete():
        for a in range(n):
            copy(a, 3, part(a, cy, 1 - c), sib).wait_recv()

    def x_complete():
        for a in range(n):
            copy(a, 2, part(a, cx, 1 - c), sib).wait_recv()

    def diagonal_complete():
        for a in range(n):
            copy(a, 6, part(a, cd, 1 - c, 0), sib).wait_recv()
            copy(a, 7, part(a, cd, 1 - c, 1), sib).wait_recv()

    def sends_done():
        for a in range(n):
            for cp in own(a) + neighbours(a) + diagonal(a):
                cp.wait_send()

    def finish():
        y_complete()
        x_complete()
        diagonal_complete()
        sends_done()

    return (send_own, pass_on_neighbours, pass_on_diagonal, finish), (y_complete, x_complete, diagonal_complete, sends_done)


RS_SEMS = 6
RS_KINDS = (((2, 2), 1, F32), ((2, 2), 1, F32), ((2, 2), 2, BF16), ((2, 2), 2, BF16), ((2, 2), 2, F32),
            ((2,), 2, BF16), ((2,), 2, BF16), ((2,), 1, F32))


def _rs_view(g):
    return g.reshape(2, 2, 2, g.shape[0] // 8, g.shape[1])


def _rs_scratch(shapes, in_vmem=False):
    kinds = RS_KINDS[1:] if in_vmem else RS_KINDS
    return [pltpu.VMEM(lead + (r // 8, w // split), dt) for lead, split, dt in kinds for r, w in shapes]


def _rs_stages(gs, outs, bufs, send_sems, recv_sems, local_sems, widths):
    n = len(gs)
    if len(bufs) < n * len(RS_KINDS):
        bufs = [None] * n + list(bufs)
    loc, ra, s_b, r_b, acc1, s_c, r_c, fin = (bufs[n * i:n * i + n] for i in range(len(RS_KINDS)))
    half_w = [w // 2 for w in widths]
    chips = [(xx, yy) for xx in range(2) for yy in range(2)]
    x, y, c = lax.axis_index("x"), lax.axis_index("y"), lax.axis_index("c")
    sib, xn, yn = (x, y, 1 - c), (1 - x, y, c), (x, 1 - y, c)

    def copy(a, j, src, dst, to):
        k = RS_SEMS * a + j
        return pltpu.make_async_remote_copy(src_ref=src, dst_ref=dst, send_sem=send_sems.at[k],
                                            recv_sem=recv_sems.at[k], device_id=to, device_id_type=MESH)

    def step_a(a):
        if callable(gs[a]):
            return [copy(a, 0, gs[a](xx, yy, 1 - c), ra[a].at[xx, yy], sib) for xx, yy in chips]
        return [copy(a, 0, gs[a].at[:, :, 1 - c], ra[a], sib),
                pltpu.make_async_copy(gs[a].at[:, :, c], loc[a], local_sems.at[a])]

    def finish_a(a):
        if callable(gs[a]):
            copy(a, 0, ra[a], ra[a], sib).wait()
            for xx, yy in chips:
                ra[a][xx, yy] = gs[a](xx, yy, c)[...] + ra[a][xx, yy]
        else:
            for cp in step_a(a):
                cp.wait()
            ra[a][...] = loc[a][...] + ra[a][...]

    def step_b(a):
        return copy(a, 1, s_b[a].at[0], r_b[a].at[0], xn), copy(a, 2, s_b[a].at[1], r_b[a].at[1], yn)

    def step_c(a):
        return copy(a, 3, s_c[a].at[0], r_c[a].at[0], yn), copy(a, 4, s_c[a].at[1], r_c[a].at[1], xn)

    def step_d(a, half):
        rows = fin[a].at[half]
        return copy(a, 5, rows, rows, sib)

    def start():
        for a in range(n):
            for cp in step_a(a):
                cp.start()

    def a_to_b():
        for a in range(n):
            finish_a(a)
            s_b[a][0] = ra[a][1 - x, :, :, :half_w[a]].astype(BF16)
            s_b[a][1] = ra[a][:, 1 - y, :, half_w[a]:].astype(BF16)
            for cp in step_b(a):
                cp.start()

    def b_to_c():
        for a in range(n):
            for cp in step_b(a):
                cp.wait()
            acc1[a][0] = ra[a][x, :, :, :half_w[a]] + r_b[a][0].astype(F32)
            acc1[a][1] = ra[a][:, y, :, half_w[a]:] + r_b[a][1].astype(F32)
            s_c[a][0] = acc1[a][0, 1 - y].astype(BF16)
            s_c[a][1] = acc1[a][1, 1 - x].astype(BF16)
            for cp in step_c(a):
                cp.start()

    def c_to_d():
        for a in range(n):
            for cp in step_c(a):
                cp.wait()
            fin[a][c, :, :half_w[a]] = acc1[a][0, y] + r_c[a][0].astype(F32)
            fin[a][c, :, half_w[a]:] = acc1[a][1, x] + r_c[a][1].astype(F32)
            step_d(a, c).start()

    def finish():
        for a in range(n):
            step_d(a, 1 - c).wait_recv()
            step_d(a, c).wait_send()
            pltpu.sync_copy(fin[a], outs[a])

    return start, a_to_b, b_to_c, c_to_d, finish


def _reduce_grads(gwt, g_ws, tiny):
    cw = gwt.shape[1] // RS_CHUNKS
    chunk_shape = (gwt.shape[0], cw)

    def body(g0, ws_in, tiny_in, *rest):
        outs, o_ws, o_tiny = rest[:RS_CHUNKS], rest[RS_CHUNKS], rest[RS_CHUNKS + 1]
        rest = rest[RS_CHUNKS + 2:]
        nb = len(RS_KINDS) * RS_CHUNKS
        sm, sa, sb, sc, acc_s, send_sems, recv_sems, local_sems = rest[nb:]
        blocks = [g0.at[:, :, :, :, pl.ds(j * cw, cw)] for j in range(RS_CHUNKS)]
        start, a_to_b, b_to_c, c_to_d, finish = _rs_stages(blocks, outs, rest[:nb], send_sems, recv_sems, local_sems,
                                                           [cw] * RS_CHUNKS)
        n_ws = ws_in.shape[0]
        sm[0:n_ws, :] = ws_in[...]
        sm[n_ws:, :] = tiny_in[...]
        x, y, c = lax.axis_index("x"), lax.axis_index("y"), lax.axis_index("c")

        def small(j, src, dst, to):
            k = RS_SEMS * RS_CHUNKS + j
            return pltpu.make_async_remote_copy(src_ref=src, dst_ref=dst, send_sem=send_sems.at[k],
                                                recv_sem=recv_sems.at[k], device_id=to, device_id_type=MESH)

        along_c, along_x, along_y = (small(0, sm, sa, (x, y, 1 - c)), small(1, acc_s, sb, (1 - x, y, c)),
                                     small(2, sb, sc, (x, 1 - y, c)))
        start()
        along_c.start()
        a_to_b()
        along_c.wait()
        acc_s[...] = sm[...] + sa[...]
        along_x.start()
        b_to_c()
        along_x.wait()
        sb[...] = acc_s[...] + sb[...]
        along_y.start()
        c_to_d()
        along_y.wait()
        o_ws[...] = sb[0:n_ws, :] + sc[0:n_ws, :]
        o_tiny[...] = sb[n_ws:, :] + sc[n_ws:, :]
        finish()

    vm = pl.BlockSpec(memory_space=pltpu.VMEM)
    hbm = pl.BlockSpec(memory_space=pl.ANY)
    small_shape = (g_ws.shape[0] + tiny.shape[0], LANES)
    scratch = _rs_scratch([chunk_shape] * RS_CHUNKS) + [pltpu.VMEM(small_shape, F32) for _ in range(5)]
    nsem = RS_SEMS * RS_CHUNKS + 3
    scratch += [pltpu.SemaphoreType.DMA((nsem,)), pltpu.SemaphoreType.DMA((nsem,)), pltpu.SemaphoreType.DMA((RS_CHUNKS,))]
    return _call(
        body, name="reduce_grads",
        out_shape=[jax.ShapeDtypeStruct((2, gwt.shape[0] // 8, cw), F32)] * RS_CHUNKS
        + [jax.ShapeDtypeStruct(g_ws.shape, F32), jax.ShapeDtypeStruct(tiny.shape, F32)],
        in_specs=[hbm, vm, vm],
        out_specs=[hbm] * RS_CHUNKS + [vm, vm],
        scratch_shapes=scratch,
        compiler_params=_params(),
    )(_rs_view(gwt), g_ws, tiny)


def _adam_update(w, g, m, v):
    nm = ADAM_B1 * m + (1.0 - ADAM_B1) * g
    nv = ADAM_B2 * v + (1.0 - ADAM_B2) * (g * g)
    m_hat = nm / (1.0 - ADAM_B1 ** ADAM_STEP)
    v_hat = nv / (1.0 - ADAM_B2 ** ADAM_STEP)
    return -ADAM_LR * (m_hat / (jnp.sqrt(v_hat) + ADAM_EPS) + ADAM_WD * w), nm, nv


def _adamw(w, g, m, v):
    rows, cols = w.shape
    tm = max(t for t in range(8, 257, 8) if rows % t == 0)
    parts = tuple(g) if isinstance(g, (tuple, list)) else (g,)
    n = len(parts)

    def body(w_ref, m_ref, v_ref, *refs):
        gv = jnp.concatenate([r[...] for r in refs[:n]], axis=1)
        d_ref, nm_ref, nv_ref = refs[n:n + 3]
        d_ref[...], nm_ref[...], nv_ref[...] = _adam_update(w_ref[...], gv, m_ref[...], v_ref[...])
        if n > 1:
            refs[n + 3][...] = gv

    blk = pl.BlockSpec((tm, cols), lambda i: (i, 0))
    nout = 3 if n == 1 else 4
    res = _call(
        body, name="adamw", grid=(rows // tm,),
        in_specs=[blk] * 3 + [pl.BlockSpec((tm, p.shape[1]), lambda i: (i, 0)) for p in parts], out_specs=[blk] * nout,
        out_shape=[jax.ShapeDtypeStruct((rows, cols), F32)] * nout,
        compiler_params=_params(),
    )(w, m, v, *parts)
    return (parts[0] if n == 1 else res[3], *res[:3])


def _adamw_tiny(tiny, weights, ms, vs):
    shapes = [w.shape for w in weights]
    n = len(weights)

    def grad_of(t_ref, k, shape):
        base = 8 * k
        if shape[1] > LANES:
            return [t_ref[base + j:base + j + 1, :] for j in range(shape[1] // LANES)]
        return [t_ref[base:base + shape[0], 0:shape[1]]]

    def body(t_ref, *refs):
        w_refs, m_refs, v_refs = refs[:n], refs[n:2 * n], refs[2 * n:3 * n]
        loss_ref, outs = refs[3 * n], refs[3 * n + 1:]
        loss_ref[...] = (0.5 / D_MODEL) * jnp.sum(t_ref[8 * n:8 * n + 8, :], keepdims=True)
        for k, shape in enumerate(shapes):
            g_ref, d_ref, nm_ref, nv_ref = outs[4 * k:4 * k + 4]
            for j, g in enumerate(grad_of(t_ref, k, shape)):
                cols = slice(j * LANES, (j + 1) * LANES) if shape[1] > LANES else slice(None)
                g_ref[:, cols] = g
                d_ref[:, cols], nm_ref[:, cols], nv_ref[:, cols] = _adam_update(
                    w_refs[k][:, cols], g, m_refs[k][:, cols], v_refs[k][:, cols])

    out_shape = [jax.ShapeDtypeStruct((1, 1), F32)]
    for shape in shapes:
        out_shape += [jax.ShapeDtypeStruct(shape, F32)] * 4
    return _call(body, name="adamw_tiny", out_shape=out_shape, compiler_params=_params())(tiny, *weights, *ms, *vs)


def _local_grads(x, mem, tgt, norm_gain, wt_sh, gmlp_v_gain, gmlp_w_s, gmlp_b, attn_q_gain, attn_k_gain,
                 mem_norm_gain, wkv_sh, mem_q_gain, mem_k_gain, wo_sh):
    vg = gmlp_v_gain.reshape(1, GMLP_W)
    bias_full = jnp.repeat(gmlp_b.T, HEAD_DIM, axis=1)
    gq2, gk2 = jnp.tile(attn_q_gain, (1, 2)), jnp.tile(attn_k_gain, (1, 2))
    qg4, kg4 = jnp.tile(mem_q_gain, (1, 4)), jnp.tile(mem_k_gain, (1, 4))

    proj, wt, wkv, wo = _gather_proj(x, norm_gain, wt_sh, wkv_sh, wo_sh)
    yg = _gmlp_fwd(proj, vg, gmlp_w_s, bias_full)
    o, lse, ya = _attn_fwd(proj, gq2, gk2)
    kv, hm = _mem_kv(mem, mem_norm_gain, wkv)
    om, ym = _mem_fwd(proj, kv, qg4, kg4)
    dy, dyc, g_wo, err2 = _out_loss(yg, ya, ym, x, tgt, wo)
    dmq, dmg, g_mq, g_mk, g_wkv, g_mng = _mem_bwd(proj, om, dyc, kv, hm, mem, mem_norm_gain, wkv, qg4, kg4)
    daq, dak, dav, dag, g_aq, g_ak, g_wkv_sh, g_wo_sh = _attn_bwd(proj, o, lse, dyc, gq2, gk2, g_wkv, g_wo)
    dg, g_ws, g_b, g_vg = _gmlp_bwd(proj, dyc, vg, gmlp_w_s, bias_full)
    gx, g_wt, g_ng = _proj_bwd(x, dy, norm_gain, wt, dg, daq, dak, dav, dag, dmq, dmg)

    tiny = jnp.concatenate([g_ng, g_vg, g_b, g_aq, g_ak, g_mng, g_mq, g_mk, err2], axis=0)
    return gx, g_wt, g_wkv_sh, g_wo_sh, g_ws.reshape(4 * CHUNK, CHUNK), tiny


def kernel(x, mem, norm_gain, w_in, gmlp_v_gain, gmlp_w_s, gmlp_b, attn_q_gain, attn_k_gain, mem_norm_gain, w_mem_kv, mem_q_gain, mem_k_gain, w_out, loss_target, m_norm_gain, m_w_in, m_gmlp_v_gain, m_gmlp_w_s, m_gmlp_b, m_attn_q_gain, m_attn_k_gain, m_mem_norm_gain, m_w_mem_kv, m_mem_q_gain, m_mem_k_gain, m_w_out, v_norm_gain, v_w_in, v_gmlp_v_gain, v_gmlp_w_s, v_gmlp_b, v_attn_q_gain, v_attn_k_gain, v_mem_norm_gain, v_w_mem_kv, v_mem_q_gain, v_mem_k_gain, v_w_out):
    gx, g_wt, g_wkv_sh, g_wo_sh, g_ws, tiny = _local_grads(
        x[0], mem[0], loss_target[0], norm_gain, w_in[0].T, gmlp_v_gain[0], gmlp_w_s[0], gmlp_b[0],
        attn_q_gain, attn_k_gain, mem_norm_gain, w_mem_kv[0], mem_q_gain, mem_k_gain, w_out[0])
    *g_wt_sh, g_ws, tiny = _reduce_grads(g_wt, g_ws, tiny)
    chip_block = lambda g: g.reshape(2 * g.shape[1], g.shape[2])
    g_wt_sh = tuple(chip_block(g) for g in g_wt_sh)
    g_wkv_sh, g_wo_sh = chip_block(g_wkv_sh), chip_block(g_wo_sh)

    ws = (norm_gain, w_in, gmlp_v_gain, gmlp_w_s, gmlp_b, attn_q_gain, attn_k_gain, mem_norm_gain, w_mem_kv,
          mem_q_gain, mem_k_gain, w_out)
    ms = (m_norm_gain, m_w_in, m_gmlp_v_gain, m_gmlp_w_s, m_gmlp_b, m_attn_q_gain, m_attn_k_gain, m_mem_norm_gain,
          m_w_mem_kv, m_mem_q_gain, m_mem_k_gain, m_w_out)
    vs = (v_norm_gain, v_w_in, v_gmlp_v_gain, v_gmlp_w_s, v_gmlp_b, v_attn_q_gain, v_attn_k_gain, v_mem_norm_gain,
          v_w_mem_kv, v_mem_q_gain, v_mem_k_gain, v_w_out)
    form = {1: lambda a: a[0].T, 3: lambda a: a.reshape(4 * CHUNK, CHUNK), 2: lambda a: a[0], 4: lambda a: a[0],
            8: lambda a: a[0], 11: lambda a: a[0]}
    back = {1: lambda a: a.T[None], 3: lambda a: a.reshape(1, 4, CHUNK, CHUNK), 2: lambda a: a[None],
            4: lambda a: a[None], 8: lambda a: a[None], 11: lambda a: a[None]}
    fwd = lambda t, i: form.get(i, lambda a: a)(t[i])
    out = {}
    for i, g in ((1, g_wt_sh), (3, g_ws), (8, g_wkv_sh), (11, g_wo_sh)):
        out[i] = _adamw(fwd(ws, i), g, fwd(ms, i), fwd(vs, i))
    res = _adamw_tiny(tiny, [fwd(ws, i) for i in TINY_ORDER], [fwd(ms, i) for i in TINY_ORDER],
                      [fwd(vs, i) for i in TINY_ORDER])
    for k, i in enumerate(TINY_ORDER):
        out[i] = res[1 + 4 * k:5 + 4 * k]
    leaves = [[back.get(i, lambda a: a)(out[i][j]) for i in range(12)] for j in range(4)]
    return (res[0].reshape(()), gx[None], *leaves[0], *leaves[1], *leaves[2], *leaves[3])
```

```python
import functools
import math

import jax
import jax.numpy as jnp
from jax import lax
from jax.experimental import pallas as pl
from jax.experimental.pallas import tpu as pltpu

F32 = jnp.float32
BF16 = jnp.bfloat16

SEQ = 4096
D_MODEL = 1024
HEAD_DIM = 64
LANES = 128
CHUNK = 128
GMLP_W, ATTN_W, MEM_W = 256, 512, 256
IN_W = 3 * GMLP_W + 4 * ATTN_W + 2 * MEM_W
MEM_LEN = 256
DILATIONS = (16, 4, 1)
EPS = 1e-6
QK_SCALE = 1.0 / math.sqrt(HEAD_DIM)
C_GU, C_GV, C_GG, C_AQ, C_AK, C_AV, C_AG, C_MQ, C_MG = 0, 256, 512, 768, 1280, 1792, 2304, 2816, 3072

ADAM_LR, ADAM_B1, ADAM_B2, ADAM_EPS, ADAM_WD, ADAM_STEP = 0.001, 0.9, 0.999, 1e-08, 0.01, 10

VMEM_LIMIT = 48 * 1024 * 1024
RS_CHUNKS = 4
ATTN_UNROLL = 4
MESH = pl.DeviceIdType.MESH

TINY_ORDER = (0, 2, 4, 5, 6, 7, 9, 10)


def _call(body, **kw):
    return pl.pallas_call(body, **kw)


def _params(**kw):
    return pltpu.CompilerParams(vmem_limit_bytes=VMEM_LIMIT, **kw)


def _dot(a, b):
    return jnp.dot(a, b, preferred_element_type=F32)


def _dot_nt(a, b):
    return lax.dot_general(a, b, (((1,), (1,)), ((), ())), preferred_element_type=F32)


def _dot_tn(a, b):
    return lax.dot_general(a, b, (((0,), (0,)), ((), ())), preferred_element_type=F32)


def _head_blockdiag():
    r = lax.shift_right_logical(lax.broadcasted_iota(jnp.int32, (LANES, LANES), 0), 6)
    c = lax.shift_right_logical(lax.broadcasted_iota(jnp.int32, (LANES, LANES), 1), 6)
    return jnp.where(r == c, 1.0, 0.0).astype(BF16)


def _headsum(v, bd):
    hi = v.astype(BF16)
    lo = (v - hi.astype(F32)).astype(BF16)
    return _dot(hi, bd) + _dot(lo, bd)


def _lo_mask(rows):
    return lax.broadcasted_iota(jnp.int32, (rows, LANES), 1) < HEAD_DIM


def _sigmoid(x):
    return 1.0 / (1.0 + jnp.exp(-x))


def _fold_heads(v):
    return v + pltpu.roll(v, HEAD_DIM, 1)


def _put_rows(ref, vec, accumulate=False):
    for j in range(vec.shape[1] // LANES):
        piece = vec[:, j * LANES:(j + 1) * LANES]
        ref[j:j + 1, :] = ref[j:j + 1, :] + piece if accumulate else piece


def _gather_proj(x, gain, wt_sh, *ride_along):
    tm = 512
    nrow = SEQ // tm
    nride = len(ride_along)
    widths = (768, 896, 768, 896)
    pair = 2 * wt_sh.shape[0]
    assert pair % LANES == 0 and sum(widths[:2]) == pair

    def body(x_ref, g_ref, wt_sh_ref, *rest):
        shards, rest = rest[:nride], rest[nride:]
        proj_hbm, wt_hbm, gathered = rest[0], rest[1], rest[2:2 + nride]
        h_scr, land, res = rest[2 + nride:5 + nride]
        lands, (send0, recv0, send1, recv1, out_sems, copy_sems) = rest[5 + nride:5 + 2 * nride], rest[5 + 2 * nride:]
        u, i = pl.program_id(0), pl.program_id(1)
        cx_, cy_ = lax.axis_index("x"), lax.axis_index("y")
        (send_own, pass_on_neighbours, pass_on_diagonal, _), (y_complete, x_complete, diagonal_complete, sends_done) = (
            _gather_stages((wt_sh_ref,), (land,), send0, recv0))
        ride, _ = _gather_stages(shards, lands, send1, recv1)
        first = lambda k: (u == k) & (i == 0)
        last = (u == 3) & (i == nrow - 1)
        copies = [pltpu.make_async_copy(land, wt_hbm, copy_sems.at[0])] + [
            pltpu.make_async_copy(src, dst, copy_sems.at[1 + k]) for k, (src, dst) in enumerate(zip(lands, gathered))]

        pl.when(first(0))(send_own)

        @pl.when(u == 0)
        def _():
            xv = x_ref[...]
            ms = jnp.mean(xv * xv, axis=-1, keepdims=True)
            h_scr[pl.ds(pl.multiple_of(i * tm, tm), tm), :] = (xv * lax.rsqrt(ms + EPS) * g_ref[...]).astype(BF16)

        @pl.when(first(1))
        def _():
            pass_on_neighbours()
            ride[0]()
            y_complete()

        @pl.when(first(2))
        def _():
            x_complete()
            pass_on_diagonal()
            ride[1]()

        @pl.when(first(3))
        def _():
            diagonal_complete()
            copies[0].start()
            ride[2]()

        col0 = (pair * cx_ + 896 * cy_, pair * cx_ + 768 * (1 - cy_),
                pair * (1 - cx_) + 896 * cy_, pair * (1 - cx_) + 768 * (1 - cy_))
        slot = i % 2
        rows = pl.ds(pl.multiple_of(i * tm, tm), tm)

        def writeback(k, rows_):
            c0 = pl.multiple_of(col0[k], LANES)
            return pltpu.make_async_copy(res.at[slot, :, pl.ds(0, widths[k])], proj_hbm.at[rows_, pl.ds(c0, widths[k])],
                                         out_sems.at[slot])

        for k in range(4):
            @pl.when(u == k)
            def _(k=k):
                pl.when(i >= 2)(writeback(k, rows).wait)
                if k > 0:
                    pl.when(i < 2)(writeback(k - 1, rows).wait)
                w_rows = land[pl.ds(pl.multiple_of(col0[k], LANES), widths[k]), :]
                res[slot, :, 0:widths[k]] = _dot_nt(h_scr[rows, :], w_rows)
                writeback(k, rows).start()

        @pl.when(last)
        def _():
            sends_done()
            ride[3]()
            for cp in copies[1:]:
                cp.start()
            for cp in copies:
                cp.wait()
            pltpu.make_async_copy(res.at[0, :, pl.ds(0, widths[3])], proj_hbm.at[rows, pl.ds(0, widths[3])], out_sems.at[0]).wait()
            pltpu.make_async_copy(res.at[1, :, pl.ds(0, widths[3])], proj_hbm.at[rows, pl.ds(0, widths[3])], out_sems.at[1]).wait()

    full = [jax.ShapeDtypeStruct((4 * a.shape[0], a.shape[1]), BF16) for a in (wt_sh,) + ride_along]
    hbm = pl.BlockSpec(memory_space=pl.ANY)
    const = lambda a: pl.BlockSpec(a.shape, lambda u, i: (0, 0))
    return _call(
        body, name="gather_proj", grid=(4, nrow),
        in_specs=[pl.BlockSpec((tm, D_MODEL), lambda u, i: (jnp.where(u == 0, i, nrow - 1), 0)),
                  pl.BlockSpec((1, D_MODEL), lambda u, i: (0, 0)), const(wt_sh)] + [const(a) for a in ride_along],
        out_specs=[hbm] * (2 + nride),
        out_shape=[jax.ShapeDtypeStruct((SEQ, IN_W), F32)] + full,
        scratch_shapes=[pltpu.VMEM((SEQ, D_MODEL), BF16), pltpu.VMEM(full[0].shape, BF16), pltpu.VMEM((2, tm, max(widths)), F32)]
        + [pltpu.VMEM(s.shape, BF16) for s in full[1:]]
        + [pltpu.SemaphoreType.DMA((AG_SEMS,)), pltpu.SemaphoreType.DMA((AG_SEMS,)),
           pltpu.SemaphoreType.DMA((AG_SEMS * nride,)), pltpu.SemaphoreType.DMA((AG_SEMS * nride,)),
           pltpu.SemaphoreType.DMA((2,)), pltpu.SemaphoreType.DMA((1 + nride,))],
        compiler_params=_params(),
    )(x, gain, wt_sh, *ride_along)


def _gmlp_weights(w_ref):
    ti = lax.broadcasted_iota(jnp.int32, (CHUNK, CHUNK), 0)
    si = lax.broadcasted_iota(jnp.int32, (CHUNK, CHUNK), 1)
    tril = si <= ti
    return tril, [jnp.where(tril, w_ref[h], 0.0).astype(BF16) for h in range(4)]


def _gmlp_fwd(proj, vgain, w_s, bias_full):
    tm = 512

    def body(p_ref, vg_ref, w_ref, b_ref, y_ref):
        bd = _head_blockdiag()
        lo = _lo_mask(CHUNK)
        _, wm = _gmlp_weights(w_ref)
        units = [(pl.ds(c * CHUNK, CHUNK), p) for c in range(tm // CHUNK) for p in range(2)]
        col = lambda c0, p: slice(c0 + p * LANES, c0 + (p + 1) * LANES)
        vs = [p_ref[rows, col(C_GV, p)] for rows, p in units]
        rs = [lax.rsqrt(_headsum(v * v, bd) * (1.0 / HEAD_DIM) + EPS) for v in vs]
        vns = [(v * r * vg_ref[:, col(0, p)]).astype(BF16) for v, r, (_, p) in zip(vs, rs, units)]
        sps = [jnp.where(lo, _dot(wm[2 * p], vn), _dot(wm[2 * p + 1], vn)) + b_ref[:, col(0, p)] for vn, (_, p) in zip(vns, units)]
        for sp, (rows, p) in zip(sps, units):
            gt = p_ref[rows, col(C_GG, p)]
            y_ref[rows, col(0, p)] = (p_ref[rows, col(C_GU, p)] * sp * (gt * _sigmoid(gt))).astype(BF16)

    return _call(
        body, name="gmlp_fwd", grid=(SEQ // tm,),
        in_specs=[pl.BlockSpec((tm, 3 * GMLP_W), lambda i: (i, 0)),
                  pl.BlockSpec((1, GMLP_W), lambda i: (0, 0)),
                  pl.BlockSpec((4, CHUNK, CHUNK), lambda i: (0, 0, 0)),
                  pl.BlockSpec((CHUNK, GMLP_W), lambda i: (0, 0))],
        out_specs=pl.BlockSpec((tm, GMLP_W), lambda i: (i, 0)),
        out_shape=jax.ShapeDtypeStruct((SEQ, GMLP_W), BF16),
        compiler_params=_params(),
    )(proj, vgain, w_s, bias_full)


def _gmlp_bwd(proj, dyc, vgain, w_s, bias_full):
    tm = 512
    nsteps = SEQ // tm

    def body(p_ref, dy_ref, vg_ref, w_ref, b_ref, dg_ref, gw_ref, gb_ref, gv_ref):
        i = pl.program_id(0)
        bd = _head_blockdiag()
        lo = _lo_mask(CHUNK)
        tril, wm = _gmlp_weights(w_ref)
        ri = lax.broadcasted_iota(jnp.int32, (16, LANES), 0)
        li = lax.broadcasted_iota(jnp.int32, (16, LANES), 1)
        head_rows = [jnp.where(((ri == 2 * p) & (li < HEAD_DIM)) | ((ri == 2 * p + 1) & (li >= HEAD_DIM)), 1.0, 0.0).astype(BF16)
                     for p in range(2)]

        @pl.when(i == 0)
        def _():
            gw_ref[...] = jnp.zeros_like(gw_ref)
            gb_ref[...] = jnp.zeros_like(gb_ref)
            gv_ref[...] = jnp.zeros_like(gv_ref)

        units = [(pl.ds(c * CHUNK, CHUNK), p) for c in range(tm // CHUNK) for p in range(2)]
        col = lambda c0, p: slice(c0 + p * LANES, c0 + (p + 1) * LANES)
        vs = [p_ref[rows, col(C_GV, p)] for rows, p in units]
        rs = [lax.rsqrt(_headsum(v * v, bd) * (1.0 / HEAD_DIM) + EPS) for v in vs]
        zs = [v * r for v, r in zip(vs, rs)]
        vns = [(z * vg_ref[:, col(0, p)]).astype(BF16) for z, (_, p) in zip(zs, units)]
        sps = [jnp.where(lo, _dot(wm[2 * p], vn), _dot(wm[2 * p + 1], vn)) + b_ref[:, col(0, p)] for vn, (_, p) in zip(vns, units)]
        dsps = []
        for sp, (rows, p) in zip(sps, units):
            u = p_ref[rows, col(C_GU, p)]
            gt = p_ref[rows, col(C_GG, p)]
            dy = dy_ref[rows, col(0, p)]
            sg = _sigmoid(gt)
            sl = gt * sg
            dg_ref[rows, col(C_GU, p)] = (dy * sp * sl).astype(BF16)
            dg_ref[rows, col(C_GG, p)] = (dy * u * sp * (sg * (1.0 + gt * (1.0 - sg)))).astype(BF16)
            dsps.append(dy * u * sl)
        dspbs = [dsp.astype(BF16) for dsp in dsps]
        dvns = [jnp.where(lo, _dot_tn(wm[2 * p], dspb), _dot_tn(wm[2 * p + 1], dspb)) for dspb, (_, p) in zip(dspbs, units)]
        gws = [(_dot_nt(jnp.where(lo, dsp, 0.0).astype(BF16), vn), _dot_nt(jnp.where(lo, 0.0, dsp).astype(BF16), vn))
               for dsp, vn in zip(dsps, vns)]
        gbs = [(_dot_nt(head_rows[p], dspb) + _dot_nt(head_rows[p], (dsp - dspb.astype(F32)).astype(BF16)))[0:8]
               for dsp, dspb, (_, p) in zip(dsps, dspbs, units)]
        for p in range(2):
            mine = [n for n, (_, q) in enumerate(units) if q == p]
            gw_ref[2 * p] += sum(gws[n][0] for n in mine)
            gw_ref[2 * p + 1] += sum(gws[n][1] for n in mine)
            gvp = sum(jnp.sum(dvns[n] * zs[n], axis=0, keepdims=True) for n in mine)
            gv_ref[2 * p:2 * p + 1, :] += gvp
            gv_ref[2 * p + 1:2 * p + 2, :] += pltpu.roll(gvp, HEAD_DIM, 1)
        gb_ref[...] += sum(gbs)
        for dvn, z, r, (rows, p) in zip(dvns, zs, rs, units):
            dz = dvn * vg_ref[:, col(0, p)]
            dg_ref[rows, col(C_GV, p)] = (r * (dz - z * (_headsum(dz * z, bd) * (1.0 / HEAD_DIM)))).astype(BF16)

        @pl.when(i == nsteps - 1)
        def _():
            for h in range(4):
                gw_ref[h] = jnp.where(tril, gw_ref[h], 0.0)

    return _call(
        body, name="gmlp_bwd", grid=(nsteps,),
        in_specs=[pl.BlockSpec((tm, 3 * GMLP_W), lambda i: (i, 0)),
                  pl.BlockSpec((tm, GMLP_W), lambda i: (i, 0)),
                  pl.BlockSpec((1, GMLP_W), lambda i: (0, 0)),
                  pl.BlockSpec((4, CHUNK, CHUNK), lambda i: (0, 0, 0)),
                  pl.BlockSpec((CHUNK, GMLP_W), lambda i: (0, 0))],
        out_specs=[pl.BlockSpec((tm, 3 * GMLP_W), lambda i: (i, 0)),
                   pl.BlockSpec((4, CHUNK, CHUNK), lambda i: (0, 0, 0)),
                   pl.BlockSpec((8, LANES), lambda i: (0, 0)),
                   pl.BlockSpec((8, LANES), lambda i: (0, 0))],
        out_shape=[jax.ShapeDtypeStruct((SEQ, 3 * GMLP_W), BF16),
                   jax.ShapeDtypeStruct((4, CHUNK, CHUNK), F32),
                   jax.ShapeDtypeStruct((8, LANES), F32),
                   jax.ShapeDtypeStruct((8, LANES), F32)],
        compiler_params=_params(),
    )(proj, dyc, vgain, w_s, bias_full)


def _band_masks():
    qi = lax.broadcasted_iota(jnp.int32, (CHUNK, 2 * CHUNK), 0)
    kj = lax.broadcasted_iota(jnp.int32, (CHUNK, 2 * CHUNK), 1)
    valid2 = ((kj < CHUNK) & (kj >= qi)) | ((kj >= CHUNK) & (kj - CHUNK <= qi))
    q1 = lax.broadcasted_iota(jnp.int32, (CHUNK, CHUNK), 0)
    k1 = lax.broadcasted_iota(jnp.int32, (CHUNK, CHUNK), 1)
    return k1 <= q1, valid2


def _stack_heads(v, lo):
    return jnp.concatenate([jnp.where(lo, v, 0.0), jnp.where(lo, 0.0, v)], axis=0).astype(BF16)


def _rows_of(ref, start, d):
    if d == 1:
        return ref.at[pl.ds(start if isinstance(start, int) else pl.multiple_of(start, CHUNK), CHUNK), :]
    return ref.at[pl.ds(start, CHUNK, stride=d), :]


def _unrolled(lo, hi, unroll, run):
    groups = (hi - lo) // unroll
    if groups:
        def body(g, carry):
            run([lo + g * unroll + t for t in range(unroll)])
            return carry

        lax.fori_loop(0, groups, body, 0)
    if lo + groups * unroll < hi:
        run(range(lo + groups * unroll, hi))


def _for_blocks(d, group_fn, unroll):
    nblk = SEQ // CHUNK
    sh = d.bit_length() - 1

    def first(j):
        return (j * CHUNK if d == 1 else j, None)

    def rest(j):
        start = (j & (d - 1)) + (j >> sh) * (CHUNK * d)
        return (start, start - CHUNK * d)

    _unrolled(0, d, unroll, lambda js: group_fn(d, [first(j) for j in js]))
    _unrolled(d, nblk, unroll, lambda js: group_fn(d, [rest(j) for j in js]))


def _attn_fwd(proj, gq2, gk2):
    tn = 512

    def body(q_ref, k_ref, v_ref, g_ref, gq_ref, gk_ref, o_ref, l_ref, ya_ref, qn_ref, kn_ref):
        bd = _head_blockdiag()
        lo = _lo_mask(CHUNK)
        valid1, valid2 = _band_masks()

        def norm(t, carry):
            rows = pl.ds(pl.multiple_of(t * tn, tn), tn)
            q = q_ref[rows, :]
            qn_ref[rows, :] = q * lax.rsqrt(_headsum(q * q, bd) * (1.0 / HEAD_DIM) + EPS) * (gq_ref[...] * QK_SCALE)
            k = k_ref[rows, :]
            kn_ref[rows, :] = k * lax.rsqrt(_headsum(k * k, bd) * (1.0 / HEAD_DIM) + EPS) * gk_ref[...]
            return carry

        lax.fori_loop(0, SEQ // tn, norm, 0)

        def load_kv(ref, d, start, prev):
            own = _rows_of(ref, start, d)[...]
            if prev is None:
                return own.astype(BF16)
            return jnp.concatenate([_rows_of(ref, prev, d)[...], own], axis=0).astype(BF16)

        def group(d, blocks):
            valid = valid1 if blocks[0][1] is None else valid2
            valid = jnp.concatenate([valid, valid], axis=0)
            qs = [_rows_of(qn_ref, start, d)[...] for start, _ in blocks]
            ks = [load_kv(kn_ref, d, start, prev) for start, prev in blocks]
            vs = [load_kv(v_ref, d, start, prev) for start, prev in blocks]
            ss = [_dot_nt(_stack_heads(q, lo), k) for q, k in zip(qs, ks)]
            ms, ps, ls = [], [], []
            for s in ss:
                s = jnp.where(valid, s, -jnp.inf)
                m = jnp.max(s, axis=-1, keepdims=True)
                p = jnp.exp(s - m)
                ms.append(m)
                ls.append(jnp.sum(p, axis=-1, keepdims=True))
                ps.append(p.astype(BF16))
            os_ = [_dot(p, v) for p, v in zip(ps, vs)]
            for b, (start, _) in enumerate(blocks):
                heads = lambda v: jnp.where(lo, v[:CHUNK], v[CHUNK:])
                lsum = heads(ls[b])
                ob = heads(os_[b]) * (1.0 / lsum)
                lb = heads(ms[b]) + jnp.log(lsum)
                o_rows = _rows_of(o_ref, start, d)
                l_rows = _rows_of(l_ref, start, d)
                if d != DILATIONS[0]:
                    lold = l_rows[...]
                    mx = jnp.maximum(lold, lb)
                    ea = jnp.exp(lold - mx)
                    eb = jnp.exp(lb - mx)
                    inv = 1.0 / (ea + eb)
                    ob = o_rows[...] * (ea * inv) + ob * (eb * inv)
                    lb = mx + jnp.log(ea + eb)
                o_rows[...] = ob
                l_rows[...] = lb

        for d in DILATIONS:
            _for_blocks(d, group, ATTN_UNROLL)

        def fin(t, carry):
            rows = pl.ds(pl.multiple_of(t * tn, tn), tn)
            g = g_ref[rows, :]
            ya_ref[rows, :] = (o_ref[rows, :] * (g * _sigmoid(g))).astype(BF16)
            return carry

        lax.fori_loop(0, SEQ // tn, fin, 0)

    col = lambda c0: pl.BlockSpec((SEQ, LANES), lambda p: (0, c0 // LANES + p))
    vec = pl.BlockSpec((1, LANES), lambda p: (0, 0))
    out = pl.BlockSpec((SEQ, LANES), lambda p: (0, p))
    return _call(
        body, name="attn_fwd", grid=(ATTN_W // LANES,),
        in_specs=[col(C_AQ), col(C_AK), col(C_AV), col(C_AG), vec, vec],
        out_specs=[out, out, out],
        out_shape=[jax.ShapeDtypeStruct((SEQ, ATTN_W), F32), jax.ShapeDtypeStruct((SEQ, ATTN_W), F32),
                   jax.ShapeDtypeStruct((SEQ, ATTN_W), BF16)],
        scratch_shapes=[pltpu.VMEM((SEQ, LANES), F32), pltpu.VMEM((SEQ, LANES), F32)],
        compiler_params=_params(),
    )(proj, proj, proj, proj, gq2, gk2)


def _attn_bwd(proj, o, lse, dyc, gq2, gk2, *ride_along):
    tn = 512
    npairs = ATTN_W // LANES
    nride = len(ride_along)
    nbufs = nride * len(RS_KINDS)

    def body(proj_hbm, o_hbm, l_hbm, dyc_hbm, gq_ref, gk_ref, *rest):
        ride_in, rest = rest[:nride], rest[nride:]
        dq_ref, dk_ref, dv_ref, dgt_ref, gqg_ref, gkg_ref = rest[:6]
        ride_out, rest = rest[6:6 + nride], rest[6 + nride:]
        qb_, kb_, vb_, gb_, ob_, lb_, yb_, dkb_, dvb_, sems = rest[:10]
        rs_bufs, (send_sems, recv_sems, local_sems) = rest[10:10 + nbufs], rest[10 + nbufs:]
        rs_stage = _rs_stages(ride_in, ride_out, rs_bufs, send_sems, recv_sems, local_sems, [g.shape[1] for g in ride_along])
        pair = pl.program_id(0)
        for step in range(npairs):
            pl.when(pair == step)(rs_stage[step])
        bd = _head_blockdiag()
        lo = _lo_mask(CHUNK)
        lo2 = lax.broadcasted_iota(jnp.int32, (2 * CHUNK, LANES), 1) < HEAD_DIM
        valid1, valid2 = _band_masks()
        gqs = gq_ref[...] * QK_SCALE
        gk = gk_ref[...]

        def pcol(c0, of=None):
            return acol(proj_hbm, c0, of)

        def acol(hbm, c0=0, of=None):
            of = pair if of is None else of
            return hbm.at[:, pl.ds(pl.multiple_of(c0 + of * LANES, LANES), LANES)]

        def input_loads(of):
            return [pltpu.make_async_copy(src, dst, sems.at[n]) for n, (src, dst) in enumerate((
                (pcol(C_AQ, of), qb_), (pcol(C_AK, of), kb_), (pcol(C_AG, of), gb_), (acol(o_hbm, 0, of), ob_),
                (acol(dyc_hbm, GMLP_W, of), yb_), (pcol(C_AV, of), vb_), (acol(l_hbm, 0, of), lb_)))]

        early = (0, 1, 3, 4)
        loads = input_loads(pair)
        for n, cp in enumerate(loads):
            if n in early:
                pl.when(pair == 0)(cp.start)
            else:
                cp.start()

        @pl.when(pair == 0)
        def _():
            gqg_ref[...] = jnp.zeros_like(gqg_ref)
            gkg_ref[...] = jnp.zeros_like(gkg_ref)

        def pre_qk(t, carry):
            rows = pl.ds(pl.multiple_of(t * tn, tn), tn)
            q = qb_[rows, :]
            qb_[rows, :] = q * lax.rsqrt(_headsum(q * q, bd) * (1.0 / HEAD_DIM) + EPS) * gqs
            k = kb_[rows, :]
            kb_[rows, :] = k * lax.rsqrt(_headsum(k * k, bd) * (1.0 / HEAD_DIM) + EPS) * gk
            return carry

        def pre_gate(t, carry):
            rows = pl.ds(pl.multiple_of(t * tn, tn), tn)
            g = gb_[rows, :]
            ov = ob_[rows, :]
            dya = yb_[rows, :]
            sg = _sigmoid(g)
            dgt_ref[rows, :] = (dya * ov * (sg * (1.0 + g * (1.0 - sg)))).astype(BF16)
            do = dya * (g * sg)
            yb_[rows, :] = do
            ob_[rows, :] = jnp.where(first_half, lb_[rows, :], _headsum(do * ov, bd))
            return carry

        first_half = (lax.broadcasted_iota(jnp.int32, (tn, LANES), 1) & (HEAD_DIM - 1)) < HEAD_DIM // 2
        loads[0].wait()
        loads[1].wait()
        lax.fori_loop(0, SEQ // tn, pre_qk, 0)
        for cp in loads[2:5] + loads[6:7]:
            cp.wait()
        lax.fori_loop(0, SEQ // tn, pre_gate, 0)
        loads[5].wait()
        reloads = [pltpu.make_async_copy(pcol(C_AQ), lb_, sems.at[7]), pltpu.make_async_copy(pcol(C_AK), vb_, sems.at[8])]
        reloads[0].start()

        def load_kv(ref, d, start, prev):
            own = _rows_of(ref, start, d)[...]
            if prev is None:
                return own.astype(BF16)
            return jnp.concatenate([_rows_of(ref, prev, d)[...], own], axis=0).astype(BF16)

        def group(d, blocks):
            first = blocks[0][1] is None
            valid, lok = (valid1, lo) if first else (valid2, lo2)
            chains = [(b, h) for b in range(len(blocks)) for h in range(2)]
            mask = lambda h: lo if h == 0 else ~lo
            qs = [_rows_of(qb_, start, d)[...] for start, _ in blocks]
            dos = [_rows_of(yb_, start, d)[...] for start, _ in blocks]
            lds = [_rows_of(ob_, start, d)[...] for start, _ in blocks]
            ks = [load_kv(kb_, d, start, prev) for start, prev in blocks]
            vs = [load_kv(vb_, d, start, prev) for start, prev in blocks]
            qbs = [q.astype(BF16) for q in qs]
            dobs = [do.astype(BF16) for do in dos]
            ss = [_dot_nt(jnp.where(mask(h), qs[b], 0.0).astype(BF16), ks[b]) for b, h in chains]
            dps = [_dot_nt(jnp.where(mask(h), dos[b], 0.0).astype(BF16), vs[b]) for b, h in chains]
            pbs, dss = [], []
            for s, dp, (b, h) in zip(ss, dps, chains):
                hc, dc = h * HEAD_DIM, h * HEAD_DIM + HEAD_DIM // 2
                p = jnp.exp(jnp.where(valid, s, -jnp.inf) - lds[b][:, hc:hc + 1])
                pbs.append(p.astype(BF16))
                dss.append((p * (dp - lds[b][:, dc:dc + 1])).astype(BF16))
            dqs = [_dot(ds, ks[b]) for ds, (b, h) in zip(dss, chains)]
            dks = [_dot_tn(ds, qbs[b]) for ds, (b, h) in zip(dss, chains)]
            dvs = [_dot_tn(p, dobs[b]) for p, (b, h) in zip(pbs, chains)]
            assign = d == DILATIONS[0]
            for b, (start, prev) in enumerate(blocks):
                c0, c1 = 2 * b, 2 * b + 1
                dq_rows = _rows_of(gb_, start, d)
                dqb = jnp.where(lo, dqs[c0], dqs[c1])
                dq_rows[...] = dqb if assign else dq_rows[...] + dqb
                dkc = jnp.where(lok, dks[c0], dks[c1])
                dvc = jnp.where(lok, dvs[c0], dvs[c1])
                spans = ((start, slice(0, CHUNK), True),) if first else (
                    (prev, slice(0, CHUNK), False), (start, slice(CHUNK, 2 * CHUNK), True))
                for st, sl, own in spans:
                    dk_rows = _rows_of(dkb_, st, d)
                    dv_rows = _rows_of(dvb_, st, d)
                    if assign and own:
                        dk_rows[...] = dkc[sl]
                        dv_rows[...] = dvc[sl]
                    else:
                        dk_rows[...] = dk_rows[...] + dkc[sl]
                        dv_rows[...] = dv_rows[...] + dvc[sl]

        for d in DILATIONS:
            _for_blocks(d, group, ATTN_UNROLL)

        reloads[1].start()

        @pl.when(pair < npairs - 1)
        def _():
            nxt = input_loads(pair + 1)
            for n in early:
                nxt[n].start()

        for cp in reloads:
            cp.wait()

        def post(t, carry):
            gq_acc, gk_acc = carry
            rows = pl.ds(pl.multiple_of(t * tn, tn), tn)
            outs = []
            for raw_, acc_, gain in ((lb_, gb_, gqs), (vb_, dkb_, gk)):
                a = raw_[rows, :]
                r = lax.rsqrt(_headsum(a * a, bd) * (1.0 / HEAD_DIM) + EPS)
                z = a * r
                dn = acc_[rows, :]
                dz = dn * gain
                outs.append((r * (dz - z * (_headsum(dz * z, bd) * (1.0 / HEAD_DIM))), jnp.sum(dn * z, axis=0, keepdims=True)))
            dq_ref[rows, :] = outs[0][0].astype(BF16)
            dk_ref[rows, :] = outs[1][0].astype(BF16)
            dv_ref[rows, :] = dvb_[rows, :].astype(BF16)
            return gq_acc + outs[0][1] * QK_SCALE, gk_acc + outs[1][1]

        zero = jnp.zeros((1, LANES), F32)
        gq_acc, gk_acc = lax.fori_loop(0, SEQ // tn, post, (zero, zero))
        gqg_ref[0:1, :] += gq_acc
        gkg_ref[0:1, :] += gk_acc

        @pl.when(pair == npairs - 1)
        def _():
            gqg_ref[0:1, :] = _fold_heads(gqg_ref[0:1, :])
            gkg_ref[0:1, :] = _fold_heads(gkg_ref[0:1, :])
            rs_stage[npairs]()

    hbm = pl.BlockSpec(memory_space=pl.ANY)
    vec = pl.BlockSpec((1, LANES), lambda p: (0, 0))
    blk8 = pl.BlockSpec((8, LANES), lambda p: (0, 0))
    out = pl.BlockSpec((SEQ, LANES), lambda p: (0, p))
    big = jax.ShapeDtypeStruct((SEQ, ATTN_W), BF16)
    nsem = RS_SEMS * nride
    return _call(
        body, name="attn_bwd", grid=(npairs,),
        in_specs=[hbm, hbm, hbm, hbm, vec, vec] + [hbm] * nride,
        out_specs=[out, out, out, out, blk8, blk8] + [hbm] * nride,
        out_shape=[big, big, big, big, jax.ShapeDtypeStruct((8, LANES), F32), jax.ShapeDtypeStruct((8, LANES), F32)]
        + [jax.ShapeDtypeStruct((2, g.shape[0] // 8, g.shape[1]), F32) for g in ride_along],
        scratch_shapes=[pltpu.VMEM((SEQ, LANES), F32) for _ in range(9)] + [pltpu.SemaphoreType.DMA((9,))]
        + _rs_scratch([g.shape for g in ride_along]) + [pltpu.SemaphoreType.DMA((nsem,)), pltpu.SemaphoreType.DMA((nsem,)),
                                     pltpu.SemaphoreType.DMA((nride,))],
        compiler_params=_params(),
    )(proj, o, lse, dyc, gq2, gk2, *[_rs_view(g) for g in ride_along])


def _mem_kv(mem, gain, wkv):
    def body(m_ref, g_ref, w_ref, kv_ref, hm_ref):
        mv = m_ref[...]
        ms = jnp.mean(mv * mv, axis=-1, keepdims=True)
        hm = (mv * lax.rsqrt(ms + EPS) * g_ref[...]).astype(BF16)
        hm_ref[...] = hm
        kv_ref[...] = _dot(hm, w_ref[...])

    return _call(
        body, name="mem_kv",
        out_shape=[jax.ShapeDtypeStruct((MEM_LEN, 2 * MEM_W), F32), jax.ShapeDtypeStruct((MEM_LEN, D_MODEL), BF16)],
        compiler_params=_params(),
    )(mem, gain, wkv)


def _mem_keys(kv_ref, kg_ref, bd, p):
    mk = kv_ref[:, p * LANES:(p + 1) * LANES]
    r = lax.rsqrt(_headsum(mk * mk, bd) * (1.0 / HEAD_DIM) + EPS)
    z = mk * r
    mkn = (z * kg_ref[:, p * LANES:(p + 1) * LANES]).astype(BF16)
    mvp = kv_ref[:, MEM_W + p * LANES:MEM_W + (p + 1) * LANES].astype(BF16)
    return mkn, mvp, r, z


def _mem_fwd(proj, kv, qg4, kg4):
    tm = 512

    def body(q_ref, g_ref, kv_ref, qg_ref, kg_ref, om_ref, ym_ref):
        bd = _head_blockdiag()
        lo = _lo_mask(tm)
        keys, qns = [], []
        for p in range(2):
            cs = slice(p * LANES, (p + 1) * LANES)
            keys.append(_mem_keys(kv_ref, kg_ref, bd, p)[:2])
            q = q_ref[:, cs]
            qns.append(q * lax.rsqrt(_headsum(q * q, bd) * (1.0 / HEAD_DIM) + EPS) * (qg_ref[:, cs] * QK_SCALE))
        chains = [(p, h) for p in range(2) for h in range(2)]
        ss = [_dot_nt(jnp.where(lo if h == 0 else ~lo, qns[p], 0.0).astype(BF16), keys[p][0]) for p, h in chains]
        es = [jnp.exp(s - jnp.max(s, axis=-1, keepdims=True)) for s in ss]
        os_ = [_dot(e.astype(BF16), keys[p][1]) for e, (p, h) in zip(es, chains)]
        res = [o * (1.0 / jnp.sum(e, axis=-1, keepdims=True)) for o, e in zip(os_, es)]
        for p in range(2):
            cs = slice(p * LANES, (p + 1) * LANES)
            ov = jnp.where(lo, res[2 * p], res[2 * p + 1])
            g = g_ref[:, cs]
            om_ref[:, cs] = ov
            ym_ref[:, cs] = (ov * (g * _sigmoid(g))).astype(BF16)

    vec = pl.BlockSpec((1, MEM_W), lambda i: (0, 0))
    return _call(
        body, name="mem_fwd", grid=(SEQ // tm,),
        in_specs=[pl.BlockSpec((tm, MEM_W), lambda i: (i, C_MQ // MEM_W)),
                  pl.BlockSpec((tm, MEM_W), lambda i: (i, C_MG // MEM_W)),
                  pl.BlockSpec((MEM_LEN, 2 * MEM_W), lambda i: (0, 0)), vec, vec],
        out_specs=[pl.BlockSpec((tm, MEM_W), lambda i: (i, 0)), pl.BlockSpec((tm, MEM_W), lambda i: (i, 0))],
        out_shape=[jax.ShapeDtypeStruct((SEQ, MEM_W), F32), jax.ShapeDtypeStruct((SEQ, MEM_W), BF16)],
        compiler_params=_params(),
    )(proj, proj, kv, qg4, kg4)


def _mem_bwd(proj, om, dyc, kv, hm, mem, mgain, wkv, qg4, kg4):
    tm = 512
    nsteps = SEQ // tm

    def body(q_ref, g_ref, om_ref, dy_ref, kv_ref, hm_ref, mem_ref, mg_ref, w_ref, qg_ref, kg_ref,
             dq_ref, dgt_ref, gqg_ref, gkg_ref, gw_ref, gmg_ref, dmk_ref, dmv_ref, gq_acc):
        i = pl.program_id(0)
        bd = _head_blockdiag()
        lo = _lo_mask(tm)
        lom = _lo_mask(MEM_LEN)

        @pl.when(i == 0)
        def _():
            dmk_ref[...] = jnp.zeros_like(dmk_ref)
            dmv_ref[...] = jnp.zeros_like(dmv_ref)
            gq_acc[...] = jnp.zeros_like(gq_acc)

        pairs = []
        for p in range(2):
            cs = slice(p * LANES, (p + 1) * LANES)
            mkn, mvp, _, _ = _mem_keys(kv_ref, kg_ref, bd, p)
            gqs = qg_ref[:, cs] * QK_SCALE
            q = q_ref[:, cs]
            r = lax.rsqrt(_headsum(q * q, bd) * (1.0 / HEAD_DIM) + EPS)
            z = q * r
            qn = z * gqs
            g = g_ref[:, cs]
            ov = om_ref[:, cs]
            dym = dy_ref[:, cs]
            sg = _sigmoid(g)
            dgt_ref[:, cs] = (dym * ov * (sg * (1.0 + g * (1.0 - sg)))).astype(BF16)
            do = dym * (g * sg)
            pairs.append(dict(cs=cs, mkn=mkn, mvp=mvp, gqs=gqs, r=r, z=z, qn=qn, qnb=qn.astype(BF16), do=do,
                              dob=do.astype(BF16), delta=_headsum(do * ov, bd)))
        chains = [(pr_, h) for pr_ in pairs for h in range(2)]
        mask = lambda h: lo if h == 0 else ~lo
        ss = [_dot_nt(jnp.where(mask(h), c["qn"], 0.0).astype(BF16), c["mkn"]) for c, h in chains]
        dps = [_dot_nt(jnp.where(mask(h), c["do"], 0.0).astype(BF16), c["mvp"]) for c, h in chains]
        prs, dss = [], []
        for s, dp, (c, h) in zip(ss, dps, chains):
            e = jnp.exp(s - jnp.max(s, axis=-1, keepdims=True))
            pr = e * (1.0 / jnp.sum(e, axis=-1, keepdims=True))
            prs.append(pr.astype(BF16))
            dss.append((pr * (dp - c["delta"][:, h * HEAD_DIM:h * HEAD_DIM + 1])).astype(BF16))
        dqs = [_dot(ds, c["mkn"]) for ds, (c, h) in zip(dss, chains)]
        dks = [_dot_tn(ds, c["qnb"]) for ds, (c, h) in zip(dss, chains)]
        dvs = [_dot_tn(pr, c["dob"]) for pr, (c, h) in zip(prs, chains)]
        for p, c in enumerate(pairs):
            cs, z, r = c["cs"], c["z"], c["r"]
            dqn = jnp.where(lo, dqs[2 * p], dqs[2 * p + 1])
            dmk_ref[:, cs] += jnp.where(lom, dks[2 * p], dks[2 * p + 1])
            dmv_ref[:, cs] += jnp.where(lom, dvs[2 * p], dvs[2 * p + 1])
            dz = dqn * c["gqs"]
            dq_ref[:, cs] = (r * (dz - z * (_headsum(dz * z, bd) * (1.0 / HEAD_DIM)))).astype(BF16)
            gq_acc[:, cs] += jnp.sum(dqn * z, axis=0, keepdims=True) * QK_SCALE

        @pl.when(i == nsteps - 1)
        def _():
            gqg_ref[...] = jnp.zeros_like(gqg_ref)
            gkg_ref[...] = jnp.zeros_like(gkg_ref)
            gqg_ref[0:1, :] = _fold_heads(gq_acc[:, 0:LANES] + gq_acc[:, LANES:2 * LANES])
            dkv = []
            gk = jnp.zeros((1, LANES), F32)
            for p in range(2):
                cs = slice(p * LANES, (p + 1) * LANES)
                _, _, r, z = _mem_keys(kv_ref, kg_ref, bd, p)
                dn = dmk_ref[:, cs]
                dz = dn * kg_ref[:, cs]
                gk = gk + jnp.sum(dn * z, axis=0, keepdims=True)
                dkv.append(r * (dz - z * (_headsum(dz * z, bd) * (1.0 / HEAD_DIM))))
            gkg_ref[0:1, :] = _fold_heads(gk)
            dkvb = jnp.concatenate(dkv + [dmv_ref[...]], axis=1).astype(BF16)
            gw_ref[...] = _dot_tn(hm_ref[...], dkvb)
            dhm = _dot_nt(dkvb, w_ref[...])
            mv = mem_ref[...]
            zm = mv * lax.rsqrt(jnp.mean(mv * mv, axis=-1, keepdims=True) + EPS)
            _put_rows(gmg_ref, jnp.sum(dhm * zm, axis=0, keepdims=True))

    const = lambda shape: pl.BlockSpec(shape, lambda i: (0,) * len(shape))
    row = lambda j: pl.BlockSpec((tm, MEM_W), lambda i: (i, j))
    blk8 = jax.ShapeDtypeStruct((8, LANES), F32)
    return _call(
        body, name="mem_bwd", grid=(nsteps,),
        in_specs=[row(C_MQ // MEM_W), row(C_MG // MEM_W), row(0), row((GMLP_W + ATTN_W) // MEM_W),
                  const((MEM_LEN, 2 * MEM_W)), const((MEM_LEN, D_MODEL)), const((MEM_LEN, D_MODEL)),
                  const((1, D_MODEL)), const((D_MODEL, 2 * MEM_W)), const((1, MEM_W)), const((1, MEM_W))],
        out_specs=[row(0), row(0), const((8, LANES)), const((8, LANES)),
                   const((D_MODEL, 2 * MEM_W)), const((8, LANES))],
        out_shape=[jax.ShapeDtypeStruct((SEQ, MEM_W), BF16), jax.ShapeDtypeStruct((SEQ, MEM_W), BF16),
                   blk8, blk8, jax.ShapeDtypeStruct((D_MODEL, 2 * MEM_W), F32), blk8],
        scratch_shapes=[pltpu.VMEM((MEM_LEN, MEM_W), F32), pltpu.VMEM((MEM_LEN, MEM_W), F32),
                        pltpu.VMEM((1, MEM_W), F32)],
        compiler_params=_params(),
    )(proj, proj, om, dyc, kv, hm, mem, mgain, wkv, qg4, kg4)


def _out_loss(yg, ya, ym, x, tgt, wo):
    tm = 512
    nsteps = SEQ // tm
    parts = ((0, GMLP_W), (GMLP_W, ATTN_W), (GMLP_W + ATTN_W, MEM_W))

    def body(yg_ref, ya_ref, ym_ref, x_ref, t_ref, w_ref, dy_ref, dyc_ref, gw_ref, ls_ref):
        i = pl.program_id(0)

        @pl.when(i == 0)
        def _():
            gw_ref[...] = jnp.zeros_like(gw_ref)
            ls_ref[...] = jnp.zeros_like(ls_ref)

        ys = (yg_ref[...], ya_ref[...], ym_ref[...])
        y = sum(_dot(yv, w_ref[r0:r0 + n, :]) for yv, (r0, n) in zip(ys, parts))
        err = x_ref[...] + y - t_ref[...]
        _put_rows(ls_ref, jnp.sum(err * err, axis=0, keepdims=True), accumulate=True)
        dy = err * (1.0 / D_MODEL)
        dy_ref[...] = dy
        dyb = dy.astype(BF16)
        dyc_ref[...] = _dot_nt(dyb, w_ref[...])
        for yv, (r0, n) in zip(ys, parts):
            gw_ref[r0:r0 + n, :] += _dot_tn(yv, dyb)

    row = lambda w: pl.BlockSpec((tm, w), lambda i: (i, 0))
    const = lambda shape: pl.BlockSpec(shape, lambda i: (0, 0))
    return _call(
        body, name="out_loss", grid=(nsteps,),
        in_specs=[row(GMLP_W), row(ATTN_W), row(MEM_W), row(D_MODEL), row(D_MODEL), const((D_MODEL, D_MODEL))],
        out_specs=[row(D_MODEL), row(D_MODEL), const((D_MODEL, D_MODEL)), const((8, LANES))],
        out_shape=[jax.ShapeDtypeStruct((SEQ, D_MODEL), F32), jax.ShapeDtypeStruct((SEQ, D_MODEL), F32),
                   jax.ShapeDtypeStruct((D_MODEL, D_MODEL), F32), jax.ShapeDtypeStruct((8, LANES), F32)],
        compiler_params=_params(),
    )(yg, ya, ym, x, tgt, wo)


def _proj_bwd(x, dy, gain, wt, dg, daq, dak, dav, dag, dmq, dmg):
    tm = 512
    nsteps = SEQ // tm
    pieces = ((C_GU, 3 * GMLP_W), (C_AQ, ATTN_W), (C_AK, ATTN_W), (C_AV, ATTN_W), (C_AG, ATTN_W),
              (C_MQ, MEM_W), (C_MG, MEM_W))

    def body(x_ref, dy_ref, g_ref, wt_hbm, p0, p1, p2, p3, p4, p5, p6, gx_ref, gwt_hbm, gg_ref, wt_v, acc, wt_sem, out_sems):
        i = pl.program_id(0)
        wt_load = pltpu.make_async_copy(wt_hbm, wt_v, wt_sem)

        @pl.when(i == 0)
        def _():
            wt_load.start()
            acc[...] = jnp.zeros_like(acc)
            gg_ref[...] = jnp.zeros_like(gg_ref)

        xv = x_ref[...]
        r = lax.rsqrt(jnp.mean(xv * xv, axis=-1, keepdims=True) + EPS)
        z = xv * r
        g = g_ref[...]
        h = (z * g).astype(BF16)
        pl.when(i == 0)(wt_load.wait)
        flush = [pltpu.make_async_copy(acc.at[c0:c0 + w, :], gwt_hbm.at[c0:c0 + w, :], out_sems.at[n])
                 for n, (c0, w) in enumerate(pieces)]
        dh = jnp.zeros((tm, D_MODEL), F32)
        for n, (pref, (c0, w)) in enumerate(zip((p0, p1, p2, p3, p4, p5, p6), pieces)):
            dp = pref[...]
            dh = dh + _dot(dp, wt_v[c0:c0 + w, :])
            acc[c0:c0 + w, :] += _dot_tn(dp, h)
            pl.when(i == nsteps - 1)(flush[n].start)
        _put_rows(gg_ref, jnp.sum(dh * z, axis=0, keepdims=True), accumulate=True)
        dz = dh * g
        gx_ref[...] = dy_ref[...] + r * (dz - z * jnp.mean(dz * z, axis=-1, keepdims=True))

        @pl.when(i == nsteps - 1)
        def _():
            for cp in flush:
                cp.wait()

    row = lambda w: pl.BlockSpec((tm, w), lambda i: (i, 0))
    hbm = pl.BlockSpec(memory_space=pl.ANY)
    vec = pl.BlockSpec((1, D_MODEL), lambda i: (0, 0))
    return _call(
        body, name="proj_bwd", grid=(nsteps,),
        in_specs=[row(D_MODEL), row(D_MODEL), vec, hbm] + [row(w) for _, w in pieces],
        out_specs=[row(D_MODEL), hbm, pl.BlockSpec((8, LANES), lambda i: (0, 0))],
        out_shape=[jax.ShapeDtypeStruct((SEQ, D_MODEL), F32), jax.ShapeDtypeStruct((IN_W, D_MODEL), F32),
                   jax.ShapeDtypeStruct((8, LANES), F32)],
        scratch_shapes=[pltpu.VMEM((IN_W, D_MODEL), BF16), pltpu.VMEM((IN_W, D_MODEL), F32), pltpu.SemaphoreType.DMA,
                        pltpu.SemaphoreType.DMA((len(pieces),))],
        compiler_params=_params(),
    )(x, dy, gain, wt, dg, daq, dak, dav, dag, dmq, dmg)


AG_SEMS = 8


def _gather_stages(ins, lands, send_sems, recv_sems):
    n = len(ins)
    nrows = [a.shape[0] for a in ins]
    x, y, c = lax.axis_index("x"), lax.axis_index("y"), lax.axis_index("c")
    sib, xn, yn = (x, y, 1 - c), (1 - x, y, c), (x, 1 - y, c)
    me, cx, cy, cd = 2 * x + y, 2 * (1 - x) + y, 2 * x + (1 - y), 2 * (1 - x) + (1 - y)

    def part(a, chip, hf, quarter=None):
        rows = nrows[a] // 2
        base = chip * nrows[a] + hf * rows
        if quarter is not None:
            rows = rows // 2
            base = base + quarter * rows
        return lands[a].at[pl.ds(pl.multiple_of(base, 16), rows), :]

    def copy(a, j, ref, to):
        k = AG_SEMS * a + j
        return pltpu.make_async_remote_copy(src_ref=ref, dst_ref=ref, send_sem=send_sems.at[k],
                                            recv_sem=recv_sems.at[k], device_id=to, device_id_type=MESH)

    def own(a):
        return [copy(a, 0, part(a, me, c), xn), copy(a, 1, part(a, me, c), yn)]

    def neighbours(a):
        return [copy(a, 4, part(a, cx, c, 1), yn), copy(a, 2, part(a, cx, c), sib),
                copy(a, 5, part(a, cy, c, 0), xn), copy(a, 3, part(a, cy, c), sib)]

    def diagonal(a):
        return [copy(a, 7, part(a, cd, c, 1), sib), copy(a, 6, part(a, cd, c, 0), sib)]

    def send_own():
        for a in range(n):
            lands[a][pl.ds(pl.multiple_of(me * nrows[a], 16), nrows[a]), :] = ins[a][...].astype(BF16)
            for cp in own(a):
                cp.start()

    def pass_on_neighbours():
        for a in range(n):
            copy(a, 0, part(a, cx, c), xn).wait_recv()
            copy(a, 1, part(a, cy, c), yn).wait_recv()
            for cp in neighbours(a):
                cp.start()

    def pass_on_diagonal():
        for a in range(n):
            copy(a, 4, part(a, cd, c, 1), yn).wait_recv()
            copy(a, 5, part(a, cd, c, 0), xn).wait_recv()
            for cp in diagonal(a):
                cp.start()

    def y_complete():
        for a in range(n):
            copy(a, 3, part(a, cy, 1 - c), sib).wait_recv()

    def x_complete():
        for a in range(n):
            copy(a, 2, part(a, cx, 1 - c), sib).wait_recv()

    def diagonal_complete():
        for a in range(n):
            copy(a, 6, part(a, cd, 1 - c, 0), sib).wait_recv()
            copy(a, 7, part(a, cd, 1 - c, 1), sib).wait_recv()

    def sends_done():
        for a in range(n):
            for cp in own(a) + neighbours(a) + diagonal(a):
                cp.wait_send()

    def finish():
        y_complete()
        x_complete()
        diagonal_complete()
        sends_done()

    return (send_own, pass_on_neighbours, pass_on_diagonal, finish), (y_complete, x_complete, diagonal_complete, sends_done)


RS_SEMS = 6
RS_KINDS = (((2, 2), 1, F32), ((2, 2), 1, F32), ((2, 2), 2, BF16), ((2, 2), 2, BF16), ((2, 2), 2, F32),
            ((2,), 2, BF16), ((2,), 2, BF16), ((2,), 1, F32))


def _rs_view(g):
    return g.reshape(2, 2, 2, g.shape[0] // 8, g.shape[1])


def _rs_scratch(shapes, in_vmem=False):
    kinds = RS_KINDS[1:] if in_vmem else RS_KINDS
    return [pltpu.VMEM(lead + (r // 8, w // split), dt) for lead, split, dt in kinds for r, w in shapes]


def _rs_stages(gs, outs, bufs, send_sems, recv_sems, local_sems, widths):
    n = len(gs)
    if len(bufs) < n * len(RS_KINDS):
        bufs = [None] * n + list(bufs)
    loc, ra, s_b, r_b, acc1, s_c, r_c, fin = (bufs[n * i:n * i + n] for i in range(len(RS_KINDS)))
    half_w = [w // 2 for w in widths]
    chips = [(xx, yy) for xx in range(2) for yy in range(2)]
    x, y, c = lax.axis_index("x"), lax.axis_index("y"), lax.axis_index("c")
    sib, xn, yn = (x, y, 1 - c), (1 - x, y, c), (x, 1 - y, c)

    def copy(a, j, src, dst, to):
        k = RS_SEMS * a + j
        return pltpu.make_async_remote_copy(src_ref=src, dst_ref=dst, send_sem=send_sems.at[k],
                                            recv_sem=recv_sems.at[k], device_id=to, device_id_type=MESH)

    def step_a(a):
        if callable(gs[a]):
            return [copy(a, 0, gs[a](xx, yy, 1 - c), ra[a].at[xx, yy], sib) for xx, yy in chips]
        return [copy(a, 0, gs[a].at[:, :, 1 - c], ra[a], sib),
                pltpu.make_async_copy(gs[a].at[:, :, c], loc[a], local_sems.at[a])]

    def finish_a(a):
        if callable(gs[a]):
            copy(a, 0, ra[a], ra[a], sib).wait()
            for xx, yy in chips:
                ra[a][xx, yy] = gs[a](xx, yy, c)[...] + ra[a][xx, yy]
        else:
            for cp in step_a(a):
                cp.wait()
            ra[a][...] = loc[a][...] + ra[a][...]

    def step_b(a):
        return copy(a, 1, s_b[a].at[0], r_b[a].at[0], xn), copy(a, 2, s_b[a].at[1], r_b[a].at[1], yn)

    def step_c(a):
        return copy(a, 3, s_c[a].at[0], r_c[a].at[0], yn), copy(a, 4, s_c[a].at[1], r_c[a].at[1], xn)

    def step_d(a, half):
        rows = fin[a].at[half]
        return copy(a, 5, rows, rows, sib)

    def start():
        for a in range(n):
            for cp in step_a(a):
                cp.start()

    def a_to_b():
        for a in range(n):
            finish_a(a)
            s_b[a][0] = ra[a][1 - x, :, :, :half_w[a]].astype(BF16)
            s_b[a][1] = ra[a][:, 1 - y, :, half_w[a]:].astype(BF16)
            for cp in step_b(a):
                cp.start()

    def b_to_c():
        for a in range(n):
            for cp in step_b(a):
                cp.wait()
            acc1[a][0] = ra[a][x, :, :, :half_w[a]] + r_b[a][0].astype(F32)
            acc1[a][1] = ra[a][:, y, :, half_w[a]:] + r_b[a][1].astype(F32)
            s_c[a][0] = acc1[a][0, 1 - y].astype(BF16)
            s_c[a][1] = acc1[a][1, 1 - x].astype(BF16)
            for cp in step_c(a):
                cp.start()

    def c_to_d():
        for a in range(n):
            for cp in step_c(a):
                cp.wait()
            fin[a][c, :, :half_w[a]] = acc1[a][0, y] + r_c[a][0].astype(F32)
            fin[a][c, :, half_w[a]:] = acc1[a][1, x] + r_c[a][1].astype(F32)
            step_d(a, c).start()

    def finish():
        to_hbm = [pltpu.make_async_copy(fin[a], outs[a], local_sems.at[a]) for a in range(n)]
        for a in range(n):
            step_d(a, 1 - c).wait_recv()
            step_d(a, c).wait_send()
            to_hbm[a].start()
        for cp in to_hbm:
            cp.wait()

    return start, a_to_b, b_to_c, c_to_d, finish


def _reduce_grads(gwt, g_ws, tiny):
    cw = gwt.shape[1] // RS_CHUNKS
    chunk_shape = (gwt.shape[0], cw)

    def body(g0, ws_in, tiny_in, *rest):
        outs, o_ws, o_tiny = rest[:RS_CHUNKS], rest[RS_CHUNKS], rest[RS_CHUNKS + 1]
        rest = rest[RS_CHUNKS + 2:]
        nb = len(RS_KINDS) * RS_CHUNKS
        sm, sa, sb, sc, acc_s, send_sems, recv_sems, local_sems = rest[nb:]
        blocks = [g0.at[:, :, :, :, pl.ds(j * cw, cw)] for j in range(RS_CHUNKS)]
        start, a_to_b, b_to_c, c_to_d, finish = _rs_stages(blocks, outs, rest[:nb], send_sems, recv_sems, local_sems,
                                                           [cw] * RS_CHUNKS)
        n_ws = ws_in.shape[0]
        sm[0:n_ws, :] = ws_in[...]
        sm[n_ws:, :] = tiny_in[...]
        x, y, c = lax.axis_index("x"), lax.axis_index("y"), lax.axis_index("c")

        def small(j, src, dst, to):
            k = RS_SEMS * RS_CHUNKS + j
            return pltpu.make_async_remote_copy(src_ref=src, dst_ref=dst, send_sem=send_sems.at[k],
                                                recv_sem=recv_sems.at[k], device_id=to, device_id_type=MESH)

        along_c, along_x, along_y = (small(0, sm, sa, (x, y, 1 - c)), small(1, acc_s, sb, (1 - x, y, c)),
                                     small(2, sb, sc, (x, 1 - y, c)))
        start()
        along_c.start()
        a_to_b()
        along_c.wait()
        acc_s[...] = sm[...] + sa[...]
        along_x.start()
        b_to_c()
        along_x.wait()
        sb[...] = acc_s[...] + sb[...]
        along_y.start()
        c_to_d()
        along_y.wait()
        o_ws[...] = sb[0:n_ws, :] + sc[0:n_ws, :]
        o_tiny[...] = sb[n_ws:, :] + sc[n_ws:, :]
        finish()

    vm = pl.BlockSpec(memory_space=pltpu.VMEM)
    hbm = pl.BlockSpec(memory_space=pl.ANY)
    small_shape = (g_ws.shape[0] + tiny.shape[0], LANES)
    scratch = _rs_scratch([chunk_shape] * RS_CHUNKS) + [pltpu.VMEM(small_shape, F32) for _ in range(5)]
    nsem = RS_SEMS * RS_CHUNKS + 3
    scratch += [pltpu.SemaphoreType.DMA((nsem,)), pltpu.SemaphoreType.DMA((nsem,)), pltpu.SemaphoreType.DMA((RS_CHUNKS,))]
    return _call(
        body, name="reduce_grads",
        out_shape=[jax.ShapeDtypeStruct((2, gwt.shape[0] // 8, cw), F32)] * RS_CHUNKS
        + [jax.ShapeDtypeStruct(g_ws.shape, F32), jax.ShapeDtypeStruct(tiny.shape, F32)],
        in_specs=[hbm, vm, vm],
        out_specs=[hbm] * RS_CHUNKS + [vm, vm],
        scratch_shapes=scratch,
        compiler_params=_params(),
    )(_rs_view(gwt), g_ws, tiny)


def _adam_update(w, g, m, v):
    nm = ADAM_B1 * m + (1.0 - ADAM_B1) * g
    nv = ADAM_B2 * v + (1.0 - ADAM_B2) * (g * g)
    m_hat = nm / (1.0 - ADAM_B1 ** ADAM_STEP)
    v_hat = nv / (1.0 - ADAM_B2 ** ADAM_STEP)
    return -ADAM_LR * (m_hat / (jnp.sqrt(v_hat) + ADAM_EPS) + ADAM_WD * w), nm, nv


def _adamw(w, g, m, v):
    rows, cols = w.shape
    tm = max(t for t in range(8, 257, 8) if rows % t == 0)
    parts = tuple(g) if isinstance(g, (tuple, list)) else (g,)
    n = len(parts)

    def body(w_ref, m_ref, v_ref, *refs):
        gv = jnp.concatenate([r[...] for r in refs[:n]], axis=1)
        d_ref, nm_ref, nv_ref = refs[n:n + 3]
        d_ref[...], nm_ref[...], nv_ref[...] = _adam_update(w_ref[...], gv, m_ref[...], v_ref[...])
        if n > 1:
            refs[n + 3][...] = gv

    blk = pl.BlockSpec((tm, cols), lambda i: (i, 0))
    nout = 3 if n == 1 else 4
    res = _call(
        body, name="adamw", grid=(rows // tm,),
        in_specs=[blk] * 3 + [pl.BlockSpec((tm, p.shape[1]), lambda i: (i, 0)) for p in parts], out_specs=[blk] * nout,
        out_shape=[jax.ShapeDtypeStruct((rows, cols), F32)] * nout,
        compiler_params=_params(),
    )(w, m, v, *parts)
    return (parts[0] if n == 1 else res[3], *res[:3])


def _adamw_tiny(tiny, weights, ms, vs):
    shapes = [w.shape for w in weights]
    n = len(weights)

    def grad_of(t_ref, k, shape):
        base = 8 * k
        if shape[1] > LANES:
            return [t_ref[base + j:base + j + 1, :] for j in range(shape[1] // LANES)]
        return [t_ref[base:base + shape[0], 0:shape[1]]]

    def body(t_ref, *refs):
        w_refs, m_refs, v_refs = refs[:n], refs[n:2 * n], refs[2 * n:3 * n]
        loss_ref, outs = refs[3 * n], refs[3 * n + 1:]
        loss_ref[...] = (0.5 / D_MODEL) * jnp.sum(t_ref[8 * n:8 * n + 8, :], keepdims=True)
        for k, shape in enumerate(shapes):
            g_ref, d_ref, nm_ref, nv_ref = outs[4 * k:4 * k + 4]
            for j, g in enumerate(grad_of(t_ref, k, shape)):
                cols = slice(j * LANES, (j + 1) * LANES) if shape[1] > LANES else slice(None)
                g_ref[:, cols] = g
                d_ref[:, cols], nm_ref[:, cols], nv_ref[:, cols] = _adam_update(
                    w_refs[k][:, cols], g, m_refs[k][:, cols], v_refs[k][:, cols])

    out_shape = [jax.ShapeDtypeStruct((1, 1), F32)]
    for shape in shapes:
        out_shape += [jax.ShapeDtypeStruct(shape, F32)] * 4
    return _call(body, name="adamw_tiny", out_shape=out_shape, compiler_params=_params())(tiny, *weights, *ms, *vs)


def _local_grads(x, mem, tgt, norm_gain, wt_sh, gmlp_v_gain, gmlp_w_s, gmlp_b, attn_q_gain, attn_k_gain,
                 mem_norm_gain, wkv_sh, mem_q_gain, mem_k_gain, wo_sh):
    vg = gmlp_v_gain.reshape(1, GMLP_W)
    bias_full = jnp.repeat(gmlp_b.T, HEAD_DIM, axis=1)
    gq2, gk2 = jnp.tile(attn_q_gain, (1, 2)), jnp.tile(attn_k_gain, (1, 2))
    qg4, kg4 = jnp.tile(mem_q_gain, (1, 4)), jnp.tile(mem_k_gain, (1, 4))

    proj, wt, wkv, wo = _gather_proj(x, norm_gain, wt_sh, wkv_sh, wo_sh)
    yg = _gmlp_fwd(proj, vg, gmlp_w_s, bias_full)
    o, lse, ya = _attn_fwd(proj, gq2, gk2)
    kv, hm = _mem_kv(mem, mem_norm_gain, wkv)
    om, ym = _mem_fwd(proj, kv, qg4, kg4)
    dy, dyc, g_wo, err2 = _out_loss(yg, ya, ym, x, tgt, wo)
    dmq, dmg, g_mq, g_mk, g_wkv, g_mng = _mem_bwd(proj, om, dyc, kv, hm, mem, mem_norm_gain, wkv, qg4, kg4)
    daq, dak, dav, dag, g_aq, g_ak, g_wkv_sh, g_wo_sh = _attn_bwd(proj, o, lse, dyc, gq2, gk2, g_wkv, g_wo)
    dg, g_ws, g_b, g_vg = _gmlp_bwd(proj, dyc, vg, gmlp_w_s, bias_full)
    gx, g_wt, g_ng = _proj_bwd(x, dy, norm_gain, wt, dg, daq, dak, dav, dag, dmq, dmg)

    tiny = jnp.concatenate([g_ng, g_vg, g_b, g_aq, g_ak, g_mng, g_mq, g_mk, err2], axis=0)
    return gx, g_wt, g_wkv_sh, g_wo_sh, g_ws.reshape(4 * CHUNK, CHUNK), tiny


def kernel(x, mem, norm_gain, w_in, gmlp_v_gain, gmlp_w_s, gmlp_b, attn_q_gain, attn_k_gain, mem_norm_gain, w_mem_kv, mem_q_gain, mem_k_gain, w_out, loss_target, m_norm_gain, m_w_in, m_gmlp_v_gain, m_gmlp_w_s, m_gmlp_b, m_attn_q_gain, m_attn_k_gain, m_mem_norm_gain, m_w_mem_kv, m_mem_q_gain, m_mem_k_gain, m_w_out, v_norm_gain, v_w_in, v_gmlp_v_gain, v_gmlp_w_s, v_gmlp_b, v_attn_q_gain, v_attn_k_gain, v_mem_norm_gain, v_w_mem_kv, v_mem_q_gain, v_mem_k_gain, v_w_out):
    gx, g_wt, g_wkv_sh, g_wo_sh, g_ws, tiny = _local_grads(
        x[0], mem[0], loss_target[0], norm_gain, w_in[0].T, gmlp_v_gain[0], gmlp_w_s[0], gmlp_b[0],
        attn_q_gain, attn_k_gain, mem_norm_gain, w_mem_kv[0], mem_q_gain, mem_k_gain, w_out[0])
    *g_wt_sh, g_ws, tiny = _reduce_grads(g_wt, g_ws, tiny)
    chip_block = lambda g: g.reshape(2 * g.shape[1], g.shape[2])
    g_wt_sh = tuple(chip_block(g) for g in g_wt_sh)
    g_wkv_sh, g_wo_sh = chip_block(g_wkv_sh), chip_block(g_wo_sh)

    ws = (norm_gain, w_in, gmlp_v_gain, gmlp_w_s, gmlp_b, attn_q_gain, attn_k_gain, mem_norm_gain, w_mem_kv,
          mem_q_gain, mem_k_gain, w_out)
    ms = (m_norm_gain, m_w_in, m_gmlp_v_gain, m_gmlp_w_s, m_gmlp_b, m_attn_q_gain, m_attn_k_gain, m_mem_norm_gain,
          m_w_mem_kv, m_mem_q_gain, m_mem_k_gain, m_w_out)
    vs = (v_norm_gain, v_w_in, v_gmlp_v_gain, v_gmlp_w_s, v_gmlp_b, v_attn_q_gain, v_attn_k_gain, v_mem_norm_gain,
          v_w_mem_kv, v_mem_q_gain, v_mem_k_gain, v_w_out)
    form = {1: lambda a: a[0].T, 3: lambda a: a.reshape(4 * CHUNK, CHUNK), 2: lambda a: a[0], 4: lambda a: a[0],
            8: lambda a: a[0], 11: lambda a: a[0]}
    back = {1: lambda a: a.T[None], 3: lambda a: a.reshape(1, 4, CHUNK, CHUNK), 2: lambda a: a[None],
            4: lambda a: a[None], 8: lambda a: a[None], 11: lambda a: a[None]}
    fwd = lambda t, i: form.get(i, lambda a: a)(t[i])
    out = {}
    for i, g in ((1, g_wt_sh), (3, g_ws), (8, g_wkv_sh), (11, g_wo_sh)):
        out[i] = _adamw(fwd(ws, i), g, fwd(ms, i), fwd(vs, i))
    res = _adamw_tiny(tiny, [fwd(ws, i) for i in TINY_ORDER], [fwd(ms, i) for i in TINY_ORDER],
                      [fwd(vs, i) for i in TINY_ORDER])
    for k, i in enumerate(TINY_ORDER):
        out[i] = res[1 + 4 * k:5 + 4 * k]
    leaves = [[back.get(i, lambda a: a)(out[i][j]) for i in range(12)] for j in range(4)]
    return (res[0].reshape(()), gx[None], *leaves[0], *leaves[1], *leaves[2], *leaves[3])
```

```python
import math

import jax
import jax.numpy as jnp
from jax import lax
from jax.experimental import pallas as pl
from jax.experimental.pallas import tpu as pltpu

F32 = jnp.float32
BF16 = jnp.bfloat16

SEQ = 4096
D_MODEL = 1024
HEAD_DIM = 64
LANES = 128
CHUNK = 128
GMLP_W, ATTN_W, MEM_W = 256, 512, 256
IN_W = 3 * GMLP_W + 4 * ATTN_W + 2 * MEM_W
MEM_LEN = 256
DILATIONS = (16, 4, 1)
EPS = 1e-6
QK_SCALE = 1.0 / math.sqrt(HEAD_DIM)
C_GU, C_GV, C_GG, C_AQ, C_AK, C_AV, C_AG, C_MQ, C_MG = 0, 256, 512, 768, 1280, 1792, 2304, 2816, 3072

ADAM_LR, ADAM_B1, ADAM_B2, ADAM_EPS, ADAM_WD, ADAM_STEP = 0.001, 0.9, 0.999, 1e-08, 0.01, 10

VMEM_LIMIT = 48 * 1024 * 1024
RS_CHUNKS = 4
ATTN_UNROLL = 4
MESH = pl.DeviceIdType.MESH

TINY_ORDER = (0, 2, 4, 5, 6, 7, 9, 10)


def _call(body, **kw):
    return pl.pallas_call(body, **kw)


def _params(**kw):
    return pltpu.CompilerParams(vmem_limit_bytes=VMEM_LIMIT, **kw)


def _dot(a, b):
    return jnp.dot(a, b, preferred_element_type=F32)


def _dot_nt(a, b):
    return lax.dot_general(a, b, (((1,), (1,)), ((), ())), preferred_element_type=F32)


def _dot_tn(a, b):
    return lax.dot_general(a, b, (((0,), (0,)), ((), ())), preferred_element_type=F32)


def _head_blockdiag():
    r = lax.shift_right_logical(lax.broadcasted_iota(jnp.int32, (LANES, LANES), 0), 6)
    c = lax.shift_right_logical(lax.broadcasted_iota(jnp.int32, (LANES, LANES), 1), 6)
    return jnp.where(r == c, 1.0, 0.0).astype(BF16)


def _headsum(v, bd):
    hi = v.astype(BF16)
    lo = (v - hi.astype(F32)).astype(BF16)
    return _dot(hi, bd) + _dot(lo, bd)


def _lo_mask(rows):
    return lax.broadcasted_iota(jnp.int32, (rows, LANES), 1) < HEAD_DIM


def _sigmoid(x):
    return 1.0 / (1.0 + jnp.exp(-x))


def _fold_heads(v):
    return v + pltpu.roll(v, HEAD_DIM, 1)


def _put_rows(ref, vec, accumulate=False):
    for j in range(vec.shape[1] // LANES):
        piece = vec[:, j * LANES:(j + 1) * LANES]
        ref[j:j + 1, :] = ref[j:j + 1, :] + piece if accumulate else piece


def _gather_proj(x, gain, wt_sh, *ride_along):
    tm = 512
    nrow = SEQ // tm
    nride = len(ride_along)
    widths = (768, 384, 512, 768, 896)
    nunits = len(widths)
    late_send_at = 4
    pair = 2 * wt_sh.shape[0]
    assert pair % LANES == 0 and sum(widths[:3]) == pair and wt_sh.shape[0] // 2 >= widths[1]

    def body(x_ref, g_ref, wt_sh_ref, *rest):
        shards, rest = rest[:nride], rest[nride:]
        proj_hbm, wt_hbm, gathered = rest[0], rest[1], rest[2:2 + nride]
        h_scr, land, res = rest[2 + nride:5 + nride]
        lands, (send0, recv0, send1, recv1, out_sems, copy_sems) = rest[5 + nride:5 + 2 * nride], rest[5 + 2 * nride:]
        u, i = pl.program_id(0), pl.program_id(1)
        cx_, cy_, south = lax.axis_index("x"), lax.axis_index("y"), lax.axis_index("c") == 0
        ((_, _, pass_on_diagonal, _), (y_complete, x_complete, diagonal_complete, sends_done),
         (cast_own, send_x, send_y, got_x, got_y)) = _gather_stages((wt_sh_ref,), (land,), send0, recv0)
        ride = _gather_stages(shards, lands, send1, recv1)[0]
        first = lambda k: (u == k) & (i == 0)
        last = (u == nunits - 1) & (i == nrow - 1)
        copies = [pltpu.make_async_copy(land, wt_hbm, copy_sems.at[0])] + [
            pltpu.make_async_copy(src, dst, copy_sems.at[1 + k]) for k, (src, dst) in enumerate(zip(lands, gathered))]

        def both(*fns):
            def run():
                for fn in fns:
                    fn()
            return run

        @pl.when(first(0))
        def _():
            cast_own()
            pl.when(south)(send_y)
            pl.when(~south)(send_x)

        @pl.when((u == 0) & (i == late_send_at))
        def _():
            pl.when(south)(send_x)
            pl.when(~south)(send_y)

        @pl.when(u == 0)
        def _():
            xv = x_ref[...]
            ms = jnp.mean(xv * xv, axis=-1, keepdims=True)
            h_scr[pl.ds(pl.multiple_of(i * tm, tm), tm), :] = (xv * lax.rsqrt(ms + EPS) * g_ref[...]).astype(BF16)

        @pl.when(first(1))
        def _():
            pl.when(south)(got_y)
            pl.when(~south)(both(got_x, y_complete))

        @pl.when(first(2))
        def _():
            pl.when(south)(both(got_x, y_complete))
            pl.when(~south)(got_y)
            ride[0]()

        @pl.when(first(3))
        def _():
            x_complete()
            pass_on_diagonal()
            ride[1]()

        @pl.when(first(4))
        def _():
            diagonal_complete()
            copies[0].start()
            ride[2]()

        mine, other = pair * cx_, pair * (1 - cx_)
        col0 = (mine + 896 * cy_, mine + 768 * (1 - cy_), mine + 768 * (1 - cy_) + widths[1],
                other + 896 * cy_, other + 768 * (1 - cy_))
        slot = i % 2
        rows = pl.ds(pl.multiple_of(i * tm, tm), tm)

        def writeback(k, rows_):
            c0 = pl.multiple_of(col0[k], LANES)
            return pltpu.make_async_copy(res.at[slot, :, pl.ds(0, widths[k])], proj_hbm.at[rows_, pl.ds(c0, widths[k])],
                                         out_sems.at[slot])

        for k in range(nunits):
            @pl.when(u == k)
            def _(k=k):
                pl.when(i >= 2)(writeback(k, rows).wait)
                if k > 0:
                    pl.when(i < 2)(writeback(k - 1, rows).wait)
                w_rows = land[pl.ds(pl.multiple_of(col0[k], LANES), widths[k]), :]
                res[slot, :, 0:widths[k]] = _dot_nt(h_scr[rows, :], w_rows)
                writeback(k, rows).start()

        @pl.when(last)
        def _():
            sends_done()
            ride[3]()
            for cp in copies[1:]:
                cp.start()
            for cp in copies:
                cp.wait()
            for s in range(2):
                pltpu.make_async_copy(res.at[s, :, pl.ds(0, widths[-1])], proj_hbm.at[rows, pl.ds(0, widths[-1])], out_sems.at[s]).wait()

    full = [jax.ShapeDtypeStruct((4 * a.shape[0], a.shape[1]), BF16) for a in (wt_sh,) + ride_along]
    hbm = pl.BlockSpec(memory_space=pl.ANY)
    const = lambda a: pl.BlockSpec(a.shape, lambda u, i: (0, 0))
    return _call(
        body, name="gather_proj", grid=(nunits, nrow),
        in_specs=[pl.BlockSpec((tm, D_MODEL), lambda u, i: (jnp.where(u == 0, i, nrow - 1), 0)),
                  pl.BlockSpec((1, D_MODEL), lambda u, i: (0, 0)), const(wt_sh)] + [const(a) for a in ride_along],
        out_specs=[hbm] * (2 + nride),
        out_shape=[jax.ShapeDtypeStruct((SEQ, IN_W), F32)] + full,
        scratch_shapes=[pltpu.VMEM((SEQ, D_MODEL), BF16), pltpu.VMEM(full[0].shape, BF16), pltpu.VMEM((2, tm, max(widths)), F32)]
        + [pltpu.VMEM(s.shape, BF16) for s in full[1:]]
        + [pltpu.SemaphoreType.DMA((AG_SEMS,)), pltpu.SemaphoreType.DMA((AG_SEMS,)),
           pltpu.SemaphoreType.DMA((AG_SEMS * nride,)), pltpu.SemaphoreType.DMA((AG_SEMS * nride,)),
           pltpu.SemaphoreType.DMA((2,)), pltpu.SemaphoreType.DMA((1 + nride,))],
        compiler_params=_params(),
    )(x, gain, wt_sh, *ride_along)


def _gmlp_weights(w_ref):
    ti = lax.broadcasted_iota(jnp.int32, (CHUNK, CHUNK), 0)
    si = lax.broadcasted_iota(jnp.int32, (CHUNK, CHUNK), 1)
    tril = si <= ti
    return tril, [jnp.where(tril, w_ref[h], 0.0).astype(BF16) for h in range(4)]


def _gmlp_fwd(proj, vgain, w_s, bias_full):
    tm = 512

    def body(p_ref, vg_ref, w_ref, b_ref, y_ref):
        bd = _head_blockdiag()
        lo = _lo_mask(CHUNK)
        _, wm = _gmlp_weights(w_ref)
        units = [(pl.ds(c * CHUNK, CHUNK), p) for c in range(tm // CHUNK) for p in range(2)]
        col = lambda c0, p: slice(c0 + p * LANES, c0 + (p + 1) * LANES)
        vs = [p_ref[rows, col(C_GV, p)] for rows, p in units]
        rs = [lax.rsqrt(_headsum(v * v, bd) * (1.0 / HEAD_DIM) + EPS) for v in vs]
        vns = [(v * r * vg_ref[:, col(0, p)]).astype(BF16) for v, r, (_, p) in zip(vs, rs, units)]
        sps = [jnp.where(lo, _dot(wm[2 * p], vn), _dot(wm[2 * p + 1], vn)) + b_ref[:, col(0, p)] for vn, (_, p) in zip(vns, units)]
        for sp, (rows, p) in zip(sps, units):
            gt = p_ref[rows, col(C_GG, p)]
            y_ref[rows, col(0, p)] = (p_ref[rows, col(C_GU, p)] * sp * (gt * _sigmoid(gt))).astype(BF16)

    return _call(
        body, name="gmlp_fwd", grid=(SEQ // tm,),
        in_specs=[pl.BlockSpec((tm, 3 * GMLP_W), lambda i: (i, 0)),
                  pl.BlockSpec((1, GMLP_W), lambda i: (0, 0)),
                  pl.BlockSpec((4, CHUNK, CHUNK), lambda i: (0, 0, 0)),
                  pl.BlockSpec((CHUNK, GMLP_W), lambda i: (0, 0))],
        out_specs=pl.BlockSpec((tm, GMLP_W), lambda i: (i, 0)),
        out_shape=jax.ShapeDtypeStruct((SEQ, GMLP_W), BF16),
        compiler_params=_params(),
    )(proj, vgain, w_s, bias_full)


def _gmlp_bwd(proj, dyc, vgain, w_s, bias_full):
    tm = 512
    nsteps = SEQ // tm

    def body(p_ref, dy_ref, vg_ref, w_ref, b_ref, dg_ref, gw_ref, gb_ref, gv_ref):
        i = pl.program_id(0)
        bd = _head_blockdiag()
        lo = _lo_mask(CHUNK)
        tril, wm = _gmlp_weights(w_ref)
        ri = lax.broadcasted_iota(jnp.int32, (16, LANES), 0)
        li = lax.broadcasted_iota(jnp.int32, (16, LANES), 1)
        head_rows = [jnp.where(((ri == 2 * p) & (li < HEAD_DIM)) | ((ri == 2 * p + 1) & (li >= HEAD_DIM)), 1.0, 0.0).astype(BF16)
                     for p in range(2)]

        @pl.when(i == 0)
        def _():
            gw_ref[...] = jnp.zeros_like(gw_ref)
            gb_ref[...] = jnp.zeros_like(gb_ref)
            gv_ref[...] = jnp.zeros_like(gv_ref)

        units = [(pl.ds(c * CHUNK, CHUNK), p) for c in range(tm // CHUNK) for p in range(2)]
        col = lambda c0, p: slice(c0 + p * LANES, c0 + (p + 1) * LANES)
        vs = [p_ref[rows, col(C_GV, p)] for rows, p in units]
        rs = [lax.rsqrt(_headsum(v * v, bd) * (1.0 / HEAD_DIM) + EPS) for v in vs]
        zs = [v * r for v, r in zip(vs, rs)]
        vns = [(z * vg_ref[:, col(0, p)]).astype(BF16) for z, (_, p) in zip(zs, units)]
        sps = [jnp.where(lo, _dot(wm[2 * p], vn), _dot(wm[2 * p + 1], vn)) + b_ref[:, col(0, p)] for vn, (_, p) in zip(vns, units)]
        dsps = []
        for sp, (rows, p) in zip(sps, units):
            u = p_ref[rows, col(C_GU, p)]
            gt = p_ref[rows, col(C_GG, p)]
            dy = dy_ref[rows, col(0, p)]
            sg = _sigmoid(gt)
            sl = gt * sg
            dg_ref[rows, col(C_GU, p)] = (dy * sp * sl).astype(BF16)
            dg_ref[rows, col(C_GG, p)] = (dy * u * sp * (sg * (1.0 + gt * (1.0 - sg)))).astype(BF16)
            dsps.append(dy * u * sl)
        dspbs = [dsp.astype(BF16) for dsp in dsps]
        dvns = [jnp.where(lo, _dot_tn(wm[2 * p], dspb), _dot_tn(wm[2 * p + 1], dspb)) for dspb, (_, p) in zip(dspbs, units)]
        gws = [(_dot_nt(jnp.where(lo, dsp, 0.0).astype(BF16), vn), _dot_nt(jnp.where(lo, 0.0, dsp).astype(BF16), vn))
               for dsp, vn in zip(dsps, vns)]
        gbs = [(_dot_nt(head_rows[p], dspb) + _dot_nt(head_rows[p], (dsp - dspb.astype(F32)).astype(BF16)))[0:8]
               for dsp, dspb, (_, p) in zip(dsps, dspbs, units)]
        for p in range(2):
            mine = [n for n, (_, q) in enumerate(units) if q == p]
            gw_ref[2 * p] += sum(gws[n][0] for n in mine)
            gw_ref[2 * p + 1] += sum(gws[n][1] for n in mine)
            gvp = sum(jnp.sum(dvns[n] * zs[n], axis=0, keepdims=True) for n in mine)
            gv_ref[2 * p:2 * p + 1, :] += gvp
            gv_ref[2 * p + 1:2 * p + 2, :] += pltpu.roll(gvp, HEAD_DIM, 1)
        gb_ref[...] += sum(gbs)
        for dvn, z, r, (rows, p) in zip(dvns, zs, rs, units):
            dz = dvn * vg_ref[:, col(0, p)]
            dg_ref[rows, col(C_GV, p)] = (r * (dz - z * (_headsum(dz * z, bd) * (1.0 / HEAD_DIM)))).astype(BF16)

        @pl.when(i == nsteps - 1)
        def _():
            for h in range(4):
                gw_ref[h] = jnp.where(tril, gw_ref[h], 0.0)

    return _call(
        body, name="gmlp_bwd", grid=(nsteps,),
        in_specs=[pl.BlockSpec((tm, 3 * GMLP_W), lambda i: (i, 0)),
                  pl.BlockSpec((tm, GMLP_W), lambda i: (i, 0)),
                  pl.BlockSpec((1, GMLP_W), lambda i: (0, 0)),
                  pl.BlockSpec((4, CHUNK, CHUNK), lambda i: (0, 0, 0)),
                  pl.BlockSpec((CHUNK, GMLP_W), lambda i: (0, 0))],
        out_specs=[pl.BlockSpec((tm, 3 * GMLP_W), lambda i: (i, 0)),
                   pl.BlockSpec((4, CHUNK, CHUNK), lambda i: (0, 0, 0)),
                   pl.BlockSpec((8, LANES), lambda i: (0, 0)),
                   pl.BlockSpec((8, LANES), lambda i: (0, 0))],
        out_shape=[jax.ShapeDtypeStruct((SEQ, 3 * GMLP_W), BF16),
                   jax.ShapeDtypeStruct((4, CHUNK, CHUNK), F32),
                   jax.ShapeDtypeStruct((8, LANES), F32),
                   jax.ShapeDtypeStruct((8, LANES), F32)],
        compiler_params=_params(),
    )(proj, dyc, vgain, w_s, bias_full)


def _band_masks():
    qi = lax.broadcasted_iota(jnp.int32, (CHUNK, 2 * CHUNK), 0)
    kj = lax.broadcasted_iota(jnp.int32, (CHUNK, 2 * CHUNK), 1)
    valid2 = ((kj < CHUNK) & (kj >= qi)) | ((kj >= CHUNK) & (kj - CHUNK <= qi))
    q1 = lax.broadcasted_iota(jnp.int32, (CHUNK, CHUNK), 0)
    k1 = lax.broadcasted_iota(jnp.int32, (CHUNK, CHUNK), 1)
    return k1 <= q1, valid2


def _stack_heads(v, lo):
    return jnp.concatenate([jnp.where(lo, v, 0.0), jnp.where(lo, 0.0, v)], axis=0).astype(BF16)


def _rows_of(ref, start, d):
    if d == 1:
        return ref.at[pl.ds(start if isinstance(start, int) else pl.multiple_of(start, CHUNK), CHUNK), :]
    return ref.at[pl.ds(start, CHUNK, stride=d), :]


def _unrolled(lo, hi, unroll, run):
    groups = (hi - lo) // unroll
    if groups:
        def body(g, carry):
            run([lo + g * unroll + t for t in range(unroll)])
            return carry

        lax.fori_loop(0, groups, body, 0)
    if lo + groups * unroll < hi:
        run(range(lo + groups * unroll, hi))


def _for_blocks(d, group_fn, unroll):
    nblk = SEQ // CHUNK
    sh = d.bit_length() - 1

    def first(j):
        return (j * CHUNK if d == 1 else j, None)

    def rest(j):
        start = (j & (d - 1)) + (j >> sh) * (CHUNK * d)
        return (start, start - CHUNK * d)

    _unrolled(0, d, unroll, lambda js: group_fn(d, [first(j) for j in js]))
    _unrolled(d, nblk, unroll, lambda js: group_fn(d, [rest(j) for j in js]))


def _attn_fwd(proj, gq2, gk2):
    tn = 512

    def body(q_ref, k_ref, v_ref, g_ref, gq_ref, gk_ref, o_ref, l_ref, ya_ref, qn_ref, kn_ref):
        bd = _head_blockdiag()
        lo = _lo_mask(CHUNK)
        valid1, valid2 = _band_masks()

        def norm(t, carry):
            rows = pl.ds(pl.multiple_of(t * tn, tn), tn)
            q, k = q_ref[rows, :], k_ref[rows, :]
            ssq = [_headsum(a * a, bd) for a in (q, k)]
            qn_ref[rows, :] = q * lax.rsqrt(ssq[0] * (1.0 / HEAD_DIM) + EPS) * (gq_ref[...] * QK_SCALE)
            kn_ref[rows, :] = k * lax.rsqrt(ssq[1] * (1.0 / HEAD_DIM) + EPS) * gk_ref[...]
            return carry

        lax.fori_loop(0, SEQ // tn, norm, 0)

        def load_kv(ref, d, start, prev):
            own = _rows_of(ref, start, d)[...]
            if prev is None:
                return own.astype(BF16)
            return jnp.concatenate([_rows_of(ref, prev, d)[...], own], axis=0).astype(BF16)

        def group(d, blocks):
            valid = valid1 if blocks[0][1] is None else valid2
            valid = jnp.concatenate([valid, valid], axis=0)
            qs = [_rows_of(qn_ref, start, d)[...] for start, _ in blocks]
            ks = [load_kv(kn_ref, d, start, prev) for start, prev in blocks]
            vs = [load_kv(v_ref, d, start, prev) for start, prev in blocks]
            ss = [_dot_nt(_stack_heads(q, lo), k) for q, k in zip(qs, ks)]
            ms, ps, ls = [], [], []
            for s in ss:
                s = jnp.where(valid, s, -jnp.inf)
                m = jnp.max(s, axis=-1, keepdims=True)
                p = jnp.exp(s - m)
                ms.append(m)
                ls.append(jnp.sum(p, axis=-1, keepdims=True))
                ps.append(p.astype(BF16))
            os_ = [_dot(p, v) for p, v in zip(ps, vs)]
            for b, (start, _) in enumerate(blocks):
                heads = lambda v: jnp.where(lo, v[:CHUNK], v[CHUNK:])
                lsum = heads(ls[b])
                ob = heads(os_[b]) * (1.0 / lsum)
                lb = heads(ms[b]) + jnp.log(lsum)
                o_rows = _rows_of(o_ref, start, d)
                l_rows = _rows_of(l_ref, start, d)
                if d != DILATIONS[0]:
                    lold = l_rows[...]
                    mx = jnp.maximum(lold, lb)
                    ea = jnp.exp(lold - mx)
                    eb = jnp.exp(lb - mx)
                    inv = 1.0 / (ea + eb)
                    ob = o_rows[...] * (ea * inv) + ob * (eb * inv)
                    lb = mx + jnp.log(ea + eb)
                o_rows[...] = ob
                l_rows[...] = lb

        for d in DILATIONS:
            _for_blocks(d, group, ATTN_UNROLL)

        def fin(t, carry):
            rows = pl.ds(pl.multiple_of(t * tn, tn), tn)
            g = g_ref[rows, :]
            ya_ref[rows, :] = (o_ref[rows, :] * (g * _sigmoid(g))).astype(BF16)
            return carry

        lax.fori_loop(0, SEQ // tn, fin, 0)

    col = lambda c0: pl.BlockSpec((SEQ, LANES), lambda p: (0, c0 // LANES + p))
    vec = pl.BlockSpec((1, LANES), lambda p: (0, 0))
    out = pl.BlockSpec((SEQ, LANES), lambda p: (0, p))
    return _call(
        body, name="attn_fwd", grid=(ATTN_W // LANES,),
        in_specs=[col(C_AQ), col(C_AK), col(C_AV), col(C_AG), vec, vec],
        out_specs=[out, out, out],
        out_shape=[jax.ShapeDtypeStruct((SEQ, ATTN_W), F32), jax.ShapeDtypeStruct((SEQ, ATTN_W), F32),
                   jax.ShapeDtypeStruct((SEQ, ATTN_W), BF16)],
        scratch_shapes=[pltpu.VMEM((SEQ, LANES), F32), pltpu.VMEM((SEQ, LANES), F32)],
        compiler_params=_params(),
    )(proj, proj, proj, proj, gq2, gk2)


def _attn_bwd(proj, o, lse, dyc, gq2, gk2, *ride_along):
    tn = 512
    npairs = ATTN_W // LANES
    nride = len(ride_along)
    nbufs = nride * len(RS_KINDS)

    def body(proj_hbm, o_hbm, l_hbm, dyc_hbm, gq_ref, gk_ref, *rest):
        ride_in, rest = rest[:nride], rest[nride:]
        dq_ref, dk_ref, dv_ref, dgt_ref, gqg_ref, gkg_ref = rest[:6]
        ride_out, rest = rest[6:6 + nride], rest[6 + nride:]
        qb_, kb_, vb_, gb_, ob_, lb_, yb_, dkb_, dvb_, sems = rest[:10]
        rs_bufs, (send_sems, recv_sems, local_sems) = rest[10:10 + nbufs], rest[10 + nbufs:]
        rs_stage = _rs_stages(ride_in, ride_out, rs_bufs, send_sems, recv_sems, local_sems, [g.shape[1] for g in ride_along])
        pair = pl.program_id(0)
        for step in range(npairs):
            pl.when(pair == step)(rs_stage[step])
        bd = _head_blockdiag()
        lo = _lo_mask(CHUNK)
        lo2 = lax.broadcasted_iota(jnp.int32, (2 * CHUNK, LANES), 1) < HEAD_DIM
        valid1, valid2 = _band_masks()
        gqs = gq_ref[...] * QK_SCALE
        gk = gk_ref[...]

        def pcol(c0, of=None):
            return acol(proj_hbm, c0, of)

        def acol(hbm, c0=0, of=None):
            of = pair if of is None else of
            return hbm.at[:, pl.ds(pl.multiple_of(c0 + of * LANES, LANES), LANES)]

        def input_loads(of):
            return [pltpu.make_async_copy(src, dst, sems.at[n]) for n, (src, dst) in enumerate((
                (pcol(C_AQ, of), qb_), (pcol(C_AK, of), kb_), (pcol(C_AG, of), gb_), (acol(o_hbm, 0, of), ob_),
                (acol(dyc_hbm, GMLP_W, of), yb_), (pcol(C_AV, of), vb_), (acol(l_hbm, 0, of), lb_)))]

        early = (0, 1, 3, 4)
        loads = input_loads(pair)
        for n, cp in enumerate(loads):
            if n in early:
                pl.when(pair == 0)(cp.start)
            else:
                cp.start()

        @pl.when(pair == 0)
        def _():
            gqg_ref[...] = jnp.zeros_like(gqg_ref)
            gkg_ref[...] = jnp.zeros_like(gkg_ref)

        def pre_qk(t, carry):
            rows = pl.ds(pl.multiple_of(t * tn, tn), tn)
            q, k = qb_[rows, :], kb_[rows, :]
            ssq = [_headsum(a * a, bd) for a in (q, k)]
            qb_[rows, :] = q * lax.rsqrt(ssq[0] * (1.0 / HEAD_DIM) + EPS) * gqs
            kb_[rows, :] = k * lax.rsqrt(ssq[1] * (1.0 / HEAD_DIM) + EPS) * gk
            return carry

        def pre_gate(t, carry):
            rows = pl.ds(pl.multiple_of(t * tn, tn), tn)
            g = gb_[rows, :]
            ov = ob_[rows, :]
            dya = yb_[rows, :]
            sg = _sigmoid(g)
            dgt_ref[rows, :] = (dya * ov * (sg * (1.0 + g * (1.0 - sg)))).astype(BF16)
            do = dya * (g * sg)
            yb_[rows, :] = do
            ob_[rows, :] = jnp.where(first_half, lb_[rows, :], _headsum(do * ov, bd))
            return carry

        first_half = (lax.broadcasted_iota(jnp.int32, (tn, LANES), 1) & (HEAD_DIM - 1)) < HEAD_DIM // 2
        loads[0].wait()
        loads[1].wait()
        lax.fori_loop(0, SEQ // tn, pre_qk, 0)
        for cp in loads[2:5] + loads[6:7]:
            cp.wait()
        lax.fori_loop(0, SEQ // tn, pre_gate, 0)
        loads[5].wait()
        reloads = [pltpu.make_async_copy(pcol(C_AQ), lb_, sems.at[7]), pltpu.make_async_copy(pcol(C_AK), vb_, sems.at[8])]
        reloads[0].start()

        def load_kv(ref, d, start, prev):
            own = _rows_of(ref, start, d)[...]
            if prev is None:
                return own.astype(BF16)
            return jnp.concatenate([_rows_of(ref, prev, d)[...], own], axis=0).astype(BF16)

        def group(d, blocks):
            first = blocks[0][1] is None
            valid, lok = (valid1, lo) if first else (valid2, lo2)
            chains = [(b, h) for b in range(len(blocks)) for h in range(2)]
            mask = lambda h: lo if h == 0 else ~lo
            qs = [_rows_of(qb_, start, d)[...] for start, _ in blocks]
            dos = [_rows_of(yb_, start, d)[...] for start, _ in blocks]
            lds = [_rows_of(ob_, start, d)[...] for start, _ in blocks]
            ks = [load_kv(kb_, d, start, prev) for start, prev in blocks]
            vs = [load_kv(vb_, d, start, prev) for start, prev in blocks]
            qbs = [q.astype(BF16) for q in qs]
            dobs = [do.astype(BF16) for do in dos]
            ss = [_dot_nt(jnp.where(mask(h), qs[b], 0.0).astype(BF16), ks[b]) for b, h in chains]
            dps = [_dot_nt(jnp.where(mask(h), dos[b], 0.0).astype(BF16), vs[b]) for b, h in chains]
            pbs, dss = [], []
            for s, dp, (b, h) in zip(ss, dps, chains):
                hc, dc = h * HEAD_DIM, h * HEAD_DIM + HEAD_DIM // 2
                p = jnp.exp(jnp.where(valid, s, -jnp.inf) - lds[b][:, hc:hc + 1])
                pbs.append(p.astype(BF16))
                dss.append((p * (dp - lds[b][:, dc:dc + 1])).astype(BF16))
            dqs = [_dot(ds, ks[b]) for ds, (b, h) in zip(dss, chains)]
            dks = [_dot_tn(ds, qbs[b]) for ds, (b, h) in zip(dss, chains)]
            dvs = [_dot_tn(p, dobs[b]) for p, (b, h) in zip(pbs, chains)]
            assign = d == DILATIONS[0]
            for b, (start, prev) in enumerate(blocks):
                c0, c1 = 2 * b, 2 * b + 1
                dq_rows = _rows_of(gb_, start, d)
                dqb = jnp.where(lo, dqs[c0], dqs[c1])
                dq_rows[...] = dqb if assign else dq_rows[...] + dqb
                dkc = jnp.where(lok, dks[c0], dks[c1])
                dvc = jnp.where(lok, dvs[c0], dvs[c1])
                spans = ((start, slice(0, CHUNK), True),) if first else (
                    (prev, slice(0, CHUNK), False), (start, slice(CHUNK, 2 * CHUNK), True))
                for st, sl, own in spans:
                    dk_rows = _rows_of(dkb_, st, d)
                    dv_rows = _rows_of(dvb_, st, d)
                    if assign and own:
                        dk_rows[...] = dkc[sl]
                        dv_rows[...] = dvc[sl]
                    else:
                        dk_rows[...] = dk_rows[...] + dkc[sl]
                        dv_rows[...] = dv_rows[...] + dvc[sl]

        for d in DILATIONS:
            _for_blocks(d, group, ATTN_UNROLL)

        reloads[1].start()

        @pl.when(pair < npairs - 1)
        def _():
            nxt = input_loads(pair + 1)
            for n in early:
                nxt[n].start()

        for cp in reloads:
            cp.wait()

        def post(t, carry):
            gq_acc, gk_acc = carry
            rows = pl.ds(pl.multiple_of(t * tn, tn), tn)
            raws = [lb_[rows, :], vb_[rows, :]]
            dns = [gb_[rows, :], dkb_[rows, :]]
            rs = [lax.rsqrt(_headsum(a * a, bd) * (1.0 / HEAD_DIM) + EPS) for a in raws]
            zs = [a * r for a, r in zip(raws, rs)]
            dzs = [dn * gain for dn, gain in zip(dns, (gqs, gk))]
            means = [_headsum(dz * z, bd) * (1.0 / HEAD_DIM) for dz, z in zip(dzs, zs)]
            dq, dk = [r * (dz - z * mean) for r, dz, z, mean in zip(rs, dzs, zs, means)]
            gq, gkk = [jnp.sum(dn * z, axis=0, keepdims=True) for dn, z in zip(dns, zs)]
            dq_ref[rows, :] = dq.astype(BF16)
            dk_ref[rows, :] = dk.astype(BF16)
            dv_ref[rows, :] = dvb_[rows, :].astype(BF16)
            return gq_acc + gq * QK_SCALE, gk_acc + gkk

        zero = jnp.zeros((1, LANES), F32)
        gq_acc, gk_acc = lax.fori_loop(0, SEQ // tn, post, (zero, zero))
        gqg_ref[0:1, :] += gq_acc
        gkg_ref[0:1, :] += gk_acc

        @pl.when(pair == npairs - 1)
        def _():
            gqg_ref[0:1, :] = _fold_heads(gqg_ref[0:1, :])
            gkg_ref[0:1, :] = _fold_heads(gkg_ref[0:1, :])
            rs_stage[npairs]()

    hbm = pl.BlockSpec(memory_space=pl.ANY)
    vec = pl.BlockSpec((1, LANES), lambda p: (0, 0))
    blk8 = pl.BlockSpec((8, LANES), lambda p: (0, 0))
    out = pl.BlockSpec((SEQ, LANES), lambda p: (0, p))
    big = jax.ShapeDtypeStruct((SEQ, ATTN_W), BF16)
    nsem = RS_SEMS * nride
    return _call(
        body, name="attn_bwd", grid=(npairs,),
        in_specs=[hbm, hbm, hbm, hbm, vec, vec] + [hbm] * nride,
        out_specs=[out, out, out, out, blk8, blk8] + [hbm] * nride,
        out_shape=[big, big, big, big, jax.ShapeDtypeStruct((8, LANES), F32), jax.ShapeDtypeStruct((8, LANES), F32)]
        + [jax.ShapeDtypeStruct((2, g.shape[0] // 8, g.shape[1]), F32) for g in ride_along],
        scratch_shapes=[pltpu.VMEM((SEQ, LANES), F32) for _ in range(9)] + [pltpu.SemaphoreType.DMA((9,))]
        + _rs_scratch([g.shape for g in ride_along]) + [pltpu.SemaphoreType.DMA((nsem,)), pltpu.SemaphoreType.DMA((nsem,)),
                                     pltpu.SemaphoreType.DMA((nride,))],
        compiler_params=_params(),
    )(proj, o, lse, dyc, gq2, gk2, *[_rs_view(g) for g in ride_along])


def _mem_kv(mem, gain, wkv):
    def body(m_ref, g_ref, w_ref, kv_ref, hm_ref):
        mv = m_ref[...]
        ms = jnp.mean(mv * mv, axis=-1, keepdims=True)
        hm = (mv * lax.rsqrt(ms + EPS) * g_ref[...]).astype(BF16)
        hm_ref[...] = hm
        kv_ref[...] = _dot(hm, w_ref[...])

    return _call(
        body, name="mem_kv",
        out_shape=[jax.ShapeDtypeStruct((MEM_LEN, 2 * MEM_W), F32), jax.ShapeDtypeStruct((MEM_LEN, D_MODEL), BF16)],
        compiler_params=_params(),
    )(mem, gain, wkv)


def _mem_keys(kv_ref, kg_ref, bd, p):
    mk = kv_ref[:, p * LANES:(p + 1) * LANES]
    r = lax.rsqrt(_headsum(mk * mk, bd) * (1.0 / HEAD_DIM) + EPS)
    z = mk * r
    mkn = (z * kg_ref[:, p * LANES:(p + 1) * LANES]).astype(BF16)
    mvp = kv_ref[:, MEM_W + p * LANES:MEM_W + (p + 1) * LANES].astype(BF16)
    return mkn, mvp, r, z


def _mem_fwd(proj, kv, qg4, kg4):
    tm = 512

    def body(q_ref, g_ref, kv_ref, qg_ref, kg_ref, om_ref, ym_ref):
        bd = _head_blockdiag()
        lo = _lo_mask(tm)
        keys, qns = [], []
        for p in range(2):
            cs = slice(p * LANES, (p + 1) * LANES)
            keys.append(_mem_keys(kv_ref, kg_ref, bd, p)[:2])
            q = q_ref[:, cs]
            qns.append(q * lax.rsqrt(_headsum(q * q, bd) * (1.0 / HEAD_DIM) + EPS) * (qg_ref[:, cs] * QK_SCALE))
        chains = [(p, h) for p in range(2) for h in range(2)]
        ss = [_dot_nt(jnp.where(lo if h == 0 else ~lo, qns[p], 0.0).astype(BF16), keys[p][0]) for p, h in chains]
        es = [jnp.exp(s - jnp.max(s, axis=-1, keepdims=True)) for s in ss]
        os_ = [_dot(e.astype(BF16), keys[p][1]) for e, (p, h) in zip(es, chains)]
        res = [o * (1.0 / jnp.sum(e, axis=-1, keepdims=True)) for o, e in zip(os_, es)]
        for p in range(2):
            cs = slice(p * LANES, (p + 1) * LANES)
            ov = jnp.where(lo, res[2 * p], res[2 * p + 1])
            g = g_ref[:, cs]
            om_ref[:, cs] = ov
            ym_ref[:, cs] = (ov * (g * _sigmoid(g))).astype(BF16)

    vec = pl.BlockSpec((1, MEM_W), lambda i: (0, 0))
    return _call(
        body, name="mem_fwd", grid=(SEQ // tm,),
        in_specs=[pl.BlockSpec((tm, MEM_W), lambda i: (i, C_MQ // MEM_W)),
                  pl.BlockSpec((tm, MEM_W), lambda i: (i, C_MG // MEM_W)),
                  pl.BlockSpec((MEM_LEN, 2 * MEM_W), lambda i: (0, 0)), vec, vec],
        out_specs=[pl.BlockSpec((tm, MEM_W), lambda i: (i, 0)), pl.BlockSpec((tm, MEM_W), lambda i: (i, 0))],
        out_shape=[jax.ShapeDtypeStruct((SEQ, MEM_W), F32), jax.ShapeDtypeStruct((SEQ, MEM_W), BF16)],
        compiler_params=_params(),
    )(proj, proj, kv, qg4, kg4)


def _mem_bwd(proj, om, dyc, kv, hm, mem, mgain, wkv, qg4, kg4):
    tm = 512
    nsteps = SEQ // tm

    def body(q_ref, g_ref, om_ref, dy_ref, kv_ref, hm_ref, mem_ref, mg_ref, w_ref, qg_ref, kg_ref,
             dq_ref, dgt_ref, gqg_ref, gkg_ref, gw_ref, gmg_ref, dmk_ref, dmv_ref, gq_acc):
        i = pl.program_id(0)
        bd = _head_blockdiag()
        lo = _lo_mask(tm)
        lom = _lo_mask(MEM_LEN)

        @pl.when(i == 0)
        def _():
            dmk_ref[...] = jnp.zeros_like(dmk_ref)
            dmv_ref[...] = jnp.zeros_like(dmv_ref)
            gq_acc[...] = jnp.zeros_like(gq_acc)

        pairs = []
        for p in range(2):
            cs = slice(p * LANES, (p + 1) * LANES)
            mkn, mvp, _, _ = _mem_keys(kv_ref, kg_ref, bd, p)
            gqs = qg_ref[:, cs] * QK_SCALE
            q = q_ref[:, cs]
            r = lax.rsqrt(_headsum(q * q, bd) * (1.0 / HEAD_DIM) + EPS)
            z = q * r
            qn = z * gqs
            g = g_ref[:, cs]
            ov = om_ref[:, cs]
            dym = dy_ref[:, cs]
            sg = _sigmoid(g)
            dgt_ref[:, cs] = (dym * ov * (sg * (1.0 + g * (1.0 - sg)))).astype(BF16)
            do = dym * (g * sg)
            pairs.append(dict(cs=cs, mkn=mkn, mvp=mvp, gqs=gqs, r=r, z=z, qn=qn, qnb=qn.astype(BF16), do=do,
                              dob=do.astype(BF16), delta=_headsum(do * ov, bd)))
        chains = [(pr_, h) for pr_ in pairs for h in range(2)]
        mask = lambda h: lo if h == 0 else ~lo
        ss = [_dot_nt(jnp.where(mask(h), c["qn"], 0.0).astype(BF16), c["mkn"]) for c, h in chains]
        dps = [_dot_nt(jnp.where(mask(h), c["do"], 0.0).astype(BF16), c["mvp"]) for c, h in chains]
        prs, dss = [], []
        for s, dp, (c, h) in zip(ss, dps, chains):
            e = jnp.exp(s - jnp.max(s, axis=-1, keepdims=True))
            pr = e * (1.0 / jnp.sum(e, axis=-1, keepdims=True))
            prs.append(pr.astype(BF16))
            dss.append((pr * (dp - c["delta"][:, h * HEAD_DIM:h * HEAD_DIM + 1])).astype(BF16))
        dqs = [_dot(ds, c["mkn"]) for ds, (c, h) in zip(dss, chains)]
        dks = [_dot_tn(ds, c["qnb"]) for ds, (c, h) in zip(dss, chains)]
        dvs = [_dot_tn(pr, c["dob"]) for pr, (c, h) in zip(prs, chains)]
        for p, c in enumerate(pairs):
            cs, z, r = c["cs"], c["z"], c["r"]
            dqn = jnp.where(lo, dqs[2 * p], dqs[2 * p + 1])
            dmk_ref[:, cs] += jnp.where(lom, dks[2 * p], dks[2 * p + 1])
            dmv_ref[:, cs] += jnp.where(lom, dvs[2 * p], dvs[2 * p + 1])
            dz = dqn * c["gqs"]
            dq_ref[:, cs] = (r * (dz - z * (_headsum(dz * z, bd) * (1.0 / HEAD_DIM)))).astype(BF16)
            gq_acc[:, cs] += jnp.sum(dqn * z, axis=0, keepdims=True) * QK_SCALE

        @pl.when(i == nsteps - 1)
        def _():
            gqg_ref[...] = jnp.zeros_like(gqg_ref)
            gkg_ref[...] = jnp.zeros_like(gkg_ref)
            gqg_ref[0:1, :] = _fold_heads(gq_acc[:, 0:LANES] + gq_acc[:, LANES:2 * LANES])
            dkv = []
            gk = jnp.zeros((1, LANES), F32)
            for p in range(2):
                cs = slice(p * LANES, (p + 1) * LANES)
                _, _, r, z = _mem_keys(kv_ref, kg_ref, bd, p)
                dn = dmk_ref[:, cs]
                dz = dn * kg_ref[:, cs]
                gk = gk + jnp.sum(dn * z, axis=0, keepdims=True)
                dkv.append(r * (dz - z * (_headsum(dz * z, bd) * (1.0 / HEAD_DIM))))
            gkg_ref[0:1, :] = _fold_heads(gk)
            dkvb = jnp.concatenate(dkv + [dmv_ref[...]], axis=1).astype(BF16)
            gw_ref[...] = _dot_tn(hm_ref[...], dkvb)
            dhm = _dot_nt(dkvb, w_ref[...])
            mv = mem_ref[...]
            zm = mv * lax.rsqrt(jnp.mean(mv * mv, axis=-1, keepdims=True) + EPS)
            _put_rows(gmg_ref, jnp.sum(dhm * zm, axis=0, keepdims=True))

    const = lambda shape: pl.BlockSpec(shape, lambda i: (0,) * len(shape))
    row = lambda j: pl.BlockSpec((tm, MEM_W), lambda i: (i, j))
    blk8 = jax.ShapeDtypeStruct((8, LANES), F32)
    return _call(
        body, name="mem_bwd", grid=(nsteps,),
        in_specs=[row(C_MQ // MEM_W), row(C_MG // MEM_W), row(0), row((GMLP_W + ATTN_W) // MEM_W),
                  const((MEM_LEN, 2 * MEM_W)), const((MEM_LEN, D_MODEL)), const((MEM_LEN, D_MODEL)),
                  const((1, D_MODEL)), const((D_MODEL, 2 * MEM_W)), const((1, MEM_W)), const((1, MEM_W))],
        out_specs=[row(0), row(0), const((8, LANES)), const((8, LANES)),
                   const((D_MODEL, 2 * MEM_W)), const((8, LANES))],
        out_shape=[jax.ShapeDtypeStruct((SEQ, MEM_W), BF16), jax.ShapeDtypeStruct((SEQ, MEM_W), BF16),
                   blk8, blk8, jax.ShapeDtypeStruct((D_MODEL, 2 * MEM_W), F32), blk8],
        scratch_shapes=[pltpu.VMEM((MEM_LEN, MEM_W), F32), pltpu.VMEM((MEM_LEN, MEM_W), F32),
                        pltpu.VMEM((1, MEM_W), F32)],
        compiler_params=_params(),
    )(proj, proj, om, dyc, kv, hm, mem, mgain, wkv, qg4, kg4)


def _out_loss(yg, ya, ym, x, tgt, wo):
    tm = 512
    nsteps = SEQ // tm
    parts = ((0, GMLP_W), (GMLP_W, ATTN_W), (GMLP_W + ATTN_W, MEM_W))

    def body(yg_ref, ya_ref, ym_ref, x_ref, t_ref, w_ref, dy_ref, dyc_ref, gw_ref, ls_ref):
        i = pl.program_id(0)

        @pl.when(i == 0)
        def _():
            gw_ref[...] = jnp.zeros_like(gw_ref)
            ls_ref[...] = jnp.zeros_like(ls_ref)

        ys = (yg_ref[...], ya_ref[...], ym_ref[...])
        y = sum(_dot(yv, w_ref[r0:r0 + n, :]) for yv, (r0, n) in zip(ys, parts))
        err = x_ref[...] + y - t_ref[...]
        _put_rows(ls_ref, jnp.sum(err * err, axis=0, keepdims=True), accumulate=True)
        dy = err * (1.0 / D_MODEL)
        dy_ref[...] = dy
        dyb = dy.astype(BF16)
        dyc_ref[...] = _dot_nt(dyb, w_ref[...])
        for yv, (r0, n) in zip(ys, parts):
            gw_ref[r0:r0 + n, :] += _dot_tn(yv, dyb)

    row = lambda w: pl.BlockSpec((tm, w), lambda i: (i, 0))
    const = lambda shape: pl.BlockSpec(shape, lambda i: (0, 0))
    return _call(
        body, name="out_loss", grid=(nsteps,),
        in_specs=[row(GMLP_W), row(ATTN_W), row(MEM_W), row(D_MODEL), row(D_MODEL), const((D_MODEL, D_MODEL))],
        out_specs=[row(D_MODEL), row(D_MODEL), const((D_MODEL, D_MODEL)), const((8, LANES))],
        out_shape=[jax.ShapeDtypeStruct((SEQ, D_MODEL), F32), jax.ShapeDtypeStruct((SEQ, D_MODEL), F32),
                   jax.ShapeDtypeStruct((D_MODEL, D_MODEL), F32), jax.ShapeDtypeStruct((8, LANES), F32)],
        compiler_params=_params(),
    )(yg, ya, ym, x, tgt, wo)


def _proj_bwd(x, dy, gain, wt, dg, daq, dak, dav, dag, dmq, dmg):
    tm = 512
    nsteps = SEQ // tm
    pieces = ((C_GU, 3 * GMLP_W), (C_AQ, ATTN_W), (C_AK, ATTN_W), (C_AV, ATTN_W), (C_AG, ATTN_W),
              (C_MQ, MEM_W), (C_MG, MEM_W))

    def body(x_ref, dy_ref, g_ref, wt_hbm, p0, p1, p2, p3, p4, p5, p6, gx_ref, gwt_hbm, gg_ref, wt_v, acc, wt_sem, out_sems):
        i = pl.program_id(0)
        wt_load = pltpu.make_async_copy(wt_hbm, wt_v, wt_sem)

        @pl.when(i == 0)
        def _():
            wt_load.start()
            acc[...] = jnp.zeros_like(acc)
            gg_ref[...] = jnp.zeros_like(gg_ref)

        xv = x_ref[...]
        r = lax.rsqrt(jnp.mean(xv * xv, axis=-1, keepdims=True) + EPS)
        z = xv * r
        g = g_ref[...]
        h = (z * g).astype(BF16)
        pl.when(i == 0)(wt_load.wait)
        flush = [pltpu.make_async_copy(acc.at[c0:c0 + w, :], gwt_hbm.at[c0:c0 + w, :], out_sems.at[n])
                 for n, (c0, w) in enumerate(pieces)]
        dh = jnp.zeros((tm, D_MODEL), F32)
        for n, (pref, (c0, w)) in enumerate(zip((p0, p1, p2, p3, p4, p5, p6), pieces)):
            dp = pref[...]
            dh = dh + _dot(dp, wt_v[c0:c0 + w, :])
            acc[c0:c0 + w, :] += _dot_tn(dp, h)
            pl.when(i == nsteps - 1)(flush[n].start)
        _put_rows(gg_ref, jnp.sum(dh * z, axis=0, keepdims=True), accumulate=True)
        dz = dh * g
        gx_ref[...] = dy_ref[...] + r * (dz - z * jnp.mean(dz * z, axis=-1, keepdims=True))

        @pl.when(i == nsteps - 1)
        def _():
            for cp in flush:
                cp.wait()

    row = lambda w: pl.BlockSpec((tm, w), lambda i: (i, 0))
    hbm = pl.BlockSpec(memory_space=pl.ANY)
    vec = pl.BlockSpec((1, D_MODEL), lambda i: (0, 0))
    return _call(
        body, name="proj_bwd", grid=(nsteps,),
        in_specs=[row(D_MODEL), row(D_MODEL), vec, hbm] + [row(w) for _, w in pieces],
        out_specs=[row(D_MODEL), hbm, pl.BlockSpec((8, LANES), lambda i: (0, 0))],
        out_shape=[jax.ShapeDtypeStruct((SEQ, D_MODEL), F32), jax.ShapeDtypeStruct((IN_W, D_MODEL), F32),
                   jax.ShapeDtypeStruct((8, LANES), F32)],
        scratch_shapes=[pltpu.VMEM((IN_W, D_MODEL), BF16), pltpu.VMEM((IN_W, D_MODEL), F32), pltpu.SemaphoreType.DMA,
                        pltpu.SemaphoreType.DMA((len(pieces),))],
        compiler_params=_params(),
    )(x, dy, gain, wt, dg, daq, dak, dav, dag, dmq, dmg)


AG_SEMS = 8


def _gather_stages(ins, lands, send_sems, recv_sems):
    n = len(ins)
    nrows = [a.shape[0] for a in ins]
    x, y, c = lax.axis_index("x"), lax.axis_index("y"), lax.axis_index("c")
    sib, xn, yn = (x, y, 1 - c), (1 - x, y, c), (x, 1 - y, c)
    me, cx, cy, cd = 2 * x + y, 2 * (1 - x) + y, 2 * x + (1 - y), 2 * (1 - x) + (1 - y)

    def part(a, chip, hf, quarter=None):
        rows = nrows[a] // 2
        base = chip * nrows[a] + hf * rows
        if quarter is not None:
            rows = rows // 2
            base = base + quarter * rows
        return lands[a].at[pl.ds(pl.multiple_of(base, 16), rows), :]

    def copy(a, j, ref, to):
        k = AG_SEMS * a + j
        return pltpu.make_async_remote_copy(src_ref=ref, dst_ref=ref, send_sem=send_sems.at[k],
                                            recv_sem=recv_sems.at[k], device_id=to, device_id_type=MESH)

    def own(a):
        return [copy(a, 0, part(a, me, c), xn), copy(a, 1, part(a, me, c), yn)]

    def neighbours(a):
        return [copy(a, 4, part(a, cx, c, 1), yn), copy(a, 2, part(a, cx, c), sib),
                copy(a, 5, part(a, cy, c, 0), xn), copy(a, 3, part(a, cy, c), sib)]

    def diagonal(a):
        return [copy(a, 7, part(a, cd, c, 1), sib), copy(a, 6, part(a, cd, c, 0), sib)]

    def cast_own():
        for a in range(n):
            lands[a][pl.ds(pl.multiple_of(me * nrows[a], 16), nrows[a]), :] = ins[a][...].astype(BF16)

    def send_x():
        for a in range(n):
            own(a)[0].start()

    def send_y():
        for a in range(n):
            own(a)[1].start()

    def got_x():
        for a in range(n):
            copy(a, 0, part(a, cx, c), xn).wait_recv()
            for cp in neighbours(a)[0:2]:
                cp.start()

    def got_y():
        for a in range(n):
            copy(a, 1, part(a, cy, c), yn).wait_recv()
            for cp in neighbours(a)[2:4]:
                cp.start()

    def send_own():
        cast_own()
        send_x()
        send_y()

    def pass_on_neighbours():
        got_x()
        got_y()

    def pass_on_diagonal():
        for a in range(n):
            copy(a, 4, part(a, cd, c, 1), yn).wait_recv()
            copy(a, 5, part(a, cd, c, 0), xn).wait_recv()
            for cp in diagonal(a):
                cp.start()

    def y_complete():
        for a in range(n):
            copy(a, 3, part(a, cy, 1 - c), sib).wait_recv()

    def x_complete():
        for a in range(n):
            copy(a, 2, part(a, cx, 1 - c), sib).wait_recv()

    def diagonal_complete():
        for a in range(n):
            copy(a, 6, part(a, cd, 1 - c, 0), sib).wait_recv()
            copy(a, 7, part(a, cd, 1 - c, 1), sib).wait_recv()

    def sends_done():
        for a in range(n):
            for cp in own(a) + neighbours(a) + diagonal(a):
                cp.wait_send()

    def finish():
        y_complete()
        x_complete()
        diagonal_complete()
        sends_done()

    return ((send_own, pass_on_neighbours, pass_on_diagonal, finish), (y_complete, x_complete, diagonal_complete, sends_done),
            (cast_own, send_x, send_y, got_x, got_y))


RS_SEMS = 6
RS_KINDS = (((2, 2), 1, F32), ((2, 2), 1, F32), ((2, 2), 2, BF16), ((2, 2), 2, BF16), ((2, 2), 2, F32),
            ((2,), 2, BF16), ((2,), 2, BF16), ((2,), 1, F32))


def _rs_view(g):
    return g.reshape(2, 2, 2, g.shape[0] // 8, g.shape[1])


def _rs_scratch(shapes):
    return [pltpu.VMEM(lead + (r // 8, w // split), dt) for lead, split, dt in RS_KINDS for r, w in shapes]


def _rs_stages(gs, outs, bufs, send_sems, recv_sems, local_sems, widths):
    n = len(gs)
    loc, ra, s_b, r_b, acc1, s_c, r_c, fin = (bufs[n * i:n * i + n] for i in range(len(RS_KINDS)))
    half_w = [w // 2 for w in widths]
    x, y, c = lax.axis_index("x"), lax.axis_index("y"), lax.axis_index("c")
    sib, xn, yn = (x, y, 1 - c), (1 - x, y, c), (x, 1 - y, c)

    def copy(a, j, src, dst, to):
        k = RS_SEMS * a + j
        return pltpu.make_async_remote_copy(src_ref=src, dst_ref=dst, send_sem=send_sems.at[k],
                                            recv_sem=recv_sems.at[k], device_id=to, device_id_type=MESH)

    def step_a(a):
        return [copy(a, 0, gs[a].at[:, :, 1 - c], ra[a], sib),
                pltpu.make_async_copy(gs[a].at[:, :, c], loc[a], local_sems.at[a])]

    def step_b(a):
        return copy(a, 1, s_b[a].at[0], r_b[a].at[0], xn), copy(a, 2, s_b[a].at[1], r_b[a].at[1], yn)

    def step_c(a):
        return copy(a, 3, s_c[a].at[0], r_c[a].at[0], yn), copy(a, 4, s_c[a].at[1], r_c[a].at[1], xn)

    def step_d(a, half):
        rows = fin[a].at[half]
        return copy(a, 5, rows, rows, sib)

    def start():
        for a in range(n):
            for cp in step_a(a):
                cp.start()

    def a_to_b():
        for a in range(n):
            for cp in step_a(a):
                cp.wait()
            ra[a][...] = loc[a][...] + ra[a][...]
            s_b[a][0] = ra[a][1 - x, :, :, :half_w[a]].astype(BF16)
            s_b[a][1] = ra[a][:, 1 - y, :, half_w[a]:].astype(BF16)
            for cp in step_b(a):
                cp.start()

    def b_to_c():
        for a in range(n):
            for cp in step_b(a):
                cp.wait()
            acc1[a][0] = ra[a][x, :, :, :half_w[a]] + r_b[a][0].astype(F32)
            acc1[a][1] = ra[a][:, y, :, half_w[a]:] + r_b[a][1].astype(F32)
            s_c[a][0] = acc1[a][0, 1 - y].astype(BF16)
            s_c[a][1] = acc1[a][1, 1 - x].astype(BF16)
            for cp in step_c(a):
                cp.start()

    def c_to_d():
        for a in range(n):
            for cp in step_c(a):
                cp.wait()
            fin[a][c, :, :half_w[a]] = acc1[a][0, y] + r_c[a][0].astype(F32)
            fin[a][c, :, half_w[a]:] = acc1[a][1, x] + r_c[a][1].astype(F32)
            step_d(a, c).start()

    def finish():
        to_hbm = [pltpu.make_async_copy(fin[a], outs[a], local_sems.at[a]) for a in range(n)]
        for a in range(n):
            step_d(a, 1 - c).wait_recv()
            step_d(a, c).wait_send()
            to_hbm[a].start()
        for cp in to_hbm:
            cp.wait()

    return start, a_to_b, b_to_c, c_to_d, finish


def _reduce_grads(gwt, g_ws, tiny):
    cw = gwt.shape[1] // RS_CHUNKS
    chunk_shape = (gwt.shape[0], cw)

    def body(g0, ws_in, tiny_in, *rest):
        outs, o_ws, o_tiny = rest[:RS_CHUNKS], rest[RS_CHUNKS], rest[RS_CHUNKS + 1]
        rest = rest[RS_CHUNKS + 2:]
        nb = len(RS_KINDS) * RS_CHUNKS
        sm, sa, sb, sc, acc_s, send_sems, recv_sems, local_sems = rest[nb:]
        blocks = [g0.at[:, :, :, :, pl.ds(j * cw, cw)] for j in range(RS_CHUNKS)]
        start, a_to_b, b_to_c, c_to_d, finish = _rs_stages(blocks, outs, rest[:nb], send_sems, recv_sems, local_sems,
                                                           [cw] * RS_CHUNKS)
        n_ws = ws_in.shape[0]
        sm[0:n_ws, :] = ws_in[...]
        sm[n_ws:, :] = tiny_in[...]
        x, y, c = lax.axis_index("x"), lax.axis_index("y"), lax.axis_index("c")

        def small(j, src, dst, to):
            k = RS_SEMS * RS_CHUNKS + j
            return pltpu.make_async_remote_copy(src_ref=src, dst_ref=dst, send_sem=send_sems.at[k],
                                                recv_sem=recv_sems.at[k], device_id=to, device_id_type=MESH)

        along_c, along_x, along_y = (small(0, sm, sa, (x, y, 1 - c)), small(1, acc_s, sb, (1 - x, y, c)),
                                     small(2, sb, sc, (x, 1 - y, c)))
        start()
        along_c.start()
        a_to_b()
        along_c.wait()
        acc_s[...] = sm[...] + sa[...]
        along_x.start()
        b_to_c()
        along_x.wait()
        sb[...] = acc_s[...] + sb[...]
        along_y.start()
        c_to_d()
        along_y.wait()
        o_ws[...] = sb[0:n_ws, :] + sc[0:n_ws, :]
        o_tiny[...] = sb[n_ws:, :] + sc[n_ws:, :]
        finish()

    vm = pl.BlockSpec(memory_space=pltpu.VMEM)
    hbm = pl.BlockSpec(memory_space=pl.ANY)
    small_shape = (g_ws.shape[0] + tiny.shape[0], LANES)
    scratch = _rs_scratch([chunk_shape] * RS_CHUNKS) + [pltpu.VMEM(small_shape, F32) for _ in range(5)]
    nsem = RS_SEMS * RS_CHUNKS + 3
    scratch += [pltpu.SemaphoreType.DMA((nsem,)), pltpu.SemaphoreType.DMA((nsem,)), pltpu.SemaphoreType.DMA((RS_CHUNKS,))]
    return _call(
        body, name="reduce_grads",
        out_shape=[jax.ShapeDtypeStruct((2, gwt.shape[0] // 8, cw), F32)] * RS_CHUNKS
        + [jax.ShapeDtypeStruct(g_ws.shape, F32), jax.ShapeDtypeStruct(tiny.shape, F32)],
        in_specs=[hbm, vm, vm],
        out_specs=[hbm] * RS_CHUNKS + [vm, vm],
        scratch_shapes=scratch,
        compiler_params=_params(),
    )(_rs_view(gwt), g_ws, tiny)


def _adam_update(w, g, m, v):
    nm = ADAM_B1 * m + (1.0 - ADAM_B1) * g
    nv = ADAM_B2 * v + (1.0 - ADAM_B2) * (g * g)
    m_hat = nm / (1.0 - ADAM_B1 ** ADAM_STEP)
    v_hat = nv / (1.0 - ADAM_B2 ** ADAM_STEP)
    return -ADAM_LR * (m_hat / (jnp.sqrt(v_hat) + ADAM_EPS) + ADAM_WD * w), nm, nv


def _adamw(w, g, m, v):
    rows, cols = w.shape
    tm = max(t for t in range(8, 257, 8) if rows % t == 0)
    parts = tuple(g) if isinstance(g, (tuple, list)) else (g,)
    n = len(parts)

    def body(w_ref, m_ref, v_ref, *refs):
        gv = jnp.concatenate([r[...] for r in refs[:n]], axis=1)
        d_ref, nm_ref, nv_ref = refs[n:n + 3]
        d_ref[...], nm_ref[...], nv_ref[...] = _adam_update(w_ref[...], gv, m_ref[...], v_ref[...])
        if n > 1:
            refs[n + 3][...] = gv

    blk = pl.BlockSpec((tm, cols), lambda i: (i, 0))
    nout = 3 if n == 1 else 4
    res = _call(
        body, name="adamw", grid=(rows // tm,),
        in_specs=[blk] * 3 + [pl.BlockSpec((tm, p.shape[1]), lambda i: (i, 0)) for p in parts], out_specs=[blk] * nout,
        out_shape=[jax.ShapeDtypeStruct((rows, cols), F32)] * nout,
        compiler_params=_params(),
    )(w, m, v, *parts)
    return (parts[0] if n == 1 else res[3], *res[:3])


def _adamw_tiny(tiny, weights, ms, vs):
    shapes = [w.shape for w in weights]
    n = len(weights)

    def grad_of(t_ref, k, shape):
        base = 8 * k
        if shape[1] > LANES:
            return [t_ref[base + j:base + j + 1, :] for j in range(shape[1] // LANES)]
        return [t_ref[base:base + shape[0], 0:shape[1]]]

    def body(t_ref, *refs):
        w_refs, m_refs, v_refs = refs[:n], refs[n:2 * n], refs[2 * n:3 * n]
        loss_ref, outs = refs[3 * n], refs[3 * n + 1:]
        loss_ref[...] = (0.5 / D_MODEL) * jnp.sum(t_ref[8 * n:8 * n + 8, :], keepdims=True)
        for k, shape in enumerate(shapes):
            g_ref, d_ref, nm_ref, nv_ref = outs[4 * k:4 * k + 4]
            for j, g in enumerate(grad_of(t_ref, k, shape)):
                cols = slice(j * LANES, (j + 1) * LANES) if shape[1] > LANES else slice(None)
                g_ref[:, cols] = g
                d_ref[:, cols], nm_ref[:, cols], nv_ref[:, cols] = _adam_update(
                    w_refs[k][:, cols], g, m_refs[k][:, cols], v_refs[k][:, cols])

    out_shape = [jax.ShapeDtypeStruct((1, 1), F32)]
    for shape in shapes:
        out_shape += [jax.ShapeDtypeStruct(shape, F32)] * 4
    return _call(body, name="adamw_tiny", out_shape=out_shape, compiler_params=_params())(tiny, *weights, *ms, *vs)


def _local_grads(x, mem, tgt, norm_gain, wt_sh, gmlp_v_gain, gmlp_w_s, gmlp_b, attn_q_gain, attn_k_gain,
                 mem_norm_gain, wkv_sh, mem_q_gain, mem_k_gain, wo_sh):
    vg = gmlp_v_gain.reshape(1, GMLP_W)
    bias_full = jnp.repeat(gmlp_b.T, HEAD_DIM, axis=1)
    gq2, gk2 = jnp.tile(attn_q_gain, (1, 2)), jnp.tile(attn_k_gain, (1, 2))
    qg4, kg4 = jnp.tile(mem_q_gain, (1, 4)), jnp.tile(mem_k_gain, (1, 4))

    proj, wt, wkv, wo = _gather_proj(x, norm_gain, wt_sh, wkv_sh, wo_sh)
    yg = _gmlp_fwd(proj, vg, gmlp_w_s, bias_full)
    o, lse, ya = _attn_fwd(proj, gq2, gk2)
    kv, hm = _mem_kv(mem, mem_norm_gain, wkv)
    om, ym = _mem_fwd(proj, kv, qg4, kg4)
    dy, dyc, g_wo, err2 = _out_loss(yg, ya, ym, x, tgt, wo)
    dmq, dmg, g_mq, g_mk, g_wkv, g_mng = _mem_bwd(proj, om, dyc, kv, hm, mem, mem_norm_gain, wkv, qg4, kg4)
    daq, dak, dav, dag, g_aq, g_ak, g_wkv_sh, g_wo_sh = _attn_bwd(proj, o, lse, dyc, gq2, gk2, g_wkv, g_wo)
    dg, g_ws, g_b, g_vg = _gmlp_bwd(proj, dyc, vg, gmlp_w_s, bias_full)
    gx, g_wt, g_ng = _proj_bwd(x, dy, norm_gain, wt, dg, daq, dak, dav, dag, dmq, dmg)

    tiny = jnp.concatenate([g_ng, g_vg, g_b, g_aq, g_ak, g_mng, g_mq, g_mk, err2], axis=0)
    return gx, g_wt, g_wkv_sh, g_wo_sh, g_ws.reshape(4 * CHUNK, CHUNK), tiny


def kernel(x, mem, norm_gain, w_in, gmlp_v_gain, gmlp_w_s, gmlp_b, attn_q_gain, attn_k_gain, mem_norm_gain, w_mem_kv, mem_q_gain, mem_k_gain, w_out, loss_target, m_norm_gain, m_w_in, m_gmlp_v_gain, m_gmlp_w_s, m_gmlp_b, m_attn_q_gain, m_attn_k_gain, m_mem_norm_gain, m_w_mem_kv, m_mem_q_gain, m_mem_k_gain, m_w_out, v_norm_gain, v_w_in, v_gmlp_v_gain, v_gmlp_w_s, v_gmlp_b, v_attn_q_gain, v_attn_k_gain, v_mem_norm_gain, v_w_mem_kv, v_mem_q_gain, v_mem_k_gain, v_w_out):
    gx, g_wt, g_wkv_sh, g_wo_sh, g_ws, tiny = _local_grads(
        x[0], mem[0], loss_target[0], norm_gain, w_in[0].T, gmlp_v_gain[0], gmlp_w_s[0], gmlp_b[0],
        attn_q_gain, attn_k_gain, mem_norm_gain, w_mem_kv[0], mem_q_gain, mem_k_gain, w_out[0])
    *g_wt_sh, g_ws, tiny = _reduce_grads(g_wt, g_ws, tiny)
    chip_block = lambda g: g.reshape(2 * g.shape[1], g.shape[2])
    g_wt_sh = tuple(chip_block(g) for g in g_wt_sh)
    g_wkv_sh, g_wo_sh = chip_block(g_wkv_sh), chip_block(g_wo_sh)

    ws = (norm_gain, w_in, gmlp_v_gain, gmlp_w_s, gmlp_b, attn_q_gain, attn_k_gain, mem_norm_gain, w_mem_kv,
          mem_q_gain, mem_k_gain, w_out)
    ms = (m_norm_gain, m_w_in, m_gmlp_v_gain, m_gmlp_w_s, m_gmlp_b, m_attn_q_gain, m_attn_k_gain, m_mem_norm_gain,
          m_w_mem_kv, m_mem_q_gain, m_mem_k_gain, m_w_out)
    vs = (v_norm_gain, v_w_in, v_gmlp_v_gain, v_gmlp_w_s, v_gmlp_b, v_attn_q_gain, v_attn_k_gain, v_mem_norm_gain,
          v_w_mem_kv, v_mem_q_gain, v_mem_k_gain, v_w_out)
    form = {1: lambda a: a[0].T, 3: lambda a: a.reshape(4 * CHUNK, CHUNK), 2: lambda a: a[0], 4: lambda a: a[0],
            8: lambda a: a[0], 11: lambda a: a[0]}
    back = {1: lambda a: a.T[None], 3: lambda a: a.reshape(1, 4, CHUNK, CHUNK), 2: lambda a: a[None],
            4: lambda a: a[None], 8: lambda a: a[None], 11: lambda a: a[None]}
    fwd = lambda t, i: form.get(i, lambda a: a)(t[i])
    out = {}
    for i, g in ((1, g_wt_sh), (3, g_ws), (8, g_wkv_sh), (11, g_wo_sh)):
        out[i] = _adamw(fwd(ws, i), g, fwd(ms, i), fwd(vs, i))
    res = _adamw_tiny(tiny, [fwd(ws, i) for i in TINY_ORDER], [fwd(ms, i) for i in TINY_ORDER],
                      [fwd(vs, i) for i in TINY_ORDER])
    for k, i in enumerate(TINY_ORDER):
        out[i] = res[1 + 4 * k:5 + 4 * k]
    leaves = [[back.get(i, lambda a: a)(out[i][j]) for i in range(12)] for j in range(4)]
    return (res[0].reshape(()), gx[None], *leaves[0], *leaves[1], *leaves[2], *leaves[3])
```

```python
import math

import jax
import jax.numpy as jnp
from jax import lax
from jax.experimental import pallas as pl
from jax.experimental.pallas import tpu as pltpu

F32 = jnp.float32
BF16 = jnp.bfloat16

SEQ = 4096
D_MODEL = 1024
HEAD_DIM = 64
LANES = 128
CHUNK = 128
GMLP_W, ATTN_W, MEM_W = 256, 512, 256
IN_W = 3 * GMLP_W + 4 * ATTN_W + 2 * MEM_W
MEM_LEN = 256
DILATIONS = (16, 4, 1)
EPS = 1e-6
QK_SCALE = 1.0 / math.sqrt(HEAD_DIM)
C_GU, C_GV, C_GG, C_AQ, C_AK, C_AV, C_AG, C_MQ, C_MG = 0, 256, 512, 768, 1280, 1792, 2304, 2816, 3072

ADAM_LR, ADAM_B1, ADAM_B2, ADAM_EPS, ADAM_WD, ADAM_STEP = 0.001, 0.9, 0.999, 1e-08, 0.01, 10

VMEM_LIMIT = 48 * 1024 * 1024
RS_CHUNKS = 4
ATTN_UNROLL = 4
MESH = pl.DeviceIdType.MESH

TINY_ORDER = (0, 2, 4, 5, 6, 7, 9, 10)


def _call(body, **kw):
    return pl.pallas_call(body, **kw)


def _params(**kw):
    return pltpu.CompilerParams(vmem_limit_bytes=VMEM_LIMIT, **kw)


def _dot(a, b):
    return jnp.dot(a, b, preferred_element_type=F32)


def _dot_nt(a, b):
    return lax.dot_general(a, b, (((1,), (1,)), ((), ())), preferred_element_type=F32)


def _dot_tn(a, b):
    return lax.dot_general(a, b, (((0,), (0,)), ((), ())), preferred_element_type=F32)


def _head_blockdiag():
    r = lax.shift_right_logical(lax.broadcasted_iota(jnp.int32, (LANES, LANES), 0), 6)
    c = lax.shift_right_logical(lax.broadcasted_iota(jnp.int32, (LANES, LANES), 1), 6)
    return jnp.where(r == c, 1.0, 0.0).astype(BF16)


def _headsum(v, bd):
    hi = v.astype(BF16)
    lo = (v - hi.astype(F32)).astype(BF16)
    return _dot(hi, bd) + _dot(lo, bd)


def _lo_mask(rows):
    return lax.broadcasted_iota(jnp.int32, (rows, LANES), 1) < HEAD_DIM


def _sigmoid(x):
    return 1.0 / (1.0 + jnp.exp(-x))


def _fold_heads(v):
    return v + pltpu.roll(v, HEAD_DIM, 1)


def _put_rows(ref, vec, accumulate=False):
    for j in range(vec.shape[1] // LANES):
        piece = vec[:, j * LANES:(j + 1) * LANES]
        ref[j:j + 1, :] = ref[j:j + 1, :] + piece if accumulate else piece


def _gather_proj(x, gain, wt_sh):
    tm = 512
    nrow = SEQ // tm
    widths = (768, 896, 768, 896)
    nunits = len(widths)
    pair = 2 * wt_sh.shape[0]
    assert pair % LANES == 0 and sum(widths[:2]) == pair

    def body(x_ref, g_ref, wt_sh_ref, proj_hbm, wt_hbm, h_scr, land, res, send_sems, recv_sems, out_sems, copy_sem):
        u, i = pl.program_id(0), pl.program_id(1)
        cx_, cy_ = lax.axis_index("x"), lax.axis_index("y")
        (send_own, pass_on_neighbours, pass_on_diagonal, _), (y_complete, x_complete, diagonal_complete, sends_done) = (
            _gather_stages((wt_sh_ref,), (land,), send_sems, recv_sems))
        first = lambda k: (u == k) & (i == 0)
        last = (u == nunits - 1) & (i == nrow - 1)
        to_hbm = pltpu.make_async_copy(land, wt_hbm, copy_sem)

        pl.when(first(0))(send_own)

        @pl.when(u == 0)
        def _():
            xv = x_ref[...]
            ms = jnp.mean(xv * xv, axis=-1, keepdims=True)
            h_scr[pl.ds(pl.multiple_of(i * tm, tm), tm), :] = (xv * lax.rsqrt(ms + EPS) * g_ref[...]).astype(BF16)

        @pl.when(first(1))
        def _():
            pass_on_neighbours()
            y_complete()

        @pl.when(first(2))
        def _():
            x_complete()
            pass_on_diagonal()

        @pl.when(first(3))
        def _():
            diagonal_complete()
            to_hbm.start()

        mine, other = pair * cx_, pair * (1 - cx_)
        col0 = (mine + 896 * cy_, mine + 768 * (1 - cy_), other + 896 * cy_, other + 768 * (1 - cy_))
        slot = i % 2
        rows = pl.ds(pl.multiple_of(i * tm, tm), tm)

        def writeback(k, rows_):
            c0 = pl.multiple_of(col0[k], LANES)
            return pltpu.make_async_copy(res.at[slot, :, pl.ds(0, widths[k])], proj_hbm.at[rows_, pl.ds(c0, widths[k])],
                                         out_sems.at[slot])

        for k in range(nunits):
            @pl.when(u == k)
            def _(k=k):
                pl.when(i >= 2)(writeback(k, rows).wait)
                if k > 0:
                    pl.when(i < 2)(writeback(k - 1, rows).wait)
                w_rows = land[pl.ds(pl.multiple_of(col0[k], LANES), widths[k]), :]
                res[slot, :, 0:widths[k]] = _dot_nt(h_scr[rows, :], w_rows)
                writeback(k, rows).start()

        @pl.when(last)
        def _():
            sends_done()
            to_hbm.wait()
            for s in range(2):
                pltpu.make_async_copy(res.at[s, :, pl.ds(0, widths[-1])], proj_hbm.at[rows, pl.ds(0, widths[-1])], out_sems.at[s]).wait()

    full = jax.ShapeDtypeStruct((4 * wt_sh.shape[0], wt_sh.shape[1]), BF16)
    hbm = pl.BlockSpec(memory_space=pl.ANY)
    return _call(
        body, name="gather_proj", grid=(nunits, nrow),
        in_specs=[pl.BlockSpec((tm, D_MODEL), lambda u, i: (jnp.where(u == 0, i, nrow - 1), 0)),
                  pl.BlockSpec((1, D_MODEL), lambda u, i: (0, 0)), pl.BlockSpec(wt_sh.shape, lambda u, i: (0, 0))],
        out_specs=[hbm, hbm],
        out_shape=[jax.ShapeDtypeStruct((SEQ, IN_W), F32), full],
        scratch_shapes=[pltpu.VMEM((SEQ, D_MODEL), BF16), pltpu.VMEM(full.shape, BF16), pltpu.VMEM((2, tm, max(widths)), F32),
                        pltpu.SemaphoreType.DMA((AG_SEMS,)), pltpu.SemaphoreType.DMA((AG_SEMS,)),
                        pltpu.SemaphoreType.DMA((2,)), pltpu.SemaphoreType.DMA],
        compiler_params=_params(),
    )(x, gain, wt_sh)


def _gmlp_weights(w_ref):
    ti = lax.broadcasted_iota(jnp.int32, (CHUNK, CHUNK), 0)
    si = lax.broadcasted_iota(jnp.int32, (CHUNK, CHUNK), 1)
    tril = si <= ti
    return tril, [jnp.where(tril, w_ref[h], 0.0).astype(BF16) for h in range(4)]


def _gmlp_fwd(proj, vgain, w_s, bias_full):
    tm = 512

    def body(p_ref, vg_ref, w_ref, b_ref, y_ref):
        bd = _head_blockdiag()
        lo = _lo_mask(CHUNK)
        _, wm = _gmlp_weights(w_ref)
        units = [(pl.ds(c * CHUNK, CHUNK), p) for c in range(tm // CHUNK) for p in range(2)]
        col = lambda c0, p: slice(c0 + p * LANES, c0 + (p + 1) * LANES)
        vs = [p_ref[rows, col(C_GV, p)] for rows, p in units]
        rs = [lax.rsqrt(_headsum(v * v, bd) * (1.0 / HEAD_DIM) + EPS) for v in vs]
        vns = [(v * r * vg_ref[:, col(0, p)]).astype(BF16) for v, r, (_, p) in zip(vs, rs, units)]
        sps = [jnp.where(lo, _dot(wm[2 * p], vn), _dot(wm[2 * p + 1], vn)) + b_ref[:, col(0, p)] for vn, (_, p) in zip(vns, units)]
        for sp, (rows, p) in zip(sps, units):
            gt = p_ref[rows, col(C_GG, p)]
            y_ref[rows, col(0, p)] = (p_ref[rows, col(C_GU, p)] * sp * (gt * _sigmoid(gt))).astype(BF16)

    return _call(
        body, name="gmlp_fwd", grid=(SEQ // tm,),
        in_specs=[pl.BlockSpec((tm, 3 * GMLP_W), lambda i: (i, 0)),
                  pl.BlockSpec((1, GMLP_W), lambda i: (0, 0)),
                  pl.BlockSpec((4, CHUNK, CHUNK), lambda i: (0, 0, 0)),
                  pl.BlockSpec((CHUNK, GMLP_W), lambda i: (0, 0))],
        out_specs=pl.BlockSpec((tm, GMLP_W), lambda i: (i, 0)),
        out_shape=jax.ShapeDtypeStruct((SEQ, GMLP_W), BF16),
        compiler_params=_params(),
    )(proj, vgain, w_s, bias_full)


def _gmlp_bwd(proj, dyc, vgain, w_s, bias_full):
    tm = 512
    nsteps = SEQ // tm

    def body(p_ref, dy_ref, vg_ref, w_ref, b_ref, dg_ref, gw_ref, gb_ref, gv_ref):
        i = pl.program_id(0)
        bd = _head_blockdiag()
        lo = _lo_mask(CHUNK)
        tril, wm = _gmlp_weights(w_ref)
        ri = lax.broadcasted_iota(jnp.int32, (16, LANES), 0)
        li = lax.broadcasted_iota(jnp.int32, (16, LANES), 1)
        head_rows = [jnp.where(((ri == 2 * p) & (li < HEAD_DIM)) | ((ri == 2 * p + 1) & (li >= HEAD_DIM)), 1.0, 0.0).astype(BF16)
                     for p in range(2)]

        @pl.when(i == 0)
        def _():
            gw_ref[...] = jnp.zeros_like(gw_ref)
            gb_ref[...] = jnp.zeros_like(gb_ref)
            gv_ref[...] = jnp.zeros_like(gv_ref)

        units = [(pl.ds(c * CHUNK, CHUNK), p) for c in range(tm // CHUNK) for p in range(2)]
        col = lambda c0, p: slice(c0 + p * LANES, c0 + (p + 1) * LANES)
        vs = [p_ref[rows, col(C_GV, p)] for rows, p in units]
        rs = [lax.rsqrt(_headsum(v * v, bd) * (1.0 / HEAD_DIM) + EPS) for v in vs]
        zs = [v * r for v, r in zip(vs, rs)]
        vns = [(z * vg_ref[:, col(0, p)]).astype(BF16) for z, (_, p) in zip(zs, units)]
        sps = [jnp.where(lo, _dot(wm[2 * p], vn), _dot(wm[2 * p + 1], vn)) + b_ref[:, col(0, p)] for vn, (_, p) in zip(vns, units)]
        dsps = []
        for sp, (rows, p) in zip(sps, units):
            u = p_ref[rows, col(C_GU, p)]
            gt = p_ref[rows, col(C_GG, p)]
            dy = dy_ref[rows, col(0, p)]
            sg = _sigmoid(gt)
            sl = gt * sg
            dg_ref[rows, col(C_GU, p)] = (dy * sp * sl).astype(BF16)
            dg_ref[rows, col(C_GG, p)] = (dy * u * sp * (sg * (1.0 + gt * (1.0 - sg)))).astype(BF16)
            dsps.append(dy * u * sl)
        dspbs = [dsp.astype(BF16) for dsp in dsps]
        dvns = [jnp.where(lo, _dot_tn(wm[2 * p], dspb), _dot_tn(wm[2 * p + 1], dspb)) for dspb, (_, p) in zip(dspbs, units)]
        gws = [(_dot_nt(jnp.where(lo, dsp, 0.0).astype(BF16), vn), _dot_nt(jnp.where(lo, 0.0, dsp).astype(BF16), vn))
               for dsp, vn in zip(dsps, vns)]
        gbs = [(_dot_nt(head_rows[p], dspb) + _dot_nt(head_rows[p], (dsp - dspb.astype(F32)).astype(BF16)))[0:8]
               for dsp, dspb, (_, p) in zip(dsps, dspbs, units)]
        for p in range(2):
            mine = [n for n, (_, q) in enumerate(units) if q == p]
            gw_ref[2 * p] += sum(gws[n][0] for n in mine)
            gw_ref[2 * p + 1] += sum(gws[n][1] for n in mine)
            gvp = sum(jnp.sum(dvns[n] * zs[n], axis=0, keepdims=True) for n in mine)
            gv_ref[2 * p:2 * p + 1, :] += gvp
            gv_ref[2 * p + 1:2 * p + 2, :] += pltpu.roll(gvp, HEAD_DIM, 1)
        gb_ref[...] += sum(gbs)
        for dvn, z, r, (rows, p) in zip(dvns, zs, rs, units):
            dz = dvn * vg_ref[:, col(0, p)]
            dg_ref[rows, col(C_GV, p)] = (r * (dz - z * (_headsum(dz * z, bd) * (1.0 / HEAD_DIM)))).astype(BF16)

        @pl.when(i == nsteps - 1)
        def _():
            for h in range(4):
                gw_ref[h] = jnp.where(tril, gw_ref[h], 0.0)

    return _call(
        body, name="gmlp_bwd", grid=(nsteps,),
        in_specs=[pl.BlockSpec((tm, 3 * GMLP_W), lambda i: (i, 0)),
                  pl.BlockSpec((tm, GMLP_W), lambda i: (i, 0)),
                  pl.BlockSpec((1, GMLP_W), lambda i: (0, 0)),
                  pl.BlockSpec((4, CHUNK, CHUNK), lambda i: (0, 0, 0)),
                  pl.BlockSpec((CHUNK, GMLP_W), lambda i: (0, 0))],
        out_specs=[pl.BlockSpec((tm, 3 * GMLP_W), lambda i: (i, 0)),
                   pl.BlockSpec((4, CHUNK, CHUNK), lambda i: (0, 0, 0)),
                   pl.BlockSpec((8, LANES), lambda i: (0, 0)),
                   pl.BlockSpec((8, LANES), lambda i: (0, 0))],
        out_shape=[jax.ShapeDtypeStruct((SEQ, 3 * GMLP_W), BF16),
                   jax.ShapeDtypeStruct((4, CHUNK, CHUNK), F32),
                   jax.ShapeDtypeStruct((8, LANES), F32),
                   jax.ShapeDtypeStruct((8, LANES), F32)],
        compiler_params=_params(),
    )(proj, dyc, vgain, w_s, bias_full)


def _band_masks():
    qi = lax.broadcasted_iota(jnp.int32, (CHUNK, 2 * CHUNK), 0)
    kj = lax.broadcasted_iota(jnp.int32, (CHUNK, 2 * CHUNK), 1)
    valid2 = ((kj < CHUNK) & (kj >= qi)) | ((kj >= CHUNK) & (kj - CHUNK <= qi))
    q1 = lax.broadcasted_iota(jnp.int32, (CHUNK, CHUNK), 0)
    k1 = lax.broadcasted_iota(jnp.int32, (CHUNK, CHUNK), 1)
    return k1 <= q1, valid2


def _stack_heads(v, lo):
    return jnp.concatenate([jnp.where(lo, v, 0.0), jnp.where(lo, 0.0, v)], axis=0).astype(BF16)


def _rows_of(ref, start, d):
    if d == 1:
        return ref.at[pl.ds(start if isinstance(start, int) else pl.multiple_of(start, CHUNK), CHUNK), :]
    return ref.at[pl.ds(start, CHUNK, stride=d), :]


def _unrolled(lo, hi, unroll, run):
    groups = (hi - lo) // unroll
    if groups:
        def body(g, carry):
            run([lo + g * unroll + t for t in range(unroll)])
            return carry

        lax.fori_loop(0, groups, body, 0)
    if lo + groups * unroll < hi:
        run(range(lo + groups * unroll, hi))


def _for_blocks(d, group_fn, unroll):
    nblk = SEQ // CHUNK
    sh = d.bit_length() - 1

    def first(j):
        return (j * CHUNK if d == 1 else j, None)

    def rest(j):
        start = (j & (d - 1)) + (j >> sh) * (CHUNK * d)
        return (start, start - CHUNK * d)

    _unrolled(0, d, unroll, lambda js: group_fn(d, [first(j) for j in js]))
    _unrolled(d, nblk, unroll, lambda js: group_fn(d, [rest(j) for j in js]))


def _attn_fwd(proj, gq2, gk2, *ride_along):
    tn = 512
    npairs = ATTN_W // LANES
    nride = len(ride_along)

    def body(q_ref, k_ref, v_ref, g_ref, gq_ref, gk_ref, *rest):
        shards, rest = rest[:nride], rest[nride:]
        o_ref, l_ref, ya_ref = rest[:3]
        gathered, rest = rest[3:3 + nride], rest[3 + nride:]
        qn_ref, kn_ref = rest[:2]
        lands, (send_sems, recv_sems, copy_sems) = rest[2:2 + nride], rest[2 + nride:]
        pair = pl.program_id(0)
        ride = _gather_stages(shards, lands, send_sems, recv_sems)[0]
        for step in range(npairs):
            pl.when(pair == step)(ride[step])
        bd = _head_blockdiag()
        lo = _lo_mask(CHUNK)
        valid1, valid2 = _band_masks()

        def norm(t, carry):
            rows = pl.ds(pl.multiple_of(t * tn, tn), tn)
            q, k = q_ref[rows, :], k_ref[rows, :]
            ssq = [_headsum(a * a, bd) for a in (q, k)]
            qn_ref[rows, :] = q * lax.rsqrt(ssq[0] * (1.0 / HEAD_DIM) + EPS) * (gq_ref[...] * QK_SCALE)
            kn_ref[rows, :] = k * lax.rsqrt(ssq[1] * (1.0 / HEAD_DIM) + EPS) * gk_ref[...]
            return carry

        lax.fori_loop(0, SEQ // tn, norm, 0)

        def load_kv(ref, d, start, prev):
            own = _rows_of(ref, start, d)[...]
            if prev is None:
                return own.astype(BF16)
            return jnp.concatenate([_rows_of(ref, prev, d)[...], own], axis=0).astype(BF16)

        def group(d, blocks):
            valid = valid1 if blocks[0][1] is None else valid2
            valid = jnp.concatenate([valid, valid], axis=0)
            qs = [_rows_of(qn_ref, start, d)[...] for start, _ in blocks]
            ks = [load_kv(kn_ref, d, start, prev) for start, prev in blocks]
            vs = [load_kv(v_ref, d, start, prev) for start, prev in blocks]
            ss = [_dot_nt(_stack_heads(q, lo), k) for q, k in zip(qs, ks)]
            ms, ps, ls = [], [], []
            for s in ss:
                s = jnp.where(valid, s, -jnp.inf)
                m = jnp.max(s, axis=-1, keepdims=True)
                p = jnp.exp(s - m)
                ms.append(m)
                ls.append(jnp.sum(p, axis=-1, keepdims=True))
                ps.append(p.astype(BF16))
            os_ = [_dot(p, v) for p, v in zip(ps, vs)]
            for b, (start, _) in enumerate(blocks):
                heads = lambda v: jnp.where(lo, v[:CHUNK], v[CHUNK:])
                lsum = heads(ls[b])
                ob = heads(os_[b]) * (1.0 / lsum)
                lb = heads(ms[b]) + jnp.log(lsum)
                o_rows = _rows_of(o_ref, start, d)
                l_rows = _rows_of(l_ref, start, d)
                if d != DILATIONS[0]:
                    lold = l_rows[...]
                    mx = jnp.maximum(lold, lb)
                    ea = jnp.exp(lold - mx)
                    eb = jnp.exp(lb - mx)
                    inv = 1.0 / (ea + eb)
                    ob = o_rows[...] * (ea * inv) + ob * (eb * inv)
                    lb = mx + jnp.log(ea + eb)
                o_rows[...] = ob
                l_rows[...] = lb

        for d in DILATIONS:
            _for_blocks(d, group, ATTN_UNROLL)

        def fin(t, carry):
            rows = pl.ds(pl.multiple_of(t * tn, tn), tn)
            g = g_ref[rows, :]
            ya_ref[rows, :] = (o_ref[rows, :] * (g * _sigmoid(g))).astype(BF16)
            return carry

        lax.fori_loop(0, SEQ // tn, fin, 0)

        @pl.when(pair == npairs - 1)
        def _():
            to_hbm = [pltpu.make_async_copy(land, out, copy_sems.at[n]) for n, (land, out) in enumerate(zip(lands, gathered))]
            for cp in to_hbm:
                cp.start()
            for cp in to_hbm:
                cp.wait()

    col = lambda c0: pl.BlockSpec((SEQ, LANES), lambda p: (0, c0 // LANES + p))
    vec = pl.BlockSpec((1, LANES), lambda p: (0, 0))
    out = pl.BlockSpec((SEQ, LANES), lambda p: (0, p))
    full = [jax.ShapeDtypeStruct((4 * a.shape[0], a.shape[1]), BF16) for a in ride_along]
    return _call(
        body, name="attn_fwd", grid=(npairs,),
        in_specs=[col(C_AQ), col(C_AK), col(C_AV), col(C_AG), vec, vec]
        + [pl.BlockSpec(a.shape, lambda p: (0, 0)) for a in ride_along],
        out_specs=[out, out, out] + [pl.BlockSpec(memory_space=pl.ANY)] * nride,
        out_shape=[jax.ShapeDtypeStruct((SEQ, ATTN_W), F32), jax.ShapeDtypeStruct((SEQ, ATTN_W), F32),
                   jax.ShapeDtypeStruct((SEQ, ATTN_W), BF16)] + full,
        scratch_shapes=[pltpu.VMEM((SEQ, LANES), F32), pltpu.VMEM((SEQ, LANES), F32)]
        + [pltpu.VMEM(s.shape, BF16) for s in full]
        + [pltpu.SemaphoreType.DMA((AG_SEMS * nride,)), pltpu.SemaphoreType.DMA((AG_SEMS * nride,)),
           pltpu.SemaphoreType.DMA((nride,))],
        compiler_params=_params(),
    )(proj, proj, proj, proj, gq2, gk2, *ride_along)


def _attn_bwd(proj, o, lse, dyc, gq2, gk2, *ride_along):
    tn = 512
    npairs = ATTN_W // LANES
    nride = len(ride_along)
    nbufs = nride * len(RS_KINDS)

    def body(proj_hbm, o_hbm, l_hbm, dyc_hbm, gq_ref, gk_ref, *rest):
        ride_in, rest = rest[:nride], rest[nride:]
        dq_ref, dk_ref, dv_ref, dgt_ref, gqg_ref, gkg_ref = rest[:6]
        ride_out, rest = rest[6:6 + nride], rest[6 + nride:]
        qb_, kb_, vb_, gb_, ob_, lb_, yb_, dkb_, dvb_, sems = rest[:10]
        rs_bufs, (send_sems, recv_sems, local_sems) = rest[10:10 + nbufs], rest[10 + nbufs:]
        rs_stage = _rs_stages(ride_in, ride_out, rs_bufs, send_sems, recv_sems, local_sems, [g.shape[1] for g in ride_along])
        pair = pl.program_id(0)
        for step in range(npairs):
            pl.when(pair == step)(rs_stage[step])
        bd = _head_blockdiag()
        lo = _lo_mask(CHUNK)
        lo2 = lax.broadcasted_iota(jnp.int32, (2 * CHUNK, LANES), 1) < HEAD_DIM
        valid1, valid2 = _band_masks()
        gqs = gq_ref[...] * QK_SCALE
        gk = gk_ref[...]

        def pcol(c0, of=None):
            return acol(proj_hbm, c0, of)

        def acol(hbm, c0=0, of=None):
            of = pair if of is None else of
            return hbm.at[:, pl.ds(pl.multiple_of(c0 + of * LANES, LANES), LANES)]

        def input_loads(of):
            return [pltpu.make_async_copy(src, dst, sems.at[n]) for n, (src, dst) in enumerate((
                (pcol(C_AQ, of), qb_), (pcol(C_AK, of), kb_), (pcol(C_AG, of), gb_), (acol(o_hbm, 0, of), ob_),
                (acol(dyc_hbm, GMLP_W, of), yb_), (pcol(C_AV, of), vb_), (acol(l_hbm, 0, of), lb_)))]

        early = (0, 1, 3, 4)
        loads = input_loads(pair)
        for n, cp in enumerate(loads):
            if n in early:
                pl.when(pair == 0)(cp.start)
            else:
                cp.start()

        @pl.when(pair == 0)
        def _():
            gqg_ref[...] = jnp.zeros_like(gqg_ref)
            gkg_ref[...] = jnp.zeros_like(gkg_ref)

        def pre_qk(t, carry):
            rows = pl.ds(pl.multiple_of(t * tn, tn), tn)
            q, k = qb_[rows, :], kb_[rows, :]
            ssq = [_headsum(a * a, bd) for a in (q, k)]
            qb_[rows, :] = q * lax.rsqrt(ssq[0] * (1.0 / HEAD_DIM) + EPS) * gqs
            kb_[rows, :] = k * lax.rsqrt(ssq[1] * (1.0 / HEAD_DIM) + EPS) * gk
            return carry

        def pre_gate(t, carry):
            rows = pl.ds(pl.multiple_of(t * tn, tn), tn)
            g = gb_[rows, :]
            ov = ob_[rows, :]
            dya = yb_[rows, :]
            sg = _sigmoid(g)
            dgt_ref[rows, :] = (dya * ov * (sg * (1.0 + g * (1.0 - sg)))).astype(BF16)
            do = dya * (g * sg)
            yb_[rows, :] = do
            ob_[rows, :] = jnp.where(first_half, lb_[rows, :], _headsum(do * ov, bd))
            return carry

        first_half = (lax.broadcasted_iota(jnp.int32, (tn, LANES), 1) & (HEAD_DIM - 1)) < HEAD_DIM // 2
        loads[0].wait()
        loads[1].wait()
        lax.fori_loop(0, SEQ // tn, pre_qk, 0)
        for cp in loads[2:5] + loads[6:7]:
            cp.wait()
        lax.fori_loop(0, SEQ // tn, pre_gate, 0)
        loads[5].wait()
        reloads = [pltpu.make_async_copy(pcol(C_AQ), lb_, sems.at[7]), pltpu.make_async_copy(pcol(C_AK), vb_, sems.at[8])]
        reloads[0].start()

        def load_kv(ref, d, start, prev):
            own = _rows_of(ref, start, d)[...]
            if prev is None:
                return own.astype(BF16)
            return jnp.concatenate([_rows_of(ref, prev, d)[...], own], axis=0).astype(BF16)

        def group(d, blocks):
            first = blocks[0][1] is None
            valid, lok = (valid1, lo) if first else (valid2, lo2)
            chains = [(b, h) for b in range(len(blocks)) for h in range(2)]
            mask = lambda h: lo if h == 0 else ~lo
            qs = [_rows_of(qb_, start, d)[...] for start, _ in blocks]
            dos = [_rows_of(yb_, start, d)[...] for start, _ in blocks]
            lds = [_rows_of(ob_, start, d)[...] for start, _ in blocks]
            ks = [load_kv(kb_, d, start, prev) for start, prev in blocks]
            vs = [load_kv(vb_, d, start, prev) for start, prev in blocks]
            qbs = [q.astype(BF16) for q in qs]
            dobs = [do.astype(BF16) for do in dos]
            ss = [_dot_nt(jnp.where(mask(h), qs[b], 0.0).astype(BF16), ks[b]) for b, h in chains]
            dps = [_dot_nt(jnp.where(mask(h), dos[b], 0.0).astype(BF16), vs[b]) for b, h in chains]
            pbs, dss = [], []
            for s, dp, (b, h) in zip(ss, dps, chains):
                hc, dc = h * HEAD_DIM, h * HEAD_DIM + HEAD_DIM // 2
                p = jnp.exp(jnp.where(valid, s, -jnp.inf) - lds[b][:, hc:hc + 1])
                pbs.append(p.astype(BF16))
                dss.append((p * (dp - lds[b][:, dc:dc + 1])).astype(BF16))
            dqs = [_dot(ds, ks[b]) for ds, (b, h) in zip(dss, chains)]
            dks = [_dot_tn(ds, qbs[b]) for ds, (b, h) in zip(dss, chains)]
            dvs = [_dot_tn(p, dobs[b]) for p, (b, h) in zip(pbs, chains)]
            assign = d == DILATIONS[0]
            for b, (start, prev) in enumerate(blocks):
                c0, c1 = 2 * b, 2 * b + 1
                dq_rows = _rows_of(gb_, start, d)
                dqb = jnp.where(lo, dqs[c0], dqs[c1])
                dq_rows[...] = dqb if assign else dq_rows[...] + dqb
                dkc = jnp.where(lok, dks[c0], dks[c1])
                dvc = jnp.where(lok, dvs[c0], dvs[c1])
                spans = ((start, slice(0, CHUNK), True),) if first else (
                    (prev, slice(0, CHUNK), False), (start, slice(CHUNK, 2 * CHUNK), True))
                for st, sl, own in spans:
                    dk_rows = _rows_of(dkb_, st, d)
                    dv_rows = _rows_of(dvb_, st, d)
                    if assign and own:
                        dk_rows[...] = dkc[sl]
                        dv_rows[...] = dvc[sl]
                    else:
                        dk_rows[...] = dk_rows[...] + dkc[sl]
                        dv_rows[...] = dv_rows[...] + dvc[sl]

        for d in DILATIONS:
            _for_blocks(d, group, ATTN_UNROLL)

        reloads[1].start()

        @pl.when(pair < npairs - 1)
        def _():
            nxt = input_loads(pair + 1)
            for n in early:
                nxt[n].start()

        for cp in reloads:
            cp.wait()

        def post(t, carry):
            gq_acc, gk_acc = carry
            rows = pl.ds(pl.multiple_of(t * tn, tn), tn)
            raws = [lb_[rows, :], vb_[rows, :]]
            dns = [gb_[rows, :], dkb_[rows, :]]
            rs = [lax.rsqrt(_headsum(a * a, bd) * (1.0 / HEAD_DIM) + EPS) for a in raws]
            zs = [a * r for a, r in zip(raws, rs)]
            dzs = [dn * gain for dn, gain in zip(dns, (gqs, gk))]
            means = [_headsum(dz * z, bd) * (1.0 / HEAD_DIM) for dz, z in zip(dzs, zs)]
            dq, dk = [r * (dz - z * mean) for r, dz, z, mean in zip(rs, dzs, zs, means)]
            gq, gkk = [jnp.sum(dn * z, axis=0, keepdims=True) for dn, z in zip(dns, zs)]
            dq_ref[rows, :] = dq.astype(BF16)
            dk_ref[rows, :] = dk.astype(BF16)
            dv_ref[rows, :] = dvb_[rows, :].astype(BF16)
            return gq_acc + gq * QK_SCALE, gk_acc + gkk

        zero = jnp.zeros((1, LANES), F32)
        gq_acc, gk_acc = lax.fori_loop(0, SEQ // tn, post, (zero, zero))
        gqg_ref[0:1, :] += gq_acc
        gkg_ref[0:1, :] += gk_acc

        @pl.when(pair == npairs - 1)
        def _():
            gqg_ref[0:1, :] = _fold_heads(gqg_ref[0:1, :])
            gkg_ref[0:1, :] = _fold_heads(gkg_ref[0:1, :])
            rs_stage[npairs]()

    hbm = pl.BlockSpec(memory_space=pl.ANY)
    vec = pl.BlockSpec((1, LANES), lambda p: (0, 0))
    blk8 = pl.BlockSpec((8, LANES), lambda p: (0, 0))
    out = pl.BlockSpec((SEQ, LANES), lambda p: (0, p))
    big = jax.ShapeDtypeStruct((SEQ, ATTN_W), BF16)
    nsem = RS_SEMS * nride
    return _call(
        body, name="attn_bwd", grid=(npairs,),
        in_specs=[hbm, hbm, hbm, hbm, vec, vec] + [hbm] * nride,
        out_specs=[out, out, out, out, blk8, blk8] + [hbm] * nride,
        out_shape=[big, big, big, big, jax.ShapeDtypeStruct((8, LANES), F32), jax.ShapeDtypeStruct((8, LANES), F32)]
        + [jax.ShapeDtypeStruct((2, g.shape[0] // 8, g.shape[1]), F32) for g in ride_along],
        scratch_shapes=[pltpu.VMEM((SEQ, LANES), F32) for _ in range(9)] + [pltpu.SemaphoreType.DMA((9,))]
        + _rs_scratch([g.shape for g in ride_along]) + [pltpu.SemaphoreType.DMA((nsem,)), pltpu.SemaphoreType.DMA((nsem,)),
                                     pltpu.SemaphoreType.DMA((nride,))],
        compiler_params=_params(),
    )(proj, o, lse, dyc, gq2, gk2, *[_rs_view(g) for g in ride_along])


def _mem_kv(mem, gain, wkv):
    def body(m_ref, g_ref, w_ref, kv_ref, hm_ref):
        mv = m_ref[...]
        ms = jnp.mean(mv * mv, axis=-1, keepdims=True)
        hm = (mv * lax.rsqrt(ms + EPS) * g_ref[...]).astype(BF16)
        hm_ref[...] = hm
        kv_ref[...] = _dot(hm, w_ref[...])

    return _call(
        body, name="mem_kv",
        out_shape=[jax.ShapeDtypeStruct((MEM_LEN, 2 * MEM_W), F32), jax.ShapeDtypeStruct((MEM_LEN, D_MODEL), BF16)],
        compiler_params=_params(),
    )(mem, gain, wkv)


def _mem_keys(kv_ref, kg_ref, bd, p):
    mk = kv_ref[:, p * LANES:(p + 1) * LANES]
    r = lax.rsqrt(_headsum(mk * mk, bd) * (1.0 / HEAD_DIM) + EPS)
    z = mk * r
    mkn = (z * kg_ref[:, p * LANES:(p + 1) * LANES]).astype(BF16)
    mvp = kv_ref[:, MEM_W + p * LANES:MEM_W + (p + 1) * LANES].astype(BF16)
    return mkn, mvp, r, z


def _mem_fwd(proj, kv, qg4, kg4):
    tm = 512

    def body(q_ref, g_ref, kv_ref, qg_ref, kg_ref, om_ref, ym_ref):
        bd = _head_blockdiag()
        lo = _lo_mask(tm)
        keys, qns = [], []
        for p in range(2):
            cs = slice(p * LANES, (p + 1) * LANES)
            keys.append(_mem_keys(kv_ref, kg_ref, bd, p)[:2])
            q = q_ref[:, cs]
            qns.append(q * lax.rsqrt(_headsum(q * q, bd) * (1.0 / HEAD_DIM) + EPS) * (qg_ref[:, cs] * QK_SCALE))
        chains = [(p, h) for p in range(2) for h in range(2)]
        ss = [_dot_nt(jnp.where(lo if h == 0 else ~lo, qns[p], 0.0).astype(BF16), keys[p][0]) for p, h in chains]
        es = [jnp.exp(s - jnp.max(s, axis=-1, keepdims=True)) for s in ss]
        os_ = [_dot(e.astype(BF16), keys[p][1]) for e, (p, h) in zip(es, chains)]
        res = [o * (1.0 / jnp.sum(e, axis=-1, keepdims=True)) for o, e in zip(os_, es)]
        for p in range(2):
            cs = slice(p * LANES, (p + 1) * LANES)
            ov = jnp.where(lo, res[2 * p], res[2 * p + 1])
            g = g_ref[:, cs]
            om_ref[:, cs] = ov
            ym_ref[:, cs] = (ov * (g * _sigmoid(g))).astype(BF16)

    vec = pl.BlockSpec((1, MEM_W), lambda i: (0, 0))
    return _call(
        body, name="mem_fwd", grid=(SEQ // tm,),
        in_specs=[pl.BlockSpec((tm, MEM_W), lambda i: (i, C_MQ // MEM_W)),
                  pl.BlockSpec((tm, MEM_W), lambda i: (i, C_MG // MEM_W)),
                  pl.BlockSpec((MEM_LEN, 2 * MEM_W), lambda i: (0, 0)), vec, vec],
        out_specs=[pl.BlockSpec((tm, MEM_W), lambda i: (i, 0)), pl.BlockSpec((tm, MEM_W), lambda i: (i, 0))],
        out_shape=[jax.ShapeDtypeStruct((SEQ, MEM_W), F32), jax.ShapeDtypeStruct((SEQ, MEM_W), BF16)],
        compiler_params=_params(),
    )(proj, proj, kv, qg4, kg4)


def _mem_bwd(proj, om, dyc, kv, hm, mem, mgain, wkv, qg4, kg4):
    tm = 512
    nsteps = SEQ // tm

    def body(q_ref, g_ref, om_ref, dy_ref, kv_ref, hm_ref, mem_ref, mg_ref, w_ref, qg_ref, kg_ref,
             dq_ref, dgt_ref, gqg_ref, gkg_ref, gw_ref, gmg_ref, dmk_ref, dmv_ref, gq_acc):
        i = pl.program_id(0)
        bd = _head_blockdiag()
        lo = _lo_mask(tm)
        lom = _lo_mask(MEM_LEN)

        @pl.when(i == 0)
        def _():
            dmk_ref[...] = jnp.zeros_like(dmk_ref)
            dmv_ref[...] = jnp.zeros_like(dmv_ref)
            gq_acc[...] = jnp.zeros_like(gq_acc)

        pairs = []
        for p in range(2):
            cs = slice(p * LANES, (p + 1) * LANES)
            mkn, mvp, _, _ = _mem_keys(kv_ref, kg_ref, bd, p)
            gqs = qg_ref[:, cs] * QK_SCALE
            q = q_ref[:, cs]
            r = lax.rsqrt(_headsum(q * q, bd) * (1.0 / HEAD_DIM) + EPS)
            z = q * r
            qn = z * gqs
            g = g_ref[:, cs]
            ov = om_ref[:, cs]
            dym = dy_ref[:, cs]
            sg = _sigmoid(g)
            dgt_ref[:, cs] = (dym * ov * (sg * (1.0 + g * (1.0 - sg)))).astype(BF16)
            do = dym * (g * sg)
            pairs.append(dict(cs=cs, mkn=mkn, mvp=mvp, gqs=gqs, r=r, z=z, qn=qn, qnb=qn.astype(BF16), do=do,
                              dob=do.astype(BF16), delta=_headsum(do * ov, bd)))
        chains = [(pr_, h) for pr_ in pairs for h in range(2)]
        mask = lambda h: lo if h == 0 else ~lo
        ss = [_dot_nt(jnp.where(mask(h), c["qn"], 0.0).astype(BF16), c["mkn"]) for c, h in chains]
        dps = [_dot_nt(jnp.where(mask(h), c["do"], 0.0).astype(BF16), c["mvp"]) for c, h in chains]
        prs, dss = [], []
        for s, dp, (c, h) in zip(ss, dps, chains):
            e = jnp.exp(s - jnp.max(s, axis=-1, keepdims=True))
            pr = e * (1.0 / jnp.sum(e, axis=-1, keepdims=True))
            prs.append(pr.astype(BF16))
            dss.append((pr * (dp - c["delta"][:, h * HEAD_DIM:h * HEAD_DIM + 1])).astype(BF16))
        dqs = [_dot(ds, c["mkn"]) for ds, (c, h) in zip(dss, chains)]
        dks = [_dot_tn(ds, c["qnb"]) for ds, (c, h) in zip(dss, chains)]
        dvs = [_dot_tn(pr, c["dob"]) for pr, (c, h) in zip(prs, chains)]
        for p, c in enumerate(pairs):
            cs, z, r = c["cs"], c["z"], c["r"]
            dqn = jnp.where(lo, dqs[2 * p], dqs[2 * p + 1])
            dmk_ref[:, cs] += jnp.where(lom, dks[2 * p], dks[2 * p + 1])
            dmv_ref[:, cs] += jnp.where(lom, dvs[2 * p], dvs[2 * p + 1])
            dz = dqn * c["gqs"]
            dq_ref[:, cs] = (r * (dz - z * (_headsum(dz * z, bd) * (1.0 / HEAD_DIM)))).astype(BF16)
            gq_acc[:, cs] += jnp.sum(dqn * z, axis=0, keepdims=True) * QK_SCALE

        @pl.when(i == nsteps - 1)
        def _():
            gqg_ref[...] = jnp.zeros_like(gqg_ref)
            gkg_ref[...] = jnp.zeros_like(gkg_ref)
            gqg_ref[0:1, :] = _fold_heads(gq_acc[:, 0:LANES] + gq_acc[:, LANES:2 * LANES])
            dkv = []
            gk = jnp.zeros((1, LANES), F32)
            for p in range(2):
                cs = slice(p * LANES, (p + 1) * LANES)
                _, _, r, z = _mem_keys(kv_ref, kg_ref, bd, p)
                dn = dmk_ref[:, cs]
                dz = dn * kg_ref[:, cs]
                gk = gk + jnp.sum(dn * z, axis=0, keepdims=True)
                dkv.append(r * (dz - z * (_headsum(dz * z, bd) * (1.0 / HEAD_DIM))))
            gkg_ref[0:1, :] = _fold_heads(gk)
            dkvb = jnp.concatenate(dkv + [dmv_ref[...]], axis=1).astype(BF16)
            gw_ref[...] = _dot_tn(hm_ref[...], dkvb)
            dhm = _dot_nt(dkvb, w_ref[...])
            mv = mem_ref[...]
            zm = mv * lax.rsqrt(jnp.mean(mv * mv, axis=-1, keepdims=True) + EPS)
            _put_rows(gmg_ref, jnp.sum(dhm * zm, axis=0, keepdims=True))

    const = lambda shape: pl.BlockSpec(shape, lambda i: (0,) * len(shape))
    row = lambda j: pl.BlockSpec((tm, MEM_W), lambda i: (i, j))
    blk8 = jax.ShapeDtypeStruct((8, LANES), F32)
    return _call(
        body, name="mem_bwd", grid=(nsteps,),
        in_specs=[row(C_MQ // MEM_W), row(C_MG // MEM_W), row(0), row((GMLP_W + ATTN_W) // MEM_W),
                  const((MEM_LEN, 2 * MEM_W)), const((MEM_LEN, D_MODEL)), const((MEM_LEN, D_MODEL)),
                  const((1, D_MODEL)), const((D_MODEL, 2 * MEM_W)), const((1, MEM_W)), const((1, MEM_W))],
        out_specs=[row(0), row(0), const((8, LANES)), const((8, LANES)),
                   const((D_MODEL, 2 * MEM_W)), const((8, LANES))],
        out_shape=[jax.ShapeDtypeStruct((SEQ, MEM_W), BF16), jax.ShapeDtypeStruct((SEQ, MEM_W), BF16),
                   blk8, blk8, jax.ShapeDtypeStruct((D_MODEL, 2 * MEM_W), F32), blk8],
        scratch_shapes=[pltpu.VMEM((MEM_LEN, MEM_W), F32), pltpu.VMEM((MEM_LEN, MEM_W), F32),
                        pltpu.VMEM((1, MEM_W), F32)],
        compiler_params=_params(),
    )(proj, proj, om, dyc, kv, hm, mem, mgain, wkv, qg4, kg4)


def _out_loss(yg, ya, ym, x, tgt, wo):
    tm = 512
    nsteps = SEQ // tm
    parts = ((0, GMLP_W), (GMLP_W, ATTN_W), (GMLP_W + ATTN_W, MEM_W))

    def body(yg_ref, ya_ref, ym_ref, x_ref, t_ref, w_ref, dy_ref, dyc_ref, gw_ref, ls_ref):
        i = pl.program_id(0)

        @pl.when(i == 0)
        def _():
            gw_ref[...] = jnp.zeros_like(gw_ref)
            ls_ref[...] = jnp.zeros_like(ls_ref)

        ys = (yg_ref[...], ya_ref[...], ym_ref[...])
        y = sum(_dot(yv, w_ref[r0:r0 + n, :]) for yv, (r0, n) in zip(ys, parts))
        err = x_ref[...] + y - t_ref[...]
        _put_rows(ls_ref, jnp.sum(err * err, axis=0, keepdims=True), accumulate=True)
        dy = err * (1.0 / D_MODEL)
        dy_ref[...] = dy
        dyb = dy.astype(BF16)
        dyc_ref[...] = _dot_nt(dyb, w_ref[...])
        for yv, (r0, n) in zip(ys, parts):
            gw_ref[r0:r0 + n, :] += _dot_tn(yv, dyb)

    row = lambda w: pl.BlockSpec((tm, w), lambda i: (i, 0))
    const = lambda shape: pl.BlockSpec(shape, lambda i: (0, 0))
    return _call(
        body, name="out_loss", grid=(nsteps,),
        in_specs=[row(GMLP_W), row(ATTN_W), row(MEM_W), row(D_MODEL), row(D_MODEL), const((D_MODEL, D_MODEL))],
        out_specs=[row(D_MODEL), row(D_MODEL), const((D_MODEL, D_MODEL)), const((8, LANES))],
        out_shape=[jax.ShapeDtypeStruct((SEQ, D_MODEL), F32), jax.ShapeDtypeStruct((SEQ, D_MODEL), F32),
                   jax.ShapeDtypeStruct((D_MODEL, D_MODEL), F32), jax.ShapeDtypeStruct((8, LANES), F32)],
        compiler_params=_params(),
    )(yg, ya, ym, x, tgt, wo)


def _proj_bwd(x, dy, gain, wt, dg, daq, dak, dav, dag, dmq, dmg):
    tm = 512
    nsteps = SEQ // tm
    pieces = ((C_GU, 3 * GMLP_W), (C_AQ, ATTN_W), (C_AK, ATTN_W), (C_AV, ATTN_W), (C_AG, ATTN_W),
              (C_MQ, MEM_W), (C_MG, MEM_W))

    def body(x_ref, dy_ref, g_ref, wt_hbm, p0, p1, p2, p3, p4, p5, p6, gx_ref, gwt_hbm, gg_ref, wt_v, acc, wt_sem, out_sems):
        i = pl.program_id(0)
        wt_load = pltpu.make_async_copy(wt_hbm, wt_v, wt_sem)

        @pl.when(i == 0)
        def _():
            wt_load.start()
            acc[...] = jnp.zeros_like(acc)
            gg_ref[...] = jnp.zeros_like(gg_ref)

        xv = x_ref[...]
        r = lax.rsqrt(jnp.mean(xv * xv, axis=-1, keepdims=True) + EPS)
        z = xv * r
        g = g_ref[...]
        h = (z * g).astype(BF16)
        pl.when(i == 0)(wt_load.wait)
        flush = [pltpu.make_async_copy(acc.at[c0:c0 + w, :], gwt_hbm.at[c0:c0 + w, :], out_sems.at[n])
                 for n, (c0, w) in enumerate(pieces)]
        dh = jnp.zeros((tm, D_MODEL), F32)
        for n, (pref, (c0, w)) in enumerate(zip((p0, p1, p2, p3, p4, p5, p6), pieces)):
            dp = pref[...]
            dh = dh + _dot(dp, wt_v[c0:c0 + w, :])
            acc[c0:c0 + w, :] += _dot_tn(dp, h)
            pl.when(i == nsteps - 1)(flush[n].start)
        _put_rows(gg_ref, jnp.sum(dh * z, axis=0, keepdims=True), accumulate=True)
        dz = dh * g
        gx_ref[...] = dy_ref[...] + r * (dz - z * jnp.mean(dz * z, axis=-1, keepdims=True))

        @pl.when(i == nsteps - 1)
        def _():
            for cp in flush:
                cp.wait()

    row = lambda w: pl.BlockSpec((tm, w), lambda i: (i, 0))
    hbm = pl.BlockSpec(memory_space=pl.ANY)
    vec = pl.BlockSpec((1, D_MODEL), lambda i: (0, 0))
    return _call(
        body, name="proj_bwd", grid=(nsteps,),
        in_specs=[row(D_MODEL), row(D_MODEL), vec, hbm] + [row(w) for _, w in pieces],
        out_specs=[row(D_MODEL), hbm, pl.BlockSpec((8, LANES), lambda i: (0, 0))],
        out_shape=[jax.ShapeDtypeStruct((SEQ, D_MODEL), F32), jax.ShapeDtypeStruct((IN_W, D_MODEL), F32),
                   jax.ShapeDtypeStruct((8, LANES), F32)],
        scratch_shapes=[pltpu.VMEM((IN_W, D_MODEL), BF16), pltpu.VMEM((IN_W, D_MODEL), F32), pltpu.SemaphoreType.DMA,
                        pltpu.SemaphoreType.DMA((len(pieces),))],
        compiler_params=_params(),
    )(x, dy, gain, wt, dg, daq, dak, dav, dag, dmq, dmg)


AG_SEMS = 8


def _gather_stages(ins, lands, send_sems, recv_sems):
    n = len(ins)
    nrows = [a.shape[0] for a in ins]
    x, y, c = lax.axis_index("x"), lax.axis_index("y"), lax.axis_index("c")
    sib, xn, yn = (x, y, 1 - c), (1 - x, y, c), (x, 1 - y, c)
    me, cx, cy, cd = 2 * x + y, 2 * (1 - x) + y, 2 * x + (1 - y), 2 * (1 - x) + (1 - y)

    def part(a, chip, hf, quarter=None):
        rows = nrows[a] // 2
        base = chip * nrows[a] + hf * rows
        if quarter is not None:
            rows = rows // 2
            base = base + quarter * rows
        return lands[a].at[pl.ds(pl.multiple_of(base, 16), rows), :]

    def copy(a, j, ref, to):
        k = AG_SEMS * a + j
        return pltpu.make_async_remote_copy(src_ref=ref, dst_ref=ref, send_sem=send_sems.at[k],
                                            recv_sem=recv_sems.at[k], device_id=to, device_id_type=MESH)

    def own(a):
        return [copy(a, 0, part(a, me, c), xn), copy(a, 1, part(a, me, c), yn)]

    def neighbours(a):
        return [copy(a, 4, part(a, cx, c, 1), yn), copy(a, 2, part(a, cx, c), sib),
                copy(a, 5, part(a, cy, c, 0), xn), copy(a, 3, part(a, cy, c), sib)]

    def diagonal(a):
        return [copy(a, 7, part(a, cd, c, 1), sib), copy(a, 6, part(a, cd, c, 0), sib)]

    def send_own():
        for a in range(n):
            lands[a][pl.ds(pl.multiple_of(me * nrows[a], 16), nrows[a]), :] = ins[a][...].astype(BF16)
            for cp in own(a):
                cp.start()

    def pass_on_neighbours():
        for a in range(n):
            copy(a, 0, part(a, cx, c), xn).wait_recv()
            copy(a, 1, part(a, cy, c), yn).wait_recv()
            for cp in neighbours(a):
                cp.start()

    def pass_on_diagonal():
        for a in range(n):
            copy(a, 4, part(a, cd, c, 1), yn).wait_recv()
            copy(a, 5, part(a, cd, c, 0), xn).wait_recv()
            for cp in diagonal(a):
                cp.start()

    def y_complete():
        for a in range(n):
            copy(a, 3, part(a, cy, 1 - c), sib).wait_recv()

    def x_complete():
        for a in range(n):
            copy(a, 2, part(a, cx, 1 - c), sib).wait_recv()

    def diagonal_complete():
        for a in range(n):
            copy(a, 6, part(a, cd, 1 - c, 0), sib).wait_recv()
            copy(a, 7, part(a, cd, 1 - c, 1), sib).wait_recv()

    def sends_done():
        for a in range(n):
            for cp in own(a) + neighbours(a) + diagonal(a):
                cp.wait_send()

    def finish():
        y_complete()
        x_complete()
        diagonal_complete()
        sends_done()

    return (send_own, pass_on_neighbours, pass_on_diagonal, finish), (y_complete, x_complete, diagonal_complete, sends_done)


RS_SEMS = 6
RS_KINDS = (((2, 2), 1, F32), ((2, 2), 1, F32), ((2, 2), 2, BF16), ((2, 2), 2, BF16), ((2, 2), 2, F32),
            ((2,), 2, BF16), ((2,), 2, BF16), ((2,), 1, F32))


def _rs_view(g):
    return g.reshape(2, 2, 2, g.shape[0] // 8, g.shape[1])


def _rs_scratch(shapes):
    return [pltpu.VMEM(lead + (r // 8, w // split), dt) for lead, split, dt in RS_KINDS for r, w in shapes]


def _rs_stages(gs, outs, bufs, send_sems, recv_sems, local_sems, widths):
    n = len(gs)
    loc, ra, s_b, r_b, acc1, s_c, r_c, fin = (bufs[n * i:n * i + n] for i in range(len(RS_KINDS)))
    half_w = [w // 2 for w in widths]
    x, y, c = lax.axis_index("x"), lax.axis_index("y"), lax.axis_index("c")
    sib, xn, yn = (x, y, 1 - c), (1 - x, y, c), (x, 1 - y, c)

    def copy(a, j, src, dst, to):
        k = RS_SEMS * a + j
        return pltpu.make_async_remote_copy(src_ref=src, dst_ref=dst, send_sem=send_sems.at[k],
                                            recv_sem=recv_sems.at[k], device_id=to, device_id_type=MESH)

    def step_a(a):
        return [copy(a, 0, gs[a].at[:, :, 1 - c], ra[a], sib),
                pltpu.make_async_copy(gs[a].at[:, :, c], loc[a], local_sems.at[a])]

    def step_b(a):
        return copy(a, 1, s_b[a].at[0], r_b[a].at[0], xn), copy(a, 2, s_b[a].at[1], r_b[a].at[1], yn)

    def step_c(a):
        return copy(a, 3, s_c[a].at[0], r_c[a].at[0], yn), copy(a, 4, s_c[a].at[1], r_c[a].at[1], xn)

    def step_d(a, half):
        rows = fin[a].at[half]
        return copy(a, 5, rows, rows, sib)

    def start():
        for a in range(n):
            for cp in step_a(a):
                cp.start()

    def a_to_b():
        for a in range(n):
            for cp in step_a(a):
                cp.wait()
            ra[a][...] = loc[a][...] + ra[a][...]
            s_b[a][0] = ra[a][1 - x, :, :, :half_w[a]].astype(BF16)
            s_b[a][1] = ra[a][:, 1 - y, :, half_w[a]:].astype(BF16)
            for cp in step_b(a):
                cp.start()

    def b_to_c():
        for a in range(n):
            for cp in step_b(a):
                cp.wait()
            acc1[a][0] = ra[a][x, :, :, :half_w[a]] + r_b[a][0].astype(F32)
            acc1[a][1] = ra[a][:, y, :, half_w[a]:] + r_b[a][1].astype(F32)
            s_c[a][0] = acc1[a][0, 1 - y].astype(BF16)
            s_c[a][1] = acc1[a][1, 1 - x].astype(BF16)
            for cp in step_c(a):
                cp.start()

    def c_to_d():
        for a in range(n):
            for cp in step_c(a):
                cp.wait()
            fin[a][c, :, :half_w[a]] = acc1[a][0, y] + r_c[a][0].astype(F32)
            fin[a][c, :, half_w[a]:] = acc1[a][1, x] + r_c[a][1].astype(F32)
            step_d(a, c).start()

    def finish():
        to_hbm = [pltpu.make_async_copy(fin[a], outs[a], local_sems.at[a]) for a in range(n)]
        for a in range(n):
            step_d(a, 1 - c).wait_recv()
            step_d(a, c).wait_send()
            to_hbm[a].start()
        for cp in to_hbm:
            cp.wait()

    return start, a_to_b, b_to_c, c_to_d, finish


def _reduce_grads(gwt, g_ws, tiny):
    cw = gwt.shape[1] // RS_CHUNKS
    chunk_shape = (gwt.shape[0], cw)

    def body(g0, ws_in, tiny_in, *rest):
        outs, o_ws, o_tiny = rest[:RS_CHUNKS], rest[RS_CHUNKS], rest[RS_CHUNKS + 1]
        rest = rest[RS_CHUNKS + 2:]
        nb = len(RS_KINDS) * RS_CHUNKS
        sm, sa, sb, sc, acc_s, send_sems, recv_sems, local_sems = rest[nb:]
        blocks = [g0.at[:, :, :, :, pl.ds(j * cw, cw)] for j in range(RS_CHUNKS)]
        start, a_to_b, b_to_c, c_to_d, finish = _rs_stages(blocks, outs, rest[:nb], send_sems, recv_sems, local_sems,
                                                           [cw] * RS_CHUNKS)
        n_ws = ws_in.shape[0]
        sm[0:n_ws, :] = ws_in[...]
        sm[n_ws:, :] = tiny_in[...]
        x, y, c = lax.axis_index("x"), lax.axis_index("y"), lax.axis_index("c")

        def small(j, src, dst, to):
            k = RS_SEMS * RS_CHUNKS + j
            return pltpu.make_async_remote_copy(src_ref=src, dst_ref=dst, send_sem=send_sems.at[k],
                                                recv_sem=recv_sems.at[k], device_id=to, device_id_type=MESH)

        along_c, along_x, along_y = (small(0, sm, sa, (x, y, 1 - c)), small(1, acc_s, sb, (1 - x, y, c)),
                                     small(2, sb, sc, (x, 1 - y, c)))
        start()
        along_c.start()
        a_to_b()
        along_c.wait()
        acc_s[...] = sm[...] + sa[...]
        along_x.start()
        b_to_c()
        along_x.wait()
        sb[...] = acc_s[...] + sb[...]
        along_y.start()
        c_to_d()
        along_y.wait()
        o_ws[...] = sb[0:n_ws, :] + sc[0:n_ws, :]
        o_tiny[...] = sb[n_ws:, :] + sc[n_ws:, :]
        finish()

    vm = pl.BlockSpec(memory_space=pltpu.VMEM)
    hbm = pl.BlockSpec(memory_space=pl.ANY)
    small_shape = (g_ws.shape[0] + tiny.shape[0], LANES)
    scratch = _rs_scratch([chunk_shape] * RS_CHUNKS) + [pltpu.VMEM(small_shape, F32) for _ in range(5)]
    nsem = RS_SEMS * RS_CHUNKS + 3
    scratch += [pltpu.SemaphoreType.DMA((nsem,)), pltpu.SemaphoreType.DMA((nsem,)), pltpu.SemaphoreType.DMA((RS_CHUNKS,))]
    return _call(
        body, name="reduce_grads",
        out_shape=[jax.ShapeDtypeStruct((2, gwt.shape[0] // 8, cw), F32)] * RS_CHUNKS
        + [jax.ShapeDtypeStruct(g_ws.shape, F32), jax.ShapeDtypeStruct(tiny.shape, F32)],
        in_specs=[hbm, vm, vm],
        out_specs=[hbm] * RS_CHUNKS + [vm, vm],
        scratch_shapes=scratch,
        compiler_params=_params(),
    )(_rs_view(gwt), g_ws, tiny)


def _adam_update(w, g, m, v):
    nm = ADAM_B1 * m + (1.0 - ADAM_B1) * g
    nv = ADAM_B2 * v + (1.0 - ADAM_B2) * (g * g)
    m_hat = nm / (1.0 - ADAM_B1 ** ADAM_STEP)
    v_hat = nv / (1.0 - ADAM_B2 ** ADAM_STEP)
    return -ADAM_LR * (m_hat / (jnp.sqrt(v_hat) + ADAM_EPS) + ADAM_WD * w), nm, nv


def _adamw(w, g, m, v):
    rows, cols = w.shape
    tm = max(t for t in range(8, 257, 8) if rows % t == 0)
    parts = tuple(g) if isinstance(g, (tuple, list)) else (g,)
    n = len(parts)

    def body(w_ref, m_ref, v_ref, *refs):
        gv = jnp.concatenate([r[...] for r in refs[:n]], axis=1)
        d_ref, nm_ref, nv_ref = refs[n:n + 3]
        d_ref[...], nm_ref[...], nv_ref[...] = _adam_update(w_ref[...], gv, m_ref[...], v_ref[...])
        if n > 1:
            refs[n + 3][...] = gv

    blk = pl.BlockSpec((tm, cols), lambda i: (i, 0))
    nout = 3 if n == 1 else 4
    res = _call(
        body, name="adamw", grid=(rows // tm,),
        in_specs=[blk] * 3 + [pl.BlockSpec((tm, p.shape[1]), lambda i: (i, 0)) for p in parts], out_specs=[blk] * nout,
        out_shape=[jax.ShapeDtypeStruct((rows, cols), F32)] * nout,
        compiler_params=_params(),
    )(w, m, v, *parts)
    return (parts[0] if n == 1 else res[3], *res[:3])


def _adamw_tiny(tiny, weights, ms, vs):
    shapes = [w.shape for w in weights]
    n = len(weights)

    def grad_of(t_ref, k, shape):
        base = 8 * k
        if shape[1] > LANES:
            return [t_ref[base + j:base + j + 1, :] for j in range(shape[1] // LANES)]
        return [t_ref[base:base + shape[0], 0:shape[1]]]

    def body(t_ref, *refs):
        w_refs, m_refs, v_refs = refs[:n], refs[n:2 * n], refs[2 * n:3 * n]
        loss_ref, outs = refs[3 * n], refs[3 * n + 1:]
        loss_ref[...] = (0.5 / D_MODEL) * jnp.sum(t_ref[8 * n:8 * n + 8, :], keepdims=True)
        for k, shape in enumerate(shapes):
            g_ref, d_ref, nm_ref, nv_ref = outs[4 * k:4 * k + 4]
            for j, g in enumerate(grad_of(t_ref, k, shape)):
                cols = slice(j * LANES, (j + 1) * LANES) if shape[1] > LANES else slice(None)
                g_ref[:, cols] = g
                d_ref[:, cols], nm_ref[:, cols], nv_ref[:, cols] = _adam_update(
                    w_refs[k][:, cols], g, m_refs[k][:, cols], v_refs[k][:, cols])

    out_shape = [jax.ShapeDtypeStruct((1, 1), F32)]
    for shape in shapes:
        out_shape += [jax.ShapeDtypeStruct(shape, F32)] * 4
    return _call(body, name="adamw_tiny", out_shape=out_shape, compiler_params=_params())(tiny, *weights, *ms, *vs)


def _local_grads(x, mem, tgt, norm_gain, wt_sh, gmlp_v_gain, gmlp_w_s, gmlp_b, attn_q_gain, attn_k_gain,
                 mem_norm_gain, wkv_sh, mem_q_gain, mem_k_gain, wo_sh):
    vg = gmlp_v_gain.reshape(1, GMLP_W)
    bias_full = jnp.repeat(gmlp_b.T, HEAD_DIM, axis=1)
    gq2, gk2 = jnp.tile(attn_q_gain, (1, 2)), jnp.tile(attn_k_gain, (1, 2))
    qg4, kg4 = jnp.tile(mem_q_gain, (1, 4)), jnp.tile(mem_k_gain, (1, 4))

    proj, wt = _gather_proj(x, norm_gain, wt_sh)
    yg = _gmlp_fwd(proj, vg, gmlp_w_s, bias_full)
    o, lse, ya, wkv, wo = _attn_fwd(proj, gq2, gk2, wkv_sh, wo_sh)
    kv, hm = _mem_kv(mem, mem_norm_gain, wkv)
    om, ym = _mem_fwd(proj, kv, qg4, kg4)
    dy, dyc, g_wo, err2 = _out_loss(yg, ya, ym, x, tgt, wo)
    dmq, dmg, g_mq, g_mk, g_wkv, g_mng = _mem_bwd(proj, om, dyc, kv, hm, mem, mem_norm_gain, wkv, qg4, kg4)
    daq, dak, dav, dag, g_aq, g_ak, g_wkv_sh, g_wo_sh = _attn_bwd(proj, o, lse, dyc, gq2, gk2, g_wkv, g_wo)
    dg, g_ws, g_b, g_vg = _gmlp_bwd(proj, dyc, vg, gmlp_w_s, bias_full)
    gx, g_wt, g_ng = _proj_bwd(x, dy, norm_gain, wt, dg, daq, dak, dav, dag, dmq, dmg)

    tiny = jnp.concatenate([g_ng, g_vg, g_b, g_aq, g_ak, g_mng, g_mq, g_mk, err2], axis=0)
    return gx, g_wt, g_wkv_sh, g_wo_sh, g_ws.reshape(4 * CHUNK, CHUNK), tiny


def kernel(x, mem, norm_gain, w_in, gmlp_v_gain, gmlp_w_s, gmlp_b, attn_q_gain, attn_k_gain, mem_norm_gain, w_mem_kv, mem_q_gain, mem_k_gain, w_out, loss_target, m_norm_gain, m_w_in, m_gmlp_v_gain, m_gmlp_w_s, m_gmlp_b, m_attn_q_gain, m_attn_k_gain, m_mem_norm_gain, m_w_mem_kv, m_mem_q_gain, m_mem_k_gain, m_w_out, v_norm_gain, v_w_in, v_gmlp_v_gain, v_gmlp_w_s, v_gmlp_b, v_attn_q_gain, v_attn_k_gain, v_mem_norm_gain, v_w_mem_kv, v_mem_q_gain, v_mem_k_gain, v_w_out):
    gx, g_wt, g_wkv_sh, g_wo_sh, g_ws, tiny = _local_grads(
        x[0], mem[0], loss_target[0], norm_gain, w_in[0].T, gmlp_v_gain[0], gmlp_w_s[0], gmlp_b[0],
        attn_q_gain, attn_k_gain, mem_norm_gain, w_mem_kv[0], mem_q_gain, mem_k_gain, w_out[0])
    *g_wt_sh, g_ws, tiny = _reduce_grads(g_wt, g_ws, tiny)
    chip_block = lambda g: g.reshape(2 * g.shape[1], g.shape[2])
    g_wt_sh = tuple(chip_block(g) for g in g_wt_sh)
    g_wkv_sh, g_wo_sh = chip_block(g_wkv_sh), chip_block(g_wo_sh)

    ws = (norm_gain, w_in, gmlp_v_gain, gmlp_w_s, gmlp_b, attn_q_gain, attn_k_gain, mem_norm_gain, w_mem_kv,
          mem_q_gain, mem_k_gain, w_out)
    ms = (m_norm_gain, m_w_in, m_gmlp_v_gain, m_gmlp_w_s, m_gmlp_b, m_attn_q_gain, m_attn_k_gain, m_mem_norm_gain,
          m_w_mem_kv, m_mem_q_gain, m_mem_k_gain, m_w_out)
    vs = (v_norm_gain, v_w_in, v_gmlp_v_gain, v_gmlp_w_s, v_gmlp_b, v_attn_q_gain, v_attn_k_gain, v_mem_norm_gain,
          v_w_mem_kv, v_mem_q_gain, v_mem_k_gain, v_w_out)
    form = {1: lambda a: a[0].T, 3: lambda a: a.reshape(4 * CHUNK, CHUNK), 2: lambda a: a[0], 4: lambda a: a[0],
            8: lambda a: a[0], 11: lambda a: a[0]}
    back = {1: lambda a: a.T[None], 3: lambda a: a.reshape(1, 4, CHUNK, CHUNK), 2: lambda a: a[None],
            4: lambda a: a[None], 8: lambda a: a[None], 11: lambda a: a[None]}
    fwd = lambda t, i: form.get(i, lambda a: a)(t[i])
    out = {}
    for i, g in ((1, g_wt_sh), (3, g_ws), (8, g_wkv_sh), (11, g_wo_sh)):
        out[i] = _adamw(fwd(ws, i), g, fwd(ms, i), fwd(vs, i))
    res = _adamw_tiny(tiny, [fwd(ws, i) for i in TINY_ORDER], [fwd(ms, i) for i in TINY_ORDER],
                      [fwd(vs, i) for i in TINY_ORDER])
    for k, i in enumerate(TINY_ORDER):
        out[i] = res[1 + 4 * k:5 + 4 * k]
    leaves = [[back.get(i, lambda a: a)(out[i][j]) for i in range(12)] for j in range(4)]
    return (res[0].reshape(()), gx[None], *leaves[0], *leaves[1], *leaves[2], *leaves[3])
```

```python
import math

import jax
import jax.numpy as jnp
from jax import lax
from jax.experimental import pallas as pl
from jax.experimental.pallas import tpu as pltpu

F32 = jnp.float32
BF16 = jnp.bfloat16

SEQ = 4096
D_MODEL = 1024
HEAD_DIM = 64
LANES = 128
CHUNK = 128
GMLP_W, ATTN_W, MEM_W = 256, 512, 256
IN_W = 3 * GMLP_W + 4 * ATTN_W + 2 * MEM_W
MEM_LEN = 256
DILATIONS = (16, 4, 1)
EPS = 1e-6
QK_SCALE = 1.0 / math.sqrt(HEAD_DIM)
C_GU, C_GV, C_GG, C_AQ, C_AK, C_AV, C_AG, C_MQ, C_MG = 0, 256, 512, 768, 1280, 1792, 2304, 2816, 3072

ADAM_LR, ADAM_B1, ADAM_B2, ADAM_EPS, ADAM_WD, ADAM_STEP = 0.001, 0.9, 0.999, 1e-08, 0.01, 10

VMEM_LIMIT = 48 * 1024 * 1024
RS_CHUNKS = 4
ATTN_UNROLL = 4
MESH = pl.DeviceIdType.MESH

TINY_ORDER = (0, 2, 4, 5, 6, 7, 9, 10)


def _call(body, **kw):
    return pl.pallas_call(body, **kw)


def _params(**kw):
    return pltpu.CompilerParams(vmem_limit_bytes=VMEM_LIMIT, **kw)


def _dot(a, b):
    return jnp.dot(a, b, preferred_element_type=F32)


def _dot_nt(a, b):
    return lax.dot_general(a, b, (((1,), (1,)), ((), ())), preferred_element_type=F32)


def _dot_tn(a, b):
    return lax.dot_general(a, b, (((0,), (0,)), ((), ())), preferred_element_type=F32)


def _head_blockdiag():
    r = lax.shift_right_logical(lax.broadcasted_iota(jnp.int32, (LANES, LANES), 0), 6)
    c = lax.shift_right_logical(lax.broadcasted_iota(jnp.int32, (LANES, LANES), 1), 6)
    return jnp.where(r == c, 1.0, 0.0).astype(BF16)


def _headsum(v, bd):
    hi = v.astype(BF16)
    lo = (v - hi.astype(F32)).astype(BF16)
    return _dot(hi, bd) + _dot(lo, bd)


def _lo_mask(rows):
    return lax.broadcasted_iota(jnp.int32, (rows, LANES), 1) < HEAD_DIM


def _sigmoid(x):
    return 1.0 / (1.0 + jnp.exp(-x))


def _fold_heads(v):
    return v + pltpu.roll(v, HEAD_DIM, 1)


def _put_rows(ref, vec, accumulate=False):
    for j in range(vec.shape[1] // LANES):
        piece = vec[:, j * LANES:(j + 1) * LANES]
        ref[j:j + 1, :] = ref[j:j + 1, :] + piece if accumulate else piece


def _gather_proj(x, gain, wt_sh):
    tm = 512
    nrow = SEQ // tm
    widths = (768, 896, 768, 896)
    nunits = len(widths)
    pair = 2 * wt_sh.shape[0]
    assert pair % LANES == 0 and sum(widths[:2]) == pair

    def body(x_ref, g_ref, wt_sh_ref, proj_hbm, wt_hbm, h_scr, land, res, send_sems, recv_sems, out_sems, copy_sem):
        u, i = pl.program_id(0), pl.program_id(1)
        cx_, cy_ = lax.axis_index("x"), lax.axis_index("y")
        (send_own, pass_on_neighbours, pass_on_diagonal, _), (y_complete, x_complete, diagonal_complete, sends_done) = (
            _gather_stages((wt_sh_ref,), (land,), send_sems, recv_sems))
        first = lambda k: (u == k) & (i == 0)
        last = (u == nunits - 1) & (i == nrow - 1)
        to_hbm = pltpu.make_async_copy(land, wt_hbm, copy_sem)

        pl.when(first(0))(send_own)

        @pl.when(u == 0)
        def _():
            xv = x_ref[...]
            ms = jnp.mean(xv * xv, axis=-1, keepdims=True)
            h_scr[pl.ds(pl.multiple_of(i * tm, tm), tm), :] = (xv * lax.rsqrt(ms + EPS) * g_ref[...]).astype(BF16)

        @pl.when(first(1))
        def _():
            pass_on_neighbours()
            y_complete()

        @pl.when(first(2))
        def _():
            x_complete()
            pass_on_diagonal()

        @pl.when(first(3))
        def _():
            diagonal_complete()
            to_hbm.start()

        mine, other = pair * cx_, pair * (1 - cx_)
        col0 = (mine + 896 * cy_, mine + 768 * (1 - cy_), other + 896 * cy_, other + 768 * (1 - cy_))
        slot = i % 2
        rows = pl.ds(pl.multiple_of(i * tm, tm), tm)

        def writeback(k, rows_):
            c0 = pl.multiple_of(col0[k], LANES)
            return pltpu.make_async_copy(res.at[slot, :, pl.ds(0, widths[k])], proj_hbm.at[rows_, pl.ds(c0, widths[k])],
                                         out_sems.at[slot])

        for k in range(nunits):
            @pl.when(u == k)
            def _(k=k):
                pl.when(i >= 2)(writeback(k, rows).wait)
                if k > 0:
                    pl.when(i < 2)(writeback(k - 1, rows).wait)
                w_rows = land[pl.ds(pl.multiple_of(col0[k], LANES), widths[k]), :]
                res[slot, :, 0:widths[k]] = _dot_nt(h_scr[rows, :], w_rows)
                writeback(k, rows).start()

        @pl.when(last)
        def _():
            sends_done()
            to_hbm.wait()
            for s in range(2):
                pltpu.make_async_copy(res.at[s, :, pl.ds(0, widths[-1])], proj_hbm.at[rows, pl.ds(0, widths[-1])], out_sems.at[s]).wait()

    full = jax.ShapeDtypeStruct((4 * wt_sh.shape[0], wt_sh.shape[1]), BF16)
    hbm = pl.BlockSpec(memory_space=pl.ANY)
    return _call(
        body, name="gather_proj", grid=(nunits, nrow),
        in_specs=[pl.BlockSpec((tm, D_MODEL), lambda u, i: (jnp.where(u == 0, i, nrow - 1), 0)),
                  pl.BlockSpec((1, D_MODEL), lambda u, i: (0, 0)), pl.BlockSpec(wt_sh.shape, lambda u, i: (0, 0))],
        out_specs=[hbm, hbm],
        out_shape=[jax.ShapeDtypeStruct((SEQ, IN_W), F32), full],
        scratch_shapes=[pltpu.VMEM((SEQ, D_MODEL), BF16), pltpu.VMEM(full.shape, BF16), pltpu.VMEM((2, tm, max(widths)), F32),
                        pltpu.SemaphoreType.DMA((AG_SEMS,)), pltpu.SemaphoreType.DMA((AG_SEMS,)),
                        pltpu.SemaphoreType.DMA((2,)), pltpu.SemaphoreType.DMA],
        compiler_params=_params(),
    )(x, gain, wt_sh)


def _gmlp_weights(w_ref):
    ti = lax.broadcasted_iota(jnp.int32, (CHUNK, CHUNK), 0)
    si = lax.broadcasted_iota(jnp.int32, (CHUNK, CHUNK), 1)
    tril = si <= ti
    return tril, [jnp.where(tril, w_ref[h], 0.0).astype(BF16) for h in range(4)]


def _gmlp_fwd(proj, vgain, w_s, bias_full):
    tm = 512

    def body(p_ref, vg_ref, w_ref, b_ref, y_ref):
        bd = _head_blockdiag()
        lo = _lo_mask(CHUNK)
        _, wm = _gmlp_weights(w_ref)
        units = [(pl.ds(c * CHUNK, CHUNK), p) for c in range(tm // CHUNK) for p in range(2)]
        col = lambda c0, p: slice(c0 + p * LANES, c0 + (p + 1) * LANES)
        vs = [p_ref[rows, col(C_GV, p)] for rows, p in units]
        rs = [lax.rsqrt(_headsum(v * v, bd) * (1.0 / HEAD_DIM) + EPS) for v in vs]
        vns = [(v * r * vg_ref[:, col(0, p)]).astype(BF16) for v, r, (_, p) in zip(vs, rs, units)]
        sps = [jnp.where(lo, _dot(wm[2 * p], vn), _dot(wm[2 * p + 1], vn)) + b_ref[:, col(0, p)] for vn, (_, p) in zip(vns, units)]
        for sp, (rows, p) in zip(sps, units):
            gt = p_ref[rows, col(C_GG, p)]
            y_ref[rows, col(0, p)] = (p_ref[rows, col(C_GU, p)] * sp * (gt * _sigmoid(gt))).astype(BF16)

    return _call(
        body, name="gmlp_fwd", grid=(SEQ // tm,),
        in_specs=[pl.BlockSpec((tm, 3 * GMLP_W), lambda i: (i, 0)),
                  pl.BlockSpec((1, GMLP_W), lambda i: (0, 0)),
                  pl.BlockSpec((4, CHUNK, CHUNK), lambda i: (0, 0, 0)),
                  pl.BlockSpec((CHUNK, GMLP_W), lambda i: (0, 0))],
        out_specs=pl.BlockSpec((tm, GMLP_W), lambda i: (i, 0)),
        out_shape=jax.ShapeDtypeStruct((SEQ, GMLP_W), BF16),
        compiler_params=_params(),
    )(proj, vgain, w_s, bias_full)


def _gmlp_bwd(proj, dyc, vgain, w_s, bias_full):
    tm = 512
    nsteps = SEQ // tm

    def body(p_ref, dy_ref, vg_ref, w_ref, b_ref, dg_ref, gw_ref, gb_ref, gv_ref):
        i = pl.program_id(0)
        bd = _head_blockdiag()
        lo = _lo_mask(CHUNK)
        tril, wm = _gmlp_weights(w_ref)
        ri = lax.broadcasted_iota(jnp.int32, (16, LANES), 0)
        li = lax.broadcasted_iota(jnp.int32, (16, LANES), 1)
        head_rows = [jnp.where(((ri == 2 * p) & (li < HEAD_DIM)) | ((ri == 2 * p + 1) & (li >= HEAD_DIM)), 1.0, 0.0).astype(BF16)
                     for p in range(2)]

        @pl.when(i == 0)
        def _():
            gw_ref[...] = jnp.zeros_like(gw_ref)
            gb_ref[...] = jnp.zeros_like(gb_ref)
            gv_ref[...] = jnp.zeros_like(gv_ref)

        units = [(pl.ds(c * CHUNK, CHUNK), p) for c in range(tm // CHUNK) for p in range(2)]
        col = lambda c0, p: slice(c0 + p * LANES, c0 + (p + 1) * LANES)
        vs = [p_ref[rows, col(C_GV, p)] for rows, p in units]
        rs = [lax.rsqrt(_headsum(v * v, bd) * (1.0 / HEAD_DIM) + EPS) for v in vs]
        zs = [v * r for v, r in zip(vs, rs)]
        vns = [(z * vg_ref[:, col(0, p)]).astype(BF16) for z, (_, p) in zip(zs, units)]
        sps = [jnp.where(lo, _dot(wm[2 * p], vn), _dot(wm[2 * p + 1], vn)) + b_ref[:, col(0, p)] for vn, (_, p) in zip(vns, units)]
        dsps = []
        for sp, (rows, p) in zip(sps, units):
            u = p_ref[rows, col(C_GU, p)]
            gt = p_ref[rows, col(C_GG, p)]
            dy = dy_ref[rows, col(0, p)]
            sg = _sigmoid(gt)
            sl = gt * sg
            dg_ref[rows, col(C_GU, p)] = (dy * sp * sl).astype(BF16)
            dg_ref[rows, col(C_GG, p)] = (dy * u * sp * (sg * (1.0 + gt * (1.0 - sg)))).astype(BF16)
            dsps.append(dy * u * sl)
        dspbs = [dsp.astype(BF16) for dsp in dsps]
        dvns = [jnp.where(lo, _dot_tn(wm[2 * p], dspb), _dot_tn(wm[2 * p + 1], dspb)) for dspb, (_, p) in zip(dspbs, units)]
        gws = [(_dot_nt(jnp.where(lo, dsp, 0.0).astype(BF16), vn), _dot_nt(jnp.where(lo, 0.0, dsp).astype(BF16), vn))
               for dsp, vn in zip(dsps, vns)]
        gbs = [(_dot_nt(head_rows[p], dspb) + _dot_nt(head_rows[p], (dsp - dspb.astype(F32)).astype(BF16)))[0:8]
               for dsp, dspb, (_, p) in zip(dsps, dspbs, units)]
        for p in range(2):
            mine = [n for n, (_, q) in enumerate(units) if q == p]
            gw_ref[2 * p] += sum(gws[n][0] for n in mine)
            gw_ref[2 * p + 1] += sum(gws[n][1] for n in mine)
            gvp = sum(jnp.sum(dvns[n] * zs[n], axis=0, keepdims=True) for n in mine)
            gv_ref[2 * p:2 * p + 1, :] += gvp
            gv_ref[2 * p + 1:2 * p + 2, :] += pltpu.roll(gvp, HEAD_DIM, 1)
        gb_ref[...] += sum(gbs)
        for dvn, z, r, (rows, p) in zip(dvns, zs, rs, units):
            dz = dvn * vg_ref[:, col(0, p)]
            dg_ref[rows, col(C_GV, p)] = (r * (dz - z * (_headsum(dz * z, bd) * (1.0 / HEAD_DIM)))).astype(BF16)

        @pl.when(i == nsteps - 1)
        def _():
            for h in range(4):
                gw_ref[h] = jnp.where(tril, gw_ref[h], 0.0)

    return _call(
        body, name="gmlp_bwd", grid=(nsteps,),
        in_specs=[pl.BlockSpec((tm, 3 * GMLP_W), lambda i: (i, 0)),
                  pl.BlockSpec((tm, GMLP_W), lambda i: (i, 0)),
                  pl.BlockSpec((1, GMLP_W), lambda i: (0, 0)),
                  pl.BlockSpec((4, CHUNK, CHUNK), lambda i: (0, 0, 0)),
                  pl.BlockSpec((CHUNK, GMLP_W), lambda i: (0, 0))],
        out_specs=[pl.BlockSpec((tm, 3 * GMLP_W), lambda i: (i, 0)),
                   pl.BlockSpec((4, CHUNK, CHUNK), lambda i: (0, 0, 0)),
                   pl.BlockSpec((8, LANES), lambda i: (0, 0)),
                   pl.BlockSpec((8, LANES), lambda i: (0, 0))],
        out_shape=[jax.ShapeDtypeStruct((SEQ, 3 * GMLP_W), BF16),
                   jax.ShapeDtypeStruct((4, CHUNK, CHUNK), F32),
                   jax.ShapeDtypeStruct((8, LANES), F32),
                   jax.ShapeDtypeStruct((8, LANES), F32)],
        compiler_params=_params(),
    )(proj, dyc, vgain, w_s, bias_full)


def _band_masks():
    qi = lax.broadcasted_iota(jnp.int32, (CHUNK, 2 * CHUNK), 0)
    kj = lax.broadcasted_iota(jnp.int32, (CHUNK, 2 * CHUNK), 1)
    valid2 = ((kj < CHUNK) & (kj >= qi)) | ((kj >= CHUNK) & (kj - CHUNK <= qi))
    q1 = lax.broadcasted_iota(jnp.int32, (CHUNK, CHUNK), 0)
    k1 = lax.broadcasted_iota(jnp.int32, (CHUNK, CHUNK), 1)
    return k1 <= q1, valid2


def _stack_heads(v, lo):
    return jnp.concatenate([jnp.where(lo, v, 0.0), jnp.where(lo, 0.0, v)], axis=0).astype(BF16)


def _rows_of(ref, start, d):
    if d == 1:
        return ref.at[pl.ds(start if isinstance(start, int) else pl.multiple_of(start, CHUNK), CHUNK), :]
    return ref.at[pl.ds(start, CHUNK, stride=d), :]


def _unrolled(lo, hi, unroll, run):
    groups = (hi - lo) // unroll
    if groups:
        def body(g, carry):
            run([lo + g * unroll + t for t in range(unroll)])
            return carry

        lax.fori_loop(0, groups, body, 0)
    if lo + groups * unroll < hi:
        run(range(lo + groups * unroll, hi))


def _for_blocks(d, group_fn, unroll):
    nblk = SEQ // CHUNK
    sh = d.bit_length() - 1

    def first(j):
        return (j * CHUNK if d == 1 else j, None)

    def rest(j):
        start = (j & (d - 1)) + (j >> sh) * (CHUNK * d)
        return (start, start - CHUNK * d)

    _unrolled(0, d, unroll, lambda js: group_fn(d, [first(j) for j in js]))
    _unrolled(d, nblk, unroll, lambda js: group_fn(d, [rest(j) for j in js]))


def _attn_fwd(proj, gq2, gk2, *ride_along):
    tn = 512
    npairs = ATTN_W // LANES
    nride = len(ride_along)

    def body(q_ref, k_ref, v_ref, g_ref, gq_ref, gk_ref, *rest):
        shards, rest = rest[:nride], rest[nride:]
        o_ref, l_ref, ya_ref = rest[:3]
        gathered, rest = rest[3:3 + nride], rest[3 + nride:]
        qn_ref, kn_ref = rest[:2]
        lands, (send_sems, recv_sems, copy_sems) = rest[2:2 + nride], rest[2 + nride:]
        pair = pl.program_id(0)
        ride = _gather_stages(shards, lands, send_sems, recv_sems)[0]
        for step in range(npairs):
            pl.when(pair == step)(ride[step])
        bd = _head_blockdiag()
        lo = _lo_mask(CHUNK)
        valid1, valid2 = _band_masks()

        def norm(t, carry):
            rows = pl.ds(pl.multiple_of(t * tn, tn), tn)
            q, k = q_ref[rows, :], k_ref[rows, :]
            ssq = [_headsum(a * a, bd) for a in (q, k)]
            qn_ref[rows, :] = q * lax.rsqrt(ssq[0] * (1.0 / HEAD_DIM) + EPS) * (gq_ref[...] * QK_SCALE)
            kn_ref[rows, :] = k * lax.rsqrt(ssq[1] * (1.0 / HEAD_DIM) + EPS) * gk_ref[...]
            return carry

        lax.fori_loop(0, SEQ // tn, norm, 0)

        def load_kv(ref, d, start, prev):
            own = _rows_of(ref, start, d)[...]
            if prev is None:
                return own.astype(BF16)
            return jnp.concatenate([_rows_of(ref, prev, d)[...], own], axis=0).astype(BF16)

        def group(d, blocks):
            valid = valid1 if blocks[0][1] is None else valid2
            valid = jnp.concatenate([valid, valid], axis=0)
            qs = [_rows_of(qn_ref, start, d)[...] for start, _ in blocks]
            ks = [load_kv(kn_ref, d, start, prev) for start, prev in blocks]
            vs = [load_kv(v_ref, d, start, prev) for start, prev in blocks]
            ss = [_dot_nt(_stack_heads(q, lo), k) for q, k in zip(qs, ks)]
            ms, ps, ls = [], [], []
            for s in ss:
                s = jnp.where(valid, s, -jnp.inf)
                m = jnp.max(s, axis=-1, keepdims=True)
                p = jnp.exp(s - m)
                ms.append(m)
                ls.append(jnp.sum(p, axis=-1, keepdims=True))
                ps.append(p.astype(BF16))
            os_ = [_dot(p, v) for p, v in zip(ps, vs)]
            for b, (start, _) in enumerate(blocks):
                heads = lambda v: jnp.where(lo, v[:CHUNK], v[CHUNK:])
                lsum = heads(ls[b])
                ob = heads(os_[b]) * (1.0 / lsum)
                lb = heads(ms[b]) + jnp.log(lsum)
                o_rows = _rows_of(o_ref, start, d)
                l_rows = _rows_of(l_ref, start, d)
                if d != DILATIONS[0]:
                    lold = l_rows[...]
                    mx = jnp.maximum(lold, lb)
                    ea = jnp.exp(lold - mx)
                    eb = jnp.exp(lb - mx)
                    inv = 1.0 / (ea + eb)
                    ob = o_rows[...] * (ea * inv) + ob * (eb * inv)
                    lb = mx + jnp.log(ea + eb)
                o_rows[...] = ob
                l_rows[...] = lb

        for d in DILATIONS:
            _for_blocks(d, group, ATTN_UNROLL)

        def fin(t, carry):
            rows = pl.ds(pl.multiple_of(t * tn, tn), tn)
            g = g_ref[rows, :]
            ya_ref[rows, :] = (o_ref[rows, :] * (g * _sigmoid(g))).astype(BF16)
            return carry

        lax.fori_loop(0, SEQ // tn, fin, 0)

        @pl.when(pair == npairs - 1)
        def _():
            to_hbm = [pltpu.make_async_copy(land, out, copy_sems.at[n]) for n, (land, out) in enumerate(zip(lands, gathered))]
            for cp in to_hbm:
                cp.start()
            for cp in to_hbm:
                cp.wait()

    col = lambda c0: pl.BlockSpec((SEQ, LANES), lambda p: (0, c0 // LANES + p))
    vec = pl.BlockSpec((1, LANES), lambda p: (0, 0))
    out = pl.BlockSpec((SEQ, LANES), lambda p: (0, p))
    full = [jax.ShapeDtypeStruct((4 * a.shape[0], a.shape[1]), BF16) for a in ride_along]
    return _call(
        body, name="attn_fwd", grid=(npairs,),
        in_specs=[col(C_AQ), col(C_AK), col(C_AV), col(C_AG), vec, vec]
        + [pl.BlockSpec(a.shape, lambda p: (0, 0)) for a in ride_along],
        out_specs=[out, out, out] + [pl.BlockSpec(memory_space=pl.ANY)] * nride,
        out_shape=[jax.ShapeDtypeStruct((SEQ, ATTN_W), F32), jax.ShapeDtypeStruct((SEQ, ATTN_W), F32),
                   jax.ShapeDtypeStruct((SEQ, ATTN_W), BF16)] + full,
        scratch_shapes=[pltpu.VMEM((SEQ, LANES), F32), pltpu.VMEM((SEQ, LANES), F32)]
        + [pltpu.VMEM(s.shape, BF16) for s in full]
        + [pltpu.SemaphoreType.DMA((AG_SEMS * nride,)), pltpu.SemaphoreType.DMA((AG_SEMS * nride,)),
           pltpu.SemaphoreType.DMA((nride,))],
        compiler_params=_params(),
    )(proj, proj, proj, proj, gq2, gk2, *ride_along)


def _attn_bwd(proj, o, lse, dyc, gq2, gk2, *ride_along):
    tn = 512
    npairs = ATTN_W // LANES
    nride = len(ride_along)
    nbufs = nride * len(RS_KINDS)

    def body(proj_hbm, o_hbm, l_hbm, dyc_hbm, gq_ref, gk_ref, *rest):
        ride_in, rest = rest[:nride], rest[nride:]
        dq_ref, dk_ref, dv_ref, dgt_ref, gqg_ref, gkg_ref = rest[:6]
        ride_out, rest = rest[6:6 + nride], rest[6 + nride:]
        qb_, kb_, vb_, gb_, ob_, lb_, yb_, dkb_, dvb_, sems = rest[:10]
        rs_bufs, (send_sems, recv_sems, local_sems) = rest[10:10 + nbufs], rest[10 + nbufs:]
        rs_stage = _rs_stages(ride_in, ride_out, rs_bufs, send_sems, recv_sems, local_sems, [g.shape[1] for g in ride_along])
        pair = pl.program_id(0)
        for step in range(npairs):
            pl.when(pair == step)(rs_stage[step])
        bd = _head_blockdiag()
        lo = _lo_mask(CHUNK)
        lo2 = lax.broadcasted_iota(jnp.int32, (2 * CHUNK, LANES), 1) < HEAD_DIM
        valid1, valid2 = _band_masks()
        gqs = gq_ref[...] * QK_SCALE
        gk = gk_ref[...]

        def pcol(c0, of=None):
            return acol(proj_hbm, c0, of)

        def acol(hbm, c0=0, of=None):
            of = pair if of is None else of
            return hbm.at[:, pl.ds(pl.multiple_of(c0 + of * LANES, LANES), LANES)]

        def input_loads(of):
            return [pltpu.make_async_copy(src, dst, sems.at[n]) for n, (src, dst) in enumerate((
                (pcol(C_AQ, of), qb_), (pcol(C_AK, of), kb_), (pcol(C_AG, of), gb_), (acol(o_hbm, 0, of), ob_),
                (acol(dyc_hbm, GMLP_W, of), yb_), (pcol(C_AV, of), vb_), (acol(l_hbm, 0, of), lb_)))]

        early = (0, 1, 3, 4)
        loads = input_loads(pair)
        for n, cp in enumerate(loads):
            if n in early:
                pl.when(pair == 0)(cp.start)
            else:
                cp.start()

        @pl.when(pair == 0)
        def _():
            gqg_ref[...] = jnp.zeros_like(gqg_ref)
            gkg_ref[...] = jnp.zeros_like(gkg_ref)

        def pre_qk(t, carry):
            rows = pl.ds(pl.multiple_of(t * tn, tn), tn)
            q, k = qb_[rows, :], kb_[rows, :]
            ssq = [_headsum(a * a, bd) for a in (q, k)]
            qb_[rows, :] = q * lax.rsqrt(ssq[0] * (1.0 / HEAD_DIM) + EPS) * gqs
            kb_[rows, :] = k * lax.rsqrt(ssq[1] * (1.0 / HEAD_DIM) + EPS) * gk
            return carry

        def pre_gate(t, carry):
            rows = pl.ds(pl.multiple_of(t * tn, tn), tn)
            g = gb_[rows, :]
            ov = ob_[rows, :]
            dya = yb_[rows, :]
            sg = _sigmoid(g)
            dgt_ref[rows, :] = (dya * ov * (sg * (1.0 + g * (1.0 - sg)))).astype(BF16)
            do = dya * (g * sg)
            yb_[rows, :] = do
            ob_[rows, :] = jnp.where(first_half, lb_[rows, :], _headsum(do * ov, bd))
            return carry

        first_half = (lax.broadcasted_iota(jnp.int32, (tn, LANES), 1) & (HEAD_DIM - 1)) < HEAD_DIM // 2
        loads[0].wait()
        loads[1].wait()
        lax.fori_loop(0, SEQ // tn, pre_qk, 0)
        for cp in loads[2:5] + loads[6:7]:
            cp.wait()
        lax.fori_loop(0, SEQ // tn, pre_gate, 0)
        loads[5].wait()
        reloads = [pltpu.make_async_copy(pcol(C_AQ), lb_, sems.at[7]), pltpu.make_async_copy(pcol(C_AK), vb_, sems.at[8])]
        reloads[0].start()

        def load_kv(ref, d, start, prev):
            own = _rows_of(ref, start, d)[...]
            if prev is None:
                return own.astype(BF16)
            return jnp.concatenate([_rows_of(ref, prev, d)[...], own], axis=0).astype(BF16)

        def group(d, blocks):
            first = blocks[0][1] is None
            valid, lok = (valid1, lo) if first else (valid2, lo2)
            chains = [(b, h) for b in range(len(blocks)) for h in range(2)]
            mask = lambda h: lo if h == 0 else ~lo
            qs = [_rows_of(qb_, start, d)[...] for start, _ in blocks]
            dos = [_rows_of(yb_, start, d)[...] for start, _ in blocks]
            lds = [_rows_of(ob_, start, d)[...] for start, _ in blocks]
            ks = [load_kv(kb_, d, start, prev) for start, prev in blocks]
            vs = [load_kv(vb_, d, start, prev) for start, prev in blocks]
            qbs = [q.astype(BF16) for q in qs]
            dobs = [do.astype(BF16) for do in dos]
            ss = [_dot_nt(jnp.where(mask(h), qs[b], 0.0).astype(BF16), ks[b]) for b, h in chains]
            dps = [_dot_nt(jnp.where(mask(h), dos[b], 0.0).astype(BF16), vs[b]) for b, h in chains]
            pbs, dss = [], []
            for s, dp, (b, h) in zip(ss, dps, chains):
                hc, dc = h * HEAD_DIM, h * HEAD_DIM + HEAD_DIM // 2
                p = jnp.exp(jnp.where(valid, s, -jnp.inf) - lds[b][:, hc:hc + 1])
                pbs.append(p.astype(BF16))
                dss.append((p * (dp - lds[b][:, dc:dc + 1])).astype(BF16))
            dqs = [_dot(ds, ks[b]) for ds, (b, h) in zip(dss, chains)]
            dks = [_dot_tn(ds, qbs[b]) for ds, (b, h) in zip(dss, chains)]
            dvs = [_dot_tn(p, dobs[b]) for p, (b, h) in zip(pbs, chains)]
            assign = d == DILATIONS[0]
            for b, (start, prev) in enumerate(blocks):
                c0, c1 = 2 * b, 2 * b + 1
                dq_rows = _rows_of(gb_, start, d)
                dqb = jnp.where(lo, dqs[c0], dqs[c1])
                dq_rows[...] = dqb if assign else dq_rows[...] + dqb
                dkc = jnp.where(lok, dks[c0], dks[c1])
                dvc = jnp.where(lok, dvs[c0], dvs[c1])
                spans = ((start, slice(0, CHUNK), True),) if first else (
                    (prev, slice(0, CHUNK), False), (start, slice(CHUNK, 2 * CHUNK), True))
                for st, sl, own in spans:
                    dk_rows = _rows_of(dkb_, st, d)
                    dv_rows = _rows_of(dvb_, st, d)
                    if assign and own:
                        dk_rows[...] = dkc[sl]
                        dv_rows[...] = dvc[sl]
                    else:
                        dk_rows[...] = dk_rows[...] + dkc[sl]
                        dv_rows[...] = dv_rows[...] + dvc[sl]

        for d in DILATIONS:
            _for_blocks(d, group, ATTN_UNROLL)

        reloads[1].start()

        @pl.when(pair < npairs - 1)
        def _():
            nxt = input_loads(pair + 1)
            for n in early:
                nxt[n].start()

        for cp in reloads:
            cp.wait()

        def post(t, carry):
            gq_acc, gk_acc = carry
            rows = pl.ds(pl.multiple_of(t * tn, tn), tn)
            raws = [lb_[rows, :], vb_[rows, :]]
            dns = [gb_[rows, :], dkb_[rows, :]]
            rs = [lax.rsqrt(_headsum(a * a, bd) * (1.0 / HEAD_DIM) + EPS) for a in raws]
            zs = [a * r for a, r in zip(raws, rs)]
            dzs = [dn * gain for dn, gain in zip(dns, (gqs, gk))]
            means = [_headsum(dz * z, bd) * (1.0 / HEAD_DIM) for dz, z in zip(dzs, zs)]
            dq, dk = [r * (dz - z * mean) for r, dz, z, mean in zip(rs, dzs, zs, means)]
            gq, gkk = [jnp.sum(dn * z, axis=0, keepdims=True) for dn, z in zip(dns, zs)]
            dq_ref[rows, :] = dq.astype(BF16)
            dk_ref[rows, :] = dk.astype(BF16)
            dv_ref[rows, :] = dvb_[rows, :].astype(BF16)
            return gq_acc + gq * QK_SCALE, gk_acc + gkk

        zero = jnp.zeros((1, LANES), F32)
        gq_acc, gk_acc = lax.fori_loop(0, SEQ // tn, post, (zero, zero))
        gqg_ref[0:1, :] += gq_acc
        gkg_ref[0:1, :] += gk_acc

        @pl.when(pair == npairs - 1)
        def _():
            gqg_ref[0:1, :] = _fold_heads(gqg_ref[0:1, :])
            gkg_ref[0:1, :] = _fold_heads(gkg_ref[0:1, :])
            rs_stage[npairs]()

    hbm = pl.BlockSpec(memory_space=pl.ANY)
    vec = pl.BlockSpec((1, LANES), lambda p: (0, 0))
    blk8 = pl.BlockSpec((8, LANES), lambda p: (0, 0))
    out = pl.BlockSpec((SEQ, LANES), lambda p: (0, p))
    big = jax.ShapeDtypeStruct((SEQ, ATTN_W), BF16)
    nsem = RS_SEMS * nride
    return _call(
        body, name="attn_bwd", grid=(npairs,),
        in_specs=[hbm, hbm, hbm, hbm, vec, vec] + [hbm] * nride,
        out_specs=[out, out, out, out, blk8, blk8] + [hbm] * nride,
        out_shape=[big, big, big, big, jax.ShapeDtypeStruct((8, LANES), F32), jax.ShapeDtypeStruct((8, LANES), F32)]
        + [jax.ShapeDtypeStruct((2, g.shape[0] // 8, g.shape[1]), F32) for g in ride_along],
        scratch_shapes=[pltpu.VMEM((SEQ, LANES), F32) for _ in range(9)] + [pltpu.SemaphoreType.DMA((9,))]
        + _rs_scratch([g.shape for g in ride_along]) + [pltpu.SemaphoreType.DMA((nsem,)), pltpu.SemaphoreType.DMA((nsem,)),
                                     pltpu.SemaphoreType.DMA((nride,))],
        compiler_params=_params(),
    )(proj, o, lse, dyc, gq2, gk2, *[_rs_view(g) for g in ride_along])


def _mem_kv(mem, gain, wkv):
    def body(m_ref, g_ref, w_ref, kv_ref, hm_ref):
        mv = m_ref[...]
        ms = jnp.mean(mv * mv, axis=-1, keepdims=True)
        hm = (mv * lax.rsqrt(ms + EPS) * g_ref[...]).astype(BF16)
        hm_ref[...] = hm
        kv_ref[...] = _dot(hm, w_ref[...])

    return _call(
        body, name="mem_kv",
        out_shape=[jax.ShapeDtypeStruct((MEM_LEN, 2 * MEM_W), F32), jax.ShapeDtypeStruct((MEM_LEN, D_MODEL), BF16)],
        compiler_params=_params(),
    )(mem, gain, wkv)


def _mem_keys(kv_ref, kg_ref, bd, p):
    mk = kv_ref[:, p * LANES:(p + 1) * LANES]
    r = lax.rsqrt(_headsum(mk * mk, bd) * (1.0 / HEAD_DIM) + EPS)
    z = mk * r
    mkn = (z * kg_ref[:, p * LANES:(p + 1) * LANES]).astype(BF16)
    mvp = kv_ref[:, MEM_W + p * LANES:MEM_W + (p + 1) * LANES].astype(BF16)
    return mkn, mvp, r, z


def _mem_fwd(proj, kv, qg4, kg4):
    tm = 512

    def body(q_ref, g_ref, kv_ref, qg_ref, kg_ref, om_ref, ym_ref):
        bd = _head_blockdiag()
        lo = _lo_mask(tm)
        keys, qns = [], []
        for p in range(2):
            cs = slice(p * LANES, (p + 1) * LANES)
            keys.append(_mem_keys(kv_ref, kg_ref, bd, p)[:2])
            q = q_ref[:, cs]
            qns.append(q * lax.rsqrt(_headsum(q * q, bd) * (1.0 / HEAD_DIM) + EPS) * (qg_ref[:, cs] * QK_SCALE))
        chains = [(p, h) for p in range(2) for h in range(2)]
        ss = [_dot_nt(jnp.where(lo if h == 0 else ~lo, qns[p], 0.0).astype(BF16), keys[p][0]) for p, h in chains]
        es = [jnp.exp(s - jnp.max(s, axis=-1, keepdims=True)) for s in ss]
        os_ = [_dot(e.astype(BF16), keys[p][1]) for e, (p, h) in zip(es, chains)]
        res = [o * (1.0 / jnp.sum(e, axis=-1, keepdims=True)) for o, e in zip(os_, es)]
        for p in range(2):
            cs = slice(p * LANES, (p + 1) * LANES)
            ov = jnp.where(lo, res[2 * p], res[2 * p + 1])
            g = g_ref[:, cs]
            om_ref[:, cs] = ov
            ym_ref[:, cs] = (ov * (g * _sigmoid(g))).astype(BF16)

    vec = pl.BlockSpec((1, MEM_W), lambda i: (0, 0))
    return _call(
        body, name="mem_fwd", grid=(SEQ // tm,),
        in_specs=[pl.BlockSpec((tm, MEM_W), lambda i: (i, C_MQ // MEM_W)),
                  pl.BlockSpec((tm, MEM_W), lambda i: (i, C_MG // MEM_W)),
                  pl.BlockSpec((MEM_LEN, 2 * MEM_W), lambda i: (0, 0)), vec, vec],
        out_specs=[pl.BlockSpec((tm, MEM_W), lambda i: (i, 0)), pl.BlockSpec((tm, MEM_W), lambda i: (i, 0))],
        out_shape=[jax.ShapeDtypeStruct((SEQ, MEM_W), F32), jax.ShapeDtypeStruct((SEQ, MEM_W), BF16)],
        compiler_params=_params(),
    )(proj, proj, kv, qg4, kg4)


def _mem_bwd(proj, om, dyc, kv, hm, mem, mgain, wkv, qg4, kg4):
    tm = 512
    nsteps = SEQ // tm

    def body(q_ref, g_ref, om_ref, dy_ref, kv_ref, hm_ref, mem_ref, mg_ref, w_ref, qg_ref, kg_ref,
             dq_ref, dgt_ref, gqg_ref, gkg_ref, gw_ref, gmg_ref, dmk_ref, dmv_ref, gq_acc):
        i = pl.program_id(0)
        bd = _head_blockdiag()
        lo = _lo_mask(tm)
        lom = _lo_mask(MEM_LEN)

        @pl.when(i == 0)
        def _():
            dmk_ref[...] = jnp.zeros_like(dmk_ref)
            dmv_ref[...] = jnp.zeros_like(dmv_ref)
            gq_acc[...] = jnp.zeros_like(gq_acc)

        pairs = []
        for p in range(2):
            cs = slice(p * LANES, (p + 1) * LANES)
            mkn, mvp, _, _ = _mem_keys(kv_ref, kg_ref, bd, p)
            gqs = qg_ref[:, cs] * QK_SCALE
            q = q_ref[:, cs]
            r = lax.rsqrt(_headsum(q * q, bd) * (1.0 / HEAD_DIM) + EPS)
            z = q * r
            qn = z * gqs
            g = g_ref[:, cs]
            ov = om_ref[:, cs]
            dym = dy_ref[:, cs]
            sg = _sigmoid(g)
            dgt_ref[:, cs] = (dym * ov * (sg * (1.0 + g * (1.0 - sg)))).astype(BF16)
            do = dym * (g * sg)
            pairs.append(dict(cs=cs, mkn=mkn, mvp=mvp, gqs=gqs, r=r, z=z, qn=qn, qnb=qn.astype(BF16), do=do,
                              dob=do.astype(BF16), delta=_headsum(do * ov, bd)))
        chains = [(pr_, h) for pr_ in pairs for h in range(2)]
        mask = lambda h: lo if h == 0 else ~lo
        ss = [_dot_nt(jnp.where(mask(h), c["qn"], 0.0).astype(BF16), c["mkn"]) for c, h in chains]
        dps = [_dot_nt(jnp.where(mask(h), c["do"], 0.0).astype(BF16), c["mvp"]) for c, h in chains]
        prs, dss = [], []
        for s, dp, (c, h) in zip(ss, dps, chains):
            e = jnp.exp(s - jnp.max(s, axis=-1, keepdims=True))
            pr = e * (1.0 / jnp.sum(e, axis=-1, keepdims=True))
            prs.append(pr.astype(BF16))
            dss.append((pr * (dp - c["delta"][:, h * HEAD_DIM:h * HEAD_DIM + 1])).astype(BF16))
        dqs = [_dot(ds, c["mkn"]) for ds, (c, h) in zip(dss, chains)]
        dks = [_dot_tn(ds, c["qnb"]) for ds, (c, h) in zip(dss, chains)]
        dvs = [_dot_tn(pr, c["dob"]) for pr, (c, h) in zip(prs, chains)]
        for p, c in enumerate(pairs):
            cs, z, r = c["cs"], c["z"], c["r"]
            dqn = jnp.where(lo, dqs[2 * p], dqs[2 * p + 1])
            dmk_ref[:, cs] += jnp.where(lom, dks[2 * p], dks[2 * p + 1])
            dmv_ref[:, cs] += jnp.where(lom, dvs[2 * p], dvs[2 * p + 1])
            dz = dqn * c["gqs"]
            dq_ref[:, cs] = (r * (dz - z * (_headsum(dz * z, bd) * (1.0 / HEAD_DIM)))).astype(BF16)
            gq_acc[:, cs] += jnp.sum(dqn * z, axis=0, keepdims=True) * QK_SCALE

        @pl.when(i == nsteps - 1)
        def _():
            gqg_ref[...] = jnp.zeros_like(gqg_ref)
            gkg_ref[...] = jnp.zeros_like(gkg_ref)
            gqg_ref[0:1, :] = _fold_heads(gq_acc[:, 0:LANES] + gq_acc[:, LANES:2 * LANES])
            dkv = []
            gk = jnp.zeros((1, LANES), F32)
            for p in range(2):
                cs = slice(p * LANES, (p + 1) * LANES)
                _, _, r, z = _mem_keys(kv_ref, kg_ref, bd, p)
                dn = dmk_ref[:, cs]
                dz = dn * kg_ref[:, cs]
                gk = gk + jnp.sum(dn * z, axis=0, keepdims=True)
                dkv.append(r * (dz - z * (_headsum(dz * z, bd) * (1.0 / HEAD_DIM))))
            gkg_ref[0:1, :] = _fold_heads(gk)
            dkvb = jnp.concatenate(dkv + [dmv_ref[...]], axis=1).astype(BF16)
            gw_ref[...] = _dot_tn(hm_ref[...], dkvb)
            dhm = _dot_nt(dkvb, w_ref[...])
            mv = mem_ref[...]
            zm = mv * lax.rsqrt(jnp.mean(mv * mv, axis=-1, keepdims=True) + EPS)
            _put_rows(gmg_ref, jnp.sum(dhm * zm, axis=0, keepdims=True))

    const = lambda shape: pl.BlockSpec(shape, lambda i: (0,) * len(shape))
    row = lambda j: pl.BlockSpec((tm, MEM_W), lambda i: (i, j))
    blk8 = jax.ShapeDtypeStruct((8, LANES), F32)
    return _call(
        body, name="mem_bwd", grid=(nsteps,),
        in_specs=[row(C_MQ // MEM_W), row(C_MG // MEM_W), row(0), row((GMLP_W + ATTN_W) // MEM_W),
                  const((MEM_LEN, 2 * MEM_W)), const((MEM_LEN, D_MODEL)), const((MEM_LEN, D_MODEL)),
                  const((1, D_MODEL)), const((D_MODEL, 2 * MEM_W)), const((1, MEM_W)), const((1, MEM_W))],
        out_specs=[row(0), row(0), const((8, LANES)), const((8, LANES)),
                   const((D_MODEL, 2 * MEM_W)), const((8, LANES))],
        out_shape=[jax.ShapeDtypeStruct((SEQ, MEM_W), BF16), jax.ShapeDtypeStruct((SEQ, MEM_W), BF16),
                   blk8, blk8, jax.ShapeDtypeStruct((D_MODEL, 2 * MEM_W), F32), blk8],
        scratch_shapes=[pltpu.VMEM((MEM_LEN, MEM_W), F32), pltpu.VMEM((MEM_LEN, MEM_W), F32),
                        pltpu.VMEM((1, MEM_W), F32)],
        compiler_params=_params(),
    )(proj, proj, om, dyc, kv, hm, mem, mgain, wkv, qg4, kg4)


def _out_loss(yg, ya, ym, x, tgt, wo):
    tm = 512
    nsteps = SEQ // tm
    parts = ((0, GMLP_W), (GMLP_W, ATTN_W), (GMLP_W + ATTN_W, MEM_W))

    def body(yg_ref, ya_ref, ym_ref, x_ref, t_ref, w_ref, dy_ref, dyc_ref, gw_ref, ls_ref):
        i = pl.program_id(0)

        @pl.when(i == 0)
        def _():
            gw_ref[...] = jnp.zeros_like(gw_ref)
            ls_ref[...] = jnp.zeros_like(ls_ref)

        ys = (yg_ref[...], ya_ref[...], ym_ref[...])
        y = sum(_dot(yv, w_ref[r0:r0 + n, :]) for yv, (r0, n) in zip(ys, parts))
        err = x_ref[...] + y - t_ref[...]
        _put_rows(ls_ref, jnp.sum(err * err, axis=0, keepdims=True), accumulate=True)
        dy = err * (1.0 / D_MODEL)
        dy_ref[...] = dy
        dyb = dy.astype(BF16)
        dyc_ref[...] = _dot_nt(dyb, w_ref[...])
        for yv, (r0, n) in zip(ys, parts):
            gw_ref[r0:r0 + n, :] += _dot_tn(yv, dyb)

    row = lambda w: pl.BlockSpec((tm, w), lambda i: (i, 0))
    const = lambda shape: pl.BlockSpec(shape, lambda i: (0, 0))
    return _call(
        body, name="out_loss", grid=(nsteps,),
        in_specs=[row(GMLP_W), row(ATTN_W), row(MEM_W), row(D_MODEL), row(D_MODEL), const((D_MODEL, D_MODEL))],
        out_specs=[row(D_MODEL), row(D_MODEL), const((D_MODEL, D_MODEL)), const((8, LANES))],
        out_shape=[jax.ShapeDtypeStruct((SEQ, D_MODEL), F32), jax.ShapeDtypeStruct((SEQ, D_MODEL), F32),
                   jax.ShapeDtypeStruct((D_MODEL, D_MODEL), F32), jax.ShapeDtypeStruct((8, LANES), F32)],
        compiler_params=_params(),
    )(yg, ya, ym, x, tgt, wo)


def _proj_bwd(x, dy, gain, wt, dg, daq, dak, dav, dag, dmq, dmg):
    tm = 512
    nsteps = SEQ // tm
    pieces = ((C_GU, 3 * GMLP_W), (C_AQ, ATTN_W), (C_AK, ATTN_W), (C_AV, ATTN_W), (C_AG, ATTN_W),
              (C_MQ, MEM_W), (C_MG, MEM_W))

    def body(x_ref, dy_ref, g_ref, wt_hbm, p0, p1, p2, p3, p4, p5, p6, gx_ref, gwt_hbm, gg_ref, wt_v, acc, wt_sem, out_sems):
        i = pl.program_id(0)
        wt_load = pltpu.make_async_copy(wt_hbm, wt_v, wt_sem)

        @pl.when(i == 0)
        def _():
            wt_load.start()
            acc[...] = jnp.zeros_like(acc)
            gg_ref[...] = jnp.zeros_like(gg_ref)

        xv = x_ref[...]
        r = lax.rsqrt(jnp.mean(xv * xv, axis=-1, keepdims=True) + EPS)
        z = xv * r
        g = g_ref[...]
        h = (z * g).astype(BF16)
        pl.when(i == 0)(wt_load.wait)
        flush = [pltpu.make_async_copy(acc.at[c0:c0 + w, :], gwt_hbm.at[c0:c0 + w, :], out_sems.at[n])
                 for n, (c0, w) in enumerate(pieces)]
        dh = jnp.zeros((tm, D_MODEL), F32)
        for n, (pref, (c0, w)) in enumerate(zip((p0, p1, p2, p3, p4, p5, p6), pieces)):
            dp = pref[...]
            dh = dh + _dot(dp, wt_v[c0:c0 + w, :])
            acc[c0:c0 + w, :] += _dot_tn(dp, h)
            pl.when(i == nsteps - 1)(flush[n].start)
        _put_rows(gg_ref, jnp.sum(dh * z, axis=0, keepdims=True), accumulate=True)
        dz = dh * g
        gx_ref[...] = dy_ref[...] + r * (dz - z * jnp.mean(dz * z, axis=-1, keepdims=True))

        @pl.when(i == nsteps - 1)
        def _():
            for cp in flush:
                cp.wait()

    row = lambda w: pl.BlockSpec((tm, w), lambda i: (i, 0))
    hbm = pl.BlockSpec(memory_space=pl.ANY)
    vec = pl.BlockSpec((1, D_MODEL), lambda i: (0, 0))
    return _call(
        body, name="proj_bwd", grid=(nsteps,),
        in_specs=[row(D_MODEL), row(D_MODEL), vec, hbm] + [row(w) for _, w in pieces],
        out_specs=[row(D_MODEL), hbm, pl.BlockSpec((8, LANES), lambda i: (0, 0))],
        out_shape=[jax.ShapeDtypeStruct((SEQ, D_MODEL), F32), jax.ShapeDtypeStruct((IN_W, D_MODEL), F32),
                   jax.ShapeDtypeStruct((8, LANES), F32)],
        scratch_shapes=[pltpu.VMEM((IN_W, D_MODEL), BF16), pltpu.VMEM((IN_W, D_MODEL), F32), pltpu.SemaphoreType.DMA,
                        pltpu.SemaphoreType.DMA((len(pieces),))],
        compiler_params=_params(),
    )(x, dy, gain, wt, dg, daq, dak, dav, dag, dmq, dmg)


AG_SEMS = 8


def _gather_stages(ins, lands, send_sems, recv_sems):
    n = len(ins)
    nrows = [a.shape[0] for a in ins]
    x, y, c = lax.axis_index("x"), lax.axis_index("y"), lax.axis_index("c")
    sib, xn, yn = (x, y, 1 - c), (1 - x, y, c), (x, 1 - y, c)
    me, cx, cy, cd = 2 * x + y, 2 * (1 - x) + y, 2 * x + (1 - y), 2 * (1 - x) + (1 - y)

    def part(a, chip, hf, quarter=None):
        rows = nrows[a] // 2
        base = chip * nrows[a] + hf * rows
        if quarter is not None:
            rows = rows // 2
            base = base + quarter * rows
        return lands[a].at[pl.ds(pl.multiple_of(base, 16), rows), :]

    def copy(a, j, ref, to):
        k = AG_SEMS * a + j
        return pltpu.make_async_remote_copy(src_ref=ref, dst_ref=ref, send_sem=send_sems.at[k],
                                            recv_sem=recv_sems.at[k], device_id=to, device_id_type=MESH)

    def own(a):
        return [copy(a, 0, part(a, me, c), xn), copy(a, 1, part(a, me, c), yn)]

    def neighbours(a):
        return [copy(a, 4, part(a, cx, c, 1), yn), copy(a, 2, part(a, cx, c), sib),
                copy(a, 5, part(a, cy, c, 0), xn), copy(a, 3, part(a, cy, c), sib)]

    def diagonal(a):
        return [copy(a, 7, part(a, cd, c, 1), sib), copy(a, 6, part(a, cd, c, 0), sib)]

    def send_own():
        for a in range(n):
            lands[a][pl.ds(pl.multiple_of(me * nrows[a], 16), nrows[a]), :] = ins[a][...].astype(BF16)
            for cp in own(a):
                cp.start()

    def pass_on_neighbours():
        for a in range(n):
            copy(a, 0, part(a, cx, c), xn).wait_recv()
            copy(a, 1, part(a, cy, c), yn).wait_recv()
            for cp in neighbours(a):
                cp.start()

    def pass_on_diagonal():
        for a in range(n):
            copy(a, 4, part(a, cd, c, 1), yn).wait_recv()
            copy(a, 5, part(a, cd, c, 0), xn).wait_recv()
            for cp in diagonal(a):
                cp.start()

    def y_complete():
        for a in range(n):
            copy(a, 3, part(a, cy, 1 - c), sib).wait_recv()

    def x_complete():
        for a in range(n):
            copy(a, 2, part(a, cx, 1 - c), sib).wait_recv()

    def diagonal_complete():
        for a in range(n):
            copy(a, 6, part(a, cd, 1 - c, 0), sib).wait_recv()
            copy(a, 7, part(a, cd, 1 - c, 1), sib).wait_recv()

    def sends_done():
        for a in range(n):
            for cp in own(a) + neighbours(a) + diagonal(a):
                cp.wait_send()

    def finish():
        y_complete()
        x_complete()
        diagonal_complete()
        sends_done()

    return (send_own, pass_on_neighbours, pass_on_diagonal, finish), (y_complete, x_complete, diagonal_complete, sends_done)


RS_SEMS = 6
RS_KINDS = (((2, 2), 1, F32), ((2, 2), 1, F32), ((2, 2), 2, BF16), ((2, 2), 2, BF16), ((2, 2), 2, F32),
            ((2,), 2, BF16), ((2,), 2, BF16), ((2,), 1, F32))


def _rs_view(g):
    return g.reshape(2, 2, 2, g.shape[0] // 8, g.shape[1])


def _rs_scratch(shapes):
    return [pltpu.VMEM(lead + (r // 8, w // split), dt) for lead, split, dt in RS_KINDS for r, w in shapes]


def _rs_stages(gs, outs, bufs, send_sems, recv_sems, local_sems, widths):
    n = len(gs)
    loc, ra, s_b, r_b, acc1, s_c, r_c, fin = (bufs[n * i:n * i + n] for i in range(len(RS_KINDS)))
    half_w = [w // 2 for w in widths]
    x, y, c = lax.axis_index("x"), lax.axis_index("y"), lax.axis_index("c")
    sib, xn, yn = (x, y, 1 - c), (1 - x, y, c), (x, 1 - y, c)

    def copy(a, j, src, dst, to):
        k = RS_SEMS * a + j
        return pltpu.make_async_remote_copy(src_ref=src, dst_ref=dst, send_sem=send_sems.at[k],
                                            recv_sem=recv_sems.at[k], device_id=to, device_id_type=MESH)

    def step_a(a):
        return [copy(a, 0, gs[a].at[:, :, 1 - c], ra[a], sib),
                pltpu.make_async_copy(gs[a].at[:, :, c], loc[a], local_sems.at[a])]

    def step_b(a):
        return copy(a, 1, s_b[a].at[0], r_b[a].at[0], xn), copy(a, 2, s_b[a].at[1], r_b[a].at[1], yn)

    def step_c(a):
        return copy(a, 3, s_c[a].at[0], r_c[a].at[0], yn), copy(a, 4, s_c[a].at[1], r_c[a].at[1], xn)

    def step_d(a, half):
        rows = fin[a].at[half]
        return copy(a, 5, rows, rows, sib)

    def start():
        for a in range(n):
            for cp in step_a(a):
                cp.start()

    def a_to_b():
        for a in range(n):
            for cp in step_a(a):
                cp.wait()
            ra[a][...] = loc[a][...] + ra[a][...]
            s_b[a][0] = ra[a][1 - x, :, :, :half_w[a]].astype(BF16)
            s_b[a][1] = ra[a][:, 1 - y, :, half_w[a]:].astype(BF16)
            for cp in step_b(a):
                cp.start()

    def b_to_c():
        for a in range(n):
            for cp in step_b(a):
                cp.wait()
            acc1[a][0] = ra[a][x, :, :, :half_w[a]] + r_b[a][0].astype(F32)
            acc1[a][1] = ra[a][:, y, :, half_w[a]:] + r_b[a][1].astype(F32)
            s_c[a][0] = acc1[a][0, 1 - y].astype(BF16)
            s_c[a][1] = acc1[a][1, 1 - x].astype(BF16)
            for cp in step_c(a):
                cp.start()

    def c_to_d():
        for a in range(n):
            for cp in step_c(a):
                cp.wait()
            fin[a][c, :, :half_w[a]] = acc1[a][0, y] + r_c[a][0].astype(F32)
            fin[a][c, :, half_w[a]:] = acc1[a][1, x] + r_c[a][1].astype(F32)
            step_d(a, c).start()

    def finish():
        to_hbm = [pltpu.make_async_copy(fin[a], outs[a], local_sems.at[a]) for a in range(n)]
        for a in range(n):
            step_d(a, 1 - c).wait_recv()
            step_d(a, c).wait_send()
            to_hbm[a].start()
        for cp in to_hbm:
            cp.wait()

    return start, a_to_b, b_to_c, c_to_d, finish


def _reduce_grads(gwt, g_ws, tiny):
    cw = gwt.shape[1] // RS_CHUNKS
    chunk_shape = (gwt.shape[0], cw)

    def body(g0, ws_in, tiny_in, *rest):
        outs, o_ws, o_tiny = rest[:RS_CHUNKS], rest[RS_CHUNKS], rest[RS_CHUNKS + 1]
        rest = rest[RS_CHUNKS + 2:]
        nb = len(RS_KINDS) * RS_CHUNKS
        sm, sa, sb, sc, acc_s, send_sems, recv_sems, local_sems = rest[nb:]
        blocks = [g0.at[:, :, :, :, pl.ds(j * cw, cw)] for j in range(RS_CHUNKS)]
        start, a_to_b, b_to_c, c_to_d, finish = _rs_stages(blocks, outs, rest[:nb], send_sems, recv_sems, local_sems,
                                                           [cw] * RS_CHUNKS)
        n_ws = ws_in.shape[0]
        sm[0:n_ws, :] = ws_in[...]
        sm[n_ws:, :] = tiny_in[...]
        x, y, c = lax.axis_index("x"), lax.axis_index("y"), lax.axis_index("c")

        def small(j, src, dst, to):
            k = RS_SEMS * RS_CHUNKS + j
            return pltpu.make_async_remote_copy(src_ref=src, dst_ref=dst, send_sem=send_sems.at[k],
                                                recv_sem=recv_sems.at[k], device_id=to, device_id_type=MESH)

        along_c, along_x, along_y = (small(0, sm, sa, (x, y, 1 - c)), small(1, acc_s, sb, (1 - x, y, c)),
                                     small(2, sb, sc, (x, 1 - y, c)))
        start()
        along_c.start()
        a_to_b()
        along_c.wait()
        acc_s[...] = sm[...] + sa[...]
        along_x.start()
        b_to_c()
        along_x.wait()
        sb[...] = acc_s[...] + sb[...]
        along_y.start()
        c_to_d()
        along_y.wait()
        o_ws[...] = sb[0:n_ws, :] + sc[0:n_ws, :]
        o_tiny[...] = sb[n_ws:, :] + sc[n_ws:, :]
        finish()

    vm = pl.BlockSpec(memory_space=pltpu.VMEM)
    hbm = pl.BlockSpec(memory_space=pl.ANY)
    small_shape = (g_ws.shape[0] + tiny.shape[0], LANES)
    scratch = _rs_scratch([chunk_shape] * RS_CHUNKS) + [pltpu.VMEM(small_shape, F32) for _ in range(5)]
    nsem = RS_SEMS * RS_CHUNKS + 3
    scratch += [pltpu.SemaphoreType.DMA((nsem,)), pltpu.SemaphoreType.DMA((nsem,)), pltpu.SemaphoreType.DMA((RS_CHUNKS,))]
    return _call(
        body, name="reduce_grads",
        out_shape=[jax.ShapeDtypeStruct((2, gwt.shape[0] // 8, cw), F32)] * RS_CHUNKS
        + [jax.ShapeDtypeStruct(g_ws.shape, F32), jax.ShapeDtypeStruct(tiny.shape, F32)],
        in_specs=[hbm, vm, vm],
        out_specs=[hbm] * RS_CHUNKS + [vm, vm],
        scratch_shapes=scratch,
        compiler_params=_params(),
    )(_rs_view(gwt), g_ws, tiny)


def _adam_update(w, g, m, v):
    nm = ADAM_B1 * m + (1.0 - ADAM_B1) * g
    nv = ADAM_B2 * v + (1.0 - ADAM_B2) * (g * g)
    m_hat = nm / (1.0 - ADAM_B1 ** ADAM_STEP)
    v_hat = nv / (1.0 - ADAM_B2 ** ADAM_STEP)
    return -ADAM_LR * (m_hat / (jnp.sqrt(v_hat) + ADAM_EPS) + ADAM_WD * w), nm, nv


def _adamw(w, g, m, v):
    rows, cols = w.shape
    tm = max(t for t in range(8, 257, 8) if rows % t == 0)
    parts = tuple(g) if isinstance(g, (tuple, list)) else (g,)
    n = len(parts)

    def body(w_ref, m_ref, v_ref, *refs):
        gv = jnp.concatenate([r[...] for r in refs[:n]], axis=1)
        d_ref, nm_ref, nv_ref = refs[n:n + 3]
        d_ref[...], nm_ref[...], nv_ref[...] = _adam_update(w_ref[...], gv, m_ref[...], v_ref[...])
        if n > 1:
            refs[n + 3][...] = gv

    blk = pl.BlockSpec((tm, cols), lambda i: (i, 0))
    nout = 3 if n == 1 else 4
    res = _call(
        body, name="adamw", grid=(rows // tm,),
        in_specs=[blk] * 3 + [pl.BlockSpec((tm, p.shape[1]), lambda i: (i, 0)) for p in parts], out_specs=[blk] * nout,
        out_shape=[jax.ShapeDtypeStruct((rows, cols), F32)] * nout,
        compiler_params=_params(),
    )(w, m, v, *parts)
    return (parts[0] if n == 1 else res[3], *res[:3])


def _adamw_group(weights, grads, ms, vs):
    n = len(weights)

    def body(*refs):
        ins, outs = refs[:4 * n], refs[4 * n:]
        for k in range(n):
            w_ref, g_ref, m_ref, v_ref = ins[4 * k:4 * k + 4]
            outs[3 * k][...], outs[3 * k + 1][...], outs[3 * k + 2][...] = _adam_update(
                w_ref[...], g_ref[...], m_ref[...], v_ref[...])

    res = _call(
        body, name="adamw_group",
        out_shape=[jax.ShapeDtypeStruct(w.shape, F32) for w in weights for _ in range(3)],
        compiler_params=_params(),
    )(*[a for quad in zip(weights, grads, ms, vs) for a in quad])
    return [(grads[k], *res[3 * k:3 * k + 3]) for k in range(n)]


def _adamw_tiny(tiny, weights, ms, vs):
    shapes = [w.shape for w in weights]
    n = len(weights)

    def grad_of(t_ref, k, shape):
        base = 8 * k
        if shape[1] > LANES:
            return [t_ref[base + j:base + j + 1, :] for j in range(shape[1] // LANES)]
        return [t_ref[base:base + shape[0], 0:shape[1]]]

    def body(t_ref, *refs):
        w_refs, m_refs, v_refs = refs[:n], refs[n:2 * n], refs[2 * n:3 * n]
        loss_ref, outs = refs[3 * n], refs[3 * n + 1:]
        loss_ref[...] = (0.5 / D_MODEL) * jnp.sum(t_ref[8 * n:8 * n + 8, :], keepdims=True)
        for k, shape in enumerate(shapes):
            g_ref, d_ref, nm_ref, nv_ref = outs[4 * k:4 * k + 4]
            for j, g in enumerate(grad_of(t_ref, k, shape)):
                cols = slice(j * LANES, (j + 1) * LANES) if shape[1] > LANES else slice(None)
                g_ref[:, cols] = g
                d_ref[:, cols], nm_ref[:, cols], nv_ref[:, cols] = _adam_update(
                    w_refs[k][:, cols], g, m_refs[k][:, cols], v_refs[k][:, cols])

    out_shape = [jax.ShapeDtypeStruct((1, 1), F32)]
    for shape in shapes:
        out_shape += [jax.ShapeDtypeStruct(shape, F32)] * 4
    return _call(body, name="adamw_tiny", out_shape=out_shape, compiler_params=_params())(tiny, *weights, *ms, *vs)


def _local_grads(x, mem, tgt, norm_gain, wt_sh, gmlp_v_gain, gmlp_w_s, gmlp_b, attn_q_gain, attn_k_gain,
                 mem_norm_gain, wkv_sh, mem_q_gain, mem_k_gain, wo_sh):
    vg = gmlp_v_gain.reshape(1, GMLP_W)
    bias_full = jnp.repeat(gmlp_b.T, HEAD_DIM, axis=1)
    gq2, gk2 = jnp.tile(attn_q_gain, (1, 2)), jnp.tile(attn_k_gain, (1, 2))
    qg4, kg4 = jnp.tile(mem_q_gain, (1, 4)), jnp.tile(mem_k_gain, (1, 4))

    proj, wt = _gather_proj(x, norm_gain, wt_sh)
    yg = _gmlp_fwd(proj, vg, gmlp_w_s, bias_full)
    o, lse, ya, wkv, wo = _attn_fwd(proj, gq2, gk2, wkv_sh, wo_sh)
    kv, hm = _mem_kv(mem, mem_norm_gain, wkv)
    om, ym = _mem_fwd(proj, kv, qg4, kg4)
    dy, dyc, g_wo, err2 = _out_loss(yg, ya, ym, x, tgt, wo)
    dmq, dmg, g_mq, g_mk, g_wkv, g_mng = _mem_bwd(proj, om, dyc, kv, hm, mem, mem_norm_gain, wkv, qg4, kg4)
    daq, dak, dav, dag, g_aq, g_ak, g_wkv_sh, g_wo_sh = _attn_bwd(proj, o, lse, dyc, gq2, gk2, g_wkv, g_wo)
    dg, g_ws, g_b, g_vg = _gmlp_bwd(proj, dyc, vg, gmlp_w_s, bias_full)
    gx, g_wt, g_ng = _proj_bwd(x, dy, norm_gain, wt, dg, daq, dak, dav, dag, dmq, dmg)

    tiny = jnp.concatenate([g_ng, g_vg, g_b, g_aq, g_ak, g_mng, g_mq, g_mk, err2], axis=0)
    return gx, g_wt, g_wkv_sh, g_wo_sh, g_ws.reshape(4 * CHUNK, CHUNK), tiny


def kernel(x, mem, norm_gain, w_in, gmlp_v_gain, gmlp_w_s, gmlp_b, attn_q_gain, attn_k_gain, mem_norm_gain, w_mem_kv, mem_q_gain, mem_k_gain, w_out, loss_target, m_norm_gain, m_w_in, m_gmlp_v_gain, m_gmlp_w_s, m_gmlp_b, m_attn_q_gain, m_attn_k_gain, m_mem_norm_gain, m_w_mem_kv, m_mem_q_gain, m_mem_k_gain, m_w_out, v_norm_gain, v_w_in, v_gmlp_v_gain, v_gmlp_w_s, v_gmlp_b, v_attn_q_gain, v_attn_k_gain, v_mem_norm_gain, v_w_mem_kv, v_mem_q_gain, v_mem_k_gain, v_w_out):
    gx, g_wt, g_wkv_sh, g_wo_sh, g_ws, tiny = _local_grads(
        x[0], mem[0], loss_target[0], norm_gain, w_in[0].T, gmlp_v_gain[0], gmlp_w_s[0], gmlp_b[0],
        attn_q_gain, attn_k_gain, mem_norm_gain, w_mem_kv[0], mem_q_gain, mem_k_gain, w_out[0])
    *g_wt_sh, g_ws, tiny = _reduce_grads(g_wt, g_ws, tiny)
    chip_block = lambda g: g.reshape(2 * g.shape[1], g.shape[2])
    g_wt_sh = tuple(chip_block(g) for g in g_wt_sh)
    g_wkv_sh, g_wo_sh = chip_block(g_wkv_sh), chip_block(g_wo_sh)

    ws = (norm_gain, w_in, gmlp_v_gain, gmlp_w_s, gmlp_b, attn_q_gain, attn_k_gain, mem_norm_gain, w_mem_kv,
          mem_q_gain, mem_k_gain, w_out)
    ms = (m_norm_gain, m_w_in, m_gmlp_v_gain, m_gmlp_w_s, m_gmlp_b, m_attn_q_gain, m_attn_k_gain, m_mem_norm_gain,
          m_w_mem_kv, m_mem_q_gain, m_mem_k_gain, m_w_out)
    vs = (v_norm_gain, v_w_in, v_gmlp_v_gain, v_gmlp_w_s, v_gmlp_b, v_attn_q_gain, v_attn_k_gain, v_mem_norm_gain,
          v_w_mem_kv, v_mem_q_gain, v_mem_k_gain, v_w_out)
    form = {1: lambda a: a[0].T, 3: lambda a: a.reshape(4 * CHUNK, CHUNK), 2: lambda a: a[0], 4: lambda a: a[0],
            8: lambda a: a[0], 11: lambda a: a[0]}
    back = {1: lambda a: a.T[None], 3: lambda a: a.reshape(1, 4, CHUNK, CHUNK), 2: lambda a: a[None],
            4: lambda a: a[None], 8: lambda a: a[None], 11: lambda a: a[None]}
    fwd = lambda t, i: form.get(i, lambda a: a)(t[i])
    out = {}
    out[1] = _adamw(fwd(ws, 1), g_wt_sh, fwd(ms, 1), fwd(vs, 1))
    small = (3, 8, 11)
    for i, quad in zip(small, _adamw_group([fwd(ws, i) for i in small], [g_ws, g_wkv_sh, g_wo_sh],
                                           [fwd(ms, i) for i in small], [fwd(vs, i) for i in small])):
        out[i] = quad
    res = _adamw_tiny(tiny, [fwd(ws, i) for i in TINY_ORDER], [fwd(ms, i) for i in TINY_ORDER],
                      [fwd(vs, i) for i in TINY_ORDER])
    for k, i in enumerate(TINY_ORDER):
        out[i] = res[1 + 4 * k:5 + 4 * k]
    leaves = [[back.get(i, lambda a: a)(out[i][j]) for i in range(12)] for j in range(4)]
    return (res[0].reshape(()), gx[None], *leaves[0], *leaves[1], *leaves[2], *leaves[3])
```

```python
import math

import jax
import jax.numpy as jnp
from jax import lax
from jax.experimental import pallas as pl
from jax.experimental.pallas import tpu as pltpu

F32 = jnp.float32
BF16 = jnp.bfloat16

SEQ = 4096
D_MODEL = 1024
HEAD_DIM = 64
LANES = 128
CHUNK = 128
GMLP_W, ATTN_W, MEM_W = 256, 512, 256
IN_W = 3 * GMLP_W + 4 * ATTN_W + 2 * MEM_W
MEM_LEN = 256
DILATIONS = (16, 4, 1)
EPS = 1e-6
QK_SCALE = 1.0 / math.sqrt(HEAD_DIM)
C_GU, C_GV, C_GG, C_AQ, C_AK, C_AV, C_AG, C_MQ, C_MG = 0, 256, 512, 768, 1280, 1792, 2304, 2816, 3072

ADAM_LR, ADAM_B1, ADAM_B2, ADAM_EPS, ADAM_WD, ADAM_STEP = 0.001, 0.9, 0.999, 1e-08, 0.01, 10

VMEM_LIMIT = 48 * 1024 * 1024
RS_CHUNKS = 4
ATTN_UNROLL = 4
MESH = pl.DeviceIdType.MESH

TINY_ORDER = (0, 2, 4, 5, 6, 7, 9, 10)


def _call(body, **kw):
    return pl.pallas_call(body, **kw)


def _params(**kw):
    return pltpu.CompilerParams(vmem_limit_bytes=VMEM_LIMIT, **kw)


def _dot(a, b):
    return jnp.dot(a, b, preferred_element_type=F32)


def _dot_nt(a, b):
    return lax.dot_general(a, b, (((1,), (1,)), ((), ())), preferred_element_type=F32)


def _dot_tn(a, b):
    return lax.dot_general(a, b, (((0,), (0,)), ((), ())), preferred_element_type=F32)


def _head_blockdiag():
    r = lax.shift_right_logical(lax.broadcasted_iota(jnp.int32, (LANES, LANES), 0), 6)
    c = lax.shift_right_logical(lax.broadcasted_iota(jnp.int32, (LANES, LANES), 1), 6)
    return jnp.where(r == c, 1.0, 0.0).astype(BF16)


def _headsum(v, bd):
    hi = v.astype(BF16)
    lo = (v - hi.astype(F32)).astype(BF16)
    return _dot(hi, bd) + _dot(lo, bd)


def _lo_mask(rows):
    return lax.broadcasted_iota(jnp.int32, (rows, LANES), 1) < HEAD_DIM


def _sigmoid(x):
    return 1.0 / (1.0 + jnp.exp(-x))


def _fold_heads(v):
    return v + pltpu.roll(v, HEAD_DIM, 1)


def _put_rows(ref, vec, accumulate=False):
    for j in range(vec.shape[1] // LANES):
        piece = vec[:, j * LANES:(j + 1) * LANES]
        ref[j:j + 1, :] = ref[j:j + 1, :] + piece if accumulate else piece


def _gather_proj(x, gain, wt_sh):
    tm = 512
    nrow = SEQ // tm
    widths = (768, 896, 768, 896)
    nunits = len(widths)
    pair = 2 * wt_sh.shape[0]
    assert pair % LANES == 0 and sum(widths[:2]) == pair

    def body(x_ref, g_ref, wt_sh_ref, proj_hbm, wt_hbm, h_scr, land, res, send_sems, recv_sems, out_sems, copy_sem):
        u, i = pl.program_id(0), pl.program_id(1)
        cx_, cy_ = lax.axis_index("x"), lax.axis_index("y")
        (send_own, pass_on_neighbours, pass_on_diagonal, _), (y_complete, x_complete, diagonal_complete, sends_done) = (
            _gather_stages((wt_sh_ref,), (land,), send_sems, recv_sems))
        first = lambda k: (u == k) & (i == 0)
        last = (u == nunits - 1) & (i == nrow - 1)
        to_hbm = pltpu.make_async_copy(land, wt_hbm, copy_sem)

        pl.when(first(0))(send_own)

        @pl.when(u == 0)
        def _():
            xv = x_ref[...]
            ms = jnp.mean(xv * xv, axis=-1, keepdims=True)
            h_scr[pl.ds(pl.multiple_of(i * tm, tm), tm), :] = (xv * lax.rsqrt(ms + EPS) * g_ref[...]).astype(BF16)

        @pl.when(first(1))
        def _():
            pass_on_neighbours()
            y_complete()

        @pl.when(first(2))
        def _():
            x_complete()
            pass_on_diagonal()

        @pl.when(first(3))
        def _():
            diagonal_complete()
            to_hbm.start()

        mine, other = pair * cx_, pair * (1 - cx_)
        col0 = (mine + 896 * cy_, mine + 768 * (1 - cy_), other + 896 * cy_, other + 768 * (1 - cy_))
        slot = i % 2
        rows = pl.ds(pl.multiple_of(i * tm, tm), tm)

        def writeback(k, rows_):
            c0 = pl.multiple_of(col0[k], LANES)
            return pltpu.make_async_copy(res.at[slot, :, pl.ds(0, widths[k])], proj_hbm.at[rows_, pl.ds(c0, widths[k])],
                                         out_sems.at[slot])

        for k in range(nunits):
            @pl.when(u == k)
            def _(k=k):
                pl.when(i >= 2)(writeback(k, rows).wait)
                if k > 0:
                    pl.when(i < 2)(writeback(k - 1, rows).wait)
                w_rows = land[pl.ds(pl.multiple_of(col0[k], LANES), widths[k]), :]
                res[slot, :, 0:widths[k]] = _dot_nt(h_scr[rows, :], w_rows)
                writeback(k, rows).start()

        @pl.when(last)
        def _():
            sends_done()
            to_hbm.wait()
            for s in range(2):
                pltpu.make_async_copy(res.at[s, :, pl.ds(0, widths[-1])], proj_hbm.at[rows, pl.ds(0, widths[-1])], out_sems.at[s]).wait()

    full = jax.ShapeDtypeStruct((4 * wt_sh.shape[0], wt_sh.shape[1]), BF16)
    hbm = pl.BlockSpec(memory_space=pl.ANY)
    return _call(
        body, name="gather_proj", grid=(nunits, nrow),
        in_specs=[pl.BlockSpec((tm, D_MODEL), lambda u, i: (jnp.where(u == 0, i, nrow - 1), 0)),
                  pl.BlockSpec((1, D_MODEL), lambda u, i: (0, 0)), pl.BlockSpec(wt_sh.shape, lambda u, i: (0, 0))],
        out_specs=[hbm, hbm],
        out_shape=[jax.ShapeDtypeStruct((SEQ, IN_W), F32), full],
        scratch_shapes=[pltpu.VMEM((SEQ, D_MODEL), BF16), pltpu.VMEM(full.shape, BF16), pltpu.VMEM((2, tm, max(widths)), F32),
                        pltpu.SemaphoreType.DMA((AG_SEMS,)), pltpu.SemaphoreType.DMA((AG_SEMS,)),
                        pltpu.SemaphoreType.DMA((2,)), pltpu.SemaphoreType.DMA],
        compiler_params=_params(),
    )(x, gain, wt_sh)


def _gmlp_weights(w_ref):
    ti = lax.broadcasted_iota(jnp.int32, (CHUNK, CHUNK), 0)
    si = lax.broadcasted_iota(jnp.int32, (CHUNK, CHUNK), 1)
    tril = si <= ti
    return tril, [jnp.where(tril, w_ref[h], 0.0).astype(BF16) for h in range(4)]


def _gmlp_fwd(proj, vgain, w_s, bias_full):
    tm = 1024

    def body(p_ref, vg_ref, w_ref, b_ref, y_ref):
        bd = _head_blockdiag()
        lo = _lo_mask(CHUNK)
        _, wm = _gmlp_weights(w_ref)
        units = [(pl.ds(c * CHUNK, CHUNK), p) for c in range(tm // CHUNK) for p in range(2)]
        col = lambda c0, p: slice(c0 + p * LANES, c0 + (p + 1) * LANES)
        vs = [p_ref[rows, col(C_GV, p)] for rows, p in units]
        rs = [lax.rsqrt(_headsum(v * v, bd) * (1.0 / HEAD_DIM) + EPS) for v in vs]
        vns = [(v * r * vg_ref[:, col(0, p)]).astype(BF16) for v, r, (_, p) in zip(vs, rs, units)]
        sps = [jnp.where(lo, _dot(wm[2 * p], vn), _dot(wm[2 * p + 1], vn)) + b_ref[:, col(0, p)] for vn, (_, p) in zip(vns, units)]
        for sp, (rows, p) in zip(sps, units):
            gt = p_ref[rows, col(C_GG, p)]
            y_ref[rows, col(0, p)] = (p_ref[rows, col(C_GU, p)] * sp * (gt * _sigmoid(gt))).astype(BF16)

    return _call(
        body, name="gmlp_fwd", grid=(SEQ // tm,),
        in_specs=[pl.BlockSpec((tm, 3 * GMLP_W), lambda i: (i, 0)),
                  pl.BlockSpec((1, GMLP_W), lambda i: (0, 0)),
                  pl.BlockSpec((4, CHUNK, CHUNK), lambda i: (0, 0, 0)),
                  pl.BlockSpec((CHUNK, GMLP_W), lambda i: (0, 0))],
        out_specs=pl.BlockSpec((tm, GMLP_W), lambda i: (i, 0)),
        out_shape=jax.ShapeDtypeStruct((SEQ, GMLP_W), BF16),
        compiler_params=_params(),
    )(proj, vgain, w_s, bias_full)


def _gmlp_bwd(proj, dyc, vgain, w_s, bias_full):
    tm = 1024
    nsteps = SEQ // tm

    def body(p_ref, dy_ref, vg_ref, w_ref, b_ref, dg_ref, gw_ref, gb_ref, gv_ref):
        i = pl.program_id(0)
        bd = _head_blockdiag()
        lo = _lo_mask(CHUNK)
        tril, wm = _gmlp_weights(w_ref)
        ri = lax.broadcasted_iota(jnp.int32, (16, LANES), 0)
        li = lax.broadcasted_iota(jnp.int32, (16, LANES), 1)
        head_rows = [jnp.where(((ri == 2 * p) & (li < HEAD_DIM)) | ((ri == 2 * p + 1) & (li >= HEAD_DIM)), 1.0, 0.0).astype(BF16)
                     for p in range(2)]

        @pl.when(i == 0)
        def _():
            gw_ref[...] = jnp.zeros_like(gw_ref)
            gb_ref[...] = jnp.zeros_like(gb_ref)
            gv_ref[...] = jnp.zeros_like(gv_ref)

        units = [(pl.ds(c * CHUNK, CHUNK), p) for c in range(tm // CHUNK) for p in range(2)]
        col = lambda c0, p: slice(c0 + p * LANES, c0 + (p + 1) * LANES)
        vs = [p_ref[rows, col(C_GV, p)] for rows, p in units]
        rs = [lax.rsqrt(_headsum(v * v, bd) * (1.0 / HEAD_DIM) + EPS) for v in vs]
        zs = [v * r for v, r in zip(vs, rs)]
        vns = [(z * vg_ref[:, col(0, p)]).astype(BF16) for z, (_, p) in zip(zs, units)]
        sps = [jnp.where(lo, _dot(wm[2 * p], vn), _dot(wm[2 * p + 1], vn)) + b_ref[:, col(0, p)] for vn, (_, p) in zip(vns, units)]
        dsps = []
        for sp, (rows, p) in zip(sps, units):
            u = p_ref[rows, col(C_GU, p)]
            gt = p_ref[rows, col(C_GG, p)]
            dy = dy_ref[rows, col(0, p)]
            sg = _sigmoid(gt)
            sl = gt * sg
            dg_ref[rows, col(C_GU, p)] = (dy * sp * sl).astype(BF16)
            dg_ref[rows, col(C_GG, p)] = (dy * u * sp * (sg * (1.0 + gt * (1.0 - sg)))).astype(BF16)
            dsps.append(dy * u * sl)
        dspbs = [dsp.astype(BF16) for dsp in dsps]
        dvns = [jnp.where(lo, _dot_tn(wm[2 * p], dspb), _dot_tn(wm[2 * p + 1], dspb)) for dspb, (_, p) in zip(dspbs, units)]
        gws = [(_dot_nt(jnp.where(lo, dsp, 0.0).astype(BF16), vn), _dot_nt(jnp.where(lo, 0.0, dsp).astype(BF16), vn))
               for dsp, vn in zip(dsps, vns)]
        gbs = [(_dot_nt(head_rows[p], dspb) + _dot_nt(head_rows[p], (dsp - dspb.astype(F32)).astype(BF16)))[0:8]
               for dsp, dspb, (_, p) in zip(dsps, dspbs, units)]
        for p in range(2):
            mine = [n for n, (_, q) in enumerate(units) if q == p]
            gw_ref[2 * p] += sum(gws[n][0] for n in mine)
            gw_ref[2 * p + 1] += sum(gws[n][1] for n in mine)
            gvp = sum(jnp.sum(dvns[n] * zs[n], axis=0, keepdims=True) for n in mine)
            gv_ref[2 * p:2 * p + 1, :] += gvp
            gv_ref[2 * p + 1:2 * p + 2, :] += pltpu.roll(gvp, HEAD_DIM, 1)
        gb_ref[...] += sum(gbs)
        for dvn, z, r, (rows, p) in zip(dvns, zs, rs, units):
            dz = dvn * vg_ref[:, col(0, p)]
            dg_ref[rows, col(C_GV, p)] = (r * (dz - z * (_headsum(dz * z, bd) * (1.0 / HEAD_DIM)))).astype(BF16)

        @pl.when(i == nsteps - 1)
        def _():
            for h in range(4):
                gw_ref[h] = jnp.where(tril, gw_ref[h], 0.0)

    return _call(
        body, name="gmlp_bwd", grid=(nsteps,),
        in_specs=[pl.BlockSpec((tm, 3 * GMLP_W), lambda i: (i, 0)),
                  pl.BlockSpec((tm, GMLP_W), lambda i: (i, 0)),
                  pl.BlockSpec((1, GMLP_W), lambda i: (0, 0)),
                  pl.BlockSpec((4, CHUNK, CHUNK), lambda i: (0, 0, 0)),
                  pl.BlockSpec((CHUNK, GMLP_W), lambda i: (0, 0))],
        out_specs=[pl.BlockSpec((tm, 3 * GMLP_W), lambda i: (i, 0)),
                   pl.BlockSpec((4, CHUNK, CHUNK), lambda i: (0, 0, 0)),
                   pl.BlockSpec((8, LANES), lambda i: (0, 0)),
                   pl.BlockSpec((8, LANES), lambda i: (0, 0))],
        out_shape=[jax.ShapeDtypeStruct((SEQ, 3 * GMLP_W), BF16),
                   jax.ShapeDtypeStruct((4, CHUNK, CHUNK), F32),
                   jax.ShapeDtypeStruct((8, LANES), F32),
                   jax.ShapeDtypeStruct((8, LANES), F32)],
        compiler_params=_params(),
    )(proj, dyc, vgain, w_s, bias_full)


def _band_masks():
    qi = lax.broadcasted_iota(jnp.int32, (CHUNK, 2 * CHUNK), 0)
    kj = lax.broadcasted_iota(jnp.int32, (CHUNK, 2 * CHUNK), 1)
    valid2 = ((kj < CHUNK) & (kj >= qi)) | ((kj >= CHUNK) & (kj - CHUNK <= qi))
    q1 = lax.broadcasted_iota(jnp.int32, (CHUNK, CHUNK), 0)
    k1 = lax.broadcasted_iota(jnp.int32, (CHUNK, CHUNK), 1)
    return k1 <= q1, valid2


def _stack_heads(v, lo):
    return jnp.concatenate([jnp.where(lo, v, 0.0), jnp.where(lo, 0.0, v)], axis=0).astype(BF16)


def _rows_of(ref, start, d):
    if d == 1:
        return ref.at[pl.ds(start if isinstance(start, int) else pl.multiple_of(start, CHUNK), CHUNK), :]
    return ref.at[pl.ds(start, CHUNK, stride=d), :]


def _unrolled(lo, hi, unroll, run):
    groups = (hi - lo) // unroll
    if groups:
        def body(g, carry):
            run([lo + g * unroll + t for t in range(unroll)])
            return carry

        lax.fori_loop(0, groups, body, 0)
    if lo + groups * unroll < hi:
        run(range(lo + groups * unroll, hi))


def _for_blocks(d, group_fn, unroll):
    nblk = SEQ // CHUNK
    sh = d.bit_length() - 1

    def first(j):
        return (j * CHUNK if d == 1 else j, None)

    def rest(j):
        start = (j & (d - 1)) + (j >> sh) * (CHUNK * d)
        return (start, start - CHUNK * d)

    _unrolled(0, d, unroll, lambda js: group_fn(d, [first(j) for j in js]))
    _unrolled(d, nblk, unroll, lambda js: group_fn(d, [rest(j) for j in js]))


def _attn_fwd(proj, gq2, gk2, *ride_along):
    tn = 512
    npairs = ATTN_W // LANES
    nride = len(ride_along)

    def body(q_ref, k_ref, v_ref, g_ref, gq_ref, gk_ref, *rest):
        shards, rest = rest[:nride], rest[nride:]
        o_ref, l_ref, ya_ref = rest[:3]
        gathered, rest = rest[3:3 + nride], rest[3 + nride:]
        qn_ref, kn_ref = rest[:2]
        lands, (send_sems, recv_sems, copy_sems) = rest[2:2 + nride], rest[2 + nride:]
        pair = pl.program_id(0)
        ride = _gather_stages(shards, lands, send_sems, recv_sems)[0]
        for step in range(npairs):
            pl.when(pair == step)(ride[step])
        bd = _head_blockdiag()
        lo = _lo_mask(CHUNK)
        valid1, valid2 = _band_masks()

        def norm(t, carry):
            rows = pl.ds(pl.multiple_of(t * tn, tn), tn)
            q, k = q_ref[rows, :], k_ref[rows, :]
            ssq = [_headsum(a * a, bd) for a in (q, k)]
            qn_ref[rows, :] = q * lax.rsqrt(ssq[0] * (1.0 / HEAD_DIM) + EPS) * (gq_ref[...] * QK_SCALE)
            kn_ref[rows, :] = k * lax.rsqrt(ssq[1] * (1.0 / HEAD_DIM) + EPS) * gk_ref[...]
            return carry

        lax.fori_loop(0, SEQ // tn, norm, 0)

        def load_kv(ref, d, start, prev):
            own = _rows_of(ref, start, d)[...]
            if prev is None:
                return own.astype(BF16)
            return jnp.concatenate([_rows_of(ref, prev, d)[...], own], axis=0).astype(BF16)

        def group(d, blocks):
            valid = valid1 if blocks[0][1] is None else valid2
            valid = jnp.concatenate([valid, valid], axis=0)
            qs = [_rows_of(qn_ref, start, d)[...] for start, _ in blocks]
            ks = [load_kv(kn_ref, d, start, prev) for start, prev in blocks]
            vs = [load_kv(v_ref, d, start, prev) for start, prev in blocks]
            ss = [_dot_nt(_stack_heads(q, lo), k) for q, k in zip(qs, ks)]
            ms, ps, ls = [], [], []
            for s in ss:
                s = jnp.where(valid, s, -jnp.inf)
                m = jnp.max(s, axis=-1, keepdims=True)
                p = jnp.exp(s - m)
                ms.append(m)
                ls.append(jnp.sum(p, axis=-1, keepdims=True))
                ps.append(p.astype(BF16))
            os_ = [_dot(p, v) for p, v in zip(ps, vs)]
            for b, (start, _) in enumerate(blocks):
                heads = lambda v: jnp.where(lo, v[:CHUNK], v[CHUNK:])
                lsum = heads(ls[b])
                ob = heads(os_[b]) * (1.0 / lsum)
                lb = heads(ms[b]) + jnp.log(lsum)
                o_rows = _rows_of(o_ref, start, d)
                l_rows = _rows_of(l_ref, start, d)
                if d != DILATIONS[0]:
                    lold = l_rows[...]
                    mx = jnp.maximum(lold, lb)
                    ea = jnp.exp(lold - mx)
                    eb = jnp.exp(lb - mx)
                    inv = 1.0 / (ea + eb)
                    ob = o_rows[...] * (ea * inv) + ob * (eb * inv)
                    lb = mx + jnp.log(ea + eb)
                o_rows[...] = ob
                l_rows[...] = lb

        for d in DILATIONS:
            _for_blocks(d, group, ATTN_UNROLL)

        def fin(t, carry):
            rows = pl.ds(pl.multiple_of(t * tn, tn), tn)
            g = g_ref[rows, :]
            ya_ref[rows, :] = (o_ref[rows, :] * (g * _sigmoid(g))).astype(BF16)
            return carry

        lax.fori_loop(0, SEQ // tn, fin, 0)

        @pl.when(pair == npairs - 1)
        def _():
            to_hbm = [pltpu.make_async_copy(land, out, copy_sems.at[n]) for n, (land, out) in enumerate(zip(lands, gathered))]
            for cp in to_hbm:
                cp.start()
            for cp in to_hbm:
                cp.wait()

    col = lambda c0: pl.BlockSpec((SEQ, LANES), lambda p: (0, c0 // LANES + p))
    vec = pl.BlockSpec((1, LANES), lambda p: (0, 0))
    out = pl.BlockSpec((SEQ, LANES), lambda p: (0, p))
    full = [jax.ShapeDtypeStruct((4 * a.shape[0], a.shape[1]), BF16) for a in ride_along]
    return _call(
        body, name="attn_fwd", grid=(npairs,),
        in_specs=[col(C_AQ), col(C_AK), col(C_AV), col(C_AG), vec, vec]
        + [pl.BlockSpec(a.shape, lambda p: (0, 0)) for a in ride_along],
        out_specs=[out, out, out] + [pl.BlockSpec(memory_space=pl.ANY)] * nride,
        out_shape=[jax.ShapeDtypeStruct((SEQ, ATTN_W), F32), jax.ShapeDtypeStruct((SEQ, ATTN_W), F32),
                   jax.ShapeDtypeStruct((SEQ, ATTN_W), BF16)] + full,
        scratch_shapes=[pltpu.VMEM((SEQ, LANES), F32), pltpu.VMEM((SEQ, LANES), F32)]
        + [pltpu.VMEM(s.shape, BF16) for s in full]
        + [pltpu.SemaphoreType.DMA((AG_SEMS * nride,)), pltpu.SemaphoreType.DMA((AG_SEMS * nride,)),
           pltpu.SemaphoreType.DMA((nride,))],
        compiler_params=_params(),
    )(proj, proj, proj, proj, gq2, gk2, *ride_along)


def _attn_bwd(proj, o, lse, dyc, gq2, gk2, *ride_along):
    tn = 512
    npairs = ATTN_W // LANES
    nride = len(ride_along)
    nbufs = nride * len(RS_KINDS)

    def body(proj_hbm, o_hbm, l_hbm, dyc_hbm, gq_ref, gk_ref, *rest):
        ride_in, rest = rest[:nride], rest[nride:]
        dq_ref, dk_ref, dv_ref, dgt_ref, gqg_ref, gkg_ref = rest[:6]
        ride_out, rest = rest[6:6 + nride], rest[6 + nride:]
        qb_, kb_, vb_, gb_, ob_, lb_, yb_, dkb_, dvb_, sems = rest[:10]
        rs_bufs, (send_sems, recv_sems, local_sems) = rest[10:10 + nbufs], rest[10 + nbufs:]
        rs_stage = _rs_stages(ride_in, ride_out, rs_bufs, send_sems, recv_sems, local_sems, [g.shape[1] for g in ride_along])
        pair = pl.program_id(0)
        for step in range(npairs):
            pl.when(pair == step)(rs_stage[step])
        bd = _head_blockdiag()
        lo = _lo_mask(CHUNK)
        lo2 = lax.broadcasted_iota(jnp.int32, (2 * CHUNK, LANES), 1) < HEAD_DIM
        valid1, valid2 = _band_masks()
        gqs = gq_ref[...] * QK_SCALE
        gk = gk_ref[...]

        def pcol(c0, of=None):
            return acol(proj_hbm, c0, of)

        def acol(hbm, c0=0, of=None):
            of = pair if of is None else of
            return hbm.at[:, pl.ds(pl.multiple_of(c0 + of * LANES, LANES), LANES)]

        def input_loads(of):
            return [pltpu.make_async_copy(src, dst, sems.at[n]) for n, (src, dst) in enumerate((
                (pcol(C_AQ, of), qb_), (pcol(C_AK, of), kb_), (pcol(C_AG, of), gb_), (acol(o_hbm, 0, of), ob_),
                (acol(dyc_hbm, GMLP_W, of), yb_), (pcol(C_AV, of), vb_), (acol(l_hbm, 0, of), lb_)))]

        early = (0, 1, 3, 4)
        loads = input_loads(pair)
        for n, cp in enumerate(loads):
            if n in early:
                pl.when(pair == 0)(cp.start)
            else:
                cp.start()

        @pl.when(pair == 0)
        def _():
            gqg_ref[...] = jnp.zeros_like(gqg_ref)
            gkg_ref[...] = jnp.zeros_like(gkg_ref)

        def pre_qk(t, carry):
            rows = pl.ds(pl.multiple_of(t * tn, tn), tn)
            q, k = qb_[rows, :], kb_[rows, :]
            ssq = [_headsum(a * a, bd) for a in (q, k)]
            qb_[rows, :] = q * lax.rsqrt(ssq[0] * (1.0 / HEAD_DIM) + EPS) * gqs
            kb_[rows, :] = k * lax.rsqrt(ssq[1] * (1.0 / HEAD_DIM) + EPS) * gk
            return carry

        def pre_gate(t, carry):
            rows = pl.ds(pl.multiple_of(t * tn, tn), tn)
            g = gb_[rows, :]
            ov = ob_[rows, :]
            dya = yb_[rows, :]
            sg = _sigmoid(g)
            dgt_ref[rows, :] = (dya * ov * (sg * (1.0 + g * (1.0 - sg)))).astype(BF16)
            do = dya * (g * sg)
            yb_[rows, :] = do
            ob_[rows, :] = jnp.where(first_half, lb_[rows, :], _headsum(do * ov, bd))
            return carry

        first_half = (lax.broadcasted_iota(jnp.int32, (tn, LANES), 1) & (HEAD_DIM - 1)) < HEAD_DIM // 2
        loads[0].wait()
        loads[1].wait()
        lax.fori_loop(0, SEQ // tn, pre_qk, 0)
        for cp in loads[2:5] + loads[6:7]:
            cp.wait()
        lax.fori_loop(0, SEQ // tn, pre_gate, 0)
        loads[5].wait()
        reloads = [pltpu.make_async_copy(pcol(C_AQ), lb_, sems.at[7]), pltpu.make_async_copy(pcol(C_AK), vb_, sems.at[8])]
        reloads[0].start()

        def load_kv(ref, d, start, prev):
            own = _rows_of(ref, start, d)[...]
            if prev is None:
                return own.astype(BF16)
            return jnp.concatenate([_rows_of(ref, prev, d)[...], own], axis=0).astype(BF16)

        def group(d, blocks):
            first = blocks[0][1] is None
            valid, lok = (valid1, lo) if first else (valid2, lo2)
            chains = [(b, h) for b in range(len(blocks)) for h in range(2)]
            mask = lambda h: lo if h == 0 else ~lo
            qs = [_rows_of(qb_, start, d)[...] for start, _ in blocks]
            dos = [_rows_of(yb_, start, d)[...] for start, _ in blocks]
            lds = [_rows_of(ob_, start, d)[...] for start, _ in blocks]
            ks = [load_kv(kb_, d, start, prev) for start, prev in blocks]
            vs = [load_kv(vb_, d, start, prev) for start, prev in blocks]
            qbs = [q.astype(BF16) for q in qs]
            dobs = [do.astype(BF16) for do in dos]
            ss = [_dot_nt(jnp.where(mask(h), qs[b], 0.0).astype(BF16), ks[b]) for b, h in chains]
            dps = [_dot_nt(jnp.where(mask(h), dos[b], 0.0).astype(BF16), vs[b]) for b, h in chains]
            pbs, dss = [], []
            for s, dp, (b, h) in zip(ss, dps, chains):
                hc, dc = h * HEAD_DIM, h * HEAD_DIM + HEAD_DIM // 2
                p = jnp.exp(jnp.where(valid, s, -jnp.inf) - lds[b][:, hc:hc + 1])
                pbs.append(p.astype(BF16))
                dss.append((p * (dp - lds[b][:, dc:dc + 1])).astype(BF16))
            dqs = [_dot(ds, ks[b]) for ds, (b, h) in zip(dss, chains)]
            dks = [_dot_tn(ds, qbs[b]) for ds, (b, h) in zip(dss, chains)]
            dvs = [_dot_tn(p, dobs[b]) for p, (b, h) in zip(pbs, chains)]
            assign = d == DILATIONS[0]
            for b, (start, prev) in enumerate(blocks):
                c0, c1 = 2 * b, 2 * b + 1
                dq_rows = _rows_of(gb_, start, d)
                dqb = jnp.where(lo, dqs[c0], dqs[c1])
                dq_rows[...] = dqb if assign else dq_rows[...] + dqb
                dkc = jnp.where(lok, dks[c0], dks[c1])
                dvc = jnp.where(lok, dvs[c0], dvs[c1])
                spans = ((start, slice(0, CHUNK), True),) if first else (
                    (prev, slice(0, CHUNK), False), (start, slice(CHUNK, 2 * CHUNK), True))
                for st, sl, own in spans:
                    dk_rows = _rows_of(dkb_, st, d)
                    dv_rows = _rows_of(dvb_, st, d)
                    if assign and own:
                        dk_rows[...] = dkc[sl]
                        dv_rows[...] = dvc[sl]
                    else:
                        dk_rows[...] = dk_rows[...] + dkc[sl]
                        dv_rows[...] = dv_rows[...] + dvc[sl]

        for d in DILATIONS:
            _for_blocks(d, group, ATTN_UNROLL)

        reloads[1].start()

        @pl.when(pair < npairs - 1)
        def _():
            nxt = input_loads(pair + 1)
            for n in early:
                nxt[n].start()

        for cp in reloads:
            cp.wait()

        def post(t, carry):
            gq_acc, gk_acc = carry
            rows = pl.ds(pl.multiple_of(t * tn, tn), tn)
            raws = [lb_[rows, :], vb_[rows, :]]
            dns = [gb_[rows, :], dkb_[rows, :]]
            rs = [lax.rsqrt(_headsum(a * a, bd) * (1.0 / HEAD_DIM) + EPS) for a in raws]
            zs = [a * r for a, r in zip(raws, rs)]
            dzs = [dn * gain for dn, gain in zip(dns, (gqs, gk))]
            means = [_headsum(dz * z, bd) * (1.0 / HEAD_DIM) for dz, z in zip(dzs, zs)]
            dq, dk = [r * (dz - z * mean) for r, dz, z, mean in zip(rs, dzs, zs, means)]
            gq, gkk = [jnp.sum(dn * z, axis=0, keepdims=True) for dn, z in zip(dns, zs)]
            dq_ref[rows, :] = dq.astype(BF16)
            dk_ref[rows, :] = dk.astype(BF16)
            dv_ref[rows, :] = dvb_[rows, :].astype(BF16)
            return gq_acc + gq * QK_SCALE, gk_acc + gkk

        zero = jnp.zeros((1, LANES), F32)
        gq_acc, gk_acc = lax.fori_loop(0, SEQ // tn, post, (zero, zero))
        gqg_ref[0:1, :] += gq_acc
        gkg_ref[0:1, :] += gk_acc

        @pl.when(pair == npairs - 1)
        def _():
            gqg_ref[0:1, :] = _fold_heads(gqg_ref[0:1, :])
            gkg_ref[0:1, :] = _fold_heads(gkg_ref[0:1, :])
            rs_stage[npairs]()

    hbm = pl.BlockSpec(memory_space=pl.ANY)
    vec = pl.BlockSpec((1, LANES), lambda p: (0, 0))
    blk8 = pl.BlockSpec((8, LANES), lambda p: (0, 0))
    out = pl.BlockSpec((SEQ, LANES), lambda p: (0, p))
    big = jax.ShapeDtypeStruct((SEQ, ATTN_W), BF16)
    nsem = RS_SEMS * nride
    return _call(
        body, name="attn_bwd", grid=(npairs,),
        in_specs=[hbm, hbm, hbm, hbm, vec, vec] + [hbm] * nride,
        out_specs=[out, out, out, out, blk8, blk8] + [hbm] * nride,
        out_shape=[big, big, big, big, jax.ShapeDtypeStruct((8, LANES), F32), jax.ShapeDtypeStruct((8, LANES), F32)]
        + [jax.ShapeDtypeStruct((2, g.shape[0] // 8, g.shape[1]), F32) for g in ride_along],
        scratch_shapes=[pltpu.VMEM((SEQ, LANES), F32) for _ in range(9)] + [pltpu.SemaphoreType.DMA((9,))]
        + _rs_scratch([g.shape for g in ride_along]) + [pltpu.SemaphoreType.DMA((nsem,)), pltpu.SemaphoreType.DMA((nsem,)),
                                     pltpu.SemaphoreType.DMA((nride,))],
        compiler_params=_params(),
    )(proj, o, lse, dyc, gq2, gk2, *[_rs_view(g) for g in ride_along])


def _mem_kv(mem, gain, wkv):
    def body(m_ref, g_ref, w_ref, kv_ref, hm_ref):
        mv = m_ref[...]
        ms = jnp.mean(mv * mv, axis=-1, keepdims=True)
        hm = (mv * lax.rsqrt(ms + EPS) * g_ref[...]).astype(BF16)
        hm_ref[...] = hm
        kv_ref[...] = _dot(hm, w_ref[...])

    return _call(
        body, name="mem_kv",
        out_shape=[jax.ShapeDtypeStruct((MEM_LEN, 2 * MEM_W), F32), jax.ShapeDtypeStruct((MEM_LEN, D_MODEL), BF16)],
        compiler_params=_params(),
    )(mem, gain, wkv)


def _mem_keys(kv_ref, kg_ref, bd, p):
    mk = kv_ref[:, p * LANES:(p + 1) * LANES]
    r = lax.rsqrt(_headsum(mk * mk, bd) * (1.0 / HEAD_DIM) + EPS)
    z = mk * r
    mkn = (z * kg_ref[:, p * LANES:(p + 1) * LANES]).astype(BF16)
    mvp = kv_ref[:, MEM_W + p * LANES:MEM_W + (p + 1) * LANES].astype(BF16)
    return mkn, mvp, r, z


def _mem_fwd(proj, kv, qg4, kg4):
    tm = 1024

    def body(q_ref, g_ref, kv_ref, qg_ref, kg_ref, om_ref, ym_ref):
        bd = _head_blockdiag()
        lo = _lo_mask(tm)
        keys, qns = [], []
        for p in range(2):
            cs = slice(p * LANES, (p + 1) * LANES)
            keys.append(_mem_keys(kv_ref, kg_ref, bd, p)[:2])
            q = q_ref[:, cs]
            qns.append(q * lax.rsqrt(_headsum(q * q, bd) * (1.0 / HEAD_DIM) + EPS) * (qg_ref[:, cs] * QK_SCALE))
        chains = [(p, h) for p in range(2) for h in range(2)]
        ss = [_dot_nt(jnp.where(lo if h == 0 else ~lo, qns[p], 0.0).astype(BF16), keys[p][0]) for p, h in chains]
        es = [jnp.exp(s - jnp.max(s, axis=-1, keepdims=True)) for s in ss]
        os_ = [_dot(e.astype(BF16), keys[p][1]) for e, (p, h) in zip(es, chains)]
        res = [o * (1.0 / jnp.sum(e, axis=-1, keepdims=True)) for o, e in zip(os_, es)]
        for p in range(2):
            cs = slice(p * LANES, (p + 1) * LANES)
            ov = jnp.where(lo, res[2 * p], res[2 * p + 1])
            g = g_ref[:, cs]
            om_ref[:, cs] = ov
            ym_ref[:, cs] = (ov * (g * _sigmoid(g))).astype(BF16)

    vec = pl.BlockSpec((1, MEM_W), lambda i: (0, 0))
    return _call(
        body, name="mem_fwd", grid=(SEQ // tm,),
        in_specs=[pl.BlockSpec((tm, MEM_W), lambda i: (i, C_MQ // MEM_W)),
                  pl.BlockSpec((tm, MEM_W), lambda i: (i, C_MG // MEM_W)),
                  pl.BlockSpec((MEM_LEN, 2 * MEM_W), lambda i: (0, 0)), vec, vec],
        out_specs=[pl.BlockSpec((tm, MEM_W), lambda i: (i, 0)), pl.BlockSpec((tm, MEM_W), lambda i: (i, 0))],
        out_shape=[jax.ShapeDtypeStruct((SEQ, MEM_W), F32), jax.ShapeDtypeStruct((SEQ, MEM_W), BF16)],
        compiler_params=_params(),
    )(proj, proj, kv, qg4, kg4)


def _mem_bwd(proj, om, dyc, kv, hm, mem, mgain, wkv, qg4, kg4):
    tm = 1024
    nsteps = SEQ // tm

    def body(q_ref, g_ref, om_ref, dy_ref, kv_ref, hm_ref, mem_ref, mg_ref, w_ref, qg_ref, kg_ref,
             dq_ref, dgt_ref, gqg_ref, gkg_ref, gw_ref, gmg_ref, dmk_ref, dmv_ref, gq_acc):
        i = pl.program_id(0)
        bd = _head_blockdiag()
        lo = _lo_mask(tm)
        lom = _lo_mask(MEM_LEN)

        @pl.when(i == 0)
        def _():
            dmk_ref[...] = jnp.zeros_like(dmk_ref)
            dmv_ref[...] = jnp.zeros_like(dmv_ref)
            gq_acc[...] = jnp.zeros_like(gq_acc)

        pairs = []
        for p in range(2):
            cs = slice(p * LANES, (p + 1) * LANES)
            mkn, mvp, _, _ = _mem_keys(kv_ref, kg_ref, bd, p)
            gqs = qg_ref[:, cs] * QK_SCALE
            q = q_ref[:, cs]
            r = lax.rsqrt(_headsum(q * q, bd) * (1.0 / HEAD_DIM) + EPS)
            z = q * r
            qn = z * gqs
            g = g_ref[:, cs]
            ov = om_ref[:, cs]
            dym = dy_ref[:, cs]
            sg = _sigmoid(g)
            dgt_ref[:, cs] = (dym * ov * (sg * (1.0 + g * (1.0 - sg)))).astype(BF16)
            do = dym * (g * sg)
            pairs.append(dict(cs=cs, mkn=mkn, mvp=mvp, gqs=gqs, r=r, z=z, qn=qn, qnb=qn.astype(BF16), do=do,
                              dob=do.astype(BF16), delta=_headsum(do * ov, bd)))
        chains = [(pr_, h) for pr_ in pairs for h in range(2)]
        mask = lambda h: lo if h == 0 else ~lo
        ss = [_dot_nt(jnp.where(mask(h), c["qn"], 0.0).astype(BF16), c["mkn"]) for c, h in chains]
        dps = [_dot_nt(jnp.where(mask(h), c["do"], 0.0).astype(BF16), c["mvp"]) for c, h in chains]
        prs, dss = [], []
        for s, dp, (c, h) in zip(ss, dps, chains):
            e = jnp.exp(s - jnp.max(s, axis=-1, keepdims=True))
            pr = e * (1.0 / jnp.sum(e, axis=-1, keepdims=True))
            prs.append(pr.astype(BF16))
            dss.append((pr * (dp - c["delta"][:, h * HEAD_DIM:h * HEAD_DIM + 1])).astype(BF16))
        dqs = [_dot(ds, c["mkn"]) for ds, (c, h) in zip(dss, chains)]
        dks = [_dot_tn(ds, c["qnb"]) for ds, (c, h) in zip(dss, chains)]
        dvs = [_dot_tn(pr, c["dob"]) for pr, (c, h) in zip(prs, chains)]
        for p, c in enumerate(pairs):
            cs, z, r = c["cs"], c["z"], c["r"]
            dqn = jnp.where(lo, dqs[2 * p], dqs[2 * p + 1])
            dmk_ref[:, cs] += jnp.where(lom, dks[2 * p], dks[2 * p + 1])
            dmv_ref[:, cs] += jnp.where(lom, dvs[2 * p], dvs[2 * p + 1])
            dz = dqn * c["gqs"]
            dq_ref[:, cs] = (r * (dz - z * (_headsum(dz * z, bd) * (1.0 / HEAD_DIM)))).astype(BF16)
            gq_acc[:, cs] += jnp.sum(dqn * z, axis=0, keepdims=True) * QK_SCALE

        @pl.when(i == nsteps - 1)
        def _():
            gqg_ref[...] = jnp.zeros_like(gqg_ref)
            gkg_ref[...] = jnp.zeros_like(gkg_ref)
            gqg_ref[0:1, :] = _fold_heads(gq_acc[:, 0:LANES] + gq_acc[:, LANES:2 * LANES])
            dkv = []
            gk = jnp.zeros((1, LANES), F32)
            for p in range(2):
                cs = slice(p * LANES, (p + 1) * LANES)
                _, _, r, z = _mem_keys(kv_ref, kg_ref, bd, p)
                dn = dmk_ref[:, cs]
                dz = dn * kg_ref[:, cs]
                gk = gk + jnp.sum(dn * z, axis=0, keepdims=True)
                dkv.append(r * (dz - z * (_headsum(dz * z, bd) * (1.0 / HEAD_DIM))))
            gkg_ref[0:1, :] = _fold_heads(gk)
            dkvb = jnp.concatenate(dkv + [dmv_ref[...]], axis=1).astype(BF16)
            gw_ref[...] = _dot_tn(hm_ref[...], dkvb)
            dhm = _dot_nt(dkvb, w_ref[...])
            mv = mem_ref[...]
            zm = mv * lax.rsqrt(jnp.mean(mv * mv, axis=-1, keepdims=True) + EPS)
            _put_rows(gmg_ref, jnp.sum(dhm * zm, axis=0, keepdims=True))

    const = lambda shape: pl.BlockSpec(shape, lambda i: (0,) * len(shape))
    row = lambda j: pl.BlockSpec((tm, MEM_W), lambda i: (i, j))
    blk8 = jax.ShapeDtypeStruct((8, LANES), F32)
    return _call(
        body, name="mem_bwd", grid=(nsteps,),
        in_specs=[row(C_MQ // MEM_W), row(C_MG // MEM_W), row(0), row((GMLP_W + ATTN_W) // MEM_W),
                  const((MEM_LEN, 2 * MEM_W)), const((MEM_LEN, D_MODEL)), const((MEM_LEN, D_MODEL)),
                  const((1, D_MODEL)), const((D_MODEL, 2 * MEM_W)), const((1, MEM_W)), const((1, MEM_W))],
        out_specs=[row(0), row(0), const((8, LANES)), const((8, LANES)),
                   const((D_MODEL, 2 * MEM_W)), const((8, LANES))],
        out_shape=[jax.ShapeDtypeStruct((SEQ, MEM_W), BF16), jax.ShapeDtypeStruct((SEQ, MEM_W), BF16),
                   blk8, blk8, jax.ShapeDtypeStruct((D_MODEL, 2 * MEM_W), F32), blk8],
        scratch_shapes=[pltpu.VMEM((MEM_LEN, MEM_W), F32), pltpu.VMEM((MEM_LEN, MEM_W), F32),
                        pltpu.VMEM((1, MEM_W), F32)],
        compiler_params=_params(),
    )(proj, proj, om, dyc, kv, hm, mem, mgain, wkv, qg4, kg4)


def _out_loss(yg, ya, ym, x, tgt, wo):
    tm = 512
    nsteps = SEQ // tm
    parts = ((0, GMLP_W), (GMLP_W, ATTN_W), (GMLP_W + ATTN_W, MEM_W))

    def body(yg_ref, ya_ref, ym_ref, x_ref, t_ref, w_ref, dy_ref, dyc_ref, gw_ref, ls_ref):
        i = pl.program_id(0)

        @pl.when(i == 0)
        def _():
            gw_ref[...] = jnp.zeros_like(gw_ref)
            ls_ref[...] = jnp.zeros_like(ls_ref)

        ys = (yg_ref[...], ya_ref[...], ym_ref[...])
        y = sum(_dot(yv, w_ref[r0:r0 + n, :]) for yv, (r0, n) in zip(ys, parts))
        err = x_ref[...] + y - t_ref[...]
        _put_rows(ls_ref, jnp.sum(err * err, axis=0, keepdims=True), accumulate=True)
        dy = err * (1.0 / D_MODEL)
        dy_ref[...] = dy
        dyb = dy.astype(BF16)
        dyc_ref[...] = _dot_nt(dyb, w_ref[...])
        for yv, (r0, n) in zip(ys, parts):
            gw_ref[r0:r0 + n, :] += _dot_tn(yv, dyb)

    row = lambda w: pl.BlockSpec((tm, w), lambda i: (i, 0))
    const = lambda shape: pl.BlockSpec(shape, lambda i: (0, 0))
    return _call(
        body, name="out_loss", grid=(nsteps,),
        in_specs=[row(GMLP_W), row(ATTN_W), row(MEM_W), row(D_MODEL), row(D_MODEL), const((D_MODEL, D_MODEL))],
        out_specs=[row(D_MODEL), row(D_MODEL), const((D_MODEL, D_MODEL)), const((8, LANES))],
        out_shape=[jax.ShapeDtypeStruct((SEQ, D_MODEL), F32), jax.ShapeDtypeStruct((SEQ, D_MODEL), F32),
                   jax.ShapeDtypeStruct((D_MODEL, D_MODEL), F32), jax.ShapeDtypeStruct((8, LANES), F32)],
        compiler_params=_params(),
    )(yg, ya, ym, x, tgt, wo)


def _proj_bwd(x, dy, gain, wt, dg, daq, dak, dav, dag, dmq, dmg):
    tm = 512
    nsteps = SEQ // tm
    pieces = ((C_GU, 3 * GMLP_W), (C_AQ, ATTN_W), (C_AK, ATTN_W), (C_AV, ATTN_W), (C_AG, ATTN_W),
              (C_MQ, MEM_W), (C_MG, MEM_W))

    def body(x_ref, dy_ref, g_ref, wt_hbm, p0, p1, p2, p3, p4, p5, p6, gx_ref, gwt_hbm, gg_ref, wt_v, acc, wt_sem, out_sems):
        i = pl.program_id(0)
        wt_load = pltpu.make_async_copy(wt_hbm, wt_v, wt_sem)

        @pl.when(i == 0)
        def _():
            wt_load.start()
            acc[...] = jnp.zeros_like(acc)
            gg_ref[...] = jnp.zeros_like(gg_ref)

        xv = x_ref[...]
        r = lax.rsqrt(jnp.mean(xv * xv, axis=-1, keepdims=True) + EPS)
        z = xv * r
        g = g_ref[...]
        h = (z * g).astype(BF16)
        pl.when(i == 0)(wt_load.wait)
        flush = [pltpu.make_async_copy(acc.at[c0:c0 + w, :], gwt_hbm.at[c0:c0 + w, :], out_sems.at[n])
                 for n, (c0, w) in enumerate(pieces)]
        dh = jnp.zeros((tm, D_MODEL), F32)
        for n, (pref, (c0, w)) in enumerate(zip((p0, p1, p2, p3, p4, p5, p6), pieces)):
            dp = pref[...]
            dh = dh + _dot(dp, wt_v[c0:c0 + w, :])
            acc[c0:c0 + w, :] += _dot_tn(dp, h)
            pl.when(i == nsteps - 1)(flush[n].start)
        _put_rows(gg_ref, jnp.sum(dh * z, axis=0, keepdims=True), accumulate=True)
        dz = dh * g
        gx_ref[...] = dy_ref[...] + r * (dz - z * jnp.mean(dz * z, axis=-1, keepdims=True))

        @pl.when(i == nsteps - 1)
        def _():
            for cp in flush:
                cp.wait()

    row = lambda w: pl.BlockSpec((tm, w), lambda i: (i, 0))
    hbm = pl.BlockSpec(memory_space=pl.ANY)
    vec = pl.BlockSpec((1, D_MODEL), lambda i: (0, 0))
    return _call(
        body, name="proj_bwd", grid=(nsteps,),
        in_specs=[row(D_MODEL), row(D_MODEL), vec, hbm] + [row(w) for _, w in pieces],
        out_specs=[row(D_MODEL), hbm, pl.BlockSpec((8, LANES), lambda i: (0, 0))],
        out_shape=[jax.ShapeDtypeStruct((SEQ, D_MODEL), F32), jax.ShapeDtypeStruct((IN_W, D_MODEL), F32),
                   jax.ShapeDtypeStruct((8, LANES), F32)],
        scratch_shapes=[pltpu.VMEM((IN_W, D_MODEL), BF16), pltpu.VMEM((IN_W, D_MODEL), F32), pltpu.SemaphoreType.DMA,
                        pltpu.SemaphoreType.DMA((len(pieces),))],
        compiler_params=_params(),
    )(x, dy, gain, wt, dg, daq, dak, dav, dag, dmq, dmg)


AG_SEMS = 8


def _gather_stages(ins, lands, send_sems, recv_sems):
    n = len(ins)
    nrows = [a.shape[0] for a in ins]
    x, y, c = lax.axis_index("x"), lax.axis_index("y"), lax.axis_index("c")
    sib, xn, yn = (x, y, 1 - c), (1 - x, y, c), (x, 1 - y, c)
    me, cx, cy, cd = 2 * x + y, 2 * (1 - x) + y, 2 * x + (1 - y), 2 * (1 - x) + (1 - y)

    def part(a, chip, hf, quarter=None):
        rows = nrows[a] // 2
        base = chip * nrows[a] + hf * rows
        if quarter is not None:
            rows = rows // 2
            base = base + quarter * rows
        return lands[a].at[pl.ds(pl.multiple_of(base, 16), rows), :]

    def copy(a, j, ref, to):
        k = AG_SEMS * a + j
        return pltpu.make_async_remote_copy(src_ref=ref, dst_ref=ref, send_sem=send_sems.at[k],
                                            recv_sem=recv_sems.at[k], device_id=to, device_id_type=MESH)

    def own(a):
        return [copy(a, 0, part(a, me, c), xn), copy(a, 1, part(a, me, c), yn)]

    def neighbours(a):
        return [copy(a, 4, part(a, cx, c, 1), yn), copy(a, 2, part(a, cx, c), sib),
                copy(a, 5, part(a, cy, c, 0), xn), copy(a, 3, part(a, cy, c), sib)]

    def diagonal(a):
        return [copy(a, 7, part(a, cd, c, 1), sib), copy(a, 6, part(a, cd, c, 0), sib)]

    def send_own():
        for a in range(n):
            lands[a][pl.ds(pl.multiple_of(me * nrows[a], 16), nrows[a]), :] = ins[a][...].astype(BF16)
            for cp in own(a):
                cp.start()

    def pass_on_neighbours():
        for a in range(n):
            copy(a, 0, part(a, cx, c), xn).wait_recv()
            copy(a, 1, part(a, cy, c), yn).wait_recv()
            for cp in neighbours(a):
                cp.start()

    def pass_on_diagonal():
        for a in range(n):
            copy(a, 4, part(a, cd, c, 1), yn).wait_recv()
            copy(a, 5, part(a, cd, c, 0), xn).wait_recv()
            for cp in diagonal(a):
                cp.start()

    def y_complete():
        for a in range(n):
            copy(a, 3, part(a, cy, 1 - c), sib).wait_recv()

    def x_complete():
        for a in range(n):
            copy(a, 2, part(a, cx, 1 - c), sib).wait_recv()

    def diagonal_complete():
        for a in range(n):
            copy(a, 6, part(a, cd, 1 - c, 0), sib).wait_recv()
            copy(a, 7, part(a, cd, 1 - c, 1), sib).wait_recv()

    def sends_done():
        for a in range(n):
            for cp in own(a) + neighbours(a) + diagonal(a):
                cp.wait_send()

    def finish():
        y_complete()
        x_complete()
        diagonal_complete()
        sends_done()

    return (send_own, pass_on_neighbours, pass_on_diagonal, finish), (y_complete, x_complete, diagonal_complete, sends_done)


RS_SEMS = 6
RS_KINDS = (((2, 2), 1, F32), ((2, 2), 1, F32), ((2, 2), 2, BF16), ((2, 2), 2, BF16), ((2, 2), 2, F32),
            ((2,), 2, BF16), ((2,), 2, BF16), ((2,), 1, F32))


def _rs_view(g):
    return g.reshape(2, 2, 2, g.shape[0] // 8, g.shape[1])


def _rs_scratch(shapes):
    return [pltpu.VMEM(lead + (r // 8, w // split), dt) for lead, split, dt in RS_KINDS for r, w in shapes]


def _rs_stages(gs, outs, bufs, send_sems, recv_sems, local_sems, widths):
    n = len(gs)
    loc, ra, s_b, r_b, acc1, s_c, r_c, fin = (bufs[n * i:n * i + n] for i in range(len(RS_KINDS)))
    half_w = [w // 2 for w in widths]
    x, y, c = lax.axis_index("x"), lax.axis_index("y"), lax.axis_index("c")
    sib, xn, yn = (x, y, 1 - c), (1 - x, y, c), (x, 1 - y, c)

    def copy(a, j, src, dst, to):
        k = RS_SEMS * a + j
        return pltpu.make_async_remote_copy(src_ref=src, dst_ref=dst, send_sem=send_sems.at[k],
                                            recv_sem=recv_sems.at[k], device_id=to, device_id_type=MESH)

    def step_a(a):
        return [copy(a, 0, gs[a].at[:, :, 1 - c], ra[a], sib),
                pltpu.make_async_copy(gs[a].at[:, :, c], loc[a], local_sems.at[a])]

    def step_b(a):
        return copy(a, 1, s_b[a].at[0], r_b[a].at[0], xn), copy(a, 2, s_b[a].at[1], r_b[a].at[1], yn)

    def step_c(a):
        return copy(a, 3, s_c[a].at[0], r_c[a].at[0], yn), copy(a, 4, s_c[a].at[1], r_c[a].at[1], xn)

    def step_d(a, half):
        rows = fin[a].at[half]
        return copy(a, 5, rows, rows, sib)

    def start():
        for a in range(n):
            for cp in step_a(a):
                cp.start()

    def a_to_b():
        for a in range(n):
            for cp in step_a(a):
                cp.wait()
            ra[a][...] = loc[a][...] + ra[a][...]
            s_b[a][0] = ra[a][1 - x, :, :, :half_w[a]].astype(BF16)
            s_b[a][1] = ra[a][:, 1 - y, :, half_w[a]:].astype(BF16)
            for cp in step_b(a):
                cp.start()

    def b_to_c():
        for a in range(n):
            for cp in step_b(a):
                cp.wait()
            acc1[a][0] = ra[a][x, :, :, :half_w[a]] + r_b[a][0].astype(F32)
            acc1[a][1] = ra[a][:, y, :, half_w[a]:] + r_b[a][1].astype(F32)
            s_c[a][0] = acc1[a][0, 1 - y].astype(BF16)
            s_c[a][1] = acc1[a][1, 1 - x].astype(BF16)
            for cp in step_c(a):
                cp.start()

    def c_to_d():
        for a in range(n):
            for cp in step_c(a):
                cp.wait()
            fin[a][c, :, :half_w[a]] = acc1[a][0, y] + r_c[a][0].astype(F32)
            fin[a][c, :, half_w[a]:] = acc1[a][1, x] + r_c[a][1].astype(F32)
            step_d(a, c).start()

    def finish():
        to_hbm = [pltpu.make_async_copy(fin[a], outs[a], local_sems.at[a]) for a in range(n)]
        for a in range(n):
            step_d(a, 1 - c).wait_recv()
            step_d(a, c).wait_send()
            to_hbm[a].start()
        for cp in to_hbm:
            cp.wait()

    return start, a_to_b, b_to_c, c_to_d, finish


def _reduce_grads(gwt, g_ws, tiny):
    cw = gwt.shape[1] // RS_CHUNKS
    chunk_shape = (gwt.shape[0], cw)

    def body(g0, ws_in, tiny_in, *rest):
        outs, o_ws, o_tiny = rest[:RS_CHUNKS], rest[RS_CHUNKS], rest[RS_CHUNKS + 1]
        rest = rest[RS_CHUNKS + 2:]
        nb = len(RS_KINDS) * RS_CHUNKS
        sm, sa, sb, sc, acc_s, send_sems, recv_sems, local_sems = rest[nb:]
        blocks = [g0.at[:, :, :, :, pl.ds(j * cw, cw)] for j in range(RS_CHUNKS)]
        start, a_to_b, b_to_c, c_to_d, finish = _rs_stages(blocks, outs, rest[:nb], send_sems, recv_sems, local_sems,
                                                           [cw] * RS_CHUNKS)
        n_ws = ws_in.shape[0]
        sm[0:n_ws, :] = ws_in[...]
        sm[n_ws:, :] = tiny_in[...]
        x, y, c = lax.axis_index("x"), lax.axis_index("y"), lax.axis_index("c")

        def small(j, src, dst, to):
            k = RS_SEMS * RS_CHUNKS + j
            return pltpu.make_async_remote_copy(src_ref=src, dst_ref=dst, send_sem=send_sems.at[k],
                                                recv_sem=recv_sems.at[k], device_id=to, device_id_type=MESH)

        along_c, along_x, along_y = (small(0, sm, sa, (x, y, 1 - c)), small(1, acc_s, sb, (1 - x, y, c)),
                                     small(2, sb, sc, (x, 1 - y, c)))
        start()
        along_c.start()
        a_to_b()
        along_c.wait()
        acc_s[...] = sm[...] + sa[...]
        along_x.start()
        b_to_c()
        along_x.wait()
        sb[...] = acc_s[...] + sb[...]
        along_y.start()
        c_to_d()
        along_y.wait()
        o_ws[...] = sb[0:n_ws, :] + sc[0:n_ws, :]
        o_tiny[...] = sb[n_ws:, :] + sc[n_ws:, :]
        finish()

    vm = pl.BlockSpec(memory_space=pltpu.VMEM)
    hbm = pl.BlockSpec(memory_space=pl.ANY)
    small_shape = (g_ws.shape[0] + tiny.shape[0], LANES)
    scratch = _rs_scratch([chunk_shape] * RS_CHUNKS) + [pltpu.VMEM(small_shape, F32) for _ in range(5)]
    nsem = RS_SEMS * RS_CHUNKS + 3
    scratch += [pltpu.SemaphoreType.DMA((nsem,)), pltpu.SemaphoreType.DMA((nsem,)), pltpu.SemaphoreType.DMA((RS_CHUNKS,))]
    return _call(
        body, name="reduce_grads",
        out_shape=[jax.ShapeDtypeStruct((2, gwt.shape[0] // 8, cw), F32)] * RS_CHUNKS
        + [jax.ShapeDtypeStruct(g_ws.shape, F32), jax.ShapeDtypeStruct(tiny.shape, F32)],
        in_specs=[hbm, vm, vm],
        out_specs=[hbm] * RS_CHUNKS + [vm, vm],
        scratch_shapes=scratch,
        compiler_params=_params(),
    )(_rs_view(gwt), g_ws, tiny)


def _adam_update(w, g, m, v):
    nm = ADAM_B1 * m + (1.0 - ADAM_B1) * g
    nv = ADAM_B2 * v + (1.0 - ADAM_B2) * (g * g)
    m_hat = nm / (1.0 - ADAM_B1 ** ADAM_STEP)
    v_hat = nv / (1.0 - ADAM_B2 ** ADAM_STEP)
    return -ADAM_LR * (m_hat / (jnp.sqrt(v_hat) + ADAM_EPS) + ADAM_WD * w), nm, nv


def _adamw(w, g, m, v):
    rows, cols = w.shape
    tm = max(t for t in range(8, 257, 8) if rows % t == 0)
    parts = tuple(g) if isinstance(g, (tuple, list)) else (g,)
    n = len(parts)

    def body(w_ref, m_ref, v_ref, *refs):
        gv = jnp.concatenate([r[...] for r in refs[:n]], axis=1)
        d_ref, nm_ref, nv_ref = refs[n:n + 3]
        d_ref[...], nm_ref[...], nv_ref[...] = _adam_update(w_ref[...], gv, m_ref[...], v_ref[...])
        if n > 1:
            refs[n + 3][...] = gv

    blk = pl.BlockSpec((tm, cols), lambda i: (i, 0))
    nout = 3 if n == 1 else 4
    res = _call(
        body, name="adamw", grid=(rows // tm,),
        in_specs=[blk] * 3 + [pl.BlockSpec((tm, p.shape[1]), lambda i: (i, 0)) for p in parts], out_specs=[blk] * nout,
        out_shape=[jax.ShapeDtypeStruct((rows, cols), F32)] * nout,
        compiler_params=_params(),
    )(w, m, v, *parts)
    return (parts[0] if n == 1 else res[3], *res[:3])


def _adamw_tiny(tiny, weights, ms, vs):
    shapes = [w.shape for w in weights]
    n = len(weights)

    def grad_of(t_ref, k, shape):
        base = 8 * k
        if shape[1] > LANES:
            return [t_ref[base + j:base + j + 1, :] for j in range(shape[1] // LANES)]
        return [t_ref[base:base + shape[0], 0:shape[1]]]

    def body(t_ref, *refs):
        w_refs, m_refs, v_refs = refs[:n], refs[n:2 * n], refs[2 * n:3 * n]
        loss_ref, outs = refs[3 * n], refs[3 * n + 1:]
        loss_ref[...] = (0.5 / D_MODEL) * jnp.sum(t_ref[8 * n:8 * n + 8, :], keepdims=True)
        for k, shape in enumerate(shapes):
            g_ref, d_ref, nm_ref, nv_ref = outs[4 * k:4 * k + 4]
            for j, g in enumerate(grad_of(t_ref, k, shape)):
                cols = slice(j * LANES, (j + 1) * LANES) if shape[1] > LANES else slice(None)
                g_ref[:, cols] = g
                d_ref[:, cols], nm_ref[:, cols], nv_ref[:, cols] = _adam_update(
                    w_refs[k][:, cols], g, m_refs[k][:, cols], v_refs[k][:, cols])

    out_shape = [jax.ShapeDtypeStruct((1, 1), F32)]
    for shape in shapes:
        out_shape += [jax.ShapeDtypeStruct(shape, F32)] * 4
    return _call(body, name="adamw_tiny", out_shape=out_shape, compiler_params=_params())(tiny, *weights, *ms, *vs)


def _local_grads(x, mem, tgt, norm_gain, wt_sh, gmlp_v_gain, gmlp_w_s, gmlp_b, attn_q_gain, attn_k_gain,
                 mem_norm_gain, wkv_sh, mem_q_gain, mem_k_gain, wo_sh):
    vg = gmlp_v_gain.reshape(1, GMLP_W)
    bias_full = jnp.repeat(gmlp_b.T, HEAD_DIM, axis=1)
    gq2, gk2 = jnp.tile(attn_q_gain, (1, 2)), jnp.tile(attn_k_gain, (1, 2))
    qg4, kg4 = jnp.tile(mem_q_gain, (1, 4)), jnp.tile(mem_k_gain, (1, 4))

    proj, wt = _gather_proj(x, norm_gain, wt_sh)
    yg = _gmlp_fwd(proj, vg, gmlp_w_s, bias_full)
    o, lse, ya, wkv, wo = _attn_fwd(proj, gq2, gk2, wkv_sh, wo_sh)
    kv, hm = _mem_kv(mem, mem_norm_gain, wkv)
    om, ym = _mem_fwd(proj, kv, qg4, kg4)
    dy, dyc, g_wo, err2 = _out_loss(yg, ya, ym, x, tgt, wo)
    dmq, dmg, g_mq, g_mk, g_wkv, g_mng = _mem_bwd(proj, om, dyc, kv, hm, mem, mem_norm_gain, wkv, qg4, kg4)
    daq, dak, dav, dag, g_aq, g_ak, g_wkv_sh, g_wo_sh = _attn_bwd(proj, o, lse, dyc, gq2, gk2, g_wkv, g_wo)
    dg, g_ws, g_b, g_vg = _gmlp_bwd(proj, dyc, vg, gmlp_w_s, bias_full)
    gx, g_wt, g_ng = _proj_bwd(x, dy, norm_gain, wt, dg, daq, dak, dav, dag, dmq, dmg)

    tiny = jnp.concatenate([g_ng, g_vg, g_b, g_aq, g_ak, g_mng, g_mq, g_mk, err2], axis=0)
    return gx, g_wt, g_wkv_sh, g_wo_sh, g_ws.reshape(4 * CHUNK, CHUNK), tiny


def kernel(x, mem, norm_gain, w_in, gmlp_v_gain, gmlp_w_s, gmlp_b, attn_q_gain, attn_k_gain, mem_norm_gain, w_mem_kv, mem_q_gain, mem_k_gain, w_out, loss_target, m_norm_gain, m_w_in, m_gmlp_v_gain, m_gmlp_w_s, m_gmlp_b, m_attn_q_gain, m_attn_k_gain, m_mem_norm_gain, m_w_mem_kv, m_mem_q_gain, m_mem_k_gain, m_w_out, v_norm_gain, v_w_in, v_gmlp_v_gain, v_gmlp_w_s, v_gmlp_b, v_attn_q_gain, v_attn_k_gain, v_mem_norm_gain, v_w_mem_kv, v_mem_q_gain, v_mem_k_gain, v_w_out):
    gx, g_wt, g_wkv_sh, g_wo_sh, g_ws, tiny = _local_grads(
        x[0], mem[0], loss_target[0], norm_gain, w_in[0].T, gmlp_v_gain[0], gmlp_w_s[0], gmlp_b[0],
        attn_q_gain, attn_k_gain, mem_norm_gain, w_mem_kv[0], mem_q_gain, mem_k_gain, w_out[0])
    *g_wt_sh, g_ws, tiny = _reduce_grads(g_wt, g_ws, tiny)
    chip_block = lambda g: g.reshape(2 * g.shape[1], g.shape[2])
    g_wt_sh = tuple(chip_block(g) for g in g_wt_sh)
    g_wkv_sh, g_wo_sh = chip_block(g_wkv_sh), chip_block(g_wo_sh)

    ws = (norm_gain, w_in, gmlp_v_gain, gmlp_w_s, gmlp_b, attn_q_gain, attn_k_gain, mem_norm_gain, w_mem_kv,
          mem_q_gain, mem_k_gain, w_out)
    ms = (m_norm_gain, m_w_in, m_gmlp_v_gain, m_gmlp_w_s, m_gmlp_b, m_attn_q_gain, m_attn_k_gain, m_mem_norm_gain,
          m_w_mem_kv, m_mem_q_gain, m_mem_k_gain, m_w_out)
    vs = (v_norm_gain, v_w_in, v_gmlp_v_gain, v_gmlp_w_s, v_gmlp_b, v_attn_q_gain, v_attn_k_gain, v_mem_norm_gain,
          v_w_mem_kv, v_mem_q_gain, v_mem_k_gain, v_w_out)
    form = {1: lambda a: a[0].T, 3: lambda a: a.reshape(4 * CHUNK, CHUNK), 2: lambda a: a[0], 4: lambda a: a[0],
            8: lambda a: a[0], 11: lambda a: a[0]}
    back = {1: lambda a: a.T[None], 3: lambda a: a.reshape(1, 4, CHUNK, CHUNK), 2: lambda a: a[None],
            4: lambda a: a[None], 8: lambda a: a[None], 11: lambda a: a[None]}
    fwd = lambda t, i: form.get(i, lambda a: a)(t[i])
    out = {}
    for i, g in ((1, g_wt_sh), (3, g_ws), (8, g_wkv_sh), (11, g_wo_sh)):
        out[i] = _adamw(fwd(ws, i), g, fwd(ms, i), fwd(vs, i))
    res = _adamw_tiny(tiny, [fwd(ws, i) for i in TINY_ORDER], [fwd(ms, i) for i in TINY_ORDER],
                      [fwd(vs, i) for i in TINY_ORDER])
    for k, i in enumerate(TINY_ORDER):
        out[i] = res[1 + 4 * k:5 + 4 * k]
    leaves = [[back.get(i, lambda a: a)(out[i][j]) for i in range(12)] for j in range(4)]
    return (res[0].reshape(()), gx[None], *leaves[0], *leaves[1], *leaves[2], *leaves[3])
```

```python
import math

import jax
import jax.numpy as jnp
from jax import lax
from jax.experimental import pallas as pl
from jax.experimental.pallas import tpu as pltpu

F32 = jnp.float32
BF16 = jnp.bfloat16

SEQ = 4096
D_MODEL = 1024
HEAD_DIM = 64
LANES = 128
CHUNK = 128
GMLP_W, ATTN_W, MEM_W = 256, 512, 256
IN_W = 3 * GMLP_W + 4 * ATTN_W + 2 * MEM_W
MEM_LEN = 256
DILATIONS = (16, 4, 1)
EPS = 1e-6
QK_SCALE = 1.0 / math.sqrt(HEAD_DIM)
C_GU, C_GV, C_GG, C_AQ, C_AK, C_AV, C_AG, C_MQ, C_MG = 0, 256, 512, 768, 1280, 1792, 2304, 2816, 3072

ADAM_LR, ADAM_B1, ADAM_B2, ADAM_EPS, ADAM_WD, ADAM_STEP = 0.001, 0.9, 0.999, 1e-08, 0.01, 10

VMEM_LIMIT = 48 * 1024 * 1024
RS_CHUNKS = 4
ATTN_UNROLL = 4
MESH = pl.DeviceIdType.MESH

TINY_ORDER = (0, 2, 4, 5, 6, 7, 9, 10)


def _call(body, **kw):
    return pl.pallas_call(body, **kw)


def _params(**kw):
    return pltpu.CompilerParams(vmem_limit_bytes=VMEM_LIMIT, **kw)


def _dot(a, b):
    return jnp.dot(a, b, preferred_element_type=F32)


def _dot_nt(a, b):
    return lax.dot_general(a, b, (((1,), (1,)), ((), ())), preferred_element_type=F32)


def _dot_tn(a, b):
    return lax.dot_general(a, b, (((0,), (0,)), ((), ())), preferred_element_type=F32)


def _head_blockdiag():
    r = lax.shift_right_logical(lax.broadcasted_iota(jnp.int32, (LANES, LANES), 0), 6)
    c = lax.shift_right_logical(lax.broadcasted_iota(jnp.int32, (LANES, LANES), 1), 6)
    return jnp.where(r == c, 1.0, 0.0).astype(BF16)


def _headsum(v, bd):
    hi = v.astype(BF16)
    lo = (v - hi.astype(F32)).astype(BF16)
    return _dot(hi, bd) + _dot(lo, bd)


def _lo_mask(rows):
    return lax.broadcasted_iota(jnp.int32, (rows, LANES), 1) < HEAD_DIM


def _sigmoid(x):
    return 1.0 / (1.0 + jnp.exp(-x))


def _fold_heads(v):
    return v + pltpu.roll(v, HEAD_DIM, 1)


def _put_rows(ref, vec, accumulate=False):
    for j in range(vec.shape[1] // LANES):
        piece = vec[:, j * LANES:(j + 1) * LANES]
        ref[j:j + 1, :] = ref[j:j + 1, :] + piece if accumulate else piece


def _gather_proj(x, gain, wt_sh):
    tm = 1024
    nrow = SEQ // tm
    widths = (768, 896, 768, 896)
    nunits = len(widths)
    pair = 2 * wt_sh.shape[0]
    assert pair % LANES == 0 and sum(widths[:2]) == pair

    def body(x_ref, g_ref, wt_sh_ref, proj_hbm, wt_hbm, h_scr, land, res, send_sems, recv_sems, out_sems, copy_sem):
        u, i = pl.program_id(0), pl.program_id(1)
        cx_, cy_ = lax.axis_index("x"), lax.axis_index("y")
        (send_own, pass_on_neighbours, pass_on_diagonal, _), (y_complete, x_complete, diagonal_complete, sends_done) = (
            _gather_stages((wt_sh_ref,), (land,), send_sems, recv_sems))
        first = lambda k: (u == k) & (i == 0)
        last = (u == nunits - 1) & (i == nrow - 1)
        to_hbm = pltpu.make_async_copy(land, wt_hbm, copy_sem)

        pl.when(first(0))(send_own)

        @pl.when(u == 0)
        def _():
            xv = x_ref[...]
            ms = jnp.mean(xv * xv, axis=-1, keepdims=True)
            h_scr[pl.ds(pl.multiple_of(i * tm, tm), tm), :] = (xv * lax.rsqrt(ms + EPS) * g_ref[...]).astype(BF16)

        @pl.when(first(1))
        def _():
            pass_on_neighbours()
            y_complete()

        @pl.when(first(2))
        def _():
            x_complete()
            pass_on_diagonal()

        @pl.when(first(3))
        def _():
            diagonal_complete()
            to_hbm.start()

        mine, other = pair * cx_, pair * (1 - cx_)
        col0 = (mine + 896 * cy_, mine + 768 * (1 - cy_), other + 896 * cy_, other + 768 * (1 - cy_))
        slot = i % 2
        rows = pl.ds(pl.multiple_of(i * tm, tm), tm)

        def writeback(k, rows_):
            c0 = pl.multiple_of(col0[k], LANES)
            return pltpu.make_async_copy(res.at[slot, :, pl.ds(0, widths[k])], proj_hbm.at[rows_, pl.ds(c0, widths[k])],
                                         out_sems.at[slot])

        for k in range(nunits):
            @pl.when(u == k)
            def _(k=k):
                pl.when(i >= 2)(writeback(k, rows).wait)
                if k > 0:
                    pl.when(i < 2)(writeback(k - 1, rows).wait)
                w_rows = land[pl.ds(pl.multiple_of(col0[k], LANES), widths[k]), :]
                res[slot, :, 0:widths[k]] = _dot_nt(h_scr[rows, :], w_rows)
                writeback(k, rows).start()

        @pl.when(last)
        def _():
            sends_done()
            to_hbm.wait()
            for s in range(2):
                pltpu.make_async_copy(res.at[s, :, pl.ds(0, widths[-1])], proj_hbm.at[rows, pl.ds(0, widths[-1])], out_sems.at[s]).wait()

    full = jax.ShapeDtypeStruct((4 * wt_sh.shape[0], wt_sh.shape[1]), BF16)
    hbm = pl.BlockSpec(memory_space=pl.ANY)
    return _call(
        body, name="gather_proj", grid=(nunits, nrow),
        in_specs=[pl.BlockSpec((tm, D_MODEL), lambda u, i: (jnp.where(u == 0, i, nrow - 1), 0)),
                  pl.BlockSpec((1, D_MODEL), lambda u, i: (0, 0)), pl.BlockSpec(wt_sh.shape, lambda u, i: (0, 0))],
        out_specs=[hbm, hbm],
        out_shape=[jax.ShapeDtypeStruct((SEQ, IN_W), F32), full],
        scratch_shapes=[pltpu.VMEM((SEQ, D_MODEL), BF16), pltpu.VMEM(full.shape, BF16), pltpu.VMEM((2, tm, max(widths)), F32),
                        pltpu.SemaphoreType.DMA((AG_SEMS,)), pltpu.SemaphoreType.DMA((AG_SEMS,)),
                        pltpu.SemaphoreType.DMA((2,)), pltpu.SemaphoreType.DMA],
        compiler_params=_params(),
    )(x, gain, wt_sh)


def _gmlp_weights(w_ref):
    ti = lax.broadcasted_iota(jnp.int32, (CHUNK, CHUNK), 0)
    si = lax.broadcasted_iota(jnp.int32, (CHUNK, CHUNK), 1)
    tril = si <= ti
    return tril, [jnp.where(tril, w_ref[h], 0.0).astype(BF16) for h in range(4)]


def _gmlp_fwd(proj, vgain, w_s, bias_full):
    tm = 1024

    def body(p_ref, vg_ref, w_ref, b_ref, y_ref):
        bd = _head_blockdiag()
        lo = _lo_mask(CHUNK)
        _, wm = _gmlp_weights(w_ref)
        units = [(pl.ds(c * CHUNK, CHUNK), p) for c in range(tm // CHUNK) for p in range(2)]
        col = lambda c0, p: slice(c0 + p * LANES, c0 + (p + 1) * LANES)
        vs = [p_ref[rows, col(C_GV, p)] for rows, p in units]
        rs = [lax.rsqrt(_headsum(v * v, bd) * (1.0 / HEAD_DIM) + EPS) for v in vs]
        vns = [(v * r * vg_ref[:, col(0, p)]).astype(BF16) for v, r, (_, p) in zip(vs, rs, units)]
        sps = [jnp.where(lo, _dot(wm[2 * p], vn), _dot(wm[2 * p + 1], vn)) + b_ref[:, col(0, p)] for vn, (_, p) in zip(vns, units)]
        for sp, (rows, p) in zip(sps, units):
            gt = p_ref[rows, col(C_GG, p)]
            y_ref[rows, col(0, p)] = (p_ref[rows, col(C_GU, p)] * sp * (gt * _sigmoid(gt))).astype(BF16)

    return _call(
        body, name="gmlp_fwd", grid=(SEQ // tm,),
        in_specs=[pl.BlockSpec((tm, 3 * GMLP_W), lambda i: (i, 0)),
                  pl.BlockSpec((1, GMLP_W), lambda i: (0, 0)),
                  pl.BlockSpec((4, CHUNK, CHUNK), lambda i: (0, 0, 0)),
                  pl.BlockSpec((CHUNK, GMLP_W), lambda i: (0, 0))],
        out_specs=pl.BlockSpec((tm, GMLP_W), lambda i: (i, 0)),
        out_shape=jax.ShapeDtypeStruct((SEQ, GMLP_W), BF16),
        compiler_params=_params(),
    )(proj, vgain, w_s, bias_full)


def _gmlp_bwd(proj, dyc, vgain, w_s, bias_full):
    tm = 1024
    nsteps = SEQ // tm

    def body(p_ref, dy_ref, vg_ref, w_ref, b_ref, dg_ref, gw_ref, gb_ref, gv_ref):
        i = pl.program_id(0)
        bd = _head_blockdiag()
        lo = _lo_mask(CHUNK)
        tril, wm = _gmlp_weights(w_ref)
        ri = lax.broadcasted_iota(jnp.int32, (16, LANES), 0)
        li = lax.broadcasted_iota(jnp.int32, (16, LANES), 1)
        head_rows = [jnp.where(((ri == 2 * p) & (li < HEAD_DIM)) | ((ri == 2 * p + 1) & (li >= HEAD_DIM)), 1.0, 0.0).astype(BF16)
                     for p in range(2)]

        @pl.when(i == 0)
        def _():
            gw_ref[...] = jnp.zeros_like(gw_ref)
            gb_ref[...] = jnp.zeros_like(gb_ref)
            gv_ref[...] = jnp.zeros_like(gv_ref)

        units = [(pl.ds(c * CHUNK, CHUNK), p) for c in range(tm // CHUNK) for p in range(2)]
        col = lambda c0, p: slice(c0 + p * LANES, c0 + (p + 1) * LANES)
        vs = [p_ref[rows, col(C_GV, p)] for rows, p in units]
        rs = [lax.rsqrt(_headsum(v * v, bd) * (1.0 / HEAD_DIM) + EPS) for v in vs]
        zs = [v * r for v, r in zip(vs, rs)]
        vns = [(z * vg_ref[:, col(0, p)]).astype(BF16) for z, (_, p) in zip(zs, units)]
        sps = [jnp.where(lo, _dot(wm[2 * p], vn), _dot(wm[2 * p + 1], vn)) + b_ref[:, col(0, p)] for vn, (_, p) in zip(vns, units)]
        dsps = []
        for sp, (rows, p) in zip(sps, units):
            u = p_ref[rows, col(C_GU, p)]
            gt = p_ref[rows, col(C_GG, p)]
            dy = dy_ref[rows, col(0, p)]
            sg = _sigmoid(gt)
            sl = gt * sg
            dg_ref[rows, col(C_GU, p)] = (dy * sp * sl).astype(BF16)
            dg_ref[rows, col(C_GG, p)] = (dy * u * sp * (sg * (1.0 + gt * (1.0 - sg)))).astype(BF16)
            dsps.append(dy * u * sl)
        dspbs = [dsp.astype(BF16) for dsp in dsps]
        dvns = [jnp.where(lo, _dot_tn(wm[2 * p], dspb), _dot_tn(wm[2 * p + 1], dspb)) for dspb, (_, p) in zip(dspbs, units)]
        gws = [(_dot_nt(jnp.where(lo, dsp, 0.0).astype(BF16), vn), _dot_nt(jnp.where(lo, 0.0, dsp).astype(BF16), vn))
               for dsp, vn in zip(dsps, vns)]
        gbs = [(_dot_nt(head_rows[p], dspb) + _dot_nt(head_rows[p], (dsp - dspb.astype(F32)).astype(BF16)))[0:8]
               for dsp, dspb, (_, p) in zip(dsps, dspbs, units)]
        for p in range(2):
            mine = [n for n, (_, q) in enumerate(units) if q == p]
            gw_ref[2 * p] += sum(gws[n][0] for n in mine)
            gw_ref[2 * p + 1] += sum(gws[n][1] for n in mine)
            gvp = sum(jnp.sum(dvns[n] * zs[n], axis=0, keepdims=True) for n in mine)
            gv_ref[2 * p:2 * p + 1, :] += gvp
            gv_ref[2 * p + 1:2 * p + 2, :] += pltpu.roll(gvp, HEAD_DIM, 1)
        gb_ref[...] += sum(gbs)
        for dvn, z, r, (rows, p) in zip(dvns, zs, rs, units):
            dz = dvn * vg_ref[:, col(0, p)]
            dg_ref[rows, col(C_GV, p)] = (r * (dz - z * (_headsum(dz * z, bd) * (1.0 / HEAD_DIM)))).astype(BF16)

        @pl.when(i == nsteps - 1)
        def _():
            for h in range(4):
                gw_ref[h] = jnp.where(tril, gw_ref[h], 0.0)

    return _call(
        body, name="gmlp_bwd", grid=(nsteps,),
        in_specs=[pl.BlockSpec((tm, 3 * GMLP_W), lambda i: (i, 0)),
                  pl.BlockSpec((tm, GMLP_W), lambda i: (i, 0)),
                  pl.BlockSpec((1, GMLP_W), lambda i: (0, 0)),
                  pl.BlockSpec((4, CHUNK, CHUNK), lambda i: (0, 0, 0)),
                  pl.BlockSpec((CHUNK, GMLP_W), lambda i: (0, 0))],
        out_specs=[pl.BlockSpec((tm, 3 * GMLP_W), lambda i: (i, 0)),
                   pl.BlockSpec((4, CHUNK, CHUNK), lambda i: (0, 0, 0)),
                   pl.BlockSpec((8, LANES), lambda i: (0, 0)),
                   pl.BlockSpec((8, LANES), lambda i: (0, 0))],
        out_shape=[jax.ShapeDtypeStruct((SEQ, 3 * GMLP_W), BF16),
                   jax.ShapeDtypeStruct((4, CHUNK, CHUNK), F32),
                   jax.ShapeDtypeStruct((8, LANES), F32),
                   jax.ShapeDtypeStruct((8, LANES), F32)],
        compiler_params=_params(),
    )(proj, dyc, vgain, w_s, bias_full)


def _band_masks():
    qi = lax.broadcasted_iota(jnp.int32, (CHUNK, 2 * CHUNK), 0)
    kj = lax.broadcasted_iota(jnp.int32, (CHUNK, 2 * CHUNK), 1)
    valid2 = ((kj < CHUNK) & (kj >= qi)) | ((kj >= CHUNK) & (kj - CHUNK <= qi))
    q1 = lax.broadcasted_iota(jnp.int32, (CHUNK, CHUNK), 0)
    k1 = lax.broadcasted_iota(jnp.int32, (CHUNK, CHUNK), 1)
    return k1 <= q1, valid2


def _stack_heads(v, lo):
    return jnp.concatenate([jnp.where(lo, v, 0.0), jnp.where(lo, 0.0, v)], axis=0).astype(BF16)


def _rows_of(ref, start, d):
    if d == 1:
        return ref.at[pl.ds(start if isinstance(start, int) else pl.multiple_of(start, CHUNK), CHUNK), :]
    return ref.at[pl.ds(start, CHUNK, stride=d), :]


def _unrolled(lo, hi, unroll, run):
    groups = (hi - lo) // unroll
    if groups:
        def body(g, carry):
            run([lo + g * unroll + t for t in range(unroll)])
            return carry

        lax.fori_loop(0, groups, body, 0)
    if lo + groups * unroll < hi:
        run(range(lo + groups * unroll, hi))


def _for_blocks(d, group_fn, unroll):
    nblk = SEQ // CHUNK
    sh = d.bit_length() - 1

    def first(j):
        return (j * CHUNK if d == 1 else j, None)

    def rest(j):
        start = (j & (d - 1)) + (j >> sh) * (CHUNK * d)
        return (start, start - CHUNK * d)

    _unrolled(0, d, unroll, lambda js: group_fn(d, [first(j) for j in js]))
    _unrolled(d, nblk, unroll, lambda js: group_fn(d, [rest(j) for j in js]))


def _attn_fwd(proj, gq2, gk2, *ride_along):
    tn = 512
    npairs = ATTN_W // LANES
    nride = len(ride_along)

    def body(q_ref, k_ref, v_ref, g_ref, gq_ref, gk_ref, *rest):
        shards, rest = rest[:nride], rest[nride:]
        o_ref, l_ref, ya_ref = rest[:3]
        gathered, rest = rest[3:3 + nride], rest[3 + nride:]
        qn_ref, kn_ref = rest[:2]
        lands, (send_sems, recv_sems, copy_sems) = rest[2:2 + nride], rest[2 + nride:]
        pair = pl.program_id(0)
        ride = _gather_stages(shards, lands, send_sems, recv_sems)[0]
        for step in range(npairs):
            pl.when(pair == step)(ride[step])
        bd = _head_blockdiag()
        lo = _lo_mask(CHUNK)
        valid1, valid2 = _band_masks()

        def norm(t, carry):
            rows = pl.ds(pl.multiple_of(t * tn, tn), tn)
            q, k = q_ref[rows, :], k_ref[rows, :]
            ssq = [_headsum(a * a, bd) for a in (q, k)]
            qn_ref[rows, :] = q * lax.rsqrt(ssq[0] * (1.0 / HEAD_DIM) + EPS) * (gq_ref[...] * QK_SCALE)
            kn_ref[rows, :] = k * lax.rsqrt(ssq[1] * (1.0 / HEAD_DIM) + EPS) * gk_ref[...]
            return carry

        lax.fori_loop(0, SEQ // tn, norm, 0)

        def load_kv(ref, d, start, prev):
            own = _rows_of(ref, start, d)[...]
            if prev is None:
                return own.astype(BF16)
            return jnp.concatenate([_rows_of(ref, prev, d)[...], own], axis=0).astype(BF16)

        def group(d, blocks):
            valid = valid1 if blocks[0][1] is None else valid2
            valid = jnp.concatenate([valid, valid], axis=0)
            qs = [_rows_of(qn_ref, start, d)[...] for start, _ in blocks]
            ks = [load_kv(kn_ref, d, start, prev) for start, prev in blocks]
            vs = [load_kv(v_ref, d, start, prev) for start, prev in blocks]
            ss = [_dot_nt(_stack_heads(q, lo), k) for q, k in zip(qs, ks)]
            ms, ps, ls = [], [], []
            for s in ss:
                s = jnp.where(valid, s, -jnp.inf)
                m = jnp.max(s, axis=-1, keepdims=True)
                p = jnp.exp(s - m)
                ms.append(m)
                ls.append(jnp.sum(p, axis=-1, keepdims=True))
                ps.append(p.astype(BF16))
            os_ = [_dot(p, v) for p, v in zip(ps, vs)]
            for b, (start, _) in enumerate(blocks):
                heads = lambda v: jnp.where(lo, v[:CHUNK], v[CHUNK:])
                lsum = heads(ls[b])
                ob = heads(os_[b]) * (1.0 / lsum)
                lb = heads(ms[b]) + jnp.log(lsum)
                o_rows = _rows_of(o_ref, start, d)
                l_rows = _rows_of(l_ref, start, d)
                if d != DILATIONS[0]:
                    lold = l_rows[...]
                    mx = jnp.maximum(lold, lb)
                    ea = jnp.exp(lold - mx)
                    eb = jnp.exp(lb - mx)
                    inv = 1.0 / (ea + eb)
                    ob = o_rows[...] * (ea * inv) + ob * (eb * inv)
                    lb = mx + jnp.log(ea + eb)
                o_rows[...] = ob
                l_rows[...] = lb

        for d in DILATIONS:
            _for_blocks(d, group, ATTN_UNROLL)

        def fin(t, carry):
            rows = pl.ds(pl.multiple_of(t * tn, tn), tn)
            g = g_ref[rows, :]
            ya_ref[rows, :] = (o_ref[rows, :] * (g * _sigmoid(g))).astype(BF16)
            return carry

        lax.fori_loop(0, SEQ // tn, fin, 0)

        @pl.when(pair == npairs - 1)
        def _():
            to_hbm = [pltpu.make_async_copy(land, out, copy_sems.at[n]) for n, (land, out) in enumerate(zip(lands, gathered))]
            for cp in to_hbm:
                cp.start()
            for cp in to_hbm:
                cp.wait()

    col = lambda c0: pl.BlockSpec((SEQ, LANES), lambda p: (0, c0 // LANES + p))
    vec = pl.BlockSpec((1, LANES), lambda p: (0, 0))
    out = pl.BlockSpec((SEQ, LANES), lambda p: (0, p))
    full = [jax.ShapeDtypeStruct((4 * a.shape[0], a.shape[1]), BF16) for a in ride_along]
    return _call(
        body, name="attn_fwd", grid=(npairs,),
        in_specs=[col(C_AQ), col(C_AK), col(C_AV), col(C_AG), vec, vec]
        + [pl.BlockSpec(a.shape, lambda p: (0, 0)) for a in ride_along],
        out_specs=[out, out, out] + [pl.BlockSpec(memory_space=pl.ANY)] * nride,
        out_shape=[jax.ShapeDtypeStruct((SEQ, ATTN_W), F32), jax.ShapeDtypeStruct((SEQ, ATTN_W), F32),
                   jax.ShapeDtypeStruct((SEQ, ATTN_W), BF16)] + full,
        scratch_shapes=[pltpu.VMEM((SEQ, LANES), F32), pltpu.VMEM((SEQ, LANES), F32)]
        + [pltpu.VMEM(s.shape, BF16) for s in full]
        + [pltpu.SemaphoreType.DMA((AG_SEMS * nride,)), pltpu.SemaphoreType.DMA((AG_SEMS * nride,)),
           pltpu.SemaphoreType.DMA((nride,))],
        compiler_params=_params(),
    )(proj, proj, proj, proj, gq2, gk2, *ride_along)


def _attn_bwd(proj, o, lse, dyc, gq2, gk2, *ride_along):
    tn = 512
    npairs = ATTN_W // LANES
    nride = len(ride_along)
    nbufs = nride * len(RS_KINDS)

    def body(proj_hbm, o_hbm, l_hbm, dyc_hbm, gq_ref, gk_ref, *rest):
        ride_in, rest = rest[:nride], rest[nride:]
        dq_ref, dk_ref, dv_ref, dgt_ref, gqg_ref, gkg_ref = rest[:6]
        ride_out, rest = rest[6:6 + nride], rest[6 + nride:]
        qb_, kb_, vb_, gb_, ob_, lb_, yb_, dkb_, dvb_, sems = rest[:10]
        rs_bufs, (send_sems, recv_sems, local_sems) = rest[10:10 + nbufs], rest[10 + nbufs:]
        rs_stage = _rs_stages(ride_in, ride_out, rs_bufs, send_sems, recv_sems, local_sems, [g.shape[1] for g in ride_along])
        pair = pl.program_id(0)
        for step in range(npairs):
            pl.when(pair == step)(rs_stage[step])
        bd = _head_blockdiag()
        lo = _lo_mask(CHUNK)
        lo2 = lax.broadcasted_iota(jnp.int32, (2 * CHUNK, LANES), 1) < HEAD_DIM
        valid1, valid2 = _band_masks()
        gqs = gq_ref[...] * QK_SCALE
        gk = gk_ref[...]

        def pcol(c0, of=None):
            return acol(proj_hbm, c0, of)

        def acol(hbm, c0=0, of=None):
            of = pair if of is None else of
            return hbm.at[:, pl.ds(pl.multiple_of(c0 + of * LANES, LANES), LANES)]

        def input_loads(of):
            return [pltpu.make_async_copy(src, dst, sems.at[n]) for n, (src, dst) in enumerate((
                (pcol(C_AQ, of), qb_), (pcol(C_AK, of), kb_), (pcol(C_AG, of), gb_), (acol(o_hbm, 0, of), ob_),
                (acol(dyc_hbm, GMLP_W, of), yb_), (pcol(C_AV, of), vb_), (acol(l_hbm, 0, of), lb_)))]

        early = (0, 1, 3, 4)
        loads = input_loads(pair)
        for n, cp in enumerate(loads):
            if n in early:
                pl.when(pair == 0)(cp.start)
            else:
                cp.start()

        @pl.when(pair == 0)
        def _():
            gqg_ref[...] = jnp.zeros_like(gqg_ref)
            gkg_ref[...] = jnp.zeros_like(gkg_ref)

        def pre_qk(t, carry):
            rows = pl.ds(pl.multiple_of(t * tn, tn), tn)
            q, k = qb_[rows, :], kb_[rows, :]
            ssq = [_headsum(a * a, bd) for a in (q, k)]
            qb_[rows, :] = q * lax.rsqrt(ssq[0] * (1.0 / HEAD_DIM) + EPS) * gqs
            kb_[rows, :] = k * lax.rsqrt(ssq[1] * (1.0 / HEAD_DIM) + EPS) * gk
            return carry

        def pre_gate(t, carry):
            rows = pl.ds(pl.multiple_of(t * tn, tn), tn)
            g = gb_[rows, :]
            ov = ob_[rows, :]
            dya = yb_[rows, :]
            sg = _sigmoid(g)
            dgt_ref[rows, :] = (dya * ov * (sg * (1.0 + g * (1.0 - sg)))).astype(BF16)
            do = dya * (g * sg)
            yb_[rows, :] = do
            ob_[rows, :] = jnp.where(first_half, lb_[rows, :], _headsum(do * ov, bd))
            return carry

        first_half = (lax.broadcasted_iota(jnp.int32, (tn, LANES), 1) & (HEAD_DIM - 1)) < HEAD_DIM // 2
        loads[0].wait()
        loads[1].wait()
        lax.fori_loop(0, SEQ // tn, pre_qk, 0)
        for cp in loads[2:5] + loads[6:7]:
            cp.wait()
        lax.fori_loop(0, SEQ // tn, pre_gate, 0)
        loads[5].wait()
        reloads = [pltpu.make_async_copy(pcol(C_AQ), lb_, sems.at[7]), pltpu.make_async_copy(pcol(C_AK), vb_, sems.at[8])]
        reloads[0].start()

        def load_kv(ref, d, start, prev):
            own = _rows_of(ref, start, d)[...]
            if prev is None:
                return own.astype(BF16)
            return jnp.concatenate([_rows_of(ref, prev, d)[...], own], axis=0).astype(BF16)

        def group(d, blocks):
            first = blocks[0][1] is None
            valid, lok = (valid1, lo) if first else (valid2, lo2)
            chains = [(b, h) for b in range(len(blocks)) for h in range(2)]
            mask = lambda h: lo if h == 0 else ~lo
            qs = [_rows_of(qb_, start, d)[...] for start, _ in blocks]
            dos = [_rows_of(yb_, start, d)[...] for start, _ in blocks]
            lds = [_rows_of(ob_, start, d)[...] for start, _ in blocks]
            ks = [load_kv(kb_, d, start, prev) for start, prev in blocks]
            vs = [load_kv(vb_, d, start, prev) for start, prev in blocks]
            qbs = [q.astype(BF16) for q in qs]
            dobs = [do.astype(BF16) for do in dos]
            ss = [_dot_nt(jnp.where(mask(h), qs[b], 0.0).astype(BF16), ks[b]) for b, h in chains]
            dps = [_dot_nt(jnp.where(mask(h), dos[b], 0.0).astype(BF16), vs[b]) for b, h in chains]
            pbs, dss = [], []
            for s, dp, (b, h) in zip(ss, dps, chains):
                hc, dc = h * HEAD_DIM, h * HEAD_DIM + HEAD_DIM // 2
                p = jnp.exp(jnp.where(valid, s, -jnp.inf) - lds[b][:, hc:hc + 1])
                pbs.append(p.astype(BF16))
                dss.append((p * (dp - lds[b][:, dc:dc + 1])).astype(BF16))
            dqs = [_dot(ds, ks[b]) for ds, (b, h) in zip(dss, chains)]
            dks = [_dot_tn(ds, qbs[b]) for ds, (b, h) in zip(dss, chains)]
            dvs = [_dot_tn(p, dobs[b]) for p, (b, h) in zip(pbs, chains)]
            assign = d == DILATIONS[0]
            for b, (start, prev) in enumerate(blocks):
                c0, c1 = 2 * b, 2 * b + 1
                dq_rows = _rows_of(gb_, start, d)
                dqb = jnp.where(lo, dqs[c0], dqs[c1])
                dq_rows[...] = dqb if assign else dq_rows[...] + dqb
                dkc = jnp.where(lok, dks[c0], dks[c1])
                dvc = jnp.where(lok, dvs[c0], dvs[c1])
                spans = ((start, slice(0, CHUNK), True),) if first else (
                    (prev, slice(0, CHUNK), False), (start, slice(CHUNK, 2 * CHUNK), True))
                for st, sl, own in spans:
                    dk_rows = _rows_of(dkb_, st, d)
                    dv_rows = _rows_of(dvb_, st, d)
                    if assign and own:
                        dk_rows[...] = dkc[sl]
                        dv_rows[...] = dvc[sl]
                    else:
                        dk_rows[...] = dk_rows[...] + dkc[sl]
                        dv_rows[...] = dv_rows[...] + dvc[sl]

        for d in DILATIONS:
            _for_blocks(d, group, ATTN_UNROLL)

        reloads[1].start()

        @pl.when(pair < npairs - 1)
        def _():
            nxt = input_loads(pair + 1)
            for n in early:
                nxt[n].start()

        for cp in reloads:
            cp.wait()

        def post(t, carry):
            gq_acc, gk_acc = carry
            rows = pl.ds(pl.multiple_of(t * tn, tn), tn)
            raws = [lb_[rows, :], vb_[rows, :]]
            dns = [gb_[rows, :], dkb_[rows, :]]
            rs = [lax.rsqrt(_headsum(a * a, bd) * (1.0 / HEAD_DIM) + EPS) for a in raws]
            zs = [a * r for a, r in zip(raws, rs)]
            dzs = [dn * gain for dn, gain in zip(dns, (gqs, gk))]
            means = [_headsum(dz * z, bd) * (1.0 / HEAD_DIM) for dz, z in zip(dzs, zs)]
            dq, dk = [r * (dz - z * mean) for r, dz, z, mean in zip(rs, dzs, zs, means)]
            gq, gkk = [jnp.sum(dn * z, axis=0, keepdims=True) for dn, z in zip(dns, zs)]
            dq_ref[rows, :] = dq.astype(BF16)
            dk_ref[rows, :] = dk.astype(BF16)
            dv_ref[rows, :] = dvb_[rows, :].astype(BF16)
            return gq_acc + gq * QK_SCALE, gk_acc + gkk

        zero = jnp.zeros((1, LANES), F32)
        gq_acc, gk_acc = lax.fori_loop(0, SEQ // tn, post, (zero, zero))
        gqg_ref[0:1, :] += gq_acc
        gkg_ref[0:1, :] += gk_acc

        @pl.when(pair == npairs - 1)
        def _():
            gqg_ref[0:1, :] = _fold_heads(gqg_ref[0:1, :])
            gkg_ref[0:1, :] = _fold_heads(gkg_ref[0:1, :])
            rs_stage[npairs]()

    hbm = pl.BlockSpec(memory_space=pl.ANY)
    vec = pl.BlockSpec((1, LANES), lambda p: (0, 0))
    blk8 = pl.BlockSpec((8, LANES), lambda p: (0, 0))
    out = pl.BlockSpec((SEQ, LANES), lambda p: (0, p))
    big = jax.ShapeDtypeStruct((SEQ, ATTN_W), BF16)
    nsem = RS_SEMS * nride
    return _call(
        body, name="attn_bwd", grid=(npairs,),
        in_specs=[hbm, hbm, hbm, hbm, vec, vec] + [hbm] * nride,
        out_specs=[out, out, out, out, blk8, blk8] + [hbm] * nride,
        out_shape=[big, big, big, big, jax.ShapeDtypeStruct((8, LANES), F32), jax.ShapeDtypeStruct((8, LANES), F32)]
        + [jax.ShapeDtypeStruct((2, g.shape[0] // 8, g.shape[1]), F32) for g in ride_along],
        scratch_shapes=[pltpu.VMEM((SEQ, LANES), F32) for _ in range(9)] + [pltpu.SemaphoreType.DMA((9,))]
        + _rs_scratch([g.shape for g in ride_along]) + [pltpu.SemaphoreType.DMA((nsem,)), pltpu.SemaphoreType.DMA((nsem,)),
                                     pltpu.SemaphoreType.DMA((nride,))],
        compiler_params=_params(),
    )(proj, o, lse, dyc, gq2, gk2, *[_rs_view(g) for g in ride_along])


def _mem_kv(mem, gain, wkv):
    def body(m_ref, g_ref, w_ref, kv_ref, hm_ref):
        mv = m_ref[...]
        ms = jnp.mean(mv * mv, axis=-1, keepdims=True)
        hm = (mv * lax.rsqrt(ms + EPS) * g_ref[...]).astype(BF16)
        hm_ref[...] = hm
        kv_ref[...] = _dot(hm, w_ref[...])

    return _call(
        body, name="mem_kv",
        out_shape=[jax.ShapeDtypeStruct((MEM_LEN, 2 * MEM_W), F32), jax.ShapeDtypeStruct((MEM_LEN, D_MODEL), BF16)],
        compiler_params=_params(),
    )(mem, gain, wkv)


def _mem_keys(kv_ref, kg_ref, bd, p):
    mk = kv_ref[:, p * LANES:(p + 1) * LANES]
    r = lax.rsqrt(_headsum(mk * mk, bd) * (1.0 / HEAD_DIM) + EPS)
    z = mk * r
    mkn = (z * kg_ref[:, p * LANES:(p + 1) * LANES]).astype(BF16)
    mvp = kv_ref[:, MEM_W + p * LANES:MEM_W + (p + 1) * LANES].astype(BF16)
    return mkn, mvp, r, z


def _mem_fwd(proj, kv, qg4, kg4):
    tm = 1024

    def body(q_ref, g_ref, kv_ref, qg_ref, kg_ref, om_ref, ym_ref):
        bd = _head_blockdiag()
        lo = _lo_mask(tm)
        keys, qns = [], []
        for p in range(2):
            cs = slice(p * LANES, (p + 1) * LANES)
            keys.append(_mem_keys(kv_ref, kg_ref, bd, p)[:2])
            q = q_ref[:, cs]
            qns.append(q * lax.rsqrt(_headsum(q * q, bd) * (1.0 / HEAD_DIM) + EPS) * (qg_ref[:, cs] * QK_SCALE))
        chains = [(p, h) for p in range(2) for h in range(2)]
        ss = [_dot_nt(jnp.where(lo if h == 0 else ~lo, qns[p], 0.0).astype(BF16), keys[p][0]) for p, h in chains]
        es = [jnp.exp(s - jnp.max(s, axis=-1, keepdims=True)) for s in ss]
        os_ = [_dot(e.astype(BF16), keys[p][1]) for e, (p, h) in zip(es, chains)]
        res = [o * (1.0 / jnp.sum(e, axis=-1, keepdims=True)) for o, e in zip(os_, es)]
        for p in range(2):
            cs = slice(p * LANES, (p + 1) * LANES)
            ov = jnp.where(lo, res[2 * p], res[2 * p + 1])
            g = g_ref[:, cs]
            om_ref[:, cs] = ov
            ym_ref[:, cs] = (ov * (g * _sigmoid(g))).astype(BF16)

    vec = pl.BlockSpec((1, MEM_W), lambda i: (0, 0))
    return _call(
        body, name="mem_fwd", grid=(SEQ // tm,),
        in_specs=[pl.BlockSpec((tm, MEM_W), lambda i: (i, C_MQ // MEM_W)),
                  pl.BlockSpec((tm, MEM_W), lambda i: (i, C_MG // MEM_W)),
                  pl.BlockSpec((MEM_LEN, 2 * MEM_W), lambda i: (0, 0)), vec, vec],
        out_specs=[pl.BlockSpec((tm, MEM_W), lambda i: (i, 0)), pl.BlockSpec((tm, MEM_W), lambda i: (i, 0))],
        out_shape=[jax.ShapeDtypeStruct((SEQ, MEM_W), F32), jax.ShapeDtypeStruct((SEQ, MEM_W), BF16)],
        compiler_params=_params(),
    )(proj, proj, kv, qg4, kg4)


def _mem_bwd(proj, om, dyc, kv, hm, mem, mgain, wkv, qg4, kg4):
    tm = 1024
    nsteps = SEQ // tm

    def body(q_ref, g_ref, om_ref, dy_ref, kv_ref, hm_ref, mem_ref, mg_ref, w_ref, qg_ref, kg_ref,
             dq_ref, dgt_ref, gqg_ref, gkg_ref, gw_ref, gmg_ref, dmk_ref, dmv_ref, gq_acc):
        i = pl.program_id(0)
        bd = _head_blockdiag()
        lo = _lo_mask(tm)
        lom = _lo_mask(MEM_LEN)

        @pl.when(i == 0)
        def _():
            dmk_ref[...] = jnp.zeros_like(dmk_ref)
            dmv_ref[...] = jnp.zeros_like(dmv_ref)
            gq_acc[...] = jnp.zeros_like(gq_acc)

        pairs = []
        for p in range(2):
            cs = slice(p * LANES, (p + 1) * LANES)
            mkn, mvp, _, _ = _mem_keys(kv_ref, kg_ref, bd, p)
            gqs = qg_ref[:, cs] * QK_SCALE
            q = q_ref[:, cs]
            r = lax.rsqrt(_headsum(q * q, bd) * (1.0 / HEAD_DIM) + EPS)
            z = q * r
            qn = z * gqs
            g = g_ref[:, cs]
            ov = om_ref[:, cs]
            dym = dy_ref[:, cs]
            sg = _sigmoid(g)
            dgt_ref[:, cs] = (dym * ov * (sg * (1.0 + g * (1.0 - sg)))).astype(BF16)
            do = dym * (g * sg)
            pairs.append(dict(cs=cs, mkn=mkn, mvp=mvp, gqs=gqs, r=r, z=z, qn=qn, qnb=qn.astype(BF16), do=do,
                              dob=do.astype(BF16), delta=_headsum(do * ov, bd)))
        chains = [(pr_, h) for pr_ in pairs for h in range(2)]
        mask = lambda h: lo if h == 0 else ~lo
        ss = [_dot_nt(jnp.where(mask(h), c["qn"], 0.0).astype(BF16), c["mkn"]) for c, h in chains]
        dps = [_dot_nt(jnp.where(mask(h), c["do"], 0.0).astype(BF16), c["mvp"]) for c, h in chains]
        prs, dss = [], []
        for s, dp, (c, h) in zip(ss, dps, chains):
            e = jnp.exp(s - jnp.max(s, axis=-1, keepdims=True))
            pr = e * (1.0 / jnp.sum(e, axis=-1, keepdims=True))
            prs.append(pr.astype(BF16))
            dss.append((pr * (dp - c["delta"][:, h * HEAD_DIM:h * HEAD_DIM + 1])).astype(BF16))
        dqs = [_dot(ds, c["mkn"]) for ds, (c, h) in zip(dss, chains)]
        dks = [_dot_tn(ds, c["qnb"]) for ds, (c, h) in zip(dss, chains)]
        dvs = [_dot_tn(pr, c["dob"]) for pr, (c, h) in zip(prs, chains)]
        for p, c in enumerate(pairs):
            cs, z, r = c["cs"], c["z"], c["r"]
            dqn = jnp.where(lo, dqs[2 * p], dqs[2 * p + 1])
            dmk_ref[:, cs] += jnp.where(lom, dks[2 * p], dks[2 * p + 1])
            dmv_ref[:, cs] += jnp.where(lom, dvs[2 * p], dvs[2 * p + 1])
            dz = dqn * c["gqs"]
            dq_ref[:, cs] = (r * (dz - z * (_headsum(dz * z, bd) * (1.0 / HEAD_DIM)))).astype(BF16)
            gq_acc[:, cs] += jnp.sum(dqn * z, axis=0, keepdims=True) * QK_SCALE

        @pl.when(i == nsteps - 1)
        def _():
            gqg_ref[...] = jnp.zeros_like(gqg_ref)
            gkg_ref[...] = jnp.zeros_like(gkg_ref)
            gqg_ref[0:1, :] = _fold_heads(gq_acc[:, 0:LANES] + gq_acc[:, LANES:2 * LANES])
            dkv = []
            gk = jnp.zeros((1, LANES), F32)
            for p in range(2):
                cs = slice(p * LANES, (p + 1) * LANES)
                _, _, r, z = _mem_keys(kv_ref, kg_ref, bd, p)
                dn = dmk_ref[:, cs]
                dz = dn * kg_ref[:, cs]
                gk = gk + jnp.sum(dn * z, axis=0, keepdims=True)
                dkv.append(r * (dz - z * (_headsum(dz * z, bd) * (1.0 / HEAD_DIM))))
            gkg_ref[0:1, :] = _fold_heads(gk)
            dkvb = jnp.concatenate(dkv + [dmv_ref[...]], axis=1).astype(BF16)
            gw_ref[...] = _dot_tn(hm_ref[...], dkvb)
            dhm = _dot_nt(dkvb, w_ref[...])
            mv = mem_ref[...]
            zm = mv * lax.rsqrt(jnp.mean(mv * mv, axis=-1, keepdims=True) + EPS)
            _put_rows(gmg_ref, jnp.sum(dhm * zm, axis=0, keepdims=True))

    const = lambda shape: pl.BlockSpec(shape, lambda i: (0,) * len(shape))
    row = lambda j: pl.BlockSpec((tm, MEM_W), lambda i: (i, j))
    blk8 = jax.ShapeDtypeStruct((8, LANES), F32)
    return _call(
        body, name="mem_bwd", grid=(nsteps,),
        in_specs=[row(C_MQ // MEM_W), row(C_MG // MEM_W), row(0), row((GMLP_W + ATTN_W) // MEM_W),
                  const((MEM_LEN, 2 * MEM_W)), const((MEM_LEN, D_MODEL)), const((MEM_LEN, D_MODEL)),
                  const((1, D_MODEL)), const((D_MODEL, 2 * MEM_W)), const((1, MEM_W)), const((1, MEM_W))],
        out_specs=[row(0), row(0), const((8, LANES)), const((8, LANES)),
                   const((D_MODEL, 2 * MEM_W)), const((8, LANES))],
        out_shape=[jax.ShapeDtypeStruct((SEQ, MEM_W), BF16), jax.ShapeDtypeStruct((SEQ, MEM_W), BF16),
                   blk8, blk8, jax.ShapeDtypeStruct((D_MODEL, 2 * MEM_W), F32), blk8],
        scratch_shapes=[pltpu.VMEM((MEM_LEN, MEM_W), F32), pltpu.VMEM((MEM_LEN, MEM_W), F32),
                        pltpu.VMEM((1, MEM_W), F32)],
        compiler_params=_params(),
    )(proj, proj, om, dyc, kv, hm, mem, mgain, wkv, qg4, kg4)


def _out_loss(yg, ya, ym, x, tgt, wo):
    tm = 512
    nsteps = SEQ // tm
    parts = ((0, GMLP_W), (GMLP_W, ATTN_W), (GMLP_W + ATTN_W, MEM_W))

    def body(yg_ref, ya_ref, ym_ref, x_ref, t_ref, w_ref, dy_ref, dyc_ref, gw_ref, ls_ref):
        i = pl.program_id(0)

        @pl.when(i == 0)
        def _():
            gw_ref[...] = jnp.zeros_like(gw_ref)
            ls_ref[...] = jnp.zeros_like(ls_ref)

        ys = (yg_ref[...], ya_ref[...], ym_ref[...])
        y = sum(_dot(yv, w_ref[r0:r0 + n, :]) for yv, (r0, n) in zip(ys, parts))
        err = x_ref[...] + y - t_ref[...]
        _put_rows(ls_ref, jnp.sum(err * err, axis=0, keepdims=True), accumulate=True)
        dy = err * (1.0 / D_MODEL)
        dy_ref[...] = dy
        dyb = dy.astype(BF16)
        dyc_ref[...] = _dot_nt(dyb, w_ref[...])
        for yv, (r0, n) in zip(ys, parts):
            gw_ref[r0:r0 + n, :] += _dot_tn(yv, dyb)

    row = lambda w: pl.BlockSpec((tm, w), lambda i: (i, 0))
    const = lambda shape: pl.BlockSpec(shape, lambda i: (0, 0))
    return _call(
        body, name="out_loss", grid=(nsteps,),
        in_specs=[row(GMLP_W), row(ATTN_W), row(MEM_W), row(D_MODEL), row(D_MODEL), const((D_MODEL, D_MODEL))],
        out_specs=[row(D_MODEL), row(D_MODEL), const((D_MODEL, D_MODEL)), const((8, LANES))],
        out_shape=[jax.ShapeDtypeStruct((SEQ, D_MODEL), F32), jax.ShapeDtypeStruct((SEQ, D_MODEL), F32),
                   jax.ShapeDtypeStruct((D_MODEL, D_MODEL), F32), jax.ShapeDtypeStruct((8, LANES), F32)],
        compiler_params=_params(),
    )(yg, ya, ym, x, tgt, wo)


def _proj_bwd(x, dy, gain, wt, dg, daq, dak, dav, dag, dmq, dmg):
    tm = 512
    nsteps = SEQ // tm
    pieces = ((C_GU, 3 * GMLP_W), (C_AQ, ATTN_W), (C_AK, ATTN_W), (C_AV, ATTN_W), (C_AG, ATTN_W),
              (C_MQ, MEM_W), (C_MG, MEM_W))

    def body(x_ref, dy_ref, g_ref, wt_hbm, p0, p1, p2, p3, p4, p5, p6, gx_ref, gwt_hbm, gg_ref, wt_v, acc, wt_sem, out_sems):
        i = pl.program_id(0)
        wt_load = pltpu.make_async_copy(wt_hbm, wt_v, wt_sem)

        @pl.when(i == 0)
        def _():
            wt_load.start()
            acc[...] = jnp.zeros_like(acc)
            gg_ref[...] = jnp.zeros_like(gg_ref)

        xv = x_ref[...]
        r = lax.rsqrt(jnp.mean(xv * xv, axis=-1, keepdims=True) + EPS)
        z = xv * r
        g = g_ref[...]
        h = (z * g).astype(BF16)
        pl.when(i == 0)(wt_load.wait)
        flush = [pltpu.make_async_copy(acc.at[c0:c0 + w, :], gwt_hbm.at[c0:c0 + w, :], out_sems.at[n])
                 for n, (c0, w) in enumerate(pieces)]
        dh = jnp.zeros((tm, D_MODEL), F32)
        for n, (pref, (c0, w)) in enumerate(zip((p0, p1, p2, p3, p4, p5, p6), pieces)):
            dp = pref[...]
            dh = dh + _dot(dp, wt_v[c0:c0 + w, :])
            acc[c0:c0 + w, :] += _dot_tn(dp, h)
            pl.when(i == nsteps - 1)(flush[n].start)
        _put_rows(gg_ref, jnp.sum(dh * z, axis=0, keepdims=True), accumulate=True)
        dz = dh * g
        gx_ref[...] = dy_ref[...] + r * (dz - z * jnp.mean(dz * z, axis=-1, keepdims=True))

        @pl.when(i == nsteps - 1)
        def _():
            for cp in flush:
                cp.wait()

    row = lambda w: pl.BlockSpec((tm, w), lambda i: (i, 0))
    hbm = pl.BlockSpec(memory_space=pl.ANY)
    vec = pl.BlockSpec((1, D_MODEL), lambda i: (0, 0))
    return _call(
        body, name="proj_bwd", grid=(nsteps,),
        in_specs=[row(D_MODEL), row(D_MODEL), vec, hbm] + [row(w) for _, w in pieces],
        out_specs=[row(D_MODEL), hbm, pl.BlockSpec((8, LANES), lambda i: (0, 0))],
        out_shape=[jax.ShapeDtypeStruct((SEQ, D_MODEL), F32), jax.ShapeDtypeStruct((IN_W, D_MODEL), F32),
                   jax.ShapeDtypeStruct((8, LANES), F32)],
        scratch_shapes=[pltpu.VMEM((IN_W, D_MODEL), BF16), pltpu.VMEM((IN_W, D_MODEL), F32), pltpu.SemaphoreType.DMA,
                        pltpu.SemaphoreType.DMA((len(pieces),))],
        compiler_params=_params(),
    )(x, dy, gain, wt, dg, daq, dak, dav, dag, dmq, dmg)


AG_SEMS = 8


def _gather_stages(ins, lands, send_sems, recv_sems):
    n = len(ins)
    nrows = [a.shape[0] for a in ins]
    x, y, c = lax.axis_index("x"), lax.axis_index("y"), lax.axis_index("c")
    sib, xn, yn = (x, y, 1 - c), (1 - x, y, c), (x, 1 - y, c)
    me, cx, cy, cd = 2 * x + y, 2 * (1 - x) + y, 2 * x + (1 - y), 2 * (1 - x) + (1 - y)

    def part(a, chip, hf, quarter=None):
        rows = nrows[a] // 2
        base = chip * nrows[a] + hf * rows
        if quarter is not None:
            rows = rows // 2
            base = base + quarter * rows
        return lands[a].at[pl.ds(pl.multiple_of(base, 16), rows), :]

    def copy(a, j, ref, to):
        k = AG_SEMS * a + j
        return pltpu.make_async_remote_copy(src_ref=ref, dst_ref=ref, send_sem=send_sems.at[k],
                                            recv_sem=recv_sems.at[k], device_id=to, device_id_type=MESH)

    def own(a):
        return [copy(a, 0, part(a, me, c), xn), copy(a, 1, part(a, me, c), yn)]

    def neighbours(a):
        return [copy(a, 4, part(a, cx, c, 1), yn), copy(a, 2, part(a, cx, c), sib),
                copy(a, 5, part(a, cy, c, 0), xn), copy(a, 3, part(a, cy, c), sib)]

    def diagonal(a):
        return [copy(a, 7, part(a, cd, c, 1), sib), copy(a, 6, part(a, cd, c, 0), sib)]

    def send_own():
        for a in range(n):
            lands[a][pl.ds(pl.multiple_of(me * nrows[a], 16), nrows[a]), :] = ins[a][...].astype(BF16)
            for cp in own(a):
                cp.start()

    def pass_on_neighbours():
        for a in range(n):
            copy(a, 0, part(a, cx, c), xn).wait_recv()
            copy(a, 1, part(a, cy, c), yn).wait_recv()
            for cp in neighbours(a):
                cp.start()

    def pass_on_diagonal():
        for a in range(n):
            copy(a, 4, part(a, cd, c, 1), yn).wait_recv()
            copy(a, 5, part(a, cd, c, 0), xn).wait_recv()
            for cp in diagonal(a):
                cp.start()

    def y_complete():
        for a in range(n):
            copy(a, 3, part(a, cy, 1 - c), sib).wait_recv()

    def x_complete():
        for a in range(n):
            copy(a, 2, part(a, cx, 1 - c), sib).wait_recv()

    def diagonal_complete():
        for a in range(n):
            copy(a, 6, part(a, cd, 1 - c, 0), sib).wait_recv()
            copy(a, 7, part(a, cd, 1 - c, 1), sib).wait_recv()

    def sends_done():
        for a in range(n):
            for cp in own(a) + neighbours(a) + diagonal(a):
                cp.wait_send()

    def finish():
        y_complete()
        x_complete()
        diagonal_complete()
        sends_done()

    return (send_own, pass_on_neighbours, pass_on_diagonal, finish), (y_complete, x_complete, diagonal_complete, sends_done)


RS_SEMS = 6
RS_KINDS = (((2, 2), 1, F32), ((2, 2), 1, F32), ((2, 2), 2, BF16), ((2, 2), 2, BF16), ((2, 2), 2, F32),
            ((2,), 2, BF16), ((2,), 2, BF16), ((2,), 1, F32))


def _rs_view(g):
    return g.reshape(2, 2, 2, g.shape[0] // 8, g.shape[1])


def _rs_scratch(shapes):
    return [pltpu.VMEM(lead + (r // 8, w // split), dt) for lead, split, dt in RS_KINDS for r, w in shapes]


def _rs_stages(gs, outs, bufs, send_sems, recv_sems, local_sems, widths):
    n = len(gs)
    loc, ra, s_b, r_b, acc1, s_c, r_c, fin = (bufs[n * i:n * i + n] for i in range(len(RS_KINDS)))
    half_w = [w // 2 for w in widths]
    x, y, c = lax.axis_index("x"), lax.axis_index("y"), lax.axis_index("c")
    sib, xn, yn = (x, y, 1 - c), (1 - x, y, c), (x, 1 - y, c)

    def copy(a, j, src, dst, to):
        k = RS_SEMS * a + j
        return pltpu.make_async_remote_copy(src_ref=src, dst_ref=dst, send_sem=send_sems.at[k],
                                            recv_sem=recv_sems.at[k], device_id=to, device_id_type=MESH)

    def step_a(a):
        return [copy(a, 0, gs[a].at[:, :, 1 - c], ra[a], sib),
                pltpu.make_async_copy(gs[a].at[:, :, c], loc[a], local_sems.at[a])]

    def step_b(a):
        return copy(a, 1, s_b[a].at[0], r_b[a].at[0], xn), copy(a, 2, s_b[a].at[1], r_b[a].at[1], yn)

    def step_c(a):
        return copy(a, 3, s_c[a].at[0], r_c[a].at[0], yn), copy(a, 4, s_c[a].at[1], r_c[a].at[1], xn)

    def step_d(a, half):
        rows = fin[a].at[half]
        return copy(a, 5, rows, rows, sib)

    def start():
        for a in range(n):
            for cp in step_a(a):
                cp.start()

    def a_to_b():
        for a in range(n):
            for cp in step_a(a):
                cp.wait()
            ra[a][...] = loc[a][...] + ra[a][...]
            s_b[a][0] = ra[a][1 - x, :, :, :half_w[a]].astype(BF16)
            s_b[a][1] = ra[a][:, 1 - y, :, half_w[a]:].astype(BF16)
            for cp in step_b(a):
                cp.start()

    def b_to_c():
        for a in range(n):
            for cp in step_b(a):
                cp.wait()
            acc1[a][0] = ra[a][x, :, :, :half_w[a]] + r_b[a][0].astype(F32)
            acc1[a][1] = ra[a][:, y, :, half_w[a]:] + r_b[a][1].astype(F32)
            s_c[a][0] = acc1[a][0, 1 - y].astype(BF16)
            s_c[a][1] = acc1[a][1, 1 - x].astype(BF16)
            for cp in step_c(a):
                cp.start()

    def c_to_d():
        for a in range(n):
            for cp in step_c(a):
                cp.wait()
            fin[a][c, :, :half_w[a]] = acc1[a][0, y] + r_c[a][0].astype(F32)
            fin[a][c, :, half_w[a]:] = acc1[a][1, x] + r_c[a][1].astype(F32)
            step_d(a, c).start()

    def finish():
        to_hbm = [pltpu.make_async_copy(fin[a], outs[a], local_sems.at[a]) for a in range(n)]
        for a in range(n):
            step_d(a, 1 - c).wait_recv()
            step_d(a, c).wait_send()
            to_hbm[a].start()
        for cp in to_hbm:
            cp.wait()

    return start, a_to_b, b_to_c, c_to_d, finish


def _reduce_grads(gwt, g_ws, tiny):
    cw = gwt.shape[1] // RS_CHUNKS
    chunk_shape = (gwt.shape[0], cw)

    def body(g0, ws_in, tiny_in, *rest):
        outs, o_ws, o_tiny = rest[:RS_CHUNKS], rest[RS_CHUNKS], rest[RS_CHUNKS + 1]
        rest = rest[RS_CHUNKS + 2:]
        nb = len(RS_KINDS) * RS_CHUNKS
        sm, sa, sb, sc, acc_s, send_sems, recv_sems, local_sems = rest[nb:]
        blocks = [g0.at[:, :, :, :, pl.ds(j * cw, cw)] for j in range(RS_CHUNKS)]
        start, a_to_b, b_to_c, c_to_d, finish = _rs_stages(blocks, outs, rest[:nb], send_sems, recv_sems, local_sems,
                                                           [cw] * RS_CHUNKS)
        n_ws = ws_in.shape[0]
        sm[0:n_ws, :] = ws_in[...]
        sm[n_ws:, :] = tiny_in[...]
        x, y, c = lax.axis_index("x"), lax.axis_index("y"), lax.axis_index("c")

        def small(j, src, dst, to):
            k = RS_SEMS * RS_CHUNKS + j
            return pltpu.make_async_remote_copy(src_ref=src, dst_ref=dst, send_sem=send_sems.at[k],
                                                recv_sem=recv_sems.at[k], device_id=to, device_id_type=MESH)

        along_c, along_x, along_y = (small(0, sm, sa, (x, y, 1 - c)), small(1, acc_s, sb, (1 - x, y, c)),
                                     small(2, sb, sc, (x, 1 - y, c)))
        start()
        along_c.start()
        a_to_b()
        along_c.wait()
        acc_s[...] = sm[...] + sa[...]
        along_x.start()
        b_to_c()
        along_x.wait()
        sb[...] = acc_s[...] + sb[...]
        along_y.start()
        c_to_d()
        along_y.wait()
        o_ws[...] = sb[0:n_ws, :] + sc[0:n_ws, :]
        o_tiny[...] = sb[n_ws:, :] + sc[n_ws:, :]
        finish()

    vm = pl.BlockSpec(memory_space=pltpu.VMEM)
    hbm = pl.BlockSpec(memory_space=pl.ANY)
    small_shape = (g_ws.shape[0] + tiny.shape[0], LANES)
    scratch = _rs_scratch([chunk_shape] * RS_CHUNKS) + [pltpu.VMEM(small_shape, F32) for _ in range(5)]
    nsem = RS_SEMS * RS_CHUNKS + 3
    scratch += [pltpu.SemaphoreType.DMA((nsem,)), pltpu.SemaphoreType.DMA((nsem,)), pltpu.SemaphoreType.DMA((RS_CHUNKS,))]
    return _call(
        body, name="reduce_grads",
        out_shape=[jax.ShapeDtypeStruct((2, gwt.shape[0] // 8, cw), F32)] * RS_CHUNKS
        + [jax.ShapeDtypeStruct(g_ws.shape, F32), jax.ShapeDtypeStruct(tiny.shape, F32)],
        in_specs=[hbm, vm, vm],
        out_specs=[hbm] * RS_CHUNKS + [vm, vm],
        scratch_shapes=scratch,
        compiler_params=_params(),
    )(_rs_view(gwt), g_ws, tiny)


def _adam_update(w, g, m, v):
    nm = ADAM_B1 * m + (1.0 - ADAM_B1) * g
    nv = ADAM_B2 * v + (1.0 - ADAM_B2) * (g * g)
    m_hat = nm / (1.0 - ADAM_B1 ** ADAM_STEP)
    v_hat = nv / (1.0 - ADAM_B2 ** ADAM_STEP)
    return -ADAM_LR * (m_hat / (jnp.sqrt(v_hat) + ADAM_EPS) + ADAM_WD * w), nm, nv


def _adamw(w, g, m, v):
    rows, cols = w.shape
    tm = max(t for t in range(8, 257, 8) if rows % t == 0)
    parts = tuple(g) if isinstance(g, (tuple, list)) else (g,)
    n = len(parts)

    def body(w_ref, m_ref, v_ref, *refs):
        gv = jnp.concatenate([r[...] for r in refs[:n]], axis=1)
        d_ref, nm_ref, nv_ref = refs[n:n + 3]
        d_ref[...], nm_ref[...], nv_ref[...] = _adam_update(w_ref[...], gv, m_ref[...], v_ref[...])
        if n > 1:
            refs[n + 3][...] = gv

    blk = pl.BlockSpec((tm, cols), lambda i: (i, 0))
    nout = 3 if n == 1 else 4
    res = _call(
        body, name="adamw", grid=(rows // tm,),
        in_specs=[blk] * 3 + [pl.BlockSpec((tm, p.shape[1]), lambda i: (i, 0)) for p in parts], out_specs=[blk] * nout,
        out_shape=[jax.ShapeDtypeStruct((rows, cols), F32)] * nout,
        compiler_params=_params(),
    )(w, m, v, *parts)
    return (parts[0] if n == 1 else res[3], *res[:3])


def _adamw_tiny(tiny, weights, ms, vs):
    shapes = [w.shape for w in weights]
    n = len(weights)

    def grad_of(t_ref, k, shape):
        base = 8 * k
        if shape[1] > LANES:
            return [t_ref[base + j:base + j + 1, :] for j in range(shape[1] // LANES)]
        return [t_ref[base:base + shape[0], 0:shape[1]]]

    def body(t_ref, *refs):
        w_refs, m_refs, v_refs = refs[:n], refs[n:2 * n], refs[2 * n:3 * n]
        loss_ref, outs = refs[3 * n], refs[3 * n + 1:]
        loss_ref[...] = (0.5 / D_MODEL) * jnp.sum(t_ref[8 * n:8 * n + 8, :], keepdims=True)
        for k, shape in enumerate(shapes):
            g_ref, d_ref, nm_ref, nv_ref = outs[4 * k:4 * k + 4]
            for j, g in enumerate(grad_of(t_ref, k, shape)):
                cols = slice(j * LANES, (j + 1) * LANES) if shape[1] > LANES else slice(None)
                g_ref[:, cols] = g
                d_ref[:, cols], nm_ref[:, cols], nv_ref[:, cols] = _adam_update(
                    w_refs[k][:, cols], g, m_refs[k][:, cols], v_refs[k][:, cols])

    out_shape = [jax.ShapeDtypeStruct((1, 1), F32)]
    for shape in shapes:
        out_shape += [jax.ShapeDtypeStruct(shape, F32)] * 4
    return _call(body, name="adamw_tiny", out_shape=out_shape, compiler_params=_params())(tiny, *weights, *ms, *vs)


def _local_grads(x, mem, tgt, norm_gain, wt_sh, gmlp_v_gain, gmlp_w_s, gmlp_b, attn_q_gain, attn_k_gain,
                 mem_norm_gain, wkv_sh, mem_q_gain, mem_k_gain, wo_sh):
    vg = gmlp_v_gain.reshape(1, GMLP_W)
    bias_full = jnp.repeat(gmlp_b.T, HEAD_DIM, axis=1)
    gq2, gk2 = jnp.tile(attn_q_gain, (1, 2)), jnp.tile(attn_k_gain, (1, 2))
    qg4, kg4 = jnp.tile(mem_q_gain, (1, 4)), jnp.tile(mem_k_gain, (1, 4))

    proj, wt = _gather_proj(x, norm_gain, wt_sh)
    yg = _gmlp_fwd(proj, vg, gmlp_w_s, bias_full)
    o, lse, ya, wkv, wo = _attn_fwd(proj, gq2, gk2, wkv_sh, wo_sh)
    kv, hm = _mem_kv(mem, mem_norm_gain, wkv)
    om, ym = _mem_fwd(proj, kv, qg4, kg4)
    dy, dyc, g_wo, err2 = _out_loss(yg, ya, ym, x, tgt, wo)
    dmq, dmg, g_mq, g_mk, g_wkv, g_mng = _mem_bwd(proj, om, dyc, kv, hm, mem, mem_norm_gain, wkv, qg4, kg4)
    daq, dak, dav, dag, g_aq, g_ak, g_wkv_sh, g_wo_sh = _attn_bwd(proj, o, lse, dyc, gq2, gk2, g_wkv, g_wo)
    dg, g_ws, g_b, g_vg = _gmlp_bwd(proj, dyc, vg, gmlp_w_s, bias_full)
    gx, g_wt, g_ng = _proj_bwd(x, dy, norm_gain, wt, dg, daq, dak, dav, dag, dmq, dmg)

    tiny = jnp.concatenate([g_ng, g_vg, g_b, g_aq, g_ak, g_mng, g_mq, g_mk, err2], axis=0)
    return gx, g_wt, g_wkv_sh, g_wo_sh, g_ws.reshape(4 * CHUNK, CHUNK), tiny


def kernel(x, mem, norm_gain, w_in, gmlp_v_gain, gmlp_w_s, gmlp_b, attn_q_gain, attn_k_gain, mem_norm_gain, w_mem_kv, mem_q_gain, mem_k_gain, w_out, loss_target, m_norm_gain, m_w_in, m_gmlp_v_gain, m_gmlp_w_s, m_gmlp_b, m_attn_q_gain, m_attn_k_gain, m_mem_norm_gain, m_w_mem_kv, m_mem_q_gain, m_mem_k_gain, m_w_out, v_norm_gain, v_w_in, v_gmlp_v_gain, v_gmlp_w_s, v_gmlp_b, v_attn_q_gain, v_attn_k_gain, v_mem_norm_gain, v_w_mem_kv, v_mem_q_gain, v_mem_k_gain, v_w_out):
    gx, g_wt, g_wkv_sh, g_wo_sh, g_ws, tiny = _local_grads(
        x[0], mem[0], loss_target[0], norm_gain, w_in[0].T, gmlp_v_gain[0], gmlp_w_s[0], gmlp_b[0],
        attn_q_gain, attn_k_gain, mem_norm_gain, w_mem_kv[0], mem_q_gain, mem_k_gain, w_out[0])
    *g_wt_sh, g_ws, tiny = _reduce_grads(g_wt, g_ws, tiny)
    chip_block = lambda g: g.reshape(2 * g.shape[1], g.shape[2])
    g_wt_sh = tuple(chip_block(g) for g in g_wt_sh)
    g_wkv_sh, g_wo_sh = chip_block(g_wkv_sh), chip_block(g_wo_sh)

    ws = (norm_gain, w_in, gmlp_v_gain, gmlp_w_s, gmlp_b, attn_q_gain, attn_k_gain, mem_norm_gain, w_mem_kv,
          mem_q_gain, mem_k_gain, w_out)
    ms = (m_norm_gain, m_w_in, m_gmlp_v_gain, m_gmlp_w_s, m_gmlp_b, m_attn_q_gain, m_attn_k_gain, m_mem_norm_gain,
          m_w_mem_kv, m_mem_q_gain, m_mem_k_gain, m_w_out)
    vs = (v_norm_gain, v_w_in, v_gmlp_v_gain, v_gmlp_w_s, v_gmlp_b, v_attn_q_gain, v_attn_k_gain, v_mem_norm_gain,
          v_w_mem_kv, v_mem_q_gain, v_mem_k_gain, v_w_out)
    form = {1: lambda a: a[0].T, 3: lambda a: a.reshape(4 * CHUNK, CHUNK), 2: lambda a: a[0], 4: lambda a: a[0],
            8: lambda a: a[0], 11: lambda a: a[0]}
    back = {1: lambda a: a.T[None], 3: lambda a: a.reshape(1, 4, CHUNK, CHUNK), 2: lambda a: a[None],
            4: lambda a: a[None], 8: lambda a: a[None], 11: lambda a: a[None]}
    fwd = lambda t, i: form.get(i, lambda a: a)(t[i])
    out = {}
    for i, g in ((1, g_wt_sh), (3, g_ws), (8, g_wkv_sh), (11, g_wo_sh)):
        out[i] = _adamw(fwd(ws, i), g, fwd(ms, i), fwd(vs, i))
    res = _adamw_tiny(tiny, [fwd(ws, i) for i in TINY_ORDER], [fwd(ms, i) for i in TINY_ORDER],
                      [fwd(vs, i) for i in TINY_ORDER])
    for k, i in enumerate(TINY_ORDER):
        out[i] = res[1 + 4 * k:5 + 4 * k]
    leaves = [[back.get(i, lambda a: a)(out[i][j]) for i in range(12)] for j in range(4)]
    return (res[0].reshape(()), gx[None], *leaves[0], *leaves[1], *leaves[2], *leaves[3])
```

```python
import math

import jax
import jax.numpy as jnp
from jax import lax
from jax.experimental import pallas as pl
from jax.experimental.pallas import tpu as pltpu

F32 = jnp.float32
BF16 = jnp.bfloat16

SEQ = 4096
D_MODEL = 1024
HEAD_DIM = 64
LANES = 128
CHUNK = 128
GMLP_W, ATTN_W, MEM_W = 256, 512, 256
IN_W = 3 * GMLP_W + 4 * ATTN_W + 2 * MEM_W
MEM_LEN = 256
DILATIONS = (16, 4, 1)
EPS = 1e-6
QK_SCALE = 1.0 / math.sqrt(HEAD_DIM)
C_GU, C_GV, C_GG, C_AQ, C_AK, C_AV, C_AG, C_MQ, C_MG = 0, 256, 512, 768, 1280, 1792, 2304, 2816, 3072

ADAM_LR, ADAM_B1, ADAM_B2, ADAM_EPS, ADAM_WD, ADAM_STEP = 0.001, 0.9, 0.999, 1e-08, 0.01, 10

VMEM_LIMIT = 48 * 1024 * 1024
RS_CHUNKS = 4
ATTN_UNROLL = 4
MESH = pl.DeviceIdType.MESH

TINY_ORDER = (0, 2, 4, 5, 6, 7, 9, 10)


def _call(body, **kw):
    return pl.pallas_call(body, **kw)


def _params(**kw):
    return pltpu.CompilerParams(vmem_limit_bytes=VMEM_LIMIT, **kw)


def _dot(a, b):
    return jnp.dot(a, b, preferred_element_type=F32)


def _dot_nt(a, b):
    return lax.dot_general(a, b, (((1,), (1,)), ((), ())), preferred_element_type=F32)


def _dot_tn(a, b):
    return lax.dot_general(a, b, (((0,), (0,)), ((), ())), preferred_element_type=F32)


def _head_blockdiag():
    r = lax.shift_right_logical(lax.broadcasted_iota(jnp.int32, (LANES, LANES), 0), 6)
    c = lax.shift_right_logical(lax.broadcasted_iota(jnp.int32, (LANES, LANES), 1), 6)
    return jnp.where(r == c, 1.0, 0.0).astype(BF16)


def _headsum(v, bd):
    hi = v.astype(BF16)
    lo = (v - hi.astype(F32)).astype(BF16)
    return _dot(hi, bd) + _dot(lo, bd)


def _lo_mask(rows):
    return lax.broadcasted_iota(jnp.int32, (rows, LANES), 1) < HEAD_DIM


def _sigmoid(x):
    return 1.0 / (1.0 + jnp.exp(-x))


def _fold_heads(v):
    return v + pltpu.roll(v, HEAD_DIM, 1)


def _put_rows(ref, vec, accumulate=False):
    for j in range(vec.shape[1] // LANES):
        piece = vec[:, j * LANES:(j + 1) * LANES]
        ref[j:j + 1, :] = ref[j:j + 1, :] + piece if accumulate else piece


def _gather_proj(x, gain, wt_sh):
    tm = 1024
    nrow = SEQ // tm
    widths = (768, 896, 768, 896)
    nunits = len(widths)
    pair = 2 * wt_sh.shape[0]
    assert pair % LANES == 0 and sum(widths[:2]) == pair

    def body(x_ref, g_ref, wt_sh_ref, proj_hbm, wt_hbm, h_scr, land, res, send_sems, recv_sems, out_sems, copy_sem):
        u, i = pl.program_id(0), pl.program_id(1)
        cx_, cy_ = lax.axis_index("x"), lax.axis_index("y")
        (send_own, pass_on_neighbours, pass_on_diagonal, _), (y_complete, x_complete, diagonal_complete, sends_done) = (
            _gather_stages((wt_sh_ref,), (land,), send_sems, recv_sems))
        first = lambda k: (u == k) & (i == 0)
        last = (u == nunits - 1) & (i == nrow - 1)
        to_hbm = pltpu.make_async_copy(land, wt_hbm, copy_sem)

        pl.when(first(0))(send_own)

        @pl.when(u == 0)
        def _():
            xv = x_ref[...]
            ms = jnp.mean(xv * xv, axis=-1, keepdims=True)
            h_scr[pl.ds(pl.multiple_of(i * tm, tm), tm), :] = (xv * lax.rsqrt(ms + EPS) * g_ref[...]).astype(BF16)

        @pl.when(first(1))
        def _():
            pass_on_neighbours()
            y_complete()

        @pl.when(first(2))
        def _():
            x_complete()
            pass_on_diagonal()

        @pl.when(first(3))
        def _():
            diagonal_complete()
            to_hbm.start()

        mine, other = pair * cx_, pair * (1 - cx_)
        col0 = (mine + 896 * cy_, mine + 768 * (1 - cy_), other + 896 * cy_, other + 768 * (1 - cy_))
        slot = i % 2
        rows = pl.ds(pl.multiple_of(i * tm, tm), tm)

        def writeback(k, rows_):
            c0 = pl.multiple_of(col0[k], LANES)
            return pltpu.make_async_copy(res.at[slot, :, pl.ds(0, widths[k])], proj_hbm.at[rows_, pl.ds(c0, widths[k])],
                                         out_sems.at[slot])

        for k in range(nunits):
            @pl.when(u == k)
            def _(k=k):
                pl.when(i >= 2)(writeback(k, rows).wait)
                if k > 0:
                    pl.when(i < 2)(writeback(k - 1, rows).wait)
                w_rows = land[pl.ds(pl.multiple_of(col0[k], LANES), widths[k]), :]
                res[slot, :, 0:widths[k]] = _dot_nt(h_scr[rows, :], w_rows)
                writeback(k, rows).start()

        @pl.when(last)
        def _():
            sends_done()
            to_hbm.wait()
            for s in range(2):
                pltpu.make_async_copy(res.at[s, :, pl.ds(0, widths[-1])], proj_hbm.at[rows, pl.ds(0, widths[-1])], out_sems.at[s]).wait()

    full = jax.ShapeDtypeStruct((4 * wt_sh.shape[0], wt_sh.shape[1]), BF16)
    hbm = pl.BlockSpec(memory_space=pl.ANY)
    return _call(
        body, name="gather_proj", grid=(nunits, nrow),
        in_specs=[pl.BlockSpec((tm, D_MODEL), lambda u, i: (jnp.where(u == 0, i, nrow - 1), 0)),
                  pl.BlockSpec((1, D_MODEL), lambda u, i: (0, 0)), pl.BlockSpec(wt_sh.shape, lambda u, i: (0, 0))],
        out_specs=[hbm, hbm],
        out_shape=[jax.ShapeDtypeStruct((SEQ, IN_W), F32), full],
        scratch_shapes=[pltpu.VMEM((SEQ, D_MODEL), BF16), pltpu.VMEM(full.shape, BF16), pltpu.VMEM((2, tm, max(widths)), F32),
                        pltpu.SemaphoreType.DMA((AG_SEMS,)), pltpu.SemaphoreType.DMA((AG_SEMS,)),
                        pltpu.SemaphoreType.DMA((2,)), pltpu.SemaphoreType.DMA],
        compiler_params=_params(),
    )(x, gain, wt_sh)


def _gmlp_weights(w_ref):
    ti = lax.broadcasted_iota(jnp.int32, (CHUNK, CHUNK), 0)
    si = lax.broadcasted_iota(jnp.int32, (CHUNK, CHUNK), 1)
    tril = si <= ti
    return tril, [jnp.where(tril, w_ref[h], 0.0).astype(BF16) for h in range(4)]


def _gmlp_fwd(proj, vgain, w_s, bias_full):
    tm = 1024

    def body(p_ref, vg_ref, w_ref, b_ref, y_ref):
        bd = _head_blockdiag()
        lo = _lo_mask(CHUNK)
        _, wm = _gmlp_weights(w_ref)
        units = [(pl.ds(c * CHUNK, CHUNK), p) for c in range(tm // CHUNK) for p in range(2)]
        col = lambda c0, p: slice(c0 + p * LANES, c0 + (p + 1) * LANES)
        vs = [p_ref[rows, col(C_GV, p)] for rows, p in units]
        rs = [lax.rsqrt(_headsum(v * v, bd) * (1.0 / HEAD_DIM) + EPS) for v in vs]
        vns = [(v * r * vg_ref[:, col(0, p)]).astype(BF16) for v, r, (_, p) in zip(vs, rs, units)]
        sps = [jnp.where(lo, _dot(wm[2 * p], vn), _dot(wm[2 * p + 1], vn)) + b_ref[:, col(0, p)] for vn, (_, p) in zip(vns, units)]
        for sp, (rows, p) in zip(sps, units):
            gt = p_ref[rows, col(C_GG, p)]
            y_ref[rows, col(0, p)] = (p_ref[rows, col(C_GU, p)] * sp * (gt * _sigmoid(gt))).astype(BF16)

    return _call(
        body, name="gmlp_fwd", grid=(SEQ // tm,),
        in_specs=[pl.BlockSpec((tm, 3 * GMLP_W), lambda i: (i, 0)),
                  pl.BlockSpec((1, GMLP_W), lambda i: (0, 0)),
                  pl.BlockSpec((4, CHUNK, CHUNK), lambda i: (0, 0, 0)),
                  pl.BlockSpec((CHUNK, GMLP_W), lambda i: (0, 0))],
        out_specs=pl.BlockSpec((tm, GMLP_W), lambda i: (i, 0)),
        out_shape=jax.ShapeDtypeStruct((SEQ, GMLP_W), BF16),
        compiler_params=_params(),
    )(proj, vgain, w_s, bias_full)


def _gmlp_bwd(proj, dyc, vgain, w_s, bias_full):
    tm = 1024
    nsteps = SEQ // tm

    def body(p_ref, dy_ref, vg_ref, w_ref, b_ref, dg_ref, gw_ref, gb_ref, gv_ref):
        i = pl.program_id(0)
        bd = _head_blockdiag()
        lo = _lo_mask(CHUNK)
        tril, wm = _gmlp_weights(w_ref)
        ri = lax.broadcasted_iota(jnp.int32, (16, LANES), 0)
        li = lax.broadcasted_iota(jnp.int32, (16, LANES), 1)
        head_rows = [jnp.where(((ri == 2 * p) & (li < HEAD_DIM)) | ((ri == 2 * p + 1) & (li >= HEAD_DIM)), 1.0, 0.0).astype(BF16)
                     for p in range(2)]

        @pl.when(i == 0)
        def _():
            gw_ref[...] = jnp.zeros_like(gw_ref)
            gb_ref[...] = jnp.zeros_like(gb_ref)
            gv_ref[...] = jnp.zeros_like(gv_ref)

        units = [(pl.ds(c * CHUNK, CHUNK), p) for c in range(tm // CHUNK) for p in range(2)]
        col = lambda c0, p: slice(c0 + p * LANES, c0 + (p + 1) * LANES)
        vs = [p_ref[rows, col(C_GV, p)] for rows, p in units]
        rs = [lax.rsqrt(_headsum(v * v, bd) * (1.0 / HEAD_DIM) + EPS) for v in vs]
        zs = [v * r for v, r in zip(vs, rs)]
        vns = [(z * vg_ref[:, col(0, p)]).astype(BF16) for z, (_, p) in zip(zs, units)]
        sps = [jnp.where(lo, _dot(wm[2 * p], vn), _dot(wm[2 * p + 1], vn)) + b_ref[:, col(0, p)] for vn, (_, p) in zip(vns, units)]
        dsps = []
        for sp, (rows, p) in zip(sps, units):
            u = p_ref[rows, col(C_GU, p)]
            gt = p_ref[rows, col(C_GG, p)]
            dy = dy_ref[rows, col(0, p)]
            sg = _sigmoid(gt)
            sl = gt * sg
            dg_ref[rows, col(C_GU, p)] = (dy * sp * sl).astype(BF16)
            dg_ref[rows, col(C_GG, p)] = (dy * u * sp * (sg * (1.0 + gt * (1.0 - sg)))).astype(BF16)
            dsps.append(dy * u * sl)
        dspbs = [dsp.astype(BF16) for dsp in dsps]
        dvns = [jnp.where(lo, _dot_tn(wm[2 * p], dspb), _dot_tn(wm[2 * p + 1], dspb)) for dspb, (_, p) in zip(dspbs, units)]
        gws = [(_dot_nt(jnp.where(lo, dsp, 0.0).astype(BF16), vn), _dot_nt(jnp.where(lo, 0.0, dsp).astype(BF16), vn))
               for dsp, vn in zip(dsps, vns)]
        gbs = [(_dot_nt(head_rows[p], dspb) + _dot_nt(head_rows[p], (dsp - dspb.astype(F32)).astype(BF16)))[0:8]
               for dsp, dspb, (_, p) in zip(dsps, dspbs, units)]
        for p in range(2):
            mine = [n for n, (_, q) in enumerate(units) if q == p]
            gw_ref[2 * p] += sum(gws[n][0] for n in mine)
            gw_ref[2 * p + 1] += sum(gws[n][1] for n in mine)
            gvp = sum(jnp.sum(dvns[n] * zs[n], axis=0, keepdims=True) for n in mine)
            gv_ref[2 * p:2 * p + 1, :] += gvp
            gv_ref[2 * p + 1:2 * p + 2, :] += pltpu.roll(gvp, HEAD_DIM, 1)
        gb_ref[...] += sum(gbs)
        for dvn, z, r, (rows, p) in zip(dvns, zs, rs, units):
            dz = dvn * vg_ref[:, col(0, p)]
            dg_ref[rows, col(C_GV, p)] = (r * (dz - z * (_headsum(dz * z, bd) * (1.0 / HEAD_DIM)))).astype(BF16)

        @pl.when(i == nsteps - 1)
        def _():
            for h in range(4):
                gw_ref[h] = jnp.where(tril, gw_ref[h], 0.0)

    return _call(
        body, name="gmlp_bwd", grid=(nsteps,),
        in_specs=[pl.BlockSpec((tm, 3 * GMLP_W), lambda i: (i, 0)),
                  pl.BlockSpec((tm, GMLP_W), lambda i: (i, 0)),
                  pl.BlockSpec((1, GMLP_W), lambda i: (0, 0)),
                  pl.BlockSpec((4, CHUNK, CHUNK), lambda i: (0, 0, 0)),
                  pl.BlockSpec((CHUNK, GMLP_W), lambda i: (0, 0))],
        out_specs=[pl.BlockSpec((tm, 3 * GMLP_W), lambda i: (i, 0)),
                   pl.BlockSpec((4, CHUNK, CHUNK), lambda i: (0, 0, 0)),
                   pl.BlockSpec((8, LANES), lambda i: (0, 0)),
                   pl.BlockSpec((8, LANES), lambda i: (0, 0))],
        out_shape=[jax.ShapeDtypeStruct((SEQ, 3 * GMLP_W), BF16),
                   jax.ShapeDtypeStruct((4, CHUNK, CHUNK), F32),
                   jax.ShapeDtypeStruct((8, LANES), F32),
                   jax.ShapeDtypeStruct((8, LANES), F32)],
        compiler_params=_params(),
    )(proj, dyc, vgain, w_s, bias_full)


def _band_masks():
    qi = lax.broadcasted_iota(jnp.int32, (CHUNK, 2 * CHUNK), 0)
    kj = lax.broadcasted_iota(jnp.int32, (CHUNK, 2 * CHUNK), 1)
    valid2 = ((kj < CHUNK) & (kj >= qi)) | ((kj >= CHUNK) & (kj - CHUNK <= qi))
    q1 = lax.broadcasted_iota(jnp.int32, (CHUNK, CHUNK), 0)
    k1 = lax.broadcasted_iota(jnp.int32, (CHUNK, CHUNK), 1)
    return k1 <= q1, valid2


def _stack_heads(v, lo):
    return jnp.concatenate([jnp.where(lo, v, 0.0), jnp.where(lo, 0.0, v)], axis=0).astype(BF16)


def _rows_of(ref, start, d):
    if d == 1:
        return ref.at[pl.ds(start if isinstance(start, int) else pl.multiple_of(start, CHUNK), CHUNK), :]
    return ref.at[pl.ds(start, CHUNK, stride=d), :]


def _unrolled(lo, hi, unroll, run):
    groups = (hi - lo) // unroll
    if groups:
        def body(g, carry):
            run([lo + g * unroll + t for t in range(unroll)])
            return carry

        lax.fori_loop(0, groups, body, 0)
    if lo + groups * unroll < hi:
        run(range(lo + groups * unroll, hi))


def _for_blocks(d, group_fn, unroll):
    nblk = SEQ // CHUNK
    sh = d.bit_length() - 1

    def first(j):
        return (j * CHUNK if d == 1 else j, None)

    def rest(j):
        start = (j & (d - 1)) + (j >> sh) * (CHUNK * d)
        return (start, start - CHUNK * d)

    _unrolled(0, d, unroll, lambda js: group_fn(d, [first(j) for j in js]))
    _unrolled(d, nblk, unroll, lambda js: group_fn(d, [rest(j) for j in js]))


def _attn_fwd(proj, gq2, gk2, *ride_along):
    tn = 1024
    npairs = ATTN_W // LANES
    nride = len(ride_along)

    def body(q_ref, k_ref, v_ref, g_ref, gq_ref, gk_ref, *rest):
        shards, rest = rest[:nride], rest[nride:]
        o_ref, l_ref, ya_ref = rest[:3]
        gathered, rest = rest[3:3 + nride], rest[3 + nride:]
        qn_ref, kn_ref = rest[:2]
        lands, (send_sems, recv_sems, copy_sems) = rest[2:2 + nride], rest[2 + nride:]
        pair = pl.program_id(0)
        ride = _gather_stages(shards, lands, send_sems, recv_sems)[0]
        for step in range(npairs):
            pl.when(pair == step)(ride[step])
        bd = _head_blockdiag()
        lo = _lo_mask(CHUNK)
        valid1, valid2 = _band_masks()

        def norm(t, carry):
            rows = pl.ds(pl.multiple_of(t * tn, tn), tn)
            q, k = q_ref[rows, :], k_ref[rows, :]
            ssq = [_headsum(a * a, bd) for a in (q, k)]
            qn_ref[rows, :] = q * lax.rsqrt(ssq[0] * (1.0 / HEAD_DIM) + EPS) * (gq_ref[...] * QK_SCALE)
            kn_ref[rows, :] = k * lax.rsqrt(ssq[1] * (1.0 / HEAD_DIM) + EPS) * gk_ref[...]
            return carry

        lax.fori_loop(0, SEQ // tn, norm, 0)

        def load_kv(ref, d, start, prev):
            own = _rows_of(ref, start, d)[...]
            if prev is None:
                return own.astype(BF16)
            return jnp.concatenate([_rows_of(ref, prev, d)[...], own], axis=0).astype(BF16)

        def group(d, blocks):
            valid = valid1 if blocks[0][1] is None else valid2
            valid = jnp.concatenate([valid, valid], axis=0)
            qs = [_rows_of(qn_ref, start, d)[...] for start, _ in blocks]
            ks = [load_kv(kn_ref, d, start, prev) for start, prev in blocks]
            vs = [load_kv(v_ref, d, start, prev) for start, prev in blocks]
            ss = [_dot_nt(_stack_heads(q, lo), k) for q, k in zip(qs, ks)]
            ms, ps, ls = [], [], []
            for s in ss:
                s = jnp.where(valid, s, -jnp.inf)
                m = jnp.max(s, axis=-1, keepdims=True)
                p = jnp.exp(s - m)
                ms.append(m)
                ls.append(jnp.sum(p, axis=-1, keepdims=True))
                ps.append(p.astype(BF16))
            os_ = [_dot(p, v) for p, v in zip(ps, vs)]
            for b, (start, _) in enumerate(blocks):
                heads = lambda v: jnp.where(lo, v[:CHUNK], v[CHUNK:])
                lsum = heads(ls[b])
                ob = heads(os_[b]) * (1.0 / lsum)
                lb = heads(ms[b]) + jnp.log(lsum)
                o_rows = _rows_of(o_ref, start, d)
                l_rows = _rows_of(l_ref, start, d)
                if d != DILATIONS[0]:
                    lold = l_rows[...]
                    mx = jnp.maximum(lold, lb)
                    ea = jnp.exp(lold - mx)
                    eb = jnp.exp(lb - mx)
                    inv = 1.0 / (ea + eb)
                    ob = o_rows[...] * (ea * inv) + ob * (eb * inv)
                    lb = mx + jnp.log(ea + eb)
                o_rows[...] = ob
                l_rows[...] = lb

        for d in DILATIONS:
            _for_blocks(d, group, ATTN_UNROLL)

        def fin(t, carry):
            rows = pl.ds(pl.multiple_of(t * tn, tn), tn)
            g = g_ref[rows, :]
            ya_ref[rows, :] = (o_ref[rows, :] * (g * _sigmoid(g))).astype(BF16)
            return carry

        lax.fori_loop(0, SEQ // tn, fin, 0)

        @pl.when(pair == npairs - 1)
        def _():
            to_hbm = [pltpu.make_async_copy(land, out, copy_sems.at[n]) for n, (land, out) in enumerate(zip(lands, gathered))]
            for cp in to_hbm:
                cp.start()
            for cp in to_hbm:
                cp.wait()

    col = lambda c0: pl.BlockSpec((SEQ, LANES), lambda p: (0, c0 // LANES + p))
    vec = pl.BlockSpec((1, LANES), lambda p: (0, 0))
    out = pl.BlockSpec((SEQ, LANES), lambda p: (0, p))
    full = [jax.ShapeDtypeStruct((4 * a.shape[0], a.shape[1]), BF16) for a in ride_along]
    return _call(
        body, name="attn_fwd", grid=(npairs,),
        in_specs=[col(C_AQ), col(C_AK), col(C_AV), col(C_AG), vec, vec]
        + [pl.BlockSpec(a.shape, lambda p: (0, 0)) for a in ride_along],
        out_specs=[out, out, out] + [pl.BlockSpec(memory_space=pl.ANY)] * nride,
        out_shape=[jax.ShapeDtypeStruct((SEQ, ATTN_W), F32), jax.ShapeDtypeStruct((SEQ, ATTN_W), F32),
                   jax.ShapeDtypeStruct((SEQ, ATTN_W), BF16)] + full,
        scratch_shapes=[pltpu.VMEM((SEQ, LANES), F32), pltpu.VMEM((SEQ, LANES), F32)]
        + [pltpu.VMEM(s.shape, BF16) for s in full]
        + [pltpu.SemaphoreType.DMA((AG_SEMS * nride,)), pltpu.SemaphoreType.DMA((AG_SEMS * nride,)),
           pltpu.SemaphoreType.DMA((nride,))],
        compiler_params=_params(),
    )(proj, proj, proj, proj, gq2, gk2, *ride_along)


def _attn_bwd(proj, o, lse, dyc, gq2, gk2, *ride_along):
    tn = 1024
    npairs = ATTN_W // LANES
    nride = len(ride_along)
    nbufs = nride * len(RS_KINDS)

    def body(proj_hbm, o_hbm, l_hbm, dyc_hbm, gq_ref, gk_ref, *rest):
        ride_in, rest = rest[:nride], rest[nride:]
        dq_ref, dk_ref, dv_ref, dgt_ref, gqg_ref, gkg_ref = rest[:6]
        ride_out, rest = rest[6:6 + nride], rest[6 + nride:]
        qb_, kb_, vb_, gb_, ob_, lb_, yb_, dkb_, dvb_, sems = rest[:10]
        rs_bufs, (send_sems, recv_sems, local_sems) = rest[10:10 + nbufs], rest[10 + nbufs:]
        rs_stage = _rs_stages(ride_in, ride_out, rs_bufs, send_sems, recv_sems, local_sems, [g.shape[1] for g in ride_along])
        pair = pl.program_id(0)
        for step in range(npairs):
            pl.when(pair == step)(rs_stage[step])
        bd = _head_blockdiag()
        lo = _lo_mask(CHUNK)
        lo2 = lax.broadcasted_iota(jnp.int32, (2 * CHUNK, LANES), 1) < HEAD_DIM
        valid1, valid2 = _band_masks()
        gqs = gq_ref[...] * QK_SCALE
        gk = gk_ref[...]

        def pcol(c0, of=None):
            return acol(proj_hbm, c0, of)

        def acol(hbm, c0=0, of=None):
            of = pair if of is None else of
            return hbm.at[:, pl.ds(pl.multiple_of(c0 + of * LANES, LANES), LANES)]

        def input_loads(of):
            return [pltpu.make_async_copy(src, dst, sems.at[n]) for n, (src, dst) in enumerate((
                (pcol(C_AQ, of), qb_), (pcol(C_AK, of), kb_), (pcol(C_AG, of), gb_), (acol(o_hbm, 0, of), ob_),
                (acol(dyc_hbm, GMLP_W, of), yb_), (pcol(C_AV, of), vb_), (acol(l_hbm, 0, of), lb_)))]

        early = (0, 1, 3, 4)
        loads = input_loads(pair)
        for n, cp in enumerate(loads):
            if n in early:
                pl.when(pair == 0)(cp.start)
            else:
                cp.start()

        @pl.when(pair == 0)
        def _():
            gqg_ref[...] = jnp.zeros_like(gqg_ref)
            gkg_ref[...] = jnp.zeros_like(gkg_ref)

        def pre_qk(t, carry):
            rows = pl.ds(pl.multiple_of(t * tn, tn), tn)
            q, k = qb_[rows, :], kb_[rows, :]
            ssq = [_headsum(a * a, bd) for a in (q, k)]
            qb_[rows, :] = q * lax.rsqrt(ssq[0] * (1.0 / HEAD_DIM) + EPS) * gqs
            kb_[rows, :] = k * lax.rsqrt(ssq[1] * (1.0 / HEAD_DIM) + EPS) * gk
            return carry

        def pre_gate(t, carry):
            rows = pl.ds(pl.multiple_of(t * tn, tn), tn)
            g = gb_[rows, :]
            ov = ob_[rows, :]
            dya = yb_[rows, :]
            sg = _sigmoid(g)
            dgt_ref[rows, :] = (dya * ov * (sg * (1.0 + g * (1.0 - sg)))).astype(BF16)
            do = dya * (g * sg)
            yb_[rows, :] = do
            ob_[rows, :] = jnp.where(first_half, lb_[rows, :], _headsum(do * ov, bd))
            return carry

        first_half = (lax.broadcasted_iota(jnp.int32, (tn, LANES), 1) & (HEAD_DIM - 1)) < HEAD_DIM // 2
        loads[0].wait()
        loads[1].wait()
        lax.fori_loop(0, SEQ // tn, pre_qk, 0)
        for cp in loads[2:5] + loads[6:7]:
            cp.wait()
        lax.fori_loop(0, SEQ // tn, pre_gate, 0)
        loads[5].wait()
        reloads = [pltpu.make_async_copy(pcol(C_AQ), lb_, sems.at[7]), pltpu.make_async_copy(pcol(C_AK), vb_, sems.at[8])]
        reloads[0].start()

        def load_kv(ref, d, start, prev):
            own = _rows_of(ref, start, d)[...]
            if prev is None:
                return own.astype(BF16)
            return jnp.concatenate([_rows_of(ref, prev, d)[...], own], axis=0).astype(BF16)

        def group(d, blocks):
            first = blocks[0][1] is None
            valid, lok = (valid1, lo) if first else (valid2, lo2)
            chains = [(b, h) for b in range(len(blocks)) for h in range(2)]
            mask = lambda h: lo if h == 0 else ~lo
            qs = [_rows_of(qb_, start, d)[...] for start, _ in blocks]
            dos = [_rows_of(yb_, start, d)[...] for start, _ in blocks]
            lds = [_rows_of(ob_, start, d)[...] for start, _ in blocks]
            ks = [load_kv(kb_, d, start, prev) for start, prev in blocks]
            vs = [load_kv(vb_, d, start, prev) for start, prev in blocks]
            qbs = [q.astype(BF16) for q in qs]
            dobs = [do.astype(BF16) for do in dos]
            ss = [_dot_nt(jnp.where(mask(h), qs[b], 0.0).astype(BF16), ks[b]) for b, h in chains]
            dps = [_dot_nt(jnp.where(mask(h), dos[b], 0.0).astype(BF16), vs[b]) for b, h in chains]
            pbs, dss = [], []
            for s, dp, (b, h) in zip(ss, dps, chains):
                hc, dc = h * HEAD_DIM, h * HEAD_DIM + HEAD_DIM // 2
                p = jnp.exp(jnp.where(valid, s, -jnp.inf) - lds[b][:, hc:hc + 1])
                pbs.append(p.astype(BF16))
                dss.append((p * (dp - lds[b][:, dc:dc + 1])).astype(BF16))
            dqs = [_dot(ds, ks[b]) for ds, (b, h) in zip(dss, chains)]
            dks = [_dot_tn(ds, qbs[b]) for ds, (b, h) in zip(dss, chains)]
            dvs = [_dot_tn(p, dobs[b]) for p, (b, h) in zip(pbs, chains)]
            assign = d == DILATIONS[0]
            for b, (start, prev) in enumerate(blocks):
                c0, c1 = 2 * b, 2 * b + 1
                dq_rows = _rows_of(gb_, start, d)
                dqb = jnp.where(lo, dqs[c0], dqs[c1])
                dq_rows[...] = dqb if assign else dq_rows[...] + dqb
                dkc = jnp.where(lok, dks[c0], dks[c1])
                dvc = jnp.where(lok, dvs[c0], dvs[c1])
                spans = ((start, slice(0, CHUNK), True),) if first else (
                    (prev, slice(0, CHUNK), False), (start, slice(CHUNK, 2 * CHUNK), True))
                for st, sl, own in spans:
                    dk_rows = _rows_of(dkb_, st, d)
                    dv_rows = _rows_of(dvb_, st, d)
                    if assign and own:
                        dk_rows[...] = dkc[sl]
                        dv_rows[...] = dvc[sl]
                    else:
                        dk_rows[...] = dk_rows[...] + dkc[sl]
                        dv_rows[...] = dv_rows[...] + dvc[sl]

        for d in DILATIONS:
            _for_blocks(d, group, ATTN_UNROLL)

        reloads[1].start()

        @pl.when(pair < npairs - 1)
        def _():
            nxt = input_loads(pair + 1)
            for n in early:
                nxt[n].start()

        for cp in reloads:
            cp.wait()

        def post(t, carry):
            gq_acc, gk_acc = carry
            rows = pl.ds(pl.multiple_of(t * tn, tn), tn)
            raws = [lb_[rows, :], vb_[rows, :]]
            dns = [gb_[rows, :], dkb_[rows, :]]
            rs = [lax.rsqrt(_headsum(a * a, bd) * (1.0 / HEAD_DIM) + EPS) for a in raws]
            zs = [a * r for a, r in zip(raws, rs)]
            dzs = [dn * gain for dn, gain in zip(dns, (gqs, gk))]
            means = [_headsum(dz * z, bd) * (1.0 / HEAD_DIM) for dz, z in zip(dzs, zs)]
            dq, dk = [r * (dz - z * mean) for r, dz, z, mean in zip(rs, dzs, zs, means)]
            gq, gkk = [jnp.sum(dn * z, axis=0, keepdims=True) for dn, z in zip(dns, zs)]
            dq_ref[rows, :] = dq.astype(BF16)
            dk_ref[rows, :] = dk.astype(BF16)
            dv_ref[rows, :] = dvb_[rows, :].astype(BF16)
            return gq_acc + gq * QK_SCALE, gk_acc + gkk

        zero = jnp.zeros((1, LANES), F32)
        gq_acc, gk_acc = lax.fori_loop(0, SEQ // tn, post, (zero, zero))
        gqg_ref[0:1, :] += gq_acc
        gkg_ref[0:1, :] += gk_acc

        @pl.when(pair == npairs - 1)
        def _():
            gqg_ref[0:1, :] = _fold_heads(gqg_ref[0:1, :])
            gkg_ref[0:1, :] = _fold_heads(gkg_ref[0:1, :])
            rs_stage[npairs]()

    hbm = pl.BlockSpec(memory_space=pl.ANY)
    vec = pl.BlockSpec((1, LANES), lambda p: (0, 0))
    blk8 = pl.BlockSpec((8, LANES), lambda p: (0, 0))
    out = pl.BlockSpec((SEQ, LANES), lambda p: (0, p))
    big = jax.ShapeDtypeStruct((SEQ, ATTN_W), BF16)
    nsem = RS_SEMS * nride
    return _call(
        body, name="attn_bwd", grid=(npairs,),
        in_specs=[hbm, hbm, hbm, hbm, vec, vec] + [hbm] * nride,
        out_specs=[out, out, out, out, blk8, blk8] + [hbm] * nride,
        out_shape=[big, big, big, big, jax.ShapeDtypeStruct((8, LANES), F32), jax.ShapeDtypeStruct((8, LANES), F32)]
        + [jax.ShapeDtypeStruct((2, g.shape[0] // 8, g.shape[1]), F32) for g in ride_along],
        scratch_shapes=[pltpu.VMEM((SEQ, LANES), F32) for _ in range(9)] + [pltpu.SemaphoreType.DMA((9,))]
        + _rs_scratch([g.shape for g in ride_along]) + [pltpu.SemaphoreType.DMA((nsem,)), pltpu.SemaphoreType.DMA((nsem,)),
                                     pltpu.SemaphoreType.DMA((nride,))],
        compiler_params=_params(),
    )(proj, o, lse, dyc, gq2, gk2, *[_rs_view(g) for g in ride_along])


def _mem_kv(mem, gain, wkv):
    def body(m_ref, g_ref, w_ref, kv_ref, hm_ref):
        mv = m_ref[...]
        ms = jnp.mean(mv * mv, axis=-1, keepdims=True)
        hm = (mv * lax.rsqrt(ms + EPS) * g_ref[...]).astype(BF16)
        hm_ref[...] = hm
        kv_ref[...] = _dot(hm, w_ref[...])

    return _call(
        body, name="mem_kv",
        out_shape=[jax.ShapeDtypeStruct((MEM_LEN, 2 * MEM_W), F32), jax.ShapeDtypeStruct((MEM_LEN, D_MODEL), BF16)],
        compiler_params=_params(),
    )(mem, gain, wkv)


def _mem_keys(kv_ref, kg_ref, bd, p):
    mk = kv_ref[:, p * LANES:(p + 1) * LANES]
    r = lax.rsqrt(_headsum(mk * mk, bd) * (1.0 / HEAD_DIM) + EPS)
    z = mk * r
    mkn = (z * kg_ref[:, p * LANES:(p + 1) * LANES]).astype(BF16)
    mvp = kv_ref[:, MEM_W + p * LANES:MEM_W + (p + 1) * LANES].astype(BF16)
    return mkn, mvp, r, z


def _mem_fwd(proj, kv, qg4, kg4):
    tm = 1024

    def body(q_ref, g_ref, kv_ref, qg_ref, kg_ref, om_ref, ym_ref):
        bd = _head_blockdiag()
        lo = _lo_mask(tm)
        keys, qns = [], []
        for p in range(2):
            cs = slice(p * LANES, (p + 1) * LANES)
            keys.append(_mem_keys(kv_ref, kg_ref, bd, p)[:2])
            q = q_ref[:, cs]
            qns.append(q * lax.rsqrt(_headsum(q * q, bd) * (1.0 / HEAD_DIM) + EPS) * (qg_ref[:, cs] * QK_SCALE))
        chains = [(p, h) for p in range(2) for h in range(2)]
        ss = [_dot_nt(jnp.where(lo if h == 0 else ~lo, qns[p], 0.0).astype(BF16), keys[p][0]) for p, h in chains]
        es = [jnp.exp(s - jnp.max(s, axis=-1, keepdims=True)) for s in ss]
        os_ = [_dot(e.astype(BF16), keys[p][1]) for e, (p, h) in zip(es, chains)]
        res = [o * (1.0 / jnp.sum(e, axis=-1, keepdims=True)) for o, e in zip(os_, es)]
        for p in range(2):
            cs = slice(p * LANES, (p + 1) * LANES)
            ov = jnp.where(lo, res[2 * p], res[2 * p + 1])
            g = g_ref[:, cs]
            om_ref[:, cs] = ov
            ym_ref[:, cs] = (ov * (g * _sigmoid(g))).astype(BF16)

    vec = pl.BlockSpec((1, MEM_W), lambda i: (0, 0))
    return _call(
        body, name="mem_fwd", grid=(SEQ // tm,),
        in_specs=[pl.BlockSpec((tm, MEM_W), lambda i: (i, C_MQ // MEM_W)),
                  pl.BlockSpec((tm, MEM_W), lambda i: (i, C_MG // MEM_W)),
                  pl.BlockSpec((MEM_LEN, 2 * MEM_W), lambda i: (0, 0)), vec, vec],
        out_specs=[pl.BlockSpec((tm, MEM_W), lambda i: (i, 0)), pl.BlockSpec((tm, MEM_W), lambda i: (i, 0))],
        out_shape=[jax.ShapeDtypeStruct((SEQ, MEM_W), F32), jax.ShapeDtypeStruct((SEQ, MEM_W), BF16)],
        compiler_params=_params(),
    )(proj, proj, kv, qg4, kg4)


def _mem_bwd(proj, om, dyc, kv, hm, mem, mgain, wkv, qg4, kg4):
    tm = 1024
    nsteps = SEQ // tm

    def body(q_ref, g_ref, om_ref, dy_ref, kv_ref, hm_ref, mem_ref, mg_ref, w_ref, qg_ref, kg_ref,
             dq_ref, dgt_ref, gqg_ref, gkg_ref, gw_ref, gmg_ref, dmk_ref, dmv_ref, gq_acc):
        i = pl.program_id(0)
        bd = _head_blockdiag()
        lo = _lo_mask(tm)
        lom = _lo_mask(MEM_LEN)

        @pl.when(i == 0)
        def _():
            dmk_ref[...] = jnp.zeros_like(dmk_ref)
            dmv_ref[...] = jnp.zeros_like(dmv_ref)
            gq_acc[...] = jnp.zeros_like(gq_acc)

        pairs = []
        for p in range(2):
            cs = slice(p * LANES, (p + 1) * LANES)
            mkn, mvp, _, _ = _mem_keys(kv_ref, kg_ref, bd, p)
            gqs = qg_ref[:, cs] * QK_SCALE
            q = q_ref[:, cs]
            r = lax.rsqrt(_headsum(q * q, bd) * (1.0 / HEAD_DIM) + EPS)
            z = q * r
            qn = z * gqs
            g = g_ref[:, cs]
            ov = om_ref[:, cs]
            dym = dy_ref[:, cs]
            sg = _sigmoid(g)
            dgt_ref[:, cs] = (dym * ov * (sg * (1.0 + g * (1.0 - sg)))).astype(BF16)
            do = dym * (g * sg)
            pairs.append(dict(cs=cs, mkn=mkn, mvp=mvp, gqs=gqs, r=r, z=z, qn=qn, qnb=qn.astype(BF16), do=do,
                              dob=do.astype(BF16), delta=_headsum(do * ov, bd)))
        chains = [(pr_, h) for pr_ in pairs for h in range(2)]
        mask = lambda h: lo if h == 0 else ~lo
        ss = [_dot_nt(jnp.where(mask(h), c["qn"], 0.0).astype(BF16), c["mkn"]) for c, h in chains]
        dps = [_dot_nt(jnp.where(mask(h), c["do"], 0.0).astype(BF16), c["mvp"]) for c, h in chains]
        prs, dss = [], []
        for s, dp, (c, h) in zip(ss, dps, chains):
            e = jnp.exp(s - jnp.max(s, axis=-1, keepdims=True))
            pr = e * (1.0 / jnp.sum(e, axis=-1, keepdims=True))
            prs.append(pr.astype(BF16))
            dss.append((pr * (dp - c["delta"][:, h * HEAD_DIM:h * HEAD_DIM + 1])).astype(BF16))
        dqs = [_dot(ds, c["mkn"]) for ds, (c, h) in zip(dss, chains)]
        dks = [_dot_tn(ds, c["qnb"]) for ds, (c, h) in zip(dss, chains)]
        dvs = [_dot_tn(pr, c["dob"]) for pr, (c, h) in zip(prs, chains)]
        for p, c in enumerate(pairs):
            cs, z, r = c["cs"], c["z"], c["r"]
            dqn = jnp.where(lo, dqs[2 * p], dqs[2 * p + 1])
            dmk_ref[:, cs] += jnp.where(lom, dks[2 * p], dks[2 * p + 1])
            dmv_ref[:, cs] += jnp.where(lom, dvs[2 * p], dvs[2 * p + 1])
            dz = dqn * c["gqs"]
            dq_ref[:, cs] = (r * (dz - z * (_headsum(dz * z, bd) * (1.0 / HEAD_DIM)))).astype(BF16)
            gq_acc[:, cs] += jnp.sum(dqn * z, axis=0, keepdims=True) * QK_SCALE

        @pl.when(i == nsteps - 1)
        def _():
            gqg_ref[...] = jnp.zeros_like(gqg_ref)
            gkg_ref[...] = jnp.zeros_like(gkg_ref)
            gqg_ref[0:1, :] = _fold_heads(gq_acc[:, 0:LANES] + gq_acc[:, LANES:2 * LANES])
            dkv = []
            gk = jnp.zeros((1, LANES), F32)
            for p in range(2):
                cs = slice(p * LANES, (p + 1) * LANES)
                _, _, r, z = _mem_keys(kv_ref, kg_ref, bd, p)
                dn = dmk_ref[:, cs]
                dz = dn * kg_ref[:, cs]
                gk = gk + jnp.sum(dn * z, axis=0, keepdims=True)
                dkv.append(r * (dz - z * (_headsum(dz * z, bd) * (1.0 / HEAD_DIM))))
            gkg_ref[0:1, :] = _fold_heads(gk)
            dkvb = jnp.concatenate(dkv + [dmv_ref[...]], axis=1).astype(BF16)
            gw_ref[...] = _dot_tn(hm_ref[...], dkvb)
            dhm = _dot_nt(dkvb, w_ref[...])
            mv = mem_ref[...]
            zm = mv * lax.rsqrt(jnp.mean(mv * mv, axis=-1, keepdims=True) + EPS)
            _put_rows(gmg_ref, jnp.sum(dhm * zm, axis=0, keepdims=True))

    const = lambda shape: pl.BlockSpec(shape, lambda i: (0,) * len(shape))
    row = lambda j: pl.BlockSpec((tm, MEM_W), lambda i: (i, j))
    blk8 = jax.ShapeDtypeStruct((8, LANES), F32)
    return _call(
        body, name="mem_bwd", grid=(nsteps,),
        in_specs=[row(C_MQ // MEM_W), row(C_MG // MEM_W), row(0), row((GMLP_W + ATTN_W) // MEM_W),
                  const((MEM_LEN, 2 * MEM_W)), const((MEM_LEN, D_MODEL)), const((MEM_LEN, D_MODEL)),
                  const((1, D_MODEL)), const((D_MODEL, 2 * MEM_W)), const((1, MEM_W)), const((1, MEM_W))],
        out_specs=[row(0), row(0), const((8, LANES)), const((8, LANES)),
                   const((D_MODEL, 2 * MEM_W)), const((8, LANES))],
        out_shape=[jax.ShapeDtypeStruct((SEQ, MEM_W), BF16), jax.ShapeDtypeStruct((SEQ, MEM_W), BF16),
                   blk8, blk8, jax.ShapeDtypeStruct((D_MODEL, 2 * MEM_W), F32), blk8],
        scratch_shapes=[pltpu.VMEM((MEM_LEN, MEM_W), F32), pltpu.VMEM((MEM_LEN, MEM_W), F32),
                        pltpu.VMEM((1, MEM_W), F32)],
        compiler_params=_params(),
    )(proj, proj, om, dyc, kv, hm, mem, mgain, wkv, qg4, kg4)


def _out_loss(yg, ya, ym, x, tgt, wo):
    tm = 512
    nsteps = SEQ // tm
    parts = ((0, GMLP_W), (GMLP_W, ATTN_W), (GMLP_W + ATTN_W, MEM_W))

    def body(yg_ref, ya_ref, ym_ref, x_ref, t_ref, w_ref, dy_ref, dyc_ref, gw_ref, ls_ref):
        i = pl.program_id(0)

        @pl.when(i == 0)
        def _():
            gw_ref[...] = jnp.zeros_like(gw_ref)
            ls_ref[...] = jnp.zeros_like(ls_ref)

        ys = (yg_ref[...], ya_ref[...], ym_ref[...])
        y = sum(_dot(yv, w_ref[r0:r0 + n, :]) for yv, (r0, n) in zip(ys, parts))
        err = x_ref[...] + y - t_ref[...]
        _put_rows(ls_ref, jnp.sum(err * err, axis=0, keepdims=True), accumulate=True)
        dy = err * (1.0 / D_MODEL)
        dy_ref[...] = dy
        dyb = dy.astype(BF16)
        dyc_ref[...] = _dot_nt(dyb, w_ref[...])
        for yv, (r0, n) in zip(ys, parts):
            gw_ref[r0:r0 + n, :] += _dot_tn(yv, dyb)

    row = lambda w: pl.BlockSpec((tm, w), lambda i: (i, 0))
    const = lambda shape: pl.BlockSpec(shape, lambda i: (0, 0))
    return _call(
        body, name="out_loss", grid=(nsteps,),
        in_specs=[row(GMLP_W), row(ATTN_W), row(MEM_W), row(D_MODEL), row(D_MODEL), const((D_MODEL, D_MODEL))],
        out_specs=[row(D_MODEL), row(D_MODEL), const((D_MODEL, D_MODEL)), const((8, LANES))],
        out_shape=[jax.ShapeDtypeStruct((SEQ, D_MODEL), F32), jax.ShapeDtypeStruct((SEQ, D_MODEL), F32),
                   jax.ShapeDtypeStruct((D_MODEL, D_MODEL), F32), jax.ShapeDtypeStruct((8, LANES), F32)],
        compiler_params=_params(),
    )(yg, ya, ym, x, tgt, wo)


def _proj_bwd(x, dy, gain, wt, dg, daq, dak, dav, dag, dmq, dmg):
    tm = 512
    nsteps = SEQ // tm
    pieces = ((C_GU, 3 * GMLP_W), (C_AQ, ATTN_W), (C_AK, ATTN_W), (C_AV, ATTN_W), (C_AG, ATTN_W),
              (C_MQ, MEM_W), (C_MG, MEM_W))

    def body(x_ref, dy_ref, g_ref, wt_hbm, p0, p1, p2, p3, p4, p5, p6, gx_ref, gwt_hbm, gg_ref, wt_v, acc, wt_sem, out_sems):
        i = pl.program_id(0)
        wt_load = pltpu.make_async_copy(wt_hbm, wt_v, wt_sem)

        @pl.when(i == 0)
        def _():
            wt_load.start()
            acc[...] = jnp.zeros_like(acc)
            gg_ref[...] = jnp.zeros_like(gg_ref)

        xv = x_ref[...]
        r = lax.rsqrt(jnp.mean(xv * xv, axis=-1, keepdims=True) + EPS)
        z = xv * r
        g = g_ref[...]
        h = (z * g).astype(BF16)
        pl.when(i == 0)(wt_load.wait)
        flush = [pltpu.make_async_copy(acc.at[c0:c0 + w, :], gwt_hbm.at[c0:c0 + w, :], out_sems.at[n])
                 for n, (c0, w) in enumerate(pieces)]
        dh = jnp.zeros((tm, D_MODEL), F32)
        for n, (pref, (c0, w)) in enumerate(zip((p0, p1, p2, p3, p4, p5, p6), pieces)):
            dp = pref[...]
            dh = dh + _dot(dp, wt_v[c0:c0 + w, :])
            acc[c0:c0 + w, :] += _dot_tn(dp, h)
            pl.when(i == nsteps - 1)(flush[n].start)
        _put_rows(gg_ref, jnp.sum(dh * z, axis=0, keepdims=True), accumulate=True)
        dz = dh * g
        gx_ref[...] = dy_ref[...] + r * (dz - z * jnp.mean(dz * z, axis=-1, keepdims=True))

        @pl.when(i == nsteps - 1)
        def _():
            for cp in flush:
                cp.wait()

    row = lambda w: pl.BlockSpec((tm, w), lambda i: (i, 0))
    hbm = pl.BlockSpec(memory_space=pl.ANY)
    vec = pl.BlockSpec((1, D_MODEL), lambda i: (0, 0))
    return _call(
        body, name="proj_bwd", grid=(nsteps,),
        in_specs=[row(D_MODEL), row(D_MODEL), vec, hbm] + [row(w) for _, w in pieces],
        out_specs=[row(D_MODEL), hbm, pl.BlockSpec((8, LANES), lambda i: (0, 0))],
        out_shape=[jax.ShapeDtypeStruct((SEQ, D_MODEL), F32), jax.ShapeDtypeStruct((IN_W, D_MODEL), F32),
                   jax.ShapeDtypeStruct((8, LANES), F32)],
        scratch_shapes=[pltpu.VMEM((IN_W, D_MODEL), BF16), pltpu.VMEM((IN_W, D_MODEL), F32), pltpu.SemaphoreType.DMA,
                        pltpu.SemaphoreType.DMA((len(pieces),))],
        compiler_params=_params(),
    )(x, dy, gain, wt, dg, daq, dak, dav, dag, dmq, dmg)


AG_SEMS = 8


def _gather_stages(ins, lands, send_sems, recv_sems):
    n = len(ins)
    nrows = [a.shape[0] for a in ins]
    x, y, c = lax.axis_index("x"), lax.axis_index("y"), lax.axis_index("c")
    sib, xn, yn = (x, y, 1 - c), (1 - x, y, c), (x, 1 - y, c)
    me, cx, cy, cd = 2 * x + y, 2 * (1 - x) + y, 2 * x + (1 - y), 2 * (1 - x) + (1 - y)

    def part(a, chip, hf, quarter=None):
        rows = nrows[a] // 2
        base = chip * nrows[a] + hf * rows
        if quarter is not None:
            rows = rows // 2
            base = base + quarter * rows
        return lands[a].at[pl.ds(pl.multiple_of(base, 16), rows), :]

    def copy(a, j, ref, to):
        k = AG_SEMS * a + j
        return pltpu.make_async_remote_copy(src_ref=ref, dst_ref=ref, send_sem=send_sems.at[k],
                                            recv_sem=recv_sems.at[k], device_id=to, device_id_type=MESH)

    def own(a):
        return [copy(a, 0, part(a, me, c), xn), copy(a, 1, part(a, me, c), yn)]

    def neighbours(a):
        return [copy(a, 4, part(a, cx, c, 1), yn), copy(a, 2, part(a, cx, c), sib),
                copy(a, 5, part(a, cy, c, 0), xn), copy(a, 3, part(a, cy, c), sib)]

    def diagonal(a):
        return [copy(a, 7, part(a, cd, c, 1), sib), copy(a, 6, part(a, cd, c, 0), sib)]

    def send_own():
        for a in range(n):
            lands[a][pl.ds(pl.multiple_of(me * nrows[a], 16), nrows[a]), :] = ins[a][...].astype(BF16)
            for cp in own(a):
                cp.start()

    def pass_on_neighbours():
        for a in range(n):
            copy(a, 0, part(a, cx, c), xn).wait_recv()
            copy(a, 1, part(a, cy, c), yn).wait_recv()
            for cp in neighbours(a):
                cp.start()

    def pass_on_diagonal():
        for a in range(n):
            copy(a, 4, part(a, cd, c, 1), yn).wait_recv()
            copy(a, 5, part(a, cd, c, 0), xn).wait_recv()
            for cp in diagonal(a):
                cp.start()

    def y_complete():
        for a in range(n):
            copy(a, 3, part(a, cy, 1 - c), sib).wait_recv()

    def x_complete():
        for a in range(n):
            copy(a, 2, part(a, cx, 1 - c), sib).wait_recv()

    def diagonal_complete():
        for a in range(n):
            copy(a, 6, part(a, cd, 1 - c, 0), sib).wait_recv()
            copy(a, 7, part(a, cd, 1 - c, 1), sib).wait_recv()

    def sends_done():
        for a in range(n):
            for cp in own(a) + neighbours(a) + diagonal(a):
                cp.wait_send()

    def finish():
        y_complete()
        x_complete()
        diagonal_complete()
        sends_done()

    return (send_own, pass_on_neighbours, pass_on_diagonal, finish), (y_complete, x_complete, diagonal_complete, sends_done)


RS_SEMS = 6
RS_KINDS = (((2, 2), 1, F32), ((2, 2), 1, F32), ((2, 2), 2, BF16), ((2, 2), 2, BF16), ((2, 2), 2, F32),
            ((2,), 2, BF16), ((2,), 2, BF16), ((2,), 1, F32))


def _rs_view(g):
    return g.reshape(2, 2, 2, g.shape[0] // 8, g.shape[1])


def _rs_scratch(shapes):
    return [pltpu.VMEM(lead + (r // 8, w // split), dt) for lead, split, dt in RS_KINDS for r, w in shapes]


def _rs_stages(gs, outs, bufs, send_sems, recv_sems, local_sems, widths):
    n = len(gs)
    loc, ra, s_b, r_b, acc1, s_c, r_c, fin = (bufs[n * i:n * i + n] for i in range(len(RS_KINDS)))
    half_w = [w // 2 for w in widths]
    x, y, c = lax.axis_index("x"), lax.axis_index("y"), lax.axis_index("c")
    sib, xn, yn = (x, y, 1 - c), (1 - x, y, c), (x, 1 - y, c)

    def copy(a, j, src, dst, to):
        k = RS_SEMS * a + j
        return pltpu.make_async_remote_copy(src_ref=src, dst_ref=dst, send_sem=send_sems.at[k],
                                            recv_sem=recv_sems.at[k], device_id=to, device_id_type=MESH)

    def step_a(a):
        return [copy(a, 0, gs[a].at[:, :, 1 - c], ra[a], sib),
                pltpu.make_async_copy(gs[a].at[:, :, c], loc[a], local_sems.at[a])]

    def step_b(a):
        return copy(a, 1, s_b[a].at[0], r_b[a].at[0], xn), copy(a, 2, s_b[a].at[1], r_b[a].at[1], yn)

    def step_c(a):
        return copy(a, 3, s_c[a].at[0], r_c[a].at[0], yn), copy(a, 4, s_c[a].at[1], r_c[a].at[1], xn)

    def step_d(a, half):
        rows = fin[a].at[half]
        return copy(a, 5, rows, rows, sib)

    def start():
        for a in range(n):
            for cp in step_a(a):
                cp.start()

    def a_to_b():
        for a in range(n):
            for cp in step_a(a):
                cp.wait()
            ra[a][...] = loc[a][...] + ra[a][...]
            s_b[a][0] = ra[a][1 - x, :, :, :half_w[a]].astype(BF16)
            s_b[a][1] = ra[a][:, 1 - y, :, half_w[a]:].astype(BF16)
            for cp in step_b(a):
                cp.start()

    def b_to_c():
        for a in range(n):
            for cp in step_b(a):
                cp.wait()
            acc1[a][0] = ra[a][x, :, :, :half_w[a]] + r_b[a][0].astype(F32)
            acc1[a][1] = ra[a][:, y, :, half_w[a]:] + r_b[a][1].astype(F32)
            s_c[a][0] = acc1[a][0, 1 - y].astype(BF16)
            s_c[a][1] = acc1[a][1, 1 - x].astype(BF16)
            for cp in step_c(a):
                cp.start()

    def c_to_d():
        for a in range(n):
            for cp in step_c(a):
                cp.wait()
            fin[a][c, :, :half_w[a]] = acc1[a][0, y] + r_c[a][0].astype(F32)
            fin[a][c, :, half_w[a]:] = acc1[a][1, x] + r_c[a][1].astype(F32)
            step_d(a, c).start()

    def finish():
        to_hbm = [pltpu.make_async_copy(fin[a], outs[a], local_sems.at[a]) for a in range(n)]
        for a in range(n):
            step_d(a, 1 - c).wait_recv()
            step_d(a, c).wait_send()
            to_hbm[a].start()
        for cp in to_hbm:
            cp.wait()

    return start, a_to_b, b_to_c, c_to_d, finish


def _reduce_grads(gwt, g_ws, tiny):
    cw = gwt.shape[1] // RS_CHUNKS
    chunk_shape = (gwt.shape[0], cw)

    def body(g0, ws_in, tiny_in, *rest):
        outs, o_ws, o_tiny = rest[:RS_CHUNKS], rest[RS_CHUNKS], rest[RS_CHUNKS + 1]
        rest = rest[RS_CHUNKS + 2:]
        nb = len(RS_KINDS) * RS_CHUNKS
        sm, sa, sb, sc, acc_s, send_sems, recv_sems, local_sems = rest[nb:]
        blocks = [g0.at[:, :, :, :, pl.ds(j * cw, cw)] for j in range(RS_CHUNKS)]
        start, a_to_b, b_to_c, c_to_d, finish = _rs_stages(blocks, outs, rest[:nb], send_sems, recv_sems, local_sems,
                                                           [cw] * RS_CHUNKS)
        n_ws = ws_in.shape[0]
        sm[0:n_ws, :] = ws_in[...]
        sm[n_ws:, :] = tiny_in[...]
        x, y, c = lax.axis_index("x"), lax.axis_index("y"), lax.axis_index("c")

        def small(j, src, dst, to):
            k = RS_SEMS * RS_CHUNKS + j
            return pltpu.make_async_remote_copy(src_ref=src, dst_ref=dst, send_sem=send_sems.at[k],
                                                recv_sem=recv_sems.at[k], device_id=to, device_id_type=MESH)

        along_c, along_x, along_y = (small(0, sm, sa, (x, y, 1 - c)), small(1, acc_s, sb, (1 - x, y, c)),
                                     small(2, sb, sc, (x, 1 - y, c)))
        start()
        along_c.start()
        a_to_b()
        along_c.wait()
        acc_s[...] = sm[...] + sa[...]
        along_x.start()
        b_to_c()
        along_x.wait()
        sb[...] = acc_s[...] + sb[...]
        along_y.start()
        c_to_d()
        along_y.wait()
        o_ws[...] = sb[0:n_ws, :] + sc[0:n_ws, :]
        o_tiny[...] = sb[n_ws:, :] + sc[n_ws:, :]
        finish()

    vm = pl.BlockSpec(memory_space=pltpu.VMEM)
    hbm = pl.BlockSpec(memory_space=pl.ANY)
    small_shape = (g_ws.shape[0] + tiny.shape[0], LANES)
    scratch = _rs_scratch([chunk_shape] * RS_CHUNKS) + [pltpu.VMEM(small_shape, F32) for _ in range(5)]
    nsem = RS_SEMS * RS_CHUNKS + 3
    scratch += [pltpu.SemaphoreType.DMA((nsem,)), pltpu.SemaphoreType.DMA((nsem,)), pltpu.SemaphoreType.DMA((RS_CHUNKS,))]
    return _call(
        body, name="reduce_grads",
        out_shape=[jax.ShapeDtypeStruct((2, gwt.shape[0] // 8, cw), F32)] * RS_CHUNKS
        + [jax.ShapeDtypeStruct(g_ws.shape, F32), jax.ShapeDtypeStruct(tiny.shape, F32)],
        in_specs=[hbm, vm, vm],
        out_specs=[hbm] * RS_CHUNKS + [vm, vm],
        scratch_shapes=scratch,
        compiler_params=_params(),
    )(_rs_view(gwt), g_ws, tiny)


def _adam_update(w, g, m, v):
    nm = ADAM_B1 * m + (1.0 - ADAM_B1) * g
    nv = ADAM_B2 * v + (1.0 - ADAM_B2) * (g * g)
    m_hat = nm / (1.0 - ADAM_B1 ** ADAM_STEP)
    v_hat = nv / (1.0 - ADAM_B2 ** ADAM_STEP)
    return -ADAM_LR * (m_hat / (jnp.sqrt(v_hat) + ADAM_EPS) + ADAM_WD * w), nm, nv


def _adamw(w, g, m, v):
    rows, cols = w.shape
    tm = max(t for t in range(8, 257, 8) if rows % t == 0)
    parts = tuple(g) if isinstance(g, (tuple, list)) else (g,)
    n = len(parts)

    def body(w_ref, m_ref, v_ref, *refs):
        gv = jnp.concatenate([r[...] for r in refs[:n]], axis=1)
        d_ref, nm_ref, nv_ref = refs[n:n + 3]
        d_ref[...], nm_ref[...], nv_ref[...] = _adam_update(w_ref[...], gv, m_ref[...], v_ref[...])
        if n > 1:
            refs[n + 3][...] = gv

    blk = pl.BlockSpec((tm, cols), lambda i: (i, 0))
    nout = 3 if n == 1 else 4
    res = _call(
        body, name="adamw", grid=(rows // tm,),
        in_specs=[blk] * 3 + [pl.BlockSpec((tm, p.shape[1]), lambda i: (i, 0)) for p in parts], out_specs=[blk] * nout,
        out_shape=[jax.ShapeDtypeStruct((rows, cols), F32)] * nout,
        compiler_params=_params(),
    )(w, m, v, *parts)
    return (parts[0] if n == 1 else res[3], *res[:3])


def _adamw_tiny(tiny, weights, ms, vs):
    shapes = [w.shape for w in weights]
    n = len(weights)

    def grad_of(t_ref, k, shape):
        base = 8 * k
        if shape[1] > LANES:
            return [t_ref[base + j:base + j + 1, :] for j in range(shape[1] // LANES)]
        return [t_ref[base:base + shape[0], 0:shape[1]]]

    def body(t_ref, *refs):
        w_refs, m_refs, v_refs = refs[:n], refs[n:2 * n], refs[2 * n:3 * n]
        loss_ref, outs = refs[3 * n], refs[3 * n + 1:]
        loss_ref[...] = (0.5 / D_MODEL) * jnp.sum(t_ref[8 * n:8 * n + 8, :], keepdims=True)
        for k, shape in enumerate(shapes):
            g_ref, d_ref, nm_ref, nv_ref = outs[4 * k:4 * k + 4]
            for j, g in enumerate(grad_of(t_ref, k, shape)):
                cols = slice(j * LANES, (j + 1) * LANES) if shape[1] > LANES else slice(None)
                g_ref[:, cols] = g
                d_ref[:, cols], nm_ref[:, cols], nv_ref[:, cols] = _adam_update(
                    w_refs[k][:, cols], g, m_refs[k][:, cols], v_refs[k][:, cols])

    out_shape = [jax.ShapeDtypeStruct((1, 1), F32)]
    for shape in shapes:
        out_shape += [jax.ShapeDtypeStruct(shape, F32)] * 4
    return _call(body, name="adamw_tiny", out_shape=out_shape, compiler_params=_params())(tiny, *weights, *ms, *vs)


def _local_grads(x, mem, tgt, norm_gain, wt_sh, gmlp_v_gain, gmlp_w_s, gmlp_b, attn_q_gain, attn_k_gain,
                 mem_norm_gain, wkv_sh, mem_q_gain, mem_k_gain, wo_sh):
    vg = gmlp_v_gain.reshape(1, GMLP_W)
    bias_full = jnp.repeat(gmlp_b.T, HEAD_DIM, axis=1)
    gq2, gk2 = jnp.tile(attn_q_gain, (1, 2)), jnp.tile(attn_k_gain, (1, 2))
    qg4, kg4 = jnp.tile(mem_q_gain, (1, 4)), jnp.tile(mem_k_gain, (1, 4))

    proj, wt = _gather_proj(x, norm_gain, wt_sh)
    yg = _gmlp_fwd(proj, vg, gmlp_w_s, bias_full)
    o, lse, ya, wkv, wo = _attn_fwd(proj, gq2, gk2, wkv_sh, wo_sh)
    kv, hm = _mem_kv(mem, mem_norm_gain, wkv)
    om, ym = _mem_fwd(proj, kv, qg4, kg4)
    dy, dyc, g_wo, err2 = _out_loss(yg, ya, ym, x, tgt, wo)
    dmq, dmg, g_mq, g_mk, g_wkv, g_mng = _mem_bwd(proj, om, dyc, kv, hm, mem, mem_norm_gain, wkv, qg4, kg4)
    daq, dak, dav, dag, g_aq, g_ak, g_wkv_sh, g_wo_sh = _attn_bwd(proj, o, lse, dyc, gq2, gk2, g_wkv, g_wo)
    dg, g_ws, g_b, g_vg = _gmlp_bwd(proj, dyc, vg, gmlp_w_s, bias_full)
    gx, g_wt, g_ng = _proj_bwd(x, dy, norm_gain, wt, dg, daq, dak, dav, dag, dmq, dmg)

    tiny = jnp.concatenate([g_ng, g_vg, g_b, g_aq, g_ak, g_mng, g_mq, g_mk, err2], axis=0)
    return gx, g_wt, g_wkv_sh, g_wo_sh, g_ws.reshape(4 * CHUNK, CHUNK), tiny


def kernel(x, mem, norm_gain, w_in, gmlp_v_gain, gmlp_w_s, gmlp_b, attn_q_gain, attn_k_gain, mem_norm_gain, w_mem_kv, mem_q_gain, mem_k_gain, w_out, loss_target, m_norm_gain, m_w_in, m_gmlp_v_gain, m_gmlp_w_s, m_gmlp_b, m_attn_q_gain, m_attn_k_gain, m_mem_norm_gain, m_w_mem_kv, m_mem_q_gain, m_mem_k_gain, m_w_out, v_norm_gain, v_w_in, v_gmlp_v_gain, v_gmlp_w_s, v_gmlp_b, v_attn_q_gain, v_attn_k_gain, v_mem_norm_gain, v_w_mem_kv, v_mem_q_gain, v_mem_k_gain, v_w_out):
    gx, g_wt, g_wkv_sh, g_wo_sh, g_ws, tiny = _local_grads(
        x[0], mem[0], loss_target[0], norm_gain, w_in[0].T, gmlp_v_gain[0], gmlp_w_s[0], gmlp_b[0],
        attn_q_gain, attn_k_gain, mem_norm_gain, w_mem_kv[0], mem_q_gain, mem_k_gain, w_out[0])
    *g_wt_sh, g_ws, tiny = _reduce_grads(g_wt, g_ws, tiny)
    chip_block = lambda g: g.reshape(2 * g.shape[1], g.shape[2])
    g_wt_sh = tuple(chip_block(g) for g in g_wt_sh)
    g_wkv_sh, g_wo_sh = chip_block(g_wkv_sh), chip_block(g_wo_sh)

    ws = (norm_gain, w_in, gmlp_v_gain, gmlp_w_s, gmlp_b, attn_q_gain, attn_k_gain, mem_norm_gain, w_mem_kv,
          mem_q_gain, mem_k_gain, w_out)
    ms = (m_norm_gain, m_w_in, m_gmlp_v_gain, m_gmlp_w_s, m_gmlp_b, m_attn_q_gain, m_attn_k_gain, m_mem_norm_gain,
          m_w_mem_kv, m_mem_q_gain, m_mem_k_gain, m_w_out)
    vs = (v_norm_gain, v_w_in, v_gmlp_v_gain, v_gmlp_w_s, v_gmlp_b, v_attn_q_gain, v_attn_k_gain, v_mem_norm_gain,
          v_w_mem_kv, v_mem_q_gain, v_mem_k_gain, v_w_out)
    form = {1: lambda a: a[0].T, 3: lambda a: a.reshape(4 * CHUNK, CHUNK), 2: lambda a: a[0], 4: lambda a: a[0],
            8: lambda a: a[0], 11: lambda a: a[0]}
    back = {1: lambda a: a.T[None], 3: lambda a: a.reshape(1, 4, CHUNK, CHUNK), 2: lambda a: a[None],
            4: lambda a: a[None], 8: lambda a: a[None], 11: lambda a: a[None]}
    fwd = lambda t, i: form.get(i, lambda a: a)(t[i])
    out = {}
    for i, g in ((1, g_wt_sh), (3, g_ws), (8, g_wkv_sh), (11, g_wo_sh)):
        out[i] = _adamw(fwd(ws, i), g, fwd(ms, i), fwd(vs, i))
    res = _adamw_tiny(tiny, [fwd(ws, i) for i in TINY_ORDER], [fwd(ms, i) for i in TINY_ORDER],
                      [fwd(vs, i) for i in TINY_ORDER])
    for k, i in enumerate(TINY_ORDER):
        out[i] = res[1 + 4 * k:5 + 4 * k]
    leaves = [[back.get(i, lambda a: a)(out[i][j]) for i in range(12)] for j in range(4)]
    return (res[0].reshape(()), gx[None], *leaves[0], *leaves[1], *leaves[2], *leaves[3])
```

```python
import math

import jax
import jax.numpy as jnp
from jax import lax
from jax.experimental import pallas as pl
from jax.experimental.pallas import tpu as pltpu

F32 = jnp.float32
BF16 = jnp.bfloat16

SEQ = 4096
D_MODEL = 1024
HEAD_DIM = 64
LANES = 128
CHUNK = 128
GMLP_W, ATTN_W, MEM_W = 256, 512, 256
IN_W = 3 * GMLP_W + 4 * ATTN_W + 2 * MEM_W
MEM_LEN = 256
DILATIONS = (16, 4, 1)
EPS = 1e-6
QK_SCALE = 1.0 / math.sqrt(HEAD_DIM)
C_GU, C_GV, C_GG, C_AQ, C_AK, C_AV, C_AG, C_MQ, C_MG = 0, 256, 512, 768, 1280, 1792, 2304, 2816, 3072

ADAM_LR, ADAM_B1, ADAM_B2, ADAM_EPS, ADAM_WD, ADAM_STEP = 0.001, 0.9, 0.999, 1e-08, 0.01, 10

VMEM_LIMIT = 48 * 1024 * 1024
RS_CHUNKS = 4
ATTN_UNROLL = 4
MESH = pl.DeviceIdType.MESH

TINY_ORDER = (0, 2, 4, 5, 6, 7, 9, 10)


def _call(body, **kw):
    return pl.pallas_call(body, **kw)


def _params(**kw):
    return pltpu.CompilerParams(vmem_limit_bytes=VMEM_LIMIT, **kw)


def _dot(a, b):
    return jnp.dot(a, b, preferred_element_type=F32)


def _dot_nt(a, b):
    return lax.dot_general(a, b, (((1,), (1,)), ((), ())), preferred_element_type=F32)


def _dot_tn(a, b):
    return lax.dot_general(a, b, (((0,), (0,)), ((), ())), preferred_element_type=F32)


def _head_blockdiag():
    r = lax.shift_right_logical(lax.broadcasted_iota(jnp.int32, (LANES, LANES), 0), 6)
    c = lax.shift_right_logical(lax.broadcasted_iota(jnp.int32, (LANES, LANES), 1), 6)
    return jnp.where(r == c, 1.0, 0.0).astype(BF16)


def _headsum(v, bd):
    hi = v.astype(BF16)
    lo = (v - hi.astype(F32)).astype(BF16)
    return _dot(hi, bd) + _dot(lo, bd)


def _lo_mask(rows):
    return lax.broadcasted_iota(jnp.int32, (rows, LANES), 1) < HEAD_DIM


def _sigmoid(x):
    return 1.0 / (1.0 + jnp.exp(-x))


def _fold_heads(v):
    return v + pltpu.roll(v, HEAD_DIM, 1)


def _put_rows(ref, vec, accumulate=False):
    for j in range(vec.shape[1] // LANES):
        piece = vec[:, j * LANES:(j + 1) * LANES]
        ref[j:j + 1, :] = ref[j:j + 1, :] + piece if accumulate else piece


def _gather_proj(x, gain, wt_sh):
    tm = 1024
    nrow = SEQ // tm
    widths = (768, 896, 768, 896)
    nunits = len(widths)
    pair = 2 * wt_sh.shape[0]
    assert pair % LANES == 0 and sum(widths[:2]) == pair

    def body(x_ref, g_ref, wt_sh_ref, proj_hbm, wt_hbm, h_scr, land, res, send_sems, recv_sems, out_sems, copy_sem):
        u, i = pl.program_id(0), pl.program_id(1)
        cx_, cy_ = lax.axis_index("x"), lax.axis_index("y")
        (send_own, pass_on_neighbours, pass_on_diagonal, _), (y_complete, x_complete, diagonal_complete, sends_done) = (
            _gather_stages((wt_sh_ref,), (land,), send_sems, recv_sems))
        first = lambda k: (u == k) & (i == 0)
        last = (u == nunits - 1) & (i == nrow - 1)
        to_hbm = pltpu.make_async_copy(land, wt_hbm, copy_sem)

        pl.when(first(0))(send_own)

        @pl.when(u == 0)
        def _():
            xv = x_ref[...]
            ms = jnp.mean(xv * xv, axis=-1, keepdims=True)
            h_scr[pl.ds(pl.multiple_of(i * tm, tm), tm), :] = (xv * lax.rsqrt(ms + EPS) * g_ref[...]).astype(BF16)

        @pl.when(first(1))
        def _():
            pass_on_neighbours()
            y_complete()

        @pl.when(first(2))
        def _():
            x_complete()
            pass_on_diagonal()

        @pl.when(first(3))
        def _():
            diagonal_complete()
            to_hbm.start()

        mine, other = pair * cx_, pair * (1 - cx_)
        col0 = (mine + 896 * cy_, mine + 768 * (1 - cy_), other + 896 * cy_, other + 768 * (1 - cy_))
        slot = i % 2
        rows = pl.ds(pl.multiple_of(i * tm, tm), tm)

        def writeback(k, rows_):
            c0 = pl.multiple_of(col0[k], LANES)
            return pltpu.make_async_copy(res.at[slot, :, pl.ds(0, widths[k])], proj_hbm.at[rows_, pl.ds(c0, widths[k])],
                                         out_sems.at[slot])

        for k in range(nunits):
            @pl.when(u == k)
            def _(k=k):
                pl.when(i >= 2)(writeback(k, rows).wait)
                if k > 0:
                    pl.when(i < 2)(writeback(k - 1, rows).wait)
                w_rows = land[pl.ds(pl.multiple_of(col0[k], LANES), widths[k]), :]
                res[slot, :, 0:widths[k]] = _dot_nt(h_scr[rows, :], w_rows)
                writeback(k, rows).start()

        @pl.when(last)
        def _():
            sends_done()
            to_hbm.wait()
            for s in range(2):
                pltpu.make_async_copy(res.at[s, :, pl.ds(0, widths[-1])], proj_hbm.at[rows, pl.ds(0, widths[-1])], out_sems.at[s]).wait()

    full = jax.ShapeDtypeStruct((4 * wt_sh.shape[0], wt_sh.shape[1]), BF16)
    hbm = pl.BlockSpec(memory_space=pl.ANY)
    return _call(
        body, name="gather_proj", grid=(nunits, nrow),
        in_specs=[pl.BlockSpec((tm, D_MODEL), lambda u, i: (jnp.where(u == 0, i, nrow - 1), 0)),
                  pl.BlockSpec((1, D_MODEL), lambda u, i: (0, 0)), pl.BlockSpec(wt_sh.shape, lambda u, i: (0, 0))],
        out_specs=[hbm, hbm],
        out_shape=[jax.ShapeDtypeStruct((SEQ, IN_W), F32), full],
        scratch_shapes=[pltpu.VMEM((SEQ, D_MODEL), BF16), pltpu.VMEM(full.shape, BF16), pltpu.VMEM((2, tm, max(widths)), F32),
                        pltpu.SemaphoreType.DMA((AG_SEMS,)), pltpu.SemaphoreType.DMA((AG_SEMS,)),
                        pltpu.SemaphoreType.DMA((2,)), pltpu.SemaphoreType.DMA],
        compiler_params=_params(),
    )(x, gain, wt_sh)


def _gmlp_weights(w_ref):
    ti = lax.broadcasted_iota(jnp.int32, (CHUNK, CHUNK), 0)
    si = lax.broadcasted_iota(jnp.int32, (CHUNK, CHUNK), 1)
    tril = si <= ti
    return tril, [jnp.where(tril, w_ref[h], 0.0).astype(BF16) for h in range(4)]


def _gmlp_fwd(proj, vgain, w_s, bias_full):
    tm = 1024

    def body(p_ref, vg_ref, w_ref, b_ref, y_ref):
        bd = _head_blockdiag()
        lo = _lo_mask(CHUNK)
        _, wm = _gmlp_weights(w_ref)
        units = [(pl.ds(c * CHUNK, CHUNK), p) for c in range(tm // CHUNK) for p in range(2)]
        col = lambda c0, p: slice(c0 + p * LANES, c0 + (p + 1) * LANES)
        vs = [p_ref[rows, col(C_GV, p)] for rows, p in units]
        rs = [lax.rsqrt(_headsum(v * v, bd) * (1.0 / HEAD_DIM) + EPS) for v in vs]
        vns = [(v * r * vg_ref[:, col(0, p)]).astype(BF16) for v, r, (_, p) in zip(vs, rs, units)]
        sps = [jnp.where(lo, _dot(wm[2 * p], vn), _dot(wm[2 * p + 1], vn)) + b_ref[:, col(0, p)] for vn, (_, p) in zip(vns, units)]
        for sp, (rows, p) in zip(sps, units):
            gt = p_ref[rows, col(C_GG, p)]
            y_ref[rows, col(0, p)] = (p_ref[rows, col(C_GU, p)] * sp * (gt * _sigmoid(gt))).astype(BF16)

    return _call(
        body, name="gmlp_fwd", grid=(SEQ // tm,),
        in_specs=[pl.BlockSpec((tm, 3 * GMLP_W), lambda i: (i, 0)),
                  pl.BlockSpec((1, GMLP_W), lambda i: (0, 0)),
                  pl.BlockSpec((4, CHUNK, CHUNK), lambda i: (0, 0, 0)),
                  pl.BlockSpec((CHUNK, GMLP_W), lambda i: (0, 0))],
        out_specs=pl.BlockSpec((tm, GMLP_W), lambda i: (i, 0)),
        out_shape=jax.ShapeDtypeStruct((SEQ, GMLP_W), BF16),
        compiler_params=_params(),
    )(proj, vgain, w_s, bias_full)


def _gmlp_bwd(proj, dyc, vgain, w_s, bias_full):
    tm = 1024
    nsteps = SEQ // tm

    def body(p_ref, dy_ref, vg_ref, w_ref, b_ref, dg_ref, gw_ref, gb_ref, gv_ref):
        i = pl.program_id(0)
        bd = _head_blockdiag()
        lo = _lo_mask(CHUNK)
        tril, wm = _gmlp_weights(w_ref)
        ri = lax.broadcasted_iota(jnp.int32, (16, LANES), 0)
        li = lax.broadcasted_iota(jnp.int32, (16, LANES), 1)
        head_rows = [jnp.where(((ri == 2 * p) & (li < HEAD_DIM)) | ((ri == 2 * p + 1) & (li >= HEAD_DIM)), 1.0, 0.0).astype(BF16)
                     for p in range(2)]

        @pl.when(i == 0)
        def _():
            gw_ref[...] = jnp.zeros_like(gw_ref)
            gb_ref[...] = jnp.zeros_like(gb_ref)
            gv_ref[...] = jnp.zeros_like(gv_ref)

        units = [(pl.ds(c * CHUNK, CHUNK), p) for c in range(tm // CHUNK) for p in range(2)]
        col = lambda c0, p: slice(c0 + p * LANES, c0 + (p + 1) * LANES)
        vs = [p_ref[rows, col(C_GV, p)] for rows, p in units]
        rs = [lax.rsqrt(_headsum(v * v, bd) * (1.0 / HEAD_DIM) + EPS) for v in vs]
        zs = [v * r for v, r in zip(vs, rs)]
        vns = [(z * vg_ref[:, col(0, p)]).astype(BF16) for z, (_, p) in zip(zs, units)]
        sps = [jnp.where(lo, _dot(wm[2 * p], vn), _dot(wm[2 * p + 1], vn)) + b_ref[:, col(0, p)] for vn, (_, p) in zip(vns, units)]
        dsps = []
        for sp, (rows, p) in zip(sps, units):
            u = p_ref[rows, col(C_GU, p)]
            gt = p_ref[rows, col(C_GG, p)]
            dy = dy_ref[rows, col(0, p)]
            sg = _sigmoid(gt)
            sl = gt * sg
            dg_ref[rows, col(C_GU, p)] = (dy * sp * sl).astype(BF16)
            dg_ref[rows, col(C_GG, p)] = (dy * u * sp * (sg * (1.0 + gt * (1.0 - sg)))).astype(BF16)
            dsps.append(dy * u * sl)
        dspbs = [dsp.astype(BF16) for dsp in dsps]
        dvns = [jnp.where(lo, _dot_tn(wm[2 * p], dspb), _dot_tn(wm[2 * p + 1], dspb)) for dspb, (_, p) in zip(dspbs, units)]
        gws = [(_dot_nt(jnp.where(lo, dsp, 0.0).astype(BF16), vn), _dot_nt(jnp.where(lo, 0.0, dsp).astype(BF16), vn))
               for dsp, vn in zip(dsps, vns)]
        gbs = [(_dot_nt(head_rows[p], dspb) + _dot_nt(head_rows[p], (dsp - dspb.astype(F32)).astype(BF16)))[0:8]
               for dsp, dspb, (_, p) in zip(dsps, dspbs, units)]
        for p in range(2):
            mine = [n for n, (_, q) in enumerate(units) if q == p]
            gw_ref[2 * p] += sum(gws[n][0] for n in mine)
            gw_ref[2 * p + 1] += sum(gws[n][1] for n in mine)
            gvp = sum(jnp.sum(dvns[n] * zs[n], axis=0, keepdims=True) for n in mine)
            gv_ref[2 * p:2 * p + 1, :] += gvp
            gv_ref[2 * p + 1:2 * p + 2, :] += pltpu.roll(gvp, HEAD_DIM, 1)
        gb_ref[...] += sum(gbs)
        for dvn, z, r, (rows, p) in zip(dvns, zs, rs, units):
            dz = dvn * vg_ref[:, col(0, p)]
            dg_ref[rows, col(C_GV, p)] = (r * (dz - z * (_headsum(dz * z, bd) * (1.0 / HEAD_DIM)))).astype(BF16)

        @pl.when(i == nsteps - 1)
        def _():
            for h in range(4):
                gw_ref[h] = jnp.where(tril, gw_ref[h], 0.0)

    return _call(
        body, name="gmlp_bwd", grid=(nsteps,),
        in_specs=[pl.BlockSpec((tm, 3 * GMLP_W), lambda i: (i, 0)),
                  pl.BlockSpec((tm, GMLP_W), lambda i: (i, 0)),
                  pl.BlockSpec((1, GMLP_W), lambda i: (0, 0)),
                  pl.BlockSpec((4, CHUNK, CHUNK), lambda i: (0, 0, 0)),
                  pl.BlockSpec((CHUNK, GMLP_W), lambda i: (0, 0))],
        out_specs=[pl.BlockSpec((tm, 3 * GMLP_W), lambda i: (i, 0)),
                   pl.BlockSpec((4, CHUNK, CHUNK), lambda i: (0, 0, 0)),
                   pl.BlockSpec((8, LANES), lambda i: (0, 0)),
                   pl.BlockSpec((8, LANES), lambda i: (0, 0))],
        out_shape=[jax.ShapeDtypeStruct((SEQ, 3 * GMLP_W), BF16),
                   jax.ShapeDtypeStruct((4, CHUNK, CHUNK), F32),
                   jax.ShapeDtypeStruct((8, LANES), F32),
                   jax.ShapeDtypeStruct((8, LANES), F32)],
        compiler_params=_params(),
    )(proj, dyc, vgain, w_s, bias_full)


def _band_masks():
    qi = lax.broadcasted_iota(jnp.int32, (CHUNK, 2 * CHUNK), 0)
    kj = lax.broadcasted_iota(jnp.int32, (CHUNK, 2 * CHUNK), 1)
    valid2 = ((kj < CHUNK) & (kj >= qi)) | ((kj >= CHUNK) & (kj - CHUNK <= qi))
    q1 = lax.broadcasted_iota(jnp.int32, (CHUNK, CHUNK), 0)
    k1 = lax.broadcasted_iota(jnp.int32, (CHUNK, CHUNK), 1)
    return k1 <= q1, valid2


def _stack_heads(v, lo):
    return jnp.concatenate([jnp.where(lo, v, 0.0), jnp.where(lo, 0.0, v)], axis=0).astype(BF16)


def _rows_of(ref, start, d):
    if d == 1:
        return ref.at[pl.ds(start if isinstance(start, int) else pl.multiple_of(start, CHUNK), CHUNK), :]
    return ref.at[pl.ds(start, CHUNK, stride=d), :]


def _unrolled(lo, hi, unroll, run):
    groups = (hi - lo) // unroll
    if groups:
        def body(g, carry):
            run([lo + g * unroll + t for t in range(unroll)])
            return carry

        lax.fori_loop(0, groups, body, 0)
    if lo + groups * unroll < hi:
        run(range(lo + groups * unroll, hi))


def _for_blocks(d, group_fn, unroll):
    nblk = SEQ // CHUNK
    sh = d.bit_length() - 1

    def first(j):
        return (j * CHUNK if d == 1 else j, None)

    def rest(j):
        start = (j & (d - 1)) + (j >> sh) * (CHUNK * d)
        return (start, start - CHUNK * d)

    _unrolled(0, d, unroll, lambda js: group_fn(d, [first(j) for j in js]))
    _unrolled(d, nblk, unroll, lambda js: group_fn(d, [rest(j) for j in js]))


def _attn_fwd(proj, gq2, gk2, *ride_along):
    tn = 512
    npairs = ATTN_W // LANES
    nride = len(ride_along)

    def body(q_ref, k_ref, v_ref, g_ref, gq_ref, gk_ref, *rest):
        shards, rest = rest[:nride], rest[nride:]
        o_ref, l_ref, ya_ref = rest[:3]
        gathered, rest = rest[3:3 + nride], rest[3 + nride:]
        qn_ref, kn_ref = rest[:2]
        lands, (send_sems, recv_sems, copy_sems) = rest[2:2 + nride], rest[2 + nride:]
        pair = pl.program_id(0)
        ride = _gather_stages(shards, lands, send_sems, recv_sems)[0]
        for step in range(npairs):
            pl.when(pair == step)(ride[step])
        bd = _head_blockdiag()
        lo = _lo_mask(CHUNK)
        valid1, valid2 = _band_masks()

        def norm(t, carry):
            rows = pl.ds(pl.multiple_of(t * tn, tn), tn)
            q, k = q_ref[rows, :], k_ref[rows, :]
            ssq = [_headsum(a * a, bd) for a in (q, k)]
            qn_ref[rows, :] = q * lax.rsqrt(ssq[0] * (1.0 / HEAD_DIM) + EPS) * (gq_ref[...] * QK_SCALE)
            kn_ref[rows, :] = k * lax.rsqrt(ssq[1] * (1.0 / HEAD_DIM) + EPS) * gk_ref[...]
            return carry

        lax.fori_loop(0, SEQ // tn, norm, 0)

        def load_kv(ref, d, start, prev):
            own = _rows_of(ref, start, d)[...]
            if prev is None:
                return own.astype(BF16)
            return jnp.concatenate([_rows_of(ref, prev, d)[...], own], axis=0).astype(BF16)

        def group(d, blocks):
            valid = valid1 if blocks[0][1] is None else valid2
            valid = jnp.concatenate([valid, valid], axis=0)
            qs = [_rows_of(qn_ref, start, d)[...] for start, _ in blocks]
            ks = [load_kv(kn_ref, d, start, prev) for start, prev in blocks]
            vs = [load_kv(v_ref, d, start, prev) for start, prev in blocks]
            ss = [_dot_nt(_stack_heads(q, lo), k) for q, k in zip(qs, ks)]
            ms, ps, ls = [], [], []
            for s in ss:
                s = jnp.where(valid, s, -jnp.inf)
                m = jnp.max(s, axis=-1, keepdims=True)
                p = jnp.exp(s - m)
                ms.append(m)
                ls.append(jnp.sum(p, axis=-1, keepdims=True))
                ps.append(p.astype(BF16))
            os_ = [_dot(p, v) for p, v in zip(ps, vs)]
            for b, (start, _) in enumerate(blocks):
                heads = lambda v: jnp.where(lo, v[:CHUNK], v[CHUNK:])
                lsum = heads(ls[b])
                ob = heads(os_[b]) * (1.0 / lsum)
                lb = heads(ms[b]) + jnp.log(lsum)
                o_rows = _rows_of(o_ref, start, d)
                l_rows = _rows_of(l_ref, start, d)
                if d != DILATIONS[0]:
                    lold = l_rows[...]
                    mx = jnp.maximum(lold, lb)
                    ea = jnp.exp(lold - mx)
                    eb = jnp.exp(lb - mx)
                    inv = 1.0 / (ea + eb)
                    ob = o_rows[...] * (ea * inv) + ob * (eb * inv)
                    lb = mx + jnp.log(ea + eb)
                o_rows[...] = ob
                l_rows[...] = lb

        for d in DILATIONS:
            _for_blocks(d, group, ATTN_UNROLL)

        def fin(t, carry):
            rows = pl.ds(pl.multiple_of(t * tn, tn), tn)
            g = g_ref[rows, :]
            ya_ref[rows, :] = (o_ref[rows, :] * (g * _sigmoid(g))).astype(BF16)
            return carry

        lax.fori_loop(0, SEQ // tn, fin, 0)

        @pl.when(pair == npairs - 1)
        def _():
            to_hbm = [pltpu.make_async_copy(land, out, copy_sems.at[n]) for n, (land, out) in enumerate(zip(lands, gathered))]
            for cp in to_hbm:
                cp.start()
            for cp in to_hbm:
                cp.wait()

    col = lambda c0: pl.BlockSpec((SEQ, LANES), lambda p: (0, c0 // LANES + p))
    vec = pl.BlockSpec((1, LANES), lambda p: (0, 0))
    out = pl.BlockSpec((SEQ, LANES), lambda p: (0, p))
    full = [jax.ShapeDtypeStruct((4 * a.shape[0], a.shape[1]), BF16) for a in ride_along]
    return _call(
        body, name="attn_fwd", grid=(npairs,),
        in_specs=[col(C_AQ), col(C_AK), col(C_AV), col(C_AG), vec, vec]
        + [pl.BlockSpec(a.shape, lambda p: (0, 0)) for a in ride_along],
        out_specs=[out, out, out] + [pl.BlockSpec(memory_space=pl.ANY)] * nride,
        out_shape=[jax.ShapeDtypeStruct((SEQ, ATTN_W), F32), jax.ShapeDtypeStruct((SEQ, ATTN_W), F32),
                   jax.ShapeDtypeStruct((SEQ, ATTN_W), BF16)] + full,
        scratch_shapes=[pltpu.VMEM((SEQ, LANES), F32), pltpu.VMEM((SEQ, LANES), F32)]
        + [pltpu.VMEM(s.shape, BF16) for s in full]
        + [pltpu.SemaphoreType.DMA((AG_SEMS * nride,)), pltpu.SemaphoreType.DMA((AG_SEMS * nride,)),
           pltpu.SemaphoreType.DMA((nride,))],
        compiler_params=_params(),
    )(proj, proj, proj, proj, gq2, gk2, *ride_along)


def _attn_bwd(proj, o, lse, dyc, gq2, gk2, *ride_along):
    tn = 1024
    npairs = ATTN_W // LANES
    nride = len(ride_along)
    nbufs = nride * len(RS_KINDS)

    def body(proj_hbm, o_hbm, l_hbm, dyc_hbm, gq_ref, gk_ref, *rest):
        ride_in, rest = rest[:nride], rest[nride:]
        dq_ref, dk_ref, dv_ref, dgt_ref, gqg_ref, gkg_ref = rest[:6]
        ride_out, rest = rest[6:6 + nride], rest[6 + nride:]
        qb_, kb_, vb_, gb_, ob_, lb_, yb_, dkb_, dvb_, sems = rest[:10]
        rs_bufs, (send_sems, recv_sems, local_sems) = rest[10:10 + nbufs], rest[10 + nbufs:]
        rs_stage = _rs_stages(ride_in, ride_out, rs_bufs, send_sems, recv_sems, local_sems, [g.shape[1] for g in ride_along])
        pair = pl.program_id(0)
        for step in range(npairs):
            pl.when(pair == step)(rs_stage[step])
        bd = _head_blockdiag()
        lo = _lo_mask(CHUNK)
        lo2 = lax.broadcasted_iota(jnp.int32, (2 * CHUNK, LANES), 1) < HEAD_DIM
        valid1, valid2 = _band_masks()
        gqs = gq_ref[...] * QK_SCALE
        gk = gk_ref[...]

        def pcol(c0, of=None):
            return acol(proj_hbm, c0, of)

        def acol(hbm, c0=0, of=None):
            of = pair if of is None else of
            return hbm.at[:, pl.ds(pl.multiple_of(c0 + of * LANES, LANES), LANES)]

        def input_loads(of):
            return [pltpu.make_async_copy(src, dst, sems.at[n]) for n, (src, dst) in enumerate((
                (pcol(C_AQ, of), qb_), (pcol(C_AK, of), kb_), (pcol(C_AG, of), gb_), (acol(o_hbm, 0, of), ob_),
                (acol(dyc_hbm, GMLP_W, of), yb_), (pcol(C_AV, of), vb_), (acol(l_hbm, 0, of), lb_)))]

        early = (0, 1, 3, 4)
        loads = input_loads(pair)
        for n, cp in enumerate(loads):
            if n in early:
                pl.when(pair == 0)(cp.start)
            else:
                cp.start()

        @pl.when(pair == 0)
        def _():
            gqg_ref[...] = jnp.zeros_like(gqg_ref)
            gkg_ref[...] = jnp.zeros_like(gkg_ref)

        def pre_qk(t, carry):
            rows = pl.ds(pl.multiple_of(t * tn, tn), tn)
            q, k = qb_[rows, :], kb_[rows, :]
            ssq = [_headsum(a * a, bd) for a in (q, k)]
            qb_[rows, :] = q * lax.rsqrt(ssq[0] * (1.0 / HEAD_DIM) + EPS) * gqs
            kb_[rows, :] = k * lax.rsqrt(ssq[1] * (1.0 / HEAD_DIM) + EPS) * gk
            return carry

        def pre_gate(t, carry):
            rows = pl.ds(pl.multiple_of(t * tn, tn), tn)
            g = gb_[rows, :]
            ov = ob_[rows, :]
            dya = yb_[rows, :]
            sg = _sigmoid(g)
            dgt_ref[rows, :] = (dya * ov * (sg * (1.0 + g * (1.0 - sg)))).astype(BF16)
            do = dya * (g * sg)
            yb_[rows, :] = do
            ob_[rows, :] = jnp.where(first_half, lb_[rows, :], _headsum(do * ov, bd))
            return carry

        first_half = (lax.broadcasted_iota(jnp.int32, (tn, LANES), 1) & (HEAD_DIM - 1)) < HEAD_DIM // 2
        loads[0].wait()
        loads[1].wait()
        lax.fori_loop(0, SEQ // tn, pre_qk, 0)
        for cp in loads[2:5] + loads[6:7]:
            cp.wait()
        lax.fori_loop(0, SEQ // tn, pre_gate, 0)
        loads[5].wait()
        reloads = [pltpu.make_async_copy(pcol(C_AQ), lb_, sems.at[7]), pltpu.make_async_copy(pcol(C_AK), vb_, sems.at[8])]
        reloads[0].start()

        def load_kv(ref, d, start, prev):
            own = _rows_of(ref, start, d)[...]
            if prev is None:
                return own.astype(BF16)
            return jnp.concatenate([_rows_of(ref, prev, d)[...], own], axis=0).astype(BF16)

        def group(d, blocks):
            first = blocks[0][1] is None
            valid, lok = (valid1, lo) if first else (valid2, lo2)
            chains = [(b, h) for b in range(len(blocks)) for h in range(2)]
            mask = lambda h: lo if h == 0 else ~lo
            qs = [_rows_of(qb_, start, d)[...] for start, _ in blocks]
            dos = [_rows_of(yb_, start, d)[...] for start, _ in blocks]
            lds = [_rows_of(ob_, start, d)[...] for start, _ in blocks]
            ks = [load_kv(kb_, d, start, prev) for start, prev in blocks]
            vs = [load_kv(vb_, d, start, prev) for start, prev in blocks]
            qbs = [q.astype(BF16) for q in qs]
            dobs = [do.astype(BF16) for do in dos]
            ss = [_dot_nt(jnp.where(mask(h), qs[b], 0.0).astype(BF16), ks[b]) for b, h in chains]
            dps = [_dot_nt(jnp.where(mask(h), dos[b], 0.0).astype(BF16), vs[b]) for b, h in chains]
            pbs, dss = [], []
            for s, dp, (b, h) in zip(ss, dps, chains):
                hc, dc = h * HEAD_DIM, h * HEAD_DIM + HEAD_DIM // 2
                p = jnp.exp(jnp.where(valid, s, -jnp.inf) - lds[b][:, hc:hc + 1])
                pbs.append(p.astype(BF16))
                dss.append((p * (dp - lds[b][:, dc:dc + 1])).astype(BF16))
            dqs = [_dot(ds, ks[b]) for ds, (b, h) in zip(dss, chains)]
            dks = [_dot_tn(ds, qbs[b]) for ds, (b, h) in zip(dss, chains)]
            dvs = [_dot_tn(p, dobs[b]) for p, (b, h) in zip(pbs, chains)]
            assign = d == DILATIONS[0]
            for b, (start, prev) in enumerate(blocks):
                c0, c1 = 2 * b, 2 * b + 1
                dq_rows = _rows_of(gb_, start, d)
                dqb = jnp.where(lo, dqs[c0], dqs[c1])
                dq_rows[...] = dqb if assign else dq_rows[...] + dqb
                dkc = jnp.where(lok, dks[c0], dks[c1])
                dvc = jnp.where(lok, dvs[c0], dvs[c1])
                spans = ((start, slice(0, CHUNK), True),) if first else (
                    (prev, slice(0, CHUNK), False), (start, slice(CHUNK, 2 * CHUNK), True))
                for st, sl, own in spans:
                    dk_rows = _rows_of(dkb_, st, d)
                    dv_rows = _rows_of(dvb_, st, d)
                    if assign and own:
                        dk_rows[...] = dkc[sl]
                        dv_rows[...] = dvc[sl]
                    else:
                        dk_rows[...] = dk_rows[...] + dkc[sl]
                        dv_rows[...] = dv_rows[...] + dvc[sl]

        for d in DILATIONS:
            _for_blocks(d, group, ATTN_UNROLL)

        reloads[1].start()

        @pl.when(pair < npairs - 1)
        def _():
            nxt = input_loads(pair + 1)
            for n in early:
                nxt[n].start()

        for cp in reloads:
            cp.wait()

        def post(t, carry):
            gq_acc, gk_acc = carry
            rows = pl.ds(pl.multiple_of(t * tn, tn), tn)
            raws = [lb_[rows, :], vb_[rows, :]]
            dns = [gb_[rows, :], dkb_[rows, :]]
            rs = [lax.rsqrt(_headsum(a * a, bd) * (1.0 / HEAD_DIM) + EPS) for a in raws]
            zs = [a * r for a, r in zip(raws, rs)]
            dzs = [dn * gain for dn, gain in zip(dns, (gqs, gk))]
            means = [_headsum(dz * z, bd) * (1.0 / HEAD_DIM) for dz, z in zip(dzs, zs)]
            dq, dk = [r * (dz - z * mean) for r, dz, z, mean in zip(rs, dzs, zs, means)]
            gq, gkk = [jnp.sum(dn * z, axis=0, keepdims=True) for dn, z in zip(dns, zs)]
            dq_ref[rows, :] = dq.astype(BF16)
            dk_ref[rows, :] = dk.astype(BF16)
            dv_ref[rows, :] = dvb_[rows, :].astype(BF16)
            return gq_acc + gq * QK_SCALE, gk_acc + gkk

        zero = jnp.zeros((1, LANES), F32)
        gq_acc, gk_acc = lax.fori_loop(0, SEQ // tn, post, (zero, zero))
        gqg_ref[0:1, :] += gq_acc
        gkg_ref[0:1, :] += gk_acc

        @pl.when(pair == npairs - 1)
        def _():
            gqg_ref[0:1, :] = _fold_heads(gqg_ref[0:1, :])
            gkg_ref[0:1, :] = _fold_heads(gkg_ref[0:1, :])
            rs_stage[npairs]()

    hbm = pl.BlockSpec(memory_space=pl.ANY)
    vec = pl.BlockSpec((1, LANES), lambda p: (0, 0))
    blk8 = pl.BlockSpec((8, LANES), lambda p: (0, 0))
    out = pl.BlockSpec((SEQ, LANES), lambda p: (0, p))
    big = jax.ShapeDtypeStruct((SEQ, ATTN_W), BF16)
    nsem = RS_SEMS * nride
    return _call(
        body, name="attn_bwd", grid=(npairs,),
        in_specs=[hbm, hbm, hbm, hbm, vec, vec] + [hbm] * nride,
        out_specs=[out, out, out, out, blk8, blk8] + [hbm] * nride,
        out_shape=[big, big, big, big, jax.ShapeDtypeStruct((8, LANES), F32), jax.ShapeDtypeStruct((8, LANES), F32)]
        + [jax.ShapeDtypeStruct((2, g.shape[0] // 8, g.shape[1]), F32) for g in ride_along],
        scratch_shapes=[pltpu.VMEM((SEQ, LANES), F32) for _ in range(9)] + [pltpu.SemaphoreType.DMA((9,))]
        + _rs_scratch([g.shape for g in ride_along]) + [pltpu.SemaphoreType.DMA((nsem,)), pltpu.SemaphoreType.DMA((nsem,)),
                                     pltpu.SemaphoreType.DMA((nride,))],
        compiler_params=_params(),
    )(proj, o, lse, dyc, gq2, gk2, *[_rs_view(g) for g in ride_along])


def _mem_kv(mem, gain, wkv):
    def body(m_ref, g_ref, w_ref, kv_ref, hm_ref):
        mv = m_ref[...]
        ms = jnp.mean(mv * mv, axis=-1, keepdims=True)
        hm = (mv * lax.rsqrt(ms + EPS) * g_ref[...]).astype(BF16)
        hm_ref[...] = hm
        kv_ref[...] = _dot(hm, w_ref[...])

    return _call(
        body, name="mem_kv",
        out_shape=[jax.ShapeDtypeStruct((MEM_LEN, 2 * MEM_W), F32), jax.ShapeDtypeStruct((MEM_LEN, D_MODEL), BF16)],
        compiler_params=_params(),
    )(mem, gain, wkv)


def _mem_keys(kv_ref, kg_ref, bd, p):
    mk = kv_ref[:, p * LANES:(p + 1) * LANES]
    r = lax.rsqrt(_headsum(mk * mk, bd) * (1.0 / HEAD_DIM) + EPS)
    z = mk * r
    mkn = (z * kg_ref[:, p * LANES:(p + 1) * LANES]).astype(BF16)
    mvp = kv_ref[:, MEM_W + p * LANES:MEM_W + (p + 1) * LANES].astype(BF16)
    return mkn, mvp, r, z


def _mem_fwd(proj, kv, qg4, kg4):
    tm = 1024

    def body(q_ref, g_ref, kv_ref, qg_ref, kg_ref, om_ref, ym_ref):
        bd = _head_blockdiag()
        lo = _lo_mask(tm)
        keys, qns = [], []
        for p in range(2):
            cs = slice(p * LANES, (p + 1) * LANES)
            keys.append(_mem_keys(kv_ref, kg_ref, bd, p)[:2])
            q = q_ref[:, cs]
            qns.append(q * lax.rsqrt(_headsum(q * q, bd) * (1.0 / HEAD_DIM) + EPS) * (qg_ref[:, cs] * QK_SCALE))
        chains = [(p, h) for p in range(2) for h in range(2)]
        ss = [_dot_nt(jnp.where(lo if h == 0 else ~lo, qns[p], 0.0).astype(BF16), keys[p][0]) for p, h in chains]
        es = [jnp.exp(s - jnp.max(s, axis=-1, keepdims=True)) for s in ss]
        os_ = [_dot(e.astype(BF16), keys[p][1]) for e, (p, h) in zip(es, chains)]
        res = [o * (1.0 / jnp.sum(e, axis=-1, keepdims=True)) for o, e in zip(os_, es)]
        for p in range(2):
            cs = slice(p * LANES, (p + 1) * LANES)
            ov = jnp.where(lo, res[2 * p], res[2 * p + 1])
            g = g_ref[:, cs]
            om_ref[:, cs] = ov
            ym_ref[:, cs] = (ov * (g * _sigmoid(g))).astype(BF16)

    vec = pl.BlockSpec((1, MEM_W), lambda i: (0, 0))
    return _call(
        body, name="mem_fwd", grid=(SEQ // tm,),
        in_specs=[pl.BlockSpec((tm, MEM_W), lambda i: (i, C_MQ // MEM_W)),
                  pl.BlockSpec((tm, MEM_W), lambda i: (i, C_MG // MEM_W)),
                  pl.BlockSpec((MEM_LEN, 2 * MEM_W), lambda i: (0, 0)), vec, vec],
        out_specs=[pl.BlockSpec((tm, MEM_W), lambda i: (i, 0)), pl.BlockSpec((tm, MEM_W), lambda i: (i, 0))],
        out_shape=[jax.ShapeDtypeStruct((SEQ, MEM_W), F32), jax.ShapeDtypeStruct((SEQ, MEM_W), BF16)],
        compiler_params=_params(),
    )(proj, proj, kv, qg4, kg4)


def _mem_bwd(proj, om, dyc, kv, hm, mem, mgain, wkv, qg4, kg4):
    tm = 1024
    nsteps = SEQ // tm

    def body(q_ref, g_ref, om_ref, dy_ref, kv_ref, hm_ref, mem_ref, mg_ref, w_ref, qg_ref, kg_ref,
             dq_ref, dgt_ref, gqg_ref, gkg_ref, gw_ref, gmg_ref, dmk_ref, dmv_ref, gq_acc):
        i = pl.program_id(0)
        bd = _head_blockdiag()
        lo = _lo_mask(tm)
        lom = _lo_mask(MEM_LEN)

        @pl.when(i == 0)
        def _():
            dmk_ref[...] = jnp.zeros_like(dmk_ref)
            dmv_ref[...] = jnp.zeros_like(dmv_ref)
            gq_acc[...] = jnp.zeros_like(gq_acc)

        pairs = []
        for p in range(2):
            cs = slice(p * LANES, (p + 1) * LANES)
            mkn, mvp, _, _ = _mem_keys(kv_ref, kg_ref, bd, p)
            gqs = qg_ref[:, cs] * QK_SCALE
            q = q_ref[:, cs]
            r = lax.rsqrt(_headsum(q * q, bd) * (1.0 / HEAD_DIM) + EPS)
            z = q * r
            qn = z * gqs
            g = g_ref[:, cs]
            ov = om_ref[:, cs]
            dym = dy_ref[:, cs]
            sg = _sigmoid(g)
            dgt_ref[:, cs] = (dym * ov * (sg * (1.0 + g * (1.0 - sg)))).astype(BF16)
            do = dym * (g * sg)
            pairs.append(dict(cs=cs, mkn=mkn, mvp=mvp, gqs=gqs, r=r, z=z, qn=qn, qnb=qn.astype(BF16), do=do,
                              dob=do.astype(BF16), delta=_headsum(do * ov, bd)))
        chains = [(pr_, h) for pr_ in pairs for h in range(2)]
        mask = lambda h: lo if h == 0 else ~lo
        ss = [_dot_nt(jnp.where(mask(h), c["qn"], 0.0).astype(BF16), c["mkn"]) for c, h in chains]
        dps = [_dot_nt(jnp.where(mask(h), c["do"], 0.0).astype(BF16), c["mvp"]) for c, h in chains]
        prs, dss = [], []
        for s, dp, (c, h) in zip(ss, dps, chains):
            e = jnp.exp(s - jnp.max(s, axis=-1, keepdims=True))
            pr = e * (1.0 / jnp.sum(e, axis=-1, keepdims=True))
            prs.append(pr.astype(BF16))
            dss.append((pr * (dp - c["delta"][:, h * HEAD_DIM:h * HEAD_DIM + 1])).astype(BF16))
        dqs = [_dot(ds, c["mkn"]) for ds, (c, h) in zip(dss, chains)]
        dks = [_dot_tn(ds, c["qnb"]) for ds, (c, h) in zip(dss, chains)]
        dvs = [_dot_tn(pr, c["dob"]) for pr, (c, h) in zip(prs, chains)]
        for p, c in enumerate(pairs):
            cs, z, r = c["cs"], c["z"], c["r"]
            dqn = jnp.where(lo, dqs[2 * p], dqs[2 * p + 1])
            dmk_ref[:, cs] += jnp.where(lom, dks[2 * p], dks[2 * p + 1])
            dmv_ref[:, cs] += jnp.where(lom, dvs[2 * p], dvs[2 * p + 1])
            dz = dqn * c["gqs"]
            dq_ref[:, cs] = (r * (dz - z * (_headsum(dz * z, bd) * (1.0 / HEAD_DIM)))).astype(BF16)
            gq_acc[:, cs] += jnp.sum(dqn * z, axis=0, keepdims=True) * QK_SCALE

        @pl.when(i == nsteps - 1)
        def _():
            gqg_ref[...] = jnp.zeros_like(gqg_ref)
            gkg_ref[...] = jnp.zeros_like(gkg_ref)
            gqg_ref[0:1, :] = _fold_heads(gq_acc[:, 0:LANES] + gq_acc[:, LANES:2 * LANES])
            dkv = []
            gk = jnp.zeros((1, LANES), F32)
            for p in range(2):
                cs = slice(p * LANES, (p + 1) * LANES)
                _, _, r, z = _mem_keys(kv_ref, kg_ref, bd, p)
                dn = dmk_ref[:, cs]
                dz = dn * kg_ref[:, cs]
                gk = gk + jnp.sum(dn * z, axis=0, keepdims=True)
                dkv.append(r * (dz - z * (_headsum(dz * z, bd) * (1.0 / HEAD_DIM))))
            gkg_ref[0:1, :] = _fold_heads(gk)
            dkvb = jnp.concatenate(dkv + [dmv_ref[...]], axis=1).astype(BF16)
            gw_ref[...] = _dot_tn(hm_ref[...], dkvb)
            dhm = _dot_nt(dkvb, w_ref[...])
            mv = mem_ref[...]
            zm = mv * lax.rsqrt(jnp.mean(mv * mv, axis=-1, keepdims=True) + EPS)
            _put_rows(gmg_ref, jnp.sum(dhm * zm, axis=0, keepdims=True))

    const = lambda shape: pl.BlockSpec(shape, lambda i: (0,) * len(shape))
    row = lambda j: pl.BlockSpec((tm, MEM_W), lambda i: (i, j))
    blk8 = jax.ShapeDtypeStruct((8, LANES), F32)
    return _call(
        body, name="mem_bwd", grid=(nsteps,),
        in_specs=[row(C_MQ // MEM_W), row(C_MG // MEM_W), row(0), row((GMLP_W + ATTN_W) // MEM_W),
                  const((MEM_LEN, 2 * MEM_W)), const((MEM_LEN, D_MODEL)), const((MEM_LEN, D_MODEL)),
                  const((1, D_MODEL)), const((D_MODEL, 2 * MEM_W)), const((1, MEM_W)), const((1, MEM_W))],
        out_specs=[row(0), row(0), const((8, LANES)), const((8, LANES)),
                   const((D_MODEL, 2 * MEM_W)), const((8, LANES))],
        out_shape=[jax.ShapeDtypeStruct((SEQ, MEM_W), BF16), jax.ShapeDtypeStruct((SEQ, MEM_W), BF16),
                   blk8, blk8, jax.ShapeDtypeStruct((D_MODEL, 2 * MEM_W), F32), blk8],
        scratch_shapes=[pltpu.VMEM((MEM_LEN, MEM_W), F32), pltpu.VMEM((MEM_LEN, MEM_W), F32),
                        pltpu.VMEM((1, MEM_W), F32)],
        compiler_params=_params(),
    )(proj, proj, om, dyc, kv, hm, mem, mgain, wkv, qg4, kg4)


def _out_loss(yg, ya, ym, x, tgt, wo):
    tm = 512
    nsteps = SEQ // tm
    parts = ((0, GMLP_W), (GMLP_W, ATTN_W), (GMLP_W + ATTN_W, MEM_W))

    def body(yg_ref, ya_ref, ym_ref, x_ref, t_ref, w_ref, dy_ref, dyc_ref, gw_ref, ls_ref):
        i = pl.program_id(0)

        @pl.when(i == 0)
        def _():
            gw_ref[...] = jnp.zeros_like(gw_ref)
            ls_ref[...] = jnp.zeros_like(ls_ref)

        ys = (yg_ref[...], ya_ref[...], ym_ref[...])
        y = sum(_dot(yv, w_ref[r0:r0 + n, :]) for yv, (r0, n) in zip(ys, parts))
        err = x_ref[...] + y - t_ref[...]
        _put_rows(ls_ref, jnp.sum(err * err, axis=0, keepdims=True), accumulate=True)
        dy = err * (1.0 / D_MODEL)
        dy_ref[...] = dy
        dyb = dy.astype(BF16)
        dyc_ref[...] = _dot_nt(dyb, w_ref[...])
        for yv, (r0, n) in zip(ys, parts):
            gw_ref[r0:r0 + n, :] += _dot_tn(yv, dyb)

    row = lambda w: pl.BlockSpec((tm, w), lambda i: (i, 0))
    const = lambda shape: pl.BlockSpec(shape, lambda i: (0, 0))
    return _call(
        body, name="out_loss", grid=(nsteps,),
        in_specs=[row(GMLP_W), row(ATTN_W), row(MEM_W), row(D_MODEL), row(D_MODEL), const((D_MODEL, D_MODEL))],
        out_specs=[row(D_MODEL), row(D_MODEL), const((D_MODEL, D_MODEL)), const((8, LANES))],
        out_shape=[jax.ShapeDtypeStruct((SEQ, D_MODEL), F32), jax.ShapeDtypeStruct((SEQ, D_MODEL), F32),
                   jax.ShapeDtypeStruct((D_MODEL, D_MODEL), F32), jax.ShapeDtypeStruct((8, LANES), F32)],
        compiler_params=_params(),
    )(yg, ya, ym, x, tgt, wo)


def _proj_bwd(x, dy, gain, wt, dg, daq, dak, dav, dag, dmq, dmg):
    tm = 512
    nsteps = SEQ // tm
    pieces = ((C_GU, 3 * GMLP_W), (C_AQ, ATTN_W), (C_AK, ATTN_W), (C_AV, ATTN_W), (C_AG, ATTN_W),
              (C_MQ, MEM_W), (C_MG, MEM_W))

    def body(x_ref, dy_ref, g_ref, wt_hbm, p0, p1, p2, p3, p4, p5, p6, gx_ref, gwt_hbm, gg_ref, wt_v, acc, wt_sem, out_sems):
        i = pl.program_id(0)
        wt_load = pltpu.make_async_copy(wt_hbm, wt_v, wt_sem)

        @pl.when(i == 0)
        def _():
            wt_load.start()
            acc[...] = jnp.zeros_like(acc)
            gg_ref[...] = jnp.zeros_like(gg_ref)

        xv = x_ref[...]
        r = lax.rsqrt(jnp.mean(xv * xv, axis=-1, keepdims=True) + EPS)
        z = xv * r
        g = g_ref[...]
        h = (z * g).astype(BF16)
        pl.when(i == 0)(wt_load.wait)
        flush = [pltpu.make_async_copy(acc.at[c0:c0 + w, :], gwt_hbm.at[c0:c0 + w, :], out_sems.at[n])
                 for n, (c0, w) in enumerate(pieces)]
        dh = jnp.zeros((tm, D_MODEL), F32)
        for n, (pref, (c0, w)) in enumerate(zip((p0, p1, p2, p3, p4, p5, p6), pieces)):
            dp = pref[...]
            dh = dh + _dot(dp, wt_v[c0:c0 + w, :])
            acc[c0:c0 + w, :] += _dot_tn(dp, h)
            pl.when(i == nsteps - 1)(flush[n].start)
        _put_rows(gg_ref, jnp.sum(dh * z, axis=0, keepdims=True), accumulate=True)
        dz = dh * g
        gx_ref[...] = dy_ref[...] + r * (dz - z * jnp.mean(dz * z, axis=-1, keepdims=True))

        @pl.when(i == nsteps - 1)
        def _():
            for cp in flush:
                cp.wait()

    row = lambda w: pl.BlockSpec((tm, w), lambda i: (i, 0))
    hbm = pl.BlockSpec(memory_space=pl.ANY)
    vec = pl.BlockSpec((1, D_MODEL), lambda i: (0, 0))
    return _call(
        body, name="proj_bwd", grid=(nsteps,),
        in_specs=[row(D_MODEL), row(D_MODEL), vec, hbm] + [row(w) for _, w in pieces],
        out_specs=[row(D_MODEL), hbm, pl.BlockSpec((8, LANES), lambda i: (0, 0))],
        out_shape=[jax.ShapeDtypeStruct((SEQ, D_MODEL), F32), jax.ShapeDtypeStruct((IN_W, D_MODEL), F32),
                   jax.ShapeDtypeStruct((8, LANES), F32)],
        scratch_shapes=[pltpu.VMEM((IN_W, D_MODEL), BF16), pltpu.VMEM((IN_W, D_MODEL), F32), pltpu.SemaphoreType.DMA,
                        pltpu.SemaphoreType.DMA((len(pieces),))],
        compiler_params=_params(),
    )(x, dy, gain, wt, dg, daq, dak, dav, dag, dmq, dmg)


AG_SEMS = 8


def _gather_stages(ins, lands, send_sems, recv_sems):
    n = len(ins)
    nrows = [a.shape[0] for a in ins]
    x, y, c = lax.axis_index("x"), lax.axis_index("y"), lax.axis_index("c")
    sib, xn, yn = (x, y, 1 - c), (1 - x, y, c), (x, 1 - y, c)
    me, cx, cy, cd = 2 * x + y, 2 * (1 - x) + y, 2 * x + (1 - y), 2 * (1 - x) + (1 - y)

    def part(a, chip, hf, quarter=None):
        rows = nrows[a] // 2
        base = chip * nrows[a] + hf * rows
        if quarter is not None:
            rows = rows // 2
            base = base + quarter * rows
        return lands[a].at[pl.ds(pl.multiple_of(base, 16), rows), :]

    def copy(a, j, ref, to):
        k = AG_SEMS * a + j
        return pltpu.make_async_remote_copy(src_ref=ref, dst_ref=ref, send_sem=send_sems.at[k],
                                            recv_sem=recv_sems.at[k], device_id=to, device_id_type=MESH)

    def own(a):
        return [copy(a, 0, part(a, me, c), xn), copy(a, 1, part(a, me, c), yn)]

    def neighbours(a):
        return [copy(a, 4, part(a, cx, c, 1), yn), copy(a, 2, part(a, cx, c), sib),
                copy(a, 5, part(a, cy, c, 0), xn), copy(a, 3, part(a, cy, c), sib)]

    def diagonal(a):
        return [copy(a, 7, part(a, cd, c, 1), sib), copy(a, 6, part(a, cd, c, 0), sib)]

    def send_own():
        for a in range(n):
            lands[a][pl.ds(pl.multiple_of(me * nrows[a], 16), nrows[a]), :] = ins[a][...].astype(BF16)
            for cp in own(a):
                cp.start()

    def pass_on_neighbours():
        for a in range(n):
            copy(a, 0, part(a, cx, c), xn).wait_recv()
            copy(a, 1, part(a, cy, c), yn).wait_recv()
            for cp in neighbours(a):
                cp.start()

    def pass_on_diagonal():
        for a in range(n):
            copy(a, 4, part(a, cd, c, 1), yn).wait_recv()
            copy(a, 5, part(a, cd, c, 0), xn).wait_recv()
            for cp in diagonal(a):
                cp.start()

    def y_complete():
        for a in range(n):
            copy(a, 3, part(a, cy, 1 - c), sib).wait_recv()

    def x_complete():
        for a in range(n):
            copy(a, 2, part(a, cx, 1 - c), sib).wait_recv()

    def diagonal_complete():
        for a in range(n):
            copy(a, 6, part(a, cd, 1 - c, 0), sib).wait_recv()
            copy(a, 7, part(a, cd, 1 - c, 1), sib).wait_recv()

    def sends_done():
        for a in range(n):
            for cp in own(a) + neighbours(a) + diagonal(a):
                cp.wait_send()

    def finish():
        y_complete()
        x_complete()
        diagonal_complete()
        sends_done()

    return (send_own, pass_on_neighbours, pass_on_diagonal, finish), (y_complete, x_complete, diagonal_complete, sends_done)


RS_SEMS = 6
RS_KINDS = (((2, 2), 1, F32), ((2, 2), 1, F32), ((2, 2), 2, BF16), ((2, 2), 2, BF16), ((2, 2), 2, F32),
            ((2,), 2, BF16), ((2,), 2, BF16), ((2,), 1, F32))


def _rs_view(g):
    return g.reshape(2, 2, 2, g.shape[0] // 8, g.shape[1])


def _rs_scratch(shapes):
    return [pltpu.VMEM(lead + (r // 8, w // split), dt) for lead, split, dt in RS_KINDS for r, w in shapes]


def _rs_stages(gs, outs, bufs, send_sems, recv_sems, local_sems, widths):
    n = len(gs)
    loc, ra, s_b, r_b, acc1, s_c, r_c, fin = (bufs[n * i:n * i + n] for i in range(len(RS_KINDS)))
    half_w = [w // 2 for w in widths]
    x, y, c = lax.axis_index("x"), lax.axis_index("y"), lax.axis_index("c")
    sib, xn, yn = (x, y, 1 - c), (1 - x, y, c), (x, 1 - y, c)

    def copy(a, j, src, dst, to):
        k = RS_SEMS * a + j
        return pltpu.make_async_remote_copy(src_ref=src, dst_ref=dst, send_sem=send_sems.at[k],
                                            recv_sem=recv_sems.at[k], device_id=to, device_id_type=MESH)

    def step_a(a):
        return [copy(a, 0, gs[a].at[:, :, 1 - c], ra[a], sib),
                pltpu.make_async_copy(gs[a].at[:, :, c], loc[a], local_sems.at[a])]

    def step_b(a):
        return copy(a, 1, s_b[a].at[0], r_b[a].at[0], xn), copy(a, 2, s_b[a].at[1], r_b[a].at[1], yn)

    def step_c(a):
        return copy(a, 3, s_c[a].at[0], r_c[a].at[0], yn), copy(a, 4, s_c[a].at[1], r_c[a].at[1], xn)

    def step_d(a, half):
        rows = fin[a].at[half]
        return copy(a, 5, rows, rows, sib)

    def start():
        for a in range(n):
            for cp in step_a(a):
                cp.start()

    def a_to_b():
        for a in range(n):
            for cp in step_a(a):
                cp.wait()
            ra[a][...] = loc[a][...] + ra[a][...]
            s_b[a][0] = ra[a][1 - x, :, :, :half_w[a]].astype(BF16)
            s_b[a][1] = ra[a][:, 1 - y, :, half_w[a]:].astype(BF16)
            for cp in step_b(a):
                cp.start()

    def b_to_c():
        for a in range(n):
            for cp in step_b(a):
                cp.wait()
            acc1[a][0] = ra[a][x, :, :, :half_w[a]] + r_b[a][0].astype(F32)
            acc1[a][1] = ra[a][:, y, :, half_w[a]:] + r_b[a][1].astype(F32)
            s_c[a][0] = acc1[a][0, 1 - y].astype(BF16)
            s_c[a][1] = acc1[a][1, 1 - x].astype(BF16)
            for cp in step_c(a):
                cp.start()

    def c_to_d():
        for a in range(n):
            for cp in step_c(a):
                cp.wait()
            fin[a][c, :, :half_w[a]] = acc1[a][0, y] + r_c[a][0].astype(F32)
            fin[a][c, :, half_w[a]:] = acc1[a][1, x] + r_c[a][1].astype(F32)
            step_d(a, c).start()

    def finish():
        to_hbm = [pltpu.make_async_copy(fin[a], outs[a], local_sems.at[a]) for a in range(n)]
        for a in range(n):
            step_d(a, 1 - c).wait_recv()
            step_d(a, c).wait_send()
            to_hbm[a].start()
        for cp in to_hbm:
            cp.wait()

    return start, a_to_b, b_to_c, c_to_d, finish


def _reduce_grads(gwt, g_ws, tiny):
    cw = gwt.shape[1] // RS_CHUNKS
    chunk_shape = (gwt.shape[0], cw)

    def body(g0, ws_in, tiny_in, *rest):
        outs, o_ws, o_tiny = rest[:RS_CHUNKS], rest[RS_CHUNKS], rest[RS_CHUNKS + 1]
        rest = rest[RS_CHUNKS + 2:]
        nb = len(RS_KINDS) * RS_CHUNKS
        sm, sa, sb, sc, acc_s, send_sems, recv_sems, local_sems = rest[nb:]
        blocks = [g0.at[:, :, :, :, pl.ds(j * cw, cw)] for j in range(RS_CHUNKS)]
        start, a_to_b, b_to_c, c_to_d, finish = _rs_stages(blocks, outs, rest[:nb], send_sems, recv_sems, local_sems,
                                                           [cw] * RS_CHUNKS)
        n_ws = ws_in.shape[0]
        sm[0:n_ws, :] = ws_in[...]
        sm[n_ws:, :] = tiny_in[...]
        x, y, c = lax.axis_index("x"), lax.axis_index("y"), lax.axis_index("c")

        def small(j, src, dst, to):
            k = RS_SEMS * RS_CHUNKS + j
            return pltpu.make_async_remote_copy(src_ref=src, dst_ref=dst, send_sem=send_sems.at[k],
                                                recv_sem=recv_sems.at[k], device_id=to, device_id_type=MESH)

        along_c, along_x, along_y = (small(0, sm, sa, (x, y, 1 - c)), small(1, acc_s, sb, (1 - x, y, c)),
                                     small(2, sb, sc, (x, 1 - y, c)))
        start()
        along_c.start()
        a_to_b()
        along_c.wait()
        acc_s[...] = sm[...] + sa[...]
        along_x.start()
        b_to_c()
        along_x.wait()
        sb[...] = acc_s[...] + sb[...]
        along_y.start()
        c_to_d()
        along_y.wait()
        o_ws[...] = sb[0:n_ws, :] + sc[0:n_ws, :]
        o_tiny[...] = sb[n_ws:, :] + sc[n_ws:, :]
        finish()

    vm = pl.BlockSpec(memory_space=pltpu.VMEM)
    hbm = pl.BlockSpec(memory_space=pl.ANY)
    small_shape = (g_ws.shape[0] + tiny.shape[0], LANES)
    scratch = _rs_scratch([chunk_shape] * RS_CHUNKS) + [pltpu.VMEM(small_shape, F32) for _ in range(5)]
    nsem = RS_SEMS * RS_CHUNKS + 3
    scratch += [pltpu.SemaphoreType.DMA((nsem,)), pltpu.SemaphoreType.DMA((nsem,)), pltpu.SemaphoreType.DMA((RS_CHUNKS,))]
    return _call(
        body, name="reduce_grads",
        out_shape=[jax.ShapeDtypeStruct((2, gwt.shape[0] // 8, cw), F32)] * RS_CHUNKS
        + [jax.ShapeDtypeStruct(g_ws.shape, F32), jax.ShapeDtypeStruct(tiny.shape, F32)],
        in_specs=[hbm, vm, vm],
        out_specs=[hbm] * RS_CHUNKS + [vm, vm],
        scratch_shapes=scratch,
        compiler_params=_params(),
    )(_rs_view(gwt), g_ws, tiny)


def _adam_update(w, g, m, v):
    nm = ADAM_B1 * m + (1.0 - ADAM_B1) * g
    nv = ADAM_B2 * v + (1.0 - ADAM_B2) * (g * g)
    m_hat = nm / (1.0 - ADAM_B1 ** ADAM_STEP)
    v_hat = nv / (1.0 - ADAM_B2 ** ADAM_STEP)
    return -ADAM_LR * (m_hat / (jnp.sqrt(v_hat) + ADAM_EPS) + ADAM_WD * w), nm, nv


def _adamw(w, g, m, v):
    rows, cols = w.shape
    tm = max(t for t in range(8, 257, 8) if rows % t == 0)
    parts = tuple(g) if isinstance(g, (tuple, list)) else (g,)
    n = len(parts)

    def body(w_ref, m_ref, v_ref, *refs):
        gv = jnp.concatenate([r[...] for r in refs[:n]], axis=1)
        d_ref, nm_ref, nv_ref = refs[n:n + 3]
        d_ref[...], nm_ref[...], nv_ref[...] = _adam_update(w_ref[...], gv, m_ref[...], v_ref[...])
        if n > 1:
            refs[n + 3][...] = gv

    blk = pl.BlockSpec((tm, cols), lambda i: (i, 0))
    nout = 3 if n == 1 else 4
    res = _call(
        body, name="adamw", grid=(rows // tm,),
        in_specs=[blk] * 3 + [pl.BlockSpec((tm, p.shape[1]), lambda i: (i, 0)) for p in parts], out_specs=[blk] * nout,
        out_shape=[jax.ShapeDtypeStruct((rows, cols), F32)] * nout,
        compiler_params=_params(),
    )(w, m, v, *parts)
    return (parts[0] if n == 1 else res[3], *res[:3])


def _adamw_tiny(tiny, weights, ms, vs):
    shapes = [w.shape for w in weights]
    n = len(weights)

    def grad_of(t_ref, k, shape):
        base = 8 * k
        if shape[1] > LANES:
            return [t_ref[base + j:base + j + 1, :] for j in range(shape[1] // LANES)]
        return [t_ref[base:base + shape[0], 0:shape[1]]]

    def body(t_ref, *refs):
        w_refs, m_refs, v_refs = refs[:n], refs[n:2 * n], refs[2 * n:3 * n]
        loss_ref, outs = refs[3 * n], refs[3 * n + 1:]
        loss_ref[...] = (0.5 / D_MODEL) * jnp.sum(t_ref[8 * n:8 * n + 8, :], keepdims=True)
        for k, shape in enumerate(shapes):
            g_ref, d_ref, nm_ref, nv_ref = outs[4 * k:4 * k + 4]
            for j, g in enumerate(grad_of(t_ref, k, shape)):
                cols = slice(j * LANES, (j + 1) * LANES) if shape[1] > LANES else slice(None)
                g_ref[:, cols] = g
                d_ref[:, cols], nm_ref[:, cols], nv_ref[:, cols] = _adam_update(
                    w_refs[k][:, cols], g, m_refs[k][:, cols], v_refs[k][:, cols])

    out_shape = [jax.ShapeDtypeStruct((1, 1), F32)]
    for shape in shapes:
        out_shape += [jax.ShapeDtypeStruct(shape, F32)] * 4
    return _call(body, name="adamw_tiny", out_shape=out_shape, compiler_params=_params())(tiny, *weights, *ms, *vs)


def _local_grads(x, mem, tgt, norm_gain, wt_sh, gmlp_v_gain, gmlp_w_s, gmlp_b, attn_q_gain, attn_k_gain,
                 mem_norm_gain, wkv_sh, mem_q_gain, mem_k_gain, wo_sh):
    vg = gmlp_v_gain.reshape(1, GMLP_W)
    bias_full = jnp.repeat(gmlp_b.T, HEAD_DIM, axis=1)
    gq2, gk2 = jnp.tile(attn_q_gain, (1, 2)), jnp.tile(attn_k_gain, (1, 2))
    qg4, kg4 = jnp.tile(mem_q_gain, (1, 4)), jnp.tile(mem_k_gain, (1, 4))

    proj, wt = _gather_proj(x, norm_gain, wt_sh)
    yg = _gmlp_fwd(proj, vg, gmlp_w_s, bias_full)
    o, lse, ya, wkv, wo = _attn_fwd(proj, gq2, gk2, wkv_sh, wo_sh)
    kv, hm = _mem_kv(mem, mem_norm_gain, wkv)
    om, ym = _mem_fwd(proj, kv, qg4, kg4)
    dy, dyc, g_wo, err2 = _out_loss(yg, ya, ym, x, tgt, wo)
    dmq, dmg, g_mq, g_mk, g_wkv, g_mng = _mem_bwd(proj, om, dyc, kv, hm, mem, mem_norm_gain, wkv, qg4, kg4)
    daq, dak, dav, dag, g_aq, g_ak, g_wkv_sh, g_wo_sh = _attn_bwd(proj, o, lse, dyc, gq2, gk2, g_wkv, g_wo)
    dg, g_ws, g_b, g_vg = _gmlp_bwd(proj, dyc, vg, gmlp_w_s, bias_full)
    gx, g_wt, g_ng = _proj_bwd(x, dy, norm_gain, wt, dg, daq, dak, dav, dag, dmq, dmg)

    tiny = jnp.concatenate([g_ng, g_vg, g_b, g_aq, g_ak, g_mng, g_mq, g_mk, err2], axis=0)
    return gx, g_wt, g_wkv_sh, g_wo_sh, g_ws.reshape(4 * CHUNK, CHUNK), tiny


def kernel(x, mem, norm_gain, w_in, gmlp_v_gain, gmlp_w_s, gmlp_b, attn_q_gain, attn_k_gain, mem_norm_gain, w_mem_kv, mem_q_gain, mem_k_gain, w_out, loss_target, m_norm_gain, m_w_in, m_gmlp_v_gain, m_gmlp_w_s, m_gmlp_b, m_attn_q_gain, m_attn_k_gain, m_mem_norm_gain, m_w_mem_kv, m_mem_q_gain, m_mem_k_gain, m_w_out, v_norm_gain, v_w_in, v_gmlp_v_gain, v_gmlp_w_s, v_gmlp_b, v_attn_q_gain, v_attn_k_gain, v_mem_norm_gain, v_w_mem_kv, v_mem_q_gain, v_mem_k_gain, v_w_out):
    gx, g_wt, g_wkv_sh, g_wo_sh, g_ws, tiny = _local_grads(
        x[0], mem[0], loss_target[0], norm_gain, w_in[0].T, gmlp_v_gain[0], gmlp_w_s[0], gmlp_b[0],
        attn_q_gain, attn_k_gain, mem_norm_gain, w_mem_kv[0], mem_q_gain, mem_k_gain, w_out[0])
    *g_wt_sh, g_ws, tiny = _reduce_grads(g_wt, g_ws, tiny)
    chip_block = lambda g: g.reshape(2 * g.shape[1], g.shape[2])
    g_wt_sh = tuple(chip_block(g) for g in g_wt_sh)
    g_wkv_sh, g_wo_sh = chip_block(g_wkv_sh), chip_block(g_wo_sh)

    ws = (norm_gain, w_in, gmlp_v_gain, gmlp_w_s, gmlp_b, attn_q_gain, attn_k_gain, mem_norm_gain, w_mem_kv,
          mem_q_gain, mem_k_gain, w_out)
    ms = (m_norm_gain, m_w_in, m_gmlp_v_gain, m_gmlp_w_s, m_gmlp_b, m_attn_q_gain, m_attn_k_gain, m_mem_norm_gain,
          m_w_mem_kv, m_mem_q_gain, m_mem_k_gain, m_w_out)
    vs = (v_norm_gain, v_w_in, v_gmlp_v_gain, v_gmlp_w_s, v_gmlp_b, v_attn_q_gain, v_attn_k_gain, v_mem_norm_gain,
          v_w_mem_kv, v_mem_q_gain, v_mem_k_gain, v_w_out)
    form = {1: lambda a: a[0].T, 3: lambda a: a.reshape(4 * CHUNK, CHUNK), 2: lambda a: a[0], 4: lambda a: a[0],
            8: lambda a: a[0], 11: lambda a: a[0]}
    back = {1: lambda a: a.T[None], 3: lambda a: a.reshape(1, 4, CHUNK, CHUNK), 2: lambda a: a[None],
            4: lambda a: a[None], 8: lambda a: a[None], 11: lambda a: a[None]}
    fwd = lambda t, i: form.get(i, lambda a: a)(t[i])
    out = {}
    for i, g in ((1, g_wt_sh), (3, g_ws), (8, g_wkv_sh), (11, g_wo_sh)):
        out[i] = _adamw(fwd(ws, i), g, fwd(ms, i), fwd(vs, i))
    res = _adamw_tiny(tiny, [fwd(ws, i) for i in TINY_ORDER], [fwd(ms, i) for i in TINY_ORDER],
                      [fwd(vs, i) for i in TINY_ORDER])
    for k, i in enumerate(TINY_ORDER):
        out[i] = res[1 + 4 * k:5 + 4 * k]
    leaves = [[back.get(i, lambda a: a)(out[i][j]) for i in range(12)] for j in range(4)]
    return (res[0].reshape(()), gx[None], *leaves[0], *leaves[1], *leaves[2], *leaves[3])
```

```python
import math

import jax
import jax.numpy as jnp
from jax import lax
from jax.experimental import pallas as pl
from jax.experimental.pallas import tpu as pltpu

F32 = jnp.float32
BF16 = jnp.bfloat16

SEQ = 4096
D_MODEL = 1024
HEAD_DIM = 64
LANES = 128
CHUNK = 128
GMLP_W, ATTN_W, MEM_W = 256, 512, 256
IN_W = 3 * GMLP_W + 4 * ATTN_W + 2 * MEM_W
MEM_LEN = 256
DILATIONS = (16, 4, 1)
EPS = 1e-6
QK_SCALE = 1.0 / math.sqrt(HEAD_DIM)
C_GU, C_GV, C_GG, C_AQ, C_AK, C_AV, C_AG, C_MQ, C_MG = 0, 256, 512, 768, 1280, 1792, 2304, 2816, 3072

ADAM_LR, ADAM_B1, ADAM_B2, ADAM_EPS, ADAM_WD, ADAM_STEP = 0.001, 0.9, 0.999, 1e-08, 0.01, 10

VMEM_LIMIT = 48 * 1024 * 1024
RS_CHUNKS = 4
ATTN_UNROLL = 4
MESH = pl.DeviceIdType.MESH

TINY_ORDER = (0, 2, 4, 5, 6, 7, 9, 10)


def _call(body, **kw):
    return pl.pallas_call(body, **kw)


def _params(**kw):
    return pltpu.CompilerParams(vmem_limit_bytes=VMEM_LIMIT, **kw)


def _dot(a, b):
    return jnp.dot(a, b, preferred_element_type=F32)


def _dot_nt(a, b):
    return lax.dot_general(a, b, (((1,), (1,)), ((), ())), preferred_element_type=F32)


def _dot_tn(a, b):
    return lax.dot_general(a, b, (((0,), (0,)), ((), ())), preferred_element_type=F32)


def _head_blockdiag():
    r = lax.shift_right_logical(lax.broadcasted_iota(jnp.int32, (LANES, LANES), 0), 6)
    c = lax.shift_right_logical(lax.broadcasted_iota(jnp.int32, (LANES, LANES), 1), 6)
    return jnp.where(r == c, 1.0, 0.0).astype(BF16)


def _headsum(v, bd):
    hi = v.astype(BF16)
    lo = (v - hi.astype(F32)).astype(BF16)
    return _dot(hi, bd) + _dot(lo, bd)


def _lo_mask(rows):
    return lax.broadcasted_iota(jnp.int32, (rows, LANES), 1) < HEAD_DIM


def _sigmoid(x):
    return 1.0 / (1.0 + jnp.exp(-x))


def _fold_heads(v):
    return v + pltpu.roll(v, HEAD_DIM, 1)


def _put_rows(ref, vec, accumulate=False):
    for j in range(vec.shape[1] // LANES):
        piece = vec[:, j * LANES:(j + 1) * LANES]
        ref[j:j + 1, :] = ref[j:j + 1, :] + piece if accumulate else piece


def _gather_proj(x, gain, wt_sh):
    tm = 1024
    nrow = SEQ // tm
    widths = (768, 896, 768, 896)
    nunits = len(widths)
    pair = 2 * wt_sh.shape[0]
    assert pair % LANES == 0 and sum(widths[:2]) == pair

    def body(x_ref, g_ref, wt_sh_ref, proj_hbm, wt_hbm, h_scr, land, res, send_sems, recv_sems, out_sems, copy_sem):
        u, i = pl.program_id(0), pl.program_id(1)
        cx_, cy_ = lax.axis_index("x"), lax.axis_index("y")
        (send_own, pass_on_neighbours, pass_on_diagonal, _), (y_complete, x_complete, diagonal_complete, sends_done) = (
            _gather_stages((wt_sh_ref,), (land,), send_sems, recv_sems))
        first = lambda k: (u == k) & (i == 0)
        last = (u == nunits - 1) & (i == nrow - 1)
        to_hbm = pltpu.make_async_copy(land, wt_hbm, copy_sem)

        pl.when(first(0))(send_own)

        @pl.when(u == 0)
        def _():
            xv = x_ref[...]
            ms = jnp.mean(xv * xv, axis=-1, keepdims=True)
            h_scr[pl.ds(pl.multiple_of(i * tm, tm), tm), :] = (xv * lax.rsqrt(ms + EPS) * g_ref[...]).astype(BF16)

        @pl.when(first(1))
        def _():
            pass_on_neighbours()
            y_complete()

        @pl.when(first(2))
        def _():
            x_complete()
            pass_on_diagonal()

        @pl.when(first(3))
        def _():
            diagonal_complete()
            to_hbm.start()

        mine, other = pair * cx_, pair * (1 - cx_)
        col0 = (mine + 896 * cy_, mine + 768 * (1 - cy_), other + 896 * cy_, other + 768 * (1 - cy_))
        slot = i % 2
        rows = pl.ds(pl.multiple_of(i * tm, tm), tm)

        def writeback(k, rows_):
            c0 = pl.multiple_of(col0[k], LANES)
            return pltpu.make_async_copy(res.at[slot, :, pl.ds(0, widths[k])], proj_hbm.at[rows_, pl.ds(c0, widths[k])],
                                         out_sems.at[slot])

        for k in range(nunits):
            @pl.when(u == k)
            def _(k=k):
                pl.when(i >= 2)(writeback(k, rows).wait)
                if k > 0:
                    pl.when(i < 2)(writeback(k - 1, rows).wait)
                w_rows = land[pl.ds(pl.multiple_of(col0[k], LANES), widths[k]), :]
                res[slot, :, 0:widths[k]] = _dot_nt(h_scr[rows, :], w_rows)
                writeback(k, rows).start()

        @pl.when(last)
        def _():
            sends_done()
            to_hbm.wait()
            for s in range(2):
                pltpu.make_async_copy(res.at[s, :, pl.ds(0, widths[-1])], proj_hbm.at[rows, pl.ds(0, widths[-1])], out_sems.at[s]).wait()

    full = jax.ShapeDtypeStruct((4 * wt_sh.shape[0], wt_sh.shape[1]), BF16)
    hbm = pl.BlockSpec(memory_space=pl.ANY)
    return _call(
        body, name="gather_proj", grid=(nunits, nrow),
        in_specs=[pl.BlockSpec((tm, D_MODEL), lambda u, i: (jnp.where(u == 0, i, nrow - 1), 0)),
                  pl.BlockSpec((1, D_MODEL), lambda u, i: (0, 0)), pl.BlockSpec(wt_sh.shape, lambda u, i: (0, 0))],
        out_specs=[hbm, hbm],
        out_shape=[jax.ShapeDtypeStruct((SEQ, IN_W), F32), full],
        scratch_shapes=[pltpu.VMEM((SEQ, D_MODEL), BF16), pltpu.VMEM(full.shape, BF16), pltpu.VMEM((2, tm, max(widths)), F32),
                        pltpu.SemaphoreType.DMA((AG_SEMS,)), pltpu.SemaphoreType.DMA((AG_SEMS,)),
                        pltpu.SemaphoreType.DMA((2,)), pltpu.SemaphoreType.DMA],
        compiler_params=_params(),
    )(x, gain, wt_sh)


def _gmlp_weights(w_ref):
    ti = lax.broadcasted_iota(jnp.int32, (CHUNK, CHUNK), 0)
    si = lax.broadcasted_iota(jnp.int32, (CHUNK, CHUNK), 1)
    tril = si <= ti
    return tril, [jnp.where(tril, w_ref[h], 0.0).astype(BF16) for h in range(4)]


def _gmlp_fwd(proj, vgain, w_s, bias_full):
    tm = 1024

    def body(p_ref, vg_ref, w_ref, b_ref, y_ref):
        bd = _head_blockdiag()
        lo = _lo_mask(CHUNK)
        _, wm = _gmlp_weights(w_ref)
        units = [(pl.ds(c * CHUNK, CHUNK), p) for c in range(tm // CHUNK) for p in range(2)]
        col = lambda c0, p: slice(c0 + p * LANES, c0 + (p + 1) * LANES)
        vs = [p_ref[rows, col(C_GV, p)] for rows, p in units]
        rs = [lax.rsqrt(_headsum(v * v, bd) * (1.0 / HEAD_DIM) + EPS) for v in vs]
        vns = [(v * r * vg_ref[:, col(0, p)]).astype(BF16) for v, r, (_, p) in zip(vs, rs, units)]
        sps = [jnp.where(lo, _dot(wm[2 * p], vn), _dot(wm[2 * p + 1], vn)) + b_ref[:, col(0, p)] for vn, (_, p) in zip(vns, units)]
        for sp, (rows, p) in zip(sps, units):
            gt = p_ref[rows, col(C_GG, p)]
            y_ref[rows, col(0, p)] = (p_ref[rows, col(C_GU, p)] * sp * (gt * _sigmoid(gt))).astype(BF16)

    return _call(
        body, name="gmlp_fwd", grid=(SEQ // tm,),
        in_specs=[pl.BlockSpec((tm, 3 * GMLP_W), lambda i: (i, 0)),
                  pl.BlockSpec((1, GMLP_W), lambda i: (0, 0)),
                  pl.BlockSpec((4, CHUNK, CHUNK), lambda i: (0, 0, 0)),
                  pl.BlockSpec((CHUNK, GMLP_W), lambda i: (0, 0))],
        out_specs=pl.BlockSpec((tm, GMLP_W), lambda i: (i, 0)),
        out_shape=jax.ShapeDtypeStruct((SEQ, GMLP_W), BF16),
        compiler_params=_params(),
    )(proj, vgain, w_s, bias_full)


def _gmlp_bwd(proj, dyc, vgain, w_s, bias_full):
    tm = 1024
    nsteps = SEQ // tm

    def body(p_ref, dy_ref, vg_ref, w_ref, b_ref, dg_ref, gw_ref, gb_ref, gv_ref):
        i = pl.program_id(0)
        bd = _head_blockdiag()
        lo = _lo_mask(CHUNK)
        tril, wm = _gmlp_weights(w_ref)
        ri = lax.broadcasted_iota(jnp.int32, (16, LANES), 0)
        li = lax.broadcasted_iota(jnp.int32, (16, LANES), 1)
        head_rows = [jnp.where(((ri == 2 * p) & (li < HEAD_DIM)) | ((ri == 2 * p + 1) & (li >= HEAD_DIM)), 1.0, 0.0).astype(BF16)
                     for p in range(2)]

        @pl.when(i == 0)
        def _():
            gw_ref[...] = jnp.zeros_like(gw_ref)
            gb_ref[...] = jnp.zeros_like(gb_ref)
            gv_ref[...] = jnp.zeros_like(gv_ref)

        units = [(pl.ds(c * CHUNK, CHUNK), p) for c in range(tm // CHUNK) for p in range(2)]
        col = lambda c0, p: slice(c0 + p * LANES, c0 + (p + 1) * LANES)
        vs = [p_ref[rows, col(C_GV, p)] for rows, p in units]
        rs = [lax.rsqrt(_headsum(v * v, bd) * (1.0 / HEAD_DIM) + EPS) for v in vs]
        zs = [v * r for v, r in zip(vs, rs)]
        vns = [(z * vg_ref[:, col(0, p)]).astype(BF16) for z, (_, p) in zip(zs, units)]
        sps = [jnp.where(lo, _dot(wm[2 * p], vn), _dot(wm[2 * p + 1], vn)) + b_ref[:, col(0, p)] for vn, (_, p) in zip(vns, units)]
        dsps = []
        for sp, (rows, p) in zip(sps, units):
            u = p_ref[rows, col(C_GU, p)]
            gt = p_ref[rows, col(C_GG, p)]
            dy = dy_ref[rows, col(0, p)]
            sg = _sigmoid(gt)
            sl = gt * sg
            dg_ref[rows, col(C_GU, p)] = (dy * sp * sl).astype(BF16)
            dg_ref[rows, col(C_GG, p)] = (dy * u * sp * (sg * (1.0 + gt * (1.0 - sg)))).astype(BF16)
            dsps.append(dy * u * sl)
        dspbs = [dsp.astype(BF16) for dsp in dsps]
        dvns = [jnp.where(lo, _dot_tn(wm[2 * p], dspb), _dot_tn(wm[2 * p + 1], dspb)) for dspb, (_, p) in zip(dspbs, units)]
        gws = [(_dot_nt(jnp.where(lo, dsp, 0.0).astype(BF16), vn), _dot_nt(jnp.where(lo, 0.0, dsp).astype(BF16), vn))
               for dsp, vn in zip(dsps, vns)]
        gbs = [(_dot_nt(head_rows[p], dspb) + _dot_nt(head_rows[p], (dsp - dspb.astype(F32)).astype(BF16)))[0:8]
               for dsp, dspb, (_, p) in zip(dsps, dspbs, units)]
        for p in range(2):
            mine = [n for n, (_, q) in enumerate(units) if q == p]
            gw_ref[2 * p] += sum(gws[n][0] for n in mine)
            gw_ref[2 * p + 1] += sum(gws[n][1] for n in mine)
            gvp = sum(jnp.sum(dvns[n] * zs[n], axis=0, keepdims=True) for n in mine)
            gv_ref[2 * p:2 * p + 1, :] += gvp
            gv_ref[2 * p + 1:2 * p + 2, :] += pltpu.roll(gvp, HEAD_DIM, 1)
        gb_ref[...] += sum(gbs)
        for dvn, z, r, (rows, p) in zip(dvns, zs, rs, units):
            dz = dvn * vg_ref[:, col(0, p)]
            dg_ref[rows, col(C_GV, p)] = (r * (dz - z * (_headsum(dz * z, bd) * (1.0 / HEAD_DIM)))).astype(BF16)

        @pl.when(i == nsteps - 1)
        def _():
            for h in range(4):
                gw_ref[h] = jnp.where(tril, gw_ref[h], 0.0)

    return _call(
        body, name="gmlp_bwd", grid=(nsteps,),
        in_specs=[pl.BlockSpec((tm, 3 * GMLP_W), lambda i: (i, 0)),
                  pl.BlockSpec((tm, GMLP_W), lambda i: (i, 0)),
                  pl.BlockSpec((1, GMLP_W), lambda i: (0, 0)),
                  pl.BlockSpec((4, CHUNK, CHUNK), lambda i: (0, 0, 0)),
                  pl.BlockSpec((CHUNK, GMLP_W), lambda i: (0, 0))],
        out_specs=[pl.BlockSpec((tm, 3 * GMLP_W), lambda i: (i, 0)),
                   pl.BlockSpec((4, CHUNK, CHUNK), lambda i: (0, 0, 0)),
                   pl.BlockSpec((8, LANES), lambda i: (0, 0)),
                   pl.BlockSpec((8, LANES), lambda i: (0, 0))],
        out_shape=[jax.ShapeDtypeStruct((SEQ, 3 * GMLP_W), BF16),
                   jax.ShapeDtypeStruct((4, CHUNK, CHUNK), F32),
                   jax.ShapeDtypeStruct((8, LANES), F32),
                   jax.ShapeDtypeStruct((8, LANES), F32)],
        compiler_params=_params(),
    )(proj, dyc, vgain, w_s, bias_full)


def _band_masks():
    qi = lax.broadcasted_iota(jnp.int32, (CHUNK, 2 * CHUNK), 0)
    kj = lax.broadcasted_iota(jnp.int32, (CHUNK, 2 * CHUNK), 1)
    valid2 = ((kj < CHUNK) & (kj >= qi)) | ((kj >= CHUNK) & (kj - CHUNK <= qi))
    q1 = lax.broadcasted_iota(jnp.int32, (CHUNK, CHUNK), 0)
    k1 = lax.broadcasted_iota(jnp.int32, (CHUNK, CHUNK), 1)
    return k1 <= q1, valid2


def _stack_heads(v, lo):
    return jnp.concatenate([jnp.where(lo, v, 0.0), jnp.where(lo, 0.0, v)], axis=0).astype(BF16)


def _rows_of(ref, start, d):
    if d == 1:
        return ref.at[pl.ds(start if isinstance(start, int) else pl.multiple_of(start, CHUNK), CHUNK), :]
    return ref.at[pl.ds(start, CHUNK, stride=d), :]


def _unrolled(lo, hi, unroll, run):
    groups = (hi - lo) // unroll
    if groups:
        def body(g, carry):
            run([lo + g * unroll + t for t in range(unroll)])
            return carry

        lax.fori_loop(0, groups, body, 0)
    if lo + groups * unroll < hi:
        run(range(lo + groups * unroll, hi))


def _for_blocks(d, group_fn, unroll):
    nblk = SEQ // CHUNK
    sh = d.bit_length() - 1

    def first(j):
        return (j * CHUNK if d == 1 else j, None)

    def rest(j):
        start = (j & (d - 1)) + (j >> sh) * (CHUNK * d)
        return (start, start - CHUNK * d)

    _unrolled(0, d, unroll, lambda js: group_fn(d, [first(j) for j in js]))
    _unrolled(d, nblk, unroll, lambda js: group_fn(d, [rest(j) for j in js]))


def _attn_fwd(proj, gq2, gk2, *ride_along):
    tn = 512
    npairs = ATTN_W // LANES
    nride = len(ride_along)

    def body(q_ref, k_ref, v_ref, g_ref, gq_ref, gk_ref, *rest):
        shards, rest = rest[:nride], rest[nride:]
        o_ref, l_ref, ya_ref = rest[:3]
        gathered, rest = rest[3:3 + nride], rest[3 + nride:]
        qn_ref, kn_ref = rest[:2]
        lands, (send_sems, recv_sems, copy_sems) = rest[2:2 + nride], rest[2 + nride:]
        pair = pl.program_id(0)
        ride = _gather_stages(shards, lands, send_sems, recv_sems)[0]
        for step in range(npairs):
            pl.when(pair == step)(ride[step])
        bd = _head_blockdiag()
        lo = _lo_mask(CHUNK)
        valid1, valid2 = _band_masks()

        def norm(t, carry):
            rows = pl.ds(pl.multiple_of(t * tn, tn), tn)
            q, k = q_ref[rows, :], k_ref[rows, :]
            ssq = [_headsum(a * a, bd) for a in (q, k)]
            qn_ref[rows, :] = q * lax.rsqrt(ssq[0] * (1.0 / HEAD_DIM) + EPS) * (gq_ref[...] * QK_SCALE)
            kn_ref[rows, :] = k * lax.rsqrt(ssq[1] * (1.0 / HEAD_DIM) + EPS) * gk_ref[...]
            return carry

        lax.fori_loop(0, SEQ // tn, norm, 0)

        def load_kv(ref, d, start, prev):
            own = _rows_of(ref, start, d)[...]
            if prev is None:
                return own.astype(BF16)
            return jnp.concatenate([_rows_of(ref, prev, d)[...], own], axis=0).astype(BF16)

        def group(d, blocks):
            valid = valid1 if blocks[0][1] is None else valid2
            valid = jnp.concatenate([valid, valid], axis=0)
            qs = [_rows_of(qn_ref, start, d)[...] for start, _ in blocks]
            ks = [load_kv(kn_ref, d, start, prev) for start, prev in blocks]
            vs = [load_kv(v_ref, d, start, prev) for start, prev in blocks]
            ss = [_dot_nt(_stack_heads(q, lo), k) for q, k in zip(qs, ks)]
            ms, ps, ls = [], [], []
            for s in ss:
                s = jnp.where(valid, s, -jnp.inf)
                m = jnp.max(s, axis=-1, keepdims=True)
                p = jnp.exp(s - m)
                ms.append(m)
                ls.append(jnp.sum(p, axis=-1, keepdims=True))
                ps.append(p.astype(BF16))
            os_ = [_dot(p, v) for p, v in zip(ps, vs)]
            for b, (start, _) in enumerate(blocks):
                heads = lambda v: jnp.where(lo, v[:CHUNK], v[CHUNK:])
                lsum = heads(ls[b])
                ob = heads(os_[b]) * (1.0 / lsum)
                lb = heads(ms[b]) + jnp.log(lsum)
                o_rows = _rows_of(o_ref, start, d)
                l_rows = _rows_of(l_ref, start, d)
                if d != DILATIONS[0]:
                    lold = l_rows[...]
                    mx = jnp.maximum(lold, lb)
                    ea = jnp.exp(lold - mx)
                    eb = jnp.exp(lb - mx)
                    inv = 1.0 / (ea + eb)
                    ob = o_rows[...] * (ea * inv) + ob * (eb * inv)
                    lb = mx + jnp.log(ea + eb)
                o_rows[...] = ob
                l_rows[...] = lb

        for d in DILATIONS:
            _for_blocks(d, group, ATTN_UNROLL)

        def fin(t, carry):
            rows = pl.ds(pl.multiple_of(t * tn, tn), tn)
            g = g_ref[rows, :]
            ya_ref[rows, :] = (o_ref[rows, :] * (g * _sigmoid(g))).astype(BF16)
            return carry

        lax.fori_loop(0, SEQ // tn, fin, 0)

        @pl.when(pair == npairs - 1)
        def _():
            to_hbm = [pltpu.make_async_copy(land, out, copy_sems.at[n]) for n, (land, out) in enumerate(zip(lands, gathered))]
            for cp in to_hbm:
                cp.start()
            for cp in to_hbm:
                cp.wait()

    col = lambda c0: pl.BlockSpec((SEQ, LANES), lambda p: (0, c0 // LANES + p))
    vec = pl.BlockSpec((1, LANES), lambda p: (0, 0))
    out = pl.BlockSpec((SEQ, LANES), lambda p: (0, p))
    full = [jax.ShapeDtypeStruct((4 * a.shape[0], a.shape[1]), BF16) for a in ride_along]
    return _call(
        body, name="attn_fwd", grid=(npairs,),
        in_specs=[col(C_AQ), col(C_AK), col(C_AV), col(C_AG), vec, vec]
        + [pl.BlockSpec(a.shape, lambda p: (0, 0)) for a in ride_along],
        out_specs=[out, out, out] + [pl.BlockSpec(memory_space=pl.ANY)] * nride,
        out_shape=[jax.ShapeDtypeStruct((SEQ, ATTN_W), F32), jax.ShapeDtypeStruct((SEQ, ATTN_W), F32),
                   jax.ShapeDtypeStruct((SEQ, ATTN_W), BF16)] + full,
        scratch_shapes=[pltpu.VMEM((SEQ, LANES), F32), pltpu.VMEM((SEQ, LANES), F32)]
        + [pltpu.VMEM(s.shape, BF16) for s in full]
        + [pltpu.SemaphoreType.DMA((AG_SEMS * nride,)), pltpu.SemaphoreType.DMA((AG_SEMS * nride,)),
           pltpu.SemaphoreType.DMA((nride,))],
        compiler_params=_params(),
    )(proj, proj, proj, proj, gq2, gk2, *ride_along)


def _attn_bwd(proj, o, lse, dyc, gq2, gk2, *ride_along):
    tn = 2048
    npairs = ATTN_W // LANES
    nride = len(ride_along)
    nbufs = nride * len(RS_KINDS)

    def body(proj_hbm, o_hbm, l_hbm, dyc_hbm, gq_ref, gk_ref, *rest):
        ride_in, rest = rest[:nride], rest[nride:]
        dq_ref, dk_ref, dv_ref, dgt_ref, gqg_ref, gkg_ref = rest[:6]
        ride_out, rest = rest[6:6 + nride], rest[6 + nride:]
        qb_, kb_, vb_, gb_, ob_, lb_, yb_, dkb_, dvb_, sems = rest[:10]
        rs_bufs, (send_sems, recv_sems, local_sems) = rest[10:10 + nbufs], rest[10 + nbufs:]
        rs_stage = _rs_stages(ride_in, ride_out, rs_bufs, send_sems, recv_sems, local_sems, [g.shape[1] for g in ride_along])
        pair = pl.program_id(0)
        for step in range(npairs):
            pl.when(pair == step)(rs_stage[step])
        bd = _head_blockdiag()
        lo = _lo_mask(CHUNK)
        lo2 = lax.broadcasted_iota(jnp.int32, (2 * CHUNK, LANES), 1) < HEAD_DIM
        valid1, valid2 = _band_masks()
        gqs = gq_ref[...] * QK_SCALE
        gk = gk_ref[...]

        def pcol(c0, of=None):
            return acol(proj_hbm, c0, of)

        def acol(hbm, c0=0, of=None):
            of = pair if of is None else of
            return hbm.at[:, pl.ds(pl.multiple_of(c0 + of * LANES, LANES), LANES)]

        def input_loads(of):
            return [pltpu.make_async_copy(src, dst, sems.at[n]) for n, (src, dst) in enumerate((
                (pcol(C_AQ, of), qb_), (pcol(C_AK, of), kb_), (pcol(C_AG, of), gb_), (acol(o_hbm, 0, of), ob_),
                (acol(dyc_hbm, GMLP_W, of), yb_), (pcol(C_AV, of), vb_), (acol(l_hbm, 0, of), lb_)))]

        early = (0, 1, 3, 4)
        loads = input_loads(pair)
        for n, cp in enumerate(loads):
            if n in early:
                pl.when(pair == 0)(cp.start)
            else:
                cp.start()

        @pl.when(pair == 0)
        def _():
            gqg_ref[...] = jnp.zeros_like(gqg_ref)
            gkg_ref[...] = jnp.zeros_like(gkg_ref)

        def pre_qk(t, carry):
            rows = pl.ds(pl.multiple_of(t * tn, tn), tn)
            q, k = qb_[rows, :], kb_[rows, :]
            ssq = [_headsum(a * a, bd) for a in (q, k)]
            qb_[rows, :] = q * lax.rsqrt(ssq[0] * (1.0 / HEAD_DIM) + EPS) * gqs
            kb_[rows, :] = k * lax.rsqrt(ssq[1] * (1.0 / HEAD_DIM) + EPS) * gk
            return carry

        def pre_gate(t, carry):
            rows = pl.ds(pl.multiple_of(t * tn, tn), tn)
            g = gb_[rows, :]
            ov = ob_[rows, :]
            dya = yb_[rows, :]
            sg = _sigmoid(g)
            dgt_ref[rows, :] = (dya * ov * (sg * (1.0 + g * (1.0 - sg)))).astype(BF16)
            do = dya * (g * sg)
            yb_[rows, :] = do
            ob_[rows, :] = jnp.where(first_half, lb_[rows, :], _headsum(do * ov, bd))
            return carry

        first_half = (lax.broadcasted_iota(jnp.int32, (tn, LANES), 1) & (HEAD_DIM - 1)) < HEAD_DIM // 2
        loads[0].wait()
        loads[1].wait()
        lax.fori_loop(0, SEQ // tn, pre_qk, 0)
        for cp in loads[2:5] + loads[6:7]:
            cp.wait()
        lax.fori_loop(0, SEQ // tn, pre_gate, 0)
        loads[5].wait()
        reloads = [pltpu.make_async_copy(pcol(C_AQ), lb_, sems.at[7]), pltpu.make_async_copy(pcol(C_AK), vb_, sems.at[8])]
        reloads[0].start()

        def load_kv(ref, d, start, prev):
            own = _rows_of(ref, start, d)[...]
            if prev is None:
                return own.astype(BF16)
            return jnp.concatenate([_rows_of(ref, prev, d)[...], own], axis=0).astype(BF16)

        def group(d, blocks):
            first = blocks[0][1] is None
            valid, lok = (valid1, lo) if first else (valid2, lo2)
            chains = [(b, h) for b in range(len(blocks)) for h in range(2)]
            mask = lambda h: lo if h == 0 else ~lo
            qs = [_rows_of(qb_, start, d)[...] for start, _ in blocks]
            dos = [_rows_of(yb_, start, d)[...] for start, _ in blocks]
            lds = [_rows_of(ob_, start, d)[...] for start, _ in blocks]
            ks = [load_kv(kb_, d, start, prev) for start, prev in blocks]
            vs = [load_kv(vb_, d, start, prev) for start, prev in blocks]
            qbs = [q.astype(BF16) for q in qs]
            dobs = [do.astype(BF16) for do in dos]
            ss = [_dot_nt(jnp.where(mask(h), qs[b], 0.0).astype(BF16), ks[b]) for b, h in chains]
            dps = [_dot_nt(jnp.where(mask(h), dos[b], 0.0).astype(BF16), vs[b]) for b, h in chains]
            pbs, dss = [], []
            for s, dp, (b, h) in zip(ss, dps, chains):
                hc, dc = h * HEAD_DIM, h * HEAD_DIM + HEAD_DIM // 2
                p = jnp.exp(jnp.where(valid, s, -jnp.inf) - lds[b][:, hc:hc + 1])
                pbs.append(p.astype(BF16))
                dss.append((p * (dp - lds[b][:, dc:dc + 1])).astype(BF16))
            dqs = [_dot(ds, ks[b]) for ds, (b, h) in zip(dss, chains)]
            dks = [_dot_tn(ds, qbs[b]) for ds, (b, h) in zip(dss, chains)]
            dvs = [_dot_tn(p, dobs[b]) for p, (b, h) in zip(pbs, chains)]
            assign = d == DILATIONS[0]
            for b, (start, prev) in enumerate(blocks):
                c0, c1 = 2 * b, 2 * b + 1
                dq_rows = _rows_of(gb_, start, d)
                dqb = jnp.where(lo, dqs[c0], dqs[c1])
                dq_rows[...] = dqb if assign else dq_rows[...] + dqb
                dkc = jnp.where(lok, dks[c0], dks[c1])
                dvc = jnp.where(lok, dvs[c0], dvs[c1])
                spans = ((start, slice(0, CHUNK), True),) if first else (
                    (prev, slice(0, CHUNK), False), (start, slice(CHUNK, 2 * CHUNK), True))
                for st, sl, own in spans:
                    dk_rows = _rows_of(dkb_, st, d)
                    dv_rows = _rows_of(dvb_, st, d)
                    if assign and own:
                        dk_rows[...] = dkc[sl]
                        dv_rows[...] = dvc[sl]
                    else:
                        dk_rows[...] = dk_rows[...] + dkc[sl]
                        dv_rows[...] = dv_rows[...] + dvc[sl]

        for d in DILATIONS:
            _for_blocks(d, group, ATTN_UNROLL)

        reloads[1].start()

        @pl.when(pair < npairs - 1)
        def _():
            nxt = input_loads(pair + 1)
            for n in early:
                nxt[n].start()

        for cp in reloads:
            cp.wait()

        def post(t, carry):
            gq_acc, gk_acc = carry
            rows = pl.ds(pl.multiple_of(t * tn, tn), tn)
            raws = [lb_[rows, :], vb_[rows, :]]
            dns = [gb_[rows, :], dkb_[rows, :]]
            rs = [lax.rsqrt(_headsum(a * a, bd) * (1.0 / HEAD_DIM) + EPS) for a in raws]
            zs = [a * r for a, r in zip(raws, rs)]
            dzs = [dn * gain for dn, gain in zip(dns, (gqs, gk))]
            means = [_headsum(dz * z, bd) * (1.0 / HEAD_DIM) for dz, z in zip(dzs, zs)]
            dq, dk = [r * (dz - z * mean) for r, dz, z, mean in zip(rs, dzs, zs, means)]
            gq, gkk = [jnp.sum(dn * z, axis=0, keepdims=True) for dn, z in zip(dns, zs)]
            dq_ref[rows, :] = dq.astype(BF16)
            dk_ref[rows, :] = dk.astype(BF16)
            dv_ref[rows, :] = dvb_[rows, :].astype(BF16)
            return gq_acc + gq * QK_SCALE, gk_acc + gkk

        zero = jnp.zeros((1, LANES), F32)
        gq_acc, gk_acc = lax.fori_loop(0, SEQ // tn, post, (zero, zero))
        gqg_ref[0:1, :] += gq_acc
        gkg_ref[0:1, :] += gk_acc

        @pl.when(pair == npairs - 1)
        def _():
            gqg_ref[0:1, :] = _fold_heads(gqg_ref[0:1, :])
            gkg_ref[0:1, :] = _fold_heads(gkg_ref[0:1, :])
            rs_stage[npairs]()

    hbm = pl.BlockSpec(memory_space=pl.ANY)
    vec = pl.BlockSpec((1, LANES), lambda p: (0, 0))
    blk8 = pl.BlockSpec((8, LANES), lambda p: (0, 0))
    out = pl.BlockSpec((SEQ, LANES), lambda p: (0, p))
    big = jax.ShapeDtypeStruct((SEQ, ATTN_W), BF16)
    nsem = RS_SEMS * nride
    return _call(
        body, name="attn_bwd", grid=(npairs,),
        in_specs=[hbm, hbm, hbm, hbm, vec, vec] + [hbm] * nride,
        out_specs=[out, out, out, out, blk8, blk8] + [hbm] * nride,
        out_shape=[big, big, big, big, jax.ShapeDtypeStruct((8, LANES), F32), jax.ShapeDtypeStruct((8, LANES), F32)]
        + [jax.ShapeDtypeStruct((2, g.shape[0] // 8, g.shape[1]), F32) for g in ride_along],
        scratch_shapes=[pltpu.VMEM((SEQ, LANES), F32) for _ in range(9)] + [pltpu.SemaphoreType.DMA((9,))]
        + _rs_scratch([g.shape for g in ride_along]) + [pltpu.SemaphoreType.DMA((nsem,)), pltpu.SemaphoreType.DMA((nsem,)),
                                     pltpu.SemaphoreType.DMA((nride,))],
        compiler_params=_params(),
    )(proj, o, lse, dyc, gq2, gk2, *[_rs_view(g) for g in ride_along])


def _mem_kv(mem, gain, wkv):
    def body(m_ref, g_ref, w_ref, kv_ref, hm_ref):
        mv = m_ref[...]
        ms = jnp.mean(mv * mv, axis=-1, keepdims=True)
        hm = (mv * lax.rsqrt(ms + EPS) * g_ref[...]).astype(BF16)
        hm_ref[...] = hm
        kv_ref[...] = _dot(hm, w_ref[...])

    return _call(
        body, name="mem_kv",
        out_shape=[jax.ShapeDtypeStruct((MEM_LEN, 2 * MEM_W), F32), jax.ShapeDtypeStruct((MEM_LEN, D_MODEL), BF16)],
        compiler_params=_params(),
    )(mem, gain, wkv)


def _mem_keys(kv_ref, kg_ref, bd, p):
    mk = kv_ref[:, p * LANES:(p + 1) * LANES]
    r = lax.rsqrt(_headsum(mk * mk, bd) * (1.0 / HEAD_DIM) + EPS)
    z = mk * r
    mkn = (z * kg_ref[:, p * LANES:(p + 1) * LANES]).astype(BF16)
    mvp = kv_ref[:, MEM_W + p * LANES:MEM_W + (p + 1) * LANES].astype(BF16)
    return mkn, mvp, r, z


def _mem_fwd(proj, kv, qg4, kg4):
    tm = 1024

    def body(q_ref, g_ref, kv_ref, qg_ref, kg_ref, om_ref, ym_ref):
        bd = _head_blockdiag()
        lo = _lo_mask(tm)
        keys, qns = [], []
        for p in range(2):
            cs = slice(p * LANES, (p + 1) * LANES)
            keys.append(_mem_keys(kv_ref, kg_ref, bd, p)[:2])
            q = q_ref[:, cs]
            qns.append(q * lax.rsqrt(_headsum(q * q, bd) * (1.0 / HEAD_DIM) + EPS) * (qg_ref[:, cs] * QK_SCALE))
        chains = [(p, h) for p in range(2) for h in range(2)]
        ss = [_dot_nt(jnp.where(lo if h == 0 else ~lo, qns[p], 0.0).astype(BF16), keys[p][0]) for p, h in chains]
        es = [jnp.exp(s - jnp.max(s, axis=-1, keepdims=True)) for s in ss]
        os_ = [_dot(e.astype(BF16), keys[p][1]) for e, (p, h) in zip(es, chains)]
        res = [o * (1.0 / jnp.sum(e, axis=-1, keepdims=True)) for o, e in zip(os_, es)]
        for p in range(2):
            cs = slice(p * LANES, (p + 1) * LANES)
            ov = jnp.where(lo, res[2 * p], res[2 * p + 1])
            g = g_ref[:, cs]
            om_ref[:, cs] = ov
            ym_ref[:, cs] = (ov * (g * _sigmoid(g))).astype(BF16)

    vec = pl.BlockSpec((1, MEM_W), lambda i: (0, 0))
    return _call(
        body, name="mem_fwd", grid=(SEQ // tm,),
        in_specs=[pl.BlockSpec((tm, MEM_W), lambda i: (i, C_MQ // MEM_W)),
                  pl.BlockSpec((tm, MEM_W), lambda i: (i, C_MG // MEM_W)),
                  pl.BlockSpec((MEM_LEN, 2 * MEM_W), lambda i: (0, 0)), vec, vec],
        out_specs=[pl.BlockSpec((tm, MEM_W), lambda i: (i, 0)), pl.BlockSpec((tm, MEM_W), lambda i: (i, 0))],
        out_shape=[jax.ShapeDtypeStruct((SEQ, MEM_W), F32), jax.ShapeDtypeStruct((SEQ, MEM_W), BF16)],
        compiler_params=_params(),
    )(proj, proj, kv, qg4, kg4)


def _mem_bwd(proj, om, dyc, kv, hm, mem, mgain, wkv, qg4, kg4):
    tm = 1024
    nsteps = SEQ // tm

    def body(q_ref, g_ref, om_ref, dy_ref, kv_ref, hm_ref, mem_ref, mg_ref, w_ref, qg_ref, kg_ref,
             dq_ref, dgt_ref, gqg_ref, gkg_ref, gw_ref, gmg_ref, dmk_ref, dmv_ref, gq_acc):
        i = pl.program_id(0)
        bd = _head_blockdiag()
        lo = _lo_mask(tm)
        lom = _lo_mask(MEM_LEN)

        @pl.when(i == 0)
        def _():
            dmk_ref[...] = jnp.zeros_like(dmk_ref)
            dmv_ref[...] = jnp.zeros_like(dmv_ref)
            gq_acc[...] = jnp.zeros_like(gq_acc)

        pairs = []
        for p in range(2):
            cs = slice(p * LANES, (p + 1) * LANES)
            mkn, mvp, _, _ = _mem_keys(kv_ref, kg_ref, bd, p)
            gqs = qg_ref[:, cs] * QK_SCALE
            q = q_ref[:, cs]
            r = lax.rsqrt(_headsum(q * q, bd) * (1.0 / HEAD_DIM) + EPS)
            z = q * r
            qn = z * gqs
            g = g_ref[:, cs]
            ov = om_ref[:, cs]
            dym = dy_ref[:, cs]
            sg = _sigmoid(g)
            dgt_ref[:, cs] = (dym * ov * (sg * (1.0 + g * (1.0 - sg)))).astype(BF16)
            do = dym * (g * sg)
            pairs.append(dict(cs=cs, mkn=mkn, mvp=mvp, gqs=gqs, r=r, z=z, qn=qn, qnb=qn.astype(BF16), do=do,
                              dob=do.astype(BF16), delta=_headsum(do * ov, bd)))
        chains = [(pr_, h) for pr_ in pairs for h in range(2)]
        mask = lambda h: lo if h == 0 else ~lo
        ss = [_dot_nt(jnp.where(mask(h), c["qn"], 0.0).astype(BF16), c["mkn"]) for c, h in chains]
        dps = [_dot_nt(jnp.where(mask(h), c["do"], 0.0).astype(BF16), c["mvp"]) for c, h in chains]
        prs, dss = [], []
        for s, dp, (c, h) in zip(ss, dps, chains):
            e = jnp.exp(s - jnp.max(s, axis=-1, keepdims=True))
            pr = e * (1.0 / jnp.sum(e, axis=-1, keepdims=True))
            prs.append(pr.astype(BF16))
            dss.append((pr * (dp - c["delta"][:, h * HEAD_DIM:h * HEAD_DIM + 1])).astype(BF16))
        dqs = [_dot(ds, c["mkn"]) for ds, (c, h) in zip(dss, chains)]
        dks = [_dot_tn(ds, c["qnb"]) for ds, (c, h) in zip(dss, chains)]
        dvs = [_dot_tn(pr, c["dob"]) for pr, (c, h) in zip(prs, chains)]
        for p, c in enumerate(pairs):
            cs, z, r = c["cs"], c["z"], c["r"]
            dqn = jnp.where(lo, dqs[2 * p], dqs[2 * p + 1])
            dmk_ref[:, cs] += jnp.where(lom, dks[2 * p], dks[2 * p + 1])
            dmv_ref[:, cs] += jnp.where(lom, dvs[2 * p], dvs[2 * p + 1])
            dz = dqn * c["gqs"]
            dq_ref[:, cs] = (r * (dz - z * (_headsum(dz * z, bd) * (1.0 / HEAD_DIM)))).astype(BF16)
            gq_acc[:, cs] += jnp.sum(dqn * z, axis=0, keepdims=True) * QK_SCALE

        @pl.when(i == nsteps - 1)
        def _():
            gqg_ref[...] = jnp.zeros_like(gqg_ref)
            gkg_ref[...] = jnp.zeros_like(gkg_ref)
            gqg_ref[0:1, :] = _fold_heads(gq_acc[:, 0:LANES] + gq_acc[:, LANES:2 * LANES])
            dkv = []
            gk = jnp.zeros((1, LANES), F32)
            for p in range(2):
                cs = slice(p * LANES, (p + 1) * LANES)
                _, _, r, z = _mem_keys(kv_ref, kg_ref, bd, p)
                dn = dmk_ref[:, cs]
                dz = dn * kg_ref[:, cs]
                gk = gk + jnp.sum(dn * z, axis=0, keepdims=True)
                dkv.append(r * (dz - z * (_headsum(dz * z, bd) * (1.0 / HEAD_DIM))))
            gkg_ref[0:1, :] = _fold_heads(gk)
            dkvb = jnp.concatenate(dkv + [dmv_ref[...]], axis=1).astype(BF16)
            gw_ref[...] = _dot_tn(hm_ref[...], dkvb)
            dhm = _dot_nt(dkvb, w_ref[...])
            mv = mem_ref[...]
            zm = mv * lax.rsqrt(jnp.mean(mv * mv, axis=-1, keepdims=True) + EPS)
            _put_rows(gmg_ref, jnp.sum(dhm * zm, axis=0, keepdims=True))

    const = lambda shape: pl.BlockSpec(shape, lambda i: (0,) * len(shape))
    row = lambda j: pl.BlockSpec((tm, MEM_W), lambda i: (i, j))
    blk8 = jax.ShapeDtypeStruct((8, LANES), F32)
    return _call(
        body, name="mem_bwd", grid=(nsteps,),
        in_specs=[row(C_MQ // MEM_W), row(C_MG // MEM_W), row(0), row((GMLP_W + ATTN_W) // MEM_W),
                  const((MEM_LEN, 2 * MEM_W)), const((MEM_LEN, D_MODEL)), const((MEM_LEN, D_MODEL)),
                  const((1, D_MODEL)), const((D_MODEL, 2 * MEM_W)), const((1, MEM_W)), const((1, MEM_W))],
        out_specs=[row(0), row(0), const((8, LANES)), const((8, LANES)),
                   const((D_MODEL, 2 * MEM_W)), const((8, LANES))],
        out_shape=[jax.ShapeDtypeStruct((SEQ, MEM_W), BF16), jax.ShapeDtypeStruct((SEQ, MEM_W), BF16),
                   blk8, blk8, jax.ShapeDtypeStruct((D_MODEL, 2 * MEM_W), F32), blk8],
        scratch_shapes=[pltpu.VMEM((MEM_LEN, MEM_W), F32), pltpu.VMEM((MEM_LEN, MEM_W), F32),
                        pltpu.VMEM((1, MEM_W), F32)],
        compiler_params=_params(),
    )(proj, proj, om, dyc, kv, hm, mem, mgain, wkv, qg4, kg4)


def _out_loss(yg, ya, ym, x, tgt, wo):
    tm = 512
    nsteps = SEQ // tm
    parts = ((0, GMLP_W), (GMLP_W, ATTN_W), (GMLP_W + ATTN_W, MEM_W))

    def body(yg_ref, ya_ref, ym_ref, x_ref, t_ref, w_ref, dy_ref, dyc_ref, gw_ref, ls_ref):
        i = pl.program_id(0)

        @pl.when(i == 0)
        def _():
            gw_ref[...] = jnp.zeros_like(gw_ref)
            ls_ref[...] = jnp.zeros_like(ls_ref)

        ys = (yg_ref[...], ya_ref[...], ym_ref[...])
        y = sum(_dot(yv, w_ref[r0:r0 + n, :]) for yv, (r0, n) in zip(ys, parts))
        err = x_ref[...] + y - t_ref[...]
        _put_rows(ls_ref, jnp.sum(err * err, axis=0, keepdims=True), accumulate=True)
        dy = err * (1.0 / D_MODEL)
        dy_ref[...] = dy
        dyb = dy.astype(BF16)
        dyc_ref[...] = _dot_nt(dyb, w_ref[...])
        for yv, (r0, n) in zip(ys, parts):
            gw_ref[r0:r0 + n, :] += _dot_tn(yv, dyb)

    row = lambda w: pl.BlockSpec((tm, w), lambda i: (i, 0))
    const = lambda shape: pl.BlockSpec(shape, lambda i: (0, 0))
    return _call(
        body, name="out_loss", grid=(nsteps,),
        in_specs=[row(GMLP_W), row(ATTN_W), row(MEM_W), row(D_MODEL), row(D_MODEL), const((D_MODEL, D_MODEL))],
        out_specs=[row(D_MODEL), row(D_MODEL), const((D_MODEL, D_MODEL)), const((8, LANES))],
        out_shape=[jax.ShapeDtypeStruct((SEQ, D_MODEL), F32), jax.ShapeDtypeStruct((SEQ, D_MODEL), F32),
                   jax.ShapeDtypeStruct((D_MODEL, D_MODEL), F32), jax.ShapeDtypeStruct((8, LANES), F32)],
        compiler_params=_params(),
    )(yg, ya, ym, x, tgt, wo)


def _proj_bwd(x, dy, gain, wt, dg, daq, dak, dav, dag, dmq, dmg):
    tm = 512
    nsteps = SEQ // tm
    pieces = ((C_GU, 3 * GMLP_W), (C_AQ, ATTN_W), (C_AK, ATTN_W), (C_AV, ATTN_W), (C_AG, ATTN_W),
              (C_MQ, MEM_W), (C_MG, MEM_W))

    def body(x_ref, dy_ref, g_ref, wt_hbm, p0, p1, p2, p3, p4, p5, p6, gx_ref, gwt_hbm, gg_ref, wt_v, acc, wt_sem, out_sems):
        i = pl.program_id(0)
        wt_load = pltpu.make_async_copy(wt_hbm, wt_v, wt_sem)

        @pl.when(i == 0)
        def _():
            wt_load.start()
            acc[...] = jnp.zeros_like(acc)
            gg_ref[...] = jnp.zeros_like(gg_ref)

        xv = x_ref[...]
        r = lax.rsqrt(jnp.mean(xv * xv, axis=-1, keepdims=True) + EPS)
        z = xv * r
        g = g_ref[...]
        h = (z * g).astype(BF16)
        pl.when(i == 0)(wt_load.wait)
        flush = [pltpu.make_async_copy(acc.at[c0:c0 + w, :], gwt_hbm.at[c0:c0 + w, :], out_sems.at[n])
                 for n, (c0, w) in enumerate(pieces)]
        dh = jnp.zeros((tm, D_MODEL), F32)
        for n, (pref, (c0, w)) in enumerate(zip((p0, p1, p2, p3, p4, p5, p6), pieces)):
            dp = pref[...]
            dh = dh + _dot(dp, wt_v[c0:c0 + w, :])
            acc[c0:c0 + w, :] += _dot_tn(dp, h)
            pl.when(i == nsteps - 1)(flush[n].start)
        _put_rows(gg_ref, jnp.sum(dh * z, axis=0, keepdims=True), accumulate=True)
        dz = dh * g
        gx_ref[...] = dy_ref[...] + r * (dz - z * jnp.mean(dz * z, axis=-1, keepdims=True))

        @pl.when(i == nsteps - 1)
        def _():
            for cp in flush:
                cp.wait()

    row = lambda w: pl.BlockSpec((tm, w), lambda i: (i, 0))
    hbm = pl.BlockSpec(memory_space=pl.ANY)
    vec = pl.BlockSpec((1, D_MODEL), lambda i: (0, 0))
    return _call(
        body, name="proj_bwd", grid=(nsteps,),
        in_specs=[row(D_MODEL), row(D_MODEL), vec, hbm] + [row(w) for _, w in pieces],
        out_specs=[row(D_MODEL), hbm, pl.BlockSpec((8, LANES), lambda i: (0, 0))],
        out_shape=[jax.ShapeDtypeStruct((SEQ, D_MODEL), F32), jax.ShapeDtypeStruct((IN_W, D_MODEL), F32),
                   jax.ShapeDtypeStruct((8, LANES), F32)],
        scratch_shapes=[pltpu.VMEM((IN_W, D_MODEL), BF16), pltpu.VMEM((IN_W, D_MODEL), F32), pltpu.SemaphoreType.DMA,
                        pltpu.SemaphoreType.DMA((len(pieces),))],
        compiler_params=_params(),
    )(x, dy, gain, wt, dg, daq, dak, dav, dag, dmq, dmg)


AG_SEMS = 8


def _gather_stages(ins, lands, send_sems, recv_sems):
    n = len(ins)
    nrows = [a.shape[0] for a in ins]
    x, y, c = lax.axis_index("x"), lax.axis_index("y"), lax.axis_index("c")
    sib, xn, yn = (x, y, 1 - c), (1 - x, y, c), (x, 1 - y, c)
    me, cx, cy, cd = 2 * x + y, 2 * (1 - x) + y, 2 * x + (1 - y), 2 * (1 - x) + (1 - y)

    def part(a, chip, hf, quarter=None):
        rows = nrows[a] // 2
        base = chip * nrows[a] + hf * rows
        if quarter is not None:
            rows = rows // 2
            base = base + quarter * rows
        return lands[a].at[pl.ds(pl.multiple_of(base, 16), rows), :]

    def copy(a, j, ref, to):
        k = AG_SEMS * a + j
        return pltpu.make_async_remote_copy(src_ref=ref, dst_ref=ref, send_sem=send_sems.at[k],
                                            recv_sem=recv_sems.at[k], device_id=to, device_id_type=MESH)

    def own(a):
        return [copy(a, 0, part(a, me, c), xn), copy(a, 1, part(a, me, c), yn)]

    def neighbours(a):
        return [copy(a, 4, part(a, cx, c, 1), yn), copy(a, 2, part(a, cx, c), sib),
                copy(a, 5, part(a, cy, c, 0), xn), copy(a, 3, part(a, cy, c), sib)]

    def diagonal(a):
        return [copy(a, 7, part(a, cd, c, 1), sib), copy(a, 6, part(a, cd, c, 0), sib)]

    def send_own():
        for a in range(n):
            lands[a][pl.ds(pl.multiple_of(me * nrows[a], 16), nrows[a]), :] = ins[a][...].astype(BF16)
            for cp in own(a):
                cp.start()

    def pass_on_neighbours():
        for a in range(n):
            copy(a, 0, part(a, cx, c), xn).wait_recv()
            copy(a, 1, part(a, cy, c), yn).wait_recv()
            for cp in neighbours(a):
                cp.start()

    def pass_on_diagonal():
        for a in range(n):
            copy(a, 4, part(a, cd, c, 1), yn).wait_recv()
            copy(a, 5, part(a, cd, c, 0), xn).wait_recv()
            for cp in diagonal(a):
                cp.start()

    def y_complete():
        for a in range(n):
            copy(a, 3, part(a, cy, 1 - c), sib).wait_recv()

    def x_complete():
        for a in range(n):
            copy(a, 2, part(a, cx, 1 - c), sib).wait_recv()

    def diagonal_complete():
        for a in range(n):
            copy(a, 6, part(a, cd, 1 - c, 0), sib).wait_recv()
            copy(a, 7, part(a, cd, 1 - c, 1), sib).wait_recv()

    def sends_done():
        for a in range(n):
            for cp in own(a) + neighbours(a) + diagonal(a):
                cp.wait_send()

    def finish():
        y_complete()
        x_complete()
        diagonal_complete()
        sends_done()

    return (send_own, pass_on_neighbours, pass_on_diagonal, finish), (y_complete, x_complete, diagonal_complete, sends_done)


RS_SEMS = 6
RS_KINDS = (((2, 2), 1, F32), ((2, 2), 1, F32), ((2, 2), 2, BF16), ((2, 2), 2, BF16), ((2, 2), 2, F32),
            ((2,), 2, BF16), ((2,), 2, BF16), ((2,), 1, F32))


def _rs_view(g):
    return g.reshape(2, 2, 2, g.shape[0] // 8, g.shape[1])


def _rs_scratch(shapes):
    return [pltpu.VMEM(lead + (r // 8, w // split), dt) for lead, split, dt in RS_KINDS for r, w in shapes]


def _rs_stages(gs, outs, bufs, send_sems, recv_sems, local_sems, widths):
    n = len(gs)
    loc, ra, s_b, r_b, acc1, s_c, r_c, fin = (bufs[n * i:n * i + n] for i in range(len(RS_KINDS)))
    half_w = [w // 2 for w in widths]
    x, y, c = lax.axis_index("x"), lax.axis_index("y"), lax.axis_index("c")
    sib, xn, yn = (x, y, 1 - c), (1 - x, y, c), (x, 1 - y, c)

    def copy(a, j, src, dst, to):
        k = RS_SEMS * a + j
        return pltpu.make_async_remote_copy(src_ref=src, dst_ref=dst, send_sem=send_sems.at[k],
                                            recv_sem=recv_sems.at[k], device_id=to, device_id_type=MESH)

    def step_a(a):
        return [copy(a, 0, gs[a].at[:, :, 1 - c], ra[a], sib),
                pltpu.make_async_copy(gs[a].at[:, :, c], loc[a], local_sems.at[a])]

    def step_b(a):
        return copy(a, 1, s_b[a].at[0], r_b[a].at[0], xn), copy(a, 2, s_b[a].at[1], r_b[a].at[1], yn)

    def step_c(a):
        return copy(a, 3, s_c[a].at[0], r_c[a].at[0], yn), copy(a, 4, s_c[a].at[1], r_c[a].at[1], xn)

    def step_d(a, half):
        rows = fin[a].at[half]
        return copy(a, 5, rows, rows, sib)

    def start():
        for a in range(n):
            for cp in step_a(a):
                cp.start()

    def a_to_b():
        for a in range(n):
            for cp in step_a(a):
                cp.wait()
            ra[a][...] = loc[a][...] + ra[a][...]
            s_b[a][0] = ra[a][1 - x, :, :, :half_w[a]].astype(BF16)
            s_b[a][1] = ra[a][:, 1 - y, :, half_w[a]:].astype(BF16)
            for cp in step_b(a):
                cp.start()

    def b_to_c():
        for a in range(n):
            for cp in step_b(a):
                cp.wait()
            acc1[a][0] = ra[a][x, :, :, :half_w[a]] + r_b[a][0].astype(F32)
            acc1[a][1] = ra[a][:, y, :, half_w[a]:] + r_b[a][1].astype(F32)
            s_c[a][0] = acc1[a][0, 1 - y].astype(BF16)
            s_c[a][1] = acc1[a][1, 1 - x].astype(BF16)
            for cp in step_c(a):
                cp.start()

    def c_to_d():
        for a in range(n):
            for cp in step_c(a):
                cp.wait()
            fin[a][c, :, :half_w[a]] = acc1[a][0, y] + r_c[a][0].astype(F32)
            fin[a][c, :, half_w[a]:] = acc1[a][1, x] + r_c[a][1].astype(F32)
            step_d(a, c).start()

    def finish():
        to_hbm = [pltpu.make_async_copy(fin[a], outs[a], local_sems.at[a]) for a in range(n)]
        for a in range(n):
            step_d(a, 1 - c).wait_recv()
            step_d(a, c).wait_send()
            to_hbm[a].start()
        for cp in to_hbm:
            cp.wait()

    return start, a_to_b, b_to_c, c_to_d, finish


def _reduce_grads(gwt, g_ws, tiny):
    cw = gwt.shape[1] // RS_CHUNKS
    chunk_shape = (gwt.shape[0], cw)

    def body(g0, ws_in, tiny_in, *rest):
        outs, o_ws, o_tiny = rest[:RS_CHUNKS], rest[RS_CHUNKS], rest[RS_CHUNKS + 1]
        rest = rest[RS_CHUNKS + 2:]
        nb = len(RS_KINDS) * RS_CHUNKS
        sm, sa, sb, sc, acc_s, send_sems, recv_sems, local_sems = rest[nb:]
        blocks = [g0.at[:, :, :, :, pl.ds(j * cw, cw)] for j in range(RS_CHUNKS)]
        start, a_to_b, b_to_c, c_to_d, finish = _rs_stages(blocks, outs, rest[:nb], send_sems, recv_sems, local_sems,
                                                           [cw] * RS_CHUNKS)
        n_ws = ws_in.shape[0]
        sm[0:n_ws, :] = ws_in[...]
        sm[n_ws:, :] = tiny_in[...]
        x, y, c = lax.axis_index("x"), lax.axis_index("y"), lax.axis_index("c")

        def small(j, src, dst, to):
            k = RS_SEMS * RS_CHUNKS + j
            return pltpu.make_async_remote_copy(src_ref=src, dst_ref=dst, send_sem=send_sems.at[k],
                                                recv_sem=recv_sems.at[k], device_id=to, device_id_type=MESH)

        along_c, along_x, along_y = (small(0, sm, sa, (x, y, 1 - c)), small(1, acc_s, sb, (1 - x, y, c)),
                                     small(2, sb, sc, (x, 1 - y, c)))
        start()
        along_c.start()
        a_to_b()
        along_c.wait()
        acc_s[...] = sm[...] + sa[...]
        along_x.start()
        b_to_c()
        along_x.wait()
        sb[...] = acc_s[...] + sb[...]
        along_y.start()
        c_to_d()
        along_y.wait()
        o_ws[...] = sb[0:n_ws, :] + sc[0:n_ws, :]
        o_tiny[...] = sb[n_ws:, :] + sc[n_ws:, :]
        finish()

    vm = pl.BlockSpec(memory_space=pltpu.VMEM)
    hbm = pl.BlockSpec(memory_space=pl.ANY)
    small_shape = (g_ws.shape[0] + tiny.shape[0], LANES)
    scratch = _rs_scratch([chunk_shape] * RS_CHUNKS) + [pltpu.VMEM(small_shape, F32) for _ in range(5)]
    nsem = RS_SEMS * RS_CHUNKS + 3
    scratch += [pltpu.SemaphoreType.DMA((nsem,)), pltpu.SemaphoreType.DMA((nsem,)), pltpu.SemaphoreType.DMA((RS_CHUNKS,))]
    return _call(
        body, name="reduce_grads",
        out_shape=[jax.ShapeDtypeStruct((2, gwt.shape[0] // 8, cw), F32)] * RS_CHUNKS
        + [jax.ShapeDtypeStruct(g_ws.shape, F32), jax.ShapeDtypeStruct(tiny.shape, F32)],
        in_specs=[hbm, vm, vm],
        out_specs=[hbm] * RS_CHUNKS + [vm, vm],
        scratch_shapes=scratch,
        compiler_params=_params(),
    )(_rs_view(gwt), g_ws, tiny)


def _adam_update(w, g, m, v):
    nm = ADAM_B1 * m + (1.0 - ADAM_B1) * g
    nv = ADAM_B2 * v + (1.0 - ADAM_B2) * (g * g)
    m_hat = nm / (1.0 - ADAM_B1 ** ADAM_STEP)
    v_hat = nv / (1.0 - ADAM_B2 ** ADAM_STEP)
    return -ADAM_LR * (m_hat / (jnp.sqrt(v_hat) + ADAM_EPS) + ADAM_WD * w), nm, nv


def _adamw(w, g, m, v):
    rows, cols = w.shape
    tm = max(t for t in range(8, 257, 8) if rows % t == 0)
    parts = tuple(g) if isinstance(g, (tuple, list)) else (g,)
    n = len(parts)

    def body(w_ref, m_ref, v_ref, *refs):
        gv = jnp.concatenate([r[...] for r in refs[:n]], axis=1)
        d_ref, nm_ref, nv_ref = refs[n:n + 3]
        d_ref[...], nm_ref[...], nv_ref[...] = _adam_update(w_ref[...], gv, m_ref[...], v_ref[...])
        if n > 1:
            refs[n + 3][...] = gv

    blk = pl.BlockSpec((tm, cols), lambda i: (i, 0))
    nout = 3 if n == 1 else 4
    res = _call(
        body, name="adamw", grid=(rows // tm,),
        in_specs=[blk] * 3 + [pl.BlockSpec((tm, p.shape[1]), lambda i: (i, 0)) for p in parts], out_specs=[blk] * nout,
        out_shape=[jax.ShapeDtypeStruct((rows, cols), F32)] * nout,
        compiler_params=_params(),
    )(w, m, v, *parts)
    return (parts[0] if n == 1 else res[3], *res[:3])


def _adamw_tiny(tiny, weights, ms, vs):
    shapes = [w.shape for w in weights]
    n = len(weights)

    def grad_of(t_ref, k, shape):
        base = 8 * k
        if shape[1] > LANES:
            return [t_ref[base + j:base + j + 1, :] for j in range(shape[1] // LANES)]
        return [t_ref[base:base + shape[0], 0:shape[1]]]

    def body(t_ref, *refs):
        w_refs, m_refs, v_refs = refs[:n], refs[n:2 * n], refs[2 * n:3 * n]
        loss_ref, outs = refs[3 * n], refs[3 * n + 1:]
        loss_ref[...] = (0.5 / D_MODEL) * jnp.sum(t_ref[8 * n:8 * n + 8, :], keepdims=True)
        for k, shape in enumerate(shapes):
            g_ref, d_ref, nm_ref, nv_ref = outs[4 * k:4 * k + 4]
            for j, g in enumerate(grad_of(t_ref, k, shape)):
                cols = slice(j * LANES, (j + 1) * LANES) if shape[1] > LANES else slice(None)
                g_ref[:, cols] = g
                d_ref[:, cols], nm_ref[:, cols], nv_ref[:, cols] = _adam_update(
                    w_refs[k][:, cols], g, m_refs[k][:, cols], v_refs[k][:, cols])

    out_shape = [jax.ShapeDtypeStruct((1, 1), F32)]
    for shape in shapes:
        out_shape += [jax.ShapeDtypeStruct(shape, F32)] * 4
    return _call(body, name="adamw_tiny", out_shape=out_shape, compiler_params=_params())(tiny, *weights, *ms, *vs)


def _local_grads(x, mem, tgt, norm_gain, wt_sh, gmlp_v_gain, gmlp_w_s, gmlp_b, attn_q_gain, attn_k_gain,
                 mem_norm_gain, wkv_sh, mem_q_gain, mem_k_gain, wo_sh):
    vg = gmlp_v_gain.reshape(1, GMLP_W)
    bias_full = jnp.repeat(gmlp_b.T, HEAD_DIM, axis=1)
    gq2, gk2 = jnp.tile(attn_q_gain, (1, 2)), jnp.tile(attn_k_gain, (1, 2))
    qg4, kg4 = jnp.tile(mem_q_gain, (1, 4)), jnp.tile(mem_k_gain, (1, 4))

    proj, wt = _gather_proj(x, norm_gain, wt_sh)
    yg = _gmlp_fwd(proj, vg, gmlp_w_s, bias_full)
    o, lse, ya, wkv, wo = _attn_fwd(proj, gq2, gk2, wkv_sh, wo_sh)
    kv, hm = _mem_kv(mem, mem_norm_gain, wkv)
    om, ym = _mem_fwd(proj, kv, qg4, kg4)
    dy, dyc, g_wo, err2 = _out_loss(yg, ya, ym, x, tgt, wo)
    dmq, dmg, g_mq, g_mk, g_wkv, g_mng = _mem_bwd(proj, om, dyc, kv, hm, mem, mem_norm_gain, wkv, qg4, kg4)
    daq, dak, dav, dag, g_aq, g_ak, g_wkv_sh, g_wo_sh = _attn_bwd(proj, o, lse, dyc, gq2, gk2, g_wkv, g_wo)
    dg, g_ws, g_b, g_vg = _gmlp_bwd(proj, dyc, vg, gmlp_w_s, bias_full)
    gx, g_wt, g_ng = _proj_bwd(x, dy, norm_gain, wt, dg, daq, dak, dav, dag, dmq, dmg)

    tiny = jnp.concatenate([g_ng, g_vg, g_b, g_aq, g_ak, g_mng, g_mq, g_mk, err2], axis=0)
    return gx, g_wt, g_wkv_sh, g_wo_sh, g_ws.reshape(4 * CHUNK, CHUNK), tiny


def kernel(x, mem, norm_gain, w_in, gmlp_v_gain, gmlp_w_s, gmlp_b, attn_q_gain, attn_k_gain, mem_norm_gain, w_mem_kv, mem_q_gain, mem_k_gain, w_out, loss_target, m_norm_gain, m_w_in, m_gmlp_v_gain, m_gmlp_w_s, m_gmlp_b, m_attn_q_gain, m_attn_k_gain, m_mem_norm_gain, m_w_mem_kv, m_mem_q_gain, m_mem_k_gain, m_w_out, v_norm_gain, v_w_in, v_gmlp_v_gain, v_gmlp_w_s, v_gmlp_b, v_attn_q_gain, v_attn_k_gain, v_mem_norm_gain, v_w_mem_kv, v_mem_q_gain, v_mem_k_gain, v_w_out):
    gx, g_wt, g_wkv_sh, g_wo_sh, g_ws, tiny = _local_grads(
        x[0], mem[0], loss_target[0], norm_gain, w_in[0].T, gmlp_v_gain[0], gmlp_w_s[0], gmlp_b[0],
        attn_q_gain, attn_k_gain, mem_norm_gain, w_mem_kv[0], mem_q_gain, mem_k_gain, w_out[0])
    *g_wt_sh, g_ws, tiny = _reduce_grads(g_wt, g_ws, tiny)
    chip_block = lambda g: g.reshape(2 * g.shape[1], g.shape[2])
    g_wt_sh = tuple(chip_block(g) for g in g_wt_sh)
    g_wkv_sh, g_wo_sh = chip_block(g_wkv_sh), chip_block(g_wo_sh)

    ws = (norm_gain, w_in, gmlp_v_gain, gmlp_w_s, gmlp_b, attn_q_gain, attn_k_gain, mem_norm_gain, w_mem_kv,
          mem_q_gain, mem_k_gain, w_out)
    ms = (m_norm_gain, m_w_in, m_gmlp_v_gain, m_gmlp_w_s, m_gmlp_b, m_attn_q_gain, m_attn_k_gain, m_mem_norm_gain,
          m_w_mem_kv, m_mem_q_gain, m_mem_k_gain, m_w_out)
    vs = (v_norm_gain, v_w_in, v_gmlp_v_gain, v_gmlp_w_s, v_gmlp_b, v_attn_q_gain, v_attn_k_gain, v_mem_norm_gain,
          v_w_mem_kv, v_mem_q_gain, v_mem_k_gain, v_w_out)
    form = {1: lambda a: a[0].T, 3: lambda a: a.reshape(4 * CHUNK, CHUNK), 2: lambda a: a[0], 4: lambda a: a[0],
            8: lambda a: a[0], 11: lambda a: a[0]}
    back = {1: lambda a: a.T[None], 3: lambda a: a.reshape(1, 4, CHUNK, CHUNK), 2: lambda a: a[None],
            4: lambda a: a[None], 8: lambda a: a[None], 11: lambda a: a[None]}
    fwd = lambda t, i: form.get(i, lambda a: a)(t[i])
    out = {}
    for i, g in ((1, g_wt_sh), (3, g_ws), (8, g_wkv_sh), (11, g_wo_sh)):
        out[i] = _adamw(fwd(ws, i), g, fwd(ms, i), fwd(vs, i))
    res = _adamw_tiny(tiny, [fwd(ws, i) for i in TINY_ORDER], [fwd(ms, i) for i in TINY_ORDER],
                      [fwd(vs, i) for i in TINY_ORDER])
    for k, i in enumerate(TINY_ORDER):
        out[i] = res[1 + 4 * k:5 + 4 * k]
    leaves = [[back.get(i, lambda a: a)(out[i][j]) for i in range(12)] for j in range(4)]
    return (res[0].reshape(()), gx[None], *leaves[0], *leaves[1], *leaves[2], *leaves[3])
```

```python
import math

import jax
import jax.numpy as jnp
from jax import lax
from jax.experimental import pallas as pl
from jax.experimental.pallas import tpu as pltpu

F32 = jnp.float32
BF16 = jnp.bfloat16

SEQ = 4096
D_MODEL = 1024
HEAD_DIM = 64
LANES = 128
CHUNK = 128
GMLP_W, ATTN_W, MEM_W = 256, 512, 256
IN_W = 3 * GMLP_W + 4 * ATTN_W + 2 * MEM_W
MEM_LEN = 256
DILATIONS = (16, 4, 1)
EPS = 1e-6
QK_SCALE = 1.0 / math.sqrt(HEAD_DIM)
C_GU, C_GV, C_GG, C_AQ, C_AK, C_AV, C_AG, C_MQ, C_MG = 0, 256, 512, 768, 1280, 1792, 2304, 2816, 3072

ADAM_LR, ADAM_B1, ADAM_B2, ADAM_EPS, ADAM_WD, ADAM_STEP = 0.001, 0.9, 0.999, 1e-08, 0.01, 10

VMEM_LIMIT = 48 * 1024 * 1024
RS_CHUNKS = 8
ATTN_UNROLL = 4
MESH = pl.DeviceIdType.MESH

TINY_ORDER = (0, 2, 4, 5, 6, 7, 9, 10)


def _call(body, **kw):
    return pl.pallas_call(body, **kw)


def _params(**kw):
    return pltpu.CompilerParams(vmem_limit_bytes=VMEM_LIMIT, **kw)


def _dot(a, b):
    return jnp.dot(a, b, preferred_element_type=F32)


def _dot_nt(a, b):
    return lax.dot_general(a, b, (((1,), (1,)), ((), ())), preferred_element_type=F32)


def _dot_tn(a, b):
    return lax.dot_general(a, b, (((0,), (0,)), ((), ())), preferred_element_type=F32)


def _head_blockdiag():
    r = lax.shift_right_logical(lax.broadcasted_iota(jnp.int32, (LANES, LANES), 0), 6)
    c = lax.shift_right_logical(lax.broadcasted_iota(jnp.int32, (LANES, LANES), 1), 6)
    return jnp.where(r == c, 1.0, 0.0).astype(BF16)


def _headsum(v, bd):
    hi = v.astype(BF16)
    lo = (v - hi.astype(F32)).astype(BF16)
    return _dot(hi, bd) + _dot(lo, bd)


def _lo_mask(rows):
    return lax.broadcasted_iota(jnp.int32, (rows, LANES), 1) < HEAD_DIM


def _sigmoid(x):
    return 1.0 / (1.0 + jnp.exp(-x))


def _fold_heads(v):
    return v + pltpu.roll(v, HEAD_DIM, 1)


def _put_rows(ref, vec, accumulate=False):
    for j in range(vec.shape[1] // LANES):
        piece = vec[:, j * LANES:(j + 1) * LANES]
        ref[j:j + 1, :] = ref[j:j + 1, :] + piece if accumulate else piece


def _gather_proj(x, gain, wt_sh):
    tm = 1024
    nrow = SEQ // tm
    widths = (768, 896, 768, 896)
    nunits = len(widths)
    pair = 2 * wt_sh.shape[0]
    assert pair % LANES == 0 and sum(widths[:2]) == pair

    def body(x_ref, g_ref, wt_sh_ref, proj_hbm, wt_hbm, h_scr, land, res, send_sems, recv_sems, out_sems, copy_sem):
        u, i = pl.program_id(0), pl.program_id(1)
        cx_, cy_ = lax.axis_index("x"), lax.axis_index("y")
        (send_own, pass_on_neighbours, pass_on_diagonal, _), (y_complete, x_complete, diagonal_complete, sends_done) = (
            _gather_stages((wt_sh_ref,), (land,), send_sems, recv_sems))
        first = lambda k: (u == k) & (i == 0)
        last = (u == nunits - 1) & (i == nrow - 1)
        to_hbm = pltpu.make_async_copy(land, wt_hbm, copy_sem)

        pl.when(first(0))(send_own)

        @pl.when(u == 0)
        def _():
            xv = x_ref[...]
            ms = jnp.mean(xv * xv, axis=-1, keepdims=True)
            h_scr[pl.ds(pl.multiple_of(i * tm, tm), tm), :] = (xv * lax.rsqrt(ms + EPS) * g_ref[...]).astype(BF16)

        @pl.when(first(1))
        def _():
            pass_on_neighbours()
            y_complete()

        @pl.when(first(2))
        def _():
            x_complete()
            pass_on_diagonal()

        @pl.when(first(3))
        def _():
            diagonal_complete()
            to_hbm.start()

        mine, other = pair * cx_, pair * (1 - cx_)
        col0 = (mine + 896 * cy_, mine + 768 * (1 - cy_), other + 896 * cy_, other + 768 * (1 - cy_))
        slot = i % 2
        rows = pl.ds(pl.multiple_of(i * tm, tm), tm)

        def writeback(k, rows_):
            c0 = pl.multiple_of(col0[k], LANES)
            return pltpu.make_async_copy(res.at[slot, :, pl.ds(0, widths[k])], proj_hbm.at[rows_, pl.ds(c0, widths[k])],
                                         out_sems.at[slot])

        for k in range(nunits):
            @pl.when(u == k)
            def _(k=k):
                pl.when(i >= 2)(writeback(k, rows).wait)
                if k > 0:
                    pl.when(i < 2)(writeback(k - 1, rows).wait)
                w_rows = land[pl.ds(pl.multiple_of(col0[k], LANES), widths[k]), :]
                res[slot, :, 0:widths[k]] = _dot_nt(h_scr[rows, :], w_rows)
                writeback(k, rows).start()

        @pl.when(last)
        def _():
            sends_done()
            to_hbm.wait()
            for s in range(2):
                pltpu.make_async_copy(res.at[s, :, pl.ds(0, widths[-1])], proj_hbm.at[rows, pl.ds(0, widths[-1])], out_sems.at[s]).wait()

    full = jax.ShapeDtypeStruct((4 * wt_sh.shape[0], wt_sh.shape[1]), BF16)
    hbm = pl.BlockSpec(memory_space=pl.ANY)
    return _call(
        body, name="gather_proj", grid=(nunits, nrow),
        in_specs=[pl.BlockSpec((tm, D_MODEL), lambda u, i: (jnp.where(u == 0, i, nrow - 1), 0)),
                  pl.BlockSpec((1, D_MODEL), lambda u, i: (0, 0)), pl.BlockSpec(wt_sh.shape, lambda u, i: (0, 0))],
        out_specs=[hbm, hbm],
        out_shape=[jax.ShapeDtypeStruct((SEQ, IN_W), F32), full],
        scratch_shapes=[pltpu.VMEM((SEQ, D_MODEL), BF16), pltpu.VMEM(full.shape, BF16), pltpu.VMEM((2, tm, max(widths)), F32),
                        pltpu.SemaphoreType.DMA((AG_SEMS,)), pltpu.SemaphoreType.DMA((AG_SEMS,)),
                        pltpu.SemaphoreType.DMA((2,)), pltpu.SemaphoreType.DMA],
        compiler_params=_params(),
    )(x, gain, wt_sh)


def _gmlp_weights(w_ref):
    ti = lax.broadcasted_iota(jnp.int32, (CHUNK, CHUNK), 0)
    si = lax.broadcasted_iota(jnp.int32, (CHUNK, CHUNK), 1)
    tril = si <= ti
    return tril, [jnp.where(tril, w_ref[h], 0.0).astype(BF16) for h in range(4)]


def _gmlp_fwd(proj, vgain, w_s, bias_full):
    tm = 1024

    def body(p_ref, vg_ref, w_ref, b_ref, y_ref):
        bd = _head_blockdiag()
        lo = _lo_mask(CHUNK)
        _, wm = _gmlp_weights(w_ref)
        units = [(pl.ds(c * CHUNK, CHUNK), p) for c in range(tm // CHUNK) for p in range(2)]
        col = lambda c0, p: slice(c0 + p * LANES, c0 + (p + 1) * LANES)
        vs = [p_ref[rows, col(C_GV, p)] for rows, p in units]
        rs = [lax.rsqrt(_headsum(v * v, bd) * (1.0 / HEAD_DIM) + EPS) for v in vs]
        vns = [(v * r * vg_ref[:, col(0, p)]).astype(BF16) for v, r, (_, p) in zip(vs, rs, units)]
        sps = [jnp.where(lo, _dot(wm[2 * p], vn), _dot(wm[2 * p + 1], vn)) + b_ref[:, col(0, p)] for vn, (_, p) in zip(vns, units)]
        for sp, (rows, p) in zip(sps, units):
            gt = p_ref[rows, col(C_GG, p)]
            y_ref[rows, col(0, p)] = (p_ref[rows, col(C_GU, p)] * sp * (gt * _sigmoid(gt))).astype(BF16)

    return _call(
        body, name="gmlp_fwd", grid=(SEQ // tm,),
        in_specs=[pl.BlockSpec((tm, 3 * GMLP_W), lambda i: (i, 0)),
                  pl.BlockSpec((1, GMLP_W), lambda i: (0, 0)),
                  pl.BlockSpec((4, CHUNK, CHUNK), lambda i: (0, 0, 0)),
                  pl.BlockSpec((CHUNK, GMLP_W), lambda i: (0, 0))],
        out_specs=pl.BlockSpec((tm, GMLP_W), lambda i: (i, 0)),
        out_shape=jax.ShapeDtypeStruct((SEQ, GMLP_W), BF16),
        compiler_params=_params(),
    )(proj, vgain, w_s, bias_full)


def _gmlp_bwd(proj, dyc, vgain, w_s, bias_full):
    tm = 1024
    nsteps = SEQ // tm

    def body(p_ref, dy_ref, vg_ref, w_ref, b_ref, dg_ref, gw_ref, gb_ref, gv_ref):
        i = pl.program_id(0)
        bd = _head_blockdiag()
        lo = _lo_mask(CHUNK)
        tril, wm = _gmlp_weights(w_ref)
        ri = lax.broadcasted_iota(jnp.int32, (16, LANES), 0)
        li = lax.broadcasted_iota(jnp.int32, (16, LANES), 1)
        head_rows = [jnp.where(((ri == 2 * p) & (li < HEAD_DIM)) | ((ri == 2 * p + 1) & (li >= HEAD_DIM)), 1.0, 0.0).astype(BF16)
                     for p in range(2)]

        @pl.when(i == 0)
        def _():
            gw_ref[...] = jnp.zeros_like(gw_ref)
            gb_ref[...] = jnp.zeros_like(gb_ref)
            gv_ref[...] = jnp.zeros_like(gv_ref)

        units = [(pl.ds(c * CHUNK, CHUNK), p) for c in range(tm // CHUNK) for p in range(2)]
        col = lambda c0, p: slice(c0 + p * LANES, c0 + (p + 1) * LANES)
        vs = [p_ref[rows, col(C_GV, p)] for rows, p in units]
        rs = [lax.rsqrt(_headsum(v * v, bd) * (1.0 / HEAD_DIM) + EPS) for v in vs]
        zs = [v * r for v, r in zip(vs, rs)]
        vns = [(z * vg_ref[:, col(0, p)]).astype(BF16) for z, (_, p) in zip(zs, units)]
        sps = [jnp.where(lo, _dot(wm[2 * p], vn), _dot(wm[2 * p + 1], vn)) + b_ref[:, col(0, p)] for vn, (_, p) in zip(vns, units)]
        dsps = []
        for sp, (rows, p) in zip(sps, units):
            u = p_ref[rows, col(C_GU, p)]
            gt = p_ref[rows, col(C_GG, p)]
            dy = dy_ref[rows, col(0, p)]
            sg = _sigmoid(gt)
            sl = gt * sg
            dg_ref[rows, col(C_GU, p)] = (dy * sp * sl).astype(BF16)
            dg_ref[rows, col(C_GG, p)] = (dy * u * sp * (sg * (1.0 + gt * (1.0 - sg)))).astype(BF16)
            dsps.append(dy * u * sl)
        dspbs = [dsp.astype(BF16) for dsp in dsps]
        dvns = [jnp.where(lo, _dot_tn(wm[2 * p], dspb), _dot_tn(wm[2 * p + 1], dspb)) for dspb, (_, p) in zip(dspbs, units)]
        gws = [(_dot_nt(jnp.where(lo, dsp, 0.0).astype(BF16), vn), _dot_nt(jnp.where(lo, 0.0, dsp).astype(BF16), vn))
               for dsp, vn in zip(dsps, vns)]
        gbs = [(_dot_nt(head_rows[p], dspb) + _dot_nt(head_rows[p], (dsp - dspb.astype(F32)).astype(BF16)))[0:8]
               for dsp, dspb, (_, p) in zip(dsps, dspbs, units)]
        for p in range(2):
            mine = [n for n, (_, q) in enumerate(units) if q == p]
            gw_ref[2 * p] += sum(gws[n][0] for n in mine)
            gw_ref[2 * p + 1] += sum(gws[n][1] for n in mine)
            gvp = sum(jnp.sum(dvns[n] * zs[n], axis=0, keepdims=True) for n in mine)
            gv_ref[2 * p:2 * p + 1, :] += gvp
            gv_ref[2 * p + 1:2 * p + 2, :] += pltpu.roll(gvp, HEAD_DIM, 1)
        gb_ref[...] += sum(gbs)
        for dvn, z, r, (rows, p) in zip(dvns, zs, rs, units):
            dz = dvn * vg_ref[:, col(0, p)]
            dg_ref[rows, col(C_GV, p)] = (r * (dz - z * (_headsum(dz * z, bd) * (1.0 / HEAD_DIM)))).astype(BF16)

        @pl.when(i == nsteps - 1)
        def _():
            for h in range(4):
                gw_ref[h] = jnp.where(tril, gw_ref[h], 0.0)

    return _call(
        body, name="gmlp_bwd", grid=(nsteps,),
        in_specs=[pl.BlockSpec((tm, 3 * GMLP_W), lambda i: (i, 0)),
                  pl.BlockSpec((tm, GMLP_W), lambda i: (i, 0)),
                  pl.BlockSpec((1, GMLP_W), lambda i: (0, 0)),
                  pl.BlockSpec((4, CHUNK, CHUNK), lambda i: (0, 0, 0)),
                  pl.BlockSpec((CHUNK, GMLP_W), lambda i: (0, 0))],
        out_specs=[pl.BlockSpec((tm, 3 * GMLP_W), lambda i: (i, 0)),
                   pl.BlockSpec((4, CHUNK, CHUNK), lambda i: (0, 0, 0)),
                   pl.BlockSpec((8, LANES), lambda i: (0, 0)),
                   pl.BlockSpec((8, LANES), lambda i: (0, 0))],
        out_shape=[jax.ShapeDtypeStruct((SEQ, 3 * GMLP_W), BF16),
                   jax.ShapeDtypeStruct((4, CHUNK, CHUNK), F32),
                   jax.ShapeDtypeStruct((8, LANES), F32),
                   jax.ShapeDtypeStruct((8, LANES), F32)],
        compiler_params=_params(),
    )(proj, dyc, vgain, w_s, bias_full)


def _band_masks():
    qi = lax.broadcasted_iota(jnp.int32, (CHUNK, 2 * CHUNK), 0)
    kj = lax.broadcasted_iota(jnp.int32, (CHUNK, 2 * CHUNK), 1)
    valid2 = ((kj < CHUNK) & (kj >= qi)) | ((kj >= CHUNK) & (kj - CHUNK <= qi))
    q1 = lax.broadcasted_iota(jnp.int32, (CHUNK, CHUNK), 0)
    k1 = lax.broadcasted_iota(jnp.int32, (CHUNK, CHUNK), 1)
    return k1 <= q1, valid2


def _stack_heads(v, lo):
    return jnp.concatenate([jnp.where(lo, v, 0.0), jnp.where(lo, 0.0, v)], axis=0).astype(BF16)


def _rows_of(ref, start, d):
    if d == 1:
        return ref.at[pl.ds(start if isinstance(start, int) else pl.multiple_of(start, CHUNK), CHUNK), :]
    return ref.at[pl.ds(start, CHUNK, stride=d), :]


def _unrolled(lo, hi, unroll, run):
    groups = (hi - lo) // unroll
    if groups:
        def body(g, carry):
            run([lo + g * unroll + t for t in range(unroll)])
            return carry

        lax.fori_loop(0, groups, body, 0)
    if lo + groups * unroll < hi:
        run(range(lo + groups * unroll, hi))


def _for_blocks(d, group_fn, unroll):
    nblk = SEQ // CHUNK
    sh = d.bit_length() - 1

    def first(j):
        return (j * CHUNK if d == 1 else j, None)

    def rest(j):
        start = (j & (d - 1)) + (j >> sh) * (CHUNK * d)
        return (start, start - CHUNK * d)

    _unrolled(0, d, unroll, lambda js: group_fn(d, [first(j) for j in js]))
    _unrolled(d, nblk, unroll, lambda js: group_fn(d, [rest(j) for j in js]))


def _attn_fwd(proj, gq2, gk2, *ride_along):
    tn = 512
    npairs = ATTN_W // LANES
    nride = len(ride_along)

    def body(q_ref, k_ref, v_ref, g_ref, gq_ref, gk_ref, *rest):
        shards, rest = rest[:nride], rest[nride:]
        o_ref, l_ref, ya_ref = rest[:3]
        gathered, rest = rest[3:3 + nride], rest[3 + nride:]
        qn_ref, kn_ref = rest[:2]
        lands, (send_sems, recv_sems, copy_sems) = rest[2:2 + nride], rest[2 + nride:]
        pair = pl.program_id(0)
        ride = _gather_stages(shards, lands, send_sems, recv_sems)[0]
        for step in range(npairs):
            pl.when(pair == step)(ride[step])
        bd = _head_blockdiag()
        lo = _lo_mask(CHUNK)
        valid1, valid2 = _band_masks()

        def norm(t, carry):
            rows = pl.ds(pl.multiple_of(t * tn, tn), tn)
            q, k = q_ref[rows, :], k_ref[rows, :]
            ssq = [_headsum(a * a, bd) for a in (q, k)]
            qn_ref[rows, :] = q * lax.rsqrt(ssq[0] * (1.0 / HEAD_DIM) + EPS) * (gq_ref[...] * QK_SCALE)
            kn_ref[rows, :] = k * lax.rsqrt(ssq[1] * (1.0 / HEAD_DIM) + EPS) * gk_ref[...]
            return carry

        lax.fori_loop(0, SEQ // tn, norm, 0)

        def load_kv(ref, d, start, prev):
            own = _rows_of(ref, start, d)[...]
            if prev is None:
                return own.astype(BF16)
            return jnp.concatenate([_rows_of(ref, prev, d)[...], own], axis=0).astype(BF16)

        def group(d, blocks):
            valid = valid1 if blocks[0][1] is None else valid2
            valid = jnp.concatenate([valid, valid], axis=0)
            qs = [_rows_of(qn_ref, start, d)[...] for start, _ in blocks]
            ks = [load_kv(kn_ref, d, start, prev) for start, prev in blocks]
            vs = [load_kv(v_ref, d, start, prev) for start, prev in blocks]
            ss = [_dot_nt(_stack_heads(q, lo), k) for q, k in zip(qs, ks)]
            ms, ps, ls = [], [], []
            for s in ss:
                s = jnp.where(valid, s, -jnp.inf)
                m = jnp.max(s, axis=-1, keepdims=True)
                p = jnp.exp(s - m)
                ms.append(m)
                ls.append(jnp.sum(p, axis=-1, keepdims=True))
                ps.append(p.astype(BF16))
            os_ = [_dot(p, v) for p, v in zip(ps, vs)]
            for b, (start, _) in enumerate(blocks):
                heads = lambda v: jnp.where(lo, v[:CHUNK], v[CHUNK:])
                lsum = heads(ls[b])
                ob = heads(os_[b]) * (1.0 / lsum)
                lb = heads(ms[b]) + jnp.log(lsum)
                o_rows = _rows_of(o_ref, start, d)
                l_rows = _rows_of(l_ref, start, d)
                if d != DILATIONS[0]:
                    lold = l_rows[...]
                    mx = jnp.maximum(lold, lb)
                    ea = jnp.exp(lold - mx)
                    eb = jnp.exp(lb - mx)
                    inv = 1.0 / (ea + eb)
                    ob = o_rows[...] * (ea * inv) + ob * (eb * inv)
                    lb = mx + jnp.log(ea + eb)
                o_rows[...] = ob
                l_rows[...] = lb

        for d in DILATIONS:
            _for_blocks(d, group, ATTN_UNROLL)

        def fin(t, carry):
            rows = pl.ds(pl.multiple_of(t * tn, tn), tn)
            g = g_ref[rows, :]
            ya_ref[rows, :] = (o_ref[rows, :] * (g * _sigmoid(g))).astype(BF16)
            return carry

        lax.fori_loop(0, SEQ // tn, fin, 0)

        @pl.when(pair == npairs - 1)
        def _():
            to_hbm = [pltpu.make_async_copy(land, out, copy_sems.at[n]) for n, (land, out) in enumerate(zip(lands, gathered))]
            for cp in to_hbm:
                cp.start()
            for cp in to_hbm:
                cp.wait()

    col = lambda c0: pl.BlockSpec((SEQ, LANES), lambda p: (0, c0 // LANES + p))
    vec = pl.BlockSpec((1, LANES), lambda p: (0, 0))
    out = pl.BlockSpec((SEQ, LANES), lambda p: (0, p))
    full = [jax.ShapeDtypeStruct((4 * a.shape[0], a.shape[1]), BF16) for a in ride_along]
    return _call(
        body, name="attn_fwd", grid=(npairs,),
        in_specs=[col(C_AQ), col(C_AK), col(C_AV), col(C_AG), vec, vec]
        + [pl.BlockSpec(a.shape, lambda p: (0, 0)) for a in ride_along],
        out_specs=[out, out, out] + [pl.BlockSpec(memory_space=pl.ANY)] * nride,
        out_shape=[jax.ShapeDtypeStruct((SEQ, ATTN_W), F32), jax.ShapeDtypeStruct((SEQ, ATTN_W), F32),
                   jax.ShapeDtypeStruct((SEQ, ATTN_W), BF16)] + full,
        scratch_shapes=[pltpu.VMEM((SEQ, LANES), F32), pltpu.VMEM((SEQ, LANES), F32)]
        + [pltpu.VMEM(s.shape, BF16) for s in full]
        + [pltpu.SemaphoreType.DMA((AG_SEMS * nride,)), pltpu.SemaphoreType.DMA((AG_SEMS * nride,)),
           pltpu.SemaphoreType.DMA((nride,))],
        compiler_params=_params(),
    )(proj, proj, proj, proj, gq2, gk2, *ride_along)


def _attn_bwd(proj, o, lse, dyc, gq2, gk2, *ride_along):
    tn = 2048
    npairs = ATTN_W // LANES
    nride = len(ride_along)
    nbufs = nride * len(RS_KINDS)

    def body(proj_hbm, o_hbm, l_hbm, dyc_hbm, gq_ref, gk_ref, *rest):
        ride_in, rest = rest[:nride], rest[nride:]
        dq_ref, dk_ref, dv_ref, dgt_ref, gqg_ref, gkg_ref = rest[:6]
        ride_out, rest = rest[6:6 + nride], rest[6 + nride:]
        qb_, kb_, vb_, gb_, ob_, lb_, yb_, dkb_, dvb_, sems = rest[:10]
        rs_bufs, (send_sems, recv_sems, local_sems) = rest[10:10 + nbufs], rest[10 + nbufs:]
        rs_stage = _rs_stages(ride_in, ride_out, rs_bufs, send_sems, recv_sems, local_sems, [g.shape[1] for g in ride_along])
        pair = pl.program_id(0)
        for step in range(npairs):
            pl.when(pair == step)(rs_stage[step])
        bd = _head_blockdiag()
        lo = _lo_mask(CHUNK)
        lo2 = lax.broadcasted_iota(jnp.int32, (2 * CHUNK, LANES), 1) < HEAD_DIM
        valid1, valid2 = _band_masks()
        gqs = gq_ref[...] * QK_SCALE
        gk = gk_ref[...]

        def pcol(c0, of=None):
            return acol(proj_hbm, c0, of)

        def acol(hbm, c0=0, of=None):
            of = pair if of is None else of
            return hbm.at[:, pl.ds(pl.multiple_of(c0 + of * LANES, LANES), LANES)]

        def input_loads(of):
            return [pltpu.make_async_copy(src, dst, sems.at[n]) for n, (src, dst) in enumerate((
                (pcol(C_AQ, of), qb_), (pcol(C_AK, of), kb_), (pcol(C_AG, of), gb_), (acol(o_hbm, 0, of), ob_),
                (acol(dyc_hbm, GMLP_W, of), yb_), (pcol(C_AV, of), vb_), (acol(l_hbm, 0, of), lb_)))]

        early = (0, 1, 3, 4)
        loads = input_loads(pair)
        for n, cp in enumerate(loads):
            if n in early:
                pl.when(pair == 0)(cp.start)
            else:
                cp.start()

        @pl.when(pair == 0)
        def _():
            gqg_ref[...] = jnp.zeros_like(gqg_ref)
            gkg_ref[...] = jnp.zeros_like(gkg_ref)

        def pre_qk(t, carry):
            rows = pl.ds(pl.multiple_of(t * tn, tn), tn)
            q, k = qb_[rows, :], kb_[rows, :]
            ssq = [_headsum(a * a, bd) for a in (q, k)]
            qb_[rows, :] = q * lax.rsqrt(ssq[0] * (1.0 / HEAD_DIM) + EPS) * gqs
            kb_[rows, :] = k * lax.rsqrt(ssq[1] * (1.0 / HEAD_DIM) + EPS) * gk
            return carry

        def pre_gate(t, carry):
            rows = pl.ds(pl.multiple_of(t * tn, tn), tn)
            g = gb_[rows, :]
            ov = ob_[rows, :]
            dya = yb_[rows, :]
            sg = _sigmoid(g)
            dgt_ref[rows, :] = (dya * ov * (sg * (1.0 + g * (1.0 - sg)))).astype(BF16)
            do = dya * (g * sg)
            yb_[rows, :] = do
            ob_[rows, :] = jnp.where(first_half, lb_[rows, :], _headsum(do * ov, bd))
            return carry

        first_half = (lax.broadcasted_iota(jnp.int32, (tn, LANES), 1) & (HEAD_DIM - 1)) < HEAD_DIM // 2
        loads[0].wait()
        loads[1].wait()
        lax.fori_loop(0, SEQ // tn, pre_qk, 0)
        for cp in loads[2:5] + loads[6:7]:
            cp.wait()
        lax.fori_loop(0, SEQ // tn, pre_gate, 0)
        loads[5].wait()
        reloads = [pltpu.make_async_copy(pcol(C_AQ), lb_, sems.at[7]), pltpu.make_async_copy(pcol(C_AK), vb_, sems.at[8])]
        reloads[0].start()

        def load_kv(ref, d, start, prev):
            own = _rows_of(ref, start, d)[...]
            if prev is None:
                return own.astype(BF16)
            return jnp.concatenate([_rows_of(ref, prev, d)[...], own], axis=0).astype(BF16)

        def group(d, blocks):
            first = blocks[0][1] is None
            valid, lok = (valid1, lo) if first else (valid2, lo2)
            chains = [(b, h) for b in range(len(blocks)) for h in range(2)]
            mask = lambda h: lo if h == 0 else ~lo
            qs = [_rows_of(qb_, start, d)[...] for start, _ in blocks]
            dos = [_rows_of(yb_, start, d)[...] for start, _ in blocks]
            lds = [_rows_of(ob_, start, d)[...] for start, _ in blocks]
            ks = [load_kv(kb_, d, start, prev) for start, prev in blocks]
            vs = [load_kv(vb_, d, start, prev) for start, prev in blocks]
            qbs = [q.astype(BF16) for q in qs]
            dobs = [do.astype(BF16) for do in dos]
            ss = [_dot_nt(jnp.where(mask(h), qs[b], 0.0).astype(BF16), ks[b]) for b, h in chains]
            dps = [_dot_nt(jnp.where(mask(h), dos[b], 0.0).astype(BF16), vs[b]) for b, h in chains]
            pbs, dss = [], []
            for s, dp, (b, h) in zip(ss, dps, chains):
                hc, dc = h * HEAD_DIM, h * HEAD_DIM + HEAD_DIM // 2
                p = jnp.exp(jnp.where(valid, s, -jnp.inf) - lds[b][:, hc:hc + 1])
                pbs.append(p.astype(BF16))
                dss.append((p * (dp - lds[b][:, dc:dc + 1])).astype(BF16))
            dqs = [_dot(ds, ks[b]) for ds, (b, h) in zip(dss, chains)]
            dks = [_dot_tn(ds, qbs[b]) for ds, (b, h) in zip(dss, chains)]
            dvs = [_dot_tn(p, dobs[b]) for p, (b, h) in zip(pbs, chains)]
            assign = d == DILATIONS[0]
            for b, (start, prev) in enumerate(blocks):
                c0, c1 = 2 * b, 2 * b + 1
                dq_rows = _rows_of(gb_, start, d)
                dqb = jnp.where(lo, dqs[c0], dqs[c1])
                dq_rows[...] = dqb if assign else dq_rows[...] + dqb
                dkc = jnp.where(lok, dks[c0], dks[c1])
                dvc = jnp.where(lok, dvs[c0], dvs[c1])
                spans = ((start, slice(0, CHUNK), True),) if first else (
                    (prev, slice(0, CHUNK), False), (start, slice(CHUNK, 2 * CHUNK), True))
                for st, sl, own in spans:
                    dk_rows = _rows_of(dkb_, st, d)
                    dv_rows = _rows_of(dvb_, st, d)
                    if assign and own:
                        dk_rows[...] = dkc[sl]
                        dv_rows[...] = dvc[sl]
                    else:
                        dk_rows[...] = dk_rows[...] + dkc[sl]
                        dv_rows[...] = dv_rows[...] + dvc[sl]

        for d in DILATIONS:
            _for_blocks(d, group, ATTN_UNROLL)

        reloads[1].start()

        @pl.when(pair < npairs - 1)
        def _():
            nxt = input_loads(pair + 1)
            for n in early:
                nxt[n].start()

        for cp in reloads:
            cp.wait()

        def post(t, carry):
            gq_acc, gk_acc = carry
            rows = pl.ds(pl.multiple_of(t * tn, tn), tn)
            raws = [lb_[rows, :], vb_[rows, :]]
            dns = [gb_[rows, :], dkb_[rows, :]]
            rs = [lax.rsqrt(_headsum(a * a, bd) * (1.0 / HEAD_DIM) + EPS) for a in raws]
            zs = [a * r for a, r in zip(raws, rs)]
            dzs = [dn * gain for dn, gain in zip(dns, (gqs, gk))]
            means = [_headsum(dz * z, bd) * (1.0 / HEAD_DIM) for dz, z in zip(dzs, zs)]
            dq, dk = [r * (dz - z * mean) for r, dz, z, mean in zip(rs, dzs, zs, means)]
            gq, gkk = [jnp.sum(dn * z, axis=0, keepdims=True) for dn, z in zip(dns, zs)]
            dq_ref[rows, :] = dq.astype(BF16)
            dk_ref[rows, :] = dk.astype(BF16)
            dv_ref[rows, :] = dvb_[rows, :].astype(BF16)
            return gq_acc + gq * QK_SCALE, gk_acc + gkk

        zero = jnp.zeros((1, LANES), F32)
        gq_acc, gk_acc = lax.fori_loop(0, SEQ // tn, post, (zero, zero))
        gqg_ref[0:1, :] += gq_acc
        gkg_ref[0:1, :] += gk_acc

        @pl.when(pair == npairs - 1)
        def _():
            gqg_ref[0:1, :] = _fold_heads(gqg_ref[0:1, :])
            gkg_ref[0:1, :] = _fold_heads(gkg_ref[0:1, :])
            rs_stage[npairs]()

    hbm = pl.BlockSpec(memory_space=pl.ANY)
    vec = pl.BlockSpec((1, LANES), lambda p: (0, 0))
    blk8 = pl.BlockSpec((8, LANES), lambda p: (0, 0))
    out = pl.BlockSpec((SEQ, LANES), lambda p: (0, p))
    big = jax.ShapeDtypeStruct((SEQ, ATTN_W), BF16)
    nsem = RS_SEMS * nride
    return _call(
        body, name="attn_bwd", grid=(npairs,),
        in_specs=[hbm, hbm, hbm, hbm, vec, vec] + [hbm] * nride,
        out_specs=[out, out, out, out, blk8, blk8] + [hbm] * nride,
        out_shape=[big, big, big, big, jax.ShapeDtypeStruct((8, LANES), F32), jax.ShapeDtypeStruct((8, LANES), F32)]
        + [jax.ShapeDtypeStruct((2, g.shape[0] // 8, g.shape[1]), F32) for g in ride_along],
        scratch_shapes=[pltpu.VMEM((SEQ, LANES), F32) for _ in range(9)] + [pltpu.SemaphoreType.DMA((9,))]
        + _rs_scratch([g.shape for g in ride_along]) + [pltpu.SemaphoreType.DMA((nsem,)), pltpu.SemaphoreType.DMA((nsem,)),
                                     pltpu.SemaphoreType.DMA((nride,))],
        compiler_params=_params(),
    )(proj, o, lse, dyc, gq2, gk2, *[_rs_view(g) for g in ride_along])


def _mem_kv(mem, gain, wkv):
    def body(m_ref, g_ref, w_ref, kv_ref, hm_ref):
        mv = m_ref[...]
        ms = jnp.mean(mv * mv, axis=-1, keepdims=True)
        hm = (mv * lax.rsqrt(ms + EPS) * g_ref[...]).astype(BF16)
        hm_ref[...] = hm
        kv_ref[...] = _dot(hm, w_ref[...])

    return _call(
        body, name="mem_kv",
        out_shape=[jax.ShapeDtypeStruct((MEM_LEN, 2 * MEM_W), F32), jax.ShapeDtypeStruct((MEM_LEN, D_MODEL), BF16)],
        compiler_params=_params(),
    )(mem, gain, wkv)


def _mem_keys(kv_ref, kg_ref, bd, p):
    mk = kv_ref[:, p * LANES:(p + 1) * LANES]
    r = lax.rsqrt(_headsum(mk * mk, bd) * (1.0 / HEAD_DIM) + EPS)
    z = mk * r
    mkn = (z * kg_ref[:, p * LANES:(p + 1) * LANES]).astype(BF16)
    mvp = kv_ref[:, MEM_W + p * LANES:MEM_W + (p + 1) * LANES].astype(BF16)
    return mkn, mvp, r, z


def _mem_fwd(proj, kv, qg4, kg4):
    tm = 1024

    def body(q_ref, g_ref, kv_ref, qg_ref, kg_ref, om_ref, ym_ref):
        bd = _head_blockdiag()
        lo = _lo_mask(tm)
        keys, qns = [], []
        for p in range(2):
            cs = slice(p * LANES, (p + 1) * LANES)
            keys.append(_mem_keys(kv_ref, kg_ref, bd, p)[:2])
            q = q_ref[:, cs]
            qns.append(q * lax.rsqrt(_headsum(q * q, bd) * (1.0 / HEAD_DIM) + EPS) * (qg_ref[:, cs] * QK_SCALE))
        chains = [(p, h) for p in range(2) for h in range(2)]
        ss = [_dot_nt(jnp.where(lo if h == 0 else ~lo, qns[p], 0.0).astype(BF16), keys[p][0]) for p, h in chains]
        es = [jnp.exp(s - jnp.max(s, axis=-1, keepdims=True)) for s in ss]
        os_ = [_dot(e.astype(BF16), keys[p][1]) for e, (p, h) in zip(es, chains)]
        res = [o * (1.0 / jnp.sum(e, axis=-1, keepdims=True)) for o, e in zip(os_, es)]
        for p in range(2):
            cs = slice(p * LANES, (p + 1) * LANES)
            ov = jnp.where(lo, res[2 * p], res[2 * p + 1])
            g = g_ref[:, cs]
            om_ref[:, cs] = ov
            ym_ref[:, cs] = (ov * (g * _sigmoid(g))).astype(BF16)

    vec = pl.BlockSpec((1, MEM_W), lambda i: (0, 0))
    return _call(
        body, name="mem_fwd", grid=(SEQ // tm,),
        in_specs=[pl.BlockSpec((tm, MEM_W), lambda i: (i, C_MQ // MEM_W)),
                  pl.BlockSpec((tm, MEM_W), lambda i: (i, C_MG // MEM_W)),
                  pl.BlockSpec((MEM_LEN, 2 * MEM_W), lambda i: (0, 0)), vec, vec],
        out_specs=[pl.BlockSpec((tm, MEM_W), lambda i: (i, 0)), pl.BlockSpec((tm, MEM_W), lambda i: (i, 0))],
        out_shape=[jax.ShapeDtypeStruct((SEQ, MEM_W), F32), jax.ShapeDtypeStruct((SEQ, MEM_W), BF16)],
        compiler_params=_params(),
    )(proj, proj, kv, qg4, kg4)


def _mem_bwd(proj, om, dyc, kv, hm, mem, mgain, wkv, qg4, kg4):
    tm = 1024
    nsteps = SEQ // tm

    def body(q_ref, g_ref, om_ref, dy_ref, kv_ref, hm_ref, mem_ref, mg_ref, w_ref, qg_ref, kg_ref,
             dq_ref, dgt_ref, gqg_ref, gkg_ref, gw_ref, gmg_ref, dmk_ref, dmv_ref, gq_acc):
        i = pl.program_id(0)
        bd = _head_blockdiag()
        lo = _lo_mask(tm)
        lom = _lo_mask(MEM_LEN)

        @pl.when(i == 0)
        def _():
            dmk_ref[...] = jnp.zeros_like(dmk_ref)
            dmv_ref[...] = jnp.zeros_like(dmv_ref)
            gq_acc[...] = jnp.zeros_like(gq_acc)

        pairs = []
        for p in range(2):
            cs = slice(p * LANES, (p + 1) * LANES)
            mkn, mvp, _, _ = _mem_keys(kv_ref, kg_ref, bd, p)
            gqs = qg_ref[:, cs] * QK_SCALE
            q = q_ref[:, cs]
            r = lax.rsqrt(_headsum(q * q, bd) * (1.0 / HEAD_DIM) + EPS)
            z = q * r
            qn = z * gqs
            g = g_ref[:, cs]
            ov = om_ref[:, cs]
            dym = dy_ref[:, cs]
            sg = _sigmoid(g)
            dgt_ref[:, cs] = (dym * ov * (sg * (1.0 + g * (1.0 - sg)))).astype(BF16)
            do = dym * (g * sg)
            pairs.append(dict(cs=cs, mkn=mkn, mvp=mvp, gqs=gqs, r=r, z=z, qn=qn, qnb=qn.astype(BF16), do=do,
                              dob=do.astype(BF16), delta=_headsum(do * ov, bd)))
        chains = [(pr_, h) for pr_ in pairs for h in range(2)]
        mask = lambda h: lo if h == 0 else ~lo
        ss = [_dot_nt(jnp.where(mask(h), c["qn"], 0.0).astype(BF16), c["mkn"]) for c, h in chains]
        dps = [_dot_nt(jnp.where(mask(h), c["do"], 0.0).astype(BF16), c["mvp"]) for c, h in chains]
        prs, dss = [], []
        for s, dp, (c, h) in zip(ss, dps, chains):
            e = jnp.exp(s - jnp.max(s, axis=-1, keepdims=True))
            pr = e * (1.0 / jnp.sum(e, axis=-1, keepdims=True))
            prs.append(pr.astype(BF16))
            dss.append((pr * (dp - c["delta"][:, h * HEAD_DIM:h * HEAD_DIM + 1])).astype(BF16))
        dqs = [_dot(ds, c["mkn"]) for ds, (c, h) in zip(dss, chains)]
        dks = [_dot_tn(ds, c["qnb"]) for ds, (c, h) in zip(dss, chains)]
        dvs = [_dot_tn(pr, c["dob"]) for pr, (c, h) in zip(prs, chains)]
        for p, c in enumerate(pairs):
            cs, z, r = c["cs"], c["z"], c["r"]
            dqn = jnp.where(lo, dqs[2 * p], dqs[2 * p + 1])
            dmk_ref[:, cs] += jnp.where(lom, dks[2 * p], dks[2 * p + 1])
            dmv_ref[:, cs] += jnp.where(lom, dvs[2 * p], dvs[2 * p + 1])
            dz = dqn * c["gqs"]
            dq_ref[:, cs] = (r * (dz - z * (_headsum(dz * z, bd) * (1.0 / HEAD_DIM)))).astype(BF16)
            gq_acc[:, cs] += jnp.sum(dqn * z, axis=0, keepdims=True) * QK_SCALE

        @pl.when(i == nsteps - 1)
        def _():
            gqg_ref[...] = jnp.zeros_like(gqg_ref)
            gkg_ref[...] = jnp.zeros_like(gkg_ref)
            gqg_ref[0:1, :] = _fold_heads(gq_acc[:, 0:LANES] + gq_acc[:, LANES:2 * LANES])
            dkv = []
            gk = jnp.zeros((1, LANES), F32)
            for p in range(2):
                cs = slice(p * LANES, (p + 1) * LANES)
                _, _, r, z = _mem_keys(kv_ref, kg_ref, bd, p)
                dn = dmk_ref[:, cs]
                dz = dn * kg_ref[:, cs]
                gk = gk + jnp.sum(dn * z, axis=0, keepdims=True)
                dkv.append(r * (dz - z * (_headsum(dz * z, bd) * (1.0 / HEAD_DIM))))
            gkg_ref[0:1, :] = _fold_heads(gk)
            dkvb = jnp.concatenate(dkv + [dmv_ref[...]], axis=1).astype(BF16)
            gw_ref[...] = _dot_tn(hm_ref[...], dkvb)
            dhm = _dot_nt(dkvb, w_ref[...])
            mv = mem_ref[...]
            zm = mv * lax.rsqrt(jnp.mean(mv * mv, axis=-1, keepdims=True) + EPS)
            _put_rows(gmg_ref, jnp.sum(dhm * zm, axis=0, keepdims=True))

    const = lambda shape: pl.BlockSpec(shape, lambda i: (0,) * len(shape))
    row = lambda j: pl.BlockSpec((tm, MEM_W), lambda i: (i, j))
    blk8 = jax.ShapeDtypeStruct((8, LANES), F32)
    return _call(
        body, name="mem_bwd", grid=(nsteps,),
        in_specs=[row(C_MQ // MEM_W), row(C_MG // MEM_W), row(0), row((GMLP_W + ATTN_W) // MEM_W),
                  const((MEM_LEN, 2 * MEM_W)), const((MEM_LEN, D_MODEL)), const((MEM_LEN, D_MODEL)),
                  const((1, D_MODEL)), const((D_MODEL, 2 * MEM_W)), const((1, MEM_W)), const((1, MEM_W))],
        out_specs=[row(0), row(0), const((8, LANES)), const((8, LANES)),
                   const((D_MODEL, 2 * MEM_W)), const((8, LANES))],
        out_shape=[jax.ShapeDtypeStruct((SEQ, MEM_W), BF16), jax.ShapeDtypeStruct((SEQ, MEM_W), BF16),
                   blk8, blk8, jax.ShapeDtypeStruct((D_MODEL, 2 * MEM_W), F32), blk8],
        scratch_shapes=[pltpu.VMEM((MEM_LEN, MEM_W), F32), pltpu.VMEM((MEM_LEN, MEM_W), F32),
                        pltpu.VMEM((1, MEM_W), F32)],
        compiler_params=_params(),
    )(proj, proj, om, dyc, kv, hm, mem, mgain, wkv, qg4, kg4)


def _out_loss(yg, ya, ym, x, tgt, wo):
    tm = 512
    nsteps = SEQ // tm
    parts = ((0, GMLP_W), (GMLP_W, ATTN_W), (GMLP_W + ATTN_W, MEM_W))

    def body(yg_ref, ya_ref, ym_ref, x_ref, t_ref, w_ref, dy_ref, dyc_ref, gw_ref, ls_ref):
        i = pl.program_id(0)

        @pl.when(i == 0)
        def _():
            gw_ref[...] = jnp.zeros_like(gw_ref)
            ls_ref[...] = jnp.zeros_like(ls_ref)

        ys = (yg_ref[...], ya_ref[...], ym_ref[...])
        y = sum(_dot(yv, w_ref[r0:r0 + n, :]) for yv, (r0, n) in zip(ys, parts))
        err = x_ref[...] + y - t_ref[...]
        _put_rows(ls_ref, jnp.sum(err * err, axis=0, keepdims=True), accumulate=True)
        dy = err * (1.0 / D_MODEL)
        dy_ref[...] = dy
        dyb = dy.astype(BF16)
        dyc_ref[...] = _dot_nt(dyb, w_ref[...])
        for yv, (r0, n) in zip(ys, parts):
            gw_ref[r0:r0 + n, :] += _dot_tn(yv, dyb)

    row = lambda w: pl.BlockSpec((tm, w), lambda i: (i, 0))
    const = lambda shape: pl.BlockSpec(shape, lambda i: (0, 0))
    return _call(
        body, name="out_loss", grid=(nsteps,),
        in_specs=[row(GMLP_W), row(ATTN_W), row(MEM_W), row(D_MODEL), row(D_MODEL), const((D_MODEL, D_MODEL))],
        out_specs=[row(D_MODEL), row(D_MODEL), const((D_MODEL, D_MODEL)), const((8, LANES))],
        out_shape=[jax.ShapeDtypeStruct((SEQ, D_MODEL), F32), jax.ShapeDtypeStruct((SEQ, D_MODEL), F32),
                   jax.ShapeDtypeStruct((D_MODEL, D_MODEL), F32), jax.ShapeDtypeStruct((8, LANES), F32)],
        compiler_params=_params(),
    )(yg, ya, ym, x, tgt, wo)


def _proj_bwd(x, dy, gain, wt, dg, daq, dak, dav, dag, dmq, dmg):
    tm = 512
    nsteps = SEQ // tm
    pieces = ((C_GU, 3 * GMLP_W), (C_AQ, ATTN_W), (C_AK, ATTN_W), (C_AV, ATTN_W), (C_AG, ATTN_W),
              (C_MQ, MEM_W), (C_MG, MEM_W))

    def body(x_ref, dy_ref, g_ref, wt_hbm, p0, p1, p2, p3, p4, p5, p6, gx_ref, gwt_hbm, gg_ref, wt_v, acc, wt_sem, out_sems):
        i = pl.program_id(0)
        wt_load = pltpu.make_async_copy(wt_hbm, wt_v, wt_sem)

        @pl.when(i == 0)
        def _():
            wt_load.start()
            acc[...] = jnp.zeros_like(acc)
            gg_ref[...] = jnp.zeros_like(gg_ref)

        xv = x_ref[...]
        r = lax.rsqrt(jnp.mean(xv * xv, axis=-1, keepdims=True) + EPS)
        z = xv * r
        g = g_ref[...]
        h = (z * g).astype(BF16)
        pl.when(i == 0)(wt_load.wait)
        flush = [pltpu.make_async_copy(acc.at[c0:c0 + w, :], gwt_hbm.at[c0:c0 + w, :], out_sems.at[n])
                 for n, (c0, w) in enumerate(pieces)]
        dh = jnp.zeros((tm, D_MODEL), F32)
        for n, (pref, (c0, w)) in enumerate(zip((p0, p1, p2, p3, p4, p5, p6), pieces)):
            dp = pref[...]
            dh = dh + _dot(dp, wt_v[c0:c0 + w, :])
            acc[c0:c0 + w, :] += _dot_tn(dp, h)
            pl.when(i == nsteps - 1)(flush[n].start)
        _put_rows(gg_ref, jnp.sum(dh * z, axis=0, keepdims=True), accumulate=True)
        dz = dh * g
        gx_ref[...] = dy_ref[...] + r * (dz - z * jnp.mean(dz * z, axis=-1, keepdims=True))

        @pl.when(i == nsteps - 1)
        def _():
            for cp in flush:
                cp.wait()

    row = lambda w: pl.BlockSpec((tm, w), lambda i: (i, 0))
    hbm = pl.BlockSpec(memory_space=pl.ANY)
    vec = pl.BlockSpec((1, D_MODEL), lambda i: (0, 0))
    return _call(
        body, name="proj_bwd", grid=(nsteps,),
        in_specs=[row(D_MODEL), row(D_MODEL), vec, hbm] + [row(w) for _, w in pieces],
        out_specs=[row(D_MODEL), hbm, pl.BlockSpec((8, LANES), lambda i: (0, 0))],
        out_shape=[jax.ShapeDtypeStruct((SEQ, D_MODEL), F32), jax.ShapeDtypeStruct((IN_W, D_MODEL), F32),
                   jax.ShapeDtypeStruct((8, LANES), F32)],
        scratch_shapes=[pltpu.VMEM((IN_W, D_MODEL), BF16), pltpu.VMEM((IN_W, D_MODEL), F32), pltpu.SemaphoreType.DMA,
                        pltpu.SemaphoreType.DMA((len(pieces),))],
        compiler_params=_params(),
    )(x, dy, gain, wt, dg, daq, dak, dav, dag, dmq, dmg)


AG_SEMS = 8


def _gather_stages(ins, lands, send_sems, recv_sems):
    n = len(ins)
    nrows = [a.shape[0] for a in ins]
    x, y, c = lax.axis_index("x"), lax.axis_index("y"), lax.axis_index("c")
    sib, xn, yn = (x, y, 1 - c), (1 - x, y, c), (x, 1 - y, c)
    me, cx, cy, cd = 2 * x + y, 2 * (1 - x) + y, 2 * x + (1 - y), 2 * (1 - x) + (1 - y)

    def part(a, chip, hf, quarter=None):
        rows = nrows[a] // 2
        base = chip * nrows[a] + hf * rows
        if quarter is not None:
            rows = rows // 2
            base = base + quarter * rows
        return lands[a].at[pl.ds(pl.multiple_of(base, 16), rows), :]

    def copy(a, j, ref, to):
        k = AG_SEMS * a + j
        return pltpu.make_async_remote_copy(src_ref=ref, dst_ref=ref, send_sem=send_sems.at[k],
                                            recv_sem=recv_sems.at[k], device_id=to, device_id_type=MESH)

    def own(a):
        return [copy(a, 0, part(a, me, c), xn), copy(a, 1, part(a, me, c), yn)]

    def neighbours(a):
        return [copy(a, 4, part(a, cx, c, 1), yn), copy(a, 2, part(a, cx, c), sib),
                copy(a, 5, part(a, cy, c, 0), xn), copy(a, 3, part(a, cy, c), sib)]

    def diagonal(a):
        return [copy(a, 7, part(a, cd, c, 1), sib), copy(a, 6, part(a, cd, c, 0), sib)]

    def send_own():
        for a in range(n):
            lands[a][pl.ds(pl.multiple_of(me * nrows[a], 16), nrows[a]), :] = ins[a][...].astype(BF16)
            for cp in own(a):
                cp.start()

    def pass_on_neighbours():
        for a in range(n):
            copy(a, 0, part(a, cx, c), xn).wait_recv()
            copy(a, 1, part(a, cy, c), yn).wait_recv()
            for cp in neighbours(a):
                cp.start()

    def pass_on_diagonal():
        for a in range(n):
            copy(a, 4, part(a, cd, c, 1), yn).wait_recv()
            copy(a, 5, part(a, cd, c, 0), xn).wait_recv()
            for cp in diagonal(a):
                cp.start()

    def y_complete():
        for a in range(n):
            copy(a, 3, part(a, cy, 1 - c), sib).wait_recv()

    def x_complete():
        for a in range(n):
            copy(a, 2, part(a, cx, 1 - c), sib).wait_recv()

    def diagonal_complete():
        for a in range(n):
            copy(a, 6, part(a, cd, 1 - c, 0), sib).wait_recv()
            copy(a, 7, part(a, cd, 1 - c, 1), sib).wait_recv()

    def sends_done():
        for a in range(n):
            for cp in own(a) + neighbours(a) + diagonal(a):
                cp.wait_send()

    def finish():
        y_complete()
        x_complete()
        diagonal_complete()
        sends_done()

    return (send_own, pass_on_neighbours, pass_on_diagonal, finish), (y_complete, x_complete, diagonal_complete, sends_done)


RS_SEMS = 6
RS_KINDS = (((2, 2), 1, F32), ((2, 2), 1, F32), ((2, 2), 2, BF16), ((2, 2), 2, BF16), ((2, 2), 2, F32),
            ((2,), 2, BF16), ((2,), 2, BF16), ((2,), 1, F32))


def _rs_view(g):
    return g.reshape(2, 2, 2, g.shape[0] // 8, g.shape[1])


def _rs_scratch(shapes):
    return [pltpu.VMEM(lead + (r // 8, w // split), dt) for lead, split, dt in RS_KINDS for r, w in shapes]


def _rs_stages(gs, outs, bufs, send_sems, recv_sems, local_sems, widths):
    n = len(gs)
    loc, ra, s_b, r_b, acc1, s_c, r_c, fin = (bufs[n * i:n * i + n] for i in range(len(RS_KINDS)))
    half_w = [w // 2 for w in widths]
    x, y, c = lax.axis_index("x"), lax.axis_index("y"), lax.axis_index("c")
    sib, xn, yn = (x, y, 1 - c), (1 - x, y, c), (x, 1 - y, c)

    def copy(a, j, src, dst, to):
        k = RS_SEMS * a + j
        return pltpu.make_async_remote_copy(src_ref=src, dst_ref=dst, send_sem=send_sems.at[k],
                                            recv_sem=recv_sems.at[k], device_id=to, device_id_type=MESH)

    def step_a(a):
        return [copy(a, 0, gs[a].at[:, :, 1 - c], ra[a], sib),
                pltpu.make_async_copy(gs[a].at[:, :, c], loc[a], local_sems.at[a])]

    def step_b(a):
        return copy(a, 1, s_b[a].at[0], r_b[a].at[0], xn), copy(a, 2, s_b[a].at[1], r_b[a].at[1], yn)

    def step_c(a):
        return copy(a, 3, s_c[a].at[0], r_c[a].at[0], yn), copy(a, 4, s_c[a].at[1], r_c[a].at[1], xn)

    def step_d(a, half):
        rows = fin[a].at[half]
        return copy(a, 5, rows, rows, sib)

    def start():
        for a in range(n):
            for cp in step_a(a):
                cp.start()

    def a_to_b():
        for a in range(n):
            for cp in step_a(a):
                cp.wait()
            ra[a][...] = loc[a][...] + ra[a][...]
            s_b[a][0] = ra[a][1 - x, :, :, :half_w[a]].astype(BF16)
            s_b[a][1] = ra[a][:, 1 - y, :, half_w[a]:].astype(BF16)
            for cp in step_b(a):
                cp.start()

    def b_to_c():
        for a in range(n):
            for cp in step_b(a):
                cp.wait()
            acc1[a][0] = ra[a][x, :, :, :half_w[a]] + r_b[a][0].astype(F32)
            acc1[a][1] = ra[a][:, y, :, half_w[a]:] + r_b[a][1].astype(F32)
            s_c[a][0] = acc1[a][0, 1 - y].astype(BF16)
            s_c[a][1] = acc1[a][1, 1 - x].astype(BF16)
            for cp in step_c(a):
                cp.start()

    def c_to_d():
        for a in range(n):
            for cp in step_c(a):
                cp.wait()
            fin[a][c, :, :half_w[a]] = acc1[a][0, y] + r_c[a][0].astype(F32)
            fin[a][c, :, half_w[a]:] = acc1[a][1, x] + r_c[a][1].astype(F32)
            step_d(a, c).start()

    def finish():
        to_hbm = [pltpu.make_async_copy(fin[a], outs[a], local_sems.at[a]) for a in range(n)]
        for a in range(n):
            step_d(a, 1 - c).wait_recv()
            step_d(a, c).wait_send()
            to_hbm[a].start()
        for cp in to_hbm:
            cp.wait()

    return start, a_to_b, b_to_c, c_to_d, finish


def _reduce_grads(gwt, g_ws, tiny):
    cw = gwt.shape[1] // RS_CHUNKS
    chunk_shape = (gwt.shape[0], cw)

    def body(g0, ws_in, tiny_in, *rest):
        outs, o_ws, o_tiny = rest[:RS_CHUNKS], rest[RS_CHUNKS], rest[RS_CHUNKS + 1]
        rest = rest[RS_CHUNKS + 2:]
        nb = len(RS_KINDS) * RS_CHUNKS
        sm, sa, sb, sc, acc_s, send_sems, recv_sems, local_sems = rest[nb:]
        blocks = [g0.at[:, :, :, :, pl.ds(j * cw, cw)] for j in range(RS_CHUNKS)]
        start, a_to_b, b_to_c, c_to_d, finish = _rs_stages(blocks, outs, rest[:nb], send_sems, recv_sems, local_sems,
                                                           [cw] * RS_CHUNKS)
        n_ws = ws_in.shape[0]
        sm[0:n_ws, :] = ws_in[...]
        sm[n_ws:, :] = tiny_in[...]
        x, y, c = lax.axis_index("x"), lax.axis_index("y"), lax.axis_index("c")

        def small(j, src, dst, to):
            k = RS_SEMS * RS_CHUNKS + j
            return pltpu.make_async_remote_copy(src_ref=src, dst_ref=dst, send_sem=send_sems.at[k],
                                                recv_sem=recv_sems.at[k], device_id=to, device_id_type=MESH)

        along_c, along_x, along_y = (small(0, sm, sa, (x, y, 1 - c)), small(1, acc_s, sb, (1 - x, y, c)),
                                     small(2, sb, sc, (x, 1 - y, c)))
        start()
        along_c.start()
        a_to_b()
        along_c.wait()
        acc_s[...] = sm[...] + sa[...]
        along_x.start()
        b_to_c()
        along_x.wait()
        sb[...] = acc_s[...] + sb[...]
        along_y.start()
        c_to_d()
        along_y.wait()
        o_ws[...] = sb[0:n_ws, :] + sc[0:n_ws, :]
        o_tiny[...] = sb[n_ws:, :] + sc[n_ws:, :]
        finish()

    vm = pl.BlockSpec(memory_space=pltpu.VMEM)
    hbm = pl.BlockSpec(memory_space=pl.ANY)
    small_shape = (g_ws.shape[0] + tiny.shape[0], LANES)
    scratch = _rs_scratch([chunk_shape] * RS_CHUNKS) + [pltpu.VMEM(small_shape, F32) for _ in range(5)]
    nsem = RS_SEMS * RS_CHUNKS + 3
    scratch += [pltpu.SemaphoreType.DMA((nsem,)), pltpu.SemaphoreType.DMA((nsem,)), pltpu.SemaphoreType.DMA((RS_CHUNKS,))]
    return _call(
        body, name="reduce_grads",
        out_shape=[jax.ShapeDtypeStruct((2, gwt.shape[0] // 8, cw), F32)] * RS_CHUNKS
        + [jax.ShapeDtypeStruct(g_ws.shape, F32), jax.ShapeDtypeStruct(tiny.shape, F32)],
        in_specs=[hbm, vm, vm],
        out_specs=[hbm] * RS_CHUNKS + [vm, vm],
        scratch_shapes=scratch,
        compiler_params=_params(),
    )(_rs_view(gwt), g_ws, tiny)


def _adam_update(w, g, m, v):
    nm = ADAM_B1 * m + (1.0 - ADAM_B1) * g
    nv = ADAM_B2 * v + (1.0 - ADAM_B2) * (g * g)
    m_hat = nm / (1.0 - ADAM_B1 ** ADAM_STEP)
    v_hat = nv / (1.0 - ADAM_B2 ** ADAM_STEP)
    return -ADAM_LR * (m_hat / (jnp.sqrt(v_hat) + ADAM_EPS) + ADAM_WD * w), nm, nv


def _adamw(w, g, m, v):
    rows, cols = w.shape
    tm = max(t for t in range(8, 257, 8) if rows % t == 0)
    parts = tuple(g) if isinstance(g, (tuple, list)) else (g,)
    n = len(parts)

    def body(w_ref, m_ref, v_ref, *refs):
        gv = jnp.concatenate([r[...] for r in refs[:n]], axis=1)
        d_ref, nm_ref, nv_ref = refs[n:n + 3]
        d_ref[...], nm_ref[...], nv_ref[...] = _adam_update(w_ref[...], gv, m_ref[...], v_ref[...])
        if n > 1:
            refs[n + 3][...] = gv

    blk = pl.BlockSpec((tm, cols), lambda i: (i, 0))
    nout = 3 if n == 1 else 4
    res = _call(
        body, name="adamw", grid=(rows // tm,),
        in_specs=[blk] * 3 + [pl.BlockSpec((tm, p.shape[1]), lambda i: (i, 0)) for p in parts], out_specs=[blk] * nout,
        out_shape=[jax.ShapeDtypeStruct((rows, cols), F32)] * nout,
        compiler_params=_params(),
    )(w, m, v, *parts)
    return (parts[0] if n == 1 else res[3], *res[:3])


def _adamw_tiny(tiny, weights, ms, vs):
    shapes = [w.shape for w in weights]
    n = len(weights)

    def grad_of(t_ref, k, shape):
        base = 8 * k
        if shape[1] > LANES:
            return [t_ref[base + j:base + j + 1, :] for j in range(shape[1] // LANES)]
        return [t_ref[base:base + shape[0], 0:shape[1]]]

    def body(t_ref, *refs):
        w_refs, m_refs, v_refs = refs[:n], refs[n:2 * n], refs[2 * n:3 * n]
        loss_ref, outs = refs[3 * n], refs[3 * n + 1:]
        loss_ref[...] = (0.5 / D_MODEL) * jnp.sum(t_ref[8 * n:8 * n + 8, :], keepdims=True)
        for k, shape in enumerate(shapes):
            g_ref, d_ref, nm_ref, nv_ref = outs[4 * k:4 * k + 4]
            for j, g in enumerate(grad_of(t_ref, k, shape)):
                cols = slice(j * LANES, (j + 1) * LANES) if shape[1] > LANES else slice(None)
                g_ref[:, cols] = g
                d_ref[:, cols], nm_ref[:, cols], nv_ref[:, cols] = _adam_update(
                    w_refs[k][:, cols], g, m_refs[k][:, cols], v_refs[k][:, cols])

    out_shape = [jax.ShapeDtypeStruct((1, 1), F32)]
    for shape in shapes:
        out_shape += [jax.ShapeDtypeStruct(shape, F32)] * 4
    return _call(body, name="adamw_tiny", out_shape=out_shape, compiler_params=_params())(tiny, *weights, *ms, *vs)


def _local_grads(x, mem, tgt, norm_gain, wt_sh, gmlp_v_gain, gmlp_w_s, gmlp_b, attn_q_gain, attn_k_gain,
                 mem_norm_gain, wkv_sh, mem_q_gain, mem_k_gain, wo_sh):
    vg = gmlp_v_gain.reshape(1, GMLP_W)
    bias_full = jnp.repeat(gmlp_b.T, HEAD_DIM, axis=1)
    gq2, gk2 = jnp.tile(attn_q_gain, (1, 2)), jnp.tile(attn_k_gain, (1, 2))
    qg4, kg4 = jnp.tile(mem_q_gain, (1, 4)), jnp.tile(mem_k_gain, (1, 4))

    proj, wt = _gather_proj(x, norm_gain, wt_sh)
    yg = _gmlp_fwd(proj, vg, gmlp_w_s, bias_full)
    o, lse, ya, wkv, wo = _attn_fwd(proj, gq2, gk2, wkv_sh, wo_sh)
    kv, hm = _mem_kv(mem, mem_norm_gain, wkv)
    om, ym = _mem_fwd(proj, kv, qg4, kg4)
    dy, dyc, g_wo, err2 = _out_loss(yg, ya, ym, x, tgt, wo)
    dmq, dmg, g_mq, g_mk, g_wkv, g_mng = _mem_bwd(proj, om, dyc, kv, hm, mem, mem_norm_gain, wkv, qg4, kg4)
    daq, dak, dav, dag, g_aq, g_ak, g_wkv_sh, g_wo_sh = _attn_bwd(proj, o, lse, dyc, gq2, gk2, g_wkv, g_wo)
    dg, g_ws, g_b, g_vg = _gmlp_bwd(proj, dyc, vg, gmlp_w_s, bias_full)
    gx, g_wt, g_ng = _proj_bwd(x, dy, norm_gain, wt, dg, daq, dak, dav, dag, dmq, dmg)

    tiny = jnp.concatenate([g_ng, g_vg, g_b, g_aq, g_ak, g_mng, g_mq, g_mk, err2], axis=0)
    return gx, g_wt, g_wkv_sh, g_wo_sh, g_ws.reshape(4 * CHUNK, CHUNK), tiny


def kernel(x, mem, norm_gain, w_in, gmlp_v_gain, gmlp_w_s, gmlp_b, attn_q_gain, attn_k_gain, mem_norm_gain, w_mem_kv, mem_q_gain, mem_k_gain, w_out, loss_target, m_norm_gain, m_w_in, m_gmlp_v_gain, m_gmlp_w_s, m_gmlp_b, m_attn_q_gain, m_attn_k_gain, m_mem_norm_gain, m_w_mem_kv, m_mem_q_gain, m_mem_k_gain, m_w_out, v_norm_gain, v_w_in, v_gmlp_v_gain, v_gmlp_w_s, v_gmlp_b, v_attn_q_gain, v_attn_k_gain, v_mem_norm_gain, v_w_mem_kv, v_mem_q_gain, v_mem_k_gain, v_w_out):
    gx, g_wt, g_wkv_sh, g_wo_sh, g_ws, tiny = _local_grads(
        x[0], mem[0], loss_target[0], norm_gain, w_in[0].T, gmlp_v_gain[0], gmlp_w_s[0], gmlp_b[0],
        attn_q_gain, attn_k_gain, mem_norm_gain, w_mem_kv[0], mem_q_gain, mem_k_gain, w_out[0])
    *g_wt_sh, g_ws, tiny = _reduce_grads(g_wt, g_ws, tiny)
    chip_block = lambda g: g.reshape(2 * g.shape[1], g.shape[2])
    g_wt_sh = tuple(chip_block(g) for g in g_wt_sh)
    g_wkv_sh, g_wo_sh = chip_block(g_wkv_sh), chip_block(g_wo_sh)

    ws = (norm_gain, w_in, gmlp_v_gain, gmlp_w_s, gmlp_b, attn_q_gain, attn_k_gain, mem_norm_gain, w_mem_kv,
          mem_q_gain, mem_k_gain, w_out)
    ms = (m_norm_gain, m_w_in, m_gmlp_v_gain, m_gmlp_w_s, m_gmlp_b, m_attn_q_gain, m_attn_k_gain, m_mem_norm_gain,
          m_w_mem_kv, m_mem_q_gain, m_mem_k_gain, m_w_out)
    vs = (v_norm_gain, v_w_in, v_gmlp_v_gain, v_gmlp_w_s, v_gmlp_b, v_attn_q_gain, v_attn_k_gain, v_mem_norm_gain,
          v_w_mem_kv, v_mem_q_gain, v_mem_k_gain, v_w_out)
    form = {1: lambda a: a[0].T, 3: lambda a: a.reshape(4 * CHUNK, CHUNK), 2: lambda a: a[0], 4: lambda a: a[0],
            8: lambda a: a[0], 11: lambda a: a[0]}
    back = {1: lambda a: a.T[None], 3: lambda a: a.reshape(1, 4, CHUNK, CHUNK), 2: lambda a: a[None],
            4: lambda a: a[None], 8: lambda a: a[None], 11: lambda a: a[None]}
    fwd = lambda t, i: form.get(i, lambda a: a)(t[i])
    out = {}
    for i, g in ((1, g_wt_sh), (3, g_ws), (8, g_wkv_sh), (11, g_wo_sh)):
        out[i] = _adamw(fwd(ws, i), g, fwd(ms, i), fwd(vs, i))
    res = _adamw_tiny(tiny, [fwd(ws, i) for i in TINY_ORDER], [fwd(ms, i) for i in TINY_ORDER],
                      [fwd(vs, i) for i in TINY_ORDER])
    for k, i in enumerate(TINY_ORDER):
        out[i] = res[1 + 4 * k:5 + 4 * k]
    leaves = [[back.get(i, lambda a: a)(out[i][j]) for i in range(12)] for j in range(4)]
    return (res[0].reshape(()), gx[None], *leaves[0], *leaves[1], *leaves[2], *leaves[3])
```

```python
import math

import jax
import jax.numpy as jnp
from jax import lax
from jax.experimental import pallas as pl
from jax.experimental.pallas import tpu as pltpu

F32 = jnp.float32
BF16 = jnp.bfloat16

SEQ = 4096
D_MODEL = 1024
HEAD_DIM = 64
LANES = 128
CHUNK = 128
GMLP_W, ATTN_W, MEM_W = 256, 512, 256
IN_W = 3 * GMLP_W + 4 * ATTN_W + 2 * MEM_W
MEM_LEN = 256
DILATIONS = (16, 4, 1)
EPS = 1e-6
QK_SCALE = 1.0 / math.sqrt(HEAD_DIM)
C_GU, C_GV, C_GG, C_AQ, C_AK, C_AV, C_AG, C_MQ, C_MG = 0, 256, 512, 768, 1280, 1792, 2304, 2816, 3072

ADAM_LR, ADAM_B1, ADAM_B2, ADAM_EPS, ADAM_WD, ADAM_STEP = 0.001, 0.9, 0.999, 1e-08, 0.01, 10

VMEM_LIMIT = 48 * 1024 * 1024
RS_CHUNKS = 2
ATTN_UNROLL = 4
MESH = pl.DeviceIdType.MESH

TINY_ORDER = (0, 2, 4, 5, 6, 7, 9, 10)


def _call(body, **kw):
    return pl.pallas_call(body, **kw)


def _params(**kw):
    return pltpu.CompilerParams(vmem_limit_bytes=VMEM_LIMIT, **kw)


def _dot(a, b):
    return jnp.dot(a, b, preferred_element_type=F32)


def _dot_nt(a, b):
    return lax.dot_general(a, b, (((1,), (1,)), ((), ())), preferred_element_type=F32)


def _dot_tn(a, b):
    return lax.dot_general(a, b, (((0,), (0,)), ((), ())), preferred_element_type=F32)


def _head_blockdiag():
    r = lax.shift_right_logical(lax.broadcasted_iota(jnp.int32, (LANES, LANES), 0), 6)
    c = lax.shift_right_logical(lax.broadcasted_iota(jnp.int32, (LANES, LANES), 1), 6)
    return jnp.where(r == c, 1.0, 0.0).astype(BF16)


def _headsum(v, bd):
    hi = v.astype(BF16)
    lo = (v - hi.astype(F32)).astype(BF16)
    return _dot(hi, bd) + _dot(lo, bd)


def _lo_mask(rows):
    return lax.broadcasted_iota(jnp.int32, (rows, LANES), 1) < HEAD_DIM


def _sigmoid(x):
    return 1.0 / (1.0 + jnp.exp(-x))


def _fold_heads(v):
    return v + pltpu.roll(v, HEAD_DIM, 1)


def _put_rows(ref, vec, accumulate=False):
    for j in range(vec.shape[1] // LANES):
        piece = vec[:, j * LANES:(j + 1) * LANES]
        ref[j:j + 1, :] = ref[j:j + 1, :] + piece if accumulate else piece


def _gather_proj(x, gain, wt_sh):
    tm = 1024
    nrow = SEQ // tm
    widths = (768, 896, 768, 896)
    nunits = len(widths)
    pair = 2 * wt_sh.shape[0]
    assert pair % LANES == 0 and sum(widths[:2]) == pair

    def body(x_ref, g_ref, wt_sh_ref, proj_hbm, wt_hbm, h_scr, land, res, send_sems, recv_sems, out_sems, copy_sem):
        u, i = pl.program_id(0), pl.program_id(1)
        cx_, cy_ = lax.axis_index("x"), lax.axis_index("y")
        (send_own, pass_on_neighbours, pass_on_diagonal, _), (y_complete, x_complete, diagonal_complete, sends_done) = (
            _gather_stages((wt_sh_ref,), (land,), send_sems, recv_sems))
        first = lambda k: (u == k) & (i == 0)
        last = (u == nunits - 1) & (i == nrow - 1)
        to_hbm = pltpu.make_async_copy(land, wt_hbm, copy_sem)

        pl.when(first(0))(send_own)

        @pl.when(u == 0)
        def _():
            xv = x_ref[...]
            ms = jnp.mean(xv * xv, axis=-1, keepdims=True)
            h_scr[pl.ds(pl.multiple_of(i * tm, tm), tm), :] = (xv * lax.rsqrt(ms + EPS) * g_ref[...]).astype(BF16)

        @pl.when(first(1))
        def _():
            pass_on_neighbours()
            y_complete()

        @pl.when(first(2))
        def _():
            x_complete()
            pass_on_diagonal()

        @pl.when(first(3))
        def _():
            diagonal_complete()
            to_hbm.start()

        mine, other = pair * cx_, pair * (1 - cx_)
        col0 = (mine + 896 * cy_, mine + 768 * (1 - cy_), other + 896 * cy_, other + 768 * (1 - cy_))
        slot = i % 2
        rows = pl.ds(pl.multiple_of(i * tm, tm), tm)

        def writeback(k, rows_):
            c0 = pl.multiple_of(col0[k], LANES)
            return pltpu.make_async_copy(res.at[slot, :, pl.ds(0, widths[k])], proj_hbm.at[rows_, pl.ds(c0, widths[k])],
                                         out_sems.at[slot])

        for k in range(nunits):
            @pl.when(u == k)
            def _(k=k):
                pl.when(i >= 2)(writeback(k, rows).wait)
                if k > 0:
                    pl.when(i < 2)(writeback(k - 1, rows).wait)
                w_rows = land[pl.ds(pl.multiple_of(col0[k], LANES), widths[k]), :]
                res[slot, :, 0:widths[k]] = _dot_nt(h_scr[rows, :], w_rows)
                writeback(k, rows).start()

        @pl.when(last)
        def _():
            sends_done()
            to_hbm.wait()
            for s in range(2):
                pltpu.make_async_copy(res.at[s, :, pl.ds(0, widths[-1])], proj_hbm.at[rows, pl.ds(0, widths[-1])], out_sems.at[s]).wait()

    full = jax.ShapeDtypeStruct((4 * wt_sh.shape[0], wt_sh.shape[1]), BF16)
    hbm = pl.BlockSpec(memory_space=pl.ANY)
    return _call(
        body, name="gather_proj", grid=(nunits, nrow),
        in_specs=[pl.BlockSpec((tm, D_MODEL), lambda u, i: (jnp.where(u == 0, i, nrow - 1), 0)),
                  pl.BlockSpec((1, D_MODEL), lambda u, i: (0, 0)), pl.BlockSpec(wt_sh.shape, lambda u, i: (0, 0))],
        out_specs=[hbm, hbm],
        out_shape=[jax.ShapeDtypeStruct((SEQ, IN_W), F32), full],
        scratch_shapes=[pltpu.VMEM((SEQ, D_MODEL), BF16), pltpu.VMEM(full.shape, BF16), pltpu.VMEM((2, tm, max(widths)), F32),
                        pltpu.SemaphoreType.DMA((AG_SEMS,)), pltpu.SemaphoreType.DMA((AG_SEMS,)),
                        pltpu.SemaphoreType.DMA((2,)), pltpu.SemaphoreType.DMA],
        compiler_params=_params(),
    )(x, gain, wt_sh)


def _gmlp_weights(w_ref):
    ti = lax.broadcasted_iota(jnp.int32, (CHUNK, CHUNK), 0)
    si = lax.broadcasted_iota(jnp.int32, (CHUNK, CHUNK), 1)
    tril = si <= ti
    return tril, [jnp.where(tril, w_ref[h], 0.0).astype(BF16) for h in range(4)]


def _gmlp_fwd(proj, vgain, w_s, bias_full):
    tm = 1024

    def body(p_ref, vg_ref, w_ref, b_ref, y_ref):
        bd = _head_blockdiag()
        lo = _lo_mask(CHUNK)
        _, wm = _gmlp_weights(w_ref)
        units = [(pl.ds(c * CHUNK, CHUNK), p) for c in range(tm // CHUNK) for p in range(2)]
        col = lambda c0, p: slice(c0 + p * LANES, c0 + (p + 1) * LANES)
        vs = [p_ref[rows, col(C_GV, p)] for rows, p in units]
        rs = [lax.rsqrt(_headsum(v * v, bd) * (1.0 / HEAD_DIM) + EPS) for v in vs]
        vns = [(v * r * vg_ref[:, col(0, p)]).astype(BF16) for v, r, (_, p) in zip(vs, rs, units)]
        sps = [jnp.where(lo, _dot(wm[2 * p], vn), _dot(wm[2 * p + 1], vn)) + b_ref[:, col(0, p)] for vn, (_, p) in zip(vns, units)]
        for sp, (rows, p) in zip(sps, units):
            gt = p_ref[rows, col(C_GG, p)]
            y_ref[rows, col(0, p)] = (p_ref[rows, col(C_GU, p)] * sp * (gt * _sigmoid(gt))).astype(BF16)

    return _call(
        body, name="gmlp_fwd", grid=(SEQ // tm,),
        in_specs=[pl.BlockSpec((tm, 3 * GMLP_W), lambda i: (i, 0)),
                  pl.BlockSpec((1, GMLP_W), lambda i: (0, 0)),
                  pl.BlockSpec((4, CHUNK, CHUNK), lambda i: (0, 0, 0)),
                  pl.BlockSpec((CHUNK, GMLP_W), lambda i: (0, 0))],
        out_specs=pl.BlockSpec((tm, GMLP_W), lambda i: (i, 0)),
        out_shape=jax.ShapeDtypeStruct((SEQ, GMLP_W), BF16),
        compiler_params=_params(),
    )(proj, vgain, w_s, bias_full)


def _gmlp_bwd(proj, dyc, vgain, w_s, bias_full):
    tm = 1024
    nsteps = SEQ // tm

    def body(p_ref, dy_ref, vg_ref, w_ref, b_ref, dg_ref, gw_ref, gb_ref, gv_ref):
        i = pl.program_id(0)
        bd = _head_blockdiag()
        lo = _lo_mask(CHUNK)
        tril, wm = _gmlp_weights(w_ref)
        ri = lax.broadcasted_iota(jnp.int32, (16, LANES), 0)
        li = lax.broadcasted_iota(jnp.int32, (16, LANES), 1)
        head_rows = [jnp.where(((ri == 2 * p) & (li < HEAD_DIM)) | ((ri == 2 * p + 1) & (li >= HEAD_DIM)), 1.0, 0.0).astype(BF16)
                     for p in range(2)]

        @pl.when(i == 0)
        def _():
            gw_ref[...] = jnp.zeros_like(gw_ref)
            gb_ref[...] = jnp.zeros_like(gb_ref)
            gv_ref[...] = jnp.zeros_like(gv_ref)

        units = [(pl.ds(c * CHUNK, CHUNK), p) for c in range(tm // CHUNK) for p in range(2)]
        col = lambda c0, p: slice(c0 + p * LANES, c0 + (p + 1) * LANES)
        vs = [p_ref[rows, col(C_GV, p)] for rows, p in units]
        rs = [lax.rsqrt(_headsum(v * v, bd) * (1.0 / HEAD_DIM) + EPS) for v in vs]
        zs = [v * r for v, r in zip(vs, rs)]
        vns = [(z * vg_ref[:, col(0, p)]).astype(BF16) for z, (_, p) in zip(zs, units)]
        sps = [jnp.where(lo, _dot(wm[2 * p], vn), _dot(wm[2 * p + 1], vn)) + b_ref[:, col(0, p)] for vn, (_, p) in zip(vns, units)]
        dsps = []
        for sp, (rows, p) in zip(sps, units):
            u = p_ref[rows, col(C_GU, p)]
            gt = p_ref[rows, col(C_GG, p)]
            dy = dy_ref[rows, col(0, p)]
            sg = _sigmoid(gt)
            sl = gt * sg
            dg_ref[rows, col(C_GU, p)] = (dy * sp * sl).astype(BF16)
            dg_ref[rows, col(C_GG, p)] = (dy * u * sp * (sg * (1.0 + gt * (1.0 - sg)))).astype(BF16)
            dsps.append(dy * u * sl)
        dspbs = [dsp.astype(BF16) for dsp in dsps]
        dvns = [jnp.where(lo, _dot_tn(wm[2 * p], dspb), _dot_tn(wm[2 * p + 1], dspb)) for dspb, (_, p) in zip(dspbs, units)]
        gws = [(_dot_nt(jnp.where(lo, dsp, 0.0).astype(BF16), vn), _dot_nt(jnp.where(lo, 0.0, dsp).astype(BF16), vn))
               for dsp, vn in zip(dsps, vns)]
        gbs = [(_dot_nt(head_rows[p], dspb) + _dot_nt(head_rows[p], (dsp - dspb.astype(F32)).astype(BF16)))[0:8]
               for dsp, dspb, (_, p) in zip(dsps, dspbs, units)]
        for p in range(2):
            mine = [n for n, (_, q) in enumerate(units) if q == p]
            gw_ref[2 * p] += sum(gws[n][0] for n in mine)
            gw_ref[2 * p + 1] += sum(gws[n][1] for n in mine)
            gvp = sum(jnp.sum(dvns[n] * zs[n], axis=0, keepdims=True) for n in mine)
            gv_ref[2 * p:2 * p + 1, :] += gvp
            gv_ref[2 * p + 1:2 * p + 2, :] += pltpu.roll(gvp, HEAD_DIM, 1)
        gb_ref[...] += sum(gbs)
        for dvn, z, r, (rows, p) in zip(dvns, zs, rs, units):
            dz = dvn * vg_ref[:, col(0, p)]
            dg_ref[rows, col(C_GV, p)] = (r * (dz - z * (_headsum(dz * z, bd) * (1.0 / HEAD_DIM)))).astype(BF16)

        @pl.when(i == nsteps - 1)
        def _():
            for h in range(4):
                gw_ref[h] = jnp.where(tril, gw_ref[h], 0.0)

    return _call(
        body, name="gmlp_bwd", grid=(nsteps,),
        in_specs=[pl.BlockSpec((tm, 3 * GMLP_W), lambda i: (i, 0)),
                  pl.BlockSpec((tm, GMLP_W), lambda i: (i, 0)),
                  pl.BlockSpec((1, GMLP_W), lambda i: (0, 0)),
                  pl.BlockSpec((4, CHUNK, CHUNK), lambda i: (0, 0, 0)),
                  pl.BlockSpec((CHUNK, GMLP_W), lambda i: (0, 0))],
        out_specs=[pl.BlockSpec((tm, 3 * GMLP_W), lambda i: (i, 0)),
                   pl.BlockSpec((4, CHUNK, CHUNK), lambda i: (0, 0, 0)),
                   pl.BlockSpec((8, LANES), lambda i: (0, 0)),
                   pl.BlockSpec((8, LANES), lambda i: (0, 0))],
        out_shape=[jax.ShapeDtypeStruct((SEQ, 3 * GMLP_W), BF16),
                   jax.ShapeDtypeStruct((4, CHUNK, CHUNK), F32),
                   jax.ShapeDtypeStruct((8, LANES), F32),
                   jax.ShapeDtypeStruct((8, LANES), F32)],
        compiler_params=_params(),
    )(proj, dyc, vgain, w_s, bias_full)


def _band_masks():
    qi = lax.broadcasted_iota(jnp.int32, (CHUNK, 2 * CHUNK), 0)
    kj = lax.broadcasted_iota(jnp.int32, (CHUNK, 2 * CHUNK), 1)
    valid2 = ((kj < CHUNK) & (kj >= qi)) | ((kj >= CHUNK) & (kj - CHUNK <= qi))
    q1 = lax.broadcasted_iota(jnp.int32, (CHUNK, CHUNK), 0)
    k1 = lax.broadcasted_iota(jnp.int32, (CHUNK, CHUNK), 1)
    return k1 <= q1, valid2


def _stack_heads(v, lo):
    return jnp.concatenate([jnp.where(lo, v, 0.0), jnp.where(lo, 0.0, v)], axis=0).astype(BF16)


def _rows_of(ref, start, d):
    if d == 1:
        return ref.at[pl.ds(start if isinstance(start, int) else pl.multiple_of(start, CHUNK), CHUNK), :]
    return ref.at[pl.ds(start, CHUNK, stride=d), :]


def _unrolled(lo, hi, unroll, run):
    groups = (hi - lo) // unroll
    if groups:
        def body(g, carry):
            run([lo + g * unroll + t for t in range(unroll)])
            return carry

        lax.fori_loop(0, groups, body, 0)
    if lo + groups * unroll < hi:
        run(range(lo + groups * unroll, hi))


def _for_blocks(d, group_fn, unroll):
    nblk = SEQ // CHUNK
    sh = d.bit_length() - 1

    def first(j):
        return (j * CHUNK if d == 1 else j, None)

    def rest(j):
        start = (j & (d - 1)) + (j >> sh) * (CHUNK * d)
        return (start, start - CHUNK * d)

    _unrolled(0, d, unroll, lambda js: group_fn(d, [first(j) for j in js]))
    _unrolled(d, nblk, unroll, lambda js: group_fn(d, [rest(j) for j in js]))


def _attn_fwd(proj, gq2, gk2, *ride_along):
    tn = 512
    npairs = ATTN_W // LANES
    nride = len(ride_along)

    def body(q_ref, k_ref, v_ref, g_ref, gq_ref, gk_ref, *rest):
        shards, rest = rest[:nride], rest[nride:]
        o_ref, l_ref, ya_ref = rest[:3]
        gathered, rest = rest[3:3 + nride], rest[3 + nride:]
        qn_ref, kn_ref = rest[:2]
        lands, (send_sems, recv_sems, copy_sems) = rest[2:2 + nride], rest[2 + nride:]
        pair = pl.program_id(0)
        ride = _gather_stages(shards, lands, send_sems, recv_sems)[0]
        for step in range(npairs):
            pl.when(pair == step)(ride[step])
        bd = _head_blockdiag()
        lo = _lo_mask(CHUNK)
        valid1, valid2 = _band_masks()

        def norm(t, carry):
            rows = pl.ds(pl.multiple_of(t * tn, tn), tn)
            q, k = q_ref[rows, :], k_ref[rows, :]
            ssq = [_headsum(a * a, bd) for a in (q, k)]
            qn_ref[rows, :] = q * lax.rsqrt(ssq[0] * (1.0 / HEAD_DIM) + EPS) * (gq_ref[...] * QK_SCALE)
            kn_ref[rows, :] = k * lax.rsqrt(ssq[1] * (1.0 / HEAD_DIM) + EPS) * gk_ref[...]
            return carry

        lax.fori_loop(0, SEQ // tn, norm, 0)

        def load_kv(ref, d, start, prev):
            own = _rows_of(ref, start, d)[...]
            if prev is None:
                return own.astype(BF16)
            return jnp.concatenate([_rows_of(ref, prev, d)[...], own], axis=0).astype(BF16)

        def group(d, blocks):
            valid = valid1 if blocks[0][1] is None else valid2
            valid = jnp.concatenate([valid, valid], axis=0)
            qs = [_rows_of(qn_ref, start, d)[...] for start, _ in blocks]
            ks = [load_kv(kn_ref, d, start, prev) for start, prev in blocks]
            vs = [load_kv(v_ref, d, start, prev) for start, prev in blocks]
            ss = [_dot_nt(_stack_heads(q, lo), k) for q, k in zip(qs, ks)]
            ms, ps, ls = [], [], []
            for s in ss:
                s = jnp.where(valid, s, -jnp.inf)
                m = jnp.max(s, axis=-1, keepdims=True)
                p = jnp.exp(s - m)
                ms.append(m)
                ls.append(jnp.sum(p, axis=-1, keepdims=True))
                ps.append(p.astype(BF16))
            os_ = [_dot(p, v) for p, v in zip(ps, vs)]
            for b, (start, _) in enumerate(blocks):
                heads = lambda v: jnp.where(lo, v[:CHUNK], v[CHUNK:])
                lsum = heads(ls[b])
                ob = heads(os_[b]) * (1.0 / lsum)
                lb = heads(ms[b]) + jnp.log(lsum)
                o_rows = _rows_of(o_ref, start, d)
                l_rows = _rows_of(l_ref, start, d)
                if d != DILATIONS[0]:
                    lold = l_rows[...]
                    mx = jnp.maximum(lold, lb)
                    ea = jnp.exp(lold - mx)
                    eb = jnp.exp(lb - mx)
                    inv = 1.0 / (ea + eb)
                    ob = o_rows[...] * (ea * inv) + ob * (eb * inv)
                    lb = mx + jnp.log(ea + eb)
                o_rows[...] = ob
                l_rows[...] = lb

        for d in DILATIONS:
            _for_blocks(d, group, ATTN_UNROLL)

        def fin(t, carry):
            rows = pl.ds(pl.multiple_of(t * tn, tn), tn)
            g = g_ref[rows, :]
            ya_ref[rows, :] = (o_ref[rows, :] * (g * _sigmoid(g))).astype(BF16)
            return carry

        lax.fori_loop(0, SEQ // tn, fin, 0)

        @pl.when(pair == npairs - 1)
        def _():
            to_hbm = [pltpu.make_async_copy(land, out, copy_sems.at[n]) for n, (land, out) in enumerate(zip(lands, gathered))]
            for cp in to_hbm:
                cp.start()
            for cp in to_hbm:
                cp.wait()

    col = lambda c0: pl.BlockSpec((SEQ, LANES), lambda p: (0, c0 // LANES + p))
    vec = pl.BlockSpec((1, LANES), lambda p: (0, 0))
    out = pl.BlockSpec((SEQ, LANES), lambda p: (0, p))
    full = [jax.ShapeDtypeStruct((4 * a.shape[0], a.shape[1]), BF16) for a in ride_along]
    return _call(
        body, name="attn_fwd", grid=(npairs,),
        in_specs=[col(C_AQ), col(C_AK), col(C_AV), col(C_AG), vec, vec]
        + [pl.BlockSpec(a.shape, lambda p: (0, 0)) for a in ride_along],
        out_specs=[out, out, out] + [pl.BlockSpec(memory_space=pl.ANY)] * nride,
        out_shape=[jax.ShapeDtypeStruct((SEQ, ATTN_W), F32), jax.ShapeDtypeStruct((SEQ, ATTN_W), F32),
                   jax.ShapeDtypeStruct((SEQ, ATTN_W), BF16)] + full,
        scratch_shapes=[pltpu.VMEM((SEQ, LANES), F32), pltpu.VMEM((SEQ, LANES), F32)]
        + [pltpu.VMEM(s.shape, BF16) for s in full]
        + [pltpu.SemaphoreType.DMA((AG_SEMS * nride,)), pltpu.SemaphoreType.DMA((AG_SEMS * nride,)),
           pltpu.SemaphoreType.DMA((nride,))],
        compiler_params=_params(),
    )(proj, proj, proj, proj, gq2, gk2, *ride_along)


def _attn_bwd(proj, o, lse, dyc, gq2, gk2, *ride_along):
    tn = 2048
    npairs = ATTN_W // LANES
    nride = len(ride_along)
    nbufs = nride * len(RS_KINDS)

    def body(proj_hbm, o_hbm, l_hbm, dyc_hbm, gq_ref, gk_ref, *rest):
        ride_in, rest = rest[:nride], rest[nride:]
        dq_ref, dk_ref, dv_ref, dgt_ref, gqg_ref, gkg_ref = rest[:6]
        ride_out, rest = rest[6:6 + nride], rest[6 + nride:]
        qb_, kb_, vb_, gb_, ob_, lb_, yb_, dkb_, dvb_, sems = rest[:10]
        rs_bufs, (send_sems, recv_sems, local_sems) = rest[10:10 + nbufs], rest[10 + nbufs:]
        rs_stage = _rs_stages(ride_in, ride_out, rs_bufs, send_sems, recv_sems, local_sems, [g.shape[1] for g in ride_along])
        pair = pl.program_id(0)
        for step in range(npairs):
            pl.when(pair == step)(rs_stage[step])
        bd = _head_blockdiag()
        lo = _lo_mask(CHUNK)
        lo2 = lax.broadcasted_iota(jnp.int32, (2 * CHUNK, LANES), 1) < HEAD_DIM
        valid1, valid2 = _band_masks()
        gqs = gq_ref[...] * QK_SCALE
        gk = gk_ref[...]

        def pcol(c0, of=None):
            return acol(proj_hbm, c0, of)

        def acol(hbm, c0=0, of=None):
            of = pair if of is None else of
            return hbm.at[:, pl.ds(pl.multiple_of(c0 + of * LANES, LANES), LANES)]

        def input_loads(of):
            return [pltpu.make_async_copy(src, dst, sems.at[n]) for n, (src, dst) in enumerate((
                (pcol(C_AQ, of), qb_), (pcol(C_AK, of), kb_), (pcol(C_AG, of), gb_), (acol(o_hbm, 0, of), ob_),
                (acol(dyc_hbm, GMLP_W, of), yb_), (pcol(C_AV, of), vb_), (acol(l_hbm, 0, of), lb_)))]

        early = (0, 1, 3, 4)
        loads = input_loads(pair)
        for n, cp in enumerate(loads):
            if n in early:
                pl.when(pair == 0)(cp.start)
            else:
                cp.start()

        @pl.when(pair == 0)
        def _():
            gqg_ref[...] = jnp.zeros_like(gqg_ref)
            gkg_ref[...] = jnp.zeros_like(gkg_ref)

        def pre_qk(t, carry):
            rows = pl.ds(pl.multiple_of(t * tn, tn), tn)
            q, k = qb_[rows, :], kb_[rows, :]
            ssq = [_headsum(a * a, bd) for a in (q, k)]
            qb_[rows, :] = q * lax.rsqrt(ssq[0] * (1.0 / HEAD_DIM) + EPS) * gqs
            kb_[rows, :] = k * lax.rsqrt(ssq[1] * (1.0 / HEAD_DIM) + EPS) * gk
            return carry

        def pre_gate(t, carry):
            rows = pl.ds(pl.multiple_of(t * tn, tn), tn)
            g = gb_[rows, :]
            ov = ob_[rows, :]
            dya = yb_[rows, :]
            sg = _sigmoid(g)
            dgt_ref[rows, :] = (dya * ov * (sg * (1.0 + g * (1.0 - sg)))).astype(BF16)
            do = dya * (g * sg)
            yb_[rows, :] = do
            ob_[rows, :] = jnp.where(first_half, lb_[rows, :], _headsum(do * ov, bd))
            return carry

        first_half = (lax.broadcasted_iota(jnp.int32, (tn, LANES), 1) & (HEAD_DIM - 1)) < HEAD_DIM // 2
        loads[0].wait()
        loads[1].wait()
        lax.fori_loop(0, SEQ // tn, pre_qk, 0)
        for cp in loads[2:5] + loads[6:7]:
            cp.wait()
        lax.fori_loop(0, SEQ // tn, pre_gate, 0)
        loads[5].wait()
        reloads = [pltpu.make_async_copy(pcol(C_AQ), lb_, sems.at[7]), pltpu.make_async_copy(pcol(C_AK), vb_, sems.at[8])]
        reloads[0].start()

        def load_kv(ref, d, start, prev):
            own = _rows_of(ref, start, d)[...]
            if prev is None:
                return own.astype(BF16)
            return jnp.concatenate([_rows_of(ref, prev, d)[...], own], axis=0).astype(BF16)

        def group(d, blocks):
            first = blocks[0][1] is None
            valid, lok = (valid1, lo) if first else (valid2, lo2)
            chains = [(b, h) for b in range(len(blocks)) for h in range(2)]
            mask = lambda h: lo if h == 0 else ~lo
            qs = [_rows_of(qb_, start, d)[...] for start, _ in blocks]
            dos = [_rows_of(yb_, start, d)[...] for start, _ in blocks]
            lds = [_rows_of(ob_, start, d)[...] for start, _ in blocks]
            ks = [load_kv(kb_, d, start, prev) for start, prev in blocks]
            vs = [load_kv(vb_, d, start, prev) for start, prev in blocks]
            qbs = [q.astype(BF16) for q in qs]
            dobs = [do.astype(BF16) for do in dos]
            ss = [_dot_nt(jnp.where(mask(h), qs[b], 0.0).astype(BF16), ks[b]) for b, h in chains]
            dps = [_dot_nt(jnp.where(mask(h), dos[b], 0.0).astype(BF16), vs[b]) for b, h in chains]
            pbs, dss = [], []
            for s, dp, (b, h) in zip(ss, dps, chains):
                hc, dc = h * HEAD_DIM, h * HEAD_DIM + HEAD_DIM // 2
                p = jnp.exp(jnp.where(valid, s, -jnp.inf) - lds[b][:, hc:hc + 1])
                pbs.append(p.astype(BF16))
                dss.append((p * (dp - lds[b][:, dc:dc + 1])).astype(BF16))
            dqs = [_dot(ds, ks[b]) for ds, (b, h) in zip(dss, chains)]
            dks = [_dot_tn(ds, qbs[b]) for ds, (b, h) in zip(dss, chains)]
            dvs = [_dot_tn(p, dobs[b]) for p, (b, h) in zip(pbs, chains)]
            assign = d == DILATIONS[0]
            for b, (start, prev) in enumerate(blocks):
                c0, c1 = 2 * b, 2 * b + 1
                dq_rows = _rows_of(gb_, start, d)
                dqb = jnp.where(lo, dqs[c0], dqs[c1])
                dq_rows[...] = dqb if assign else dq_rows[...] + dqb
                dkc = jnp.where(lok, dks[c0], dks[c1])
                dvc = jnp.where(lok, dvs[c0], dvs[c1])
                spans = ((start, slice(0, CHUNK), True),) if first else (
                    (prev, slice(0, CHUNK), False), (start, slice(CHUNK, 2 * CHUNK), True))
                for st, sl, own in spans:
                    dk_rows = _rows_of(dkb_, st, d)
                    dv_rows = _rows_of(dvb_, st, d)
                    if assign and own:
                        dk_rows[...] = dkc[sl]
                        dv_rows[...] = dvc[sl]
                    else:
                        dk_rows[...] = dk_rows[...] + dkc[sl]
                        dv_rows[...] = dv_rows[...] + dvc[sl]

        for d in DILATIONS:
            _for_blocks(d, group, ATTN_UNROLL)

        reloads[1].start()

        @pl.when(pair < npairs - 1)
        def _():
            nxt = input_loads(pair + 1)
            for n in early:
                nxt[n].start()

        for cp in reloads:
            cp.wait()

        def post(t, carry):
            gq_acc, gk_acc = carry
            rows = pl.ds(pl.multiple_of(t * tn, tn), tn)
            raws = [lb_[rows, :], vb_[rows, :]]
            dns = [gb_[rows, :], dkb_[rows, :]]
            rs = [lax.rsqrt(_headsum(a * a, bd) * (1.0 / HEAD_DIM) + EPS) for a in raws]
            zs = [a * r for a, r in zip(raws, rs)]
            dzs = [dn * gain for dn, gain in zip(dns, (gqs, gk))]
            means = [_headsum(dz * z, bd) * (1.0 / HEAD_DIM) for dz, z in zip(dzs, zs)]
            dq, dk = [r * (dz - z * mean) for r, dz, z, mean in zip(rs, dzs, zs, means)]
            gq, gkk = [jnp.sum(dn * z, axis=0, keepdims=True) for dn, z in zip(dns, zs)]
            dq_ref[rows, :] = dq.astype(BF16)
            dk_ref[rows, :] = dk.astype(BF16)
            dv_ref[rows, :] = dvb_[rows, :].astype(BF16)
            return gq_acc + gq * QK_SCALE, gk_acc + gkk

        zero = jnp.zeros((1, LANES), F32)
        gq_acc, gk_acc = lax.fori_loop(0, SEQ // tn, post, (zero, zero))
        gqg_ref[0:1, :] += gq_acc
        gkg_ref[0:1, :] += gk_acc

        @pl.when(pair == npairs - 1)
        def _():
            gqg_ref[0:1, :] = _fold_heads(gqg_ref[0:1, :])
            gkg_ref[0:1, :] = _fold_heads(gkg_ref[0:1, :])
            rs_stage[npairs]()

    hbm = pl.BlockSpec(memory_space=pl.ANY)
    vec = pl.BlockSpec((1, LANES), lambda p: (0, 0))
    blk8 = pl.BlockSpec((8, LANES), lambda p: (0, 0))
    out = pl.BlockSpec((SEQ, LANES), lambda p: (0, p))
    big = jax.ShapeDtypeStruct((SEQ, ATTN_W), BF16)
    nsem = RS_SEMS * nride
    return _call(
        body, name="attn_bwd", grid=(npairs,),
        in_specs=[hbm, hbm, hbm, hbm, vec, vec] + [hbm] * nride,
        out_specs=[out, out, out, out, blk8, blk8] + [hbm] * nride,
        out_shape=[big, big, big, big, jax.ShapeDtypeStruct((8, LANES), F32), jax.ShapeDtypeStruct((8, LANES), F32)]
        + [jax.ShapeDtypeStruct((2, g.shape[0] // 8, g.shape[1]), F32) for g in ride_along],
        scratch_shapes=[pltpu.VMEM((SEQ, LANES), F32) for _ in range(9)] + [pltpu.SemaphoreType.DMA((9,))]
        + _rs_scratch([g.shape for g in ride_along]) + [pltpu.SemaphoreType.DMA((nsem,)), pltpu.SemaphoreType.DMA((nsem,)),
                                     pltpu.SemaphoreType.DMA((nride,))],
        compiler_params=_params(),
    )(proj, o, lse, dyc, gq2, gk2, *[_rs_view(g) for g in ride_along])


def _mem_kv(mem, gain, wkv):
    def body(m_ref, g_ref, w_ref, kv_ref, hm_ref):
        mv = m_ref[...]
        ms = jnp.mean(mv * mv, axis=-1, keepdims=True)
        hm = (mv * lax.rsqrt(ms + EPS) * g_ref[...]).astype(BF16)
        hm_ref[...] = hm
        kv_ref[...] = _dot(hm, w_ref[...])

    return _call(
        body, name="mem_kv",
        out_shape=[jax.ShapeDtypeStruct((MEM_LEN, 2 * MEM_W), F32), jax.ShapeDtypeStruct((MEM_LEN, D_MODEL), BF16)],
        compiler_params=_params(),
    )(mem, gain, wkv)


def _mem_keys(kv_ref, kg_ref, bd, p):
    mk = kv_ref[:, p * LANES:(p + 1) * LANES]
    r = lax.rsqrt(_headsum(mk * mk, bd) * (1.0 / HEAD_DIM) + EPS)
    z = mk * r
    mkn = (z * kg_ref[:, p * LANES:(p + 1) * LANES]).astype(BF16)
    mvp = kv_ref[:, MEM_W + p * LANES:MEM_W + (p + 1) * LANES].astype(BF16)
    return mkn, mvp, r, z


def _mem_fwd(proj, kv, qg4, kg4):
    tm = 1024

    def body(q_ref, g_ref, kv_ref, qg_ref, kg_ref, om_ref, ym_ref):
        bd = _head_blockdiag()
        lo = _lo_mask(tm)
        keys, qns = [], []
        for p in range(2):
            cs = slice(p * LANES, (p + 1) * LANES)
            keys.append(_mem_keys(kv_ref, kg_ref, bd, p)[:2])
            q = q_ref[:, cs]
            qns.append(q * lax.rsqrt(_headsum(q * q, bd) * (1.0 / HEAD_DIM) + EPS) * (qg_ref[:, cs] * QK_SCALE))
        chains = [(p, h) for p in range(2) for h in range(2)]
        ss = [_dot_nt(jnp.where(lo if h == 0 else ~lo, qns[p], 0.0).astype(BF16), keys[p][0]) for p, h in chains]
        es = [jnp.exp(s - jnp.max(s, axis=-1, keepdims=True)) for s in ss]
        os_ = [_dot(e.astype(BF16), keys[p][1]) for e, (p, h) in zip(es, chains)]
        res = [o * (1.0 / jnp.sum(e, axis=-1, keepdims=True)) for o, e in zip(os_, es)]
        for p in range(2):
            cs = slice(p * LANES, (p + 1) * LANES)
            ov = jnp.where(lo, res[2 * p], res[2 * p + 1])
            g = g_ref[:, cs]
            om_ref[:, cs] = ov
            ym_ref[:, cs] = (ov * (g * _sigmoid(g))).astype(BF16)

    vec = pl.BlockSpec((1, MEM_W), lambda i: (0, 0))
    return _call(
        body, name="mem_fwd", grid=(SEQ // tm,),
        in_specs=[pl.BlockSpec((tm, MEM_W), lambda i: (i, C_MQ // MEM_W)),
                  pl.BlockSpec((tm, MEM_W), lambda i: (i, C_MG // MEM_W)),
                  pl.BlockSpec((MEM_LEN, 2 * MEM_W), lambda i: (0, 0)), vec, vec],
        out_specs=[pl.BlockSpec((tm, MEM_W), lambda i: (i, 0)), pl.BlockSpec((tm, MEM_W), lambda i: (i, 0))],
        out_shape=[jax.ShapeDtypeStruct((SEQ, MEM_W), F32), jax.ShapeDtypeStruct((SEQ, MEM_W), BF16)],
        compiler_params=_params(),
    )(proj, proj, kv, qg4, kg4)


def _mem_bwd(proj, om, dyc, kv, hm, mem, mgain, wkv, qg4, kg4):
    tm = 1024
    nsteps = SEQ // tm

    def body(q_ref, g_ref, om_ref, dy_ref, kv_ref, hm_ref, mem_ref, mg_ref, w_ref, qg_ref, kg_ref,
             dq_ref, dgt_ref, gqg_ref, gkg_ref, gw_ref, gmg_ref, dmk_ref, dmv_ref, gq_acc):
        i = pl.program_id(0)
        bd = _head_blockdiag()
        lo = _lo_mask(tm)
        lom = _lo_mask(MEM_LEN)

        @pl.when(i == 0)
        def _():
            dmk_ref[...] = jnp.zeros_like(dmk_ref)
            dmv_ref[...] = jnp.zeros_like(dmv_ref)
            gq_acc[...] = jnp.zeros_like(gq_acc)

        pairs = []
        for p in range(2):
            cs = slice(p * LANES, (p + 1) * LANES)
            mkn, mvp, _, _ = _mem_keys(kv_ref, kg_ref, bd, p)
            gqs = qg_ref[:, cs] * QK_SCALE
            q = q_ref[:, cs]
            r = lax.rsqrt(_headsum(q * q, bd) * (1.0 / HEAD_DIM) + EPS)
            z = q * r
            qn = z * gqs
            g = g_ref[:, cs]
            ov = om_ref[:, cs]
            dym = dy_ref[:, cs]
            sg = _sigmoid(g)
            dgt_ref[:, cs] = (dym * ov * (sg * (1.0 + g * (1.0 - sg)))).astype(BF16)
            do = dym * (g * sg)
            pairs.append(dict(cs=cs, mkn=mkn, mvp=mvp, gqs=gqs, r=r, z=z, qn=qn, qnb=qn.astype(BF16), do=do,
                              dob=do.astype(BF16), delta=_headsum(do * ov, bd)))
        chains = [(pr_, h) for pr_ in pairs for h in range(2)]
        mask = lambda h: lo if h == 0 else ~lo
        ss = [_dot_nt(jnp.where(mask(h), c["qn"], 0.0).astype(BF16), c["mkn"]) for c, h in chains]
        dps = [_dot_nt(jnp.where(mask(h), c["do"], 0.0).astype(BF16), c["mvp"]) for c, h in chains]
        prs, dss = [], []
        for s, dp, (c, h) in zip(ss, dps, chains):
            e = jnp.exp(s - jnp.max(s, axis=-1, keepdims=True))
            pr = e * (1.0 / jnp.sum(e, axis=-1, keepdims=True))
            prs.append(pr.astype(BF16))
            dss.append((pr * (dp - c["delta"][:, h * HEAD_DIM:h * HEAD_DIM + 1])).astype(BF16))
        dqs = [_dot(ds, c["mkn"]) for ds, (c, h) in zip(dss, chains)]
        dks = [_dot_tn(ds, c["qnb"]) for ds, (c, h) in zip(dss, chains)]
        dvs = [_dot_tn(pr, c["dob"]) for pr, (c, h) in zip(prs, chains)]
        for p, c in enumerate(pairs):
            cs, z, r = c["cs"], c["z"], c["r"]
            dqn = jnp.where(lo, dqs[2 * p], dqs[2 * p + 1])
            dmk_ref[:, cs] += jnp.where(lom, dks[2 * p], dks[2 * p + 1])
            dmv_ref[:, cs] += jnp.where(lom, dvs[2 * p], dvs[2 * p + 1])
            dz = dqn * c["gqs"]
            dq_ref[:, cs] = (r * (dz - z * (_headsum(dz * z, bd) * (1.0 / HEAD_DIM)))).astype(BF16)
            gq_acc[:, cs] += jnp.sum(dqn * z, axis=0, keepdims=True) * QK_SCALE

        @pl.when(i == nsteps - 1)
        def _():
            gqg_ref[...] = jnp.zeros_like(gqg_ref)
            gkg_ref[...] = jnp.zeros_like(gkg_ref)
            gqg_ref[0:1, :] = _fold_heads(gq_acc[:, 0:LANES] + gq_acc[:, LANES:2 * LANES])
            dkv = []
            gk = jnp.zeros((1, LANES), F32)
            for p in range(2):
                cs = slice(p * LANES, (p + 1) * LANES)
                _, _, r, z = _mem_keys(kv_ref, kg_ref, bd, p)
                dn = dmk_ref[:, cs]
                dz = dn * kg_ref[:, cs]
                gk = gk + jnp.sum(dn * z, axis=0, keepdims=True)
                dkv.append(r * (dz - z * (_headsum(dz * z, bd) * (1.0 / HEAD_DIM))))
            gkg_ref[0:1, :] = _fold_heads(gk)
            dkvb = jnp.concatenate(dkv + [dmv_ref[...]], axis=1).astype(BF16)
            gw_ref[...] = _dot_tn(hm_ref[...], dkvb)
            dhm = _dot_nt(dkvb, w_ref[...])
            mv = mem_ref[...]
            zm = mv * lax.rsqrt(jnp.mean(mv * mv, axis=-1, keepdims=True) + EPS)
            _put_rows(gmg_ref, jnp.sum(dhm * zm, axis=0, keepdims=True))

    const = lambda shape: pl.BlockSpec(shape, lambda i: (0,) * len(shape))
    row = lambda j: pl.BlockSpec((tm, MEM_W), lambda i: (i, j))
    blk8 = jax.ShapeDtypeStruct((8, LANES), F32)
    return _call(
        body, name="mem_bwd", grid=(nsteps,),
        in_specs=[row(C_MQ // MEM_W), row(C_MG // MEM_W), row(0), row((GMLP_W + ATTN_W) // MEM_W),
                  const((MEM_LEN, 2 * MEM_W)), const((MEM_LEN, D_MODEL)), const((MEM_LEN, D_MODEL)),
                  const((1, D_MODEL)), const((D_MODEL, 2 * MEM_W)), const((1, MEM_W)), const((1, MEM_W))],
        out_specs=[row(0), row(0), const((8, LANES)), const((8, LANES)),
                   const((D_MODEL, 2 * MEM_W)), const((8, LANES))],
        out_shape=[jax.ShapeDtypeStruct((SEQ, MEM_W), BF16), jax.ShapeDtypeStruct((SEQ, MEM_W), BF16),
                   blk8, blk8, jax.ShapeDtypeStruct((D_MODEL, 2 * MEM_W), F32), blk8],
        scratch_shapes=[pltpu.VMEM((MEM_LEN, MEM_W), F32), pltpu.VMEM((MEM_LEN, MEM_W), F32),
                        pltpu.VMEM((1, MEM_W), F32)],
        compiler_params=_params(),
    )(proj, proj, om, dyc, kv, hm, mem, mgain, wkv, qg4, kg4)


def _out_loss(yg, ya, ym, x, tgt, wo):
    tm = 512
    nsteps = SEQ // tm
    parts = ((0, GMLP_W), (GMLP_W, ATTN_W), (GMLP_W + ATTN_W, MEM_W))

    def body(yg_ref, ya_ref, ym_ref, x_ref, t_ref, w_ref, dy_ref, dyc_ref, gw_ref, ls_ref):
        i = pl.program_id(0)

        @pl.when(i == 0)
        def _():
            gw_ref[...] = jnp.zeros_like(gw_ref)
            ls_ref[...] = jnp.zeros_like(ls_ref)

        ys = (yg_ref[...], ya_ref[...], ym_ref[...])
        y = sum(_dot(yv, w_ref[r0:r0 + n, :]) for yv, (r0, n) in zip(ys, parts))
        err = x_ref[...] + y - t_ref[...]
        _put_rows(ls_ref, jnp.sum(err * err, axis=0, keepdims=True), accumulate=True)
        dy = err * (1.0 / D_MODEL)
        dy_ref[...] = dy
        dyb = dy.astype(BF16)
        dyc_ref[...] = _dot_nt(dyb, w_ref[...])
        for yv, (r0, n) in zip(ys, parts):
            gw_ref[r0:r0 + n, :] += _dot_tn(yv, dyb)

    row = lambda w: pl.BlockSpec((tm, w), lambda i: (i, 0))
    const = lambda shape: pl.BlockSpec(shape, lambda i: (0, 0))
    return _call(
        body, name="out_loss", grid=(nsteps,),
        in_specs=[row(GMLP_W), row(ATTN_W), row(MEM_W), row(D_MODEL), row(D_MODEL), const((D_MODEL, D_MODEL))],
        out_specs=[row(D_MODEL), row(D_MODEL), const((D_MODEL, D_MODEL)), const((8, LANES))],
        out_shape=[jax.ShapeDtypeStruct((SEQ, D_MODEL), F32), jax.ShapeDtypeStruct((SEQ, D_MODEL), F32),
                   jax.ShapeDtypeStruct((D_MODEL, D_MODEL), F32), jax.ShapeDtypeStruct((8, LANES), F32)],
        compiler_params=_params(),
    )(yg, ya, ym, x, tgt, wo)


def _proj_bwd(x, dy, gain, wt, dg, daq, dak, dav, dag, dmq, dmg):
    tm = 512
    nsteps = SEQ // tm
    pieces = ((C_GU, 3 * GMLP_W), (C_AQ, ATTN_W), (C_AK, ATTN_W), (C_AV, ATTN_W), (C_AG, ATTN_W),
              (C_MQ, MEM_W), (C_MG, MEM_W))

    def body(x_ref, dy_ref, g_ref, wt_hbm, p0, p1, p2, p3, p4, p5, p6, gx_ref, gwt_hbm, gg_ref, wt_v, acc, wt_sem, out_sems):
        i = pl.program_id(0)
        wt_load = pltpu.make_async_copy(wt_hbm, wt_v, wt_sem)

        @pl.when(i == 0)
        def _():
            wt_load.start()
            acc[...] = jnp.zeros_like(acc)
            gg_ref[...] = jnp.zeros_like(gg_ref)

        xv = x_ref[...]
        r = lax.rsqrt(jnp.mean(xv * xv, axis=-1, keepdims=True) + EPS)
        z = xv * r
        g = g_ref[...]
        h = (z * g).astype(BF16)
        pl.when(i == 0)(wt_load.wait)
        flush = [pltpu.make_async_copy(acc.at[c0:c0 + w, :], gwt_hbm.at[c0:c0 + w, :], out_sems.at[n])
                 for n, (c0, w) in enumerate(pieces)]
        dh = jnp.zeros((tm, D_MODEL), F32)
        for n, (pref, (c0, w)) in enumerate(zip((p0, p1, p2, p3, p4, p5, p6), pieces)):
            dp = pref[...]
            dh = dh + _dot(dp, wt_v[c0:c0 + w, :])
            acc[c0:c0 + w, :] += _dot_tn(dp, h)
            pl.when(i == nsteps - 1)(flush[n].start)
        _put_rows(gg_ref, jnp.sum(dh * z, axis=0, keepdims=True), accumulate=True)
        dz = dh * g
        gx_ref[...] = dy_ref[...] + r * (dz - z * jnp.mean(dz * z, axis=-1, keepdims=True))

        @pl.when(i == nsteps - 1)
        def _():
            for cp in flush:
                cp.wait()

    row = lambda w: pl.BlockSpec((tm, w), lambda i: (i, 0))
    hbm = pl.BlockSpec(memory_space=pl.ANY)
    vec = pl.BlockSpec((1, D_MODEL), lambda i: (0, 0))
    return _call(
        body, name="proj_bwd", grid=(nsteps,),
        in_specs=[row(D_MODEL), row(D_MODEL), vec, hbm] + [row(w) for _, w in pieces],
        out_specs=[row(D_MODEL), hbm, pl.BlockSpec((8, LANES), lambda i: (0, 0))],
        out_shape=[jax.ShapeDtypeStruct((SEQ, D_MODEL), F32), jax.ShapeDtypeStruct((IN_W, D_MODEL), F32),
                   jax.ShapeDtypeStruct((8, LANES), F32)],
        scratch_shapes=[pltpu.VMEM((IN_W, D_MODEL), BF16), pltpu.VMEM((IN_W, D_MODEL), F32), pltpu.SemaphoreType.DMA,
                        pltpu.SemaphoreType.DMA((len(pieces),))],
        compiler_params=_params(),
    )(x, dy, gain, wt, dg, daq, dak, dav, dag, dmq, dmg)


AG_SEMS = 8


def _gather_stages(ins, lands, send_sems, recv_sems):
    n = len(ins)
    nrows = [a.shape[0] for a in ins]
    x, y, c = lax.axis_index("x"), lax.axis_index("y"), lax.axis_index("c")
    sib, xn, yn = (x, y, 1 - c), (1 - x, y, c), (x, 1 - y, c)
    me, cx, cy, cd = 2 * x + y, 2 * (1 - x) + y, 2 * x + (1 - y), 2 * (1 - x) + (1 - y)

    def part(a, chip, hf, quarter=None):
        rows = nrows[a] // 2
        base = chip * nrows[a] + hf * rows
        if quarter is not None:
            rows = rows // 2
            base = base + quarter * rows
        return lands[a].at[pl.ds(pl.multiple_of(base, 16), rows), :]

    def copy(a, j, ref, to):
        k = AG_SEMS * a + j
        return pltpu.make_async_remote_copy(src_ref=ref, dst_ref=ref, send_sem=send_sems.at[k],
                                            recv_sem=recv_sems.at[k], device_id=to, device_id_type=MESH)

    def own(a):
        return [copy(a, 0, part(a, me, c), xn), copy(a, 1, part(a, me, c), yn)]

    def neighbours(a):
        return [copy(a, 4, part(a, cx, c, 1), yn), copy(a, 2, part(a, cx, c), sib),
                copy(a, 5, part(a, cy, c, 0), xn), copy(a, 3, part(a, cy, c), sib)]

    def diagonal(a):
        return [copy(a, 7, part(a, cd, c, 1), sib), copy(a, 6, part(a, cd, c, 0), sib)]

    def send_own():
        for a in range(n):
            lands[a][pl.ds(pl.multiple_of(me * nrows[a], 16), nrows[a]), :] = ins[a][...].astype(BF16)
            for cp in own(a):
                cp.start()

    def pass_on_neighbours():
        for a in range(n):
            copy(a, 0, part(a, cx, c), xn).wait_recv()
            copy(a, 1, part(a, cy, c), yn).wait_recv()
            for cp in neighbours(a):
                cp.start()

    def pass_on_diagonal():
        for a in range(n):
            copy(a, 4, part(a, cd, c, 1), yn).wait_recv()
            copy(a, 5, part(a, cd, c, 0), xn).wait_recv()
            for cp in diagonal(a):
                cp.start()

    def y_complete():
        for a in range(n):
            copy(a, 3, part(a, cy, 1 - c), sib).wait_recv()

    def x_complete():
        for a in range(n):
            copy(a, 2, part(a, cx, 1 - c), sib).wait_recv()

    def diagonal_complete():
        for a in range(n):
            copy(a, 6, part(a, cd, 1 - c, 0), sib).wait_recv()
            copy(a, 7, part(a, cd, 1 - c, 1), sib).wait_recv()

    def sends_done():
        for a in range(n):
            for cp in own(a) + neighbours(a) + diagonal(a):
                cp.wait_send()

    def finish():
        y_complete()
        x_complete()
        diagonal_complete()
        sends_done()

    return (send_own, pass_on_neighbours, pass_on_diagonal, finish), (y_complete, x_complete, diagonal_complete, sends_done)


RS_SEMS = 6
RS_KINDS = (((2, 2), 1, F32), ((2, 2), 1, F32), ((2, 2), 2, BF16), ((2, 2), 2, BF16), ((2, 2), 2, F32),
            ((2,), 2, BF16), ((2,), 2, BF16), ((2,), 1, F32))


def _rs_view(g):
    return g.reshape(2, 2, 2, g.shape[0] // 8, g.shape[1])


def _rs_scratch(shapes):
    return [pltpu.VMEM(lead + (r // 8, w // split), dt) for lead, split, dt in RS_KINDS for r, w in shapes]


def _rs_stages(gs, outs, bufs, send_sems, recv_sems, local_sems, widths):
    n = len(gs)
    loc, ra, s_b, r_b, acc1, s_c, r_c, fin = (bufs[n * i:n * i + n] for i in range(len(RS_KINDS)))
    half_w = [w // 2 for w in widths]
    x, y, c = lax.axis_index("x"), lax.axis_index("y"), lax.axis_index("c")
    sib, xn, yn = (x, y, 1 - c), (1 - x, y, c), (x, 1 - y, c)

    def copy(a, j, src, dst, to):
        k = RS_SEMS * a + j
        return pltpu.make_async_remote_copy(src_ref=src, dst_ref=dst, send_sem=send_sems.at[k],
                                            recv_sem=recv_sems.at[k], device_id=to, device_id_type=MESH)

    def step_a(a):
        return [copy(a, 0, gs[a].at[:, :, 1 - c], ra[a], sib),
                pltpu.make_async_copy(gs[a].at[:, :, c], loc[a], local_sems.at[a])]

    def step_b(a):
        return copy(a, 1, s_b[a].at[0], r_b[a].at[0], xn), copy(a, 2, s_b[a].at[1], r_b[a].at[1], yn)

    def step_c(a):
        return copy(a, 3, s_c[a].at[0], r_c[a].at[0], yn), copy(a, 4, s_c[a].at[1], r_c[a].at[1], xn)

    def step_d(a, half):
        rows = fin[a].at[half]
        return copy(a, 5, rows, rows, sib)

    def start():
        for a in range(n):
            for cp in step_a(a):
                cp.start()

    def a_to_b():
        for a in range(n):
            for cp in step_a(a):
                cp.wait()
            ra[a][...] = loc[a][...] + ra[a][...]
            s_b[a][0] = ra[a][1 - x, :, :, :half_w[a]].astype(BF16)
            s_b[a][1] = ra[a][:, 1 - y, :, half_w[a]:].astype(BF16)
            for cp in step_b(a):
                cp.start()

    def b_to_c():
        for a in range(n):
            for cp in step_b(a):
                cp.wait()
            acc1[a][0] = ra[a][x, :, :, :half_w[a]] + r_b[a][0].astype(F32)
            acc1[a][1] = ra[a][:, y, :, half_w[a]:] + r_b[a][1].astype(F32)
            s_c[a][0] = acc1[a][0, 1 - y].astype(BF16)
            s_c[a][1] = acc1[a][1, 1 - x].astype(BF16)
            for cp in step_c(a):
                cp.start()

    def c_to_d():
        for a in range(n):
            for cp in step_c(a):
                cp.wait()
            fin[a][c, :, :half_w[a]] = acc1[a][0, y] + r_c[a][0].astype(F32)
            fin[a][c, :, half_w[a]:] = acc1[a][1, x] + r_c[a][1].astype(F32)
            step_d(a, c).start()

    def finish():
        to_hbm = [pltpu.make_async_copy(fin[a], outs[a], local_sems.at[a]) for a in range(n)]
        for a in range(n):
            step_d(a, 1 - c).wait_recv()
            step_d(a, c).wait_send()
            to_hbm[a].start()
        for cp in to_hbm:
            cp.wait()

    return start, a_to_b, b_to_c, c_to_d, finish


def _reduce_grads(gwt, g_ws, tiny):
    cw = gwt.shape[1] // RS_CHUNKS
    chunk_shape = (gwt.shape[0], cw)

    def body(g0, ws_in, tiny_in, *rest):
        outs, o_ws, o_tiny = rest[:RS_CHUNKS], rest[RS_CHUNKS], rest[RS_CHUNKS + 1]
        rest = rest[RS_CHUNKS + 2:]
        nb = len(RS_KINDS) * RS_CHUNKS
        sm, sa, sb, sc, acc_s, send_sems, recv_sems, local_sems = rest[nb:]
        blocks = [g0.at[:, :, :, :, pl.ds(j * cw, cw)] for j in range(RS_CHUNKS)]
        start, a_to_b, b_to_c, c_to_d, finish = _rs_stages(blocks, outs, rest[:nb], send_sems, recv_sems, local_sems,
                                                           [cw] * RS_CHUNKS)
        n_ws = ws_in.shape[0]
        sm[0:n_ws, :] = ws_in[...]
        sm[n_ws:, :] = tiny_in[...]
        x, y, c = lax.axis_index("x"), lax.axis_index("y"), lax.axis_index("c")

        def small(j, src, dst, to):
            k = RS_SEMS * RS_CHUNKS + j
            return pltpu.make_async_remote_copy(src_ref=src, dst_ref=dst, send_sem=send_sems.at[k],
                                                recv_sem=recv_sems.at[k], device_id=to, device_id_type=MESH)

        along_c, along_x, along_y = (small(0, sm, sa, (x, y, 1 - c)), small(1, acc_s, sb, (1 - x, y, c)),
                                     small(2, sb, sc, (x, 1 - y, c)))
        start()
        along_c.start()
        a_to_b()
        along_c.wait()
        acc_s[...] = sm[...] + sa[...]
        along_x.start()
        b_to_c()
        along_x.wait()
        sb[...] = acc_s[...] + sb[...]
        along_y.start()
        c_to_d()
        along_y.wait()
        o_ws[...] = sb[0:n_ws, :] + sc[0:n_ws, :]
        o_tiny[...] = sb[n_ws:, :] + sc[n_ws:, :]
        finish()

    vm = pl.BlockSpec(memory_space=pltpu.VMEM)
    hbm = pl.BlockSpec(memory_space=pl.ANY)
    small_shape = (g_ws.shape[0] + tiny.shape[0], LANES)
    scratch = _rs_scratch([chunk_shape] * RS_CHUNKS) + [pltpu.VMEM(small_shape, F32) for _ in range(5)]
    nsem = RS_SEMS * RS_CHUNKS + 3
    scratch += [pltpu.SemaphoreType.DMA((nsem,)), pltpu.SemaphoreType.DMA((nsem,)), pltpu.SemaphoreType.DMA((RS_CHUNKS,))]
    return _call(
        body, name="reduce_grads",
        out_shape=[jax.ShapeDtypeStruct((2, gwt.shape[0] // 8, cw), F32)] * RS_CHUNKS
        + [jax.ShapeDtypeStruct(g_ws.shape, F32), jax.ShapeDtypeStruct(tiny.shape, F32)],
        in_specs=[hbm, vm, vm],
        out_specs=[hbm] * RS_CHUNKS + [vm, vm],
        scratch_shapes=scratch,
        compiler_params=_params(),
    )(_rs_view(gwt), g_ws, tiny)


def _adam_update(w, g, m, v):
    nm = ADAM_B1 * m + (1.0 - ADAM_B1) * g
    nv = ADAM_B2 * v + (1.0 - ADAM_B2) * (g * g)
    m_hat = nm / (1.0 - ADAM_B1 ** ADAM_STEP)
    v_hat = nv / (1.0 - ADAM_B2 ** ADAM_STEP)
    return -ADAM_LR * (m_hat / (jnp.sqrt(v_hat) + ADAM_EPS) + ADAM_WD * w), nm, nv


def _adamw(w, g, m, v):
    rows, cols = w.shape
    tm = max(t for t in range(8, 257, 8) if rows % t == 0)
    parts = tuple(g) if isinstance(g, (tuple, list)) else (g,)
    n = len(parts)

    def body(w_ref, m_ref, v_ref, *refs):
        gv = jnp.concatenate([r[...] for r in refs[:n]], axis=1)
        d_ref, nm_ref, nv_ref = refs[n:n + 3]
        d_ref[...], nm_ref[...], nv_ref[...] = _adam_update(w_ref[...], gv, m_ref[...], v_ref[...])
        if n > 1:
            refs[n + 3][...] = gv

    blk = pl.BlockSpec((tm, cols), lambda i: (i, 0))
    nout = 3 if n == 1 else 4
    res = _call(
        body, name="adamw", grid=(rows // tm,),
        in_specs=[blk] * 3 + [pl.BlockSpec((tm, p.shape[1]), lambda i: (i, 0)) for p in parts], out_specs=[blk] * nout,
        out_shape=[jax.ShapeDtypeStruct((rows, cols), F32)] * nout,
        compiler_params=_params(),
    )(w, m, v, *parts)
    return (parts[0] if n == 1 else res[3], *res[:3])


def _adamw_tiny(tiny, weights, ms, vs):
    shapes = [w.shape for w in weights]
    n = len(weights)

    def grad_of(t_ref, k, shape):
        base = 8 * k
        if shape[1] > LANES:
            return [t_ref[base + j:base + j + 1, :] for j in range(shape[1] // LANES)]
        return [t_ref[base:base + shape[0], 0:shape[1]]]

    def body(t_ref, *refs):
        w_refs, m_refs, v_refs = refs[:n], refs[n:2 * n], refs[2 * n:3 * n]
        loss_ref, outs = refs[3 * n], refs[3 * n + 1:]
        loss_ref[...] = (0.5 / D_MODEL) * jnp.sum(t_ref[8 * n:8 * n + 8, :], keepdims=True)
        for k, shape in enumerate(shapes):
            g_ref, d_ref, nm_ref, nv_ref = outs[4 * k:4 * k + 4]
            for j, g in enumerate(grad_of(t_ref, k, shape)):
                cols = slice(j * LANES, (j + 1) * LANES) if shape[1] > LANES else slice(None)
                g_ref[:, cols] = g
                d_ref[:, cols], nm_ref[:, cols], nv_ref[:, cols] = _adam_update(
                    w_refs[k][:, cols], g, m_refs[k][:, cols], v_refs[k][:, cols])

    out_shape = [jax.ShapeDtypeStruct((1, 1), F32)]
    for shape in shapes:
        out_shape += [jax.ShapeDtypeStruct(shape, F32)] * 4
    return _call(body, name="adamw_tiny", out_shape=out_shape, compiler_params=_params())(tiny, *weights, *ms, *vs)


def _local_grads(x, mem, tgt, norm_gain, wt_sh, gmlp_v_gain, gmlp_w_s, gmlp_b, attn_q_gain, attn_k_gain,
                 mem_norm_gain, wkv_sh, mem_q_gain, mem_k_gain, wo_sh):
    vg = gmlp_v_gain.reshape(1, GMLP_W)
    bias_full = jnp.repeat(gmlp_b.T, HEAD_DIM, axis=1)
    gq2, gk2 = jnp.tile(attn_q_gain, (1, 2)), jnp.tile(attn_k_gain, (1, 2))
    qg4, kg4 = jnp.tile(mem_q_gain, (1, 4)), jnp.tile(mem_k_gain, (1, 4))

    proj, wt = _gather_proj(x, norm_gain, wt_sh)
    yg = _gmlp_fwd(proj, vg, gmlp_w_s, bias_full)
    o, lse, ya, wkv, wo = _attn_fwd(proj, gq2, gk2, wkv_sh, wo_sh)
    kv, hm = _mem_kv(mem, mem_norm_gain, wkv)
    om, ym = _mem_fwd(proj, kv, qg4, kg4)
    dy, dyc, g_wo, err2 = _out_loss(yg, ya, ym, x, tgt, wo)
    dmq, dmg, g_mq, g_mk, g_wkv, g_mng = _mem_bwd(proj, om, dyc, kv, hm, mem, mem_norm_gain, wkv, qg4, kg4)
    daq, dak, dav, dag, g_aq, g_ak, g_wkv_sh, g_wo_sh = _attn_bwd(proj, o, lse, dyc, gq2, gk2, g_wkv, g_wo)
    dg, g_ws, g_b, g_vg = _gmlp_bwd(proj, dyc, vg, gmlp_w_s, bias_full)
    gx, g_wt, g_ng = _proj_bwd(x, dy, norm_gain, wt, dg, daq, dak, dav, dag, dmq, dmg)

    tiny = jnp.concatenate([g_ng, g_vg, g_b, g_aq, g_ak, g_mng, g_mq, g_mk, err2], axis=0)
    return gx, g_wt, g_wkv_sh, g_wo_sh, g_ws.reshape(4 * CHUNK, CHUNK), tiny


def kernel(x, mem, norm_gain, w_in, gmlp_v_gain, gmlp_w_s, gmlp_b, attn_q_gain, attn_k_gain, mem_norm_gain, w_mem_kv, mem_q_gain, mem_k_gain, w_out, loss_target, m_norm_gain, m_w_in, m_gmlp_v_gain, m_gmlp_w_s, m_gmlp_b, m_attn_q_gain, m_attn_k_gain, m_mem_norm_gain, m_w_mem_kv, m_mem_q_gain, m_mem_k_gain, m_w_out, v_norm_gain, v_w_in, v_gmlp_v_gain, v_gmlp_w_s, v_gmlp_b, v_attn_q_gain, v_attn_k_gain, v_mem_norm_gain, v_w_mem_kv, v_mem_q_gain, v_mem_k_gain, v_w_out):
    gx, g_wt, g_wkv_sh, g_wo_sh, g_ws, tiny = _local_grads(
        x[0], mem[0], loss_target[0], norm_gain, w_in[0].T, gmlp_v_gain[0], gmlp_w_s[0], gmlp_b[0],
        attn_q_gain, attn_k_gain, mem_norm_gain, w_mem_kv[0], mem_q_gain, mem_k_gain, w_out[0])
    *g_wt_sh, g_ws, tiny = _reduce_grads(g_wt, g_ws, tiny)
    chip_block = lambda g: g.reshape(2 * g.shape[1], g.shape[2])
    g_wt_sh = tuple(chip_block(g) for g in g_wt_sh)
    g_wkv_sh, g_wo_sh = chip_block(g_wkv_sh), chip_block(g_wo_sh)

    ws = (norm_gain, w_in, gmlp_v_gain, gmlp_w_s, gmlp_b, attn_q_gain, attn_k_gain, mem_norm_gain, w_mem_kv,
          mem_q_gain, mem_k_gain, w_out)
    ms = (m_norm_gain, m_w_in, m_gmlp_v_gain, m_gmlp_w_s, m_gmlp_b, m_attn_q_gain, m_attn_k_gain, m_mem_norm_gain,
          m_w_mem_kv, m_mem_q_gain, m_mem_k_gain, m_w_out)
    vs = (v_norm_gain, v_w_in, v_gmlp_v_gain, v_gmlp_w_s, v_gmlp_b, v_attn_q_gain, v_attn_k_gain, v_mem_norm_gain,
          v_w_mem_kv, v_mem_q_gain, v_mem_k_gain, v_w_out)
    form = {1: lambda a: a[0].T, 3: lambda a: a.reshape(4 * CHUNK, CHUNK), 2: lambda a: a[0], 4: lambda a: a[0],
            8: lambda a: a[0], 11: lambda a: a[0]}
    back = {1: lambda a: a.T[None], 3: lambda a: a.reshape(1, 4, CHUNK, CHUNK), 2: lambda a: a[None],
            4: lambda a: a[None], 8: lambda a: a[None], 11: lambda a: a[None]}
    fwd = lambda t, i: form.get(i, lambda a: a)(t[i])
    out = {}
    for i, g in ((1, g_wt_sh), (3, g_ws), (8, g_wkv_sh), (11, g_wo_sh)):
        out[i] = _adamw(fwd(ws, i), g, fwd(ms, i), fwd(vs, i))
    res = _adamw_tiny(tiny, [fwd(ws, i) for i in TINY_ORDER], [fwd(ms, i) for i in TINY_ORDER],
                      [fwd(vs, i) for i in TINY_ORDER])
    for k, i in enumerate(TINY_ORDER):
        out[i] = res[1 + 4 * k:5 + 4 * k]
    leaves = [[back.get(i, lambda a: a)(out[i][j]) for i in range(12)] for j in range(4)]
    return (res[0].reshape(()), gx[None], *leaves[0], *leaves[1], *leaves[2], *leaves[3])
```

```python
import math

import jax
import jax.numpy as jnp
from jax import lax
from jax.experimental import pallas as pl
from jax.experimental.pallas import tpu as pltpu

F32 = jnp.float32
BF16 = jnp.bfloat16

SEQ = 4096
D_MODEL = 1024
HEAD_DIM = 64
LANES = 128
CHUNK = 128
GMLP_W, ATTN_W, MEM_W = 256, 512, 256
IN_W = 3 * GMLP_W + 4 * ATTN_W + 2 * MEM_W
MEM_LEN = 256
DILATIONS = (16, 4, 1)
EPS = 1e-6
QK_SCALE = 1.0 / math.sqrt(HEAD_DIM)
C_GU, C_GV, C_GG, C_AQ, C_AK, C_AV, C_AG, C_MQ, C_MG = 0, 256, 512, 768, 1280, 1792, 2304, 2816, 3072

ADAM_LR, ADAM_B1, ADAM_B2, ADAM_EPS, ADAM_WD, ADAM_STEP = 0.001, 0.9, 0.999, 1e-08, 0.01, 10

VMEM_LIMIT = 48 * 1024 * 1024
RS_CHUNKS = 4
ATTN_UNROLL = 4
MESH = pl.DeviceIdType.MESH

TINY_ORDER = (0, 2, 4, 5, 6, 7, 9, 10)


def _call(body, **kw):
    return pl.pallas_call(body, **kw)


def _params(**kw):
    return pltpu.CompilerParams(vmem_limit_bytes=VMEM_LIMIT, **kw)


def _dot(a, b):
    return jnp.dot(a, b, preferred_element_type=F32)


def _dot_nt(a, b):
    return lax.dot_general(a, b, (((1,), (1,)), ((), ())), preferred_element_type=F32)


def _dot_tn(a, b):
    return lax.dot_general(a, b, (((0,), (0,)), ((), ())), preferred_element_type=F32)


def _head_blockdiag():
    r = lax.shift_right_logical(lax.broadcasted_iota(jnp.int32, (LANES, LANES), 0), 6)
    c = lax.shift_right_logical(lax.broadcasted_iota(jnp.int32, (LANES, LANES), 1), 6)
    return jnp.where(r == c, 1.0, 0.0).astype(BF16)


def _headsum(v, bd):
    hi = v.astype(BF16)
    lo = (v - hi.astype(F32)).astype(BF16)
    return _dot(hi, bd) + _dot(lo, bd)


def _lo_mask(rows):
    return lax.broadcasted_iota(jnp.int32, (rows, LANES), 1) < HEAD_DIM


def _sigmoid(x):
    return 1.0 / (1.0 + jnp.exp(-x))


def _fold_heads(v):
    return v + pltpu.roll(v, HEAD_DIM, 1)


def _put_rows(ref, vec, accumulate=False):
    for j in range(vec.shape[1] // LANES):
        piece = vec[:, j * LANES:(j + 1) * LANES]
        ref[j:j + 1, :] = ref[j:j + 1, :] + piece if accumulate else piece


def _gather_proj(x, gain, wt_sh):
    tm = 1024
    nrow = SEQ // tm
    widths = (768, 896, 768, 896)
    nunits = len(widths)
    pair = 2 * wt_sh.shape[0]
    assert pair % LANES == 0 and sum(widths[:2]) == pair

    def body(x_ref, g_ref, wt_sh_ref, proj_hbm, wt_hbm, h_scr, land, res, send_sems, recv_sems, out_sems, copy_sem):
        u, i = pl.program_id(0), pl.program_id(1)
        cx_, cy_ = lax.axis_index("x"), lax.axis_index("y")
        (send_own, pass_on_neighbours, pass_on_diagonal, _), (y_complete, x_complete, diagonal_complete, sends_done) = (
            _gather_stages((wt_sh_ref,), (land,), send_sems, recv_sems))
        first = lambda k: (u == k) & (i == 0)
        last = (u == nunits - 1) & (i == nrow - 1)
        to_hbm = pltpu.make_async_copy(land, wt_hbm, copy_sem)

        pl.when(first(0))(send_own)

        @pl.when(u == 0)
        def _():
            xv = x_ref[...]
            ms = jnp.mean(xv * xv, axis=-1, keepdims=True)
            h_scr[pl.ds(pl.multiple_of(i * tm, tm), tm), :] = (xv * lax.rsqrt(ms + EPS) * g_ref[...]).astype(BF16)

        @pl.when(first(1))
        def _():
            pass_on_neighbours()
            y_complete()

        @pl.when(first(2))
        def _():
            x_complete()
            pass_on_diagonal()

        @pl.when(first(3))
        def _():
            diagonal_complete()
            to_hbm.start()

        mine, other = pair * cx_, pair * (1 - cx_)
        col0 = (mine + 896 * cy_, mine + 768 * (1 - cy_), other + 896 * cy_, other + 768 * (1 - cy_))
        slot = i % 2
        rows = pl.ds(pl.multiple_of(i * tm, tm), tm)

        def writeback(k, rows_):
            c0 = pl.multiple_of(col0[k], LANES)
            return pltpu.make_async_copy(res.at[slot, :, pl.ds(0, widths[k])], proj_hbm.at[rows_, pl.ds(c0, widths[k])],
                                         out_sems.at[slot])

        for k in range(nunits):
            @pl.when(u == k)
            def _(k=k):
                pl.when(i >= 2)(writeback(k, rows).wait)
                if k > 0:
                    pl.when(i < 2)(writeback(k - 1, rows).wait)
                w_rows = land[pl.ds(pl.multiple_of(col0[k], LANES), widths[k]), :]
                res[slot, :, 0:widths[k]] = _dot_nt(h_scr[rows, :], w_rows)
                writeback(k, rows).start()

        @pl.when(last)
        def _():
            sends_done()
            to_hbm.wait()
            for s in range(2):
                pltpu.make_async_copy(res.at[s, :, pl.ds(0, widths[-1])], proj_hbm.at[rows, pl.ds(0, widths[-1])], out_sems.at[s]).wait()

    full = jax.ShapeDtypeStruct((4 * wt_sh.shape[0], wt_sh.shape[1]), BF16)
    hbm = pl.BlockSpec(memory_space=pl.ANY)
    return _call(
        body, name="gather_proj", grid=(nunits, nrow),
        in_specs=[pl.BlockSpec((tm, D_MODEL), lambda u, i: (jnp.where(u == 0, i, nrow - 1), 0)),
                  pl.BlockSpec((1, D_MODEL), lambda u, i: (0, 0)), pl.BlockSpec(wt_sh.shape, lambda u, i: (0, 0))],
        out_specs=[hbm, hbm],
        out_shape=[jax.ShapeDtypeStruct((SEQ, IN_W), F32), full],
        scratch_shapes=[pltpu.VMEM((SEQ, D_MODEL), BF16), pltpu.VMEM(full.shape, BF16), pltpu.VMEM((2, tm, max(widths)), F32),
                        pltpu.SemaphoreType.DMA((AG_SEMS,)), pltpu.SemaphoreType.DMA((AG_SEMS,)),
                        pltpu.SemaphoreType.DMA((2,)), pltpu.SemaphoreType.DMA],
        compiler_params=_params(),
    )(x, gain, wt_sh)


def _gmlp_weights(w_ref):
    ti = lax.broadcasted_iota(jnp.int32, (CHUNK, CHUNK), 0)
    si = lax.broadcasted_iota(jnp.int32, (CHUNK, CHUNK), 1)
    tril = si <= ti
    return tril, [jnp.where(tril, w_ref[h], 0.0).astype(BF16) for h in range(4)]


def _gmlp_fwd(proj, vgain, w_s, bias_full):
    tm = 1024

    def body(p_ref, vg_ref, w_ref, b_ref, y_ref):
        bd = _head_blockdiag()
        lo = _lo_mask(CHUNK)
        _, wm = _gmlp_weights(w_ref)
        units = [(pl.ds(c * CHUNK, CHUNK), p) for c in range(tm // CHUNK) for p in range(2)]
        col = lambda c0, p: slice(c0 + p * LANES, c0 + (p + 1) * LANES)
        vs = [p_ref[rows, col(C_GV, p)] for rows, p in units]
        rs = [lax.rsqrt(_headsum(v * v, bd) * (1.0 / HEAD_DIM) + EPS) for v in vs]
        vns = [(v * r * vg_ref[:, col(0, p)]).astype(BF16) for v, r, (_, p) in zip(vs, rs, units)]
        sps = [jnp.where(lo, _dot(wm[2 * p], vn), _dot(wm[2 * p + 1], vn)) + b_ref[:, col(0, p)] for vn, (_, p) in zip(vns, units)]
        for sp, (rows, p) in zip(sps, units):
            gt = p_ref[rows, col(C_GG, p)]
            y_ref[rows, col(0, p)] = (p_ref[rows, col(C_GU, p)] * sp * (gt * _sigmoid(gt))).astype(BF16)

    return _call(
        body, name="gmlp_fwd", grid=(SEQ // tm,),
        in_specs=[pl.BlockSpec((tm, 3 * GMLP_W), lambda i: (i, 0)),
                  pl.BlockSpec((1, GMLP_W), lambda i: (0, 0)),
                  pl.BlockSpec((4, CHUNK, CHUNK), lambda i: (0, 0, 0)),
                  pl.BlockSpec((CHUNK, GMLP_W), lambda i: (0, 0))],
        out_specs=pl.BlockSpec((tm, GMLP_W), lambda i: (i, 0)),
        out_shape=jax.ShapeDtypeStruct((SEQ, GMLP_W), BF16),
        compiler_params=_params(),
    )(proj, vgain, w_s, bias_full)


def _gmlp_bwd(proj, dyc, vgain, w_s, bias_full):
    tm = 1024
    nsteps = SEQ // tm

    def body(p_ref, dy_ref, vg_ref, w_ref, b_ref, dg_ref, gw_ref, gb_ref, gv_ref):
        i = pl.program_id(0)
        bd = _head_blockdiag()
        lo = _lo_mask(CHUNK)
        tril, wm = _gmlp_weights(w_ref)
        ri = lax.broadcasted_iota(jnp.int32, (16, LANES), 0)
        li = lax.broadcasted_iota(jnp.int32, (16, LANES), 1)
        head_rows = [jnp.where(((ri == 2 * p) & (li < HEAD_DIM)) | ((ri == 2 * p + 1) & (li >= HEAD_DIM)), 1.0, 0.0).astype(BF16)
                     for p in range(2)]

        @pl.when(i == 0)
        def _():
            gw_ref[...] = jnp.zeros_like(gw_ref)
            gb_ref[...] = jnp.zeros_like(gb_ref)
            gv_ref[...] = jnp.zeros_like(gv_ref)

        units = [(pl.ds(c * CHUNK, CHUNK), p) for c in range(tm // CHUNK) for p in range(2)]
        col = lambda c0, p: slice(c0 + p * LANES, c0 + (p + 1) * LANES)
        vs = [p_ref[rows, col(C_GV, p)] for rows, p in units]
        rs = [lax.rsqrt(_headsum(v * v, bd) * (1.0 / HEAD_DIM) + EPS) for v in vs]
        zs = [v * r for v, r in zip(vs, rs)]
        vns = [(z * vg_ref[:, col(0, p)]).astype(BF16) for z, (_, p) in zip(zs, units)]
        sps = [jnp.where(lo, _dot(wm[2 * p], vn), _dot(wm[2 * p + 1], vn)) + b_ref[:, col(0, p)] for vn, (_, p) in zip(vns, units)]
        dsps = []
        for sp, (rows, p) in zip(sps, units):
            u = p_ref[rows, col(C_GU, p)]
            gt = p_ref[rows, col(C_GG, p)]
            dy = dy_ref[rows, col(0, p)]
            sg = _sigmoid(gt)
            sl = gt * sg
            dg_ref[rows, col(C_GU, p)] = (dy * sp * sl).astype(BF16)
            dg_ref[rows, col(C_GG, p)] = (dy * u * sp * (sg * (1.0 + gt * (1.0 - sg)))).astype(BF16)
            dsps.append(dy * u * sl)
        dspbs = [dsp.astype(BF16) for dsp in dsps]
        dvns = [jnp.where(lo, _dot_tn(wm[2 * p], dspb), _dot_tn(wm[2 * p + 1], dspb)) for dspb, (_, p) in zip(dspbs, units)]
        gws = [(_dot_nt(jnp.where(lo, dsp, 0.0).astype(BF16), vn), _dot_nt(jnp.where(lo, 0.0, dsp).astype(BF16), vn))
               for dsp, vn in zip(dsps, vns)]
        gbs = [(_dot_nt(head_rows[p], dspb) + _dot_nt(head_rows[p], (dsp - dspb.astype(F32)).astype(BF16)))[0:8]
               for dsp, dspb, (_, p) in zip(dsps, dspbs, units)]
        for p in range(2):
            mine = [n for n, (_, q) in enumerate(units) if q == p]
            gw_ref[2 * p] += sum(gws[n][0] for n in mine)
            gw_ref[2 * p + 1] += sum(gws[n][1] for n in mine)
            gvp = sum(jnp.sum(dvns[n] * zs[n], axis=0, keepdims=True) for n in mine)
            gv_ref[2 * p:2 * p + 1, :] += gvp
            gv_ref[2 * p + 1:2 * p + 2, :] += pltpu.roll(gvp, HEAD_DIM, 1)
        gb_ref[...] += sum(gbs)
        for dvn, z, r, (rows, p) in zip(dvns, zs, rs, units):
            dz = dvn * vg_ref[:, col(0, p)]
            dg_ref[rows, col(C_GV, p)] = (r * (dz - z * (_headsum(dz * z, bd) * (1.0 / HEAD_DIM)))).astype(BF16)

        @pl.when(i == nsteps - 1)
        def _():
            for h in range(4):
                gw_ref[h] = jnp.where(tril, gw_ref[h], 0.0)

    return _call(
        body, name="gmlp_bwd", grid=(nsteps,),
        in_specs=[pl.BlockSpec((tm, 3 * GMLP_W), lambda i: (i, 0)),
                  pl.BlockSpec((tm, GMLP_W), lambda i: (i, 0)),
                  pl.BlockSpec((1, GMLP_W), lambda i: (0, 0)),
                  pl.BlockSpec((4, CHUNK, CHUNK), lambda i: (0, 0, 0)),
                  pl.BlockSpec((CHUNK, GMLP_W), lambda i: (0, 0))],
        out_specs=[pl.BlockSpec((tm, 3 * GMLP_W), lambda i: (i, 0)),
                   pl.BlockSpec((4, CHUNK, CHUNK), lambda i: (0, 0, 0)),
                   pl.BlockSpec((8, LANES), lambda i: (0, 0)),
                   pl.BlockSpec((8, LANES), lambda i: (0, 0))],
        out_shape=[jax.ShapeDtypeStruct((SEQ, 3 * GMLP_W), BF16),
                   jax.ShapeDtypeStruct((4, CHUNK, CHUNK), F32),
                   jax.ShapeDtypeStruct((8, LANES), F32),
                   jax.ShapeDtypeStruct((8, LANES), F32)],
        compiler_params=_params(),
    )(proj, dyc, vgain, w_s, bias_full)


def _band_masks():
    qi = lax.broadcasted_iota(jnp.int32, (CHUNK, 2 * CHUNK), 0)
    kj = lax.broadcasted_iota(jnp.int32, (CHUNK, 2 * CHUNK), 1)
    valid2 = ((kj < CHUNK) & (kj >= qi)) | ((kj >= CHUNK) & (kj - CHUNK <= qi))
    q1 = lax.broadcasted_iota(jnp.int32, (CHUNK, CHUNK), 0)
    k1 = lax.broadcasted_iota(jnp.int32, (CHUNK, CHUNK), 1)
    return k1 <= q1, valid2


def _stack_heads(v, lo):
    return jnp.concatenate([jnp.where(lo, v, 0.0), jnp.where(lo, 0.0, v)], axis=0).astype(BF16)


def _rows_of(ref, start, d):
    if d == 1:
        return ref.at[pl.ds(start if isinstance(start, int) else pl.multiple_of(start, CHUNK), CHUNK), :]
    return ref.at[pl.ds(start, CHUNK, stride=d), :]


def _unrolled(lo, hi, unroll, run):
    groups = (hi - lo) // unroll
    if groups:
        def body(g, carry):
            run([lo + g * unroll + t for t in range(unroll)])
            return carry

        lax.fori_loop(0, groups, body, 0)
    if lo + groups * unroll < hi:
        run(range(lo + groups * unroll, hi))


def _for_blocks(d, group_fn, unroll):
    nblk = SEQ // CHUNK
    sh = d.bit_length() - 1

    def first(j):
        return (j * CHUNK if d == 1 else j, None)

    def rest(j):
        start = (j & (d - 1)) + (j >> sh) * (CHUNK * d)
        return (start, start - CHUNK * d)

    _unrolled(0, d, unroll, lambda js: group_fn(d, [first(j) for j in js]))
    _unrolled(d, nblk, unroll, lambda js: group_fn(d, [rest(j) for j in js]))


def _attn_fwd(proj, gq2, gk2, *ride_along):
    tn = 2048
    npairs = ATTN_W // LANES
    nride = len(ride_along)

    def body(q_ref, k_ref, v_ref, g_ref, gq_ref, gk_ref, *rest):
        shards, rest = rest[:nride], rest[nride:]
        o_ref, l_ref, ya_ref = rest[:3]
        gathered, rest = rest[3:3 + nride], rest[3 + nride:]
        qn_ref, kn_ref = rest[:2]
        lands, (send_sems, recv_sems, copy_sems) = rest[2:2 + nride], rest[2 + nride:]
        pair = pl.program_id(0)
        ride = _gather_stages(shards, lands, send_sems, recv_sems)[0]
        for step in range(npairs):
            pl.when(pair == step)(ride[step])
        bd = _head_blockdiag()
        lo = _lo_mask(CHUNK)
        valid1, valid2 = _band_masks()

        def norm(t, carry):
            rows = pl.ds(pl.multiple_of(t * tn, tn), tn)
            q, k = q_ref[rows, :], k_ref[rows, :]
            ssq = [_headsum(a * a, bd) for a in (q, k)]
            qn_ref[rows, :] = q * lax.rsqrt(ssq[0] * (1.0 / HEAD_DIM) + EPS) * (gq_ref[...] * QK_SCALE)
            kn_ref[rows, :] = k * lax.rsqrt(ssq[1] * (1.0 / HEAD_DIM) + EPS) * gk_ref[...]
            return carry

        lax.fori_loop(0, SEQ // tn, norm, 0)

        def load_kv(ref, d, start, prev):
            own = _rows_of(ref, start, d)[...]
            if prev is None:
                return own.astype(BF16)
            return jnp.concatenate([_rows_of(ref, prev, d)[...], own], axis=0).astype(BF16)

        def group(d, blocks):
            valid = valid1 if blocks[0][1] is None else valid2
            valid = jnp.concatenate([valid, valid], axis=0)
            qs = [_rows_of(qn_ref, start, d)[...] for start, _ in blocks]
            ks = [load_kv(kn_ref, d, start, prev) for start, prev in blocks]
            vs = [load_kv(v_ref, d, start, prev) for start, prev in blocks]
            ss = [_dot_nt(_stack_heads(q, lo), k) for q, k in zip(qs, ks)]
            ms, ps, ls = [], [], []
            for s in ss:
                s = jnp.where(valid, s, -jnp.inf)
                m = jnp.max(s, axis=-1, keepdims=True)
                p = jnp.exp(s - m)
                ms.append(m)
                ls.append(jnp.sum(p, axis=-1, keepdims=True))
                ps.append(p.astype(BF16))
            os_ = [_dot(p, v) for p, v in zip(ps, vs)]
            for b, (start, _) in enumerate(blocks):
                heads = lambda v: jnp.where(lo, v[:CHUNK], v[CHUNK:])
                lsum = heads(ls[b])
                ob = heads(os_[b]) * (1.0 / lsum)
                lb = heads(ms[b]) + jnp.log(lsum)
                o_rows = _rows_of(o_ref, start, d)
                l_rows = _rows_of(l_ref, start, d)
                if d != DILATIONS[0]:
                    lold = l_rows[...]
                    mx = jnp.maximum(lold, lb)
                    ea = jnp.exp(lold - mx)
                    eb = jnp.exp(lb - mx)
                    inv = 1.0 / (ea + eb)
                    ob = o_rows[...] * (ea * inv) + ob * (eb * inv)
                    lb = mx + jnp.log(ea + eb)
                o_rows[...] = ob
                l_rows[...] = lb

        for d in DILATIONS:
            _for_blocks(d, group, ATTN_UNROLL)

        def fin(t, carry):
            rows = pl.ds(pl.multiple_of(t * tn, tn), tn)
            g = g_ref[rows, :]
            ya_ref[rows, :] = (o_ref[rows, :] * (g * _sigmoid(g))).astype(BF16)
            return carry

        lax.fori_loop(0, SEQ // tn, fin, 0)

        @pl.when(pair == npairs - 1)
        def _():
            to_hbm = [pltpu.make_async_copy(land, out, copy_sems.at[n]) for n, (land, out) in enumerate(zip(lands, gathered))]
            for cp in to_hbm:
                cp.start()
            for cp in to_hbm:
                cp.wait()

    col = lambda c0: pl.BlockSpec((SEQ, LANES), lambda p: (0, c0 // LANES + p))
    vec = pl.BlockSpec((1, LANES), lambda p: (0, 0))
    out = pl.BlockSpec((SEQ, LANES), lambda p: (0, p))
    full = [jax.ShapeDtypeStruct((4 * a.shape[0], a.shape[1]), BF16) for a in ride_along]
    return _call(
        body, name="attn_fwd", grid=(npairs,),
        in_specs=[col(C_AQ), col(C_AK), col(C_AV), col(C_AG), vec, vec]
        + [pl.BlockSpec(a.shape, lambda p: (0, 0)) for a in ride_along],
        out_specs=[out, out, out] + [pl.BlockSpec(memory_space=pl.ANY)] * nride,
        out_shape=[jax.ShapeDtypeStruct((SEQ, ATTN_W), F32), jax.ShapeDtypeStruct((SEQ, ATTN_W), F32),
                   jax.ShapeDtypeStruct((SEQ, ATTN_W), BF16)] + full,
        scratch_shapes=[pltpu.VMEM((SEQ, LANES), F32), pltpu.VMEM((SEQ, LANES), F32)]
        + [pltpu.VMEM(s.shape, BF16) for s in full]
        + [pltpu.SemaphoreType.DMA((AG_SEMS * nride,)), pltpu.SemaphoreType.DMA((AG_SEMS * nride,)),
           pltpu.SemaphoreType.DMA((nride,))],
        compiler_params=_params(),
    )(proj, proj, proj, proj, gq2, gk2, *ride_along)


def _attn_bwd(proj, o, lse, dyc, gq2, gk2, *ride_along):
    tn = 2048
    npairs = ATTN_W // LANES
    nride = len(ride_along)
    nbufs = nride * len(RS_KINDS)

    def body(proj_hbm, o_hbm, l_hbm, dyc_hbm, gq_ref, gk_ref, *rest):
        ride_in, rest = rest[:nride], rest[nride:]
        dq_ref, dk_ref, dv_ref, dgt_ref, gqg_ref, gkg_ref = rest[:6]
        ride_out, rest = rest[6:6 + nride], rest[6 + nride:]
        qb_, kb_, vb_, gb_, ob_, lb_, yb_, dkb_, dvb_, sems = rest[:10]
        rs_bufs, (send_sems, recv_sems, local_sems) = rest[10:10 + nbufs], rest[10 + nbufs:]
        rs_stage = _rs_stages(ride_in, ride_out, rs_bufs, send_sems, recv_sems, local_sems, [g.shape[1] for g in ride_along])
        pair = pl.program_id(0)
        for step in range(npairs):
            pl.when(pair == step)(rs_stage[step])
        bd = _head_blockdiag()
        lo = _lo_mask(CHUNK)
        lo2 = lax.broadcasted_iota(jnp.int32, (2 * CHUNK, LANES), 1) < HEAD_DIM
        valid1, valid2 = _band_masks()
        gqs = gq_ref[...] * QK_SCALE
        gk = gk_ref[...]

        def pcol(c0, of=None):
            return acol(proj_hbm, c0, of)

        def acol(hbm, c0=0, of=None):
            of = pair if of is None else of
            return hbm.at[:, pl.ds(pl.multiple_of(c0 + of * LANES, LANES), LANES)]

        def input_loads(of):
            return [pltpu.make_async_copy(src, dst, sems.at[n]) for n, (src, dst) in enumerate((
                (pcol(C_AQ, of), qb_), (pcol(C_AK, of), kb_), (pcol(C_AG, of), gb_), (acol(o_hbm, 0, of), ob_),
                (acol(dyc_hbm, GMLP_W, of), yb_), (pcol(C_AV, of), vb_), (acol(l_hbm, 0, of), lb_)))]

        early = (0, 1, 3, 4)
        loads = input_loads(pair)
        for n, cp in enumerate(loads):
            if n in early:
                pl.when(pair == 0)(cp.start)
            else:
                cp.start()

        @pl.when(pair == 0)
        def _():
            gqg_ref[...] = jnp.zeros_like(gqg_ref)
            gkg_ref[...] = jnp.zeros_like(gkg_ref)

        def pre_qk(t, carry):
            rows = pl.ds(pl.multiple_of(t * tn, tn), tn)
            q, k = qb_[rows, :], kb_[rows, :]
            ssq = [_headsum(a * a, bd) for a in (q, k)]
            qb_[rows, :] = q * lax.rsqrt(ssq[0] * (1.0 / HEAD_DIM) + EPS) * gqs
            kb_[rows, :] = k * lax.rsqrt(ssq[1] * (1.0 / HEAD_DIM) + EPS) * gk
            return carry

        def pre_gate(t, carry):
            rows = pl.ds(pl.multiple_of(t * tn, tn), tn)
            g = gb_[rows, :]
            ov = ob_[rows, :]
            dya = yb_[rows, :]
            sg = _sigmoid(g)
            dgt_ref[rows, :] = (dya * ov * (sg * (1.0 + g * (1.0 - sg)))).astype(BF16)
            do = dya * (g * sg)
            yb_[rows, :] = do
            ob_[rows, :] = jnp.where(first_half, lb_[rows, :], _headsum(do * ov, bd))
            return carry

        first_half = (lax.broadcasted_iota(jnp.int32, (tn, LANES), 1) & (HEAD_DIM - 1)) < HEAD_DIM // 2
        loads[0].wait()
        loads[1].wait()
        lax.fori_loop(0, SEQ // tn, pre_qk, 0)
        for cp in loads[2:5] + loads[6:7]:
            cp.wait()
        lax.fori_loop(0, SEQ // tn, pre_gate, 0)
        loads[5].wait()
        reloads = [pltpu.make_async_copy(pcol(C_AQ), lb_, sems.at[7]), pltpu.make_async_copy(pcol(C_AK), vb_, sems.at[8])]
        reloads[0].start()

        def load_kv(ref, d, start, prev):
            own = _rows_of(ref, start, d)[...]
            if prev is None:
                return own.astype(BF16)
            return jnp.concatenate([_rows_of(ref, prev, d)[...], own], axis=0).astype(BF16)

        def group(d, blocks):
            first = blocks[0][1] is None
            valid, lok = (valid1, lo) if first else (valid2, lo2)
            chains = [(b, h) for b in range(len(blocks)) for h in range(2)]
            mask = lambda h: lo if h == 0 else ~lo
            qs = [_rows_of(qb_, start, d)[...] for start, _ in blocks]
            dos = [_rows_of(yb_, start, d)[...] for start, _ in blocks]
            lds = [_rows_of(ob_, start, d)[...] for start, _ in blocks]
            ks = [load_kv(kb_, d, start, prev) for start, prev in blocks]
            vs = [load_kv(vb_, d, start, prev) for start, prev in blocks]
            qbs = [q.astype(BF16) for q in qs]
            dobs = [do.astype(BF16) for do in dos]
            ss = [_dot_nt(jnp.where(mask(h), qs[b], 0.0).astype(BF16), ks[b]) for b, h in chains]
            dps = [_dot_nt(jnp.where(mask(h), dos[b], 0.0).astype(BF16), vs[b]) for b, h in chains]
            pbs, dss = [], []
            for s, dp, (b, h) in zip(ss, dps, chains):
                hc, dc = h * HEAD_DIM, h * HEAD_DIM + HEAD_DIM // 2
                p = jnp.exp(jnp.where(valid, s, -jnp.inf) - lds[b][:, hc:hc + 1])
                pbs.append(p.astype(BF16))
                dss.append((p * (dp - lds[b][:, dc:dc + 1])).astype(BF16))
            dqs = [_dot(ds, ks[b]) for ds, (b, h) in zip(dss, chains)]
            dks = [_dot_tn(ds, qbs[b]) for ds, (b, h) in zip(dss, chains)]
            dvs = [_dot_tn(p, dobs[b]) for p, (b, h) in zip(pbs, chains)]
            assign = d == DILATIONS[0]
            for b, (start, prev) in enumerate(blocks):
                c0, c1 = 2 * b, 2 * b + 1
                dq_rows = _rows_of(gb_, start, d)
                dqb = jnp.where(lo, dqs[c0], dqs[c1])
                dq_rows[...] = dqb if assign else dq_rows[...] + dqb
                dkc = jnp.where(lok, dks[c0], dks[c1])
                dvc = jnp.where(lok, dvs[c0], dvs[c1])
                spans = ((start, slice(0, CHUNK), True),) if first else (
                    (prev, slice(0, CHUNK), False), (start, slice(CHUNK, 2 * CHUNK), True))
                for st, sl, own in spans:
                    dk_rows = _rows_of(dkb_, st, d)
                    dv_rows = _rows_of(dvb_, st, d)
                    if assign and own:
                        dk_rows[...] = dkc[sl]
                        dv_rows[...] = dvc[sl]
                    else:
                        dk_rows[...] = dk_rows[...] + dkc[sl]
                        dv_rows[...] = dv_rows[...] + dvc[sl]

        for d in DILATIONS:
            _for_blocks(d, group, ATTN_UNROLL)

        reloads[1].start()

        @pl.when(pair < npairs - 1)
        def _():
            nxt = input_loads(pair + 1)
            for n in early:
                nxt[n].start()

        for cp in reloads:
            cp.wait()

        def post(t, carry):
            gq_acc, gk_acc = carry
            rows = pl.ds(pl.multiple_of(t * tn, tn), tn)
            raws = [lb_[rows, :], vb_[rows, :]]
            dns = [gb_[rows, :], dkb_[rows, :]]
            rs = [lax.rsqrt(_headsum(a * a, bd) * (1.0 / HEAD_DIM) + EPS) for a in raws]
            zs = [a * r for a, r in zip(raws, rs)]
            dzs = [dn * gain for dn, gain in zip(dns, (gqs, gk))]
            means = [_headsum(dz * z, bd) * (1.0 / HEAD_DIM) for dz, z in zip(dzs, zs)]
            dq, dk = [r * (dz - z * mean) for r, dz, z, mean in zip(rs, dzs, zs, means)]
            gq, gkk = [jnp.sum(dn * z, axis=0, keepdims=True) for dn, z in zip(dns, zs)]
            dq_ref[rows, :] = dq.astype(BF16)
            dk_ref[rows, :] = dk.astype(BF16)
            dv_ref[rows, :] = dvb_[rows, :].astype(BF16)
            return gq_acc + gq * QK_SCALE, gk_acc + gkk

        zero = jnp.zeros((1, LANES), F32)
        gq_acc, gk_acc = lax.fori_loop(0, SEQ // tn, post, (zero, zero))
        gqg_ref[0:1, :] += gq_acc
        gkg_ref[0:1, :] += gk_acc

        @pl.when(pair == npairs - 1)
        def _():
            gqg_ref[0:1, :] = _fold_heads(gqg_ref[0:1, :])
            gkg_ref[0:1, :] = _fold_heads(gkg_ref[0:1, :])
            rs_stage[npairs]()

    hbm = pl.BlockSpec(memory_space=pl.ANY)
    vec = pl.BlockSpec((1, LANES), lambda p: (0, 0))
    blk8 = pl.BlockSpec((8, LANES), lambda p: (0, 0))
    out = pl.BlockSpec((SEQ, LANES), lambda p: (0, p))
    big = jax.ShapeDtypeStruct((SEQ, ATTN_W), BF16)
    nsem = RS_SEMS * nride
    return _call(
        body, name="attn_bwd", grid=(npairs,),
        in_specs=[hbm, hbm, hbm, hbm, vec, vec] + [hbm] * nride,
        out_specs=[out, out, out, out, blk8, blk8] + [hbm] * nride,
        out_shape=[big, big, big, big, jax.ShapeDtypeStruct((8, LANES), F32), jax.ShapeDtypeStruct((8, LANES), F32)]
        + [jax.ShapeDtypeStruct((2, g.shape[0] // 8, g.shape[1]), F32) for g in ride_along],
        scratch_shapes=[pltpu.VMEM((SEQ, LANES), F32) for _ in range(9)] + [pltpu.SemaphoreType.DMA((9,))]
        + _rs_scratch([g.shape for g in ride_along]) + [pltpu.SemaphoreType.DMA((nsem,)), pltpu.SemaphoreType.DMA((nsem,)),
                                     pltpu.SemaphoreType.DMA((nride,))],
        compiler_params=_params(),
    )(proj, o, lse, dyc, gq2, gk2, *[_rs_view(g) for g in ride_along])


def _mem_kv(mem, gain, wkv):
    def body(m_ref, g_ref, w_ref, kv_ref, hm_ref):
        mv = m_ref[...]
        ms = jnp.mean(mv * mv, axis=-1, keepdims=True)
        hm = (mv * lax.rsqrt(ms + EPS) * g_ref[...]).astype(BF16)
        hm_ref[...] = hm
        kv_ref[...] = _dot(hm, w_ref[...])

    return _call(
        body, name="mem_kv",
        out_shape=[jax.ShapeDtypeStruct((MEM_LEN, 2 * MEM_W), F32), jax.ShapeDtypeStruct((MEM_LEN, D_MODEL), BF16)],
        compiler_params=_params(),
    )(mem, gain, wkv)


def _mem_keys(kv_ref, kg_ref, bd, p):
    mk = kv_ref[:, p * LANES:(p + 1) * LANES]
    r = lax.rsqrt(_headsum(mk * mk, bd) * (1.0 / HEAD_DIM) + EPS)
    z = mk * r
    mkn = (z * kg_ref[:, p * LANES:(p + 1) * LANES]).astype(BF16)
    mvp = kv_ref[:, MEM_W + p * LANES:MEM_W + (p + 1) * LANES].astype(BF16)
    return mkn, mvp, r, z


def _mem_fwd(proj, kv, qg4, kg4):
    tm = 1024

    def body(q_ref, g_ref, kv_ref, qg_ref, kg_ref, om_ref, ym_ref):
        bd = _head_blockdiag()
        lo = _lo_mask(tm)
        keys, qns = [], []
        for p in range(2):
            cs = slice(p * LANES, (p + 1) * LANES)
            keys.append(_mem_keys(kv_ref, kg_ref, bd, p)[:2])
            q = q_ref[:, cs]
            qns.append(q * lax.rsqrt(_headsum(q * q, bd) * (1.0 / HEAD_DIM) + EPS) * (qg_ref[:, cs] * QK_SCALE))
        chains = [(p, h) for p in range(2) for h in range(2)]
        ss = [_dot_nt(jnp.where(lo if h == 0 else ~lo, qns[p], 0.0).astype(BF16), keys[p][0]) for p, h in chains]
        es = [jnp.exp(s - jnp.max(s, axis=-1, keepdims=True)) for s in ss]
        os_ = [_dot(e.astype(BF16), keys[p][1]) for e, (p, h) in zip(es, chains)]
        res = [o * (1.0 / jnp.sum(e, axis=-1, keepdims=True)) for o, e in zip(os_, es)]
        for p in range(2):
            cs = slice(p * LANES, (p + 1) * LANES)
            ov = jnp.where(lo, res[2 * p], res[2 * p + 1])
            g = g_ref[:, cs]
            om_ref[:, cs] = ov
            ym_ref[:, cs] = (ov * (g * _sigmoid(g))).astype(BF16)

    vec = pl.BlockSpec((1, MEM_W), lambda i: (0, 0))
    return _call(
        body, name="mem_fwd", grid=(SEQ // tm,),
        in_specs=[pl.BlockSpec((tm, MEM_W), lambda i: (i, C_MQ // MEM_W)),
                  pl.BlockSpec((tm, MEM_W), lambda i: (i, C_MG // MEM_W)),
                  pl.BlockSpec((MEM_LEN, 2 * MEM_W), lambda i: (0, 0)), vec, vec],
        out_specs=[pl.BlockSpec((tm, MEM_W), lambda i: (i, 0)), pl.BlockSpec((tm, MEM_W), lambda i: (i, 0))],
        out_shape=[jax.ShapeDtypeStruct((SEQ, MEM_W), F32), jax.ShapeDtypeStruct((SEQ, MEM_W), BF16)],
        compiler_params=_params(),
    )(proj, proj, kv, qg4, kg4)


def _mem_bwd(proj, om, dyc, kv, hm, mem, mgain, wkv, qg4, kg4):
    tm = 1024
    nsteps = SEQ // tm

    def body(q_ref, g_ref, om_ref, dy_ref, kv_ref, hm_ref, mem_ref, mg_ref, w_ref, qg_ref, kg_ref,
             dq_ref, dgt_ref, gqg_ref, gkg_ref, gw_ref, gmg_ref, dmk_ref, dmv_ref, gq_acc):
        i = pl.program_id(0)
        bd = _head_blockdiag()
        lo = _lo_mask(tm)
        lom = _lo_mask(MEM_LEN)

        @pl.when(i == 0)
        def _():
            dmk_ref[...] = jnp.zeros_like(dmk_ref)
            dmv_ref[...] = jnp.zeros_like(dmv_ref)
            gq_acc[...] = jnp.zeros_like(gq_acc)

        pairs = []
        for p in range(2):
            cs = slice(p * LANES, (p + 1) * LANES)
            mkn, mvp, _, _ = _mem_keys(kv_ref, kg_ref, bd, p)
            gqs = qg_ref[:, cs] * QK_SCALE
            q = q_ref[:, cs]
            r = lax.rsqrt(_headsum(q * q, bd) * (1.0 / HEAD_DIM) + EPS)
            z = q * r
            qn = z * gqs
            g = g_ref[:, cs]
            ov = om_ref[:, cs]
            dym = dy_ref[:, cs]
            sg = _sigmoid(g)
            dgt_ref[:, cs] = (dym * ov * (sg * (1.0 + g * (1.0 - sg)))).astype(BF16)
            do = dym * (g * sg)
            pairs.append(dict(cs=cs, mkn=mkn, mvp=mvp, gqs=gqs, r=r, z=z, qn=qn, qnb=qn.astype(BF16), do=do,
                              dob=do.astype(BF16), delta=_headsum(do * ov, bd)))
        chains = [(pr_, h) for pr_ in pairs for h in range(2)]
        mask = lambda h: lo if h == 0 else ~lo
        ss = [_dot_nt(jnp.where(mask(h), c["qn"], 0.0).astype(BF16), c["mkn"]) for c, h in chains]
        dps = [_dot_nt(jnp.where(mask(h), c["do"], 0.0).astype(BF16), c["mvp"]) for c, h in chains]
        prs, dss = [], []
        for s, dp, (c, h) in zip(ss, dps, chains):
            e = jnp.exp(s - jnp.max(s, axis=-1, keepdims=True))
            pr = e * (1.0 / jnp.sum(e, axis=-1, keepdims=True))
            prs.append(pr.astype(BF16))
            dss.append((pr * (dp - c["delta"][:, h * HEAD_DIM:h * HEAD_DIM + 1])).astype(BF16))
        dqs = [_dot(ds, c["mkn"]) for ds, (c, h) in zip(dss, chains)]
        dks = [_dot_tn(ds, c["qnb"]) for ds, (c, h) in zip(dss, chains)]
        dvs = [_dot_tn(pr, c["dob"]) for pr, (c, h) in zip(prs, chains)]
        for p, c in enumerate(pairs):
            cs, z, r = c["cs"], c["z"], c["r"]
            dqn = jnp.where(lo, dqs[2 * p], dqs[2 * p + 1])
            dmk_ref[:, cs] += jnp.where(lom, dks[2 * p], dks[2 * p + 1])
            dmv_ref[:, cs] += jnp.where(lom, dvs[2 * p], dvs[2 * p + 1])
            dz = dqn * c["gqs"]
            dq_ref[:, cs] = (r * (dz - z * (_headsum(dz * z, bd) * (1.0 / HEAD_DIM)))).astype(BF16)
            gq_acc[:, cs] += jnp.sum(dqn * z, axis=0, keepdims=True) * QK_SCALE

        @pl.when(i == nsteps - 1)
        def _():
            gqg_ref[...] = jnp.zeros_like(gqg_ref)
            gkg_ref[...] = jnp.zeros_like(gkg_ref)
            gqg_ref[0:1, :] = _fold_heads(gq_acc[:, 0:LANES] + gq_acc[:, LANES:2 * LANES])
            dkv = []
            gk = jnp.zeros((1, LANES), F32)
            for p in range(2):
                cs = slice(p * LANES, (p + 1) * LANES)
                _, _, r, z = _mem_keys(kv_ref, kg_ref, bd, p)
                dn = dmk_ref[:, cs]
                dz = dn * kg_ref[:, cs]
                gk = gk + jnp.sum(dn * z, axis=0, keepdims=True)
                dkv.append(r * (dz - z * (_headsum(dz * z, bd) * (1.0 / HEAD_DIM))))
            gkg_ref[0:1, :] = _fold_heads(gk)
            dkvb = jnp.concatenate(dkv + [dmv_ref[...]], axis=1).astype(BF16)
            gw_ref[...] = _dot_tn(hm_ref[...], dkvb)
            dhm = _dot_nt(dkvb, w_ref[...])
            mv = mem_ref[...]
            zm = mv * lax.rsqrt(jnp.mean(mv * mv, axis=-1, keepdims=True) + EPS)
            _put_rows(gmg_ref, jnp.sum(dhm * zm, axis=0, keepdims=True))

    const = lambda shape: pl.BlockSpec(shape, lambda i: (0,) * len(shape))
    row = lambda j: pl.BlockSpec((tm, MEM_W), lambda i: (i, j))
    blk8 = jax.ShapeDtypeStruct((8, LANES), F32)
    return _call(
        body, name="mem_bwd", grid=(nsteps,),
        in_specs=[row(C_MQ // MEM_W), row(C_MG // MEM_W), row(0), row((GMLP_W + ATTN_W) // MEM_W),
                  const((MEM_LEN, 2 * MEM_W)), const((MEM_LEN, D_MODEL)), const((MEM_LEN, D_MODEL)),
                  const((1, D_MODEL)), const((D_MODEL, 2 * MEM_W)), const((1, MEM_W)), const((1, MEM_W))],
        out_specs=[row(0), row(0), const((8, LANES)), const((8, LANES)),
                   const((D_MODEL, 2 * MEM_W)), const((8, LANES))],
        out_shape=[jax.ShapeDtypeStruct((SEQ, MEM_W), BF16), jax.ShapeDtypeStruct((SEQ, MEM_W), BF16),
                   blk8, blk8, jax.ShapeDtypeStruct((D_MODEL, 2 * MEM_W), F32), blk8],
        scratch_shapes=[pltpu.VMEM((MEM_LEN, MEM_W), F32), pltpu.VMEM((MEM_LEN, MEM_W), F32),
                        pltpu.VMEM((1, MEM_W), F32)],
        compiler_params=_params(),
    )(proj, proj, om, dyc, kv, hm, mem, mgain, wkv, qg4, kg4)


def _out_loss(yg, ya, ym, x, tgt, wo):
    tm = 512
    nsteps = SEQ // tm
    parts = ((0, GMLP_W), (GMLP_W, ATTN_W), (GMLP_W + ATTN_W, MEM_W))

    def body(yg_ref, ya_ref, ym_ref, x_ref, t_ref, w_ref, dy_ref, dyc_ref, gw_ref, ls_ref):
        i = pl.program_id(0)

        @pl.when(i == 0)
        def _():
            gw_ref[...] = jnp.zeros_like(gw_ref)
            ls_ref[...] = jnp.zeros_like(ls_ref)

        ys = (yg_ref[...], ya_ref[...], ym_ref[...])
        y = sum(_dot(yv, w_ref[r0:r0 + n, :]) for yv, (r0, n) in zip(ys, parts))
        err = x_ref[...] + y - t_ref[...]
        _put_rows(ls_ref, jnp.sum(err * err, axis=0, keepdims=True), accumulate=True)
        dy = err * (1.0 / D_MODEL)
        dy_ref[...] = dy
        dyb = dy.astype(BF16)
        dyc_ref[...] = _dot_nt(dyb, w_ref[...])
        for yv, (r0, n) in zip(ys, parts):
            gw_ref[r0:r0 + n, :] += _dot_tn(yv, dyb)

    row = lambda w: pl.BlockSpec((tm, w), lambda i: (i, 0))
    const = lambda shape: pl.BlockSpec(shape, lambda i: (0, 0))
    return _call(
        body, name="out_loss", grid=(nsteps,),
        in_specs=[row(GMLP_W), row(ATTN_W), row(MEM_W), row(D_MODEL), row(D_MODEL), const((D_MODEL, D_MODEL))],
        out_specs=[row(D_MODEL), row(D_MODEL), const((D_MODEL, D_MODEL)), const((8, LANES))],
        out_shape=[jax.ShapeDtypeStruct((SEQ, D_MODEL), F32), jax.ShapeDtypeStruct((SEQ, D_MODEL), F32),
                   jax.ShapeDtypeStruct((D_MODEL, D_MODEL), F32), jax.ShapeDtypeStruct((8, LANES), F32)],
        compiler_params=_params(),
    )(yg, ya, ym, x, tgt, wo)


def _proj_bwd(x, dy, gain, wt, dg, daq, dak, dav, dag, dmq, dmg):
    tm = 512
    nsteps = SEQ // tm
    pieces = ((C_GU, 3 * GMLP_W), (C_AQ, ATTN_W), (C_AK, ATTN_W), (C_AV, ATTN_W), (C_AG, ATTN_W),
              (C_MQ, MEM_W), (C_MG, MEM_W))

    def body(x_ref, dy_ref, g_ref, wt_hbm, p0, p1, p2, p3, p4, p5, p6, gx_ref, gwt_hbm, gg_ref, wt_v, acc, wt_sem, out_sems):
        i = pl.program_id(0)
        wt_load = pltpu.make_async_copy(wt_hbm, wt_v, wt_sem)

        @pl.when(i == 0)
        def _():
            wt_load.start()
            acc[...] = jnp.zeros_like(acc)
            gg_ref[...] = jnp.zeros_like(gg_ref)

        xv = x_ref[...]
        r = lax.rsqrt(jnp.mean(xv * xv, axis=-1, keepdims=True) + EPS)
        z = xv * r
        g = g_ref[...]
        h = (z * g).astype(BF16)
        pl.when(i == 0)(wt_load.wait)
        flush = [pltpu.make_async_copy(acc.at[c0:c0 + w, :], gwt_hbm.at[c0:c0 + w, :], out_sems.at[n])
                 for n, (c0, w) in enumerate(pieces)]
        dh = jnp.zeros((tm, D_MODEL), F32)
        for n, (pref, (c0, w)) in enumerate(zip((p0, p1, p2, p3, p4, p5, p6), pieces)):
            dp = pref[...]
            dh = dh + _dot(dp, wt_v[c0:c0 + w, :])
            acc[c0:c0 + w, :] += _dot_tn(dp, h)
            pl.when(i == nsteps - 1)(flush[n].start)
        _put_rows(gg_ref, jnp.sum(dh * z, axis=0, keepdims=True), accumulate=True)
        dz = dh * g
        gx_ref[...] = dy_ref[...] + r * (dz - z * jnp.mean(dz * z, axis=-1, keepdims=True))

        @pl.when(i == nsteps - 1)
        def _():
            for cp in flush:
                cp.wait()

    row = lambda w: pl.BlockSpec((tm, w), lambda i: (i, 0))
    hbm = pl.BlockSpec(memory_space=pl.ANY)
    vec = pl.BlockSpec((1, D_MODEL), lambda i: (0, 0))
    return _call(
        body, name="proj_bwd", grid=(nsteps,),
        in_specs=[row(D_MODEL), row(D_MODEL), vec, hbm] + [row(w) for _, w in pieces],
        out_specs=[row(D_MODEL), hbm, pl.BlockSpec((8, LANES), lambda i: (0, 0))],
        out_shape=[jax.ShapeDtypeStruct((SEQ, D_MODEL), F32), jax.ShapeDtypeStruct((IN_W, D_MODEL), F32),
                   jax.ShapeDtypeStruct((8, LANES), F32)],
        scratch_shapes=[pltpu.VMEM((IN_W, D_MODEL), BF16), pltpu.VMEM((IN_W, D_MODEL), F32), pltpu.SemaphoreType.DMA,
                        pltpu.SemaphoreType.DMA((len(pieces),))],
        compiler_params=_params(),
    )(x, dy, gain, wt, dg, daq, dak, dav, dag, dmq, dmg)


AG_SEMS = 8


def _gather_stages(ins, lands, send_sems, recv_sems):
    n = len(ins)
    nrows = [a.shape[0] for a in ins]
    x, y, c = lax.axis_index("x"), lax.axis_index("y"), lax.axis_index("c")
    sib, xn, yn = (x, y, 1 - c), (1 - x, y, c), (x, 1 - y, c)
    me, cx, cy, cd = 2 * x + y, 2 * (1 - x) + y, 2 * x + (1 - y), 2 * (1 - x) + (1 - y)

    def part(a, chip, hf, quarter=None):
        rows = nrows[a] // 2
        base = chip * nrows[a] + hf * rows
        if quarter is not None:
            rows = rows // 2
            base = base + quarter * rows
        return lands[a].at[pl.ds(pl.multiple_of(base, 16), rows), :]

    def copy(a, j, ref, to):
        k = AG_SEMS * a + j
        return pltpu.make_async_remote_copy(src_ref=ref, dst_ref=ref, send_sem=send_sems.at[k],
                                            recv_sem=recv_sems.at[k], device_id=to, device_id_type=MESH)

    def own(a):
        return [copy(a, 0, part(a, me, c), xn), copy(a, 1, part(a, me, c), yn)]

    def neighbours(a):
        return [copy(a, 4, part(a, cx, c, 1), yn), copy(a, 2, part(a, cx, c), sib),
                copy(a, 5, part(a, cy, c, 0), xn), copy(a, 3, part(a, cy, c), sib)]

    def diagonal(a):
        return [copy(a, 7, part(a, cd, c, 1), sib), copy(a, 6, part(a, cd, c, 0), sib)]

    def send_own():
        for a in range(n):
            lands[a][pl.ds(pl.multiple_of(me * nrows[a], 16), nrows[a]), :] = ins[a][...].astype(BF16)
            for cp in own(a):
                cp.start()

    def pass_on_neighbours():
        for a in range(n):
            copy(a, 0, part(a, cx, c), xn).wait_recv()
            copy(a, 1, part(a, cy, c), yn).wait_recv()
            for cp in neighbours(a):
                cp.start()

    def pass_on_diagonal():
        for a in range(n):
            copy(a, 4, part(a, cd, c, 1), yn).wait_recv()
            copy(a, 5, part(a, cd, c, 0), xn).wait_recv()
            for cp in diagonal(a):
                cp.start()

    def y_complete():
        for a in range(n):
            copy(a, 3, part(a, cy, 1 - c), sib).wait_recv()

    def x_complete():
        for a in range(n):
            copy(a, 2, part(a, cx, 1 - c), sib).wait_recv()

    def diagonal_complete():
        for a in range(n):
            copy(a, 6, part(a, cd, 1 - c, 0), sib).wait_recv()
            copy(a, 7, part(a, cd, 1 - c, 1), sib).wait_recv()

    def sends_done():
        for a in range(n):
            for cp in own(a) + neighbours(a) + diagonal(a):
                cp.wait_send()

    def finish():
        y_complete()
        x_complete()
        diagonal_complete()
        sends_done()

    return (send_own, pass_on_neighbours, pass_on_diagonal, finish), (y_complete, x_complete, diagonal_complete, sends_done)


RS_SEMS = 6
RS_KINDS = (((2, 2), 1, F32), ((2, 2), 1, F32), ((2, 2), 2, BF16), ((2, 2), 2, BF16), ((2, 2), 2, F32),
            ((2,), 2, BF16), ((2,), 2, BF16), ((2,), 1, F32))


def _rs_view(g):
    return g.reshape(2, 2, 2, g.shape[0] // 8, g.shape[1])


def _rs_scratch(shapes):
    return [pltpu.VMEM(lead + (r // 8, w // split), dt) for lead, split, dt in RS_KINDS for r, w in shapes]


def _rs_stages(gs, outs, bufs, send_sems, recv_sems, local_sems, widths):
    n = len(gs)
    loc, ra, s_b, r_b, acc1, s_c, r_c, fin = (bufs[n * i:n * i + n] for i in range(len(RS_KINDS)))
    half_w = [w // 2 for w in widths]
    x, y, c = lax.axis_index("x"), lax.axis_index("y"), lax.axis_index("c")
    sib, xn, yn = (x, y, 1 - c), (1 - x, y, c), (x, 1 - y, c)

    def copy(a, j, src, dst, to):
        k = RS_SEMS * a + j
        return pltpu.make_async_remote_copy(src_ref=src, dst_ref=dst, send_sem=send_sems.at[k],
                                            recv_sem=recv_sems.at[k], device_id=to, device_id_type=MESH)

    def step_a(a):
        return [copy(a, 0, gs[a].at[:, :, 1 - c], ra[a], sib),
                pltpu.make_async_copy(gs[a].at[:, :, c], loc[a], local_sems.at[a])]

    def step_b(a):
        return copy(a, 1, s_b[a].at[0], r_b[a].at[0], xn), copy(a, 2, s_b[a].at[1], r_b[a].at[1], yn)

    def step_c(a):
        return copy(a, 3, s_c[a].at[0], r_c[a].at[0], yn), copy(a, 4, s_c[a].at[1], r_c[a].at[1], xn)

    def step_d(a, half):
        rows = fin[a].at[half]
        return copy(a, 5, rows, rows, sib)

    def start():
        for a in range(n):
            for cp in step_a(a):
                cp.start()

    def a_to_b():
        for a in range(n):
            for cp in step_a(a):
                cp.wait()
            ra[a][...] = loc[a][...] + ra[a][...]
            s_b[a][0] = ra[a][1 - x, :, :, :half_w[a]].astype(BF16)
            s_b[a][1] = ra[a][:, 1 - y, :, half_w[a]:].astype(BF16)
            for cp in step_b(a):
                cp.start()

    def b_to_c():
        for a in range(n):
            for cp in step_b(a):
                cp.wait()
            acc1[a][0] = ra[a][x, :, :, :half_w[a]] + r_b[a][0].astype(F32)
            acc1[a][1] = ra[a][:, y, :, half_w[a]:] + r_b[a][1].astype(F32)
            s_c[a][0] = acc1[a][0, 1 - y].astype(BF16)
            s_c[a][1] = acc1[a][1, 1 - x].astype(BF16)
            for cp in step_c(a):
                cp.start()

    def c_to_d():
        for a in range(n):
            for cp in step_c(a):
                cp.wait()
            fin[a][c, :, :half_w[a]] = acc1[a][0, y] + r_c[a][0].astype(F32)
            fin[a][c, :, half_w[a]:] = acc1[a][1, x] + r_c[a][1].astype(F32)
            step_d(a, c).start()

    def finish():
        to_hbm = [pltpu.make_async_copy(fin[a], outs[a], local_sems.at[a]) for a in range(n)]
        for a in range(n):
            step_d(a, 1 - c).wait_recv()
            step_d(a, c).wait_send()
            to_hbm[a].start()
        for cp in to_hbm:
            cp.wait()

    return start, a_to_b, b_to_c, c_to_d, finish


def _reduce_grads(gwt, g_ws, tiny):
    cw = gwt.shape[1] // RS_CHUNKS
    chunk_shape = (gwt.shape[0], cw)

    def body(g0, ws_in, tiny_in, *rest):
        outs, o_ws, o_tiny = rest[:RS_CHUNKS], rest[RS_CHUNKS], rest[RS_CHUNKS + 1]
        rest = rest[RS_CHUNKS + 2:]
        nb = len(RS_KINDS) * RS_CHUNKS
        sm, sa, sb, sc, acc_s, send_sems, recv_sems, local_sems = rest[nb:]
        blocks = [g0.at[:, :, :, :, pl.ds(j * cw, cw)] for j in range(RS_CHUNKS)]
        start, a_to_b, b_to_c, c_to_d, finish = _rs_stages(blocks, outs, rest[:nb], send_sems, recv_sems, local_sems,
                                                           [cw] * RS_CHUNKS)
        n_ws = ws_in.shape[0]
        sm[0:n_ws, :] = ws_in[...]
        sm[n_ws:, :] = tiny_in[...]
        x, y, c = lax.axis_index("x"), lax.axis_index("y"), lax.axis_index("c")

        def small(j, src, dst, to):
            k = RS_SEMS * RS_CHUNKS + j
            return pltpu.make_async_remote_copy(src_ref=src, dst_ref=dst, send_sem=send_sems.at[k],
                                                recv_sem=recv_sems.at[k], device_id=to, device_id_type=MESH)

        along_c, along_x, along_y = (small(0, sm, sa, (x, y, 1 - c)), small(1, acc_s, sb, (1 - x, y, c)),
                                     small(2, sb, sc, (x, 1 - y, c)))
        start()
        along_c.start()
        a_to_b()
        along_c.wait()
        acc_s[...] = sm[...] + sa[...]
        along_x.start()
        b_to_c()
        along_x.wait()
        sb[...] = acc_s[...] + sb[...]
        along_y.start()
        c_to_d()
        along_y.wait()
        o_ws[...] = sb[0:n_ws, :] + sc[0:n_ws, :]
        o_tiny[...] = sb[n_ws:, :] + sc[n_ws:, :]
        finish()

    vm = pl.BlockSpec(memory_space=pltpu.VMEM)
    hbm = pl.BlockSpec(memory_space=pl.ANY)
    small_shape = (g_ws.shape[0] + tiny.shape[0], LANES)
    scratch = _rs_scratch([chunk_shape] * RS_CHUNKS) + [pltpu.VMEM(small_shape, F32) for _ in range(5)]
    nsem = RS_SEMS * RS_CHUNKS + 3
    scratch += [pltpu.SemaphoreType.DMA((nsem,)), pltpu.SemaphoreType.DMA((nsem,)), pltpu.SemaphoreType.DMA((RS_CHUNKS,))]
    return _call(
        body, name="reduce_grads",
        out_shape=[jax.ShapeDtypeStruct((2, gwt.shape[0] // 8, cw), F32)] * RS_CHUNKS
        + [jax.ShapeDtypeStruct(g_ws.shape, F32), jax.ShapeDtypeStruct(tiny.shape, F32)],
        in_specs=[hbm, vm, vm],
        out_specs=[hbm] * RS_CHUNKS + [vm, vm],
        scratch_shapes=scratch,
        compiler_params=_params(),
    )(_rs_view(gwt), g_ws, tiny)


def _adam_update(w, g, m, v):
    nm = ADAM_B1 * m + (1.0 - ADAM_B1) * g
    nv = ADAM_B2 * v + (1.0 - ADAM_B2) * (g * g)
    m_hat = nm / (1.0 - ADAM_B1 ** ADAM_STEP)
    v_hat = nv / (1.0 - ADAM_B2 ** ADAM_STEP)
    return -ADAM_LR * (m_hat / (jnp.sqrt(v_hat) + ADAM_EPS) + ADAM_WD * w), nm, nv


def _adamw(w, g, m, v):
    rows, cols = w.shape
    tm = max(t for t in range(8, 257, 8) if rows % t == 0)
    parts = tuple(g) if isinstance(g, (tuple, list)) else (g,)
    n = len(parts)

    def body(w_ref, m_ref, v_ref, *refs):
        gv = jnp.concatenate([r[...] for r in refs[:n]], axis=1)
        d_ref, nm_ref, nv_ref = refs[n:n + 3]
        d_ref[...], nm_ref[...], nv_ref[...] = _adam_update(w_ref[...], gv, m_ref[...], v_ref[...])
        if n > 1:
            refs[n + 3][...] = gv

    blk = pl.BlockSpec((tm, cols), lambda i: (i, 0))
    nout = 3 if n == 1 else 4
    res = _call(
        body, name="adamw", grid=(rows // tm,),
        in_specs=[blk] * 3 + [pl.BlockSpec((tm, p.shape[1]), lambda i: (i, 0)) for p in parts], out_specs=[blk] * nout,
        out_shape=[jax.ShapeDtypeStruct((rows, cols), F32)] * nout,
        compiler_params=_params(),
    )(w, m, v, *parts)
    return (parts[0] if n == 1 else res[3], *res[:3])


def _adamw_tiny(tiny, weights, ms, vs):
    shapes = [w.shape for w in weights]
    n = len(weights)

    def grad_of(t_ref, k, shape):
        base = 8 * k
        if shape[1] > LANES:
            return [t_ref[base + j:base + j + 1, :] for j in range(shape[1] // LANES)]
        return [t_ref[base:base + shape[0], 0:shape[1]]]

    def body(t_ref, *refs):
        w_refs, m_refs, v_refs = refs[:n], refs[n:2 * n], refs[2 * n:3 * n]
        loss_ref, outs = refs[3 * n], refs[3 * n + 1:]
        loss_ref[...] = (0.5 / D_MODEL) * jnp.sum(t_ref[8 * n:8 * n + 8, :], keepdims=True)
        for k, shape in enumerate(shapes):
            g_ref, d_ref, nm_ref, nv_ref = outs[4 * k:4 * k + 4]
            for j, g in enumerate(grad_of(t_ref, k, shape)):
                cols = slice(j * LANES, (j + 1) * LANES) if shape[1] > LANES else slice(None)
                g_ref[:, cols] = g
                d_ref[:, cols], nm_ref[:, cols], nv_ref[:, cols] = _adam_update(
                    w_refs[k][:, cols], g, m_refs[k][:, cols], v_refs[k][:, cols])

    out_shape = [jax.ShapeDtypeStruct((1, 1), F32)]
    for shape in shapes:
        out_shape += [jax.ShapeDtypeStruct(shape, F32)] * 4
    return _call(body, name="adamw_tiny", out_shape=out_shape, compiler_params=_params())(tiny, *weights, *ms, *vs)


def _local_grads(x, mem, tgt, norm_gain, wt_sh, gmlp_v_gain, gmlp_w_s, gmlp_b, attn_q_gain, attn_k_gain,
                 mem_norm_gain, wkv_sh, mem_q_gain, mem_k_gain, wo_sh):
    vg = gmlp_v_gain.reshape(1, GMLP_W)
    bias_full = jnp.repeat(gmlp_b.T, HEAD_DIM, axis=1)
    gq2, gk2 = jnp.tile(attn_q_gain, (1, 2)), jnp.tile(attn_k_gain, (1, 2))
    qg4, kg4 = jnp.tile(mem_q_gain, (1, 4)), jnp.tile(mem_k_gain, (1, 4))

    proj, wt = _gather_proj(x, norm_gain, wt_sh)
    yg = _gmlp_fwd(proj, vg, gmlp_w_s, bias_full)
    o, lse, ya, wkv, wo = _attn_fwd(proj, gq2, gk2, wkv_sh, wo_sh)
    kv, hm = _mem_kv(mem, mem_norm_gain, wkv)
    om, ym = _mem_fwd(proj, kv, qg4, kg4)
    dy, dyc, g_wo, err2 = _out_loss(yg, ya, ym, x, tgt, wo)
    dmq, dmg, g_mq, g_mk, g_wkv, g_mng = _mem_bwd(proj, om, dyc, kv, hm, mem, mem_norm_gain, wkv, qg4, kg4)
    daq, dak, dav, dag, g_aq, g_ak, g_wkv_sh, g_wo_sh = _attn_bwd(proj, o, lse, dyc, gq2, gk2, g_wkv, g_wo)
    dg, g_ws, g_b, g_vg = _gmlp_bwd(proj, dyc, vg, gmlp_w_s, bias_full)
    gx, g_wt, g_ng = _proj_bwd(x, dy, norm_gain, wt, dg, daq, dak, dav, dag, dmq, dmg)

    tiny = jnp.concatenate([g_ng, g_vg, g_b, g_aq, g_ak, g_mng, g_mq, g_mk, err2], axis=0)
    return gx, g_wt, g_wkv_sh, g_wo_sh, g_ws.reshape(4 * CHUNK, CHUNK), tiny


def kernel(x, mem, norm_gain, w_in, gmlp_v_gain, gmlp_w_s, gmlp_b, attn_q_gain, attn_k_gain, mem_norm_gain, w_mem_kv, mem_q_gain, mem_k_gain, w_out, loss_target, m_norm_gain, m_w_in, m_gmlp_v_gain, m_gmlp_w_s, m_gmlp_b, m_attn_q_gain, m_attn_k_gain, m_mem_norm_gain, m_w_mem_kv, m_mem_q_gain, m_mem_k_gain, m_w_out, v_norm_gain, v_w_in, v_gmlp_v_gain, v_gmlp_w_s, v_gmlp_b, v_attn_q_gain, v_attn_k_gain, v_mem_norm_gain, v_w_mem_kv, v_mem_q_gain, v_mem_k_gain, v_w_out):
    gx, g_wt, g_wkv_sh, g_wo_sh, g_ws, tiny = _local_grads(
        x[0], mem[0], loss_target[0], norm_gain, w_in[0].T, gmlp_v_gain[0], gmlp_w_s[0], gmlp_b[0],
        attn_q_gain, attn_k_gain, mem_norm_gain, w_mem_kv[0], mem_q_gain, mem_k_gain, w_out[0])
    *g_wt_sh, g_ws, tiny = _reduce_grads(g_wt, g_ws, tiny)
    chip_block = lambda g: g.reshape(2 * g.shape[1], g.shape[2])
    g_wt_sh = tuple(chip_block(g) for g in g_wt_sh)
    g_wkv_sh, g_wo_sh = chip_block(g_wkv_sh), chip_block(g_wo_sh)

    ws = (norm_gain, w_in, gmlp_v_gain, gmlp_w_s, gmlp_b, attn_q_gain, attn_k_gain, mem_norm_gain, w_mem_kv,
          mem_q_gain, mem_k_gain, w_out)
    ms = (m_norm_gain, m_w_in, m_gmlp_v_gain, m_gmlp_w_s, m_gmlp_b, m_attn_q_gain, m_attn_k_gain, m_mem_norm_gain,
          m_w_mem_kv, m_mem_q_gain, m_mem_k_gain, m_w_out)
    vs = (v_norm_gain, v_w_in, v_gmlp_v_gain, v_gmlp_w_s, v_gmlp_b, v_attn_q_gain, v_attn_k_gain, v_mem_norm_gain,
          v_w_mem_kv, v_mem_q_gain, v_mem_k_gain, v_w_out)
    form = {1: lambda a: a[0].T, 3: lambda a: a.reshape(4 * CHUNK, CHUNK), 2: lambda a: a[0], 4: lambda a: a[0],
            8: lambda a: a[0], 11: lambda a: a[0]}
    back = {1: lambda a: a.T[None], 3: lambda a: a.reshape(1, 4, CHUNK, CHUNK), 2: lambda a: a[None],
            4: lambda a: a[None], 8: lambda a: a[None], 11: lambda a: a[None]}
    fwd = lambda t, i: form.get(i, lambda a: a)(t[i])
    out = {}
    for i, g in ((1, g_wt_sh), (3, g_ws), (8, g_wkv_sh), (11, g_wo_sh)):
        out[i] = _adamw(fwd(ws, i), g, fwd(ms, i), fwd(vs, i))
    res = _adamw_tiny(tiny, [fwd(ws, i) for i in TINY_ORDER], [fwd(ms, i) for i in TINY_ORDER],
                      [fwd(vs, i) for i in TINY_ORDER])
    for k, i in enumerate(TINY_ORDER):
        out[i] = res[1 + 4 * k:5 + 4 * k]
    leaves = [[back.get(i, lambda a: a)(out[i][j]) for i in range(12)] for j in range(4)]
    return (res[0].reshape(()), gx[None], *leaves[0], *leaves[1], *leaves[2], *leaves[3])
```

```python
import math

import jax
import jax.numpy as jnp
from jax import lax
from jax.experimental import pallas as pl
from jax.experimental.pallas import tpu as pltpu

F32 = jnp.float32
BF16 = jnp.bfloat16

SEQ = 4096
D_MODEL = 1024
HEAD_DIM = 64
LANES = 128
CHUNK = 128
GMLP_W, ATTN_W, MEM_W = 256, 512, 256
IN_W = 3 * GMLP_W + 4 * ATTN_W + 2 * MEM_W
MEM_LEN = 256
DILATIONS = (16, 4, 1)
EPS = 1e-6
QK_SCALE = 1.0 / math.sqrt(HEAD_DIM)
C_GU, C_GV, C_GG, C_AQ, C_AK, C_AV, C_AG, C_MQ, C_MG = 0, 256, 512, 768, 1280, 1792, 2304, 2816, 3072

ADAM_LR, ADAM_B1, ADAM_B2, ADAM_EPS, ADAM_WD, ADAM_STEP = 0.001, 0.9, 0.999, 1e-08, 0.01, 10

VMEM_LIMIT = 48 * 1024 * 1024
RS_CHUNKS = 4
ATTN_UNROLL = 4
MESH = pl.DeviceIdType.MESH

TINY_ORDER = (0, 2, 4, 5, 6, 7, 9, 10)


def _call(body, **kw):
    return pl.pallas_call(body, **kw)


def _params(**kw):
    return pltpu.CompilerParams(vmem_limit_bytes=VMEM_LIMIT, **kw)


def _dot(a, b):
    return jnp.dot(a, b, preferred_element_type=F32)


def _dot_nt(a, b):
    return lax.dot_general(a, b, (((1,), (1,)), ((), ())), preferred_element_type=F32)


def _dot_tn(a, b):
    return lax.dot_general(a, b, (((0,), (0,)), ((), ())), preferred_element_type=F32)


def _head_blockdiag():
    r = lax.shift_right_logical(lax.broadcasted_iota(jnp.int32, (LANES, LANES), 0), 6)
    c = lax.shift_right_logical(lax.broadcasted_iota(jnp.int32, (LANES, LANES), 1), 6)
    return jnp.where(r == c, 1.0, 0.0).astype(BF16)


def _headsum(v, bd):
    hi = v.astype(BF16)
    lo = (v - hi.astype(F32)).astype(BF16)
    return _dot(hi, bd) + _dot(lo, bd)


def _lo_mask(rows):
    return lax.broadcasted_iota(jnp.int32, (rows, LANES), 1) < HEAD_DIM


def _sigmoid(x):
    return 1.0 / (1.0 + jnp.exp(-x))


def _fold_heads(v):
    return v + pltpu.roll(v, HEAD_DIM, 1)


def _put_rows(ref, vec, accumulate=False):
    for j in range(vec.shape[1] // LANES):
        piece = vec[:, j * LANES:(j + 1) * LANES]
        ref[j:j + 1, :] = ref[j:j + 1, :] + piece if accumulate else piece


def _gather_proj(x, gain, wt_sh):
    tm = 1024
    nrow = SEQ // tm
    widths = (768, 896, 768, 896)
    nunits = len(widths)
    pair = 2 * wt_sh.shape[0]
    assert pair % LANES == 0 and sum(widths[:2]) == pair

    def body(x_ref, g_ref, wt_sh_ref, proj_hbm, wt_hbm, h_scr, land, res, send_sems, recv_sems, out_sems, copy_sem):
        u, i = pl.program_id(0), pl.program_id(1)
        cx_, cy_ = lax.axis_index("x"), lax.axis_index("y")
        (send_own, pass_on_neighbours, pass_on_diagonal, _), (y_complete, x_complete, diagonal_complete, sends_done) = (
            _gather_stages((wt_sh_ref,), (land,), send_sems, recv_sems))
        first = lambda k: (u == k) & (i == 0)
        last = (u == nunits - 1) & (i == nrow - 1)
        to_hbm = pltpu.make_async_copy(land, wt_hbm, copy_sem)

        pl.when(first(0))(send_own)

        @pl.when(u == 0)
        def _():
            xv = x_ref[...]
            ms = jnp.mean(xv * xv, axis=-1, keepdims=True)
            h_scr[pl.ds(pl.multiple_of(i * tm, tm), tm), :] = (xv * lax.rsqrt(ms + EPS) * g_ref[...]).astype(BF16)

        @pl.when(first(1))
        def _():
            pass_on_neighbours()
            y_complete()

        @pl.when(first(2))
        def _():
            x_complete()
            pass_on_diagonal()

        @pl.when(first(3))
        def _():
            diagonal_complete()
            to_hbm.start()

        mine, other = pair * cx_, pair * (1 - cx_)
        col0 = (mine + 896 * cy_, mine + 768 * (1 - cy_), other + 896 * cy_, other + 768 * (1 - cy_))
        slot = i % 2
        rows = pl.ds(pl.multiple_of(i * tm, tm), tm)

        def writeback(k, rows_):
            c0 = pl.multiple_of(col0[k], LANES)
            return pltpu.make_async_copy(res.at[slot, :, pl.ds(0, widths[k])], proj_hbm.at[rows_, pl.ds(c0, widths[k])],
                                         out_sems.at[slot])

        for k in range(nunits):
            @pl.when(u == k)
            def _(k=k):
                pl.when(i >= 2)(writeback(k, rows).wait)
                if k > 0:
                    pl.when(i < 2)(writeback(k - 1, rows).wait)
                w_rows = land[pl.ds(pl.multiple_of(col0[k], LANES), widths[k]), :]
                res[slot, :, 0:widths[k]] = _dot_nt(h_scr[rows, :], w_rows)
                writeback(k, rows).start()

        @pl.when(last)
        def _():
            sends_done()
            to_hbm.wait()
            for s in range(2):
                pltpu.make_async_copy(res.at[s, :, pl.ds(0, widths[-1])], proj_hbm.at[rows, pl.ds(0, widths[-1])], out_sems.at[s]).wait()

    full = jax.ShapeDtypeStruct((4 * wt_sh.shape[0], wt_sh.shape[1]), BF16)
    hbm = pl.BlockSpec(memory_space=pl.ANY)
    return _call(
        body, name="gather_proj", grid=(nunits, nrow),
        in_specs=[pl.BlockSpec((tm, D_MODEL), lambda u, i: (jnp.where(u == 0, i, nrow - 1), 0)),
                  pl.BlockSpec((1, D_MODEL), lambda u, i: (0, 0)), pl.BlockSpec(wt_sh.shape, lambda u, i: (0, 0))],
        out_specs=[hbm, hbm],
        out_shape=[jax.ShapeDtypeStruct((SEQ, IN_W), F32), full],
        scratch_shapes=[pltpu.VMEM((SEQ, D_MODEL), BF16), pltpu.VMEM(full.shape, BF16), pltpu.VMEM((2, tm, max(widths)), F32),
                        pltpu.SemaphoreType.DMA((AG_SEMS,)), pltpu.SemaphoreType.DMA((AG_SEMS,)),
                        pltpu.SemaphoreType.DMA((2,)), pltpu.SemaphoreType.DMA],
        compiler_params=_params(),
    )(x, gain, wt_sh)


def _gmlp_weights(w_ref):
    ti = lax.broadcasted_iota(jnp.int32, (CHUNK, CHUNK), 0)
    si = lax.broadcasted_iota(jnp.int32, (CHUNK, CHUNK), 1)
    tril = si <= ti
    return tril, [jnp.where(tril, w_ref[h], 0.0).astype(BF16) for h in range(4)]


def _gmlp_fwd(proj, vgain, w_s, bias_full):
    tm = 1024

    def body(p_ref, vg_ref, w_ref, b_ref, y_ref):
        bd = _head_blockdiag()
        lo = _lo_mask(CHUNK)
        _, wm = _gmlp_weights(w_ref)
        units = [(pl.ds(c * CHUNK, CHUNK), p) for c in range(tm // CHUNK) for p in range(2)]
        col = lambda c0, p: slice(c0 + p * LANES, c0 + (p + 1) * LANES)
        vs = [p_ref[rows, col(C_GV, p)] for rows, p in units]
        rs = [lax.rsqrt(_headsum(v * v, bd) * (1.0 / HEAD_DIM) + EPS) for v in vs]
        vns = [(v * r * vg_ref[:, col(0, p)]).astype(BF16) for v, r, (_, p) in zip(vs, rs, units)]
        sps = [jnp.where(lo, _dot(wm[2 * p], vn), _dot(wm[2 * p + 1], vn)) + b_ref[:, col(0, p)] for vn, (_, p) in zip(vns, units)]
        for sp, (rows, p) in zip(sps, units):
            gt = p_ref[rows, col(C_GG, p)]
            y_ref[rows, col(0, p)] = (p_ref[rows, col(C_GU, p)] * sp * (gt * _sigmoid(gt))).astype(BF16)

    return _call(
        body, name="gmlp_fwd", grid=(SEQ // tm,),
        in_specs=[pl.BlockSpec((tm, 3 * GMLP_W), lambda i: (i, 0)),
                  pl.BlockSpec((1, GMLP_W), lambda i: (0, 0)),
                  pl.BlockSpec((4, CHUNK, CHUNK), lambda i: (0, 0, 0)),
                  pl.BlockSpec((CHUNK, GMLP_W), lambda i: (0, 0))],
        out_specs=pl.BlockSpec((tm, GMLP_W), lambda i: (i, 0)),
        out_shape=jax.ShapeDtypeStruct((SEQ, GMLP_W), BF16),
        compiler_params=_params(),
    )(proj, vgain, w_s, bias_full)


def _gmlp_bwd(proj, dyc, vgain, w_s, bias_full):
    tm = 1024
    nsteps = SEQ // tm

    def body(p_ref, dy_ref, vg_ref, w_ref, b_ref, dg_ref, gw_ref, gb_ref, gv_ref):
        i = pl.program_id(0)
        bd = _head_blockdiag()
        lo = _lo_mask(CHUNK)
        tril, wm = _gmlp_weights(w_ref)
        ri = lax.broadcasted_iota(jnp.int32, (16, LANES), 0)
        li = lax.broadcasted_iota(jnp.int32, (16, LANES), 1)
        head_rows = [jnp.where(((ri == 2 * p) & (li < HEAD_DIM)) | ((ri == 2 * p + 1) & (li >= HEAD_DIM)), 1.0, 0.0).astype(BF16)
                     for p in range(2)]

        @pl.when(i == 0)
        def _():
            gw_ref[...] = jnp.zeros_like(gw_ref)
            gb_ref[...] = jnp.zeros_like(gb_ref)
            gv_ref[...] = jnp.zeros_like(gv_ref)

        units = [(pl.ds(c * CHUNK, CHUNK), p) for c in range(tm // CHUNK) for p in range(2)]
        col = lambda c0, p: slice(c0 + p * LANES, c0 + (p + 1) * LANES)
        vs = [p_ref[rows, col(C_GV, p)] for rows, p in units]
        rs = [lax.rsqrt(_headsum(v * v, bd) * (1.0 / HEAD_DIM) + EPS) for v in vs]
        zs = [v * r for v, r in zip(vs, rs)]
        vns = [(z * vg_ref[:, col(0, p)]).astype(BF16) for z, (_, p) in zip(zs, units)]
        sps = [jnp.where(lo, _dot(wm[2 * p], vn), _dot(wm[2 * p + 1], vn)) + b_ref[:, col(0, p)] for vn, (_, p) in zip(vns, units)]
        dsps = []
        for sp, (rows, p) in zip(sps, units):
            u = p_ref[rows, col(C_GU, p)]
            gt = p_ref[rows, col(C_GG, p)]
            dy = dy_ref[rows, col(0, p)]
            sg = _sigmoid(gt)
            sl = gt * sg
            dg_ref[rows, col(C_GU, p)] = (dy * sp * sl).astype(BF16)
            dg_ref[rows, col(C_GG, p)] = (dy * u * sp * (sg * (1.0 + gt * (1.0 - sg)))).astype(BF16)
            dsps.append(dy * u * sl)
        dspbs = [dsp.astype(BF16) for dsp in dsps]
        dvns = [jnp.where(lo, _dot_tn(wm[2 * p], dspb), _dot_tn(wm[2 * p + 1], dspb)) for dspb, (_, p) in zip(dspbs, units)]
        gws = [(_dot_nt(jnp.where(lo, dsp, 0.0).astype(BF16), vn), _dot_nt(jnp.where(lo, 0.0, dsp).astype(BF16), vn))
               for dsp, vn in zip(dsps, vns)]
        gbs = [(_dot_nt(head_rows[p], dspb) + _dot_nt(head_rows[p], (dsp - dspb.astype(F32)).astype(BF16)))[0:8]
               for dsp, dspb, (_, p) in zip(dsps, dspbs, units)]
        for p in range(2):
            mine = [n for n, (_, q) in enumerate(units) if q == p]
            gw_ref[2 * p] += sum(gws[n][0] for n in mine)
            gw_ref[2 * p + 1] += sum(gws[n][1] for n in mine)
            gvp = sum(jnp.sum(dvns[n] * zs[n], axis=0, keepdims=True) for n in mine)
            gv_ref[2 * p:2 * p + 1, :] += gvp
            gv_ref[2 * p + 1:2 * p + 2, :] += pltpu.roll(gvp, HEAD_DIM, 1)
        gb_ref[...] += sum(gbs)
        for dvn, z, r, (rows, p) in zip(dvns, zs, rs, units):
            dz = dvn * vg_ref[:, col(0, p)]
            dg_ref[rows, col(C_GV, p)] = (r * (dz - z * (_headsum(dz * z, bd) * (1.0 / HEAD_DIM)))).astype(BF16)

        @pl.when(i == nsteps - 1)
        def _():
            for h in range(4):
                gw_ref[h] = jnp.where(tril, gw_ref[h], 0.0)

    return _call(
        body, name="gmlp_bwd", grid=(nsteps,),
        in_specs=[pl.BlockSpec((tm, 3 * GMLP_W), lambda i: (i, 0)),
                  pl.BlockSpec((tm, GMLP_W), lambda i: (i, 0)),
                  pl.BlockSpec((1, GMLP_W), lambda i: (0, 0)),
                  pl.BlockSpec((4, CHUNK, CHUNK), lambda i: (0, 0, 0)),
                  pl.BlockSpec((CHUNK, GMLP_W), lambda i: (0, 0))],
        out_specs=[pl.BlockSpec((tm, 3 * GMLP_W), lambda i: (i, 0)),
                   pl.BlockSpec((4, CHUNK, CHUNK), lambda i: (0, 0, 0)),
                   pl.BlockSpec((8, LANES), lambda i: (0, 0)),
                   pl.BlockSpec((8, LANES), lambda i: (0, 0))],
        out_shape=[jax.ShapeDtypeStruct((SEQ, 3 * GMLP_W), BF16),
                   jax.ShapeDtypeStruct((4, CHUNK, CHUNK), F32),
                   jax.ShapeDtypeStruct((8, LANES), F32),
                   jax.ShapeDtypeStruct((8, LANES), F32)],
        compiler_params=_params(),
    )(proj, dyc, vgain, w_s, bias_full)


def _band_masks():
    qi = lax.broadcasted_iota(jnp.int32, (CHUNK, 2 * CHUNK), 0)
    kj = lax.broadcasted_iota(jnp.int32, (CHUNK, 2 * CHUNK), 1)
    valid2 = ((kj < CHUNK) & (kj >= qi)) | ((kj >= CHUNK) & (kj - CHUNK <= qi))
    q1 = lax.broadcasted_iota(jnp.int32, (CHUNK, CHUNK), 0)
    k1 = lax.broadcasted_iota(jnp.int32, (CHUNK, CHUNK), 1)
    return k1 <= q1, valid2


def _stack_heads(v, lo):
    return jnp.concatenate([jnp.where(lo, v, 0.0), jnp.where(lo, 0.0, v)], axis=0).astype(BF16)


def _rows_of(ref, start, d):
    if d == 1:
        return ref.at[pl.ds(start if isinstance(start, int) else pl.multiple_of(start, CHUNK), CHUNK), :]
    return ref.at[pl.ds(start, CHUNK, stride=d), :]


def _unrolled(lo, hi, unroll, run):
    groups = (hi - lo) // unroll
    if groups:
        def body(g, carry):
            run([lo + g * unroll + t for t in range(unroll)])
            return carry

        lax.fori_loop(0, groups, body, 0)
    if lo + groups * unroll < hi:
        run(range(lo + groups * unroll, hi))


def _for_blocks(d, group_fn, unroll):
    nblk = SEQ // CHUNK
    sh = d.bit_length() - 1

    def first(j):
        return (j * CHUNK if d == 1 else j, None)

    def rest(j):
        start = (j & (d - 1)) + (j >> sh) * (CHUNK * d)
        return (start, start - CHUNK * d)

    _unrolled(0, d, unroll, lambda js: group_fn(d, [first(j) for j in js]))
    _unrolled(d, nblk, unroll, lambda js: group_fn(d, [rest(j) for j in js]))


def _attn_fwd(proj, gq2, gk2, *ride_along):
    tn_norm, tn = 2048, 512
    npairs = ATTN_W // LANES
    nride = len(ride_along)

    def body(q_ref, k_ref, v_ref, g_ref, gq_ref, gk_ref, *rest):
        shards, rest = rest[:nride], rest[nride:]
        o_ref, l_ref, ya_ref = rest[:3]
        gathered, rest = rest[3:3 + nride], rest[3 + nride:]
        qn_ref, kn_ref = rest[:2]
        lands, (send_sems, recv_sems, copy_sems) = rest[2:2 + nride], rest[2 + nride:]
        pair = pl.program_id(0)
        ride = _gather_stages(shards, lands, send_sems, recv_sems)[0]
        for step in range(npairs):
            pl.when(pair == step)(ride[step])
        bd = _head_blockdiag()
        lo = _lo_mask(CHUNK)
        valid1, valid2 = _band_masks()

        def norm(t, carry):
            rows = pl.ds(pl.multiple_of(t * tn_norm, tn_norm), tn_norm)
            q, k = q_ref[rows, :], k_ref[rows, :]
            ssq = [_headsum(a * a, bd) for a in (q, k)]
            qn_ref[rows, :] = q * lax.rsqrt(ssq[0] * (1.0 / HEAD_DIM) + EPS) * (gq_ref[...] * QK_SCALE)
            kn_ref[rows, :] = k * lax.rsqrt(ssq[1] * (1.0 / HEAD_DIM) + EPS) * gk_ref[...]
            return carry

        lax.fori_loop(0, SEQ // tn_norm, norm, 0)

        def load_kv(ref, d, start, prev):
            own = _rows_of(ref, start, d)[...]
            if prev is None:
                return own.astype(BF16)
            return jnp.concatenate([_rows_of(ref, prev, d)[...], own], axis=0).astype(BF16)

        def group(d, blocks):
            valid = valid1 if blocks[0][1] is None else valid2
            valid = jnp.concatenate([valid, valid], axis=0)
            qs = [_rows_of(qn_ref, start, d)[...] for start, _ in blocks]
            ks = [load_kv(kn_ref, d, start, prev) for start, prev in blocks]
            vs = [load_kv(v_ref, d, start, prev) for start, prev in blocks]
            ss = [_dot_nt(_stack_heads(q, lo), k) for q, k in zip(qs, ks)]
            ms, ps, ls = [], [], []
            for s in ss:
                s = jnp.where(valid, s, -jnp.inf)
                m = jnp.max(s, axis=-1, keepdims=True)
                p = jnp.exp(s - m)
                ms.append(m)
                ls.append(jnp.sum(p, axis=-1, keepdims=True))
                ps.append(p.astype(BF16))
            os_ = [_dot(p, v) for p, v in zip(ps, vs)]
            for b, (start, _) in enumerate(blocks):
                heads = lambda v: jnp.where(lo, v[:CHUNK], v[CHUNK:])
                lsum = heads(ls[b])
                ob = heads(os_[b]) * (1.0 / lsum)
                lb = heads(ms[b]) + jnp.log(lsum)
                o_rows = _rows_of(o_ref, start, d)
                l_rows = _rows_of(l_ref, start, d)
                if d != DILATIONS[0]:
                    lold = l_rows[...]
                    mx = jnp.maximum(lold, lb)
                    ea = jnp.exp(lold - mx)
                    eb = jnp.exp(lb - mx)
                    inv = 1.0 / (ea + eb)
                    ob = o_rows[...] * (ea * inv) + ob * (eb * inv)
                    lb = mx + jnp.log(ea + eb)
                o_rows[...] = ob
                l_rows[...] = lb

        for d in DILATIONS:
            _for_blocks(d, group, ATTN_UNROLL)

        def fin(t, carry):
            rows = pl.ds(pl.multiple_of(t * tn, tn), tn)
            g = g_ref[rows, :]
            ya_ref[rows, :] = (o_ref[rows, :] * (g * _sigmoid(g))).astype(BF16)
            return carry

        lax.fori_loop(0, SEQ // tn, fin, 0)

        @pl.when(pair == npairs - 1)
        def _():
            to_hbm = [pltpu.make_async_copy(land, out, copy_sems.at[n]) for n, (land, out) in enumerate(zip(lands, gathered))]
            for cp in to_hbm:
                cp.start()
            for cp in to_hbm:
                cp.wait()

    col = lambda c0: pl.BlockSpec((SEQ, LANES), lambda p: (0, c0 // LANES + p))
    vec = pl.BlockSpec((1, LANES), lambda p: (0, 0))
    out = pl.BlockSpec((SEQ, LANES), lambda p: (0, p))
    full = [jax.ShapeDtypeStruct((4 * a.shape[0], a.shape[1]), BF16) for a in ride_along]
    return _call(
        body, name="attn_fwd", grid=(npairs,),
        in_specs=[col(C_AQ), col(C_AK), col(C_AV), col(C_AG), vec, vec]
        + [pl.BlockSpec(a.shape, lambda p: (0, 0)) for a in ride_along],
        out_specs=[out, out, out] + [pl.BlockSpec(memory_space=pl.ANY)] * nride,
        out_shape=[jax.ShapeDtypeStruct((SEQ, ATTN_W), F32), jax.ShapeDtypeStruct((SEQ, ATTN_W), F32),
                   jax.ShapeDtypeStruct((SEQ, ATTN_W), BF16)] + full,
        scratch_shapes=[pltpu.VMEM((SEQ, LANES), F32), pltpu.VMEM((SEQ, LANES), F32)]
        + [pltpu.VMEM(s.shape, BF16) for s in full]
        + [pltpu.SemaphoreType.DMA((AG_SEMS * nride,)), pltpu.SemaphoreType.DMA((AG_SEMS * nride,)),
           pltpu.SemaphoreType.DMA((nride,))],
        compiler_params=_params(),
    )(proj, proj, proj, proj, gq2, gk2, *ride_along)


def _attn_bwd(proj, o, lse, dyc, gq2, gk2, *ride_along):
    tn = 2048
    npairs = ATTN_W // LANES
    nride = len(ride_along)
    nbufs = nride * len(RS_KINDS)

    def body(proj_hbm, o_hbm, l_hbm, dyc_hbm, gq_ref, gk_ref, *rest):
        ride_in, rest = rest[:nride], rest[nride:]
        dq_ref, dk_ref, dv_ref, dgt_ref, gqg_ref, gkg_ref = rest[:6]
        ride_out, rest = rest[6:6 + nride], rest[6 + nride:]
        qb_, kb_, vb_, gb_, ob_, lb_, yb_, dkb_, dvb_, sems = rest[:10]
        rs_bufs, (send_sems, recv_sems, local_sems) = rest[10:10 + nbufs], rest[10 + nbufs:]
        rs_stage = _rs_stages(ride_in, ride_out, rs_bufs, send_sems, recv_sems, local_sems, [g.shape[1] for g in ride_along])
        pair = pl.program_id(0)
        for step in range(npairs):
            pl.when(pair == step)(rs_stage[step])
        bd = _head_blockdiag()
        lo = _lo_mask(CHUNK)
        lo2 = lax.broadcasted_iota(jnp.int32, (2 * CHUNK, LANES), 1) < HEAD_DIM
        valid1, valid2 = _band_masks()
        gqs = gq_ref[...] * QK_SCALE
        gk = gk_ref[...]

        def pcol(c0, of=None):
            return acol(proj_hbm, c0, of)

        def acol(hbm, c0=0, of=None):
            of = pair if of is None else of
            return hbm.at[:, pl.ds(pl.multiple_of(c0 + of * LANES, LANES), LANES)]

        def input_loads(of):
            return [pltpu.make_async_copy(src, dst, sems.at[n]) for n, (src, dst) in enumerate((
                (pcol(C_AQ, of), qb_), (pcol(C_AK, of), kb_), (pcol(C_AG, of), gb_), (acol(o_hbm, 0, of), ob_),
                (acol(dyc_hbm, GMLP_W, of), yb_), (pcol(C_AV, of), vb_), (acol(l_hbm, 0, of), lb_)))]

        early = (0, 1, 3, 4)
        loads = input_loads(pair)
        for n, cp in enumerate(loads):
            if n in early:
                pl.when(pair == 0)(cp.start)
            else:
                cp.start()

        @pl.when(pair == 0)
        def _():
            gqg_ref[...] = jnp.zeros_like(gqg_ref)
            gkg_ref[...] = jnp.zeros_like(gkg_ref)

        def pre_qk(t, carry):
            rows = pl.ds(pl.multiple_of(t * tn, tn), tn)
            q, k = qb_[rows, :], kb_[rows, :]
            ssq = [_headsum(a * a, bd) for a in (q, k)]
            qb_[rows, :] = q * lax.rsqrt(ssq[0] * (1.0 / HEAD_DIM) + EPS) * gqs
            kb_[rows, :] = k * lax.rsqrt(ssq[1] * (1.0 / HEAD_DIM) + EPS) * gk
            return carry

        def pre_gate(t, carry):
            rows = pl.ds(pl.multiple_of(t * tn, tn), tn)
            g = gb_[rows, :]
            ov = ob_[rows, :]
            dya = yb_[rows, :]
            sg = _sigmoid(g)
            dgt_ref[rows, :] = (dya * ov * (sg * (1.0 + g * (1.0 - sg)))).astype(BF16)
            do = dya * (g * sg)
            yb_[rows, :] = do
            ob_[rows, :] = jnp.where(first_half, lb_[rows, :], _headsum(do * ov, bd))
            return carry

        first_half = (lax.broadcasted_iota(jnp.int32, (tn, LANES), 1) & (HEAD_DIM - 1)) < HEAD_DIM // 2
        loads[0].wait()
        loads[1].wait()
        lax.fori_loop(0, SEQ // tn, pre_qk, 0)
        for cp in loads[2:5] + loads[6:7]:
            cp.wait()
        lax.fori_loop(0, SEQ // tn, pre_gate, 0)
        loads[5].wait()
        reloads = [pltpu.make_async_copy(pcol(C_AQ), lb_, sems.at[7]), pltpu.make_async_copy(pcol(C_AK), vb_, sems.at[8])]
        reloads[0].start()

        def load_kv(ref, d, start, prev):
            own = _rows_of(ref, start, d)[...]
            if prev is None:
                return own.astype(BF16)
            return jnp.concatenate([_rows_of(ref, prev, d)[...], own], axis=0).astype(BF16)

        def group(d, blocks):
            first = blocks[0][1] is None
            valid, lok = (valid1, lo) if first else (valid2, lo2)
            chains = [(b, h) for b in range(len(blocks)) for h in range(2)]
            mask = lambda h: lo if h == 0 else ~lo
            qs = [_rows_of(qb_, start, d)[...] for start, _ in blocks]
            dos = [_rows_of(yb_, start, d)[...] for start, _ in blocks]
            lds = [_rows_of(ob_, start, d)[...] for start, _ in blocks]
            ks = [load_kv(kb_, d, start, prev) for start, prev in blocks]
            vs = [load_kv(vb_, d, start, prev) for start, prev in blocks]
            qbs = [q.astype(BF16) for q in qs]
            dobs = [do.astype(BF16) for do in dos]
            ss = [_dot_nt(jnp.where(mask(h), qs[b], 0.0).astype(BF16), ks[b]) for b, h in chains]
            dps = [_dot_nt(jnp.where(mask(h), dos[b], 0.0).astype(BF16), vs[b]) for b, h in chains]
            pbs, dss = [], []
            for s, dp, (b, h) in zip(ss, dps, chains):
                hc, dc = h * HEAD_DIM, h * HEAD_DIM + HEAD_DIM // 2
                p = jnp.exp(jnp.where(valid, s, -jnp.inf) - lds[b][:, hc:hc + 1])
                pbs.append(p.astype(BF16))
                dss.append((p * (dp - lds[b][:, dc:dc + 1])).astype(BF16))
            dqs = [_dot(ds, ks[b]) for ds, (b, h) in zip(dss, chains)]
            dks = [_dot_tn(ds, qbs[b]) for ds, (b, h) in zip(dss, chains)]
            dvs = [_dot_tn(p, dobs[b]) for p, (b, h) in zip(pbs, chains)]
            assign = d == DILATIONS[0]
            for b, (start, prev) in enumerate(blocks):
                c0, c1 = 2 * b, 2 * b + 1
                dq_rows = _rows_of(gb_, start, d)
                dqb = jnp.where(lo, dqs[c0], dqs[c1])
                dq_rows[...] = dqb if assign else dq_rows[...] + dqb
                dkc = jnp.where(lok, dks[c0], dks[c1])
                dvc = jnp.where(lok, dvs[c0], dvs[c1])
                spans = ((start, slice(0, CHUNK), True),) if first else (
                    (prev, slice(0, CHUNK), False), (start, slice(CHUNK, 2 * CHUNK), True))
                for st, sl, own in spans:
                    dk_rows = _rows_of(dkb_, st, d)
                    dv_rows = _rows_of(dvb_, st, d)
                    if assign and own:
                        dk_rows[...] = dkc[sl]
                        dv_rows[...] = dvc[sl]
                    else:
                        dk_rows[...] = dk_rows[...] + dkc[sl]
                        dv_rows[...] = dv_rows[...] + dvc[sl]

        for d in DILATIONS:
            _for_blocks(d, group, ATTN_UNROLL)

        reloads[1].start()

        @pl.when(pair < npairs - 1)
        def _():
            nxt = input_loads(pair + 1)
            for n in early:
                nxt[n].start()

        for cp in reloads:
            cp.wait()

        def post(t, carry):
            gq_acc, gk_acc = carry
            rows = pl.ds(pl.multiple_of(t * tn, tn), tn)
            raws = [lb_[rows, :], vb_[rows, :]]
            dns = [gb_[rows, :], dkb_[rows, :]]
            rs = [lax.rsqrt(_headsum(a * a, bd) * (1.0 / HEAD_DIM) + EPS) for a in raws]
            zs = [a * r for a, r in zip(raws, rs)]
            dzs = [dn * gain for dn, gain in zip(dns, (gqs, gk))]
            means = [_headsum(dz * z, bd) * (1.0 / HEAD_DIM) for dz, z in zip(dzs, zs)]
            dq, dk = [r * (dz - z * mean) for r, dz, z, mean in zip(rs, dzs, zs, means)]
            gq, gkk = [jnp.sum(dn * z, axis=0, keepdims=True) for dn, z in zip(dns, zs)]
            dq_ref[rows, :] = dq.astype(BF16)
            dk_ref[rows, :] = dk.astype(BF16)
            dv_ref[rows, :] = dvb_[rows, :].astype(BF16)
            return gq_acc + gq * QK_SCALE, gk_acc + gkk

        zero = jnp.zeros((1, LANES), F32)
        gq_acc, gk_acc = lax.fori_loop(0, SEQ // tn, post, (zero, zero))
        gqg_ref[0:1, :] += gq_acc
        gkg_ref[0:1, :] += gk_acc

        @pl.when(pair == npairs - 1)
        def _():
            gqg_ref[0:1, :] = _fold_heads(gqg_ref[0:1, :])
            gkg_ref[0:1, :] = _fold_heads(gkg_ref[0:1, :])
            rs_stage[npairs]()

    hbm = pl.BlockSpec(memory_space=pl.ANY)
    vec = pl.BlockSpec((1, LANES), lambda p: (0, 0))
    blk8 = pl.BlockSpec((8, LANES), lambda p: (0, 0))
    out = pl.BlockSpec((SEQ, LANES), lambda p: (0, p))
    big = jax.ShapeDtypeStruct((SEQ, ATTN_W), BF16)
    nsem = RS_SEMS * nride
    return _call(
        body, name="attn_bwd", grid=(npairs,),
        in_specs=[hbm, hbm, hbm, hbm, vec, vec] + [hbm] * nride,
        out_specs=[out, out, out, out, blk8, blk8] + [hbm] * nride,
        out_shape=[big, big, big, big, jax.ShapeDtypeStruct((8, LANES), F32), jax.ShapeDtypeStruct((8, LANES), F32)]
        + [jax.ShapeDtypeStruct((2, g.shape[0] // 8, g.shape[1]), F32) for g in ride_along],
        scratch_shapes=[pltpu.VMEM((SEQ, LANES), F32) for _ in range(9)] + [pltpu.SemaphoreType.DMA((9,))]
        + _rs_scratch([g.shape for g in ride_along]) + [pltpu.SemaphoreType.DMA((nsem,)), pltpu.SemaphoreType.DMA((nsem,)),
                                     pltpu.SemaphoreType.DMA((nride,))],
        compiler_params=_params(),
    )(proj, o, lse, dyc, gq2, gk2, *[_rs_view(g) for g in ride_along])


def _mem_kv(mem, gain, wkv):
    def body(m_ref, g_ref, w_ref, kv_ref, hm_ref):
        mv = m_ref[...]
        ms = jnp.mean(mv * mv, axis=-1, keepdims=True)
        hm = (mv * lax.rsqrt(ms + EPS) * g_ref[...]).astype(BF16)
        hm_ref[...] = hm
        kv_ref[...] = _dot(hm, w_ref[...])

    return _call(
        body, name="mem_kv",
        out_shape=[jax.ShapeDtypeStruct((MEM_LEN, 2 * MEM_W), F32), jax.ShapeDtypeStruct((MEM_LEN, D_MODEL), BF16)],
        compiler_params=_params(),
    )(mem, gain, wkv)


def _mem_keys(kv_ref, kg_ref, bd, p):
    mk = kv_ref[:, p * LANES:(p + 1) * LANES]
    r = lax.rsqrt(_headsum(mk * mk, bd) * (1.0 / HEAD_DIM) + EPS)
    z = mk * r
    mkn = (z * kg_ref[:, p * LANES:(p + 1) * LANES]).astype(BF16)
    mvp = kv_ref[:, MEM_W + p * LANES:MEM_W + (p + 1) * LANES].astype(BF16)
    return mkn, mvp, r, z


def _mem_fwd(proj, kv, qg4, kg4):
    tm = 1024

    def body(q_ref, g_ref, kv_ref, qg_ref, kg_ref, om_ref, ym_ref):
        bd = _head_blockdiag()
        lo = _lo_mask(tm)
        keys, qns = [], []
        for p in range(2):
            cs = slice(p * LANES, (p + 1) * LANES)
            keys.append(_mem_keys(kv_ref, kg_ref, bd, p)[:2])
            q = q_ref[:, cs]
            qns.append(q * lax.rsqrt(_headsum(q * q, bd) * (1.0 / HEAD_DIM) + EPS) * (qg_ref[:, cs] * QK_SCALE))
        chains = [(p, h) for p in range(2) for h in range(2)]
        ss = [_dot_nt(jnp.where(lo if h == 0 else ~lo, qns[p], 0.0).astype(BF16), keys[p][0]) for p, h in chains]
        es = [jnp.exp(s - jnp.max(s, axis=-1, keepdims=True)) for s in ss]
        os_ = [_dot(e.astype(BF16), keys[p][1]) for e, (p, h) in zip(es, chains)]
        res = [o * (1.0 / jnp.sum(e, axis=-1, keepdims=True)) for o, e in zip(os_, es)]
        for p in range(2):
            cs = slice(p * LANES, (p + 1) * LANES)
            ov = jnp.where(lo, res[2 * p], res[2 * p + 1])
            g = g_ref[:, cs]
            om_ref[:, cs] = ov
            ym_ref[:, cs] = (ov * (g * _sigmoid(g))).astype(BF16)

    vec = pl.BlockSpec((1, MEM_W), lambda i: (0, 0))
    return _call(
        body, name="mem_fwd", grid=(SEQ // tm,),
        in_specs=[pl.BlockSpec((tm, MEM_W), lambda i: (i, C_MQ // MEM_W)),
                  pl.BlockSpec((tm, MEM_W), lambda i: (i, C_MG // MEM_W)),
                  pl.BlockSpec((MEM_LEN, 2 * MEM_W), lambda i: (0, 0)), vec, vec],
        out_specs=[pl.BlockSpec((tm, MEM_W), lambda i: (i, 0)), pl.BlockSpec((tm, MEM_W), lambda i: (i, 0))],
        out_shape=[jax.ShapeDtypeStruct((SEQ, MEM_W), F32), jax.ShapeDtypeStruct((SEQ, MEM_W), BF16)],
        compiler_params=_params(),
    )(proj, proj, kv, qg4, kg4)


def _mem_bwd(proj, om, dyc, kv, hm, mem, mgain, wkv, qg4, kg4):
    tm = 1024
    nsteps = SEQ // tm

    def body(q_ref, g_ref, om_ref, dy_ref, kv_ref, hm_ref, mem_ref, mg_ref, w_ref, qg_ref, kg_ref,
             dq_ref, dgt_ref, gqg_ref, gkg_ref, gw_ref, gmg_ref, dmk_ref, dmv_ref, gq_acc):
        i = pl.program_id(0)
        bd = _head_blockdiag()
        lo = _lo_mask(tm)
        lom = _lo_mask(MEM_LEN)

        @pl.when(i == 0)
        def _():
            dmk_ref[...] = jnp.zeros_like(dmk_ref)
            dmv_ref[...] = jnp.zeros_like(dmv_ref)
            gq_acc[...] = jnp.zeros_like(gq_acc)

        pairs = []
        for p in range(2):
            cs = slice(p * LANES, (p + 1) * LANES)
            mkn, mvp, _, _ = _mem_keys(kv_ref, kg_ref, bd, p)
            gqs = qg_ref[:, cs] * QK_SCALE
            q = q_ref[:, cs]
            r = lax.rsqrt(_headsum(q * q, bd) * (1.0 / HEAD_DIM) + EPS)
            z = q * r
            qn = z * gqs
            g = g_ref[:, cs]
            ov = om_ref[:, cs]
            dym = dy_ref[:, cs]
            sg = _sigmoid(g)
            dgt_ref[:, cs] = (dym * ov * (sg * (1.0 + g * (1.0 - sg)))).astype(BF16)
            do = dym * (g * sg)
            pairs.append(dict(cs=cs, mkn=mkn, mvp=mvp, gqs=gqs, r=r, z=z, qn=qn, qnb=qn.astype(BF16), do=do,
                              dob=do.astype(BF16), delta=_headsum(do * ov, bd)))
        chains = [(pr_, h) for pr_ in pairs for h in range(2)]
        mask = lambda h: lo if h == 0 else ~lo
        ss = [_dot_nt(jnp.where(mask(h), c["qn"], 0.0).astype(BF16), c["mkn"]) for c, h in chains]
        dps = [_dot_nt(jnp.where(mask(h), c["do"], 0.0).astype(BF16), c["mvp"]) for c, h in chains]
        prs, dss = [], []
        for s, dp, (c, h) in zip(ss, dps, chains):
            e = jnp.exp(s - jnp.max(s, axis=-1, keepdims=True))
            pr = e * (1.0 / jnp.sum(e, axis=-1, keepdims=True))
            prs.append(pr.astype(BF16))
            dss.append((pr * (dp - c["delta"][:, h * HEAD_DIM:h * HEAD_DIM + 1])).astype(BF16))
        dqs = [_dot(ds, c["mkn"]) for ds, (c, h) in zip(dss, chains)]
        dks = [_dot_tn(ds, c["qnb"]) for ds, (c, h) in zip(dss, chains)]
        dvs = [_dot_tn(pr, c["dob"]) for pr, (c, h) in zip(prs, chains)]
        for p, c in enumerate(pairs):
            cs, z, r = c["cs"], c["z"], c["r"]
            dqn = jnp.where(lo, dqs[2 * p], dqs[2 * p + 1])
            dmk_ref[:, cs] += jnp.where(lom, dks[2 * p], dks[2 * p + 1])
            dmv_ref[:, cs] += jnp.where(lom, dvs[2 * p], dvs[2 * p + 1])
            dz = dqn * c["gqs"]
            dq_ref[:, cs] = (r * (dz - z * (_headsum(dz * z, bd) * (1.0 / HEAD_DIM)))).astype(BF16)
            gq_acc[:, cs] += jnp.sum(dqn * z, axis=0, keepdims=True) * QK_SCALE

        @pl.when(i == nsteps - 1)
        def _():
            gqg_ref[...] = jnp.zeros_like(gqg_ref)
            gkg_ref[...] = jnp.zeros_like(gkg_ref)
            gqg_ref[0:1, :] = _fold_heads(gq_acc[:, 0:LANES] + gq_acc[:, LANES:2 * LANES])
            dkv = []
            gk = jnp.zeros((1, LANES), F32)
            for p in range(2):
                cs = slice(p * LANES, (p + 1) * LANES)
                _, _, r, z = _mem_keys(kv_ref, kg_ref, bd, p)
                dn = dmk_ref[:, cs]
                dz = dn * kg_ref[:, cs]
                gk = gk + jnp.sum(dn * z, axis=0, keepdims=True)
                dkv.append(r * (dz - z * (_headsum(dz * z, bd) * (1.0 / HEAD_DIM))))
            gkg_ref[0:1, :] = _fold_heads(gk)
            dkvb = jnp.concatenate(dkv + [dmv_ref[...]], axis=1).astype(BF16)
            gw_ref[...] = _dot_tn(hm_ref[...], dkvb)
            dhm = _dot_nt(dkvb, w_ref[...])
            mv = mem_ref[...]
            zm = mv * lax.rsqrt(jnp.mean(mv * mv, axis=-1, keepdims=True) + EPS)
            _put_rows(gmg_ref, jnp.sum(dhm * zm, axis=0, keepdims=True))

    const = lambda shape: pl.BlockSpec(shape, lambda i: (0,) * len(shape))
    row = lambda j: pl.BlockSpec((tm, MEM_W), lambda i: (i, j))
    blk8 = jax.ShapeDtypeStruct((8, LANES), F32)
    return _call(
        body, name="mem_bwd", grid=(nsteps,),
        in_specs=[row(C_MQ // MEM_W), row(C_MG // MEM_W), row(0), row((GMLP_W + ATTN_W) // MEM_W),
                  const((MEM_LEN, 2 * MEM_W)), const((MEM_LEN, D_MODEL)), const((MEM_LEN, D_MODEL)),
                  const((1, D_MODEL)), const((D_MODEL, 2 * MEM_W)), const((1, MEM_W)), const((1, MEM_W))],
        out_specs=[row(0), row(0), const((8, LANES)), const((8, LANES)),
                   const((D_MODEL, 2 * MEM_W)), const((8, LANES))],
        out_shape=[jax.ShapeDtypeStruct((SEQ, MEM_W), BF16), jax.ShapeDtypeStruct((SEQ, MEM_W), BF16),
                   blk8, blk8, jax.ShapeDtypeStruct((D_MODEL, 2 * MEM_W), F32), blk8],
        scratch_shapes=[pltpu.VMEM((MEM_LEN, MEM_W), F32), pltpu.VMEM((MEM_LEN, MEM_W), F32),
                        pltpu.VMEM((1, MEM_W), F32)],
        compiler_params=_params(),
    )(proj, proj, om, dyc, kv, hm, mem, mgain, wkv, qg4, kg4)


def _out_loss(yg, ya, ym, x, tgt, wo):
    tm = 512
    nsteps = SEQ // tm
    parts = ((0, GMLP_W), (GMLP_W, ATTN_W), (GMLP_W + ATTN_W, MEM_W))

    def body(yg_ref, ya_ref, ym_ref, x_ref, t_ref, w_ref, dy_ref, dyc_ref, gw_ref, ls_ref):
        i = pl.program_id(0)

        @pl.when(i == 0)
        def _():
            gw_ref[...] = jnp.zeros_like(gw_ref)
            ls_ref[...] = jnp.zeros_like(ls_ref)

        ys = (yg_ref[...], ya_ref[...], ym_ref[...])
        y = sum(_dot(yv, w_ref[r0:r0 + n, :]) for yv, (r0, n) in zip(ys, parts))
        err = x_ref[...] + y - t_ref[...]
        _put_rows(ls_ref, jnp.sum(err * err, axis=0, keepdims=True), accumulate=True)
        dy = err * (1.0 / D_MODEL)
        dy_ref[...] = dy
        dyb = dy.astype(BF16)
        dyc_ref[...] = _dot_nt(dyb, w_ref[...])
        for yv, (r0, n) in zip(ys, parts):
            gw_ref[r0:r0 + n, :] += _dot_tn(yv, dyb)

    row = lambda w: pl.BlockSpec((tm, w), lambda i: (i, 0))
    const = lambda shape: pl.BlockSpec(shape, lambda i: (0, 0))
    return _call(
        body, name="out_loss", grid=(nsteps,),
        in_specs=[row(GMLP_W), row(ATTN_W), row(MEM_W), row(D_MODEL), row(D_MODEL), const((D_MODEL, D_MODEL))],
        out_specs=[row(D_MODEL), row(D_MODEL), const((D_MODEL, D_MODEL)), const((8, LANES))],
        out_shape=[jax.ShapeDtypeStruct((SEQ, D_MODEL), F32), jax.ShapeDtypeStruct((SEQ, D_MODEL), F32),
                   jax.ShapeDtypeStruct((D_MODEL, D_MODEL), F32), jax.ShapeDtypeStruct((8, LANES), F32)],
        compiler_params=_params(),
    )(yg, ya, ym, x, tgt, wo)


def _proj_bwd(x, dy, gain, wt, dg, daq, dak, dav, dag, dmq, dmg):
    tm = 512
    nsteps = SEQ // tm
    pieces = ((C_GU, 3 * GMLP_W), (C_AQ, ATTN_W), (C_AK, ATTN_W), (C_AV, ATTN_W), (C_AG, ATTN_W),
              (C_MQ, MEM_W), (C_MG, MEM_W))

    def body(x_ref, dy_ref, g_ref, wt_hbm, p0, p1, p2, p3, p4, p5, p6, gx_ref, gwt_hbm, gg_ref, wt_v, acc, wt_sem, out_sems):
        i = pl.program_id(0)
        wt_load = pltpu.make_async_copy(wt_hbm, wt_v, wt_sem)

        @pl.when(i == 0)
        def _():
            wt_load.start()
            acc[...] = jnp.zeros_like(acc)
            gg_ref[...] = jnp.zeros_like(gg_ref)

        xv = x_ref[...]
        r = lax.rsqrt(jnp.mean(xv * xv, axis=-1, keepdims=True) + EPS)
        z = xv * r
        g = g_ref[...]
        h = (z * g).astype(BF16)
        pl.when(i == 0)(wt_load.wait)
        flush = [pltpu.make_async_copy(acc.at[c0:c0 + w, :], gwt_hbm.at[c0:c0 + w, :], out_sems.at[n])
                 for n, (c0, w) in enumerate(pieces)]
        dh = jnp.zeros((tm, D_MODEL), F32)
        for n, (pref, (c0, w)) in enumerate(zip((p0, p1, p2, p3, p4, p5, p6), pieces)):
            dp = pref[...]
            dh = dh + _dot(dp, wt_v[c0:c0 + w, :])
            acc[c0:c0 + w, :] += _dot_tn(dp, h)
            pl.when(i == nsteps - 1)(flush[n].start)
        _put_rows(gg_ref, jnp.sum(dh * z, axis=0, keepdims=True), accumulate=True)
        dz = dh * g
        gx_ref[...] = dy_ref[...] + r * (dz - z * jnp.mean(dz * z, axis=-1, keepdims=True))

        @pl.when(i == nsteps - 1)
        def _():
            for cp in flush:
                cp.wait()

    row = lambda w: pl.BlockSpec((tm, w), lambda i: (i, 0))
    hbm = pl.BlockSpec(memory_space=pl.ANY)
    vec = pl.BlockSpec((1, D_MODEL), lambda i: (0, 0))
    return _call(
        body, name="proj_bwd", grid=(nsteps,),
        in_specs=[row(D_MODEL), row(D_MODEL), vec, hbm] + [row(w) for _, w in pieces],
        out_specs=[row(D_MODEL), hbm, pl.BlockSpec((8, LANES), lambda i: (0, 0))],
        out_shape=[jax.ShapeDtypeStruct((SEQ, D_MODEL), F32), jax.ShapeDtypeStruct((IN_W, D_MODEL), F32),
                   jax.ShapeDtypeStruct((8, LANES), F32)],
        scratch_shapes=[pltpu.VMEM((IN_W, D_MODEL), BF16), pltpu.VMEM((IN_W, D_MODEL), F32), pltpu.SemaphoreType.DMA,
                        pltpu.SemaphoreType.DMA((len(pieces),))],
        compiler_params=_params(),
    )(x, dy, gain, wt, dg, daq, dak, dav, dag, dmq, dmg)


AG_SEMS = 8


def _gather_stages(ins, lands, send_sems, recv_sems):
    n = len(ins)
    nrows = [a.shape[0] for a in ins]
    x, y, c = lax.axis_index("x"), lax.axis_index("y"), lax.axis_index("c")
    sib, xn, yn = (x, y, 1 - c), (1 - x, y, c), (x, 1 - y, c)
    me, cx, cy, cd = 2 * x + y, 2 * (1 - x) + y, 2 * x + (1 - y), 2 * (1 - x) + (1 - y)

    def part(a, chip, hf, quarter=None):
        rows = nrows[a] // 2
        base = chip * nrows[a] + hf * rows
        if quarter is not None:
            rows = rows // 2
            base = base + quarter * rows
        return lands[a].at[pl.ds(pl.multiple_of(base, 16), rows), :]

    def copy(a, j, ref, to):
        k = AG_SEMS * a + j
        return pltpu.make_async_remote_copy(src_ref=ref, dst_ref=ref, send_sem=send_sems.at[k],
                                            recv_sem=recv_sems.at[k], device_id=to, device_id_type=MESH)

    def own(a):
        return [copy(a, 0, part(a, me, c), xn), copy(a, 1, part(a, me, c), yn)]

    def neighbours(a):
        return [copy(a, 4, part(a, cx, c, 1), yn), copy(a, 2, part(a, cx, c), sib),
                copy(a, 5, part(a, cy, c, 0), xn), copy(a, 3, part(a, cy, c), sib)]

    def diagonal(a):
        return [copy(a, 7, part(a, cd, c, 1), sib), copy(a, 6, part(a, cd, c, 0), sib)]

    def send_own():
        for a in range(n):
            lands[a][pl.ds(pl.multiple_of(me * nrows[a], 16), nrows[a]), :] = ins[a][...].astype(BF16)
            for cp in own(a):
                cp.start()

    def pass_on_neighbours():
        for a in range(n):
            copy(a, 0, part(a, cx, c), xn).wait_recv()
            copy(a, 1, part(a, cy, c), yn).wait_recv()
            for cp in neighbours(a):
                cp.start()

    def pass_on_diagonal():
        for a in range(n):
            copy(a, 4, part(a, cd, c, 1), yn).wait_recv()
            copy(a, 5, part(a, cd, c, 0), xn).wait_recv()
            for cp in diagonal(a):
                cp.start()

    def y_complete():
        for a in range(n):
            copy(a, 3, part(a, cy, 1 - c), sib).wait_recv()

    def x_complete():
        for a in range(n):
            copy(a, 2, part(a, cx, 1 - c), sib).wait_recv()

    def diagonal_complete():
        for a in range(n):
            copy(a, 6, part(a, cd, 1 - c, 0), sib).wait_recv()
            copy(a, 7, part(a, cd, 1 - c, 1), sib).wait_recv()

    def sends_done():
        for a in range(n):
            for cp in own(a) + neighbours(a) + diagonal(a):
                cp.wait_send()

    def finish():
        y_complete()
        x_complete()
        diagonal_complete()
        sends_done()

    return (send_own, pass_on_neighbours, pass_on_diagonal, finish), (y_complete, x_complete, diagonal_complete, sends_done)


RS_SEMS = 6
RS_KINDS = (((2, 2), 1, F32), ((2, 2), 1, F32), ((2, 2), 2, BF16), ((2, 2), 2, BF16), ((2, 2), 2, F32),
            ((2,), 2, BF16), ((2,), 2, BF16), ((2,), 1, F32))


def _rs_view(g):
    return g.reshape(2, 2, 2, g.shape[0] // 8, g.shape[1])


def _rs_scratch(shapes):
    return [pltpu.VMEM(lead + (r // 8, w // split), dt) for lead, split, dt in RS_KINDS for r, w in shapes]


def _rs_stages(gs, outs, bufs, send_sems, recv_sems, local_sems, widths):
    n = len(gs)
    loc, ra, s_b, r_b, acc1, s_c, r_c, fin = (bufs[n * i:n * i + n] for i in range(len(RS_KINDS)))
    half_w = [w // 2 for w in widths]
    x, y, c = lax.axis_index("x"), lax.axis_index("y"), lax.axis_index("c")
    sib, xn, yn = (x, y, 1 - c), (1 - x, y, c), (x, 1 - y, c)

    def copy(a, j, src, dst, to):
        k = RS_SEMS * a + j
        return pltpu.make_async_remote_copy(src_ref=src, dst_ref=dst, send_sem=send_sems.at[k],
                                            recv_sem=recv_sems.at[k], device_id=to, device_id_type=MESH)

    def step_a(a):
        return [copy(a, 0, gs[a].at[:, :, 1 - c], ra[a], sib),
                pltpu.make_async_copy(gs[a].at[:, :, c], loc[a], local_sems.at[a])]

    def step_b(a):
        return copy(a, 1, s_b[a].at[0], r_b[a].at[0], xn), copy(a, 2, s_b[a].at[1], r_b[a].at[1], yn)

    def step_c(a):
        return copy(a, 3, s_c[a].at[0], r_c[a].at[0], yn), copy(a, 4, s_c[a].at[1], r_c[a].at[1], xn)

    def step_d(a, half):
        rows = fin[a].at[half]
        return copy(a, 5, rows, rows, sib)

    def start():
        for a in range(n):
            for cp in step_a(a):
                cp.start()

    def a_to_b():
        for a in range(n):
            for cp in step_a(a):
                cp.wait()
            ra[a][...] = loc[a][...] + ra[a][...]
            s_b[a][0] = ra[a][1 - x, :, :, :half_w[a]].astype(BF16)
            s_b[a][1] = ra[a][:, 1 - y, :, half_w[a]:].astype(BF16)
            for cp in step_b(a):
                cp.start()

    def b_to_c():
        for a in range(n):
            for cp in step_b(a):
                cp.wait()
            acc1[a][0] = ra[a][x, :, :, :half_w[a]] + r_b[a][0].astype(F32)
            acc1[a][1] = ra[a][:, y, :, half_w[a]:] + r_b[a][1].astype(F32)
            s_c[a][0] = acc1[a][0, 1 - y].astype(BF16)
            s_c[a][1] = acc1[a][1, 1 - x].astype(BF16)
            for cp in step_c(a):
                cp.start()

    def c_to_d():
        for a in range(n):
            for cp in step_c(a):
                cp.wait()
            fin[a][c, :, :half_w[a]] = acc1[a][0, y] + r_c[a][0].astype(F32)
            fin[a][c, :, half_w[a]:] = acc1[a][1, x] + r_c[a][1].astype(F32)
            step_d(a, c).start()

    def finish():
        to_hbm = [pltpu.make_async_copy(fin[a], outs[a], local_sems.at[a]) for a in range(n)]
        for a in range(n):
            step_d(a, 1 - c).wait_recv()
            step_d(a, c).wait_send()
            to_hbm[a].start()
        for cp in to_hbm:
            cp.wait()

    return start, a_to_b, b_to_c, c_to_d, finish


def _reduce_grads(gwt, g_ws, tiny):
    cw = gwt.shape[1] // RS_CHUNKS
    chunk_shape = (gwt.shape[0], cw)

    def body(g0, ws_in, tiny_in, *rest):
        outs, o_ws, o_tiny = rest[:RS_CHUNKS], rest[RS_CHUNKS], rest[RS_CHUNKS + 1]
        rest = rest[RS_CHUNKS + 2:]
        nb = len(RS_KINDS) * RS_CHUNKS
        sm, sa, sb, sc, acc_s, send_sems, recv_sems, local_sems = rest[nb:]
        blocks = [g0.at[:, :, :, :, pl.ds(j * cw, cw)] for j in range(RS_CHUNKS)]
        start, a_to_b, b_to_c, c_to_d, finish = _rs_stages(blocks, outs, rest[:nb], send_sems, recv_sems, local_sems,
                                                           [cw] * RS_CHUNKS)
        n_ws = ws_in.shape[0]
        sm[0:n_ws, :] = ws_in[...]
        sm[n_ws:, :] = tiny_in[...]
        x, y, c = lax.axis_index("x"), lax.axis_index("y"), lax.axis_index("c")

        def small(j, src, dst, to):
            k = RS_SEMS * RS_CHUNKS + j
            return pltpu.make_async_remote_copy(src_ref=src, dst_ref=dst, send_sem=send_sems.at[k],
                                                recv_sem=recv_sems.at[k], device_id=to, device_id_type=MESH)

        along_c, along_x, along_y = (small(0, sm, sa, (x, y, 1 - c)), small(1, acc_s, sb, (1 - x, y, c)),
                                     small(2, sb, sc, (x, 1 - y, c)))
        start()
        along_c.start()
        a_to_b()
        along_c.wait()
        acc_s[...] = sm[...] + sa[...]
        along_x.start()
        b_to_c()
        along_x.wait()
        sb[...] = acc_s[...] + sb[...]
        along_y.start()
        c_to_d()
        along_y.wait()
        o_ws[...] = sb[0:n_ws, :] + sc[0:n_ws, :]
        o_tiny[...] = sb[n_ws:, :] + sc[n_ws:, :]
        finish()

    vm = pl.BlockSpec(memory_space=pltpu.VMEM)
    hbm = pl.BlockSpec(memory_space=pl.ANY)
    small_shape = (g_ws.shape[0] + tiny.shape[0], LANES)
    scratch = _rs_scratch([chunk_shape] * RS_CHUNKS) + [pltpu.VMEM(small_shape, F32) for _ in range(5)]
    nsem = RS_SEMS * RS_CHUNKS + 3
    scratch += [pltpu.SemaphoreType.DMA((nsem,)), pltpu.SemaphoreType.DMA((nsem,)), pltpu.SemaphoreType.DMA((RS_CHUNKS,))]
    return _call(
        body, name="reduce_grads",
        out_shape=[jax.ShapeDtypeStruct((2, gwt.shape[0] // 8, cw), F32)] * RS_CHUNKS
        + [jax.ShapeDtypeStruct(g_ws.shape, F32), jax.ShapeDtypeStruct(tiny.shape, F32)],
        in_specs=[hbm, vm, vm],
        out_specs=[hbm] * RS_CHUNKS + [vm, vm],
        scratch_shapes=scratch,
        compiler_params=_params(),
    )(_rs_view(gwt), g_ws, tiny)


def _adam_update(w, g, m, v):
    nm = ADAM_B1 * m + (1.0 - ADAM_B1) * g
    nv = ADAM_B2 * v + (1.0 - ADAM_B2) * (g * g)
    m_hat = nm / (1.0 - ADAM_B1 ** ADAM_STEP)
    v_hat = nv / (1.0 - ADAM_B2 ** ADAM_STEP)
    return -ADAM_LR * (m_hat / (jnp.sqrt(v_hat) + ADAM_EPS) + ADAM_WD * w), nm, nv


def _adamw(w, g, m, v):
    rows, cols = w.shape
    tm = max(t for t in range(8, 257, 8) if rows % t == 0)
    parts = tuple(g) if isinstance(g, (tuple, list)) else (g,)
    n = len(parts)

    def body(w_ref, m_ref, v_ref, *refs):
        gv = jnp.concatenate([r[...] for r in refs[:n]], axis=1)
        d_ref, nm_ref, nv_ref = refs[n:n + 3]
        d_ref[...], nm_ref[...], nv_ref[...] = _adam_update(w_ref[...], gv, m_ref[...], v_ref[...])
        if n > 1:
            refs[n + 3][...] = gv

    blk = pl.BlockSpec((tm, cols), lambda i: (i, 0))
    nout = 3 if n == 1 else 4
    res = _call(
        body, name="adamw", grid=(rows // tm,),
        in_specs=[blk] * 3 + [pl.BlockSpec((tm, p.shape[1]), lambda i: (i, 0)) for p in parts], out_specs=[blk] * nout,
        out_shape=[jax.ShapeDtypeStruct((rows, cols), F32)] * nout,
        compiler_params=_params(),
    )(w, m, v, *parts)
    return (parts[0] if n == 1 else res[3], *res[:3])


def _adamw_tiny(tiny, weights, ms, vs):
    shapes = [w.shape for w in weights]
    n = len(weights)

    def grad_of(t_ref, k, shape):
        base = 8 * k
        if shape[1] > LANES:
            return [t_ref[base + j:base + j + 1, :] for j in range(shape[1] // LANES)]
        return [t_ref[base:base + shape[0], 0:shape[1]]]

    def body(t_ref, *refs):
        w_refs, m_refs, v_refs = refs[:n], refs[n:2 * n], refs[2 * n:3 * n]
        loss_ref, outs = refs[3 * n], refs[3 * n + 1:]
        loss_ref[...] = (0.5 / D_MODEL) * jnp.sum(t_ref[8 * n:8 * n + 8, :], keepdims=True)
        for k, shape in enumerate(shapes):
            g_ref, d_ref, nm_ref, nv_ref = outs[4 * k:4 * k + 4]
            for j, g in enumerate(grad_of(t_ref, k, shape)):
                cols = slice(j * LANES, (j + 1) * LANES) if shape[1] > LANES else slice(None)
                g_ref[:, cols] = g
                d_ref[:, cols], nm_ref[:, cols], nv_ref[:, cols] = _adam_update(
                    w_refs[k][:, cols], g, m_refs[k][:, cols], v_refs[k][:, cols])

    out_shape = [jax.ShapeDtypeStruct((1, 1), F32)]
    for shape in shapes:
        out_shape += [jax.ShapeDtypeStruct(shape, F32)] * 4
    return _call(body, name="adamw_tiny", out_shape=out_shape, compiler_params=_params())(tiny, *weights, *ms, *vs)


def _local_grads(x, mem, tgt, norm_gain, wt_sh, gmlp_v_gain, gmlp_w_s, gmlp_b, attn_q_gain, attn_k_gain,
                 mem_norm_gain, wkv_sh, mem_q_gain, mem_k_gain, wo_sh):
    vg = gmlp_v_gain.reshape(1, GMLP_W)
    bias_full = jnp.repeat(gmlp_b.T, HEAD_DIM, axis=1)
    gq2, gk2 = jnp.tile(attn_q_gain, (1, 2)), jnp.tile(attn_k_gain, (1, 2))
    qg4, kg4 = jnp.tile(mem_q_gain, (1, 4)), jnp.tile(mem_k_gain, (1, 4))

    proj, wt = _gather_proj(x, norm_gain, wt_sh)
    yg = _gmlp_fwd(proj, vg, gmlp_w_s, bias_full)
    o, lse, ya, wkv, wo = _attn_fwd(proj, gq2, gk2, wkv_sh, wo_sh)
    kv, hm = _mem_kv(mem, mem_norm_gain, wkv)
    om, ym = _mem_fwd(proj, kv, qg4, kg4)
    dy, dyc, g_wo, err2 = _out_loss(yg, ya, ym, x, tgt, wo)
    dmq, dmg, g_mq, g_mk, g_wkv, g_mng = _mem_bwd(proj, om, dyc, kv, hm, mem, mem_norm_gain, wkv, qg4, kg4)
    daq, dak, dav, dag, g_aq, g_ak, g_wkv_sh, g_wo_sh = _attn_bwd(proj, o, lse, dyc, gq2, gk2, g_wkv, g_wo)
    dg, g_ws, g_b, g_vg = _gmlp_bwd(proj, dyc, vg, gmlp_w_s, bias_full)
    gx, g_wt, g_ng = _proj_bwd(x, dy, norm_gain, wt, dg, daq, dak, dav, dag, dmq, dmg)

    tiny = jnp.concatenate([g_ng, g_vg, g_b, g_aq, g_ak, g_mng, g_mq, g_mk, err2], axis=0)
    return gx, g_wt, g_wkv_sh, g_wo_sh, g_ws.reshape(4 * CHUNK, CHUNK), tiny


def kernel(x, mem, norm_gain, w_in, gmlp_v_gain, gmlp_w_s, gmlp_b, attn_q_gain, attn_k_gain, mem_norm_gain, w_mem_kv, mem_q_gain, mem_k_gain, w_out, loss_target, m_norm_gain, m_w_in, m_gmlp_v_gain, m_gmlp_w_s, m_gmlp_b, m_attn_q_gain, m_attn_k_gain, m_mem_norm_gain, m_w_mem_kv, m_mem_q_gain, m_mem_k_gain, m_w_out, v_norm_gain, v_w_in, v_gmlp_v_gain, v_gmlp_w_s, v_gmlp_b, v_attn_q_gain, v_attn_k_gain, v_mem_norm_gain, v_w_mem_kv, v_mem_q_gain, v_mem_k_gain, v_w_out):
    gx, g_wt, g_wkv_sh, g_wo_sh, g_ws, tiny = _local_grads(
        x[0], mem[0], loss_target[0], norm_gain, w_in[0].T, gmlp_v_gain[0], gmlp_w_s[0], gmlp_b[0],
        attn_q_gain, attn_k_gain, mem_norm_gain, w_mem_kv[0], mem_q_gain, mem_k_gain, w_out[0])
    *g_wt_sh, g_ws, tiny = _reduce_grads(g_wt, g_ws, tiny)
    chip_block = lambda g: g.reshape(2 * g.shape[1], g.shape[2])
    g_wt_sh = tuple(chip_block(g) for g in g_wt_sh)
    g_wkv_sh, g_wo_sh = chip_block(g_wkv_sh), chip_block(g_wo_sh)

    ws = (norm_gain, w_in, gmlp_v_gain, gmlp_w_s, gmlp_b, attn_q_gain, attn_k_gain, mem_norm_gain, w_mem_kv,
          mem_q_gain, mem_k_gain, w_out)
    ms = (m_norm_gain, m_w_in, m_gmlp_v_gain, m_gmlp_w_s, m_gmlp_b, m_attn_q_gain, m_attn_k_gain, m_mem_norm_gain,
          m_w_mem_kv, m_mem_q_gain, m_mem_k_gain, m_w_out)
    vs = (v_norm_gain, v_w_in, v_gmlp_v_gain, v_gmlp_w_s, v_gmlp_b, v_attn_q_gain, v_attn_k_gain, v_mem_norm_gain,
          v_w_mem_kv, v_mem_q_gain, v_mem_k_gain, v_w_out)
    form = {1: lambda a: a[0].T, 3: lambda a: a.reshape(4 * CHUNK, CHUNK), 2: lambda a: a[0], 4: lambda a: a[0],
            8: lambda a: a[0], 11: lambda a: a[0]}
    back = {1: lambda a: a.T[None], 3: lambda a: a.reshape(1, 4, CHUNK, CHUNK), 2: lambda a: a[None],
            4: lambda a: a[None], 8: lambda a: a[None], 11: lambda a: a[None]}
    fwd = lambda t, i: form.get(i, lambda a: a)(t[i])
    out = {}
    for i, g in ((1, g_wt_sh), (3, g_ws), (8, g_wkv_sh), (11, g_wo_sh)):
        out[i] = _adamw(fwd(ws, i), g, fwd(ms, i), fwd(vs, i))
    res = _adamw_tiny(tiny, [fwd(ws, i) for i in TINY_ORDER], [fwd(ms, i) for i in TINY_ORDER],
                      [fwd(vs, i) for i in TINY_ORDER])
    for k, i in enumerate(TINY_ORDER):
        out[i] = res[1 + 4 * k:5 + 4 * k]
    leaves = [[back.get(i, lambda a: a)(out[i][j]) for i in range(12)] for j in range(4)]
    return (res[0].reshape(()), gx[None], *leaves[0], *leaves[1], *leaves[2], *leaves[3])
```

```python
import math

import jax
import jax.numpy as jnp
from jax import lax
from jax.experimental import pallas as pl
from jax.experimental.pallas import tpu as pltpu

F32 = jnp.float32
BF16 = jnp.bfloat16

SEQ = 4096
D_MODEL = 1024
HEAD_DIM = 64
LANES = 128
CHUNK = 128
GMLP_W, ATTN_W, MEM_W = 256, 512, 256
IN_W = 3 * GMLP_W + 4 * ATTN_W + 2 * MEM_W
MEM_LEN = 256
DILATIONS = (16, 4, 1)
EPS = 1e-6
QK_SCALE = 1.0 / math.sqrt(HEAD_DIM)
C_GU, C_GV, C_GG, C_AQ, C_AK, C_AV, C_AG, C_MQ, C_MG = 0, 256, 512, 768, 1280, 1792, 2304, 2816, 3072

ADAM_LR, ADAM_B1, ADAM_B2, ADAM_EPS, ADAM_WD, ADAM_STEP = 0.001, 0.9, 0.999, 1e-08, 0.01, 10

VMEM_LIMIT = 48 * 1024 * 1024
RS_CHUNKS = 4
ATTN_UNROLL = 4
MESH = pl.DeviceIdType.MESH

TINY_ORDER = (0, 2, 4, 5, 6, 7, 9, 10)


def _call(body, **kw):
    return pl.pallas_call(body, **kw)


def _params(**kw):
    return pltpu.CompilerParams(vmem_limit_bytes=VMEM_LIMIT, **kw)


def _dot(a, b):
    return jnp.dot(a, b, preferred_element_type=F32)


def _dot_nt(a, b):
    return lax.dot_general(a, b, (((1,), (1,)), ((), ())), preferred_element_type=F32)


def _dot_tn(a, b):
    return lax.dot_general(a, b, (((0,), (0,)), ((), ())), preferred_element_type=F32)


def _head_blockdiag():
    r = lax.shift_right_logical(lax.broadcasted_iota(jnp.int32, (LANES, LANES), 0), 6)
    c = lax.shift_right_logical(lax.broadcasted_iota(jnp.int32, (LANES, LANES), 1), 6)
    return jnp.where(r == c, 1.0, 0.0).astype(BF16)


def _headsum(v, bd):
    hi = v.astype(BF16)
    lo = (v - hi.astype(F32)).astype(BF16)
    return _dot(hi, bd) + _dot(lo, bd)


def _lo_mask(rows):
    return lax.broadcasted_iota(jnp.int32, (rows, LANES), 1) < HEAD_DIM


def _sigmoid(x):
    return 1.0 / (1.0 + jnp.exp(-x))


def _fold_heads(v):
    return v + pltpu.roll(v, HEAD_DIM, 1)


def _put_rows(ref, vec, accumulate=False):
    for j in range(vec.shape[1] // LANES):
        piece = vec[:, j * LANES:(j + 1) * LANES]
        ref[j:j + 1, :] = ref[j:j + 1, :] + piece if accumulate else piece


def _gather_proj(x, gain, wt_sh):
    tm = 1024
    nrow = SEQ // tm
    widths = (768, 896, 768, 896)
    nunits = len(widths)
    pair = 2 * wt_sh.shape[0]
    assert pair % LANES == 0 and sum(widths[:2]) == pair

    def body(x_ref, g_ref, wt_sh_ref, proj_hbm, wt_hbm, h_scr, land, res, send_sems, recv_sems, out_sems, copy_sem):
        u, i = pl.program_id(0), pl.program_id(1)
        cx_, cy_ = lax.axis_index("x"), lax.axis_index("y")
        (send_own, pass_on_neighbours, pass_on_diagonal, _), (y_complete, x_complete, diagonal_complete, sends_done) = (
            _gather_stages((wt_sh_ref,), (land,), send_sems, recv_sems))
        first = lambda k: (u == k) & (i == 0)
        last = (u == nunits - 1) & (i == nrow - 1)
        to_hbm = pltpu.make_async_copy(land, wt_hbm, copy_sem)

        pl.when(first(0))(send_own)

        @pl.when(u == 0)
        def _():
            xv = x_ref[...]
            ms = jnp.mean(xv * xv, axis=-1, keepdims=True)
            h_scr[pl.ds(pl.multiple_of(i * tm, tm), tm), :] = (xv * lax.rsqrt(ms + EPS) * g_ref[...]).astype(BF16)

        @pl.when(first(1))
        def _():
            pass_on_neighbours()
            y_complete()

        @pl.when(first(2))
        def _():
            x_complete()
            pass_on_diagonal()

        @pl.when(first(3))
        def _():
            diagonal_complete()
            to_hbm.start()

        mine, other = pair * cx_, pair * (1 - cx_)
        col0 = (mine + 896 * cy_, mine + 768 * (1 - cy_), other + 896 * cy_, other + 768 * (1 - cy_))
        slot = i % 2
        rows = pl.ds(pl.multiple_of(i * tm, tm), tm)

        def writeback(k, rows_):
            c0 = pl.multiple_of(col0[k], LANES)
            return pltpu.make_async_copy(res.at[slot, :, pl.ds(0, widths[k])], proj_hbm.at[rows_, pl.ds(c0, widths[k])],
                                         out_sems.at[slot])

        for k in range(nunits):
            @pl.when(u == k)
            def _(k=k):
                pl.when(i >= 2)(writeback(k, rows).wait)
                if k > 0:
                    pl.when(i < 2)(writeback(k - 1, rows).wait)
                w_rows = land[pl.ds(pl.multiple_of(col0[k], LANES), widths[k]), :]
                res[slot, :, 0:widths[k]] = _dot_nt(h_scr[rows, :], w_rows)
                writeback(k, rows).start()

        @pl.when(last)
        def _():
            sends_done()
            to_hbm.wait()
            for s in range(2):
                pltpu.make_async_copy(res.at[s, :, pl.ds(0, widths[-1])], proj_hbm.at[rows, pl.ds(0, widths[-1])], out_sems.at[s]).wait()

    full = jax.ShapeDtypeStruct((4 * wt_sh.shape[0], wt_sh.shape[1]), BF16)
    hbm = pl.BlockSpec(memory_space=pl.ANY)
    return _call(
        body, name="gather_proj", grid=(nunits, nrow),
        in_specs=[pl.BlockSpec((tm, D_MODEL), lambda u, i: (jnp.where(u == 0, i, nrow - 1), 0)),
                  pl.BlockSpec((1, D_MODEL), lambda u, i: (0, 0)), pl.BlockSpec(wt_sh.shape, lambda u, i: (0, 0))],
        out_specs=[hbm, hbm],
        out_shape=[jax.ShapeDtypeStruct((SEQ, IN_W), F32), full],
        scratch_shapes=[pltpu.VMEM((SEQ, D_MODEL), BF16), pltpu.VMEM(full.shape, BF16), pltpu.VMEM((2, tm, max(widths)), F32),
                        pltpu.SemaphoreType.DMA((AG_SEMS,)), pltpu.SemaphoreType.DMA((AG_SEMS,)),
                        pltpu.SemaphoreType.DMA((2,)), pltpu.SemaphoreType.DMA],
        compiler_params=_params(),
    )(x, gain, wt_sh)


def _gmlp_weights(w_ref):
    ti = lax.broadcasted_iota(jnp.int32, (CHUNK, CHUNK), 0)
    si = lax.broadcasted_iota(jnp.int32, (CHUNK, CHUNK), 1)
    tril = si <= ti
    return tril, [jnp.where(tril, w_ref[h], 0.0).astype(BF16) for h in range(4)]


def _gmlp_fwd(proj, vgain, w_s, bias_full):
    tm = 1024

    def body(p_ref, vg_ref, w_ref, b_ref, y_ref):
        bd = _head_blockdiag()
        lo = _lo_mask(CHUNK)
        _, wm = _gmlp_weights(w_ref)
        units = [(pl.ds(c * CHUNK, CHUNK), p) for c in range(tm // CHUNK) for p in range(2)]
        col = lambda c0, p: slice(c0 + p * LANES, c0 + (p + 1) * LANES)
        vs = [p_ref[rows, col(C_GV, p)] for rows, p in units]
        rs = [lax.rsqrt(_headsum(v * v, bd) * (1.0 / HEAD_DIM) + EPS) for v in vs]
        vns = [(v * r * vg_ref[:, col(0, p)]).astype(BF16) for v, r, (_, p) in zip(vs, rs, units)]
        sps = [jnp.where(lo, _dot(wm[2 * p], vn), _dot(wm[2 * p + 1], vn)) + b_ref[:, col(0, p)] for vn, (_, p) in zip(vns, units)]
        for sp, (rows, p) in zip(sps, units):
            gt = p_ref[rows, col(C_GG, p)]
            y_ref[rows, col(0, p)] = (p_ref[rows, col(C_GU, p)] * sp * (gt * _sigmoid(gt))).astype(BF16)

    return _call(
        body, name="gmlp_fwd", grid=(SEQ // tm,),
        in_specs=[pl.BlockSpec((tm, 3 * GMLP_W), lambda i: (i, 0)),
                  pl.BlockSpec((1, GMLP_W), lambda i: (0, 0)),
                  pl.BlockSpec((4, CHUNK, CHUNK), lambda i: (0, 0, 0)),
                  pl.BlockSpec((CHUNK, GMLP_W), lambda i: (0, 0))],
        out_specs=pl.BlockSpec((tm, GMLP_W), lambda i: (i, 0)),
        out_shape=jax.ShapeDtypeStruct((SEQ, GMLP_W), BF16),
        compiler_params=_params(),
    )(proj, vgain, w_s, bias_full)


def _gmlp_bwd(proj, dyc, vgain, w_s, bias_full):
    tm = 1024
    nsteps = SEQ // tm

    def body(p_ref, dy_ref, vg_ref, w_ref, b_ref, dg_ref, gw_ref, gb_ref, gv_ref):
        i = pl.program_id(0)
        bd = _head_blockdiag()
        lo = _lo_mask(CHUNK)
        tril, wm = _gmlp_weights(w_ref)
        ri = lax.broadcasted_iota(jnp.int32, (16, LANES), 0)
        li = lax.broadcasted_iota(jnp.int32, (16, LANES), 1)
        head_rows = [jnp.where(((ri == 2 * p) & (li < HEAD_DIM)) | ((ri == 2 * p + 1) & (li >= HEAD_DIM)), 1.0, 0.0).astype(BF16)
                     for p in range(2)]

        @pl.when(i == 0)
        def _():
            gw_ref[...] = jnp.zeros_like(gw_ref)
            gb_ref[...] = jnp.zeros_like(gb_ref)
            gv_ref[...] = jnp.zeros_like(gv_ref)

        units = [(pl.ds(c * CHUNK, CHUNK), p) for c in range(tm // CHUNK) for p in range(2)]
        col = lambda c0, p: slice(c0 + p * LANES, c0 + (p + 1) * LANES)
        vs = [p_ref[rows, col(C_GV, p)] for rows, p in units]
        rs = [lax.rsqrt(_headsum(v * v, bd) * (1.0 / HEAD_DIM) + EPS) for v in vs]
        zs = [v * r for v, r in zip(vs, rs)]
        vns = [(z * vg_ref[:, col(0, p)]).astype(BF16) for z, (_, p) in zip(zs, units)]
        sps = [jnp.where(lo, _dot(wm[2 * p], vn), _dot(wm[2 * p + 1], vn)) + b_ref[:, col(0, p)] for vn, (_, p) in zip(vns, units)]
        dsps = []
        for sp, (rows, p) in zip(sps, units):
            u = p_ref[rows, col(C_GU, p)]
            gt = p_ref[rows, col(C_GG, p)]
            dy = dy_ref[rows, col(0, p)]
            sg = _sigmoid(gt)
            sl = gt * sg
            dg_ref[rows, col(C_GU, p)] = (dy * sp * sl).astype(BF16)
            dg_ref[rows, col(C_GG, p)] = (dy * u * sp * (sg * (1.0 + gt * (1.0 - sg)))).astype(BF16)
            dsps.append(dy * u * sl)
        dspbs = [dsp.astype(BF16) for dsp in dsps]
        dvns = [jnp.where(lo, _dot_tn(wm[2 * p], dspb), _dot_tn(wm[2 * p + 1], dspb)) for dspb, (_, p) in zip(dspbs, units)]
        gws = [(_dot_nt(jnp.where(lo, dsp, 0.0).astype(BF16), vn), _dot_nt(jnp.where(lo, 0.0, dsp).astype(BF16), vn))
               for dsp, vn in zip(dsps, vns)]
        gbs = [(_dot_nt(head_rows[p], dspb) + _dot_nt(head_rows[p], (dsp - dspb.astype(F32)).astype(BF16)))[0:8]
               for dsp, dspb, (_, p) in zip(dsps, dspbs, units)]
        for p in range(2):
            mine = [n for n, (_, q) in enumerate(units) if q == p]
            gw_ref[2 * p] += sum(gws[n][0] for n in mine)
            gw_ref[2 * p + 1] += sum(gws[n][1] for n in mine)
            gvp = sum(jnp.sum(dvns[n] * zs[n], axis=0, keepdims=True) for n in mine)
            gv_ref[2 * p:2 * p + 1, :] += gvp
            gv_ref[2 * p + 1:2 * p + 2, :] += pltpu.roll(gvp, HEAD_DIM, 1)
        gb_ref[...] += sum(gbs)
        for dvn, z, r, (rows, p) in zip(dvns, zs, rs, units):
            dz = dvn * vg_ref[:, col(0, p)]
            dg_ref[rows, col(C_GV, p)] = (r * (dz - z * (_headsum(dz * z, bd) * (1.0 / HEAD_DIM)))).astype(BF16)

        @pl.when(i == nsteps - 1)
        def _():
            for h in range(4):
                gw_ref[h] = jnp.where(tril, gw_ref[h], 0.0)

    return _call(
        body, name="gmlp_bwd", grid=(nsteps,),
        in_specs=[pl.BlockSpec((tm, 3 * GMLP_W), lambda i: (i, 0)),
                  pl.BlockSpec((tm, GMLP_W), lambda i: (i, 0)),
                  pl.BlockSpec((1, GMLP_W), lambda i: (0, 0)),
                  pl.BlockSpec((4, CHUNK, CHUNK), lambda i: (0, 0, 0)),
                  pl.BlockSpec((CHUNK, GMLP_W), lambda i: (0, 0))],
        out_specs=[pl.BlockSpec((tm, 3 * GMLP_W), lambda i: (i, 0)),
                   pl.BlockSpec((4, CHUNK, CHUNK), lambda i: (0, 0, 0)),
                   pl.BlockSpec((8, LANES), lambda i: (0, 0)),
                   pl.BlockSpec((8, LANES), lambda i: (0, 0))],
        out_shape=[jax.ShapeDtypeStruct((SEQ, 3 * GMLP_W), BF16),
                   jax.ShapeDtypeStruct((4, CHUNK, CHUNK), F32),
                   jax.ShapeDtypeStruct((8, LANES), F32),
                   jax.ShapeDtypeStruct((8, LANES), F32)],
        compiler_params=_params(),
    )(proj, dyc, vgain, w_s, bias_full)


def _band_masks():
    qi = lax.broadcasted_iota(jnp.int32, (CHUNK, 2 * CHUNK), 0)
    kj = lax.broadcasted_iota(jnp.int32, (CHUNK, 2 * CHUNK), 1)
    valid2 = ((kj < CHUNK) & (kj >= qi)) | ((kj >= CHUNK) & (kj - CHUNK <= qi))
    q1 = lax.broadcasted_iota(jnp.int32, (CHUNK, CHUNK), 0)
    k1 = lax.broadcasted_iota(jnp.int32, (CHUNK, CHUNK), 1)
    return k1 <= q1, valid2


def _stack_heads(v, lo):
    return jnp.concatenate([jnp.where(lo, v, 0.0), jnp.where(lo, 0.0, v)], axis=0).astype(BF16)


def _rows_of(ref, start, d):
    if d == 1:
        return ref.at[pl.ds(start if isinstance(start, int) else pl.multiple_of(start, CHUNK), CHUNK), :]
    return ref.at[pl.ds(start, CHUNK, stride=d), :]


def _unrolled(lo, hi, unroll, run):
    groups = (hi - lo) // unroll
    if groups:
        def body(g, carry):
            run([lo + g * unroll + t for t in range(unroll)])
            return carry

        lax.fori_loop(0, groups, body, 0)
    if lo + groups * unroll < hi:
        run(range(lo + groups * unroll, hi))


def _for_blocks(d, group_fn, unroll):
    nblk = SEQ // CHUNK
    sh = d.bit_length() - 1

    def first(j):
        return (j * CHUNK if d == 1 else j, None)

    def rest(j):
        start = (j & (d - 1)) + (j >> sh) * (CHUNK * d)
        return (start, start - CHUNK * d)

    _unrolled(0, d, unroll, lambda js: group_fn(d, [first(j) for j in js]))
    _unrolled(d, nblk, unroll, lambda js: group_fn(d, [rest(j) for j in js]))


def _attn_fwd(proj, gq2, gk2, *ride_along):
    tn_norm, tn = 2048, 256
    npairs = ATTN_W // LANES
    nride = len(ride_along)

    def body(q_ref, k_ref, v_ref, g_ref, gq_ref, gk_ref, *rest):
        shards, rest = rest[:nride], rest[nride:]
        o_ref, l_ref, ya_ref = rest[:3]
        gathered, rest = rest[3:3 + nride], rest[3 + nride:]
        qn_ref, kn_ref = rest[:2]
        lands, (send_sems, recv_sems, copy_sems) = rest[2:2 + nride], rest[2 + nride:]
        pair = pl.program_id(0)
        ride = _gather_stages(shards, lands, send_sems, recv_sems)[0]
        for step in range(npairs):
            pl.when(pair == step)(ride[step])
        bd = _head_blockdiag()
        lo = _lo_mask(CHUNK)
        valid1, valid2 = _band_masks()

        def norm(t, carry):
            rows = pl.ds(pl.multiple_of(t * tn_norm, tn_norm), tn_norm)
            q, k = q_ref[rows, :], k_ref[rows, :]
            ssq = [_headsum(a * a, bd) for a in (q, k)]
            qn_ref[rows, :] = q * lax.rsqrt(ssq[0] * (1.0 / HEAD_DIM) + EPS) * (gq_ref[...] * QK_SCALE)
            kn_ref[rows, :] = k * lax.rsqrt(ssq[1] * (1.0 / HEAD_DIM) + EPS) * gk_ref[...]
            return carry

        lax.fori_loop(0, SEQ // tn_norm, norm, 0)

        def load_kv(ref, d, start, prev):
            own = _rows_of(ref, start, d)[...]
            if prev is None:
                return own.astype(BF16)
            return jnp.concatenate([_rows_of(ref, prev, d)[...], own], axis=0).astype(BF16)

        def group(d, blocks):
            valid = valid1 if blocks[0][1] is None else valid2
            valid = jnp.concatenate([valid, valid], axis=0)
            qs = [_rows_of(qn_ref, start, d)[...] for start, _ in blocks]
            ks = [load_kv(kn_ref, d, start, prev) for start, prev in blocks]
            vs = [load_kv(v_ref, d, start, prev) for start, prev in blocks]
            ss = [_dot_nt(_stack_heads(q, lo), k) for q, k in zip(qs, ks)]
            ms, ps, ls = [], [], []
            for s in ss:
                s = jnp.where(valid, s, -jnp.inf)
                m = jnp.max(s, axis=-1, keepdims=True)
                p = jnp.exp(s - m)
                ms.append(m)
                ls.append(jnp.sum(p, axis=-1, keepdims=True))
                ps.append(p.astype(BF16))
            os_ = [_dot(p, v) for p, v in zip(ps, vs)]
            for b, (start, _) in enumerate(blocks):
                heads = lambda v: jnp.where(lo, v[:CHUNK], v[CHUNK:])
                lsum = heads(ls[b])
                ob = heads(os_[b]) * (1.0 / lsum)
                lb = heads(ms[b]) + jnp.log(lsum)
                o_rows = _rows_of(o_ref, start, d)
                l_rows = _rows_of(l_ref, start, d)
                if d != DILATIONS[0]:
                    lold = l_rows[...]
                    mx = jnp.maximum(lold, lb)
                    ea = jnp.exp(lold - mx)
                    eb = jnp.exp(lb - mx)
                    inv = 1.0 / (ea + eb)
                    ob = o_rows[...] * (ea * inv) + ob * (eb * inv)
                    lb = mx + jnp.log(ea + eb)
                o_rows[...] = ob
                l_rows[...] = lb

        for d in DILATIONS:
            _for_blocks(d, group, ATTN_UNROLL)

        def fin(t, carry):
            rows = pl.ds(pl.multiple_of(t * tn, tn), tn)
            g = g_ref[rows, :]
            ya_ref[rows, :] = (o_ref[rows, :] * (g * _sigmoid(g))).astype(BF16)
            return carry

        lax.fori_loop(0, SEQ // tn, fin, 0)

        @pl.when(pair == npairs - 1)
        def _():
            to_hbm = [pltpu.make_async_copy(land, out, copy_sems.at[n]) for n, (land, out) in enumerate(zip(lands, gathered))]
            for cp in to_hbm:
                cp.start()
            for cp in to_hbm:
                cp.wait()

    col = lambda c0: pl.BlockSpec((SEQ, LANES), lambda p: (0, c0 // LANES + p))
    vec = pl.BlockSpec((1, LANES), lambda p: (0, 0))
    out = pl.BlockSpec((SEQ, LANES), lambda p: (0, p))
    full = [jax.ShapeDtypeStruct((4 * a.shape[0], a.shape[1]), BF16) for a in ride_along]
    return _call(
        body, name="attn_fwd", grid=(npairs,),
        in_specs=[col(C_AQ), col(C_AK), col(C_AV), col(C_AG), vec, vec]
        + [pl.BlockSpec(a.shape, lambda p: (0, 0)) for a in ride_along],
        out_specs=[out, out, out] + [pl.BlockSpec(memory_space=pl.ANY)] * nride,
        out_shape=[jax.ShapeDtypeStruct((SEQ, ATTN_W), F32), jax.ShapeDtypeStruct((SEQ, ATTN_W), F32),
                   jax.ShapeDtypeStruct((SEQ, ATTN_W), BF16)] + full,
        scratch_shapes=[pltpu.VMEM((SEQ, LANES), F32), pltpu.VMEM((SEQ, LANES), F32)]
        + [pltpu.VMEM(s.shape, BF16) for s in full]
        + [pltpu.SemaphoreType.DMA((AG_SEMS * nride,)), pltpu.SemaphoreType.DMA((AG_SEMS * nride,)),
           pltpu.SemaphoreType.DMA((nride,))],
        compiler_params=_params(),
    )(proj, proj, proj, proj, gq2, gk2, *ride_along)


def _attn_bwd(proj, o, lse, dyc, gq2, gk2, *ride_along):
    tn = 2048
    npairs = ATTN_W // LANES
    nride = len(ride_along)
    nbufs = nride * len(RS_KINDS)

    def body(proj_hbm, o_hbm, l_hbm, dyc_hbm, gq_ref, gk_ref, *rest):
        ride_in, rest = rest[:nride], rest[nride:]
        dq_ref, dk_ref, dv_ref, dgt_ref, gqg_ref, gkg_ref = rest[:6]
        ride_out, rest = rest[6:6 + nride], rest[6 + nride:]
        qb_, kb_, vb_, gb_, ob_, lb_, yb_, dkb_, dvb_, sems = rest[:10]
        rs_bufs, (send_sems, recv_sems, local_sems) = rest[10:10 + nbufs], rest[10 + nbufs:]
        rs_stage = _rs_stages(ride_in, ride_out, rs_bufs, send_sems, recv_sems, local_sems, [g.shape[1] for g in ride_along])
        pair = pl.program_id(0)
        for step in range(npairs):
            pl.when(pair == step)(rs_stage[step])
        bd = _head_blockdiag()
        lo = _lo_mask(CHUNK)
        lo2 = lax.broadcasted_iota(jnp.int32, (2 * CHUNK, LANES), 1) < HEAD_DIM
        valid1, valid2 = _band_masks()
        gqs = gq_ref[...] * QK_SCALE
        gk = gk_ref[...]

        def pcol(c0, of=None):
            return acol(proj_hbm, c0, of)

        def acol(hbm, c0=0, of=None):
            of = pair if of is None else of
            return hbm.at[:, pl.ds(pl.multiple_of(c0 + of * LANES, LANES), LANES)]

        def input_loads(of):
            return [pltpu.make_async_copy(src, dst, sems.at[n]) for n, (src, dst) in enumerate((
                (pcol(C_AQ, of), qb_), (pcol(C_AK, of), kb_), (pcol(C_AG, of), gb_), (acol(o_hbm, 0, of), ob_),
                (acol(dyc_hbm, GMLP_W, of), yb_), (pcol(C_AV, of), vb_), (acol(l_hbm, 0, of), lb_)))]

        early = (0, 1, 3, 4)
        loads = input_loads(pair)
        for n, cp in enumerate(loads):
            if n in early:
                pl.when(pair == 0)(cp.start)
            else:
                cp.start()

        @pl.when(pair == 0)
        def _():
            gqg_ref[...] = jnp.zeros_like(gqg_ref)
            gkg_ref[...] = jnp.zeros_like(gkg_ref)

        def pre_qk(t, carry):
            rows = pl.ds(pl.multiple_of(t * tn, tn), tn)
            q, k = qb_[rows, :], kb_[rows, :]
            ssq = [_headsum(a * a, bd) for a in (q, k)]
            qb_[rows, :] = q * lax.rsqrt(ssq[0] * (1.0 / HEAD_DIM) + EPS) * gqs
            kb_[rows, :] = k * lax.rsqrt(ssq[1] * (1.0 / HEAD_DIM) + EPS) * gk
            return carry

        def pre_gate(t, carry):
            rows = pl.ds(pl.multiple_of(t * tn, tn), tn)
            g = gb_[rows, :]
            ov = ob_[rows, :]
            dya = yb_[rows, :]
            sg = _sigmoid(g)
            dgt_ref[rows, :] = (dya * ov * (sg * (1.0 + g * (1.0 - sg)))).astype(BF16)
            do = dya * (g * sg)
            yb_[rows, :] = do
            ob_[rows, :] = jnp.where(first_half, lb_[rows, :], _headsum(do * ov, bd))
            return carry

        first_half = (lax.broadcasted_iota(jnp.int32, (tn, LANES), 1) & (HEAD_DIM - 1)) < HEAD_DIM // 2
        loads[0].wait()
        loads[1].wait()
        lax.fori_loop(0, SEQ // tn, pre_qk, 0)
        for cp in loads[2:5] + loads[6:7]:
            cp.wait()
        lax.fori_loop(0, SEQ // tn, pre_gate, 0)
        loads[5].wait()
        reloads = [pltpu.make_async_copy(pcol(C_AQ), lb_, sems.at[7]), pltpu.make_async_copy(pcol(C_AK), vb_, sems.at[8])]
        reloads[0].start()

        def load_kv(ref, d, start, prev):
            own = _rows_of(ref, start, d)[...]
            if prev is None:
                return own.astype(BF16)
            return jnp.concatenate([_rows_of(ref, prev, d)[...], own], axis=0).astype(BF16)

        def group(d, blocks):
            first = blocks[0][1] is None
            valid, lok = (valid1, lo) if first else (valid2, lo2)
            chains = [(b, h) for b in range(len(blocks)) for h in range(2)]
            mask = lambda h: lo if h == 0 else ~lo
            qs = [_rows_of(qb_, start, d)[...] for start, _ in blocks]
            dos = [_rows_of(yb_, start, d)[...] for start, _ in blocks]
            lds = [_rows_of(ob_, start, d)[...] for start, _ in blocks]
            ks = [load_kv(kb_, d, start, prev) for start, prev in blocks]
            vs = [load_kv(vb_, d, start, prev) for start, prev in blocks]
            qbs = [q.astype(BF16) for q in qs]
            dobs = [do.astype(BF16) for do in dos]
            ss = [_dot_nt(jnp.where(mask(h), qs[b], 0.0).astype(BF16), ks[b]) for b, h in chains]
            dps = [_dot_nt(jnp.where(mask(h), dos[b], 0.0).astype(BF16), vs[b]) for b, h in chains]
            pbs, dss = [], []
            for s, dp, (b, h) in zip(ss, dps, chains):
                hc, dc = h * HEAD_DIM, h * HEAD_DIM + HEAD_DIM // 2
                p = jnp.exp(jnp.where(valid, s, -jnp.inf) - lds[b][:, hc:hc + 1])
                pbs.append(p.astype(BF16))
                dss.append((p * (dp - lds[b][:, dc:dc + 1])).astype(BF16))
            dqs = [_dot(ds, ks[b]) for ds, (b, h) in zip(dss, chains)]
            dks = [_dot_tn(ds, qbs[b]) for ds, (b, h) in zip(dss, chains)]
            dvs = [_dot_tn(p, dobs[b]) for p, (b, h) in zip(pbs, chains)]
            assign = d == DILATIONS[0]
            for b, (start, prev) in enumerate(blocks):
                c0, c1 = 2 * b, 2 * b + 1
                dq_rows = _rows_of(gb_, start, d)
                dqb = jnp.where(lo, dqs[c0], dqs[c1])
                dq_rows[...] = dqb if assign else dq_rows[...] + dqb
                dkc = jnp.where(lok, dks[c0], dks[c1])
                dvc = jnp.where(lok, dvs[c0], dvs[c1])
                spans = ((start, slice(0, CHUNK), True),) if first else (
                    (prev, slice(0, CHUNK), False), (start, slice(CHUNK, 2 * CHUNK), True))
                for st, sl, own in spans:
                    dk_rows = _rows_of(dkb_, st, d)
                    dv_rows = _rows_of(dvb_, st, d)
                    if assign and own:
                        dk_rows[...] = dkc[sl]
                        dv_rows[...] = dvc[sl]
                    else:
                        dk_rows[...] = dk_rows[...] + dkc[sl]
                        dv_rows[...] = dv_rows[...] + dvc[sl]

        for d in DILATIONS:
            _for_blocks(d, group, ATTN_UNROLL)

        reloads[1].start()

        @pl.when(pair < npairs - 1)
        def _():
            nxt = input_loads(pair + 1)
            for n in early:
                nxt[n].start()

        for cp in reloads:
            cp.wait()

        def post(t, carry):
            gq_acc, gk_acc = carry
            rows = pl.ds(pl.multiple_of(t * tn, tn), tn)
            raws = [lb_[rows, :], vb_[rows, :]]
            dns = [gb_[rows, :], dkb_[rows, :]]
            rs = [lax.rsqrt(_headsum(a * a, bd) * (1.0 / HEAD_DIM) + EPS) for a in raws]
            zs = [a * r for a, r in zip(raws, rs)]
            dzs = [dn * gain for dn, gain in zip(dns, (gqs, gk))]
            means = [_headsum(dz * z, bd) * (1.0 / HEAD_DIM) for dz, z in zip(dzs, zs)]
            dq, dk = [r * (dz - z * mean) for r, dz, z, mean in zip(rs, dzs, zs, means)]
            gq, gkk = [jnp.sum(dn * z, axis=0, keepdims=True) for dn, z in zip(dns, zs)]
            dq_ref[rows, :] = dq.astype(BF16)
            dk_ref[rows, :] = dk.astype(BF16)
            dv_ref[rows, :] = dvb_[rows, :].astype(BF16)
            return gq_acc + gq * QK_SCALE, gk_acc + gkk

        zero = jnp.zeros((1, LANES), F32)
        gq_acc, gk_acc = lax.fori_loop(0, SEQ // tn, post, (zero, zero))
        gqg_ref[0:1, :] += gq_acc
        gkg_ref[0:1, :] += gk_acc

        @pl.when(pair == npairs - 1)
        def _():
            gqg_ref[0:1, :] = _fold_heads(gqg_ref[0:1, :])
            gkg_ref[0:1, :] = _fold_heads(gkg_ref[0:1, :])
            rs_stage[npairs]()

    hbm = pl.BlockSpec(memory_space=pl.ANY)
    vec = pl.BlockSpec((1, LANES), lambda p: (0, 0))
    blk8 = pl.BlockSpec((8, LANES), lambda p: (0, 0))
    out = pl.BlockSpec((SEQ, LANES), lambda p: (0, p))
    big = jax.ShapeDtypeStruct((SEQ, ATTN_W), BF16)
    nsem = RS_SEMS * nride
    return _call(
        body, name="attn_bwd", grid=(npairs,),
        in_specs=[hbm, hbm, hbm, hbm, vec, vec] + [hbm] * nride,
        out_specs=[out, out, out, out, blk8, blk8] + [hbm] * nride,
        out_shape=[big, big, big, big, jax.ShapeDtypeStruct((8, LANES), F32), jax.ShapeDtypeStruct((8, LANES), F32)]
        + [jax.ShapeDtypeStruct((2, g.shape[0] // 8, g.shape[1]), F32) for g in ride_along],
        scratch_shapes=[pltpu.VMEM((SEQ, LANES), F32) for _ in range(9)] + [pltpu.SemaphoreType.DMA((9,))]
        + _rs_scratch([g.shape for g in ride_along]) + [pltpu.SemaphoreType.DMA((nsem,)), pltpu.SemaphoreType.DMA((nsem,)),
                                     pltpu.SemaphoreType.DMA((nride,))],
        compiler_params=_params(),
    )(proj, o, lse, dyc, gq2, gk2, *[_rs_view(g) for g in ride_along])


def _mem_kv(mem, gain, wkv):
    def body(m_ref, g_ref, w_ref, kv_ref, hm_ref):
        mv = m_ref[...]
        ms = jnp.mean(mv * mv, axis=-1, keepdims=True)
        hm = (mv * lax.rsqrt(ms + EPS) * g_ref[...]).astype(BF16)
        hm_ref[...] = hm
        kv_ref[...] = _dot(hm, w_ref[...])

    return _call(
        body, name="mem_kv",
        out_shape=[jax.ShapeDtypeStruct((MEM_LEN, 2 * MEM_W), F32), jax.ShapeDtypeStruct((MEM_LEN, D_MODEL), BF16)],
        compiler_params=_params(),
    )(mem, gain, wkv)


def _mem_keys(kv_ref, kg_ref, bd, p):
    mk = kv_ref[:, p * LANES:(p + 1) * LANES]
    r = lax.rsqrt(_headsum(mk * mk, bd) * (1.0 / HEAD_DIM) + EPS)
    z = mk * r
    mkn = (z * kg_ref[:, p * LANES:(p + 1) * LANES]).astype(BF16)
    mvp = kv_ref[:, MEM_W + p * LANES:MEM_W + (p + 1) * LANES].astype(BF16)
    return mkn, mvp, r, z


def _mem_fwd(proj, kv, qg4, kg4):
    tm = 1024

    def body(q_ref, g_ref, kv_ref, qg_ref, kg_ref, om_ref, ym_ref):
        bd = _head_blockdiag()
        lo = _lo_mask(tm)
        keys, qns = [], []
        for p in range(2):
            cs = slice(p * LANES, (p + 1) * LANES)
            keys.append(_mem_keys(kv_ref, kg_ref, bd, p)[:2])
            q = q_ref[:, cs]
            qns.append(q * lax.rsqrt(_headsum(q * q, bd) * (1.0 / HEAD_DIM) + EPS) * (qg_ref[:, cs] * QK_SCALE))
        chains = [(p, h) for p in range(2) for h in range(2)]
        ss = [_dot_nt(jnp.where(lo if h == 0 else ~lo, qns[p], 0.0).astype(BF16), keys[p][0]) for p, h in chains]
        es = [jnp.exp(s - jnp.max(s, axis=-1, keepdims=True)) for s in ss]
        os_ = [_dot(e.astype(BF16), keys[p][1]) for e, (p, h) in zip(es, chains)]
        res = [o * (1.0 / jnp.sum(e, axis=-1, keepdims=True)) for o, e in zip(os_, es)]
        for p in range(2):
            cs = slice(p * LANES, (p + 1) * LANES)
            ov = jnp.where(lo, res[2 * p], res[2 * p + 1])
            g = g_ref[:, cs]
            om_ref[:, cs] = ov
            ym_ref[:, cs] = (ov * (g * _sigmoid(g))).astype(BF16)

    vec = pl.BlockSpec((1, MEM_W), lambda i: (0, 0))
    return _call(
        body, name="mem_fwd", grid=(SEQ // tm,),
        in_specs=[pl.BlockSpec((tm, MEM_W), lambda i: (i, C_MQ // MEM_W)),
                  pl.BlockSpec((tm, MEM_W), lambda i: (i, C_MG // MEM_W)),
                  pl.BlockSpec((MEM_LEN, 2 * MEM_W), lambda i: (0, 0)), vec, vec],
        out_specs=[pl.BlockSpec((tm, MEM_W), lambda i: (i, 0)), pl.BlockSpec((tm, MEM_W), lambda i: (i, 0))],
        out_shape=[jax.ShapeDtypeStruct((SEQ, MEM_W), F32), jax.ShapeDtypeStruct((SEQ, MEM_W), BF16)],
        compiler_params=_params(),
    )(proj, proj, kv, qg4, kg4)


def _mem_bwd(proj, om, dyc, kv, hm, mem, mgain, wkv, qg4, kg4):
    tm = 1024
    nsteps = SEQ // tm

    def body(q_ref, g_ref, om_ref, dy_ref, kv_ref, hm_ref, mem_ref, mg_ref, w_ref, qg_ref, kg_ref,
             dq_ref, dgt_ref, gqg_ref, gkg_ref, gw_ref, gmg_ref, dmk_ref, dmv_ref, gq_acc):
        i = pl.program_id(0)
        bd = _head_blockdiag()
        lo = _lo_mask(tm)
        lom = _lo_mask(MEM_LEN)

        @pl.when(i == 0)
        def _():
            dmk_ref[...] = jnp.zeros_like(dmk_ref)
            dmv_ref[...] = jnp.zeros_like(dmv_ref)
            gq_acc[...] = jnp.zeros_like(gq_acc)

        pairs = []
        for p in range(2):
            cs = slice(p * LANES, (p + 1) * LANES)
            mkn, mvp, _, _ = _mem_keys(kv_ref, kg_ref, bd, p)
            gqs = qg_ref[:, cs] * QK_SCALE
            q = q_ref[:, cs]
            r = lax.rsqrt(_headsum(q * q, bd) * (1.0 / HEAD_DIM) + EPS)
            z = q * r
            qn = z * gqs
            g = g_ref[:, cs]
            ov = om_ref[:, cs]
            dym = dy_ref[:, cs]
            sg = _sigmoid(g)
            dgt_ref[:, cs] = (dym * ov * (sg * (1.0 + g * (1.0 - sg)))).astype(BF16)
            do = dym * (g * sg)
            pairs.append(dict(cs=cs, mkn=mkn, mvp=mvp, gqs=gqs, r=r, z=z, qn=qn, qnb=qn.astype(BF16), do=do,
                              dob=do.astype(BF16), delta=_headsum(do * ov, bd)))
        chains = [(pr_, h) for pr_ in pairs for h in range(2)]
        mask = lambda h: lo if h == 0 else ~lo
        ss = [_dot_nt(jnp.where(mask(h), c["qn"], 0.0).astype(BF16), c["mkn"]) for c, h in chains]
        dps = [_dot_nt(jnp.where(mask(h), c["do"], 0.0).astype(BF16), c["mvp"]) for c, h in chains]
        prs, dss = [], []
        for s, dp, (c, h) in zip(ss, dps, chains):
            e = jnp.exp(s - jnp.max(s, axis=-1, keepdims=True))
            pr = e * (1.0 / jnp.sum(e, axis=-1, keepdims=True))
            prs.append(pr.astype(BF16))
            dss.append((pr * (dp - c["delta"][:, h * HEAD_DIM:h * HEAD_DIM + 1])).astype(BF16))
        dqs = [_dot(ds, c["mkn"]) for ds, (c, h) in zip(dss, chains)]
        dks = [_dot_tn(ds, c["qnb"]) for ds, (c, h) in zip(dss, chains)]
        dvs = [_dot_tn(pr, c["dob"]) for pr, (c, h) in zip(prs, chains)]
        for p, c in enumerate(pairs):
            cs, z, r = c["cs"], c["z"], c["r"]
            dqn = jnp.where(lo, dqs[2 * p], dqs[2 * p + 1])
            dmk_ref[:, cs] += jnp.where(lom, dks[2 * p], dks[2 * p + 1])
            dmv_ref[:, cs] += jnp.where(lom, dvs[2 * p], dvs[2 * p + 1])
            dz = dqn * c["gqs"]
            dq_ref[:, cs] = (r * (dz - z * (_headsum(dz * z, bd) * (1.0 / HEAD_DIM)))).astype(BF16)
            gq_acc[:, cs] += jnp.sum(dqn * z, axis=0, keepdims=True) * QK_SCALE

        @pl.when(i == nsteps - 1)
        def _():
            gqg_ref[...] = jnp.zeros_like(gqg_ref)
            gkg_ref[...] = jnp.zeros_like(gkg_ref)
            gqg_ref[0:1, :] = _fold_heads(gq_acc[:, 0:LANES] + gq_acc[:, LANES:2 * LANES])
            dkv = []
            gk = jnp.zeros((1, LANES), F32)
            for p in range(2):
                cs = slice(p * LANES, (p + 1) * LANES)
                _, _, r, z = _mem_keys(kv_ref, kg_ref, bd, p)
                dn = dmk_ref[:, cs]
                dz = dn * kg_ref[:, cs]
                gk = gk + jnp.sum(dn * z, axis=0, keepdims=True)
                dkv.append(r * (dz - z * (_headsum(dz * z, bd) * (1.0 / HEAD_DIM))))
            gkg_ref[0:1, :] = _fold_heads(gk)
            dkvb = jnp.concatenate(dkv + [dmv_ref[...]], axis=1).astype(BF16)
            gw_ref[...] = _dot_tn(hm_ref[...], dkvb)
            dhm = _dot_nt(dkvb, w_ref[...])
            mv = mem_ref[...]
            zm = mv * lax.rsqrt(jnp.mean(mv * mv, axis=-1, keepdims=True) + EPS)
            _put_rows(gmg_ref, jnp.sum(dhm * zm, axis=0, keepdims=True))

    const = lambda shape: pl.BlockSpec(shape, lambda i: (0,) * len(shape))
    row = lambda j: pl.BlockSpec((tm, MEM_W), lambda i: (i, j))
    blk8 = jax.ShapeDtypeStruct((8, LANES), F32)
    return _call(
        body, name="mem_bwd", grid=(nsteps,),
        in_specs=[row(C_MQ // MEM_W), row(C_MG // MEM_W), row(0), row((GMLP_W + ATTN_W) // MEM_W),
                  const((MEM_LEN, 2 * MEM_W)), const((MEM_LEN, D_MODEL)), const((MEM_LEN, D_MODEL)),
                  const((1, D_MODEL)), const((D_MODEL, 2 * MEM_W)), const((1, MEM_W)), const((1, MEM_W))],
        out_specs=[row(0), row(0), const((8, LANES)), const((8, LANES)),
                   const((D_MODEL, 2 * MEM_W)), const((8, LANES))],
        out_shape=[jax.ShapeDtypeStruct((SEQ, MEM_W), BF16), jax.ShapeDtypeStruct((SEQ, MEM_W), BF16),
                   blk8, blk8, jax.ShapeDtypeStruct((D_MODEL, 2 * MEM_W), F32), blk8],
        scratch_shapes=[pltpu.VMEM((MEM_LEN, MEM_W), F32), pltpu.VMEM((MEM_LEN, MEM_W), F32),
                        pltpu.VMEM((1, MEM_W), F32)],
        compiler_params=_params(),
    )(proj, proj, om, dyc, kv, hm, mem, mgain, wkv, qg4, kg4)


def _out_loss(yg, ya, ym, x, tgt, wo):
    tm = 512
    nsteps = SEQ // tm
    parts = ((0, GMLP_W), (GMLP_W, ATTN_W), (GMLP_W + ATTN_W, MEM_W))

    def body(yg_ref, ya_ref, ym_ref, x_ref, t_ref, w_ref, dy_ref, dyc_ref, gw_ref, ls_ref):
        i = pl.program_id(0)

        @pl.when(i == 0)
        def _():
            gw_ref[...] = jnp.zeros_like(gw_ref)
            ls_ref[...] = jnp.zeros_like(ls_ref)

        ys = (yg_ref[...], ya_ref[...], ym_ref[...])
        y = sum(_dot(yv, w_ref[r0:r0 + n, :]) for yv, (r0, n) in zip(ys, parts))
        err = x_ref[...] + y - t_ref[...]
        _put_rows(ls_ref, jnp.sum(err * err, axis=0, keepdims=True), accumulate=True)
        dy = err * (1.0 / D_MODEL)
        dy_ref[...] = dy
        dyb = dy.astype(BF16)
        dyc_ref[...] = _dot_nt(dyb, w_ref[...])
        for yv, (r0, n) in zip(ys, parts):
            gw_ref[r0:r0 + n, :] += _dot_tn(yv, dyb)

    row = lambda w: pl.BlockSpec((tm, w), lambda i: (i, 0))
    const = lambda shape: pl.BlockSpec(shape, lambda i: (0, 0))
    return _call(
        body, name="out_loss", grid=(nsteps,),
        in_specs=[row(GMLP_W), row(ATTN_W), row(MEM_W), row(D_MODEL), row(D_MODEL), const((D_MODEL, D_MODEL))],
        out_specs=[row(D_MODEL), row(D_MODEL), const((D_MODEL, D_MODEL)), const((8, LANES))],
        out_shape=[jax.ShapeDtypeStruct((SEQ, D_MODEL), F32), jax.ShapeDtypeStruct((SEQ, D_MODEL), F32),
                   jax.ShapeDtypeStruct((D_MODEL, D_MODEL), F32), jax.ShapeDtypeStruct((8, LANES), F32)],
        compiler_params=_params(),
    )(yg, ya, ym, x, tgt, wo)


def _proj_bwd(x, dy, gain, wt, dg, daq, dak, dav, dag, dmq, dmg):
    tm = 512
    nsteps = SEQ // tm
    pieces = ((C_GU, 3 * GMLP_W), (C_AQ, ATTN_W), (C_AK, ATTN_W), (C_AV, ATTN_W), (C_AG, ATTN_W),
              (C_MQ, MEM_W), (C_MG, MEM_W))

    def body(x_ref, dy_ref, g_ref, wt_hbm, p0, p1, p2, p3, p4, p5, p6, gx_ref, gwt_hbm, gg_ref, wt_v, acc, wt_sem, out_sems):
        i = pl.program_id(0)
        wt_load = pltpu.make_async_copy(wt_hbm, wt_v, wt_sem)

        @pl.when(i == 0)
        def _():
            wt_load.start()
            acc[...] = jnp.zeros_like(acc)
            gg_ref[...] = jnp.zeros_like(gg_ref)

        xv = x_ref[...]
        r = lax.rsqrt(jnp.mean(xv * xv, axis=-1, keepdims=True) + EPS)
        z = xv * r
        g = g_ref[...]
        h = (z * g).astype(BF16)
        pl.when(i == 0)(wt_load.wait)
        flush = [pltpu.make_async_copy(acc.at[c0:c0 + w, :], gwt_hbm.at[c0:c0 + w, :], out_sems.at[n])
                 for n, (c0, w) in enumerate(pieces)]
        dh = jnp.zeros((tm, D_MODEL), F32)
        for n, (pref, (c0, w)) in enumerate(zip((p0, p1, p2, p3, p4, p5, p6), pieces)):
            dp = pref[...]
            dh = dh + _dot(dp, wt_v[c0:c0 + w, :])
            acc[c0:c0 + w, :] += _dot_tn(dp, h)
            pl.when(i == nsteps - 1)(flush[n].start)
        _put_rows(gg_ref, jnp.sum(dh * z, axis=0, keepdims=True), accumulate=True)
        dz = dh * g
        gx_ref[...] = dy_ref[...] + r * (dz - z * jnp.mean(dz * z, axis=-1, keepdims=True))

        @pl.when(i == nsteps - 1)
        def _():
            for cp in flush:
                cp.wait()

    row = lambda w: pl.BlockSpec((tm, w), lambda i: (i, 0))
    hbm = pl.BlockSpec(memory_space=pl.ANY)
    vec = pl.BlockSpec((1, D_MODEL), lambda i: (0, 0))
    return _call(
        body, name="proj_bwd", grid=(nsteps,),
        in_specs=[row(D_MODEL), row(D_MODEL), vec, hbm] + [row(w) for _, w in pieces],
        out_specs=[row(D_MODEL), hbm, pl.BlockSpec((8, LANES), lambda i: (0, 0))],
        out_shape=[jax.ShapeDtypeStruct((SEQ, D_MODEL), F32), jax.ShapeDtypeStruct((IN_W, D_MODEL), F32),
                   jax.ShapeDtypeStruct((8, LANES), F32)],
        scratch_shapes=[pltpu.VMEM((IN_W, D_MODEL), BF16), pltpu.VMEM((IN_W, D_MODEL), F32), pltpu.SemaphoreType.DMA,
                        pltpu.SemaphoreType.DMA((len(pieces),))],
        compiler_params=_params(),
    )(x, dy, gain, wt, dg, daq, dak, dav, dag, dmq, dmg)


AG_SEMS = 8


def _gather_stages(ins, lands, send_sems, recv_sems):
    n = len(ins)
    nrows = [a.shape[0] for a in ins]
    x, y, c = lax.axis_index("x"), lax.axis_index("y"), lax.axis_index("c")
    sib, xn, yn = (x, y, 1 - c), (1 - x, y, c), (x, 1 - y, c)
    me, cx, cy, cd = 2 * x + y, 2 * (1 - x) + y, 2 * x + (1 - y), 2 * (1 - x) + (1 - y)

    def part(a, chip, hf, quarter=None):
        rows = nrows[a] // 2
        base = chip * nrows[a] + hf * rows
        if quarter is not None:
            rows = rows // 2
            base = base + quarter * rows
        return lands[a].at[pl.ds(pl.multiple_of(base, 16), rows), :]

    def copy(a, j, ref, to):
        k = AG_SEMS * a + j
        return pltpu.make_async_remote_copy(src_ref=ref, dst_ref=ref, send_sem=send_sems.at[k],
                                            recv_sem=recv_sems.at[k], device_id=to, device_id_type=MESH)

    def own(a):
        return [copy(a, 0, part(a, me, c), xn), copy(a, 1, part(a, me, c), yn)]

    def neighbours(a):
        return [copy(a, 4, part(a, cx, c, 1), yn), copy(a, 2, part(a, cx, c), sib),
                copy(a, 5, part(a, cy, c, 0), xn), copy(a, 3, part(a, cy, c), sib)]

    def diagonal(a):
        return [copy(a, 7, part(a, cd, c, 1), sib), copy(a, 6, part(a, cd, c, 0), sib)]

    def send_own():
        for a in range(n):
            lands[a][pl.ds(pl.multiple_of(me * nrows[a], 16), nrows[a]), :] = ins[a][...].astype(BF16)
            for cp in own(a):
                cp.start()

    def pass_on_neighbours():
        for a in range(n):
            copy(a, 0, part(a, cx, c), xn).wait_recv()
            copy(a, 1, part(a, cy, c), yn).wait_recv()
            for cp in neighbours(a):
                cp.start()

    def pass_on_diagonal():
        for a in range(n):
            copy(a, 4, part(a, cd, c, 1), yn).wait_recv()
            copy(a, 5, part(a, cd, c, 0), xn).wait_recv()
            for cp in diagonal(a):
                cp.start()

    def y_complete():
        for a in range(n):
            copy(a, 3, part(a, cy, 1 - c), sib).wait_recv()

    def x_complete():
        for a in range(n):
            copy(a, 2, part(a, cx, 1 - c), sib).wait_recv()

    def diagonal_complete():
        for a in range(n):
            copy(a, 6, part(a, cd, 1 - c, 0), sib).wait_recv()
            copy(a, 7, part(a, cd, 1 - c, 1), sib).wait_recv()

    def sends_done():
        for a in range(n):
            for cp in own(a) + neighbours(a) + diagonal(a):
                cp.wait_send()

    def finish():
        y_complete()
        x_complete()
        diagonal_complete()
        sends_done()

    return (send_own, pass_on_neighbours, pass_on_diagonal, finish), (y_complete, x_complete, diagonal_complete, sends_done)


RS_SEMS = 6
RS_KINDS = (((2, 2), 1, F32), ((2, 2), 1, F32), ((2, 2), 2, BF16), ((2, 2), 2, BF16), ((2, 2), 2, F32),
            ((2,), 2, BF16), ((2,), 2, BF16), ((2,), 1, F32))


def _rs_view(g):
    return g.reshape(2, 2, 2, g.shape[0] // 8, g.shape[1])


def _rs_scratch(shapes):
    return [pltpu.VMEM(lead + (r // 8, w // split), dt) for lead, split, dt in RS_KINDS for r, w in shapes]


def _rs_stages(gs, outs, bufs, send_sems, recv_sems, local_sems, widths):
    n = len(gs)
    loc, ra, s_b, r_b, acc1, s_c, r_c, fin = (bufs[n * i:n * i + n] for i in range(len(RS_KINDS)))
    half_w = [w // 2 for w in widths]
    x, y, c = lax.axis_index("x"), lax.axis_index("y"), lax.axis_index("c")
    sib, xn, yn = (x, y, 1 - c), (1 - x, y, c), (x, 1 - y, c)

    def copy(a, j, src, dst, to):
        k = RS_SEMS * a + j
        return pltpu.make_async_remote_copy(src_ref=src, dst_ref=dst, send_sem=send_sems.at[k],
                                            recv_sem=recv_sems.at[k], device_id=to, device_id_type=MESH)

    def step_a(a):
        return [copy(a, 0, gs[a].at[:, :, 1 - c], ra[a], sib),
                pltpu.make_async_copy(gs[a].at[:, :, c], loc[a], local_sems.at[a])]

    def step_b(a):
        return copy(a, 1, s_b[a].at[0], r_b[a].at[0], xn), copy(a, 2, s_b[a].at[1], r_b[a].at[1], yn)

    def step_c(a):
        return copy(a, 3, s_c[a].at[0], r_c[a].at[0], yn), copy(a, 4, s_c[a].at[1], r_c[a].at[1], xn)

    def step_d(a, half):
        rows = fin[a].at[half]
        return copy(a, 5, rows, rows, sib)

    def start():
        for a in range(n):
            for cp in step_a(a):
                cp.start()

    def a_to_b():
        for a in range(n):
            for cp in step_a(a):
                cp.wait()
            ra[a][...] = loc[a][...] + ra[a][...]
            s_b[a][0] = ra[a][1 - x, :, :, :half_w[a]].astype(BF16)
            s_b[a][1] = ra[a][:, 1 - y, :, half_w[a]:].astype(BF16)
            for cp in step_b(a):
                cp.start()

    def b_to_c():
        for a in range(n):
            for cp in step_b(a):
                cp.wait()
            acc1[a][0] = ra[a][x, :, :, :half_w[a]] + r_b[a][0].astype(F32)
            acc1[a][1] = ra[a][:, y, :, half_w[a]:] + r_b[a][1].astype(F32)
            s_c[a][0] = acc1[a][0, 1 - y].astype(BF16)
            s_c[a][1] = acc1[a][1, 1 - x].astype(BF16)
            for cp in step_c(a):
                cp.start()

    def c_to_d():
        for a in range(n):
            for cp in step_c(a):
                cp.wait()
            fin[a][c, :, :half_w[a]] = acc1[a][0, y] + r_c[a][0].astype(F32)
            fin[a][c, :, half_w[a]:] = acc1[a][1, x] + r_c[a][1].astype(F32)
            step_d(a, c).start()

    def finish():
        to_hbm = [pltpu.make_async_copy(fin[a], outs[a], local_sems.at[a]) for a in range(n)]
        for a in range(n):
            step_d(a, 1 - c).wait_recv()
            step_d(a, c).wait_send()
            to_hbm[a].start()
        for cp in to_hbm:
            cp.wait()

    return start, a_to_b, b_to_c, c_to_d, finish


def _reduce_grads(gwt, g_ws, tiny):
    cw = gwt.shape[1] // RS_CHUNKS
    chunk_shape = (gwt.shape[0], cw)

    def body(g0, ws_in, tiny_in, *rest):
        outs, o_ws, o_tiny = rest[:RS_CHUNKS], rest[RS_CHUNKS], rest[RS_CHUNKS + 1]
        rest = rest[RS_CHUNKS + 2:]
        nb = len(RS_KINDS) * RS_CHUNKS
        sm, sa, sb, sc, acc_s, send_sems, recv_sems, local_sems = rest[nb:]
        blocks = [g0.at[:, :, :, :, pl.ds(j * cw, cw)] for j in range(RS_CHUNKS)]
        start, a_to_b, b_to_c, c_to_d, finish = _rs_stages(blocks, outs, rest[:nb], send_sems, recv_sems, local_sems,
                                                           [cw] * RS_CHUNKS)
        n_ws = ws_in.shape[0]
        sm[0:n_ws, :] = ws_in[...]
        sm[n_ws:, :] = tiny_in[...]
        x, y, c = lax.axis_index("x"), lax.axis_index("y"), lax.axis_index("c")

        def small(j, src, dst, to):
            k = RS_SEMS * RS_CHUNKS + j
            return pltpu.make_async_remote_copy(src_ref=src, dst_ref=dst, send_sem=send_sems.at[k],
                                                recv_sem=recv_sems.at[k], device_id=to, device_id_type=MESH)

        along_c, along_x, along_y = (small(0, sm, sa, (x, y, 1 - c)), small(1, acc_s, sb, (1 - x, y, c)),
                                     small(2, sb, sc, (x, 1 - y, c)))
        start()
        along_c.start()
        a_to_b()
        along_c.wait()
        acc_s[...] = sm[...] + sa[...]
        along_x.start()
        b_to_c()
        along_x.wait()
        sb[...] = acc_s[...] + sb[...]
        along_y.start()
        c_to_d()
        along_y.wait()
        o_ws[...] = sb[0:n_ws, :] + sc[0:n_ws, :]
        o_tiny[...] = sb[n_ws:, :] + sc[n_ws:, :]
        finish()

    vm = pl.BlockSpec(memory_space=pltpu.VMEM)
    hbm = pl.BlockSpec(memory_space=pl.ANY)
    small_shape = (g_ws.shape[0] + tiny.shape[0], LANES)
    scratch = _rs_scratch([chunk_shape] * RS_CHUNKS) + [pltpu.VMEM(small_shape, F32) for _ in range(5)]
    nsem = RS_SEMS * RS_CHUNKS + 3
    scratch += [pltpu.SemaphoreType.DMA((nsem,)), pltpu.SemaphoreType.DMA((nsem,)), pltpu.SemaphoreType.DMA((RS_CHUNKS,))]
    return _call(
        body, name="reduce_grads",
        out_shape=[jax.ShapeDtypeStruct((2, gwt.shape[0] // 8, cw), F32)] * RS_CHUNKS
        + [jax.ShapeDtypeStruct(g_ws.shape, F32), jax.ShapeDtypeStruct(tiny.shape, F32)],
        in_specs=[hbm, vm, vm],
        out_specs=[hbm] * RS_CHUNKS + [vm, vm],
        scratch_shapes=scratch,
        compiler_params=_params(),
    )(_rs_view(gwt), g_ws, tiny)


def _adam_update(w, g, m, v):
    nm = ADAM_B1 * m + (1.0 - ADAM_B1) * g
    nv = ADAM_B2 * v + (1.0 - ADAM_B2) * (g * g)
    m_hat = nm / (1.0 - ADAM_B1 ** ADAM_STEP)
    v_hat = nv / (1.0 - ADAM_B2 ** ADAM_STEP)
    return -ADAM_LR * (m_hat / (jnp.sqrt(v_hat) + ADAM_EPS) + ADAM_WD * w), nm, nv


def _adamw(w, g, m, v):
    rows, cols = w.shape
    tm = max(t for t in range(8, 257, 8) if rows % t == 0)
    parts = tuple(g) if isinstance(g, (tuple, list)) else (g,)
    n = len(parts)

    def body(w_ref, m_ref, v_ref, *refs):
        gv = jnp.concatenate([r[...] for r in refs[:n]], axis=1)
        d_ref, nm_ref, nv_ref = refs[n:n + 3]
        d_ref[...], nm_ref[...], nv_ref[...] = _adam_update(w_ref[...], gv, m_ref[...], v_ref[...])
        if n > 1:
            refs[n + 3][...] = gv

    blk = pl.BlockSpec((tm, cols), lambda i: (i, 0))
    nout = 3 if n == 1 else 4
    res = _call(
        body, name="adamw", grid=(rows // tm,),
        in_specs=[blk] * 3 + [pl.BlockSpec((tm, p.shape[1]), lambda i: (i, 0)) for p in parts], out_specs=[blk] * nout,
        out_shape=[jax.ShapeDtypeStruct((rows, cols), F32)] * nout,
        compiler_params=_params(),
    )(w, m, v, *parts)
    return (parts[0] if n == 1 else res[3], *res[:3])


def _adamw_tiny(tiny, weights, ms, vs):
    shapes = [w.shape for w in weights]
    n = len(weights)

    def grad_of(t_ref, k, shape):
        base = 8 * k
        if shape[1] > LANES:
            return [t_ref[base + j:base + j + 1, :] for j in range(shape[1] // LANES)]
        return [t_ref[base:base + shape[0], 0:shape[1]]]

    def body(t_ref, *refs):
        w_refs, m_refs, v_refs = refs[:n], refs[n:2 * n], refs[2 * n:3 * n]
        loss_ref, outs = refs[3 * n], refs[3 * n + 1:]
        loss_ref[...] = (0.5 / D_MODEL) * jnp.sum(t_ref[8 * n:8 * n + 8, :], keepdims=True)
        for k, shape in enumerate(shapes):
            g_ref, d_ref, nm_ref, nv_ref = outs[4 * k:4 * k + 4]
            for j, g in enumerate(grad_of(t_ref, k, shape)):
                cols = slice(j * LANES, (j + 1) * LANES) if shape[1] > LANES else slice(None)
                g_ref[:, cols] = g
                d_ref[:, cols], nm_ref[:, cols], nv_ref[:, cols] = _adam_update(
                    w_refs[k][:, cols], g, m_refs[k][:, cols], v_refs[k][:, cols])

    out_shape = [jax.ShapeDtypeStruct((1, 1), F32)]
    for shape in shapes:
        out_shape += [jax.ShapeDtypeStruct(shape, F32)] * 4
    return _call(body, name="adamw_tiny", out_shape=out_shape, compiler_params=_params())(tiny, *weights, *ms, *vs)


def _local_grads(x, mem, tgt, norm_gain, wt_sh, gmlp_v_gain, gmlp_w_s, gmlp_b, attn_q_gain, attn_k_gain,
                 mem_norm_gain, wkv_sh, mem_q_gain, mem_k_gain, wo_sh):
    vg = gmlp_v_gain.reshape(1, GMLP_W)
    bias_full = jnp.repeat(gmlp_b.T, HEAD_DIM, axis=1)
    gq2, gk2 = jnp.tile(attn_q_gain, (1, 2)), jnp.tile(attn_k_gain, (1, 2))
    qg4, kg4 = jnp.tile(mem_q_gain, (1, 4)), jnp.tile(mem_k_gain, (1, 4))

    proj, wt = _gather_proj(x, norm_gain, wt_sh)
    yg = _gmlp_fwd(proj, vg, gmlp_w_s, bias_full)
    o, lse, ya, wkv, wo = _attn_fwd(proj, gq2, gk2, wkv_sh, wo_sh)
    kv, hm = _mem_kv(mem, mem_norm_gain, wkv)
    om, ym = _mem_fwd(proj, kv, qg4, kg4)
    dy, dyc, g_wo, err2 = _out_loss(yg, ya, ym, x, tgt, wo)
    dmq, dmg, g_mq, g_mk, g_wkv, g_mng = _mem_bwd(proj, om, dyc, kv, hm, mem, mem_norm_gain, wkv, qg4, kg4)
    daq, dak, dav, dag, g_aq, g_ak, g_wkv_sh, g_wo_sh = _attn_bwd(proj, o, lse, dyc, gq2, gk2, g_wkv, g_wo)
    dg, g_ws, g_b, g_vg = _gmlp_bwd(proj, dyc, vg, gmlp_w_s, bias_full)
    gx, g_wt, g_ng = _proj_bwd(x, dy, norm_gain, wt, dg, daq, dak, dav, dag, dmq, dmg)

    tiny = jnp.concatenate([g_ng, g_vg, g_b, g_aq, g_ak, g_mng, g_mq, g_mk, err2], axis=0)
    return gx, g_wt, g_wkv_sh, g_wo_sh, g_ws.reshape(4 * CHUNK, CHUNK), tiny


def kernel(x, mem, norm_gain, w_in, gmlp_v_gain, gmlp_w_s, gmlp_b, attn_q_gain, attn_k_gain, mem_norm_gain, w_mem_kv, mem_q_gain, mem_k_gain, w_out, loss_target, m_norm_gain, m_w_in, m_gmlp_v_gain, m_gmlp_w_s, m_gmlp_b, m_attn_q_gain, m_attn_k_gain, m_mem_norm_gain, m_w_mem_kv, m_mem_q_gain, m_mem_k_gain, m_w_out, v_norm_gain, v_w_in, v_gmlp_v_gain, v_gmlp_w_s, v_gmlp_b, v_attn_q_gain, v_attn_k_gain, v_mem_norm_gain, v_w_mem_kv, v_mem_q_gain, v_mem_k_gain, v_w_out):
    gx, g_wt, g_wkv_sh, g_wo_sh, g_ws, tiny = _local_grads(
        x[0], mem[0], loss_target[0], norm_gain, w_in[0].T, gmlp_v_gain[0], gmlp_w_s[0], gmlp_b[0],
        attn_q_gain, attn_k_gain, mem_norm_gain, w_mem_kv[0], mem_q_gain, mem_k_gain, w_out[0])
    *g_wt_sh, g_ws, tiny = _reduce_grads(g_wt, g_ws, tiny)
    chip_block = lambda g: g.reshape(2 * g.shape[1], g.shape[2])
    g_wt_sh = tuple(chip_block(g) for g in g_wt_sh)
    g_wkv_sh, g_wo_sh = chip_block(g_wkv_sh), chip_block(g_wo_sh)

    ws = (norm_gain, w_in, gmlp_v_gain, gmlp_w_s, gmlp_b, attn_q_gain, attn_k_gain, mem_norm_gain, w_mem_kv,
          mem_q_gain, mem_k_gain, w_out)
    ms = (m_norm_gain, m_w_in, m_gmlp_v_gain, m_gmlp_w_s, m_gmlp_b, m_attn_q_gain, m_attn_k_gain, m_mem_norm_gain,
          m_w_mem_kv, m_mem_q_gain, m_mem_k_gain, m_w_out)
    vs = (v_norm_gain, v_w_in, v_gmlp_v_gain, v_gmlp_w_s, v_gmlp_b, v_attn_q_gain, v_attn_k_gain, v_mem_norm_gain,
          v_w_mem_kv, v_mem_q_gain, v_mem_k_gain, v_w_out)
    form = {1: lambda a: a[0].T, 3: lambda a: a.reshape(4 * CHUNK, CHUNK), 2: lambda a: a[0], 4: lambda a: a[0],
            8: lambda a: a[0], 11: lambda a: a[0]}
    back = {1: lambda a: a.T[None], 3: lambda a: a.reshape(1, 4, CHUNK, CHUNK), 2: lambda a: a[None],
            4: lambda a: a[None], 8: lambda a: a[None], 11: lambda a: a[None]}
    fwd = lambda t, i: form.get(i, lambda a: a)(t[i])
    out = {}
    for i, g in ((1, g_wt_sh), (3, g_ws), (8, g_wkv_sh), (11, g_wo_sh)):
        out[i] = _adamw(fwd(ws, i), g, fwd(ms, i), fwd(vs, i))
    res = _adamw_tiny(tiny, [fwd(ws, i) for i in TINY_ORDER], [fwd(ms, i) for i in TINY_ORDER],
                      [fwd(vs, i) for i in TINY_ORDER])
    for k, i in enumerate(TINY_ORDER):
        out[i] = res[1 + 4 * k:5 + 4 * k]
    leaves = [[back.get(i, lambda a: a)(out[i][j]) for i in range(12)] for j in range(4)]
    return (res[0].reshape(()), gx[None], *leaves[0], *leaves[1], *leaves[2], *leaves[3])
```

```python
import math

import jax
import jax.numpy as jnp
from jax import lax
from jax.experimental import pallas as pl
from jax.experimental.pallas import tpu as pltpu

F32 = jnp.float32
BF16 = jnp.bfloat16

SEQ = 4096
D_MODEL = 1024
HEAD_DIM = 64
LANES = 128
CHUNK = 128
GMLP_W, ATTN_W, MEM_W = 256, 512, 256
IN_W = 3 * GMLP_W + 4 * ATTN_W + 2 * MEM_W
MEM_LEN = 256
DILATIONS = (16, 4, 1)
EPS = 1e-6
QK_SCALE = 1.0 / math.sqrt(HEAD_DIM)
C_GU, C_GV, C_GG, C_AQ, C_AK, C_AV, C_AG, C_MQ, C_MG = 0, 256, 512, 768, 1280, 1792, 2304, 2816, 3072

ADAM_LR, ADAM_B1, ADAM_B2, ADAM_EPS, ADAM_WD, ADAM_STEP = 0.001, 0.9, 0.999, 1e-08, 0.01, 10

VMEM_LIMIT = 48 * 1024 * 1024
RS_CHUNKS = 4
ATTN_UNROLL = 4
MESH = pl.DeviceIdType.MESH

TINY_ORDER = (0, 2, 4, 5, 6, 7, 9, 10)


def _call(body, **kw):
    return pl.pallas_call(body, **kw)


def _params(**kw):
    return pltpu.CompilerParams(vmem_limit_bytes=VMEM_LIMIT, **kw)


def _dot(a, b):
    return jnp.dot(a, b, preferred_element_type=F32)


def _dot_nt(a, b):
    return lax.dot_general(a, b, (((1,), (1,)), ((), ())), preferred_element_type=F32)


def _dot_tn(a, b):
    return lax.dot_general(a, b, (((0,), (0,)), ((), ())), preferred_element_type=F32)


def _head_blockdiag():
    r = lax.shift_right_logical(lax.broadcasted_iota(jnp.int32, (LANES, LANES), 0), 6)
    c = lax.shift_right_logical(lax.broadcasted_iota(jnp.int32, (LANES, LANES), 1), 6)
    return jnp.where(r == c, 1.0, 0.0).astype(BF16)


def _headsum(v, bd):
    hi = v.astype(BF16)
    lo = (v - hi.astype(F32)).astype(BF16)
    return _dot(hi, bd) + _dot(lo, bd)


def _lo_mask(rows):
    return lax.broadcasted_iota(jnp.int32, (rows, LANES), 1) < HEAD_DIM


def _sigmoid(x):
    return 1.0 / (1.0 + jnp.exp(-x))


def _fold_heads(v):
    return v + pltpu.roll(v, HEAD_DIM, 1)


def _put_rows(ref, vec, accumulate=False):
    for j in range(vec.shape[1] // LANES):
        piece = vec[:, j * LANES:(j + 1) * LANES]
        ref[j:j + 1, :] = ref[j:j + 1, :] + piece if accumulate else piece


def _gather_proj(x, gain, wt_sh):
    tm = 1024
    nrow = SEQ // tm
    widths = (768, 896, 768, 896)
    nunits = len(widths)
    pair = 2 * wt_sh.shape[0]
    assert pair % LANES == 0 and sum(widths[:2]) == pair

    def body(x_ref, g_ref, wt_sh_ref, proj_hbm, wt_hbm, h_scr, land, res, send_sems, recv_sems, out_sems, copy_sem):
        u, i = pl.program_id(0), pl.program_id(1)
        cx_, cy_ = lax.axis_index("x"), lax.axis_index("y")
        (send_own, pass_on_neighbours, pass_on_diagonal, _), (y_complete, x_complete, diagonal_complete, sends_done) = (
            _gather_stages((wt_sh_ref,), (land,), send_sems, recv_sems))
        first = lambda k: (u == k) & (i == 0)
        last = (u == nunits - 1) & (i == nrow - 1)
        to_hbm = pltpu.make_async_copy(land, wt_hbm, copy_sem)

        pl.when(first(0))(send_own)

        @pl.when(u == 0)
        def _():
            xv = x_ref[...]
            ms = jnp.mean(xv * xv, axis=-1, keepdims=True)
            h_scr[pl.ds(pl.multiple_of(i * tm, tm), tm), :] = (xv * lax.rsqrt(ms + EPS) * g_ref[...]).astype(BF16)

        @pl.when(first(1))
        def _():
            pass_on_neighbours()
            y_complete()

        @pl.when(first(2))
        def _():
            x_complete()
            pass_on_diagonal()

        @pl.when(first(3))
        def _():
            diagonal_complete()
            to_hbm.start()

        mine, other = pair * cx_, pair * (1 - cx_)
        col0 = (mine + 896 * cy_, mine + 768 * (1 - cy_), other + 896 * cy_, other + 768 * (1 - cy_))
        slot = i % 2
        rows = pl.ds(pl.multiple_of(i * tm, tm), tm)

        def writeback(k, rows_):
            c0 = pl.multiple_of(col0[k], LANES)
            return pltpu.make_async_copy(res.at[slot, :, pl.ds(0, widths[k])], proj_hbm.at[rows_, pl.ds(c0, widths[k])],
                                         out_sems.at[slot])

        for k in range(nunits):
            @pl.when(u == k)
            def _(k=k):
                pl.when(i >= 2)(writeback(k, rows).wait)
                if k > 0:
                    pl.when(i < 2)(writeback(k - 1, rows).wait)
                w_rows = land[pl.ds(pl.multiple_of(col0[k], LANES), widths[k]), :]
                res[slot, :, 0:widths[k]] = _dot_nt(h_scr[rows, :], w_rows)
                writeback(k, rows).start()

        @pl.when(last)
        def _():
            sends_done()
            to_hbm.wait()
            for s in range(2):
                pltpu.make_async_copy(res.at[s, :, pl.ds(0, widths[-1])], proj_hbm.at[rows, pl.ds(0, widths[-1])], out_sems.at[s]).wait()

    full = jax.ShapeDtypeStruct((4 * wt_sh.shape[0], wt_sh.shape[1]), BF16)
    hbm = pl.BlockSpec(memory_space=pl.ANY)
    return _call(
        body, name="gather_proj", grid=(nunits, nrow),
        in_specs=[pl.BlockSpec((tm, D_MODEL), lambda u, i: (jnp.where(u == 0, i, nrow - 1), 0)),
                  pl.BlockSpec((1, D_MODEL), lambda u, i: (0, 0)), pl.BlockSpec(wt_sh.shape, lambda u, i: (0, 0))],
        out_specs=[hbm, hbm],
        out_shape=[jax.ShapeDtypeStruct((SEQ, IN_W), F32), full],
        scratch_shapes=[pltpu.VMEM((SEQ, D_MODEL), BF16), pltpu.VMEM(full.shape, BF16), pltpu.VMEM((2, tm, max(widths)), F32),
                        pltpu.SemaphoreType.DMA((AG_SEMS,)), pltpu.SemaphoreType.DMA((AG_SEMS,)),
                        pltpu.SemaphoreType.DMA((2,)), pltpu.SemaphoreType.DMA],
        compiler_params=_params(),
    )(x, gain, wt_sh)


def _gmlp_weights(w_ref):
    ti = lax.broadcasted_iota(jnp.int32, (CHUNK, CHUNK), 0)
    si = lax.broadcasted_iota(jnp.int32, (CHUNK, CHUNK), 1)
    tril = si <= ti
    return tril, [jnp.where(tril, w_ref[h], 0.0).astype(BF16) for h in range(4)]


def _gmlp_fwd(proj, vgain, w_s, bias_full):
    tm = 1024

    def body(p_ref, vg_ref, w_ref, b_ref, y_ref):
        bd = _head_blockdiag()
        lo = _lo_mask(CHUNK)
        _, wm = _gmlp_weights(w_ref)
        units = [(pl.ds(c * CHUNK, CHUNK), p) for c in range(tm // CHUNK) for p in range(2)]
        col = lambda c0, p: slice(c0 + p * LANES, c0 + (p + 1) * LANES)
        vs = [p_ref[rows, col(C_GV, p)] for rows, p in units]
        rs = [lax.rsqrt(_headsum(v * v, bd) * (1.0 / HEAD_DIM) + EPS) for v in vs]
        vns = [(v * r * vg_ref[:, col(0, p)]).astype(BF16) for v, r, (_, p) in zip(vs, rs, units)]
        sps = [jnp.where(lo, _dot(wm[2 * p], vn), _dot(wm[2 * p + 1], vn)) + b_ref[:, col(0, p)] for vn, (_, p) in zip(vns, units)]
        for sp, (rows, p) in zip(sps, units):
            gt = p_ref[rows, col(C_GG, p)]
            y_ref[rows, col(0, p)] = (p_ref[rows, col(C_GU, p)] * sp * (gt * _sigmoid(gt))).astype(BF16)

    return _call(
        body, name="gmlp_fwd", grid=(SEQ // tm,),
        in_specs=[pl.BlockSpec((tm, 3 * GMLP_W), lambda i: (i, 0)),
                  pl.BlockSpec((1, GMLP_W), lambda i: (0, 0)),
                  pl.BlockSpec((4, CHUNK, CHUNK), lambda i: (0, 0, 0)),
                  pl.BlockSpec((CHUNK, GMLP_W), lambda i: (0, 0))],
        out_specs=pl.BlockSpec((tm, GMLP_W), lambda i: (i, 0)),
        out_shape=jax.ShapeDtypeStruct((SEQ, GMLP_W), BF16),
        compiler_params=_params(),
    )(proj, vgain, w_s, bias_full)


def _gmlp_bwd(proj, dyc, vgain, w_s, bias_full):
    tm = 1024
    nsteps = SEQ // tm

    def body(p_ref, dy_ref, vg_ref, w_ref, b_ref, dg_ref, gw_ref, gb_ref, gv_ref):
        i = pl.program_id(0)
        bd = _head_blockdiag()
        lo = _lo_mask(CHUNK)
        tril, wm = _gmlp_weights(w_ref)
        ri = lax.broadcasted_iota(jnp.int32, (16, LANES), 0)
        li = lax.broadcasted_iota(jnp.int32, (16, LANES), 1)
        head_rows = [jnp.where(((ri == 2 * p) & (li < HEAD_DIM)) | ((ri == 2 * p + 1) & (li >= HEAD_DIM)), 1.0, 0.0).astype(BF16)
                     for p in range(2)]

        @pl.when(i == 0)
        def _():
            gw_ref[...] = jnp.zeros_like(gw_ref)
            gb_ref[...] = jnp.zeros_like(gb_ref)
            gv_ref[...] = jnp.zeros_like(gv_ref)

        units = [(pl.ds(c * CHUNK, CHUNK), p) for c in range(tm // CHUNK) for p in range(2)]
        col = lambda c0, p: slice(c0 + p * LANES, c0 + (p + 1) * LANES)
        vs = [p_ref[rows, col(C_GV, p)] for rows, p in units]
        rs = [lax.rsqrt(_headsum(v * v, bd) * (1.0 / HEAD_DIM) + EPS) for v in vs]
        zs = [v * r for v, r in zip(vs, rs)]
        vns = [(z * vg_ref[:, col(0, p)]).astype(BF16) for z, (_, p) in zip(zs, units)]
        sps = [jnp.where(lo, _dot(wm[2 * p], vn), _dot(wm[2 * p + 1], vn)) + b_ref[:, col(0, p)] for vn, (_, p) in zip(vns, units)]
        dsps = []
        for sp, (rows, p) in zip(sps, units):
            u = p_ref[rows, col(C_GU, p)]
            gt = p_ref[rows, col(C_GG, p)]
            dy = dy_ref[rows, col(0, p)]
            sg = _sigmoid(gt)
            sl = gt * sg
            dg_ref[rows, col(C_GU, p)] = (dy * sp * sl).astype(BF16)
            dg_ref[rows, col(C_GG, p)] = (dy * u * sp * (sg * (1.0 + gt * (1.0 - sg)))).astype(BF16)
            dsps.append(dy * u * sl)
        dspbs = [dsp.astype(BF16) for dsp in dsps]
        dvns = [jnp.where(lo, _dot_tn(wm[2 * p], dspb), _dot_tn(wm[2 * p + 1], dspb)) for dspb, (_, p) in zip(dspbs, units)]
        gws = [(_dot_nt(jnp.where(lo, dsp, 0.0).astype(BF16), vn), _dot_nt(jnp.where(lo, 0.0, dsp).astype(BF16), vn))
               for dsp, vn in zip(dsps, vns)]
        gbs = [(_dot_nt(head_rows[p], dspb) + _dot_nt(head_rows[p], (dsp - dspb.astype(F32)).astype(BF16)))[0:8]
               for dsp, dspb, (_, p) in zip(dsps, dspbs, units)]
        for p in range(2):
            mine = [n for n, (_, q) in enumerate(units) if q == p]
            gw_ref[2 * p] += sum(gws[n][0] for n in mine)
            gw_ref[2 * p + 1] += sum(gws[n][1] for n in mine)
            gvp = sum(jnp.sum(dvns[n] * zs[n], axis=0, keepdims=True) for n in mine)
            gv_ref[2 * p:2 * p + 1, :] += gvp
            gv_ref[2 * p + 1:2 * p + 2, :] += pltpu.roll(gvp, HEAD_DIM, 1)
        gb_ref[...] += sum(gbs)
        for dvn, z, r, (rows, p) in zip(dvns, zs, rs, units):
            dz = dvn * vg_ref[:, col(0, p)]
            dg_ref[rows, col(C_GV, p)] = (r * (dz - z * (_headsum(dz * z, bd) * (1.0 / HEAD_DIM)))).astype(BF16)

        @pl.when(i == nsteps - 1)
        def _():
            for h in range(4):
                gw_ref[h] = jnp.where(tril, gw_ref[h], 0.0)

    return _call(
        body, name="gmlp_bwd", grid=(nsteps,),
        in_specs=[pl.BlockSpec((tm, 3 * GMLP_W), lambda i: (i, 0)),
                  pl.BlockSpec((tm, GMLP_W), lambda i: (i, 0)),
                  pl.BlockSpec((1, GMLP_W), lambda i: (0, 0)),
                  pl.BlockSpec((4, CHUNK, CHUNK), lambda i: (0, 0, 0)),
                  pl.BlockSpec((CHUNK, GMLP_W), lambda i: (0, 0))],
        out_specs=[pl.BlockSpec((tm, 3 * GMLP_W), lambda i: (i, 0)),
                   pl.BlockSpec((4, CHUNK, CHUNK), lambda i: (0, 0, 0)),
                   pl.BlockSpec((8, LANES), lambda i: (0, 0)),
                   pl.BlockSpec((8, LANES), lambda i: (0, 0))],
        out_shape=[jax.ShapeDtypeStruct((SEQ, 3 * GMLP_W), BF16),
                   jax.ShapeDtypeStruct((4, CHUNK, CHUNK), F32),
                   jax.ShapeDtypeStruct((8, LANES), F32),
                   jax.ShapeDtypeStruct((8, LANES), F32)],
        compiler_params=_params(),
    )(proj, dyc, vgain, w_s, bias_full)


def _band_masks():
    qi = lax.broadcasted_iota(jnp.int32, (CHUNK, 2 * CHUNK), 0)
    kj = lax.broadcasted_iota(jnp.int32, (CHUNK, 2 * CHUNK), 1)
    valid2 = ((kj < CHUNK) & (kj >= qi)) | ((kj >= CHUNK) & (kj - CHUNK <= qi))
    q1 = lax.broadcasted_iota(jnp.int32, (CHUNK, CHUNK), 0)
    k1 = lax.broadcasted_iota(jnp.int32, (CHUNK, CHUNK), 1)
    return k1 <= q1, valid2


def _stack_heads(v, lo):
    return jnp.concatenate([jnp.where(lo, v, 0.0), jnp.where(lo, 0.0, v)], axis=0).astype(BF16)


def _rows_of(ref, start, d):
    if d == 1:
        return ref.at[pl.ds(start if isinstance(start, int) else pl.multiple_of(start, CHUNK), CHUNK), :]
    return ref.at[pl.ds(start, CHUNK, stride=d), :]


def _unrolled(lo, hi, unroll, run):
    groups = (hi - lo) // unroll
    if groups:
        def body(g, carry):
            run([lo + g * unroll + t for t in range(unroll)])
            return carry

        lax.fori_loop(0, groups, body, 0)
    if lo + groups * unroll < hi:
        run(range(lo + groups * unroll, hi))


def _for_blocks(d, group_fn, unroll):
    nblk = SEQ // CHUNK
    sh = d.bit_length() - 1

    def first(j):
        return (j * CHUNK if d == 1 else j, None)

    def rest(j):
        start = (j & (d - 1)) + (j >> sh) * (CHUNK * d)
        return (start, start - CHUNK * d)

    _unrolled(0, d, unroll, lambda js: group_fn(d, [first(j) for j in js]))
    _unrolled(d, nblk, unroll, lambda js: group_fn(d, [rest(j) for j in js]))


def _attn_fwd(proj, gq2, gk2, *ride_along):
    tn_norm, tn = 2048, 512
    npairs = ATTN_W // LANES
    nride = len(ride_along)

    def body(q_ref, k_ref, v_ref, g_ref, gq_ref, gk_ref, *rest):
        shards, rest = rest[:nride], rest[nride:]
        o_ref, l_ref, ya_ref = rest[:3]
        gathered, rest = rest[3:3 + nride], rest[3 + nride:]
        qn_ref, kn_ref = rest[:2]
        lands, (send_sems, recv_sems, copy_sems) = rest[2:2 + nride], rest[2 + nride:]
        pair = pl.program_id(0)
        ride = _gather_stages(shards, lands, send_sems, recv_sems)[0]
        for step in range(npairs):
            pl.when(pair == step)(ride[step])
        bd = _head_blockdiag()
        lo = _lo_mask(CHUNK)
        valid1, valid2 = _band_masks()

        def norm(t, carry):
            rows = pl.ds(pl.multiple_of(t * tn_norm, tn_norm), tn_norm)
            q, k = q_ref[rows, :], k_ref[rows, :]
            ssq = [_headsum(a * a, bd) for a in (q, k)]
            qn_ref[rows, :] = q * lax.rsqrt(ssq[0] * (1.0 / HEAD_DIM) + EPS) * (gq_ref[...] * QK_SCALE)
            kn_ref[rows, :] = k * lax.rsqrt(ssq[1] * (1.0 / HEAD_DIM) + EPS) * gk_ref[...]
            return carry

        lax.fori_loop(0, SEQ // tn_norm, norm, 0)

        def load_kv(ref, d, start, prev):
            own = _rows_of(ref, start, d)[...]
            if prev is None:
                return own.astype(BF16)
            return jnp.concatenate([_rows_of(ref, prev, d)[...], own], axis=0).astype(BF16)

        def group(d, blocks):
            valid = valid1 if blocks[0][1] is None else valid2
            valid = jnp.concatenate([valid, valid], axis=0)
            qs = [_rows_of(qn_ref, start, d)[...] for start, _ in blocks]
            ks = [load_kv(kn_ref, d, start, prev) for start, prev in blocks]
            vs = [load_kv(v_ref, d, start, prev) for start, prev in blocks]
            ss = [_dot_nt(_stack_heads(q, lo), k) for q, k in zip(qs, ks)]
            ms, ps, ls = [], [], []
            for s in ss:
                s = jnp.where(valid, s, -jnp.inf)
                m = jnp.max(s, axis=-1, keepdims=True)
                p = jnp.exp(s - m)
                ms.append(m)
                ls.append(jnp.sum(p, axis=-1, keepdims=True))
                ps.append(p.astype(BF16))
            os_ = [_dot(p, v) for p, v in zip(ps, vs)]
            for b, (start, _) in enumerate(blocks):
                heads = lambda v: jnp.where(lo, v[:CHUNK], v[CHUNK:])
                lsum = heads(ls[b])
                ob = heads(os_[b]) * (1.0 / lsum)
                lb = heads(ms[b]) + jnp.log(lsum)
                o_rows = _rows_of(o_ref, start, d)
                l_rows = _rows_of(l_ref, start, d)
                if d != DILATIONS[0]:
                    lold = l_rows[...]
                    mx = jnp.maximum(lold, lb)
                    ea = jnp.exp(lold - mx)
                    eb = jnp.exp(lb - mx)
                    inv = 1.0 / (ea + eb)
                    ob = o_rows[...] * (ea * inv) + ob * (eb * inv)
                    lb = mx + jnp.log(ea + eb)
                o_rows[...] = ob
                l_rows[...] = lb

        for d in DILATIONS:
            _for_blocks(d, group, ATTN_UNROLL)

        def fin(t, carry):
            rows = pl.ds(pl.multiple_of(t * tn, tn), tn)
            g = g_ref[rows, :]
            ya_ref[rows, :] = (o_ref[rows, :] * (g * _sigmoid(g))).astype(BF16)
            return carry

        lax.fori_loop(0, SEQ // tn, fin, 0)

        @pl.when(pair == npairs - 1)
        def _():
            to_hbm = [pltpu.make_async_copy(land, out, copy_sems.at[n]) for n, (land, out) in enumerate(zip(lands, gathered))]
            for cp in to_hbm:
                cp.start()
            for cp in to_hbm:
                cp.wait()

    col = lambda c0: pl.BlockSpec((SEQ, LANES), lambda p: (0, c0 // LANES + p))
    vec = pl.BlockSpec((1, LANES), lambda p: (0, 0))
    out = pl.BlockSpec((SEQ, LANES), lambda p: (0, p))
    full = [jax.ShapeDtypeStruct((4 * a.shape[0], a.shape[1]), BF16) for a in ride_along]
    return _call(
        body, name="attn_fwd", grid=(npairs,),
        in_specs=[col(C_AQ), col(C_AK), col(C_AV), col(C_AG), vec, vec]
        + [pl.BlockSpec(a.shape, lambda p: (0, 0)) for a in ride_along],
        out_specs=[out, out, out] + [pl.BlockSpec(memory_space=pl.ANY)] * nride,
        out_shape=[jax.ShapeDtypeStruct((SEQ, ATTN_W), F32), jax.ShapeDtypeStruct((SEQ, ATTN_W), F32),
                   jax.ShapeDtypeStruct((SEQ, ATTN_W), BF16)] + full,
        scratch_shapes=[pltpu.VMEM((SEQ, LANES), F32), pltpu.VMEM((SEQ, LANES), F32)]
        + [pltpu.VMEM(s.shape, BF16) for s in full]
        + [pltpu.SemaphoreType.DMA((AG_SEMS * nride,)), pltpu.SemaphoreType.DMA((AG_SEMS * nride,)),
           pltpu.SemaphoreType.DMA((nride,))],
        compiler_params=_params(),
    )(proj, proj, proj, proj, gq2, gk2, *ride_along)


def _attn_bwd(proj, o, lse, dyc, gq2, gk2, *ride_along):
    tn = 2048
    npairs = ATTN_W // LANES
    nride = len(ride_along)
    nbufs = nride * len(RS_KINDS)

    def body(proj_hbm, o_hbm, l_hbm, dyc_hbm, gq_ref, gk_ref, *rest):
        ride_in, rest = rest[:nride], rest[nride:]
        dq_ref, dk_ref, dv_ref, dgt_ref, gqg_ref, gkg_ref = rest[:6]
        ride_out, rest = rest[6:6 + nride], rest[6 + nride:]
        qb_, kb_, vb_, gb_, ob_, lb_, yb_, dkb_, dvb_, sems = rest[:10]
        rs_bufs, (send_sems, recv_sems, local_sems) = rest[10:10 + nbufs], rest[10 + nbufs:]
        rs_stage = _rs_stages(ride_in, ride_out, rs_bufs, send_sems, recv_sems, local_sems, [g.shape[1] for g in ride_along])
        pair = pl.program_id(0)
        for step in range(npairs):
            pl.when(pair == step)(rs_stage[step])
        bd = _head_blockdiag()
        lo = _lo_mask(CHUNK)
        lo2 = lax.broadcasted_iota(jnp.int32, (2 * CHUNK, LANES), 1) < HEAD_DIM
        valid1, valid2 = _band_masks()
        gqs = gq_ref[...] * QK_SCALE
        gk = gk_ref[...]

        def pcol(c0, of=None):
            return acol(proj_hbm, c0, of)

        def acol(hbm, c0=0, of=None):
            of = pair if of is None else of
            return hbm.at[:, pl.ds(pl.multiple_of(c0 + of * LANES, LANES), LANES)]

        def input_loads(of):
            return [pltpu.make_async_copy(src, dst, sems.at[n]) for n, (src, dst) in enumerate((
                (pcol(C_AQ, of), qb_), (pcol(C_AK, of), kb_), (pcol(C_AG, of), gb_), (acol(o_hbm, 0, of), ob_),
                (acol(dyc_hbm, GMLP_W, of), yb_), (pcol(C_AV, of), vb_), (acol(l_hbm, 0, of), lb_)))]

        early = (0, 1, 3, 4)
        loads = input_loads(pair)
        for n, cp in enumerate(loads):
            if n in early:
                pl.when(pair == 0)(cp.start)
            else:
                cp.start()

        @pl.when(pair == 0)
        def _():
            gqg_ref[...] = jnp.zeros_like(gqg_ref)
            gkg_ref[...] = jnp.zeros_like(gkg_ref)

        def pre_qk(t, carry):
            rows = pl.ds(pl.multiple_of(t * tn, tn), tn)
            q, k = qb_[rows, :], kb_[rows, :]
            ssq = [_headsum(a * a, bd) for a in (q, k)]
            qb_[rows, :] = q * lax.rsqrt(ssq[0] * (1.0 / HEAD_DIM) + EPS) * gqs
            kb_[rows, :] = k * lax.rsqrt(ssq[1] * (1.0 / HEAD_DIM) + EPS) * gk
            return carry

        def pre_gate(t, carry):
            rows = pl.ds(pl.multiple_of(t * tn, tn), tn)
            g = gb_[rows, :]
            ov = ob_[rows, :]
            dya = yb_[rows, :]
            sg = _sigmoid(g)
            dgt_ref[rows, :] = (dya * ov * (sg * (1.0 + g * (1.0 - sg)))).astype(BF16)
            do = dya * (g * sg)
            yb_[rows, :] = do
            ob_[rows, :] = jnp.where(first_half, lb_[rows, :], _headsum(do * ov, bd))
            return carry

        first_half = (lax.broadcasted_iota(jnp.int32, (tn, LANES), 1) & (HEAD_DIM - 1)) < HEAD_DIM // 2
        loads[0].wait()
        loads[1].wait()
        lax.fori_loop(0, SEQ // tn, pre_qk, 0)
        for cp in loads[2:5] + loads[6:7]:
            cp.wait()
        lax.fori_loop(0, SEQ // tn, pre_gate, 0)
        loads[5].wait()
        reloads = [pltpu.make_async_copy(pcol(C_AQ), lb_, sems.at[7]), pltpu.make_async_copy(pcol(C_AK), vb_, sems.at[8])]
        reloads[0].start()

        def load_kv(ref, d, start, prev):
            own = _rows_of(ref, start, d)[...]
            if prev is None:
                return own.astype(BF16)
            return jnp.concatenate([_rows_of(ref, prev, d)[...], own], axis=0).astype(BF16)

        def group(d, blocks):
            first = blocks[0][1] is None
            valid, lok = (valid1, lo) if first else (valid2, lo2)
            chains = [(b, h) for b in range(len(blocks)) for h in range(2)]
            mask = lambda h: lo if h == 0 else ~lo
            qs = [_rows_of(qb_, start, d)[...] for start, _ in blocks]
            dos = [_rows_of(yb_, start, d)[...] for start, _ in blocks]
            lds = [_rows_of(ob_, start, d)[...] for start, _ in blocks]
            ks = [load_kv(kb_, d, start, prev) for start, prev in blocks]
            vs = [load_kv(vb_, d, start, prev) for start, prev in blocks]
            qbs = [q.astype(BF16) for q in qs]
            dobs = [do.astype(BF16) for do in dos]
            ss = [_dot_nt(jnp.where(mask(h), qs[b], 0.0).astype(BF16), ks[b]) for b, h in chains]
            dps = [_dot_nt(jnp.where(mask(h), dos[b], 0.0).astype(BF16), vs[b]) for b, h in chains]
            pbs, dss = [], []
            for s, dp, (b, h) in zip(ss, dps, chains):
                hc, dc = h * HEAD_DIM, h * HEAD_DIM + HEAD_DIM // 2
                p = jnp.exp(jnp.where(valid, s, -jnp.inf) - lds[b][:, hc:hc + 1])
                pbs.append(p.astype(BF16))
                dss.append((p * (dp - lds[b][:, dc:dc + 1])).astype(BF16))
            dqs = [_dot(ds, ks[b]) for ds, (b, h) in zip(dss, chains)]
            dks = [_dot_tn(ds, qbs[b]) for ds, (b, h) in zip(dss, chains)]
            dvs = [_dot_tn(p, dobs[b]) for p, (b, h) in zip(pbs, chains)]
            assign = d == DILATIONS[0]
            for b, (start, prev) in enumerate(blocks):
                c0, c1 = 2 * b, 2 * b + 1
                dq_rows = _rows_of(gb_, start, d)
                dqb = jnp.where(lo, dqs[c0], dqs[c1])
                dq_rows[...] = dqb if assign else dq_rows[...] + dqb
                dkc = jnp.where(lok, dks[c0], dks[c1])
                dvc = jnp.where(lok, dvs[c0], dvs[c1])
                spans = ((start, slice(0, CHUNK), True),) if first else (
                    (prev, slice(0, CHUNK), False), (start, slice(CHUNK, 2 * CHUNK), True))
                for st, sl, own in spans:
                    dk_rows = _rows_of(dkb_, st, d)
                    dv_rows = _rows_of(dvb_, st, d)
                    if assign and own:
                        dk_rows[...] = dkc[sl]
                        dv_rows[...] = dvc[sl]
                    else:
                        dk_rows[...] = dk_rows[...] + dkc[sl]
                        dv_rows[...] = dv_rows[...] + dvc[sl]

        for d in DILATIONS:
            _for_blocks(d, group, ATTN_UNROLL)

        reloads[1].start()

        @pl.when(pair < npairs - 1)
        def _():
            nxt = input_loads(pair + 1)
            for n in early:
                nxt[n].start()

        for cp in reloads:
            cp.wait()

        def post(t, carry):
            gq_acc, gk_acc = carry
            rows = pl.ds(pl.multiple_of(t * tn, tn), tn)
            raws = [lb_[rows, :], vb_[rows, :]]
            dns = [gb_[rows, :], dkb_[rows, :]]
            rs = [lax.rsqrt(_headsum(a * a, bd) * (1.0 / HEAD_DIM) + EPS) for a in raws]
            zs = [a * r for a, r in zip(raws, rs)]
            dzs = [dn * gain for dn, gain in zip(dns, (gqs, gk))]
            means = [_headsum(dz * z, bd) * (1.0 / HEAD_DIM) for dz, z in zip(dzs, zs)]
            dq, dk = [r * (dz - z * mean) for r, dz, z, mean in zip(rs, dzs, zs, means)]
            gq, gkk = [jnp.sum(dn * z, axis=0, keepdims=True) for dn, z in zip(dns, zs)]
            dq_ref[rows, :] = dq.astype(BF16)
            dk_ref[rows, :] = dk.astype(BF16)
            dv_ref[rows, :] = dvb_[rows, :].astype(BF16)
            return gq_acc + gq * QK_SCALE, gk_acc + gkk

        zero = jnp.zeros((1, LANES), F32)
        gq_acc, gk_acc = lax.fori_loop(0, SEQ // tn, post, (zero, zero))
        gqg_ref[0:1, :] += gq_acc
        gkg_ref[0:1, :] += gk_acc

        @pl.when(pair == npairs - 1)
        def _():
            gqg_ref[0:1, :] = _fold_heads(gqg_ref[0:1, :])
            gkg_ref[0:1, :] = _fold_heads(gkg_ref[0:1, :])
            rs_stage[npairs]()

    hbm = pl.BlockSpec(memory_space=pl.ANY)
    vec = pl.BlockSpec((1, LANES), lambda p: (0, 0))
    blk8 = pl.BlockSpec((8, LANES), lambda p: (0, 0))
    out = pl.BlockSpec((SEQ, LANES), lambda p: (0, p))
    big = jax.ShapeDtypeStruct((SEQ, ATTN_W), BF16)
    nsem = RS_SEMS * nride
    return _call(
        body, name="attn_bwd", grid=(npairs,),
        in_specs=[hbm, hbm, hbm, hbm, vec, vec] + [hbm] * nride,
        out_specs=[out, out, out, out, blk8, blk8] + [hbm] * nride,
        out_shape=[big, big, big, big, jax.ShapeDtypeStruct((8, LANES), F32), jax.ShapeDtypeStruct((8, LANES), F32)]
        + [jax.ShapeDtypeStruct((2, g.shape[0] // 8, g.shape[1]), F32) for g in ride_along],
        scratch_shapes=[pltpu.VMEM((SEQ, LANES), F32) for _ in range(9)] + [pltpu.SemaphoreType.DMA((9,))]
        + _rs_scratch([g.shape for g in ride_along]) + [pltpu.SemaphoreType.DMA((nsem,)), pltpu.SemaphoreType.DMA((nsem,)),
                                     pltpu.SemaphoreType.DMA((nride,))],
        compiler_params=_params(),
    )(proj, o, lse, dyc, gq2, gk2, *[_rs_view(g) for g in ride_along])


def _mem_kv(mem, gain, wkv):
    def body(m_ref, g_ref, w_ref, kv_ref, hm_ref):
        mv = m_ref[...]
        ms = jnp.mean(mv * mv, axis=-1, keepdims=True)
        hm = (mv * lax.rsqrt(ms + EPS) * g_ref[...]).astype(BF16)
        hm_ref[...] = hm
        kv_ref[...] = _dot(hm, w_ref[...])

    return _call(
        body, name="mem_kv",
        out_shape=[jax.ShapeDtypeStruct((MEM_LEN, 2 * MEM_W), F32), jax.ShapeDtypeStruct((MEM_LEN, D_MODEL), BF16)],
        compiler_params=_params(),
    )(mem, gain, wkv)


def _mem_keys(kv_ref, kg_ref, bd, p):
    mk = kv_ref[:, p * LANES:(p + 1) * LANES]
    r = lax.rsqrt(_headsum(mk * mk, bd) * (1.0 / HEAD_DIM) + EPS)
    z = mk * r
    mkn = (z * kg_ref[:, p * LANES:(p + 1) * LANES]).astype(BF16)
    mvp = kv_ref[:, MEM_W + p * LANES:MEM_W + (p + 1) * LANES].astype(BF16)
    return mkn, mvp, r, z


def _mem_fwd(proj, kv, qg4, kg4):
    tm = 1024

    def body(q_ref, g_ref, kv_ref, qg_ref, kg_ref, om_ref, ym_ref):
        bd = _head_blockdiag()
        lo = _lo_mask(tm)
        keys, qns = [], []
        for p in range(2):
            cs = slice(p * LANES, (p + 1) * LANES)
            keys.append(_mem_keys(kv_ref, kg_ref, bd, p)[:2])
            q = q_ref[:, cs]
            qns.append(q * lax.rsqrt(_headsum(q * q, bd) * (1.0 / HEAD_DIM) + EPS) * (qg_ref[:, cs] * QK_SCALE))
        chains = [(p, h) for p in range(2) for h in range(2)]
        ss = [_dot_nt(jnp.where(lo if h == 0 else ~lo, qns[p], 0.0).astype(BF16), keys[p][0]) for p, h in chains]
        es = [jnp.exp(s - jnp.max(s, axis=-1, keepdims=True)) for s in ss]
        os_ = [_dot(e.astype(BF16), keys[p][1]) for e, (p, h) in zip(es, chains)]
        res = [o * (1.0 / jnp.sum(e, axis=-1, keepdims=True)) for o, e in zip(os_, es)]
        for p in range(2):
            cs = slice(p * LANES, (p + 1) * LANES)
            ov = jnp.where(lo, res[2 * p], res[2 * p + 1])
            g = g_ref[:, cs]
            om_ref[:, cs] = ov
            ym_ref[:, cs] = (ov * (g * _sigmoid(g))).astype(BF16)

    vec = pl.BlockSpec((1, MEM_W), lambda i: (0, 0))
    return _call(
        body, name="mem_fwd", grid=(SEQ // tm,),
        in_specs=[pl.BlockSpec((tm, MEM_W), lambda i: (i, C_MQ // MEM_W)),
                  pl.BlockSpec((tm, MEM_W), lambda i: (i, C_MG // MEM_W)),
                  pl.BlockSpec((MEM_LEN, 2 * MEM_W), lambda i: (0, 0)), vec, vec],
        out_specs=[pl.BlockSpec((tm, MEM_W), lambda i: (i, 0)), pl.BlockSpec((tm, MEM_W), lambda i: (i, 0))],
        out_shape=[jax.ShapeDtypeStruct((SEQ, MEM_W), F32), jax.ShapeDtypeStruct((SEQ, MEM_W), BF16)],
        compiler_params=_params(),
    )(proj, proj, kv, qg4, kg4)


def _mem_bwd(proj, om, dyc, kv, hm, mem, mgain, wkv, qg4, kg4):
    tm = 1024
    nsteps = SEQ // tm

    def body(q_ref, g_ref, om_ref, dy_ref, kv_ref, hm_ref, mem_ref, mg_ref, w_ref, qg_ref, kg_ref,
             dq_ref, dgt_ref, gqg_ref, gkg_ref, gw_ref, gmg_ref, dmk_ref, dmv_ref, gq_acc):
        i = pl.program_id(0)
        bd = _head_blockdiag()
        lo = _lo_mask(tm)
        lom = _lo_mask(MEM_LEN)

        @pl.when(i == 0)
        def _():
            dmk_ref[...] = jnp.zeros_like(dmk_ref)
            dmv_ref[...] = jnp.zeros_like(dmv_ref)
            gq_acc[...] = jnp.zeros_like(gq_acc)

        pairs = []
        for p in range(2):
            cs = slice(p * LANES, (p + 1) * LANES)
            mkn, mvp, _, _ = _mem_keys(kv_ref, kg_ref, bd, p)
            gqs = qg_ref[:, cs] * QK_SCALE
            q = q_ref[:, cs]
            r = lax.rsqrt(_headsum(q * q, bd) * (1.0 / HEAD_DIM) + EPS)
            z = q * r
            qn = z * gqs
            g = g_ref[:, cs]
            ov = om_ref[:, cs]
            dym = dy_ref[:, cs]
            sg = _sigmoid(g)
            dgt_ref[:, cs] = (dym * ov * (sg * (1.0 + g * (1.0 - sg)))).astype(BF16)
            do = dym * (g * sg)
            pairs.append(dict(cs=cs, mkn=mkn, mvp=mvp, gqs=gqs, r=r, z=z, qn=qn, qnb=qn.astype(BF16), do=do,
                              dob=do.astype(BF16), delta=_headsum(do * ov, bd)))
        chains = [(pr_, h) for pr_ in pairs for h in range(2)]
        mask = lambda h: lo if h == 0 else ~lo
        ss = [_dot_nt(jnp.where(mask(h), c["qn"], 0.0).astype(BF16), c["mkn"]) for c, h in chains]
        dps = [_dot_nt(jnp.where(mask(h), c["do"], 0.0).astype(BF16), c["mvp"]) for c, h in chains]
        prs, dss = [], []
        for s, dp, (c, h) in zip(ss, dps, chains):
            e = jnp.exp(s - jnp.max(s, axis=-1, keepdims=True))
            pr = e * (1.0 / jnp.sum(e, axis=-1, keepdims=True))
            prs.append(pr.astype(BF16))
            dss.append((pr * (dp - c["delta"][:, h * HEAD_DIM:h * HEAD_DIM + 1])).astype(BF16))
        dqs = [_dot(ds, c["mkn"]) for ds, (c, h) in zip(dss, chains)]
        dks = [_dot_tn(ds, c["qnb"]) for ds, (c, h) in zip(dss, chains)]
        dvs = [_dot_tn(pr, c["dob"]) for pr, (c, h) in zip(prs, chains)]
        for p, c in enumerate(pairs):
            cs, z, r = c["cs"], c["z"], c["r"]
            dqn = jnp.where(lo, dqs[2 * p], dqs[2 * p + 1])
            dmk_ref[:, cs] += jnp.where(lom, dks[2 * p], dks[2 * p + 1])
            dmv_ref[:, cs] += jnp.where(lom, dvs[2 * p], dvs[2 * p + 1])
            dz = dqn * c["gqs"]
            dq_ref[:, cs] = (r * (dz - z * (_headsum(dz * z, bd) * (1.0 / HEAD_DIM)))).astype(BF16)
            gq_acc[:, cs] += jnp.sum(dqn * z, axis=0, keepdims=True) * QK_SCALE

        @pl.when(i == nsteps - 1)
        def _():
            gqg_ref[...] = jnp.zeros_like(gqg_ref)
            gkg_ref[...] = jnp.zeros_like(gkg_ref)
            gqg_ref[0:1, :] = _fold_heads(gq_acc[:, 0:LANES] + gq_acc[:, LANES:2 * LANES])
            dkv = []
            gk = jnp.zeros((1, LANES), F32)
            for p in range(2):
                cs = slice(p * LANES, (p + 1) * LANES)
                _, _, r, z = _mem_keys(kv_ref, kg_ref, bd, p)
                dn = dmk_ref[:, cs]
                dz = dn * kg_ref[:, cs]
                gk = gk + jnp.sum(dn * z, axis=0, keepdims=True)
                dkv.append(r * (dz - z * (_headsum(dz * z, bd) * (1.0 / HEAD_DIM))))
            gkg_ref[0:1, :] = _fold_heads(gk)
            dkvb = jnp.concatenate(dkv + [dmv_ref[...]], axis=1).astype(BF16)
            gw_ref[...] = _dot_tn(hm_ref[...], dkvb)
            dhm = _dot_nt(dkvb, w_ref[...])
            mv = mem_ref[...]
            zm = mv * lax.rsqrt(jnp.mean(mv * mv, axis=-1, keepdims=True) + EPS)
            _put_rows(gmg_ref, jnp.sum(dhm * zm, axis=0, keepdims=True))

    const = lambda shape: pl.BlockSpec(shape, lambda i: (0,) * len(shape))
    row = lambda j: pl.BlockSpec((tm, MEM_W), lambda i: (i, j))
    blk8 = jax.ShapeDtypeStruct((8, LANES), F32)
    return _call(
        body, name="mem_bwd", grid=(nsteps,),
        in_specs=[row(C_MQ // MEM_W), row(C_MG // MEM_W), row(0), row((GMLP_W + ATTN_W) // MEM_W),
                  const((MEM_LEN, 2 * MEM_W)), const((MEM_LEN, D_MODEL)), const((MEM_LEN, D_MODEL)),
                  const((1, D_MODEL)), const((D_MODEL, 2 * MEM_W)), const((1, MEM_W)), const((1, MEM_W))],
        out_specs=[row(0), row(0), const((8, LANES)), const((8, LANES)),
                   const((D_MODEL, 2 * MEM_W)), const((8, LANES))],
        out_shape=[jax.ShapeDtypeStruct((SEQ, MEM_W), BF16), jax.ShapeDtypeStruct((SEQ, MEM_W), BF16),
                   blk8, blk8, jax.ShapeDtypeStruct((D_MODEL, 2 * MEM_W), F32), blk8],
        scratch_shapes=[pltpu.VMEM((MEM_LEN, MEM_W), F32), pltpu.VMEM((MEM_LEN, MEM_W), F32),
                        pltpu.VMEM((1, MEM_W), F32)],
        compiler_params=_params(),
    )(proj, proj, om, dyc, kv, hm, mem, mgain, wkv, qg4, kg4)


def _out_loss(yg, ya, ym, x, tgt, wo):
    tm = 512
    nsteps = SEQ // tm
    parts = ((0, GMLP_W), (GMLP_W, ATTN_W), (GMLP_W + ATTN_W, MEM_W))

    def body(yg_ref, ya_ref, ym_ref, x_ref, t_ref, w_ref, dy_ref, dyc_ref, gw_ref, ls_ref):
        i = pl.program_id(0)

        @pl.when(i == 0)
        def _():
            gw_ref[...] = jnp.zeros_like(gw_ref)
            ls_ref[...] = jnp.zeros_like(ls_ref)

        ys = (yg_ref[...], ya_ref[...], ym_ref[...])
        y = sum(_dot(yv, w_ref[r0:r0 + n, :]) for yv, (r0, n) in zip(ys, parts))
        err = x_ref[...] + y - t_ref[...]
        _put_rows(ls_ref, jnp.sum(err * err, axis=0, keepdims=True), accumulate=True)
        dy = err * (1.0 / D_MODEL)
        dy_ref[...] = dy
        dyb = dy.astype(BF16)
        dyc_ref[...] = _dot_nt(dyb, w_ref[...])
        for yv, (r0, n) in zip(ys, parts):
            gw_ref[r0:r0 + n, :] += _dot_tn(yv, dyb)

    row = lambda w: pl.BlockSpec((tm, w), lambda i: (i, 0))
    const = lambda shape: pl.BlockSpec(shape, lambda i: (0, 0))
    return _call(
        body, name="out_loss", grid=(nsteps,),
        in_specs=[row(GMLP_W), row(ATTN_W), row(MEM_W), row(D_MODEL), row(D_MODEL), const((D_MODEL, D_MODEL))],
        out_specs=[row(D_MODEL), row(D_MODEL), const((D_MODEL, D_MODEL)), const((8, LANES))],
        out_shape=[jax.ShapeDtypeStruct((SEQ, D_MODEL), F32), jax.ShapeDtypeStruct((SEQ, D_MODEL), F32),
                   jax.ShapeDtypeStruct((D_MODEL, D_MODEL), F32), jax.ShapeDtypeStruct((8, LANES), F32)],
        compiler_params=_params(),
    )(yg, ya, ym, x, tgt, wo)


def _proj_bwd(x, dy, gain, wt, dg, daq, dak, dav, dag, dmq, dmg):
    tm = 512
    nsteps = SEQ // tm
    pieces = ((C_GU, 3 * GMLP_W), (C_AQ, ATTN_W), (C_AK, ATTN_W), (C_AV, ATTN_W), (C_AG, ATTN_W),
              (C_MQ, MEM_W), (C_MG, MEM_W))

    def body(x_ref, dy_ref, g_ref, wt_hbm, p0, p1, p2, p3, p4, p5, p6, gx_ref, gwt_hbm, gg_ref, wt_v, acc, wt_sem, out_sems):
        i = pl.program_id(0)
        wt_load = pltpu.make_async_copy(wt_hbm, wt_v, wt_sem)

        @pl.when(i == 0)
        def _():
            wt_load.start()
            acc[...] = jnp.zeros_like(acc)
            gg_ref[...] = jnp.zeros_like(gg_ref)

        xv = x_ref[...]
        r = lax.rsqrt(jnp.mean(xv * xv, axis=-1, keepdims=True) + EPS)
        z = xv * r
        g = g_ref[...]
        h = (z * g).astype(BF16)
        pl.when(i == 0)(wt_load.wait)
        flush = [pltpu.make_async_copy(acc.at[c0:c0 + w, :], gwt_hbm.at[c0:c0 + w, :], out_sems.at[n])
                 for n, (c0, w) in enumerate(pieces)]
        dh = jnp.zeros((tm, D_MODEL), F32)
        for n, (pref, (c0, w)) in enumerate(zip((p0, p1, p2, p3, p4, p5, p6), pieces)):
            dp = pref[...]
            dh = dh + _dot(dp, wt_v[c0:c0 + w, :])
            acc[c0:c0 + w, :] += _dot_tn(dp, h)
            pl.when(i == nsteps - 1)(flush[n].start)
        _put_rows(gg_ref, jnp.sum(dh * z, axis=0, keepdims=True), accumulate=True)
        dz = dh * g
        gx_ref[...] = dy_ref[...] + r * (dz - z * jnp.mean(dz * z, axis=-1, keepdims=True))

        @pl.when(i == nsteps - 1)
        def _():
            for cp in flush:
                cp.wait()

    row = lambda w: pl.BlockSpec((tm, w), lambda i: (i, 0))
    hbm = pl.BlockSpec(memory_space=pl.ANY)
    vec = pl.BlockSpec((1, D_MODEL), lambda i: (0, 0))
    return _call(
        body, name="proj_bwd", grid=(nsteps,),
        in_specs=[row(D_MODEL), row(D_MODEL), vec, hbm] + [row(w) for _, w in pieces],
        out_specs=[row(D_MODEL), hbm, pl.BlockSpec((8, LANES), lambda i: (0, 0))],
        out_shape=[jax.ShapeDtypeStruct((SEQ, D_MODEL), F32), jax.ShapeDtypeStruct((IN_W, D_MODEL), F32),
                   jax.ShapeDtypeStruct((8, LANES), F32)],
        scratch_shapes=[pltpu.VMEM((IN_W, D_MODEL), BF16), pltpu.VMEM((IN_W, D_MODEL), F32), pltpu.SemaphoreType.DMA,
                        pltpu.SemaphoreType.DMA((len(pieces),))],
        compiler_params=_params(),
    )(x, dy, gain, wt, dg, daq, dak, dav, dag, dmq, dmg)


AG_SEMS = 8


def _gather_stages(ins, lands, send_sems, recv_sems):
    n = len(ins)
    nrows = [a.shape[0] for a in ins]
    x, y, c = lax.axis_index("x"), lax.axis_index("y"), lax.axis_index("c")
    sib, xn, yn = (x, y, 1 - c), (1 - x, y, c), (x, 1 - y, c)
    me, cx, cy, cd = 2 * x + y, 2 * (1 - x) + y, 2 * x + (1 - y), 2 * (1 - x) + (1 - y)

    def part(a, chip, hf, quarter=None):
        rows = nrows[a] // 2
        base = chip * nrows[a] + hf * rows
        if quarter is not None:
            rows = rows // 2
            base = base + quarter * rows
        return lands[a].at[pl.ds(pl.multiple_of(base, 16), rows), :]

    def copy(a, j, ref, to):
        k = AG_SEMS * a + j
        return pltpu.make_async_remote_copy(src_ref=ref, dst_ref=ref, send_sem=send_sems.at[k],
                                            recv_sem=recv_sems.at[k], device_id=to, device_id_type=MESH)

    def own(a):
        return [copy(a, 0, part(a, me, c), xn), copy(a, 1, part(a, me, c), yn)]

    def neighbours(a):
        return [copy(a, 4, part(a, cx, c, 1), yn), copy(a, 2, part(a, cx, c), sib),
                copy(a, 5, part(a, cy, c, 0), xn), copy(a, 3, part(a, cy, c), sib)]

    def diagonal(a):
        return [copy(a, 7, part(a, cd, c, 1), sib), copy(a, 6, part(a, cd, c, 0), sib)]

    def send_own():
        for a in range(n):
            lands[a][pl.ds(pl.multiple_of(me * nrows[a], 16), nrows[a]), :] = ins[a][...].astype(BF16)
            for cp in own(a):
                cp.start()

    def pass_on_neighbours():
        for a in range(n):
            copy(a, 0, part(a, cx, c), xn).wait_recv()
            copy(a, 1, part(a, cy, c), yn).wait_recv()
            for cp in neighbours(a):
                cp.start()

    def pass_on_diagonal():
        for a in range(n):
            copy(a, 4, part(a, cd, c, 1), yn).wait_recv()
            copy(a, 5, part(a, cd, c, 0), xn).wait_recv()
            for cp in diagonal(a):
                cp.start()

    def y_complete():
        for a in range(n):
            copy(a, 3, part(a, cy, 1 - c), sib).wait_recv()

    def x_complete():
        for a in range(n):
            copy(a, 2, part(a, cx, 1 - c), sib).wait_recv()

    def diagonal_complete():
        for a in range(n):
            copy(a, 6, part(a, cd, 1 - c, 0), sib).wait_recv()
            copy(a, 7, part(a, cd, 1 - c, 1), sib).wait_recv()

    def sends_done():
        for a in range(n):
            for cp in own(a) + neighbours(a) + diagonal(a):
                cp.wait_send()

    def finish():
        y_complete()
        x_complete()
        diagonal_complete()
        sends_done()

    return (send_own, pass_on_neighbours, pass_on_diagonal, finish), (y_complete, x_complete, diagonal_complete, sends_done)


RS_SEMS = 6
RS_KINDS = (((2, 2), 1, F32), ((2, 2), 1, F32), ((2, 2), 2, BF16), ((2, 2), 2, BF16), ((2, 2), 2, F32),
            ((2,), 2, BF16), ((2,), 2, BF16), ((2,), 1, F32))


def _rs_view(g):
    return g.reshape(2, 2, 2, g.shape[0] // 8, g.shape[1])


def _rs_scratch(shapes):
    return [pltpu.VMEM(lead + (r // 8, w // split), dt) for lead, split, dt in RS_KINDS for r, w in shapes]


def _rs_stages(gs, outs, bufs, send_sems, recv_sems, local_sems, widths):
    n = len(gs)
    loc, ra, s_b, r_b, acc1, s_c, r_c, fin = (bufs[n * i:n * i + n] for i in range(len(RS_KINDS)))
    half_w = [w // 2 for w in widths]
    x, y, c = lax.axis_index("x"), lax.axis_index("y"), lax.axis_index("c")
    sib, xn, yn = (x, y, 1 - c), (1 - x, y, c), (x, 1 - y, c)

    def copy(a, j, src, dst, to):
        k = RS_SEMS * a + j
        return pltpu.make_async_remote_copy(src_ref=src, dst_ref=dst, send_sem=send_sems.at[k],
                                            recv_sem=recv_sems.at[k], device_id=to, device_id_type=MESH)

    def step_a(a):
        return [copy(a, 0, gs[a].at[:, :, 1 - c], ra[a], sib),
                pltpu.make_async_copy(gs[a].at[:, :, c], loc[a], local_sems.at[a])]

    def step_b(a):
        return copy(a, 1, s_b[a].at[0], r_b[a].at[0], xn), copy(a, 2, s_b[a].at[1], r_b[a].at[1], yn)

    def step_c(a):
        return copy(a, 3, s_c[a].at[0], r_c[a].at[0], yn), copy(a, 4, s_c[a].at[1], r_c[a].at[1], xn)

    def step_d(a, half):
        rows = fin[a].at[half]
        return copy(a, 5, rows, rows, sib)

    def start():
        for a in range(n):
            for priority, cp in enumerate(step_a(a)):
                cp.start(priority=priority)

    def a_to_b():
        for a in range(n):
            for cp in step_a(a):
                cp.wait()
            ra[a][...] = loc[a][...] + ra[a][...]
            s_b[a][0] = ra[a][1 - x, :, :, :half_w[a]].astype(BF16)
            s_b[a][1] = ra[a][:, 1 - y, :, half_w[a]:].astype(BF16)
            for cp in step_b(a):
                cp.start()

    def b_to_c():
        for a in range(n):
            for cp in step_b(a):
                cp.wait()
            acc1[a][0] = ra[a][x, :, :, :half_w[a]] + r_b[a][0].astype(F32)
            acc1[a][1] = ra[a][:, y, :, half_w[a]:] + r_b[a][1].astype(F32)
            s_c[a][0] = acc1[a][0, 1 - y].astype(BF16)
            s_c[a][1] = acc1[a][1, 1 - x].astype(BF16)
            for cp in step_c(a):
                cp.start()

    def c_to_d():
        for a in range(n):
            for cp in step_c(a):
                cp.wait()
            fin[a][c, :, :half_w[a]] = acc1[a][0, y] + r_c[a][0].astype(F32)
            fin[a][c, :, half_w[a]:] = acc1[a][1, x] + r_c[a][1].astype(F32)
            step_d(a, c).start()

    def finish():
        to_hbm = [pltpu.make_async_copy(fin[a], outs[a], local_sems.at[a]) for a in range(n)]
        for a in range(n):
            step_d(a, 1 - c).wait_recv()
            step_d(a, c).wait_send()
            to_hbm[a].start()
        for cp in to_hbm:
            cp.wait()

    return start, a_to_b, b_to_c, c_to_d, finish


def _reduce_grads(gwt, g_ws, tiny):
    cw = gwt.shape[1] // RS_CHUNKS
    chunk_shape = (gwt.shape[0], cw)

    def body(g0, ws_in, tiny_in, *rest):
        outs, o_ws, o_tiny = rest[:RS_CHUNKS], rest[RS_CHUNKS], rest[RS_CHUNKS + 1]
        rest = rest[RS_CHUNKS + 2:]
        nb = len(RS_KINDS) * RS_CHUNKS
        sm, sa, sb, sc, acc_s, send_sems, recv_sems, local_sems = rest[nb:]
        blocks = [g0.at[:, :, :, :, pl.ds(j * cw, cw)] for j in range(RS_CHUNKS)]
        start, a_to_b, b_to_c, c_to_d, finish = _rs_stages(blocks, outs, rest[:nb], send_sems, recv_sems, local_sems,
                                                           [cw] * RS_CHUNKS)
        n_ws = ws_in.shape[0]
        sm[0:n_ws, :] = ws_in[...]
        sm[n_ws:, :] = tiny_in[...]
        x, y, c = lax.axis_index("x"), lax.axis_index("y"), lax.axis_index("c")

        def small(j, src, dst, to):
            k = RS_SEMS * RS_CHUNKS + j
            return pltpu.make_async_remote_copy(src_ref=src, dst_ref=dst, send_sem=send_sems.at[k],
                                                recv_sem=recv_sems.at[k], device_id=to, device_id_type=MESH)

        along_c, along_x, along_y = (small(0, sm, sa, (x, y, 1 - c)), small(1, acc_s, sb, (1 - x, y, c)),
                                     small(2, sb, sc, (x, 1 - y, c)))
        start()
        along_c.start()
        a_to_b()
        along_c.wait()
        acc_s[...] = sm[...] + sa[...]
        along_x.start()
        b_to_c()
        along_x.wait()
        sb[...] = acc_s[...] + sb[...]
        along_y.start()
        c_to_d()
        along_y.wait()
        o_ws[...] = sb[0:n_ws, :] + sc[0:n_ws, :]
        o_tiny[...] = sb[n_ws:, :] + sc[n_ws:, :]
        finish()

    vm = pl.BlockSpec(memory_space=pltpu.VMEM)
    hbm = pl.BlockSpec(memory_space=pl.ANY)
    small_shape = (g_ws.shape[0] + tiny.shape[0], LANES)
    scratch = _rs_scratch([chunk_shape] * RS_CHUNKS) + [pltpu.VMEM(small_shape, F32) for _ in range(5)]
    nsem = RS_SEMS * RS_CHUNKS + 3
    scratch += [pltpu.SemaphoreType.DMA((nsem,)), pltpu.SemaphoreType.DMA((nsem,)), pltpu.SemaphoreType.DMA((RS_CHUNKS,))]
    return _call(
        body, name="reduce_grads",
        out_shape=[jax.ShapeDtypeStruct((2, gwt.shape[0] // 8, cw), F32)] * RS_CHUNKS
        + [jax.ShapeDtypeStruct(g_ws.shape, F32), jax.ShapeDtypeStruct(tiny.shape, F32)],
        in_specs=[hbm, vm, vm],
        out_specs=[hbm] * RS_CHUNKS + [vm, vm],
        scratch_shapes=scratch,
        compiler_params=_params(),
    )(_rs_view(gwt), g_ws, tiny)


def _adam_update(w, g, m, v):
    nm = ADAM_B1 * m + (1.0 - ADAM_B1) * g
    nv = ADAM_B2 * v + (1.0 - ADAM_B2) * (g * g)
    m_hat = nm / (1.0 - ADAM_B1 ** ADAM_STEP)
    v_hat = nv / (1.0 - ADAM_B2 ** ADAM_STEP)
    return -ADAM_LR * (m_hat / (jnp.sqrt(v_hat) + ADAM_EPS) + ADAM_WD * w), nm, nv


def _adamw(w, g, m, v):
    rows, cols = w.shape
    tm = max(t for t in range(8, 257, 8) if rows % t == 0)
    parts = tuple(g) if isinstance(g, (tuple, list)) else (g,)
    n = len(parts)

    def body(w_ref, m_ref, v_ref, *refs):
        gv = jnp.concatenate([r[...] for r in refs[:n]], axis=1)
        d_ref, nm_ref, nv_ref = refs[n:n + 3]
        d_ref[...], nm_ref[...], nv_ref[...] = _adam_update(w_ref[...], gv, m_ref[...], v_ref[...])
        if n > 1:
            refs[n + 3][...] = gv

    blk = pl.BlockSpec((tm, cols), lambda i: (i, 0))
    nout = 3 if n == 1 else 4
    res = _call(
        body, name="adamw", grid=(rows // tm,),
        in_specs=[blk] * 3 + [pl.BlockSpec((tm, p.shape[1]), lambda i: (i, 0)) for p in parts], out_specs=[blk] * nout,
        out_shape=[jax.ShapeDtypeStruct((rows, cols), F32)] * nout,
        compiler_params=_params(),
    )(w, m, v, *parts)
    return (parts[0] if n == 1 else res[3], *res[:3])


def _adamw_tiny(tiny, weights, ms, vs):
    shapes = [w.shape for w in weights]
    n = len(weights)

    def grad_of(t_ref, k, shape):
        base = 8 * k
        if shape[1] > LANES:
            return [t_ref[base + j:base + j + 1, :] for j in range(shape[1] // LANES)]
        return [t_ref[base:base + shape[0], 0:shape[1]]]

    def body(t_ref, *refs):
        w_refs, m_refs, v_refs = refs[:n], refs[n:2 * n], refs[2 * n:3 * n]
        loss_ref, outs = refs[3 * n], refs[3 * n + 1:]
        loss_ref[...] = (0.5 / D_MODEL) * jnp.sum(t_ref[8 * n:8 * n + 8, :], keepdims=True)
        for k, shape in enumerate(shapes):
            g_ref, d_ref, nm_ref, nv_ref = outs[4 * k:4 * k + 4]
            for j, g in enumerate(grad_of(t_ref, k, shape)):
                cols = slice(j * LANES, (j + 1) * LANES) if shape[1] > LANES else slice(None)
                g_ref[:, cols] = g
                d_ref[:, cols], nm_ref[:, cols], nv_ref[:, cols] = _adam_update(
                    w_refs[k][:, cols], g, m_refs[k][:, cols], v_refs[k][:, cols])

    out_shape = [jax.ShapeDtypeStruct((1, 1), F32)]
    for shape in shapes:
        out_shape += [jax.ShapeDtypeStruct(shape, F32)] * 4
    return _call(body, name="adamw_tiny", out_shape=out_shape, compiler_params=_params())(tiny, *weights, *ms, *vs)


def _local_grads(x, mem, tgt, norm_gain, wt_sh, gmlp_v_gain, gmlp_w_s, gmlp_b, attn_q_gain, attn_k_gain,
                 mem_norm_gain, wkv_sh, mem_q_gain, mem_k_gain, wo_sh):
    vg = gmlp_v_gain.reshape(1, GMLP_W)
    bias_full = jnp.repeat(gmlp_b.T, HEAD_DIM, axis=1)
    gq2, gk2 = jnp.tile(attn_q_gain, (1, 2)), jnp.tile(attn_k_gain, (1, 2))
    qg4, kg4 = jnp.tile(mem_q_gain, (1, 4)), jnp.tile(mem_k_gain, (1, 4))

    proj, wt = _gather_proj(x, norm_gain, wt_sh)
    yg = _gmlp_fwd(proj, vg, gmlp_w_s, bias_full)
    o, lse, ya, wkv, wo = _attn_fwd(proj, gq2, gk2, wkv_sh, wo_sh)
    kv, hm = _mem_kv(mem, mem_norm_gain, wkv)
    om, ym = _mem_fwd(proj, kv, qg4, kg4)
    dy, dyc, g_wo, err2 = _out_loss(yg, ya, ym, x, tgt, wo)
    dmq, dmg, g_mq, g_mk, g_wkv, g_mng = _mem_bwd(proj, om, dyc, kv, hm, mem, mem_norm_gain, wkv, qg4, kg4)
    daq, dak, dav, dag, g_aq, g_ak, g_wkv_sh, g_wo_sh = _attn_bwd(proj, o, lse, dyc, gq2, gk2, g_wkv, g_wo)
    dg, g_ws, g_b, g_vg = _gmlp_bwd(proj, dyc, vg, gmlp_w_s, bias_full)
    gx, g_wt, g_ng = _proj_bwd(x, dy, norm_gain, wt, dg, daq, dak, dav, dag, dmq, dmg)

    tiny = jnp.concatenate([g_ng, g_vg, g_b, g_aq, g_ak, g_mng, g_mq, g_mk, err2], axis=0)
    return gx, g_wt, g_wkv_sh, g_wo_sh, g_ws.reshape(4 * CHUNK, CHUNK), tiny


def kernel(x, mem, norm_gain, w_in, gmlp_v_gain, gmlp_w_s, gmlp_b, attn_q_gain, attn_k_gain, mem_norm_gain, w_mem_kv, mem_q_gain, mem_k_gain, w_out, loss_target, m_norm_gain, m_w_in, m_gmlp_v_gain, m_gmlp_w_s, m_gmlp_b, m_attn_q_gain, m_attn_k_gain, m_mem_norm_gain, m_w_mem_kv, m_mem_q_gain, m_mem_k_gain, m_w_out, v_norm_gain, v_w_in, v_gmlp_v_gain, v_gmlp_w_s, v_gmlp_b, v_attn_q_gain, v_attn_k_gain, v_mem_norm_gain, v_w_mem_kv, v_mem_q_gain, v_mem_k_gain, v_w_out):
    gx, g_wt, g_wkv_sh, g_wo_sh, g_ws, tiny = _local_grads(
        x[0], mem[0], loss_target[0], norm_gain, w_in[0].T, gmlp_v_gain[0], gmlp_w_s[0], gmlp_b[0],
        attn_q_gain, attn_k_gain, mem_norm_gain, w_mem_kv[0], mem_q_gain, mem_k_gain, w_out[0])
    *g_wt_sh, g_ws, tiny = _reduce_grads(g_wt, g_ws, tiny)
    chip_block = lambda g: g.reshape(2 * g.shape[1], g.shape[2])
    g_wt_sh = tuple(chip_block(g) for g in g_wt_sh)
    g_wkv_sh, g_wo_sh = chip_block(g_wkv_sh), chip_block(g_wo_sh)

    ws = (norm_gain, w_in, gmlp_v_gain, gmlp_w_s, gmlp_b, attn_q_gain, attn_k_gain, mem_norm_gain, w_mem_kv,
          mem_q_gain, mem_k_gain, w_out)
    ms = (m_norm_gain, m_w_in, m_gmlp_v_gain, m_gmlp_w_s, m_gmlp_b, m_attn_q_gain, m_attn_k_gain, m_mem_norm_gain,
          m_w_mem_kv, m_mem_q_gain, m_mem_k_gain, m_w_out)
    vs = (v_norm_gain, v_w_in, v_gmlp_v_gain, v_gmlp_w_s, v_gmlp_b, v_attn_q_gain, v_attn_k_gain, v_mem_norm_gain,
          v_w_mem_kv, v_mem_q_gain, v_mem_k_gain, v_w_out)
    form = {1: lambda a: a[0].T, 3: lambda a: a.reshape(4 * CHUNK, CHUNK), 2: lambda a: a[0], 4: lambda a: a[0],
            8: lambda a: a[0], 11: lambda a: a[0]}
    back = {1: lambda a: a.T[None], 3: lambda a: a.reshape(1, 4, CHUNK, CHUNK), 2: lambda a: a[None],
            4: lambda a: a[None], 8: lambda a: a[None], 11: lambda a: a[None]}
    fwd = lambda t, i: form.get(i, lambda a: a)(t[i])
    out = {}
    for i, g in ((1, g_wt_sh), (3, g_ws), (8, g_wkv_sh), (11, g_wo_sh)):
        out[i] = _adamw(fwd(ws, i), g, fwd(ms, i), fwd(vs, i))
    res = _adamw_tiny(tiny, [fwd(ws, i) for i in TINY_ORDER], [fwd(ms, i) for i in TINY_ORDER],
                      [fwd(vs, i) for i in TINY_ORDER])
    for k, i in enumerate(TINY_ORDER):
        out[i] = res[1 + 4 * k:5 + 4 * k]
    leaves = [[back.get(i, lambda a: a)(out[i][j]) for i in range(12)] for j in range(4)]
    return (res[0].reshape(()), gx[None], *leaves[0], *leaves[1], *leaves[2], *leaves[3])
```
